```python
import jax, jax.numpy as jnp
from jax import lax
import numpy as np

D_MODEL = 1024
BATCH = 16
SEQ = 2048
DEPTH = 1

D_FF = 2816
W_A = D_MODEL // 2
H_A = 8
HD_A = W_A // H_A
W_B = D_MODEL - W_A
G_B = 8
CHUNK = 128
CONV_K = 31
N_MOD = 9
EPS = 1e-6
HALF = 0.5

kernel_name = "hybrid_gmlp_conformer_macaron_adaln"


def _rms_norm(x, g):
    xf = x.astype(jnp.float32)
    xf = xf * lax.rsqrt(jnp.mean(xf * xf, axis=-1, keepdims=True) + EPS)
    return (xf * g.astype(jnp.float32)).astype(x.dtype)


def _layer_norm(x, g, b):
    xf = x.astype(jnp.float32)
    mu = jnp.mean(xf, axis=-1, keepdims=True)
    xc = xf - mu
    var = jnp.mean(xc * xc, axis=-1, keepdims=True)
    y = xc * lax.rsqrt(var + EPS) * g.astype(jnp.float32) + b.astype(jnp.float32)
    return y.astype(x.dtype)


def _modulate(h, shift, scale):
    return h * (1 + scale[:, None, :]) + shift[:, None, :]


def _swiglu(h, w_in, w_out):
    gate, up = jnp.split(h @ w_in, 2, axis=-1)
    return (jax.nn.silu(gate) * up) @ w_out


def _hybrid_mixer(h, w_mix_in, gmlp_norm_g, gmlp_norm_b, w_spatial, b_spatial,
                  conv_w, conv_b, conv_norm_g, conv_norm_b, g_out_a, g_out_b, w_mix_out):
    bsz, seq, _ = h.shape
    proj = h @ w_mix_in
    u, v, a, g = jnp.split(proj, [W_A, 2 * W_A, 2 * W_A + W_B], axis=-1)

    v = _layer_norm(v, gmlp_norm_g, gmlp_norm_b)
    n_chunks = seq // CHUNK
    v = v.reshape(bsz, n_chunks, CHUNK, H_A, HD_A)
    causal = jnp.tril(jnp.ones((CHUNK, CHUNK), dtype=bool))
    w_s = jnp.where(causal[None], w_spatial, jnp.zeros_like(w_spatial))
    z = jnp.einsum('hts,bnshd->bnthd', w_s, v) + b_spatial.T[None, None, :, :, None]
    y_a = u * z.reshape(bsz, seq, W_A)

    glu = a * jax.nn.sigmoid(g)
    conv = lax.conv_general_dilated(
        glu, conv_w[:, None, :], window_strides=(1,), padding=[(CONV_K - 1, 0)],
        dimension_numbers=('NWC', 'WIO', 'NWC'), feature_group_count=W_B) + conv_b
    y_b = jax.nn.silu(_layer_norm(conv, conv_norm_g, conv_norm_b))

    y = jnp.concatenate([_rms_norm(y_a, g_out_a), _rms_norm(y_b, g_out_b)], axis=-1)
    return y @ w_mix_out


def _fwd_setup_inputs(seed: int = 0) -> dict:
    key = jax.random.key(seed)
    ks = jax.random.split(key, 32)
    L, D = DEPTH, D_MODEL

    def nrm(k, shape, std):
        return std * jax.random.normal(k, shape, jnp.float32)

    def gain(k, shape):
        return 1.0 + 0.05 * jax.random.normal(k, shape, jnp.float32)

    return {
        "x": nrm(ks[0], (BATCH, SEQ, D), 1.0),
        "c": nrm(ks[1], (BATCH, D), 1.0),
        "w_ada": nrm(ks[2], (L, D, N_MOD * D), 0.5 * D ** -0.5),
        "b_ada": nrm(ks[3], (L, N_MOD * D), 0.02),
        "g_pre_f1": gain(ks[4], (L, D)),
        "g_post_f1": gain(ks[5], (L, D)),
        "w_f1_in": nrm(ks[6], (L, D, 2 * D_FF), D ** -0.5),
        "w_f1_out": nrm(ks[7], (L, D_FF, D), D_FF ** -0.5),
        "g_pre_m": gain(ks[8], (L, D)),
        "g_post_m": gain(ks[9], (L, D)),
        "w_mix_in": nrm(ks[10], (L, D, 2 * W_A + 2 * W_B), D ** -0.5),
        "gmlp_norm_g": gain(ks[11], (L, W_A)),
        "gmlp_norm_b": nrm(ks[12], (L, W_A), 0.02),
        "w_spatial": nrm(ks[13], (L, H_A, CHUNK, CHUNK), CHUNK ** -0.5),
        "b_spatial": gain(ks[14], (L, H_A, CHUNK)),
        "conv_w": nrm(ks[15], (L, CONV_K, W_B), CONV_K ** -0.5),
        "conv_b": nrm(ks[16], (L, W_B), 0.02),
        "conv_norm_g": gain(ks[17], (L, W_B)),
        "conv_norm_b": nrm(ks[18], (L, W_B), 0.02),
        "g_out_a": gain(ks[19], (L, W_A)),
        "g_out_b": gain(ks[20], (L, W_B)),
        "w_mix_out": nrm(ks[21], (L, W_A + W_B, D), (W_A + W_B) ** -0.5),
        "g_pre_f2": gain(ks[22], (L, D)),
        "g_post_f2": gain(ks[23], (L, D)),
        "w_f2_in": nrm(ks[24], (L, D, 2 * D_FF), D ** -0.5),
        "w_f2_out": nrm(ks[25], (L, D_FF, D), D_FF ** -0.5),
    }


def _fwd_reference(x, c, w_ada, b_ada, g_pre_f1, g_post_f1, w_f1_in, w_f1_out,
              g_pre_m, g_post_m, w_mix_in, gmlp_norm_g, gmlp_norm_b, w_spatial, b_spatial,
              conv_w, conv_b, conv_norm_g, conv_norm_b, g_out_a, g_out_b, w_mix_out,
              g_pre_f2, g_post_f2, w_f2_in, w_f2_out):
    for l in range(DEPTH):
        ada = jax.nn.silu(c) @ w_ada[l] + b_ada[l]
        sh1, sc1, gt1, sh2, sc2, gt2, sh3, sc3, gt3 = jnp.split(ada, N_MOD, axis=-1)

        h = _modulate(_rms_norm(x, g_pre_f1[l]), sh1, sc1)
        x = x + HALF * gt1[:, None, :] * _rms_norm(_swiglu(h, w_f1_in[l], w_f1_out[l]), g_post_f1[l])

        h = _modulate(_rms_norm(x, g_pre_m[l]), sh2, sc2)
        y = _hybrid_mixer(h, w_mix_in[l], gmlp_norm_g[l], gmlp_norm_b[l], w_spatial[l], b_spatial[l],
                          conv_w[l], conv_b[l], conv_norm_g[l], conv_norm_b[l],
                          g_out_a[l], g_out_b[l], w_mix_out[l])
        x = x + gt2[:, None, :] * _rms_norm(y, g_post_m[l])

        h = _modulate(_rms_norm(x, g_pre_f2[l]), sh3, sc3)
        x = x + HALF * gt3[:, None, :] * _rms_norm(_swiglu(h, w_f2_in[l], w_f2_out[l]), g_post_f2[l])
    return x


import jax as _jax
import jax.numpy as _jnp

TWIN_FORMAT = 'train_step'
FWD_PARAMS = ['x', 'c', 'w_ada', 'b_ada', 'g_pre_f1', 'g_post_f1', 'w_f1_in', 'w_f1_out', 'g_pre_m', 'g_post_m', 'w_mix_in', 'gmlp_norm_g', 'gmlp_norm_b', 'w_spatial', 'b_spatial', 'conv_w', 'conv_b', 'conv_norm_g', 'conv_norm_b', 'g_out_a', 'g_out_b', 'w_mix_out', 'g_pre_f2', 'g_post_f2', 'w_f2_in', 'w_f2_out']
TWIN_WEIGHTS = ['w_ada', 'b_ada', 'g_pre_f1', 'g_post_f1', 'w_f1_in', 'w_f1_out', 'g_pre_m', 'g_post_m', 'w_mix_in', 'gmlp_norm_g', 'gmlp_norm_b', 'w_spatial', 'b_spatial', 'conv_w', 'conv_b', 'conv_norm_g', 'conv_norm_b', 'g_out_a', 'g_out_b', 'w_mix_out', 'g_pre_f2', 'g_post_f2', 'w_f2_in', 'w_f2_out']
TWIN_DIFF_INPUT = 'x'
TWIN_INPUTS = ['x', 'c', 'w_ada', 'b_ada', 'g_pre_f1', 'g_post_f1', 'w_f1_in', 'w_f1_out', 'g_pre_m', 'g_post_m', 'w_mix_in', 'gmlp_norm_g', 'gmlp_norm_b', 'w_spatial', 'b_spatial', 'conv_w', 'conv_b', 'conv_norm_g', 'conv_norm_b', 'g_out_a', 'g_out_b', 'w_mix_out', 'g_pre_f2', 'g_post_f2', 'w_f2_in', 'w_f2_out', 'loss_target', 'm_w_ada', 'm_b_ada', 'm_g_pre_f1', 'm_g_post_f1', 'm_w_f1_in', 'm_w_f1_out', 'm_g_pre_m', 'm_g_post_m', 'm_w_mix_in', 'm_gmlp_norm_g', 'm_gmlp_norm_b', 'm_w_spatial', 'm_b_spatial', 'm_conv_w', 'm_conv_b', 'm_conv_norm_g', 'm_conv_norm_b', 'm_g_out_a', 'm_g_out_b', 'm_w_mix_out', 'm_g_pre_f2', 'm_g_post_f2', 'm_w_f2_in', 'm_w_f2_out', 'v_w_ada', 'v_b_ada', 'v_g_pre_f1', 'v_g_post_f1', 'v_w_f1_in', 'v_w_f1_out', 'v_g_pre_m', 'v_g_post_m', 'v_w_mix_in', 'v_gmlp_norm_g', 'v_gmlp_norm_b', 'v_w_spatial', 'v_b_spatial', 'v_conv_w', 'v_conv_b', 'v_conv_norm_g', 'v_conv_norm_b', 'v_g_out_a', 'v_g_out_b', 'v_w_mix_out', 'v_g_pre_f2', 'v_g_post_f2', 'v_w_f2_in', 'v_w_f2_out']
TWIN_OUTPUTS = ['loss', 'grad_x', 'grad_w_ada', 'grad_b_ada', 'grad_g_pre_f1', 'grad_g_post_f1', 'grad_w_f1_in', 'grad_w_f1_out', 'grad_g_pre_m', 'grad_g_post_m', 'grad_w_mix_in', 'grad_gmlp_norm_g', 'grad_gmlp_norm_b', 'grad_w_spatial', 'grad_b_spatial', 'grad_conv_w', 'grad_conv_b', 'grad_conv_norm_g', 'grad_conv_norm_b', 'grad_g_out_a', 'grad_g_out_b', 'grad_w_mix_out', 'grad_g_pre_f2', 'grad_g_post_f2', 'grad_w_f2_in', 'grad_w_f2_out', 'delta_w_ada', 'delta_b_ada', 'delta_g_pre_f1', 'delta_g_post_f1', 'delta_w_f1_in', 'delta_w_f1_out', 'delta_g_pre_m', 'delta_g_post_m', 'delta_w_mix_in', 'delta_gmlp_norm_g', 'delta_gmlp_norm_b', 'delta_w_spatial', 'delta_b_spatial', 'delta_conv_w', 'delta_conv_b', 'delta_conv_norm_g', 'delta_conv_norm_b', 'delta_g_out_a', 'delta_g_out_b', 'delta_w_mix_out', 'delta_g_pre_f2', 'delta_g_post_f2', 'delta_w_f2_in', 'delta_w_f2_out', 'new_m_w_ada', 'new_m_b_ada', 'new_m_g_pre_f1', 'new_m_g_post_f1', 'new_m_w_f1_in', 'new_m_w_f1_out', 'new_m_g_pre_m', 'new_m_g_post_m', 'new_m_w_mix_in', 'new_m_gmlp_norm_g', 'new_m_gmlp_norm_b', 'new_m_w_spatial', 'new_m_b_spatial', 'new_m_conv_w', 'new_m_conv_b', 'new_m_conv_norm_g', 'new_m_conv_norm_b', 'new_m_g_out_a', 'new_m_g_out_b', 'new_m_w_mix_out', 'new_m_g_pre_f2', 'new_m_g_post_f2', 'new_m_w_f2_in', 'new_m_w_f2_out', 'new_v_w_ada', 'new_v_b_ada', 'new_v_g_pre_f1', 'new_v_g_post_f1', 'new_v_w_f1_in', 'new_v_w_f1_out', 'new_v_g_pre_m', 'new_v_g_post_m', 'new_v_w_mix_in', 'new_v_gmlp_norm_g', 'new_v_gmlp_norm_b', 'new_v_w_spatial', 'new_v_b_spatial', 'new_v_conv_w', 'new_v_conv_b', 'new_v_conv_norm_g', 'new_v_conv_norm_b', 'new_v_g_out_a', 'new_v_g_out_b', 'new_v_w_mix_out', 'new_v_g_pre_f2', 'new_v_g_post_f2', 'new_v_w_f2_in', 'new_v_w_f2_out']
TWIN_LEAF_KINDS = {'loss': 'loss', 'grad_x': 'grad_x', 'grad_w_ada': 'grad_w', 'grad_b_ada': 'grad_w', 'grad_g_pre_f1': 'grad_w', 'grad_g_post_f1': 'grad_w', 'grad_w_f1_in': 'grad_w', 'grad_w_f1_out': 'grad_w', 'grad_g_pre_m': 'grad_w', 'grad_g_post_m': 'grad_w', 'grad_w_mix_in': 'grad_w', 'grad_gmlp_norm_g': 'grad_w', 'grad_gmlp_norm_b': 'grad_w', 'grad_w_spatial': 'grad_w', 'grad_b_spatial': 'grad_w', 'grad_conv_w': 'grad_w', 'grad_conv_b': 'grad_w', 'grad_conv_norm_g': 'grad_w', 'grad_conv_norm_b': 'grad_w', 'grad_g_out_a': 'grad_w', 'grad_g_out_b': 'grad_w', 'grad_w_mix_out': 'grad_w', 'grad_g_pre_f2': 'grad_w', 'grad_g_post_f2': 'grad_w', 'grad_w_f2_in': 'grad_w', 'grad_w_f2_out': 'grad_w', 'delta_w_ada': 'delta_w', 'delta_b_ada': 'delta_w', 'delta_g_pre_f1': 'delta_w', 'delta_g_post_f1': 'delta_w', 'delta_w_f1_in': 'delta_w', 'delta_w_f1_out': 'delta_w', 'delta_g_pre_m': 'delta_w', 'delta_g_post_m': 'delta_w', 'delta_w_mix_in': 'delta_w', 'delta_gmlp_norm_g': 'delta_w', 'delta_gmlp_norm_b': 'delta_w', 'delta_w_spatial': 'delta_w', 'delta_b_spatial': 'delta_w', 'delta_conv_w': 'delta_w', 'delta_conv_b': 'delta_w', 'delta_conv_norm_g': 'delta_w', 'delta_conv_norm_b': 'delta_w', 'delta_g_out_a': 'delta_w', 'delta_g_out_b': 'delta_w', 'delta_w_mix_out': 'delta_w', 'delta_g_pre_f2': 'delta_w', 'delta_g_post_f2': 'delta_w', 'delta_w_f2_in': 'delta_w', 'delta_w_f2_out': 'delta_w', 'new_m_w_ada': 'new_m', 'new_m_b_ada': 'new_m', 'new_m_g_pre_f1': 'new_m', 'new_m_g_post_f1': 'new_m', 'new_m_w_f1_in': 'new_m', 'new_m_w_f1_out': 'new_m', 'new_m_g_pre_m': 'new_m', 'new_m_g_post_m': 'new_m', 'new_m_w_mix_in': 'new_m', 'new_m_gmlp_norm_g': 'new_m', 'new_m_gmlp_norm_b': 'new_m', 'new_m_w_spatial': 'new_m', 'new_m_b_spatial': 'new_m', 'new_m_conv_w': 'new_m', 'new_m_conv_b': 'new_m', 'new_m_conv_norm_g': 'new_m', 'new_m_conv_norm_b': 'new_m', 'new_m_g_out_a': 'new_m', 'new_m_g_out_b': 'new_m', 'new_m_w_mix_out': 'new_m', 'new_m_g_pre_f2': 'new_m', 'new_m_g_post_f2': 'new_m', 'new_m_w_f2_in': 'new_m', 'new_m_w_f2_out': 'new_m', 'new_v_w_ada': 'new_v', 'new_v_b_ada': 'new_v', 'new_v_g_pre_f1': 'new_v', 'new_v_g_post_f1': 'new_v', 'new_v_w_f1_in': 'new_v', 'new_v_w_f1_out': 'new_v', 'new_v_g_pre_m': 'new_v', 'new_v_g_post_m': 'new_v', 'new_v_w_mix_in': 'new_v', 'new_v_gmlp_norm_g': 'new_v', 'new_v_gmlp_norm_b': 'new_v', 'new_v_w_spatial': 'new_v', 'new_v_b_spatial': 'new_v', 'new_v_conv_w': 'new_v', 'new_v_conv_b': 'new_v', 'new_v_conv_norm_g': 'new_v', 'new_v_conv_norm_b': 'new_v', 'new_v_g_out_a': 'new_v', 'new_v_g_out_b': 'new_v', 'new_v_w_mix_out': 'new_v', 'new_v_g_pre_f2': 'new_v', 'new_v_g_post_f2': 'new_v', 'new_v_w_f2_in': 'new_v', 'new_v_w_f2_out': 'new_v'}


def _forward(args):
    return _fwd_reference(*[args[k] for k in FWD_PARAMS])


def _output_shape():
    out = _jax.eval_shape(lambda: _forward(_fwd_setup_inputs(0)))
    return out.shape, out.dtype

N_MICROBATCH = 1
ADAM_LR = 0.001
ADAM_B1 = 0.9
ADAM_B2 = 0.999
ADAM_EPS = 1e-08
ADAM_WD = 0.01
ADAM_STEP = 10
PER_EXAMPLE_BATCH_AXIS = {'x': 0, 'c': 0, 'loss_target': 0}
SHARED_INPUTS = []
_WEIGHT_DTYPES = {'w_ada': _jnp.float32, 'b_ada': _jnp.float32, 'g_pre_f1': _jnp.float32, 'g_post_f1': _jnp.float32, 'w_f1_in': _jnp.float32, 'w_f1_out': _jnp.float32, 'g_pre_m': _jnp.float32, 'g_post_m': _jnp.float32, 'w_mix_in': _jnp.float32, 'gmlp_norm_g': _jnp.float32, 'gmlp_norm_b': _jnp.float32, 'w_spatial': _jnp.float32, 'b_spatial': _jnp.float32, 'conv_w': _jnp.float32, 'conv_b': _jnp.float32, 'conv_norm_g': _jnp.float32, 'conv_norm_b': _jnp.float32, 'g_out_a': _jnp.float32, 'g_out_b': _jnp.float32, 'w_mix_out': _jnp.float32, 'g_pre_f2': _jnp.float32, 'g_post_f2': _jnp.float32, 'w_f2_in': _jnp.float32, 'w_f2_out': _jnp.float32}
MOMENT_SCALE = {'w_ada': 1.085030e+00, 'b_ada': 2.037594e+00, 'g_pre_f1': 6.344382e-02, 'g_post_f1': 9.076577e-01, 'w_f1_in': 3.060727e-02, 'w_f1_out': 5.332201e-02, 'g_pre_m': 1.072641e-01, 'g_post_m': 3.741169e+00, 'w_mix_in': 1.049636e-01, 'gmlp_norm_g': 4.621143e-02, 'gmlp_norm_b': 4.598082e-02, 'w_spatial': 3.264994e-02, 'b_spatial': 4.754478e-02, 'conv_w': 1.357000e-01, 'conv_b': 6.824044e-01, 'conv_norm_g': 2.803333e-01, 'conv_norm_b': 4.471771e-01, 'g_out_a': 1.601953e-01, 'g_out_b': 1.902983e-01, 'w_mix_out': 1.935451e-01, 'g_pre_f2': 6.531422e-02, 'g_post_f2': 9.207680e-01, 'w_f2_in': 2.937242e-02, 'w_f2_out': 5.223750e-02}


def _to_microbatches(a, axis):
    t = _jnp.moveaxis(a, axis, 0)
    t = t.reshape((N_MICROBATCH, t.shape[0] // N_MICROBATCH) + t.shape[1:])
    return _jnp.moveaxis(t, 1, axis + 1)


def setup_inputs(seed: int = 0) -> dict:
    inp = _fwd_setup_inputs(seed)
    key = _jax.random.fold_in(_jax.random.key(seed), 7919)
    shape, _ = _output_shape()
    out = dict(inp)
    out["loss_target"] = _jax.random.normal(_jax.random.fold_in(key, 0), shape, _jnp.float32)
    for i, name in enumerate(TWIN_WEIGHTS):
        w = inp[name].astype(_jnp.float32)
        if MOMENT_SCALE is None:
            s = _jnp.sqrt(_jnp.mean(_jnp.square(w)) + 1e-30)
        else:
            s = MOMENT_SCALE[name]
        km, kv = _jax.random.split(_jax.random.fold_in(key, i + 1))
        out[name] = w
        out["m_" + name] = s * _jax.random.normal(km, w.shape, _jnp.float32)
        out["v_" + name] = (s * s) * _jax.random.uniform(kv, w.shape, _jnp.float32, 0.5, 1.5)
    if N_MICROBATCH > 1:
        for name, axis in PER_EXAMPLE_BATCH_AXIS.items():
            out[name] = _to_microbatches(out[name], axis)
    return {'x': out['x'], 'c': out['c'], 'w_ada': out['w_ada'], 'b_ada': out['b_ada'], 'g_pre_f1': out['g_pre_f1'], 'g_post_f1': out['g_post_f1'], 'w_f1_in': out['w_f1_in'], 'w_f1_out': out['w_f1_out'], 'g_pre_m': out['g_pre_m'], 'g_post_m': out['g_post_m'], 'w_mix_in': out['w_mix_in'], 'gmlp_norm_g': out['gmlp_norm_g'], 'gmlp_norm_b': out['gmlp_norm_b'], 'w_spatial': out['w_spatial'], 'b_spatial': out['b_spatial'], 'conv_w': out['conv_w'], 'conv_b': out['conv_b'], 'conv_norm_g': out['conv_norm_g'], 'conv_norm_b': out['conv_norm_b'], 'g_out_a': out['g_out_a'], 'g_out_b': out['g_out_b'], 'w_mix_out': out['w_mix_out'], 'g_pre_f2': out['g_pre_f2'], 'g_post_f2': out['g_post_f2'], 'w_f2_in': out['w_f2_in'], 'w_f2_out': out['w_f2_out'], 'loss_target': out['loss_target'], 'm_w_ada': out['m_w_ada'], 'm_b_ada': out['m_b_ada'], 'm_g_pre_f1': out['m_g_pre_f1'], 'm_g_post_f1': out['m_g_post_f1'], 'm_w_f1_in': out['m_w_f1_in'], 'm_w_f1_out': out['m_w_f1_out'], 'm_g_pre_m': out['m_g_pre_m'], 'm_g_post_m': out['m_g_post_m'], 'm_w_mix_in': out['m_w_mix_in'], 'm_gmlp_norm_g': out['m_gmlp_norm_g'], 'm_gmlp_norm_b': out['m_gmlp_norm_b'], 'm_w_spatial': out['m_w_spatial'], 'm_b_spatial': out['m_b_spatial'], 'm_conv_w': out['m_conv_w'], 'm_conv_b': out['m_conv_b'], 'm_conv_norm_g': out['m_conv_norm_g'], 'm_conv_norm_b': out['m_conv_norm_b'], 'm_g_out_a': out['m_g_out_a'], 'm_g_out_b': out['m_g_out_b'], 'm_w_mix_out': out['m_w_mix_out'], 'm_g_pre_f2': out['m_g_pre_f2'], 'm_g_post_f2': out['m_g_post_f2'], 'm_w_f2_in': out['m_w_f2_in'], 'm_w_f2_out': out['m_w_f2_out'], 'v_w_ada': out['v_w_ada'], 'v_b_ada': out['v_b_ada'], 'v_g_pre_f1': out['v_g_pre_f1'], 'v_g_post_f1': out['v_g_post_f1'], 'v_w_f1_in': out['v_w_f1_in'], 'v_w_f1_out': out['v_w_f1_out'], 'v_g_pre_m': out['v_g_pre_m'], 'v_g_post_m': out['v_g_post_m'], 'v_w_mix_in': out['v_w_mix_in'], 'v_gmlp_norm_g': out['v_gmlp_norm_g'], 'v_gmlp_norm_b': out['v_gmlp_norm_b'], 'v_w_spatial': out['v_w_spatial'], 'v_b_spatial': out['v_b_spatial'], 'v_conv_w': out['v_conv_w'], 'v_conv_b': out['v_conv_b'], 'v_conv_norm_g': out['v_conv_norm_g'], 'v_conv_norm_b': out['v_conv_norm_b'], 'v_g_out_a': out['v_g_out_a'], 'v_g_out_b': out['v_g_out_b'], 'v_w_mix_out': out['v_w_mix_out'], 'v_g_pre_f2': out['v_g_pre_f2'], 'v_g_post_f2': out['v_g_post_f2'], 'v_w_f2_in': out['v_w_f2_in'], 'v_w_f2_out': out['v_w_f2_out']}


def _loss(weights, diff, rest, loss_target):
    with _jax.named_scope("forward"):
        args = {**rest, TWIN_DIFF_INPUT: diff, **{k: w.astype(_WEIGHT_DTYPES[k]) for k, w in weights.items()}}
        y = _forward(args)
    with _jax.named_scope("loss_head"):
        err = _jnp.square(y.astype(_jnp.float32) - loss_target)
        return 0.5 * _jnp.sum(_jnp.mean(err, axis=-1)) if err.ndim else 0.5 * err


def _adamw(w, g, m, v):
    m = ADAM_B1 * m + (1.0 - ADAM_B1) * g
    v = ADAM_B2 * v + (1.0 - ADAM_B2) * _jnp.square(g)
    m_hat = m / (1.0 - ADAM_B1 ** ADAM_STEP)
    v_hat = v / (1.0 - ADAM_B2 ** ADAM_STEP)
    delta = -ADAM_LR * (m_hat / (_jnp.sqrt(v_hat) + ADAM_EPS) + ADAM_WD * w)
    return delta, m, v


def reference(x, c, w_ada, b_ada, g_pre_f1, g_post_f1, w_f1_in, w_f1_out, g_pre_m, g_post_m, w_mix_in, gmlp_norm_g, gmlp_norm_b, w_spatial, b_spatial, conv_w, conv_b, conv_norm_g, conv_norm_b, g_out_a, g_out_b, w_mix_out, g_pre_f2, g_post_f2, w_f2_in, w_f2_out, loss_target, m_w_ada, m_b_ada, m_g_pre_f1, m_g_post_f1, m_w_f1_in, m_w_f1_out, m_g_pre_m, m_g_post_m, m_w_mix_in, m_gmlp_norm_g, m_gmlp_norm_b, m_w_spatial, m_b_spatial, m_conv_w, m_conv_b, m_conv_norm_g, m_conv_norm_b, m_g_out_a, m_g_out_b, m_w_mix_out, m_g_pre_f2, m_g_post_f2, m_w_f2_in, m_w_f2_out, v_w_ada, v_b_ada, v_g_pre_f1, v_g_post_f1, v_w_f1_in, v_w_f1_out, v_g_pre_m, v_g_post_m, v_w_mix_in, v_gmlp_norm_g, v_gmlp_norm_b, v_w_spatial, v_b_spatial, v_conv_w, v_conv_b, v_conv_norm_g, v_conv_norm_b, v_g_out_a, v_g_out_b, v_w_mix_out, v_g_pre_f2, v_g_post_f2, v_w_f2_in, v_w_f2_out):
    given = dict(x=x, c=c, w_ada=w_ada, b_ada=b_ada, g_pre_f1=g_pre_f1, g_post_f1=g_post_f1, w_f1_in=w_f1_in, w_f1_out=w_f1_out, g_pre_m=g_pre_m, g_post_m=g_post_m, w_mix_in=w_mix_in, gmlp_norm_g=gmlp_norm_g, gmlp_norm_b=gmlp_norm_b, w_spatial=w_spatial, b_spatial=b_spatial, conv_w=conv_w, conv_b=conv_b, conv_norm_g=conv_norm_g, conv_norm_b=conv_norm_b, g_out_a=g_out_a, g_out_b=g_out_b, w_mix_out=w_mix_out, g_pre_f2=g_pre_f2, g_post_f2=g_post_f2, w_f2_in=w_f2_in, w_f2_out=w_f2_out, loss_target=loss_target, m_w_ada=m_w_ada, m_b_ada=m_b_ada, m_g_pre_f1=m_g_pre_f1, m_g_post_f1=m_g_post_f1, m_w_f1_in=m_w_f1_in, m_w_f1_out=m_w_f1_out, m_g_pre_m=m_g_pre_m, m_g_post_m=m_g_post_m, m_w_mix_in=m_w_mix_in, m_gmlp_norm_g=m_gmlp_norm_g, m_gmlp_norm_b=m_gmlp_norm_b, m_w_spatial=m_w_spatial, m_b_spatial=m_b_spatial, m_conv_w=m_conv_w, m_conv_b=m_conv_b, m_conv_norm_g=m_conv_norm_g, m_conv_norm_b=m_conv_norm_b, m_g_out_a=m_g_out_a, m_g_out_b=m_g_out_b, m_w_mix_out=m_w_mix_out, m_g_pre_f2=m_g_pre_f2, m_g_post_f2=m_g_post_f2, m_w_f2_in=m_w_f2_in, m_w_f2_out=m_w_f2_out, v_w_ada=v_w_ada, v_b_ada=v_b_ada, v_g_pre_f1=v_g_pre_f1, v_g_post_f1=v_g_post_f1, v_w_f1_in=v_w_f1_in, v_w_f1_out=v_w_f1_out, v_g_pre_m=v_g_pre_m, v_g_post_m=v_g_post_m, v_w_mix_in=v_w_mix_in, v_gmlp_norm_g=v_gmlp_norm_g, v_gmlp_norm_b=v_gmlp_norm_b, v_w_spatial=v_w_spatial, v_b_spatial=v_b_spatial, v_conv_w=v_conv_w, v_conv_b=v_conv_b, v_conv_norm_g=v_conv_norm_g, v_conv_norm_b=v_conv_norm_b, v_g_out_a=v_g_out_a, v_g_out_b=v_g_out_b, v_w_mix_out=v_w_mix_out, v_g_pre_f2=v_g_pre_f2, v_g_post_f2=v_g_post_f2, v_w_f2_in=v_w_f2_in, v_w_f2_out=v_w_f2_out)
    weights = {n: given[n] for n in TWIN_WEIGHTS}
    shared = {n: given[n] for n in SHARED_INPUTS}
    per_example = {n: given[n] for n in ['x', 'c']}
    grad_fn = _jax.value_and_grad(_loss, argnums=(0, 1))

    def one_microbatch(ex, loss_target):
        ex = dict(ex)
        diff = ex.pop(TWIN_DIFF_INPUT)
        return grad_fn(weights, diff, {**shared, **ex}, loss_target)

    if N_MICROBATCH == 1:
        loss, (grad_w, grad_x) = one_microbatch(per_example, given["loss_target"])
    else:
        def body(carry, xs):
            loss_sum, grad_sum = carry
            l_k, (gw_k, gx_k) = one_microbatch(xs[0], xs[1])
            with _jax.named_scope("update"):
                return (loss_sum + l_k, _jax.tree.map(_jnp.add, grad_sum, gw_k)), gx_k

        init = (_jnp.zeros((), _jnp.float32), _jax.tree.map(_jnp.zeros_like, weights))
        (loss, grad_w), grad_x = _jax.lax.scan(body, init, (per_example, given["loss_target"]))
    with _jax.named_scope("update"):
        delta_w, new_m, new_v = {}, {}, {}
        for n in TWIN_WEIGHTS:
            delta_w[n], new_m[n], new_v[n] = _adamw(weights[n], grad_w[n], given["m_" + n], given["v_" + n])
    return (loss, grad_x, *[grad_w[n] for n in TWIN_WEIGHTS], *[delta_w[n] for n in TWIN_WEIGHTS],
            *[new_m[n] for n in TWIN_WEIGHTS], *[new_v[n] for n in TWIN_WEIGHTS])
```

```python
import functools

import jax
import jax.numpy as jnp
from jax import lax
from jax.experimental import pallas as pl
from jax.experimental.pallas import tpu as pltpu

F32 = jnp.float32
BF16 = jnp.bfloat16

D = 1024
DFF = 2816
NDEV = 8
FB = 2 * DFF // NDEV
FBP = 768
FO = DFF // NDEV
WA = 512
NHEAD = 8
HD = 64
CHUNK = 128
CONV_K = 31
HALO = 32
MB = 2 * (WA + WA) // NDEV
MO = D // NDEV
ADA_B = 9 * D // NDEV
EPS = 1e-6
HALF = 0.5

ADAM_LR = 0.001
ADAM_B1 = 0.9
ADAM_B2 = 0.999
ADAM_EPS = 1e-08
ADAM_WD = 0.01
ADAM_STEP = 10

VMEM_LIMIT = 56 * 1024 * 1024
MESH = pl.DeviceIdType.MESH
FLIPS = ((0, 0, 1), (1, 0, 0), (0, 1, 0), (1, 1, 0), (1, 0, 1), (0, 1, 1), (1, 1, 1))


def _dot(a, b):
    return lax.dot_general(a, b, (((1,), (0,)), ((), ())), preferred_element_type=F32)


def _dot_nt(a, b):
    return lax.dot_general(a, b, (((1,), (1,)), ((), ())), preferred_element_type=F32)


def _dot_tn(a, b):
    return lax.dot_general(a, b, (((0,), (0,)), ((), ())), preferred_element_type=F32)


def _rowmean(v):
    return jnp.mean(v, axis=-1, keepdims=True)


def _colsum(v):
    return jnp.sum(v, axis=0, keepdims=True)


def _sigmoid(v):
    return 1.0 / (1.0 + jnp.exp(-v))


def _params(n_axes=1):
    return pltpu.CompilerParams(dimension_semantics=("arbitrary",) * n_axes, vmem_limit_bytes=VMEM_LIMIT)


def _const_spec(shape):
    nd = len(shape)
    return pl.BlockSpec(shape, lambda *_: (0,) * nd, pipeline_mode=pl.Buffered(1))


def _me():
    return lax.axis_index("x"), lax.axis_index("y"), lax.axis_index("c")


def _flip(me, f):
    return tuple(1 - v if b else v for v, b in zip(me, f))


def _lin(p):
    return 4 * p[0] + 2 * p[1] + p[2]


def _ffn_fwd(x, mod, gvec, w_in, w_out, tm, name):
    T = x.shape[0]
    nt = T // tm
    tps = nt // mod.shape[0]

    def body(x_ref, mod_ref, g_ref, win_ref, wout_ref, xo_ref, gu_ref, y_ref):
        xv = x_ref[...]
        sh, sc, gt = mod_ref[0:1, :], mod_ref[1:2, :], mod_ref[2:3, :]
        r = lax.rsqrt(_rowmean(xv * xv) + EPS)
        h = (xv * r * g_ref[0:1, :]) * (1.0 + sc) + sh
        hb = h.astype(BF16)
        y = jnp.zeros((tm, D), F32)
        for cidx in range(4):
            gate = _dot(hb, win_ref[cidx])
            up = _dot(hb, win_ref[4 + cidx])
            gu_ref[cidx] = gate.astype(BF16)
            gu_ref[4 + cidx] = up.astype(BF16)
            act = gate * _sigmoid(gate) * up
            y = y + _dot(act.astype(BF16), wout_ref[cidx])
        y_ref[...] = y
        ry = lax.rsqrt(_rowmean(y * y) + EPS)
        xo_ref[...] = xv + (HALF * gt) * (y * ry * g_ref[1:2, :])

    return pl.pallas_call(
        body,
        name=name,
        grid=(nt,),
        in_specs=[
            pl.BlockSpec((tm, D), lambda i: (i, 0)),
            pl.BlockSpec((None, 8, D), lambda i: (i // tps, 0, 0)),
            _const_spec((8, D)),
            _const_spec((8, D, FBP)),
            _const_spec((4, FBP, D)),
        ],
        out_specs=[
            pl.BlockSpec((tm, D), lambda i: (i, 0)),
            pl.BlockSpec((8, tm, FBP), lambda i: (0, i, 0)),
            pl.BlockSpec((tm, D), lambda i: (i, 0)),
        ],
        out_shape=[
            jax.ShapeDtypeStruct((T, D), F32),
            jax.ShapeDtypeStruct((8, T, FBP), BF16),
            jax.ShapeDtypeStruct((T, D), F32),
        ],
        compiler_params=_params(),
    )(x, mod, gvec, w_in, w_out)


def _ffn_bwd(dxo, x, y, gu, mod, gvec, w_in, w_out, tm, name):
    T = x.shape[0]
    nt = T // tm
    nb = mod.shape[0]
    tps = nt // nb

    def body(dxo_ref, x_ref, y_ref, gu_ref, mod_ref, g_ref, win_ref, wout_ref,
             dx_ref, dg_ref, act_ref, hb_ref, dyb_ref, mg_ref, vg_ref):
        i = pl.program_id(0)
        xv = x_ref[...]
        dxo_v = dxo_ref[...]
        yv = y_ref[...]
        sh, sc, gt = mod_ref[0:1, :], mod_ref[1:2, :], mod_ref[2:3, :]
        gpre, gpost = g_ref[0:1, :], g_ref[1:2, :]
        r = lax.rsqrt(_rowmean(xv * xv) + EPS)
        xh = xv * r
        n = xh * gpre
        hb = (n * (1.0 + sc) + sh).astype(BF16)
        hb_ref[...] = hb
        ry = lax.rsqrt(_rowmean(yv * yv) + EPS)
        yh = yv * ry
        d_gt = _colsum(HALF * dxo_v * (yh * gpost))
        dp = (HALF * gt) * dxo_v
        d_gpost = _colsum(dp * yh)
        dyh = dp * gpost
        dy = ry * (dyh - yh * _rowmean(dyh * yh))
        dyb = dy.astype(BF16)
        dyb_ref[...] = dyb
        dh = jnp.zeros((tm, D), F32)
        for cidx in range(4):
            gate = gu_ref[cidx].astype(F32)
            up = gu_ref[4 + cidx].astype(F32)
            sig = _sigmoid(gate)
            s = gate * sig
            act_ref[cidx] = (s * up).astype(BF16)
            d_act = _dot_nt(dyb, wout_ref[cidx])
            d_up = (d_act * s).astype(BF16)
            d_gate = (d_act * up * (sig * (1.0 + gate * (1.0 - sig)))).astype(BF16)
            dg_ref[cidx] = d_gate
            dg_ref[4 + cidx] = d_up
            dh = dh + _dot_nt(d_gate, win_ref[cidx]) + _dot_nt(d_up, win_ref[4 + cidx])
        d_sc = _colsum(dh * n)
        d_sh = _colsum(dh)
        dn = dh * (1.0 + sc)
        d_gpre = _colsum(dn * xh)
        dxh = dn * gpre
        dx_ref[...] = dxo_v + r * (dxh - xh * _rowmean(dxh * xh))

        @pl.when(i % tps == 0)
        def _():
            mg_ref[...] = jnp.zeros((8, D), F32)

        @pl.when(i == 0)
        def _():
            vg_ref[...] = jnp.zeros((8, D), F32)

        mg_ref[0:1, :] += d_sh
        mg_ref[1:2, :] += d_sc
        mg_ref[2:3, :] += d_gt
        vg_ref[0:1, :] += d_gpre
        vg_ref[1:2, :] += d_gpost

    tile = pl.BlockSpec((tm, D), lambda i: (i, 0))
    return pl.pallas_call(
        body,
        name=name,
        grid=(nt,),
        in_specs=[
            tile, tile, tile,
            pl.BlockSpec((8, tm, FBP), lambda i: (0, i, 0)),
            pl.BlockSpec((None, 8, D), lambda i: (i // tps, 0, 0)),
            _const_spec((8, D)),
            _const_spec((8, D, FBP)),
            _const_spec((4, FBP, D)),
        ],
        out_specs=[
            tile,
            pl.BlockSpec((8, tm, FBP), lambda i: (0, i, 0)),
            pl.BlockSpec((4, tm, FBP), lambda i: (0, i, 0)),
            tile, tile,
            pl.BlockSpec((None, 8, D), lambda i: (i // tps, 0, 0)),
            pl.BlockSpec((8, D), lambda i: (0, 0)),
        ],
        out_shape=[
            jax.ShapeDtypeStruct((T, D), F32),
            jax.ShapeDtypeStruct((8, T, FBP), BF16),
            jax.ShapeDtypeStruct((4, T, FBP), BF16),
            jax.ShapeDtypeStruct((T, D), BF16),
            jax.ShapeDtypeStruct((T, D), BF16),
            jax.ShapeDtypeStruct((nb, 8, D), F32),
            jax.ShapeDtypeStruct((8, D), F32),
        ],
        compiler_params=_params(),
    )(dxo, x, y, gu, mod, gvec, w_in, w_out)


def _masked_spatial(ws_ref):
    row = lax.broadcasted_iota(jnp.int32, (CHUNK, CHUNK), 0)
    col = lax.broadcasted_iota(jnp.int32, (CHUNK, CHUNK), 1)
    keep = col <= row
    return [jnp.where(keep, ws_ref[hd], 0.0).astype(BF16) for hd in range(NHEAD)]


def _spatial_gate(wm, vb_chunk, lane_head):
    z = jnp.zeros((CHUNK, WA), F32)
    for hd in range(NHEAD):
        z = jnp.where(lane_head == hd, _dot(wm[hd], vb_chunk), z)
    return z


def _layer_norm_stats(v):
    mu = _rowmean(v)
    vc = v - mu
    rstd = lax.rsqrt(_rowmean(vc * vc) + EPS)
    return vc * rstd, rstd


def _causal_conv(ext_ref, cw_ref, bias, tm, rows=64):
    off = HALO - (CONV_K - 1)
    out = []
    for rc in range(tm // rows):
        acc = jnp.broadcast_to(bias, (rows, WA))
        for k in range(CONV_K):
            acc = acc + cw_ref[k:k + 1, :] * ext_ref[rc * rows + off + k:rc * rows + off + k + rows, :]
        out.append(acc)
    return jnp.concatenate(out, axis=0)


def _mixer_fwd(x, mod, gvec, w_mi, w_mo, v512, ws, bias_full, cw, tm, name):
    T = x.shape[0]
    nt = T // tm
    tps = nt // mod.shape[0]

    def body(x_ref, mod_ref, g_ref, wmi_ref, wmo_ref, v_ref, ws_ref, bias_ref, cw_ref,
             xo_ref, proj_ref, ym_ref, glu_ext):
        i = pl.program_id(0)
        xv = x_ref[...]
        sh, sc, gt = mod_ref[0:1, :], mod_ref[1:2, :], mod_ref[2:3, :]
        r = lax.rsqrt(_rowmean(xv * xv) + EPS)
        hb = ((xv * r * g_ref[0:1, :]) * (1.0 + sc) + sh).astype(BF16)
        for j in range(NDEV):
            proj_ref[:, j * MB:(j + 1) * MB] = _dot(hb, wmi_ref[j])
        u = proj_ref[:, 0:WA]
        v0 = proj_ref[:, WA:2 * WA]
        a = proj_ref[:, 2 * WA:3 * WA]
        g = proj_ref[:, 3 * WA:4 * WA]
        vh, _ = _layer_norm_stats(v0)
        vb = (vh * v_ref[0:1, :] + v_ref[1:2, :]).astype(BF16)
        wm = _masked_spatial(ws_ref)
        lane_head = lax.broadcasted_iota(jnp.int32, (CHUNK, WA), 1) >> 6
        ya = []
        for q in range(tm // CHUNK):
            z = _spatial_gate(wm, vb[q * CHUNK:(q + 1) * CHUNK, :], lane_head) + bias_ref[...]
            ya.append(u[q * CHUNK:(q + 1) * CHUNK, :] * z)
        ya = jnp.concatenate(ya, axis=0)
        glu = a * _sigmoid(g)

        @pl.when(i % tps == 0)
        def _():
            glu_ext[0:HALO, :] = jnp.zeros((HALO, WA), F32)

        glu_ext[HALO:HALO + tm, :] = glu
        conv = _causal_conv(glu_ext, cw_ref, v_ref[2:3, :], tm)
        glu_ext[0:HALO, :] = glu_ext[tm:tm + HALO, :]
        ch, _ = _layer_norm_stats(conv)
        cn = ch * v_ref[3:4, :] + v_ref[4:5, :]
        yb = cn * _sigmoid(cn)
        pa = ya * lax.rsqrt(_rowmean(ya * ya) + EPS) * v_ref[5:6, :]
        pb = yb * lax.rsqrt(_rowmean(yb * yb) + EPS) * v_ref[6:7, :]
        ycat = jnp.concatenate([pa, pb], axis=1).astype(BF16)
        ym = _dot(ycat, wmo_ref[...])
        ym_ref[...] = ym
        rm = lax.rsqrt(_rowmean(ym * ym) + EPS)
        xo_ref[...] = xv + gt * (ym * rm * g_ref[1:2, :])

    tile = pl.BlockSpec((tm, D), lambda i: (i, 0))
    return pl.pallas_call(
        body,
        name=name,
        grid=(nt,),
        in_specs=[
            tile,
            pl.BlockSpec((None, 8, D), lambda i: (i // tps, 0, 0)),
            _const_spec((8, D)),
            _const_spec((NDEV, D, MB)),
            _const_spec((D, D)),
            _const_spec((8, WA)),
            _const_spec((NHEAD, CHUNK, CHUNK)),
            _const_spec((CHUNK, WA)),
            _const_spec((32, WA)),
        ],
        out_specs=[tile, pl.BlockSpec((tm, 4 * WA), lambda i: (i, 0)), tile],
        out_shape=[
            jax.ShapeDtypeStruct((T, D), F32),
            jax.ShapeDtypeStruct((T, 4 * WA), F32),
            jax.ShapeDtypeStruct((T, D), F32),
        ],
        scratch_shapes=[pltpu.VMEM((tm + HALO, WA), F32)],
        compiler_params=_params(),
    )(x, mod, gvec, w_mi, w_mo, v512, ws, bias_full, cw)


def _mixer_bwd_a(dxo, ym, proj, mod, gvec, w_mo, v512, ws, bias_full, cw, esel, tm, name):
    T = dxo.shape[0]
    nt = T // tm
    nb = mod.shape[0]
    tps = nt // nb
    hpt = tm // HALO

    def body(dxo_ref, ym_ref, proj_ref, halo_ref, mod_ref, g_ref, wmo_ref, v_ref, ws_ref, bias_ref, cw_ref,
             e_ref, dpart_ref, dymb_ref, ycat_ref, mg_ref, vg_ref, v5g_ref, gws_ref, gbs_ref,
             glu_ext, dbs_acc):
        i = pl.program_id(0)
        dxo_v = dxo_ref[...]
        ymv = ym_ref[...]
        gt = mod_ref[2:3, :]
        gpost = g_ref[1:2, :]
        rm = lax.rsqrt(_rowmean(ymv * ymv) + EPS)
        ymh = ymv * rm
        d_gt = _colsum(dxo_v * (ymh * gpost))
        dpm = gt * dxo_v
        d_gpost = _colsum(dpm * ymh)
        dymh = dpm * gpost
        dym = (rm * (dymh - ymh * _rowmean(dymh * ymh))).astype(BF16)
        dymb_ref[...] = dym
        dycat = _dot_nt(dym, wmo_ref[...])
        u = proj_ref[:, 0:WA]
        v0 = proj_ref[:, WA:2 * WA]
        a = proj_ref[:, 2 * WA:3 * WA]
        g = proj_ref[:, 3 * WA:4 * WA]
        vh, rv = _layer_norm_stats(v0)
        vb = (vh * v_ref[0:1, :] + v_ref[1:2, :]).astype(BF16)
        wm = _masked_spatial(ws_ref)
        lane_head = lax.broadcasted_iota(jnp.int32, (CHUNK, WA), 1) >> 6
        zs = []
        for q in range(tm // CHUNK):
            zs.append(_spatial_gate(wm, vb[q * CHUNK:(q + 1) * CHUNK, :], lane_head) + bias_ref[...])
        z = jnp.concatenate(zs, axis=0)
        ya = u * z
        ra = lax.rsqrt(_rowmean(ya * ya) + EPS)
        yah = ya * ra
        first = i % tps == 0
        ah = halo_ref[:, 0:WA]
        gh = halo_ref[:, WA:2 * WA]
        glu_ext[0:HALO, :] = jnp.where(first, 0.0, ah * _sigmoid(gh))
        glu_ext[HALO:HALO + tm, :] = a * _sigmoid(g)
        conv = _causal_conv(glu_ext, cw_ref, v_ref[2:3, :], tm)
        ch, rc = _layer_norm_stats(conv)
        cn = ch * v_ref[3:4, :] + v_ref[4:5, :]
        sg = _sigmoid(cn)
        yb = cn * sg
        rb = lax.rsqrt(_rowmean(yb * yb) + EPS)
        ybh = yb * rb
        ycat_ref[...] = jnp.concatenate([yah * v_ref[5:6, :], ybh * v_ref[6:7, :]], axis=1).astype(BF16)
        dpa = dycat[:, 0:WA]
        dpb = dycat[:, WA:2 * WA]
        d_goa = _colsum(dpa * yah)
        d_gob = _colsum(dpb * ybh)
        dyah = dpa * v_ref[5:6, :]
        dybh = dpb * v_ref[6:7, :]
        dya = ra * (dyah - yah * _rowmean(dyah * yah))
        dyb = rb * (dybh - ybh * _rowmean(dybh * ybh))
        dpart_ref[:, 0:WA] = dya * z
        dz = dya * u

        @pl.when(i == 0)
        def _():
            gws_ref[...] = jnp.zeros((NHEAD, CHUNK, CHUNK), F32)
            dbs_acc[...] = jnp.zeros((CHUNK, WA), F32)
            vg_ref[...] = jnp.zeros((8, D), F32)
            v5g_ref[...] = jnp.zeros((8, WA), F32)

        dvs = []
        for q in range(tm // CHUNK):
            dz_q = dz[q * CHUNK:(q + 1) * CHUNK, :]
            vb_q = vb[q * CHUNK:(q + 1) * CHUNK, :]
            dbs_acc[...] += dz_q
            dzb = dz_q.astype(BF16)
            dv_q = jnp.zeros((CHUNK, WA), F32)
            for hd in range(NHEAD):
                dv_q = jnp.where(lane_head == hd, _dot_tn(wm[hd], dzb), dv_q)
                dz_hd = jnp.where(lane_head == hd, dz_q, 0.0).astype(BF16)
                gws_ref[hd] += _dot_nt(dz_hd, vb_q)
            dvs.append(dv_q)
        dv = jnp.concatenate(dvs, axis=0)
        d_gng = _colsum(dv * vh)
        d_gnb = _colsum(dv)
        dvh = dv * v_ref[0:1, :]
        dpart_ref[:, WA:2 * WA] = rv * (dvh - _rowmean(dvh) - vh * _rowmean(dvh * vh))
        dcn = dyb * (sg * (1.0 + cn * (1.0 - sg)))
        d_cng = _colsum(dcn * ch)
        d_cnb = _colsum(dcn)
        dch = dcn * v_ref[3:4, :]
        dconv = rc * (dch - _rowmean(dch) - ch * _rowmean(dch * ch))
        dpart_ref[:, 2 * WA:3 * WA] = dconv
        dpart_ref[:, 3 * WA:4 * WA] = jnp.zeros((tm, WA), F32)
        d_cb = _colsum(dconv)

        @pl.when(i % tps == 0)
        def _():
            mg_ref[...] = jnp.zeros((8, D), F32)

        mg_ref[2:3, :] += d_gt
        vg_ref[1:2, :] += d_gpost
        v5g_ref[0:1, :] += d_gng
        v5g_ref[1:2, :] += d_gnb
        v5g_ref[2:3, :] += d_cb
        v5g_ref[3:4, :] += d_cng
        v5g_ref[4:5, :] += d_cnb
        v5g_ref[5:6, :] += d_goa
        v5g_ref[6:7, :] += d_gob

        @pl.when(i == nt - 1)
        def _():
            row = lax.broadcasted_iota(jnp.int32, (CHUNK, CHUNK), 0)
            col = lax.broadcasted_iota(jnp.int32, (CHUNK, CHUNK), 1)
            for hd in range(NHEAD):
                gws_ref[hd] = jnp.where(col <= row, gws_ref[hd], 0.0)
            gbs_ref[...] = lax.dot_general(e_ref[...], dbs_acc[...], (((1,), (1,)), ((), ())),
                                           precision=lax.Precision.HIGHEST, preferred_element_type=F32)

    tile = pl.BlockSpec((tm, D), lambda i: (i, 0))
    ptile = pl.BlockSpec((tm, 4 * WA), lambda i: (i, 0))
    return pl.pallas_call(
        body,
        name=name,
        grid=(nt,),
        in_specs=[
            tile, tile, ptile,
            pl.BlockSpec((HALO, 2 * WA), lambda i: (jnp.maximum(i * hpt - 1, 0), 1)),
            pl.BlockSpec((None, 8, D), lambda i: (i // tps, 0, 0)),
            _const_spec((8, D)),
            _const_spec((D, D)),
            _const_spec((8, WA)),
            _const_spec((NHEAD, CHUNK, CHUNK)),
            _const_spec((CHUNK, WA)),
            _const_spec((32, WA)),
            _const_spec((8, WA)),
        ],
        out_specs=[
            ptile,
            pl.BlockSpec((tm, D), lambda i: (i, 0)),
            pl.BlockSpec((tm, D), lambda i: (i, 0)),
            pl.BlockSpec((None, 8, D), lambda i: (i // tps, 0, 0)),
            pl.BlockSpec((8, D), lambda i: (0, 0)),
            pl.BlockSpec((8, WA), lambda i: (0, 0)),
            pl.BlockSpec((NHEAD, CHUNK, CHUNK), lambda i: (0, 0, 0)),
            pl.BlockSpec((8, CHUNK), lambda i: (0, 0)),
        ],
        out_shape=[
            jax.ShapeDtypeStruct((T, 4 * WA), F32),
            jax.ShapeDtypeStruct((T, D), BF16),
            jax.ShapeDtypeStruct((T, D), BF16),
            jax.ShapeDtypeStruct((nb, 8, D), F32),
            jax.ShapeDtypeStruct((8, D), F32),
            jax.ShapeDtypeStruct((8, WA), F32),
            jax.ShapeDtypeStruct((NHEAD, CHUNK, CHUNK), F32),
            jax.ShapeDtypeStruct((8, CHUNK), F32),
        ],
        scratch_shapes=[pltpu.VMEM((tm + HALO, WA), F32), pltpu.VMEM((CHUNK, WA), F32)],
        compiler_params=_params(),
    )(dxo, ym, proj, proj, mod, gvec, w_mo, v512, ws, bias_full, cw, esel)


def _mixer_bwd_b(dxo, x, dpart, proj, mod, gvec, w_mi, cw, tm, name):
    T = x.shape[0]
    nt = T // tm
    nb = mod.shape[0]
    tps = nt // nb
    hpt = tm // HALO
    nh = T // HALO
    off = HALO - (CONV_K - 1)
    rows = 64

    def body(dxo_ref, x_ref, dpart_ref, dnext_ref, ag_ref, halo_ref, mod_ref, g_ref, wmi_ref, cw_ref,
             dx_ref, dproj_ref, hb_ref, mg_ref, vg_ref, dcw_ref, glu_ext, dconv_ext):
        i = pl.program_id(0)
        first = i % tps == 0
        last = i % tps == tps - 1
        a = ag_ref[:, 0:WA]
        g = ag_ref[:, WA:2 * WA]
        sgg = _sigmoid(g)
        glu_ext[0:HALO, :] = jnp.where(first, 0.0, halo_ref[:, 0:WA] * _sigmoid(halo_ref[:, WA:2 * WA]))
        glu_ext[HALO:HALO + tm, :] = a * sgg
        dconv_ext[0:tm, :] = dpart_ref[:, 2 * WA:3 * WA]
        dconv_ext[tm:tm + HALO, :] = jnp.where(last, 0.0, dnext_ref[...])

        @pl.when(i == 0)
        def _():
            dcw_ref[...] = jnp.zeros((32, WA), F32)
            vg_ref[...] = jnp.zeros((8, D), F32)

        dglu = []
        for rc in range(tm // rows):
            dc = dconv_ext[rc * rows:(rc + 1) * rows, :]
            acc = jnp.zeros((rows, WA), F32)
            for k in range(CONV_K):
                dcw_ref[k:k + 1, :] += _colsum(dc * glu_ext[rc * rows + off + k:rc * rows + off + k + rows, :])
                acc = acc + cw_ref[k:k + 1, :] * dconv_ext[rc * rows + (CONV_K - 1) - k:
                                                            rc * rows + (CONV_K - 1) - k + rows, :]
            dglu.append(acc)
        dglu = jnp.concatenate(dglu, axis=0)
        da = dglu * sgg
        dgg = dglu * a * (sgg * (1.0 - sgg))
        dproj_ref[:, 0:2 * WA] = dpart_ref[:, 0:2 * WA].astype(BF16)
        dproj_ref[:, 2 * WA:3 * WA] = da.astype(BF16)
        dproj_ref[:, 3 * WA:4 * WA] = dgg.astype(BF16)
        dh = jnp.zeros((tm, D), F32)
        for j in range(NDEV):
            dh = dh + _dot_nt(dproj_ref[:, j * MB:(j + 1) * MB], wmi_ref[j])
        xv = x_ref[...]
        sc, sh = mod_ref[1:2, :], mod_ref[0:1, :]
        gpre = g_ref[0:1, :]
        r = lax.rsqrt(_rowmean(xv * xv) + EPS)
        xh = xv * r
        n = xh * gpre
        hb_ref[...] = (n * (1.0 + sc) + sh).astype(BF16)
        d_sc = _colsum(dh * n)
        d_sh = _colsum(dh)
        dn = dh * (1.0 + sc)
        d_gpre = _colsum(dn * xh)
        dxh = dn * gpre
        dx_ref[...] = dxo_ref[...] + r * (dxh - xh * _rowmean(dxh * xh))

        @pl.when(first)
        def _():
            mg_ref[...] = jnp.zeros((8, D), F32)

        mg_ref[0:1, :] += d_sh
        mg_ref[1:2, :] += d_sc
        vg_ref[0:1, :] += d_gpre

    tile = pl.BlockSpec((tm, D), lambda i: (i, 0))
    return pl.pallas_call(
        body,
        name=name,
        grid=(nt,),
        in_specs=[
            tile, tile,
            pl.BlockSpec((tm, 4 * WA), lambda i: (i, 0)),
            pl.BlockSpec((HALO, WA), lambda i: (jnp.minimum((i + 1) * hpt, nh - 1), 2)),
            pl.BlockSpec((tm, 2 * WA), lambda i: (i, 1)),
            pl.BlockSpec((HALO, 2 * WA), lambda i: (jnp.maximum(i * hpt - 1, 0), 1)),
            pl.BlockSpec((None, 8, D), lambda i: (i // tps, 0, 0)),
            _const_spec((8, D)),
            _const_spec((NDEV, D, MB)),
            _const_spec((32, WA)),
        ],
        out_specs=[
            tile,
            pl.BlockSpec((tm, 4 * WA), lambda i: (i, 0)),
            tile,
            pl.BlockSpec((None, 8, D), lambda i: (i // tps, 0, 0)),
            pl.BlockSpec((8, D), lambda i: (0, 0)),
            pl.BlockSpec((32, WA), lambda i: (0, 0)),
        ],
        out_shape=[
            jax.ShapeDtypeStruct((T, D), F32),
            jax.ShapeDtypeStruct((T, 4 * WA), BF16),
            jax.ShapeDtypeStruct((T, D), BF16),
            jax.ShapeDtypeStruct((nb, 8, D), F32),
            jax.ShapeDtypeStruct((8, D), F32),
            jax.ShapeDtypeStruct((32, WA), F32),
        ],
        scratch_shapes=[pltpu.VMEM((tm + HALO, WA), F32), pltpu.VMEM((tm + HALO, WA), F32)],
        compiler_params=_params(),
    )(dxo, x, dpart, dpart, proj, proj, mod, gvec, w_mi, cw)


def _loss_head(y, target, tm, name):
    T = y.shape[0]

    def body(y_ref, t_ref, dy_ref, loss_ref):
        @pl.when(pl.program_id(0) == 0)
        def _():
            loss_ref[...] = jnp.zeros((8, 128), F32)

        err = y_ref[...] - t_ref[...]
        dy_ref[...] = err * (1.0 / D)
        part = jnp.sum(_rowmean(err * err), axis=0, keepdims=True)
        loss_ref[...] += HALF * part

    tile = pl.BlockSpec((tm, D), lambda i: (i, 0))
    return pl.pallas_call(
        body,
        name=name,
        grid=(T // tm,),
        in_specs=[tile, tile],
        out_specs=[tile, pl.BlockSpec((8, 128), lambda i: (0, 0))],
        out_shape=[jax.ShapeDtypeStruct((T, D), F32), jax.ShapeDtypeStruct((8, 128), F32)],
        compiler_params=_params(),
    )(y, target)


def _matmul_tn(a, b, a_spec, b_spec, o_spec, out_shape, acc_shape, grid, name):
    nk = grid[1]

    def body(a_ref, b_ref, o_ref, acc):
        k = pl.program_id(1)

        @pl.when(k == 0)
        def _():
            acc[...] = jnp.zeros(acc_shape, F32)

        acc[...] += _dot_tn(a_ref[...], b_ref[...])

        @pl.when(k == nk - 1)
        def _():
            o_ref[...] = acc[...].astype(BF16)

    return pl.pallas_call(
        body,
        name=name,
        grid=grid,
        in_specs=[a_spec, b_spec],
        out_specs=o_spec,
        out_shape=jax.ShapeDtypeStruct(out_shape, BF16),
        scratch_shapes=[pltpu.VMEM(acc_shape, F32)],
        compiler_params=_params(2),
    )(a, b)


def _grad_w_in(hb, dg, tk, name):
    T = hb.shape[0]
    return _matmul_tn(
        hb, dg,
        pl.BlockSpec((tk, D), lambda j, k: (k, 0)),
        pl.BlockSpec((None, tk, FBP), lambda j, k: (j, k, 0)),
        pl.BlockSpec((None, D, FBP), lambda j, k: (j, 0, 0)),
        (8, D, FBP), (D, FBP), (8, T // tk), name)


def _grad_w_out(act, dyb, tk, name):
    T = dyb.shape[0]
    return _matmul_tn(
        act, dyb,
        pl.BlockSpec((None, tk, FBP), lambda j, k: (j, k, 0)),
        pl.BlockSpec((tk, D), lambda j, k: (k, 0)),
        pl.BlockSpec((None, FBP, D), lambda j, k: (j, 0, 0)),
        (4, FBP, D), (FBP, D), (4, T // tk), name)


def _grad_w_mi(hb, dproj, tk, name):
    T = hb.shape[0]
    return _matmul_tn(
        hb, dproj,
        pl.BlockSpec((tk, D), lambda j, k: (k, 0)),
        pl.BlockSpec((tk, MB), lambda j, k: (k, j)),
        pl.BlockSpec((None, D, MB), lambda j, k: (j, 0, 0)),
        (8, D, MB), (D, MB), (8, T // tk), name)


def _grad_w_mo(ycat, dym, tk, name):
    T = ycat.shape[0]
    return _matmul_tn(
        ycat, dym,
        pl.BlockSpec((tk, 256), lambda j, k: (k, j)),
        pl.BlockSpec((tk, D), lambda j, k: (k, 0)),
        pl.BlockSpec((256, D), lambda j, k: (j, 0)),
        (D, D), (256, D), (4, T // tk), name)


def _adamw_math(w, g, m, v):
    m2 = ADAM_B1 * m + (1.0 - ADAM_B1) * g
    v2 = ADAM_B2 * v + (1.0 - ADAM_B2) * (g * g)
    m_hat = m2 / (1.0 - ADAM_B1 ** ADAM_STEP)
    v_hat = v2 / (1.0 - ADAM_B2 ** ADAM_STEP)
    delta = -ADAM_LR * (m_hat / (jnp.sqrt(v_hat) + ADAM_EPS) + ADAM_WD * w)
    return delta, m2, v2


def _adamw_reduce(parts, w, m, v, tr, name):
    R, C = w.shape
    Cp = parts.shape[2]

    def body(p_ref, w_ref, m_ref, v_ref, g_ref, d_ref, m2_ref, v2_ref):
        g = p_ref[0].astype(F32)
        for s in range(1, NDEV):
            g = g + p_ref[s].astype(F32)
        g = g[:, 0:C]
        g_ref[...] = g
        d_ref[...], m2_ref[...], v2_ref[...] = _adamw_math(w_ref[...], g, m_ref[...], v_ref[...])

    blk = pl.BlockSpec((tr, C), lambda i: (i, 0))
    return pl.pallas_call(
        body,
        name=name,
        grid=(R // tr,),
        in_specs=[pl.BlockSpec((NDEV, tr, Cp), lambda i: (0, i, 0)), blk, blk, blk],
        out_specs=[blk, blk, blk, blk],
        out_shape=[jax.ShapeDtypeStruct((R, C), F32)] * 4,
        compiler_params=_params(),
    )(parts, w, m, v)


def _adamw_plain(w, g, m, v, tr, name):
    R, C = w.shape

    def body(w_ref, g_ref, m_ref, v_ref, d_ref, m2_ref, v2_ref):
        d_ref[...], m2_ref[...], v2_ref[...] = _adamw_math(w_ref[...], g_ref[...], m_ref[...], v_ref[...])

    blk = pl.BlockSpec((tr, C), lambda i: (i, 0))
    return pl.pallas_call(
        body,
        name=name,
        grid=(R // tr,),
        in_specs=[blk, blk, blk, blk],
        out_specs=[blk, blk, blk],
        out_shape=[jax.ShapeDtypeStruct((R, C), F32)] * 3,
        compiler_params=_params(),
    )(w, g, m, v)


def _remote(src, dst, send_sem, recv_sem, dev):
    return pltpu.make_async_remote_copy(src_ref=src, dst_ref=dst, send_sem=send_sem, recv_sem=recv_sem,
                                        device_id=dev, device_id_type=MESH)


def _ada_fwd(c_pad, w_ada_b, b_cols, cw_pad):
    def body(c_ref, w_ref, b_ref, cwp_ref, ada_ref, sc_ref, cw_ref, cbuf, send_buf, ssem, rsem):
        me = _me()
        mi = _lin(me)
        cbuf[mi] = c_ref[...]
        cw_ref[mi] = cwp_ref[...]
        peers = [_flip(me, f) for f in FLIPS]
        first = []
        for k, p in enumerate(peers):
            first.append(_remote(cbuf.at[mi], cbuf.at[mi], ssem.at[k], rsem.at[k], p))
            first.append(_remote(cw_ref.at[mi], cw_ref.at[mi], ssem.at[7 + k], rsem.at[7 + k], p))
        for cp in first:
            cp.start()
        for k, p in enumerate(peers):
            pi = _lin(p)
            _remote(cbuf.at[pi], cbuf.at[pi], ssem.at[k], rsem.at[k], p).wait_recv()
            _remote(cw_ref.at[pi], cw_ref.at[pi], ssem.at[7 + k], rsem.at[7 + k], p).wait_recv()
        c_all = cbuf[...].reshape(8 * 8, D)
        sc = c_all * _sigmoid(c_all)
        sc_ref[...] = sc
        res = _dot(sc.astype(BF16), w_ref[...]) + b_ref[...]
        send_buf[...] = res.reshape(8, 8, ADA_B)
        ada_ref[mi] = send_buf[mi]
        second = []
        for k, p in enumerate(peers):
            pi = _lin(p)
            second.append(_remote(send_buf.at[pi], ada_ref.at[mi], ssem.at[14 + k], rsem.at[14 + k], p))
        for cp in second:
            cp.start()
        for k, p in enumerate(peers):
            pi = _lin(p)
            _remote(send_buf.at[mi], ada_ref.at[pi], ssem.at[14 + k], rsem.at[14 + k], p).wait_recv()
        for cp in first + second:
            cp.wait_send()

    vm = pl.BlockSpec(memory_space=pltpu.VMEM)
    return pl.pallas_call(
        body,
        name="ada_fwd",
        in_specs=[vm, vm, vm, vm],
        out_specs=[vm, vm, vm],
        out_shape=[
            jax.ShapeDtypeStruct((8, 8, ADA_B), F32),
            jax.ShapeDtypeStruct((64, D), F32),
            jax.ShapeDtypeStruct((8, 32, 64), F32),
        ],
        scratch_shapes=[
            pltpu.VMEM((8, 8, D), F32),
            pltpu.VMEM((8, 8, ADA_B), F32),
            pltpu.SemaphoreType.DMA((21,)),
            pltpu.SemaphoreType.DMA((21,)),
        ],
        compiler_params=pltpu.CompilerParams(vmem_limit_bytes=VMEM_LIMIT),
    )(c_pad, w_ada_b, b_cols, cw_pad)


def _ada_bwd(dada, sc_all):
    def body(d_ref, sc_ref, gw_ref, gb_ref, rbuf, ssem, rsem):
        me = _me()
        mi = _lin(me)
        peers = [_flip(me, f) for f in FLIPS]
        rbuf[mi] = d_ref[mi]
        first = []
        for k, p in enumerate(peers):
            first.append(_remote(d_ref.at[_lin(p)], rbuf.at[mi], ssem.at[k], rsem.at[k], p))
        for cp in first:
            cp.start()
        for k, p in enumerate(peers):
            _remote(d_ref.at[mi], rbuf.at[_lin(p)], ssem.at[k], rsem.at[k], p).wait_recv()
        dd = rbuf[...].reshape(64, ADA_B)
        gw_ref[...] = _dot_tn(sc_ref[...].astype(BF16), dd.astype(BF16))
        gb_ref[mi] = jnp.broadcast_to(_colsum(dd), (8, ADA_B))
        second = []
        for k, p in enumerate(peers):
            second.append(_remote(gb_ref.at[mi], gb_ref.at[mi], ssem.at[7 + k], rsem.at[7 + k], p))
        for cp in second:
            cp.start()
        for k, p in enumerate(peers):
            pi = _lin(p)
            _remote(gb_ref.at[pi], gb_ref.at[pi], ssem.at[7 + k], rsem.at[7 + k], p).wait_recv()
        for cp in first + second:
            cp.wait_send()

    vm = pl.BlockSpec(memory_space=pltpu.VMEM)
    return pl.pallas_call(
        body,
        name="ada_bwd",
        in_specs=[vm, vm],
        out_specs=[vm, vm],
        out_shape=[jax.ShapeDtypeStruct((D, ADA_B), F32), jax.ShapeDtypeStruct((8, 8, ADA_B), F32)],
        scratch_shapes=[
            pltpu.VMEM((8, 8, ADA_B), F32),
            pltpu.SemaphoreType.DMA((14,)),
            pltpu.SemaphoreType.DMA((14,)),
        ],
        compiler_params=pltpu.CompilerParams(vmem_limit_bytes=VMEM_LIMIT),
    )(dada, sc_all)


def _gather_weights(shards, outs_init):
    kinds = ("in", "out", "in", "out", "mi", "mo")
    n = len(shards)

    def slot(kind, ref, p):
        if kind == "out":
            return ref.at[2 * p[0] + p[1], pl.ds(p[2] * FO, FO), :]
        return ref.at[_lin(p)]

    def body(*refs):
        srcs = refs[:n]
        outs = refs[2 * n:3 * n]
        ssem, rsem, lsem = refs[3 * n:]
        me = _me()
        sib = _flip(me, (0, 0, 1))
        chips = [_flip(me, f) for f in ((1, 0, 0), (0, 1, 0), (1, 1, 0))]
        local, first, passed = [], [], []
        for a in range(n):
            mine = slot(kinds[a], outs[a], me)
            cp = pltpu.make_async_copy(srcs[a], mine, lsem.at[a])
            cp.start()
            local.append(cp)
            first.append(_remote(srcs[a], mine, ssem.at[7 * a], rsem.at[7 * a], sib))
            for j, p in enumerate(chips):
                first.append(_remote(srcs[a], mine, ssem.at[7 * a + 1 + j], rsem.at[7 * a + 1 + j], p))
        for cp in first:
            cp.start()
        for j, p in enumerate(chips):
            for a in range(n):
                blk = slot(kinds[a], outs[a], p)
                _remote(blk, blk, ssem.at[7 * a + 1 + j], rsem.at[7 * a + 1 + j], p).wait_recv()
                cp = _remote(blk, blk, ssem.at[7 * a + 4 + j], rsem.at[7 * a + 4 + j], sib)
                cp.start()
                passed.append(cp)
        for a in range(n):
            blk = slot(kinds[a], outs[a], sib)
            _remote(blk, blk, ssem.at[7 * a], rsem.at[7 * a], sib).wait_recv()
            for j, p in enumerate(chips):
                blk = slot(kinds[a], outs[a], _flip(p, (0, 0, 1)))
                _remote(blk, blk, ssem.at[7 * a + 4 + j], rsem.at[7 * a + 4 + j], sib).wait_recv()
        for cp in first + passed:
            cp.wait_send()
        for cp in local:
            cp.wait()

    hbm = pl.BlockSpec(memory_space=pl.ANY)
    return pl.pallas_call(
        body,
        name="gather_weights",
        in_specs=[hbm] * (2 * n),
        out_specs=[hbm] * n,
        out_shape=[jax.ShapeDtypeStruct(o.shape, o.dtype) for o in outs_init],
        input_output_aliases={n + a: a for a in range(n)},
        scratch_shapes=[
            pltpu.SemaphoreType.DMA((7 * n,)),
            pltpu.SemaphoreType.DMA((7 * n,)),
            pltpu.SemaphoreType.DMA((n,)),
        ],
    )(*shards, *outs_init)


def _scatter_grads(grads):
    kinds = ("in", "out", "in", "out", "mi", "mo")
    n = len(grads)
    shapes = {"in": (NDEV, D, FBP), "out": (NDEV, FO, D), "mi": (NDEV, D, MB), "mo": (NDEV, MO, D)}

    def block(kind, ref, p):
        if kind == "out":
            return ref.at[2 * p[0] + p[1], pl.ds(p[2] * FO, FO), :]
        return ref.at[_lin(p)]

    def body(*refs):
        srcs = refs[:n]
        outs = refs[n:2 * n]
        ssem, rsem, lsem = refs[2 * n:]
        me = _me()
        mi = _lin(me)
        peers = [_flip(me, f) for f in FLIPS]
        local, sent = [], []
        for a in range(n):
            cp = pltpu.make_async_copy(block(kinds[a], srcs[a], me), outs[a].at[mi], lsem.at[a])
            cp.start()
            local.append(cp)
        for k, p in enumerate(peers):
            for a in range(n):
                cp = _remote(block(kinds[a], srcs[a], p), outs[a].at[mi], ssem.at[7 * a + k], rsem.at[7 * a + k], p)
                cp.start()
                sent.append(cp)
        for k, p in enumerate(peers):
            for a in range(n):
                _remote(block(kinds[a], srcs[a], me), outs[a].at[_lin(p)], ssem.at[7 * a + k], rsem.at[7 * a + k], p).wait_recv()
        for cp in sent:
            cp.wait_send()
        for cp in local:
            cp.wait()

    hbm = pl.BlockSpec(memory_space=pl.ANY)
    return pl.pallas_call(
        body,
        name="scatter_grads",
        in_specs=[hbm] * n,
        out_specs=[hbm] * n,
        out_shape=[jax.ShapeDtypeStruct(shapes[k], BF16) for k in kinds],
        scratch_shapes=[
            pltpu.SemaphoreType.DMA((7 * n,)),
            pltpu.SemaphoreType.DMA((7 * n,)),
            pltpu.SemaphoreType.DMA((n,)),
        ],
    )(*grads)


SMALL_D = ("g_pre_f1", "g_post_f1", "g_pre_m", "g_post_m", "g_pre_f2", "g_post_f2")
SMALL_W = ("gmlp_norm_g", "gmlp_norm_b", "conv_b", "conv_norm_g", "conv_norm_b", "g_out_a", "g_out_b")


def _finish_small(p1, p2, p3, p4, wmv):
    nw = len(wmv) // 3

    def body(*refs):
        p_refs = refs[0:4]
        w_refs = refs[4:4 + 3 * nw]
        o = 4 + 3 * nw
        gcw_ref = refs[o]
        out_refs = refs[o + 1:o + 1 + 4 * nw]
        bufs = refs[o + 1 + 4 * nw:o + 5 + 4 * nw]
        ssem, rsem = refs[o + 5 + 4 * nw:]
        me = _me()
        mi = _lin(me)
        peers = [_flip(me, f) for f in FLIPS]
        sent = []
        for a in range(4):
            bufs[a][mi] = p_refs[a][...]
        for k, p in enumerate(peers):
            for a in range(4):
                cp = _remote(bufs[a].at[mi], bufs[a].at[mi], ssem.at[7 * a + k], rsem.at[7 * a + k], p)
                cp.start()
                sent.append(cp)
        for k, p in enumerate(peers):
            pi = _lin(p)
            for a in range(4):
                _remote(bufs[a].at[pi], bufs[a].at[pi], ssem.at[7 * a + k], rsem.at[7 * a + k], p).wait_recv()
        for cp in sent:
            cp.wait_send()
        sums = []
        for a in range(4):
            s = bufs[a][0]
            for dev in range(1, NDEV):
                s = s + bufs[a][dev]
            sums.append(s)
        s1, s2, s3, s4 = sums
        gcw_ref[...] = s2[8:40, :]
        grads = [s1[r:r + 1, :] for r in range(6)] + [s2[r:r + 1, :] for r in range(7)] + [s3, s4]
        for t in range(nw):
            w_ref, m_ref, v_ref = w_refs[3 * t:3 * t + 3]
            g_ref, d_ref, m2_ref, v2_ref = out_refs[4 * t:4 * t + 4]
            g = grads[t]
            g_ref[...] = g
            d_ref[...], m2_ref[...], v2_ref[...] = _adamw_math(w_ref[...], g, m_ref[...], v_ref[...])

    vm = pl.BlockSpec(memory_space=pltpu.VMEM)
    out_shape = [jax.ShapeDtypeStruct((32, WA), F32)]
    for t in range(nw):
        out_shape += [jax.ShapeDtypeStruct(wmv[3 * t].shape, F32)] * 4
    return pl.pallas_call(
        body,
        name="finish_small",
        in_specs=[vm] * (4 + 3 * nw),
        out_specs=[vm] * (1 + 4 * nw),
        out_shape=out_shape,
        scratch_shapes=[
            pltpu.VMEM((NDEV,) + p1.shape, F32),
            pltpu.VMEM((NDEV,) + p2.shape, F32),
            pltpu.VMEM((NDEV,) + p3.shape, F32),
            pltpu.VMEM((NDEV,) + p4.shape, F32),
            pltpu.SemaphoreType.DMA((28,)),
            pltpu.SemaphoreType.DMA((28,)),
        ],
        compiler_params=pltpu.CompilerParams(vmem_limit_bytes=VMEM_LIMIT),
    )(p1, p2, p3, p4, *wmv)


def kernel(x, c, w_ada, b_ada, g_pre_f1, g_post_f1, w_f1_in, w_f1_out, g_pre_m, g_post_m, w_mix_in, gmlp_norm_g, gmlp_norm_b, w_spatial, b_spatial, conv_w, conv_b, conv_norm_g, conv_norm_b, g_out_a, g_out_b, w_mix_out, g_pre_f2, g_post_f2, w_f2_in, w_f2_out, loss_target, m_w_ada, m_b_ada, m_g_pre_f1, m_g_post_f1, m_w_f1_in, m_w_f1_out, m_g_pre_m, m_g_post_m, m_w_mix_in, m_gmlp_norm_g, m_gmlp_norm_b, m_w_spatial, m_b_spatial, m_conv_w, m_conv_b, m_conv_norm_g, m_conv_norm_b, m_g_out_a, m_g_out_b, m_w_mix_out, m_g_pre_f2, m_g_post_f2, m_w_f2_in, m_w_f2_out, v_w_ada, v_b_ada, v_g_pre_f1, v_g_post_f1, v_w_f1_in, v_w_f1_out, v_g_pre_m, v_g_post_m, v_w_mix_in, v_gmlp_norm_g, v_gmlp_norm_b, v_w_spatial, v_b_spatial, v_conv_w, v_conv_b, v_conv_norm_g, v_conv_norm_b, v_g_out_a, v_g_out_b, v_w_mix_out, v_g_pre_f2, v_g_post_f2, v_w_f2_in, v_w_f2_out):
    given = dict(locals())
    bl, seq, _ = x.shape
    T = bl * seq
    tm = min(256, seq // 2)
    tk = min(512, T)
    me = (lax.axis_index("x"), lax.axis_index("y"), lax.axis_index("c"))
    mi = _lin(me)

    c_pad = jnp.pad(c, ((0, 8 - bl), (0, 0)))
    b_cols = lax.dynamic_slice(b_ada, (0, mi * ADA_B), (1, ADA_B))
    cw_pad = jnp.pad(conv_w[0], ((0, 1), (0, 0)))
    ada_blk, sc_all, cw_all = _ada_fwd(c_pad, w_ada[0].astype(BF16), b_cols, cw_pad)
    ada = ada_blk[:, 0:bl, :].transpose(1, 0, 2).reshape(bl, 9, D)
    pad5 = jnp.zeros((bl, 5, D), F32)
    mod1 = jnp.concatenate([ada[:, 0:3], pad5], axis=1)
    mod2 = jnp.concatenate([ada[:, 3:6], pad5], axis=1)
    mod3 = jnp.concatenate([ada[:, 6:9], pad5], axis=1)
    cw_full = cw_all.transpose(1, 0, 2).reshape(32, WA)

    def pad_in(w):
        return jnp.pad(w[0].astype(BF16), ((0, 0), (0, FBP - FB)))

    shards = [pad_in(w_f1_in), w_f1_out[0].astype(BF16), pad_in(w_f2_in), w_f2_out[0].astype(BF16),
              w_mix_in[0].astype(BF16), w_mix_out[0].astype(BF16)]
    inits = [jnp.zeros((NDEV, D, FBP), BF16), jnp.zeros((4, FBP, D), BF16),
             jnp.zeros((NDEV, D, FBP), BF16), jnp.zeros((4, FBP, D), BF16),
             jnp.zeros((NDEV, D, MB), BF16), jnp.zeros((NDEV, MO, D), BF16)]
    wi1, wo1, wi2, wo2, wmi, wmo = _gather_weights(shards, inits)
    wmo = wmo.reshape(D, D)

    zrow = jnp.zeros((1, D), F32)
    gv1 = jnp.concatenate([g_pre_f1, g_post_f1] + [zrow] * 6, axis=0)
    gvm = jnp.concatenate([g_pre_m, g_post_m] + [zrow] * 6, axis=0)
    gv2 = jnp.concatenate([g_pre_f2, g_post_f2] + [zrow] * 6, axis=0)
    v512 = jnp.concatenate([gmlp_norm_g, gmlp_norm_b, conv_b, conv_norm_g, conv_norm_b, g_out_a, g_out_b,
                            jnp.zeros((1, WA), F32)], axis=0)
    ws = w_spatial[0]
    bias_full = jnp.repeat(b_spatial[0].T, HD, axis=1)
    esel = (lax.broadcasted_iota(jnp.int32, (8, WA), 1) // HD == lax.broadcasted_iota(jnp.int32, (8, WA), 0)).astype(F32)

    x0 = x.reshape(T, D)
    x1, gu1, y1 = _ffn_fwd(x0, mod1, gv1, wi1, wo1, tm, "ffn1_fwd")
    x2, proj, ym = _mixer_fwd(x1, mod2, gvm, wmi, wmo, v512, ws, bias_full, cw_full, tm, "mixer_fwd")
    x3, gu2, y2 = _ffn_fwd(x2, mod3, gv2, wi2, wo2, tm, "ffn2_fwd")
    dx3, loss_blk = _loss_head(x3, loss_target.reshape(T, D), tm, "loss_head")
    loss = lax.psum(loss_blk[0, 0], ("x", "y", "c"))

    dx2, dg2, act2, hb2, dyb2, mg3, vg3 = _ffn_bwd(dx3, x2, y2, gu2, mod3, gv2, wi2, wo2, tm, "ffn2_bwd")
    g_wi2 = _grad_w_in(hb2, dg2, tk, "ffn2_gw_in")
    g_wo2 = _grad_w_out(act2, dyb2, tk, "ffn2_gw_out")
    dpart, dymb, ycat, mg2a, vgma, v5g, gws, gbs = _mixer_bwd_a(
        dx2, ym, proj, mod2, gvm, wmo, v512, ws, bias_full, cw_full, esel, tm, "mixer_bwd_a")
    dx1, dproj, hbm, mg2b, vgmb, dcw = _mixer_bwd_b(dx2, x1, dpart, proj, mod2, gvm, wmi, cw_full, tm, "mixer_bwd_b")
    g_wmi = _grad_w_mi(hbm, dproj, tk, "mixer_gw_in")
    g_wmo = _grad_w_mo(ycat, dymb, tk, "mixer_gw_out")
    dx0, dg1, act1, hb1, dyb1, mg1, vg1 = _ffn_bwd(dx1, x0, y1, gu1, mod1, gv1, wi1, wo1, tm, "ffn1_bwd")
    g_wi1 = _grad_w_in(hb1, dg1, tk, "ffn1_gw_in")
    g_wo1 = _grad_w_out(act1, dyb1, tk, "ffn1_gw_out")

    dada = jnp.concatenate([mg1[:, 0:3], mg2b[:, 0:2], mg2a[:, 2:3], mg3[:, 0:3]], axis=1)
    dada = dada.reshape(bl, NDEV, ADA_B).transpose(1, 0, 2)
    dada = jnp.pad(dada, ((0, 0), (0, 8 - bl), (0, 0)))
    g_wada, gb_all = _ada_bwd(dada, sc_all)
    g_bada = gb_all[:, 0, :].reshape(1, 9 * D)

    parts = _scatter_grads([g_wi1, g_wo1, g_wi2, g_wo2, g_wmi, g_wmo.reshape(NDEV, MO, D)])
    res = {}
    big = (("w_f1_in", parts[0], 256), ("w_f1_out", parts[1], FO), ("w_f2_in", parts[2], 256),
           ("w_f2_out", parts[3], FO), ("w_mix_in", parts[4], 256), ("w_mix_out", parts[5], MO))
    for nm, part, tr in big:
        w2 = given[nm][0]
        g, dlt, m2, v2 = _adamw_reduce(part, w2, given["m_" + nm][0], given["v_" + nm][0], tr, "adamw_" + nm)
        res[nm] = tuple(t[None] for t in (g, dlt, m2, v2))
    dlt, m2, v2 = _adamw_plain(w_ada[0], g_wada, m_w_ada[0], v_w_ada[0], 256, "adamw_w_ada")
    res["w_ada"] = (g_wada[None], dlt[None], m2[None], v2[None])
    dlt, m2, v2 = _adamw_plain(b_ada.reshape(72, 128), g_bada.reshape(72, 128), m_b_ada.reshape(72, 128),
                               v_b_ada.reshape(72, 128), 72, "adamw_b_ada")
    res["b_ada"] = tuple(t.reshape(1, 9 * D) for t in (g_bada, dlt, m2, v2))

    p1 = jnp.concatenate([vg1[0:2], vgmb[0:1], vgma[1:2], vg3[0:2], jnp.zeros((2, D), F32)], axis=0)
    p2 = jnp.concatenate([v5g, dcw], axis=0)
    small = SMALL_D + SMALL_W + ("w_spatial", "b_spatial")
    wmv = []
    for nm in small:
        for pre in ("", "m_", "v_"):
            wmv.append(given[pre + nm][0] if nm in ("w_spatial", "b_spatial") else given[pre + nm])
    outs = _finish_small(p1, p2, gws, gbs, wmv)
    gcw_full = outs[0]
    for t, nm in enumerate(small):
        quad = outs[1 + 4 * t:5 + 4 * t]
        res[nm] = tuple(q[None] for q in quad) if nm in ("w_spatial", "b_spatial") else tuple(quad)
    g_cw = lax.dynamic_slice(gcw_full, (0, mi * 64), (32, 64))
    dlt, m2, v2 = _adamw_plain(jnp.pad(conv_w[0], ((0, 1), (0, 0))), g_cw, jnp.pad(m_conv_w[0], ((0, 1), (0, 0))),
                               jnp.pad(v_conv_w[0], ((0, 1), (0, 0)), constant_values=1.0), 32, "adamw_conv_w")
    res["conv_w"] = tuple(t[0:CONV_K][None] for t in (g_cw, dlt, m2, v2))

    order = ["w_ada", "b_ada", "g_pre_f1", "g_post_f1", "w_f1_in", "w_f1_out", "g_pre_m", "g_post_m", "w_mix_in",
             "gmlp_norm_g", "gmlp_norm_b", "w_spatial", "b_spatial", "conv_w", "conv_b", "conv_norm_g", "conv_norm_b",
             "g_out_a", "g_out_b", "w_mix_out", "g_pre_f2", "g_post_f2", "w_f2_in", "w_f2_out"]
    out = [loss, dx0.reshape(bl, seq, D)]
    for k in range(4):
        out += [res[nm][k] for nm in order]
    return tuple(out)
```

```python
import jax
import jax.numpy as jnp
from jax import lax
from jax.experimental import pallas as pl
from jax.experimental.pallas import tpu as pltpu

F32 = jnp.float32
BF16 = jnp.bfloat16

D = 1024
DFF = 2816
NDEV = 8
FB = 2 * DFF // NDEV
FBP = 768
FO = DFF // NDEV
WA = 512
NHEAD = 8
HD = 64
CHUNK = 128
CONV_K = 31
HALO = 32
MB = 2 * (WA + WA) // NDEV
MO = D // NDEV
ADA_B = 9 * D // NDEV
EPS = 1e-6
HALF = 0.5

ADAM_LR = 0.001
ADAM_B1 = 0.9
ADAM_B2 = 0.999
ADAM_EPS = 1e-08
ADAM_WD = 0.01
ADAM_STEP = 10

VMEM_LIMIT = 56 * 1024 * 1024
MESH = pl.DeviceIdType.MESH
FLIPS = ((0, 0, 1), (1, 0, 0), (0, 1, 0), (1, 1, 0), (1, 0, 1), (0, 1, 1), (1, 1, 1))
CHIP_FLIPS = ((1, 0, 0), (0, 1, 0), (1, 1, 0))
HBM = pl.BlockSpec(memory_space=pl.ANY)
VM = pl.BlockSpec(memory_space=pltpu.VMEM)


def _dot(a, b):
    return lax.dot_general(a, b, (((1,), (0,)), ((), ())), preferred_element_type=F32)


def _dot_nt(a, b):
    return lax.dot_general(a, b, (((1,), (1,)), ((), ())), preferred_element_type=F32)


def _dot_tn(a, b):
    return lax.dot_general(a, b, (((0,), (0,)), ((), ())), preferred_element_type=F32)


def _rowmean(v):
    return jnp.mean(v, axis=-1, keepdims=True)


def _colsum(v):
    return jnp.sum(v, axis=0, keepdims=True)


def _sigmoid(v):
    return 1.0 / (1.0 + jnp.exp(-v))


def _const_spec(shape):
    nd = len(shape)
    return pl.BlockSpec(shape, lambda *_: (0,) * nd, pipeline_mode=pl.Buffered(1))


def _me():
    return lax.axis_index("x"), lax.axis_index("y"), lax.axis_index("c")


def _flip(me, f):
    return tuple(1 - v if b else v for v, b in zip(me, f))


def _lin(p):
    return 4 * p[0] + 2 * p[1] + p[2]


def _remote(src, dst, send_sem, recv_sem, dev):
    return pltpu.make_async_remote_copy(src_ref=src, dst_ref=dst, send_sem=send_sem, recv_sem=recv_sem,
                                        device_id=dev, device_id_type=MESH)


def _blk(kind, ref, p):
    if kind == "out":
        return ref.at[2 * p[0] + p[1], pl.ds(p[2] * FO, FO), :]
    return ref.at[_lin(p)]


class _Gather:
    def __init__(self, shards, kinds, zpad):
        self.kinds = kinds
        self.n = len(shards)
        self.ins = list(shards) + [zpad]
        self.out_shape = [jax.ShapeDtypeStruct((4, FBP, D) if k == "out" else (NDEV,) + s.shape, BF16)
                          for s, k in zip(shards, kinds)]
        self.n_out = sum(k == "out" for k in kinds)
        self.sems = [pltpu.SemaphoreType.DMA((7 * self.n,)), pltpu.SemaphoreType.DMA((7 * self.n,)),
                     pltpu.SemaphoreType.DMA((self.n + 4 * max(self.n_out, 1),))]

    def _first(self, ins, outs, sems):
        ssem, rsem, lsem = sems
        me = _me()
        sib = _flip(me, (0, 0, 1))
        cps, loc = [], []
        nz = 0
        for a in range(self.n):
            mine = _blk(self.kinds[a], outs[a], me)
            loc.append(pltpu.make_async_copy(ins[a], mine, lsem.at[a]))
            if self.kinds[a] == "out":
                for q in range(4):
                    loc.append(pltpu.make_async_copy(ins[self.n], outs[a].at[q, pl.ds(FB, FBP - FB), :],
                                                     lsem.at[self.n + 4 * nz + q]))
                nz += 1
            cps.append(_remote(ins[a], mine, ssem.at[7 * a], rsem.at[7 * a], sib))
            for j, f in enumerate(CHIP_FLIPS):
                cps.append(_remote(ins[a], mine, ssem.at[7 * a + 1 + j], rsem.at[7 * a + 1 + j], _flip(me, f)))
        return cps, loc

    def _passed(self, outs, sems):
        ssem, rsem, _ = sems
        me = _me()
        sib = _flip(me, (0, 0, 1))
        cps = []
        for j, f in enumerate(CHIP_FLIPS):
            for a in range(self.n):
                blk = _blk(self.kinds[a], outs[a], _flip(me, f))
                cps.append(_remote(blk, blk, ssem.at[7 * a + 4 + j], rsem.at[7 * a + 4 + j], sib))
        return cps

    def start(self, ins, outs, sems):
        cps, loc = self._first(ins, outs, sems)
        for cp in loc + cps:
            cp.start()

    def mid(self, ins, outs, sems):
        ssem, rsem, _ = sems
        me = _me()
        passed = self._passed(outs, sems)
        t = 0
        for j, f in enumerate(CHIP_FLIPS):
            for a in range(self.n):
                blk = _blk(self.kinds[a], outs[a], _flip(me, f))
                _remote(blk, blk, ssem.at[7 * a + 1 + j], rsem.at[7 * a + 1 + j], _flip(me, f)).wait_recv()
                passed[t].start()
                t += 1

    def end(self, ins, outs, sems):
        ssem, rsem, _ = sems
        me = _me()
        sib = _flip(me, (0, 0, 1))
        for a in range(self.n):
            blk = _blk(self.kinds[a], outs[a], sib)
            _remote(blk, blk, ssem.at[7 * a], rsem.at[7 * a], sib).wait_recv()
            for j, f in enumerate(CHIP_FLIPS):
                blk = _blk(self.kinds[a], outs[a], _flip(_flip(me, f), (0, 0, 1)))
                _remote(blk, blk, ssem.at[7 * a + 4 + j], rsem.at[7 * a + 4 + j], sib).wait_recv()
        cps, loc = self._first(ins, outs, sems)
        for cp in cps + self._passed(outs, sems):
            cp.wait_send()
        for cp in loc:
            cp.wait()


class _Scatter:
    def __init__(self, grads, kinds):
        self.kinds = kinds
        self.n = len(grads)
        self.ins = list(grads)
        self.out_shape = [jax.ShapeDtypeStruct((NDEV, FO, D) if k == "out" else g.shape, BF16)
                          for g, k in zip(grads, kinds)]
        self.sems = [pltpu.SemaphoreType.DMA((7 * self.n,)), pltpu.SemaphoreType.DMA((7 * self.n,)),
                     pltpu.SemaphoreType.DMA((self.n,))]

    def _copies(self, ins, outs, sems):
        ssem, rsem, lsem = sems
        me = _me()
        mi = _lin(me)
        loc = [pltpu.make_async_copy(_blk(self.kinds[a], ins[a], me), outs[a].at[mi], lsem.at[a]) for a in range(self.n)]
        cps = []
        for k, f in enumerate(FLIPS):
            p = _flip(me, f)
            for a in range(self.n):
                cps.append(_remote(_blk(self.kinds[a], ins[a], p), outs[a].at[mi], ssem.at[7 * a + k], rsem.at[7 * a + k], p))
        return cps, loc

    def start(self, ins, outs, sems):
        cps, loc = self._copies(ins, outs, sems)
        for cp in loc + cps:
            cp.start()

    mid = None

    def end(self, ins, outs, sems):
        ssem, rsem, _ = sems
        me = _me()
        for k, f in enumerate(FLIPS):
            p = _flip(me, f)
            for a in range(self.n):
                _remote(_blk(self.kinds[a], ins[a], me), outs[a].at[_lin(p)], ssem.at[7 * a + k], rsem.at[7 * a + k], p).wait_recv()
        cps, loc = self._copies(ins, outs, sems)
        for cp in cps:
            cp.wait_send()
        for cp in loc:
            cp.wait()


def _call(core, *, name, grid, in_specs, out_specs, out_shape, args, scratch=(), jobs=()):
    n_in, n_out, n_sc = len(in_specs), len(out_specs), len(scratch)
    steps = 1
    for g in grid:
        steps *= g

    def body(*refs):
        pos = [0]

        def take(k):
            r = refs[pos[0]:pos[0] + k]
            pos[0] += k
            return r

        ins = take(n_in)
        j_ins = [take(len(j.ins)) for j in jobs]
        outs = take(n_out)
        j_outs = [take(len(j.out_shape)) for j in jobs]
        scs = take(n_sc)
        j_sems = [take(len(j.sems)) for j in jobs]
        if len(grid) == 2:
            step = pl.program_id(0) * grid[1] + pl.program_id(1)
        elif len(grid) == 1:
            step = pl.program_id(0)
        else:
            step = 0
        for j, ji, jo, js in zip(jobs, j_ins, j_outs, j_sems):
            if grid:
                pl.when(step == 0)(lambda j=j, ji=ji, jo=jo, js=js: j.start(ji, jo, js))
            else:
                j.start(ji, jo, js)
        for j, ji, jo, js in zip(jobs, j_ins, j_outs, j_sems):
            if j.mid is not None:
                if grid:
                    pl.when(step == (3 * steps) // 4)(lambda j=j, ji=ji, jo=jo, js=js: j.mid(ji, jo, js))
                else:
                    j.mid(ji, jo, js)
        if core is not None:
            core(ins, outs, scs)
        for j, ji, jo, js in zip(jobs, j_ins, j_outs, j_sems):
            if grid:
                pl.when(step == steps - 1)(lambda j=j, ji=ji, jo=jo, js=js: j.end(ji, jo, js))
            else:
                j.end(ji, jo, js)

    all_in = list(in_specs)
    all_args = list(args)
    all_out = list(out_specs)
    all_shape = list(out_shape)
    all_sc = list(scratch)
    for j in jobs:
        all_in += [HBM] * len(j.ins)
        all_args += j.ins
    for j in jobs:
        all_out += [HBM] * len(j.out_shape)
        all_shape += j.out_shape
        all_sc += j.sems
    params = dict(vmem_limit_bytes=VMEM_LIMIT)
    if grid:
        params["dimension_semantics"] = ("arbitrary",) * len(grid)
    res = pl.pallas_call(
        body, name=name, grid=grid, in_specs=all_in, out_specs=all_out, out_shape=all_shape,
        scratch_shapes=all_sc, compiler_params=pltpu.CompilerParams(**params),
    )(*all_args)
    core_res = list(res[:n_out])
    job_res = []
    pos = n_out
    for j in jobs:
        job_res.append(list(res[pos:pos + len(j.out_shape)]))
        pos += len(j.out_shape)
    return core_res, job_res


def _ffn_fwd(x, mod, gvec, w_in, w_out, tm, name, jobs=()):
    T = x.shape[0]
    nt = T // tm
    tps = nt // mod.shape[0]

    def core(ins, outs, _):
        x_ref, mod_ref, g_ref, win_ref, wout_ref = ins
        xo_ref, gu_ref, y_ref = outs
        xv = x_ref[...]
        sh, sc, gt = mod_ref[0:1, :], mod_ref[1:2, :], mod_ref[2:3, :]
        r = lax.rsqrt(_rowmean(xv * xv) + EPS)
        h = (xv * r * g_ref[0:1, :]) * (1.0 + sc) + sh
        hb = h.astype(BF16)
        y = jnp.zeros((tm, D), F32)
        for cidx in range(4):
            gate = _dot_nt(hb, win_ref[cidx])
            up = _dot_nt(hb, win_ref[4 + cidx])
            gu_ref[cidx] = gate.astype(BF16)
            gu_ref[4 + cidx] = up.astype(BF16)
            act = gate * _sigmoid(gate) * up
            y = y + _dot(act.astype(BF16), wout_ref[cidx])
        y_ref[...] = y
        ry = lax.rsqrt(_rowmean(y * y) + EPS)
        xo_ref[...] = xv + (HALF * gt) * (y * ry * g_ref[1:2, :])

    tile = pl.BlockSpec((tm, D), lambda i: (i, 0))
    return _call(
        core, name=name, grid=(nt,), jobs=jobs,
        in_specs=[tile, pl.BlockSpec((None, 8, D), lambda i: (i // tps, 0, 0)), _const_spec((8, D)),
                  _const_spec((8, FBP, D)), _const_spec((4, FBP, D))],
        out_specs=[tile, pl.BlockSpec((8, tm, FBP), lambda i: (0, i, 0)), tile],
        out_shape=[jax.ShapeDtypeStruct((T, D), F32), jax.ShapeDtypeStruct((8, T, FBP), BF16),
                   jax.ShapeDtypeStruct((T, D), F32)],
        args=[x, mod, gvec, w_in, w_out])


def _ffn_bwd(dxo, x, y, gu, mod, gvec, w_in, w_out, tm, name, jobs=()):
    T = x.shape[0]
    nt = T // tm
    nb = mod.shape[0]
    tps = nt // nb

    def core(ins, outs, _):
        dxo_ref, x_ref, y_ref, gu_ref, mod_ref, g_ref, win_ref, wout_ref = ins
        dx_ref, dg_ref, act_ref, hb_ref, dyb_ref, mg_ref, vg_ref = outs
        i = pl.program_id(0)
        xv = x_ref[...]
        dxo_v = dxo_ref[...]
        yv = y_ref[...]
        sh, sc, gt = mod_ref[0:1, :], mod_ref[1:2, :], mod_ref[2:3, :]
        gpre, gpost = g_ref[0:1, :], g_ref[1:2, :]
        r = lax.rsqrt(_rowmean(xv * xv) + EPS)
        xh = xv * r
        n = xh * gpre
        hb = (n * (1.0 + sc) + sh).astype(BF16)
        hb_ref[...] = hb
        ry = lax.rsqrt(_rowmean(yv * yv) + EPS)
        yh = yv * ry
        d_gt = _colsum(HALF * dxo_v * (yh * gpost))
        dp = (HALF * gt) * dxo_v
        d_gpost = _colsum(dp * yh)
        dyh = dp * gpost
        dy = ry * (dyh - yh * _rowmean(dyh * yh))
        dyb = dy.astype(BF16)
        dyb_ref[...] = dyb
        dh = jnp.zeros((tm, D), F32)
        for cidx in range(4):
            gate = gu_ref[cidx].astype(F32)
            up = gu_ref[4 + cidx].astype(F32)
            sig = _sigmoid(gate)
            s = gate * sig
            act_ref[cidx] = (s * up).astype(BF16)
            d_act = _dot_nt(dyb, wout_ref[cidx])
            d_up = (d_act * s).astype(BF16)
            d_gate = (d_act * up * (sig * (1.0 + gate * (1.0 - sig)))).astype(BF16)
            dg_ref[cidx] = d_gate
            dg_ref[4 + cidx] = d_up
            dh = dh + _dot(d_gate, win_ref[cidx]) + _dot(d_up, win_ref[4 + cidx])
        d_sc = _colsum(dh * n)
        d_sh = _colsum(dh)
        dn = dh * (1.0 + sc)
        d_gpre = _colsum(dn * xh)
        dxh = dn * gpre
        dx_ref[...] = dxo_v + r * (dxh - xh * _rowmean(dxh * xh))

        @pl.when(i % tps == 0)
        def _():
            mg_ref[...] = jnp.zeros((8, D), F32)

        @pl.when(i == 0)
        def _():
            vg_ref[...] = jnp.zeros((8, D), F32)

        mg_ref[0:1, :] += d_sh
        mg_ref[1:2, :] += d_sc
        mg_ref[2:3, :] += d_gt
        vg_ref[0:1, :] += d_gpre
        vg_ref[1:2, :] += d_gpost

    tile = pl.BlockSpec((tm, D), lambda i: (i, 0))
    return _call(
        core, name=name, grid=(nt,), jobs=jobs,
        in_specs=[tile, tile, tile, pl.BlockSpec((8, tm, FBP), lambda i: (0, i, 0)),
                  pl.BlockSpec((None, 8, D), lambda i: (i // tps, 0, 0)), _const_spec((8, D)),
                  _const_spec((8, FBP, D)), _const_spec((4, FBP, D))],
        out_specs=[tile, pl.BlockSpec((8, tm, FBP), lambda i: (0, i, 0)),
                   pl.BlockSpec((4, tm, FBP), lambda i: (0, i, 0)), tile, tile,
                   pl.BlockSpec((None, 8, D), lambda i: (i // tps, 0, 0)), pl.BlockSpec((8, D), lambda i: (0, 0))],
        out_shape=[jax.ShapeDtypeStruct((T, D), F32), jax.ShapeDtypeStruct((8, T, FBP), BF16),
                   jax.ShapeDtypeStruct((4, T, FBP), BF16), jax.ShapeDtypeStruct((T, D), BF16),
                   jax.ShapeDtypeStruct((T, D), BF16), jax.ShapeDtypeStruct((nb, 8, D), F32),
                   jax.ShapeDtypeStruct((8, D), F32)],
        args=[dxo, x, y, gu, mod, gvec, w_in, w_out])


def _masked_spatial(ws_ref):
    row = lax.broadcasted_iota(jnp.int32, (CHUNK, CHUNK), 0)
    col = lax.broadcasted_iota(jnp.int32, (CHUNK, CHUNK), 1)
    keep = col <= row
    return [jnp.where(keep, ws_ref[hd], 0.0).astype(BF16) for hd in range(NHEAD)]


def _spatial_gate(wm, vb_chunk, lane_head):
    z = jnp.zeros((CHUNK, WA), F32)
    for hd in range(NHEAD):
        z = jnp.where(lane_head == hd, _dot(wm[hd], vb_chunk), z)
    return z


def _layer_norm_stats(v):
    mu = _rowmean(v)
    vc = v - mu
    rstd = lax.rsqrt(_rowmean(vc * vc) + EPS)
    return vc * rstd, rstd


def _causal_conv(ext_ref, cw_ref, bias, tm, rows=64):
    off = HALO - (CONV_K - 1)
    out = []
    for rc in range(tm // rows):
        acc = jnp.broadcast_to(bias, (rows, WA))
        for k in range(CONV_K):
            acc = acc + cw_ref[k:k + 1, :] * ext_ref[rc * rows + off + k:rc * rows + off + k + rows, :]
        out.append(acc)
    return jnp.concatenate(out, axis=0)


def _mixer_fwd(x, mod, gvec, w_mi, w_mo, v512, ws, bias_full, cw, tm, name, jobs=()):
    T = x.shape[0]
    nt = T // tm
    tps = nt // mod.shape[0]

    def core(ins, outs, scs):
        x_ref, mod_ref, g_ref, wmi_ref, wmo_ref, v_ref, ws_ref, bias_ref, cw_ref = ins
        xo_ref, proj_ref, ym_ref = outs
        (glu_ext,) = scs
        i = pl.program_id(0)
        xv = x_ref[...]
        sh, sc, gt = mod_ref[0:1, :], mod_ref[1:2, :], mod_ref[2:3, :]
        r = lax.rsqrt(_rowmean(xv * xv) + EPS)
        hb = ((xv * r * g_ref[0:1, :]) * (1.0 + sc) + sh).astype(BF16)
        for j in range(NDEV):
            proj_ref[:, j * MB:(j + 1) * MB] = _dot(hb, wmi_ref[j])
        u = proj_ref[:, 0:WA]
        v0 = proj_ref[:, WA:2 * WA]
        a = proj_ref[:, 2 * WA:3 * WA]
        g = proj_ref[:, 3 * WA:4 * WA]
        vh, _ = _layer_norm_stats(v0)
        vb = (vh * v_ref[0:1, :] + v_ref[1:2, :]).astype(BF16)
        wm = _masked_spatial(ws_ref)
        lane_head = lax.broadcasted_iota(jnp.int32, (CHUNK, WA), 1) >> 6
        ya = []
        for q in range(tm // CHUNK):
            z = _spatial_gate(wm, vb[q * CHUNK:(q + 1) * CHUNK, :], lane_head) + bias_ref[...]
            ya.append(u[q * CHUNK:(q + 1) * CHUNK, :] * z)
        ya = jnp.concatenate(ya, axis=0)
        glu = a * _sigmoid(g)

        @pl.when(i % tps == 0)
        def _():
            glu_ext[0:HALO, :] = jnp.zeros((HALO, WA), F32)

        glu_ext[HALO:HALO + tm, :] = glu
        conv = _causal_conv(glu_ext, cw_ref, v_ref[2:3, :], tm)
        glu_ext[0:HALO, :] = glu_ext[tm:tm + HALO, :]
        ch, _ = _layer_norm_stats(conv)
        cn = ch * v_ref[3:4, :] + v_ref[4:5, :]
        yb = cn * _sigmoid(cn)
        pa = ya * lax.rsqrt(_rowmean(ya * ya) + EPS) * v_ref[5:6, :]
        pb = yb * lax.rsqrt(_rowmean(yb * yb) + EPS) * v_ref[6:7, :]
        ycat = jnp.concatenate([pa, pb], axis=1).astype(BF16)
        ym = _dot(ycat, wmo_ref[...])
        ym_ref[...] = ym
        rm = lax.rsqrt(_rowmean(ym * ym) + EPS)
        xo_ref[...] = xv + gt * (ym * rm * g_ref[1:2, :])

    tile = pl.BlockSpec((tm, D), lambda i: (i, 0))
    return _call(
        core, name=name, grid=(nt,), jobs=jobs,
        in_specs=[tile, pl.BlockSpec((None, 8, D), lambda i: (i // tps, 0, 0)), _const_spec((8, D)),
                  _const_spec((NDEV, D, MB)), _const_spec((D, D)), _const_spec((8, WA)),
                  _const_spec((NHEAD, CHUNK, CHUNK)), _const_spec((CHUNK, WA)), _const_spec((32, WA))],
        out_specs=[tile, pl.BlockSpec((tm, 4 * WA), lambda i: (i, 0)), tile],
        out_shape=[jax.ShapeDtypeStruct((T, D), F32), jax.ShapeDtypeStruct((T, 4 * WA), F32),
                   jax.ShapeDtypeStruct((T, D), F32)],
        scratch=[pltpu.VMEM((tm + HALO, WA), F32)],
        args=[x, mod, gvec, w_mi, w_mo, v512, ws, bias_full, cw])


def _mixer_bwd_a(dxo, ym, proj, mod, gvec, w_mo, v512, ws, bias_full, cw, esel, tm, name, jobs=()):
    T = dxo.shape[0]
    nt = T // tm
    nb = mod.shape[0]
    tps = nt // nb
    hpt = tm // HALO

    def core(ins, outs, scs):
        dxo_ref, ym_ref, proj_ref, halo_ref, mod_ref, g_ref, wmo_ref, v_ref, ws_ref, bias_ref, cw_ref, e_ref = ins
        dpart_ref, dymb_ref, ycat_ref, mg_ref, vg_ref, v5g_ref, gws_ref, gbs_ref = outs
        glu_ext, dbs_acc = scs
        i = pl.program_id(0)
        dxo_v = dxo_ref[...]
        ymv = ym_ref[...]
        gt = mod_ref[2:3, :]
        gpost = g_ref[1:2, :]
        rm = lax.rsqrt(_rowmean(ymv * ymv) + EPS)
        ymh = ymv * rm
        d_gt = _colsum(dxo_v * (ymh * gpost))
        dpm = gt * dxo_v
        d_gpost = _colsum(dpm * ymh)
        dymh = dpm * gpost
        dym = (rm * (dymh - ymh * _rowmean(dymh * ymh))).astype(BF16)
        dymb_ref[...] = dym
        dycat = _dot_nt(dym, wmo_ref[...])
        u = proj_ref[:, 0:WA]
        v0 = proj_ref[:, WA:2 * WA]
        a = proj_ref[:, 2 * WA:3 * WA]
        g = proj_ref[:, 3 * WA:4 * WA]
        vh, rv = _layer_norm_stats(v0)
        vb = (vh * v_ref[0:1, :] + v_ref[1:2, :]).astype(BF16)
        wm = _masked_spatial(ws_ref)
        lane_head = lax.broadcasted_iota(jnp.int32, (CHUNK, WA), 1) >> 6
        zs = []
        for q in range(tm // CHUNK):
            zs.append(_spatial_gate(wm, vb[q * CHUNK:(q + 1) * CHUNK, :], lane_head) + bias_ref[...])
        z = jnp.concatenate(zs, axis=0)
        ya = u * z
        ra = lax.rsqrt(_rowmean(ya * ya) + EPS)
        yah = ya * ra
        first = i % tps == 0
        ah = halo_ref[:, 0:WA]
        gh = halo_ref[:, WA:2 * WA]
        glu_ext[0:HALO, :] = jnp.where(first, 0.0, ah * _sigmoid(gh))
        glu_ext[HALO:HALO + tm, :] = a * _sigmoid(g)
        conv = _causal_conv(glu_ext, cw_ref, v_ref[2:3, :], tm)
        ch, rc = _layer_norm_stats(conv)
        cn = ch * v_ref[3:4, :] + v_ref[4:5, :]
        sg = _sigmoid(cn)
        yb = cn * sg
        rb = lax.rsqrt(_rowmean(yb * yb) + EPS)
        ybh = yb * rb
        ycat_ref[...] = jnp.concatenate([yah * v_ref[5:6, :], ybh * v_ref[6:7, :]], axis=1).astype(BF16)
        dpa = dycat[:, 0:WA]
        dpb = dycat[:, WA:2 * WA]
        d_goa = _colsum(dpa * yah)
        d_gob = _colsum(dpb * ybh)
        dyah = dpa * v_ref[5:6, :]
        dybh = dpb * v_ref[6:7, :]
        dya = ra * (dyah - yah * _rowmean(dyah * yah))
        dyb = rb * (dybh - ybh * _rowmean(dybh * ybh))
        dpart_ref[:, 0:WA] = dya * z
        dz = dya * u

        @pl.when(i == 0)
        def _():
            gws_ref[...] = jnp.zeros((NHEAD, CHUNK, CHUNK), F32)
            dbs_acc[...] = jnp.zeros((CHUNK, WA), F32)
            vg_ref[...] = jnp.zeros((8, D), F32)
            v5g_ref[...] = jnp.zeros((8, WA), F32)

        dvs = []
        for q in range(tm // CHUNK):
            dz_q = dz[q * CHUNK:(q + 1) * CHUNK, :]
            vb_q = vb[q * CHUNK:(q + 1) * CHUNK, :]
            dbs_acc[...] += dz_q
            dzb = dz_q.astype(BF16)
            dv_q = jnp.zeros((CHUNK, WA), F32)
            for hd in range(NHEAD):
                dv_q = jnp.where(lane_head == hd, _dot_tn(wm[hd], dzb), dv_q)
                dz_hd = jnp.where(lane_head == hd, dz_q, 0.0).astype(BF16)
                gws_ref[hd] += _dot_nt(dz_hd, vb_q)
            dvs.append(dv_q)
        dv = jnp.concatenate(dvs, axis=0)
        d_gng = _colsum(dv * vh)
        d_gnb = _colsum(dv)
        dvh = dv * v_ref[0:1, :]
        dpart_ref[:, WA:2 * WA] = rv * (dvh - _rowmean(dvh) - vh * _rowmean(dvh * vh))
        dcn = dyb * (sg * (1.0 + cn * (1.0 - sg)))
        d_cng = _colsum(dcn * ch)
        d_cnb = _colsum(dcn)
        dch = dcn * v_ref[3:4, :]
        dconv = rc * (dch - _rowmean(dch) - ch * _rowmean(dch * ch))
        dpart_ref[:, 2 * WA:3 * WA] = dconv
        dpart_ref[:, 3 * WA:4 * WA] = jnp.zeros((tm, WA), F32)
        d_cb = _colsum(dconv)

        @pl.when(i % tps == 0)
        def _():
            mg_ref[...] = jnp.zeros((8, D), F32)

        mg_ref[2:3, :] += d_gt
        vg_ref[1:2, :] += d_gpost
        v5g_ref[0:1, :] += d_gng
        v5g_ref[1:2, :] += d_gnb
        v5g_ref[2:3, :] += d_cb
        v5g_ref[3:4, :] += d_cng
        v5g_ref[4:5, :] += d_cnb
        v5g_ref[5:6, :] += d_goa
        v5g_ref[6:7, :] += d_gob

        @pl.when(i == nt - 1)
        def _():
            row = lax.broadcasted_iota(jnp.int32, (CHUNK, CHUNK), 0)
            col = lax.broadcasted_iota(jnp.int32, (CHUNK, CHUNK), 1)
            for hd in range(NHEAD):
                gws_ref[hd] = jnp.where(col <= row, gws_ref[hd], 0.0)
            gbs_ref[...] = lax.dot_general(e_ref[...], dbs_acc[...], (((1,), (1,)), ((), ())),
                                           precision=lax.Precision.HIGHEST, preferred_element_type=F32)

    tile = pl.BlockSpec((tm, D), lambda i: (i, 0))
    ptile = pl.BlockSpec((tm, 4 * WA), lambda i: (i, 0))
    return _call(
        core, name=name, grid=(nt,), jobs=jobs,
        in_specs=[tile, tile, ptile,
                  pl.BlockSpec((HALO, 2 * WA), lambda i: (jnp.maximum(i * hpt - 1, 0), 1)),
                  pl.BlockSpec((None, 8, D), lambda i: (i // tps, 0, 0)), _const_spec((8, D)), _const_spec((D, D)),
                  _const_spec((8, WA)), _const_spec((NHEAD, CHUNK, CHUNK)), _const_spec((CHUNK, WA)),
                  _const_spec((32, WA)), _const_spec((8, WA))],
        out_specs=[ptile, tile, tile, pl.BlockSpec((None, 8, D), lambda i: (i // tps, 0, 0)),
                   pl.BlockSpec((8, D), lambda i: (0, 0)), pl.BlockSpec((8, WA), lambda i: (0, 0)),
                   pl.BlockSpec((NHEAD, CHUNK, CHUNK), lambda i: (0, 0, 0)), pl.BlockSpec((8, CHUNK), lambda i: (0, 0))],
        out_shape=[jax.ShapeDtypeStruct((T, 4 * WA), F32), jax.ShapeDtypeStruct((T, D), BF16),
                   jax.ShapeDtypeStruct((T, D), BF16), jax.ShapeDtypeStruct((nb, 8, D), F32),
                   jax.ShapeDtypeStruct((8, D), F32), jax.ShapeDtypeStruct((8, WA), F32),
                   jax.ShapeDtypeStruct((NHEAD, CHUNK, CHUNK), F32), jax.ShapeDtypeStruct((8, CHUNK), F32)],
        scratch=[pltpu.VMEM((tm + HALO, WA), F32), pltpu.VMEM((CHUNK, WA), F32)],
        args=[dxo, ym, proj, proj, mod, gvec, w_mo, v512, ws, bias_full, cw, esel])


def _mixer_bwd_b(dxo, x, dpart, proj, mod, gvec, w_mi, cw, tm, name, jobs=()):
    T = x.shape[0]
    nt = T // tm
    nb = mod.shape[0]
    tps = nt // nb
    hpt = tm // HALO
    nh = T // HALO
    off = HALO - (CONV_K - 1)
    rows = 64

    def core(ins, outs, scs):
        dxo_ref, x_ref, dpart_ref, dnext_ref, ag_ref, halo_ref, mod_ref, g_ref, wmi_ref, cw_ref = ins
        dx_ref, dproj_ref, hb_ref, mg_ref, vg_ref, dcw_ref = outs
        glu_ext, dconv_ext = scs
        i = pl.program_id(0)
        first = i % tps == 0
        last = i % tps == tps - 1
        a = ag_ref[:, 0:WA]
        g = ag_ref[:, WA:2 * WA]
        sgg = _sigmoid(g)
        glu_ext[0:HALO, :] = jnp.where(first, 0.0, halo_ref[:, 0:WA] * _sigmoid(halo_ref[:, WA:2 * WA]))
        glu_ext[HALO:HALO + tm, :] = a * sgg
        dconv_ext[0:tm, :] = dpart_ref[:, 2 * WA:3 * WA]
        dconv_ext[tm:tm + HALO, :] = jnp.where(last, 0.0, dnext_ref[...])

        @pl.when(i == 0)
        def _():
            dcw_ref[...] = jnp.zeros((32, WA), F32)
            vg_ref[...] = jnp.zeros((8, D), F32)

        dglu = []
        for rc in range(tm // rows):
            dc = dconv_ext[rc * rows:(rc + 1) * rows, :]
            acc = jnp.zeros((rows, WA), F32)
            for k in range(CONV_K):
                dcw_ref[k:k + 1, :] += _colsum(dc * glu_ext[rc * rows + off + k:rc * rows + off + k + rows, :])
                acc = acc + cw_ref[k:k + 1, :] * dconv_ext[rc * rows + (CONV_K - 1) - k:
                                                            rc * rows + (CONV_K - 1) - k + rows, :]
            dglu.append(acc)
        dglu = jnp.concatenate(dglu, axis=0)
        da = dglu * sgg
        dgg = dglu * a * (sgg * (1.0 - sgg))
        dproj_ref[:, 0:2 * WA] = dpart_ref[:, 0:2 * WA].astype(BF16)
        dproj_ref[:, 2 * WA:3 * WA] = da.astype(BF16)
        dproj_ref[:, 3 * WA:4 * WA] = dgg.astype(BF16)
        dh = jnp.zeros((tm, D), F32)
        for j in range(NDEV):
            dh = dh + _dot_nt(dproj_ref[:, j * MB:(j + 1) * MB], wmi_ref[j])
        xv = x_ref[...]
        sc, sh = mod_ref[1:2, :], mod_ref[0:1, :]
        gpre = g_ref[0:1, :]
        r = lax.rsqrt(_rowmean(xv * xv) + EPS)
        xh = xv * r
        n = xh * gpre
        hb_ref[...] = (n * (1.0 + sc) + sh).astype(BF16)
        d_sc = _colsum(dh * n)
        d_sh = _colsum(dh)
        dn = dh * (1.0 + sc)
        d_gpre = _colsum(dn * xh)
        dxh = dn * gpre
        dx_ref[...] = dxo_ref[...] + r * (dxh - xh * _rowmean(dxh * xh))

        @pl.when(first)
        def _():
            mg_ref[...] = jnp.zeros((8, D), F32)

        mg_ref[0:1, :] += d_sh
        mg_ref[1:2, :] += d_sc
        vg_ref[0:1, :] += d_gpre

    tile = pl.BlockSpec((tm, D), lambda i: (i, 0))
    return _call(
        core, name=name, grid=(nt,), jobs=jobs,
        in_specs=[tile, tile, pl.BlockSpec((tm, 4 * WA), lambda i: (i, 0)),
                  pl.BlockSpec((HALO, WA), lambda i: (jnp.minimum((i + 1) * hpt, nh - 1), 2)),
                  pl.BlockSpec((tm, 2 * WA), lambda i: (i, 1)),
                  pl.BlockSpec((HALO, 2 * WA), lambda i: (jnp.maximum(i * hpt - 1, 0), 1)),
                  pl.BlockSpec((None, 8, D), lambda i: (i // tps, 0, 0)), _const_spec((8, D)),
                  _const_spec((NDEV, D, MB)), _const_spec((32, WA))],
        out_specs=[tile, pl.BlockSpec((tm, 4 * WA), lambda i: (i, 0)), tile,
                   pl.BlockSpec((None, 8, D), lambda i: (i // tps, 0, 0)), pl.BlockSpec((8, D), lambda i: (0, 0)),
                   pl.BlockSpec((32, WA), lambda i: (0, 0))],
        out_shape=[jax.ShapeDtypeStruct((T, D), F32), jax.ShapeDtypeStruct((T, 4 * WA), BF16),
                   jax.ShapeDtypeStruct((T, D), BF16), jax.ShapeDtypeStruct((nb, 8, D), F32),
                   jax.ShapeDtypeStruct((8, D), F32), jax.ShapeDtypeStruct((32, WA), F32)],
        scratch=[pltpu.VMEM((tm + HALO, WA), F32), pltpu.VMEM((tm + HALO, WA), F32)],
        args=[dxo, x, dpart, dpart, proj, proj, mod, gvec, w_mi, cw])


def _loss_head(y, target, tm, name):
    T = y.shape[0]

    def core(ins, outs, _):
        y_ref, t_ref = ins
        dy_ref, loss_ref = outs

        @pl.when(pl.program_id(0) == 0)
        def _():
            loss_ref[...] = jnp.zeros((8, D), F32)

        err = y_ref[...] - t_ref[...]
        dy_ref[...] = err * (1.0 / D)
        part = jnp.sum(_rowmean(err * err), axis=0, keepdims=True)
        loss_ref[...] += HALF * part

    tile = pl.BlockSpec((tm, D), lambda i: (i, 0))
    return _call(
        core, name=name, grid=(T // tm,),
        in_specs=[tile, tile], out_specs=[tile, pl.BlockSpec((8, D), lambda i: (0, 0))],
        out_shape=[jax.ShapeDtypeStruct((T, D), F32), jax.ShapeDtypeStruct((8, D), F32)],
        args=[y, target])[0]


def _matmul_tn(a, b, a_spec, b_spec, o_spec, out_shape, acc_shape, grid, name, jobs=()):
    nk = grid[1]

    def core(ins, outs, scs):
        a_ref, b_ref = ins
        (o_ref,) = outs
        (acc,) = scs
        k = pl.program_id(1)

        @pl.when(k == 0)
        def _():
            acc[...] = jnp.zeros(acc_shape, F32)

        acc[...] += _dot_tn(a_ref[...], b_ref[...])

        @pl.when(k == nk - 1)
        def _():
            o_ref[...] = acc[...].astype(BF16)

    return _call(
        core, name=name, grid=grid, jobs=jobs, in_specs=[a_spec, b_spec], out_specs=[o_spec],
        out_shape=[jax.ShapeDtypeStruct(out_shape, BF16)], scratch=[pltpu.VMEM(acc_shape, F32)], args=[a, b])


def _grad_rows(a3, b, tk, name, jobs=()):
    n, T, _ = a3.shape
    return _matmul_tn(
        a3, b,
        pl.BlockSpec((None, tk, FBP), lambda j, k: (j, k, 0)),
        pl.BlockSpec((tk, D), lambda j, k: (k, 0)),
        pl.BlockSpec((None, FBP, D), lambda j, k: (j, 0, 0)),
        (n, FBP, D), (FBP, D), (n, T // tk), name, jobs)


def _grad_w_mi(hb, dproj, tk, name):
    T = hb.shape[0]
    return _matmul_tn(
        hb, dproj,
        pl.BlockSpec((tk, D), lambda j, k: (k, 0)),
        pl.BlockSpec((tk, MB), lambda j, k: (k, j)),
        pl.BlockSpec((None, D, MB), lambda j, k: (j, 0, 0)),
        (8, D, MB), (D, MB), (8, T // tk), name)


def _grad_w_mo(ycat, dym, tk, name):
    T = ycat.shape[0]
    return _matmul_tn(
        ycat, dym,
        pl.BlockSpec((tk, 256), lambda j, k: (k, j)),
        pl.BlockSpec((tk, D), lambda j, k: (k, 0)),
        pl.BlockSpec((256, D), lambda j, k: (j, 0)),
        (D, D), (256, D), (4, T // tk), name)


def _adamw_math(w, g, m, v):
    m2 = ADAM_B1 * m + (1.0 - ADAM_B1) * g
    v2 = ADAM_B2 * v + (1.0 - ADAM_B2) * (g * g)
    m_hat = m2 / (1.0 - ADAM_B1 ** ADAM_STEP)
    v_hat = v2 / (1.0 - ADAM_B2 ** ADAM_STEP)
    delta = -ADAM_LR * (m_hat / (jnp.sqrt(v_hat) + ADAM_EPS) + ADAM_WD * w)
    return delta, m2, v2


def _adamw_reduce(parts, w, m, v, tr, name):
    R, C = w.shape

    def core(ins, outs, _):
        p_ref, w_ref, m_ref, v_ref = ins
        g_ref, d_ref, m2_ref, v2_ref = outs
        g = p_ref[0].astype(F32)
        for s in range(1, NDEV):
            g = g + p_ref[s].astype(F32)
        g_ref[...] = g
        d_ref[...], m2_ref[...], v2_ref[...] = _adamw_math(w_ref[...], g, m_ref[...], v_ref[...])

    blk = pl.BlockSpec((tr, C), lambda i: (i, 0))
    return _call(
        core, name=name, grid=(R // tr,),
        in_specs=[pl.BlockSpec((NDEV, tr, C), lambda i: (0, i, 0)), blk, blk, blk],
        out_specs=[blk, blk, blk, blk], out_shape=[jax.ShapeDtypeStruct((R, C), F32)] * 4,
        args=[parts, w, m, v])[0]


def _adamw_ada(sc_all, dd, w, m, v, tr, name):
    R, C = w.shape

    def core(ins, outs, _):
        sc_ref, dd_ref, w_ref, m_ref, v_ref = ins
        g_ref, d_ref, m2_ref, v2_ref = outs
        g = _dot_tn(sc_ref[...].astype(BF16), dd_ref[...].astype(BF16))
        g_ref[...] = g
        d_ref[...], m2_ref[...], v2_ref[...] = _adamw_math(w_ref[...], g, m_ref[...], v_ref[...])

    blk = pl.BlockSpec((tr, C), lambda i: (i, 0))
    return _call(
        core, name=name, grid=(R // tr,),
        in_specs=[pl.BlockSpec((64, tr), lambda i: (0, i)), pl.BlockSpec((64, C), lambda i: (0, 0)), blk, blk, blk],
        out_specs=[blk, blk, blk, blk], out_shape=[jax.ShapeDtypeStruct((R, C), F32)] * 4,
        args=[sc_all, dd, w, m, v])[0]


def _adamw_small(grads, wmv, name):
    nw = len(grads)
    srcs = []
    for arr, _ in grads:
        if not any(arr is s for s in srcs):
            srcs.append(arr)
    src_of = [[arr is s for s in srcs].index(True) for arr, _ in grads]

    def core(ins, outs, _):
        s_refs = ins[:len(srcs)]
        w_refs = ins[len(srcs):]
        for t in range(nw):
            row = grads[t][1]
            g = s_refs[src_of[t]][...] if row is None else s_refs[src_of[t]][row:row + 1, :]
            w_ref, m_ref, v_ref = w_refs[3 * t:3 * t + 3]
            g_ref, d_ref, m2_ref, v2_ref = outs[4 * t:4 * t + 4]
            g_ref[...] = g
            d_ref[...], m2_ref[...], v2_ref[...] = _adamw_math(w_ref[...], g, m_ref[...], v_ref[...])

    out_shape = []
    for t in range(nw):
        out_shape += [jax.ShapeDtypeStruct(wmv[3 * t].shape, F32)] * 4
    return _call(
        core, name=name, grid=(), in_specs=[VM] * (len(srcs) + 3 * nw), out_specs=[VM] * (4 * nw),
        out_shape=out_shape, args=srcs + list(wmv))[0]


def _ada_fwd(c_pad, w_ada, b_cols, cw_pad):
    def body(c_ref, w_ref, b_ref, cwp_ref, ada_ref, sc_ref, cw_ref, cbuf, send_buf, ssem, rsem):
        me = _me()
        mi = _lin(me)
        cbuf[mi] = c_ref[...]
        cw_ref[mi] = cwp_ref[...]
        peers = [_flip(me, f) for f in FLIPS]
        first = []
        for k, p in enumerate(peers):
            first.append(_remote(cbuf.at[mi], cbuf.at[mi], ssem.at[k], rsem.at[k], p))
            first.append(_remote(cw_ref.at[mi], cw_ref.at[mi], ssem.at[7 + k], rsem.at[7 + k], p))
        for cp in first:
            cp.start()
        for k, p in enumerate(peers):
            pi = _lin(p)
            _remote(cbuf.at[pi], cbuf.at[pi], ssem.at[k], rsem.at[k], p).wait_recv()
            _remote(cw_ref.at[pi], cw_ref.at[pi], ssem.at[7 + k], rsem.at[7 + k], p).wait_recv()
        c_all = cbuf[...].reshape(8 * 8, D)
        sc = c_all * _sigmoid(c_all)
        sc_ref[...] = sc
        res = _dot(sc.astype(BF16), w_ref[...].astype(BF16)) + b_ref[...]
        send_buf[...] = res.reshape(8, 8, ADA_B)
        ada_ref[mi] = send_buf[mi]
        second = []
        for k, p in enumerate(peers):
            second.append(_remote(send_buf.at[_lin(p)], ada_ref.at[mi], ssem.at[14 + k], rsem.at[14 + k], p))
        for cp in second:
            cp.start()
        for k, p in enumerate(peers):
            _remote(send_buf.at[mi], ada_ref.at[_lin(p)], ssem.at[14 + k], rsem.at[14 + k], p).wait_recv()
        for cp in first + second:
            cp.wait_send()

    return pl.pallas_call(
        body,
        name="ada_fwd",
        in_specs=[VM, VM, VM, VM],
        out_specs=[VM, VM, VM],
        out_shape=[
            jax.ShapeDtypeStruct((8, 8, ADA_B), F32),
            jax.ShapeDtypeStruct((64, D), F32),
            jax.ShapeDtypeStruct((8, 32, 64), F32),
        ],
        scratch_shapes=[
            pltpu.VMEM((8, 8, D), F32),
            pltpu.VMEM((8, 8, ADA_B), F32),
            pltpu.SemaphoreType.DMA((21,)),
            pltpu.SemaphoreType.DMA((21,)),
        ],
        compiler_params=pltpu.CompilerParams(vmem_limit_bytes=VMEM_LIMIT),
    )(c_pad, w_ada, b_cols, cw_pad)


def _ada_bwd(dada):
    def body(d_ref, dd_ref, gb_ref, rbuf, ssem, rsem):
        me = _me()
        mi = _lin(me)
        peers = [_flip(me, f) for f in FLIPS]
        rbuf[mi] = d_ref[mi]
        first = []
        for k, p in enumerate(peers):
            first.append(_remote(d_ref.at[_lin(p)], rbuf.at[mi], ssem.at[k], rsem.at[k], p))
        for cp in first:
            cp.start()
        for k, p in enumerate(peers):
            _remote(d_ref.at[mi], rbuf.at[_lin(p)], ssem.at[k], rsem.at[k], p).wait_recv()
        dd = rbuf[...].reshape(64, ADA_B)
        dd_ref[...] = dd
        gb_ref[mi] = jnp.broadcast_to(_colsum(dd), (8, ADA_B))
        second = []
        for k, p in enumerate(peers):
            second.append(_remote(gb_ref.at[mi], gb_ref.at[mi], ssem.at[7 + k], rsem.at[7 + k], p))
        for cp in second:
            cp.start()
        for k, p in enumerate(peers):
            pi = _lin(p)
            _remote(gb_ref.at[pi], gb_ref.at[pi], ssem.at[7 + k], rsem.at[7 + k], p).wait_recv()
        for cp in first + second:
            cp.wait_send()

    return pl.pallas_call(
        body,
        name="ada_bwd",
        in_specs=[VM],
        out_specs=[VM, VM],
        out_shape=[jax.ShapeDtypeStruct((64, ADA_B), F32), jax.ShapeDtypeStruct((8, 8, ADA_B), F32)],
        scratch_shapes=[
            pltpu.VMEM((8, 8, ADA_B), F32),
            pltpu.SemaphoreType.DMA((14,)),
            pltpu.SemaphoreType.DMA((14,)),
        ],
        compiler_params=pltpu.CompilerParams(vmem_limit_bytes=VMEM_LIMIT),
    )(dada)


def _tail_exchange(p_small, job):
    shapes = [p.shape for p in p_small]

    def body(*refs):
        p_refs = refs[0:4]
        j_ins = refs[4:4 + len(job.ins)]
        o = 4 + len(job.ins)
        s_refs = refs[o:o + 4]
        j_outs = refs[o + 4:o + 4 + len(job.out_shape)]
        o = o + 4 + len(job.out_shape)
        bufs = refs[o:o + 4]
        ssem, rsem = refs[o + 4:o + 6]
        j_sems = refs[o + 6:]
        me = _me()
        mi = _lin(me)
        job.start(j_ins, j_outs, j_sems)
        sent = []
        for a in range(4):
            bufs[a][mi] = p_refs[a][...]
        for k, f in enumerate(FLIPS):
            p = _flip(me, f)
            for a in range(4):
                cp = _remote(bufs[a].at[mi], bufs[a].at[mi], ssem.at[7 * a + k], rsem.at[7 * a + k], p)
                cp.start()
                sent.append(cp)
        for k, f in enumerate(FLIPS):
            p = _flip(me, f)
            pi = _lin(p)
            for a in range(4):
                _remote(bufs[a].at[pi], bufs[a].at[pi], ssem.at[7 * a + k], rsem.at[7 * a + k], p).wait_recv()
        for cp in sent:
            cp.wait_send()
        for a in range(4):
            s = bufs[a][0]
            for dev in range(1, NDEV):
                s = s + bufs[a][dev]
            s_refs[a][...] = s
        job.end(j_ins, j_outs, j_sems)

    res = pl.pallas_call(
        body,
        name="tail_exchange",
        in_specs=[VM] * 4 + [HBM] * len(job.ins),
        out_specs=[VM] * 4 + [HBM] * len(job.out_shape),
        out_shape=[jax.ShapeDtypeStruct(s, F32) for s in shapes] + job.out_shape,
        scratch_shapes=[pltpu.VMEM((NDEV,) + s, F32) for s in shapes]
        + [pltpu.SemaphoreType.DMA((28,)), pltpu.SemaphoreType.DMA((28,))] + job.sems,
        compiler_params=pltpu.CompilerParams(vmem_limit_bytes=VMEM_LIMIT),
    )(*p_small, *job.ins)
    return list(res[:4]), list(res[4:])


SMALL_D = ("g_pre_f1", "g_post_f1", "g_pre_m", "g_post_m", "g_pre_f2", "g_post_f2")
SMALL_W = ("gmlp_norm_g", "gmlp_norm_b", "conv_b", "conv_norm_g", "conv_norm_b", "g_out_a", "g_out_b")


def kernel(x, c, w_ada, b_ada, g_pre_f1, g_post_f1, w_f1_in, w_f1_out, g_pre_m, g_post_m, w_mix_in, gmlp_norm_g, gmlp_norm_b, w_spatial, b_spatial, conv_w, conv_b, conv_norm_g, conv_norm_b, g_out_a, g_out_b, w_mix_out, g_pre_f2, g_post_f2, w_f2_in, w_f2_out, loss_target, m_w_ada, m_b_ada, m_g_pre_f1, m_g_post_f1, m_w_f1_in, m_w_f1_out, m_g_pre_m, m_g_post_m, m_w_mix_in, m_gmlp_norm_g, m_gmlp_norm_b, m_w_spatial, m_b_spatial, m_conv_w, m_conv_b, m_conv_norm_g, m_conv_norm_b, m_g_out_a, m_g_out_b, m_w_mix_out, m_g_pre_f2, m_g_post_f2, m_w_f2_in, m_w_f2_out, v_w_ada, v_b_ada, v_g_pre_f1, v_g_post_f1, v_w_f1_in, v_w_f1_out, v_g_pre_m, v_g_post_m, v_w_mix_in, v_gmlp_norm_g, v_gmlp_norm_b, v_w_spatial, v_b_spatial, v_conv_w, v_conv_b, v_conv_norm_g, v_conv_norm_b, v_g_out_a, v_g_out_b, v_w_mix_out, v_g_pre_f2, v_g_post_f2, v_w_f2_in, v_w_f2_out):
    given = dict(locals())
    bl, seq, _ = x.shape
    T = bl * seq
    tm = min(256, seq // 2)
    tk = min(512, T)
    mi = _lin((lax.axis_index("x"), lax.axis_index("y"), lax.axis_index("c")))

    c_pad = jnp.pad(c, ((0, 8 - bl), (0, 0)))
    b_cols = lax.dynamic_slice(b_ada, (0, mi * ADA_B), (1, ADA_B))
    cw_pad = jnp.pad(conv_w[0], ((0, 1), (0, 0)))
    ada_blk, sc_all, cw_all = _ada_fwd(c_pad, w_ada[0], b_cols, cw_pad)
    ada = ada_blk[:, 0:bl, :].transpose(1, 0, 2).reshape(bl, 9, D)
    pad5 = jnp.zeros((bl, 5, D), F32)
    mod1 = jnp.concatenate([ada[:, 0:3], pad5], axis=1)
    mod2 = jnp.concatenate([ada[:, 3:6], pad5], axis=1)
    mod3 = jnp.concatenate([ada[:, 6:9], pad5], axis=1)
    cw_full = cw_all.transpose(1, 0, 2).reshape(32, WA)

    def shard_in(w):
        return jnp.pad(w[0].T.astype(BF16), ((0, FBP - FB), (0, 0)))

    zpad = jnp.zeros((FBP - FB, D), BF16)
    g_f1 = _Gather([shard_in(w_f1_in), w_f1_out[0].astype(BF16)], ("rows", "out"), zpad)
    g_mx = _Gather([w_mix_in[0].astype(BF16), w_mix_out[0].astype(BF16), w_f2_out[0].astype(BF16)],
                   ("rows", "rows", "out"), zpad)
    g_f2 = _Gather([shard_in(w_f2_in)], ("rows",), zpad)
    (wi1, wo1), = _call(None, name="gather_f1", grid=(), in_specs=[], out_specs=[], out_shape=[], args=[], jobs=[g_f1])[1]

    zrow = jnp.zeros((1, D), F32)
    gv1 = jnp.concatenate([g_pre_f1, g_post_f1] + [zrow] * 6, axis=0)
    gvm = jnp.concatenate([g_pre_m, g_post_m] + [zrow] * 6, axis=0)
    gv2 = jnp.concatenate([g_pre_f2, g_post_f2] + [zrow] * 6, axis=0)
    v512 = jnp.concatenate([gmlp_norm_g, gmlp_norm_b, conv_b, conv_norm_g, conv_norm_b, g_out_a, g_out_b,
                            jnp.zeros((1, WA), F32)], axis=0)
    ws = w_spatial[0]
    bias_full = jnp.repeat(b_spatial[0].T, HD, axis=1)
    esel = (lax.broadcasted_iota(jnp.int32, (8, WA), 1) // HD == lax.broadcasted_iota(jnp.int32, (8, WA), 0)).astype(F32)

    x0 = x.reshape(T, D)
    (x1, gu1, y1), ((wmi, wmo, wo2),) = _ffn_fwd(x0, mod1, gv1, wi1, wo1, tm, "ffn1_fwd", jobs=[g_mx])
    wmo = wmo.reshape(D, D)
    (x2, proj, ym), ((wi2,),) = _mixer_fwd(x1, mod2, gvm, wmi, wmo, v512, ws, bias_full, cw_full, tm, "mixer_fwd", jobs=[g_f2])
    (x3, gu2, y2), _ = _ffn_fwd(x2, mod3, gv2, wi2, wo2, tm, "ffn2_fwd")
    dx3, loss_blk = _loss_head(x3, loss_target.reshape(T, D), tm, "loss_head")

    (dx2, dg2, act2, hb2, dyb2, mg3, vg3), _ = _ffn_bwd(dx3, x2, y2, gu2, mod3, gv2, wi2, wo2, tm, "ffn2_bwd")
    (g_wi2,), _ = _grad_rows(dg2, hb2, tk, "ffn2_gw_in")
    (g_wo2,), _ = _grad_rows(act2, dyb2, tk, "ffn2_gw_out")
    (dpart, dymb, ycat, mg2a, vgma, v5g, gws, gbs), ((p_wi2,),) = _mixer_bwd_a(
        dx2, ym, proj, mod2, gvm, wmo, v512, ws, bias_full, cw_full, esel, tm, "mixer_bwd_a",
        jobs=[_Scatter([g_wi2], ("rows",))])
    (dx1, dproj, hbm, mg2b, vgmb, dcw), ((p_wo2,),) = _mixer_bwd_b(
        dx2, x1, dpart, proj, mod2, gvm, wmi, cw_full, tm, "mixer_bwd_b", jobs=[_Scatter([g_wo2], ("out",))])
    (g_wmi,), _ = _grad_w_mi(hbm, dproj, tk, "mixer_gw_in")
    (g_wmo,), _ = _grad_w_mo(ycat, dymb, tk, "mixer_gw_out")
    (dx0, dg1, act1, hb1, dyb1, mg1, vg1), ((p_wmi, p_wmo),) = _ffn_bwd(
        dx1, x0, y1, gu1, mod1, gv1, wi1, wo1, tm, "ffn1_bwd",
        jobs=[_Scatter([g_wmi, g_wmo.reshape(NDEV, MO, D)], ("rows", "rows"))])
    (g_wo1,), _ = _grad_rows(act1, dyb1, tk, "ffn1_gw_out")
    (g_wi1,), ((p_wo1,),) = _grad_rows(dg1, hb1, tk, "ffn1_gw_in", jobs=[_Scatter([g_wo1], ("out",))])

    dada = jnp.concatenate([mg1[:, 0:3], mg2b[:, 0:2], mg2a[:, 2:3], mg3[:, 0:3]], axis=1)
    dada = dada.reshape(bl, NDEV, ADA_B).transpose(1, 0, 2)
    dada = jnp.pad(dada, ((0, 0), (0, 8 - bl), (0, 0)))
    dd_all, gb_all = _ada_bwd(dada)
    g_bada = gb_all[:, 0, :].reshape(1, 9 * D)

    p1 = jnp.concatenate([vg1[0:2], vgmb[0:1], vgma[1:2], vg3[0:2], loss_blk[0:1], zrow], axis=0)
    p2 = jnp.concatenate([v5g, dcw], axis=0)
    (s1, s2, s3, s4), (p_wi1,) = _tail_exchange([p1, p2, gws, gbs], _Scatter([g_wi1], ("rows",)))
    loss = s1[6, 0]

    res = {}
    for nm, part in (("w_f1_in", p_wi1), ("w_f2_in", p_wi2)):
        quad = _adamw_reduce(part, given[nm][0].T, given["m_" + nm][0].T, given["v_" + nm][0].T, FO, "adamw_" + nm)
        res[nm] = tuple(t.T[None] for t in quad)
    for nm, part, tr in (("w_f1_out", p_wo1, FO), ("w_f2_out", p_wo2, FO), ("w_mix_in", p_wmi, 256), ("w_mix_out", p_wmo, MO)):
        quad = _adamw_reduce(part, given[nm][0], given["m_" + nm][0], given["v_" + nm][0], tr, "adamw_" + nm)
        res[nm] = tuple(t[None] for t in quad)
    quad = _adamw_ada(sc_all, dd_all, w_ada[0], m_w_ada[0], v_w_ada[0], 256, "adamw_w_ada")
    res["w_ada"] = tuple(t[None] for t in quad)

    small = SMALL_D + SMALL_W + ("w_spatial", "b_spatial", "b_ada", "conv_w")
    g_cw = lax.dynamic_slice(s2, (8, mi * 64), (32, 64))
    grads = [(s1, r) for r in range(6)] + [(s2, r) for r in range(7)] + [(s3, None), (s4, None), (g_bada, None), (g_cw, None)]
    wmv = []
    for nm in small:
        for pre in ("", "m_", "v_"):
            a = given[pre + nm]
            if nm in ("w_spatial", "b_spatial"):
                a = a[0]
            elif nm == "conv_w":
                a = jnp.pad(a[0], ((0, 1), (0, 0)), constant_values=1.0 if pre == "v_" else 0.0)
            wmv.append(a)
    outs = _adamw_small(grads, wmv, "adamw_small")
    for t, nm in enumerate(small):
        quad = outs[4 * t:4 * t + 4]
        if nm in ("w_spatial", "b_spatial"):
            quad = [q[None] for q in quad]
        elif nm == "conv_w":
            quad = [q[0:CONV_K][None] for q in quad]
        res[nm] = tuple(quad)

    order = ["w_ada", "b_ada", "g_pre_f1", "g_post_f1", "w_f1_in", "w_f1_out", "g_pre_m", "g_post_m", "w_mix_in",
             "gmlp_norm_g", "gmlp_norm_b", "w_spatial", "b_spatial", "conv_w", "conv_b", "conv_norm_g", "conv_norm_b",
             "g_out_a", "g_out_b", "w_mix_out", "g_pre_f2", "g_post_f2", "w_f2_in", "w_f2_out"]
    out = [loss, dx0.reshape(bl, seq, D)]
    for k in range(4):
        out += [res[nm][k] for nm in order]
    return tuple(out)
```

```python
import jax
import jax.numpy as jnp
from jax import lax
from jax.experimental import pallas as pl
from jax.experimental.pallas import tpu as pltpu

F32 = jnp.float32
BF16 = jnp.bfloat16

D = 1024
DFF = 2816
NDEV = 8
FB = 2 * DFF // NDEV
FBP = 768
FO = DFF // NDEV
WA = 512
NHEAD = 8
HD = 64
CHUNK = 128
CONV_K = 31
HALO = 32
MB = 2 * (WA + WA) // NDEV
MO = D // NDEV
ADA_B = 9 * D // NDEV
EPS = 1e-6
HALF = 0.5

ADAM_LR = 0.001
ADAM_B1 = 0.9
ADAM_B2 = 0.999
ADAM_EPS = 1e-08
ADAM_WD = 0.01
ADAM_STEP = 10

VMEM_LIMIT = 56 * 1024 * 1024
MESH = pl.DeviceIdType.MESH
FLIPS = ((0, 0, 1), (1, 0, 0), (0, 1, 0), (1, 1, 0), (1, 0, 1), (0, 1, 1), (1, 1, 1))
CHIP_FLIPS = ((1, 0, 0), (0, 1, 0), (1, 1, 0))
HBM = pl.BlockSpec(memory_space=pl.ANY)
VM = pl.BlockSpec(memory_space=pltpu.VMEM)


def _dot(a, b):
    return lax.dot_general(a, b, (((1,), (0,)), ((), ())), preferred_element_type=F32)


def _dot_nt(a, b):
    return lax.dot_general(a, b, (((1,), (1,)), ((), ())), preferred_element_type=F32)


def _dot_tn(a, b):
    return lax.dot_general(a, b, (((0,), (0,)), ((), ())), preferred_element_type=F32)


def _rowmean(v):
    return jnp.mean(v, axis=-1, keepdims=True)


def _colsum(v):
    return jnp.sum(v, axis=0, keepdims=True)


def _sigmoid(v):
    return 1.0 / (1.0 + jnp.exp(-v))


def _const_spec(shape):
    nd = len(shape)
    return pl.BlockSpec(shape, lambda *_: (0,) * nd, pipeline_mode=pl.Buffered(1))


def _me():
    return lax.axis_index("x"), lax.axis_index("y"), lax.axis_index("c")


def _flip(me, f):
    return tuple(1 - v if b else v for v, b in zip(me, f))


def _lin(p):
    return 4 * p[0] + 2 * p[1] + p[2]


def _remote(src, dst, send_sem, recv_sem, dev):
    return pltpu.make_async_remote_copy(src_ref=src, dst_ref=dst, send_sem=send_sem, recv_sem=recv_sem,
                                        device_id=dev, device_id_type=MESH)


def _blk(kind, ref, p):
    if kind == "out":
        return ref.at[2 * p[0] + p[1], pl.ds(p[2] * FO, FO), :]
    return ref.at[_lin(p)]


class _Gather:
    def __init__(self, shards, kinds, zpad):
        self.kinds = kinds
        self.n = len(shards)
        self.ins = list(shards) + [zpad]
        self.out_shape = [jax.ShapeDtypeStruct((4, FBP, D) if k == "out" else (NDEV,) + s.shape, BF16)
                          for s, k in zip(shards, kinds)]
        self.n_out = sum(k == "out" for k in kinds)
        self.sems = [pltpu.SemaphoreType.DMA((7 * self.n,)), pltpu.SemaphoreType.DMA((7 * self.n,)),
                     pltpu.SemaphoreType.DMA((self.n + 4 * max(self.n_out, 1),))]

    def _first(self, ins, outs, sems):
        ssem, rsem, lsem = sems
        me = _me()
        sib = _flip(me, (0, 0, 1))
        cps, loc = [], []
        nz = 0
        for a in range(self.n):
            mine = _blk(self.kinds[a], outs[a], me)
            loc.append(pltpu.make_async_copy(ins[a], mine, lsem.at[a]))
            if self.kinds[a] == "out":
                for q in range(4):
                    loc.append(pltpu.make_async_copy(ins[self.n], outs[a].at[q, pl.ds(FB, FBP - FB), :],
                                                     lsem.at[self.n + 4 * nz + q]))
                nz += 1
            cps.append(_remote(ins[a], mine, ssem.at[7 * a], rsem.at[7 * a], sib))
            for j, f in enumerate(CHIP_FLIPS):
                cps.append(_remote(ins[a], mine, ssem.at[7 * a + 1 + j], rsem.at[7 * a + 1 + j], _flip(me, f)))
        return cps, loc

    def _passed(self, outs, sems):
        ssem, rsem, _ = sems
        me = _me()
        sib = _flip(me, (0, 0, 1))
        cps = []
        for j, f in enumerate(CHIP_FLIPS):
            for a in range(self.n):
                blk = _blk(self.kinds[a], outs[a], _flip(me, f))
                cps.append(_remote(blk, blk, ssem.at[7 * a + 4 + j], rsem.at[7 * a + 4 + j], sib))
        return cps

    def start(self, ins, outs, sems):
        cps, loc = self._first(ins, outs, sems)
        for cp in loc + cps:
            cp.start()

    def mid(self, ins, outs, sems):
        ssem, rsem, _ = sems
        me = _me()
        passed = self._passed(outs, sems)
        t = 0
        for j, f in enumerate(CHIP_FLIPS):
            for a in range(self.n):
                blk = _blk(self.kinds[a], outs[a], _flip(me, f))
                _remote(blk, blk, ssem.at[7 * a + 1 + j], rsem.at[7 * a + 1 + j], _flip(me, f)).wait_recv()
                passed[t].start()
                t += 1

    def end(self, ins, outs, sems):
        ssem, rsem, _ = sems
        me = _me()
        sib = _flip(me, (0, 0, 1))
        for a in range(self.n):
            blk = _blk(self.kinds[a], outs[a], sib)
            _remote(blk, blk, ssem.at[7 * a], rsem.at[7 * a], sib).wait_recv()
            for j, f in enumerate(CHIP_FLIPS):
                blk = _blk(self.kinds[a], outs[a], _flip(_flip(me, f), (0, 0, 1)))
                _remote(blk, blk, ssem.at[7 * a + 4 + j], rsem.at[7 * a + 4 + j], sib).wait_recv()
        cps, loc = self._first(ins, outs, sems)
        for cp in cps + self._passed(outs, sems):
            cp.wait_send()
        for cp in loc:
            cp.wait()


class _Scatter:
    def __init__(self, grads, kinds):
        self.kinds = kinds
        self.n = len(grads)
        self.ins = list(grads)
        self.out_shape = [jax.ShapeDtypeStruct((NDEV, FO, D) if k == "out" else g.shape, BF16)
                          for g, k in zip(grads, kinds)]
        self.sems = [pltpu.SemaphoreType.DMA((7 * self.n,)), pltpu.SemaphoreType.DMA((7 * self.n,)),
                     pltpu.SemaphoreType.DMA((self.n,))]

    def _copies(self, ins, outs, sems):
        ssem, rsem, lsem = sems
        me = _me()
        mi = _lin(me)
        loc = [pltpu.make_async_copy(_blk(self.kinds[a], ins[a], me), outs[a].at[mi], lsem.at[a]) for a in range(self.n)]
        cps = []
        for k, f in enumerate(FLIPS):
            p = _flip(me, f)
            for a in range(self.n):
                cps.append(_remote(_blk(self.kinds[a], ins[a], p), outs[a].at[mi], ssem.at[7 * a + k], rsem.at[7 * a + k], p))
        return cps, loc

    def start(self, ins, outs, sems):
        cps, loc = self._copies(ins, outs, sems)
        for cp in loc + cps:
            cp.start()

    mid = None

    def end(self, ins, outs, sems):
        ssem, rsem, _ = sems
        me = _me()
        for k, f in enumerate(FLIPS):
            p = _flip(me, f)
            for a in range(self.n):
                _remote(_blk(self.kinds[a], ins[a], me), outs[a].at[_lin(p)], ssem.at[7 * a + k], rsem.at[7 * a + k], p).wait_recv()
        cps, loc = self._copies(ins, outs, sems)
        for cp in cps:
            cp.wait_send()
        for cp in loc:
            cp.wait()


def _call(core, *, name, grid, in_specs, out_specs, out_shape, args, scratch=(), jobs=()):
    n_in, n_out, n_sc = len(in_specs), len(out_specs), len(scratch)
    steps = 1
    for g in grid:
        steps *= g

    def body(*refs):
        pos = [0]

        def take(k):
            r = refs[pos[0]:pos[0] + k]
            pos[0] += k
            return r

        ins = take(n_in)
        j_ins = [take(len(j.ins)) for j in jobs]
        outs = take(n_out)
        j_outs = [take(len(j.out_shape)) for j in jobs]
        scs = take(n_sc)
        j_sems = [take(len(j.sems)) for j in jobs]
        if len(grid) == 2:
            step = pl.program_id(0) * grid[1] + pl.program_id(1)
        elif len(grid) == 1:
            step = pl.program_id(0)
        else:
            step = 0
        for j, ji, jo, js in zip(jobs, j_ins, j_outs, j_sems):
            if grid:
                pl.when(step == 0)(lambda j=j, ji=ji, jo=jo, js=js: j.start(ji, jo, js))
            else:
                j.start(ji, jo, js)
        for j, ji, jo, js in zip(jobs, j_ins, j_outs, j_sems):
            if j.mid is not None:
                if grid:
                    pl.when(step == (3 * steps) // 4)(lambda j=j, ji=ji, jo=jo, js=js: j.mid(ji, jo, js))
                else:
                    j.mid(ji, jo, js)
        if core is not None:
            core(ins, outs, scs)
        for j, ji, jo, js in zip(jobs, j_ins, j_outs, j_sems):
            if grid:
                pl.when(step == steps - 1)(lambda j=j, ji=ji, jo=jo, js=js: j.end(ji, jo, js))
            else:
                j.end(ji, jo, js)

    all_in = list(in_specs)
    all_args = list(args)
    all_out = list(out_specs)
    all_shape = list(out_shape)
    all_sc = list(scratch)
    for j in jobs:
        all_in += [HBM] * len(j.ins)
        all_args += j.ins
    for j in jobs:
        all_out += [HBM] * len(j.out_shape)
        all_shape += j.out_shape
        all_sc += j.sems
    params = dict(vmem_limit_bytes=VMEM_LIMIT)
    if grid:
        params["dimension_semantics"] = ("arbitrary",) * len(grid)
    res = pl.pallas_call(
        body, name=name, grid=grid, in_specs=all_in, out_specs=all_out, out_shape=all_shape,
        scratch_shapes=all_sc, compiler_params=pltpu.CompilerParams(**params),
    )(*all_args)
    core_res = list(res[:n_out])
    job_res = []
    pos = n_out
    for j in jobs:
        job_res.append(list(res[pos:pos + len(j.out_shape)]))
        pos += len(j.out_shape)
    return core_res, job_res


def _ffn_fwd(x, mod, gvec, w_in, w_out, tm, name, jobs=()):
    T = x.shape[0]
    nt = T // tm
    tps = nt // mod.shape[0]

    def core(ins, outs, _):
        x_ref, mod_ref, g_ref, win_ref, wout_ref = ins
        xo_ref, gu_ref, y_ref = outs
        xv = x_ref[...]
        sh, sc, gt = mod_ref[0:1, :], mod_ref[1:2, :], mod_ref[2:3, :]
        r = lax.rsqrt(_rowmean(xv * xv) + EPS)
        h = (xv * r * g_ref[0:1, :]) * (1.0 + sc) + sh
        hb = h.astype(BF16)
        y = jnp.zeros((tm, D), F32)
        for cidx in range(4):
            gate = _dot_nt(hb, win_ref[cidx])
            up = _dot_nt(hb, win_ref[4 + cidx])
            gu_ref[cidx] = gate.astype(BF16)
            gu_ref[4 + cidx] = up.astype(BF16)
            act = gate * _sigmoid(gate) * up
            y = y + _dot(act.astype(BF16), wout_ref[cidx])
        y_ref[...] = y
        ry = lax.rsqrt(_rowmean(y * y) + EPS)
        xo_ref[...] = xv + (HALF * gt) * (y * ry * g_ref[1:2, :])

    tile = pl.BlockSpec((tm, D), lambda i: (i, 0))
    return _call(
        core, name=name, grid=(nt,), jobs=jobs,
        in_specs=[tile, pl.BlockSpec((None, 8, D), lambda i: (i // tps, 0, 0)), _const_spec((8, D)),
                  _const_spec((8, FBP, D)), _const_spec((4, FBP, D))],
        out_specs=[tile, pl.BlockSpec((8, tm, FBP), lambda i: (0, i, 0)), tile],
        out_shape=[jax.ShapeDtypeStruct((T, D), F32), jax.ShapeDtypeStruct((8, T, FBP), BF16),
                   jax.ShapeDtypeStruct((T, D), F32)],
        args=[x, mod, gvec, w_in, w_out])


def _ffn_bwd(dxo, x, y, gu, mod, gvec, w_in, w_out, tm, name, jobs=()):
    T = x.shape[0]
    nt = T // tm
    nb = mod.shape[0]
    tps = nt // nb

    def core(ins, outs, _):
        dxo_ref, x_ref, y_ref, gu_ref, mod_ref, g_ref, win_ref, wout_ref = ins
        dx_ref, dg_ref, act_ref, hb_ref, dyb_ref, mg_ref, vg_ref = outs
        i = pl.program_id(0)
        xv = x_ref[...]
        dxo_v = dxo_ref[...]
        yv = y_ref[...]
        sh, sc, gt = mod_ref[0:1, :], mod_ref[1:2, :], mod_ref[2:3, :]
        gpre, gpost = g_ref[0:1, :], g_ref[1:2, :]
        r = lax.rsqrt(_rowmean(xv * xv) + EPS)
        xh = xv * r
        n = xh * gpre
        hb = (n * (1.0 + sc) + sh).astype(BF16)
        hb_ref[...] = hb
        ry = lax.rsqrt(_rowmean(yv * yv) + EPS)
        yh = yv * ry
        d_gt = _colsum(HALF * dxo_v * (yh * gpost))
        dp = (HALF * gt) * dxo_v
        d_gpost = _colsum(dp * yh)
        dyh = dp * gpost
        dy = ry * (dyh - yh * _rowmean(dyh * yh))
        dyb = dy.astype(BF16)
        dyb_ref[...] = dyb
        dh = jnp.zeros((tm, D), F32)
        for cidx in range(4):
            gate = gu_ref[cidx].astype(F32)
            up = gu_ref[4 + cidx].astype(F32)
            sig = _sigmoid(gate)
            s = gate * sig
            act_ref[cidx] = (s * up).astype(BF16)
            d_act = _dot_nt(dyb, wout_ref[cidx])
            d_up = (d_act * s).astype(BF16)
            d_gate = (d_act * up * (sig * (1.0 + gate * (1.0 - sig)))).astype(BF16)
            dg_ref[cidx] = d_gate
            dg_ref[4 + cidx] = d_up
            dh = dh + _dot(d_gate, win_ref[cidx]) + _dot(d_up, win_ref[4 + cidx])
        d_sc = _colsum(dh * n)
        d_sh = _colsum(dh)
        dn = dh * (1.0 + sc)
        d_gpre = _colsum(dn * xh)
        dxh = dn * gpre
        dx_ref[...] = dxo_v + r * (dxh - xh * _rowmean(dxh * xh))

        @pl.when(i % tps == 0)
        def _():
            mg_ref[...] = jnp.zeros((8, D), F32)

        @pl.when(i == 0)
        def _():
            vg_ref[...] = jnp.zeros((8, D), F32)

        mg_ref[0:1, :] += d_sh
        mg_ref[1:2, :] += d_sc
        mg_ref[2:3, :] += d_gt
        vg_ref[0:1, :] += d_gpre
        vg_ref[1:2, :] += d_gpost

    tile = pl.BlockSpec((tm, D), lambda i: (i, 0))
    return _call(
        core, name=name, grid=(nt,), jobs=jobs,
        in_specs=[tile, tile, tile, pl.BlockSpec((8, tm, FBP), lambda i: (0, i, 0)),
                  pl.BlockSpec((None, 8, D), lambda i: (i // tps, 0, 0)), _const_spec((8, D)),
                  _const_spec((8, FBP, D)), _const_spec((4, FBP, D))],
        out_specs=[tile, pl.BlockSpec((8, tm, FBP), lambda i: (0, i, 0)),
                   pl.BlockSpec((4, tm, FBP), lambda i: (0, i, 0)), tile, tile,
                   pl.BlockSpec((None, 8, D), lambda i: (i // tps, 0, 0)), pl.BlockSpec((8, D), lambda i: (0, 0))],
        out_shape=[jax.ShapeDtypeStruct((T, D), F32), jax.ShapeDtypeStruct((8, T, FBP), BF16),
                   jax.ShapeDtypeStruct((4, T, FBP), BF16), jax.ShapeDtypeStruct((T, D), BF16),
                   jax.ShapeDtypeStruct((T, D), BF16), jax.ShapeDtypeStruct((nb, 8, D), F32),
                   jax.ShapeDtypeStruct((8, D), F32)],
        args=[dxo, x, y, gu, mod, gvec, w_in, w_out])


def _masked_spatial(ws_ref):
    row = lax.broadcasted_iota(jnp.int32, (CHUNK, CHUNK), 0)
    col = lax.broadcasted_iota(jnp.int32, (CHUNK, CHUNK), 1)
    keep = col <= row
    return [jnp.where(keep, ws_ref[hd], 0.0).astype(BF16) for hd in range(NHEAD)]


def _spatial_gate(wm, vb_chunk, lane_head):
    z = jnp.zeros((CHUNK, WA), F32)
    for hd in range(NHEAD):
        z = jnp.where(lane_head == hd, _dot(wm[hd], vb_chunk), z)
    return z


def _layer_norm_stats(v):
    mu = _rowmean(v)
    vc = v - mu
    rstd = lax.rsqrt(_rowmean(vc * vc) + EPS)
    return vc * rstd, rstd


def _causal_conv(ext_ref, cw_ref, bias, tm, rows=64):
    off = HALO - (CONV_K - 1)
    out = []
    for rc in range(tm // rows):
        acc = jnp.broadcast_to(bias, (rows, WA))
        for k in range(CONV_K):
            acc = acc + cw_ref[k:k + 1, :] * ext_ref[rc * rows + off + k:rc * rows + off + k + rows, :]
        out.append(acc)
    return jnp.concatenate(out, axis=0)


def _mixer_fwd(x, mod, gvec, w_mi, w_mo, v512, ws, bias_full, cw, tm, name, jobs=()):
    T = x.shape[0]
    nt = T // tm
    tps = nt // mod.shape[0]

    def core(ins, outs, scs):
        x_ref, mod_ref, g_ref, wmi_ref, wmo_ref, v_ref, ws_ref, bias_ref, cw_ref = ins
        xo_ref, proj_ref, ym_ref = outs
        (glu_ext,) = scs
        i = pl.program_id(0)
        xv = x_ref[...]
        sh, sc, gt = mod_ref[0:1, :], mod_ref[1:2, :], mod_ref[2:3, :]
        r = lax.rsqrt(_rowmean(xv * xv) + EPS)
        hb = ((xv * r * g_ref[0:1, :]) * (1.0 + sc) + sh).astype(BF16)
        for j in range(NDEV):
            proj_ref[:, j * MB:(j + 1) * MB] = _dot(hb, wmi_ref[j])
        u = proj_ref[:, 0:WA]
        v0 = proj_ref[:, WA:2 * WA]
        a = proj_ref[:, 2 * WA:3 * WA]
        g = proj_ref[:, 3 * WA:4 * WA]
        vh, _ = _layer_norm_stats(v0)
        vb = (vh * v_ref[0:1, :] + v_ref[1:2, :]).astype(BF16)
        wm = _masked_spatial(ws_ref)
        lane_head = lax.broadcasted_iota(jnp.int32, (CHUNK, WA), 1) >> 6
        ya = []
        for q in range(tm // CHUNK):
            z = _spatial_gate(wm, vb[q * CHUNK:(q + 1) * CHUNK, :], lane_head) + bias_ref[...]
            ya.append(u[q * CHUNK:(q + 1) * CHUNK, :] * z)
        ya = jnp.concatenate(ya, axis=0)
        glu = a * _sigmoid(g)

        @pl.when(i % tps == 0)
        def _():
            glu_ext[0:HALO, :] = jnp.zeros((HALO, WA), F32)

        glu_ext[HALO:HALO + tm, :] = glu
        conv = _causal_conv(glu_ext, cw_ref, v_ref[2:3, :], tm)
        glu_ext[0:HALO, :] = glu_ext[tm:tm + HALO, :]
        ch, _ = _layer_norm_stats(conv)
        cn = ch * v_ref[3:4, :] + v_ref[4:5, :]
        yb = cn * _sigmoid(cn)
        pa = ya * lax.rsqrt(_rowmean(ya * ya) + EPS) * v_ref[5:6, :]
        pb = yb * lax.rsqrt(_rowmean(yb * yb) + EPS) * v_ref[6:7, :]
        ycat = jnp.concatenate([pa, pb], axis=1).astype(BF16)
        ym = _dot(ycat, wmo_ref[...])
        ym_ref[...] = ym
        rm = lax.rsqrt(_rowmean(ym * ym) + EPS)
        xo_ref[...] = xv + gt * (ym * rm * g_ref[1:2, :])

    tile = pl.BlockSpec((tm, D), lambda i: (i, 0))
    return _call(
        core, name=name, grid=(nt,), jobs=jobs,
        in_specs=[tile, pl.BlockSpec((None, 8, D), lambda i: (i // tps, 0, 0)), _const_spec((8, D)),
                  _const_spec((NDEV, D, MB)), _const_spec((D, D)), _const_spec((8, WA)),
                  _const_spec((NHEAD, CHUNK, CHUNK)), _const_spec((CHUNK, WA)), _const_spec((32, WA))],
        out_specs=[tile, pl.BlockSpec((tm, 4 * WA), lambda i: (i, 0)), tile],
        out_shape=[jax.ShapeDtypeStruct((T, D), F32), jax.ShapeDtypeStruct((T, 4 * WA), F32),
                   jax.ShapeDtypeStruct((T, D), F32)],
        scratch=[pltpu.VMEM((tm + HALO, WA), F32)],
        args=[x, mod, gvec, w_mi, w_mo, v512, ws, bias_full, cw])


def _mixer_bwd_a(dxo, ym, proj, mod, gvec, w_mo, v512, ws, bias_full, cw, esel, tm, name, jobs=()):
    T = dxo.shape[0]
    nt = T // tm
    nb = mod.shape[0]
    tps = nt // nb
    hpt = tm // HALO

    def core(ins, outs, scs):
        dxo_ref, ym_ref, proj_ref, halo_ref, mod_ref, g_ref, wmo_ref, v_ref, ws_ref, bias_ref, cw_ref, e_ref = ins
        dpart_ref, dymb_ref, ycat_ref, mg_ref, vg_ref, v5g_ref, gws_ref, gbs_ref = outs
        glu_ext, dbs_acc = scs
        i = pl.program_id(0)
        dxo_v = dxo_ref[...]
        ymv = ym_ref[...]
        gt = mod_ref[2:3, :]
        gpost = g_ref[1:2, :]
        rm = lax.rsqrt(_rowmean(ymv * ymv) + EPS)
        ymh = ymv * rm
        d_gt = _colsum(dxo_v * (ymh * gpost))
        dpm = gt * dxo_v
        d_gpost = _colsum(dpm * ymh)
        dymh = dpm * gpost
        dym = (rm * (dymh - ymh * _rowmean(dymh * ymh))).astype(BF16)
        dymb_ref[...] = dym
        dycat = _dot_nt(dym, wmo_ref[...])
        u = proj_ref[:, 0:WA]
        v0 = proj_ref[:, WA:2 * WA]
        a = proj_ref[:, 2 * WA:3 * WA]
        g = proj_ref[:, 3 * WA:4 * WA]
        vh, rv = _layer_norm_stats(v0)
        vb = (vh * v_ref[0:1, :] + v_ref[1:2, :]).astype(BF16)
        wm = _masked_spatial(ws_ref)
        lane_head = lax.broadcasted_iota(jnp.int32, (CHUNK, WA), 1) >> 6
        zs = []
        for q in range(tm // CHUNK):
            zs.append(_spatial_gate(wm, vb[q * CHUNK:(q + 1) * CHUNK, :], lane_head) + bias_ref[...])
        z = jnp.concatenate(zs, axis=0)
        ya = u * z
        ra = lax.rsqrt(_rowmean(ya * ya) + EPS)
        yah = ya * ra
        first = i % tps == 0
        ah = halo_ref[:, 0:WA]
        gh = halo_ref[:, WA:2 * WA]
        glu_ext[0:HALO, :] = jnp.where(first, 0.0, ah * _sigmoid(gh))
        glu_ext[HALO:HALO + tm, :] = a * _sigmoid(g)
        conv = _causal_conv(glu_ext, cw_ref, v_ref[2:3, :], tm)
        ch, rc = _layer_norm_stats(conv)
        cn = ch * v_ref[3:4, :] + v_ref[4:5, :]
        sg = _sigmoid(cn)
        yb = cn * sg
        rb = lax.rsqrt(_rowmean(yb * yb) + EPS)
        ybh = yb * rb
        ycat_ref[...] = jnp.concatenate([yah * v_ref[5:6, :], ybh * v_ref[6:7, :]], axis=1).astype(BF16)
        dpa = dycat[:, 0:WA]
        dpb = dycat[:, WA:2 * WA]
        d_goa = _colsum(dpa * yah)
        d_gob = _colsum(dpb * ybh)
        dyah = dpa * v_ref[5:6, :]
        dybh = dpb * v_ref[6:7, :]
        dya = ra * (dyah - yah * _rowmean(dyah * yah))
        dyb = rb * (dybh - ybh * _rowmean(dybh * ybh))
        dpart_ref[:, 0:WA] = dya * z
        dz = dya * u

        @pl.when(i == 0)
        def _():
            gws_ref[...] = jnp.zeros((NHEAD, CHUNK, CHUNK), F32)
            dbs_acc[...] = jnp.zeros((CHUNK, WA), F32)
            vg_ref[...] = jnp.zeros((8, D), F32)
            v5g_ref[...] = jnp.zeros((8, WA), F32)

        dvs = []
        for q in range(tm // CHUNK):
            dz_q = dz[q * CHUNK:(q + 1) * CHUNK, :]
            vb_q = vb[q * CHUNK:(q + 1) * CHUNK, :]
            dbs_acc[...] += dz_q
            dzb = dz_q.astype(BF16)
            dv_q = jnp.zeros((CHUNK, WA), F32)
            for hd in range(NHEAD):
                dv_q = jnp.where(lane_head == hd, _dot_tn(wm[hd], dzb), dv_q)
                dz_hd = jnp.where(lane_head == hd, dz_q, 0.0).astype(BF16)
                gws_ref[hd] += _dot_nt(dz_hd, vb_q)
            dvs.append(dv_q)
        dv = jnp.concatenate(dvs, axis=0)
        d_gng = _colsum(dv * vh)
        d_gnb = _colsum(dv)
        dvh = dv * v_ref[0:1, :]
        dpart_ref[:, WA:2 * WA] = rv * (dvh - _rowmean(dvh) - vh * _rowmean(dvh * vh))
        dcn = dyb * (sg * (1.0 + cn * (1.0 - sg)))
        d_cng = _colsum(dcn * ch)
        d_cnb = _colsum(dcn)
        dch = dcn * v_ref[3:4, :]
        dconv = rc * (dch - _rowmean(dch) - ch * _rowmean(dch * ch))
        dpart_ref[:, 2 * WA:3 * WA] = dconv
        dpart_ref[:, 3 * WA:4 * WA] = jnp.zeros((tm, WA), F32)
        d_cb = _colsum(dconv)

        @pl.when(i % tps == 0)
        def _():
            mg_ref[...] = jnp.zeros((8, D), F32)

        mg_ref[2:3, :] += d_gt
        vg_ref[1:2, :] += d_gpost
        v5g_ref[0:1, :] += d_gng
        v5g_ref[1:2, :] += d_gnb
        v5g_ref[2:3, :] += d_cb
        v5g_ref[3:4, :] += d_cng
        v5g_ref[4:5, :] += d_cnb
        v5g_ref[5:6, :] += d_goa
        v5g_ref[6:7, :] += d_gob

        @pl.when(i == nt - 1)
        def _():
            row = lax.broadcasted_iota(jnp.int32, (CHUNK, CHUNK), 0)
            col = lax.broadcasted_iota(jnp.int32, (CHUNK, CHUNK), 1)
            for hd in range(NHEAD):
                gws_ref[hd] = jnp.where(col <= row, gws_ref[hd], 0.0)
            gbs_ref[...] = lax.dot_general(e_ref[...], dbs_acc[...], (((1,), (1,)), ((), ())),
                                           precision=lax.Precision.HIGHEST, preferred_element_type=F32)

    tile = pl.BlockSpec((tm, D), lambda i: (i, 0))
    ptile = pl.BlockSpec((tm, 4 * WA), lambda i: (i, 0))
    return _call(
        core, name=name, grid=(nt,), jobs=jobs,
        in_specs=[tile, tile, ptile,
                  pl.BlockSpec((HALO, 2 * WA), lambda i: (jnp.maximum(i * hpt - 1, 0), 1)),
                  pl.BlockSpec((None, 8, D), lambda i: (i // tps, 0, 0)), _const_spec((8, D)), _const_spec((D, D)),
                  _const_spec((8, WA)), _const_spec((NHEAD, CHUNK, CHUNK)), _const_spec((CHUNK, WA)),
                  _const_spec((32, WA)), _const_spec((8, WA))],
        out_specs=[ptile, tile, tile, pl.BlockSpec((None, 8, D), lambda i: (i // tps, 0, 0)),
                   pl.BlockSpec((8, D), lambda i: (0, 0)), pl.BlockSpec((8, WA), lambda i: (0, 0)),
                   pl.BlockSpec((NHEAD, CHUNK, CHUNK), lambda i: (0, 0, 0)), pl.BlockSpec((8, CHUNK), lambda i: (0, 0))],
        out_shape=[jax.ShapeDtypeStruct((T, 4 * WA), F32), jax.ShapeDtypeStruct((T, D), BF16),
                   jax.ShapeDtypeStruct((T, D), BF16), jax.ShapeDtypeStruct((nb, 8, D), F32),
                   jax.ShapeDtypeStruct((8, D), F32), jax.ShapeDtypeStruct((8, WA), F32),
                   jax.ShapeDtypeStruct((NHEAD, CHUNK, CHUNK), F32), jax.ShapeDtypeStruct((8, CHUNK), F32)],
        scratch=[pltpu.VMEM((tm + HALO, WA), F32), pltpu.VMEM((CHUNK, WA), F32)],
        args=[dxo, ym, proj, proj, mod, gvec, w_mo, v512, ws, bias_full, cw, esel])


def _mixer_bwd_b(dxo, x, dpart, proj, mod, gvec, w_mi, cw, tm, name, jobs=()):
    T = x.shape[0]
    nt = T // tm
    nb = mod.shape[0]
    tps = nt // nb
    hpt = tm // HALO
    nh = T // HALO
    off = HALO - (CONV_K - 1)
    rows = 64

    def core(ins, outs, scs):
        dxo_ref, x_ref, dpart_ref, dnext_ref, ag_ref, halo_ref, mod_ref, g_ref, wmi_ref, cw_ref = ins
        dx_ref, dproj_ref, hb_ref, mg_ref, vg_ref, dcw_ref = outs
        glu_ext, dconv_ext = scs
        i = pl.program_id(0)
        first = i % tps == 0
        last = i % tps == tps - 1
        a = ag_ref[:, 0:WA]
        g = ag_ref[:, WA:2 * WA]
        sgg = _sigmoid(g)
        glu_ext[0:HALO, :] = jnp.where(first, 0.0, halo_ref[:, 0:WA] * _sigmoid(halo_ref[:, WA:2 * WA]))
        glu_ext[HALO:HALO + tm, :] = a * sgg
        dconv_ext[0:tm, :] = dpart_ref[:, 2 * WA:3 * WA]
        dconv_ext[tm:tm + HALO, :] = jnp.where(last, 0.0, dnext_ref[...])

        @pl.when(i == 0)
        def _():
            dcw_ref[...] = jnp.zeros((32, WA), F32)
            vg_ref[...] = jnp.zeros((8, D), F32)

        dglu = []
        for rc in range(tm // rows):
            dc = dconv_ext[rc * rows:(rc + 1) * rows, :]
            acc = jnp.zeros((rows, WA), F32)
            for k in range(CONV_K):
                dcw_ref[k:k + 1, :] += _colsum(dc * glu_ext[rc * rows + off + k:rc * rows + off + k + rows, :])
                acc = acc + cw_ref[k:k + 1, :] * dconv_ext[rc * rows + (CONV_K - 1) - k:
                                                            rc * rows + (CONV_K - 1) - k + rows, :]
            dglu.append(acc)
        dglu = jnp.concatenate(dglu, axis=0)
        da = dglu * sgg
        dgg = dglu * a * (sgg * (1.0 - sgg))
        dproj_ref[:, 0:2 * WA] = dpart_ref[:, 0:2 * WA].astype(BF16)
        dproj_ref[:, 2 * WA:3 * WA] = da.astype(BF16)
        dproj_ref[:, 3 * WA:4 * WA] = dgg.astype(BF16)
        dh = jnp.zeros((tm, D), F32)
        for j in range(NDEV):
            dh = dh + _dot_nt(dproj_ref[:, j * MB:(j + 1) * MB], wmi_ref[j])
        xv = x_ref[...]
        sc, sh = mod_ref[1:2, :], mod_ref[0:1, :]
        gpre = g_ref[0:1, :]
        r = lax.rsqrt(_rowmean(xv * xv) + EPS)
        xh = xv * r
        n = xh * gpre
        hb_ref[...] = (n * (1.0 + sc) + sh).astype(BF16)
        d_sc = _colsum(dh * n)
        d_sh = _colsum(dh)
        dn = dh * (1.0 + sc)
        d_gpre = _colsum(dn * xh)
        dxh = dn * gpre
        dx_ref[...] = dxo_ref[...] + r * (dxh - xh * _rowmean(dxh * xh))

        @pl.when(first)
        def _():
            mg_ref[...] = jnp.zeros((8, D), F32)

        mg_ref[0:1, :] += d_sh
        mg_ref[1:2, :] += d_sc
        vg_ref[0:1, :] += d_gpre

    tile = pl.BlockSpec((tm, D), lambda i: (i, 0))
    return _call(
        core, name=name, grid=(nt,), jobs=jobs,
        in_specs=[tile, tile, pl.BlockSpec((tm, 4 * WA), lambda i: (i, 0)),
                  pl.BlockSpec((HALO, WA), lambda i: (jnp.minimum((i + 1) * hpt, nh - 1), 2)),
                  pl.BlockSpec((tm, 2 * WA), lambda i: (i, 1)),
                  pl.BlockSpec((HALO, 2 * WA), lambda i: (jnp.maximum(i * hpt - 1, 0), 1)),
                  pl.BlockSpec((None, 8, D), lambda i: (i // tps, 0, 0)), _const_spec((8, D)),
                  _const_spec((NDEV, D, MB)), _const_spec((32, WA))],
        out_specs=[tile, pl.BlockSpec((tm, 4 * WA), lambda i: (i, 0)), tile,
                   pl.BlockSpec((None, 8, D), lambda i: (i // tps, 0, 0)), pl.BlockSpec((8, D), lambda i: (0, 0)),
                   pl.BlockSpec((32, WA), lambda i: (0, 0))],
        out_shape=[jax.ShapeDtypeStruct((T, D), F32), jax.ShapeDtypeStruct((T, 4 * WA), BF16),
                   jax.ShapeDtypeStruct((T, D), BF16), jax.ShapeDtypeStruct((nb, 8, D), F32),
                   jax.ShapeDtypeStruct((8, D), F32), jax.ShapeDtypeStruct((32, WA), F32)],
        scratch=[pltpu.VMEM((tm + HALO, WA), F32), pltpu.VMEM((tm + HALO, WA), F32)],
        args=[dxo, x, dpart, dpart, proj, proj, mod, gvec, w_mi, cw])


def _loss_head(y, target, tm, name):
    T = y.shape[0]

    def core(ins, outs, _):
        y_ref, t_ref = ins
        dy_ref, loss_ref = outs

        @pl.when(pl.program_id(0) == 0)
        def _():
            loss_ref[...] = jnp.zeros((8, D), F32)

        err = y_ref[...] - t_ref[...]
        dy_ref[...] = err * (1.0 / D)
        part = jnp.sum(_rowmean(err * err), axis=0, keepdims=True)
        loss_ref[...] += HALF * part

    tile = pl.BlockSpec((tm, D), lambda i: (i, 0))
    return _call(
        core, name=name, grid=(T // tm,),
        in_specs=[tile, tile], out_specs=[tile, pl.BlockSpec((8, D), lambda i: (0, 0))],
        out_shape=[jax.ShapeDtypeStruct((T, D), F32), jax.ShapeDtypeStruct((8, D), F32)],
        args=[y, target])[0]


def _matmul_tn(a, b, a_spec, b_spec, o_spec, out_shape, acc_shape, grid, name, jobs=()):
    nk = grid[1]

    def core_one(ins, outs, _):
        outs[0][...] = _dot_tn(ins[0][...], ins[1][...]).astype(BF16)

    if nk == 1:
        return _call(core_one, name=name, grid=grid, jobs=jobs, in_specs=[a_spec, b_spec], out_specs=[o_spec],
                     out_shape=[jax.ShapeDtypeStruct(out_shape, BF16)], args=[a, b])

    def core(ins, outs, scs):
        a_ref, b_ref = ins
        (o_ref,) = outs
        (acc,) = scs
        k = pl.program_id(1)

        @pl.when(k == 0)
        def _():
            acc[...] = jnp.zeros(acc_shape, F32)

        acc[...] += _dot_tn(a_ref[...], b_ref[...])

        @pl.when(k == nk - 1)
        def _():
            o_ref[...] = acc[...].astype(BF16)

    return _call(
        core, name=name, grid=grid, jobs=jobs, in_specs=[a_spec, b_spec], out_specs=[o_spec],
        out_shape=[jax.ShapeDtypeStruct(out_shape, BF16)], scratch=[pltpu.VMEM(acc_shape, F32)], args=[a, b])


def _grad_rows(a3, b, tk, name, jobs=()):
    n, T, _ = a3.shape
    return _matmul_tn(
        a3, b,
        pl.BlockSpec((None, tk, FBP), lambda j, k: (j, k, 0)),
        pl.BlockSpec((tk, D), lambda j, k: (k, 0)),
        pl.BlockSpec((None, FBP, D), lambda j, k: (j, 0, 0)),
        (n, FBP, D), (FBP, D), (n, T // tk), name, jobs)


def _grad_w_mi(hb, dproj, tk, name):
    T = hb.shape[0]
    return _matmul_tn(
        hb, dproj,
        pl.BlockSpec((tk, D), lambda j, k: (k, 0)),
        pl.BlockSpec((tk, MB), lambda j, k: (k, j)),
        pl.BlockSpec((None, D, MB), lambda j, k: (j, 0, 0)),
        (8, D, MB), (D, MB), (8, T // tk), name)


def _grad_w_mo(ycat, dym, tk, name):
    T = ycat.shape[0]
    return _matmul_tn(
        ycat, dym,
        pl.BlockSpec((tk, 256), lambda j, k: (k, j)),
        pl.BlockSpec((tk, D), lambda j, k: (k, 0)),
        pl.BlockSpec((256, D), lambda j, k: (j, 0)),
        (D, D), (256, D), (4, T // tk), name)


def _adamw_math(w, g, m, v):
    m2 = ADAM_B1 * m + (1.0 - ADAM_B1) * g
    v2 = ADAM_B2 * v + (1.0 - ADAM_B2) * (g * g)
    m_hat = m2 / (1.0 - ADAM_B1 ** ADAM_STEP)
    v_hat = v2 / (1.0 - ADAM_B2 ** ADAM_STEP)
    delta = -ADAM_LR * (m_hat / (jnp.sqrt(v_hat) + ADAM_EPS) + ADAM_WD * w)
    return delta, m2, v2


def _adamw_reduce(parts, w, m, v, tr, name):
    R, C = w.shape

    def core(ins, outs, _):
        p_ref, w_ref, m_ref, v_ref = ins
        g_ref, d_ref, m2_ref, v2_ref = outs
        g = p_ref[0].astype(F32)
        for s in range(1, NDEV):
            g = g + p_ref[s].astype(F32)
        g_ref[...] = g
        d_ref[...], m2_ref[...], v2_ref[...] = _adamw_math(w_ref[...], g, m_ref[...], v_ref[...])

    blk = pl.BlockSpec((tr, C), lambda i: (i, 0))
    return _call(
        core, name=name, grid=(R // tr,),
        in_specs=[pl.BlockSpec((NDEV, tr, C), lambda i: (0, i, 0)), blk, blk, blk],
        out_specs=[blk, blk, blk, blk], out_shape=[jax.ShapeDtypeStruct((R, C), F32)] * 4,
        args=[parts, w, m, v])[0]


def _adamw_ada(sc_all, dd, w, m, v, tr, name):
    R, C = w.shape

    def core(ins, outs, _):
        sc_ref, dd_ref, w_ref, m_ref, v_ref = ins
        g_ref, d_ref, m2_ref, v2_ref = outs
        g = _dot_tn(sc_ref[...].astype(BF16), dd_ref[...].astype(BF16))
        g_ref[...] = g
        d_ref[...], m2_ref[...], v2_ref[...] = _adamw_math(w_ref[...], g, m_ref[...], v_ref[...])

    blk = pl.BlockSpec((tr, C), lambda i: (i, 0))
    return _call(
        core, name=name, grid=(R // tr,),
        in_specs=[pl.BlockSpec((64, tr), lambda i: (0, i)), pl.BlockSpec((64, C), lambda i: (0, 0)), blk, blk, blk],
        out_specs=[blk, blk, blk, blk], out_shape=[jax.ShapeDtypeStruct((R, C), F32)] * 4,
        args=[sc_all, dd, w, m, v])[0]


def _adamw_small(grads, wmv, name):
    nw = len(grads)
    srcs = []
    for arr, _ in grads:
        if not any(arr is s for s in srcs):
            srcs.append(arr)
    src_of = [[arr is s for s in srcs].index(True) for arr, _ in grads]

    def core(ins, outs, _):
        s_refs = ins[:len(srcs)]
        w_refs = ins[len(srcs):]
        for t in range(nw):
            row = grads[t][1]
            g = s_refs[src_of[t]][...] if row is None else s_refs[src_of[t]][row:row + 1, :]
            w_ref, m_ref, v_ref = w_refs[3 * t:3 * t + 3]
            g_ref, d_ref, m2_ref, v2_ref = outs[4 * t:4 * t + 4]
            g_ref[...] = g
            d_ref[...], m2_ref[...], v2_ref[...] = _adamw_math(w_ref[...], g, m_ref[...], v_ref[...])

    out_shape = []
    for t in range(nw):
        out_shape += [jax.ShapeDtypeStruct(wmv[3 * t].shape, F32)] * 4
    return _call(
        core, name=name, grid=(), in_specs=[VM] * (len(srcs) + 3 * nw), out_specs=[VM] * (4 * nw),
        out_shape=out_shape, args=srcs + list(wmv))[0]


def _ada_fwd(c_pad, w_ada, b_cols, cw_pad):
    def body(c_ref, w_ref, b_ref, cwp_ref, ada_ref, sc_ref, cw_ref, cbuf, send_buf, ssem, rsem):
        me = _me()
        mi = _lin(me)
        cbuf[mi] = c_ref[...]
        cw_ref[mi] = cwp_ref[...]
        peers = [_flip(me, f) for f in FLIPS]
        first = []
        for k, p in enumerate(peers):
            first.append(_remote(cbuf.at[mi], cbuf.at[mi], ssem.at[k], rsem.at[k], p))
            first.append(_remote(cw_ref.at[mi], cw_ref.at[mi], ssem.at[7 + k], rsem.at[7 + k], p))
        for cp in first:
            cp.start()
        for k, p in enumerate(peers):
            pi = _lin(p)
            _remote(cbuf.at[pi], cbuf.at[pi], ssem.at[k], rsem.at[k], p).wait_recv()
            _remote(cw_ref.at[pi], cw_ref.at[pi], ssem.at[7 + k], rsem.at[7 + k], p).wait_recv()
        c_all = cbuf[...].reshape(8 * 8, D)
        sc = c_all * _sigmoid(c_all)
        sc_ref[...] = sc
        res = _dot(sc.astype(BF16), w_ref[...].astype(BF16)) + b_ref[...]
        send_buf[...] = res.reshape(8, 8, ADA_B)
        ada_ref[mi] = send_buf[mi]
        second = []
        for k, p in enumerate(peers):
            second.append(_remote(send_buf.at[_lin(p)], ada_ref.at[mi], ssem.at[14 + k], rsem.at[14 + k], p))
        for cp in second:
            cp.start()
        for k, p in enumerate(peers):
            _remote(send_buf.at[mi], ada_ref.at[_lin(p)], ssem.at[14 + k], rsem.at[14 + k], p).wait_recv()
        for cp in first + second:
            cp.wait_send()

    return pl.pallas_call(
        body,
        name="ada_fwd",
        in_specs=[VM, VM, VM, VM],
        out_specs=[VM, VM, VM],
        out_shape=[
            jax.ShapeDtypeStruct((8, 8, ADA_B), F32),
            jax.ShapeDtypeStruct((64, D), F32),
            jax.ShapeDtypeStruct((8, 32, 64), F32),
        ],
        scratch_shapes=[
            pltpu.VMEM((8, 8, D), F32),
            pltpu.VMEM((8, 8, ADA_B), F32),
            pltpu.SemaphoreType.DMA((21,)),
            pltpu.SemaphoreType.DMA((21,)),
        ],
        compiler_params=pltpu.CompilerParams(vmem_limit_bytes=VMEM_LIMIT),
    )(c_pad, w_ada, b_cols, cw_pad)


def _ada_bwd(dada):
    def body(d_ref, dd_ref, gb_ref, rbuf, ssem, rsem):
        me = _me()
        mi = _lin(me)
        peers = [_flip(me, f) for f in FLIPS]
        rbuf[mi] = d_ref[mi]
        first = []
        for k, p in enumerate(peers):
            first.append(_remote(d_ref.at[_lin(p)], rbuf.at[mi], ssem.at[k], rsem.at[k], p))
        for cp in first:
            cp.start()
        for k, p in enumerate(peers):
            _remote(d_ref.at[mi], rbuf.at[_lin(p)], ssem.at[k], rsem.at[k], p).wait_recv()
        dd = rbuf[...].reshape(64, ADA_B)
        dd_ref[...] = dd
        gb_ref[mi] = jnp.broadcast_to(_colsum(dd), (8, ADA_B))
        second = []
        for k, p in enumerate(peers):
            second.append(_remote(gb_ref.at[mi], gb_ref.at[mi], ssem.at[7 + k], rsem.at[7 + k], p))
        for cp in second:
            cp.start()
        for k, p in enumerate(peers):
            pi = _lin(p)
            _remote(gb_ref.at[pi], gb_ref.at[pi], ssem.at[7 + k], rsem.at[7 + k], p).wait_recv()
        for cp in first + second:
            cp.wait_send()

    return pl.pallas_call(
        body,
        name="ada_bwd",
        in_specs=[VM],
        out_specs=[VM, VM],
        out_shape=[jax.ShapeDtypeStruct((64, ADA_B), F32), jax.ShapeDtypeStruct((8, 8, ADA_B), F32)],
        scratch_shapes=[
            pltpu.VMEM((8, 8, ADA_B), F32),
            pltpu.SemaphoreType.DMA((14,)),
            pltpu.SemaphoreType.DMA((14,)),
        ],
        compiler_params=pltpu.CompilerParams(vmem_limit_bytes=VMEM_LIMIT),
    )(dada)


def _tail_exchange(p_small, job):
    shapes = [p.shape for p in p_small]

    def body(*refs):
        p_refs = refs[0:4]
        j_ins = refs[4:4 + len(job.ins)]
        o = 4 + len(job.ins)
        s_refs = refs[o:o + 4]
        j_outs = refs[o + 4:o + 4 + len(job.out_shape)]
        o = o + 4 + len(job.out_shape)
        bufs = refs[o:o + 4]
        ssem, rsem = refs[o + 4:o + 6]
        j_sems = refs[o + 6:]
        me = _me()
        mi = _lin(me)
        job.start(j_ins, j_outs, j_sems)
        sent = []
        for a in range(4):
            bufs[a][mi] = p_refs[a][...]
        for k, f in enumerate(FLIPS):
            p = _flip(me, f)
            for a in range(4):
                cp = _remote(bufs[a].at[mi], bufs[a].at[mi], ssem.at[7 * a + k], rsem.at[7 * a + k], p)
                cp.start()
                sent.append(cp)
        for k, f in enumerate(FLIPS):
            p = _flip(me, f)
            pi = _lin(p)
            for a in range(4):
                _remote(bufs[a].at[pi], bufs[a].at[pi], ssem.at[7 * a + k], rsem.at[7 * a + k], p).wait_recv()
        for cp in sent:
            cp.wait_send()
        for a in range(4):
            s = bufs[a][0]
            for dev in range(1, NDEV):
                s = s + bufs[a][dev]
            s_refs[a][...] = s
        job.end(j_ins, j_outs, j_sems)

    res = pl.pallas_call(
        body,
        name="tail_exchange",
        in_specs=[VM] * 4 + [HBM] * len(job.ins),
        out_specs=[VM] * 4 + [HBM] * len(job.out_shape),
        out_shape=[jax.ShapeDtypeStruct(s, F32) for s in shapes] + job.out_shape,
        scratch_shapes=[pltpu.VMEM((NDEV,) + s, F32) for s in shapes]
        + [pltpu.SemaphoreType.DMA((28,)), pltpu.SemaphoreType.DMA((28,))] + job.sems,
        compiler_params=pltpu.CompilerParams(vmem_limit_bytes=VMEM_LIMIT),
    )(*p_small, *job.ins)
    return list(res[:4]), list(res[4:])


SMALL_D = ("g_pre_f1", "g_post_f1", "g_pre_m", "g_post_m", "g_pre_f2", "g_post_f2")
SMALL_W = ("gmlp_norm_g", "gmlp_norm_b", "conv_b", "conv_norm_g", "conv_norm_b", "g_out_a", "g_out_b")


def kernel(x, c, w_ada, b_ada, g_pre_f1, g_post_f1, w_f1_in, w_f1_out, g_pre_m, g_post_m, w_mix_in, gmlp_norm_g, gmlp_norm_b, w_spatial, b_spatial, conv_w, conv_b, conv_norm_g, conv_norm_b, g_out_a, g_out_b, w_mix_out, g_pre_f2, g_post_f2, w_f2_in, w_f2_out, loss_target, m_w_ada, m_b_ada, m_g_pre_f1, m_g_post_f1, m_w_f1_in, m_w_f1_out, m_g_pre_m, m_g_post_m, m_w_mix_in, m_gmlp_norm_g, m_gmlp_norm_b, m_w_spatial, m_b_spatial, m_conv_w, m_conv_b, m_conv_norm_g, m_conv_norm_b, m_g_out_a, m_g_out_b, m_w_mix_out, m_g_pre_f2, m_g_post_f2, m_w_f2_in, m_w_f2_out, v_w_ada, v_b_ada, v_g_pre_f1, v_g_post_f1, v_w_f1_in, v_w_f1_out, v_g_pre_m, v_g_post_m, v_w_mix_in, v_gmlp_norm_g, v_gmlp_norm_b, v_w_spatial, v_b_spatial, v_conv_w, v_conv_b, v_conv_norm_g, v_conv_norm_b, v_g_out_a, v_g_out_b, v_w_mix_out, v_g_pre_f2, v_g_post_f2, v_w_f2_in, v_w_f2_out):
    given = dict(locals())
    bl, seq, _ = x.shape
    T = bl * seq
    tm = min(256, seq // 2)
    tk = T
    mi = _lin((lax.axis_index("x"), lax.axis_index("y"), lax.axis_index("c")))

    c_pad = jnp.pad(c, ((0, 8 - bl), (0, 0)))
    b_cols = lax.dynamic_slice(b_ada, (0, mi * ADA_B), (1, ADA_B))
    cw_pad = jnp.pad(conv_w[0], ((0, 1), (0, 0)))
    ada_blk, sc_all, cw_all = _ada_fwd(c_pad, w_ada[0], b_cols, cw_pad)
    ada = ada_blk[:, 0:bl, :].transpose(1, 0, 2).reshape(bl, 9, D)
    pad5 = jnp.zeros((bl, 5, D), F32)
    mod1 = jnp.concatenate([ada[:, 0:3], pad5], axis=1)
    mod2 = jnp.concatenate([ada[:, 3:6], pad5], axis=1)
    mod3 = jnp.concatenate([ada[:, 6:9], pad5], axis=1)
    cw_full = cw_all.transpose(1, 0, 2).reshape(32, WA)

    def shard_in(w):
        return jnp.pad(w[0].T.astype(BF16), ((0, FBP - FB), (0, 0)))

    zpad = jnp.zeros((FBP - FB, D), BF16)
    g_f1 = _Gather([shard_in(w_f1_in), w_f1_out[0].astype(BF16)], ("rows", "out"), zpad)
    g_mx = _Gather([w_mix_in[0].astype(BF16), w_mix_out[0].astype(BF16), w_f2_out[0].astype(BF16)],
                   ("rows", "rows", "out"), zpad)
    g_f2 = _Gather([shard_in(w_f2_in)], ("rows",), zpad)
    (wi1, wo1), = _call(None, name="gather_f1", grid=(), in_specs=[], out_specs=[], out_shape=[], args=[], jobs=[g_f1])[1]

    zrow = jnp.zeros((1, D), F32)
    gv1 = jnp.concatenate([g_pre_f1, g_post_f1] + [zrow] * 6, axis=0)
    gvm = jnp.concatenate([g_pre_m, g_post_m] + [zrow] * 6, axis=0)
    gv2 = jnp.concatenate([g_pre_f2, g_post_f2] + [zrow] * 6, axis=0)
    v512 = jnp.concatenate([gmlp_norm_g, gmlp_norm_b, conv_b, conv_norm_g, conv_norm_b, g_out_a, g_out_b,
                            jnp.zeros((1, WA), F32)], axis=0)
    ws = w_spatial[0]
    bias_full = jnp.repeat(b_spatial[0].T, HD, axis=1)
    esel = (lax.broadcasted_iota(jnp.int32, (8, WA), 1) // HD == lax.broadcasted_iota(jnp.int32, (8, WA), 0)).astype(F32)

    x0 = x.reshape(T, D)
    (x1, gu1, y1), ((wmi, wmo, wo2),) = _ffn_fwd(x0, mod1, gv1, wi1, wo1, tm, "ffn1_fwd", jobs=[g_mx])
    wmo = wmo.reshape(D, D)
    (x2, proj, ym), ((wi2,),) = _mixer_fwd(x1, mod2, gvm, wmi, wmo, v512, ws, bias_full, cw_full, tm, "mixer_fwd", jobs=[g_f2])
    (x3, gu2, y2), _ = _ffn_fwd(x2, mod3, gv2, wi2, wo2, tm, "ffn2_fwd")
    dx3, loss_blk = _loss_head(x3, loss_target.reshape(T, D), tm, "loss_head")

    (dx2, dg2, act2, hb2, dyb2, mg3, vg3), _ = _ffn_bwd(dx3, x2, y2, gu2, mod3, gv2, wi2, wo2, tm, "ffn2_bwd")
    (g_wi2,), _ = _grad_rows(dg2, hb2, tk, "ffn2_gw_in")
    (g_wo2,), _ = _grad_rows(act2, dyb2, tk, "ffn2_gw_out")
    (dpart, dymb, ycat, mg2a, vgma, v5g, gws, gbs), ((p_wi2,),) = _mixer_bwd_a(
        dx2, ym, proj, mod2, gvm, wmo, v512, ws, bias_full, cw_full, esel, tm, "mixer_bwd_a",
        jobs=[_Scatter([g_wi2], ("rows",))])
    (dx1, dproj, hbm, mg2b, vgmb, dcw), ((p_wo2,),) = _mixer_bwd_b(
        dx2, x1, dpart, proj, mod2, gvm, wmi, cw_full, tm, "mixer_bwd_b", jobs=[_Scatter([g_wo2], ("out",))])
    (g_wmi,), _ = _grad_w_mi(hbm, dproj, tk, "mixer_gw_in")
    (g_wmo,), _ = _grad_w_mo(ycat, dymb, tk, "mixer_gw_out")
    (dx0, dg1, act1, hb1, dyb1, mg1, vg1), ((p_wmi, p_wmo),) = _ffn_bwd(
        dx1, x0, y1, gu1, mod1, gv1, wi1, wo1, tm, "ffn1_bwd",
        jobs=[_Scatter([g_wmi, g_wmo.reshape(NDEV, MO, D)], ("rows", "rows"))])
    (g_wo1,), _ = _grad_rows(act1, dyb1, tk, "ffn1_gw_out")
    (g_wi1,), ((p_wo1,),) = _grad_rows(dg1, hb1, tk, "ffn1_gw_in", jobs=[_Scatter([g_wo1], ("out",))])

    dada = jnp.concatenate([mg1[:, 0:3], mg2b[:, 0:2], mg2a[:, 2:3], mg3[:, 0:3]], axis=1)
    dada = dada.reshape(bl, NDEV, ADA_B).transpose(1, 0, 2)
    dada = jnp.pad(dada, ((0, 0), (0, 8 - bl), (0, 0)))
    dd_all, gb_all = _ada_bwd(dada)
    g_bada = gb_all[:, 0, :].reshape(1, 9 * D)

    p1 = jnp.concatenate([vg1[0:2], vgmb[0:1], vgma[1:2], vg3[0:2], loss_blk[0:1], zrow], axis=0)
    p2 = jnp.concatenate([v5g, dcw], axis=0)
    (s1, s2, s3, s4), (p_wi1,) = _tail_exchange([p1, p2, gws, gbs], _Scatter([g_wi1], ("rows",)))
    loss = s1[6, 0]

    res = {}
    for nm, part in (("w_f1_in", p_wi1), ("w_f2_in", p_wi2)):
        quad = _adamw_reduce(part, given[nm][0].T, given["m_" + nm][0].T, given["v_" + nm][0].T, FO, "adamw_" + nm)
        res[nm] = tuple(t.T[None] for t in quad)
    for nm, part, tr in (("w_f1_out", p_wo1, FO), ("w_f2_out", p_wo2, FO), ("w_mix_in", p_wmi, 256), ("w_mix_out", p_wmo, MO)):
        quad = _adamw_reduce(part, given[nm][0], given["m_" + nm][0], given["v_" + nm][0], tr, "adamw_" + nm)
        res[nm] = tuple(t[None] for t in quad)
    quad = _adamw_ada(sc_all, dd_all, w_ada[0], m_w_ada[0], v_w_ada[0], 256, "adamw_w_ada")
    res["w_ada"] = tuple(t[None] for t in quad)

    small = SMALL_D + SMALL_W + ("w_spatial", "b_spatial", "b_ada", "conv_w")
    g_cw = lax.dynamic_slice(s2, (8, mi * 64), (32, 64))
    grads = [(s1, r) for r in range(6)] + [(s2, r) for r in range(7)] + [(s3, None), (s4, None), (g_bada, None), (g_cw, None)]
    wmv = []
    for nm in small:
        for pre in ("", "m_", "v_"):
            a = given[pre + nm]
            if nm in ("w_spatial", "b_spatial"):
                a = a[0]
            elif nm == "conv_w":
                a = jnp.pad(a[0], ((0, 1), (0, 0)), constant_values=1.0 if pre == "v_" else 0.0)
            wmv.append(a)
    outs = _adamw_small(grads, wmv, "adamw_small")
    for t, nm in enumerate(small):
        quad = outs[4 * t:4 * t + 4]
        if nm in ("w_spatial", "b_spatial"):
            quad = [q[None] for q in quad]
        elif nm == "conv_w":
            quad = [q[0:CONV_K][None] for q in quad]
        res[nm] = tuple(quad)

    order = ["w_ada", "b_ada", "g_pre_f1", "g_post_f1", "w_f1_in", "w_f1_out", "g_pre_m", "g_post_m", "w_mix_in",
             "gmlp_norm_g", "gmlp_norm_b", "w_spatial", "b_spatial", "conv_w", "conv_b", "conv_norm_g", "conv_norm_b",
             "g_out_a", "g_out_b", "w_mix_out", "g_pre_f2", "g_post_f2", "w_f2_in", "w_f2_out"]
    out = [loss, dx0.reshape(bl, seq, D)]
    for k in range(4):
        out += [res[nm][k] for nm in order]
    return tuple(out)
```

```python
import jax
import jax.numpy as jnp
from jax import lax
from jax.experimental import pallas as pl
from jax.experimental.pallas import tpu as pltpu

F32 = jnp.float32
BF16 = jnp.bfloat16

D = 1024
DFF = 2816
NDEV = 8
FB = 2 * DFF // NDEV
FBP = 768
FO = DFF // NDEV
WA = 512
NHEAD = 8
HD = 64
CHUNK = 128
CONV_K = 31
HALO = 32
MB = 2 * (WA + WA) // NDEV
MO = D // NDEV
ADA_B = 9 * D // NDEV
EPS = 1e-6
HALF = 0.5

ADAM_LR = 0.001
ADAM_B1 = 0.9
ADAM_B2 = 0.999
ADAM_EPS = 1e-08
ADAM_WD = 0.01
ADAM_STEP = 10

VMEM_LIMIT = 56 * 1024 * 1024
MESH = pl.DeviceIdType.MESH
FLIPS = ((0, 0, 1), (1, 0, 0), (0, 1, 0), (1, 1, 0), (1, 0, 1), (0, 1, 1), (1, 1, 1))
CHIP_FLIPS = ((1, 0, 0), (0, 1, 0), (1, 1, 0))
HBM = pl.BlockSpec(memory_space=pl.ANY)
VM = pl.BlockSpec(memory_space=pltpu.VMEM)


def _dot(a, b):
    return lax.dot_general(a, b, (((1,), (0,)), ((), ())), preferred_element_type=F32)


def _dot_nt(a, b):
    return lax.dot_general(a, b, (((1,), (1,)), ((), ())), preferred_element_type=F32)


def _dot_tn(a, b):
    return lax.dot_general(a, b, (((0,), (0,)), ((), ())), preferred_element_type=F32)


def _rowmean(v):
    return jnp.mean(v, axis=-1, keepdims=True)


def _colsum(v):
    return jnp.sum(v, axis=0, keepdims=True)


def _sigmoid(v):
    return 1.0 / (1.0 + jnp.exp(-v))


def _const_spec(shape):
    nd = len(shape)
    return pl.BlockSpec(shape, lambda *_: (0,) * nd, pipeline_mode=pl.Buffered(1))


def _me():
    return lax.axis_index("x"), lax.axis_index("y"), lax.axis_index("c")


def _flip(me, f):
    return tuple(1 - v if b else v for v, b in zip(me, f))


def _lin(p):
    return 4 * p[0] + 2 * p[1] + p[2]


def _remote(src, dst, send_sem, recv_sem, dev):
    return pltpu.make_async_remote_copy(src_ref=src, dst_ref=dst, send_sem=send_sem, recv_sem=recv_sem,
                                        device_id=dev, device_id_type=MESH)


def _blk(kind, ref, p):
    if kind == "out":
        return ref.at[2 * p[0] + p[1], pl.ds(p[2] * FO, FO), :]
    return ref.at[_lin(p)]


class _Gather:
    def __init__(self, shards, kinds, zpad):
        self.kinds = kinds
        self.n = len(shards)
        self.ins = list(shards) + [zpad]
        self.out_shape = [jax.ShapeDtypeStruct((4, FBP, D) if k == "out" else (NDEV,) + s.shape, BF16)
                          for s, k in zip(shards, kinds)]
        self.n_out = sum(k == "out" for k in kinds)
        self.sems = [pltpu.SemaphoreType.DMA((7 * self.n,)), pltpu.SemaphoreType.DMA((7 * self.n,)),
                     pltpu.SemaphoreType.DMA((self.n + 4 * max(self.n_out, 1),))]

    def _first(self, ins, outs, sems):
        ssem, rsem, lsem = sems
        me = _me()
        sib = _flip(me, (0, 0, 1))
        cps, loc = [], []
        nz = 0
        for a in range(self.n):
            mine = _blk(self.kinds[a], outs[a], me)
            loc.append(pltpu.make_async_copy(ins[a], mine, lsem.at[a]))
            if self.kinds[a] == "out":
                for q in range(4):
                    loc.append(pltpu.make_async_copy(ins[self.n], outs[a].at[q, pl.ds(FB, FBP - FB), :],
                                                     lsem.at[self.n + 4 * nz + q]))
                nz += 1
            cps.append(_remote(ins[a], mine, ssem.at[7 * a], rsem.at[7 * a], sib))
            for j, f in enumerate(CHIP_FLIPS):
                cps.append(_remote(ins[a], mine, ssem.at[7 * a + 1 + j], rsem.at[7 * a + 1 + j], _flip(me, f)))
        return cps, loc

    def _passed(self, outs, sems):
        ssem, rsem, _ = sems
        me = _me()
        sib = _flip(me, (0, 0, 1))
        cps = []
        for j, f in enumerate(CHIP_FLIPS):
            for a in range(self.n):
                blk = _blk(self.kinds[a], outs[a], _flip(me, f))
                cps.append(_remote(blk, blk, ssem.at[7 * a + 4 + j], rsem.at[7 * a + 4 + j], sib))
        return cps

    def start(self, ins, outs, sems):
        cps, loc = self._first(ins, outs, sems)
        for cp in loc + cps:
            cp.start()

    def mid(self, ins, outs, sems):
        ssem, rsem, _ = sems
        me = _me()
        passed = self._passed(outs, sems)
        t = 0
        for j, f in enumerate(CHIP_FLIPS):
            for a in range(self.n):
                blk = _blk(self.kinds[a], outs[a], _flip(me, f))
                _remote(blk, blk, ssem.at[7 * a + 1 + j], rsem.at[7 * a + 1 + j], _flip(me, f)).wait_recv()
                passed[t].start()
                t += 1

    def end(self, ins, outs, sems):
        ssem, rsem, _ = sems
        me = _me()
        sib = _flip(me, (0, 0, 1))
        for a in range(self.n):
            blk = _blk(self.kinds[a], outs[a], sib)
            _remote(blk, blk, ssem.at[7 * a], rsem.at[7 * a], sib).wait_recv()
            for j, f in enumerate(CHIP_FLIPS):
                blk = _blk(self.kinds[a], outs[a], _flip(_flip(me, f), (0, 0, 1)))
                _remote(blk, blk, ssem.at[7 * a + 4 + j], rsem.at[7 * a + 4 + j], sib).wait_recv()
        cps, loc = self._first(ins, outs, sems)
        for cp in cps + self._passed(outs, sems):
            cp.wait_send()
        for cp in loc:
            cp.wait()


class _Scatter:
    def __init__(self, grads, kinds):
        self.kinds = kinds
        self.n = len(grads)
        self.ins = list(grads)
        self.out_shape = [jax.ShapeDtypeStruct((NDEV, FO, D) if k == "out" else g.shape, BF16)
                          for g, k in zip(grads, kinds)]
        self.sems = [pltpu.SemaphoreType.DMA((7 * self.n,)), pltpu.SemaphoreType.DMA((7 * self.n,)),
                     pltpu.SemaphoreType.DMA((self.n,))]

    def _copies(self, ins, outs, sems):
        ssem, rsem, lsem = sems
        me = _me()
        mi = _lin(me)
        loc = [pltpu.make_async_copy(_blk(self.kinds[a], ins[a], me), outs[a].at[mi], lsem.at[a]) for a in range(self.n)]
        cps = []
        for k, f in enumerate(FLIPS):
            p = _flip(me, f)
            for a in range(self.n):
                cps.append(_remote(_blk(self.kinds[a], ins[a], p), outs[a].at[mi], ssem.at[7 * a + k], rsem.at[7 * a + k], p))
        return cps, loc

    def start(self, ins, outs, sems):
        cps, loc = self._copies(ins, outs, sems)
        for cp in loc + cps:
            cp.start()

    mid = None

    def end(self, ins, outs, sems):
        ssem, rsem, _ = sems
        me = _me()
        for k, f in enumerate(FLIPS):
            p = _flip(me, f)
            for a in range(self.n):
                _remote(_blk(self.kinds[a], ins[a], me), outs[a].at[_lin(p)], ssem.at[7 * a + k], rsem.at[7 * a + k], p).wait_recv()
        cps, loc = self._copies(ins, outs, sems)
        for cp in cps:
            cp.wait_send()
        for cp in loc:
            cp.wait()


def _call(core, *, name, grid, in_specs, out_specs, out_shape, args, scratch=(), jobs=()):
    n_in, n_out, n_sc = len(in_specs), len(out_specs), len(scratch)
    steps = 1
    for g in grid:
        steps *= g

    def body(*refs):
        pos = [0]

        def take(k):
            r = refs[pos[0]:pos[0] + k]
            pos[0] += k
            return r

        ins = take(n_in)
        j_ins = [take(len(j.ins)) for j in jobs]
        outs = take(n_out)
        j_outs = [take(len(j.out_shape)) for j in jobs]
        scs = take(n_sc)
        j_sems = [take(len(j.sems)) for j in jobs]
        if len(grid) == 2:
            step = pl.program_id(0) * grid[1] + pl.program_id(1)
        elif len(grid) == 1:
            step = pl.program_id(0)
        else:
            step = 0
        for j, ji, jo, js in zip(jobs, j_ins, j_outs, j_sems):
            if grid:
                pl.when(step == 0)(lambda j=j, ji=ji, jo=jo, js=js: j.start(ji, jo, js))
            else:
                j.start(ji, jo, js)
        for j, ji, jo, js in zip(jobs, j_ins, j_outs, j_sems):
            if j.mid is not None:
                if grid:
                    pl.when(step == (3 * steps) // 4)(lambda j=j, ji=ji, jo=jo, js=js: j.mid(ji, jo, js))
                else:
                    j.mid(ji, jo, js)
        if core is not None:
            core(ins, outs, scs)
        for j, ji, jo, js in zip(jobs, j_ins, j_outs, j_sems):
            if grid:
                pl.when(step == steps - 1)(lambda j=j, ji=ji, jo=jo, js=js: j.end(ji, jo, js))
            else:
                j.end(ji, jo, js)

    all_in = list(in_specs)
    all_args = list(args)
    all_out = list(out_specs)
    all_shape = list(out_shape)
    all_sc = list(scratch)
    for j in jobs:
        all_in += [HBM] * len(j.ins)
        all_args += j.ins
    for j in jobs:
        all_out += [HBM] * len(j.out_shape)
        all_shape += j.out_shape
        all_sc += j.sems
    params = dict(vmem_limit_bytes=VMEM_LIMIT)
    if grid:
        params["dimension_semantics"] = ("arbitrary",) * len(grid)
    res = pl.pallas_call(
        body, name=name, grid=grid, in_specs=all_in, out_specs=all_out, out_shape=all_shape,
        scratch_shapes=all_sc, compiler_params=pltpu.CompilerParams(**params),
    )(*all_args)
    core_res = list(res[:n_out])
    job_res = []
    pos = n_out
    for j in jobs:
        job_res.append(list(res[pos:pos + len(j.out_shape)]))
        pos += len(j.out_shape)
    return core_res, job_res


def _ffn_fwd(x, mod, gvec, w_in, w_out, tm, name, jobs=()):
    T = x.shape[0]
    nt = T // tm
    tps = nt // mod.shape[0]

    def core(ins, outs, _):
        x_ref, mod_ref, g_ref, win_ref, wout_ref = ins
        xo_ref, gu_ref, y_ref = outs
        xv = x_ref[...]
        sh, sc, gt = mod_ref[0:1, :], mod_ref[1:2, :], mod_ref[2:3, :]
        r = lax.rsqrt(_rowmean(xv * xv) + EPS)
        h = (xv * r * g_ref[0:1, :]) * (1.0 + sc) + sh
        hb = h.astype(BF16)
        y = jnp.zeros((tm, D), F32)
        for cidx in range(4):
            gate = _dot_nt(hb, win_ref[cidx])
            up = _dot_nt(hb, win_ref[4 + cidx])
            gu_ref[cidx] = gate.astype(BF16)
            gu_ref[4 + cidx] = up.astype(BF16)
            act = gate * _sigmoid(gate) * up
            y = y + _dot(act.astype(BF16), wout_ref[cidx])
        y_ref[...] = y
        ry = lax.rsqrt(_rowmean(y * y) + EPS)
        xo_ref[...] = xv + (HALF * gt) * (y * ry * g_ref[1:2, :])

    tile = pl.BlockSpec((tm, D), lambda i: (i, 0))
    return _call(
        core, name=name, grid=(nt,), jobs=jobs,
        in_specs=[tile, pl.BlockSpec((None, 8, D), lambda i: (i // tps, 0, 0)), _const_spec((8, D)),
                  _const_spec((8, FBP, D)), _const_spec((4, FBP, D))],
        out_specs=[tile, pl.BlockSpec((8, tm, FBP), lambda i: (0, i, 0)), tile],
        out_shape=[jax.ShapeDtypeStruct((T, D), F32), jax.ShapeDtypeStruct((8, T, FBP), BF16),
                   jax.ShapeDtypeStruct((T, D), F32)],
        args=[x, mod, gvec, w_in, w_out])


def _ffn_bwd(dxo, x, y, gu, mod, gvec, w_in, w_out, tm, name, jobs=()):
    T = x.shape[0]
    nt = T // tm
    nb = mod.shape[0]
    tps = nt // nb

    def core(ins, outs, _):
        dxo_ref, x_ref, y_ref, gu_ref, mod_ref, g_ref, win_ref, wout_ref = ins
        dx_ref, dg_ref, act_ref, hb_ref, dyb_ref, mg_ref, vg_ref = outs
        i = pl.program_id(0)
        xv = x_ref[...]
        dxo_v = dxo_ref[...]
        yv = y_ref[...]
        sh, sc, gt = mod_ref[0:1, :], mod_ref[1:2, :], mod_ref[2:3, :]
        gpre, gpost = g_ref[0:1, :], g_ref[1:2, :]
        r = lax.rsqrt(_rowmean(xv * xv) + EPS)
        xh = xv * r
        n = xh * gpre
        hb = (n * (1.0 + sc) + sh).astype(BF16)
        hb_ref[...] = hb
        ry = lax.rsqrt(_rowmean(yv * yv) + EPS)
        yh = yv * ry
        d_gt = _colsum(HALF * dxo_v * (yh * gpost))
        dp = (HALF * gt) * dxo_v
        d_gpost = _colsum(dp * yh)
        dyh = dp * gpost
        dy = ry * (dyh - yh * _rowmean(dyh * yh))
        dyb = dy.astype(BF16)
        dyb_ref[...] = dyb
        dh = jnp.zeros((tm, D), F32)
        for cidx in range(4):
            gate = gu_ref[cidx].astype(F32)
            up = gu_ref[4 + cidx].astype(F32)
            sig = _sigmoid(gate)
            s = gate * sig
            act_ref[cidx] = (s * up).astype(BF16)
            d_act = _dot_nt(dyb, wout_ref[cidx])
            d_up = (d_act * s).astype(BF16)
            d_gate = (d_act * up * (sig * (1.0 + gate * (1.0 - sig)))).astype(BF16)
            dg_ref[cidx] = d_gate
            dg_ref[4 + cidx] = d_up
            dh = dh + _dot(d_gate, win_ref[cidx]) + _dot(d_up, win_ref[4 + cidx])
        d_sc = _colsum(dh * n)
        d_sh = _colsum(dh)
        dn = dh * (1.0 + sc)
        d_gpre = _colsum(dn * xh)
        dxh = dn * gpre
        dx_ref[...] = dxo_v + r * (dxh - xh * _rowmean(dxh * xh))

        @pl.when(i % tps == 0)
        def _():
            mg_ref[...] = jnp.zeros((8, D), F32)

        @pl.when(i == 0)
        def _():
            vg_ref[...] = jnp.zeros((8, D), F32)

        mg_ref[0:1, :] += d_sh
        mg_ref[1:2, :] += d_sc
        mg_ref[2:3, :] += d_gt
        vg_ref[0:1, :] += d_gpre
        vg_ref[1:2, :] += d_gpost

    tile = pl.BlockSpec((tm, D), lambda i: (i, 0))
    return _call(
        core, name=name, grid=(nt,), jobs=jobs,
        in_specs=[tile, tile, tile, pl.BlockSpec((8, tm, FBP), lambda i: (0, i, 0)),
                  pl.BlockSpec((None, 8, D), lambda i: (i // tps, 0, 0)), _const_spec((8, D)),
                  _const_spec((8, FBP, D)), _const_spec((4, FBP, D))],
        out_specs=[tile, pl.BlockSpec((8, tm, FBP), lambda i: (0, i, 0)),
                   pl.BlockSpec((4, tm, FBP), lambda i: (0, i, 0)), tile, tile,
                   pl.BlockSpec((None, 8, D), lambda i: (i // tps, 0, 0)), pl.BlockSpec((8, D), lambda i: (0, 0))],
        out_shape=[jax.ShapeDtypeStruct((T, D), F32), jax.ShapeDtypeStruct((8, T, FBP), BF16),
                   jax.ShapeDtypeStruct((4, T, FBP), BF16), jax.ShapeDtypeStruct((T, D), BF16),
                   jax.ShapeDtypeStruct((T, D), BF16), jax.ShapeDtypeStruct((nb, 8, D), F32),
                   jax.ShapeDtypeStruct((8, D), F32)],
        args=[dxo, x, y, gu, mod, gvec, w_in, w_out])


def _masked_spatial(ws_ref):
    row = lax.broadcasted_iota(jnp.int32, (CHUNK, CHUNK), 0)
    col = lax.broadcasted_iota(jnp.int32, (CHUNK, CHUNK), 1)
    keep = col <= row
    return [jnp.where(keep, ws_ref[hd], 0.0).astype(BF16) for hd in range(NHEAD)]


def _spatial_gate(wm, vb_chunk, lane_head):
    z = jnp.zeros((CHUNK, WA), F32)
    for hd in range(NHEAD):
        z = jnp.where(lane_head == hd, _dot(wm[hd], vb_chunk), z)
    return z


def _layer_norm_stats(v):
    mu = _rowmean(v)
    vc = v - mu
    rstd = lax.rsqrt(_rowmean(vc * vc) + EPS)
    return vc * rstd, rstd


def _pitch(tm):
    p = tm // 8
    while p % 8 != 4:
        p += 1
    return p


def _lanes(s):
    return slice(s * 128, (s + 1) * 128)


def _to_slabs(ref, row0, val):
    for s in range(4):
        ref[s, row0:row0 + val.shape[0], :] = val[:, _lanes(s)]


def _tap_sum(src, out, cw_ref, bias, tm, start):
    p = _pitch(tm)
    for s in range(4):
        accs = [jnp.broadcast_to(bias[:, _lanes(s)], (8, 128))] * p
        for k in range(CONV_K):
            w = jnp.broadcast_to(cw_ref[k:k + 1, _lanes(s)], (8, 128))
            for v in range(p):
                accs[v] = accs[v] + w * src[s, pl.ds(v + start(k), 8, stride=p), :]
        for v in range(p):
            out[s, pl.ds(v, 8, stride=p), :] = accs[v]
    return jnp.concatenate([out[s, 0:tm, :] for s in range(4)], axis=1)


def _mixer_fwd(x, mod, gvec, w_mi, w_mo, v512, ws, bias_full, cw, tm, name, jobs=()):
    T = x.shape[0]
    nt = T // tm
    tps = nt // mod.shape[0]
    ext_rows = 8 * _pitch(tm)

    def core(ins, outs, scs):
        x_ref, mod_ref, g_ref, wmi_ref, wmo_ref, v_ref, ws_ref, bias_ref, cw_ref = ins
        xo_ref, proj_ref, ym_ref, conv_ref = outs
        glu_ext, conv_scr = scs
        i = pl.program_id(0)
        xv = x_ref[...]
        sh, sc, gt = mod_ref[0:1, :], mod_ref[1:2, :], mod_ref[2:3, :]
        r = lax.rsqrt(_rowmean(xv * xv) + EPS)
        hb = ((xv * r * g_ref[0:1, :]) * (1.0 + sc) + sh).astype(BF16)
        for j in range(NDEV):
            proj_ref[:, j * MB:(j + 1) * MB] = _dot(hb, wmi_ref[j])
        u = proj_ref[:, 0:WA]
        v0 = proj_ref[:, WA:2 * WA]
        a = proj_ref[:, 2 * WA:3 * WA]
        g = proj_ref[:, 3 * WA:4 * WA]
        vh, _ = _layer_norm_stats(v0)
        vb = (vh * v_ref[0:1, :] + v_ref[1:2, :]).astype(BF16)
        wm = _masked_spatial(ws_ref)
        lane_head = lax.broadcasted_iota(jnp.int32, (CHUNK, WA), 1) >> 6
        ya = []
        for q in range(tm // CHUNK):
            z = _spatial_gate(wm, vb[q * CHUNK:(q + 1) * CHUNK, :], lane_head) + bias_ref[...]
            ya.append(u[q * CHUNK:(q + 1) * CHUNK, :] * z)
        ya = jnp.concatenate(ya, axis=0)
        glu = a * _sigmoid(g)

        @pl.when(i == 0)
        def _():
            glu_ext[:, HALO + tm:HALO + ext_rows, :] = jnp.zeros((4, ext_rows - tm, 128), F32)

        @pl.when(i % tps == 0)
        def _():
            glu_ext[:, 0:HALO, :] = jnp.zeros((4, HALO, 128), F32)

        _to_slabs(glu_ext, HALO, glu)
        conv = _tap_sum(glu_ext, conv_scr, cw_ref, v_ref[2:3, :], tm, lambda k: HALO - (CONV_K - 1) + k)
        conv_ref[...] = conv
        glu_ext[:, 0:HALO, :] = glu_ext[:, tm:tm + HALO, :]
        ch, _ = _layer_norm_stats(conv)
        cn = ch * v_ref[3:4, :] + v_ref[4:5, :]
        yb = cn * _sigmoid(cn)
        pa = ya * lax.rsqrt(_rowmean(ya * ya) + EPS) * v_ref[5:6, :]
        pb = yb * lax.rsqrt(_rowmean(yb * yb) + EPS) * v_ref[6:7, :]
        ycat = jnp.concatenate([pa, pb], axis=1).astype(BF16)
        ym = _dot(ycat, wmo_ref[...])
        ym_ref[...] = ym
        rm = lax.rsqrt(_rowmean(ym * ym) + EPS)
        xo_ref[...] = xv + gt * (ym * rm * g_ref[1:2, :])

    tile = pl.BlockSpec((tm, D), lambda i: (i, 0))
    return _call(
        core, name=name, grid=(nt,), jobs=jobs,
        in_specs=[tile, pl.BlockSpec((None, 8, D), lambda i: (i // tps, 0, 0)), _const_spec((8, D)),
                  _const_spec((NDEV, D, MB)), _const_spec((D, D)), _const_spec((8, WA)),
                  _const_spec((NHEAD, CHUNK, CHUNK)), _const_spec((CHUNK, WA)), _const_spec((32, WA))],
        out_specs=[tile, pl.BlockSpec((tm, 4 * WA), lambda i: (i, 0)), tile, pl.BlockSpec((tm, WA), lambda i: (i, 0))],
        out_shape=[jax.ShapeDtypeStruct((T, D), F32), jax.ShapeDtypeStruct((T, 4 * WA), F32),
                   jax.ShapeDtypeStruct((T, D), F32), jax.ShapeDtypeStruct((T, WA), F32)],
        scratch=[pltpu.VMEM((4, HALO + ext_rows, 128), F32), pltpu.VMEM((4, ext_rows, 128), F32)],
        args=[x, mod, gvec, w_mi, w_mo, v512, ws, bias_full, cw])


def _mixer_bwd_a(dxo, ym, proj, conv, mod, gvec, w_mo, v512, ws, bias_full, esel, tm, name, jobs=()):
    T = dxo.shape[0]
    nt = T // tm
    nb = mod.shape[0]
    tps = nt // nb

    def core(ins, outs, scs):
        dxo_ref, ym_ref, proj_ref, conv_ref, mod_ref, g_ref, wmo_ref, v_ref, ws_ref, bias_ref, e_ref = ins
        dpart_ref, dymb_ref, ycat_ref, mg_ref, vg_ref, v5g_ref, gws_ref, gbs_ref = outs
        (dbs_acc,) = scs
        i = pl.program_id(0)
        dxo_v = dxo_ref[...]
        ymv = ym_ref[...]
        gt = mod_ref[2:3, :]
        gpost = g_ref[1:2, :]
        rm = lax.rsqrt(_rowmean(ymv * ymv) + EPS)
        ymh = ymv * rm
        d_gt = _colsum(dxo_v * (ymh * gpost))
        dpm = gt * dxo_v
        d_gpost = _colsum(dpm * ymh)
        dymh = dpm * gpost
        dym = (rm * (dymh - ymh * _rowmean(dymh * ymh))).astype(BF16)
        dymb_ref[...] = dym
        dycat = _dot_nt(dym, wmo_ref[...])
        u = proj_ref[:, 0:WA]
        v0 = proj_ref[:, WA:2 * WA]
        vh, rv = _layer_norm_stats(v0)
        vb = (vh * v_ref[0:1, :] + v_ref[1:2, :]).astype(BF16)
        wm = _masked_spatial(ws_ref)
        lane_head = lax.broadcasted_iota(jnp.int32, (CHUNK, WA), 1) >> 6
        zs = []
        for q in range(tm // CHUNK):
            zs.append(_spatial_gate(wm, vb[q * CHUNK:(q + 1) * CHUNK, :], lane_head) + bias_ref[...])
        z = jnp.concatenate(zs, axis=0)
        ya = u * z
        ra = lax.rsqrt(_rowmean(ya * ya) + EPS)
        yah = ya * ra
        ch, rc = _layer_norm_stats(conv_ref[...])
        cn = ch * v_ref[3:4, :] + v_ref[4:5, :]
        sg = _sigmoid(cn)
        yb = cn * sg
        rb = lax.rsqrt(_rowmean(yb * yb) + EPS)
        ybh = yb * rb
        ycat_ref[...] = jnp.concatenate([yah * v_ref[5:6, :], ybh * v_ref[6:7, :]], axis=1).astype(BF16)
        dpa = dycat[:, 0:WA]
        dpb = dycat[:, WA:2 * WA]
        d_goa = _colsum(dpa * yah)
        d_gob = _colsum(dpb * ybh)
        dyah = dpa * v_ref[5:6, :]
        dybh = dpb * v_ref[6:7, :]
        dya = ra * (dyah - yah * _rowmean(dyah * yah))
        dyb = rb * (dybh - ybh * _rowmean(dybh * ybh))
        dpart_ref[:, 0:WA] = dya * z
        dz = dya * u

        @pl.when(i == 0)
        def _():
            gws_ref[...] = jnp.zeros((NHEAD, CHUNK, CHUNK), F32)
            dbs_acc[...] = jnp.zeros((CHUNK, WA), F32)
            vg_ref[...] = jnp.zeros((8, D), F32)
            v5g_ref[...] = jnp.zeros((8, WA), F32)

        dvs = []
        for q in range(tm // CHUNK):
            dz_q = dz[q * CHUNK:(q + 1) * CHUNK, :]
            vb_q = vb[q * CHUNK:(q + 1) * CHUNK, :]
            dbs_acc[...] += dz_q
            dzb = dz_q.astype(BF16)
            dv_q = jnp.zeros((CHUNK, WA), F32)
            for hd in range(NHEAD):
                dv_q = jnp.where(lane_head == hd, _dot_tn(wm[hd], dzb), dv_q)
                dz_hd = jnp.where(lane_head == hd, dz_q, 0.0).astype(BF16)
                gws_ref[hd] += _dot_nt(dz_hd, vb_q)
            dvs.append(dv_q)
        dv = jnp.concatenate(dvs, axis=0)
        d_gng = _colsum(dv * vh)
        d_gnb = _colsum(dv)
        dvh = dv * v_ref[0:1, :]
        dpart_ref[:, WA:2 * WA] = rv * (dvh - _rowmean(dvh) - vh * _rowmean(dvh * vh))
        dcn = dyb * (sg * (1.0 + cn * (1.0 - sg)))
        d_cng = _colsum(dcn * ch)
        d_cnb = _colsum(dcn)
        dch = dcn * v_ref[3:4, :]
        dconv = rc * (dch - _rowmean(dch) - ch * _rowmean(dch * ch))
        dpart_ref[:, 2 * WA:3 * WA] = dconv
        dpart_ref[:, 3 * WA:4 * WA] = jnp.zeros((tm, WA), F32)
        d_cb = _colsum(dconv)

        @pl.when(i % tps == 0)
        def _():
            mg_ref[...] = jnp.zeros((8, D), F32)

        mg_ref[2:3, :] += d_gt
        vg_ref[1:2, :] += d_gpost
        v5g_ref[0:1, :] += d_gng
        v5g_ref[1:2, :] += d_gnb
        v5g_ref[2:3, :] += d_cb
        v5g_ref[3:4, :] += d_cng
        v5g_ref[4:5, :] += d_cnb
        v5g_ref[5:6, :] += d_goa
        v5g_ref[6:7, :] += d_gob

        @pl.when(i == nt - 1)
        def _():
            row = lax.broadcasted_iota(jnp.int32, (CHUNK, CHUNK), 0)
            col = lax.broadcasted_iota(jnp.int32, (CHUNK, CHUNK), 1)
            for hd in range(NHEAD):
                gws_ref[hd] = jnp.where(col <= row, gws_ref[hd], 0.0)
            gbs_ref[...] = lax.dot_general(e_ref[...], dbs_acc[...], (((1,), (1,)), ((), ())),
                                           precision=lax.Precision.HIGHEST, preferred_element_type=F32)

    tile = pl.BlockSpec((tm, D), lambda i: (i, 0))
    ptile = pl.BlockSpec((tm, 4 * WA), lambda i: (i, 0))
    return _call(
        core, name=name, grid=(nt,), jobs=jobs,
        in_specs=[tile, tile, pl.BlockSpec((tm, 2 * WA), lambda i: (i, 0)), pl.BlockSpec((tm, WA), lambda i: (i, 0)),
                  pl.BlockSpec((None, 8, D), lambda i: (i // tps, 0, 0)), _const_spec((8, D)), _const_spec((D, D)),
                  _const_spec((8, WA)), _const_spec((NHEAD, CHUNK, CHUNK)), _const_spec((CHUNK, WA)),
                  _const_spec((8, WA))],
        out_specs=[ptile, tile, tile, pl.BlockSpec((None, 8, D), lambda i: (i // tps, 0, 0)),
                   pl.BlockSpec((8, D), lambda i: (0, 0)), pl.BlockSpec((8, WA), lambda i: (0, 0)),
                   pl.BlockSpec((NHEAD, CHUNK, CHUNK), lambda i: (0, 0, 0)), pl.BlockSpec((8, CHUNK), lambda i: (0, 0))],
        out_shape=[jax.ShapeDtypeStruct((T, 4 * WA), F32), jax.ShapeDtypeStruct((T, D), BF16),
                   jax.ShapeDtypeStruct((T, D), BF16), jax.ShapeDtypeStruct((nb, 8, D), F32),
                   jax.ShapeDtypeStruct((8, D), F32), jax.ShapeDtypeStruct((8, WA), F32),
                   jax.ShapeDtypeStruct((NHEAD, CHUNK, CHUNK), F32), jax.ShapeDtypeStruct((8, CHUNK), F32)],
        scratch=[pltpu.VMEM((CHUNK, WA), F32)],
        args=[dxo, ym, proj, conv, mod, gvec, w_mo, v512, ws, bias_full, esel])


def _mixer_bwd_b(dxo, x, dpart, proj, mod, gvec, w_mi, cw, tm, name, jobs=()):
    T = x.shape[0]
    nt = T // tm
    nb = mod.shape[0]
    tps = nt // nb
    hpt = tm // HALO
    nh = T // HALO
    off = HALO - (CONV_K - 1)
    p = _pitch(tm)
    ext_rows = 8 * p

    def core(ins, outs, scs):
        dxo_ref, x_ref, dpart_ref, dnext_ref, ag_ref, halo_ref, mod_ref, g_ref, wmi_ref, cw_ref = ins
        dx_ref, dproj_ref, hb_ref, mg_ref, vg_ref, dcw_ref = outs
        glu_ext, dconv_ext, dglu_scr, dcw_acc = scs
        i = pl.program_id(0)
        first = i % tps == 0
        last = i % tps == tps - 1
        a = ag_ref[:, 0:WA]
        g = ag_ref[:, WA:2 * WA]
        sgg = _sigmoid(g)

        @pl.when(i == 0)
        def _():
            glu_ext[:, HALO + tm:HALO + ext_rows, :] = jnp.zeros((4, ext_rows - tm, 128), F32)
            dconv_ext[:, HALO + tm:HALO + ext_rows, :] = jnp.zeros((4, ext_rows - tm, 128), F32)
            dcw_acc[...] = jnp.zeros((32, 8, WA), F32)
            vg_ref[...] = jnp.zeros((8, D), F32)

        _to_slabs(glu_ext, 0, jnp.where(first, 0.0, halo_ref[:, 0:WA] * _sigmoid(halo_ref[:, WA:2 * WA])))
        _to_slabs(glu_ext, HALO, a * sgg)
        _to_slabs(dconv_ext, 0, dpart_ref[:, 2 * WA:3 * WA])
        _to_slabs(dconv_ext, tm, jnp.where(last, 0.0, dnext_ref[...]))
        sub = lax.broadcasted_iota(jnp.int32, (8, 128), 0)
        for s in range(4):
            accs = [jnp.zeros((8, 128), F32)] * CONV_K
            for v in range(p):
                dc = jnp.where(v + p * sub < tm, dconv_ext[s, pl.ds(v, 8, stride=p), :], 0.0)
                for k in range(CONV_K):
                    accs[k] = accs[k] + dc * glu_ext[s, pl.ds(v + off + k, 8, stride=p), :]
            for k in range(CONV_K):
                dcw_acc[k, :, _lanes(s)] += accs[k]
        dglu = _tap_sum(dconv_ext, dglu_scr, cw_ref, jnp.zeros((1, WA), F32), tm, lambda k: (CONV_K - 1) - k)

        @pl.when(i == nt - 1)
        def _():
            for k in range(CONV_K):
                dcw_ref[k:k + 1, :] = jnp.sum(dcw_acc[k], axis=0, keepdims=True)
            dcw_ref[CONV_K:32, :] = jnp.zeros((32 - CONV_K, WA), F32)

        da = dglu * sgg
        dgg = dglu * a * (sgg * (1.0 - sgg))
        dproj_ref[:, 0:2 * WA] = dpart_ref[:, 0:2 * WA].astype(BF16)
        dproj_ref[:, 2 * WA:3 * WA] = da.astype(BF16)
        dproj_ref[:, 3 * WA:4 * WA] = dgg.astype(BF16)
        dh = jnp.zeros((tm, D), F32)
        for j in range(NDEV):
            dh = dh + _dot_nt(dproj_ref[:, j * MB:(j + 1) * MB], wmi_ref[j])
        xv = x_ref[...]
        sc, sh = mod_ref[1:2, :], mod_ref[0:1, :]
        gpre = g_ref[0:1, :]
        r = lax.rsqrt(_rowmean(xv * xv) + EPS)
        xh = xv * r
        n = xh * gpre
        hb_ref[...] = (n * (1.0 + sc) + sh).astype(BF16)
        d_sc = _colsum(dh * n)
        d_sh = _colsum(dh)
        dn = dh * (1.0 + sc)
        d_gpre = _colsum(dn * xh)
        dxh = dn * gpre
        dx_ref[...] = dxo_ref[...] + r * (dxh - xh * _rowmean(dxh * xh))

        @pl.when(first)
        def _():
            mg_ref[...] = jnp.zeros((8, D), F32)

        mg_ref[0:1, :] += d_sh
        mg_ref[1:2, :] += d_sc
        vg_ref[0:1, :] += d_gpre

    tile = pl.BlockSpec((tm, D), lambda i: (i, 0))
    return _call(
        core, name=name, grid=(nt,), jobs=jobs,
        in_specs=[tile, tile, pl.BlockSpec((tm, 4 * WA), lambda i: (i, 0)),
                  pl.BlockSpec((HALO, WA), lambda i: (jnp.minimum((i + 1) * hpt, nh - 1), 2)),
                  pl.BlockSpec((tm, 2 * WA), lambda i: (i, 1)),
                  pl.BlockSpec((HALO, 2 * WA), lambda i: (jnp.maximum(i * hpt - 1, 0), 1)),
                  pl.BlockSpec((None, 8, D), lambda i: (i // tps, 0, 0)), _const_spec((8, D)),
                  _const_spec((NDEV, D, MB)), _const_spec((32, WA))],
        out_specs=[tile, pl.BlockSpec((tm, 4 * WA), lambda i: (i, 0)), tile,
                   pl.BlockSpec((None, 8, D), lambda i: (i // tps, 0, 0)), pl.BlockSpec((8, D), lambda i: (0, 0)),
                   pl.BlockSpec((32, WA), lambda i: (0, 0))],
        out_shape=[jax.ShapeDtypeStruct((T, D), F32), jax.ShapeDtypeStruct((T, 4 * WA), BF16),
                   jax.ShapeDtypeStruct((T, D), BF16), jax.ShapeDtypeStruct((nb, 8, D), F32),
                   jax.ShapeDtypeStruct((8, D), F32), jax.ShapeDtypeStruct((32, WA), F32)],
        scratch=[pltpu.VMEM((4, HALO + ext_rows, 128), F32), pltpu.VMEM((4, HALO + ext_rows, 128), F32),
                 pltpu.VMEM((4, ext_rows, 128), F32), pltpu.VMEM((32, 8, WA), F32)],
        args=[dxo, x, dpart, dpart, proj, proj, mod, gvec, w_mi, cw])


def _loss_head(y, target, tm, name):
    T = y.shape[0]

    def core(ins, outs, _):
        y_ref, t_ref = ins
        dy_ref, loss_ref = outs

        @pl.when(pl.program_id(0) == 0)
        def _():
            loss_ref[...] = jnp.zeros((8, D), F32)

        err = y_ref[...] - t_ref[...]
        dy_ref[...] = err * (1.0 / D)
        part = jnp.sum(_rowmean(err * err), axis=0, keepdims=True)
        loss_ref[...] += HALF * part

    tile = pl.BlockSpec((tm, D), lambda i: (i, 0))
    return _call(
        core, name=name, grid=(T // tm,),
        in_specs=[tile, tile], out_specs=[tile, pl.BlockSpec((8, D), lambda i: (0, 0))],
        out_shape=[jax.ShapeDtypeStruct((T, D), F32), jax.ShapeDtypeStruct((8, D), F32)],
        args=[y, target])[0]


def _matmul_tn(a, b, a_spec, b_spec, o_spec, out_shape, acc_shape, grid, name, jobs=()):
    nk = grid[1]

    def core_one(ins, outs, _):
        outs[0][...] = _dot_tn(ins[0][...], ins[1][...]).astype(BF16)

    if nk == 1:
        return _call(core_one, name=name, grid=grid, jobs=jobs, in_specs=[a_spec, b_spec], out_specs=[o_spec],
                     out_shape=[jax.ShapeDtypeStruct(out_shape, BF16)], args=[a, b])

    def core(ins, outs, scs):
        a_ref, b_ref = ins
        (o_ref,) = outs
        (acc,) = scs
        k = pl.program_id(1)

        @pl.when(k == 0)
        def _():
            acc[...] = jnp.zeros(acc_shape, F32)

        acc[...] += _dot_tn(a_ref[...], b_ref[...])

        @pl.when(k == nk - 1)
        def _():
            o_ref[...] = acc[...].astype(BF16)

    return _call(
        core, name=name, grid=grid, jobs=jobs, in_specs=[a_spec, b_spec], out_specs=[o_spec],
        out_shape=[jax.ShapeDtypeStruct(out_shape, BF16)], scratch=[pltpu.VMEM(acc_shape, F32)], args=[a, b])


def _grad_rows(a3, b, tk, name, jobs=()):
    n, T, _ = a3.shape
    return _matmul_tn(
        a3, b,
        pl.BlockSpec((None, tk, FBP), lambda j, k: (j, k, 0)),
        pl.BlockSpec((tk, D), lambda j, k: (k, 0)),
        pl.BlockSpec((None, FBP, D), lambda j, k: (j, 0, 0)),
        (n, FBP, D), (FBP, D), (n, T // tk), name, jobs)


def _grad_w_mi(hb, dproj, tk, name):
    T = hb.shape[0]
    return _matmul_tn(
        hb, dproj,
        pl.BlockSpec((tk, D), lambda j, k: (k, 0)),
        pl.BlockSpec((tk, MB), lambda j, k: (k, j)),
        pl.BlockSpec((None, D, MB), lambda j, k: (j, 0, 0)),
        (8, D, MB), (D, MB), (8, T // tk), name)


def _grad_w_mo(ycat, dym, tk, name):
    T = ycat.shape[0]
    return _matmul_tn(
        ycat, dym,
        pl.BlockSpec((tk, 256), lambda j, k: (k, j)),
        pl.BlockSpec((tk, D), lambda j, k: (k, 0)),
        pl.BlockSpec((256, D), lambda j, k: (j, 0)),
        (D, D), (256, D), (4, T // tk), name)


def _adamw_math(w, g, m, v):
    m2 = ADAM_B1 * m + (1.0 - ADAM_B1) * g
    v2 = ADAM_B2 * v + (1.0 - ADAM_B2) * (g * g)
    m_hat = m2 / (1.0 - ADAM_B1 ** ADAM_STEP)
    v_hat = v2 / (1.0 - ADAM_B2 ** ADAM_STEP)
    delta = -ADAM_LR * (m_hat / (jnp.sqrt(v_hat) + ADAM_EPS) + ADAM_WD * w)
    return delta, m2, v2


def _adamw_reduce(parts, w, m, v, tr, name):
    R, C = w.shape

    def core(ins, outs, _):
        p_ref, w_ref, m_ref, v_ref = ins
        g_ref, d_ref, m2_ref, v2_ref = outs
        g = p_ref[0].astype(F32)
        for s in range(1, NDEV):
            g = g + p_ref[s].astype(F32)
        g_ref[...] = g
        d_ref[...], m2_ref[...], v2_ref[...] = _adamw_math(w_ref[...], g, m_ref[...], v_ref[...])

    blk = pl.BlockSpec((tr, C), lambda i: (i, 0))
    return _call(
        core, name=name, grid=(R // tr,),
        in_specs=[pl.BlockSpec((NDEV, tr, C), lambda i: (0, i, 0)), blk, blk, blk],
        out_specs=[blk, blk, blk, blk], out_shape=[jax.ShapeDtypeStruct((R, C), F32)] * 4,
        args=[parts, w, m, v])[0]


def _adamw_ada(sc_all, dd, w, m, v, tr, name):
    R, C = w.shape

    def core(ins, outs, _):
        sc_ref, dd_ref, w_ref, m_ref, v_ref = ins
        g_ref, d_ref, m2_ref, v2_ref = outs
        g = _dot_tn(sc_ref[...].astype(BF16), dd_ref[...].astype(BF16))
        g_ref[...] = g
        d_ref[...], m2_ref[...], v2_ref[...] = _adamw_math(w_ref[...], g, m_ref[...], v_ref[...])

    blk = pl.BlockSpec((tr, C), lambda i: (i, 0))
    return _call(
        core, name=name, grid=(R // tr,),
        in_specs=[pl.BlockSpec((64, tr), lambda i: (0, i)), pl.BlockSpec((64, C), lambda i: (0, 0)), blk, blk, blk],
        out_specs=[blk, blk, blk, blk], out_shape=[jax.ShapeDtypeStruct((R, C), F32)] * 4,
        args=[sc_all, dd, w, m, v])[0]


def _adamw_small(grads, wmv, name):
    nw = len(grads)
    srcs = []
    for arr, _ in grads:
        if not any(arr is s for s in srcs):
            srcs.append(arr)
    src_of = [[arr is s for s in srcs].index(True) for arr, _ in grads]

    def core(ins, outs, _):
        s_refs = ins[:len(srcs)]
        w_refs = ins[len(srcs):]
        for t in range(nw):
            row = grads[t][1]
            g = s_refs[src_of[t]][...] if row is None else s_refs[src_of[t]][row:row + 1, :]
            w_ref, m_ref, v_ref = w_refs[3 * t:3 * t + 3]
            g_ref, d_ref, m2_ref, v2_ref = outs[4 * t:4 * t + 4]
            g_ref[...] = g
            d_ref[...], m2_ref[...], v2_ref[...] = _adamw_math(w_ref[...], g, m_ref[...], v_ref[...])

    out_shape = []
    for t in range(nw):
        out_shape += [jax.ShapeDtypeStruct(wmv[3 * t].shape, F32)] * 4
    return _call(
        core, name=name, grid=(), in_specs=[VM] * (len(srcs) + 3 * nw), out_specs=[VM] * (4 * nw),
        out_shape=out_shape, args=srcs + list(wmv))[0]


def _ada_fwd(c_pad, w_ada, b_cols, cw_pad):
    def body(c_ref, w_ref, b_ref, cwp_ref, ada_ref, sc_ref, cw_ref, cbuf, send_buf, ssem, rsem):
        me = _me()
        mi = _lin(me)
        cbuf[mi] = c_ref[...]
        cw_ref[mi] = cwp_ref[...]
        peers = [_flip(me, f) for f in FLIPS]
        first = []
        for k, p in enumerate(peers):
            first.append(_remote(cbuf.at[mi], cbuf.at[mi], ssem.at[k], rsem.at[k], p))
            first.append(_remote(cw_ref.at[mi], cw_ref.at[mi], ssem.at[7 + k], rsem.at[7 + k], p))
        for cp in first:
            cp.start()
        for k, p in enumerate(peers):
            pi = _lin(p)
            _remote(cbuf.at[pi], cbuf.at[pi], ssem.at[k], rsem.at[k], p).wait_recv()
            _remote(cw_ref.at[pi], cw_ref.at[pi], ssem.at[7 + k], rsem.at[7 + k], p).wait_recv()
        c_all = cbuf[...].reshape(8 * 8, D)
        sc = c_all * _sigmoid(c_all)
        sc_ref[...] = sc
        res = _dot(sc.astype(BF16), w_ref[...].astype(BF16)) + b_ref[...]
        send_buf[...] = res.reshape(8, 8, ADA_B)
        ada_ref[mi] = send_buf[mi]
        second = []
        for k, p in enumerate(peers):
            second.append(_remote(send_buf.at[_lin(p)], ada_ref.at[mi], ssem.at[14 + k], rsem.at[14 + k], p))
        for cp in second:
            cp.start()
        for k, p in enumerate(peers):
            _remote(send_buf.at[mi], ada_ref.at[_lin(p)], ssem.at[14 + k], rsem.at[14 + k], p).wait_recv()
        for cp in first + second:
            cp.wait_send()

    return pl.pallas_call(
        body,
        name="ada_fwd",
        in_specs=[VM, VM, VM, VM],
        out_specs=[VM, VM, VM],
        out_shape=[
            jax.ShapeDtypeStruct((8, 8, ADA_B), F32),
            jax.ShapeDtypeStruct((64, D), F32),
            jax.ShapeDtypeStruct((8, 32, 64), F32),
        ],
        scratch_shapes=[
            pltpu.VMEM((8, 8, D), F32),
            pltpu.VMEM((8, 8, ADA_B), F32),
            pltpu.SemaphoreType.DMA((21,)),
            pltpu.SemaphoreType.DMA((21,)),
        ],
        compiler_params=pltpu.CompilerParams(vmem_limit_bytes=VMEM_LIMIT),
    )(c_pad, w_ada, b_cols, cw_pad)


def _ada_bwd(dada):
    def body(d_ref, dd_ref, gb_ref, rbuf, ssem, rsem):
        me = _me()
        mi = _lin(me)
        peers = [_flip(me, f) for f in FLIPS]
        rbuf[mi] = d_ref[mi]
        first = []
        for k, p in enumerate(peers):
            first.append(_remote(d_ref.at[_lin(p)], rbuf.at[mi], ssem.at[k], rsem.at[k], p))
        for cp in first:
            cp.start()
        for k, p in enumerate(peers):
            _remote(d_ref.at[mi], rbuf.at[_lin(p)], ssem.at[k], rsem.at[k], p).wait_recv()
        dd = rbuf[...].reshape(64, ADA_B)
        dd_ref[...] = dd
        gb_ref[mi] = jnp.broadcast_to(_colsum(dd), (8, ADA_B))
        second = []
        for k, p in enumerate(peers):
            second.append(_remote(gb_ref.at[mi], gb_ref.at[mi], ssem.at[7 + k], rsem.at[7 + k], p))
        for cp in second:
            cp.start()
        for k, p in enumerate(peers):
            pi = _lin(p)
            _remote(gb_ref.at[pi], gb_ref.at[pi], ssem.at[7 + k], rsem.at[7 + k], p).wait_recv()
        for cp in first + second:
            cp.wait_send()

    return pl.pallas_call(
        body,
        name="ada_bwd",
        in_specs=[VM],
        out_specs=[VM, VM],
        out_shape=[jax.ShapeDtypeStruct((64, ADA_B), F32), jax.ShapeDtypeStruct((8, 8, ADA_B), F32)],
        scratch_shapes=[
            pltpu.VMEM((8, 8, ADA_B), F32),
            pltpu.SemaphoreType.DMA((14,)),
            pltpu.SemaphoreType.DMA((14,)),
        ],
        compiler_params=pltpu.CompilerParams(vmem_limit_bytes=VMEM_LIMIT),
    )(dada)


def _tail_exchange(p_small, job):
    shapes = [p.shape for p in p_small]

    def body(*refs):
        p_refs = refs[0:4]
        j_ins = refs[4:4 + len(job.ins)]
        o = 4 + len(job.ins)
        s_refs = refs[o:o + 4]
        j_outs = refs[o + 4:o + 4 + len(job.out_shape)]
        o = o + 4 + len(job.out_shape)
        bufs = refs[o:o + 4]
        ssem, rsem = refs[o + 4:o + 6]
        j_sems = refs[o + 6:]
        me = _me()
        mi = _lin(me)
        job.start(j_ins, j_outs, j_sems)
        sent = []
        for a in range(4):
            bufs[a][mi] = p_refs[a][...]
        for k, f in enumerate(FLIPS):
            p = _flip(me, f)
            for a in range(4):
                cp = _remote(bufs[a].at[mi], bufs[a].at[mi], ssem.at[7 * a + k], rsem.at[7 * a + k], p)
                cp.start()
                sent.append(cp)
        for k, f in enumerate(FLIPS):
            p = _flip(me, f)
            pi = _lin(p)
            for a in range(4):
                _remote(bufs[a].at[pi], bufs[a].at[pi], ssem.at[7 * a + k], rsem.at[7 * a + k], p).wait_recv()
        for cp in sent:
            cp.wait_send()
        for a in range(4):
            s = bufs[a][0]
            for dev in range(1, NDEV):
                s = s + bufs[a][dev]
            s_refs[a][...] = s
        job.end(j_ins, j_outs, j_sems)

    res = pl.pallas_call(
        body,
        name="tail_exchange",
        in_specs=[VM] * 4 + [HBM] * len(job.ins),
        out_specs=[VM] * 4 + [HBM] * len(job.out_shape),
        out_shape=[jax.ShapeDtypeStruct(s, F32) for s in shapes] + job.out_shape,
        scratch_shapes=[pltpu.VMEM((NDEV,) + s, F32) for s in shapes]
        + [pltpu.SemaphoreType.DMA((28,)), pltpu.SemaphoreType.DMA((28,))] + job.sems,
        compiler_params=pltpu.CompilerParams(vmem_limit_bytes=VMEM_LIMIT),
    )(*p_small, *job.ins)
    return list(res[:4]), list(res[4:])


SMALL_D = ("g_pre_f1", "g_post_f1", "g_pre_m", "g_post_m", "g_pre_f2", "g_post_f2")
SMALL_W = ("gmlp_norm_g", "gmlp_norm_b", "conv_b", "conv_norm_g", "conv_norm_b", "g_out_a", "g_out_b")


def kernel(x, c, w_ada, b_ada, g_pre_f1, g_post_f1, w_f1_in, w_f1_out, g_pre_m, g_post_m, w_mix_in, gmlp_norm_g, gmlp_norm_b, w_spatial, b_spatial, conv_w, conv_b, conv_norm_g, conv_norm_b, g_out_a, g_out_b, w_mix_out, g_pre_f2, g_post_f2, w_f2_in, w_f2_out, loss_target, m_w_ada, m_b_ada, m_g_pre_f1, m_g_post_f1, m_w_f1_in, m_w_f1_out, m_g_pre_m, m_g_post_m, m_w_mix_in, m_gmlp_norm_g, m_gmlp_norm_b, m_w_spatial, m_b_spatial, m_conv_w, m_conv_b, m_conv_norm_g, m_conv_norm_b, m_g_out_a, m_g_out_b, m_w_mix_out, m_g_pre_f2, m_g_post_f2, m_w_f2_in, m_w_f2_out, v_w_ada, v_b_ada, v_g_pre_f1, v_g_post_f1, v_w_f1_in, v_w_f1_out, v_g_pre_m, v_g_post_m, v_w_mix_in, v_gmlp_norm_g, v_gmlp_norm_b, v_w_spatial, v_b_spatial, v_conv_w, v_conv_b, v_conv_norm_g, v_conv_norm_b, v_g_out_a, v_g_out_b, v_w_mix_out, v_g_pre_f2, v_g_post_f2, v_w_f2_in, v_w_f2_out):
    given = dict(locals())
    bl, seq, _ = x.shape
    T = bl * seq
    tm = min(256, seq // 2)
    tk = T
    mi = _lin((lax.axis_index("x"), lax.axis_index("y"), lax.axis_index("c")))

    c_pad = jnp.pad(c, ((0, 8 - bl), (0, 0)))
    b_cols = lax.dynamic_slice(b_ada, (0, mi * ADA_B), (1, ADA_B))
    cw_pad = jnp.pad(conv_w[0], ((0, 1), (0, 0)))
    ada_blk, sc_all, cw_all = _ada_fwd(c_pad, w_ada[0], b_cols, cw_pad)
    ada = ada_blk[:, 0:bl, :].transpose(1, 0, 2).reshape(bl, 9, D)
    pad5 = jnp.zeros((bl, 5, D), F32)
    mod1 = jnp.concatenate([ada[:, 0:3], pad5], axis=1)
    mod2 = jnp.concatenate([ada[:, 3:6], pad5], axis=1)
    mod3 = jnp.concatenate([ada[:, 6:9], pad5], axis=1)
    cw_full = cw_all.transpose(1, 0, 2).reshape(32, WA)

    def shard_in(w):
        return jnp.pad(w[0].T.astype(BF16), ((0, FBP - FB), (0, 0)))

    zpad = jnp.zeros((FBP - FB, D), BF16)
    g_f1 = _Gather([shard_in(w_f1_in), w_f1_out[0].astype(BF16)], ("rows", "out"), zpad)
    g_mx = _Gather([w_mix_in[0].astype(BF16), w_mix_out[0].astype(BF16), w_f2_out[0].astype(BF16)],
                   ("rows", "rows", "out"), zpad)
    g_f2 = _Gather([shard_in(w_f2_in)], ("rows",), zpad)
    (wi1, wo1), = _call(None, name="gather_f1", grid=(), in_specs=[], out_specs=[], out_shape=[], args=[], jobs=[g_f1])[1]

    zrow = jnp.zeros((1, D), F32)
    gv1 = jnp.concatenate([g_pre_f1, g_post_f1] + [zrow] * 6, axis=0)
    gvm = jnp.concatenate([g_pre_m, g_post_m] + [zrow] * 6, axis=0)
    gv2 = jnp.concatenate([g_pre_f2, g_post_f2] + [zrow] * 6, axis=0)
    v512 = jnp.concatenate([gmlp_norm_g, gmlp_norm_b, conv_b, conv_norm_g, conv_norm_b, g_out_a, g_out_b,
                            jnp.zeros((1, WA), F32)], axis=0)
    ws = w_spatial[0]
    bias_full = jnp.repeat(b_spatial[0].T, HD, axis=1)
    esel = (lax.broadcasted_iota(jnp.int32, (8, WA), 1) // HD == lax.broadcasted_iota(jnp.int32, (8, WA), 0)).astype(F32)

    x0 = x.reshape(T, D)
    (x1, gu1, y1), ((wmi, wmo, wo2),) = _ffn_fwd(x0, mod1, gv1, wi1, wo1, tm, "ffn1_fwd", jobs=[g_mx])
    wmo = wmo.reshape(D, D)
    (x2, proj, ym, conv), ((wi2,),) = _mixer_fwd(x1, mod2, gvm, wmi, wmo, v512, ws, bias_full, cw_full, tm, "mixer_fwd", jobs=[g_f2])
    (x3, gu2, y2), _ = _ffn_fwd(x2, mod3, gv2, wi2, wo2, tm, "ffn2_fwd")
    dx3, loss_blk = _loss_head(x3, loss_target.reshape(T, D), tm, "loss_head")

    (dx2, dg2, act2, hb2, dyb2, mg3, vg3), _ = _ffn_bwd(dx3, x2, y2, gu2, mod3, gv2, wi2, wo2, tm, "ffn2_bwd")
    (g_wi2,), _ = _grad_rows(dg2, hb2, tk, "ffn2_gw_in")
    (g_wo2,), _ = _grad_rows(act2, dyb2, tk, "ffn2_gw_out")
    (dpart, dymb, ycat, mg2a, vgma, v5g, gws, gbs), ((p_wi2,),) = _mixer_bwd_a(
        dx2, ym, proj, conv, mod2, gvm, wmo, v512, ws, bias_full, esel, tm, "mixer_bwd_a",
        jobs=[_Scatter([g_wi2], ("rows",))])
    (dx1, dproj, hbm, mg2b, vgmb, dcw), ((p_wo2,),) = _mixer_bwd_b(
        dx2, x1, dpart, proj, mod2, gvm, wmi, cw_full, tm, "mixer_bwd_b", jobs=[_Scatter([g_wo2], ("out",))])
    (g_wmi,), _ = _grad_w_mi(hbm, dproj, tk, "mixer_gw_in")
    (g_wmo,), _ = _grad_w_mo(ycat, dymb, tk, "mixer_gw_out")
    (dx0, dg1, act1, hb1, dyb1, mg1, vg1), ((p_wmi, p_wmo),) = _ffn_bwd(
        dx1, x0, y1, gu1, mod1, gv1, wi1, wo1, tm, "ffn1_bwd",
        jobs=[_Scatter([g_wmi, g_wmo.reshape(NDEV, MO, D)], ("rows", "rows"))])
    (g_wo1,), _ = _grad_rows(act1, dyb1, tk, "ffn1_gw_out")
    (g_wi1,), ((p_wo1,),) = _grad_rows(dg1, hb1, tk, "ffn1_gw_in", jobs=[_Scatter([g_wo1], ("out",))])

    dada = jnp.concatenate([mg1[:, 0:3], mg2b[:, 0:2], mg2a[:, 2:3], mg3[:, 0:3]], axis=1)
    dada = dada.reshape(bl, NDEV, ADA_B).transpose(1, 0, 2)
    dada = jnp.pad(dada, ((0, 0), (0, 8 - bl), (0, 0)))
    dd_all, gb_all = _ada_bwd(dada)
    g_bada = gb_all[:, 0, :].reshape(1, 9 * D)

    p1 = jnp.concatenate([vg1[0:2], vgmb[0:1], vgma[1:2], vg3[0:2], loss_blk[0:1], zrow], axis=0)
    p2 = jnp.concatenate([v5g, dcw], axis=0)
    (s1, s2, s3, s4), (p_wi1,) = _tail_exchange([p1, p2, gws, gbs], _Scatter([g_wi1], ("rows",)))
    loss = s1[6, 0]

    res = {}
    for nm, part in (("w_f1_in", p_wi1), ("w_f2_in", p_wi2)):
        quad = _adamw_reduce(part, given[nm][0].T, given["m_" + nm][0].T, given["v_" + nm][0].T, FO, "adamw_" + nm)
        res[nm] = tuple(t.T[None] for t in quad)
    for nm, part, tr in (("w_f1_out", p_wo1, FO), ("w_f2_out", p_wo2, FO), ("w_mix_in", p_wmi, 256), ("w_mix_out", p_wmo, MO)):
        quad = _adamw_reduce(part, given[nm][0], given["m_" + nm][0], given["v_" + nm][0], tr, "adamw_" + nm)
        res[nm] = tuple(t[None] for t in quad)
    quad = _adamw_ada(sc_all, dd_all, w_ada[0], m_w_ada[0], v_w_ada[0], 256, "adamw_w_ada")
    res["w_ada"] = tuple(t[None] for t in quad)

    small = SMALL_D + SMALL_W + ("w_spatial", "b_spatial", "b_ada", "conv_w")
    g_cw = lax.dynamic_slice(s2, (8, mi * 64), (32, 64))
    grads = [(s1, r) for r in range(6)] + [(s2, r) for r in range(7)] + [(s3, None), (s4, None), (g_bada, None), (g_cw, None)]
    wmv = []
    for nm in small:
        for pre in ("", "m_", "v_"):
            a = given[pre + nm]
            if nm in ("w_spatial", "b_spatial"):
                a = a[0]
            elif nm == "conv_w":
                a = jnp.pad(a[0], ((0, 1), (0, 0)), constant_values=1.0 if pre == "v_" else 0.0)
            wmv.append(a)
    outs = _adamw_small(grads, wmv, "adamw_small")
    for t, nm in enumerate(small):
        quad = outs[4 * t:4 * t + 4]
        if nm in ("w_spatial", "b_spatial"):
            quad = [q[None] for q in quad]
        elif nm == "conv_w":
            quad = [q[0:CONV_K][None] for q in quad]
        res[nm] = tuple(quad)

    order = ["w_ada", "b_ada", "g_pre_f1", "g_post_f1", "w_f1_in", "w_f1_out", "g_pre_m", "g_post_m", "w_mix_in",
             "gmlp_norm_g", "gmlp_norm_b", "w_spatial", "b_spatial", "conv_w", "conv_b", "conv_norm_g", "conv_norm_b",
             "g_out_a", "g_out_b", "w_mix_out", "g_pre_f2", "g_post_f2", "w_f2_in", "w_f2_out"]
    out = [loss, dx0.reshape(bl, seq, D)]
    for k in range(4):
        out += [res[nm][k] for nm in order]
    return tuple(out)
```

```python
import jax
import jax.numpy as jnp
from jax import lax
from jax.experimental import pallas as pl
from jax.experimental.pallas import tpu as pltpu

F32 = jnp.float32
BF16 = jnp.bfloat16

D = 1024
DFF = 2816
NDEV = 8
FB = 2 * DFF // NDEV
FBP = 768
FO = DFF // NDEV
WA = 512
NHEAD = 8
HD = 64
CHUNK = 128
CONV_K = 31
HALO = 32
MB = 2 * (WA + WA) // NDEV
MO = D // NDEV
ADA_B = 9 * D // NDEV
EPS = 1e-6
HALF = 0.5

ADAM_LR = 0.001
ADAM_B1 = 0.9
ADAM_B2 = 0.999
ADAM_EPS = 1e-08
ADAM_WD = 0.01
ADAM_STEP = 10

VMEM_LIMIT = 56 * 1024 * 1024
MESH = pl.DeviceIdType.MESH
FLIPS = ((0, 0, 1), (1, 0, 0), (0, 1, 0), (1, 1, 0), (1, 0, 1), (0, 1, 1), (1, 1, 1))
CHIP_FLIPS = ((1, 0, 0), (0, 1, 0), (1, 1, 0))
HBM = pl.BlockSpec(memory_space=pl.ANY)
VM = pl.BlockSpec(memory_space=pltpu.VMEM)


def _dot(a, b):
    return lax.dot_general(a, b, (((1,), (0,)), ((), ())), preferred_element_type=F32)


def _dot_nt(a, b):
    return lax.dot_general(a, b, (((1,), (1,)), ((), ())), preferred_element_type=F32)


def _dot_tn(a, b):
    return lax.dot_general(a, b, (((0,), (0,)), ((), ())), preferred_element_type=F32)


def _rowmean(v):
    return jnp.mean(v, axis=-1, keepdims=True)


def _colsum(v):
    return jnp.sum(v, axis=0, keepdims=True)


def _sigmoid(v):
    return 1.0 / (1.0 + jnp.exp(-v))


def _const_spec(shape):
    nd = len(shape)
    return pl.BlockSpec(shape, lambda *_: (0,) * nd, pipeline_mode=pl.Buffered(1))


def _me():
    return lax.axis_index("x"), lax.axis_index("y"), lax.axis_index("c")


def _flip(me, f):
    return tuple(1 - v if b else v for v, b in zip(me, f))


def _lin(p):
    return 4 * p[0] + 2 * p[1] + p[2]


def _remote(src, dst, send_sem, recv_sem, dev):
    return pltpu.make_async_remote_copy(src_ref=src, dst_ref=dst, send_sem=send_sem, recv_sem=recv_sem,
                                        device_id=dev, device_id_type=MESH)


def _blk(kind, ref, p):
    if kind == "out":
        return ref.at[2 * p[0] + p[1], pl.ds(p[2] * FO, FO), :]
    return ref.at[_lin(p)]


class _Gather:
    def __init__(self, shards, kinds, zpad, late_mid=False):
        self.late_mid = late_mid
        self.kinds = kinds
        self.n = len(shards)
        self.ins = list(shards) + [zpad]
        self.out_shape = [jax.ShapeDtypeStruct((4, FBP, D) if k == "out" else (NDEV,) + s.shape, BF16)
                          for s, k in zip(shards, kinds)]
        self.n_out = sum(k == "out" for k in kinds)
        self.sems = [pltpu.SemaphoreType.DMA((7 * self.n,)), pltpu.SemaphoreType.DMA((7 * self.n,)),
                     pltpu.SemaphoreType.DMA((self.n + 4 * max(self.n_out, 1),))]

    def _first(self, ins, outs, sems):
        ssem, rsem, lsem = sems
        me = _me()
        sib = _flip(me, (0, 0, 1))
        cps, loc = [], []
        nz = 0
        for a in range(self.n):
            mine = _blk(self.kinds[a], outs[a], me)
            loc.append(pltpu.make_async_copy(ins[a], mine, lsem.at[a]))
            if self.kinds[a] == "out":
                for q in range(4):
                    loc.append(pltpu.make_async_copy(ins[self.n], outs[a].at[q, pl.ds(FB, FBP - FB), :],
                                                     lsem.at[self.n + 4 * nz + q]))
                nz += 1
            cps.append(_remote(ins[a], mine, ssem.at[7 * a], rsem.at[7 * a], sib))
            for j, f in enumerate(CHIP_FLIPS):
                cps.append(_remote(ins[a], mine, ssem.at[7 * a + 1 + j], rsem.at[7 * a + 1 + j], _flip(me, f)))
        return cps, loc

    def _passed(self, outs, sems):
        ssem, rsem, _ = sems
        me = _me()
        sib = _flip(me, (0, 0, 1))
        cps = []
        for j, f in enumerate(CHIP_FLIPS):
            for a in range(self.n):
                blk = _blk(self.kinds[a], outs[a], _flip(me, f))
                cps.append(_remote(blk, blk, ssem.at[7 * a + 4 + j], rsem.at[7 * a + 4 + j], sib))
        return cps

    def start(self, ins, outs, sems):
        cps, loc = self._first(ins, outs, sems)
        for cp in loc + cps:
            cp.start()

    def mid(self, ins, outs, sems):
        ssem, rsem, _ = sems
        me = _me()
        passed = self._passed(outs, sems)
        t = 0
        for j, f in enumerate(CHIP_FLIPS):
            for a in range(self.n):
                blk = _blk(self.kinds[a], outs[a], _flip(me, f))
                _remote(blk, blk, ssem.at[7 * a + 1 + j], rsem.at[7 * a + 1 + j], _flip(me, f)).wait_recv()
                passed[t].start()
                t += 1

    def end(self, ins, outs, sems):
        ssem, rsem, _ = sems
        me = _me()
        sib = _flip(me, (0, 0, 1))
        for a in range(self.n):
            blk = _blk(self.kinds[a], outs[a], sib)
            _remote(blk, blk, ssem.at[7 * a], rsem.at[7 * a], sib).wait_recv()
            for j, f in enumerate(CHIP_FLIPS):
                blk = _blk(self.kinds[a], outs[a], _flip(_flip(me, f), (0, 0, 1)))
                _remote(blk, blk, ssem.at[7 * a + 4 + j], rsem.at[7 * a + 4 + j], sib).wait_recv()
        cps, loc = self._first(ins, outs, sems)
        for cp in cps + self._passed(outs, sems):
            cp.wait_send()
        for cp in loc:
            cp.wait()


class _ChipScatter:
    def __init__(self, grads):
        self.n = len(grads)
        self.ins = list(grads)
        self.out_shape = [jax.ShapeDtypeStruct(g.shape, BF16) for g in grads]
        self.sems = [pltpu.SemaphoreType.DMA((3 * self.n,)), pltpu.SemaphoreType.DMA((3 * self.n,)),
                     pltpu.SemaphoreType.DMA((self.n,))]

    def _copies(self, ins, outs, sems):
        ssem, rsem, lsem = sems
        me = _me()
        mq = 2 * me[0] + me[1]
        loc = [pltpu.make_async_copy(ins[a].at[mq], outs[a].at[mq], lsem.at[a]) for a in range(self.n)]
        cps = []
        for k, f in enumerate(CHIP_FLIPS):
            p = _flip(me, f)
            for a in range(self.n):
                cps.append(_remote(ins[a].at[2 * p[0] + p[1]], outs[a].at[mq], ssem.at[3 * a + k], rsem.at[3 * a + k], p))
        return cps, loc

    def start(self, ins, outs, sems):
        cps, loc = self._copies(ins, outs, sems)
        for cp in loc + cps:
            cp.start()

    mid = None

    def end(self, ins, outs, sems):
        ssem, rsem, _ = sems
        me = _me()
        mq = 2 * me[0] + me[1]
        for k, f in enumerate(CHIP_FLIPS):
            p = _flip(me, f)
            for a in range(self.n):
                _remote(ins[a].at[mq], outs[a].at[2 * p[0] + p[1]], ssem.at[3 * a + k], rsem.at[3 * a + k], p).wait_recv()
        cps, loc = self._copies(ins, outs, sems)
        for cp in cps:
            cp.wait_send()
        for cp in loc:
            cp.wait()


def _call(core, *, name, grid, in_specs, out_specs, out_shape, args, scratch=(), jobs=()):
    n_in, n_out, n_sc = len(in_specs), len(out_specs), len(scratch)
    steps = 1
    for g in grid:
        steps *= g

    def body(*refs):
        pos = [0]

        def take(k):
            r = refs[pos[0]:pos[0] + k]
            pos[0] += k
            return r

        ins = take(n_in)
        j_ins = [take(len(j.ins)) for j in jobs]
        outs = take(n_out)
        j_outs = [take(len(j.out_shape)) for j in jobs]
        scs = take(n_sc)
        j_sems = [take(len(j.sems)) for j in jobs]
        if len(grid) == 2:
            step = pl.program_id(0) * grid[1] + pl.program_id(1)
        elif len(grid) == 1:
            step = pl.program_id(0)
        else:
            step = 0
        for j, ji, jo, js in zip(jobs, j_ins, j_outs, j_sems):
            if grid:
                pl.when(step == 0)(lambda j=j, ji=ji, jo=jo, js=js: j.start(ji, jo, js))
            else:
                j.start(ji, jo, js)
        for j, ji, jo, js in zip(jobs, j_ins, j_outs, j_sems):
            if j.mid is not None:
                if grid:
                    at = steps - 1 if j.late_mid else (3 * steps) // 4
                    pl.when(step == at)(lambda j=j, ji=ji, jo=jo, js=js: j.mid(ji, jo, js))
                else:
                    j.mid(ji, jo, js)
        if core is not None:
            core(ins, outs, scs)
        for j, ji, jo, js in zip(jobs, j_ins, j_outs, j_sems):
            if grid:
                pl.when(step == steps - 1)(lambda j=j, ji=ji, jo=jo, js=js: j.end(ji, jo, js))
            else:
                j.end(ji, jo, js)

    all_in = list(in_specs)
    all_args = list(args)
    all_out = list(out_specs)
    all_shape = list(out_shape)
    all_sc = list(scratch)
    for j in jobs:
        all_in += [HBM] * len(j.ins)
        all_args += j.ins
    for j in jobs:
        all_out += [HBM] * len(j.out_shape)
        all_shape += j.out_shape
        all_sc += j.sems
    params = dict(vmem_limit_bytes=VMEM_LIMIT)
    if grid:
        params["dimension_semantics"] = ("arbitrary",) * len(grid)
    res = pl.pallas_call(
        body, name=name, grid=grid, in_specs=all_in, out_specs=all_out, out_shape=all_shape,
        scratch_shapes=all_sc, compiler_params=pltpu.CompilerParams(**params),
    )(*all_args)
    core_res = list(res[:n_out])
    job_res = []
    pos = n_out
    for j in jobs:
        job_res.append(list(res[pos:pos + len(j.out_shape)]))
        pos += len(j.out_shape)
    return core_res, job_res


def _ffn_fwd(x, mod, gvec, w_in, w_out, tm, name, jobs=()):
    T = x.shape[0]
    nt = T // tm
    tps = nt // mod.shape[0]

    def core(ins, outs, _):
        x_ref, mod_ref, g_ref, win_ref, wout_ref = ins
        xo_ref, gu_ref, y_ref = outs
        xv = x_ref[...]
        sh, sc, gt = mod_ref[0:1, :], mod_ref[1:2, :], mod_ref[2:3, :]
        r = lax.rsqrt(_rowmean(xv * xv) + EPS)
        h = (xv * r * g_ref[0:1, :]) * (1.0 + sc) + sh
        hb = h.astype(BF16)
        y = jnp.zeros((tm, D), F32)
        for cidx in range(4):
            gate = _dot_nt(hb, win_ref[cidx])
            up = _dot_nt(hb, win_ref[4 + cidx])
            gu_ref[cidx] = gate.astype(BF16)
            gu_ref[4 + cidx] = up.astype(BF16)
            act = gate * _sigmoid(gate) * up
            y = y + _dot(act.astype(BF16), wout_ref[cidx])
        y_ref[...] = y
        ry = lax.rsqrt(_rowmean(y * y) + EPS)
        xo_ref[...] = xv + (HALF * gt) * (y * ry * g_ref[1:2, :])

    tile = pl.BlockSpec((tm, D), lambda i: (i, 0))
    return _call(
        core, name=name, grid=(nt,), jobs=jobs,
        in_specs=[tile, pl.BlockSpec((None, 8, D), lambda i: (i // tps, 0, 0)), _const_spec((8, D)),
                  _const_spec((8, FBP, D)), _const_spec((4, FBP, D))],
        out_specs=[tile, pl.BlockSpec((8, tm, FBP), lambda i: (0, i, 0)), tile],
        out_shape=[jax.ShapeDtypeStruct((T, D), F32), jax.ShapeDtypeStruct((8, T, FBP), BF16),
                   jax.ShapeDtypeStruct((T, D), F32)],
        args=[x, mod, gvec, w_in, w_out])


def _ffn_bwd(dxo, x, y, gu, mod, gvec, w_in, w_out, tm, name, jobs=()):
    T = x.shape[0]
    nt = T // tm
    nb = mod.shape[0]
    tps = nt // nb

    def core(ins, outs, _):
        dxo_ref, x_ref, y_ref, gu_ref, mod_ref, g_ref, win_ref, wout_ref = ins
        dx_ref, dg_ref, act_ref, hb_ref, dyb_ref, mg_ref, vg_ref = outs
        i = pl.program_id(0)
        xv = x_ref[...]
        dxo_v = dxo_ref[...]
        yv = y_ref[...]
        sh, sc, gt = mod_ref[0:1, :], mod_ref[1:2, :], mod_ref[2:3, :]
        gpre, gpost = g_ref[0:1, :], g_ref[1:2, :]
        r = lax.rsqrt(_rowmean(xv * xv) + EPS)
        xh = xv * r
        n = xh * gpre
        hb = (n * (1.0 + sc) + sh).astype(BF16)
        hb_ref[...] = hb
        ry = lax.rsqrt(_rowmean(yv * yv) + EPS)
        yh = yv * ry
        d_gt = _colsum(HALF * dxo_v * (yh * gpost))
        dp = (HALF * gt) * dxo_v
        d_gpost = _colsum(dp * yh)
        dyh = dp * gpost
        dy = ry * (dyh - yh * _rowmean(dyh * yh))
        dyb = dy.astype(BF16)
        dyb_ref[...] = dyb
        dh = jnp.zeros((tm, D), F32)
        for cidx in range(4):
            gate = gu_ref[cidx].astype(F32)
            up = gu_ref[4 + cidx].astype(F32)
            sig = _sigmoid(gate)
            s = gate * sig
            act_ref[cidx] = (s * up).astype(BF16)
            d_act = _dot_nt(dyb, wout_ref[cidx])
            d_up = (d_act * s).astype(BF16)
            d_gate = (d_act * up * (sig * (1.0 + gate * (1.0 - sig)))).astype(BF16)
            dg_ref[cidx] = d_gate
            dg_ref[4 + cidx] = d_up
            dh = dh + _dot(d_gate, win_ref[cidx]) + _dot(d_up, win_ref[4 + cidx])
        d_sc = _colsum(dh * n)
        d_sh = _colsum(dh)
        dn = dh * (1.0 + sc)
        d_gpre = _colsum(dn * xh)
        dxh = dn * gpre
        dx_ref[...] = dxo_v + r * (dxh - xh * _rowmean(dxh * xh))

        @pl.when(i % tps == 0)
        def _():
            mg_ref[...] = jnp.zeros((8, D), F32)

        @pl.when(i == 0)
        def _():
            vg_ref[...] = jnp.zeros((8, D), F32)

        mg_ref[0:1, :] += d_sh
        mg_ref[1:2, :] += d_sc
        mg_ref[2:3, :] += d_gt
        vg_ref[0:1, :] += d_gpre
        vg_ref[1:2, :] += d_gpost

    tile = pl.BlockSpec((tm, D), lambda i: (i, 0))
    return _call(
        core, name=name, grid=(nt,), jobs=jobs,
        in_specs=[tile, tile, tile, pl.BlockSpec((8, tm, FBP), lambda i: (0, i, 0)),
                  pl.BlockSpec((None, 8, D), lambda i: (i // tps, 0, 0)), _const_spec((8, D)),
                  _const_spec((8, FBP, D)), _const_spec((4, FBP, D))],
        out_specs=[tile, pl.BlockSpec((8, tm, FBP), lambda i: (0, i, 0)),
                   pl.BlockSpec((4, tm, FBP), lambda i: (0, i, 0)), tile, tile,
                   pl.BlockSpec((None, 8, D), lambda i: (i // tps, 0, 0)), pl.BlockSpec((8, D), lambda i: (0, 0))],
        out_shape=[jax.ShapeDtypeStruct((T, D), F32), jax.ShapeDtypeStruct((8, T, FBP), BF16),
                   jax.ShapeDtypeStruct((4, T, FBP), BF16), jax.ShapeDtypeStruct((T, D), BF16),
                   jax.ShapeDtypeStruct((T, D), BF16), jax.ShapeDtypeStruct((nb, 8, D), F32),
                   jax.ShapeDtypeStruct((8, D), F32)],
        args=[dxo, x, y, gu, mod, gvec, w_in, w_out])


def _masked_spatial(ws_ref):
    row = lax.broadcasted_iota(jnp.int32, (CHUNK, CHUNK), 0)
    col = lax.broadcasted_iota(jnp.int32, (CHUNK, CHUNK), 1)
    keep = col <= row
    return [jnp.where(keep, ws_ref[hd], 0.0).astype(BF16) for hd in range(NHEAD)]


def _spatial_gate(wm, vb_chunk, lane_head):
    z = jnp.zeros((CHUNK, WA), F32)
    for hd in range(NHEAD):
        z = jnp.where(lane_head == hd, _dot(wm[hd], vb_chunk), z)
    return z


def _layer_norm_stats(v):
    mu = _rowmean(v)
    vc = v - mu
    rstd = lax.rsqrt(_rowmean(vc * vc) + EPS)
    return vc * rstd, rstd


def _pitch(tm):
    p = tm // 8
    while p % 8 != 4:
        p += 1
    return p


def _lanes(s):
    return slice(s * 128, (s + 1) * 128)


def _to_slabs(ref, row0, val):
    for s in range(4):
        ref[s, row0:row0 + val.shape[0], :] = val[:, _lanes(s)]


def _tap_sum(src, out, cw_ref, bias, tm, start):
    p = _pitch(tm)
    for s in range(4):
        accs = [jnp.broadcast_to(bias[:, _lanes(s)], (8, 128))] * p
        for k in range(CONV_K):
            w = jnp.broadcast_to(cw_ref[k:k + 1, _lanes(s)], (8, 128))
            for v in range(p):
                accs[v] = accs[v] + w * src[s, pl.ds(v + start(k), 8, stride=p), :]
        for v in range(p):
            out[s, pl.ds(v, 8, stride=p), :] = accs[v]
    return jnp.concatenate([out[s, 0:tm, :] for s in range(4)], axis=1)


def _mixer_fwd(x, mod, gvec, w_mi, w_mo, v512, ws, bias_full, cw, tm, name, jobs=()):
    T = x.shape[0]
    nt = T // tm
    tps = nt // mod.shape[0]
    ext_rows = 8 * _pitch(tm)

    def core(ins, outs, scs):
        x_ref, mod_ref, g_ref, wmi_ref, wmo_ref, v_ref, ws_ref, bias_ref, cw_ref = ins
        xo_ref, proj_ref, ym_ref, conv_ref = outs
        glu_ext, conv_scr = scs
        i = pl.program_id(0)
        xv = x_ref[...]
        sh, sc, gt = mod_ref[0:1, :], mod_ref[1:2, :], mod_ref[2:3, :]
        r = lax.rsqrt(_rowmean(xv * xv) + EPS)
        hb = ((xv * r * g_ref[0:1, :]) * (1.0 + sc) + sh).astype(BF16)
        for j in range(NDEV):
            proj_ref[:, j * MB:(j + 1) * MB] = _dot(hb, wmi_ref[j])
        u = proj_ref[:, 0:WA]
        v0 = proj_ref[:, WA:2 * WA]
        a = proj_ref[:, 2 * WA:3 * WA]
        g = proj_ref[:, 3 * WA:4 * WA]
        vh, _ = _layer_norm_stats(v0)
        vb = (vh * v_ref[0:1, :] + v_ref[1:2, :]).astype(BF16)
        wm = _masked_spatial(ws_ref)
        lane_head = lax.broadcasted_iota(jnp.int32, (CHUNK, WA), 1) >> 6
        ya = []
        for q in range(tm // CHUNK):
            z = _spatial_gate(wm, vb[q * CHUNK:(q + 1) * CHUNK, :], lane_head) + bias_ref[...]
            ya.append(u[q * CHUNK:(q + 1) * CHUNK, :] * z)
        ya = jnp.concatenate(ya, axis=0)
        glu = a * _sigmoid(g)

        @pl.when(i == 0)
        def _():
            glu_ext[:, HALO + tm:HALO + ext_rows, :] = jnp.zeros((4, ext_rows - tm, 128), F32)

        @pl.when(i % tps == 0)
        def _():
            glu_ext[:, 0:HALO, :] = jnp.zeros((4, HALO, 128), F32)

        _to_slabs(glu_ext, HALO, glu)
        conv = _tap_sum(glu_ext, conv_scr, cw_ref, v_ref[2:3, :], tm, lambda k: HALO - (CONV_K - 1) + k)
        conv_ref[...] = conv
        glu_ext[:, 0:HALO, :] = glu_ext[:, tm:tm + HALO, :]
        ch, _ = _layer_norm_stats(conv)
        cn = ch * v_ref[3:4, :] + v_ref[4:5, :]
        yb = cn * _sigmoid(cn)
        pa = ya * lax.rsqrt(_rowmean(ya * ya) + EPS) * v_ref[5:6, :]
        pb = yb * lax.rsqrt(_rowmean(yb * yb) + EPS) * v_ref[6:7, :]
        ycat = jnp.concatenate([pa, pb], axis=1).astype(BF16)
        ym = _dot(ycat, wmo_ref[...])
        ym_ref[...] = ym
        rm = lax.rsqrt(_rowmean(ym * ym) + EPS)
        xo_ref[...] = xv + gt * (ym * rm * g_ref[1:2, :])

    tile = pl.BlockSpec((tm, D), lambda i: (i, 0))
    return _call(
        core, name=name, grid=(nt,), jobs=jobs,
        in_specs=[tile, pl.BlockSpec((None, 8, D), lambda i: (i // tps, 0, 0)), _const_spec((8, D)),
                  _const_spec((NDEV, D, MB)), _const_spec((D, D)), _const_spec((8, WA)),
                  _const_spec((NHEAD, CHUNK, CHUNK)), _const_spec((CHUNK, WA)), _const_spec((32, WA))],
        out_specs=[tile, pl.BlockSpec((tm, 4 * WA), lambda i: (i, 0)), tile, pl.BlockSpec((tm, WA), lambda i: (i, 0))],
        out_shape=[jax.ShapeDtypeStruct((T, D), F32), jax.ShapeDtypeStruct((T, 4 * WA), F32),
                   jax.ShapeDtypeStruct((T, D), F32), jax.ShapeDtypeStruct((T, WA), F32)],
        scratch=[pltpu.VMEM((4, HALO + ext_rows, 128), F32), pltpu.VMEM((4, ext_rows, 128), F32)],
        args=[x, mod, gvec, w_mi, w_mo, v512, ws, bias_full, cw])


def _mixer_bwd_a(dxo, ym, proj, conv, mod, gvec, w_mo, v512, ws, bias_full, esel, tm, name, jobs=()):
    T = dxo.shape[0]
    nt = T // tm
    nb = mod.shape[0]
    tps = nt // nb

    def core(ins, outs, scs):
        dxo_ref, ym_ref, proj_ref, conv_ref, mod_ref, g_ref, wmo_ref, v_ref, ws_ref, bias_ref, e_ref = ins
        dpart_ref, dymb_ref, ycat_ref, mg_ref, vg_ref, v5g_ref, gws_ref, gbs_ref = outs
        (dbs_acc,) = scs
        i = pl.program_id(0)
        dxo_v = dxo_ref[...]
        ymv = ym_ref[...]
        gt = mod_ref[2:3, :]
        gpost = g_ref[1:2, :]
        rm = lax.rsqrt(_rowmean(ymv * ymv) + EPS)
        ymh = ymv * rm
        d_gt = _colsum(dxo_v * (ymh * gpost))
        dpm = gt * dxo_v
        d_gpost = _colsum(dpm * ymh)
        dymh = dpm * gpost
        dym = (rm * (dymh - ymh * _rowmean(dymh * ymh))).astype(BF16)
        dymb_ref[...] = dym
        dycat = _dot_nt(dym, wmo_ref[...])
        u = proj_ref[:, 0:WA]
        v0 = proj_ref[:, WA:2 * WA]
        vh, rv = _layer_norm_stats(v0)
        vb = (vh * v_ref[0:1, :] + v_ref[1:2, :]).astype(BF16)
        wm = _masked_spatial(ws_ref)
        lane_head = lax.broadcasted_iota(jnp.int32, (CHUNK, WA), 1) >> 6
        zs = []
        for q in range(tm // CHUNK):
            zs.append(_spatial_gate(wm, vb[q * CHUNK:(q + 1) * CHUNK, :], lane_head) + bias_ref[...])
        z = jnp.concatenate(zs, axis=0)
        ya = u * z
        ra = lax.rsqrt(_rowmean(ya * ya) + EPS)
        yah = ya * ra
        ch, rc = _layer_norm_stats(conv_ref[...])
        cn = ch * v_ref[3:4, :] + v_ref[4:5, :]
        sg = _sigmoid(cn)
        yb = cn * sg
        rb = lax.rsqrt(_rowmean(yb * yb) + EPS)
        ybh = yb * rb
        ycat_ref[...] = jnp.concatenate([yah * v_ref[5:6, :], ybh * v_ref[6:7, :]], axis=1).astype(BF16)
        dpa = dycat[:, 0:WA]
        dpb = dycat[:, WA:2 * WA]
        d_goa = _colsum(dpa * yah)
        d_gob = _colsum(dpb * ybh)
        dyah = dpa * v_ref[5:6, :]
        dybh = dpb * v_ref[6:7, :]
        dya = ra * (dyah - yah * _rowmean(dyah * yah))
        dyb = rb * (dybh - ybh * _rowmean(dybh * ybh))
        dpart_ref[:, 0:WA] = dya * z
        dz = dya * u

        @pl.when(i == 0)
        def _():
            gws_ref[...] = jnp.zeros((NHEAD, CHUNK, CHUNK), F32)
            dbs_acc[...] = jnp.zeros((CHUNK, WA), F32)
            vg_ref[...] = jnp.zeros((8, D), F32)
            v5g_ref[...] = jnp.zeros((8, WA), F32)

        dvs = []
        for q in range(tm // CHUNK):
            dz_q = dz[q * CHUNK:(q + 1) * CHUNK, :]
            vb_q = vb[q * CHUNK:(q + 1) * CHUNK, :]
            dbs_acc[...] += dz_q
            dzb = dz_q.astype(BF16)
            dv_q = jnp.zeros((CHUNK, WA), F32)
            for hd in range(NHEAD):
                dv_q = jnp.where(lane_head == hd, _dot_tn(wm[hd], dzb), dv_q)
                dz_hd = jnp.where(lane_head == hd, dz_q, 0.0).astype(BF16)
                gws_ref[hd] += _dot_nt(dz_hd, vb_q)
            dvs.append(dv_q)
        dv = jnp.concatenate(dvs, axis=0)
        d_gng = _colsum(dv * vh)
        d_gnb = _colsum(dv)
        dvh = dv * v_ref[0:1, :]
        dpart_ref[:, WA:2 * WA] = rv * (dvh - _rowmean(dvh) - vh * _rowmean(dvh * vh))
        dcn = dyb * (sg * (1.0 + cn * (1.0 - sg)))
        d_cng = _colsum(dcn * ch)
        d_cnb = _colsum(dcn)
        dch = dcn * v_ref[3:4, :]
        dconv = rc * (dch - _rowmean(dch) - ch * _rowmean(dch * ch))
        dpart_ref[:, 2 * WA:3 * WA] = dconv
        dpart_ref[:, 3 * WA:4 * WA] = jnp.zeros((tm, WA), F32)
        d_cb = _colsum(dconv)

        @pl.when(i % tps == 0)
        def _():
            mg_ref[...] = jnp.zeros((8, D), F32)

        mg_ref[2:3, :] += d_gt
        vg_ref[1:2, :] += d_gpost
        v5g_ref[0:1, :] += d_gng
        v5g_ref[1:2, :] += d_gnb
        v5g_ref[2:3, :] += d_cb
        v5g_ref[3:4, :] += d_cng
        v5g_ref[4:5, :] += d_cnb
        v5g_ref[5:6, :] += d_goa
        v5g_ref[6:7, :] += d_gob

        @pl.when(i == nt - 1)
        def _():
            row = lax.broadcasted_iota(jnp.int32, (CHUNK, CHUNK), 0)
            col = lax.broadcasted_iota(jnp.int32, (CHUNK, CHUNK), 1)
            for hd in range(NHEAD):
                gws_ref[hd] = jnp.where(col <= row, gws_ref[hd], 0.0)
            gbs_ref[...] = lax.dot_general(e_ref[...], dbs_acc[...], (((1,), (1,)), ((), ())),
                                           precision=lax.Precision.HIGHEST, preferred_element_type=F32)

    tile = pl.BlockSpec((tm, D), lambda i: (i, 0))
    ptile = pl.BlockSpec((tm, 4 * WA), lambda i: (i, 0))
    return _call(
        core, name=name, grid=(nt,), jobs=jobs,
        in_specs=[tile, tile, pl.BlockSpec((tm, 2 * WA), lambda i: (i, 0)), pl.BlockSpec((tm, WA), lambda i: (i, 0)),
                  pl.BlockSpec((None, 8, D), lambda i: (i // tps, 0, 0)), _const_spec((8, D)), _const_spec((D, D)),
                  _const_spec((8, WA)), _const_spec((NHEAD, CHUNK, CHUNK)), _const_spec((CHUNK, WA)),
                  _const_spec((8, WA))],
        out_specs=[ptile, tile, tile, pl.BlockSpec((None, 8, D), lambda i: (i // tps, 0, 0)),
                   pl.BlockSpec((8, D), lambda i: (0, 0)), pl.BlockSpec((8, WA), lambda i: (0, 0)),
                   pl.BlockSpec((NHEAD, CHUNK, CHUNK), lambda i: (0, 0, 0)), pl.BlockSpec((8, CHUNK), lambda i: (0, 0))],
        out_shape=[jax.ShapeDtypeStruct((T, 4 * WA), F32), jax.ShapeDtypeStruct((T, D), BF16),
                   jax.ShapeDtypeStruct((T, D), BF16), jax.ShapeDtypeStruct((nb, 8, D), F32),
                   jax.ShapeDtypeStruct((8, D), F32), jax.ShapeDtypeStruct((8, WA), F32),
                   jax.ShapeDtypeStruct((NHEAD, CHUNK, CHUNK), F32), jax.ShapeDtypeStruct((8, CHUNK), F32)],
        scratch=[pltpu.VMEM((CHUNK, WA), F32)],
        args=[dxo, ym, proj, conv, mod, gvec, w_mo, v512, ws, bias_full, esel])


def _mixer_bwd_b(dxo, x, dpart, proj, mod, gvec, w_mi, cw, tm, name, jobs=()):
    T = x.shape[0]
    nt = T // tm
    nb = mod.shape[0]
    tps = nt // nb
    hpt = tm // HALO
    nh = T // HALO
    off = HALO - (CONV_K - 1)
    p = _pitch(tm)
    ext_rows = 8 * p

    def core(ins, outs, scs):
        dxo_ref, x_ref, dpart_ref, dnext_ref, ag_ref, halo_ref, mod_ref, g_ref, wmi_ref, cw_ref = ins
        dx_ref, dproj_ref, hb_ref, mg_ref, vg_ref, dcw_ref = outs
        glu_ext, dconv_ext, dglu_scr, dcw_acc = scs
        i = pl.program_id(0)
        first = i % tps == 0
        last = i % tps == tps - 1
        a = ag_ref[:, 0:WA]
        g = ag_ref[:, WA:2 * WA]
        sgg = _sigmoid(g)

        @pl.when(i == 0)
        def _():
            glu_ext[:, HALO + tm:HALO + ext_rows, :] = jnp.zeros((4, ext_rows - tm, 128), F32)
            dconv_ext[:, HALO + tm:HALO + ext_rows, :] = jnp.zeros((4, ext_rows - tm, 128), F32)
            dcw_acc[...] = jnp.zeros((32, 8, WA), F32)
            vg_ref[...] = jnp.zeros((8, D), F32)

        _to_slabs(glu_ext, 0, jnp.where(first, 0.0, halo_ref[:, 0:WA] * _sigmoid(halo_ref[:, WA:2 * WA])))
        _to_slabs(glu_ext, HALO, a * sgg)
        _to_slabs(dconv_ext, 0, dpart_ref[:, 2 * WA:3 * WA])
        _to_slabs(dconv_ext, tm, jnp.where(last, 0.0, dnext_ref[...]))
        sub = lax.broadcasted_iota(jnp.int32, (8, 128), 0)
        for s in range(4):
            accs = [jnp.zeros((8, 128), F32)] * CONV_K
            for v in range(p):
                dc = jnp.where(v + p * sub < tm, dconv_ext[s, pl.ds(v, 8, stride=p), :], 0.0)
                for k in range(CONV_K):
                    accs[k] = accs[k] + dc * glu_ext[s, pl.ds(v + off + k, 8, stride=p), :]
            for k in range(CONV_K):
                dcw_acc[k, :, _lanes(s)] += accs[k]
        dglu = _tap_sum(dconv_ext, dglu_scr, cw_ref, jnp.zeros((1, WA), F32), tm, lambda k: (CONV_K - 1) - k)

        @pl.when(i == nt - 1)
        def _():
            for k in range(CONV_K):
                dcw_ref[k:k + 1, :] = jnp.sum(dcw_acc[k], axis=0, keepdims=True)
            dcw_ref[CONV_K:32, :] = jnp.zeros((32 - CONV_K, WA), F32)

        da = dglu * sgg
        dgg = dglu * a * (sgg * (1.0 - sgg))
        dproj_ref[:, 0:2 * WA] = dpart_ref[:, 0:2 * WA].astype(BF16)
        dproj_ref[:, 2 * WA:3 * WA] = da.astype(BF16)
        dproj_ref[:, 3 * WA:4 * WA] = dgg.astype(BF16)
        dh = jnp.zeros((tm, D), F32)
        for j in range(NDEV):
            dh = dh + _dot_nt(dproj_ref[:, j * MB:(j + 1) * MB], wmi_ref[j])
        xv = x_ref[...]
        sc, sh = mod_ref[1:2, :], mod_ref[0:1, :]
        gpre = g_ref[0:1, :]
        r = lax.rsqrt(_rowmean(xv * xv) + EPS)
        xh = xv * r
        n = xh * gpre
        hb_ref[...] = (n * (1.0 + sc) + sh).astype(BF16)
        d_sc = _colsum(dh * n)
        d_sh = _colsum(dh)
        dn = dh * (1.0 + sc)
        d_gpre = _colsum(dn * xh)
        dxh = dn * gpre
        dx_ref[...] = dxo_ref[...] + r * (dxh - xh * _rowmean(dxh * xh))

        @pl.when(first)
        def _():
            mg_ref[...] = jnp.zeros((8, D), F32)

        mg_ref[0:1, :] += d_sh
        mg_ref[1:2, :] += d_sc
        vg_ref[0:1, :] += d_gpre

    tile = pl.BlockSpec((tm, D), lambda i: (i, 0))
    return _call(
        core, name=name, grid=(nt,), jobs=jobs,
        in_specs=[tile, tile, pl.BlockSpec((tm, 4 * WA), lambda i: (i, 0)),
                  pl.BlockSpec((HALO, WA), lambda i: (jnp.minimum((i + 1) * hpt, nh - 1), 2)),
                  pl.BlockSpec((tm, 2 * WA), lambda i: (i, 1)),
                  pl.BlockSpec((HALO, 2 * WA), lambda i: (jnp.maximum(i * hpt - 1, 0), 1)),
                  pl.BlockSpec((None, 8, D), lambda i: (i // tps, 0, 0)), _const_spec((8, D)),
                  _const_spec((NDEV, D, MB)), _const_spec((32, WA))],
        out_specs=[tile, pl.BlockSpec((tm, 4 * WA), lambda i: (i, 0)), tile,
                   pl.BlockSpec((None, 8, D), lambda i: (i // tps, 0, 0)), pl.BlockSpec((8, D), lambda i: (0, 0)),
                   pl.BlockSpec((32, WA), lambda i: (0, 0))],
        out_shape=[jax.ShapeDtypeStruct((T, D), F32), jax.ShapeDtypeStruct((T, 4 * WA), BF16),
                   jax.ShapeDtypeStruct((T, D), BF16), jax.ShapeDtypeStruct((nb, 8, D), F32),
                   jax.ShapeDtypeStruct((8, D), F32), jax.ShapeDtypeStruct((32, WA), F32)],
        scratch=[pltpu.VMEM((4, HALO + ext_rows, 128), F32), pltpu.VMEM((4, HALO + ext_rows, 128), F32),
                 pltpu.VMEM((4, ext_rows, 128), F32), pltpu.VMEM((32, 8, WA), F32)],
        args=[dxo, x, dpart, dpart, proj, proj, mod, gvec, w_mi, cw])


def _loss_head(y, target, tm, name):
    T = y.shape[0]

    def core(ins, outs, _):
        y_ref, t_ref = ins
        dy_ref, loss_ref = outs

        @pl.when(pl.program_id(0) == 0)
        def _():
            loss_ref[...] = jnp.zeros((8, D), F32)

        err = y_ref[...] - t_ref[...]
        dy_ref[...] = err * (1.0 / D)
        part = jnp.sum(_rowmean(err * err), axis=0, keepdims=True)
        loss_ref[...] += HALF * part

    tile = pl.BlockSpec((tm, D), lambda i: (i, 0))
    return _call(
        core, name=name, grid=(T // tm,),
        in_specs=[tile, tile], out_specs=[tile, pl.BlockSpec((8, D), lambda i: (0, 0))],
        out_shape=[jax.ShapeDtypeStruct((T, D), F32), jax.ShapeDtypeStruct((8, D), F32)],
        args=[y, target])[0]


def _grad_chip(a, b, a_spec, b_spec, prod_shape, half, name, jobs=()):
    steps = 8 if half is None else 4
    R = prod_shape[0] if half is None else half
    C = prod_shape[1]

    def core(ins, outs, scs):
        a_ref, b_ref = ins
        (o_ref,) = outs
        own, snd, rcv, ssem, rsem, lsem = scs
        s = pl.program_id(0)
        c = lax.axis_index("c")
        me = _me()
        sib = _flip(me, (0, 0, 1))
        prod = _dot_tn(a_ref[...], b_ref[...]).astype(BF16)
        if half is None:
            q = s // 2

            @pl.when(s % 2 == c)
            def _():
                own[q] = prod

            @pl.when(s % 2 != c)
            def _():
                snd[q] = prod
                _remote(snd.at[q], rcv.at[q], ssem.at[q], rsem.at[q], sib).start()
        else:
            lo = prod[0:half, :]
            hi = prod[half:2 * half, :]
            own[s] = jnp.where(c == 0, lo, hi)
            snd[s] = jnp.where(c == 0, hi, lo)
            _remote(snd.at[s], rcv.at[s], ssem.at[s], rsem.at[s], sib).start()

        @pl.when(s == steps - 1)
        def _():
            for q4 in range(4):
                cp = _remote(snd.at[q4], rcv.at[q4], ssem.at[q4], rsem.at[q4], sib)
                cp.wait_recv()
                cp.wait_send()
                snd[q4] = (own[q4].astype(F32) + rcv[q4].astype(F32)).astype(BF16)
            out = pltpu.make_async_copy(snd, o_ref, lsem)
            out.start()
            out.wait()

    return _call(
        core, name=name, grid=(steps,), jobs=jobs, in_specs=[a_spec, b_spec], out_specs=[HBM],
        out_shape=[jax.ShapeDtypeStruct((4, R, C), BF16)],
        scratch=[pltpu.VMEM((4, R, C), BF16), pltpu.VMEM((4, R, C), BF16), pltpu.VMEM((4, R, C), BF16),
                 pltpu.SemaphoreType.DMA((4,)), pltpu.SemaphoreType.DMA((4,)), pltpu.SemaphoreType.DMA],
        args=[a, b])


def _grad_w_in(dg, hb, name, jobs=()):
    T = hb.shape[0]
    return _grad_chip(dg, hb, pl.BlockSpec((None, T, FBP), lambda s: (s, 0, 0)), _const_spec((T, D)),
                      (FBP, D), None, name, jobs)


def _grad_w_out(act, dyb, name, jobs=()):
    T = dyb.shape[0]
    return _grad_chip(act, dyb, pl.BlockSpec((None, T, FBP), lambda s: (s, 0, 0)), _const_spec((T, D)),
                      (FBP, D), FO, name, jobs)


def _grad_w_mi(hb, dproj, name, jobs=()):
    T = hb.shape[0]
    return _grad_chip(hb, dproj, _const_spec((T, D)), pl.BlockSpec((T, MB), lambda s: (0, s)),
                      (D, MB), None, name, jobs)


def _grad_w_mo(ycat, dym, name, jobs=()):
    T = ycat.shape[0]
    return _grad_chip(ycat, dym, pl.BlockSpec((T, 2 * MO), lambda s: (0, s)), _const_spec((T, D)),
                      (2 * MO, D), MO, name, jobs)


def _adamw_math(w, g, m, v):
    m2 = ADAM_B1 * m + (1.0 - ADAM_B1) * g
    v2 = ADAM_B2 * v + (1.0 - ADAM_B2) * (g * g)
    m_hat = m2 / (1.0 - ADAM_B1 ** ADAM_STEP)
    v_hat = v2 / (1.0 - ADAM_B2 ** ADAM_STEP)
    delta = -ADAM_LR * (m_hat / (jnp.sqrt(v_hat) + ADAM_EPS) + ADAM_WD * w)
    return delta, m2, v2


def _adamw_reduce(parts, w, m, v, tr, name):
    R, C = w.shape

    def core(ins, outs, _):
        p_ref, w_ref, m_ref, v_ref = ins
        g_ref, d_ref, m2_ref, v2_ref = outs
        g = p_ref[0].astype(F32)
        for s in range(1, 4):
            g = g + p_ref[s].astype(F32)
        g_ref[...] = g
        d_ref[...], m2_ref[...], v2_ref[...] = _adamw_math(w_ref[...], g, m_ref[...], v_ref[...])

    blk = pl.BlockSpec((tr, C), lambda i: (i, 0))
    return _call(
        core, name=name, grid=(R // tr,),
        in_specs=[pl.BlockSpec((4, tr, C), lambda i: (0, i, 0)), blk, blk, blk],
        out_specs=[blk, blk, blk, blk], out_shape=[jax.ShapeDtypeStruct((R, C), F32)] * 4,
        args=[parts, w, m, v])[0]


def _adamw_ada(sc_all, dd, w, m, v, tr, name):
    R, C = w.shape

    def core(ins, outs, _):
        sc_ref, dd_ref, w_ref, m_ref, v_ref = ins
        g_ref, d_ref, m2_ref, v2_ref = outs
        g = _dot_tn(sc_ref[...].astype(BF16), dd_ref[...].astype(BF16))
        g_ref[...] = g
        d_ref[...], m2_ref[...], v2_ref[...] = _adamw_math(w_ref[...], g, m_ref[...], v_ref[...])

    blk = pl.BlockSpec((tr, C), lambda i: (i, 0))
    return _call(
        core, name=name, grid=(R // tr,),
        in_specs=[pl.BlockSpec((64, tr), lambda i: (0, i)), pl.BlockSpec((64, C), lambda i: (0, 0)), blk, blk, blk],
        out_specs=[blk, blk, blk, blk], out_shape=[jax.ShapeDtypeStruct((R, C), F32)] * 4,
        args=[sc_all, dd, w, m, v])[0]


def _adamw_small(grads, wmv, name):
    nw = len(grads)
    srcs = []
    for arr, _ in grads:
        if not any(arr is s for s in srcs):
            srcs.append(arr)
    src_of = [[arr is s for s in srcs].index(True) for arr, _ in grads]

    def core(ins, outs, _):
        s_refs = ins[:len(srcs)]
        w_refs = ins[len(srcs):]
        for t in range(nw):
            row = grads[t][1]
            g = s_refs[src_of[t]][...] if row is None else s_refs[src_of[t]][row:row + 1, :]
            w_ref, m_ref, v_ref = w_refs[3 * t:3 * t + 3]
            g_ref, d_ref, m2_ref, v2_ref = outs[4 * t:4 * t + 4]
            g_ref[...] = g
            d_ref[...], m2_ref[...], v2_ref[...] = _adamw_math(w_ref[...], g, m_ref[...], v_ref[...])

    out_shape = []
    for t in range(nw):
        out_shape += [jax.ShapeDtypeStruct(wmv[3 * t].shape, F32)] * 4
    return _call(
        core, name=name, grid=(), in_specs=[VM] * (len(srcs) + 3 * nw), out_specs=[VM] * (4 * nw),
        out_shape=out_shape, args=srcs + list(wmv))[0]


def _ada_fwd(c_pad, w_ada, b_cols, cw_pad):
    def body(c_ref, w_ref, b_ref, cwp_ref, ada_ref, sc_ref, cw_ref, cbuf, send_buf, ssem, rsem):
        me = _me()
        mi = _lin(me)
        cbuf[mi] = c_ref[...]
        cw_ref[mi] = cwp_ref[...]
        peers = [_flip(me, f) for f in FLIPS]
        first = []
        for k, p in enumerate(peers):
            first.append(_remote(cbuf.at[mi], cbuf.at[mi], ssem.at[k], rsem.at[k], p))
            first.append(_remote(cw_ref.at[mi], cw_ref.at[mi], ssem.at[7 + k], rsem.at[7 + k], p))
        for cp in first:
            cp.start()
        for k, p in enumerate(peers):
            pi = _lin(p)
            _remote(cbuf.at[pi], cbuf.at[pi], ssem.at[k], rsem.at[k], p).wait_recv()
            _remote(cw_ref.at[pi], cw_ref.at[pi], ssem.at[7 + k], rsem.at[7 + k], p).wait_recv()
        c_all = cbuf[...].reshape(8 * 8, D)
        sc = c_all * _sigmoid(c_all)
        sc_ref[...] = sc
        res = _dot(sc.astype(BF16), w_ref[...].astype(BF16)) + b_ref[...]
        send_buf[...] = res.reshape(8, 8, ADA_B)
        ada_ref[mi] = send_buf[mi]
        second = []
        for k, p in enumerate(peers):
            second.append(_remote(send_buf.at[_lin(p)], ada_ref.at[mi], ssem.at[14 + k], rsem.at[14 + k], p))
        for cp in second:
            cp.start()
        for k, p in enumerate(peers):
            _remote(send_buf.at[mi], ada_ref.at[_lin(p)], ssem.at[14 + k], rsem.at[14 + k], p).wait_recv()
        for cp in first + second:
            cp.wait_send()

    return pl.pallas_call(
        body,
        name="ada_fwd",
        in_specs=[VM, VM, VM, VM],
        out_specs=[VM, VM, VM],
        out_shape=[
            jax.ShapeDtypeStruct((8, 8, ADA_B), F32),
            jax.ShapeDtypeStruct((64, D), F32),
            jax.ShapeDtypeStruct((8, 32, 64), F32),
        ],
        scratch_shapes=[
            pltpu.VMEM((8, 8, D), F32),
            pltpu.VMEM((8, 8, ADA_B), F32),
            pltpu.SemaphoreType.DMA((21,)),
            pltpu.SemaphoreType.DMA((21,)),
        ],
        compiler_params=pltpu.CompilerParams(vmem_limit_bytes=VMEM_LIMIT),
    )(c_pad, w_ada, b_cols, cw_pad)


def _ada_bwd(dada):
    def body(d_ref, dd_ref, gb_ref, rbuf, ssem, rsem):
        me = _me()
        mi = _lin(me)
        peers = [_flip(me, f) for f in FLIPS]
        rbuf[mi] = d_ref[mi]
        first = []
        for k, p in enumerate(peers):
            first.append(_remote(d_ref.at[_lin(p)], rbuf.at[mi], ssem.at[k], rsem.at[k], p))
        for cp in first:
            cp.start()
        for k, p in enumerate(peers):
            _remote(d_ref.at[mi], rbuf.at[_lin(p)], ssem.at[k], rsem.at[k], p).wait_recv()
        dd = rbuf[...].reshape(64, ADA_B)
        dd_ref[...] = dd
        gb_ref[mi] = jnp.broadcast_to(_colsum(dd), (8, ADA_B))
        second = []
        for k, p in enumerate(peers):
            second.append(_remote(gb_ref.at[mi], gb_ref.at[mi], ssem.at[7 + k], rsem.at[7 + k], p))
        for cp in second:
            cp.start()
        for k, p in enumerate(peers):
            pi = _lin(p)
            _remote(gb_ref.at[pi], gb_ref.at[pi], ssem.at[7 + k], rsem.at[7 + k], p).wait_recv()
        for cp in first + second:
            cp.wait_send()

    return pl.pallas_call(
        body,
        name="ada_bwd",
        in_specs=[VM],
        out_specs=[VM, VM],
        out_shape=[jax.ShapeDtypeStruct((64, ADA_B), F32), jax.ShapeDtypeStruct((8, 8, ADA_B), F32)],
        scratch_shapes=[
            pltpu.VMEM((8, 8, ADA_B), F32),
            pltpu.SemaphoreType.DMA((14,)),
            pltpu.SemaphoreType.DMA((14,)),
        ],
        compiler_params=pltpu.CompilerParams(vmem_limit_bytes=VMEM_LIMIT),
    )(dada)


def _tail_exchange(p_small, job):
    shapes = [p.shape for p in p_small]

    def body(*refs):
        p_refs = refs[0:4]
        j_ins = refs[4:4 + len(job.ins)]
        o = 4 + len(job.ins)
        s_refs = refs[o:o + 4]
        j_outs = refs[o + 4:o + 4 + len(job.out_shape)]
        o = o + 4 + len(job.out_shape)
        bufs = refs[o:o + 4]
        ssem, rsem = refs[o + 4:o + 6]
        j_sems = refs[o + 6:]
        me = _me()
        mi = _lin(me)
        job.start(j_ins, j_outs, j_sems)
        sent = []
        for a in range(4):
            bufs[a][mi] = p_refs[a][...]
        for k, f in enumerate(FLIPS):
            p = _flip(me, f)
            for a in range(4):
                cp = _remote(bufs[a].at[mi], bufs[a].at[mi], ssem.at[7 * a + k], rsem.at[7 * a + k], p)
                cp.start()
                sent.append(cp)
        for k, f in enumerate(FLIPS):
            p = _flip(me, f)
            pi = _lin(p)
            for a in range(4):
                _remote(bufs[a].at[pi], bufs[a].at[pi], ssem.at[7 * a + k], rsem.at[7 * a + k], p).wait_recv()
        for cp in sent:
            cp.wait_send()
        for a in range(4):
            s = bufs[a][0]
            for dev in range(1, NDEV):
                s = s + bufs[a][dev]
            s_refs[a][...] = s
        job.end(j_ins, j_outs, j_sems)

    res = pl.pallas_call(
        body,
        name="tail_exchange",
        in_specs=[VM] * 4 + [HBM] * len(job.ins),
        out_specs=[VM] * 4 + [HBM] * len(job.out_shape),
        out_shape=[jax.ShapeDtypeStruct(s, F32) for s in shapes] + job.out_shape,
        scratch_shapes=[pltpu.VMEM((NDEV,) + s, F32) for s in shapes]
        + [pltpu.SemaphoreType.DMA((28,)), pltpu.SemaphoreType.DMA((28,))] + job.sems,
        compiler_params=pltpu.CompilerParams(vmem_limit_bytes=VMEM_LIMIT),
    )(*p_small, *job.ins)
    return list(res[:4]), list(res[4:])


SMALL_D = ("g_pre_f1", "g_post_f1", "g_pre_m", "g_post_m", "g_pre_f2", "g_post_f2")
SMALL_W = ("gmlp_norm_g", "gmlp_norm_b", "conv_b", "conv_norm_g", "conv_norm_b", "g_out_a", "g_out_b")


def kernel(x, c, w_ada, b_ada, g_pre_f1, g_post_f1, w_f1_in, w_f1_out, g_pre_m, g_post_m, w_mix_in, gmlp_norm_g, gmlp_norm_b, w_spatial, b_spatial, conv_w, conv_b, conv_norm_g, conv_norm_b, g_out_a, g_out_b, w_mix_out, g_pre_f2, g_post_f2, w_f2_in, w_f2_out, loss_target, m_w_ada, m_b_ada, m_g_pre_f1, m_g_post_f1, m_w_f1_in, m_w_f1_out, m_g_pre_m, m_g_post_m, m_w_mix_in, m_gmlp_norm_g, m_gmlp_norm_b, m_w_spatial, m_b_spatial, m_conv_w, m_conv_b, m_conv_norm_g, m_conv_norm_b, m_g_out_a, m_g_out_b, m_w_mix_out, m_g_pre_f2, m_g_post_f2, m_w_f2_in, m_w_f2_out, v_w_ada, v_b_ada, v_g_pre_f1, v_g_post_f1, v_w_f1_in, v_w_f1_out, v_g_pre_m, v_g_post_m, v_w_mix_in, v_gmlp_norm_g, v_gmlp_norm_b, v_w_spatial, v_b_spatial, v_conv_w, v_conv_b, v_conv_norm_g, v_conv_norm_b, v_g_out_a, v_g_out_b, v_w_mix_out, v_g_pre_f2, v_g_post_f2, v_w_f2_in, v_w_f2_out):
    given = dict(locals())
    bl, seq, _ = x.shape
    T = bl * seq
    tm = min(256, seq // 2)
    mi = _lin((lax.axis_index("x"), lax.axis_index("y"), lax.axis_index("c")))

    c_pad = jnp.pad(c, ((0, 8 - bl), (0, 0)))
    b_cols = lax.dynamic_slice(b_ada, (0, mi * ADA_B), (1, ADA_B))
    cw_pad = jnp.pad(conv_w[0], ((0, 1), (0, 0)))
    ada_blk, sc_all, cw_all = _ada_fwd(c_pad, w_ada[0], b_cols, cw_pad)
    ada = ada_blk[:, 0:bl, :].transpose(1, 0, 2).reshape(bl, 9, D)
    pad5 = jnp.zeros((bl, 5, D), F32)
    mod1 = jnp.concatenate([ada[:, 0:3], pad5], axis=1)
    mod2 = jnp.concatenate([ada[:, 3:6], pad5], axis=1)
    mod3 = jnp.concatenate([ada[:, 6:9], pad5], axis=1)
    cw_full = cw_all.transpose(1, 0, 2).reshape(32, WA)

    def shard_in(w):
        return jnp.pad(w[0].T.astype(BF16), ((0, FBP - FB), (0, 0)))

    zpad = jnp.zeros((FBP - FB, D), BF16)
    g_f1 = _Gather([shard_in(w_f1_in), w_f1_out[0].astype(BF16)], ("rows", "out"), zpad)
    g_mx = _Gather([w_mix_in[0].astype(BF16), w_mix_out[0].astype(BF16), w_f2_out[0].astype(BF16)],
                   ("rows", "rows", "out"), zpad)
    g_f2 = _Gather([shard_in(w_f2_in)], ("rows",), zpad, late_mid=True)
    (wi1, wo1), = _call(None, name="gather_f1", grid=(), in_specs=[], out_specs=[], out_shape=[], args=[], jobs=[g_f1])[1]

    zrow = jnp.zeros((1, D), F32)
    gv1 = jnp.concatenate([g_pre_f1, g_post_f1] + [zrow] * 6, axis=0)
    gvm = jnp.concatenate([g_pre_m, g_post_m] + [zrow] * 6, axis=0)
    gv2 = jnp.concatenate([g_pre_f2, g_post_f2] + [zrow] * 6, axis=0)
    v512 = jnp.concatenate([gmlp_norm_g, gmlp_norm_b, conv_b, conv_norm_g, conv_norm_b, g_out_a, g_out_b,
                            jnp.zeros((1, WA), F32)], axis=0)
    ws = w_spatial[0]
    bias_full = jnp.repeat(b_spatial[0].T, HD, axis=1)
    esel = (lax.broadcasted_iota(jnp.int32, (8, WA), 1) // HD == lax.broadcasted_iota(jnp.int32, (8, WA), 0)).astype(F32)

    x0 = x.reshape(T, D)
    (x1, gu1, y1), ((wmi, wmo, wo2),) = _ffn_fwd(x0, mod1, gv1, wi1, wo1, tm, "ffn1_fwd", jobs=[g_mx])
    wmo = wmo.reshape(D, D)
    (x2, proj, ym, conv), ((wi2,),) = _mixer_fwd(x1, mod2, gvm, wmi, wmo, v512, ws, bias_full, cw_full, tm, "mixer_fwd", jobs=[g_f2])
    (x3, gu2, y2), _ = _ffn_fwd(x2, mod3, gv2, wi2, wo2, tm, "ffn2_fwd")
    dx3, loss_blk = _loss_head(x3, loss_target.reshape(T, D), tm, "loss_head")

    (dx2, dg2, act2, hb2, dyb2, mg3, vg3), _ = _ffn_bwd(dx3, x2, y2, gu2, mod3, gv2, wi2, wo2, tm, "ffn2_bwd")
    (g_wi2,), _ = _grad_w_in(dg2, hb2, "ffn2_gw_in")
    (g_wo2,), _ = _grad_w_out(act2, dyb2, "ffn2_gw_out")
    (dpart, dymb, ycat, mg2a, vgma, v5g, gws, gbs), ((p_wi2,),) = _mixer_bwd_a(
        dx2, ym, proj, conv, mod2, gvm, wmo, v512, ws, bias_full, esel, tm, "mixer_bwd_a",
        jobs=[_ChipScatter([g_wi2])])
    (dx1, dproj, hbm, mg2b, vgmb, dcw), ((p_wo2,),) = _mixer_bwd_b(
        dx2, x1, dpart, proj, mod2, gvm, wmi, cw_full, tm, "mixer_bwd_b", jobs=[_ChipScatter([g_wo2])])
    (g_wmi,), _ = _grad_w_mi(hbm, dproj, "mixer_gw_in")
    (g_wmo,), _ = _grad_w_mo(ycat, dymb, "mixer_gw_out")
    (dx0, dg1, act1, hb1, dyb1, mg1, vg1), ((p_wmi, p_wmo),) = _ffn_bwd(
        dx1, x0, y1, gu1, mod1, gv1, wi1, wo1, tm, "ffn1_bwd", jobs=[_ChipScatter([g_wmi, g_wmo])])
    (g_wo1,), _ = _grad_w_out(act1, dyb1, "ffn1_gw_out")
    (g_wi1,), ((p_wo1,),) = _grad_w_in(dg1, hb1, "ffn1_gw_in", jobs=[_ChipScatter([g_wo1])])

    dada = jnp.concatenate([mg1[:, 0:3], mg2b[:, 0:2], mg2a[:, 2:3], mg3[:, 0:3]], axis=1)
    dada = dada.reshape(bl, NDEV, ADA_B).transpose(1, 0, 2)
    dada = jnp.pad(dada, ((0, 0), (0, 8 - bl), (0, 0)))
    dd_all, gb_all = _ada_bwd(dada)
    g_bada = gb_all[:, 0, :].reshape(1, 9 * D)

    p1 = jnp.concatenate([vg1[0:2], vgmb[0:1], vgma[1:2], vg3[0:2], loss_blk[0:1], zrow], axis=0)
    p2 = jnp.concatenate([v5g, dcw], axis=0)
    (s1, s2, s3, s4), (p_wi1,) = _tail_exchange([p1, p2, gws, gbs], _ChipScatter([g_wi1]))
    loss = s1[6, 0]

    res = {}
    for nm, part in (("w_f1_in", p_wi1), ("w_f2_in", p_wi2)):
        quad = _adamw_reduce(part, given[nm][0].T, given["m_" + nm][0].T, given["v_" + nm][0].T, FO, "adamw_" + nm)
        res[nm] = tuple(t.T[None] for t in quad)
    for nm, part, tr in (("w_f1_out", p_wo1, FO), ("w_f2_out", p_wo2, FO), ("w_mix_in", p_wmi, 256), ("w_mix_out", p_wmo, MO)):
        quad = _adamw_reduce(part, given[nm][0], given["m_" + nm][0], given["v_" + nm][0], tr, "adamw_" + nm)
        res[nm] = tuple(t[None] for t in quad)
    quad = _adamw_ada(sc_all, dd_all, w_ada[0], m_w_ada[0], v_w_ada[0], 256, "adamw_w_ada")
    res["w_ada"] = tuple(t[None] for t in quad)

    small = SMALL_D + SMALL_W + ("w_spatial", "b_spatial", "b_ada", "conv_w")
    g_cw = lax.dynamic_slice(s2, (8, mi * 64), (32, 64))
    grads = [(s1, r) for r in range(6)] + [(s2, r) for r in range(7)] + [(s3, None), (s4, None), (g_bada, None), (g_cw, None)]
    wmv = []
    for nm in small:
        for pre in ("", "m_", "v_"):
            a = given[pre + nm]
            if nm in ("w_spatial", "b_spatial"):
                a = a[0]
            elif nm == "conv_w":
                a = jnp.pad(a[0], ((0, 1), (0, 0)), constant_values=1.0 if pre == "v_" else 0.0)
            wmv.append(a)
    outs = _adamw_small(grads, wmv, "adamw_small")
    for t, nm in enumerate(small):
        quad = outs[4 * t:4 * t + 4]
        if nm in ("w_spatial", "b_spatial"):
            quad = [q[None] for q in quad]
        elif nm == "conv_w":
            quad = [q[0:CONV_K][None] for q in quad]
        res[nm] = tuple(quad)

    order = ["w_ada", "b_ada", "g_pre_f1", "g_post_f1", "w_f1_in", "w_f1_out", "g_pre_m", "g_post_m", "w_mix_in",
             "gmlp_norm_g", "gmlp_norm_b", "w_spatial", "b_spatial", "conv_w", "conv_b", "conv_norm_g", "conv_norm_b",
             "g_out_a", "g_out_b", "w_mix_out", "g_pre_f2", "g_post_f2", "w_f2_in", "w_f2_out"]
    out = [loss, dx0.reshape(bl, seq, D)]
    for k in range(4):
        out += [res[nm][k] for nm in order]
    return tuple(out)
```

```python
import jax
import jax.numpy as jnp
from jax import lax
from jax.experimental import pallas as pl
from jax.experimental.pallas import tpu as pltpu

F32 = jnp.float32
BF16 = jnp.bfloat16

D = 1024
DFF = 2816
NDEV = 8
FB = 2 * DFF // NDEV
FBP = 768
FO = DFF // NDEV
WA = 512
NHEAD = 8
HD = 64
CHUNK = 128
CONV_K = 31
HALO = 32
MB = 2 * (WA + WA) // NDEV
MO = D // NDEV
ADA_B = 9 * D // NDEV
EPS = 1e-6
HALF = 0.5

ADAM_LR = 0.001
ADAM_B1 = 0.9
ADAM_B2 = 0.999
ADAM_EPS = 1e-08
ADAM_WD = 0.01
ADAM_STEP = 10

VMEM_LIMIT = 56 * 1024 * 1024
MESH = pl.DeviceIdType.MESH
FLIPS = ((0, 0, 1), (1, 0, 0), (0, 1, 0), (1, 1, 0), (1, 0, 1), (0, 1, 1), (1, 1, 1))
CHIP_FLIPS = ((1, 0, 0), (0, 1, 0), (1, 1, 0))
HBM = pl.BlockSpec(memory_space=pl.ANY)
VM = pl.BlockSpec(memory_space=pltpu.VMEM)


def _dot(a, b):
    return lax.dot_general(a, b, (((1,), (0,)), ((), ())), preferred_element_type=F32)


def _dot_nt(a, b):
    return lax.dot_general(a, b, (((1,), (1,)), ((), ())), preferred_element_type=F32)


def _dot_tn(a, b):
    return lax.dot_general(a, b, (((0,), (0,)), ((), ())), preferred_element_type=F32)


def _rowmean(v):
    return jnp.mean(v, axis=-1, keepdims=True)


def _colsum(v):
    return jnp.sum(v, axis=0, keepdims=True)


def _sigmoid(v):
    return 1.0 / (1.0 + jnp.exp(-v))


def _const_spec(shape):
    nd = len(shape)
    return pl.BlockSpec(shape, lambda *_: (0,) * nd, pipeline_mode=pl.Buffered(1))


def _me():
    return lax.axis_index("x"), lax.axis_index("y"), lax.axis_index("c")


def _flip(me, f):
    return tuple(1 - v if b else v for v, b in zip(me, f))


def _lin(p):
    return 4 * p[0] + 2 * p[1] + p[2]


def _remote(src, dst, send_sem, recv_sem, dev):
    return pltpu.make_async_remote_copy(src_ref=src, dst_ref=dst, send_sem=send_sem, recv_sem=recv_sem,
                                        device_id=dev, device_id_type=MESH)


def _blk(kind, ref, p):
    if kind == "out":
        return ref.at[2 * p[0] + p[1], pl.ds(p[2] * FO, FO), :]
    return ref.at[_lin(p)]


class _Gather:
    def __init__(self, shards, kinds, zpad, late_mid=False):
        self.late_mid = late_mid
        self.kinds = kinds
        self.n = len(shards)
        self.ins = list(shards) + [zpad]
        self.out_shape = [jax.ShapeDtypeStruct((4, FBP, D) if k == "out" else (NDEV,) + s.shape, BF16)
                          for s, k in zip(shards, kinds)]
        self.n_out = sum(k == "out" for k in kinds)
        self.sems = [pltpu.SemaphoreType.DMA((7 * self.n,)), pltpu.SemaphoreType.DMA((7 * self.n,)),
                     pltpu.SemaphoreType.DMA((self.n + 4 * max(self.n_out, 1),))]

    def _first(self, ins, outs, sems):
        ssem, rsem, lsem = sems
        me = _me()
        sib = _flip(me, (0, 0, 1))
        cps, loc = [], []
        nz = 0
        for a in range(self.n):
            mine = _blk(self.kinds[a], outs[a], me)
            loc.append(pltpu.make_async_copy(ins[a], mine, lsem.at[a]))
            if self.kinds[a] == "out":
                for q in range(4):
                    loc.append(pltpu.make_async_copy(ins[self.n], outs[a].at[q, pl.ds(FB, FBP - FB), :],
                                                     lsem.at[self.n + 4 * nz + q]))
                nz += 1
            cps.append(_remote(ins[a], mine, ssem.at[7 * a], rsem.at[7 * a], sib))
            for j, f in enumerate(CHIP_FLIPS):
                cps.append(_remote(ins[a], mine, ssem.at[7 * a + 1 + j], rsem.at[7 * a + 1 + j], _flip(me, f)))
        return cps, loc

    def _passed(self, outs, sems):
        ssem, rsem, _ = sems
        me = _me()
        sib = _flip(me, (0, 0, 1))
        cps = []
        for j, f in enumerate(CHIP_FLIPS):
            for a in range(self.n):
                blk = _blk(self.kinds[a], outs[a], _flip(me, f))
                cps.append(_remote(blk, blk, ssem.at[7 * a + 4 + j], rsem.at[7 * a + 4 + j], sib))
        return cps

    def start(self, ins, outs, sems):
        cps, loc = self._first(ins, outs, sems)
        for cp in loc + cps:
            cp.start()

    def mid(self, ins, outs, sems):
        ssem, rsem, _ = sems
        me = _me()
        passed = self._passed(outs, sems)
        t = 0
        for j, f in enumerate(CHIP_FLIPS):
            for a in range(self.n):
                blk = _blk(self.kinds[a], outs[a], _flip(me, f))
                _remote(blk, blk, ssem.at[7 * a + 1 + j], rsem.at[7 * a + 1 + j], _flip(me, f)).wait_recv()
                passed[t].start()
                t += 1

    def end(self, ins, outs, sems):
        ssem, rsem, _ = sems
        me = _me()
        sib = _flip(me, (0, 0, 1))
        for a in range(self.n):
            blk = _blk(self.kinds[a], outs[a], sib)
            _remote(blk, blk, ssem.at[7 * a], rsem.at[7 * a], sib).wait_recv()
            for j, f in enumerate(CHIP_FLIPS):
                blk = _blk(self.kinds[a], outs[a], _flip(_flip(me, f), (0, 0, 1)))
                _remote(blk, blk, ssem.at[7 * a + 4 + j], rsem.at[7 * a + 4 + j], sib).wait_recv()
        cps, loc = self._first(ins, outs, sems)
        for cp in cps + self._passed(outs, sems):
            cp.wait_send()
        for cp in loc:
            cp.wait()


class _ChipScatter:
    def __init__(self, grads):
        self.n = len(grads)
        self.ins = list(grads)
        self.out_shape = [jax.ShapeDtypeStruct(g.shape, BF16) for g in grads]
        self.sems = [pltpu.SemaphoreType.DMA((3 * self.n,)), pltpu.SemaphoreType.DMA((3 * self.n,)),
                     pltpu.SemaphoreType.DMA((self.n,))]

    def _copies(self, ins, outs, sems):
        ssem, rsem, lsem = sems
        me = _me()
        mq = 2 * me[0] + me[1]
        loc = [pltpu.make_async_copy(ins[a].at[mq], outs[a].at[mq], lsem.at[a]) for a in range(self.n)]
        cps = []
        for k, f in enumerate(CHIP_FLIPS):
            p = _flip(me, f)
            for a in range(self.n):
                cps.append(_remote(ins[a].at[2 * p[0] + p[1]], outs[a].at[mq], ssem.at[3 * a + k], rsem.at[3 * a + k], p))
        return cps, loc

    def start(self, ins, outs, sems):
        cps, loc = self._copies(ins, outs, sems)
        for cp in loc + cps:
            cp.start()

    mid = None

    def end(self, ins, outs, sems):
        ssem, rsem, _ = sems
        me = _me()
        mq = 2 * me[0] + me[1]
        for k, f in enumerate(CHIP_FLIPS):
            p = _flip(me, f)
            for a in range(self.n):
                _remote(ins[a].at[mq], outs[a].at[2 * p[0] + p[1]], ssem.at[3 * a + k], rsem.at[3 * a + k], p).wait_recv()
        cps, loc = self._copies(ins, outs, sems)
        for cp in cps:
            cp.wait_send()
        for cp in loc:
            cp.wait()


class _AllGather:
    def __init__(self, parts):
        self.n = len(parts)
        self.ins = list(parts)
        self.out_shape = [jax.ShapeDtypeStruct((NDEV,) + p.shape, p.dtype) for p in parts]
        self.sems = [pltpu.SemaphoreType.DMA((7 * self.n,)), pltpu.SemaphoreType.DMA((7 * self.n,)),
                     pltpu.SemaphoreType.DMA((self.n,))]

    def _copies(self, ins, outs, sems):
        ssem, rsem, lsem = sems
        me = _me()
        mi = _lin(me)
        loc = [pltpu.make_async_copy(ins[a], outs[a].at[mi], lsem.at[a]) for a in range(self.n)]
        cps = []
        for k, f in enumerate(FLIPS):
            for a in range(self.n):
                cps.append(_remote(ins[a], outs[a].at[mi], ssem.at[7 * a + k], rsem.at[7 * a + k], _flip(me, f)))
        return cps, loc

    def start(self, ins, outs, sems):
        cps, loc = self._copies(ins, outs, sems)
        for cp in loc + cps:
            cp.start()

    mid = None

    def end(self, ins, outs, sems):
        ssem, rsem, _ = sems
        me = _me()
        for k, f in enumerate(FLIPS):
            p = _flip(me, f)
            for a in range(self.n):
                _remote(ins[a], outs[a].at[_lin(p)], ssem.at[7 * a + k], rsem.at[7 * a + k], p).wait_recv()
        cps, loc = self._copies(ins, outs, sems)
        for cp in cps:
            cp.wait_send()
        for cp in loc:
            cp.wait()


def _call(core, *, name, grid, in_specs, out_specs, out_shape, args, scratch=(), jobs=()):
    n_in, n_out, n_sc = len(in_specs), len(out_specs), len(scratch)
    steps = 1
    for g in grid:
        steps *= g

    def body(*refs):
        pos = [0]

        def take(k):
            r = refs[pos[0]:pos[0] + k]
            pos[0] += k
            return r

        ins = take(n_in)
        j_ins = [take(len(j.ins)) for j in jobs]
        outs = take(n_out)
        j_outs = [take(len(j.out_shape)) for j in jobs]
        scs = take(n_sc)
        j_sems = [take(len(j.sems)) for j in jobs]
        if len(grid) == 2:
            step = pl.program_id(0) * grid[1] + pl.program_id(1)
        elif len(grid) == 1:
            step = pl.program_id(0)
        else:
            step = 0
        for j, ji, jo, js in zip(jobs, j_ins, j_outs, j_sems):
            if grid:
                pl.when(step == 0)(lambda j=j, ji=ji, jo=jo, js=js: j.start(ji, jo, js))
            else:
                j.start(ji, jo, js)
        for j, ji, jo, js in zip(jobs, j_ins, j_outs, j_sems):
            if j.mid is not None and grid:
                at = steps - 1 if j.late_mid else (3 * steps) // 4
                pl.when(step == at)(lambda j=j, ji=ji, jo=jo, js=js: j.mid(ji, jo, js))
        if core is not None:
            core(ins, outs, scs)
        for j, ji, jo, js in zip(jobs, j_ins, j_outs, j_sems):
            if grid:
                pl.when(step == steps - 1)(lambda j=j, ji=ji, jo=jo, js=js: j.end(ji, jo, js))
            else:
                if j.mid is not None:
                    j.mid(ji, jo, js)
                j.end(ji, jo, js)

    all_in = list(in_specs)
    all_args = list(args)
    all_out = list(out_specs)
    all_shape = list(out_shape)
    all_sc = list(scratch)
    for j in jobs:
        all_in += [HBM] * len(j.ins)
        all_args += j.ins
    for j in jobs:
        all_out += [HBM] * len(j.out_shape)
        all_shape += j.out_shape
        all_sc += j.sems
    params = dict(vmem_limit_bytes=VMEM_LIMIT)
    if grid:
        params["dimension_semantics"] = ("arbitrary",) * len(grid)
    res = pl.pallas_call(
        body, name=name, grid=grid, in_specs=all_in, out_specs=all_out, out_shape=all_shape,
        scratch_shapes=all_sc, compiler_params=pltpu.CompilerParams(**params),
    )(*all_args)
    core_res = list(res[:n_out])
    job_res = []
    pos = n_out
    for j in jobs:
        job_res.append(list(res[pos:pos + len(j.out_shape)]))
        pos += len(j.out_shape)
    return core_res, job_res


def _ffn_fwd(x, mod, gvec, w_in, w_out, tm, name, jobs=()):
    T = x.shape[0]
    nt = T // tm
    tps = nt // mod.shape[0]

    def core(ins, outs, _):
        x_ref, mod_ref, g_ref, win_ref, wout_ref = ins
        xo_ref, gu_ref, y_ref = outs
        xv = x_ref[...]
        sh, sc, gt = mod_ref[0:1, :], mod_ref[1:2, :], mod_ref[2:3, :]
        r = lax.rsqrt(_rowmean(xv * xv) + EPS)
        h = (xv * r * g_ref[0:1, :]) * (1.0 + sc) + sh
        hb = h.astype(BF16)
        y = jnp.zeros((tm, D), F32)
        for cidx in range(4):
            gate = _dot_nt(hb, win_ref[cidx])
            up = _dot_nt(hb, win_ref[4 + cidx])
            gu_ref[cidx] = gate.astype(BF16)
            gu_ref[4 + cidx] = up.astype(BF16)
            act = gate * _sigmoid(gate) * up
            y = y + _dot(act.astype(BF16), wout_ref[cidx])
        y_ref[...] = y
        ry = lax.rsqrt(_rowmean(y * y) + EPS)
        xo_ref[...] = xv + (HALF * gt) * (y * ry * g_ref[1:2, :])

    tile = pl.BlockSpec((tm, D), lambda i: (i, 0))
    return _call(
        core, name=name, grid=(nt,), jobs=jobs,
        in_specs=[tile, pl.BlockSpec((None, 8, D), lambda i: (i // tps, 0, 0)), _const_spec((8, D)),
                  _const_spec((8, FBP, D)), _const_spec((4, FBP, D))],
        out_specs=[tile, pl.BlockSpec((8, tm, FBP), lambda i: (0, i, 0)), tile],
        out_shape=[jax.ShapeDtypeStruct((T, D), F32), jax.ShapeDtypeStruct((8, T, FBP), BF16),
                   jax.ShapeDtypeStruct((T, D), F32)],
        args=[x, mod, gvec, w_in, w_out])


def _ffn_bwd(dxo, x, y, gu, mod, gvec, w_in, w_out, tm, name, jobs=()):
    T = x.shape[0]
    nt = T // tm
    nb = mod.shape[0]
    tps = nt // nb

    def core(ins, outs, _):
        dxo_ref, x_ref, y_ref, gu_ref, mod_ref, g_ref, win_ref, wout_ref = ins
        dx_ref, dg_ref, act_ref, hb_ref, dyb_ref, mg_ref, vg_ref = outs
        i = pl.program_id(0)
        xv = x_ref[...]
        dxo_v = dxo_ref[...]
        yv = y_ref[...]
        sh, sc, gt = mod_ref[0:1, :], mod_ref[1:2, :], mod_ref[2:3, :]
        gpre, gpost = g_ref[0:1, :], g_ref[1:2, :]
        r = lax.rsqrt(_rowmean(xv * xv) + EPS)
        xh = xv * r
        n = xh * gpre
        hb = (n * (1.0 + sc) + sh).astype(BF16)
        hb_ref[...] = hb
        ry = lax.rsqrt(_rowmean(yv * yv) + EPS)
        yh = yv * ry
        d_gt = _colsum(HALF * dxo_v * (yh * gpost))
        dp = (HALF * gt) * dxo_v
        d_gpost = _colsum(dp * yh)
        dyh = dp * gpost
        dy = ry * (dyh - yh * _rowmean(dyh * yh))
        dyb = dy.astype(BF16)
        dyb_ref[...] = dyb
        dh = jnp.zeros((tm, D), F32)
        for cidx in range(4):
            gate = gu_ref[cidx].astype(F32)
            up = gu_ref[4 + cidx].astype(F32)
            sig = _sigmoid(gate)
            s = gate * sig
            act_ref[cidx] = (s * up).astype(BF16)
            d_act = _dot_nt(dyb, wout_ref[cidx])
            d_up = (d_act * s).astype(BF16)
            d_gate = (d_act * up * (sig * (1.0 + gate * (1.0 - sig)))).astype(BF16)
            dg_ref[cidx] = d_gate
            dg_ref[4 + cidx] = d_up
            dh = dh + _dot(d_gate, win_ref[cidx]) + _dot(d_up, win_ref[4 + cidx])
        d_sc = _colsum(dh * n)
        d_sh = _colsum(dh)
        dn = dh * (1.0 + sc)
        d_gpre = _colsum(dn * xh)
        dxh = dn * gpre
        dx_ref[...] = dxo_v + r * (dxh - xh * _rowmean(dxh * xh))

        @pl.when(i % tps == 0)
        def _():
            mg_ref[...] = jnp.zeros((8, D), F32)

        @pl.when(i == 0)
        def _():
            vg_ref[...] = jnp.zeros((8, D), F32)

        mg_ref[0:1, :] += d_sh
        mg_ref[1:2, :] += d_sc
        mg_ref[2:3, :] += d_gt
        vg_ref[0:1, :] += d_gpre
        vg_ref[1:2, :] += d_gpost

    tile = pl.BlockSpec((tm, D), lambda i: (i, 0))
    return _call(
        core, name=name, grid=(nt,), jobs=jobs,
        in_specs=[tile, tile, tile, pl.BlockSpec((8, tm, FBP), lambda i: (0, i, 0)),
                  pl.BlockSpec((None, 8, D), lambda i: (i // tps, 0, 0)), _const_spec((8, D)),
                  _const_spec((8, FBP, D)), _const_spec((4, FBP, D))],
        out_specs=[tile, pl.BlockSpec((8, tm, FBP), lambda i: (0, i, 0)),
                   pl.BlockSpec((4, tm, FBP), lambda i: (0, i, 0)), tile, tile,
                   pl.BlockSpec((None, 8, D), lambda i: (i // tps, 0, 0)), pl.BlockSpec((8, D), lambda i: (0, 0))],
        out_shape=[jax.ShapeDtypeStruct((T, D), F32), jax.ShapeDtypeStruct((8, T, FBP), BF16),
                   jax.ShapeDtypeStruct((4, T, FBP), BF16), jax.ShapeDtypeStruct((T, D), BF16),
                   jax.ShapeDtypeStruct((T, D), BF16), jax.ShapeDtypeStruct((nb, 8, D), F32),
                   jax.ShapeDtypeStruct((8, D), F32)],
        args=[dxo, x, y, gu, mod, gvec, w_in, w_out])


def _masked_spatial(ws_ref):
    row = lax.broadcasted_iota(jnp.int32, (CHUNK, CHUNK), 0)
    col = lax.broadcasted_iota(jnp.int32, (CHUNK, CHUNK), 1)
    keep = col <= row
    return [jnp.where(keep, ws_ref[hd], 0.0).astype(BF16) for hd in range(NHEAD)]


def _spatial_gate(wm, vb_chunk, lane_head):
    z = jnp.zeros((CHUNK, WA), F32)
    for hd in range(NHEAD):
        z = jnp.where(lane_head == hd, _dot(wm[hd], vb_chunk), z)
    return z


def _layer_norm_stats(v):
    mu = _rowmean(v)
    vc = v - mu
    rstd = lax.rsqrt(_rowmean(vc * vc) + EPS)
    return vc * rstd, rstd


def _pitch(tm):
    p = tm // 8
    while p % 8 != 4:
        p += 1
    return p


def _lanes(s):
    return slice(s * 128, (s + 1) * 128)


def _to_slabs(ref, row0, val):
    for s in range(4):
        ref[s, row0:row0 + val.shape[0], :] = val[:, _lanes(s)]


def _tap_sum(src, out, cw_ref, bias, tm, start):
    p = _pitch(tm)
    for s in range(4):
        accs = [jnp.broadcast_to(bias[:, _lanes(s)], (8, 128))] * p
        for k in range(CONV_K):
            w = jnp.broadcast_to(cw_ref[k:k + 1, _lanes(s)], (8, 128))
            for v in range(p):
                accs[v] = accs[v] + w * src[s, pl.ds(v + start(k), 8, stride=p), :]
        for v in range(p):
            out[s, pl.ds(v, 8, stride=p), :] = accs[v]
    return jnp.concatenate([out[s, 0:tm, :] for s in range(4)], axis=1)


def _mixer_fwd(x, mod, gvec, w_mi, w_mo, v512, ws, bias_full, cw, tm, name, jobs=()):
    T = x.shape[0]
    nt = T // tm
    tps = nt // mod.shape[0]
    ext_rows = 8 * _pitch(tm)

    def core(ins, outs, scs):
        x_ref, mod_ref, g_ref, wmi_ref, wmo_ref, v_ref, ws_ref, bias_ref, cw_ref = ins
        xo_ref, proj_ref, ym_ref, conv_ref = outs
        glu_ext, conv_scr = scs
        i = pl.program_id(0)
        xv = x_ref[...]
        sh, sc, gt = mod_ref[0:1, :], mod_ref[1:2, :], mod_ref[2:3, :]
        r = lax.rsqrt(_rowmean(xv * xv) + EPS)
        hb = ((xv * r * g_ref[0:1, :]) * (1.0 + sc) + sh).astype(BF16)
        for j in range(NDEV):
            proj_ref[:, j * MB:(j + 1) * MB] = _dot(hb, wmi_ref[j])
        u = proj_ref[:, 0:WA]
        v0 = proj_ref[:, WA:2 * WA]
        a = proj_ref[:, 2 * WA:3 * WA]
        g = proj_ref[:, 3 * WA:4 * WA]
        vh, _ = _layer_norm_stats(v0)
        vb = (vh * v_ref[0:1, :] + v_ref[1:2, :]).astype(BF16)
        wm = _masked_spatial(ws_ref)
        lane_head = lax.broadcasted_iota(jnp.int32, (CHUNK, WA), 1) >> 6
        ya = []
        for q in range(tm // CHUNK):
            z = _spatial_gate(wm, vb[q * CHUNK:(q + 1) * CHUNK, :], lane_head) + bias_ref[...]
            ya.append(u[q * CHUNK:(q + 1) * CHUNK, :] * z)
        ya = jnp.concatenate(ya, axis=0)
        glu = a * _sigmoid(g)

        @pl.when(i == 0)
        def _():
            glu_ext[:, HALO + tm:HALO + ext_rows, :] = jnp.zeros((4, ext_rows - tm, 128), F32)

        @pl.when(i % tps == 0)
        def _():
            glu_ext[:, 0:HALO, :] = jnp.zeros((4, HALO, 128), F32)

        _to_slabs(glu_ext, HALO, glu)
        conv = _tap_sum(glu_ext, conv_scr, cw_ref, v_ref[2:3, :], tm, lambda k: HALO - (CONV_K - 1) + k)
        conv_ref[...] = conv
        glu_ext[:, 0:HALO, :] = glu_ext[:, tm:tm + HALO, :]
        ch, _ = _layer_norm_stats(conv)
        cn = ch * v_ref[3:4, :] + v_ref[4:5, :]
        yb = cn * _sigmoid(cn)
        pa = ya * lax.rsqrt(_rowmean(ya * ya) + EPS) * v_ref[5:6, :]
        pb = yb * lax.rsqrt(_rowmean(yb * yb) + EPS) * v_ref[6:7, :]
        ycat = jnp.concatenate([pa, pb], axis=1).astype(BF16)
        ym = _dot(ycat, wmo_ref[...])
        ym_ref[...] = ym
        rm = lax.rsqrt(_rowmean(ym * ym) + EPS)
        xo_ref[...] = xv + gt * (ym * rm * g_ref[1:2, :])

    tile = pl.BlockSpec((tm, D), lambda i: (i, 0))
    return _call(
        core, name=name, grid=(nt,), jobs=jobs,
        in_specs=[tile, pl.BlockSpec((None, 8, D), lambda i: (i // tps, 0, 0)), _const_spec((8, D)),
                  _const_spec((NDEV, D, MB)), _const_spec((D, D)), _const_spec((8, WA)),
                  _const_spec((NHEAD, CHUNK, CHUNK)), _const_spec((CHUNK, WA)), _const_spec((32, WA))],
        out_specs=[tile, pl.BlockSpec((tm, 4 * WA), lambda i: (i, 0)), tile, pl.BlockSpec((tm, WA), lambda i: (i, 0))],
        out_shape=[jax.ShapeDtypeStruct((T, D), F32), jax.ShapeDtypeStruct((T, 4 * WA), F32),
                   jax.ShapeDtypeStruct((T, D), F32), jax.ShapeDtypeStruct((T, WA), F32)],
        scratch=[pltpu.VMEM((4, HALO + ext_rows, 128), F32), pltpu.VMEM((4, ext_rows, 128), F32)],
        args=[x, mod, gvec, w_mi, w_mo, v512, ws, bias_full, cw])


def _mixer_bwd_a(dxo, ym, proj, conv, mod, gvec, w_mo, v512, ws, bias_full, esel, tm, name, jobs=()):
    T = dxo.shape[0]
    nt = T // tm
    nb = mod.shape[0]
    tps = nt // nb

    def core(ins, outs, scs):
        dxo_ref, ym_ref, proj_ref, conv_ref, mod_ref, g_ref, wmo_ref, v_ref, ws_ref, bias_ref, e_ref = ins
        dpart_ref, dymb_ref, ycat_ref, mg_ref, vg_ref, v5g_ref, gws_ref, gbs_ref = outs
        (dbs_acc,) = scs
        i = pl.program_id(0)
        dxo_v = dxo_ref[...]
        ymv = ym_ref[...]
        gt = mod_ref[2:3, :]
        gpost = g_ref[1:2, :]
        rm = lax.rsqrt(_rowmean(ymv * ymv) + EPS)
        ymh = ymv * rm
        d_gt = _colsum(dxo_v * (ymh * gpost))
        dpm = gt * dxo_v
        d_gpost = _colsum(dpm * ymh)
        dymh = dpm * gpost
        dym = (rm * (dymh - ymh * _rowmean(dymh * ymh))).astype(BF16)
        dymb_ref[...] = dym
        dycat = _dot_nt(dym, wmo_ref[...])
        u = proj_ref[:, 0:WA]
        v0 = proj_ref[:, WA:2 * WA]
        vh, rv = _layer_norm_stats(v0)
        vb = (vh * v_ref[0:1, :] + v_ref[1:2, :]).astype(BF16)
        wm = _masked_spatial(ws_ref)
        lane_head = lax.broadcasted_iota(jnp.int32, (CHUNK, WA), 1) >> 6
        zs = []
        for q in range(tm // CHUNK):
            zs.append(_spatial_gate(wm, vb[q * CHUNK:(q + 1) * CHUNK, :], lane_head) + bias_ref[...])
        z = jnp.concatenate(zs, axis=0)
        ya = u * z
        ra = lax.rsqrt(_rowmean(ya * ya) + EPS)
        yah = ya * ra
        ch, rc = _layer_norm_stats(conv_ref[...])
        cn = ch * v_ref[3:4, :] + v_ref[4:5, :]
        sg = _sigmoid(cn)
        yb = cn * sg
        rb = lax.rsqrt(_rowmean(yb * yb) + EPS)
        ybh = yb * rb
        ycat_ref[...] = jnp.concatenate([yah * v_ref[5:6, :], ybh * v_ref[6:7, :]], axis=1).astype(BF16)
        dpa = dycat[:, 0:WA]
        dpb = dycat[:, WA:2 * WA]
        d_goa = _colsum(dpa * yah)
        d_gob = _colsum(dpb * ybh)
        dyah = dpa * v_ref[5:6, :]
        dybh = dpb * v_ref[6:7, :]
        dya = ra * (dyah - yah * _rowmean(dyah * yah))
        dyb = rb * (dybh - ybh * _rowmean(dybh * ybh))
        dpart_ref[:, 0:WA] = dya * z
        dz = dya * u

        @pl.when(i == 0)
        def _():
            gws_ref[...] = jnp.zeros((NHEAD, CHUNK, CHUNK), F32)
            dbs_acc[...] = jnp.zeros((CHUNK, WA), F32)
            vg_ref[...] = jnp.zeros((8, D), F32)
            v5g_ref[...] = jnp.zeros((8, WA), F32)

        dvs = []
        for q in range(tm // CHUNK):
            dz_q = dz[q * CHUNK:(q + 1) * CHUNK, :]
            vb_q = vb[q * CHUNK:(q + 1) * CHUNK, :]
            dbs_acc[...] += dz_q
            dzb = dz_q.astype(BF16)
            dv_q = jnp.zeros((CHUNK, WA), F32)
            for hd in range(NHEAD):
                dv_q = jnp.where(lane_head == hd, _dot_tn(wm[hd], dzb), dv_q)
                dz_hd = jnp.where(lane_head == hd, dz_q, 0.0).astype(BF16)
                gws_ref[hd] += _dot_nt(dz_hd, vb_q)
            dvs.append(dv_q)
        dv = jnp.concatenate(dvs, axis=0)
        d_gng = _colsum(dv * vh)
        d_gnb = _colsum(dv)
        dvh = dv * v_ref[0:1, :]
        dpart_ref[:, WA:2 * WA] = rv * (dvh - _rowmean(dvh) - vh * _rowmean(dvh * vh))
        dcn = dyb * (sg * (1.0 + cn * (1.0 - sg)))
        d_cng = _colsum(dcn * ch)
        d_cnb = _colsum(dcn)
        dch = dcn * v_ref[3:4, :]
        dconv = rc * (dch - _rowmean(dch) - ch * _rowmean(dch * ch))
        dpart_ref[:, 2 * WA:3 * WA] = dconv
        dpart_ref[:, 3 * WA:4 * WA] = jnp.zeros((tm, WA), F32)
        d_cb = _colsum(dconv)

        @pl.when(i % tps == 0)
        def _():
            mg_ref[...] = jnp.zeros((8, D), F32)

        mg_ref[2:3, :] += d_gt
        vg_ref[1:2, :] += d_gpost
        v5g_ref[0:1, :] += d_gng
        v5g_ref[1:2, :] += d_gnb
        v5g_ref[2:3, :] += d_cb
        v5g_ref[3:4, :] += d_cng
        v5g_ref[4:5, :] += d_cnb
        v5g_ref[5:6, :] += d_goa
        v5g_ref[6:7, :] += d_gob

        @pl.when(i == nt - 1)
        def _():
            row = lax.broadcasted_iota(jnp.int32, (CHUNK, CHUNK), 0)
            col = lax.broadcasted_iota(jnp.int32, (CHUNK, CHUNK), 1)
            for hd in range(NHEAD):
                gws_ref[hd] = jnp.where(col <= row, gws_ref[hd], 0.0)
            gbs_ref[...] = lax.dot_general(e_ref[...], dbs_acc[...], (((1,), (1,)), ((), ())),
                                           precision=lax.Precision.HIGHEST, preferred_element_type=F32)

    tile = pl.BlockSpec((tm, D), lambda i: (i, 0))
    ptile = pl.BlockSpec((tm, 4 * WA), lambda i: (i, 0))
    return _call(
        core, name=name, grid=(nt,), jobs=jobs,
        in_specs=[tile, tile, pl.BlockSpec((tm, 2 * WA), lambda i: (i, 0)), pl.BlockSpec((tm, WA), lambda i: (i, 0)),
                  pl.BlockSpec((None, 8, D), lambda i: (i // tps, 0, 0)), _const_spec((8, D)), _const_spec((D, D)),
                  _const_spec((8, WA)), _const_spec((NHEAD, CHUNK, CHUNK)), _const_spec((CHUNK, WA)),
                  _const_spec((8, WA))],
        out_specs=[ptile, tile, tile, pl.BlockSpec((None, 8, D), lambda i: (i // tps, 0, 0)),
                   pl.BlockSpec((8, D), lambda i: (0, 0)), pl.BlockSpec((8, WA), lambda i: (0, 0)),
                   pl.BlockSpec((NHEAD, CHUNK, CHUNK), lambda i: (0, 0, 0)), pl.BlockSpec((8, CHUNK), lambda i: (0, 0))],
        out_shape=[jax.ShapeDtypeStruct((T, 4 * WA), F32), jax.ShapeDtypeStruct((T, D), BF16),
                   jax.ShapeDtypeStruct((T, D), BF16), jax.ShapeDtypeStruct((nb, 8, D), F32),
                   jax.ShapeDtypeStruct((8, D), F32), jax.ShapeDtypeStruct((8, WA), F32),
                   jax.ShapeDtypeStruct((NHEAD, CHUNK, CHUNK), F32), jax.ShapeDtypeStruct((8, CHUNK), F32)],
        scratch=[pltpu.VMEM((CHUNK, WA), F32)],
        args=[dxo, ym, proj, conv, mod, gvec, w_mo, v512, ws, bias_full, esel])


def _mixer_bwd_b(dxo, x, dpart, proj, mod, gvec, w_mi, cw, tm, name, jobs=()):
    T = x.shape[0]
    nt = T // tm
    nb = mod.shape[0]
    tps = nt // nb
    hpt = tm // HALO
    nh = T // HALO
    off = HALO - (CONV_K - 1)
    p = _pitch(tm)
    ext_rows = 8 * p

    def core(ins, outs, scs):
        dxo_ref, x_ref, dpart_ref, dnext_ref, ag_ref, halo_ref, mod_ref, g_ref, wmi_ref, cw_ref = ins
        dx_ref, dproj_ref, hb_ref, mg_ref, vg_ref, dcw_ref = outs
        glu_ext, dconv_ext, dglu_scr, dcw_acc = scs
        i = pl.program_id(0)
        first = i % tps == 0
        last = i % tps == tps - 1
        a = ag_ref[:, 0:WA]
        g = ag_ref[:, WA:2 * WA]
        sgg = _sigmoid(g)

        @pl.when(i == 0)
        def _():
            glu_ext[:, HALO + tm:HALO + ext_rows, :] = jnp.zeros((4, ext_rows - tm, 128), F32)
            dconv_ext[:, HALO + tm:HALO + ext_rows, :] = jnp.zeros((4, ext_rows - tm, 128), F32)
            dcw_acc[...] = jnp.zeros((32, 8, WA), F32)
            vg_ref[...] = jnp.zeros((8, D), F32)

        _to_slabs(glu_ext, 0, jnp.where(first, 0.0, halo_ref[:, 0:WA] * _sigmoid(halo_ref[:, WA:2 * WA])))
        _to_slabs(glu_ext, HALO, a * sgg)
        _to_slabs(dconv_ext, 0, dpart_ref[:, 2 * WA:3 * WA])
        _to_slabs(dconv_ext, tm, jnp.where(last, 0.0, dnext_ref[...]))
        sub = lax.broadcasted_iota(jnp.int32, (8, 128), 0)
        for s in range(4):
            accs = [jnp.zeros((8, 128), F32)] * CONV_K
            for v in range(p):
                dc = jnp.where(v + p * sub < tm, dconv_ext[s, pl.ds(v, 8, stride=p), :], 0.0)
                for k in range(CONV_K):
                    accs[k] = accs[k] + dc * glu_ext[s, pl.ds(v + off + k, 8, stride=p), :]
            for k in range(CONV_K):
                dcw_acc[k, :, _lanes(s)] += accs[k]
        dglu = _tap_sum(dconv_ext, dglu_scr, cw_ref, jnp.zeros((1, WA), F32), tm, lambda k: (CONV_K - 1) - k)

        @pl.when(i == nt - 1)
        def _():
            for k in range(CONV_K):
                dcw_ref[k:k + 1, :] = jnp.sum(dcw_acc[k], axis=0, keepdims=True)
            dcw_ref[CONV_K:32, :] = jnp.zeros((32 - CONV_K, WA), F32)

        da = dglu * sgg
        dgg = dglu * a * (sgg * (1.0 - sgg))
        dproj_ref[:, 0:2 * WA] = dpart_ref[:, 0:2 * WA].astype(BF16)
        dproj_ref[:, 2 * WA:3 * WA] = da.astype(BF16)
        dproj_ref[:, 3 * WA:4 * WA] = dgg.astype(BF16)
        dh = jnp.zeros((tm, D), F32)
        for j in range(NDEV):
            dh = dh + _dot_nt(dproj_ref[:, j * MB:(j + 1) * MB], wmi_ref[j])
        xv = x_ref[...]
        sc, sh = mod_ref[1:2, :], mod_ref[0:1, :]
        gpre = g_ref[0:1, :]
        r = lax.rsqrt(_rowmean(xv * xv) + EPS)
        xh = xv * r
        n = xh * gpre
        hb_ref[...] = (n * (1.0 + sc) + sh).astype(BF16)
        d_sc = _colsum(dh * n)
        d_sh = _colsum(dh)
        dn = dh * (1.0 + sc)
        d_gpre = _colsum(dn * xh)
        dxh = dn * gpre
        dx_ref[...] = dxo_ref[...] + r * (dxh - xh * _rowmean(dxh * xh))

        @pl.when(first)
        def _():
            mg_ref[...] = jnp.zeros((8, D), F32)

        mg_ref[0:1, :] += d_sh
        mg_ref[1:2, :] += d_sc
        vg_ref[0:1, :] += d_gpre

    tile = pl.BlockSpec((tm, D), lambda i: (i, 0))
    return _call(
        core, name=name, grid=(nt,), jobs=jobs,
        in_specs=[tile, tile, pl.BlockSpec((tm, 4 * WA), lambda i: (i, 0)),
                  pl.BlockSpec((HALO, WA), lambda i: (jnp.minimum((i + 1) * hpt, nh - 1), 2)),
                  pl.BlockSpec((tm, 2 * WA), lambda i: (i, 1)),
                  pl.BlockSpec((HALO, 2 * WA), lambda i: (jnp.maximum(i * hpt - 1, 0), 1)),
                  pl.BlockSpec((None, 8, D), lambda i: (i // tps, 0, 0)), _const_spec((8, D)),
                  _const_spec((NDEV, D, MB)), _const_spec((32, WA))],
        out_specs=[tile, pl.BlockSpec((tm, 4 * WA), lambda i: (i, 0)), tile,
                   pl.BlockSpec((None, 8, D), lambda i: (i // tps, 0, 0)), pl.BlockSpec((8, D), lambda i: (0, 0)),
                   pl.BlockSpec((32, WA), lambda i: (0, 0))],
        out_shape=[jax.ShapeDtypeStruct((T, D), F32), jax.ShapeDtypeStruct((T, 4 * WA), BF16),
                   jax.ShapeDtypeStruct((T, D), BF16), jax.ShapeDtypeStruct((nb, 8, D), F32),
                   jax.ShapeDtypeStruct((8, D), F32), jax.ShapeDtypeStruct((32, WA), F32)],
        scratch=[pltpu.VMEM((4, HALO + ext_rows, 128), F32), pltpu.VMEM((4, HALO + ext_rows, 128), F32),
                 pltpu.VMEM((4, ext_rows, 128), F32), pltpu.VMEM((32, 8, WA), F32)],
        args=[dxo, x, dpart, dpart, proj, proj, mod, gvec, w_mi, cw])


def _loss_head(y, target, tm, name):
    T = y.shape[0]

    def core(ins, outs, _):
        y_ref, t_ref = ins
        dy_ref, loss_ref = outs

        @pl.when(pl.program_id(0) == 0)
        def _():
            loss_ref[...] = jnp.zeros((8, D), F32)

        err = y_ref[...] - t_ref[...]
        dy_ref[...] = err * (1.0 / D)
        part = jnp.sum(_rowmean(err * err), axis=0, keepdims=True)
        loss_ref[...] += HALF * part

    tile = pl.BlockSpec((tm, D), lambda i: (i, 0))
    return _call(
        core, name=name, grid=(T // tm,),
        in_specs=[tile, tile], out_specs=[tile, pl.BlockSpec((8, D), lambda i: (0, 0))],
        out_shape=[jax.ShapeDtypeStruct((T, D), F32), jax.ShapeDtypeStruct((8, D), F32)],
        args=[y, target])[0]


def _grad_chip(a, b, a_spec, b_spec, prod_shape, half, name, jobs=()):
    steps = 8 if half is None else 4
    R = prod_shape[0] if half is None else half
    C = prod_shape[1]

    def core(ins, outs, scs):
        a_ref, b_ref = ins
        (o_ref,) = outs
        own, snd, rcv, ssem, rsem, lsem = scs
        s = pl.program_id(0)
        c = lax.axis_index("c")
        me = _me()
        sib = _flip(me, (0, 0, 1))
        prod = _dot_tn(a_ref[...], b_ref[...]).astype(BF16)
        if half is None:
            q = s // 2

            @pl.when(s % 2 == c)
            def _():
                own[q] = prod

            @pl.when(s % 2 != c)
            def _():
                snd[q] = prod
                _remote(snd.at[q], rcv.at[q], ssem.at[q], rsem.at[q], sib).start()
        else:
            lo = prod[0:half, :]
            hi = prod[half:2 * half, :]
            own[s] = jnp.where(c == 0, lo, hi)
            snd[s] = jnp.where(c == 0, hi, lo)
            _remote(snd.at[s], rcv.at[s], ssem.at[s], rsem.at[s], sib).start()

        @pl.when(s == steps - 1)
        def _():
            for q4 in range(4):
                cp = _remote(snd.at[q4], rcv.at[q4], ssem.at[q4], rsem.at[q4], sib)
                cp.wait_recv()
                cp.wait_send()
                snd[q4] = (own[q4].astype(F32) + rcv[q4].astype(F32)).astype(BF16)
            out = pltpu.make_async_copy(snd, o_ref, lsem)
            out.start()
            out.wait()

    return _call(
        core, name=name, grid=(steps,), jobs=jobs, in_specs=[a_spec, b_spec], out_specs=[HBM],
        out_shape=[jax.ShapeDtypeStruct((4, R, C), BF16)],
        scratch=[pltpu.VMEM((4, R, C), BF16), pltpu.VMEM((4, R, C), BF16), pltpu.VMEM((4, R, C), BF16),
                 pltpu.SemaphoreType.DMA((4,)), pltpu.SemaphoreType.DMA((4,)), pltpu.SemaphoreType.DMA],
        args=[a, b])


def _grad_w_in(dg, hb, name, jobs=()):
    T = hb.shape[0]
    return _grad_chip(dg, hb, pl.BlockSpec((None, T, FBP), lambda s: (s, 0, 0)), _const_spec((T, D)),
                      (FBP, D), None, name, jobs)


def _grad_w_out(act, dyb, name, jobs=()):
    T = dyb.shape[0]
    return _grad_chip(act, dyb, pl.BlockSpec((None, T, FBP), lambda s: (s, 0, 0)), _const_spec((T, D)),
                      (FBP, D), FO, name, jobs)


def _grad_w_mi(hb, dproj, name, jobs=()):
    T = hb.shape[0]
    return _grad_chip(hb, dproj, _const_spec((T, D)), pl.BlockSpec((T, MB), lambda s: (0, s)),
                      (D, MB), None, name, jobs)


def _grad_w_mo(ycat, dym, name, jobs=()):
    T = ycat.shape[0]
    return _grad_chip(ycat, dym, pl.BlockSpec((T, 2 * MO), lambda s: (0, s)), _const_spec((T, D)),
                      (2 * MO, D), MO, name, jobs)


def _adamw_math(w, g, m, v):
    m2 = ADAM_B1 * m + (1.0 - ADAM_B1) * g
    v2 = ADAM_B2 * v + (1.0 - ADAM_B2) * (g * g)
    m_hat = m2 / (1.0 - ADAM_B1 ** ADAM_STEP)
    v_hat = v2 / (1.0 - ADAM_B2 ** ADAM_STEP)
    delta = -ADAM_LR * (m_hat / (jnp.sqrt(v_hat) + ADAM_EPS) + ADAM_WD * w)
    return delta, m2, v2


def _adamw_reduce(parts, w, m, v, tr, name):
    R, C = w.shape

    def core(ins, outs, _):
        p_ref, w_ref, m_ref, v_ref = ins
        g_ref, d_ref, m2_ref, v2_ref = outs
        g = p_ref[0].astype(F32)
        for s in range(1, 4):
            g = g + p_ref[s].astype(F32)
        g_ref[...] = g
        d_ref[...], m2_ref[...], v2_ref[...] = _adamw_math(w_ref[...], g, m_ref[...], v_ref[...])

    blk = pl.BlockSpec((tr, C), lambda i: (i, 0))
    return _call(
        core, name=name, grid=(R // tr,),
        in_specs=[pl.BlockSpec((4, tr, C), lambda i: (0, i, 0)), blk, blk, blk],
        out_specs=[blk, blk, blk, blk], out_shape=[jax.ShapeDtypeStruct((R, C), F32)] * 4,
        args=[parts, w, m, v])[0]


def _adamw_ada(sc_all, dd, w, m, v, tr, name):
    R, C = w.shape

    def core(ins, outs, _):
        sc_ref, dd_ref, w_ref, m_ref, v_ref = ins
        g_ref, d_ref, m2_ref, v2_ref = outs
        g = _dot_tn(sc_ref[...].astype(BF16), dd_ref[...].astype(BF16))
        g_ref[...] = g
        d_ref[...], m2_ref[...], v2_ref[...] = _adamw_math(w_ref[...], g, m_ref[...], v_ref[...])

    blk = pl.BlockSpec((tr, C), lambda i: (i, 0))
    return _call(
        core, name=name, grid=(R // tr,),
        in_specs=[pl.BlockSpec((64, tr), lambda i: (0, i)), pl.BlockSpec((64, C), lambda i: (0, 0)), blk, blk, blk],
        out_specs=[blk, blk, blk, blk], out_shape=[jax.ShapeDtypeStruct((R, C), F32)] * 4,
        args=[sc_all, dd, w, m, v])[0]


def _adamw_small(gathered, plain, grads, wmv, emit, name):
    nw = len(grads)
    ng, npl, ne = len(gathered), len(plain), len(emit)

    def core(ins, outs, _):
        srcs = []
        for a in range(ng):
            s = ins[a][0]
            for dev in range(1, NDEV):
                s = s + ins[a][dev]
            srcs.append(s)
        srcs += [ins[ng + a][...] for a in range(npl)]
        w_refs = ins[ng + npl:]
        for e, a in enumerate(emit):
            outs[e][...] = srcs[a]
        for t in range(nw):
            src, row = grads[t]
            g = srcs[src] if row is None else srcs[src][row:row + 1, :]
            w_ref, m_ref, v_ref = w_refs[3 * t:3 * t + 3]
            g_ref, d_ref, m2_ref, v2_ref = outs[ne + 4 * t:ne + 4 * t + 4]
            g_ref[...] = g
            d_ref[...], m2_ref[...], v2_ref[...] = _adamw_math(w_ref[...], g, m_ref[...], v_ref[...])

    out_shape = [jax.ShapeDtypeStruct(gathered[a].shape[1:], F32) for a in emit]
    for t in range(nw):
        out_shape += [jax.ShapeDtypeStruct(wmv[3 * t].shape, F32)] * 4
    return _call(
        core, name=name, grid=(), in_specs=[VM] * (ng + npl + 3 * nw), out_specs=[VM] * (ne + 4 * nw),
        out_shape=out_shape, args=list(gathered) + list(plain) + list(wmv))[0]


def _ada_fwd(c_pad, w_ada, b_cols, cw_pad, jobs=()):
    def core(ins, outs, scs):
        c_ref, w_ref, b_ref, cwp_ref = ins
        ada_ref, sc_ref, cw_ref = outs
        cbuf, send_buf, ssem, rsem = scs
        me = _me()
        mi = _lin(me)
        cbuf[mi] = c_ref[...]
        cw_ref[mi] = cwp_ref[...]
        peers = [_flip(me, f) for f in FLIPS]
        first = []
        for k, p in enumerate(peers):
            first.append(_remote(cbuf.at[mi], cbuf.at[mi], ssem.at[k], rsem.at[k], p))
            first.append(_remote(cw_ref.at[mi], cw_ref.at[mi], ssem.at[7 + k], rsem.at[7 + k], p))
        for cp in first:
            cp.start()
        for k, p in enumerate(peers):
            pi = _lin(p)
            _remote(cbuf.at[pi], cbuf.at[pi], ssem.at[k], rsem.at[k], p).wait_recv()
            _remote(cw_ref.at[pi], cw_ref.at[pi], ssem.at[7 + k], rsem.at[7 + k], p).wait_recv()
        c_all = cbuf[...].reshape(8 * 8, D)
        sc = c_all * _sigmoid(c_all)
        sc_ref[...] = sc
        res = _dot(sc.astype(BF16), w_ref[...].astype(BF16)) + b_ref[...]
        send_buf[...] = res.reshape(8, 8, ADA_B)
        ada_ref[mi] = send_buf[mi]
        second = []
        for k, p in enumerate(peers):
            second.append(_remote(send_buf.at[_lin(p)], ada_ref.at[mi], ssem.at[14 + k], rsem.at[14 + k], p))
        for cp in second:
            cp.start()
        for k, p in enumerate(peers):
            _remote(send_buf.at[mi], ada_ref.at[_lin(p)], ssem.at[14 + k], rsem.at[14 + k], p).wait_recv()
        for cp in first + second:
            cp.wait_send()

    return _call(
        core, name="ada_fwd", grid=(), jobs=jobs, in_specs=[VM, VM, VM, VM], out_specs=[VM, VM, VM],
        out_shape=[jax.ShapeDtypeStruct((8, 8, ADA_B), F32), jax.ShapeDtypeStruct((64, D), F32),
                   jax.ShapeDtypeStruct((8, 32, 64), F32)],
        scratch=[pltpu.VMEM((8, 8, D), F32), pltpu.VMEM((8, 8, ADA_B), F32),
                 pltpu.SemaphoreType.DMA((21,)), pltpu.SemaphoreType.DMA((21,))],
        args=[c_pad, w_ada, b_cols, cw_pad])


def _ada_bwd(dada):
    def body(d_ref, dd_ref, gb_ref, rbuf, ssem, rsem):
        me = _me()
        mi = _lin(me)
        peers = [_flip(me, f) for f in FLIPS]
        rbuf[mi] = d_ref[mi]
        first = []
        for k, p in enumerate(peers):
            first.append(_remote(d_ref.at[_lin(p)], rbuf.at[mi], ssem.at[k], rsem.at[k], p))
        for cp in first:
            cp.start()
        for k, p in enumerate(peers):
            _remote(d_ref.at[mi], rbuf.at[_lin(p)], ssem.at[k], rsem.at[k], p).wait_recv()
        dd = rbuf[...].reshape(64, ADA_B)
        dd_ref[...] = dd
        gb_ref[mi] = jnp.broadcast_to(_colsum(dd), (8, ADA_B))
        second = []
        for k, p in enumerate(peers):
            second.append(_remote(gb_ref.at[mi], gb_ref.at[mi], ssem.at[7 + k], rsem.at[7 + k], p))
        for cp in second:
            cp.start()
        for k, p in enumerate(peers):
            pi = _lin(p)
            _remote(gb_ref.at[pi], gb_ref.at[pi], ssem.at[7 + k], rsem.at[7 + k], p).wait_recv()
        for cp in first + second:
            cp.wait_send()

    return pl.pallas_call(
        body,
        name="ada_bwd",
        in_specs=[VM],
        out_specs=[VM, VM],
        out_shape=[jax.ShapeDtypeStruct((64, ADA_B), F32), jax.ShapeDtypeStruct((8, 8, ADA_B), F32)],
        scratch_shapes=[
            pltpu.VMEM((8, 8, ADA_B), F32),
            pltpu.SemaphoreType.DMA((14,)),
            pltpu.SemaphoreType.DMA((14,)),
        ],
        compiler_params=pltpu.CompilerParams(vmem_limit_bytes=VMEM_LIMIT),
    )(dada)


SMALL_D = ("g_pre_f1", "g_post_f1", "g_pre_m", "g_post_m", "g_pre_f2", "g_post_f2")
SMALL_W = ("gmlp_norm_g", "gmlp_norm_b", "conv_b", "conv_norm_g", "conv_norm_b", "g_out_a", "g_out_b")


def kernel(x, c, w_ada, b_ada, g_pre_f1, g_post_f1, w_f1_in, w_f1_out, g_pre_m, g_post_m, w_mix_in, gmlp_norm_g, gmlp_norm_b, w_spatial, b_spatial, conv_w, conv_b, conv_norm_g, conv_norm_b, g_out_a, g_out_b, w_mix_out, g_pre_f2, g_post_f2, w_f2_in, w_f2_out, loss_target, m_w_ada, m_b_ada, m_g_pre_f1, m_g_post_f1, m_w_f1_in, m_w_f1_out, m_g_pre_m, m_g_post_m, m_w_mix_in, m_gmlp_norm_g, m_gmlp_norm_b, m_w_spatial, m_b_spatial, m_conv_w, m_conv_b, m_conv_norm_g, m_conv_norm_b, m_g_out_a, m_g_out_b, m_w_mix_out, m_g_pre_f2, m_g_post_f2, m_w_f2_in, m_w_f2_out, v_w_ada, v_b_ada, v_g_pre_f1, v_g_post_f1, v_w_f1_in, v_w_f1_out, v_g_pre_m, v_g_post_m, v_w_mix_in, v_gmlp_norm_g, v_gmlp_norm_b, v_w_spatial, v_b_spatial, v_conv_w, v_conv_b, v_conv_norm_g, v_conv_norm_b, v_g_out_a, v_g_out_b, v_w_mix_out, v_g_pre_f2, v_g_post_f2, v_w_f2_in, v_w_f2_out):
    given = dict(locals())
    bl, seq, _ = x.shape
    T = bl * seq
    tm = min(256, seq // 2)
    mi = _lin((lax.axis_index("x"), lax.axis_index("y"), lax.axis_index("c")))

    def shard_in(w):
        return jnp.pad(w[0].T.astype(BF16), ((0, FBP - FB), (0, 0)))

    zpad = jnp.zeros((FBP - FB, D), BF16)
    g_f1 = _Gather([shard_in(w_f1_in), w_f1_out[0].astype(BF16)], ("rows", "out"), zpad)
    g_mx = _Gather([w_mix_in[0].astype(BF16), w_mix_out[0].astype(BF16), w_f2_out[0].astype(BF16)],
                   ("rows", "rows", "out"), zpad)
    g_f2 = _Gather([shard_in(w_f2_in)], ("rows",), zpad, late_mid=True)

    c_pad = jnp.pad(c, ((0, 8 - bl), (0, 0)))
    b_cols = lax.dynamic_slice(b_ada, (0, mi * ADA_B), (1, ADA_B))
    cw_pad = jnp.pad(conv_w[0], ((0, 1), (0, 0)))
    (ada_blk, sc_all, cw_all), ((wi1, wo1),) = _ada_fwd(c_pad, w_ada[0], b_cols, cw_pad, jobs=[g_f1])
    ada = ada_blk[:, 0:bl, :].transpose(1, 0, 2).reshape(bl, 9, D)
    pad5 = jnp.zeros((bl, 5, D), F32)
    mod1 = jnp.concatenate([ada[:, 0:3], pad5], axis=1)
    mod2 = jnp.concatenate([ada[:, 3:6], pad5], axis=1)
    mod3 = jnp.concatenate([ada[:, 6:9], pad5], axis=1)
    cw_full = cw_all.transpose(1, 0, 2).reshape(32, WA)

    zrow = jnp.zeros((1, D), F32)
    gv1 = jnp.concatenate([g_pre_f1, g_post_f1] + [zrow] * 6, axis=0)
    gvm = jnp.concatenate([g_pre_m, g_post_m] + [zrow] * 6, axis=0)
    gv2 = jnp.concatenate([g_pre_f2, g_post_f2] + [zrow] * 6, axis=0)
    v512 = jnp.concatenate([gmlp_norm_g, gmlp_norm_b, conv_b, conv_norm_g, conv_norm_b, g_out_a, g_out_b,
                            jnp.zeros((1, WA), F32)], axis=0)
    ws = w_spatial[0]
    bias_full = jnp.repeat(b_spatial[0].T, HD, axis=1)
    esel = (lax.broadcasted_iota(jnp.int32, (8, WA), 1) // HD == lax.broadcasted_iota(jnp.int32, (8, WA), 0)).astype(F32)

    x0 = x.reshape(T, D)
    (x1, gu1, y1), ((wmi, wmo, wo2),) = _ffn_fwd(x0, mod1, gv1, wi1, wo1, tm, "ffn1_fwd", jobs=[g_mx])
    wmo = wmo.reshape(D, D)
    (x2, proj, ym, conv), ((wi2,),) = _mixer_fwd(x1, mod2, gvm, wmi, wmo, v512, ws, bias_full, cw_full, tm, "mixer_fwd", jobs=[g_f2])
    (x3, gu2, y2), _ = _ffn_fwd(x2, mod3, gv2, wi2, wo2, tm, "ffn2_fwd")
    dx3, loss_blk = _loss_head(x3, loss_target.reshape(T, D), tm, "loss_head")

    (dx2, dg2, act2, hb2, dyb2, mg3, vg3), _ = _ffn_bwd(dx3, x2, y2, gu2, mod3, gv2, wi2, wo2, tm, "ffn2_bwd")
    (g_wi2,), _ = _grad_w_in(dg2, hb2, "ffn2_gw_in")
    (g_wo2,), _ = _grad_w_out(act2, dyb2, "ffn2_gw_out")
    (dpart, dymb, ycat, mg2a, vgma, v5g, gws, gbs), ((p_wi2,),) = _mixer_bwd_a(
        dx2, ym, proj, conv, mod2, gvm, wmo, v512, ws, bias_full, esel, tm, "mixer_bwd_a",
        jobs=[_ChipScatter([g_wi2])])
    (dx1, dproj, hbm, mg2b, vgmb, dcw), ((p_wo2,),) = _mixer_bwd_b(
        dx2, x1, dpart, proj, mod2, gvm, wmi, cw_full, tm, "mixer_bwd_b", jobs=[_ChipScatter([g_wo2])])
    (g_wmi,), _ = _grad_w_mi(hbm, dproj, "mixer_gw_in")
    (g_wmo,), _ = _grad_w_mo(ycat, dymb, "mixer_gw_out")
    p2 = jnp.concatenate([v5g, dcw], axis=0)
    (dx0, dg1, act1, hb1, dyb1, mg1, vg1), ((p_wmi, p_wmo), (a2, a3, a4)) = _ffn_bwd(
        dx1, x0, y1, gu1, mod1, gv1, wi1, wo1, tm, "ffn1_bwd",
        jobs=[_ChipScatter([g_wmi, g_wmo]), _AllGather([p2, gws, gbs])])
    (g_wo1,), _ = _grad_w_out(act1, dyb1, "ffn1_gw_out")
    (g_wi1,), ((p_wo1,),) = _grad_w_in(dg1, hb1, "ffn1_gw_in", jobs=[_ChipScatter([g_wo1])])

    dada = jnp.concatenate([mg1[:, 0:3], mg2b[:, 0:2], mg2a[:, 2:3], mg3[:, 0:3]], axis=1)
    dada = dada.reshape(bl, NDEV, ADA_B).transpose(1, 0, 2)
    dada = jnp.pad(dada, ((0, 0), (0, 8 - bl), (0, 0)))
    dd_all, gb_all = _ada_bwd(dada)
    g_bada = gb_all[:, 0, :].reshape(1, 9 * D)

    p1 = jnp.concatenate([vg1[0:2], vgmb[0:1], vgma[1:2], vg3[0:2], loss_blk[0:1], zrow], axis=0)
    (p_wi1,), (a1,) = _call(None, name="tail_exchange", grid=(), in_specs=[], out_specs=[], out_shape=[], args=[],
                            jobs=[_ChipScatter([g_wi1]), _AllGather([p1])])[1]

    res = {}
    for nm, part in (("w_f1_in", p_wi1), ("w_f2_in", p_wi2)):
        quad = _adamw_reduce(part, given[nm][0].T, given["m_" + nm][0].T, given["v_" + nm][0].T, FO, "adamw_" + nm)
        res[nm] = tuple(t.T[None] for t in quad)
    for nm, part, tr in (("w_f1_out", p_wo1, FO), ("w_f2_out", p_wo2, FO), ("w_mix_in", p_wmi, 256), ("w_mix_out", p_wmo, MO)):
        quad = _adamw_reduce(part, given[nm][0], given["m_" + nm][0], given["v_" + nm][0], tr, "adamw_" + nm)
        res[nm] = tuple(t[None] for t in quad)
    quad = _adamw_ada(sc_all, dd_all, w_ada[0], m_w_ada[0], v_w_ada[0], 256, "adamw_w_ada")
    res["w_ada"] = tuple(t[None] for t in quad)

    small = SMALL_D + SMALL_W + ("w_spatial", "b_spatial", "b_ada")
    grads = [(0, r) for r in range(6)] + [(1, r) for r in range(7)] + [(2, None), (3, None), (4, None)]
    wmv = []
    for nm in small:
        for pre in ("", "m_", "v_"):
            wmv.append(given[pre + nm][0] if nm in ("w_spatial", "b_spatial") else given[pre + nm])
    outs = _adamw_small([a1, a2, a3, a4], [g_bada], grads, wmv, (0, 1), "adamw_small")
    loss = outs[0][6, 0]
    for t, nm in enumerate(small):
        quad = outs[2 + 4 * t:6 + 4 * t]
        res[nm] = tuple(q[None] for q in quad) if nm in ("w_spatial", "b_spatial") else tuple(quad)
    g_cw = lax.dynamic_slice(outs[1], (8, mi * 64), (32, 64))
    wmv = [jnp.pad(given[pre + "conv_w"][0], ((0, 1), (0, 0)), constant_values=1.0 if pre == "v_" else 0.0)
           for pre in ("", "m_", "v_")]
    quad = _adamw_small([], [g_cw], [(0, None)], wmv, (), "adamw_conv_w")
    res["conv_w"] = tuple(q[0:CONV_K][None] for q in quad)

    order = ["w_ada", "b_ada", "g_pre_f1", "g_post_f1", "w_f1_in", "w_f1_out", "g_pre_m", "g_post_m", "w_mix_in",
             "gmlp_norm_g", "gmlp_norm_b", "w_spatial", "b_spatial", "conv_w", "conv_b", "conv_norm_g", "conv_norm_b",
             "g_out_a", "g_out_b", "w_mix_out", "g_pre_f2", "g_post_f2", "w_f2_in", "w_f2_out"]
    out = [loss, dx0.reshape(bl, seq, D)]
    for k in range(4):
        out += [res[nm][k] for nm in order]
    return tuple(out)
```

```python
import jax
import jax.numpy as jnp
from jax import lax
from jax.experimental import pallas as pl
from jax.experimental.pallas import tpu as pltpu

F32 = jnp.float32
BF16 = jnp.bfloat16

D = 1024
DFF = 2816
NDEV = 8
FB = 2 * DFF // NDEV
FBP = 768
FO = DFF // NDEV
WA = 512
NHEAD = 8
HD = 64
CHUNK = 128
CONV_K = 31
HALO = 32
MB = 2 * (WA + WA) // NDEV
MO = D // NDEV
ADA_B = 9 * D // NDEV
EPS = 1e-6
HALF = 0.5

ADAM_LR = 0.001
ADAM_B1 = 0.9
ADAM_B2 = 0.999
ADAM_EPS = 1e-08
ADAM_WD = 0.01
ADAM_STEP = 10

VMEM_LIMIT = 56 * 1024 * 1024
MESH = pl.DeviceIdType.MESH
FLIPS = ((0, 0, 1), (1, 0, 0), (0, 1, 0), (1, 1, 0), (1, 0, 1), (0, 1, 1), (1, 1, 1))
CHIP_FLIPS = ((1, 0, 0), (0, 1, 0), (1, 1, 0))
HBM = pl.BlockSpec(memory_space=pl.ANY)
VM = pl.BlockSpec(memory_space=pltpu.VMEM)


def _dot(a, b):
    return lax.dot_general(a, b, (((1,), (0,)), ((), ())), preferred_element_type=F32)


def _dot_nt(a, b):
    return lax.dot_general(a, b, (((1,), (1,)), ((), ())), preferred_element_type=F32)


def _dot_tn(a, b):
    return lax.dot_general(a, b, (((0,), (0,)), ((), ())), preferred_element_type=F32)


def _rowmean(v):
    return jnp.mean(v, axis=-1, keepdims=True)


def _colsum(v):
    return jnp.sum(v, axis=0, keepdims=True)


def _sigmoid(v):
    return 1.0 / (1.0 + jnp.exp(-v))


def _const_spec(shape):
    nd = len(shape)
    return pl.BlockSpec(shape, lambda *_: (0,) * nd, pipeline_mode=pl.Buffered(1))


def _me():
    return lax.axis_index("x"), lax.axis_index("y"), lax.axis_index("c")


def _flip(me, f):
    return tuple(1 - v if b else v for v, b in zip(me, f))


def _lin(p):
    return 4 * p[0] + 2 * p[1] + p[2]


def _remote(src, dst, send_sem, recv_sem, dev):
    return pltpu.make_async_remote_copy(src_ref=src, dst_ref=dst, send_sem=send_sem, recv_sem=recv_sem,
                                        device_id=dev, device_id_type=MESH)


def _blk(kind, ref, p):
    if kind == "out":
        return ref.at[2 * p[0] + p[1], pl.ds(p[2] * FO, FO), :]
    return ref.at[_lin(p)]


class _Gather:
    def __init__(self, shards, kinds, zpad, late_mid=False):
        self.late_mid = late_mid
        self.kinds = kinds
        self.n = len(shards)
        self.ins = list(shards) + [zpad]
        self.out_shape = [jax.ShapeDtypeStruct((4, FBP, D) if k == "out" else (NDEV,) + s.shape, BF16)
                          for s, k in zip(shards, kinds)]
        self.n_out = sum(k == "out" for k in kinds)
        self.sems = [pltpu.SemaphoreType.DMA((7 * self.n,)), pltpu.SemaphoreType.DMA((7 * self.n,)),
                     pltpu.SemaphoreType.DMA((self.n + 4 * max(self.n_out, 1),))]

    def _first(self, ins, outs, sems):
        ssem, rsem, lsem = sems
        me = _me()
        sib = _flip(me, (0, 0, 1))
        cps, loc = [], []
        nz = 0
        for a in range(self.n):
            mine = _blk(self.kinds[a], outs[a], me)
            loc.append(pltpu.make_async_copy(ins[a], mine, lsem.at[a]))
            if self.kinds[a] == "out":
                for q in range(4):
                    loc.append(pltpu.make_async_copy(ins[self.n], outs[a].at[q, pl.ds(FB, FBP - FB), :],
                                                     lsem.at[self.n + 4 * nz + q]))
                nz += 1
            cps.append(_remote(ins[a], mine, ssem.at[7 * a], rsem.at[7 * a], sib))
            for j, f in enumerate(CHIP_FLIPS):
                cps.append(_remote(ins[a], mine, ssem.at[7 * a + 1 + j], rsem.at[7 * a + 1 + j], _flip(me, f)))
        return cps, loc

    def _passed(self, outs, sems):
        ssem, rsem, _ = sems
        me = _me()
        sib = _flip(me, (0, 0, 1))
        cps = []
        for j, f in enumerate(CHIP_FLIPS):
            for a in range(self.n):
                blk = _blk(self.kinds[a], outs[a], _flip(me, f))
                cps.append(_remote(blk, blk, ssem.at[7 * a + 4 + j], rsem.at[7 * a + 4 + j], sib))
        return cps

    def start(self, ins, outs, sems):
        cps, loc = self._first(ins, outs, sems)
        for cp in loc + cps:
            cp.start()

    def mid(self, ins, outs, sems):
        ssem, rsem, _ = sems
        me = _me()
        passed = self._passed(outs, sems)
        t = 0
        for j, f in enumerate(CHIP_FLIPS):
            for a in range(self.n):
                blk = _blk(self.kinds[a], outs[a], _flip(me, f))
                _remote(blk, blk, ssem.at[7 * a + 1 + j], rsem.at[7 * a + 1 + j], _flip(me, f)).wait_recv()
                passed[t].start()
                t += 1

    def end(self, ins, outs, sems):
        ssem, rsem, _ = sems
        me = _me()
        sib = _flip(me, (0, 0, 1))
        for a in range(self.n):
            blk = _blk(self.kinds[a], outs[a], sib)
            _remote(blk, blk, ssem.at[7 * a], rsem.at[7 * a], sib).wait_recv()
            for j, f in enumerate(CHIP_FLIPS):
                blk = _blk(self.kinds[a], outs[a], _flip(_flip(me, f), (0, 0, 1)))
                _remote(blk, blk, ssem.at[7 * a + 4 + j], rsem.at[7 * a + 4 + j], sib).wait_recv()
        cps, loc = self._first(ins, outs, sems)
        for cp in cps + self._passed(outs, sems):
            cp.wait_send()
        for cp in loc:
            cp.wait()


class _ChipScatter:
    def __init__(self, grads):
        self.n = len(grads)
        self.ins = list(grads)
        self.out_shape = [jax.ShapeDtypeStruct(g.shape, BF16) for g in grads]
        self.sems = [pltpu.SemaphoreType.DMA((3 * self.n,)), pltpu.SemaphoreType.DMA((3 * self.n,)),
                     pltpu.SemaphoreType.DMA((self.n,))]

    def _copies(self, ins, outs, sems):
        ssem, rsem, lsem = sems
        me = _me()
        mq = 2 * me[0] + me[1]
        loc = [pltpu.make_async_copy(ins[a].at[mq], outs[a].at[mq], lsem.at[a]) for a in range(self.n)]
        cps = []
        for k, f in enumerate(CHIP_FLIPS):
            p = _flip(me, f)
            for a in range(self.n):
                cps.append(_remote(ins[a].at[2 * p[0] + p[1]], outs[a].at[mq], ssem.at[3 * a + k], rsem.at[3 * a + k], p))
        return cps, loc

    def start(self, ins, outs, sems):
        cps, loc = self._copies(ins, outs, sems)
        for cp in loc + cps:
            cp.start()

    mid = None

    def end(self, ins, outs, sems):
        ssem, rsem, _ = sems
        me = _me()
        mq = 2 * me[0] + me[1]
        for k, f in enumerate(CHIP_FLIPS):
            p = _flip(me, f)
            for a in range(self.n):
                _remote(ins[a].at[mq], outs[a].at[2 * p[0] + p[1]], ssem.at[3 * a + k], rsem.at[3 * a + k], p).wait_recv()
        cps, loc = self._copies(ins, outs, sems)
        for cp in cps:
            cp.wait_send()
        for cp in loc:
            cp.wait()


class _AllGather:
    def __init__(self, parts):
        self.n = len(parts)
        self.ins = list(parts)
        self.out_shape = [jax.ShapeDtypeStruct((NDEV,) + p.shape, p.dtype) for p in parts]
        self.sems = [pltpu.SemaphoreType.DMA((7 * self.n,)), pltpu.SemaphoreType.DMA((7 * self.n,)),
                     pltpu.SemaphoreType.DMA((self.n,))]

    def _copies(self, ins, outs, sems):
        ssem, rsem, lsem = sems
        me = _me()
        mi = _lin(me)
        loc = [pltpu.make_async_copy(ins[a], outs[a].at[mi], lsem.at[a]) for a in range(self.n)]
        cps = []
        for k, f in enumerate(FLIPS):
            for a in range(self.n):
                cps.append(_remote(ins[a], outs[a].at[mi], ssem.at[7 * a + k], rsem.at[7 * a + k], _flip(me, f)))
        return cps, loc

    def start(self, ins, outs, sems):
        cps, loc = self._copies(ins, outs, sems)
        for cp in loc + cps:
            cp.start()

    mid = None

    def end(self, ins, outs, sems):
        ssem, rsem, _ = sems
        me = _me()
        for k, f in enumerate(FLIPS):
            p = _flip(me, f)
            for a in range(self.n):
                _remote(ins[a], outs[a].at[_lin(p)], ssem.at[7 * a + k], rsem.at[7 * a + k], p).wait_recv()
        cps, loc = self._copies(ins, outs, sems)
        for cp in cps:
            cp.wait_send()
        for cp in loc:
            cp.wait()


def _call(core, *, name, grid, in_specs, out_specs, out_shape, args, scratch=(), jobs=()):
    n_in, n_out, n_sc = len(in_specs), len(out_specs), len(scratch)
    steps = 1
    for g in grid:
        steps *= g

    def body(*refs):
        pos = [0]

        def take(k):
            r = refs[pos[0]:pos[0] + k]
            pos[0] += k
            return r

        ins = take(n_in)
        j_ins = [take(len(j.ins)) for j in jobs]
        outs = take(n_out)
        j_outs = [take(len(j.out_shape)) for j in jobs]
        scs = take(n_sc)
        j_sems = [take(len(j.sems)) for j in jobs]
        if len(grid) == 2:
            step = pl.program_id(0) * grid[1] + pl.program_id(1)
        elif len(grid) == 1:
            step = pl.program_id(0)
        else:
            step = 0
        for j, ji, jo, js in zip(jobs, j_ins, j_outs, j_sems):
            if grid:
                pl.when(step == 0)(lambda j=j, ji=ji, jo=jo, js=js: j.start(ji, jo, js))
            else:
                j.start(ji, jo, js)
        for j, ji, jo, js in zip(jobs, j_ins, j_outs, j_sems):
            if j.mid is not None and grid:
                at = steps - 1 if j.late_mid else (3 * steps) // 4
                pl.when(step == at)(lambda j=j, ji=ji, jo=jo, js=js: j.mid(ji, jo, js))
        if core is not None:
            core(ins, outs, scs)
        for j, ji, jo, js in zip(jobs, j_ins, j_outs, j_sems):
            if grid:
                pl.when(step == steps - 1)(lambda j=j, ji=ji, jo=jo, js=js: j.end(ji, jo, js))
            else:
                if j.mid is not None:
                    j.mid(ji, jo, js)
                j.end(ji, jo, js)

    all_in = list(in_specs)
    all_args = list(args)
    all_out = list(out_specs)
    all_shape = list(out_shape)
    all_sc = list(scratch)
    for j in jobs:
        all_in += [HBM] * len(j.ins)
        all_args += j.ins
    for j in jobs:
        all_out += [HBM] * len(j.out_shape)
        all_shape += j.out_shape
        all_sc += j.sems
    params = dict(vmem_limit_bytes=VMEM_LIMIT)
    if grid:
        params["dimension_semantics"] = ("arbitrary",) * len(grid)
    res = pl.pallas_call(
        body, name=name, grid=grid, in_specs=all_in, out_specs=all_out, out_shape=all_shape,
        scratch_shapes=all_sc, compiler_params=pltpu.CompilerParams(**params),
    )(*all_args)
    core_res = list(res[:n_out])
    job_res = []
    pos = n_out
    for j in jobs:
        job_res.append(list(res[pos:pos + len(j.out_shape)]))
        pos += len(j.out_shape)
    return core_res, job_res


def _ffn_fwd(x, mod, gvec, w_in, w_out, tm, name, jobs=(), target=None):
    T = x.shape[0]
    nt = T // tm
    tps = nt // mod.shape[0]

    def core(ins, outs, _):
        x_ref, mod_ref, g_ref, win_ref, wout_ref = ins[:5]
        xo_ref, gu_ref, y_ref = outs[:3]
        xv = x_ref[...]
        sh, sc, gt = mod_ref[0:1, :], mod_ref[1:2, :], mod_ref[2:3, :]
        r = lax.rsqrt(_rowmean(xv * xv) + EPS)
        h = (xv * r * g_ref[0:1, :]) * (1.0 + sc) + sh
        hb = h.astype(BF16)
        y = jnp.zeros((tm, D), F32)
        for cidx in range(4):
            gate = _dot_nt(hb, win_ref[cidx])
            up = _dot_nt(hb, win_ref[4 + cidx])
            gu_ref[cidx] = gate.astype(BF16)
            gu_ref[4 + cidx] = up.astype(BF16)
            act = gate * _sigmoid(gate) * up
            y = y + _dot(act.astype(BF16), wout_ref[cidx])
        y_ref[...] = y
        ry = lax.rsqrt(_rowmean(y * y) + EPS)
        xo = xv + (HALF * gt) * (y * ry * g_ref[1:2, :])
        if target is None:
            xo_ref[...] = xo
        else:
            loss_ref = outs[3]

            @pl.when(pl.program_id(0) == 0)
            def _():
                loss_ref[...] = jnp.zeros((8, D), F32)

            err = xo - ins[5][...]
            xo_ref[...] = err * (1.0 / D)
            loss_ref[...] += HALF * jnp.sum(_rowmean(err * err), axis=0, keepdims=True)

    tile = pl.BlockSpec((tm, D), lambda i: (i, 0))
    extra = target is not None
    return _call(
        core, name=name, grid=(nt,), jobs=jobs,
        in_specs=[tile, pl.BlockSpec((None, 8, D), lambda i: (i // tps, 0, 0)), _const_spec((8, D)),
                  _const_spec((8, FBP, D)), _const_spec((4, FBP, D))] + [tile] * extra,
        out_specs=[tile, pl.BlockSpec((8, tm, FBP), lambda i: (0, i, 0)), tile]
        + [pl.BlockSpec((8, D), lambda i: (0, 0))] * extra,
        out_shape=[jax.ShapeDtypeStruct((T, D), F32), jax.ShapeDtypeStruct((8, T, FBP), BF16),
                   jax.ShapeDtypeStruct((T, D), F32)] + [jax.ShapeDtypeStruct((8, D), F32)] * extra,
        args=[x, mod, gvec, w_in, w_out] + [target] * extra)


def _ffn_bwd(dxo, x, y, gu, mod, gvec, w_in, w_out, tm, name, jobs=()):
    T = x.shape[0]
    nt = T // tm
    nb = mod.shape[0]
    tps = nt // nb

    def core(ins, outs, _):
        dxo_ref, x_ref, y_ref, gu_ref, mod_ref, g_ref, win_ref, wout_ref = ins
        dx_ref, dg_ref, act_ref, hb_ref, dyb_ref, mg_ref, vg_ref = outs
        i = pl.program_id(0)
        xv = x_ref[...]
        dxo_v = dxo_ref[...]
        yv = y_ref[...]
        sh, sc, gt = mod_ref[0:1, :], mod_ref[1:2, :], mod_ref[2:3, :]
        gpre, gpost = g_ref[0:1, :], g_ref[1:2, :]
        r = lax.rsqrt(_rowmean(xv * xv) + EPS)
        xh = xv * r
        n = xh * gpre
        hb = (n * (1.0 + sc) + sh).astype(BF16)
        hb_ref[...] = hb
        ry = lax.rsqrt(_rowmean(yv * yv) + EPS)
        yh = yv * ry
        d_gt = _colsum(HALF * dxo_v * (yh * gpost))
        dp = (HALF * gt) * dxo_v
        d_gpost = _colsum(dp * yh)
        dyh = dp * gpost
        dy = ry * (dyh - yh * _rowmean(dyh * yh))
        dyb = dy.astype(BF16)
        dyb_ref[...] = dyb
        dh = jnp.zeros((tm, D), F32)
        for cidx in range(4):
            gate = gu_ref[cidx].astype(F32)
            up = gu_ref[4 + cidx].astype(F32)
            sig = _sigmoid(gate)
            s = gate * sig
            act_ref[cidx] = (s * up).astype(BF16)
            d_act = _dot_nt(dyb, wout_ref[cidx])
            d_up = (d_act * s).astype(BF16)
            d_gate = (d_act * up * (sig * (1.0 + gate * (1.0 - sig)))).astype(BF16)
            dg_ref[cidx] = d_gate
            dg_ref[4 + cidx] = d_up
            dh = dh + _dot(d_gate, win_ref[cidx]) + _dot(d_up, win_ref[4 + cidx])
        d_sc = _colsum(dh * n)
        d_sh = _colsum(dh)
        dn = dh * (1.0 + sc)
        d_gpre = _colsum(dn * xh)
        dxh = dn * gpre
        dx_ref[...] = dxo_v + r * (dxh - xh * _rowmean(dxh * xh))

        @pl.when(i % tps == 0)
        def _():
            mg_ref[...] = jnp.zeros((8, D), F32)

        @pl.when(i == 0)
        def _():
            vg_ref[...] = jnp.zeros((8, D), F32)

        mg_ref[0:1, :] += d_sh
        mg_ref[1:2, :] += d_sc
        mg_ref[2:3, :] += d_gt
        vg_ref[0:1, :] += d_gpre
        vg_ref[1:2, :] += d_gpost

    tile = pl.BlockSpec((tm, D), lambda i: (i, 0))
    return _call(
        core, name=name, grid=(nt,), jobs=jobs,
        in_specs=[tile, tile, tile, pl.BlockSpec((8, tm, FBP), lambda i: (0, i, 0)),
                  pl.BlockSpec((None, 8, D), lambda i: (i // tps, 0, 0)), _const_spec((8, D)),
                  _const_spec((8, FBP, D)), _const_spec((4, FBP, D))],
        out_specs=[tile, pl.BlockSpec((8, tm, FBP), lambda i: (0, i, 0)),
                   pl.BlockSpec((4, tm, FBP), lambda i: (0, i, 0)), tile, tile,
                   pl.BlockSpec((None, 8, D), lambda i: (i // tps, 0, 0)), pl.BlockSpec((8, D), lambda i: (0, 0))],
        out_shape=[jax.ShapeDtypeStruct((T, D), F32), jax.ShapeDtypeStruct((8, T, FBP), BF16),
                   jax.ShapeDtypeStruct((4, T, FBP), BF16), jax.ShapeDtypeStruct((T, D), BF16),
                   jax.ShapeDtypeStruct((T, D), BF16), jax.ShapeDtypeStruct((nb, 8, D), F32),
                   jax.ShapeDtypeStruct((8, D), F32)],
        args=[dxo, x, y, gu, mod, gvec, w_in, w_out])


def _masked_spatial(ws_ref):
    row = lax.broadcasted_iota(jnp.int32, (CHUNK, CHUNK), 0)
    col = lax.broadcasted_iota(jnp.int32, (CHUNK, CHUNK), 1)
    keep = col <= row
    return [jnp.where(keep, ws_ref[hd], 0.0).astype(BF16) for hd in range(NHEAD)]


def _spatial_gate(wm, vb_chunk, lane_head):
    z = jnp.zeros((CHUNK, WA), F32)
    for hd in range(NHEAD):
        z = jnp.where(lane_head == hd, _dot(wm[hd], vb_chunk), z)
    return z


def _layer_norm_stats(v):
    mu = _rowmean(v)
    vc = v - mu
    rstd = lax.rsqrt(_rowmean(vc * vc) + EPS)
    return vc * rstd, rstd


def _pitch(tm):
    p = tm // 8
    while p % 8 != 4:
        p += 1
    return p


def _lanes(s):
    return slice(s * 128, (s + 1) * 128)


def _to_slabs(ref, row0, val):
    for s in range(4):
        ref[s, row0:row0 + val.shape[0], :] = val[:, _lanes(s)]


def _tap_sum(src, out, cw_ref, bias, tm, start):
    p = _pitch(tm)
    for s in range(4):
        accs = [jnp.broadcast_to(bias[:, _lanes(s)], (8, 128))] * p
        for k in range(CONV_K):
            w = jnp.broadcast_to(cw_ref[k:k + 1, _lanes(s)], (8, 128))
            for v in range(p):
                accs[v] = accs[v] + w * src[s, pl.ds(v + start(k), 8, stride=p), :]
        for v in range(p):
            out[s, pl.ds(v, 8, stride=p), :] = accs[v]
    return jnp.concatenate([out[s, 0:tm, :] for s in range(4)], axis=1)


def _mixer_fwd(x, mod, gvec, w_mi, w_mo, v512, ws, bias_full, cw, tm, name, jobs=()):
    T = x.shape[0]
    nt = T // tm
    tps = nt // mod.shape[0]
    ext_rows = 8 * _pitch(tm)

    def core(ins, outs, scs):
        x_ref, mod_ref, g_ref, wmi_ref, wmo_ref, v_ref, ws_ref, bias_ref, cw_ref = ins
        xo_ref, proj_ref, ym_ref, conv_ref = outs
        glu_ext, conv_scr = scs
        i = pl.program_id(0)
        xv = x_ref[...]
        sh, sc, gt = mod_ref[0:1, :], mod_ref[1:2, :], mod_ref[2:3, :]
        r = lax.rsqrt(_rowmean(xv * xv) + EPS)
        hb = ((xv * r * g_ref[0:1, :]) * (1.0 + sc) + sh).astype(BF16)
        for j in range(NDEV):
            proj_ref[:, j * MB:(j + 1) * MB] = _dot(hb, wmi_ref[j])
        u = proj_ref[:, 0:WA]
        v0 = proj_ref[:, WA:2 * WA]
        a = proj_ref[:, 2 * WA:3 * WA]
        g = proj_ref[:, 3 * WA:4 * WA]
        vh, _ = _layer_norm_stats(v0)
        vb = (vh * v_ref[0:1, :] + v_ref[1:2, :]).astype(BF16)
        wm = _masked_spatial(ws_ref)
        lane_head = lax.broadcasted_iota(jnp.int32, (CHUNK, WA), 1) >> 6
        ya = []
        for q in range(tm // CHUNK):
            z = _spatial_gate(wm, vb[q * CHUNK:(q + 1) * CHUNK, :], lane_head) + bias_ref[...]
            ya.append(u[q * CHUNK:(q + 1) * CHUNK, :] * z)
        ya = jnp.concatenate(ya, axis=0)
        glu = a * _sigmoid(g)

        @pl.when(i == 0)
        def _():
            glu_ext[:, HALO + tm:HALO + ext_rows, :] = jnp.zeros((4, ext_rows - tm, 128), F32)

        @pl.when(i % tps == 0)
        def _():
            glu_ext[:, 0:HALO, :] = jnp.zeros((4, HALO, 128), F32)

        _to_slabs(glu_ext, HALO, glu)
        conv = _tap_sum(glu_ext, conv_scr, cw_ref, v_ref[2:3, :], tm, lambda k: HALO - (CONV_K - 1) + k)
        conv_ref[...] = conv
        glu_ext[:, 0:HALO, :] = glu_ext[:, tm:tm + HALO, :]
        ch, _ = _layer_norm_stats(conv)
        cn = ch * v_ref[3:4, :] + v_ref[4:5, :]
        yb = cn * _sigmoid(cn)
        pa = ya * lax.rsqrt(_rowmean(ya * ya) + EPS) * v_ref[5:6, :]
        pb = yb * lax.rsqrt(_rowmean(yb * yb) + EPS) * v_ref[6:7, :]
        ycat = jnp.concatenate([pa, pb], axis=1).astype(BF16)
        ym = _dot(ycat, wmo_ref[...])
        ym_ref[...] = ym
        rm = lax.rsqrt(_rowmean(ym * ym) + EPS)
        xo_ref[...] = xv + gt * (ym * rm * g_ref[1:2, :])

    tile = pl.BlockSpec((tm, D), lambda i: (i, 0))
    return _call(
        core, name=name, grid=(nt,), jobs=jobs,
        in_specs=[tile, pl.BlockSpec((None, 8, D), lambda i: (i // tps, 0, 0)), _const_spec((8, D)),
                  _const_spec((NDEV, D, MB)), _const_spec((D, D)), _const_spec((8, WA)),
                  _const_spec((NHEAD, CHUNK, CHUNK)), _const_spec((CHUNK, WA)), _const_spec((32, WA))],
        out_specs=[tile, pl.BlockSpec((tm, 4 * WA), lambda i: (i, 0)), tile, pl.BlockSpec((tm, WA), lambda i: (i, 0))],
        out_shape=[jax.ShapeDtypeStruct((T, D), F32), jax.ShapeDtypeStruct((T, 4 * WA), F32),
                   jax.ShapeDtypeStruct((T, D), F32), jax.ShapeDtypeStruct((T, WA), F32)],
        scratch=[pltpu.VMEM((4, HALO + ext_rows, 128), F32), pltpu.VMEM((4, ext_rows, 128), F32)],
        args=[x, mod, gvec, w_mi, w_mo, v512, ws, bias_full, cw])


def _mixer_bwd_a(dxo, ym, proj, conv, mod, gvec, w_mo, v512, ws, bias_full, esel, tm, name, jobs=()):
    T = dxo.shape[0]
    nt = T // tm
    nb = mod.shape[0]
    tps = nt // nb

    def core(ins, outs, scs):
        dxo_ref, ym_ref, proj_ref, conv_ref, mod_ref, g_ref, wmo_ref, v_ref, ws_ref, bias_ref, e_ref = ins
        dpart_ref, dymb_ref, ycat_ref, mg_ref, vg_ref, v5g_ref, gws_ref, gbs_ref = outs
        (dbs_acc,) = scs
        i = pl.program_id(0)
        dxo_v = dxo_ref[...]
        ymv = ym_ref[...]
        gt = mod_ref[2:3, :]
        gpost = g_ref[1:2, :]
        rm = lax.rsqrt(_rowmean(ymv * ymv) + EPS)
        ymh = ymv * rm
        d_gt = _colsum(dxo_v * (ymh * gpost))
        dpm = gt * dxo_v
        d_gpost = _colsum(dpm * ymh)
        dymh = dpm * gpost
        dym = (rm * (dymh - ymh * _rowmean(dymh * ymh))).astype(BF16)
        dymb_ref[...] = dym
        dycat = _dot_nt(dym, wmo_ref[...])
        u = proj_ref[:, 0:WA]
        v0 = proj_ref[:, WA:2 * WA]
        vh, rv = _layer_norm_stats(v0)
        vb = (vh * v_ref[0:1, :] + v_ref[1:2, :]).astype(BF16)
        wm = _masked_spatial(ws_ref)
        lane_head = lax.broadcasted_iota(jnp.int32, (CHUNK, WA), 1) >> 6
        zs = []
        for q in range(tm // CHUNK):
            zs.append(_spatial_gate(wm, vb[q * CHUNK:(q + 1) * CHUNK, :], lane_head) + bias_ref[...])
        z = jnp.concatenate(zs, axis=0)
        ya = u * z
        ra = lax.rsqrt(_rowmean(ya * ya) + EPS)
        yah = ya * ra
        ch, rc = _layer_norm_stats(conv_ref[...])
        cn = ch * v_ref[3:4, :] + v_ref[4:5, :]
        sg = _sigmoid(cn)
        yb = cn * sg
        rb = lax.rsqrt(_rowmean(yb * yb) + EPS)
        ybh = yb * rb
        ycat_ref[...] = jnp.concatenate([yah * v_ref[5:6, :], ybh * v_ref[6:7, :]], axis=1).astype(BF16)
        dpa = dycat[:, 0:WA]
        dpb = dycat[:, WA:2 * WA]
        d_goa = _colsum(dpa * yah)
        d_gob = _colsum(dpb * ybh)
        dyah = dpa * v_ref[5:6, :]
        dybh = dpb * v_ref[6:7, :]
        dya = ra * (dyah - yah * _rowmean(dyah * yah))
        dyb = rb * (dybh - ybh * _rowmean(dybh * ybh))
        dpart_ref[:, 0:WA] = dya * z
        dz = dya * u

        @pl.when(i == 0)
        def _():
            gws_ref[...] = jnp.zeros((NHEAD, CHUNK, CHUNK), F32)
            dbs_acc[...] = jnp.zeros((CHUNK, WA), F32)
            vg_ref[...] = jnp.zeros((8, D), F32)
            v5g_ref[...] = jnp.zeros((8, WA), F32)

        dvs = []
        for q in range(tm // CHUNK):
            dz_q = dz[q * CHUNK:(q + 1) * CHUNK, :]
            vb_q = vb[q * CHUNK:(q + 1) * CHUNK, :]
            dbs_acc[...] += dz_q
            dzb = dz_q.astype(BF16)
            dv_q = jnp.zeros((CHUNK, WA), F32)
            for hd in range(NHEAD):
                dv_q = jnp.where(lane_head == hd, _dot_tn(wm[hd], dzb), dv_q)
                dz_hd = jnp.where(lane_head == hd, dz_q, 0.0).astype(BF16)
                gws_ref[hd] += _dot_nt(dz_hd, vb_q)
            dvs.append(dv_q)
        dv = jnp.concatenate(dvs, axis=0)
        d_gng = _colsum(dv * vh)
        d_gnb = _colsum(dv)
        dvh = dv * v_ref[0:1, :]
        dpart_ref[:, WA:2 * WA] = rv * (dvh - _rowmean(dvh) - vh * _rowmean(dvh * vh))
        dcn = dyb * (sg * (1.0 + cn * (1.0 - sg)))
        d_cng = _colsum(dcn * ch)
        d_cnb = _colsum(dcn)
        dch = dcn * v_ref[3:4, :]
        dconv = rc * (dch - _rowmean(dch) - ch * _rowmean(dch * ch))
        dpart_ref[:, 2 * WA:3 * WA] = dconv
        dpart_ref[:, 3 * WA:4 * WA] = jnp.zeros((tm, WA), F32)
        d_cb = _colsum(dconv)

        @pl.when(i % tps == 0)
        def _():
            mg_ref[...] = jnp.zeros((8, D), F32)

        mg_ref[2:3, :] += d_gt
        vg_ref[1:2, :] += d_gpost
        v5g_ref[0:1, :] += d_gng
        v5g_ref[1:2, :] += d_gnb
        v5g_ref[2:3, :] += d_cb
        v5g_ref[3:4, :] += d_cng
        v5g_ref[4:5, :] += d_cnb
        v5g_ref[5:6, :] += d_goa
        v5g_ref[6:7, :] += d_gob

        @pl.when(i == nt - 1)
        def _():
            row = lax.broadcasted_iota(jnp.int32, (CHUNK, CHUNK), 0)
            col = lax.broadcasted_iota(jnp.int32, (CHUNK, CHUNK), 1)
            for hd in range(NHEAD):
                gws_ref[hd] = jnp.where(col <= row, gws_ref[hd], 0.0)
            gbs_ref[...] = lax.dot_general(e_ref[...], dbs_acc[...], (((1,), (1,)), ((), ())),
                                           precision=lax.Precision.HIGHEST, preferred_element_type=F32)

    tile = pl.BlockSpec((tm, D), lambda i: (i, 0))
    ptile = pl.BlockSpec((tm, 4 * WA), lambda i: (i, 0))
    return _call(
        core, name=name, grid=(nt,), jobs=jobs,
        in_specs=[tile, tile, pl.BlockSpec((tm, 2 * WA), lambda i: (i, 0)), pl.BlockSpec((tm, WA), lambda i: (i, 0)),
                  pl.BlockSpec((None, 8, D), lambda i: (i // tps, 0, 0)), _const_spec((8, D)), _const_spec((D, D)),
                  _const_spec((8, WA)), _const_spec((NHEAD, CHUNK, CHUNK)), _const_spec((CHUNK, WA)),
                  _const_spec((8, WA))],
        out_specs=[ptile, tile, tile, pl.BlockSpec((None, 8, D), lambda i: (i // tps, 0, 0)),
                   pl.BlockSpec((8, D), lambda i: (0, 0)), pl.BlockSpec((8, WA), lambda i: (0, 0)),
                   pl.BlockSpec((NHEAD, CHUNK, CHUNK), lambda i: (0, 0, 0)), pl.BlockSpec((8, CHUNK), lambda i: (0, 0))],
        out_shape=[jax.ShapeDtypeStruct((T, 4 * WA), F32), jax.ShapeDtypeStruct((T, D), BF16),
                   jax.ShapeDtypeStruct((T, D), BF16), jax.ShapeDtypeStruct((nb, 8, D), F32),
                   jax.ShapeDtypeStruct((8, D), F32), jax.ShapeDtypeStruct((8, WA), F32),
                   jax.ShapeDtypeStruct((NHEAD, CHUNK, CHUNK), F32), jax.ShapeDtypeStruct((8, CHUNK), F32)],
        scratch=[pltpu.VMEM((CHUNK, WA), F32)],
        args=[dxo, ym, proj, conv, mod, gvec, w_mo, v512, ws, bias_full, esel])


def _mixer_bwd_b(dxo, x, dpart, proj, mod, gvec, w_mi, cw, tm, name, jobs=()):
    T = x.shape[0]
    nt = T // tm
    nb = mod.shape[0]
    tps = nt // nb
    hpt = tm // HALO
    nh = T // HALO
    off = HALO - (CONV_K - 1)
    p = _pitch(tm)
    ext_rows = 8 * p

    def core(ins, outs, scs):
        dxo_ref, x_ref, dpart_ref, dnext_ref, ag_ref, halo_ref, mod_ref, g_ref, wmi_ref, cw_ref = ins
        dx_ref, dproj_ref, hb_ref, mg_ref, vg_ref, dcw_ref = outs
        glu_ext, dconv_ext, dglu_scr, dcw_acc = scs
        i = pl.program_id(0)
        first = i % tps == 0
        last = i % tps == tps - 1
        a = ag_ref[:, 0:WA]
        g = ag_ref[:, WA:2 * WA]
        sgg = _sigmoid(g)

        @pl.when(i == 0)
        def _():
            glu_ext[:, HALO + tm:HALO + ext_rows, :] = jnp.zeros((4, ext_rows - tm, 128), F32)
            dconv_ext[:, HALO + tm:HALO + ext_rows, :] = jnp.zeros((4, ext_rows - tm, 128), F32)
            dcw_acc[...] = jnp.zeros((32, 8, WA), F32)
            vg_ref[...] = jnp.zeros((8, D), F32)

        _to_slabs(glu_ext, 0, jnp.where(first, 0.0, halo_ref[:, 0:WA] * _sigmoid(halo_ref[:, WA:2 * WA])))
        _to_slabs(glu_ext, HALO, a * sgg)
        _to_slabs(dconv_ext, 0, dpart_ref[:, 2 * WA:3 * WA])
        _to_slabs(dconv_ext, tm, jnp.where(last, 0.0, dnext_ref[...]))
        sub = lax.broadcasted_iota(jnp.int32, (8, 128), 0)
        for s in range(4):
            accs = [jnp.zeros((8, 128), F32)] * CONV_K
            for v in range(p):
                dc = jnp.where(v + p * sub < tm, dconv_ext[s, pl.ds(v, 8, stride=p), :], 0.0)
                for k in range(CONV_K):
                    accs[k] = accs[k] + dc * glu_ext[s, pl.ds(v + off + k, 8, stride=p), :]
            for k in range(CONV_K):
                dcw_acc[k, :, _lanes(s)] += accs[k]
        dglu = _tap_sum(dconv_ext, dglu_scr, cw_ref, jnp.zeros((1, WA), F32), tm, lambda k: (CONV_K - 1) - k)

        @pl.when(i == nt - 1)
        def _():
            for k in range(CONV_K):
                dcw_ref[k:k + 1, :] = jnp.sum(dcw_acc[k], axis=0, keepdims=True)
            dcw_ref[CONV_K:32, :] = jnp.zeros((32 - CONV_K, WA), F32)

        da = dglu * sgg
        dgg = dglu * a * (sgg * (1.0 - sgg))
        dproj_ref[:, 0:2 * WA] = dpart_ref[:, 0:2 * WA].astype(BF16)
        dproj_ref[:, 2 * WA:3 * WA] = da.astype(BF16)
        dproj_ref[:, 3 * WA:4 * WA] = dgg.astype(BF16)
        dh = jnp.zeros((tm, D), F32)
        for j in range(NDEV):
            dh = dh + _dot_nt(dproj_ref[:, j * MB:(j + 1) * MB], wmi_ref[j])
        xv = x_ref[...]
        sc, sh = mod_ref[1:2, :], mod_ref[0:1, :]
        gpre = g_ref[0:1, :]
        r = lax.rsqrt(_rowmean(xv * xv) + EPS)
        xh = xv * r
        n = xh * gpre
        hb_ref[...] = (n * (1.0 + sc) + sh).astype(BF16)
        d_sc = _colsum(dh * n)
        d_sh = _colsum(dh)
        dn = dh * (1.0 + sc)
        d_gpre = _colsum(dn * xh)
        dxh = dn * gpre
        dx_ref[...] = dxo_ref[...] + r * (dxh - xh * _rowmean(dxh * xh))

        @pl.when(first)
        def _():
            mg_ref[...] = jnp.zeros((8, D), F32)

        mg_ref[0:1, :] += d_sh
        mg_ref[1:2, :] += d_sc
        vg_ref[0:1, :] += d_gpre

    tile = pl.BlockSpec((tm, D), lambda i: (i, 0))
    return _call(
        core, name=name, grid=(nt,), jobs=jobs,
        in_specs=[tile, tile, pl.BlockSpec((tm, 4 * WA), lambda i: (i, 0)),
                  pl.BlockSpec((HALO, WA), lambda i: (jnp.minimum((i + 1) * hpt, nh - 1), 2)),
                  pl.BlockSpec((tm, 2 * WA), lambda i: (i, 1)),
                  pl.BlockSpec((HALO, 2 * WA), lambda i: (jnp.maximum(i * hpt - 1, 0), 1)),
                  pl.BlockSpec((None, 8, D), lambda i: (i // tps, 0, 0)), _const_spec((8, D)),
                  _const_spec((NDEV, D, MB)), _const_spec((32, WA))],
        out_specs=[tile, pl.BlockSpec((tm, 4 * WA), lambda i: (i, 0)), tile,
                   pl.BlockSpec((None, 8, D), lambda i: (i // tps, 0, 0)), pl.BlockSpec((8, D), lambda i: (0, 0)),
                   pl.BlockSpec((32, WA), lambda i: (0, 0))],
        out_shape=[jax.ShapeDtypeStruct((T, D), F32), jax.ShapeDtypeStruct((T, 4 * WA), BF16),
                   jax.ShapeDtypeStruct((T, D), BF16), jax.ShapeDtypeStruct((nb, 8, D), F32),
                   jax.ShapeDtypeStruct((8, D), F32), jax.ShapeDtypeStruct((32, WA), F32)],
        scratch=[pltpu.VMEM((4, HALO + ext_rows, 128), F32), pltpu.VMEM((4, HALO + ext_rows, 128), F32),
                 pltpu.VMEM((4, ext_rows, 128), F32), pltpu.VMEM((32, 8, WA), F32)],
        args=[dxo, x, dpart, dpart, proj, proj, mod, gvec, w_mi, cw])


def _grad_chip(a, b, a_spec, b_spec, prod_shape, half, name, jobs=()):
    steps = 8 if half is None else 4
    R = prod_shape[0] if half is None else half
    C = prod_shape[1]

    def core(ins, outs, scs):
        a_ref, b_ref = ins
        (o_ref,) = outs
        own, snd, rcv, ssem, rsem, lsem = scs
        s = pl.program_id(0)
        c = lax.axis_index("c")
        me = _me()
        sib = _flip(me, (0, 0, 1))
        prod = _dot_tn(a_ref[...], b_ref[...]).astype(BF16)
        if half is None:
            q = s // 2

            @pl.when(s % 2 == c)
            def _():
                own[q] = prod

            @pl.when(s % 2 != c)
            def _():
                snd[q] = prod
                _remote(snd.at[q], rcv.at[q], ssem.at[q], rsem.at[q], sib).start()
        else:
            lo = prod[0:half, :]
            hi = prod[half:2 * half, :]
            own[s] = jnp.where(c == 0, lo, hi)
            snd[s] = jnp.where(c == 0, hi, lo)
            _remote(snd.at[s], rcv.at[s], ssem.at[s], rsem.at[s], sib).start()

        @pl.when(s == steps - 1)
        def _():
            for q4 in range(4):
                cp = _remote(snd.at[q4], rcv.at[q4], ssem.at[q4], rsem.at[q4], sib)
                cp.wait_recv()
                cp.wait_send()
                snd[q4] = (own[q4].astype(F32) + rcv[q4].astype(F32)).astype(BF16)
            out = pltpu.make_async_copy(snd, o_ref, lsem)
            out.start()
            out.wait()

    return _call(
        core, name=name, grid=(steps,), jobs=jobs, in_specs=[a_spec, b_spec], out_specs=[HBM],
        out_shape=[jax.ShapeDtypeStruct((4, R, C), BF16)],
        scratch=[pltpu.VMEM((4, R, C), BF16), pltpu.VMEM((4, R, C), BF16), pltpu.VMEM((4, R, C), BF16),
                 pltpu.SemaphoreType.DMA((4,)), pltpu.SemaphoreType.DMA((4,)), pltpu.SemaphoreType.DMA],
        args=[a, b])


def _grad_w_in(dg, hb, name, jobs=()):
    T = hb.shape[0]
    return _grad_chip(dg, hb, pl.BlockSpec((None, T, FBP), lambda s: (s, 0, 0)), _const_spec((T, D)),
                      (FBP, D), None, name, jobs)


def _grad_w_out(act, dyb, name, jobs=()):
    T = dyb.shape[0]
    return _grad_chip(act, dyb, pl.BlockSpec((None, T, FBP), lambda s: (s, 0, 0)), _const_spec((T, D)),
                      (FBP, D), FO, name, jobs)


def _grad_w_mi(hb, dproj, name, jobs=()):
    T = hb.shape[0]
    return _grad_chip(hb, dproj, _const_spec((T, D)), pl.BlockSpec((T, MB), lambda s: (0, s)),
                      (D, MB), None, name, jobs)


def _grad_w_mo(ycat, dym, name, jobs=()):
    T = ycat.shape[0]
    return _grad_chip(ycat, dym, pl.BlockSpec((T, 2 * MO), lambda s: (0, s)), _const_spec((T, D)),
                      (2 * MO, D), MO, name, jobs)


def _adamw_math(w, g, m, v):
    m2 = ADAM_B1 * m + (1.0 - ADAM_B1) * g
    v2 = ADAM_B2 * v + (1.0 - ADAM_B2) * (g * g)
    m_hat = m2 / (1.0 - ADAM_B1 ** ADAM_STEP)
    v_hat = v2 / (1.0 - ADAM_B2 ** ADAM_STEP)
    delta = -ADAM_LR * (m_hat / (jnp.sqrt(v_hat) + ADAM_EPS) + ADAM_WD * w)
    return delta, m2, v2


def _adamw_reduce(parts, w, m, v, tr, name):
    R, C = w.shape

    def core(ins, outs, _):
        p_ref, w_ref, m_ref, v_ref = ins
        g_ref, d_ref, m2_ref, v2_ref = outs
        g = p_ref[0].astype(F32)
        for s in range(1, 4):
            g = g + p_ref[s].astype(F32)
        g_ref[...] = g
        d_ref[...], m2_ref[...], v2_ref[...] = _adamw_math(w_ref[...], g, m_ref[...], v_ref[...])

    blk = pl.BlockSpec((tr, C), lambda i: (i, 0))
    return _call(
        core, name=name, grid=(R // tr,),
        in_specs=[pl.BlockSpec((4, tr, C), lambda i: (0, i, 0)), blk, blk, blk],
        out_specs=[blk, blk, blk, blk], out_shape=[jax.ShapeDtypeStruct((R, C), F32)] * 4,
        args=[parts, w, m, v])[0]


def _adamw_ada(sc_all, dd, w, m, v, tr, name):
    R, C = w.shape

    def core(ins, outs, _):
        sc_ref, dd_ref, w_ref, m_ref, v_ref = ins
        g_ref, d_ref, m2_ref, v2_ref = outs
        g = _dot_tn(sc_ref[...].astype(BF16), dd_ref[...].astype(BF16))
        g_ref[...] = g
        d_ref[...], m2_ref[...], v2_ref[...] = _adamw_math(w_ref[...], g, m_ref[...], v_ref[...])

    blk = pl.BlockSpec((tr, C), lambda i: (i, 0))
    return _call(
        core, name=name, grid=(R // tr,),
        in_specs=[pl.BlockSpec((64, tr), lambda i: (0, i)), pl.BlockSpec((64, C), lambda i: (0, 0)), blk, blk, blk],
        out_specs=[blk, blk, blk, blk], out_shape=[jax.ShapeDtypeStruct((R, C), F32)] * 4,
        args=[sc_all, dd, w, m, v])[0]


def _adamw_small(gathered, plain, grads, wmv, emit, name):
    nw = len(grads)
    ng, npl, ne = len(gathered), len(plain), len(emit)

    def core(ins, outs, _):
        srcs = []
        for a in range(ng):
            s = ins[a][0]
            for dev in range(1, NDEV):
                s = s + ins[a][dev]
            srcs.append(s)
        srcs += [ins[ng + a][...] for a in range(npl)]
        w_refs = ins[ng + npl:]
        for e, a in enumerate(emit):
            outs[e][...] = srcs[a]
        for t in range(nw):
            src, row = grads[t]
            g = srcs[src] if row is None else srcs[src][row:row + 1, :]
            w_ref, m_ref, v_ref = w_refs[3 * t:3 * t + 3]
            g_ref, d_ref, m2_ref, v2_ref = outs[ne + 4 * t:ne + 4 * t + 4]
            g_ref[...] = g
            d_ref[...], m2_ref[...], v2_ref[...] = _adamw_math(w_ref[...], g, m_ref[...], v_ref[...])

    out_shape = [jax.ShapeDtypeStruct(gathered[a].shape[1:], F32) for a in emit]
    for t in range(nw):
        out_shape += [jax.ShapeDtypeStruct(wmv[3 * t].shape, F32)] * 4
    return _call(
        core, name=name, grid=(), in_specs=[VM] * (ng + npl + 3 * nw), out_specs=[VM] * (ne + 4 * nw),
        out_shape=out_shape, args=list(gathered) + list(plain) + list(wmv))[0]


def _ada_fwd(c_pad, w_ada, b_cols, cw_pad, jobs=()):
    def core(ins, outs, scs):
        c_ref, w_ref, b_ref, cwp_ref = ins
        ada_ref, sc_ref, cw_ref = outs
        cbuf, send_buf, ssem, rsem = scs
        me = _me()
        mi = _lin(me)
        cbuf[mi] = c_ref[...]
        cw_ref[mi] = cwp_ref[...]
        peers = [_flip(me, f) for f in FLIPS]
        first = []
        for k, p in enumerate(peers):
            first.append(_remote(cbuf.at[mi], cbuf.at[mi], ssem.at[k], rsem.at[k], p))
            first.append(_remote(cw_ref.at[mi], cw_ref.at[mi], ssem.at[7 + k], rsem.at[7 + k], p))
        for cp in first:
            cp.start()
        for k, p in enumerate(peers):
            pi = _lin(p)
            _remote(cbuf.at[pi], cbuf.at[pi], ssem.at[k], rsem.at[k], p).wait_recv()
            _remote(cw_ref.at[pi], cw_ref.at[pi], ssem.at[7 + k], rsem.at[7 + k], p).wait_recv()
        c_all = cbuf[...].reshape(8 * 8, D)
        sc = c_all * _sigmoid(c_all)
        sc_ref[...] = sc
        res = _dot(sc.astype(BF16), w_ref[...].astype(BF16)) + b_ref[...]
        send_buf[...] = res.reshape(8, 8, ADA_B)
        ada_ref[mi] = send_buf[mi]
        second = []
        for k, p in enumerate(peers):
            second.append(_remote(send_buf.at[_lin(p)], ada_ref.at[mi], ssem.at[14 + k], rsem.at[14 + k], p))
        for cp in second:
            cp.start()
        for k, p in enumerate(peers):
            _remote(send_buf.at[mi], ada_ref.at[_lin(p)], ssem.at[14 + k], rsem.at[14 + k], p).wait_recv()
        for cp in first + second:
            cp.wait_send()

    return _call(
        core, name="ada_fwd", grid=(), jobs=jobs, in_specs=[VM, VM, VM, VM], out_specs=[VM, VM, VM],
        out_shape=[jax.ShapeDtypeStruct((8, 8, ADA_B), F32), jax.ShapeDtypeStruct((64, D), F32),
                   jax.ShapeDtypeStruct((8, 32, 64), F32)],
        scratch=[pltpu.VMEM((8, 8, D), F32), pltpu.VMEM((8, 8, ADA_B), F32),
                 pltpu.SemaphoreType.DMA((21,)), pltpu.SemaphoreType.DMA((21,))],
        args=[c_pad, w_ada, b_cols, cw_pad])


def _ada_bwd(dada):
    def body(d_ref, dd_ref, gb_ref, rbuf, ssem, rsem):
        me = _me()
        mi = _lin(me)
        peers = [_flip(me, f) for f in FLIPS]
        rbuf[mi] = d_ref[mi]
        first = []
        for k, p in enumerate(peers):
            first.append(_remote(d_ref.at[_lin(p)], rbuf.at[mi], ssem.at[k], rsem.at[k], p))
        for cp in first:
            cp.start()
        for k, p in enumerate(peers):
            _remote(d_ref.at[mi], rbuf.at[_lin(p)], ssem.at[k], rsem.at[k], p).wait_recv()
        dd = rbuf[...].reshape(64, ADA_B)
        dd_ref[...] = dd
        gb_ref[mi] = jnp.broadcast_to(_colsum(dd), (8, ADA_B))
        second = []
        for k, p in enumerate(peers):
            second.append(_remote(gb_ref.at[mi], gb_ref.at[mi], ssem.at[7 + k], rsem.at[7 + k], p))
        for cp in second:
            cp.start()
        for k, p in enumerate(peers):
            pi = _lin(p)
            _remote(gb_ref.at[pi], gb_ref.at[pi], ssem.at[7 + k], rsem.at[7 + k], p).wait_recv()
        for cp in first + second:
            cp.wait_send()

    return pl.pallas_call(
        body,
        name="ada_bwd",
        in_specs=[VM],
        out_specs=[VM, VM],
        out_shape=[jax.ShapeDtypeStruct((64, ADA_B), F32), jax.ShapeDtypeStruct((8, 8, ADA_B), F32)],
        scratch_shapes=[
            pltpu.VMEM((8, 8, ADA_B), F32),
            pltpu.SemaphoreType.DMA((14,)),
            pltpu.SemaphoreType.DMA((14,)),
        ],
        compiler_params=pltpu.CompilerParams(vmem_limit_bytes=VMEM_LIMIT),
    )(dada)


SMALL_D = ("g_pre_f1", "g_post_f1", "g_pre_m", "g_post_m", "g_pre_f2", "g_post_f2")
SMALL_W = ("gmlp_norm_g", "gmlp_norm_b", "conv_b", "conv_norm_g", "conv_norm_b", "g_out_a", "g_out_b")


def kernel(x, c, w_ada, b_ada, g_pre_f1, g_post_f1, w_f1_in, w_f1_out, g_pre_m, g_post_m, w_mix_in, gmlp_norm_g, gmlp_norm_b, w_spatial, b_spatial, conv_w, conv_b, conv_norm_g, conv_norm_b, g_out_a, g_out_b, w_mix_out, g_pre_f2, g_post_f2, w_f2_in, w_f2_out, loss_target, m_w_ada, m_b_ada, m_g_pre_f1, m_g_post_f1, m_w_f1_in, m_w_f1_out, m_g_pre_m, m_g_post_m, m_w_mix_in, m_gmlp_norm_g, m_gmlp_norm_b, m_w_spatial, m_b_spatial, m_conv_w, m_conv_b, m_conv_norm_g, m_conv_norm_b, m_g_out_a, m_g_out_b, m_w_mix_out, m_g_pre_f2, m_g_post_f2, m_w_f2_in, m_w_f2_out, v_w_ada, v_b_ada, v_g_pre_f1, v_g_post_f1, v_w_f1_in, v_w_f1_out, v_g_pre_m, v_g_post_m, v_w_mix_in, v_gmlp_norm_g, v_gmlp_norm_b, v_w_spatial, v_b_spatial, v_conv_w, v_conv_b, v_conv_norm_g, v_conv_norm_b, v_g_out_a, v_g_out_b, v_w_mix_out, v_g_pre_f2, v_g_post_f2, v_w_f2_in, v_w_f2_out):
    given = dict(locals())
    bl, seq, _ = x.shape
    T = bl * seq
    tm = min(256, seq // 2)
    mi = _lin((lax.axis_index("x"), lax.axis_index("y"), lax.axis_index("c")))

    def shard_in(w):
        return jnp.pad(w[0].T.astype(BF16), ((0, FBP - FB), (0, 0)))

    zpad = jnp.zeros((FBP - FB, D), BF16)
    g_f1 = _Gather([shard_in(w_f1_in), w_f1_out[0].astype(BF16)], ("rows", "out"), zpad)
    g_mx = _Gather([w_mix_in[0].astype(BF16), w_mix_out[0].astype(BF16), w_f2_out[0].astype(BF16)],
                   ("rows", "rows", "out"), zpad)
    g_f2 = _Gather([shard_in(w_f2_in)], ("rows",), zpad, late_mid=True)

    c_pad = jnp.pad(c, ((0, 8 - bl), (0, 0)))
    b_cols = lax.dynamic_slice(b_ada, (0, mi * ADA_B), (1, ADA_B))
    cw_pad = jnp.pad(conv_w[0], ((0, 1), (0, 0)))
    (ada_blk, sc_all, cw_all), ((wi1, wo1),) = _ada_fwd(c_pad, w_ada[0], b_cols, cw_pad, jobs=[g_f1])
    ada = ada_blk[:, 0:bl, :].transpose(1, 0, 2).reshape(bl, 9, D)
    pad5 = jnp.zeros((bl, 5, D), F32)
    mod1 = jnp.concatenate([ada[:, 0:3], pad5], axis=1)
    mod2 = jnp.concatenate([ada[:, 3:6], pad5], axis=1)
    mod3 = jnp.concatenate([ada[:, 6:9], pad5], axis=1)
    cw_full = cw_all.transpose(1, 0, 2).reshape(32, WA)

    zrow = jnp.zeros((1, D), F32)
    gv1 = jnp.concatenate([g_pre_f1, g_post_f1] + [zrow] * 6, axis=0)
    gvm = jnp.concatenate([g_pre_m, g_post_m] + [zrow] * 6, axis=0)
    gv2 = jnp.concatenate([g_pre_f2, g_post_f2] + [zrow] * 6, axis=0)
    v512 = jnp.concatenate([gmlp_norm_g, gmlp_norm_b, conv_b, conv_norm_g, conv_norm_b, g_out_a, g_out_b,
                            jnp.zeros((1, WA), F32)], axis=0)
    ws = w_spatial[0]
    bias_full = jnp.repeat(b_spatial[0].T, HD, axis=1)
    esel = (lax.broadcasted_iota(jnp.int32, (8, WA), 1) // HD == lax.broadcasted_iota(jnp.int32, (8, WA), 0)).astype(F32)

    x0 = x.reshape(T, D)
    (x1, gu1, y1), ((wmi, wmo, wo2),) = _ffn_fwd(x0, mod1, gv1, wi1, wo1, tm, "ffn1_fwd", jobs=[g_mx])
    wmo = wmo.reshape(D, D)
    (x2, proj, ym, conv), ((wi2,),) = _mixer_fwd(x1, mod2, gvm, wmi, wmo, v512, ws, bias_full, cw_full, tm, "mixer_fwd", jobs=[g_f2])
    (dx3, gu2, y2, loss_blk), _ = _ffn_fwd(x2, mod3, gv2, wi2, wo2, tm, "ffn2_fwd", target=loss_target.reshape(T, D))

    (dx2, dg2, act2, hb2, dyb2, mg3, vg3), _ = _ffn_bwd(dx3, x2, y2, gu2, mod3, gv2, wi2, wo2, tm, "ffn2_bwd")
    (g_wi2,), _ = _grad_w_in(dg2, hb2, "ffn2_gw_in")
    (g_wo2,), _ = _grad_w_out(act2, dyb2, "ffn2_gw_out")
    (dpart, dymb, ycat, mg2a, vgma, v5g, gws, gbs), ((p_wi2,),) = _mixer_bwd_a(
        dx2, ym, proj, conv, mod2, gvm, wmo, v512, ws, bias_full, esel, tm, "mixer_bwd_a",
        jobs=[_ChipScatter([g_wi2])])
    (dx1, dproj, hbm, mg2b, vgmb, dcw), ((p_wo2,),) = _mixer_bwd_b(
        dx2, x1, dpart, proj, mod2, gvm, wmi, cw_full, tm, "mixer_bwd_b", jobs=[_ChipScatter([g_wo2])])
    (g_wmi,), _ = _grad_w_mi(hbm, dproj, "mixer_gw_in")
    (g_wmo,), _ = _grad_w_mo(ycat, dymb, "mixer_gw_out")
    p2 = jnp.concatenate([v5g, dcw], axis=0)
    (dx0, dg1, act1, hb1, dyb1, mg1, vg1), ((p_wmi, p_wmo), (a2, a3, a4)) = _ffn_bwd(
        dx1, x0, y1, gu1, mod1, gv1, wi1, wo1, tm, "ffn1_bwd",
        jobs=[_ChipScatter([g_wmi, g_wmo]), _AllGather([p2, gws, gbs])])
    (g_wo1,), _ = _grad_w_out(act1, dyb1, "ffn1_gw_out")
    (g_wi1,), ((p_wo1,),) = _grad_w_in(dg1, hb1, "ffn1_gw_in", jobs=[_ChipScatter([g_wo1])])

    dada = jnp.concatenate([mg1[:, 0:3], mg2b[:, 0:2], mg2a[:, 2:3], mg3[:, 0:3]], axis=1)
    dada = dada.reshape(bl, NDEV, ADA_B).transpose(1, 0, 2)
    dada = jnp.pad(dada, ((0, 0), (0, 8 - bl), (0, 0)))
    dd_all, gb_all = _ada_bwd(dada)
    g_bada = gb_all[:, 0, :].reshape(1, 9 * D)

    p1 = jnp.concatenate([vg1[0:2], vgmb[0:1], vgma[1:2], vg3[0:2], loss_blk[0:1], zrow], axis=0)
    (p_wi1,), (a1,) = _call(None, name="tail_exchange", grid=(), in_specs=[], out_specs=[], out_shape=[], args=[],
                            jobs=[_ChipScatter([g_wi1]), _AllGather([p1])])[1]

    res = {}
    for nm, part in (("w_f1_in", p_wi1), ("w_f2_in", p_wi2)):
        quad = _adamw_reduce(part, given[nm][0].T, given["m_" + nm][0].T, given["v_" + nm][0].T, FO, "adamw_" + nm)
        res[nm] = tuple(t.T[None] for t in quad)
    for nm, part, tr in (("w_f1_out", p_wo1, FO), ("w_f2_out", p_wo2, FO), ("w_mix_in", p_wmi, 256), ("w_mix_out", p_wmo, MO)):
        quad = _adamw_reduce(part, given[nm][0], given["m_" + nm][0], given["v_" + nm][0], tr, "adamw_" + nm)
        res[nm] = tuple(t[None] for t in quad)
    quad = _adamw_ada(sc_all, dd_all, w_ada[0], m_w_ada[0], v_w_ada[0], 256, "adamw_w_ada")
    res["w_ada"] = tuple(t[None] for t in quad)

    small = SMALL_D + SMALL_W + ("w_spatial", "b_spatial", "b_ada")
    grads = [(0, r) for r in range(6)] + [(1, r) for r in range(7)] + [(2, None), (3, None), (4, None)]
    wmv = []
    for nm in small:
        for pre in ("", "m_", "v_"):
            wmv.append(given[pre + nm][0] if nm in ("w_spatial", "b_spatial") else given[pre + nm])
    outs = _adamw_small([a1, a2, a3, a4], [g_bada], grads, wmv, (0, 1), "adamw_small")
    loss = outs[0][6, 0]
    for t, nm in enumerate(small):
        quad = outs[2 + 4 * t:6 + 4 * t]
        res[nm] = tuple(q[None] for q in quad) if nm in ("w_spatial", "b_spatial") else tuple(quad)
    g_cw = lax.dynamic_slice(outs[1], (8, mi * 64), (32, 64))
    wmv = [jnp.pad(given[pre + "conv_w"][0], ((0, 1), (0, 0)), constant_values=1.0 if pre == "v_" else 0.0)
           for pre in ("", "m_", "v_")]
    quad = _adamw_small([], [g_cw], [(0, None)], wmv, (), "adamw_conv_w")
    res["conv_w"] = tuple(q[0:CONV_K][None] for q in quad)

    order = ["w_ada", "b_ada", "g_pre_f1", "g_post_f1", "w_f1_in", "w_f1_out", "g_pre_m", "g_post_m", "w_mix_in",
             "gmlp_norm_g", "gmlp_norm_b", "w_spatial", "b_spatial", "conv_w", "conv_b", "conv_norm_g", "conv_norm_b",
             "g_out_a", "g_out_b", "w_mix_out", "g_pre_f2", "g_post_f2", "w_f2_in", "w_f2_out"]
    out = [loss, dx0.reshape(bl, seq, D)]
    for k in range(4):
        out += [res[nm][k] for nm in order]
    return tuple(out)
```

```python
import jax
import jax.numpy as jnp
from jax import lax
from jax.experimental import pallas as pl
from jax.experimental.pallas import tpu as pltpu

F32 = jnp.float32
BF16 = jnp.bfloat16

D = 1024
DFF = 2816
NDEV = 8
FB = 2 * DFF // NDEV
FBP = 704
FO = DFF // NDEV
WA = 512
NHEAD = 8
HD = 64
CHUNK = 128
CONV_K = 31
HALO = 32
MB = 2 * (WA + WA) // NDEV
MO = D // NDEV
ADA_B = 9 * D // NDEV
EPS = 1e-6
HALF = 0.5

ADAM_LR = 0.001
ADAM_B1 = 0.9
ADAM_B2 = 0.999
ADAM_EPS = 1e-08
ADAM_WD = 0.01
ADAM_STEP = 10

VMEM_LIMIT = 56 * 1024 * 1024
MESH = pl.DeviceIdType.MESH
FLIPS = ((0, 0, 1), (1, 0, 0), (0, 1, 0), (1, 1, 0), (1, 0, 1), (0, 1, 1), (1, 1, 1))
CHIP_FLIPS = ((1, 0, 0), (0, 1, 0), (1, 1, 0))
HBM = pl.BlockSpec(memory_space=pl.ANY)
VM = pl.BlockSpec(memory_space=pltpu.VMEM)


def _dot(a, b):
    return lax.dot_general(a, b, (((1,), (0,)), ((), ())), preferred_element_type=F32)


def _dot_nt(a, b):
    return lax.dot_general(a, b, (((1,), (1,)), ((), ())), preferred_element_type=F32)


def _dot_tn(a, b):
    return lax.dot_general(a, b, (((0,), (0,)), ((), ())), preferred_element_type=F32)


def _rowmean(v):
    return jnp.mean(v, axis=-1, keepdims=True)


def _colsum(v):
    return jnp.sum(v, axis=0, keepdims=True)


def _sigmoid(v):
    return 1.0 / (1.0 + jnp.exp(-v))


def _const_spec(shape):
    nd = len(shape)
    return pl.BlockSpec(shape, lambda *_: (0,) * nd, pipeline_mode=pl.Buffered(1))


def _me():
    return lax.axis_index("x"), lax.axis_index("y"), lax.axis_index("c")


def _flip(me, f):
    return tuple(1 - v if b else v for v, b in zip(me, f))


def _lin(p):
    return 4 * p[0] + 2 * p[1] + p[2]


def _remote(src, dst, send_sem, recv_sem, dev):
    return pltpu.make_async_remote_copy(src_ref=src, dst_ref=dst, send_sem=send_sem, recv_sem=recv_sem,
                                        device_id=dev, device_id_type=MESH)


def _blk(kind, ref, p):
    if kind == "out":
        return ref.at[2 * p[0] + p[1], pl.ds(p[2] * FO, FO), :]
    return ref.at[_lin(p)]


class _Gather:
    def __init__(self, shards, kinds, zpad, late_mid=False):
        self.late_mid = late_mid
        self.kinds = kinds
        self.n = len(shards)
        self.ins = list(shards) + [zpad]
        self.out_shape = [jax.ShapeDtypeStruct((4, FBP, D) if k == "out" else (NDEV,) + s.shape, BF16)
                          for s, k in zip(shards, kinds)]
        self.n_out = sum(k == "out" for k in kinds)
        self.sems = [pltpu.SemaphoreType.DMA((7 * self.n,)), pltpu.SemaphoreType.DMA((7 * self.n,)),
                     pltpu.SemaphoreType.DMA((self.n + 4 * max(self.n_out, 1),))]

    def _first(self, ins, outs, sems):
        ssem, rsem, lsem = sems
        me = _me()
        sib = _flip(me, (0, 0, 1))
        cps, loc = [], []
        nz = 0
        for a in range(self.n):
            mine = _blk(self.kinds[a], outs[a], me)
            loc.append(pltpu.make_async_copy(ins[a], mine, lsem.at[a]))
            if self.kinds[a] == "out" and FBP > FB:
                for q in range(4):
                    loc.append(pltpu.make_async_copy(ins[self.n], outs[a].at[q, pl.ds(FB, FBP - FB), :],
                                                     lsem.at[self.n + 4 * nz + q]))
                nz += 1
            cps.append(_remote(ins[a], mine, ssem.at[7 * a], rsem.at[7 * a], sib))
            for j, f in enumerate(CHIP_FLIPS):
                cps.append(_remote(ins[a], mine, ssem.at[7 * a + 1 + j], rsem.at[7 * a + 1 + j], _flip(me, f)))
        return cps, loc

    def _passed(self, outs, sems):
        ssem, rsem, _ = sems
        me = _me()
        sib = _flip(me, (0, 0, 1))
        cps = []
        for j, f in enumerate(CHIP_FLIPS):
            for a in range(self.n):
                blk = _blk(self.kinds[a], outs[a], _flip(me, f))
                cps.append(_remote(blk, blk, ssem.at[7 * a + 4 + j], rsem.at[7 * a + 4 + j], sib))
        return cps

    def start(self, ins, outs, sems):
        cps, loc = self._first(ins, outs, sems)
        for cp in loc + cps:
            cp.start()

    def mid(self, ins, outs, sems):
        ssem, rsem, _ = sems
        me = _me()
        passed = self._passed(outs, sems)
        t = 0
        for j, f in enumerate(CHIP_FLIPS):
            for a in range(self.n):
                blk = _blk(self.kinds[a], outs[a], _flip(me, f))
                _remote(blk, blk, ssem.at[7 * a + 1 + j], rsem.at[7 * a + 1 + j], _flip(me, f)).wait_recv()
                passed[t].start()
                t += 1

    def end(self, ins, outs, sems):
        ssem, rsem, _ = sems
        me = _me()
        sib = _flip(me, (0, 0, 1))
        for a in range(self.n):
            blk = _blk(self.kinds[a], outs[a], sib)
            _remote(blk, blk, ssem.at[7 * a], rsem.at[7 * a], sib).wait_recv()
            for j, f in enumerate(CHIP_FLIPS):
                blk = _blk(self.kinds[a], outs[a], _flip(_flip(me, f), (0, 0, 1)))
                _remote(blk, blk, ssem.at[7 * a + 4 + j], rsem.at[7 * a + 4 + j], sib).wait_recv()
        cps, loc = self._first(ins, outs, sems)
        for cp in cps + self._passed(outs, sems):
            cp.wait_send()
        for cp in loc:
            cp.wait()


class _ChipScatter:
    def __init__(self, grads):
        self.n = len(grads)
        self.ins = list(grads)
        self.out_shape = [jax.ShapeDtypeStruct(g.shape, BF16) for g in grads]
        self.sems = [pltpu.SemaphoreType.DMA((3 * self.n,)), pltpu.SemaphoreType.DMA((3 * self.n,)),
                     pltpu.SemaphoreType.DMA((self.n,))]

    def _copies(self, ins, outs, sems):
        ssem, rsem, lsem = sems
        me = _me()
        mq = 2 * me[0] + me[1]
        loc = [pltpu.make_async_copy(ins[a].at[mq], outs[a].at[mq], lsem.at[a]) for a in range(self.n)]
        cps = []
        for k, f in enumerate(CHIP_FLIPS):
            p = _flip(me, f)
            for a in range(self.n):
                cps.append(_remote(ins[a].at[2 * p[0] + p[1]], outs[a].at[mq], ssem.at[3 * a + k], rsem.at[3 * a + k], p))
        return cps, loc

    def start(self, ins, outs, sems):
        cps, loc = self._copies(ins, outs, sems)
        for cp in loc + cps:
            cp.start()

    mid = None

    def end(self, ins, outs, sems):
        ssem, rsem, _ = sems
        me = _me()
        mq = 2 * me[0] + me[1]
        for k, f in enumerate(CHIP_FLIPS):
            p = _flip(me, f)
            for a in range(self.n):
                _remote(ins[a].at[mq], outs[a].at[2 * p[0] + p[1]], ssem.at[3 * a + k], rsem.at[3 * a + k], p).wait_recv()
        cps, loc = self._copies(ins, outs, sems)
        for cp in cps:
            cp.wait_send()
        for cp in loc:
            cp.wait()


class _AllGather:
    def __init__(self, parts):
        self.n = len(parts)
        self.ins = list(parts)
        self.out_shape = [jax.ShapeDtypeStruct((NDEV,) + p.shape, p.dtype) for p in parts]
        self.sems = [pltpu.SemaphoreType.DMA((7 * self.n,)), pltpu.SemaphoreType.DMA((7 * self.n,)),
                     pltpu.SemaphoreType.DMA((self.n,))]

    def _copies(self, ins, outs, sems):
        ssem, rsem, lsem = sems
        me = _me()
        mi = _lin(me)
        loc = [pltpu.make_async_copy(ins[a], outs[a].at[mi], lsem.at[a]) for a in range(self.n)]
        cps = []
        for k, f in enumerate(FLIPS):
            for a in range(self.n):
                cps.append(_remote(ins[a], outs[a].at[mi], ssem.at[7 * a + k], rsem.at[7 * a + k], _flip(me, f)))
        return cps, loc

    def start(self, ins, outs, sems):
        cps, loc = self._copies(ins, outs, sems)
        for cp in loc + cps:
            cp.start()

    mid = None

    def end(self, ins, outs, sems):
        ssem, rsem, _ = sems
        me = _me()
        for k, f in enumerate(FLIPS):
            p = _flip(me, f)
            for a in range(self.n):
                _remote(ins[a], outs[a].at[_lin(p)], ssem.at[7 * a + k], rsem.at[7 * a + k], p).wait_recv()
        cps, loc = self._copies(ins, outs, sems)
        for cp in cps:
            cp.wait_send()
        for cp in loc:
            cp.wait()


def _call(core, *, name, grid, in_specs, out_specs, out_shape, args, scratch=(), jobs=()):
    n_in, n_out, n_sc = len(in_specs), len(out_specs), len(scratch)
    steps = 1
    for g in grid:
        steps *= g

    def body(*refs):
        pos = [0]

        def take(k):
            r = refs[pos[0]:pos[0] + k]
            pos[0] += k
            return r

        ins = take(n_in)
        j_ins = [take(len(j.ins)) for j in jobs]
        outs = take(n_out)
        j_outs = [take(len(j.out_shape)) for j in jobs]
        scs = take(n_sc)
        j_sems = [take(len(j.sems)) for j in jobs]
        if len(grid) == 2:
            step = pl.program_id(0) * grid[1] + pl.program_id(1)
        elif len(grid) == 1:
            step = pl.program_id(0)
        else:
            step = 0
        for j, ji, jo, js in zip(jobs, j_ins, j_outs, j_sems):
            if grid:
                pl.when(step == 0)(lambda j=j, ji=ji, jo=jo, js=js: j.start(ji, jo, js))
            else:
                j.start(ji, jo, js)
        for j, ji, jo, js in zip(jobs, j_ins, j_outs, j_sems):
            if j.mid is not None and grid:
                at = steps - 1 if j.late_mid else (3 * steps) // 4
                pl.when(step == at)(lambda j=j, ji=ji, jo=jo, js=js: j.mid(ji, jo, js))
        if core is not None:
            core(ins, outs, scs)
        for j, ji, jo, js in zip(jobs, j_ins, j_outs, j_sems):
            if grid:
                pl.when(step == steps - 1)(lambda j=j, ji=ji, jo=jo, js=js: j.end(ji, jo, js))
            else:
                if j.mid is not None:
                    j.mid(ji, jo, js)
                j.end(ji, jo, js)

    all_in = list(in_specs)
    all_args = list(args)
    all_out = list(out_specs)
    all_shape = list(out_shape)
    all_sc = list(scratch)
    for j in jobs:
        all_in += [HBM] * len(j.ins)
        all_args += j.ins
    for j in jobs:
        all_out += [HBM] * len(j.out_shape)
        all_shape += j.out_shape
        all_sc += j.sems
    params = dict(vmem_limit_bytes=VMEM_LIMIT)
    if grid:
        params["dimension_semantics"] = ("arbitrary",) * len(grid)
    res = pl.pallas_call(
        body, name=name, grid=grid, in_specs=all_in, out_specs=all_out, out_shape=all_shape,
        scratch_shapes=all_sc, compiler_params=pltpu.CompilerParams(**params),
    )(*all_args)
    core_res = list(res[:n_out])
    job_res = []
    pos = n_out
    for j in jobs:
        job_res.append(list(res[pos:pos + len(j.out_shape)]))
        pos += len(j.out_shape)
    return core_res, job_res


def _ffn_fwd(x, mod, gvec, w_in, w_out, tm, name, jobs=(), target=None):
    T = x.shape[0]
    nt = T // tm
    tps = nt // mod.shape[0]

    def core(ins, outs, _):
        x_ref, mod_ref, g_ref, win_ref, wout_ref = ins[:5]
        xo_ref, gu_ref, y_ref = outs[:3]
        xv = x_ref[...]
        sh, sc, gt = mod_ref[0:1, :], mod_ref[1:2, :], mod_ref[2:3, :]
        r = lax.rsqrt(_rowmean(xv * xv) + EPS)
        h = (xv * r * g_ref[0:1, :]) * (1.0 + sc) + sh
        hb = h.astype(BF16)
        y = jnp.zeros((tm, D), F32)
        for cidx in range(4):
            gate = _dot_nt(hb, win_ref[cidx])
            up = _dot_nt(hb, win_ref[4 + cidx])
            gu_ref[cidx] = gate.astype(BF16)
            gu_ref[4 + cidx] = up.astype(BF16)
            act = gate * _sigmoid(gate) * up
            y = y + _dot(act.astype(BF16), wout_ref[cidx])
        y_ref[...] = y
        ry = lax.rsqrt(_rowmean(y * y) + EPS)
        xo = xv + (HALF * gt) * (y * ry * g_ref[1:2, :])
        if target is None:
            xo_ref[...] = xo
        else:
            loss_ref = outs[3]

            @pl.when(pl.program_id(0) == 0)
            def _():
                loss_ref[...] = jnp.zeros((8, D), F32)

            err = xo - ins[5][...]
            xo_ref[...] = err * (1.0 / D)
            loss_ref[...] += HALF * jnp.sum(_rowmean(err * err), axis=0, keepdims=True)

    tile = pl.BlockSpec((tm, D), lambda i: (i, 0))
    extra = target is not None
    return _call(
        core, name=name, grid=(nt,), jobs=jobs,
        in_specs=[tile, pl.BlockSpec((None, 8, D), lambda i: (i // tps, 0, 0)), _const_spec((8, D)),
                  _const_spec((8, FBP, D)), _const_spec((4, FBP, D))] + [tile] * extra,
        out_specs=[tile, pl.BlockSpec((8, tm, FBP), lambda i: (0, i, 0)), tile]
        + [pl.BlockSpec((8, D), lambda i: (0, 0))] * extra,
        out_shape=[jax.ShapeDtypeStruct((T, D), F32), jax.ShapeDtypeStruct((8, T, FBP), BF16),
                   jax.ShapeDtypeStruct((T, D), F32)] + [jax.ShapeDtypeStruct((8, D), F32)] * extra,
        args=[x, mod, gvec, w_in, w_out] + [target] * extra)


def _ffn_bwd(dxo, x, y, gu, mod, gvec, w_in, w_out, tm, name, jobs=()):
    T = x.shape[0]
    nt = T // tm
    nb = mod.shape[0]
    tps = nt // nb

    def core(ins, outs, _):
        dxo_ref, x_ref, y_ref, gu_ref, mod_ref, g_ref, win_ref, wout_ref = ins
        dx_ref, dg_ref, act_ref, hb_ref, dyb_ref, mg_ref, vg_ref = outs
        i = pl.program_id(0)
        xv = x_ref[...]
        dxo_v = dxo_ref[...]
        yv = y_ref[...]
        sh, sc, gt = mod_ref[0:1, :], mod_ref[1:2, :], mod_ref[2:3, :]
        gpre, gpost = g_ref[0:1, :], g_ref[1:2, :]
        r = lax.rsqrt(_rowmean(xv * xv) + EPS)
        xh = xv * r
        n = xh * gpre
        hb = (n * (1.0 + sc) + sh).astype(BF16)
        hb_ref[...] = hb
        ry = lax.rsqrt(_rowmean(yv * yv) + EPS)
        yh = yv * ry
        d_gt = _colsum(HALF * dxo_v * (yh * gpost))
        dp = (HALF * gt) * dxo_v
        d_gpost = _colsum(dp * yh)
        dyh = dp * gpost
        dy = ry * (dyh - yh * _rowmean(dyh * yh))
        dyb = dy.astype(BF16)
        dyb_ref[...] = dyb
        dh = jnp.zeros((tm, D), F32)
        for cidx in range(4):
            gate = gu_ref[cidx].astype(F32)
            up = gu_ref[4 + cidx].astype(F32)
            sig = _sigmoid(gate)
            s = gate * sig
            act_ref[cidx] = (s * up).astype(BF16)
            d_act = _dot_nt(dyb, wout_ref[cidx])
            d_up = (d_act * s).astype(BF16)
            d_gate = (d_act * up * (sig * (1.0 + gate * (1.0 - sig)))).astype(BF16)
            dg_ref[cidx] = d_gate
            dg_ref[4 + cidx] = d_up
            dh = dh + _dot(d_gate, win_ref[cidx]) + _dot(d_up, win_ref[4 + cidx])
        d_sc = _colsum(dh * n)
        d_sh = _colsum(dh)
        dn = dh * (1.0 + sc)
        d_gpre = _colsum(dn * xh)
        dxh = dn * gpre
        dx_ref[...] = dxo_v + r * (dxh - xh * _rowmean(dxh * xh))

        @pl.when(i % tps == 0)
        def _():
            mg_ref[...] = jnp.zeros((8, D), F32)

        @pl.when(i == 0)
        def _():
            vg_ref[...] = jnp.zeros((8, D), F32)

        mg_ref[0:1, :] += d_sh
        mg_ref[1:2, :] += d_sc
        mg_ref[2:3, :] += d_gt
        vg_ref[0:1, :] += d_gpre
        vg_ref[1:2, :] += d_gpost

    tile = pl.BlockSpec((tm, D), lambda i: (i, 0))
    return _call(
        core, name=name, grid=(nt,), jobs=jobs,
        in_specs=[tile, tile, tile, pl.BlockSpec((8, tm, FBP), lambda i: (0, i, 0)),
                  pl.BlockSpec((None, 8, D), lambda i: (i // tps, 0, 0)), _const_spec((8, D)),
                  _const_spec((8, FBP, D)), _const_spec((4, FBP, D))],
        out_specs=[tile, pl.BlockSpec((8, tm, FBP), lambda i: (0, i, 0)),
                   pl.BlockSpec((4, tm, FBP), lambda i: (0, i, 0)), tile, tile,
                   pl.BlockSpec((None, 8, D), lambda i: (i // tps, 0, 0)), pl.BlockSpec((8, D), lambda i: (0, 0))],
        out_shape=[jax.ShapeDtypeStruct((T, D), F32), jax.ShapeDtypeStruct((8, T, FBP), BF16),
                   jax.ShapeDtypeStruct((4, T, FBP), BF16), jax.ShapeDtypeStruct((T, D), BF16),
                   jax.ShapeDtypeStruct((T, D), BF16), jax.ShapeDtypeStruct((nb, 8, D), F32),
                   jax.ShapeDtypeStruct((8, D), F32)],
        args=[dxo, x, y, gu, mod, gvec, w_in, w_out])


def _masked_spatial(ws_ref):
    row = lax.broadcasted_iota(jnp.int32, (CHUNK, CHUNK), 0)
    col = lax.broadcasted_iota(jnp.int32, (CHUNK, CHUNK), 1)
    keep = col <= row
    return [jnp.where(keep, ws_ref[hd], 0.0).astype(BF16) for hd in range(NHEAD)]


def _spatial_gate(wm, vb_chunk, lane_head):
    z = jnp.zeros((CHUNK, WA), F32)
    for hd in range(NHEAD):
        z = jnp.where(lane_head == hd, _dot(wm[hd], vb_chunk), z)
    return z


def _layer_norm_stats(v):
    mu = _rowmean(v)
    vc = v - mu
    rstd = lax.rsqrt(_rowmean(vc * vc) + EPS)
    return vc * rstd, rstd


def _pitch(tm):
    p = tm // 8
    while p % 8 != 4:
        p += 1
    return p


def _lanes(s):
    return slice(s * 128, (s + 1) * 128)


def _to_slabs(ref, row0, val):
    for s in range(4):
        ref[s, row0:row0 + val.shape[0], :] = val[:, _lanes(s)]


def _tap_sum(src, out, cw_ref, bias, tm, start):
    p = _pitch(tm)
    for s in range(4):
        accs = [jnp.broadcast_to(bias[:, _lanes(s)], (8, 128))] * p
        for k in range(CONV_K):
            w = jnp.broadcast_to(cw_ref[k:k + 1, _lanes(s)], (8, 128))
            for v in range(p):
                accs[v] = accs[v] + w * src[s, pl.ds(v + start(k), 8, stride=p), :]
        for v in range(p):
            out[s, pl.ds(v, 8, stride=p), :] = accs[v]
    return jnp.concatenate([out[s, 0:tm, :] for s in range(4)], axis=1)


def _mixer_fwd(x, mod, gvec, w_mi, w_mo, v512, ws, bias_full, cw, tm, name, jobs=()):
    T = x.shape[0]
    nt = T // tm
    tps = nt // mod.shape[0]
    ext_rows = 8 * _pitch(tm)

    def core(ins, outs, scs):
        x_ref, mod_ref, g_ref, wmi_ref, wmo_ref, v_ref, ws_ref, bias_ref, cw_ref = ins
        xo_ref, proj_ref, ym_ref, conv_ref = outs
        glu_ext, conv_scr = scs
        i = pl.program_id(0)
        xv = x_ref[...]
        sh, sc, gt = mod_ref[0:1, :], mod_ref[1:2, :], mod_ref[2:3, :]
        r = lax.rsqrt(_rowmean(xv * xv) + EPS)
        hb = ((xv * r * g_ref[0:1, :]) * (1.0 + sc) + sh).astype(BF16)
        for j in range(NDEV):
            proj_ref[:, j * MB:(j + 1) * MB] = _dot(hb, wmi_ref[j])
        u = proj_ref[:, 0:WA]
        v0 = proj_ref[:, WA:2 * WA]
        a = proj_ref[:, 2 * WA:3 * WA]
        g = proj_ref[:, 3 * WA:4 * WA]
        vh, _ = _layer_norm_stats(v0)
        vb = (vh * v_ref[0:1, :] + v_ref[1:2, :]).astype(BF16)
        wm = _masked_spatial(ws_ref)
        lane_head = lax.broadcasted_iota(jnp.int32, (CHUNK, WA), 1) >> 6
        ya = []
        for q in range(tm // CHUNK):
            z = _spatial_gate(wm, vb[q * CHUNK:(q + 1) * CHUNK, :], lane_head) + bias_ref[...]
            ya.append(u[q * CHUNK:(q + 1) * CHUNK, :] * z)
        ya = jnp.concatenate(ya, axis=0)
        glu = a * _sigmoid(g)

        @pl.when(i == 0)
        def _():
            glu_ext[:, HALO + tm:HALO + ext_rows, :] = jnp.zeros((4, ext_rows - tm, 128), F32)

        @pl.when(i % tps == 0)
        def _():
            glu_ext[:, 0:HALO, :] = jnp.zeros((4, HALO, 128), F32)

        _to_slabs(glu_ext, HALO, glu)
        conv = _tap_sum(glu_ext, conv_scr, cw_ref, v_ref[2:3, :], tm, lambda k: HALO - (CONV_K - 1) + k)
        conv_ref[...] = conv
        glu_ext[:, 0:HALO, :] = glu_ext[:, tm:tm + HALO, :]
        ch, _ = _layer_norm_stats(conv)
        cn = ch * v_ref[3:4, :] + v_ref[4:5, :]
        yb = cn * _sigmoid(cn)
        pa = ya * lax.rsqrt(_rowmean(ya * ya) + EPS) * v_ref[5:6, :]
        pb = yb * lax.rsqrt(_rowmean(yb * yb) + EPS) * v_ref[6:7, :]
        ycat = jnp.concatenate([pa, pb], axis=1).astype(BF16)
        ym = _dot(ycat, wmo_ref[...])
        ym_ref[...] = ym
        rm = lax.rsqrt(_rowmean(ym * ym) + EPS)
        xo_ref[...] = xv + gt * (ym * rm * g_ref[1:2, :])

    tile = pl.BlockSpec((tm, D), lambda i: (i, 0))
    return _call(
        core, name=name, grid=(nt,), jobs=jobs,
        in_specs=[tile, pl.BlockSpec((None, 8, D), lambda i: (i // tps, 0, 0)), _const_spec((8, D)),
                  _const_spec((NDEV, D, MB)), _const_spec((D, D)), _const_spec((8, WA)),
                  _const_spec((NHEAD, CHUNK, CHUNK)), _const_spec((CHUNK, WA)), _const_spec((32, WA))],
        out_specs=[tile, pl.BlockSpec((tm, 4 * WA), lambda i: (i, 0)), tile, pl.BlockSpec((tm, WA), lambda i: (i, 0))],
        out_shape=[jax.ShapeDtypeStruct((T, D), F32), jax.ShapeDtypeStruct((T, 4 * WA), F32),
                   jax.ShapeDtypeStruct((T, D), F32), jax.ShapeDtypeStruct((T, WA), F32)],
        scratch=[pltpu.VMEM((4, HALO + ext_rows, 128), F32), pltpu.VMEM((4, ext_rows, 128), F32)],
        args=[x, mod, gvec, w_mi, w_mo, v512, ws, bias_full, cw])


def _mixer_bwd_a(dxo, ym, proj, conv, mod, gvec, w_mo, v512, ws, bias_full, esel, tm, name, jobs=()):
    T = dxo.shape[0]
    nt = T // tm
    nb = mod.shape[0]
    tps = nt // nb

    def core(ins, outs, scs):
        dxo_ref, ym_ref, proj_ref, conv_ref, mod_ref, g_ref, wmo_ref, v_ref, ws_ref, bias_ref, e_ref = ins
        dpart_ref, dymb_ref, ycat_ref, mg_ref, vg_ref, v5g_ref, gws_ref, gbs_ref = outs
        (dbs_acc,) = scs
        i = pl.program_id(0)
        dxo_v = dxo_ref[...]
        ymv = ym_ref[...]
        gt = mod_ref[2:3, :]
        gpost = g_ref[1:2, :]
        rm = lax.rsqrt(_rowmean(ymv * ymv) + EPS)
        ymh = ymv * rm
        d_gt = _colsum(dxo_v * (ymh * gpost))
        dpm = gt * dxo_v
        d_gpost = _colsum(dpm * ymh)
        dymh = dpm * gpost
        dym = (rm * (dymh - ymh * _rowmean(dymh * ymh))).astype(BF16)
        dymb_ref[...] = dym
        dycat = _dot_nt(dym, wmo_ref[...])
        u = proj_ref[:, 0:WA]
        v0 = proj_ref[:, WA:2 * WA]
        vh, rv = _layer_norm_stats(v0)
        vb = (vh * v_ref[0:1, :] + v_ref[1:2, :]).astype(BF16)
        wm = _masked_spatial(ws_ref)
        lane_head = lax.broadcasted_iota(jnp.int32, (CHUNK, WA), 1) >> 6
        zs = []
        for q in range(tm // CHUNK):
            zs.append(_spatial_gate(wm, vb[q * CHUNK:(q + 1) * CHUNK, :], lane_head) + bias_ref[...])
        z = jnp.concatenate(zs, axis=0)
        ya = u * z
        ra = lax.rsqrt(_rowmean(ya * ya) + EPS)
        yah = ya * ra
        ch, rc = _layer_norm_stats(conv_ref[...])
        cn = ch * v_ref[3:4, :] + v_ref[4:5, :]
        sg = _sigmoid(cn)
        yb = cn * sg
        rb = lax.rsqrt(_rowmean(yb * yb) + EPS)
        ybh = yb * rb
        ycat_ref[...] = jnp.concatenate([yah * v_ref[5:6, :], ybh * v_ref[6:7, :]], axis=1).astype(BF16)
        dpa = dycat[:, 0:WA]
        dpb = dycat[:, WA:2 * WA]
        d_goa = _colsum(dpa * yah)
        d_gob = _colsum(dpb * ybh)
        dyah = dpa * v_ref[5:6, :]
        dybh = dpb * v_ref[6:7, :]
        dya = ra * (dyah - yah * _rowmean(dyah * yah))
        dyb = rb * (dybh - ybh * _rowmean(dybh * ybh))
        dpart_ref[:, 0:WA] = dya * z
        dz = dya * u

        @pl.when(i == 0)
        def _():
            gws_ref[...] = jnp.zeros((NHEAD, CHUNK, CHUNK), F32)
            dbs_acc[...] = jnp.zeros((CHUNK, WA), F32)
            vg_ref[...] = jnp.zeros((8, D), F32)
            v5g_ref[...] = jnp.zeros((8, WA), F32)

        dvs = []
        for q in range(tm // CHUNK):
            dz_q = dz[q * CHUNK:(q + 1) * CHUNK, :]
            vb_q = vb[q * CHUNK:(q + 1) * CHUNK, :]
            dbs_acc[...] += dz_q
            dzb = dz_q.astype(BF16)
            dv_q = jnp.zeros((CHUNK, WA), F32)
            for hd in range(NHEAD):
                dv_q = jnp.where(lane_head == hd, _dot_tn(wm[hd], dzb), dv_q)
                dz_hd = jnp.where(lane_head == hd, dz_q, 0.0).astype(BF16)
                gws_ref[hd] += _dot_nt(dz_hd, vb_q)
            dvs.append(dv_q)
        dv = jnp.concatenate(dvs, axis=0)
        d_gng = _colsum(dv * vh)
        d_gnb = _colsum(dv)
        dvh = dv * v_ref[0:1, :]
        dpart_ref[:, WA:2 * WA] = rv * (dvh - _rowmean(dvh) - vh * _rowmean(dvh * vh))
        dcn = dyb * (sg * (1.0 + cn * (1.0 - sg)))
        d_cng = _colsum(dcn * ch)
        d_cnb = _colsum(dcn)
        dch = dcn * v_ref[3:4, :]
        dconv = rc * (dch - _rowmean(dch) - ch * _rowmean(dch * ch))
        dpart_ref[:, 2 * WA:3 * WA] = dconv
        dpart_ref[:, 3 * WA:4 * WA] = jnp.zeros((tm, WA), F32)
        d_cb = _colsum(dconv)

        @pl.when(i % tps == 0)
        def _():
            mg_ref[...] = jnp.zeros((8, D), F32)

        mg_ref[2:3, :] += d_gt
        vg_ref[1:2, :] += d_gpost
        v5g_ref[0:1, :] += d_gng
        v5g_ref[1:2, :] += d_gnb
        v5g_ref[2:3, :] += d_cb
        v5g_ref[3:4, :] += d_cng
        v5g_ref[4:5, :] += d_cnb
        v5g_ref[5:6, :] += d_goa
        v5g_ref[6:7, :] += d_gob

        @pl.when(i == nt - 1)
        def _():
            row = lax.broadcasted_iota(jnp.int32, (CHUNK, CHUNK), 0)
            col = lax.broadcasted_iota(jnp.int32, (CHUNK, CHUNK), 1)
            for hd in range(NHEAD):
                gws_ref[hd] = jnp.where(col <= row, gws_ref[hd], 0.0)
            gbs_ref[...] = lax.dot_general(e_ref[...], dbs_acc[...], (((1,), (1,)), ((), ())),
                                           precision=lax.Precision.HIGHEST, preferred_element_type=F32)

    tile = pl.BlockSpec((tm, D), lambda i: (i, 0))
    ptile = pl.BlockSpec((tm, 4 * WA), lambda i: (i, 0))
    return _call(
        core, name=name, grid=(nt,), jobs=jobs,
        in_specs=[tile, tile, pl.BlockSpec((tm, 2 * WA), lambda i: (i, 0)), pl.BlockSpec((tm, WA), lambda i: (i, 0)),
                  pl.BlockSpec((None, 8, D), lambda i: (i // tps, 0, 0)), _const_spec((8, D)), _const_spec((D, D)),
                  _const_spec((8, WA)), _const_spec((NHEAD, CHUNK, CHUNK)), _const_spec((CHUNK, WA)),
                  _const_spec((8, WA))],
        out_specs=[ptile, tile, tile, pl.BlockSpec((None, 8, D), lambda i: (i // tps, 0, 0)),
                   pl.BlockSpec((8, D), lambda i: (0, 0)), pl.BlockSpec((8, WA), lambda i: (0, 0)),
                   pl.BlockSpec((NHEAD, CHUNK, CHUNK), lambda i: (0, 0, 0)), pl.BlockSpec((8, CHUNK), lambda i: (0, 0))],
        out_shape=[jax.ShapeDtypeStruct((T, 4 * WA), F32), jax.ShapeDtypeStruct((T, D), BF16),
                   jax.ShapeDtypeStruct((T, D), BF16), jax.ShapeDtypeStruct((nb, 8, D), F32),
                   jax.ShapeDtypeStruct((8, D), F32), jax.ShapeDtypeStruct((8, WA), F32),
                   jax.ShapeDtypeStruct((NHEAD, CHUNK, CHUNK), F32), jax.ShapeDtypeStruct((8, CHUNK), F32)],
        scratch=[pltpu.VMEM((CHUNK, WA), F32)],
        args=[dxo, ym, proj, conv, mod, gvec, w_mo, v512, ws, bias_full, esel])


def _mixer_bwd_b(dxo, x, dpart, proj, mod, gvec, w_mi, cw, tm, name, jobs=()):
    T = x.shape[0]
    nt = T // tm
    nb = mod.shape[0]
    tps = nt // nb
    hpt = tm // HALO
    nh = T // HALO
    off = HALO - (CONV_K - 1)
    p = _pitch(tm)
    ext_rows = 8 * p

    def core(ins, outs, scs):
        dxo_ref, x_ref, dpart_ref, dnext_ref, ag_ref, halo_ref, mod_ref, g_ref, wmi_ref, cw_ref = ins
        dx_ref, dproj_ref, hb_ref, mg_ref, vg_ref, dcw_ref = outs
        glu_ext, dconv_ext, dglu_scr, dcw_acc = scs
        i = pl.program_id(0)
        first = i % tps == 0
        last = i % tps == tps - 1
        a = ag_ref[:, 0:WA]
        g = ag_ref[:, WA:2 * WA]
        sgg = _sigmoid(g)

        @pl.when(i == 0)
        def _():
            glu_ext[:, HALO + tm:HALO + ext_rows, :] = jnp.zeros((4, ext_rows - tm, 128), F32)
            dconv_ext[:, HALO + tm:HALO + ext_rows, :] = jnp.zeros((4, ext_rows - tm, 128), F32)
            dcw_acc[...] = jnp.zeros((32, 8, WA), F32)
            vg_ref[...] = jnp.zeros((8, D), F32)

        _to_slabs(glu_ext, 0, jnp.where(first, 0.0, halo_ref[:, 0:WA] * _sigmoid(halo_ref[:, WA:2 * WA])))
        _to_slabs(glu_ext, HALO, a * sgg)
        _to_slabs(dconv_ext, 0, dpart_ref[:, 2 * WA:3 * WA])
        _to_slabs(dconv_ext, tm, jnp.where(last, 0.0, dnext_ref[...]))
        sub = lax.broadcasted_iota(jnp.int32, (8, 128), 0)
        for s in range(4):
            accs = [jnp.zeros((8, 128), F32)] * CONV_K
            for v in range(p):
                dc = jnp.where(v + p * sub < tm, dconv_ext[s, pl.ds(v, 8, stride=p), :], 0.0)
                for k in range(CONV_K):
                    accs[k] = accs[k] + dc * glu_ext[s, pl.ds(v + off + k, 8, stride=p), :]
            for k in range(CONV_K):
                dcw_acc[k, :, _lanes(s)] += accs[k]
        dglu = _tap_sum(dconv_ext, dglu_scr, cw_ref, jnp.zeros((1, WA), F32), tm, lambda k: (CONV_K - 1) - k)

        @pl.when(i == nt - 1)
        def _():
            for k in range(CONV_K):
                dcw_ref[k:k + 1, :] = jnp.sum(dcw_acc[k], axis=0, keepdims=True)
            dcw_ref[CONV_K:32, :] = jnp.zeros((32 - CONV_K, WA), F32)

        da = dglu * sgg
        dgg = dglu * a * (sgg * (1.0 - sgg))
        dproj_ref[:, 0:2 * WA] = dpart_ref[:, 0:2 * WA].astype(BF16)
        dproj_ref[:, 2 * WA:3 * WA] = da.astype(BF16)
        dproj_ref[:, 3 * WA:4 * WA] = dgg.astype(BF16)
        dh = jnp.zeros((tm, D), F32)
        for j in range(NDEV):
            dh = dh + _dot_nt(dproj_ref[:, j * MB:(j + 1) * MB], wmi_ref[j])
        xv = x_ref[...]
        sc, sh = mod_ref[1:2, :], mod_ref[0:1, :]
        gpre = g_ref[0:1, :]
        r = lax.rsqrt(_rowmean(xv * xv) + EPS)
        xh = xv * r
        n = xh * gpre
        hb_ref[...] = (n * (1.0 + sc) + sh).astype(BF16)
        d_sc = _colsum(dh * n)
        d_sh = _colsum(dh)
        dn = dh * (1.0 + sc)
        d_gpre = _colsum(dn * xh)
        dxh = dn * gpre
        dx_ref[...] = dxo_ref[...] + r * (dxh - xh * _rowmean(dxh * xh))

        @pl.when(first)
        def _():
            mg_ref[...] = jnp.zeros((8, D), F32)

        mg_ref[0:1, :] += d_sh
        mg_ref[1:2, :] += d_sc
        vg_ref[0:1, :] += d_gpre

    tile = pl.BlockSpec((tm, D), lambda i: (i, 0))
    return _call(
        core, name=name, grid=(nt,), jobs=jobs,
        in_specs=[tile, tile, pl.BlockSpec((tm, 4 * WA), lambda i: (i, 0)),
                  pl.BlockSpec((HALO, WA), lambda i: (jnp.minimum((i + 1) * hpt, nh - 1), 2)),
                  pl.BlockSpec((tm, 2 * WA), lambda i: (i, 1)),
                  pl.BlockSpec((HALO, 2 * WA), lambda i: (jnp.maximum(i * hpt - 1, 0), 1)),
                  pl.BlockSpec((None, 8, D), lambda i: (i // tps, 0, 0)), _const_spec((8, D)),
                  _const_spec((NDEV, D, MB)), _const_spec((32, WA))],
        out_specs=[tile, pl.BlockSpec((tm, 4 * WA), lambda i: (i, 0)), tile,
                   pl.BlockSpec((None, 8, D), lambda i: (i // tps, 0, 0)), pl.BlockSpec((8, D), lambda i: (0, 0)),
                   pl.BlockSpec((32, WA), lambda i: (0, 0))],
        out_shape=[jax.ShapeDtypeStruct((T, D), F32), jax.ShapeDtypeStruct((T, 4 * WA), BF16),
                   jax.ShapeDtypeStruct((T, D), BF16), jax.ShapeDtypeStruct((nb, 8, D), F32),
                   jax.ShapeDtypeStruct((8, D), F32), jax.ShapeDtypeStruct((32, WA), F32)],
        scratch=[pltpu.VMEM((4, HALO + ext_rows, 128), F32), pltpu.VMEM((4, HALO + ext_rows, 128), F32),
                 pltpu.VMEM((4, ext_rows, 128), F32), pltpu.VMEM((32, 8, WA), F32)],
        args=[dxo, x, dpart, dpart, proj, proj, mod, gvec, w_mi, cw])


def _grad_chip(a, b, a_spec, b_spec, prod_shape, half, name, jobs=()):
    steps = 8 if half is None else 4
    R = prod_shape[0] if half is None else half
    C = prod_shape[1]

    def core(ins, outs, scs):
        a_ref, b_ref = ins
        (o_ref,) = outs
        own, snd, rcv, ssem, rsem, lsem = scs
        s = pl.program_id(0)
        c = lax.axis_index("c")
        me = _me()
        sib = _flip(me, (0, 0, 1))
        prod = _dot_tn(a_ref[...], b_ref[...]).astype(BF16)
        if half is None:
            q = s // 2

            @pl.when(s % 2 == c)
            def _():
                own[q] = prod

            @pl.when(s % 2 != c)
            def _():
                snd[q] = prod
                _remote(snd.at[q], rcv.at[q], ssem.at[q], rsem.at[q], sib).start()
        else:
            lo = prod[0:half, :]
            hi = prod[half:2 * half, :]
            own[s] = jnp.where(c == 0, lo, hi)
            snd[s] = jnp.where(c == 0, hi, lo)
            _remote(snd.at[s], rcv.at[s], ssem.at[s], rsem.at[s], sib).start()

        @pl.when(s == steps - 1)
        def _():
            for q4 in range(4):
                cp = _remote(snd.at[q4], rcv.at[q4], ssem.at[q4], rsem.at[q4], sib)
                cp.wait_recv()
                cp.wait_send()
                snd[q4] = (own[q4].astype(F32) + rcv[q4].astype(F32)).astype(BF16)
            out = pltpu.make_async_copy(snd, o_ref, lsem)
            out.start()
            out.wait()

    return _call(
        core, name=name, grid=(steps,), jobs=jobs, in_specs=[a_spec, b_spec], out_specs=[HBM],
        out_shape=[jax.ShapeDtypeStruct((4, R, C), BF16)],
        scratch=[pltpu.VMEM((4, R, C), BF16), pltpu.VMEM((4, R, C), BF16), pltpu.VMEM((4, R, C), BF16),
                 pltpu.SemaphoreType.DMA((4,)), pltpu.SemaphoreType.DMA((4,)), pltpu.SemaphoreType.DMA],
        args=[a, b])


def _grad_w_in(dg, hb, name, jobs=()):
    T = hb.shape[0]
    return _grad_chip(dg, hb, pl.BlockSpec((None, T, FBP), lambda s: (s, 0, 0)), _const_spec((T, D)),
                      (FBP, D), None, name, jobs)


def _grad_w_out(act, dyb, name, jobs=()):
    T = dyb.shape[0]
    return _grad_chip(act, dyb, pl.BlockSpec((None, T, FBP), lambda s: (s, 0, 0)), _const_spec((T, D)),
                      (FBP, D), FO, name, jobs)


def _grad_w_mi(hb, dproj, name, jobs=()):
    T = hb.shape[0]
    return _grad_chip(hb, dproj, _const_spec((T, D)), pl.BlockSpec((T, MB), lambda s: (0, s)),
                      (D, MB), None, name, jobs)


def _grad_w_mo(ycat, dym, name, jobs=()):
    T = ycat.shape[0]
    return _grad_chip(ycat, dym, pl.BlockSpec((T, 2 * MO), lambda s: (0, s)), _const_spec((T, D)),
                      (2 * MO, D), MO, name, jobs)


def _adamw_math(w, g, m, v):
    m2 = ADAM_B1 * m + (1.0 - ADAM_B1) * g
    v2 = ADAM_B2 * v + (1.0 - ADAM_B2) * (g * g)
    m_hat = m2 / (1.0 - ADAM_B1 ** ADAM_STEP)
    v_hat = v2 / (1.0 - ADAM_B2 ** ADAM_STEP)
    delta = -ADAM_LR * (m_hat / (jnp.sqrt(v_hat) + ADAM_EPS) + ADAM_WD * w)
    return delta, m2, v2


def _adamw_reduce(parts, w, m, v, tr, name):
    R, C = w.shape

    def core(ins, outs, _):
        p_ref, w_ref, m_ref, v_ref = ins
        g_ref, d_ref, m2_ref, v2_ref = outs
        g = p_ref[0].astype(F32)
        for s in range(1, 4):
            g = g + p_ref[s].astype(F32)
        g_ref[...] = g
        d_ref[...], m2_ref[...], v2_ref[...] = _adamw_math(w_ref[...], g, m_ref[...], v_ref[...])

    blk = pl.BlockSpec((tr, C), lambda i: (i, 0))
    return _call(
        core, name=name, grid=(R // tr,),
        in_specs=[pl.BlockSpec((4, tr, C), lambda i: (0, i, 0)), blk, blk, blk],
        out_specs=[blk, blk, blk, blk], out_shape=[jax.ShapeDtypeStruct((R, C), F32)] * 4,
        args=[parts, w, m, v])[0]


def _adamw_ada(sc_all, dd, w, m, v, tr, name):
    R, C = w.shape

    def core(ins, outs, _):
        sc_ref, dd_ref, w_ref, m_ref, v_ref = ins
        g_ref, d_ref, m2_ref, v2_ref = outs
        g = _dot_tn(sc_ref[...].astype(BF16), dd_ref[...].astype(BF16))
        g_ref[...] = g
        d_ref[...], m2_ref[...], v2_ref[...] = _adamw_math(w_ref[...], g, m_ref[...], v_ref[...])

    blk = pl.BlockSpec((tr, C), lambda i: (i, 0))
    return _call(
        core, name=name, grid=(R // tr,),
        in_specs=[pl.BlockSpec((64, tr), lambda i: (0, i)), pl.BlockSpec((64, C), lambda i: (0, 0)), blk, blk, blk],
        out_specs=[blk, blk, blk, blk], out_shape=[jax.ShapeDtypeStruct((R, C), F32)] * 4,
        args=[sc_all, dd, w, m, v])[0]


def _adamw_small(gathered, plain, grads, wmv, emit, name):
    nw = len(grads)
    ng, npl, ne = len(gathered), len(plain), len(emit)

    def core(ins, outs, _):
        srcs = []
        for a in range(ng):
            s = ins[a][0]
            for dev in range(1, NDEV):
                s = s + ins[a][dev]
            srcs.append(s)
        srcs += [ins[ng + a][...] for a in range(npl)]
        w_refs = ins[ng + npl:]
        for e, a in enumerate(emit):
            outs[e][...] = srcs[a]
        for t in range(nw):
            src, row = grads[t]
            g = srcs[src] if row is None else srcs[src][row:row + 1, :]
            w_ref, m_ref, v_ref = w_refs[3 * t:3 * t + 3]
            g_ref, d_ref, m2_ref, v2_ref = outs[ne + 4 * t:ne + 4 * t + 4]
            g_ref[...] = g
            d_ref[...], m2_ref[...], v2_ref[...] = _adamw_math(w_ref[...], g, m_ref[...], v_ref[...])

    out_shape = [jax.ShapeDtypeStruct(gathered[a].shape[1:], F32) for a in emit]
    for t in range(nw):
        out_shape += [jax.ShapeDtypeStruct(wmv[3 * t].shape, F32)] * 4
    return _call(
        core, name=name, grid=(), in_specs=[VM] * (ng + npl + 3 * nw), out_specs=[VM] * (ne + 4 * nw),
        out_shape=out_shape, args=list(gathered) + list(plain) + list(wmv))[0]


def _ada_fwd(c_pad, w_ada, b_cols, cw_pad, jobs=()):
    def core(ins, outs, scs):
        c_ref, w_ref, b_ref, cwp_ref = ins
        ada_ref, sc_ref, cw_ref = outs
        cbuf, send_buf, ssem, rsem = scs
        me = _me()
        mi = _lin(me)
        cbuf[mi] = c_ref[...]
        cw_ref[mi] = cwp_ref[...]
        peers = [_flip(me, f) for f in FLIPS]
        first = []
        for k, p in enumerate(peers):
            first.append(_remote(cbuf.at[mi], cbuf.at[mi], ssem.at[k], rsem.at[k], p))
            first.append(_remote(cw_ref.at[mi], cw_ref.at[mi], ssem.at[7 + k], rsem.at[7 + k], p))
        for cp in first:
            cp.start()
        for k, p in enumerate(peers):
            pi = _lin(p)
            _remote(cbuf.at[pi], cbuf.at[pi], ssem.at[k], rsem.at[k], p).wait_recv()
            _remote(cw_ref.at[pi], cw_ref.at[pi], ssem.at[7 + k], rsem.at[7 + k], p).wait_recv()
        c_all = cbuf[...].reshape(8 * 8, D)
        sc = c_all * _sigmoid(c_all)
        sc_ref[...] = sc
        res = _dot(sc.astype(BF16), w_ref[...].astype(BF16)) + b_ref[...]
        send_buf[...] = res.reshape(8, 8, ADA_B)
        ada_ref[mi] = send_buf[mi]
        second = []
        for k, p in enumerate(peers):
            second.append(_remote(send_buf.at[_lin(p)], ada_ref.at[mi], ssem.at[14 + k], rsem.at[14 + k], p))
        for cp in second:
            cp.start()
        for k, p in enumerate(peers):
            _remote(send_buf.at[mi], ada_ref.at[_lin(p)], ssem.at[14 + k], rsem.at[14 + k], p).wait_recv()
        for cp in first + second:
            cp.wait_send()

    return _call(
        core, name="ada_fwd", grid=(), jobs=jobs, in_specs=[VM, VM, VM, VM], out_specs=[VM, VM, VM],
        out_shape=[jax.ShapeDtypeStruct((8, 8, ADA_B), F32), jax.ShapeDtypeStruct((64, D), F32),
                   jax.ShapeDtypeStruct((8, 32, 64), F32)],
        scratch=[pltpu.VMEM((8, 8, D), F32), pltpu.VMEM((8, 8, ADA_B), F32),
                 pltpu.SemaphoreType.DMA((21,)), pltpu.SemaphoreType.DMA((21,))],
        args=[c_pad, w_ada, b_cols, cw_pad])


def _ada_bwd(dada):
    def body(d_ref, dd_ref, gb_ref, rbuf, ssem, rsem):
        me = _me()
        mi = _lin(me)
        peers = [_flip(me, f) for f in FLIPS]
        rbuf[mi] = d_ref[mi]
        first = []
        for k, p in enumerate(peers):
            first.append(_remote(d_ref.at[_lin(p)], rbuf.at[mi], ssem.at[k], rsem.at[k], p))
        for cp in first:
            cp.start()
        for k, p in enumerate(peers):
            _remote(d_ref.at[mi], rbuf.at[_lin(p)], ssem.at[k], rsem.at[k], p).wait_recv()
        dd = rbuf[...].reshape(64, ADA_B)
        dd_ref[...] = dd
        gb_ref[mi] = jnp.broadcast_to(_colsum(dd), (8, ADA_B))
        second = []
        for k, p in enumerate(peers):
            second.append(_remote(gb_ref.at[mi], gb_ref.at[mi], ssem.at[7 + k], rsem.at[7 + k], p))
        for cp in second:
            cp.start()
        for k, p in enumerate(peers):
            pi = _lin(p)
            _remote(gb_ref.at[pi], gb_ref.at[pi], ssem.at[7 + k], rsem.at[7 + k], p).wait_recv()
        for cp in first + second:
            cp.wait_send()

    return pl.pallas_call(
        body,
        name="ada_bwd",
        in_specs=[VM],
        out_specs=[VM, VM],
        out_shape=[jax.ShapeDtypeStruct((64, ADA_B), F32), jax.ShapeDtypeStruct((8, 8, ADA_B), F32)],
        scratch_shapes=[
            pltpu.VMEM((8, 8, ADA_B), F32),
            pltpu.SemaphoreType.DMA((14,)),
            pltpu.SemaphoreType.DMA((14,)),
        ],
        compiler_params=pltpu.CompilerParams(vmem_limit_bytes=VMEM_LIMIT),
    )(dada)


SMALL_D = ("g_pre_f1", "g_post_f1", "g_pre_m", "g_post_m", "g_pre_f2", "g_post_f2")
SMALL_W = ("gmlp_norm_g", "gmlp_norm_b", "conv_b", "conv_norm_g", "conv_norm_b", "g_out_a", "g_out_b")


def kernel(x, c, w_ada, b_ada, g_pre_f1, g_post_f1, w_f1_in, w_f1_out, g_pre_m, g_post_m, w_mix_in, gmlp_norm_g, gmlp_norm_b, w_spatial, b_spatial, conv_w, conv_b, conv_norm_g, conv_norm_b, g_out_a, g_out_b, w_mix_out, g_pre_f2, g_post_f2, w_f2_in, w_f2_out, loss_target, m_w_ada, m_b_ada, m_g_pre_f1, m_g_post_f1, m_w_f1_in, m_w_f1_out, m_g_pre_m, m_g_post_m, m_w_mix_in, m_gmlp_norm_g, m_gmlp_norm_b, m_w_spatial, m_b_spatial, m_conv_w, m_conv_b, m_conv_norm_g, m_conv_norm_b, m_g_out_a, m_g_out_b, m_w_mix_out, m_g_pre_f2, m_g_post_f2, m_w_f2_in, m_w_f2_out, v_w_ada, v_b_ada, v_g_pre_f1, v_g_post_f1, v_w_f1_in, v_w_f1_out, v_g_pre_m, v_g_post_m, v_w_mix_in, v_gmlp_norm_g, v_gmlp_norm_b, v_w_spatial, v_b_spatial, v_conv_w, v_conv_b, v_conv_norm_g, v_conv_norm_b, v_g_out_a, v_g_out_b, v_w_mix_out, v_g_pre_f2, v_g_post_f2, v_w_f2_in, v_w_f2_out):
    given = dict(locals())
    bl, seq, _ = x.shape
    T = bl * seq
    tm = min(256, seq // 2)
    mi = _lin((lax.axis_index("x"), lax.axis_index("y"), lax.axis_index("c")))

    def shard_in(w):
        return jnp.pad(w[0].T.astype(BF16), ((0, FBP - FB), (0, 0)))

    zpad = jnp.zeros((max(FBP - FB, 16), D), BF16)
    g_f1 = _Gather([shard_in(w_f1_in), w_f1_out[0].astype(BF16)], ("rows", "out"), zpad)
    g_mx = _Gather([w_mix_in[0].astype(BF16), w_mix_out[0].astype(BF16), w_f2_out[0].astype(BF16)],
                   ("rows", "rows", "out"), zpad)
    g_f2 = _Gather([shard_in(w_f2_in)], ("rows",), zpad, late_mid=True)

    c_pad = jnp.pad(c, ((0, 8 - bl), (0, 0)))
    b_cols = lax.dynamic_slice(b_ada, (0, mi * ADA_B), (1, ADA_B))
    cw_pad = jnp.pad(conv_w[0], ((0, 1), (0, 0)))
    (ada_blk, sc_all, cw_all), ((wi1, wo1),) = _ada_fwd(c_pad, w_ada[0], b_cols, cw_pad, jobs=[g_f1])
    ada = ada_blk[:, 0:bl, :].transpose(1, 0, 2).reshape(bl, 9, D)
    pad5 = jnp.zeros((bl, 5, D), F32)
    mod1 = jnp.concatenate([ada[:, 0:3], pad5], axis=1)
    mod2 = jnp.concatenate([ada[:, 3:6], pad5], axis=1)
    mod3 = jnp.concatenate([ada[:, 6:9], pad5], axis=1)
    cw_full = cw_all.transpose(1, 0, 2).reshape(32, WA)

    zrow = jnp.zeros((1, D), F32)
    gv1 = jnp.concatenate([g_pre_f1, g_post_f1] + [zrow] * 6, axis=0)
    gvm = jnp.concatenate([g_pre_m, g_post_m] + [zrow] * 6, axis=0)
    gv2 = jnp.concatenate([g_pre_f2, g_post_f2] + [zrow] * 6, axis=0)
    v512 = jnp.concatenate([gmlp_norm_g, gmlp_norm_b, conv_b, conv_norm_g, conv_norm_b, g_out_a, g_out_b,
                            jnp.zeros((1, WA), F32)], axis=0)
    ws = w_spatial[0]
    bias_full = jnp.repeat(b_spatial[0].T, HD, axis=1)
    esel = (lax.broadcasted_iota(jnp.int32, (8, WA), 1) // HD == lax.broadcasted_iota(jnp.int32, (8, WA), 0)).astype(F32)

    x0 = x.reshape(T, D)
    (x1, gu1, y1), ((wmi, wmo, wo2),) = _ffn_fwd(x0, mod1, gv1, wi1, wo1, tm, "ffn1_fwd", jobs=[g_mx])
    wmo = wmo.reshape(D, D)
    (x2, proj, ym, conv), ((wi2,),) = _mixer_fwd(x1, mod2, gvm, wmi, wmo, v512, ws, bias_full, cw_full, tm, "mixer_fwd", jobs=[g_f2])
    (dx3, gu2, y2, loss_blk), _ = _ffn_fwd(x2, mod3, gv2, wi2, wo2, tm, "ffn2_fwd", target=loss_target.reshape(T, D))

    (dx2, dg2, act2, hb2, dyb2, mg3, vg3), _ = _ffn_bwd(dx3, x2, y2, gu2, mod3, gv2, wi2, wo2, tm, "ffn2_bwd")
    (g_wi2,), _ = _grad_w_in(dg2, hb2, "ffn2_gw_in")
    (g_wo2,), _ = _grad_w_out(act2, dyb2, "ffn2_gw_out")
    (dpart, dymb, ycat, mg2a, vgma, v5g, gws, gbs), ((p_wi2,),) = _mixer_bwd_a(
        dx2, ym, proj, conv, mod2, gvm, wmo, v512, ws, bias_full, esel, tm, "mixer_bwd_a",
        jobs=[_ChipScatter([g_wi2])])
    (dx1, dproj, hbm, mg2b, vgmb, dcw), ((p_wo2,),) = _mixer_bwd_b(
        dx2, x1, dpart, proj, mod2, gvm, wmi, cw_full, tm, "mixer_bwd_b", jobs=[_ChipScatter([g_wo2])])
    (g_wmi,), _ = _grad_w_mi(hbm, dproj, "mixer_gw_in")
    (g_wmo,), _ = _grad_w_mo(ycat, dymb, "mixer_gw_out")
    p2 = jnp.concatenate([v5g, dcw], axis=0)
    (dx0, dg1, act1, hb1, dyb1, mg1, vg1), ((p_wmi, p_wmo), (a2, a3, a4)) = _ffn_bwd(
        dx1, x0, y1, gu1, mod1, gv1, wi1, wo1, tm, "ffn1_bwd",
        jobs=[_ChipScatter([g_wmi, g_wmo]), _AllGather([p2, gws, gbs])])
    (g_wo1,), _ = _grad_w_out(act1, dyb1, "ffn1_gw_out")
    (g_wi1,), ((p_wo1,),) = _grad_w_in(dg1, hb1, "ffn1_gw_in", jobs=[_ChipScatter([g_wo1])])

    dada = jnp.concatenate([mg1[:, 0:3], mg2b[:, 0:2], mg2a[:, 2:3], mg3[:, 0:3]], axis=1)
    dada = dada.reshape(bl, NDEV, ADA_B).transpose(1, 0, 2)
    dada = jnp.pad(dada, ((0, 0), (0, 8 - bl), (0, 0)))
    dd_all, gb_all = _ada_bwd(dada)
    g_bada = gb_all[:, 0, :].reshape(1, 9 * D)

    p1 = jnp.concatenate([vg1[0:2], vgmb[0:1], vgma[1:2], vg3[0:2], loss_blk[0:1], zrow], axis=0)
    (p_wi1,), (a1,) = _call(None, name="tail_exchange", grid=(), in_specs=[], out_specs=[], out_shape=[], args=[],
                            jobs=[_ChipScatter([g_wi1]), _AllGather([p1])])[1]

    res = {}
    for nm, part in (("w_f1_in", p_wi1), ("w_f2_in", p_wi2)):
        quad = _adamw_reduce(part, given[nm][0].T, given["m_" + nm][0].T, given["v_" + nm][0].T, FO, "adamw_" + nm)
        res[nm] = tuple(t.T[None] for t in quad)
    for nm, part, tr in (("w_f1_out", p_wo1, FO), ("w_f2_out", p_wo2, FO), ("w_mix_in", p_wmi, 256), ("w_mix_out", p_wmo, MO)):
        quad = _adamw_reduce(part, given[nm][0], given["m_" + nm][0], given["v_" + nm][0], tr, "adamw_" + nm)
        res[nm] = tuple(t[None] for t in quad)
    quad = _adamw_ada(sc_all, dd_all, w_ada[0], m_w_ada[0], v_w_ada[0], 256, "adamw_w_ada")
    res["w_ada"] = tuple(t[None] for t in quad)

    small = SMALL_D + SMALL_W + ("w_spatial", "b_spatial", "b_ada")
    grads = [(0, r) for r in range(6)] + [(1, r) for r in range(7)] + [(2, None), (3, None), (4, None)]
    wmv = []
    for nm in small:
        for pre in ("", "m_", "v_"):
            wmv.append(given[pre + nm][0] if nm in ("w_spatial", "b_spatial") else given[pre + nm])
    outs = _adamw_small([a1, a2, a3, a4], [g_bada], grads, wmv, (0, 1), "adamw_small")
    loss = outs[0][6, 0]
    for t, nm in enumerate(small):
        quad = outs[2 + 4 * t:6 + 4 * t]
        res[nm] = tuple(q[None] for q in quad) if nm in ("w_spatial", "b_spatial") else tuple(quad)
    g_cw = lax.dynamic_slice(outs[1], (8, mi * 64), (32, 64))
    wmv = [jnp.pad(given[pre + "conv_w"][0], ((0, 1), (0, 0)), constant_values=1.0 if pre == "v_" else 0.0)
           for pre in ("", "m_", "v_")]
    quad = _adamw_small([], [g_cw], [(0, None)], wmv, (), "adamw_conv_w")
    res["conv_w"] = tuple(q[0:CONV_K][None] for q in quad)

    order = ["w_ada", "b_ada", "g_pre_f1", "g_post_f1", "w_f1_in", "w_f1_out", "g_pre_m", "g_post_m", "w_mix_in",
             "gmlp_norm_g", "gmlp_norm_b", "w_spatial", "b_spatial", "conv_w", "conv_b", "conv_norm_g", "conv_norm_b",
             "g_out_a", "g_out_b", "w_mix_out", "g_pre_f2", "g_post_f2", "w_f2_in", "w_f2_out"]
    out = [loss, dx0.reshape(bl, seq, D)]
    for k in range(4):
        out += [res[nm][k] for nm in order]
    return tuple(out)
```

```python
import jax
import jax.numpy as jnp
from jax import lax
from jax.experimental import pallas as pl
from jax.experimental.pallas import tpu as pltpu

F32 = jnp.float32
BF16 = jnp.bfloat16

D = 1024
DFF = 2816
NDEV = 8
FB = 2 * DFF // NDEV
FBP = 704
FO = DFF // NDEV
WA = 512
NHEAD = 8
HD = 64
CHUNK = 128
CONV_K = 31
HALO = 32
MB = 2 * (WA + WA) // NDEV
MO = D // NDEV
ADA_B = 9 * D // NDEV
EPS = 1e-6
HALF = 0.5

ADAM_LR = 0.001
ADAM_B1 = 0.9
ADAM_B2 = 0.999
ADAM_EPS = 1e-08
ADAM_WD = 0.01
ADAM_STEP = 10

VMEM_LIMIT = 56 * 1024 * 1024
MESH = pl.DeviceIdType.MESH
FLIPS = ((0, 0, 1), (1, 0, 0), (0, 1, 0), (1, 1, 0), (1, 0, 1), (0, 1, 1), (1, 1, 1))
CHIP_FLIPS = ((1, 0, 0), (0, 1, 0), (1, 1, 0))
HBM = pl.BlockSpec(memory_space=pl.ANY)
VM = pl.BlockSpec(memory_space=pltpu.VMEM)


def _dot(a, b):
    return lax.dot_general(a, b, (((1,), (0,)), ((), ())), preferred_element_type=F32)


def _dot_nt(a, b):
    return lax.dot_general(a, b, (((1,), (1,)), ((), ())), preferred_element_type=F32)


def _dot_tn(a, b):
    return lax.dot_general(a, b, (((0,), (0,)), ((), ())), preferred_element_type=F32)


def _rowmean(v):
    return jnp.mean(v, axis=-1, keepdims=True)


def _colsum(v):
    return jnp.sum(v, axis=0, keepdims=True)


def _sigmoid(v):
    return 0.5 * jnp.tanh(0.5 * v) + 0.5


def _const_spec(shape):
    nd = len(shape)
    return pl.BlockSpec(shape, lambda *_: (0,) * nd, pipeline_mode=pl.Buffered(1))


def _me():
    return lax.axis_index("x"), lax.axis_index("y"), lax.axis_index("c")


def _flip(me, f):
    return tuple(1 - v if b else v for v, b in zip(me, f))


def _lin(p):
    return 4 * p[0] + 2 * p[1] + p[2]


def _remote(src, dst, send_sem, recv_sem, dev):
    return pltpu.make_async_remote_copy(src_ref=src, dst_ref=dst, send_sem=send_sem, recv_sem=recv_sem,
                                        device_id=dev, device_id_type=MESH)


def _blk(kind, ref, p):
    if kind == "out":
        return ref.at[2 * p[0] + p[1], pl.ds(p[2] * FO, FO), :]
    return ref.at[_lin(p)]


class _Gather:
    def __init__(self, shards, kinds, zpad, late_mid=False):
        self.late_mid = late_mid
        self.kinds = kinds
        self.n = len(shards)
        self.ins = list(shards) + [zpad]
        self.out_shape = [jax.ShapeDtypeStruct((4, FBP, D) if k == "out" else (NDEV,) + s.shape, s.dtype)
                          for s, k in zip(shards, kinds)]
        self.n_out = sum(k == "out" for k in kinds)
        self.sems = [pltpu.SemaphoreType.DMA((7 * self.n,)), pltpu.SemaphoreType.DMA((7 * self.n,)),
                     pltpu.SemaphoreType.DMA((self.n + 4 * max(self.n_out, 1),))]

    def _first(self, ins, outs, sems):
        ssem, rsem, lsem = sems
        me = _me()
        sib = _flip(me, (0, 0, 1))
        cps, loc = [], []
        nz = 0
        for a in range(self.n):
            mine = _blk(self.kinds[a], outs[a], me)
            loc.append(pltpu.make_async_copy(ins[a], mine, lsem.at[a]))
            if self.kinds[a] == "out" and FBP > FB:
                for q in range(4):
                    loc.append(pltpu.make_async_copy(ins[self.n], outs[a].at[q, pl.ds(FB, FBP - FB), :],
                                                     lsem.at[self.n + 4 * nz + q]))
                nz += 1
            cps.append(_remote(ins[a], mine, ssem.at[7 * a], rsem.at[7 * a], sib))
            for j, f in enumerate(CHIP_FLIPS):
                cps.append(_remote(ins[a], mine, ssem.at[7 * a + 1 + j], rsem.at[7 * a + 1 + j], _flip(me, f)))
        return cps, loc

    def _passed(self, outs, sems):
        ssem, rsem, _ = sems
        me = _me()
        sib = _flip(me, (0, 0, 1))
        cps = []
        for j, f in enumerate(CHIP_FLIPS):
            for a in range(self.n):
                blk = _blk(self.kinds[a], outs[a], _flip(me, f))
                cps.append(_remote(blk, blk, ssem.at[7 * a + 4 + j], rsem.at[7 * a + 4 + j], sib))
        return cps

    def start(self, ins, outs, sems):
        cps, loc = self._first(ins, outs, sems)
        for cp in loc + cps:
            cp.start()

    def mid(self, ins, outs, sems):
        ssem, rsem, _ = sems
        me = _me()
        passed = self._passed(outs, sems)
        t = 0
        for j, f in enumerate(CHIP_FLIPS):
            for a in range(self.n):
                blk = _blk(self.kinds[a], outs[a], _flip(me, f))
                _remote(blk, blk, ssem.at[7 * a + 1 + j], rsem.at[7 * a + 1 + j], _flip(me, f)).wait_recv()
                passed[t].start()
                t += 1

    def end(self, ins, outs, sems):
        ssem, rsem, _ = sems
        me = _me()
        sib = _flip(me, (0, 0, 1))
        for a in range(self.n):
            blk = _blk(self.kinds[a], outs[a], sib)
            _remote(blk, blk, ssem.at[7 * a], rsem.at[7 * a], sib).wait_recv()
            for j, f in enumerate(CHIP_FLIPS):
                blk = _blk(self.kinds[a], outs[a], _flip(_flip(me, f), (0, 0, 1)))
                _remote(blk, blk, ssem.at[7 * a + 4 + j], rsem.at[7 * a + 4 + j], sib).wait_recv()
        cps, loc = self._first(ins, outs, sems)
        for cp in cps + self._passed(outs, sems):
            cp.wait_send()
        for cp in loc:
            cp.wait()


class _ChipScatter:
    def __init__(self, grads):
        self.n = len(grads)
        self.ins = list(grads)
        self.out_shape = [jax.ShapeDtypeStruct(g.shape, BF16) for g in grads]
        self.sems = [pltpu.SemaphoreType.DMA((3 * self.n,)), pltpu.SemaphoreType.DMA((3 * self.n,)),
                     pltpu.SemaphoreType.DMA((self.n,))]

    def _copies(self, ins, outs, sems):
        ssem, rsem, lsem = sems
        me = _me()
        mq = 2 * me[0] + me[1]
        loc = [pltpu.make_async_copy(ins[a].at[mq], outs[a].at[mq], lsem.at[a]) for a in range(self.n)]
        cps = []
        for k, f in enumerate(CHIP_FLIPS):
            p = _flip(me, f)
            for a in range(self.n):
                cps.append(_remote(ins[a].at[2 * p[0] + p[1]], outs[a].at[mq], ssem.at[3 * a + k], rsem.at[3 * a + k], p))
        return cps, loc

    def start(self, ins, outs, sems):
        cps, loc = self._copies(ins, outs, sems)
        for cp in loc + cps:
            cp.start()

    mid = None

    def end(self, ins, outs, sems):
        ssem, rsem, _ = sems
        me = _me()
        mq = 2 * me[0] + me[1]
        for k, f in enumerate(CHIP_FLIPS):
            p = _flip(me, f)
            for a in range(self.n):
                _remote(ins[a].at[mq], outs[a].at[2 * p[0] + p[1]], ssem.at[3 * a + k], rsem.at[3 * a + k], p).wait_recv()
        cps, loc = self._copies(ins, outs, sems)
        for cp in cps:
            cp.wait_send()
        for cp in loc:
            cp.wait()


class _AllGather:
    def __init__(self, parts):
        self.n = len(parts)
        self.ins = list(parts)
        self.out_shape = [jax.ShapeDtypeStruct((NDEV,) + p.shape, p.dtype) for p in parts]
        self.sems = [pltpu.SemaphoreType.DMA((7 * self.n,)), pltpu.SemaphoreType.DMA((7 * self.n,)),
                     pltpu.SemaphoreType.DMA((self.n,))]

    def _copies(self, ins, outs, sems):
        ssem, rsem, lsem = sems
        me = _me()
        mi = _lin(me)
        loc = [pltpu.make_async_copy(ins[a], outs[a].at[mi], lsem.at[a]) for a in range(self.n)]
        cps = []
        for k, f in enumerate(FLIPS):
            for a in range(self.n):
                cps.append(_remote(ins[a], outs[a].at[mi], ssem.at[7 * a + k], rsem.at[7 * a + k], _flip(me, f)))
        return cps, loc

    def start(self, ins, outs, sems):
        cps, loc = self._copies(ins, outs, sems)
        for cp in loc + cps:
            cp.start()

    mid = None

    def end(self, ins, outs, sems):
        ssem, rsem, _ = sems
        me = _me()
        for k, f in enumerate(FLIPS):
            p = _flip(me, f)
            for a in range(self.n):
                _remote(ins[a], outs[a].at[_lin(p)], ssem.at[7 * a + k], rsem.at[7 * a + k], p).wait_recv()
        cps, loc = self._copies(ins, outs, sems)
        for cp in cps:
            cp.wait_send()
        for cp in loc:
            cp.wait()


def _call(core, *, name, grid, in_specs, out_specs, out_shape, args, scratch=(), jobs=()):
    n_in, n_out, n_sc = len(in_specs), len(out_specs), len(scratch)
    steps = 1
    for g in grid:
        steps *= g

    def body(*refs):
        pos = [0]

        def take(k):
            r = refs[pos[0]:pos[0] + k]
            pos[0] += k
            return r

        ins = take(n_in)
        j_ins = [take(len(j.ins)) for j in jobs]
        outs = take(n_out)
        j_outs = [take(len(j.out_shape)) for j in jobs]
        scs = take(n_sc)
        j_sems = [take(len(j.sems)) for j in jobs]
        if len(grid) == 2:
            step = pl.program_id(0) * grid[1] + pl.program_id(1)
        elif len(grid) == 1:
            step = pl.program_id(0)
        else:
            step = 0
        for j, ji, jo, js in zip(jobs, j_ins, j_outs, j_sems):
            if grid:
                pl.when(step == 0)(lambda j=j, ji=ji, jo=jo, js=js: j.start(ji, jo, js))
            else:
                j.start(ji, jo, js)
        for j, ji, jo, js in zip(jobs, j_ins, j_outs, j_sems):
            if j.mid is not None and grid:
                at = steps - 1 if j.late_mid else (3 * steps) // 4
                pl.when(step == at)(lambda j=j, ji=ji, jo=jo, js=js: j.mid(ji, jo, js))
        if core is not None:
            core(ins, outs, scs)
        for j, ji, jo, js in zip(jobs, j_ins, j_outs, j_sems):
            if grid:
                pl.when(step == steps - 1)(lambda j=j, ji=ji, jo=jo, js=js: j.end(ji, jo, js))
            else:
                if j.mid is not None:
                    j.mid(ji, jo, js)
                j.end(ji, jo, js)

    all_in = list(in_specs)
    all_args = list(args)
    all_out = list(out_specs)
    all_shape = list(out_shape)
    all_sc = list(scratch)
    for j in jobs:
        all_in += [HBM] * len(j.ins)
        all_args += j.ins
    for j in jobs:
        all_out += [HBM] * len(j.out_shape)
        all_shape += j.out_shape
        all_sc += j.sems
    params = dict(vmem_limit_bytes=VMEM_LIMIT)
    if grid:
        params["dimension_semantics"] = ("arbitrary",) * len(grid)
    res = pl.pallas_call(
        body, name=name, grid=grid, in_specs=all_in, out_specs=all_out, out_shape=all_shape,
        scratch_shapes=all_sc, compiler_params=pltpu.CompilerParams(**params),
    )(*all_args)
    core_res = list(res[:n_out])
    job_res = []
    pos = n_out
    for j in jobs:
        job_res.append(list(res[pos:pos + len(j.out_shape)]))
        pos += len(j.out_shape)
    return core_res, job_res


def _ffn_fwd(x, mod, gvec, w_in, w_out, tm, name, jobs=(), target=None):
    T = x.shape[0]
    nt = T // tm
    tps = nt // mod.shape[0]

    def core(ins, outs, _):
        x_ref, mod_ref, g_ref, win_ref, wout_ref = ins[:5]
        xo_ref, gu_ref, y_ref = outs[:3]
        xv = x_ref[...]
        sh, sc, gt = mod_ref[0:1, :], mod_ref[1:2, :], mod_ref[2:3, :]
        r = lax.rsqrt(_rowmean(xv * xv) + EPS)
        h = (xv * r * g_ref[0:1, :]) * (1.0 + sc) + sh
        hb = h.astype(BF16)
        y = jnp.zeros((tm, D), F32)
        for cidx in range(4):
            gate = _dot_nt(hb, win_ref[cidx])
            up = _dot_nt(hb, win_ref[4 + cidx])
            gu_ref[cidx] = gate.astype(BF16)
            gu_ref[4 + cidx] = up.astype(BF16)
            act = gate * _sigmoid(gate) * up
            y = y + _dot(act.astype(BF16), wout_ref[cidx])
        y_ref[...] = y
        ry = lax.rsqrt(_rowmean(y * y) + EPS)
        xo = xv + (HALF * gt) * (y * ry * g_ref[1:2, :])
        if target is None:
            xo_ref[...] = xo
        else:
            loss_ref = outs[3]

            @pl.when(pl.program_id(0) == 0)
            def _():
                loss_ref[...] = jnp.zeros((8, D), F32)

            err = xo - ins[5][...]
            xo_ref[...] = err * (1.0 / D)
            loss_ref[...] += HALF * jnp.sum(_rowmean(err * err), axis=0, keepdims=True)

    tile = pl.BlockSpec((tm, D), lambda i: (i, 0))
    extra = target is not None
    return _call(
        core, name=name, grid=(nt,), jobs=jobs,
        in_specs=[tile, pl.BlockSpec((None, 8, D), lambda i: (i // tps, 0, 0)), _const_spec((8, D)),
                  _const_spec((8, FBP, D)), _const_spec((4, FBP, D))] + [tile] * extra,
        out_specs=[tile, pl.BlockSpec((8, tm, FBP), lambda i: (0, i, 0)), tile]
        + [pl.BlockSpec((8, D), lambda i: (0, 0))] * extra,
        out_shape=[jax.ShapeDtypeStruct((T, D), F32), jax.ShapeDtypeStruct((8, T, FBP), BF16),
                   jax.ShapeDtypeStruct((T, D), F32)] + [jax.ShapeDtypeStruct((8, D), F32)] * extra,
        args=[x, mod, gvec, w_in, w_out] + [target] * extra)


def _ffn_bwd(dxo, x, y, gu, mod, gvec, w_in, w_out, tm, name, jobs=()):
    T = x.shape[0]
    nt = T // tm
    nb = mod.shape[0]
    tps = nt // nb

    def core(ins, outs, _):
        dxo_ref, x_ref, y_ref, gu_ref, mod_ref, g_ref, win_ref, wout_ref = ins
        dx_ref, dg_ref, act_ref, hb_ref, dyb_ref, mg_ref, vg_ref = outs
        i = pl.program_id(0)
        xv = x_ref[...]
        dxo_v = dxo_ref[...]
        yv = y_ref[...]
        sh, sc, gt = mod_ref[0:1, :], mod_ref[1:2, :], mod_ref[2:3, :]
        gpre, gpost = g_ref[0:1, :], g_ref[1:2, :]
        r = lax.rsqrt(_rowmean(xv * xv) + EPS)
        xh = xv * r
        n = xh * gpre
        hb = (n * (1.0 + sc) + sh).astype(BF16)
        hb_ref[...] = hb
        ry = lax.rsqrt(_rowmean(yv * yv) + EPS)
        yh = yv * ry
        d_gt = _colsum(HALF * dxo_v * (yh * gpost))
        dp = (HALF * gt) * dxo_v
        d_gpost = _colsum(dp * yh)
        dyh = dp * gpost
        dy = ry * (dyh - yh * _rowmean(dyh * yh))
        dyb = dy.astype(BF16)
        dyb_ref[...] = dyb
        dh = jnp.zeros((tm, D), F32)
        for cidx in range(4):
            gate = gu_ref[cidx].astype(F32)
            up = gu_ref[4 + cidx].astype(F32)
            sig = _sigmoid(gate)
            s = gate * sig
            act_ref[cidx] = (s * up).astype(BF16)
            d_act = _dot_nt(dyb, wout_ref[cidx])
            d_up = (d_act * s).astype(BF16)
            d_gate = (d_act * up * (sig * (1.0 + gate * (1.0 - sig)))).astype(BF16)
            dg_ref[cidx] = d_gate
            dg_ref[4 + cidx] = d_up
            dh = dh + _dot(d_gate, win_ref[cidx]) + _dot(d_up, win_ref[4 + cidx])
        d_sc = _colsum(dh * n)
        d_sh = _colsum(dh)
        dn = dh * (1.0 + sc)
        d_gpre = _colsum(dn * xh)
        dxh = dn * gpre
        dx_ref[...] = dxo_v + r * (dxh - xh * _rowmean(dxh * xh))

        @pl.when(i % tps == 0)
        def _():
            mg_ref[...] = jnp.zeros((8, D), F32)

        @pl.when(i == 0)
        def _():
            vg_ref[...] = jnp.zeros((8, D), F32)

        mg_ref[0:1, :] += d_sh
        mg_ref[1:2, :] += d_sc
        mg_ref[2:3, :] += d_gt
        vg_ref[0:1, :] += d_gpre
        vg_ref[1:2, :] += d_gpost

    tile = pl.BlockSpec((tm, D), lambda i: (i, 0))
    return _call(
        core, name=name, grid=(nt,), jobs=jobs,
        in_specs=[tile, tile, tile, pl.BlockSpec((8, tm, FBP), lambda i: (0, i, 0)),
                  pl.BlockSpec((None, 8, D), lambda i: (i // tps, 0, 0)), _const_spec((8, D)),
                  _const_spec((8, FBP, D)), _const_spec((4, FBP, D))],
        out_specs=[tile, pl.BlockSpec((8, tm, FBP), lambda i: (0, i, 0)),
                   pl.BlockSpec((4, tm, FBP), lambda i: (0, i, 0)), tile, tile,
                   pl.BlockSpec((None, 8, D), lambda i: (i // tps, 0, 0)), pl.BlockSpec((8, D), lambda i: (0, 0))],
        out_shape=[jax.ShapeDtypeStruct((T, D), F32), jax.ShapeDtypeStruct((8, T, FBP), BF16),
                   jax.ShapeDtypeStruct((4, T, FBP), BF16), jax.ShapeDtypeStruct((T, D), BF16),
                   jax.ShapeDtypeStruct((T, D), BF16), jax.ShapeDtypeStruct((nb, 8, D), F32),
                   jax.ShapeDtypeStruct((8, D), F32)],
        args=[dxo, x, y, gu, mod, gvec, w_in, w_out])


def _masked_spatial(ws_ref):
    row = lax.broadcasted_iota(jnp.int32, (CHUNK, CHUNK), 0)
    col = lax.broadcasted_iota(jnp.int32, (CHUNK, CHUNK), 1)
    keep = col <= row
    return [jnp.where(keep, ws_ref[hd], 0.0).astype(BF16) for hd in range(NHEAD)]


def _spatial_gate(wm, vb_chunk, lane_head):
    z = jnp.zeros((CHUNK, WA), F32)
    for hd in range(NHEAD):
        z = jnp.where(lane_head == hd, _dot(wm[hd], vb_chunk), z)
    return z


def _layer_norm_stats(v):
    mu = _rowmean(v)
    vc = v - mu
    rstd = lax.rsqrt(_rowmean(vc * vc) + EPS)
    return vc * rstd, rstd


def _pitch(tm):
    p = tm // 8
    while p % 8 != 4:
        p += 1
    return p


def _lanes(s):
    return slice(s * 128, (s + 1) * 128)


def _to_slabs(ref, row0, val):
    for s in range(4):
        ref[s, row0:row0 + val.shape[0], :] = val[:, _lanes(s)]


def _tap_sum(src, out, cw_ref, bias, tm, start):
    p = _pitch(tm)
    for s in range(4):
        accs = [jnp.broadcast_to(bias[:, _lanes(s)], (8, 128))] * p
        for k in range(CONV_K):
            w = jnp.broadcast_to(cw_ref[k:k + 1, _lanes(s)], (8, 128))
            for v in range(p):
                accs[v] = accs[v] + w * src[s, pl.ds(v + start(k), 8, stride=p), :]
        for v in range(p):
            out[s, pl.ds(v, 8, stride=p), :] = accs[v]
    return jnp.concatenate([out[s, 0:tm, :] for s in range(4)], axis=1)


def _mixer_fwd(x, mod, gvec, w_mi, w_mo, v512, ws, bias_full, cw, tm, name, jobs=()):
    T = x.shape[0]
    nt = T // tm
    tps = nt // mod.shape[0]
    ext_rows = 8 * _pitch(tm)

    def core(ins, outs, scs):
        x_ref, mod_ref, g_ref, wmi_ref, wmo_ref, v_ref, ws_ref, bias_ref, cw_ref = ins
        xo_ref, proj_ref, ym_ref, conv_ref = outs
        glu_ext, conv_scr = scs
        i = pl.program_id(0)
        xv = x_ref[...]
        sh, sc, gt = mod_ref[0:1, :], mod_ref[1:2, :], mod_ref[2:3, :]
        r = lax.rsqrt(_rowmean(xv * xv) + EPS)
        hb = ((xv * r * g_ref[0:1, :]) * (1.0 + sc) + sh).astype(BF16)
        for j in range(NDEV):
            proj_ref[:, j * MB:(j + 1) * MB] = _dot(hb, wmi_ref[j])
        u = proj_ref[:, 0:WA]
        v0 = proj_ref[:, WA:2 * WA]
        a = proj_ref[:, 2 * WA:3 * WA]
        g = proj_ref[:, 3 * WA:4 * WA]
        vh, _ = _layer_norm_stats(v0)
        vb = (vh * v_ref[0:1, :] + v_ref[1:2, :]).astype(BF16)
        wm = _masked_spatial(ws_ref)
        lane_head = lax.broadcasted_iota(jnp.int32, (CHUNK, WA), 1) >> 6
        ya = []
        for q in range(tm // CHUNK):
            z = _spatial_gate(wm, vb[q * CHUNK:(q + 1) * CHUNK, :], lane_head) + bias_ref[...]
            ya.append(u[q * CHUNK:(q + 1) * CHUNK, :] * z)
        ya = jnp.concatenate(ya, axis=0)
        glu = a * _sigmoid(g)

        @pl.when(i == 0)
        def _():
            glu_ext[:, HALO + tm:HALO + ext_rows, :] = jnp.zeros((4, ext_rows - tm, 128), F32)

        @pl.when(i % tps == 0)
        def _():
            glu_ext[:, 0:HALO, :] = jnp.zeros((4, HALO, 128), F32)

        _to_slabs(glu_ext, HALO, glu)
        conv = _tap_sum(glu_ext, conv_scr, cw_ref, v_ref[2:3, :], tm, lambda k: HALO - (CONV_K - 1) + k)
        conv_ref[...] = conv
        glu_ext[:, 0:HALO, :] = glu_ext[:, tm:tm + HALO, :]
        ch, _ = _layer_norm_stats(conv)
        cn = ch * v_ref[3:4, :] + v_ref[4:5, :]
        yb = cn * _sigmoid(cn)
        pa = ya * lax.rsqrt(_rowmean(ya * ya) + EPS) * v_ref[5:6, :]
        pb = yb * lax.rsqrt(_rowmean(yb * yb) + EPS) * v_ref[6:7, :]
        ycat = jnp.concatenate([pa, pb], axis=1).astype(BF16)
        ym = _dot(ycat, wmo_ref[...])
        ym_ref[...] = ym
        rm = lax.rsqrt(_rowmean(ym * ym) + EPS)
        xo_ref[...] = xv + gt * (ym * rm * g_ref[1:2, :])

    tile = pl.BlockSpec((tm, D), lambda i: (i, 0))
    return _call(
        core, name=name, grid=(nt,), jobs=jobs,
        in_specs=[tile, pl.BlockSpec((None, 8, D), lambda i: (i // tps, 0, 0)), _const_spec((8, D)),
                  _const_spec((NDEV, D, MB)), _const_spec((D, D)), _const_spec((8, WA)),
                  _const_spec((NHEAD, CHUNK, CHUNK)), _const_spec((CHUNK, WA)), _const_spec((32, WA))],
        out_specs=[tile, pl.BlockSpec((tm, 4 * WA), lambda i: (i, 0)), tile, pl.BlockSpec((tm, WA), lambda i: (i, 0))],
        out_shape=[jax.ShapeDtypeStruct((T, D), F32), jax.ShapeDtypeStruct((T, 4 * WA), F32),
                   jax.ShapeDtypeStruct((T, D), F32), jax.ShapeDtypeStruct((T, WA), F32)],
        scratch=[pltpu.VMEM((4, HALO + ext_rows, 128), F32), pltpu.VMEM((4, ext_rows, 128), F32)],
        args=[x, mod, gvec, w_mi, w_mo, v512, ws, bias_full, cw])


def _mixer_bwd_a(dxo, ym, proj, conv, mod, gvec, w_mo, v512, ws, bias_full, esel, tm, name, jobs=()):
    T = dxo.shape[0]
    nt = T // tm
    nb = mod.shape[0]
    tps = nt // nb

    def core(ins, outs, scs):
        dxo_ref, ym_ref, proj_ref, conv_ref, mod_ref, g_ref, wmo_ref, v_ref, ws_ref, bias_ref, e_ref = ins
        dpart_ref, dymb_ref, ycat_ref, mg_ref, vg_ref, v5g_ref, gws_ref, gbs_ref = outs
        (dbs_acc,) = scs
        i = pl.program_id(0)
        dxo_v = dxo_ref[...]
        ymv = ym_ref[...]
        gt = mod_ref[2:3, :]
        gpost = g_ref[1:2, :]
        rm = lax.rsqrt(_rowmean(ymv * ymv) + EPS)
        ymh = ymv * rm
        d_gt = _colsum(dxo_v * (ymh * gpost))
        dpm = gt * dxo_v
        d_gpost = _colsum(dpm * ymh)
        dymh = dpm * gpost
        dym = (rm * (dymh - ymh * _rowmean(dymh * ymh))).astype(BF16)
        dymb_ref[...] = dym
        dycat = _dot_nt(dym, wmo_ref[...])
        u = proj_ref[:, 0:WA]
        v0 = proj_ref[:, WA:2 * WA]
        vh, rv = _layer_norm_stats(v0)
        vb = (vh * v_ref[0:1, :] + v_ref[1:2, :]).astype(BF16)
        wm = _masked_spatial(ws_ref)
        lane_head = lax.broadcasted_iota(jnp.int32, (CHUNK, WA), 1) >> 6
        zs = []
        for q in range(tm // CHUNK):
            zs.append(_spatial_gate(wm, vb[q * CHUNK:(q + 1) * CHUNK, :], lane_head) + bias_ref[...])
        z = jnp.concatenate(zs, axis=0)
        ya = u * z
        ra = lax.rsqrt(_rowmean(ya * ya) + EPS)
        yah = ya * ra
        ch, rc = _layer_norm_stats(conv_ref[...])
        cn = ch * v_ref[3:4, :] + v_ref[4:5, :]
        sg = _sigmoid(cn)
        yb = cn * sg
        rb = lax.rsqrt(_rowmean(yb * yb) + EPS)
        ybh = yb * rb
        ycat_ref[...] = jnp.concatenate([yah * v_ref[5:6, :], ybh * v_ref[6:7, :]], axis=1).astype(BF16)
        dpa = dycat[:, 0:WA]
        dpb = dycat[:, WA:2 * WA]
        d_goa = _colsum(dpa * yah)
        d_gob = _colsum(dpb * ybh)
        dyah = dpa * v_ref[5:6, :]
        dybh = dpb * v_ref[6:7, :]
        dya = ra * (dyah - yah * _rowmean(dyah * yah))
        dyb = rb * (dybh - ybh * _rowmean(dybh * ybh))
        dpart_ref[:, 0:WA] = dya * z
        dz = dya * u

        @pl.when(i == 0)
        def _():
            gws_ref[...] = jnp.zeros((NHEAD, CHUNK, CHUNK), F32)
            dbs_acc[...] = jnp.zeros((CHUNK, WA), F32)
            vg_ref[...] = jnp.zeros((8, D), F32)
            v5g_ref[...] = jnp.zeros((8, WA), F32)

        dvs = []
        for q in range(tm // CHUNK):
            dz_q = dz[q * CHUNK:(q + 1) * CHUNK, :]
            vb_q = vb[q * CHUNK:(q + 1) * CHUNK, :]
            dbs_acc[...] += dz_q
            dzb = dz_q.astype(BF16)
            dv_q = jnp.zeros((CHUNK, WA), F32)
            for hd in range(NHEAD):
                dv_q = jnp.where(lane_head == hd, _dot_tn(wm[hd], dzb), dv_q)
                dz_hd = jnp.where(lane_head == hd, dz_q, 0.0).astype(BF16)
                gws_ref[hd] += _dot_nt(dz_hd, vb_q)
            dvs.append(dv_q)
        dv = jnp.concatenate(dvs, axis=0)
        d_gng = _colsum(dv * vh)
        d_gnb = _colsum(dv)
        dvh = dv * v_ref[0:1, :]
        dpart_ref[:, WA:2 * WA] = rv * (dvh - _rowmean(dvh) - vh * _rowmean(dvh * vh))
        dcn = dyb * (sg * (1.0 + cn * (1.0 - sg)))
        d_cng = _colsum(dcn * ch)
        d_cnb = _colsum(dcn)
        dch = dcn * v_ref[3:4, :]
        dconv = rc * (dch - _rowmean(dch) - ch * _rowmean(dch * ch))
        dpart_ref[:, 2 * WA:3 * WA] = dconv
        dpart_ref[:, 3 * WA:4 * WA] = jnp.zeros((tm, WA), F32)
        d_cb = _colsum(dconv)

        @pl.when(i % tps == 0)
        def _():
            mg_ref[...] = jnp.zeros((8, D), F32)

        mg_ref[2:3, :] += d_gt
        vg_ref[1:2, :] += d_gpost
        v5g_ref[0:1, :] += d_gng
        v5g_ref[1:2, :] += d_gnb
        v5g_ref[2:3, :] += d_cb
        v5g_ref[3:4, :] += d_cng
        v5g_ref[4:5, :] += d_cnb
        v5g_ref[5:6, :] += d_goa
        v5g_ref[6:7, :] += d_gob

        @pl.when(i == nt - 1)
        def _():
            row = lax.broadcasted_iota(jnp.int32, (CHUNK, CHUNK), 0)
            col = lax.broadcasted_iota(jnp.int32, (CHUNK, CHUNK), 1)
            for hd in range(NHEAD):
                gws_ref[hd] = jnp.where(col <= row, gws_ref[hd], 0.0)
            gbs_ref[...] = lax.dot_general(e_ref[...], dbs_acc[...], (((1,), (1,)), ((), ())),
                                           precision=lax.Precision.HIGHEST, preferred_element_type=F32)

    tile = pl.BlockSpec((tm, D), lambda i: (i, 0))
    ptile = pl.BlockSpec((tm, 4 * WA), lambda i: (i, 0))
    return _call(
        core, name=name, grid=(nt,), jobs=jobs,
        in_specs=[tile, tile, pl.BlockSpec((tm, 2 * WA), lambda i: (i, 0)), pl.BlockSpec((tm, WA), lambda i: (i, 0)),
                  pl.BlockSpec((None, 8, D), lambda i: (i // tps, 0, 0)), _const_spec((8, D)), _const_spec((D, D)),
                  _const_spec((8, WA)), _const_spec((NHEAD, CHUNK, CHUNK)), _const_spec((CHUNK, WA)),
                  _const_spec((8, WA))],
        out_specs=[ptile, tile, tile, pl.BlockSpec((None, 8, D), lambda i: (i // tps, 0, 0)),
                   pl.BlockSpec((8, D), lambda i: (0, 0)), pl.BlockSpec((8, WA), lambda i: (0, 0)),
                   pl.BlockSpec((NHEAD, CHUNK, CHUNK), lambda i: (0, 0, 0)), pl.BlockSpec((8, CHUNK), lambda i: (0, 0))],
        out_shape=[jax.ShapeDtypeStruct((T, 4 * WA), F32), jax.ShapeDtypeStruct((T, D), BF16),
                   jax.ShapeDtypeStruct((T, D), BF16), jax.ShapeDtypeStruct((nb, 8, D), F32),
                   jax.ShapeDtypeStruct((8, D), F32), jax.ShapeDtypeStruct((8, WA), F32),
                   jax.ShapeDtypeStruct((NHEAD, CHUNK, CHUNK), F32), jax.ShapeDtypeStruct((8, CHUNK), F32)],
        scratch=[pltpu.VMEM((CHUNK, WA), F32)],
        args=[dxo, ym, proj, conv, mod, gvec, w_mo, v512, ws, bias_full, esel])


def _mixer_bwd_b(dxo, x, dpart, proj, mod, gvec, w_mi, cw, tm, name, jobs=()):
    T = x.shape[0]
    nt = T // tm
    nb = mod.shape[0]
    tps = nt // nb
    hpt = tm // HALO
    nh = T // HALO
    off = HALO - (CONV_K - 1)
    p = _pitch(tm)
    ext_rows = 8 * p

    def core(ins, outs, scs):
        dxo_ref, x_ref, dpart_ref, dnext_ref, ag_ref, halo_ref, mod_ref, g_ref, wmi_ref, cw_ref = ins
        dx_ref, dproj_ref, hb_ref, mg_ref, vg_ref, dcw_ref = outs
        glu_ext, dconv_ext, dglu_scr, dcw_acc = scs
        i = pl.program_id(0)
        first = i % tps == 0
        last = i % tps == tps - 1
        a = ag_ref[:, 0:WA]
        g = ag_ref[:, WA:2 * WA]
        sgg = _sigmoid(g)

        @pl.when(i == 0)
        def _():
            glu_ext[:, HALO + tm:HALO + ext_rows, :] = jnp.zeros((4, ext_rows - tm, 128), F32)
            dconv_ext[:, HALO + tm:HALO + ext_rows, :] = jnp.zeros((4, ext_rows - tm, 128), F32)
            dcw_acc[...] = jnp.zeros((32, 8, WA), F32)
            vg_ref[...] = jnp.zeros((8, D), F32)

        _to_slabs(glu_ext, 0, jnp.where(first, 0.0, halo_ref[:, 0:WA] * _sigmoid(halo_ref[:, WA:2 * WA])))
        _to_slabs(glu_ext, HALO, a * sgg)
        _to_slabs(dconv_ext, 0, dpart_ref[:, 2 * WA:3 * WA])
        _to_slabs(dconv_ext, tm, jnp.where(last, 0.0, dnext_ref[...]))
        sub = lax.broadcasted_iota(jnp.int32, (8, 128), 0)
        for s in range(4):
            accs = [jnp.zeros((8, 128), F32)] * CONV_K
            for v in range(p):
                dc = jnp.where(v + p * sub < tm, dconv_ext[s, pl.ds(v, 8, stride=p), :], 0.0)
                for k in range(CONV_K):
                    accs[k] = accs[k] + dc * glu_ext[s, pl.ds(v + off + k, 8, stride=p), :]
            for k in range(CONV_K):
                dcw_acc[k, :, _lanes(s)] += accs[k]
        dglu = _tap_sum(dconv_ext, dglu_scr, cw_ref, jnp.zeros((1, WA), F32), tm, lambda k: (CONV_K - 1) - k)

        @pl.when(i == nt - 1)
        def _():
            for k in range(CONV_K):
                dcw_ref[k:k + 1, :] = jnp.sum(dcw_acc[k], axis=0, keepdims=True)
            dcw_ref[CONV_K:32, :] = jnp.zeros((32 - CONV_K, WA), F32)

        da = dglu * sgg
        dgg = dglu * a * (sgg * (1.0 - sgg))
        dproj_ref[:, 0:2 * WA] = dpart_ref[:, 0:2 * WA].astype(BF16)
        dproj_ref[:, 2 * WA:3 * WA] = da.astype(BF16)
        dproj_ref[:, 3 * WA:4 * WA] = dgg.astype(BF16)
        dh = jnp.zeros((tm, D), F32)
        for j in range(NDEV):
            dh = dh + _dot_nt(dproj_ref[:, j * MB:(j + 1) * MB], wmi_ref[j])
        xv = x_ref[...]
        sc, sh = mod_ref[1:2, :], mod_ref[0:1, :]
        gpre = g_ref[0:1, :]
        r = lax.rsqrt(_rowmean(xv * xv) + EPS)
        xh = xv * r
        n = xh * gpre
        hb_ref[...] = (n * (1.0 + sc) + sh).astype(BF16)
        d_sc = _colsum(dh * n)
        d_sh = _colsum(dh)
        dn = dh * (1.0 + sc)
        d_gpre = _colsum(dn * xh)
        dxh = dn * gpre
        dx_ref[...] = dxo_ref[...] + r * (dxh - xh * _rowmean(dxh * xh))

        @pl.when(first)
        def _():
            mg_ref[...] = jnp.zeros((8, D), F32)

        mg_ref[0:1, :] += d_sh
        mg_ref[1:2, :] += d_sc
        vg_ref[0:1, :] += d_gpre

    tile = pl.BlockSpec((tm, D), lambda i: (i, 0))
    return _call(
        core, name=name, grid=(nt,), jobs=jobs,
        in_specs=[tile, tile, pl.BlockSpec((tm, 4 * WA), lambda i: (i, 0)),
                  pl.BlockSpec((HALO, WA), lambda i: (jnp.minimum((i + 1) * hpt, nh - 1), 2)),
                  pl.BlockSpec((tm, 2 * WA), lambda i: (i, 1)),
                  pl.BlockSpec((HALO, 2 * WA), lambda i: (jnp.maximum(i * hpt - 1, 0), 1)),
                  pl.BlockSpec((None, 8, D), lambda i: (i // tps, 0, 0)), _const_spec((8, D)),
                  _const_spec((NDEV, D, MB)), _const_spec((32, WA))],
        out_specs=[tile, pl.BlockSpec((tm, 4 * WA), lambda i: (i, 0)), tile,
                   pl.BlockSpec((None, 8, D), lambda i: (i // tps, 0, 0)), pl.BlockSpec((8, D), lambda i: (0, 0)),
                   pl.BlockSpec((32, WA), lambda i: (0, 0))],
        out_shape=[jax.ShapeDtypeStruct((T, D), F32), jax.ShapeDtypeStruct((T, 4 * WA), BF16),
                   jax.ShapeDtypeStruct((T, D), BF16), jax.ShapeDtypeStruct((nb, 8, D), F32),
                   jax.ShapeDtypeStruct((8, D), F32), jax.ShapeDtypeStruct((32, WA), F32)],
        scratch=[pltpu.VMEM((4, HALO + ext_rows, 128), F32), pltpu.VMEM((4, HALO + ext_rows, 128), F32),
                 pltpu.VMEM((4, ext_rows, 128), F32), pltpu.VMEM((32, 8, WA), F32)],
        args=[dxo, x, dpart, dpart, proj, proj, mod, gvec, w_mi, cw])


def _grad_chip(a, b, a_spec, b_spec, prod_shape, half, name, jobs=()):
    steps = 8 if half is None else 4
    R = prod_shape[0] if half is None else half
    C = prod_shape[1]

    def core(ins, outs, scs):
        a_ref, b_ref = ins
        (o_ref,) = outs
        own, snd, rcv, ssem, rsem, lsem = scs
        s = pl.program_id(0)
        c = lax.axis_index("c")
        me = _me()
        sib = _flip(me, (0, 0, 1))
        prod = _dot_tn(a_ref[...], b_ref[...]).astype(BF16)
        if half is None:
            q = s // 2

            @pl.when(s % 2 == c)
            def _():
                own[q] = prod

            @pl.when(s % 2 != c)
            def _():
                snd[q] = prod
                _remote(snd.at[q], rcv.at[q], ssem.at[q], rsem.at[q], sib).start()
        else:
            lo = prod[0:half, :]
            hi = prod[half:2 * half, :]
            own[s] = jnp.where(c == 0, lo, hi)
            snd[s] = jnp.where(c == 0, hi, lo)
            _remote(snd.at[s], rcv.at[s], ssem.at[s], rsem.at[s], sib).start()

        @pl.when(s == steps - 1)
        def _():
            for q4 in range(4):
                cp = _remote(snd.at[q4], rcv.at[q4], ssem.at[q4], rsem.at[q4], sib)
                cp.wait_recv()
                cp.wait_send()
                snd[q4] = (own[q4].astype(F32) + rcv[q4].astype(F32)).astype(BF16)
            out = pltpu.make_async_copy(snd, o_ref, lsem)
            out.start()
            out.wait()

    return _call(
        core, name=name, grid=(steps,), jobs=jobs, in_specs=[a_spec, b_spec], out_specs=[HBM],
        out_shape=[jax.ShapeDtypeStruct((4, R, C), BF16)],
        scratch=[pltpu.VMEM((4, R, C), BF16), pltpu.VMEM((4, R, C), BF16), pltpu.VMEM((4, R, C), BF16),
                 pltpu.SemaphoreType.DMA((4,)), pltpu.SemaphoreType.DMA((4,)), pltpu.SemaphoreType.DMA],
        args=[a, b])


def _grad_w_in(dg, hb, name, jobs=()):
    T = hb.shape[0]
    return _grad_chip(dg, hb, pl.BlockSpec((None, T, FBP), lambda s: (s, 0, 0)), _const_spec((T, D)),
                      (FBP, D), None, name, jobs)


def _grad_w_out(act, dyb, name, jobs=()):
    T = dyb.shape[0]
    return _grad_chip(act, dyb, pl.BlockSpec((None, T, FBP), lambda s: (s, 0, 0)), _const_spec((T, D)),
                      (FBP, D), FO, name, jobs)


def _grad_w_mi(hb, dproj, name, jobs=()):
    T = hb.shape[0]
    return _grad_chip(hb, dproj, _const_spec((T, D)), pl.BlockSpec((T, MB), lambda s: (0, s)),
                      (D, MB), None, name, jobs)


def _grad_w_mo(ycat, dym, name, jobs=()):
    T = ycat.shape[0]
    return _grad_chip(ycat, dym, pl.BlockSpec((T, 2 * MO), lambda s: (0, s)), _const_spec((T, D)),
                      (2 * MO, D), MO, name, jobs)


def _adamw_math(w, g, m, v):
    m2 = ADAM_B1 * m + (1.0 - ADAM_B1) * g
    v2 = ADAM_B2 * v + (1.0 - ADAM_B2) * (g * g)
    m_hat = m2 / (1.0 - ADAM_B1 ** ADAM_STEP)
    v_hat = v2 / (1.0 - ADAM_B2 ** ADAM_STEP)
    delta = -ADAM_LR * (m_hat / (jnp.sqrt(v_hat) + ADAM_EPS) + ADAM_WD * w)
    return delta, m2, v2


def _adamw_reduce(parts, w, m, v, tr, name):
    R, C = w.shape

    def core(ins, outs, _):
        p_ref, w_ref, m_ref, v_ref = ins
        g_ref, d_ref, m2_ref, v2_ref = outs
        g = p_ref[0].astype(F32)
        for s in range(1, 4):
            g = g + p_ref[s].astype(F32)
        g_ref[...] = g
        d_ref[...], m2_ref[...], v2_ref[...] = _adamw_math(w_ref[...], g, m_ref[...], v_ref[...])

    blk = pl.BlockSpec((tr, C), lambda i: (i, 0))
    return _call(
        core, name=name, grid=(R // tr,),
        in_specs=[pl.BlockSpec((4, tr, C), lambda i: (0, i, 0)), blk, blk, blk],
        out_specs=[blk, blk, blk, blk], out_shape=[jax.ShapeDtypeStruct((R, C), F32)] * 4,
        args=[parts, w, m, v])[0]


def _adamw_ada(sc_all, dd, w, m, v, tr, name):
    R, C = w.shape

    def core(ins, outs, _):
        sc_ref, dd_ref, w_ref, m_ref, v_ref = ins
        g_ref, d_ref, m2_ref, v2_ref = outs
        g = _dot_tn(sc_ref[...].astype(BF16), dd_ref[...].astype(BF16))
        g_ref[...] = g
        d_ref[...], m2_ref[...], v2_ref[...] = _adamw_math(w_ref[...], g, m_ref[...], v_ref[...])

    blk = pl.BlockSpec((tr, C), lambda i: (i, 0))
    return _call(
        core, name=name, grid=(R // tr,),
        in_specs=[pl.BlockSpec((64, tr), lambda i: (0, i)), pl.BlockSpec((64, C), lambda i: (0, 0)), blk, blk, blk],
        out_specs=[blk, blk, blk, blk], out_shape=[jax.ShapeDtypeStruct((R, C), F32)] * 4,
        args=[sc_all, dd, w, m, v])[0]


def _adamw_small(gathered, plain, grads, wmv, emit, name):
    nw = len(grads)
    ng, npl, ne = len(gathered), len(plain), len(emit)

    def core(ins, outs, _):
        srcs = []
        for a in range(ng):
            s = ins[a][0]
            for dev in range(1, NDEV):
                s = s + ins[a][dev]
            srcs.append(s)
        srcs += [ins[ng + a][...] for a in range(npl)]
        w_refs = ins[ng + npl:]
        for e, a in enumerate(emit):
            outs[e][...] = srcs[a]
        for t in range(nw):
            src, row = grads[t]
            g = srcs[src] if row is None else srcs[src][row:row + 1, :]
            w_ref, m_ref, v_ref = w_refs[3 * t:3 * t + 3]
            g_ref, d_ref, m2_ref, v2_ref = outs[ne + 4 * t:ne + 4 * t + 4]
            g_ref[...] = g
            d_ref[...], m2_ref[...], v2_ref[...] = _adamw_math(w_ref[...], g, m_ref[...], v_ref[...])

    out_shape = [jax.ShapeDtypeStruct(gathered[a].shape[1:], F32) for a in emit]
    for t in range(nw):
        out_shape += [jax.ShapeDtypeStruct(wmv[3 * t].shape, F32)] * 4
    return _call(
        core, name=name, grid=(), in_specs=[VM] * (ng + npl + 3 * nw), out_specs=[VM] * (ne + 4 * nw),
        out_shape=out_shape, args=list(gathered) + list(plain) + list(wmv))[0]


def _ada_fwd(c_pad, w_ada, b_cols, cw_pad, jobs=()):
    def core(ins, outs, scs):
        c_ref, w_ref, b_ref, cwp_ref = ins
        ada_ref, sc_ref, cw_ref = outs
        cbuf, send_buf, ssem, rsem = scs
        me = _me()
        mi = _lin(me)
        cbuf[mi] = c_ref[...]
        cw_ref[mi] = cwp_ref[...]
        peers = [_flip(me, f) for f in FLIPS]
        first = []
        for k, p in enumerate(peers):
            first.append(_remote(cbuf.at[mi], cbuf.at[mi], ssem.at[k], rsem.at[k], p))
            first.append(_remote(cw_ref.at[mi], cw_ref.at[mi], ssem.at[7 + k], rsem.at[7 + k], p))
        for cp in first:
            cp.start()
        for k, p in enumerate(peers):
            pi = _lin(p)
            _remote(cbuf.at[pi], cbuf.at[pi], ssem.at[k], rsem.at[k], p).wait_recv()
            _remote(cw_ref.at[pi], cw_ref.at[pi], ssem.at[7 + k], rsem.at[7 + k], p).wait_recv()
        c_all = cbuf[...].reshape(8 * 8, D)
        sc = c_all * _sigmoid(c_all)
        sc_ref[...] = sc
        res = _dot(sc.astype(BF16), w_ref[...].astype(BF16)) + b_ref[...]
        send_buf[...] = res.reshape(8, 8, ADA_B)
        ada_ref[mi] = send_buf[mi]
        second = []
        for k, p in enumerate(peers):
            second.append(_remote(send_buf.at[_lin(p)], ada_ref.at[mi], ssem.at[14 + k], rsem.at[14 + k], p))
        for cp in second:
            cp.start()
        for k, p in enumerate(peers):
            _remote(send_buf.at[mi], ada_ref.at[_lin(p)], ssem.at[14 + k], rsem.at[14 + k], p).wait_recv()
        for cp in first + second:
            cp.wait_send()

    return _call(
        core, name="ada_fwd", grid=(), jobs=jobs, in_specs=[VM, VM, VM, VM], out_specs=[VM, VM, VM],
        out_shape=[jax.ShapeDtypeStruct((8, 8, ADA_B), F32), jax.ShapeDtypeStruct((64, D), F32),
                   jax.ShapeDtypeStruct((8, 32, 64), F32)],
        scratch=[pltpu.VMEM((8, 8, D), F32), pltpu.VMEM((8, 8, ADA_B), F32),
                 pltpu.SemaphoreType.DMA((21,)), pltpu.SemaphoreType.DMA((21,))],
        args=[c_pad, w_ada, b_cols, cw_pad])


def _ada_bwd(dada, jobs=()):
    def core(ins, outs, scs):
        (d_ref,) = ins
        dd_ref, gb_ref = outs
        rbuf, ssem, rsem = scs
        me = _me()
        mi = _lin(me)
        peers = [_flip(me, f) for f in FLIPS]
        rbuf[mi] = d_ref[mi]
        first = []
        for k, p in enumerate(peers):
            first.append(_remote(d_ref.at[_lin(p)], rbuf.at[mi], ssem.at[k], rsem.at[k], p))
        for cp in first:
            cp.start()
        for k, p in enumerate(peers):
            _remote(d_ref.at[mi], rbuf.at[_lin(p)], ssem.at[k], rsem.at[k], p).wait_recv()
        dd = rbuf[...].reshape(64, ADA_B)
        dd_ref[...] = dd
        gb_ref[mi] = jnp.broadcast_to(_colsum(dd), (8, ADA_B))
        second = []
        for k, p in enumerate(peers):
            second.append(_remote(gb_ref.at[mi], gb_ref.at[mi], ssem.at[7 + k], rsem.at[7 + k], p))
        for cp in second:
            cp.start()
        for k, p in enumerate(peers):
            pi = _lin(p)
            _remote(gb_ref.at[pi], gb_ref.at[pi], ssem.at[7 + k], rsem.at[7 + k], p).wait_recv()
        for cp in first + second:
            cp.wait_send()

    return _call(
        core, name="ada_bwd", grid=(), jobs=jobs, in_specs=[VM], out_specs=[VM, VM],
        out_shape=[jax.ShapeDtypeStruct((64, ADA_B), F32), jax.ShapeDtypeStruct((8, 8, ADA_B), F32)],
        scratch=[pltpu.VMEM((8, 8, ADA_B), F32), pltpu.SemaphoreType.DMA((14,)), pltpu.SemaphoreType.DMA((14,))],
        args=[dada])


SMALL_D = ("g_pre_f1", "g_post_f1", "g_pre_m", "g_post_m", "g_pre_f2", "g_post_f2")
SMALL_W = ("gmlp_norm_g", "gmlp_norm_b", "conv_b", "conv_norm_g", "conv_norm_b", "g_out_a", "g_out_b")


def kernel(x, c, w_ada, b_ada, g_pre_f1, g_post_f1, w_f1_in, w_f1_out, g_pre_m, g_post_m, w_mix_in, gmlp_norm_g, gmlp_norm_b, w_spatial, b_spatial, conv_w, conv_b, conv_norm_g, conv_norm_b, g_out_a, g_out_b, w_mix_out, g_pre_f2, g_post_f2, w_f2_in, w_f2_out, loss_target, m_w_ada, m_b_ada, m_g_pre_f1, m_g_post_f1, m_w_f1_in, m_w_f1_out, m_g_pre_m, m_g_post_m, m_w_mix_in, m_gmlp_norm_g, m_gmlp_norm_b, m_w_spatial, m_b_spatial, m_conv_w, m_conv_b, m_conv_norm_g, m_conv_norm_b, m_g_out_a, m_g_out_b, m_w_mix_out, m_g_pre_f2, m_g_post_f2, m_w_f2_in, m_w_f2_out, v_w_ada, v_b_ada, v_g_pre_f1, v_g_post_f1, v_w_f1_in, v_w_f1_out, v_g_pre_m, v_g_post_m, v_w_mix_in, v_gmlp_norm_g, v_gmlp_norm_b, v_w_spatial, v_b_spatial, v_conv_w, v_conv_b, v_conv_norm_g, v_conv_norm_b, v_g_out_a, v_g_out_b, v_w_mix_out, v_g_pre_f2, v_g_post_f2, v_w_f2_in, v_w_f2_out):
    given = dict(locals())
    bl, seq, _ = x.shape
    T = bl * seq
    tm = min(256, seq // 2)
    mi = _lin((lax.axis_index("x"), lax.axis_index("y"), lax.axis_index("c")))

    def shard_in(w):
        return jnp.pad(w[0].T.astype(BF16), ((0, FBP - FB), (0, 0)))

    zpad = jnp.zeros((max(FBP - FB, 16), D), BF16)
    g_f1 = _Gather([shard_in(w_f1_in), w_f1_out[0].astype(BF16)], ("rows", "out"), zpad)
    g_mx = _Gather([w_mix_in[0].astype(BF16), w_mix_out[0].astype(BF16), w_f2_out[0].astype(BF16)],
                   ("rows", "rows", "out"), zpad)
    g_f2 = _Gather([shard_in(w_f2_in)], ("rows",), zpad, late_mid=True)

    c_pad = jnp.pad(c, ((0, 8 - bl), (0, 0)))
    b_cols = lax.dynamic_slice(b_ada, (0, mi * ADA_B), (1, ADA_B))
    cw_pad = jnp.pad(conv_w[0], ((0, 1), (0, 0)))
    (ada_blk, sc_all, cw_all), ((wi1, wo1),) = _ada_fwd(c_pad, w_ada[0], b_cols, cw_pad, jobs=[g_f1])
    ada = ada_blk[:, 0:bl, :].transpose(1, 0, 2).reshape(bl, 9, D)
    pad5 = jnp.zeros((bl, 5, D), F32)
    mod1 = jnp.concatenate([ada[:, 0:3], pad5], axis=1)
    mod2 = jnp.concatenate([ada[:, 3:6], pad5], axis=1)
    mod3 = jnp.concatenate([ada[:, 6:9], pad5], axis=1)
    cw_full = cw_all.transpose(1, 0, 2).reshape(32, WA)

    zrow = jnp.zeros((1, D), F32)
    gv1 = jnp.concatenate([g_pre_f1, g_post_f1] + [zrow] * 6, axis=0)
    gvm = jnp.concatenate([g_pre_m, g_post_m] + [zrow] * 6, axis=0)
    gv2 = jnp.concatenate([g_pre_f2, g_post_f2] + [zrow] * 6, axis=0)
    v512 = jnp.concatenate([gmlp_norm_g, gmlp_norm_b, conv_b, conv_norm_g, conv_norm_b, g_out_a, g_out_b,
                            jnp.zeros((1, WA), F32)], axis=0)
    ws = w_spatial[0]
    bias_full = jnp.repeat(b_spatial[0].T, HD, axis=1)
    esel = (lax.broadcasted_iota(jnp.int32, (8, WA), 1) // HD == lax.broadcasted_iota(jnp.int32, (8, WA), 0)).astype(F32)

    x0 = x.reshape(T, D)
    (x1, gu1, y1), ((wmi, wmo, wo2),) = _ffn_fwd(x0, mod1, gv1, wi1, wo1, tm, "ffn1_fwd", jobs=[g_mx])
    wmo = wmo.reshape(D, D)
    (x2, proj, ym, conv), ((wi2,),) = _mixer_fwd(x1, mod2, gvm, wmi, wmo, v512, ws, bias_full, cw_full, tm, "mixer_fwd", jobs=[g_f2])
    (dx3, gu2, y2, loss_blk), _ = _ffn_fwd(x2, mod3, gv2, wi2, wo2, tm, "ffn2_fwd", target=loss_target.reshape(T, D))

    (dx2, dg2, act2, hb2, dyb2, mg3, vg3), _ = _ffn_bwd(dx3, x2, y2, gu2, mod3, gv2, wi2, wo2, tm, "ffn2_bwd")
    (g_wi2,), _ = _grad_w_in(dg2, hb2, "ffn2_gw_in")
    (g_wo2,), _ = _grad_w_out(act2, dyb2, "ffn2_gw_out")
    (dpart, dymb, ycat, mg2a, vgma, v5g, gws, gbs), ((p_wi2,),) = _mixer_bwd_a(
        dx2, ym, proj, conv, mod2, gvm, wmo, v512, ws, bias_full, esel, tm, "mixer_bwd_a",
        jobs=[_ChipScatter([g_wi2])])
    (dx1, dproj, hbm, mg2b, vgmb, dcw), ((p_wo2,),) = _mixer_bwd_b(
        dx2, x1, dpart, proj, mod2, gvm, wmi, cw_full, tm, "mixer_bwd_b", jobs=[_ChipScatter([g_wo2])])
    (g_wmi,), _ = _grad_w_mi(hbm, dproj, "mixer_gw_in")
    (g_wmo,), _ = _grad_w_mo(ycat, dymb, "mixer_gw_out")
    p2 = jnp.concatenate([v5g, dcw], axis=0)
    (dx0, dg1, act1, hb1, dyb1, mg1, vg1), _ = _ffn_bwd(dx1, x0, y1, gu1, mod1, gv1, wi1, wo1, tm, "ffn1_bwd")
    (g_wo1,), ((p_wmi, p_wmo),) = _grad_w_out(act1, dyb1, "ffn1_gw_out", jobs=[_ChipScatter([g_wmi, g_wmo])])
    (g_wi1,), ((a2, a3, a4), (p_wo1,)) = _grad_w_in(
        dg1, hb1, "ffn1_gw_in", jobs=[_Gather([p2, gws, gbs], ("rows",) * 3, zpad), _ChipScatter([g_wo1])])

    dada = jnp.concatenate([mg1[:, 0:3], mg2b[:, 0:2], mg2a[:, 2:3], mg3[:, 0:3]], axis=1)
    dada = dada.reshape(bl, NDEV, ADA_B).transpose(1, 0, 2)
    dada = jnp.pad(dada, ((0, 0), (0, 8 - bl), (0, 0)))
    p1 = jnp.concatenate([vg1[0:2], vgmb[0:1], vgma[1:2], vg3[0:2], loss_blk[0:1], zrow], axis=0)
    (dd_all, gb_all), ((p_wi1,), (a1,)) = _ada_bwd(dada, jobs=[_ChipScatter([g_wi1]), _AllGather([p1])])
    g_bada = gb_all[:, 0, :].reshape(1, 9 * D)

    res = {}
    for nm, part in (("w_f1_in", p_wi1), ("w_f2_in", p_wi2)):
        quad = _adamw_reduce(part, given[nm][0].T, given["m_" + nm][0].T, given["v_" + nm][0].T, FO, "adamw_" + nm)
        res[nm] = tuple(t.T[None] for t in quad)
    for nm, part, tr in (("w_f1_out", p_wo1, FO), ("w_f2_out", p_wo2, FO), ("w_mix_in", p_wmi, 256), ("w_mix_out", p_wmo, MO)):
        quad = _adamw_reduce(part, given[nm][0], given["m_" + nm][0], given["v_" + nm][0], tr, "adamw_" + nm)
        res[nm] = tuple(t[None] for t in quad)
    quad = _adamw_ada(sc_all, dd_all, w_ada[0], m_w_ada[0], v_w_ada[0], 256, "adamw_w_ada")
    res["w_ada"] = tuple(t[None] for t in quad)

    small = SMALL_D + SMALL_W + ("w_spatial", "b_spatial", "b_ada")
    grads = [(0, r) for r in range(6)] + [(1, r) for r in range(7)] + [(2, None), (3, None), (4, None)]
    wmv = []
    for nm in small:
        for pre in ("", "m_", "v_"):
            wmv.append(given[pre + nm][0] if nm in ("w_spatial", "b_spatial") else given[pre + nm])
    outs = _adamw_small([a1, a2, a3, a4], [g_bada], grads, wmv, (0, 1), "adamw_small")
    loss = outs[0][6, 0]
    for t, nm in enumerate(small):
        quad = outs[2 + 4 * t:6 + 4 * t]
        res[nm] = tuple(q[None] for q in quad) if nm in ("w_spatial", "b_spatial") else tuple(quad)
    g_cw = lax.dynamic_slice(outs[1], (8, mi * 64), (32, 64))
    wmv = [jnp.pad(given[pre + "conv_w"][0], ((0, 1), (0, 0)), constant_values=1.0 if pre == "v_" else 0.0)
           for pre in ("", "m_", "v_")]
    quad = _adamw_small([], [g_cw], [(0, None)], wmv, (), "adamw_conv_w")
    res["conv_w"] = tuple(q[0:CONV_K][None] for q in quad)

    order = ["w_ada", "b_ada", "g_pre_f1", "g_post_f1", "w_f1_in", "w_f1_out", "g_pre_m", "g_post_m", "w_mix_in",
             "gmlp_norm_g", "gmlp_norm_b", "w_spatial", "b_spatial", "conv_w", "conv_b", "conv_norm_g", "conv_norm_b",
             "g_out_a", "g_out_b", "w_mix_out", "g_pre_f2", "g_post_f2", "w_f2_in", "w_f2_out"]
    out = [loss, dx0.reshape(bl, seq, D)]
    for k in range(4):
        out += [res[nm][k] for nm in order]
    return tuple(out)
```

```python
import jax
import jax.numpy as jnp
from jax import lax
from jax.experimental import pallas as pl
from jax.experimental.pallas import tpu as pltpu

F32 = jnp.float32
BF16 = jnp.bfloat16

D = 1024
DFF = 2816
NDEV = 8
FB = 2 * DFF // NDEV
FBP = 704
FO = DFF // NDEV
WA = 512
NHEAD = 8
HD = 64
CHUNK = 128
CONV_K = 31
HALO = 32
MB = 2 * (WA + WA) // NDEV
MO = D // NDEV
ADA_B = 9 * D // NDEV
EPS = 1e-6
HALF = 0.5

ADAM_LR = 0.001
ADAM_B1 = 0.9
ADAM_B2 = 0.999
ADAM_EPS = 1e-08
ADAM_WD = 0.01
ADAM_STEP = 10

VMEM_LIMIT = 56 * 1024 * 1024
MESH = pl.DeviceIdType.MESH
FLIPS = ((0, 0, 1), (1, 0, 0), (0, 1, 0), (1, 1, 0), (1, 0, 1), (0, 1, 1), (1, 1, 1))
CHIP_FLIPS = ((1, 0, 0), (0, 1, 0), (1, 1, 0))
HBM = pl.BlockSpec(memory_space=pl.ANY)
VM = pl.BlockSpec(memory_space=pltpu.VMEM)


def _dot(a, b):
    return lax.dot_general(a, b, (((1,), (0,)), ((), ())), preferred_element_type=F32)


def _dot_nt(a, b):
    return lax.dot_general(a, b, (((1,), (1,)), ((), ())), preferred_element_type=F32)


def _dot_tn(a, b):
    return lax.dot_general(a, b, (((0,), (0,)), ((), ())), preferred_element_type=F32)


def _rowmean(v):
    return jnp.mean(v, axis=-1, keepdims=True)


def _colsum(v):
    return jnp.sum(v, axis=0, keepdims=True)


def _sigmoid(v):
    return 0.5 * jnp.tanh(0.5 * v) + 0.5


def _const_spec(shape):
    nd = len(shape)
    return pl.BlockSpec(shape, lambda *_: (0,) * nd, pipeline_mode=pl.Buffered(1))


def _me():
    return lax.axis_index("x"), lax.axis_index("y"), lax.axis_index("c")


def _flip(me, f):
    return tuple(1 - v if b else v for v, b in zip(me, f))


def _lin(p):
    return 4 * p[0] + 2 * p[1] + p[2]


def _remote(src, dst, send_sem, recv_sem, dev):
    return pltpu.make_async_remote_copy(src_ref=src, dst_ref=dst, send_sem=send_sem, recv_sem=recv_sem,
                                        device_id=dev, device_id_type=MESH)


def _blk(kind, ref, p):
    if kind == "out":
        return ref.at[2 * p[0] + p[1], pl.ds(p[2] * FO, FO), :]
    return ref.at[_lin(p)]


class _Gather:
    def __init__(self, shards, kinds, zpad, late_mid=False):
        self.late_mid = late_mid
        self.kinds = kinds
        self.n = len(shards)
        self.ins = list(shards) + [zpad]
        self.out_shape = [jax.ShapeDtypeStruct((4, FBP, D) if k == "out" else (NDEV,) + s.shape, s.dtype)
                          for s, k in zip(shards, kinds)]
        self.n_out = sum(k == "out" for k in kinds)
        self.sems = [pltpu.SemaphoreType.DMA((7 * self.n,)), pltpu.SemaphoreType.DMA((7 * self.n,)),
                     pltpu.SemaphoreType.DMA((self.n + 4 * max(self.n_out, 1),))]

    def _first(self, ins, outs, sems):
        ssem, rsem, lsem = sems
        me = _me()
        sib = _flip(me, (0, 0, 1))
        cps, loc = [], []
        nz = 0
        for a in range(self.n):
            mine = _blk(self.kinds[a], outs[a], me)
            loc.append(pltpu.make_async_copy(ins[a], mine, lsem.at[a]))
            if self.kinds[a] == "out" and FBP > FB:
                for q in range(4):
                    loc.append(pltpu.make_async_copy(ins[self.n], outs[a].at[q, pl.ds(FB, FBP - FB), :],
                                                     lsem.at[self.n + 4 * nz + q]))
                nz += 1
            cps.append(_remote(ins[a], mine, ssem.at[7 * a], rsem.at[7 * a], sib))
            for j, f in enumerate(CHIP_FLIPS):
                cps.append(_remote(ins[a], mine, ssem.at[7 * a + 1 + j], rsem.at[7 * a + 1 + j], _flip(me, f)))
        return cps, loc

    def _passed(self, outs, sems):
        ssem, rsem, _ = sems
        me = _me()
        sib = _flip(me, (0, 0, 1))
        cps = []
        for j, f in enumerate(CHIP_FLIPS):
            for a in range(self.n):
                blk = _blk(self.kinds[a], outs[a], _flip(me, f))
                cps.append(_remote(blk, blk, ssem.at[7 * a + 4 + j], rsem.at[7 * a + 4 + j], sib))
        return cps

    def start(self, ins, outs, sems):
        cps, loc = self._first(ins, outs, sems)
        for cp in loc + cps:
            cp.start()

    def mid(self, ins, outs, sems):
        ssem, rsem, _ = sems
        me = _me()
        passed = self._passed(outs, sems)
        t = 0
        for j, f in enumerate(CHIP_FLIPS):
            for a in range(self.n):
                blk = _blk(self.kinds[a], outs[a], _flip(me, f))
                _remote(blk, blk, ssem.at[7 * a + 1 + j], rsem.at[7 * a + 1 + j], _flip(me, f)).wait_recv()
                passed[t].start()
                t += 1

    def end(self, ins, outs, sems):
        ssem, rsem, _ = sems
        me = _me()
        sib = _flip(me, (0, 0, 1))
        for a in range(self.n):
            blk = _blk(self.kinds[a], outs[a], sib)
            _remote(blk, blk, ssem.at[7 * a], rsem.at[7 * a], sib).wait_recv()
            for j, f in enumerate(CHIP_FLIPS):
                blk = _blk(self.kinds[a], outs[a], _flip(_flip(me, f), (0, 0, 1)))
                _remote(blk, blk, ssem.at[7 * a + 4 + j], rsem.at[7 * a + 4 + j], sib).wait_recv()
        cps, loc = self._first(ins, outs, sems)
        for cp in cps + self._passed(outs, sems):
            cp.wait_send()
        for cp in loc:
            cp.wait()


class _ChipScatter:
    def __init__(self, grads):
        self.n = len(grads)
        self.ins = list(grads)
        self.out_shape = [jax.ShapeDtypeStruct(g.shape, BF16) for g in grads]
        self.sems = [pltpu.SemaphoreType.DMA((3 * self.n,)), pltpu.SemaphoreType.DMA((3 * self.n,)),
                     pltpu.SemaphoreType.DMA((self.n,))]

    def _copies(self, ins, outs, sems):
        ssem, rsem, lsem = sems
        me = _me()
        mq = 2 * me[0] + me[1]
        loc = [pltpu.make_async_copy(ins[a].at[mq], outs[a].at[mq], lsem.at[a]) for a in range(self.n)]
        cps = []
        for k, f in enumerate(CHIP_FLIPS):
            p = _flip(me, f)
            for a in range(self.n):
                cps.append(_remote(ins[a].at[2 * p[0] + p[1]], outs[a].at[mq], ssem.at[3 * a + k], rsem.at[3 * a + k], p))
        return cps, loc

    def start(self, ins, outs, sems):
        cps, loc = self._copies(ins, outs, sems)
        for cp in loc + cps:
            cp.start()

    mid = None

    def end(self, ins, outs, sems):
        ssem, rsem, _ = sems
        me = _me()
        mq = 2 * me[0] + me[1]
        for k, f in enumerate(CHIP_FLIPS):
            p = _flip(me, f)
            for a in range(self.n):
                _remote(ins[a].at[mq], outs[a].at[2 * p[0] + p[1]], ssem.at[3 * a + k], rsem.at[3 * a + k], p).wait_recv()
        cps, loc = self._copies(ins, outs, sems)
        for cp in cps:
            cp.wait_send()
        for cp in loc:
            cp.wait()


class _AllGather:
    def __init__(self, parts):
        self.n = len(parts)
        self.ins = list(parts)
        self.out_shape = [jax.ShapeDtypeStruct((NDEV,) + p.shape, p.dtype) for p in parts]
        self.sems = [pltpu.SemaphoreType.DMA((7 * self.n,)), pltpu.SemaphoreType.DMA((7 * self.n,)),
                     pltpu.SemaphoreType.DMA((self.n,))]

    def _copies(self, ins, outs, sems):
        ssem, rsem, lsem = sems
        me = _me()
        mi = _lin(me)
        loc = [pltpu.make_async_copy(ins[a], outs[a].at[mi], lsem.at[a]) for a in range(self.n)]
        cps = []
        for k, f in enumerate(FLIPS):
            for a in range(self.n):
                cps.append(_remote(ins[a], outs[a].at[mi], ssem.at[7 * a + k], rsem.at[7 * a + k], _flip(me, f)))
        return cps, loc

    def start(self, ins, outs, sems):
        cps, loc = self._copies(ins, outs, sems)
        for cp in loc + cps:
            cp.start()

    mid = None

    def end(self, ins, outs, sems):
        ssem, rsem, _ = sems
        me = _me()
        for k, f in enumerate(FLIPS):
            p = _flip(me, f)
            for a in range(self.n):
                _remote(ins[a], outs[a].at[_lin(p)], ssem.at[7 * a + k], rsem.at[7 * a + k], p).wait_recv()
        cps, loc = self._copies(ins, outs, sems)
        for cp in cps:
            cp.wait_send()
        for cp in loc:
            cp.wait()


def _call(core, *, name, grid, in_specs, out_specs, out_shape, args, scratch=(), jobs=()):
    n_in, n_out, n_sc = len(in_specs), len(out_specs), len(scratch)
    steps = 1
    for g in grid:
        steps *= g

    def body(*refs):
        pos = [0]

        def take(k):
            r = refs[pos[0]:pos[0] + k]
            pos[0] += k
            return r

        ins = take(n_in)
        j_ins = [take(len(j.ins)) for j in jobs]
        outs = take(n_out)
        j_outs = [take(len(j.out_shape)) for j in jobs]
        scs = take(n_sc)
        j_sems = [take(len(j.sems)) for j in jobs]
        if len(grid) == 2:
            step = pl.program_id(0) * grid[1] + pl.program_id(1)
        elif len(grid) == 1:
            step = pl.program_id(0)
        else:
            step = 0
        for j, ji, jo, js in zip(jobs, j_ins, j_outs, j_sems):
            if grid:
                pl.when(step == 0)(lambda j=j, ji=ji, jo=jo, js=js: j.start(ji, jo, js))
            else:
                j.start(ji, jo, js)
        for j, ji, jo, js in zip(jobs, j_ins, j_outs, j_sems):
            if j.mid is not None and grid:
                at = steps - 1 if j.late_mid else (3 * steps) // 4
                pl.when(step == at)(lambda j=j, ji=ji, jo=jo, js=js: j.mid(ji, jo, js))
        if core is not None:
            core(ins, outs, scs)
        for j, ji, jo, js in zip(jobs, j_ins, j_outs, j_sems):
            if grid:
                pl.when(step == steps - 1)(lambda j=j, ji=ji, jo=jo, js=js: j.end(ji, jo, js))
            else:
                if j.mid is not None:
                    j.mid(ji, jo, js)
                j.end(ji, jo, js)

    all_in = list(in_specs)
    all_args = list(args)
    all_out = list(out_specs)
    all_shape = list(out_shape)
    all_sc = list(scratch)
    for j in jobs:
        all_in += [HBM] * len(j.ins)
        all_args += j.ins
    for j in jobs:
        all_out += [HBM] * len(j.out_shape)
        all_shape += j.out_shape
        all_sc += j.sems
    params = dict(vmem_limit_bytes=VMEM_LIMIT)
    if grid:
        params["dimension_semantics"] = ("arbitrary",) * len(grid)
    res = pl.pallas_call(
        body, name=name, grid=grid, in_specs=all_in, out_specs=all_out, out_shape=all_shape,
        scratch_shapes=all_sc, compiler_params=pltpu.CompilerParams(**params),
    )(*all_args)
    core_res = list(res[:n_out])
    job_res = []
    pos = n_out
    for j in jobs:
        job_res.append(list(res[pos:pos + len(j.out_shape)]))
        pos += len(j.out_shape)
    return core_res, job_res


def _ffn_fwd(x, mod, gvec, w_in, w_out, tm, name, jobs=(), target=None):
    T = x.shape[0]
    nt = T // tm
    tps = nt // mod.shape[0]

    def core(ins, outs, _):
        x_ref, mod_ref, g_ref, win_ref, wout_ref = ins[:5]
        xo_ref, gu_ref, y_ref = outs[:3]
        xv = x_ref[...]
        sh, sc, gt = mod_ref[0:1, :], mod_ref[1:2, :], mod_ref[2:3, :]
        r = lax.rsqrt(_rowmean(xv * xv) + EPS)
        h = (xv * r * g_ref[0:1, :]) * (1.0 + sc) + sh
        hb = h.astype(BF16)
        y = jnp.zeros((tm, D), F32)
        for cidx in range(4):
            gate = _dot_nt(hb, win_ref[cidx])
            up = _dot_nt(hb, win_ref[4 + cidx])
            gu_ref[cidx] = gate.astype(BF16)
            gu_ref[4 + cidx] = up.astype(BF16)
            act = gate * _sigmoid(gate) * up
            y = y + _dot(act.astype(BF16), wout_ref[cidx])
        y_ref[...] = y
        ry = lax.rsqrt(_rowmean(y * y) + EPS)
        xo = xv + (HALF * gt) * (y * ry * g_ref[1:2, :])
        if target is None:
            xo_ref[...] = xo
        else:
            loss_ref = outs[3]

            @pl.when(pl.program_id(0) == 0)
            def _():
                loss_ref[...] = jnp.zeros((8, D), F32)

            err = xo - ins[5][...]
            xo_ref[...] = err * (1.0 / D)
            loss_ref[...] += HALF * jnp.sum(_rowmean(err * err), axis=0, keepdims=True)

    tile = pl.BlockSpec((tm, D), lambda i: (i, 0))
    extra = target is not None
    return _call(
        core, name=name, grid=(nt,), jobs=jobs,
        in_specs=[tile, pl.BlockSpec((None, 8, D), lambda i: (i // tps, 0, 0)), _const_spec((8, D)),
                  _const_spec((8, FBP, D)), _const_spec((4, FBP, D))] + [tile] * extra,
        out_specs=[tile, pl.BlockSpec((8, tm, FBP), lambda i: (0, i, 0)), tile]
        + [pl.BlockSpec((8, D), lambda i: (0, 0))] * extra,
        out_shape=[jax.ShapeDtypeStruct((T, D), F32), jax.ShapeDtypeStruct((8, T, FBP), BF16),
                   jax.ShapeDtypeStruct((T, D), F32)] + [jax.ShapeDtypeStruct((8, D), F32)] * extra,
        args=[x, mod, gvec, w_in, w_out] + [target] * extra)


def _ffn_bwd(dxo, x, y, gu, mod, gvec, w_in, w_out, tm, name, jobs=()):
    T = x.shape[0]
    nt = T // tm
    nb = mod.shape[0]
    tps = nt // nb

    def core(ins, outs, _):
        dxo_ref, x_ref, y_ref, gu_ref, mod_ref, g_ref, win_ref, wout_ref = ins
        dx_ref, dg_ref, act_ref, hb_ref, dyb_ref, mg_ref, vg_ref = outs
        i = pl.program_id(0)
        xv = x_ref[...]
        dxo_v = dxo_ref[...]
        yv = y_ref[...]
        sh, sc, gt = mod_ref[0:1, :], mod_ref[1:2, :], mod_ref[2:3, :]
        gpre, gpost = g_ref[0:1, :], g_ref[1:2, :]
        r = lax.rsqrt(_rowmean(xv * xv) + EPS)
        xh = xv * r
        n = xh * gpre
        hb = (n * (1.0 + sc) + sh).astype(BF16)
        hb_ref[...] = hb
        ry = lax.rsqrt(_rowmean(yv * yv) + EPS)
        yh = yv * ry
        d_gt = _colsum(HALF * dxo_v * (yh * gpost))
        dp = (HALF * gt) * dxo_v
        d_gpost = _colsum(dp * yh)
        dyh = dp * gpost
        dy = ry * (dyh - yh * _rowmean(dyh * yh))
        dyb = dy.astype(BF16)
        dyb_ref[...] = dyb
        dh = jnp.zeros((tm, D), F32)
        for cidx in range(4):
            gate = gu_ref[cidx].astype(F32)
            up = gu_ref[4 + cidx].astype(F32)
            sig = _sigmoid(gate)
            s = gate * sig
            act_ref[cidx] = (s * up).astype(BF16)
            d_act = _dot_nt(dyb, wout_ref[cidx])
            d_up = (d_act * s).astype(BF16)
            d_gate = (d_act * up * (sig * (1.0 + gate * (1.0 - sig)))).astype(BF16)
            dg_ref[cidx] = d_gate
            dg_ref[4 + cidx] = d_up
            dh = dh + _dot(d_gate, win_ref[cidx]) + _dot(d_up, win_ref[4 + cidx])
        d_sc = _colsum(dh * n)
        d_sh = _colsum(dh)
        dn = dh * (1.0 + sc)
        d_gpre = _colsum(dn * xh)
        dxh = dn * gpre
        dx_ref[...] = dxo_v + r * (dxh - xh * _rowmean(dxh * xh))

        @pl.when(i % tps == 0)
        def _():
            mg_ref[...] = jnp.zeros((8, D), F32)

        @pl.when(i == 0)
        def _():
            vg_ref[...] = jnp.zeros((8, D), F32)

        mg_ref[0:1, :] += d_sh
        mg_ref[1:2, :] += d_sc
        mg_ref[2:3, :] += d_gt
        vg_ref[0:1, :] += d_gpre
        vg_ref[1:2, :] += d_gpost

    tile = pl.BlockSpec((tm, D), lambda i: (i, 0))
    return _call(
        core, name=name, grid=(nt,), jobs=jobs,
        in_specs=[tile, tile, tile, pl.BlockSpec((8, tm, FBP), lambda i: (0, i, 0)),
                  pl.BlockSpec((None, 8, D), lambda i: (i // tps, 0, 0)), _const_spec((8, D)),
                  _const_spec((8, FBP, D)), _const_spec((4, FBP, D))],
        out_specs=[tile, pl.BlockSpec((8, tm, FBP), lambda i: (0, i, 0)),
                   pl.BlockSpec((4, tm, FBP), lambda i: (0, i, 0)), tile, tile,
                   pl.BlockSpec((None, 8, D), lambda i: (i // tps, 0, 0)), pl.BlockSpec((8, D), lambda i: (0, 0))],
        out_shape=[jax.ShapeDtypeStruct((T, D), F32), jax.ShapeDtypeStruct((8, T, FBP), BF16),
                   jax.ShapeDtypeStruct((4, T, FBP), BF16), jax.ShapeDtypeStruct((T, D), BF16),
                   jax.ShapeDtypeStruct((T, D), BF16), jax.ShapeDtypeStruct((nb, 8, D), F32),
                   jax.ShapeDtypeStruct((8, D), F32)],
        args=[dxo, x, y, gu, mod, gvec, w_in, w_out])


def _masked_spatial(ws_ref):
    row = lax.broadcasted_iota(jnp.int32, (CHUNK, CHUNK), 0)
    col = lax.broadcasted_iota(jnp.int32, (CHUNK, CHUNK), 1)
    keep = col <= row
    return [jnp.where(keep, ws_ref[hd], 0.0).astype(BF16) for hd in range(NHEAD)]


def _spatial_gate(wm, vb_chunk, lane_head):
    z = jnp.zeros((CHUNK, WA), F32)
    for hd in range(NHEAD):
        z = jnp.where(lane_head == hd, _dot(wm[hd], vb_chunk), z)
    return z


def _layer_norm_stats(v):
    mu = _rowmean(v)
    vc = v - mu
    rstd = lax.rsqrt(_rowmean(vc * vc) + EPS)
    return vc * rstd, rstd


def _pitch(tm):
    p = tm // 8
    while p % 8 != 4:
        p += 1
    return p


def _lanes(s):
    return slice(s * 128, (s + 1) * 128)


def _to_slabs(ref, row0, val):
    for s in range(4):
        ref[s, row0:row0 + val.shape[0], :] = val[:, _lanes(s)]


def _tap_sum(src, out, cw_ref, bias, tm, start):
    p = _pitch(tm)
    for s in range(4):
        accs = [jnp.broadcast_to(bias[:, _lanes(s)], (8, 128))] * p
        for k in range(CONV_K):
            w = jnp.broadcast_to(cw_ref[k:k + 1, _lanes(s)], (8, 128))
            for v in range(p):
                accs[v] = accs[v] + w * src[s, pl.ds(v + start(k), 8, stride=p), :]
        for v in range(p):
            out[s, pl.ds(v, 8, stride=p), :] = accs[v]
    return jnp.concatenate([out[s, 0:tm, :] for s in range(4)], axis=1)


def _mixer_fwd(x, mod, gvec, w_mi, w_mo, v512, ws, bias_full, cw, tm, name, jobs=()):
    T = x.shape[0]
    nt = T // tm
    tps = nt // mod.shape[0]
    ext_rows = 8 * _pitch(tm)

    def core(ins, outs, scs):
        x_ref, mod_ref, g_ref, wmi_ref, wmo_ref, v_ref, ws_ref, bias_ref, cw_ref = ins
        xo_ref, proj_ref, ym_ref, conv_ref = outs
        glu_ext, conv_scr = scs
        i = pl.program_id(0)
        xv = x_ref[...]
        sh, sc, gt = mod_ref[0:1, :], mod_ref[1:2, :], mod_ref[2:3, :]
        r = lax.rsqrt(_rowmean(xv * xv) + EPS)
        hb = ((xv * r * g_ref[0:1, :]) * (1.0 + sc) + sh).astype(BF16)
        for j in range(NDEV):
            proj_ref[:, j * MB:(j + 1) * MB] = _dot(hb, wmi_ref[j])
        u = proj_ref[:, 0:WA]
        v0 = proj_ref[:, WA:2 * WA]
        a = proj_ref[:, 2 * WA:3 * WA]
        g = proj_ref[:, 3 * WA:4 * WA]
        vh, _ = _layer_norm_stats(v0)
        vb = (vh * v_ref[0:1, :] + v_ref[1:2, :]).astype(BF16)
        wm = _masked_spatial(ws_ref)
        lane_head = lax.broadcasted_iota(jnp.int32, (CHUNK, WA), 1) >> 6
        ya = []
        for q in range(tm // CHUNK):
            z = _spatial_gate(wm, vb[q * CHUNK:(q + 1) * CHUNK, :], lane_head) + bias_ref[...]
            ya.append(u[q * CHUNK:(q + 1) * CHUNK, :] * z)
        ya = jnp.concatenate(ya, axis=0)
        glu = a * _sigmoid(g)

        @pl.when(i == 0)
        def _():
            glu_ext[:, HALO + tm:HALO + ext_rows, :] = jnp.zeros((4, ext_rows - tm, 128), F32)

        @pl.when(i % tps == 0)
        def _():
            glu_ext[:, 0:HALO, :] = jnp.zeros((4, HALO, 128), F32)

        _to_slabs(glu_ext, HALO, glu)
        conv = _tap_sum(glu_ext, conv_scr, cw_ref, v_ref[2:3, :], tm, lambda k: HALO - (CONV_K - 1) + k)
        conv_ref[...] = conv
        glu_ext[:, 0:HALO, :] = glu_ext[:, tm:tm + HALO, :]
        ch, _ = _layer_norm_stats(conv)
        cn = ch * v_ref[3:4, :] + v_ref[4:5, :]
        yb = cn * _sigmoid(cn)
        pa = ya * lax.rsqrt(_rowmean(ya * ya) + EPS) * v_ref[5:6, :]
        pb = yb * lax.rsqrt(_rowmean(yb * yb) + EPS) * v_ref[6:7, :]
        ycat = jnp.concatenate([pa, pb], axis=1).astype(BF16)
        ym = _dot(ycat, wmo_ref[...])
        ym_ref[...] = ym
        rm = lax.rsqrt(_rowmean(ym * ym) + EPS)
        xo_ref[...] = xv + gt * (ym * rm * g_ref[1:2, :])

    tile = pl.BlockSpec((tm, D), lambda i: (i, 0))
    return _call(
        core, name=name, grid=(nt,), jobs=jobs,
        in_specs=[tile, pl.BlockSpec((None, 8, D), lambda i: (i // tps, 0, 0)), _const_spec((8, D)),
                  _const_spec((NDEV, D, MB)), _const_spec((D, D)), _const_spec((8, WA)),
                  _const_spec((NHEAD, CHUNK, CHUNK)), _const_spec((CHUNK, WA)), _const_spec((32, WA))],
        out_specs=[tile, pl.BlockSpec((tm, 4 * WA), lambda i: (i, 0)), tile, pl.BlockSpec((tm, WA), lambda i: (i, 0))],
        out_shape=[jax.ShapeDtypeStruct((T, D), F32), jax.ShapeDtypeStruct((T, 4 * WA), F32),
                   jax.ShapeDtypeStruct((T, D), F32), jax.ShapeDtypeStruct((T, WA), F32)],
        scratch=[pltpu.VMEM((4, HALO + ext_rows, 128), F32), pltpu.VMEM((4, ext_rows, 128), F32)],
        args=[x, mod, gvec, w_mi, w_mo, v512, ws, bias_full, cw])


def _mixer_bwd_a(dxo, ym, proj, conv, mod, gvec, w_mo, v512, ws, bias_full, esel, tm, name, jobs=()):
    T = dxo.shape[0]
    nt = T // tm
    nb = mod.shape[0]
    tps = nt // nb

    def core(ins, outs, scs):
        dxo_ref, ym_ref, proj_ref, conv_ref, mod_ref, g_ref, wmo_ref, v_ref, ws_ref, bias_ref, e_ref = ins
        dpart_ref, dymb_ref, ycat_ref, mg_ref, vg_ref, v5g_ref, gws_ref, gbs_ref = outs
        (dbs_acc,) = scs
        i = pl.program_id(0)
        dxo_v = dxo_ref[...]
        ymv = ym_ref[...]
        gt = mod_ref[2:3, :]
        gpost = g_ref[1:2, :]
        rm = lax.rsqrt(_rowmean(ymv * ymv) + EPS)
        ymh = ymv * rm
        d_gt = _colsum(dxo_v * (ymh * gpost))
        dpm = gt * dxo_v
        d_gpost = _colsum(dpm * ymh)
        dymh = dpm * gpost
        dym = (rm * (dymh - ymh * _rowmean(dymh * ymh))).astype(BF16)
        dymb_ref[...] = dym
        dycat = _dot_nt(dym, wmo_ref[...])
        u = proj_ref[:, 0:WA]
        v0 = proj_ref[:, WA:2 * WA]
        vh, rv = _layer_norm_stats(v0)
        vb = (vh * v_ref[0:1, :] + v_ref[1:2, :]).astype(BF16)
        wm = _masked_spatial(ws_ref)
        lane_head = lax.broadcasted_iota(jnp.int32, (CHUNK, WA), 1) >> 6
        zs = []
        for q in range(tm // CHUNK):
            zs.append(_spatial_gate(wm, vb[q * CHUNK:(q + 1) * CHUNK, :], lane_head) + bias_ref[...])
        z = jnp.concatenate(zs, axis=0)
        ya = u * z
        ra = lax.rsqrt(_rowmean(ya * ya) + EPS)
        yah = ya * ra
        ch, rc = _layer_norm_stats(conv_ref[...])
        cn = ch * v_ref[3:4, :] + v_ref[4:5, :]
        sg = _sigmoid(cn)
        yb = cn * sg
        rb = lax.rsqrt(_rowmean(yb * yb) + EPS)
        ybh = yb * rb
        ycat_ref[...] = jnp.concatenate([yah * v_ref[5:6, :], ybh * v_ref[6:7, :]], axis=1).astype(BF16)
        dpa = dycat[:, 0:WA]
        dpb = dycat[:, WA:2 * WA]
        d_goa = _colsum(dpa * yah)
        d_gob = _colsum(dpb * ybh)
        dyah = dpa * v_ref[5:6, :]
        dybh = dpb * v_ref[6:7, :]
        dya = ra * (dyah - yah * _rowmean(dyah * yah))
        dyb = rb * (dybh - ybh * _rowmean(dybh * ybh))
        dpart_ref[:, 0:WA] = dya * z
        dz = dya * u

        @pl.when(i == 0)
        def _():
            gws_ref[...] = jnp.zeros((NHEAD, CHUNK, CHUNK), F32)
            dbs_acc[...] = jnp.zeros((CHUNK, WA), F32)
            vg_ref[...] = jnp.zeros((8, D), F32)
            v5g_ref[...] = jnp.zeros((8, WA), F32)

        dvs = []
        for q in range(tm // CHUNK):
            dz_q = dz[q * CHUNK:(q + 1) * CHUNK, :]
            vb_q = vb[q * CHUNK:(q + 1) * CHUNK, :]
            dbs_acc[...] += dz_q
            dzb = dz_q.astype(BF16)
            dv_q = jnp.zeros((CHUNK, WA), F32)
            for hd in range(NHEAD):
                dv_q = jnp.where(lane_head == hd, _dot_tn(wm[hd], dzb), dv_q)
                dz_hd = jnp.where(lane_head == hd, dz_q, 0.0).astype(BF16)
                gws_ref[hd] += _dot_nt(dz_hd, vb_q)
            dvs.append(dv_q)
        dv = jnp.concatenate(dvs, axis=0)
        d_gng = _colsum(dv * vh)
        d_gnb = _colsum(dv)
        dvh = dv * v_ref[0:1, :]
        dpart_ref[:, WA:2 * WA] = rv * (dvh - _rowmean(dvh) - vh * _rowmean(dvh * vh))
        dcn = dyb * (sg * (1.0 + cn * (1.0 - sg)))
        d_cng = _colsum(dcn * ch)
        d_cnb = _colsum(dcn)
        dch = dcn * v_ref[3:4, :]
        dconv = rc * (dch - _rowmean(dch) - ch * _rowmean(dch * ch))
        dpart_ref[:, 2 * WA:3 * WA] = dconv
        dpart_ref[:, 3 * WA:4 * WA] = jnp.zeros((tm, WA), F32)
        d_cb = _colsum(dconv)

        @pl.when(i % tps == 0)
        def _():
            mg_ref[...] = jnp.zeros((8, D), F32)

        mg_ref[2:3, :] += d_gt
        vg_ref[1:2, :] += d_gpost
        v5g_ref[0:1, :] += d_gng
        v5g_ref[1:2, :] += d_gnb
        v5g_ref[2:3, :] += d_cb
        v5g_ref[3:4, :] += d_cng
        v5g_ref[4:5, :] += d_cnb
        v5g_ref[5:6, :] += d_goa
        v5g_ref[6:7, :] += d_gob

        @pl.when(i == nt - 1)
        def _():
            row = lax.broadcasted_iota(jnp.int32, (CHUNK, CHUNK), 0)
            col = lax.broadcasted_iota(jnp.int32, (CHUNK, CHUNK), 1)
            for hd in range(NHEAD):
                gws_ref[hd] = jnp.where(col <= row, gws_ref[hd], 0.0)
            gbs_ref[...] = lax.dot_general(e_ref[...], dbs_acc[...], (((1,), (1,)), ((), ())),
                                           precision=lax.Precision.HIGHEST, preferred_element_type=F32)

    tile = pl.BlockSpec((tm, D), lambda i: (i, 0))
    ptile = pl.BlockSpec((tm, 4 * WA), lambda i: (i, 0))
    return _call(
        core, name=name, grid=(nt,), jobs=jobs,
        in_specs=[tile, tile, pl.BlockSpec((tm, 2 * WA), lambda i: (i, 0)), pl.BlockSpec((tm, WA), lambda i: (i, 0)),
                  pl.BlockSpec((None, 8, D), lambda i: (i // tps, 0, 0)), _const_spec((8, D)), _const_spec((D, D)),
                  _const_spec((8, WA)), _const_spec((NHEAD, CHUNK, CHUNK)), _const_spec((CHUNK, WA)),
                  _const_spec((8, WA))],
        out_specs=[ptile, tile, tile, pl.BlockSpec((None, 8, D), lambda i: (i // tps, 0, 0)),
                   pl.BlockSpec((8, D), lambda i: (0, 0)), pl.BlockSpec((8, WA), lambda i: (0, 0)),
                   pl.BlockSpec((NHEAD, CHUNK, CHUNK), lambda i: (0, 0, 0)), pl.BlockSpec((8, CHUNK), lambda i: (0, 0))],
        out_shape=[jax.ShapeDtypeStruct((T, 4 * WA), F32), jax.ShapeDtypeStruct((T, D), BF16),
                   jax.ShapeDtypeStruct((T, D), BF16), jax.ShapeDtypeStruct((nb, 8, D), F32),
                   jax.ShapeDtypeStruct((8, D), F32), jax.ShapeDtypeStruct((8, WA), F32),
                   jax.ShapeDtypeStruct((NHEAD, CHUNK, CHUNK), F32), jax.ShapeDtypeStruct((8, CHUNK), F32)],
        scratch=[pltpu.VMEM((CHUNK, WA), F32)],
        args=[dxo, ym, proj, conv, mod, gvec, w_mo, v512, ws, bias_full, esel])


def _mixer_bwd_b(dxo, x, dpart, proj, mod, gvec, w_mi, cw, tm, name, jobs=()):
    T = x.shape[0]
    nt = T // tm
    nb = mod.shape[0]
    tps = nt // nb
    hpt = tm // HALO
    nh = T // HALO
    off = HALO - (CONV_K - 1)
    p = _pitch(tm)
    ext_rows = 8 * p

    def core(ins, outs, scs):
        dxo_ref, x_ref, dpart_ref, dnext_ref, ag_ref, halo_ref, mod_ref, g_ref, wmi_ref, cw_ref = ins
        dx_ref, dproj_ref, hb_ref, mg_ref, vg_ref, dcw_ref = outs
        glu_ext, dconv_ext, dglu_scr, dcw_acc = scs
        i = pl.program_id(0)
        first = i % tps == 0
        last = i % tps == tps - 1
        a = ag_ref[:, 0:WA]
        g = ag_ref[:, WA:2 * WA]
        sgg = _sigmoid(g)

        @pl.when(i == 0)
        def _():
            glu_ext[:, HALO + tm:HALO + ext_rows, :] = jnp.zeros((4, ext_rows - tm, 128), F32)
            dconv_ext[:, HALO + tm:HALO + ext_rows, :] = jnp.zeros((4, ext_rows - tm, 128), F32)
            dcw_acc[...] = jnp.zeros((32, 8, WA), F32)
            vg_ref[...] = jnp.zeros((8, D), F32)

        _to_slabs(glu_ext, 0, jnp.where(first, 0.0, halo_ref[:, 0:WA] * _sigmoid(halo_ref[:, WA:2 * WA])))
        _to_slabs(glu_ext, HALO, a * sgg)
        _to_slabs(dconv_ext, 0, dpart_ref[:, 2 * WA:3 * WA])
        _to_slabs(dconv_ext, tm, jnp.where(last, 0.0, dnext_ref[...]))
        sub = lax.broadcasted_iota(jnp.int32, (8, 128), 0)
        for s in range(4):
            accs = [jnp.zeros((8, 128), F32)] * CONV_K
            for v in range(p):
                dc = jnp.where(v + p * sub < tm, dconv_ext[s, pl.ds(v, 8, stride=p), :], 0.0)
                for k in range(CONV_K):
                    accs[k] = accs[k] + dc * glu_ext[s, pl.ds(v + off + k, 8, stride=p), :]
            for k in range(CONV_K):
                dcw_acc[k, :, _lanes(s)] += accs[k]
        dglu = _tap_sum(dconv_ext, dglu_scr, cw_ref, jnp.zeros((1, WA), F32), tm, lambda k: (CONV_K - 1) - k)

        @pl.when(i == nt - 1)
        def _():
            for k in range(CONV_K):
                dcw_ref[k:k + 1, :] = jnp.sum(dcw_acc[k], axis=0, keepdims=True)
            dcw_ref[CONV_K:32, :] = jnp.zeros((32 - CONV_K, WA), F32)

        da = dglu * sgg
        dgg = dglu * a * (sgg * (1.0 - sgg))
        dproj_ref[:, 0:2 * WA] = dpart_ref[:, 0:2 * WA].astype(BF16)
        dproj_ref[:, 2 * WA:3 * WA] = da.astype(BF16)
        dproj_ref[:, 3 * WA:4 * WA] = dgg.astype(BF16)
        dh = jnp.zeros((tm, D), F32)
        for j in range(NDEV):
            dh = dh + _dot_nt(dproj_ref[:, j * MB:(j + 1) * MB], wmi_ref[j])
        xv = x_ref[...]
        sc, sh = mod_ref[1:2, :], mod_ref[0:1, :]
        gpre = g_ref[0:1, :]
        r = lax.rsqrt(_rowmean(xv * xv) + EPS)
        xh = xv * r
        n = xh * gpre
        hb_ref[...] = (n * (1.0 + sc) + sh).astype(BF16)
        d_sc = _colsum(dh * n)
        d_sh = _colsum(dh)
        dn = dh * (1.0 + sc)
        d_gpre = _colsum(dn * xh)
        dxh = dn * gpre
        dx_ref[...] = dxo_ref[...] + r * (dxh - xh * _rowmean(dxh * xh))

        @pl.when(first)
        def _():
            mg_ref[...] = jnp.zeros((8, D), F32)

        mg_ref[0:1, :] += d_sh
        mg_ref[1:2, :] += d_sc
        vg_ref[0:1, :] += d_gpre

    tile = pl.BlockSpec((tm, D), lambda i: (i, 0))
    return _call(
        core, name=name, grid=(nt,), jobs=jobs,
        in_specs=[tile, tile, pl.BlockSpec((tm, 4 * WA), lambda i: (i, 0)),
                  pl.BlockSpec((HALO, WA), lambda i: (jnp.minimum((i + 1) * hpt, nh - 1), 2)),
                  pl.BlockSpec((tm, 2 * WA), lambda i: (i, 1)),
                  pl.BlockSpec((HALO, 2 * WA), lambda i: (jnp.maximum(i * hpt - 1, 0), 1)),
                  pl.BlockSpec((None, 8, D), lambda i: (i // tps, 0, 0)), _const_spec((8, D)),
                  _const_spec((NDEV, D, MB)), _const_spec((32, WA))],
        out_specs=[tile, pl.BlockSpec((tm, 4 * WA), lambda i: (i, 0)), tile,
                   pl.BlockSpec((None, 8, D), lambda i: (i // tps, 0, 0)), pl.BlockSpec((8, D), lambda i: (0, 0)),
                   pl.BlockSpec((32, WA), lambda i: (0, 0))],
        out_shape=[jax.ShapeDtypeStruct((T, D), F32), jax.ShapeDtypeStruct((T, 4 * WA), BF16),
                   jax.ShapeDtypeStruct((T, D), BF16), jax.ShapeDtypeStruct((nb, 8, D), F32),
                   jax.ShapeDtypeStruct((8, D), F32), jax.ShapeDtypeStruct((32, WA), F32)],
        scratch=[pltpu.VMEM((4, HALO + ext_rows, 128), F32), pltpu.VMEM((4, HALO + ext_rows, 128), F32),
                 pltpu.VMEM((4, ext_rows, 128), F32), pltpu.VMEM((32, 8, WA), F32)],
        args=[dxo, x, dpart, dpart, proj, proj, mod, gvec, w_mi, cw])


def _grad_chip(a, b, a_spec, b_spec, prod_shape, half, name, jobs=()):
    steps = 8 if half is None else 4
    R = prod_shape[0] if half is None else half
    C = prod_shape[1]

    def core(ins, outs, scs):
        a_ref, b_ref = ins
        (o_ref,) = outs
        own, snd, rcv, ssem, rsem, lsem = scs
        s = pl.program_id(0)
        c = lax.axis_index("c")
        me = _me()
        sib = _flip(me, (0, 0, 1))
        prod = _dot_tn(a_ref[...], b_ref[...]).astype(BF16)
        if half is None:
            q = s // 2

            @pl.when(s % 2 == c)
            def _():
                own[q] = prod

            @pl.when(s % 2 != c)
            def _():
                snd[q] = prod
                _remote(snd.at[q], rcv.at[q], ssem.at[q], rsem.at[q], sib).start()
        else:
            lo = prod[0:half, :]
            hi = prod[half:2 * half, :]
            own[s] = jnp.where(c == 0, lo, hi)
            snd[s] = jnp.where(c == 0, hi, lo)
            _remote(snd.at[s], rcv.at[s], ssem.at[s], rsem.at[s], sib).start()

        @pl.when(s == steps - 1)
        def _():
            for q4 in range(4):
                cp = _remote(snd.at[q4], rcv.at[q4], ssem.at[q4], rsem.at[q4], sib)
                cp.wait_recv()
                cp.wait_send()
                snd[q4] = (own[q4].astype(F32) + rcv[q4].astype(F32)).astype(BF16)
            out = pltpu.make_async_copy(snd, o_ref, lsem)
            out.start()
            out.wait()

    return _call(
        core, name=name, grid=(steps,), jobs=jobs, in_specs=[a_spec, b_spec], out_specs=[HBM],
        out_shape=[jax.ShapeDtypeStruct((4, R, C), BF16)],
        scratch=[pltpu.VMEM((4, R, C), BF16), pltpu.VMEM((4, R, C), BF16), pltpu.VMEM((4, R, C), BF16),
                 pltpu.SemaphoreType.DMA((4,)), pltpu.SemaphoreType.DMA((4,)), pltpu.SemaphoreType.DMA],
        args=[a, b])


def _grad_w_in(dg, hb, name, jobs=()):
    T = hb.shape[0]
    return _grad_chip(dg, hb, pl.BlockSpec((None, T, FBP), lambda s: (s, 0, 0)), _const_spec((T, D)),
                      (FBP, D), None, name, jobs)


def _grad_w_out(act, dyb, name, jobs=()):
    T = dyb.shape[0]
    return _grad_chip(act, dyb, pl.BlockSpec((None, T, FBP), lambda s: (s, 0, 0)), _const_spec((T, D)),
                      (FBP, D), FO, name, jobs)


def _grad_w_mi(hb, dproj, name, jobs=()):
    T = hb.shape[0]
    return _grad_chip(hb, dproj, _const_spec((T, D)), pl.BlockSpec((T, MB), lambda s: (0, s)),
                      (D, MB), None, name, jobs)


def _grad_w_mo(ycat, dym, name, jobs=()):
    T = ycat.shape[0]
    return _grad_chip(ycat, dym, pl.BlockSpec((T, 2 * MO), lambda s: (0, s)), _const_spec((T, D)),
                      (2 * MO, D), MO, name, jobs)


def _adamw_math(w, g, m, v):
    m2 = ADAM_B1 * m + (1.0 - ADAM_B1) * g
    v2 = ADAM_B2 * v + (1.0 - ADAM_B2) * (g * g)
    m_hat = m2 / (1.0 - ADAM_B1 ** ADAM_STEP)
    v_hat = v2 / (1.0 - ADAM_B2 ** ADAM_STEP)
    delta = -ADAM_LR * (m_hat / (jnp.sqrt(v_hat) + ADAM_EPS) + ADAM_WD * w)
    return delta, m2, v2


def _adamw_reduce(parts, w, m, v, tr, name, own=None, after=None):
    R, C = w.shape

    def core(ins, outs, _):
        p_ref, w_ref, m_ref, v_ref = ins[:4]
        g_ref, d_ref, m2_ref, v2_ref = outs
        if own is None:
            terms = [p_ref[s].astype(F32) for s in range(4)]
        else:
            mq = 2 * lax.axis_index("x") + lax.axis_index("y")
            mine = ins[4][...].astype(F32)
            terms = [jnp.where(mq == s, mine, p_ref[s].astype(F32)) for s in range(4)]
        g = terms[0]
        for s in range(1, 4):
            g = g + terms[s]
        g_ref[...] = g
        d_ref[...], m2_ref[...], v2_ref[...] = _adamw_math(w_ref[...], g, m_ref[...], v_ref[...])

    blk = pl.BlockSpec((tr, C), lambda i: (i, 0))
    in_specs = [pl.BlockSpec((4, tr, C), lambda i: (0, i, 0)), blk, blk, blk]
    args = [parts, w, m, v]
    if own is not None:
        mq = 2 * lax.axis_index("x") + lax.axis_index("y")
        in_specs.append(pl.BlockSpec((tr, C), lambda i: (i, 0)))
        args.append(lax.dynamic_index_in_dim(own, mq, 0, keepdims=False))
    if after is not None:
        in_specs.append(HBM)
        args.append(after)
    return _call(
        core, name=name, grid=(R // tr,), in_specs=in_specs,
        out_specs=[blk, blk, blk, blk], out_shape=[jax.ShapeDtypeStruct((R, C), F32)] * 4, args=args)[0]


HBM_ONLY = pl.BlockSpec(memory_space=pltpu.HBM)
SEM = pl.BlockSpec(memory_space=pltpu.SEMAPHORE)
EFFECT = pltpu.SideEffectType.DATAFLOW_SIDE_EFFECTING


def _chip_scatter_start(g):
    def body(g_ref, land_ref, ssem, rsem, g_thru, land_thru, token):
        me = _me()
        mq = 2 * me[0] + me[1]
        for k, f in enumerate(CHIP_FLIPS):
            p = _flip(me, f)
            _remote(g_ref.at[2 * p[0] + p[1]], land_ref.at[mq], ssem.at[k], rsem.at[k], p).start()
        token[...] = jnp.zeros_like(token)

    land = pltpu.with_memory_space_constraint(lax.empty(g.shape, g.dtype), pltpu.HBM)
    return pl.pallas_call(
        body, name="tail_start",
        out_shape=(pltpu.SemaphoreType.DMA((3,)), pltpu.SemaphoreType.DMA((3,)), pltpu.HBM(g.shape, g.dtype),
                   pltpu.HBM(g.shape, g.dtype), jax.ShapeDtypeStruct((8, 128), F32)),
        in_specs=(HBM_ONLY, HBM_ONLY), out_specs=(SEM, SEM, HBM_ONLY, HBM_ONLY, VM),
        input_output_aliases={0: 2, 1: 3},
        compiler_params=pltpu.CompilerParams(has_side_effects=EFFECT),
    )(pltpu.with_memory_space_constraint(g, pltpu.HBM), land)


def _chip_scatter_wait(ssem, rsem, g_thru, land_thru, after):
    def body(g_ref, land_ref, ssem, rsem, after_ref, g_dead, got_ref):
        me = _me()
        mq = 2 * me[0] + me[1]
        for k, f in enumerate(CHIP_FLIPS):
            p = _flip(me, f)
            pq = 2 * p[0] + p[1]
            _remote(g_ref.at[pq], land_ref.at[mq], ssem.at[k], rsem.at[k], p).wait_send()
            _remote(g_ref.at[mq], land_ref.at[pq], ssem.at[k], rsem.at[k], p).wait_recv()

    return pl.pallas_call(
        body, name="tail_wait",
        out_shape=(pltpu.HBM(g_thru.shape, g_thru.dtype), pltpu.HBM(g_thru.shape, g_thru.dtype)),
        in_specs=(HBM_ONLY, HBM_ONLY, SEM, SEM, HBM), out_specs=(HBM_ONLY, HBM_ONLY),
        input_output_aliases={0: 0, 1: 1},
        compiler_params=pltpu.CompilerParams(has_side_effects=EFFECT),
    )(g_thru, land_thru, ssem, rsem, after)


def _adamw_ada(sc_all, dd, w, m, v, tr, name):
    R, C = w.shape

    def core(ins, outs, _):
        sc_ref, dd_ref, w_ref, m_ref, v_ref = ins
        g_ref, d_ref, m2_ref, v2_ref = outs
        g = _dot_tn(sc_ref[...].astype(BF16), dd_ref[...].astype(BF16))
        g_ref[...] = g
        d_ref[...], m2_ref[...], v2_ref[...] = _adamw_math(w_ref[...], g, m_ref[...], v_ref[...])

    blk = pl.BlockSpec((tr, C), lambda i: (i, 0))
    return _call(
        core, name=name, grid=(R // tr,),
        in_specs=[pl.BlockSpec((64, tr), lambda i: (0, i)), pl.BlockSpec((64, C), lambda i: (0, 0)), blk, blk, blk],
        out_specs=[blk, blk, blk, blk], out_shape=[jax.ShapeDtypeStruct((R, C), F32)] * 4,
        args=[sc_all, dd, w, m, v])[0]


def _adamw_small(gathered, plain, grads, wmv, emit, name):
    nw = len(grads)
    ng, npl, ne = len(gathered), len(plain), len(emit)

    def core(ins, outs, _):
        srcs = []
        for a in range(ng):
            s = ins[a][0]
            for dev in range(1, NDEV):
                s = s + ins[a][dev]
            srcs.append(s)
        srcs += [ins[ng + a][...] for a in range(npl)]
        w_refs = ins[ng + npl:]
        for e, a in enumerate(emit):
            outs[e][...] = srcs[a]
        for t in range(nw):
            src, row = grads[t]
            g = srcs[src] if row is None else srcs[src][row:row + 1, :]
            w_ref, m_ref, v_ref = w_refs[3 * t:3 * t + 3]
            g_ref, d_ref, m2_ref, v2_ref = outs[ne + 4 * t:ne + 4 * t + 4]
            g_ref[...] = g
            d_ref[...], m2_ref[...], v2_ref[...] = _adamw_math(w_ref[...], g, m_ref[...], v_ref[...])

    out_shape = [jax.ShapeDtypeStruct(gathered[a].shape[1:], F32) for a in emit]
    for t in range(nw):
        out_shape += [jax.ShapeDtypeStruct(wmv[3 * t].shape, F32)] * 4
    return _call(
        core, name=name, grid=(), in_specs=[VM] * (ng + npl + 3 * nw), out_specs=[VM] * (ne + 4 * nw),
        out_shape=out_shape, args=list(gathered) + list(plain) + list(wmv))[0]


def _ada_fwd(c_pad, w_ada, b_cols, cw_pad, jobs=()):
    def core(ins, outs, scs):
        c_ref, w_ref, b_ref, cwp_ref = ins
        ada_ref, sc_ref, cw_ref = outs
        cbuf, send_buf, ssem, rsem = scs
        me = _me()
        mi = _lin(me)
        cbuf[mi] = c_ref[...]
        cw_ref[mi] = cwp_ref[...]
        peers = [_flip(me, f) for f in FLIPS]
        first = []
        for k, p in enumerate(peers):
            first.append(_remote(cbuf.at[mi], cbuf.at[mi], ssem.at[k], rsem.at[k], p))
            first.append(_remote(cw_ref.at[mi], cw_ref.at[mi], ssem.at[7 + k], rsem.at[7 + k], p))
        for cp in first:
            cp.start()
        for k, p in enumerate(peers):
            pi = _lin(p)
            _remote(cbuf.at[pi], cbuf.at[pi], ssem.at[k], rsem.at[k], p).wait_recv()
            _remote(cw_ref.at[pi], cw_ref.at[pi], ssem.at[7 + k], rsem.at[7 + k], p).wait_recv()
        c_all = cbuf[...].reshape(8 * 8, D)
        sc = c_all * _sigmoid(c_all)
        sc_ref[...] = sc
        res = _dot(sc.astype(BF16), w_ref[...].astype(BF16)) + b_ref[...]
        send_buf[...] = res.reshape(8, 8, ADA_B)
        ada_ref[mi] = send_buf[mi]
        second = []
        for k, p in enumerate(peers):
            second.append(_remote(send_buf.at[_lin(p)], ada_ref.at[mi], ssem.at[14 + k], rsem.at[14 + k], p))
        for cp in second:
            cp.start()
        for k, p in enumerate(peers):
            _remote(send_buf.at[mi], ada_ref.at[_lin(p)], ssem.at[14 + k], rsem.at[14 + k], p).wait_recv()
        for cp in first + second:
            cp.wait_send()

    return _call(
        core, name="ada_fwd", grid=(), jobs=jobs, in_specs=[VM, VM, VM, VM], out_specs=[VM, VM, VM],
        out_shape=[jax.ShapeDtypeStruct((8, 8, ADA_B), F32), jax.ShapeDtypeStruct((64, D), F32),
                   jax.ShapeDtypeStruct((8, 32, 64), F32)],
        scratch=[pltpu.VMEM((8, 8, D), F32), pltpu.VMEM((8, 8, ADA_B), F32),
                 pltpu.SemaphoreType.DMA((21,)), pltpu.SemaphoreType.DMA((21,))],
        args=[c_pad, w_ada, b_cols, cw_pad])


def _ada_bwd(dada, jobs=()):
    def core(ins, outs, scs):
        (d_ref,) = ins
        dd_ref, gb_ref = outs
        rbuf, ssem, rsem = scs
        me = _me()
        mi = _lin(me)
        peers = [_flip(me, f) for f in FLIPS]
        rbuf[mi] = d_ref[mi]
        first = []
        for k, p in enumerate(peers):
            first.append(_remote(d_ref.at[_lin(p)], rbuf.at[mi], ssem.at[k], rsem.at[k], p))
        for cp in first:
            cp.start()
        for k, p in enumerate(peers):
            _remote(d_ref.at[mi], rbuf.at[_lin(p)], ssem.at[k], rsem.at[k], p).wait_recv()
        dd = rbuf[...].reshape(64, ADA_B)
        dd_ref[...] = dd
        gb_ref[mi] = jnp.broadcast_to(_colsum(dd), (8, ADA_B))
        second = []
        for k, p in enumerate(peers):
            second.append(_remote(gb_ref.at[mi], gb_ref.at[mi], ssem.at[7 + k], rsem.at[7 + k], p))
        for cp in second:
            cp.start()
        for k, p in enumerate(peers):
            pi = _lin(p)
            _remote(gb_ref.at[pi], gb_ref.at[pi], ssem.at[7 + k], rsem.at[7 + k], p).wait_recv()
        for cp in first + second:
            cp.wait_send()

    return _call(
        core, name="ada_bwd", grid=(), jobs=jobs, in_specs=[VM], out_specs=[VM, VM],
        out_shape=[jax.ShapeDtypeStruct((64, ADA_B), F32), jax.ShapeDtypeStruct((8, 8, ADA_B), F32)],
        scratch=[pltpu.VMEM((8, 8, ADA_B), F32), pltpu.SemaphoreType.DMA((14,)), pltpu.SemaphoreType.DMA((14,))],
        args=[dada])


SMALL_D = ("g_pre_f1", "g_post_f1", "g_pre_m", "g_post_m", "g_pre_f2", "g_post_f2")
SMALL_W = ("gmlp_norm_g", "gmlp_norm_b", "conv_b", "conv_norm_g", "conv_norm_b", "g_out_a", "g_out_b")


def kernel(x, c, w_ada, b_ada, g_pre_f1, g_post_f1, w_f1_in, w_f1_out, g_pre_m, g_post_m, w_mix_in, gmlp_norm_g, gmlp_norm_b, w_spatial, b_spatial, conv_w, conv_b, conv_norm_g, conv_norm_b, g_out_a, g_out_b, w_mix_out, g_pre_f2, g_post_f2, w_f2_in, w_f2_out, loss_target, m_w_ada, m_b_ada, m_g_pre_f1, m_g_post_f1, m_w_f1_in, m_w_f1_out, m_g_pre_m, m_g_post_m, m_w_mix_in, m_gmlp_norm_g, m_gmlp_norm_b, m_w_spatial, m_b_spatial, m_conv_w, m_conv_b, m_conv_norm_g, m_conv_norm_b, m_g_out_a, m_g_out_b, m_w_mix_out, m_g_pre_f2, m_g_post_f2, m_w_f2_in, m_w_f2_out, v_w_ada, v_b_ada, v_g_pre_f1, v_g_post_f1, v_w_f1_in, v_w_f1_out, v_g_pre_m, v_g_post_m, v_w_mix_in, v_gmlp_norm_g, v_gmlp_norm_b, v_w_spatial, v_b_spatial, v_conv_w, v_conv_b, v_conv_norm_g, v_conv_norm_b, v_g_out_a, v_g_out_b, v_w_mix_out, v_g_pre_f2, v_g_post_f2, v_w_f2_in, v_w_f2_out):
    given = dict(locals())
    bl, seq, _ = x.shape
    T = bl * seq
    tm = min(256, seq // 2)
    mi = _lin((lax.axis_index("x"), lax.axis_index("y"), lax.axis_index("c")))

    def shard_in(w):
        return jnp.pad(w[0].T.astype(BF16), ((0, FBP - FB), (0, 0)))

    zpad = jnp.zeros((max(FBP - FB, 16), D), BF16)
    g_f1 = _Gather([shard_in(w_f1_in), w_f1_out[0].astype(BF16)], ("rows", "out"), zpad)
    g_mx = _Gather([w_mix_in[0].astype(BF16), w_mix_out[0].astype(BF16), w_f2_out[0].astype(BF16)],
                   ("rows", "rows", "out"), zpad)
    g_f2 = _Gather([shard_in(w_f2_in)], ("rows",), zpad, late_mid=True)

    c_pad = jnp.pad(c, ((0, 8 - bl), (0, 0)))
    b_cols = lax.dynamic_slice(b_ada, (0, mi * ADA_B), (1, ADA_B))
    cw_pad = jnp.pad(conv_w[0], ((0, 1), (0, 0)))
    (ada_blk, sc_all, cw_all), ((wi1, wo1),) = _ada_fwd(c_pad, w_ada[0], b_cols, cw_pad, jobs=[g_f1])
    ada = ada_blk[:, 0:bl, :].transpose(1, 0, 2).reshape(bl, 9, D)
    pad5 = jnp.zeros((bl, 5, D), F32)
    mod1 = jnp.concatenate([ada[:, 0:3], pad5], axis=1)
    mod2 = jnp.concatenate([ada[:, 3:6], pad5], axis=1)
    mod3 = jnp.concatenate([ada[:, 6:9], pad5], axis=1)
    cw_full = cw_all.transpose(1, 0, 2).reshape(32, WA)

    zrow = jnp.zeros((1, D), F32)
    gv1 = jnp.concatenate([g_pre_f1, g_post_f1] + [zrow] * 6, axis=0)
    gvm = jnp.concatenate([g_pre_m, g_post_m] + [zrow] * 6, axis=0)
    gv2 = jnp.concatenate([g_pre_f2, g_post_f2] + [zrow] * 6, axis=0)
    v512 = jnp.concatenate([gmlp_norm_g, gmlp_norm_b, conv_b, conv_norm_g, conv_norm_b, g_out_a, g_out_b,
                            jnp.zeros((1, WA), F32)], axis=0)
    ws = w_spatial[0]
    bias_full = jnp.repeat(b_spatial[0].T, HD, axis=1)
    esel = (lax.broadcasted_iota(jnp.int32, (8, WA), 1) // HD == lax.broadcasted_iota(jnp.int32, (8, WA), 0)).astype(F32)

    x0 = x.reshape(T, D)
    (x1, gu1, y1), ((wmi, wmo, wo2),) = _ffn_fwd(x0, mod1, gv1, wi1, wo1, tm, "ffn1_fwd", jobs=[g_mx])
    wmo = wmo.reshape(D, D)
    (x2, proj, ym, conv), ((wi2,),) = _mixer_fwd(x1, mod2, gvm, wmi, wmo, v512, ws, bias_full, cw_full, tm, "mixer_fwd", jobs=[g_f2])
    (dx3, gu2, y2, loss_blk), _ = _ffn_fwd(x2, mod3, gv2, wi2, wo2, tm, "ffn2_fwd", target=loss_target.reshape(T, D))

    (dx2, dg2, act2, hb2, dyb2, mg3, vg3), _ = _ffn_bwd(dx3, x2, y2, gu2, mod3, gv2, wi2, wo2, tm, "ffn2_bwd")
    (g_wi2,), _ = _grad_w_in(dg2, hb2, "ffn2_gw_in")
    (g_wo2,), _ = _grad_w_out(act2, dyb2, "ffn2_gw_out")
    (dpart, dymb, ycat, mg2a, vgma, v5g, gws, gbs), ((p_wi2,),) = _mixer_bwd_a(
        dx2, ym, proj, conv, mod2, gvm, wmo, v512, ws, bias_full, esel, tm, "mixer_bwd_a",
        jobs=[_ChipScatter([g_wi2])])
    (dx1, dproj, hbm, mg2b, vgmb, dcw), ((p_wo2,),) = _mixer_bwd_b(
        dx2, x1, dpart, proj, mod2, gvm, wmi, cw_full, tm, "mixer_bwd_b", jobs=[_ChipScatter([g_wo2])])
    (g_wmi,), _ = _grad_w_mi(hbm, dproj, "mixer_gw_in")
    (g_wmo,), _ = _grad_w_mo(ycat, dymb, "mixer_gw_out")
    p2 = jnp.concatenate([v5g, dcw], axis=0)
    (dx0, dg1, act1, hb1, dyb1, mg1, vg1), _ = _ffn_bwd(dx1, x0, y1, gu1, mod1, gv1, wi1, wo1, tm, "ffn1_bwd")
    (g_wo1,), ((p_wmi, p_wmo),) = _grad_w_out(act1, dyb1, "ffn1_gw_out", jobs=[_ChipScatter([g_wmi, g_wmo])])
    (g_wi1,), ((a2, a3, a4), (p_wo1,)) = _grad_w_in(
        dg1, hb1, "ffn1_gw_in", jobs=[_Gather([p2, gws, gbs], ("rows",) * 3, zpad), _ChipScatter([g_wo1])])

    ssem, rsem, g_thru, land_thru, token = _chip_scatter_start(g_wi1)

    dada = jnp.concatenate([mg1[:, 0:3], mg2b[:, 0:2], mg2a[:, 2:3], mg3[:, 0:3]], axis=1)
    dada = dada.reshape(bl, NDEV, ADA_B).transpose(1, 0, 2)
    dada = jnp.pad(dada, ((0, 0), (0, 8 - bl), (0, 0))) + token[0, 0]
    p1 = jnp.concatenate([vg1[0:2], vgmb[0:1], vgma[1:2], vg3[0:2], loss_blk[0:1], zrow], axis=0)
    (dd_all, gb_all), ((a1,),) = _ada_bwd(dada, jobs=[_AllGather([p1])])
    g_bada = gb_all[:, 0, :].reshape(1, 9 * D)

    res = {}
    quad = _adamw_ada(sc_all, dd_all, w_ada[0], m_w_ada[0], v_w_ada[0], 256, "adamw_w_ada")
    res["w_ada"] = tuple(t[None] for t in quad)
    quad = _adamw_reduce(p_wi2, w_f2_in[0].T, m_w_f2_in[0].T, v_w_f2_in[0].T, FO, "adamw_w_f2_in", after=quad[1])
    res["w_f2_in"] = tuple(t.T[None] for t in quad)
    for nm, part, tr in (("w_f2_out", p_wo2, FO), ("w_mix_in", p_wmi, 256), ("w_mix_out", p_wmo, MO), ("w_f1_out", p_wo1, FO)):
        quad = _adamw_reduce(part, given[nm][0], given["m_" + nm][0], given["v_" + nm][0], tr, "adamw_" + nm, after=quad[1])
        res[nm] = tuple(t[None] for t in quad)
    g_wi1, p_wi1 = _chip_scatter_wait(ssem, rsem, g_thru, land_thru, quad[1])
    quad = _adamw_reduce(p_wi1, w_f1_in[0].T, m_w_f1_in[0].T, v_w_f1_in[0].T, FO, "adamw_w_f1_in", own=g_wi1)
    res["w_f1_in"] = tuple(t.T[None] for t in quad)

    small = SMALL_D + SMALL_W + ("w_spatial", "b_spatial", "b_ada")
    grads = [(0, r) for r in range(6)] + [(1, r) for r in range(7)] + [(2, None), (3, None), (4, None)]
    wmv = []
    for nm in small:
        for pre in ("", "m_", "v_"):
            wmv.append(given[pre + nm][0] if nm in ("w_spatial", "b_spatial") else given[pre + nm])
    outs = _adamw_small([a1, a2, a3, a4], [g_bada], grads, wmv, (0, 1), "adamw_small")
    loss = outs[0][6, 0]
    for t, nm in enumerate(small):
        quad = outs[2 + 4 * t:6 + 4 * t]
        res[nm] = tuple(q[None] for q in quad) if nm in ("w_spatial", "b_spatial") else tuple(quad)
    g_cw = lax.dynamic_slice(outs[1], (8, mi * 64), (32, 64))
    wmv = [jnp.pad(given[pre + "conv_w"][0], ((0, 1), (0, 0)), constant_values=1.0 if pre == "v_" else 0.0)
           for pre in ("", "m_", "v_")]
    quad = _adamw_small([], [g_cw], [(0, None)], wmv, (), "adamw_conv_w")
    res["conv_w"] = tuple(q[0:CONV_K][None] for q in quad)

    order = ["w_ada", "b_ada", "g_pre_f1", "g_post_f1", "w_f1_in", "w_f1_out", "g_pre_m", "g_post_m", "w_mix_in",
             "gmlp_norm_g", "gmlp_norm_b", "w_spatial", "b_spatial", "conv_w", "conv_b", "conv_norm_g", "conv_norm_b",
             "g_out_a", "g_out_b", "w_mix_out", "g_pre_f2", "g_post_f2", "w_f2_in", "w_f2_out"]
    out = [loss, dx0.reshape(bl, seq, D)]
    for k in range(4):
        out += [res[nm][k] for nm in order]
    return tuple(out)
```

```python
import jax
import jax.numpy as jnp
from jax import lax
from jax.experimental import pallas as pl
from jax.experimental.pallas import tpu as pltpu

F32 = jnp.float32
BF16 = jnp.bfloat16

D = 1024
DFF = 2816
NDEV = 8
FB = 2 * DFF // NDEV
FBP = 704
FO = DFF // NDEV
WA = 512
NHEAD = 8
HD = 64
CHUNK = 128
CONV_K = 31
HALO = 32
MB = 2 * (WA + WA) // NDEV
MO = D // NDEV
ADA_B = 9 * D // NDEV
EPS = 1e-6
HALF = 0.5

ADAM_LR = 0.001
ADAM_B1 = 0.9
ADAM_B2 = 0.999
ADAM_EPS = 1e-08
ADAM_WD = 0.01
ADAM_STEP = 10

VMEM_LIMIT = 56 * 1024 * 1024
MESH = pl.DeviceIdType.MESH
FLIPS = ((0, 0, 1), (1, 0, 0), (0, 1, 0), (1, 1, 0), (1, 0, 1), (0, 1, 1), (1, 1, 1))
CHIP_FLIPS = ((1, 0, 0), (0, 1, 0), (1, 1, 0))
HBM = pl.BlockSpec(memory_space=pl.ANY)
VM = pl.BlockSpec(memory_space=pltpu.VMEM)


def _dot(a, b):
    return lax.dot_general(a, b, (((1,), (0,)), ((), ())), preferred_element_type=F32)


def _dot_nt(a, b):
    return lax.dot_general(a, b, (((1,), (1,)), ((), ())), preferred_element_type=F32)


def _dot_tn(a, b):
    return lax.dot_general(a, b, (((0,), (0,)), ((), ())), preferred_element_type=F32)


def _rowmean(v):
    return jnp.mean(v, axis=-1, keepdims=True)


def _colsum(v):
    return jnp.sum(v, axis=0, keepdims=True)


def _sigmoid(v):
    return 0.5 * jnp.tanh(0.5 * v) + 0.5


def _const_spec(shape):
    nd = len(shape)
    return pl.BlockSpec(shape, lambda *_: (0,) * nd, pipeline_mode=pl.Buffered(1))


def _me():
    return lax.axis_index("x"), lax.axis_index("y"), lax.axis_index("c")


def _flip(me, f):
    return tuple(1 - v if b else v for v, b in zip(me, f))


def _lin(p):
    return 4 * p[0] + 2 * p[1] + p[2]


def _remote(src, dst, send_sem, recv_sem, dev):
    return pltpu.make_async_remote_copy(src_ref=src, dst_ref=dst, send_sem=send_sem, recv_sem=recv_sem,
                                        device_id=dev, device_id_type=MESH)


def _blk(kind, ref, p):
    if kind == "out":
        return ref.at[2 * p[0] + p[1], pl.ds(p[2] * FO, FO), :]
    return ref.at[_lin(p)]


class _Gather:
    def __init__(self, shards, kinds, zpad, late_mid=False):
        self.late_mid = late_mid
        self.kinds = kinds
        self.n = len(shards)
        self.ins = list(shards) + [zpad]
        self.out_shape = [jax.ShapeDtypeStruct((4, FBP, D) if k == "out" else (NDEV,) + s.shape, s.dtype)
                          for s, k in zip(shards, kinds)]
        self.n_out = sum(k == "out" for k in kinds)
        self.sems = [pltpu.SemaphoreType.DMA((7 * self.n,)), pltpu.SemaphoreType.DMA((7 * self.n,)),
                     pltpu.SemaphoreType.DMA((self.n + 4 * max(self.n_out, 1),))]

    def _first(self, ins, outs, sems):
        ssem, rsem, lsem = sems
        me = _me()
        sib = _flip(me, (0, 0, 1))
        cps, loc = [], []
        nz = 0
        for a in range(self.n):
            mine = _blk(self.kinds[a], outs[a], me)
            loc.append(pltpu.make_async_copy(ins[a], mine, lsem.at[a]))
            if self.kinds[a] == "out" and FBP > FB:
                for q in range(4):
                    loc.append(pltpu.make_async_copy(ins[self.n], outs[a].at[q, pl.ds(FB, FBP - FB), :],
                                                     lsem.at[self.n + 4 * nz + q]))
                nz += 1
            cps.append(_remote(ins[a], mine, ssem.at[7 * a], rsem.at[7 * a], sib))
            for j, f in enumerate(CHIP_FLIPS):
                cps.append(_remote(ins[a], mine, ssem.at[7 * a + 1 + j], rsem.at[7 * a + 1 + j], _flip(me, f)))
        return cps, loc

    def _passed(self, outs, sems):
        ssem, rsem, _ = sems
        me = _me()
        sib = _flip(me, (0, 0, 1))
        cps = []
        for j, f in enumerate(CHIP_FLIPS):
            for a in range(self.n):
                blk = _blk(self.kinds[a], outs[a], _flip(me, f))
                cps.append(_remote(blk, blk, ssem.at[7 * a + 4 + j], rsem.at[7 * a + 4 + j], sib))
        return cps

    def start(self, ins, outs, sems):
        cps, loc = self._first(ins, outs, sems)
        for cp in loc + cps:
            cp.start()

    def mid(self, ins, outs, sems):
        ssem, rsem, _ = sems
        me = _me()
        passed = self._passed(outs, sems)
        t = 0
        for j, f in enumerate(CHIP_FLIPS):
            for a in range(self.n):
                blk = _blk(self.kinds[a], outs[a], _flip(me, f))
                _remote(blk, blk, ssem.at[7 * a + 1 + j], rsem.at[7 * a + 1 + j], _flip(me, f)).wait_recv()
                passed[t].start()
                t += 1

    def end(self, ins, outs, sems):
        ssem, rsem, _ = sems
        me = _me()
        sib = _flip(me, (0, 0, 1))
        for a in range(self.n):
            blk = _blk(self.kinds[a], outs[a], sib)
            _remote(blk, blk, ssem.at[7 * a], rsem.at[7 * a], sib).wait_recv()
            for j, f in enumerate(CHIP_FLIPS):
                blk = _blk(self.kinds[a], outs[a], _flip(_flip(me, f), (0, 0, 1)))
                _remote(blk, blk, ssem.at[7 * a + 4 + j], rsem.at[7 * a + 4 + j], sib).wait_recv()
        cps, loc = self._first(ins, outs, sems)
        for cp in cps + self._passed(outs, sems):
            cp.wait_send()
        for cp in loc:
            cp.wait()


class _ChipScatter:
    def __init__(self, grads):
        self.n = len(grads)
        self.ins = list(grads)
        self.out_shape = [jax.ShapeDtypeStruct(g.shape, BF16) for g in grads]
        self.sems = [pltpu.SemaphoreType.DMA((3 * self.n,)), pltpu.SemaphoreType.DMA((3 * self.n,)),
                     pltpu.SemaphoreType.DMA((self.n,))]

    def _copies(self, ins, outs, sems):
        ssem, rsem, lsem = sems
        me = _me()
        mq = 2 * me[0] + me[1]
        loc = [pltpu.make_async_copy(ins[a].at[mq], outs[a].at[mq], lsem.at[a]) for a in range(self.n)]
        cps = []
        for k, f in enumerate(CHIP_FLIPS):
            p = _flip(me, f)
            for a in range(self.n):
                cps.append(_remote(ins[a].at[2 * p[0] + p[1]], outs[a].at[mq], ssem.at[3 * a + k], rsem.at[3 * a + k], p))
        return cps, loc

    def start(self, ins, outs, sems):
        cps, loc = self._copies(ins, outs, sems)
        for cp in loc + cps:
            cp.start()

    mid = None

    def end(self, ins, outs, sems):
        ssem, rsem, _ = sems
        me = _me()
        mq = 2 * me[0] + me[1]
        for k, f in enumerate(CHIP_FLIPS):
            p = _flip(me, f)
            for a in range(self.n):
                _remote(ins[a].at[mq], outs[a].at[2 * p[0] + p[1]], ssem.at[3 * a + k], rsem.at[3 * a + k], p).wait_recv()
        cps, loc = self._copies(ins, outs, sems)
        for cp in cps:
            cp.wait_send()
        for cp in loc:
            cp.wait()


class _AllGather:
    def __init__(self, parts):
        self.n = len(parts)
        self.ins = list(parts)
        self.out_shape = [jax.ShapeDtypeStruct((NDEV,) + p.shape, p.dtype) for p in parts]
        self.sems = [pltpu.SemaphoreType.DMA((7 * self.n,)), pltpu.SemaphoreType.DMA((7 * self.n,)),
                     pltpu.SemaphoreType.DMA((self.n,))]

    def _copies(self, ins, outs, sems):
        ssem, rsem, lsem = sems
        me = _me()
        mi = _lin(me)
        loc = [pltpu.make_async_copy(ins[a], outs[a].at[mi], lsem.at[a]) for a in range(self.n)]
        cps = []
        for k, f in enumerate(FLIPS):
            for a in range(self.n):
                cps.append(_remote(ins[a], outs[a].at[mi], ssem.at[7 * a + k], rsem.at[7 * a + k], _flip(me, f)))
        return cps, loc

    def start(self, ins, outs, sems):
        cps, loc = self._copies(ins, outs, sems)
        for cp in loc + cps:
            cp.start()

    mid = None

    def end(self, ins, outs, sems):
        ssem, rsem, _ = sems
        me = _me()
        for k, f in enumerate(FLIPS):
            p = _flip(me, f)
            for a in range(self.n):
                _remote(ins[a], outs[a].at[_lin(p)], ssem.at[7 * a + k], rsem.at[7 * a + k], p).wait_recv()
        cps, loc = self._copies(ins, outs, sems)
        for cp in cps:
            cp.wait_send()
        for cp in loc:
            cp.wait()


def _call(core, *, name, grid, in_specs, out_specs, out_shape, args, scratch=(), jobs=()):
    n_in, n_out, n_sc = len(in_specs), len(out_specs), len(scratch)
    steps = 1
    for g in grid:
        steps *= g

    def body(*refs):
        pos = [0]

        def take(k):
            r = refs[pos[0]:pos[0] + k]
            pos[0] += k
            return r

        ins = take(n_in)
        j_ins = [take(len(j.ins)) for j in jobs]
        outs = take(n_out)
        j_outs = [take(len(j.out_shape)) for j in jobs]
        scs = take(n_sc)
        j_sems = [take(len(j.sems)) for j in jobs]
        if len(grid) == 2:
            step = pl.program_id(0) * grid[1] + pl.program_id(1)
        elif len(grid) == 1:
            step = pl.program_id(0)
        else:
            step = 0
        for j, ji, jo, js in zip(jobs, j_ins, j_outs, j_sems):
            if grid:
                pl.when(step == 0)(lambda j=j, ji=ji, jo=jo, js=js: j.start(ji, jo, js))
            else:
                j.start(ji, jo, js)
        for j, ji, jo, js in zip(jobs, j_ins, j_outs, j_sems):
            if j.mid is not None and grid:
                at = steps - 1 if j.late_mid else (3 * steps) // 4
                pl.when(step == at)(lambda j=j, ji=ji, jo=jo, js=js: j.mid(ji, jo, js))
        if core is not None:
            core(ins, outs, scs)
        for j, ji, jo, js in zip(jobs, j_ins, j_outs, j_sems):
            if grid:
                pl.when(step == steps - 1)(lambda j=j, ji=ji, jo=jo, js=js: j.end(ji, jo, js))
            else:
                if j.mid is not None:
                    j.mid(ji, jo, js)
                j.end(ji, jo, js)

    all_in = list(in_specs)
    all_args = list(args)
    all_out = list(out_specs)
    all_shape = list(out_shape)
    all_sc = list(scratch)
    for j in jobs:
        all_in += [HBM] * len(j.ins)
        all_args += j.ins
    for j in jobs:
        all_out += [HBM] * len(j.out_shape)
        all_shape += j.out_shape
        all_sc += j.sems
    params = dict(vmem_limit_bytes=VMEM_LIMIT)
    if grid:
        params["dimension_semantics"] = ("arbitrary",) * len(grid)
    res = pl.pallas_call(
        body, name=name, grid=grid, in_specs=all_in, out_specs=all_out, out_shape=all_shape,
        scratch_shapes=all_sc, compiler_params=pltpu.CompilerParams(**params),
    )(*all_args)
    core_res = list(res[:n_out])
    job_res = []
    pos = n_out
    for j in jobs:
        job_res.append(list(res[pos:pos + len(j.out_shape)]))
        pos += len(j.out_shape)
    return core_res, job_res


def _ffn_fwd(x, mod, gvec, w_in, w_out, tm, name, jobs=(), target=None):
    T = x.shape[0]
    nt = T // tm
    tps = nt // mod.shape[0]

    def core(ins, outs, _):
        x_ref, mod_ref, g_ref, win_ref, wout_ref = ins[:5]
        xo_ref, gu_ref, y_ref = outs[:3]
        xv = x_ref[...]
        sh, sc, gt = mod_ref[0:1, :], mod_ref[1:2, :], mod_ref[2:3, :]
        r = lax.rsqrt(_rowmean(xv * xv) + EPS)
        h = (xv * r * g_ref[0:1, :]) * (1.0 + sc) + sh
        hb = h.astype(BF16)
        y = jnp.zeros((tm, D), F32)
        for cidx in range(4):
            gate = _dot_nt(hb, win_ref[cidx])
            up = _dot_nt(hb, win_ref[4 + cidx])
            gu_ref[cidx] = gate.astype(BF16)
            gu_ref[4 + cidx] = up.astype(BF16)
            act = gate * _sigmoid(gate) * up
            y = y + _dot(act.astype(BF16), wout_ref[cidx])
        y_ref[...] = y
        ry = lax.rsqrt(_rowmean(y * y) + EPS)
        xo = xv + (HALF * gt) * (y * ry * g_ref[1:2, :])
        if target is None:
            xo_ref[...] = xo
        else:
            loss_ref = outs[3]

            @pl.when(pl.program_id(0) == 0)
            def _():
                loss_ref[...] = jnp.zeros((8, D), F32)

            err = xo - ins[5][...]
            xo_ref[...] = err * (1.0 / D)
            loss_ref[...] += HALF * jnp.sum(_rowmean(err * err), axis=0, keepdims=True)

    tile = pl.BlockSpec((tm, D), lambda i: (i, 0))
    extra = target is not None
    return _call(
        core, name=name, grid=(nt,), jobs=jobs,
        in_specs=[tile, pl.BlockSpec((None, 8, D), lambda i: (i // tps, 0, 0)), _const_spec((8, D)),
                  _const_spec((8, FBP, D)), _const_spec((4, FBP, D))] + [tile] * extra,
        out_specs=[tile, pl.BlockSpec((8, tm, FBP), lambda i: (0, i, 0)), tile]
        + [pl.BlockSpec((8, D), lambda i: (0, 0))] * extra,
        out_shape=[jax.ShapeDtypeStruct((T, D), F32), jax.ShapeDtypeStruct((8, T, FBP), BF16),
                   jax.ShapeDtypeStruct((T, D), F32)] + [jax.ShapeDtypeStruct((8, D), F32)] * extra,
        args=[x, mod, gvec, w_in, w_out] + [target] * extra)


def _ffn_bwd(dxo, x, y, gu, mod, gvec, w_in, w_out, tm, name, jobs=()):
    T = x.shape[0]
    nt = T // tm
    nb = mod.shape[0]
    tps = nt // nb

    def core(ins, outs, _):
        dxo_ref, x_ref, y_ref, gu_ref, mod_ref, g_ref, win_ref, wout_ref = ins
        dx_ref, dg_ref, act_ref, hb_ref, dyb_ref, mg_ref, vg_ref = outs
        i = pl.program_id(0)
        xv = x_ref[...]
        dxo_v = dxo_ref[...]
        yv = y_ref[...]
        sh, sc, gt = mod_ref[0:1, :], mod_ref[1:2, :], mod_ref[2:3, :]
        gpre, gpost = g_ref[0:1, :], g_ref[1:2, :]
        r = lax.rsqrt(_rowmean(xv * xv) + EPS)
        xh = xv * r
        n = xh * gpre
        hb = (n * (1.0 + sc) + sh).astype(BF16)
        hb_ref[...] = hb
        ry = lax.rsqrt(_rowmean(yv * yv) + EPS)
        yh = yv * ry
        d_gt = _colsum(HALF * dxo_v * (yh * gpost))
        dp = (HALF * gt) * dxo_v
        d_gpost = _colsum(dp * yh)
        dyh = dp * gpost
        dy = ry * (dyh - yh * _rowmean(dyh * yh))
        dyb = dy.astype(BF16)
        dyb_ref[...] = dyb
        dh = jnp.zeros((tm, D), F32)
        for cidx in range(4):
            gate = gu_ref[cidx].astype(F32)
            up = gu_ref[4 + cidx].astype(F32)
            sig = _sigmoid(gate)
            s = gate * sig
            act_ref[cidx] = (s * up).astype(BF16)
            d_act = _dot_nt(dyb, wout_ref[cidx])
            d_up = (d_act * s).astype(BF16)
            d_gate = (d_act * up * (sig * (1.0 + gate * (1.0 - sig)))).astype(BF16)
            dg_ref[cidx] = d_gate
            dg_ref[4 + cidx] = d_up
            dh = dh + _dot(d_gate, win_ref[cidx]) + _dot(d_up, win_ref[4 + cidx])
        d_sc = _colsum(dh * n)
        d_sh = _colsum(dh)
        dn = dh * (1.0 + sc)
        d_gpre = _colsum(dn * xh)
        dxh = dn * gpre
        dx_ref[...] = dxo_v + r * (dxh - xh * _rowmean(dxh * xh))

        @pl.when(i % tps == 0)
        def _():
            mg_ref[...] = jnp.zeros((8, D), F32)

        @pl.when(i == 0)
        def _():
            vg_ref[...] = jnp.zeros((8, D), F32)

        mg_ref[0:1, :] += d_sh
        mg_ref[1:2, :] += d_sc
        mg_ref[2:3, :] += d_gt
        vg_ref[0:1, :] += d_gpre
        vg_ref[1:2, :] += d_gpost

    tile = pl.BlockSpec((tm, D), lambda i: (i, 0))
    return _call(
        core, name=name, grid=(nt,), jobs=jobs,
        in_specs=[tile, tile, tile, pl.BlockSpec((8, tm, FBP), lambda i: (0, i, 0)),
                  pl.BlockSpec((None, 8, D), lambda i: (i // tps, 0, 0)), _const_spec((8, D)),
                  _const_spec((8, FBP, D)), _const_spec((4, FBP, D))],
        out_specs=[tile, pl.BlockSpec((8, tm, FBP), lambda i: (0, i, 0)),
                   pl.BlockSpec((4, tm, FBP), lambda i: (0, i, 0)), tile, tile,
                   pl.BlockSpec((None, 8, D), lambda i: (i // tps, 0, 0)), pl.BlockSpec((8, D), lambda i: (0, 0))],
        out_shape=[jax.ShapeDtypeStruct((T, D), F32), jax.ShapeDtypeStruct((8, T, FBP), BF16),
                   jax.ShapeDtypeStruct((4, T, FBP), BF16), jax.ShapeDtypeStruct((T, D), BF16),
                   jax.ShapeDtypeStruct((T, D), BF16), jax.ShapeDtypeStruct((nb, 8, D), F32),
                   jax.ShapeDtypeStruct((8, D), F32)],
        args=[dxo, x, y, gu, mod, gvec, w_in, w_out])


def _masked_spatial(ws_ref):
    row = lax.broadcasted_iota(jnp.int32, (CHUNK, CHUNK), 0)
    col = lax.broadcasted_iota(jnp.int32, (CHUNK, CHUNK), 1)
    keep = col <= row
    return [jnp.where(keep, ws_ref[hd], 0.0).astype(BF16) for hd in range(NHEAD)]


def _spatial_gate(wm, vb_chunk, lane_head):
    z = jnp.zeros((CHUNK, WA), F32)
    for hd in range(NHEAD):
        z = jnp.where(lane_head == hd, _dot(wm[hd], vb_chunk), z)
    return z


def _layer_norm_stats(v):
    mu = _rowmean(v)
    vc = v - mu
    rstd = lax.rsqrt(_rowmean(vc * vc) + EPS)
    return vc * rstd, rstd


def _pitch(tm):
    p = tm // 8
    while p % 8 != 4:
        p += 1
    return p


def _lanes(s):
    return slice(s * 128, (s + 1) * 128)


def _to_slabs(ref, row0, val):
    for s in range(4):
        ref[s, row0:row0 + val.shape[0], :] = val[:, _lanes(s)]


def _tap_sum(src, out, cw_ref, bias, tm, start):
    p = _pitch(tm)
    for s in range(4):
        accs = [jnp.broadcast_to(bias[:, _lanes(s)], (8, 128))] * p
        for k in range(CONV_K):
            w = jnp.broadcast_to(cw_ref[k:k + 1, _lanes(s)], (8, 128))
            for v in range(p):
                accs[v] = accs[v] + w * src[s, pl.ds(v + start(k), 8, stride=p), :]
        for v in range(p):
            out[s, pl.ds(v, 8, stride=p), :] = accs[v]
    return jnp.concatenate([out[s, 0:tm, :] for s in range(4)], axis=1)


def _mixer_fwd(x, mod, gvec, w_mi, w_mo, v512, ws, bias_full, cw, tm, name, jobs=()):
    T = x.shape[0]
    nt = T // tm
    tps = nt // mod.shape[0]
    ext_rows = 8 * _pitch(tm)

    def core(ins, outs, scs):
        x_ref, mod_ref, g_ref, wmi_ref, wmo_ref, v_ref, ws_ref, bias_ref, cw_ref = ins
        xo_ref, proj_ref, ym_ref, conv_ref = outs
        glu_ext, conv_scr = scs
        i = pl.program_id(0)
        xv = x_ref[...]
        sh, sc, gt = mod_ref[0:1, :], mod_ref[1:2, :], mod_ref[2:3, :]
        r = lax.rsqrt(_rowmean(xv * xv) + EPS)
        hb = ((xv * r * g_ref[0:1, :]) * (1.0 + sc) + sh).astype(BF16)
        for j in range(NDEV):
            proj_ref[:, j * MB:(j + 1) * MB] = _dot(hb, wmi_ref[j])
        u = proj_ref[:, 0:WA]
        v0 = proj_ref[:, WA:2 * WA]
        a = proj_ref[:, 2 * WA:3 * WA]
        g = proj_ref[:, 3 * WA:4 * WA]
        vh, _ = _layer_norm_stats(v0)
        vb = (vh * v_ref[0:1, :] + v_ref[1:2, :]).astype(BF16)
        wm = _masked_spatial(ws_ref)
        lane_head = lax.broadcasted_iota(jnp.int32, (CHUNK, WA), 1) >> 6
        ya = []
        for q in range(tm // CHUNK):
            z = _spatial_gate(wm, vb[q * CHUNK:(q + 1) * CHUNK, :], lane_head) + bias_ref[...]
            ya.append(u[q * CHUNK:(q + 1) * CHUNK, :] * z)
        ya = jnp.concatenate(ya, axis=0)
        glu = a * _sigmoid(g)

        @pl.when(i == 0)
        def _():
            glu_ext[:, HALO + tm:HALO + ext_rows, :] = jnp.zeros((4, ext_rows - tm, 128), F32)

        @pl.when(i % tps == 0)
        def _():
            glu_ext[:, 0:HALO, :] = jnp.zeros((4, HALO, 128), F32)

        _to_slabs(glu_ext, HALO, glu)
        conv = _tap_sum(glu_ext, conv_scr, cw_ref, v_ref[2:3, :], tm, lambda k: HALO - (CONV_K - 1) + k)
        conv_ref[...] = conv
        glu_ext[:, 0:HALO, :] = glu_ext[:, tm:tm + HALO, :]
        ch, _ = _layer_norm_stats(conv)
        cn = ch * v_ref[3:4, :] + v_ref[4:5, :]
        yb = cn * _sigmoid(cn)
        pa = ya * lax.rsqrt(_rowmean(ya * ya) + EPS) * v_ref[5:6, :]
        pb = yb * lax.rsqrt(_rowmean(yb * yb) + EPS) * v_ref[6:7, :]
        ycat = jnp.concatenate([pa, pb], axis=1).astype(BF16)
        ym = _dot(ycat, wmo_ref[...])
        ym_ref[...] = ym
        rm = lax.rsqrt(_rowmean(ym * ym) + EPS)
        xo_ref[...] = xv + gt * (ym * rm * g_ref[1:2, :])

    tile = pl.BlockSpec((tm, D), lambda i: (i, 0))
    return _call(
        core, name=name, grid=(nt,), jobs=jobs,
        in_specs=[tile, pl.BlockSpec((None, 8, D), lambda i: (i // tps, 0, 0)), _const_spec((8, D)),
                  _const_spec((NDEV, D, MB)), _const_spec((D, D)), _const_spec((8, WA)),
                  _const_spec((NHEAD, CHUNK, CHUNK)), _const_spec((CHUNK, WA)), _const_spec((32, WA))],
        out_specs=[tile, pl.BlockSpec((tm, 4 * WA), lambda i: (i, 0)), tile, pl.BlockSpec((tm, WA), lambda i: (i, 0))],
        out_shape=[jax.ShapeDtypeStruct((T, D), F32), jax.ShapeDtypeStruct((T, 4 * WA), F32),
                   jax.ShapeDtypeStruct((T, D), F32), jax.ShapeDtypeStruct((T, WA), F32)],
        scratch=[pltpu.VMEM((4, HALO + ext_rows, 128), F32), pltpu.VMEM((4, ext_rows, 128), F32)],
        args=[x, mod, gvec, w_mi, w_mo, v512, ws, bias_full, cw])


def _mixer_bwd_a(dxo, ym, proj, conv, mod, gvec, w_mo, v512, ws, bias_full, esel, tm, name, jobs=()):
    T = dxo.shape[0]
    nt = T // tm
    nb = mod.shape[0]
    tps = nt // nb

    def core(ins, outs, scs):
        dxo_ref, ym_ref, proj_ref, conv_ref, mod_ref, g_ref, wmo_ref, v_ref, ws_ref, bias_ref, e_ref = ins
        dpart_ref, dymb_ref, ycat_ref, mg_ref, vg_ref, v5g_ref, gws_ref, gbs_ref = outs
        (dbs_acc,) = scs
        i = pl.program_id(0)
        dxo_v = dxo_ref[...]
        ymv = ym_ref[...]
        gt = mod_ref[2:3, :]
        gpost = g_ref[1:2, :]
        rm = lax.rsqrt(_rowmean(ymv * ymv) + EPS)
        ymh = ymv * rm
        d_gt = _colsum(dxo_v * (ymh * gpost))
        dpm = gt * dxo_v
        d_gpost = _colsum(dpm * ymh)
        dymh = dpm * gpost
        dym = (rm * (dymh - ymh * _rowmean(dymh * ymh))).astype(BF16)
        dymb_ref[...] = dym
        dycat = _dot_nt(dym, wmo_ref[...])
        u = proj_ref[:, 0:WA]
        v0 = proj_ref[:, WA:2 * WA]
        vh, rv = _layer_norm_stats(v0)
        vb = (vh * v_ref[0:1, :] + v_ref[1:2, :]).astype(BF16)
        wm = _masked_spatial(ws_ref)
        lane_head = lax.broadcasted_iota(jnp.int32, (CHUNK, WA), 1) >> 6
        zs = []
        for q in range(tm // CHUNK):
            zs.append(_spatial_gate(wm, vb[q * CHUNK:(q + 1) * CHUNK, :], lane_head) + bias_ref[...])
        z = jnp.concatenate(zs, axis=0)
        ya = u * z
        ra = lax.rsqrt(_rowmean(ya * ya) + EPS)
        yah = ya * ra
        ch, rc = _layer_norm_stats(conv_ref[...])
        cn = ch * v_ref[3:4, :] + v_ref[4:5, :]
        sg = _sigmoid(cn)
        yb = cn * sg
        rb = lax.rsqrt(_rowmean(yb * yb) + EPS)
        ybh = yb * rb
        ycat_ref[...] = jnp.concatenate([yah * v_ref[5:6, :], ybh * v_ref[6:7, :]], axis=1).astype(BF16)
        dpa = dycat[:, 0:WA]
        dpb = dycat[:, WA:2 * WA]
        d_goa = _colsum(dpa * yah)
        d_gob = _colsum(dpb * ybh)
        dyah = dpa * v_ref[5:6, :]
        dybh = dpb * v_ref[6:7, :]
        dya = ra * (dyah - yah * _rowmean(dyah * yah))
        dyb = rb * (dybh - ybh * _rowmean(dybh * ybh))
        dpart_ref[:, 0:WA] = dya * z
        dz = dya * u

        @pl.when(i == 0)
        def _():
            gws_ref[...] = jnp.zeros((NHEAD, CHUNK, CHUNK), F32)
            dbs_acc[...] = jnp.zeros((CHUNK, WA), F32)
            vg_ref[...] = jnp.zeros((8, D), F32)
            v5g_ref[...] = jnp.zeros((8, WA), F32)

        dvs = []
        for q in range(tm // CHUNK):
            dz_q = dz[q * CHUNK:(q + 1) * CHUNK, :]
            vb_q = vb[q * CHUNK:(q + 1) * CHUNK, :]
            dbs_acc[...] += dz_q
            dzb = dz_q.astype(BF16)
            dv_q = jnp.zeros((CHUNK, WA), F32)
            for hd in range(NHEAD):
                dv_q = jnp.where(lane_head == hd, _dot_tn(wm[hd], dzb), dv_q)
                dz_hd = jnp.where(lane_head == hd, dz_q, 0.0).astype(BF16)
                gws_ref[hd] += _dot_nt(dz_hd, vb_q)
            dvs.append(dv_q)
        dv = jnp.concatenate(dvs, axis=0)
        d_gng = _colsum(dv * vh)
        d_gnb = _colsum(dv)
        dvh = dv * v_ref[0:1, :]
        dpart_ref[:, WA:2 * WA] = rv * (dvh - _rowmean(dvh) - vh * _rowmean(dvh * vh))
        dcn = dyb * (sg * (1.0 + cn * (1.0 - sg)))
        d_cng = _colsum(dcn * ch)
        d_cnb = _colsum(dcn)
        dch = dcn * v_ref[3:4, :]
        dconv = rc * (dch - _rowmean(dch) - ch * _rowmean(dch * ch))
        dpart_ref[:, 2 * WA:3 * WA] = dconv
        dpart_ref[:, 3 * WA:4 * WA] = jnp.zeros((tm, WA), F32)
        d_cb = _colsum(dconv)

        @pl.when(i % tps == 0)
        def _():
            mg_ref[...] = jnp.zeros((8, D), F32)

        mg_ref[2:3, :] += d_gt
        vg_ref[1:2, :] += d_gpost
        v5g_ref[0:1, :] += d_gng
        v5g_ref[1:2, :] += d_gnb
        v5g_ref[2:3, :] += d_cb
        v5g_ref[3:4, :] += d_cng
        v5g_ref[4:5, :] += d_cnb
        v5g_ref[5:6, :] += d_goa
        v5g_ref[6:7, :] += d_gob

        @pl.when(i == nt - 1)
        def _():
            row = lax.broadcasted_iota(jnp.int32, (CHUNK, CHUNK), 0)
            col = lax.broadcasted_iota(jnp.int32, (CHUNK, CHUNK), 1)
            for hd in range(NHEAD):
                gws_ref[hd] = jnp.where(col <= row, gws_ref[hd], 0.0)
            gbs_ref[...] = lax.dot_general(e_ref[...], dbs_acc[...], (((1,), (1,)), ((), ())),
                                           precision=lax.Precision.HIGHEST, preferred_element_type=F32)

    tile = pl.BlockSpec((tm, D), lambda i: (i, 0))
    ptile = pl.BlockSpec((tm, 4 * WA), lambda i: (i, 0))
    return _call(
        core, name=name, grid=(nt,), jobs=jobs,
        in_specs=[tile, tile, pl.BlockSpec((tm, 2 * WA), lambda i: (i, 0)), pl.BlockSpec((tm, WA), lambda i: (i, 0)),
                  pl.BlockSpec((None, 8, D), lambda i: (i // tps, 0, 0)), _const_spec((8, D)), _const_spec((D, D)),
                  _const_spec((8, WA)), _const_spec((NHEAD, CHUNK, CHUNK)), _const_spec((CHUNK, WA)),
                  _const_spec((8, WA))],
        out_specs=[ptile, tile, tile, pl.BlockSpec((None, 8, D), lambda i: (i // tps, 0, 0)),
                   pl.BlockSpec((8, D), lambda i: (0, 0)), pl.BlockSpec((8, WA), lambda i: (0, 0)),
                   pl.BlockSpec((NHEAD, CHUNK, CHUNK), lambda i: (0, 0, 0)), pl.BlockSpec((8, CHUNK), lambda i: (0, 0))],
        out_shape=[jax.ShapeDtypeStruct((T, 4 * WA), F32), jax.ShapeDtypeStruct((T, D), BF16),
                   jax.ShapeDtypeStruct((T, D), BF16), jax.ShapeDtypeStruct((nb, 8, D), F32),
                   jax.ShapeDtypeStruct((8, D), F32), jax.ShapeDtypeStruct((8, WA), F32),
                   jax.ShapeDtypeStruct((NHEAD, CHUNK, CHUNK), F32), jax.ShapeDtypeStruct((8, CHUNK), F32)],
        scratch=[pltpu.VMEM((CHUNK, WA), F32)],
        args=[dxo, ym, proj, conv, mod, gvec, w_mo, v512, ws, bias_full, esel])


def _mixer_bwd_b(dxo, x, dpart, proj, mod, gvec, w_mi, cw, tm, name, jobs=()):
    T = x.shape[0]
    nt = T // tm
    nb = mod.shape[0]
    tps = nt // nb
    hpt = tm // HALO
    nh = T // HALO
    off = HALO - (CONV_K - 1)
    p = _pitch(tm)
    ext_rows = 8 * p

    def core(ins, outs, scs):
        dxo_ref, x_ref, dpart_ref, dnext_ref, ag_ref, halo_ref, mod_ref, g_ref, wmi_ref, cw_ref = ins
        dx_ref, dproj_ref, hb_ref, mg_ref, vg_ref, dcw_ref = outs
        glu_ext, dconv_ext, dglu_scr, dcw_acc = scs
        i = pl.program_id(0)
        first = i % tps == 0
        last = i % tps == tps - 1
        a = ag_ref[:, 0:WA]
        g = ag_ref[:, WA:2 * WA]
        sgg = _sigmoid(g)

        @pl.when(i == 0)
        def _():
            glu_ext[:, HALO + tm:HALO + ext_rows, :] = jnp.zeros((4, ext_rows - tm, 128), F32)
            dconv_ext[:, HALO + tm:HALO + ext_rows, :] = jnp.zeros((4, ext_rows - tm, 128), F32)
            dcw_acc[...] = jnp.zeros((32, 8, WA), F32)
            vg_ref[...] = jnp.zeros((8, D), F32)

        _to_slabs(glu_ext, 0, jnp.where(first, 0.0, halo_ref[:, 0:WA] * _sigmoid(halo_ref[:, WA:2 * WA])))
        _to_slabs(glu_ext, HALO, a * sgg)
        _to_slabs(dconv_ext, 0, dpart_ref[:, 2 * WA:3 * WA])
        _to_slabs(dconv_ext, tm, jnp.where(last, 0.0, dnext_ref[...]))
        sub = lax.broadcasted_iota(jnp.int32, (8, 128), 0)
        for s in range(4):
            accs = [jnp.zeros((8, 128), F32)] * CONV_K
            for v in range(p):
                dc = jnp.where(v + p * sub < tm, dconv_ext[s, pl.ds(v, 8, stride=p), :], 0.0)
                for k in range(CONV_K):
                    accs[k] = accs[k] + dc * glu_ext[s, pl.ds(v + off + k, 8, stride=p), :]
            for k in range(CONV_K):
                dcw_acc[k, :, _lanes(s)] += accs[k]
        dglu = _tap_sum(dconv_ext, dglu_scr, cw_ref, jnp.zeros((1, WA), F32), tm, lambda k: (CONV_K - 1) - k)

        @pl.when(i == nt - 1)
        def _():
            for k in range(CONV_K):
                dcw_ref[k:k + 1, :] = jnp.sum(dcw_acc[k], axis=0, keepdims=True)
            dcw_ref[CONV_K:32, :] = jnp.zeros((32 - CONV_K, WA), F32)

        da = dglu * sgg
        dgg = dglu * a * (sgg * (1.0 - sgg))
        dproj_ref[:, 0:2 * WA] = dpart_ref[:, 0:2 * WA].astype(BF16)
        dproj_ref[:, 2 * WA:3 * WA] = da.astype(BF16)
        dproj_ref[:, 3 * WA:4 * WA] = dgg.astype(BF16)
        dh = jnp.zeros((tm, D), F32)
        for j in range(NDEV):
            dh = dh + _dot_nt(dproj_ref[:, j * MB:(j + 1) * MB], wmi_ref[j])
        xv = x_ref[...]
        sc, sh = mod_ref[1:2, :], mod_ref[0:1, :]
        gpre = g_ref[0:1, :]
        r = lax.rsqrt(_rowmean(xv * xv) + EPS)
        xh = xv * r
        n = xh * gpre
        hb_ref[...] = (n * (1.0 + sc) + sh).astype(BF16)
        d_sc = _colsum(dh * n)
        d_sh = _colsum(dh)
        dn = dh * (1.0 + sc)
        d_gpre = _colsum(dn * xh)
        dxh = dn * gpre
        dx_ref[...] = dxo_ref[...] + r * (dxh - xh * _rowmean(dxh * xh))

        @pl.when(first)
        def _():
            mg_ref[...] = jnp.zeros((8, D), F32)

        mg_ref[0:1, :] += d_sh
        mg_ref[1:2, :] += d_sc
        vg_ref[0:1, :] += d_gpre

    tile = pl.BlockSpec((tm, D), lambda i: (i, 0))
    return _call(
        core, name=name, grid=(nt,), jobs=jobs,
        in_specs=[tile, tile, pl.BlockSpec((tm, 4 * WA), lambda i: (i, 0)),
                  pl.BlockSpec((HALO, WA), lambda i: (jnp.minimum((i + 1) * hpt, nh - 1), 2)),
                  pl.BlockSpec((tm, 2 * WA), lambda i: (i, 1)),
                  pl.BlockSpec((HALO, 2 * WA), lambda i: (jnp.maximum(i * hpt - 1, 0), 1)),
                  pl.BlockSpec((None, 8, D), lambda i: (i // tps, 0, 0)), _const_spec((8, D)),
                  _const_spec((NDEV, D, MB)), _const_spec((32, WA))],
        out_specs=[tile, pl.BlockSpec((tm, 4 * WA), lambda i: (i, 0)), tile,
                   pl.BlockSpec((None, 8, D), lambda i: (i // tps, 0, 0)), pl.BlockSpec((8, D), lambda i: (0, 0)),
                   pl.BlockSpec((32, WA), lambda i: (0, 0))],
        out_shape=[jax.ShapeDtypeStruct((T, D), F32), jax.ShapeDtypeStruct((T, 4 * WA), BF16),
                   jax.ShapeDtypeStruct((T, D), BF16), jax.ShapeDtypeStruct((nb, 8, D), F32),
                   jax.ShapeDtypeStruct((8, D), F32), jax.ShapeDtypeStruct((32, WA), F32)],
        scratch=[pltpu.VMEM((4, HALO + ext_rows, 128), F32), pltpu.VMEM((4, HALO + ext_rows, 128), F32),
                 pltpu.VMEM((4, ext_rows, 128), F32), pltpu.VMEM((32, 8, WA), F32)],
        args=[dxo, x, dpart, dpart, proj, proj, mod, gvec, w_mi, cw])


def _grad_chip(a, b, a_spec, b_spec, prod_shape, half, name, jobs=()):
    steps = 8 if half is None else 4
    R = prod_shape[0] if half is None else half
    C = prod_shape[1]

    def core(ins, outs, scs):
        a_ref, b_ref = ins
        (o_ref,) = outs
        own, snd, rcv, ssem, rsem, lsem = scs
        s = pl.program_id(0)
        c = lax.axis_index("c")
        me = _me()
        sib = _flip(me, (0, 0, 1))
        prod = _dot_tn(a_ref[...], b_ref[...]).astype(BF16)
        if half is None:
            q = s // 2

            @pl.when(s % 2 == c)
            def _():
                own[q] = prod

            @pl.when(s % 2 != c)
            def _():
                snd[q] = prod
                _remote(snd.at[q], rcv.at[q], ssem.at[q], rsem.at[q], sib).start()
        else:
            lo = prod[0:half, :]
            hi = prod[half:2 * half, :]
            own[s] = jnp.where(c == 0, lo, hi)
            snd[s] = jnp.where(c == 0, hi, lo)
            _remote(snd.at[s], rcv.at[s], ssem.at[s], rsem.at[s], sib).start()

        @pl.when(s == steps - 1)
        def _():
            for q4 in range(4):
                cp = _remote(snd.at[q4], rcv.at[q4], ssem.at[q4], rsem.at[q4], sib)
                cp.wait_recv()
                cp.wait_send()
                snd[q4] = (own[q4].astype(F32) + rcv[q4].astype(F32)).astype(BF16)
            out = pltpu.make_async_copy(snd, o_ref, lsem)
            out.start()
            out.wait()

    return _call(
        core, name=name, grid=(steps,), jobs=jobs, in_specs=[a_spec, b_spec], out_specs=[HBM],
        out_shape=[jax.ShapeDtypeStruct((4, R, C), BF16)],
        scratch=[pltpu.VMEM((4, R, C), BF16), pltpu.VMEM((4, R, C), BF16), pltpu.VMEM((4, R, C), BF16),
                 pltpu.SemaphoreType.DMA((4,)), pltpu.SemaphoreType.DMA((4,)), pltpu.SemaphoreType.DMA],
        args=[a, b])


def _grad_w_in(dg, hb, name, jobs=()):
    T = hb.shape[0]
    return _grad_chip(dg, hb, pl.BlockSpec((None, T, FBP), lambda s: (s, 0, 0)), _const_spec((T, D)),
                      (FBP, D), None, name, jobs)


def _grad_w_out(act, dyb, name, jobs=()):
    T = dyb.shape[0]
    return _grad_chip(act, dyb, pl.BlockSpec((None, T, FBP), lambda s: (s, 0, 0)), _const_spec((T, D)),
                      (FBP, D), FO, name, jobs)


def _grad_w_mi(hb, dproj, name, jobs=()):
    T = hb.shape[0]
    return _grad_chip(hb, dproj, _const_spec((T, D)), pl.BlockSpec((T, MB), lambda s: (0, s)),
                      (D, MB), None, name, jobs)


def _grad_w_mo(ycat, dym, name, jobs=()):
    T = ycat.shape[0]
    return _grad_chip(ycat, dym, pl.BlockSpec((T, 2 * MO), lambda s: (0, s)), _const_spec((T, D)),
                      (2 * MO, D), MO, name, jobs)


def _adamw_math(w, g, m, v):
    m2 = ADAM_B1 * m + (1.0 - ADAM_B1) * g
    v2 = ADAM_B2 * v + (1.0 - ADAM_B2) * (g * g)
    m_hat = m2 / (1.0 - ADAM_B1 ** ADAM_STEP)
    v_hat = v2 / (1.0 - ADAM_B2 ** ADAM_STEP)
    delta = -ADAM_LR * (m_hat / (jnp.sqrt(v_hat) + ADAM_EPS) + ADAM_WD * w)
    return delta, m2, v2


def _adamw_reduce(parts, w, m, v, tr, name, own=None, after=None):
    R, C = w.shape

    def core(ins, outs, _):
        p_ref, w_ref, m_ref, v_ref = ins[:4]
        g_ref, d_ref, m2_ref, v2_ref = outs
        if own is None:
            terms = [p_ref[s].astype(F32) for s in range(4)]
        else:
            mq = 2 * lax.axis_index("x") + lax.axis_index("y")
            mine = ins[4][...].astype(F32)
            terms = [jnp.where(mq == s, mine, p_ref[s].astype(F32)) for s in range(4)]
        g = terms[0]
        for s in range(1, 4):
            g = g + terms[s]
        g_ref[...] = g
        d_ref[...], m2_ref[...], v2_ref[...] = _adamw_math(w_ref[...], g, m_ref[...], v_ref[...])

    blk = pl.BlockSpec((tr, C), lambda i: (i, 0))
    in_specs = [pl.BlockSpec((4, tr, C), lambda i: (0, i, 0)), blk, blk, blk]
    args = [parts, w, m, v]
    if own is not None:
        mq = 2 * lax.axis_index("x") + lax.axis_index("y")
        in_specs.append(pl.BlockSpec((tr, C), lambda i: (i, 0)))
        args.append(lax.dynamic_index_in_dim(own, mq, 0, keepdims=False))
    if after is not None:
        in_specs.append(HBM)
        args.append(after)
    return _call(
        core, name=name, grid=(R // tr,), in_specs=in_specs,
        out_specs=[blk, blk, blk, blk], out_shape=[jax.ShapeDtypeStruct((R, C), F32)] * 4, args=args)[0]


HBM_ONLY = pl.BlockSpec(memory_space=pltpu.HBM)
SEM = pl.BlockSpec(memory_space=pltpu.SEMAPHORE)
EFFECT = pltpu.SideEffectType.DATAFLOW_SIDE_EFFECTING


def _chip_scatter_start(g):
    def body(g_ref, land_ref, ssem, rsem, g_thru, land_thru, token):
        me = _me()
        mq = 2 * me[0] + me[1]
        for k, f in enumerate(CHIP_FLIPS):
            p = _flip(me, f)
            _remote(g_ref.at[2 * p[0] + p[1]], land_ref.at[mq], ssem.at[k], rsem.at[k], p).start()
        token[...] = jnp.zeros_like(token)

    land = pltpu.with_memory_space_constraint(lax.empty(g.shape, g.dtype), pltpu.HBM)
    return pl.pallas_call(
        body, name="tail_start",
        out_shape=(pltpu.SemaphoreType.DMA((3,)), pltpu.SemaphoreType.DMA((3,)), pltpu.HBM(g.shape, g.dtype),
                   pltpu.HBM(g.shape, g.dtype), jax.ShapeDtypeStruct((8, 128), F32)),
        in_specs=(HBM_ONLY, HBM_ONLY), out_specs=(SEM, SEM, HBM_ONLY, HBM_ONLY, VM),
        input_output_aliases={0: 2, 1: 3},
        compiler_params=pltpu.CompilerParams(has_side_effects=EFFECT),
    )(pltpu.with_memory_space_constraint(g, pltpu.HBM), land)


def _chip_scatter_wait(ssem, rsem, g_thru, land_thru, after):
    def body(g_ref, land_ref, ssem, rsem, after_ref, g_dead, got_ref):
        me = _me()
        mq = 2 * me[0] + me[1]
        for k, f in enumerate(CHIP_FLIPS):
            p = _flip(me, f)
            pq = 2 * p[0] + p[1]
            _remote(g_ref.at[pq], land_ref.at[mq], ssem.at[k], rsem.at[k], p).wait_send()
            _remote(g_ref.at[mq], land_ref.at[pq], ssem.at[k], rsem.at[k], p).wait_recv()

    return pl.pallas_call(
        body, name="tail_wait",
        out_shape=(pltpu.HBM(g_thru.shape, g_thru.dtype), pltpu.HBM(g_thru.shape, g_thru.dtype)),
        in_specs=(HBM_ONLY, HBM_ONLY, SEM, SEM, HBM), out_specs=(HBM_ONLY, HBM_ONLY),
        input_output_aliases={0: 0, 1: 1},
        compiler_params=pltpu.CompilerParams(has_side_effects=EFFECT),
    )(g_thru, land_thru, ssem, rsem, after)


def _adamw_ada(sc_all, dd, w, m, v, tr, name, after=None):
    R, C = w.shape

    def core(ins, outs, _):
        sc_ref, dd_ref, w_ref, m_ref, v_ref = ins[:5]
        g_ref, d_ref, m2_ref, v2_ref = outs
        g = _dot_tn(sc_ref[...].astype(BF16), dd_ref[...].astype(BF16))
        g_ref[...] = g
        d_ref[...], m2_ref[...], v2_ref[...] = _adamw_math(w_ref[...], g, m_ref[...], v_ref[...])

    blk = pl.BlockSpec((tr, C), lambda i: (i, 0))
    return _call(
        core, name=name, grid=(R // tr,),
        in_specs=[pl.BlockSpec((64, tr), lambda i: (0, i)), pl.BlockSpec((64, C), lambda i: (0, 0)), blk, blk, blk]
        + [HBM] * (after is not None),
        out_specs=[blk, blk, blk, blk], out_shape=[jax.ShapeDtypeStruct((R, C), F32)] * 4,
        args=[sc_all, dd, w, m, v] + [after] * (after is not None))[0]


def _adamw_small(gathered, plain, grads, wmv, emit, name):
    nw = len(grads)
    ng, npl, ne = len(gathered), len(plain), len(emit)

    def core(ins, outs, _):
        srcs = []
        for a in range(ng):
            s = ins[a][0]
            for dev in range(1, NDEV):
                s = s + ins[a][dev]
            srcs.append(s)
        srcs += [ins[ng + a][...] for a in range(npl)]
        w_refs = ins[ng + npl:]
        for e, a in enumerate(emit):
            outs[e][...] = srcs[a]
        for t in range(nw):
            src, row = grads[t]
            g = srcs[src] if row is None else srcs[src][row:row + 1, :]
            w_ref, m_ref, v_ref = w_refs[3 * t:3 * t + 3]
            g_ref, d_ref, m2_ref, v2_ref = outs[ne + 4 * t:ne + 4 * t + 4]
            g_ref[...] = g
            d_ref[...], m2_ref[...], v2_ref[...] = _adamw_math(w_ref[...], g, m_ref[...], v_ref[...])

    out_shape = [jax.ShapeDtypeStruct(gathered[a].shape[1:], F32) for a in emit]
    for t in range(nw):
        out_shape += [jax.ShapeDtypeStruct(wmv[3 * t].shape, F32)] * 4
    return _call(
        core, name=name, grid=(), in_specs=[VM] * (ng + npl + 3 * nw), out_specs=[VM] * (ne + 4 * nw),
        out_shape=out_shape, args=list(gathered) + list(plain) + list(wmv))[0]


def _ada_fwd(c_pad, w_ada, b_cols, cw_pad, jobs=()):
    def core(ins, outs, scs):
        c_ref, w_ref, b_ref, cwp_ref = ins
        ada_ref, sc_ref, cw_ref = outs
        cbuf, send_buf, ssem, rsem = scs
        me = _me()
        mi = _lin(me)
        cbuf[mi] = c_ref[...]
        cw_ref[mi] = cwp_ref[...]
        peers = [_flip(me, f) for f in FLIPS]
        first = []
        for k, p in enumerate(peers):
            first.append(_remote(cbuf.at[mi], cbuf.at[mi], ssem.at[k], rsem.at[k], p))
            first.append(_remote(cw_ref.at[mi], cw_ref.at[mi], ssem.at[7 + k], rsem.at[7 + k], p))
        for cp in first:
            cp.start()
        for k, p in enumerate(peers):
            pi = _lin(p)
            _remote(cbuf.at[pi], cbuf.at[pi], ssem.at[k], rsem.at[k], p).wait_recv()
            _remote(cw_ref.at[pi], cw_ref.at[pi], ssem.at[7 + k], rsem.at[7 + k], p).wait_recv()
        c_all = cbuf[...].reshape(8 * 8, D)
        sc = c_all * _sigmoid(c_all)
        sc_ref[...] = sc
        res = _dot(sc.astype(BF16), w_ref[...].astype(BF16)) + b_ref[...]
        send_buf[...] = res.reshape(8, 8, ADA_B)
        ada_ref[mi] = send_buf[mi]
        second = []
        for k, p in enumerate(peers):
            second.append(_remote(send_buf.at[_lin(p)], ada_ref.at[mi], ssem.at[14 + k], rsem.at[14 + k], p))
        for cp in second:
            cp.start()
        for k, p in enumerate(peers):
            _remote(send_buf.at[mi], ada_ref.at[_lin(p)], ssem.at[14 + k], rsem.at[14 + k], p).wait_recv()
        for cp in first + second:
            cp.wait_send()

    return _call(
        core, name="ada_fwd", grid=(), jobs=jobs, in_specs=[VM, VM, VM, VM], out_specs=[VM, VM, VM],
        out_shape=[jax.ShapeDtypeStruct((8, 8, ADA_B), F32), jax.ShapeDtypeStruct((64, D), F32),
                   jax.ShapeDtypeStruct((8, 32, 64), F32)],
        scratch=[pltpu.VMEM((8, 8, D), F32), pltpu.VMEM((8, 8, ADA_B), F32),
                 pltpu.SemaphoreType.DMA((21,)), pltpu.SemaphoreType.DMA((21,))],
        args=[c_pad, w_ada, b_cols, cw_pad])


def _ada_bwd(dada, jobs=()):
    def core(ins, outs, scs):
        (d_ref,) = ins
        dd_ref, gb_ref = outs
        rbuf, ssem, rsem = scs
        me = _me()
        mi = _lin(me)
        peers = [_flip(me, f) for f in FLIPS]
        rbuf[mi] = d_ref[mi]
        first = []
        for k, p in enumerate(peers):
            first.append(_remote(d_ref.at[_lin(p)], rbuf.at[mi], ssem.at[k], rsem.at[k], p))
        for cp in first:
            cp.start()
        for k, p in enumerate(peers):
            _remote(d_ref.at[mi], rbuf.at[_lin(p)], ssem.at[k], rsem.at[k], p).wait_recv()
        dd = rbuf[...].reshape(64, ADA_B)
        dd_ref[...] = dd
        gb_ref[mi] = jnp.broadcast_to(_colsum(dd), (8, ADA_B))
        second = []
        for k, p in enumerate(peers):
            second.append(_remote(gb_ref.at[mi], gb_ref.at[mi], ssem.at[7 + k], rsem.at[7 + k], p))
        for cp in second:
            cp.start()
        for k, p in enumerate(peers):
            pi = _lin(p)
            _remote(gb_ref.at[pi], gb_ref.at[pi], ssem.at[7 + k], rsem.at[7 + k], p).wait_recv()
        for cp in first + second:
            cp.wait_send()

    return _call(
        core, name="ada_bwd", grid=(), jobs=jobs, in_specs=[VM], out_specs=[VM, VM],
        out_shape=[jax.ShapeDtypeStruct((64, ADA_B), F32), jax.ShapeDtypeStruct((8, 8, ADA_B), F32)],
        scratch=[pltpu.VMEM((8, 8, ADA_B), F32), pltpu.SemaphoreType.DMA((14,)), pltpu.SemaphoreType.DMA((14,))],
        args=[dada])


SMALL_D = ("g_pre_f1", "g_post_f1", "g_pre_m", "g_post_m", "g_pre_f2", "g_post_f2")
SMALL_W = ("gmlp_norm_g", "gmlp_norm_b", "conv_b", "conv_norm_g", "conv_norm_b", "g_out_a", "g_out_b")


def kernel(x, c, w_ada, b_ada, g_pre_f1, g_post_f1, w_f1_in, w_f1_out, g_pre_m, g_post_m, w_mix_in, gmlp_norm_g, gmlp_norm_b, w_spatial, b_spatial, conv_w, conv_b, conv_norm_g, conv_norm_b, g_out_a, g_out_b, w_mix_out, g_pre_f2, g_post_f2, w_f2_in, w_f2_out, loss_target, m_w_ada, m_b_ada, m_g_pre_f1, m_g_post_f1, m_w_f1_in, m_w_f1_out, m_g_pre_m, m_g_post_m, m_w_mix_in, m_gmlp_norm_g, m_gmlp_norm_b, m_w_spatial, m_b_spatial, m_conv_w, m_conv_b, m_conv_norm_g, m_conv_norm_b, m_g_out_a, m_g_out_b, m_w_mix_out, m_g_pre_f2, m_g_post_f2, m_w_f2_in, m_w_f2_out, v_w_ada, v_b_ada, v_g_pre_f1, v_g_post_f1, v_w_f1_in, v_w_f1_out, v_g_pre_m, v_g_post_m, v_w_mix_in, v_gmlp_norm_g, v_gmlp_norm_b, v_w_spatial, v_b_spatial, v_conv_w, v_conv_b, v_conv_norm_g, v_conv_norm_b, v_g_out_a, v_g_out_b, v_w_mix_out, v_g_pre_f2, v_g_post_f2, v_w_f2_in, v_w_f2_out):
    given = dict(locals())
    bl, seq, _ = x.shape
    T = bl * seq
    tm = min(256, seq // 2)
    mi = _lin((lax.axis_index("x"), lax.axis_index("y"), lax.axis_index("c")))

    def shard_in(w):
        return jnp.pad(w[0].T.astype(BF16), ((0, FBP - FB), (0, 0)))

    zpad = jnp.zeros((max(FBP - FB, 16), D), BF16)
    g_f1 = _Gather([shard_in(w_f1_in), w_f1_out[0].astype(BF16)], ("rows", "out"), zpad)
    g_mx = _Gather([w_mix_in[0].astype(BF16), w_mix_out[0].astype(BF16), w_f2_out[0].astype(BF16)],
                   ("rows", "rows", "out"), zpad)
    g_f2 = _Gather([shard_in(w_f2_in)], ("rows",), zpad, late_mid=True)

    c_pad = jnp.pad(c, ((0, 8 - bl), (0, 0)))
    b_cols = lax.dynamic_slice(b_ada, (0, mi * ADA_B), (1, ADA_B))
    cw_pad = jnp.pad(conv_w[0], ((0, 1), (0, 0)))
    (ada_blk, sc_all, cw_all), ((wi1, wo1),) = _ada_fwd(c_pad, w_ada[0], b_cols, cw_pad, jobs=[g_f1])
    ada = ada_blk[:, 0:bl, :].transpose(1, 0, 2).reshape(bl, 9, D)
    pad5 = jnp.zeros((bl, 5, D), F32)
    mod1 = jnp.concatenate([ada[:, 0:3], pad5], axis=1)
    mod2 = jnp.concatenate([ada[:, 3:6], pad5], axis=1)
    mod3 = jnp.concatenate([ada[:, 6:9], pad5], axis=1)
    cw_full = cw_all.transpose(1, 0, 2).reshape(32, WA)

    zrow = jnp.zeros((1, D), F32)
    gv1 = jnp.concatenate([g_pre_f1, g_post_f1] + [zrow] * 6, axis=0)
    gvm = jnp.concatenate([g_pre_m, g_post_m] + [zrow] * 6, axis=0)
    gv2 = jnp.concatenate([g_pre_f2, g_post_f2] + [zrow] * 6, axis=0)
    v512 = jnp.concatenate([gmlp_norm_g, gmlp_norm_b, conv_b, conv_norm_g, conv_norm_b, g_out_a, g_out_b,
                            jnp.zeros((1, WA), F32)], axis=0)
    ws = w_spatial[0]
    bias_full = jnp.repeat(b_spatial[0].T, HD, axis=1)
    esel = (lax.broadcasted_iota(jnp.int32, (8, WA), 1) // HD == lax.broadcasted_iota(jnp.int32, (8, WA), 0)).astype(F32)

    x0 = x.reshape(T, D)
    (x1, gu1, y1), ((wmi, wmo, wo2),) = _ffn_fwd(x0, mod1, gv1, wi1, wo1, tm, "ffn1_fwd", jobs=[g_mx])
    wmo = wmo.reshape(D, D)
    (x2, proj, ym, conv), ((wi2,),) = _mixer_fwd(x1, mod2, gvm, wmi, wmo, v512, ws, bias_full, cw_full, tm, "mixer_fwd", jobs=[g_f2])
    (dx3, gu2, y2, loss_blk), _ = _ffn_fwd(x2, mod3, gv2, wi2, wo2, tm, "ffn2_fwd", target=loss_target.reshape(T, D))

    (dx2, dg2, act2, hb2, dyb2, mg3, vg3), _ = _ffn_bwd(dx3, x2, y2, gu2, mod3, gv2, wi2, wo2, tm, "ffn2_bwd")
    (g_wi2,), _ = _grad_w_in(dg2, hb2, "ffn2_gw_in")
    (g_wo2,), _ = _grad_w_out(act2, dyb2, "ffn2_gw_out")
    (dpart, dymb, ycat, mg2a, vgma, v5g, gws, gbs), ((p_wi2,),) = _mixer_bwd_a(
        dx2, ym, proj, conv, mod2, gvm, wmo, v512, ws, bias_full, esel, tm, "mixer_bwd_a",
        jobs=[_ChipScatter([g_wi2])])
    (dx1, dproj, hbm, mg2b, vgmb, dcw), ((p_wo2,),) = _mixer_bwd_b(
        dx2, x1, dpart, proj, mod2, gvm, wmi, cw_full, tm, "mixer_bwd_b", jobs=[_ChipScatter([g_wo2])])
    (g_wmi,), _ = _grad_w_mi(hbm, dproj, "mixer_gw_in")
    (g_wmo,), _ = _grad_w_mo(ycat, dymb, "mixer_gw_out")
    p2 = jnp.concatenate([v5g, dcw], axis=0)
    (dx0, dg1, act1, hb1, dyb1, mg1, vg1), _ = _ffn_bwd(dx1, x0, y1, gu1, mod1, gv1, wi1, wo1, tm, "ffn1_bwd")

    dada = jnp.concatenate([mg1[:, 0:3], mg2b[:, 0:2], mg2a[:, 2:3], mg3[:, 0:3]], axis=1)
    dada = dada.reshape(bl, NDEV, ADA_B).transpose(1, 0, 2)
    dada = jnp.pad(dada, ((0, 0), (0, 8 - bl), (0, 0)))
    p1 = jnp.concatenate([vg1[0:2], vgmb[0:1], vgma[1:2], vg3[0:2], loss_blk[0:1], zrow], axis=0)
    (dd_all, gb_all), ((a1,),) = _ada_bwd(dada, jobs=[_AllGather([p1])])
    g_bada = gb_all[:, 0, :].reshape(1, 9 * D)

    (g_wo1,), ((p_wmi, p_wmo),) = _grad_w_out(act1, dyb1, "ffn1_gw_out", jobs=[_ChipScatter([g_wmi, g_wmo])])
    (g_wi1,), ((a2, a3, a4), (p_wo1,)) = _grad_w_in(
        dg1, hb1, "ffn1_gw_in", jobs=[_Gather([p2, gws, gbs], ("rows",) * 3, zpad), _ChipScatter([g_wo1])])

    ssem, rsem, g_thru, land_thru, token = _chip_scatter_start(g_wi1)

    res = {}
    quad = _adamw_reduce(p_wi2, w_f2_in[0].T, m_w_f2_in[0].T, v_w_f2_in[0].T, FO, "adamw_w_f2_in", after=token)
    res["w_f2_in"] = tuple(t.T[None] for t in quad)
    for nm, part, tr in (("w_f2_out", p_wo2, FO), ("w_mix_in", p_wmi, 256), ("w_mix_out", p_wmo, MO), ("w_f1_out", p_wo1, FO)):
        quad = _adamw_reduce(part, given[nm][0], given["m_" + nm][0], given["v_" + nm][0], tr, "adamw_" + nm, after=quad[1])
        res[nm] = tuple(t[None] for t in quad)
    quad = _adamw_ada(sc_all, dd_all, w_ada[0], m_w_ada[0], v_w_ada[0], 256, "adamw_w_ada", after=quad[1])
    res["w_ada"] = tuple(t[None] for t in quad)
    g_wi1, p_wi1 = _chip_scatter_wait(ssem, rsem, g_thru, land_thru, quad[1])
    quad = _adamw_reduce(p_wi1, w_f1_in[0].T, m_w_f1_in[0].T, v_w_f1_in[0].T, FO, "adamw_w_f1_in", own=g_wi1)
    res["w_f1_in"] = tuple(t.T[None] for t in quad)

    small = SMALL_D + SMALL_W + ("w_spatial", "b_spatial", "b_ada")
    grads = [(0, r) for r in range(6)] + [(1, r) for r in range(7)] + [(2, None), (3, None), (4, None)]
    wmv = []
    for nm in small:
        for pre in ("", "m_", "v_"):
            wmv.append(given[pre + nm][0] if nm in ("w_spatial", "b_spatial") else given[pre + nm])
    outs = _adamw_small([a1, a2, a3, a4], [g_bada], grads, wmv, (0, 1), "adamw_small")
    loss = outs[0][6, 0]
    for t, nm in enumerate(small):
        quad = outs[2 + 4 * t:6 + 4 * t]
        res[nm] = tuple(q[None] for q in quad) if nm in ("w_spatial", "b_spatial") else tuple(quad)
    g_cw = lax.dynamic_slice(outs[1], (8, mi * 64), (32, 64))
    wmv = [jnp.pad(given[pre + "conv_w"][0], ((0, 1), (0, 0)), constant_values=1.0 if pre == "v_" else 0.0)
           for pre in ("", "m_", "v_")]
    quad = _adamw_small([], [g_cw], [(0, None)], wmv, (), "adamw_conv_w")
    res["conv_w"] = tuple(q[0:CONV_K][None] for q in quad)

    order = ["w_ada", "b_ada", "g_pre_f1", "g_post_f1", "w_f1_in", "w_f1_out", "g_pre_m", "g_post_m", "w_mix_in",
             "gmlp_norm_g", "gmlp_norm_b", "w_spatial", "b_spatial", "conv_w", "conv_b", "conv_norm_g", "conv_norm_b",
             "g_out_a", "g_out_b", "w_mix_out", "g_pre_f2", "g_post_f2", "w_f2_in", "w_f2_out"]
    out = [loss, dx0.reshape(bl, seq, D)]
    for k in range(4):
        out += [res[nm][k] for nm in order]
    return tuple(out)
```

```python
import jax
import jax.numpy as jnp
from jax import lax
from jax.experimental import pallas as pl
from jax.experimental.pallas import tpu as pltpu

F32 = jnp.float32
BF16 = jnp.bfloat16

D = 1024
DFF = 2816
NDEV = 8
FB = 2 * DFF // NDEV
FBP = 704
FO = DFF // NDEV
WA = 512
NHEAD = 8
HD = 64
CHUNK = 128
CONV_K = 31
HALO = 32
MB = 2 * (WA + WA) // NDEV
MO = D // NDEV
ADA_B = 9 * D // NDEV
EPS = 1e-6
HALF = 0.5

ADAM_LR = 0.001
ADAM_B1 = 0.9
ADAM_B2 = 0.999
ADAM_EPS = 1e-08
ADAM_WD = 0.01
ADAM_STEP = 10

VMEM_LIMIT = 56 * 1024 * 1024
MESH = pl.DeviceIdType.MESH
FLIPS = ((0, 0, 1), (1, 0, 0), (0, 1, 0), (1, 1, 0), (1, 0, 1), (0, 1, 1), (1, 1, 1))
CHIP_FLIPS = ((1, 0, 0), (0, 1, 0), (1, 1, 0))
HBM = pl.BlockSpec(memory_space=pl.ANY)
VM = pl.BlockSpec(memory_space=pltpu.VMEM)


def _dot(a, b):
    return lax.dot_general(a, b, (((1,), (0,)), ((), ())), preferred_element_type=F32)


def _dot_nt(a, b):
    return lax.dot_general(a, b, (((1,), (1,)), ((), ())), preferred_element_type=F32)


def _dot_tn(a, b):
    return lax.dot_general(a, b, (((0,), (0,)), ((), ())), preferred_element_type=F32)


def _rowmean(v):
    return jnp.mean(v, axis=-1, keepdims=True)


def _colsum(v):
    return jnp.sum(v, axis=0, keepdims=True)


def _sigmoid(v):
    return 0.5 * jnp.tanh(0.5 * v) + 0.5


def _const_spec(shape):
    nd = len(shape)
    return pl.BlockSpec(shape, lambda *_: (0,) * nd, pipeline_mode=pl.Buffered(1))


def _me():
    return lax.axis_index("x"), lax.axis_index("y"), lax.axis_index("c")


def _flip(me, f):
    return tuple(1 - v if b else v for v, b in zip(me, f))


def _lin(p):
    return 4 * p[0] + 2 * p[1] + p[2]


def _remote(src, dst, send_sem, recv_sem, dev):
    return pltpu.make_async_remote_copy(src_ref=src, dst_ref=dst, send_sem=send_sem, recv_sem=recv_sem,
                                        device_id=dev, device_id_type=MESH)


def _blk(kind, ref, p):
    if kind == "out":
        return ref.at[2 * p[0] + p[1], pl.ds(p[2] * FO, FO), :]
    return ref.at[_lin(p)]


class _Gather:
    def __init__(self, shards, kinds, zpad, late_mid=False):
        self.late_mid = late_mid
        self.kinds = kinds
        self.n = len(shards)
        self.ins = list(shards) + [zpad]
        self.out_shape = [jax.ShapeDtypeStruct((4, FBP, D) if k == "out" else (NDEV,) + s.shape, s.dtype)
                          for s, k in zip(shards, kinds)]
        self.n_out = sum(k == "out" for k in kinds)
        self.sems = [pltpu.SemaphoreType.DMA((7 * self.n,)), pltpu.SemaphoreType.DMA((7 * self.n,)),
                     pltpu.SemaphoreType.DMA((self.n + 4 * max(self.n_out, 1),))]

    def _first(self, ins, outs, sems):
        ssem, rsem, lsem = sems
        me = _me()
        sib = _flip(me, (0, 0, 1))
        cps, loc = [], []
        nz = 0
        for a in range(self.n):
            mine = _blk(self.kinds[a], outs[a], me)
            loc.append(pltpu.make_async_copy(ins[a], mine, lsem.at[a]))
            if self.kinds[a] == "out" and FBP > FB:
                for q in range(4):
                    loc.append(pltpu.make_async_copy(ins[self.n], outs[a].at[q, pl.ds(FB, FBP - FB), :],
                                                     lsem.at[self.n + 4 * nz + q]))
                nz += 1
            cps.append(_remote(ins[a], mine, ssem.at[7 * a], rsem.at[7 * a], sib))
            for j, f in enumerate(CHIP_FLIPS):
                cps.append(_remote(ins[a], mine, ssem.at[7 * a + 1 + j], rsem.at[7 * a + 1 + j], _flip(me, f)))
        return cps, loc

    def _passed(self, outs, sems):
        ssem, rsem, _ = sems
        me = _me()
        sib = _flip(me, (0, 0, 1))
        cps = []
        for j, f in enumerate(CHIP_FLIPS):
            for a in range(self.n):
                blk = _blk(self.kinds[a], outs[a], _flip(me, f))
                cps.append(_remote(blk, blk, ssem.at[7 * a + 4 + j], rsem.at[7 * a + 4 + j], sib))
        return cps

    def start(self, ins, outs, sems):
        cps, loc = self._first(ins, outs, sems)
        for cp in loc + cps:
            cp.start()

    def mid(self, ins, outs, sems):
        ssem, rsem, _ = sems
        me = _me()
        passed = self._passed(outs, sems)
        t = 0
        for j, f in enumerate(CHIP_FLIPS):
            for a in range(self.n):
                blk = _blk(self.kinds[a], outs[a], _flip(me, f))
                _remote(blk, blk, ssem.at[7 * a + 1 + j], rsem.at[7 * a + 1 + j], _flip(me, f)).wait_recv()
                passed[t].start()
                t += 1

    def end(self, ins, outs, sems):
        ssem, rsem, _ = sems
        me = _me()
        sib = _flip(me, (0, 0, 1))
        for a in range(self.n):
            blk = _blk(self.kinds[a], outs[a], sib)
            _remote(blk, blk, ssem.at[7 * a], rsem.at[7 * a], sib).wait_recv()
            for j, f in enumerate(CHIP_FLIPS):
                blk = _blk(self.kinds[a], outs[a], _flip(_flip(me, f), (0, 0, 1)))
                _remote(blk, blk, ssem.at[7 * a + 4 + j], rsem.at[7 * a + 4 + j], sib).wait_recv()
        cps, loc = self._first(ins, outs, sems)
        for cp in cps + self._passed(outs, sems):
            cp.wait_send()
        for cp in loc:
            cp.wait()


class _AllGather:
    def __init__(self, parts):
        self.n = len(parts)
        self.ins = list(parts)
        self.out_shape = [jax.ShapeDtypeStruct((NDEV,) + p.shape, p.dtype) for p in parts]
        self.sems = [pltpu.SemaphoreType.DMA((7 * self.n,)), pltpu.SemaphoreType.DMA((7 * self.n,)),
                     pltpu.SemaphoreType.DMA((self.n,))]

    def _copies(self, ins, outs, sems):
        ssem, rsem, lsem = sems
        me = _me()
        mi = _lin(me)
        loc = [pltpu.make_async_copy(ins[a], outs[a].at[mi], lsem.at[a]) for a in range(self.n)]
        cps = []
        for k, f in enumerate(FLIPS):
            for a in range(self.n):
                cps.append(_remote(ins[a], outs[a].at[mi], ssem.at[7 * a + k], rsem.at[7 * a + k], _flip(me, f)))
        return cps, loc

    def start(self, ins, outs, sems):
        cps, loc = self._copies(ins, outs, sems)
        for cp in loc + cps:
            cp.start()

    mid = None

    def end(self, ins, outs, sems):
        ssem, rsem, _ = sems
        me = _me()
        for k, f in enumerate(FLIPS):
            p = _flip(me, f)
            for a in range(self.n):
                _remote(ins[a], outs[a].at[_lin(p)], ssem.at[7 * a + k], rsem.at[7 * a + k], p).wait_recv()
        cps, loc = self._copies(ins, outs, sems)
        for cp in cps:
            cp.wait_send()
        for cp in loc:
            cp.wait()


def _call(core, *, name, grid, in_specs, out_specs, out_shape, args, scratch=(), jobs=()):
    n_in, n_out, n_sc = len(in_specs), len(out_specs), len(scratch)
    steps = 1
    for g in grid:
        steps *= g

    def body(*refs):
        pos = [0]

        def take(k):
            r = refs[pos[0]:pos[0] + k]
            pos[0] += k
            return r

        ins = take(n_in)
        j_ins = [take(len(j.ins)) for j in jobs]
        outs = take(n_out)
        j_outs = [take(len(j.out_shape)) for j in jobs]
        scs = take(n_sc)
        j_sems = [take(len(j.sems)) for j in jobs]
        if len(grid) == 2:
            step = pl.program_id(0) * grid[1] + pl.program_id(1)
        elif len(grid) == 1:
            step = pl.program_id(0)
        else:
            step = 0
        for j, ji, jo, js in zip(jobs, j_ins, j_outs, j_sems):
            if grid:
                pl.when(step == 0)(lambda j=j, ji=ji, jo=jo, js=js: j.start(ji, jo, js))
            else:
                j.start(ji, jo, js)
        for j, ji, jo, js in zip(jobs, j_ins, j_outs, j_sems):
            if j.mid is not None and grid:
                at = steps - 1 if j.late_mid else (3 * steps) // 4
                pl.when(step == at)(lambda j=j, ji=ji, jo=jo, js=js: j.mid(ji, jo, js))
        if core is not None:
            core(ins, outs, scs)
        for j, ji, jo, js in zip(jobs, j_ins, j_outs, j_sems):
            if grid:
                pl.when(step == steps - 1)(lambda j=j, ji=ji, jo=jo, js=js: j.end(ji, jo, js))
            else:
                if j.mid is not None:
                    j.mid(ji, jo, js)
                j.end(ji, jo, js)

    all_in = list(in_specs)
    all_args = list(args)
    all_out = list(out_specs)
    all_shape = list(out_shape)
    all_sc = list(scratch)
    for j in jobs:
        all_in += [HBM] * len(j.ins)
        all_args += j.ins
    for j in jobs:
        all_out += [HBM] * len(j.out_shape)
        all_shape += j.out_shape
        all_sc += j.sems
    params = dict(vmem_limit_bytes=VMEM_LIMIT)
    if grid:
        params["dimension_semantics"] = ("arbitrary",) * len(grid)
    res = pl.pallas_call(
        body, name=name, grid=grid, in_specs=all_in, out_specs=all_out, out_shape=all_shape,
        scratch_shapes=all_sc, compiler_params=pltpu.CompilerParams(**params),
    )(*all_args)
    core_res = list(res[:n_out])
    job_res = []
    pos = n_out
    for j in jobs:
        job_res.append(list(res[pos:pos + len(j.out_shape)]))
        pos += len(j.out_shape)
    return core_res, job_res


def _ffn_fwd(x, mod, gvec, w_in, w_out, tm, name, jobs=(), target=None):
    T = x.shape[0]
    nt = T // tm
    tps = nt // mod.shape[0]

    def core(ins, outs, _):
        x_ref, mod_ref, g_ref, win_ref, wout_ref = ins[:5]
        xo_ref, gu_ref, y_ref = outs[:3]
        xv = x_ref[...]
        sh, sc, gt = mod_ref[0:1, :], mod_ref[1:2, :], mod_ref[2:3, :]
        r = lax.rsqrt(_rowmean(xv * xv) + EPS)
        h = (xv * r * g_ref[0:1, :]) * (1.0 + sc) + sh
        hb = h.astype(BF16)
        y = jnp.zeros((tm, D), F32)
        for cidx in range(4):
            gate = _dot_nt(hb, win_ref[cidx])
            up = _dot_nt(hb, win_ref[4 + cidx])
            gu_ref[cidx] = gate.astype(BF16)
            gu_ref[4 + cidx] = up.astype(BF16)
            act = gate * _sigmoid(gate) * up
            y = y + _dot(act.astype(BF16), wout_ref[cidx])
        y_ref[...] = y
        ry = lax.rsqrt(_rowmean(y * y) + EPS)
        xo = xv + (HALF * gt) * (y * ry * g_ref[1:2, :])
        if target is None:
            xo_ref[...] = xo
        else:
            loss_ref = outs[3]

            @pl.when(pl.program_id(0) == 0)
            def _():
                loss_ref[...] = jnp.zeros((8, D), F32)

            err = xo - ins[5][...]
            xo_ref[...] = err * (1.0 / D)
            loss_ref[...] += HALF * jnp.sum(_rowmean(err * err), axis=0, keepdims=True)

    tile = pl.BlockSpec((tm, D), lambda i: (i, 0))
    extra = target is not None
    return _call(
        core, name=name, grid=(nt,), jobs=jobs,
        in_specs=[tile, pl.BlockSpec((None, 8, D), lambda i: (i // tps, 0, 0)), _const_spec((8, D)),
                  _const_spec((8, FBP, D)), _const_spec((4, FBP, D))] + [tile] * extra,
        out_specs=[tile, pl.BlockSpec((8, tm, FBP), lambda i: (0, i, 0)), tile]
        + [pl.BlockSpec((8, D), lambda i: (0, 0))] * extra,
        out_shape=[jax.ShapeDtypeStruct((T, D), F32), jax.ShapeDtypeStruct((8, T, FBP), BF16),
                   jax.ShapeDtypeStruct((T, D), F32)] + [jax.ShapeDtypeStruct((8, D), F32)] * extra,
        args=[x, mod, gvec, w_in, w_out] + [target] * extra)


def _ffn_bwd(dxo, x, y, gu, mod, gvec, w_in, w_out, tm, name, jobs=()):
    T = x.shape[0]
    nt = T // tm
    nb = mod.shape[0]
    tps = nt // nb

    def core(ins, outs, _):
        dxo_ref, x_ref, y_ref, gu_ref, mod_ref, g_ref, win_ref, wout_ref = ins
        dx_ref, dg_ref, act_ref, hb_ref, dyb_ref, mg_ref, vg_ref = outs
        i = pl.program_id(0)
        xv = x_ref[...]
        dxo_v = dxo_ref[...]
        yv = y_ref[...]
        sh, sc, gt = mod_ref[0:1, :], mod_ref[1:2, :], mod_ref[2:3, :]
        gpre, gpost = g_ref[0:1, :], g_ref[1:2, :]
        r = lax.rsqrt(_rowmean(xv * xv) + EPS)
        xh = xv * r
        n = xh * gpre
        hb = (n * (1.0 + sc) + sh).astype(BF16)
        hb_ref[...] = hb
        ry = lax.rsqrt(_rowmean(yv * yv) + EPS)
        yh = yv * ry
        d_gt = _colsum(HALF * dxo_v * (yh * gpost))
        dp = (HALF * gt) * dxo_v
        d_gpost = _colsum(dp * yh)
        dyh = dp * gpost
        dy = ry * (dyh - yh * _rowmean(dyh * yh))
        dyb = dy.astype(BF16)
        dyb_ref[...] = dyb
        dh = jnp.zeros((tm, D), F32)
        for cidx in range(4):
            gate = gu_ref[cidx].astype(F32)
            up = gu_ref[4 + cidx].astype(F32)
            sig = _sigmoid(gate)
            s = gate * sig
            act_ref[cidx] = (s * up).astype(BF16)
            d_act = _dot_nt(dyb, wout_ref[cidx])
            d_up = (d_act * s).astype(BF16)
            d_gate = (d_act * up * (sig * (1.0 + gate * (1.0 - sig)))).astype(BF16)
            dg_ref[cidx] = d_gate
            dg_ref[4 + cidx] = d_up
            dh = dh + _dot(d_gate, win_ref[cidx]) + _dot(d_up, win_ref[4 + cidx])
        d_sc = _colsum(dh * n)
        d_sh = _colsum(dh)
        dn = dh * (1.0 + sc)
        d_gpre = _colsum(dn * xh)
        dxh = dn * gpre
        dx_ref[...] = dxo_v + r * (dxh - xh * _rowmean(dxh * xh))

        @pl.when(i % tps == 0)
        def _():
            mg_ref[...] = jnp.zeros((8, D), F32)

        @pl.when(i == 0)
        def _():
            vg_ref[...] = jnp.zeros((8, D), F32)

        mg_ref[0:1, :] += d_sh
        mg_ref[1:2, :] += d_sc
        mg_ref[2:3, :] += d_gt
        vg_ref[0:1, :] += d_gpre
        vg_ref[1:2, :] += d_gpost

    tile = pl.BlockSpec((tm, D), lambda i: (i, 0))
    return _call(
        core, name=name, grid=(nt,), jobs=jobs,
        in_specs=[tile, tile, tile, pl.BlockSpec((8, tm, FBP), lambda i: (0, i, 0)),
                  pl.BlockSpec((None, 8, D), lambda i: (i // tps, 0, 0)), _const_spec((8, D)),
                  _const_spec((8, FBP, D)), _const_spec((4, FBP, D))],
        out_specs=[tile, pl.BlockSpec((8, tm, FBP), lambda i: (0, i, 0)),
                   pl.BlockSpec((4, tm, FBP), lambda i: (0, i, 0)), tile, tile,
                   pl.BlockSpec((None, 8, D), lambda i: (i // tps, 0, 0)), pl.BlockSpec((8, D), lambda i: (0, 0))],
        out_shape=[jax.ShapeDtypeStruct((T, D), F32), jax.ShapeDtypeStruct((8, T, FBP), BF16),
                   jax.ShapeDtypeStruct((4, T, FBP), BF16), jax.ShapeDtypeStruct((T, D), BF16),
                   jax.ShapeDtypeStruct((T, D), BF16), jax.ShapeDtypeStruct((nb, 8, D), F32),
                   jax.ShapeDtypeStruct((8, D), F32)],
        args=[dxo, x, y, gu, mod, gvec, w_in, w_out])


def _masked_spatial(ws_ref):
    row = lax.broadcasted_iota(jnp.int32, (CHUNK, CHUNK), 0)
    col = lax.broadcasted_iota(jnp.int32, (CHUNK, CHUNK), 1)
    keep = col <= row
    return [jnp.where(keep, ws_ref[hd], 0.0).astype(BF16) for hd in range(NHEAD)]


def _spatial_gate(wm, vb_chunk, lane_head):
    z = jnp.zeros((CHUNK, WA), F32)
    for hd in range(NHEAD):
        z = jnp.where(lane_head == hd, _dot(wm[hd], vb_chunk), z)
    return z


def _layer_norm_stats(v):
    mu = _rowmean(v)
    vc = v - mu
    rstd = lax.rsqrt(_rowmean(vc * vc) + EPS)
    return vc * rstd, rstd


def _pitch(tm):
    p = tm // 8
    while p % 8 != 4:
        p += 1
    return p


def _lanes(s):
    return slice(s * 128, (s + 1) * 128)


def _to_slabs(ref, row0, val):
    for s in range(4):
        ref[s, row0:row0 + val.shape[0], :] = val[:, _lanes(s)]


def _tap_sum(src, out, cw_ref, bias, tm, start):
    p = _pitch(tm)
    for s in range(4):
        accs = [jnp.broadcast_to(bias[:, _lanes(s)], (8, 128))] * p
        for k in range(CONV_K):
            w = jnp.broadcast_to(cw_ref[k:k + 1, _lanes(s)], (8, 128))
            for v in range(p):
                accs[v] = accs[v] + w * src[s, pl.ds(v + start(k), 8, stride=p), :]
        for v in range(p):
            out[s, pl.ds(v, 8, stride=p), :] = accs[v]
    return jnp.concatenate([out[s, 0:tm, :] for s in range(4)], axis=1)


def _mixer_fwd(x, mod, gvec, w_mi, w_mo, v512, ws, bias_full, cw, tm, name, jobs=()):
    T = x.shape[0]
    nt = T // tm
    tps = nt // mod.shape[0]
    ext_rows = 8 * _pitch(tm)

    def core(ins, outs, scs):
        x_ref, mod_ref, g_ref, wmi_ref, wmo_ref, v_ref, ws_ref, bias_ref, cw_ref = ins
        xo_ref, proj_ref, ym_ref, conv_ref = outs
        glu_ext, conv_scr = scs
        i = pl.program_id(0)
        xv = x_ref[...]
        sh, sc, gt = mod_ref[0:1, :], mod_ref[1:2, :], mod_ref[2:3, :]
        r = lax.rsqrt(_rowmean(xv * xv) + EPS)
        hb = ((xv * r * g_ref[0:1, :]) * (1.0 + sc) + sh).astype(BF16)
        for j in range(NDEV):
            proj_ref[:, j * MB:(j + 1) * MB] = _dot(hb, wmi_ref[j])
        u = proj_ref[:, 0:WA]
        v0 = proj_ref[:, WA:2 * WA]
        a = proj_ref[:, 2 * WA:3 * WA]
        g = proj_ref[:, 3 * WA:4 * WA]
        vh, _ = _layer_norm_stats(v0)
        vb = (vh * v_ref[0:1, :] + v_ref[1:2, :]).astype(BF16)
        wm = _masked_spatial(ws_ref)
        lane_head = lax.broadcasted_iota(jnp.int32, (CHUNK, WA), 1) >> 6
        ya = []
        for q in range(tm // CHUNK):
            z = _spatial_gate(wm, vb[q * CHUNK:(q + 1) * CHUNK, :], lane_head) + bias_ref[...]
            ya.append(u[q * CHUNK:(q + 1) * CHUNK, :] * z)
        ya = jnp.concatenate(ya, axis=0)
        glu = a * _sigmoid(g)

        @pl.when(i == 0)
        def _():
            glu_ext[:, HALO + tm:HALO + ext_rows, :] = jnp.zeros((4, ext_rows - tm, 128), F32)

        @pl.when(i % tps == 0)
        def _():
            glu_ext[:, 0:HALO, :] = jnp.zeros((4, HALO, 128), F32)

        _to_slabs(glu_ext, HALO, glu)
        conv = _tap_sum(glu_ext, conv_scr, cw_ref, v_ref[2:3, :], tm, lambda k: HALO - (CONV_K - 1) + k)
        conv_ref[...] = conv
        glu_ext[:, 0:HALO, :] = glu_ext[:, tm:tm + HALO, :]
        ch, _ = _layer_norm_stats(conv)
        cn = ch * v_ref[3:4, :] + v_ref[4:5, :]
        yb = cn * _sigmoid(cn)
        pa = ya * lax.rsqrt(_rowmean(ya * ya) + EPS) * v_ref[5:6, :]
        pb = yb * lax.rsqrt(_rowmean(yb * yb) + EPS) * v_ref[6:7, :]
        ycat = jnp.concatenate([pa, pb], axis=1).astype(BF16)
        ym = _dot(ycat, wmo_ref[...])
        ym_ref[...] = ym
        rm = lax.rsqrt(_rowmean(ym * ym) + EPS)
        xo_ref[...] = xv + gt * (ym * rm * g_ref[1:2, :])

    tile = pl.BlockSpec((tm, D), lambda i: (i, 0))
    return _call(
        core, name=name, grid=(nt,), jobs=jobs,
        in_specs=[tile, pl.BlockSpec((None, 8, D), lambda i: (i // tps, 0, 0)), _const_spec((8, D)),
                  _const_spec((NDEV, D, MB)), _const_spec((D, D)), _const_spec((8, WA)),
                  _const_spec((NHEAD, CHUNK, CHUNK)), _const_spec((CHUNK, WA)), _const_spec((32, WA))],
        out_specs=[tile, pl.BlockSpec((tm, 4 * WA), lambda i: (i, 0)), tile, pl.BlockSpec((tm, WA), lambda i: (i, 0))],
        out_shape=[jax.ShapeDtypeStruct((T, D), F32), jax.ShapeDtypeStruct((T, 4 * WA), F32),
                   jax.ShapeDtypeStruct((T, D), F32), jax.ShapeDtypeStruct((T, WA), F32)],
        scratch=[pltpu.VMEM((4, HALO + ext_rows, 128), F32), pltpu.VMEM((4, ext_rows, 128), F32)],
        args=[x, mod, gvec, w_mi, w_mo, v512, ws, bias_full, cw])


def _mixer_bwd_a(dxo, ym, proj, conv, mod, gvec, w_mo, v512, ws, bias_full, esel, tm, name, jobs=()):
    T = dxo.shape[0]
    nt = T // tm
    nb = mod.shape[0]
    tps = nt // nb

    def core(ins, outs, scs):
        dxo_ref, ym_ref, proj_ref, conv_ref, mod_ref, g_ref, wmo_ref, v_ref, ws_ref, bias_ref, e_ref = ins
        dpart_ref, dymb_ref, ycat_ref, mg_ref, vg_ref, v5g_ref, gws_ref, gbs_ref = outs
        (dbs_acc,) = scs
        i = pl.program_id(0)
        dxo_v = dxo_ref[...]
        ymv = ym_ref[...]
        gt = mod_ref[2:3, :]
        gpost = g_ref[1:2, :]
        rm = lax.rsqrt(_rowmean(ymv * ymv) + EPS)
        ymh = ymv * rm
        d_gt = _colsum(dxo_v * (ymh * gpost))
        dpm = gt * dxo_v
        d_gpost = _colsum(dpm * ymh)
        dymh = dpm * gpost
        dym = (rm * (dymh - ymh * _rowmean(dymh * ymh))).astype(BF16)
        dymb_ref[...] = dym
        dycat = _dot_nt(dym, wmo_ref[...])
        u = proj_ref[:, 0:WA]
        v0 = proj_ref[:, WA:2 * WA]
        vh, rv = _layer_norm_stats(v0)
        vb = (vh * v_ref[0:1, :] + v_ref[1:2, :]).astype(BF16)
        wm = _masked_spatial(ws_ref)
        lane_head = lax.broadcasted_iota(jnp.int32, (CHUNK, WA), 1) >> 6
        zs = []
        for q in range(tm // CHUNK):
            zs.append(_spatial_gate(wm, vb[q * CHUNK:(q + 1) * CHUNK, :], lane_head) + bias_ref[...])
        z = jnp.concatenate(zs, axis=0)
        ya = u * z
        ra = lax.rsqrt(_rowmean(ya * ya) + EPS)
        yah = ya * ra
        ch, rc = _layer_norm_stats(conv_ref[...])
        cn = ch * v_ref[3:4, :] + v_ref[4:5, :]
        sg = _sigmoid(cn)
        yb = cn * sg
        rb = lax.rsqrt(_rowmean(yb * yb) + EPS)
        ybh = yb * rb
        ycat_ref[...] = jnp.concatenate([yah * v_ref[5:6, :], ybh * v_ref[6:7, :]], axis=1).astype(BF16)
        dpa = dycat[:, 0:WA]
        dpb = dycat[:, WA:2 * WA]
        d_goa = _colsum(dpa * yah)
        d_gob = _colsum(dpb * ybh)
        dyah = dpa * v_ref[5:6, :]
        dybh = dpb * v_ref[6:7, :]
        dya = ra * (dyah - yah * _rowmean(dyah * yah))
        dyb = rb * (dybh - ybh * _rowmean(dybh * ybh))
        dpart_ref[:, 0:WA] = dya * z
        dz = dya * u

        @pl.when(i == 0)
        def _():
            gws_ref[...] = jnp.zeros((NHEAD, CHUNK, CHUNK), F32)
            dbs_acc[...] = jnp.zeros((CHUNK, WA), F32)
            vg_ref[...] = jnp.zeros((8, D), F32)
            v5g_ref[...] = jnp.zeros((8, WA), F32)

        dvs = []
        for q in range(tm // CHUNK):
            dz_q = dz[q * CHUNK:(q + 1) * CHUNK, :]
            vb_q = vb[q * CHUNK:(q + 1) * CHUNK, :]
            dbs_acc[...] += dz_q
            dzb = dz_q.astype(BF16)
            dv_q = jnp.zeros((CHUNK, WA), F32)
            for hd in range(NHEAD):
                dv_q = jnp.where(lane_head == hd, _dot_tn(wm[hd], dzb), dv_q)
                dz_hd = jnp.where(lane_head == hd, dz_q, 0.0).astype(BF16)
                gws_ref[hd] += _dot_nt(dz_hd, vb_q)
            dvs.append(dv_q)
        dv = jnp.concatenate(dvs, axis=0)
        d_gng = _colsum(dv * vh)
        d_gnb = _colsum(dv)
        dvh = dv * v_ref[0:1, :]
        dpart_ref[:, WA:2 * WA] = rv * (dvh - _rowmean(dvh) - vh * _rowmean(dvh * vh))
        dcn = dyb * (sg * (1.0 + cn * (1.0 - sg)))
        d_cng = _colsum(dcn * ch)
        d_cnb = _colsum(dcn)
        dch = dcn * v_ref[3:4, :]
        dconv = rc * (dch - _rowmean(dch) - ch * _rowmean(dch * ch))
        dpart_ref[:, 2 * WA:3 * WA] = dconv
        dpart_ref[:, 3 * WA:4 * WA] = jnp.zeros((tm, WA), F32)
        d_cb = _colsum(dconv)

        @pl.when(i % tps == 0)
        def _():
            mg_ref[...] = jnp.zeros((8, D), F32)

        mg_ref[2:3, :] += d_gt
        vg_ref[1:2, :] += d_gpost
        v5g_ref[0:1, :] += d_gng
        v5g_ref[1:2, :] += d_gnb
        v5g_ref[2:3, :] += d_cb
        v5g_ref[3:4, :] += d_cng
        v5g_ref[4:5, :] += d_cnb
        v5g_ref[5:6, :] += d_goa
        v5g_ref[6:7, :] += d_gob

        @pl.when(i == nt - 1)
        def _():
            row = lax.broadcasted_iota(jnp.int32, (CHUNK, CHUNK), 0)
            col = lax.broadcasted_iota(jnp.int32, (CHUNK, CHUNK), 1)
            for hd in range(NHEAD):
                gws_ref[hd] = jnp.where(col <= row, gws_ref[hd], 0.0)
            gbs_ref[...] = lax.dot_general(e_ref[...], dbs_acc[...], (((1,), (1,)), ((), ())),
                                           precision=lax.Precision.HIGHEST, preferred_element_type=F32)

    tile = pl.BlockSpec((tm, D), lambda i: (i, 0))
    ptile = pl.BlockSpec((tm, 4 * WA), lambda i: (i, 0))
    return _call(
        core, name=name, grid=(nt,), jobs=jobs,
        in_specs=[tile, tile, pl.BlockSpec((tm, 2 * WA), lambda i: (i, 0)), pl.BlockSpec((tm, WA), lambda i: (i, 0)),
                  pl.BlockSpec((None, 8, D), lambda i: (i // tps, 0, 0)), _const_spec((8, D)), _const_spec((D, D)),
                  _const_spec((8, WA)), _const_spec((NHEAD, CHUNK, CHUNK)), _const_spec((CHUNK, WA)),
                  _const_spec((8, WA))],
        out_specs=[ptile, tile, tile, pl.BlockSpec((None, 8, D), lambda i: (i // tps, 0, 0)),
                   pl.BlockSpec((8, D), lambda i: (0, 0)), pl.BlockSpec((8, WA), lambda i: (0, 0)),
                   pl.BlockSpec((NHEAD, CHUNK, CHUNK), lambda i: (0, 0, 0)), pl.BlockSpec((8, CHUNK), lambda i: (0, 0))],
        out_shape=[jax.ShapeDtypeStruct((T, 4 * WA), F32), jax.ShapeDtypeStruct((T, D), BF16),
                   jax.ShapeDtypeStruct((T, D), BF16), jax.ShapeDtypeStruct((nb, 8, D), F32),
                   jax.ShapeDtypeStruct((8, D), F32), jax.ShapeDtypeStruct((8, WA), F32),
                   jax.ShapeDtypeStruct((NHEAD, CHUNK, CHUNK), F32), jax.ShapeDtypeStruct((8, CHUNK), F32)],
        scratch=[pltpu.VMEM((CHUNK, WA), F32)],
        args=[dxo, ym, proj, conv, mod, gvec, w_mo, v512, ws, bias_full, esel])


def _mixer_bwd_b(dxo, x, dpart, proj, mod, gvec, w_mi, cw, tm, name, jobs=()):
    T = x.shape[0]
    nt = T // tm
    nb = mod.shape[0]
    tps = nt // nb
    hpt = tm // HALO
    nh = T // HALO
    off = HALO - (CONV_K - 1)
    p = _pitch(tm)
    ext_rows = 8 * p

    def core(ins, outs, scs):
        dxo_ref, x_ref, dpart_ref, dnext_ref, ag_ref, halo_ref, mod_ref, g_ref, wmi_ref, cw_ref = ins
        dx_ref, dproj_ref, hb_ref, mg_ref, vg_ref, dcw_ref = outs
        glu_ext, dconv_ext, dglu_scr, dcw_acc = scs
        i = pl.program_id(0)
        first = i % tps == 0
        last = i % tps == tps - 1
        a = ag_ref[:, 0:WA]
        g = ag_ref[:, WA:2 * WA]
        sgg = _sigmoid(g)

        @pl.when(i == 0)
        def _():
            glu_ext[:, HALO + tm:HALO + ext_rows, :] = jnp.zeros((4, ext_rows - tm, 128), F32)
            dconv_ext[:, HALO + tm:HALO + ext_rows, :] = jnp.zeros((4, ext_rows - tm, 128), F32)
            dcw_acc[...] = jnp.zeros((32, 8, WA), F32)
            vg_ref[...] = jnp.zeros((8, D), F32)

        _to_slabs(glu_ext, 0, jnp.where(first, 0.0, halo_ref[:, 0:WA] * _sigmoid(halo_ref[:, WA:2 * WA])))
        _to_slabs(glu_ext, HALO, a * sgg)
        _to_slabs(dconv_ext, 0, dpart_ref[:, 2 * WA:3 * WA])
        _to_slabs(dconv_ext, tm, jnp.where(last, 0.0, dnext_ref[...]))
        sub = lax.broadcasted_iota(jnp.int32, (8, 128), 0)
        for s in range(4):
            accs = [jnp.zeros((8, 128), F32)] * CONV_K
            for v in range(p):
                dc = jnp.where(v + p * sub < tm, dconv_ext[s, pl.ds(v, 8, stride=p), :], 0.0)
                for k in range(CONV_K):
                    accs[k] = accs[k] + dc * glu_ext[s, pl.ds(v + off + k, 8, stride=p), :]
            for k in range(CONV_K):
                dcw_acc[k, :, _lanes(s)] += accs[k]
        dglu = _tap_sum(dconv_ext, dglu_scr, cw_ref, jnp.zeros((1, WA), F32), tm, lambda k: (CONV_K - 1) - k)

        @pl.when(i == nt - 1)
        def _():
            for k in range(CONV_K):
                dcw_ref[k:k + 1, :] = jnp.sum(dcw_acc[k], axis=0, keepdims=True)
            dcw_ref[CONV_K:32, :] = jnp.zeros((32 - CONV_K, WA), F32)

        da = dglu * sgg
        dgg = dglu * a * (sgg * (1.0 - sgg))
        dproj_ref[:, 0:2 * WA] = dpart_ref[:, 0:2 * WA].astype(BF16)
        dproj_ref[:, 2 * WA:3 * WA] = da.astype(BF16)
        dproj_ref[:, 3 * WA:4 * WA] = dgg.astype(BF16)
        dh = jnp.zeros((tm, D), F32)
        for j in range(NDEV):
            dh = dh + _dot_nt(dproj_ref[:, j * MB:(j + 1) * MB], wmi_ref[j])
        xv = x_ref[...]
        sc, sh = mod_ref[1:2, :], mod_ref[0:1, :]
        gpre = g_ref[0:1, :]
        r = lax.rsqrt(_rowmean(xv * xv) + EPS)
        xh = xv * r
        n = xh * gpre
        hb_ref[...] = (n * (1.0 + sc) + sh).astype(BF16)
        d_sc = _colsum(dh * n)
        d_sh = _colsum(dh)
        dn = dh * (1.0 + sc)
        d_gpre = _colsum(dn * xh)
        dxh = dn * gpre
        dx_ref[...] = dxo_ref[...] + r * (dxh - xh * _rowmean(dxh * xh))

        @pl.when(first)
        def _():
            mg_ref[...] = jnp.zeros((8, D), F32)

        mg_ref[0:1, :] += d_sh
        mg_ref[1:2, :] += d_sc
        vg_ref[0:1, :] += d_gpre

    tile = pl.BlockSpec((tm, D), lambda i: (i, 0))
    return _call(
        core, name=name, grid=(nt,), jobs=jobs,
        in_specs=[tile, tile, pl.BlockSpec((tm, 4 * WA), lambda i: (i, 0)),
                  pl.BlockSpec((HALO, WA), lambda i: (jnp.minimum((i + 1) * hpt, nh - 1), 2)),
                  pl.BlockSpec((tm, 2 * WA), lambda i: (i, 1)),
                  pl.BlockSpec((HALO, 2 * WA), lambda i: (jnp.maximum(i * hpt - 1, 0), 1)),
                  pl.BlockSpec((None, 8, D), lambda i: (i // tps, 0, 0)), _const_spec((8, D)),
                  _const_spec((NDEV, D, MB)), _const_spec((32, WA))],
        out_specs=[tile, pl.BlockSpec((tm, 4 * WA), lambda i: (i, 0)), tile,
                   pl.BlockSpec((None, 8, D), lambda i: (i // tps, 0, 0)), pl.BlockSpec((8, D), lambda i: (0, 0)),
                   pl.BlockSpec((32, WA), lambda i: (0, 0))],
        out_shape=[jax.ShapeDtypeStruct((T, D), F32), jax.ShapeDtypeStruct((T, 4 * WA), BF16),
                   jax.ShapeDtypeStruct((T, D), BF16), jax.ShapeDtypeStruct((nb, 8, D), F32),
                   jax.ShapeDtypeStruct((8, D), F32), jax.ShapeDtypeStruct((32, WA), F32)],
        scratch=[pltpu.VMEM((4, HALO + ext_rows, 128), F32), pltpu.VMEM((4, HALO + ext_rows, 128), F32),
                 pltpu.VMEM((4, ext_rows, 128), F32), pltpu.VMEM((32, 8, WA), F32)],
        args=[dxo, x, dpart, dpart, proj, proj, mod, gvec, w_mi, cw])


def _grad_chip(a, b, a_spec, b_spec, prod_shape, half, name, jobs=()):
    steps = 8 if half is None else 4
    R = prod_shape[0] if half is None else half
    C = prod_shape[1]

    def core(ins, outs, scs):
        a_ref, b_ref = ins
        (o_ref,) = outs
        own, snd, rcv, ssem, rsem, lsem = scs
        s = pl.program_id(0)
        c = lax.axis_index("c")
        me = _me()
        sib = _flip(me, (0, 0, 1))
        prod = _dot_tn(a_ref[...], b_ref[...]).astype(BF16)
        if half is None:
            q = s // 2

            @pl.when(s % 2 == c)
            def _():
                own[q] = prod

            @pl.when(s % 2 != c)
            def _():
                snd[q] = prod
                _remote(snd.at[q], rcv.at[q], ssem.at[q], rsem.at[q], sib).start()
        else:
            lo = prod[0:half, :]
            hi = prod[half:2 * half, :]
            own[s] = jnp.where(c == 0, lo, hi)
            snd[s] = jnp.where(c == 0, hi, lo)
            _remote(snd.at[s], rcv.at[s], ssem.at[s], rsem.at[s], sib).start()

        @pl.when(s == steps - 1)
        def _():
            for q4 in range(4):
                cp = _remote(snd.at[q4], rcv.at[q4], ssem.at[q4], rsem.at[q4], sib)
                cp.wait_recv()
                cp.wait_send()
                snd[q4] = (own[q4].astype(F32) + rcv[q4].astype(F32)).astype(BF16)
            out = pltpu.make_async_copy(snd, o_ref, lsem)
            out.start()
            out.wait()

    return _call(
        core, name=name, grid=(steps,), jobs=jobs, in_specs=[a_spec, b_spec], out_specs=[HBM],
        out_shape=[jax.ShapeDtypeStruct((4, R, C), BF16)],
        scratch=[pltpu.VMEM((4, R, C), BF16), pltpu.VMEM((4, R, C), BF16), pltpu.VMEM((4, R, C), BF16),
                 pltpu.SemaphoreType.DMA((4,)), pltpu.SemaphoreType.DMA((4,)), pltpu.SemaphoreType.DMA],
        args=[a, b])


def _grad_w_in(dg, hb, name, jobs=()):
    T = hb.shape[0]
    return _grad_chip(dg, hb, pl.BlockSpec((None, T, FBP), lambda s: (s, 0, 0)), _const_spec((T, D)),
                      (FBP, D), None, name, jobs)


def _grad_w_out(act, dyb, name, jobs=()):
    T = dyb.shape[0]
    return _grad_chip(act, dyb, pl.BlockSpec((None, T, FBP), lambda s: (s, 0, 0)), _const_spec((T, D)),
                      (FBP, D), FO, name, jobs)


def _grad_w_mi(hb, dproj, name, jobs=()):
    T = hb.shape[0]
    return _grad_chip(hb, dproj, _const_spec((T, D)), pl.BlockSpec((T, MB), lambda s: (0, s)),
                      (D, MB), None, name, jobs)


def _grad_w_mo(ycat, dym, name, jobs=()):
    T = ycat.shape[0]
    return _grad_chip(ycat, dym, pl.BlockSpec((T, 2 * MO), lambda s: (0, s)), _const_spec((T, D)),
                      (2 * MO, D), MO, name, jobs)


def _adamw_math(w, g, m, v):
    m2 = ADAM_B1 * m + (1.0 - ADAM_B1) * g
    v2 = ADAM_B2 * v + (1.0 - ADAM_B2) * (g * g)
    m_hat = m2 / (1.0 - ADAM_B1 ** ADAM_STEP)
    v_hat = v2 / (1.0 - ADAM_B2 ** ADAM_STEP)
    delta = -ADAM_LR * (m_hat / (jnp.sqrt(v_hat) + ADAM_EPS) + ADAM_WD * w)
    return delta, m2, v2


def _adamw_reduce(parts, w, m, v, tr, name, own=None, after=None):
    R, C = w.shape

    def core(ins, outs, _):
        p_ref, w_ref, m_ref, v_ref = ins[:4]
        g_ref, d_ref, m2_ref, v2_ref = outs
        if own is None:
            terms = [p_ref[s].astype(F32) for s in range(4)]
        else:
            mq = 2 * lax.axis_index("x") + lax.axis_index("y")
            mine = ins[4][...].astype(F32)
            terms = [jnp.where(mq == s, mine, p_ref[s].astype(F32)) for s in range(4)]
        g = terms[0]
        for s in range(1, 4):
            g = g + terms[s]
        g_ref[...] = g
        d_ref[...], m2_ref[...], v2_ref[...] = _adamw_math(w_ref[...], g, m_ref[...], v_ref[...])

    blk = pl.BlockSpec((tr, C), lambda i: (i, 0))
    in_specs = [pl.BlockSpec((4, tr, C), lambda i: (0, i, 0)), blk, blk, blk]
    args = [parts, w, m, v]
    if own is not None:
        mq = 2 * lax.axis_index("x") + lax.axis_index("y")
        in_specs.append(pl.BlockSpec((tr, C), lambda i: (i, 0)))
        args.append(lax.dynamic_index_in_dim(own, mq, 0, keepdims=False))
    if after is not None:
        in_specs.append(HBM)
        args.append(after)
    return _call(
        core, name=name, grid=(R // tr,), in_specs=in_specs,
        out_specs=[blk, blk, blk, blk], out_shape=[jax.ShapeDtypeStruct((R, C), F32)] * 4, args=args)[0]


HBM_ONLY = pl.BlockSpec(memory_space=pltpu.HBM)
SEM = pl.BlockSpec(memory_space=pltpu.SEMAPHORE)
EFFECT = pltpu.SideEffectType.DATAFLOW_SIDE_EFFECTING


def _chip_scatter_start(gs, name):
    n = len(gs)

    def body(*refs):
        g_refs, land_refs = refs[:n], refs[n:2 * n]
        ssem, rsem = refs[2 * n:2 * n + 2]
        token = refs[-1]
        me = _me()
        mq = 2 * me[0] + me[1]
        for k, f in enumerate(CHIP_FLIPS):
            p = _flip(me, f)
            for a in range(n):
                _remote(g_refs[a].at[2 * p[0] + p[1]], land_refs[a].at[mq], ssem.at[3 * a + k], rsem.at[3 * a + k], p).start()
        token[...] = jnp.zeros_like(token)

    gs = [pltpu.with_memory_space_constraint(g, pltpu.HBM) for g in gs]
    lands = [pltpu.with_memory_space_constraint(lax.empty(g.shape, g.dtype), pltpu.HBM) for g in gs]
    res = pl.pallas_call(
        body, name=name,
        out_shape=(pltpu.SemaphoreType.DMA((3 * n,)), pltpu.SemaphoreType.DMA((3 * n,)))
        + tuple(pltpu.HBM(g.shape, g.dtype) for g in gs) * 2 + (jax.ShapeDtypeStruct((8, 128), F32),),
        in_specs=(HBM_ONLY,) * (2 * n), out_specs=(SEM, SEM) + (HBM_ONLY,) * (2 * n) + (VM,),
        input_output_aliases={a: 2 + a for a in range(2 * n)},
        compiler_params=pltpu.CompilerParams(has_side_effects=EFFECT),
    )(*gs, *lands)
    return res[:-1], res[-1]


def _chip_scatter_wait(handle, after, name):
    ssem, rsem = handle[:2]
    n = (len(handle) - 2) // 2
    thru = handle[2:]

    def body(*refs):
        g_refs, land_refs = refs[:n], refs[n:2 * n]
        ssem, rsem = refs[2 * n:2 * n + 2]
        me = _me()
        mq = 2 * me[0] + me[1]
        for k, f in enumerate(CHIP_FLIPS):
            p = _flip(me, f)
            pq = 2 * p[0] + p[1]
            for a in range(n):
                _remote(g_refs[a].at[pq], land_refs[a].at[mq], ssem.at[3 * a + k], rsem.at[3 * a + k], p).wait_send()
                _remote(g_refs[a].at[mq], land_refs[a].at[pq], ssem.at[3 * a + k], rsem.at[3 * a + k], p).wait_recv()

    res = pl.pallas_call(
        body, name=name,
        out_shape=tuple(pltpu.HBM(t.shape, t.dtype) for t in thru),
        in_specs=(HBM_ONLY,) * (2 * n) + (SEM, SEM, HBM), out_specs=(HBM_ONLY,) * (2 * n),
        input_output_aliases={a: a for a in range(2 * n)},
        compiler_params=pltpu.CompilerParams(has_side_effects=EFFECT),
    )(*thru, ssem, rsem, after)
    return list(res[:n]), list(res[n:])


def _adamw_ada(sc_all, dd, w, m, v, tr, name, after=None):
    R, C = w.shape

    def core(ins, outs, _):
        sc_ref, dd_ref, w_ref, m_ref, v_ref = ins[:5]
        g_ref, d_ref, m2_ref, v2_ref = outs
        g = _dot_tn(sc_ref[...].astype(BF16), dd_ref[...].astype(BF16))
        g_ref[...] = g
        d_ref[...], m2_ref[...], v2_ref[...] = _adamw_math(w_ref[...], g, m_ref[...], v_ref[...])

    blk = pl.BlockSpec((tr, C), lambda i: (i, 0))
    return _call(
        core, name=name, grid=(R // tr,),
        in_specs=[pl.BlockSpec((64, tr), lambda i: (0, i)), pl.BlockSpec((64, C), lambda i: (0, 0)), blk, blk, blk]
        + [HBM] * (after is not None),
        out_specs=[blk, blk, blk, blk], out_shape=[jax.ShapeDtypeStruct((R, C), F32)] * 4,
        args=[sc_all, dd, w, m, v] + [after] * (after is not None))[0]


def _adamw_small(gathered, plain, grads, wmv, emit, name):
    nw = len(grads)
    ng, npl, ne = len(gathered), len(plain), len(emit)

    def core(ins, outs, _):
        srcs = []
        for a in range(ng):
            s = ins[a][0]
            for dev in range(1, NDEV):
                s = s + ins[a][dev]
            srcs.append(s)
        srcs += [ins[ng + a][...] for a in range(npl)]
        w_refs = ins[ng + npl:]
        for e, a in enumerate(emit):
            outs[e][...] = srcs[a]
        for t in range(nw):
            src, row = grads[t]
            g = srcs[src] if row is None else srcs[src][row:row + 1, :]
            w_ref, m_ref, v_ref = w_refs[3 * t:3 * t + 3]
            g_ref, d_ref, m2_ref, v2_ref = outs[ne + 4 * t:ne + 4 * t + 4]
            g_ref[...] = g
            d_ref[...], m2_ref[...], v2_ref[...] = _adamw_math(w_ref[...], g, m_ref[...], v_ref[...])

    out_shape = [jax.ShapeDtypeStruct(gathered[a].shape[1:], F32) for a in emit]
    for t in range(nw):
        out_shape += [jax.ShapeDtypeStruct(wmv[3 * t].shape, F32)] * 4
    return _call(
        core, name=name, grid=(), in_specs=[VM] * (ng + npl + 3 * nw), out_specs=[VM] * (ne + 4 * nw),
        out_shape=out_shape, args=list(gathered) + list(plain) + list(wmv))[0]


def _ada_fwd(c_pad, w_ada, b_cols, cw_pad, jobs=()):
    def core(ins, outs, scs):
        c_ref, w_ref, b_ref, cwp_ref = ins
        ada_ref, sc_ref, cw_ref = outs
        cbuf, send_buf, ssem, rsem = scs
        me = _me()
        mi = _lin(me)
        cbuf[mi] = c_ref[...]
        cw_ref[mi] = cwp_ref[...]
        peers = [_flip(me, f) for f in FLIPS]
        first = []
        for k, p in enumerate(peers):
            first.append(_remote(cbuf.at[mi], cbuf.at[mi], ssem.at[k], rsem.at[k], p))
            first.append(_remote(cw_ref.at[mi], cw_ref.at[mi], ssem.at[7 + k], rsem.at[7 + k], p))
        for cp in first:
            cp.start()
        for k, p in enumerate(peers):
            pi = _lin(p)
            _remote(cbuf.at[pi], cbuf.at[pi], ssem.at[k], rsem.at[k], p).wait_recv()
            _remote(cw_ref.at[pi], cw_ref.at[pi], ssem.at[7 + k], rsem.at[7 + k], p).wait_recv()
        c_all = cbuf[...].reshape(8 * 8, D)
        sc = c_all * _sigmoid(c_all)
        sc_ref[...] = sc
        res = _dot(sc.astype(BF16), w_ref[...].astype(BF16)) + b_ref[...]
        send_buf[...] = res.reshape(8, 8, ADA_B)
        ada_ref[mi] = send_buf[mi]
        second = []
        for k, p in enumerate(peers):
            second.append(_remote(send_buf.at[_lin(p)], ada_ref.at[mi], ssem.at[14 + k], rsem.at[14 + k], p))
        for cp in second:
            cp.start()
        for k, p in enumerate(peers):
            _remote(send_buf.at[mi], ada_ref.at[_lin(p)], ssem.at[14 + k], rsem.at[14 + k], p).wait_recv()
        for cp in first + second:
            cp.wait_send()

    return _call(
        core, name="ada_fwd", grid=(), jobs=jobs, in_specs=[VM, VM, VM, VM], out_specs=[VM, VM, VM],
        out_shape=[jax.ShapeDtypeStruct((8, 8, ADA_B), F32), jax.ShapeDtypeStruct((64, D), F32),
                   jax.ShapeDtypeStruct((8, 32, 64), F32)],
        scratch=[pltpu.VMEM((8, 8, D), F32), pltpu.VMEM((8, 8, ADA_B), F32),
                 pltpu.SemaphoreType.DMA((21,)), pltpu.SemaphoreType.DMA((21,))],
        args=[c_pad, w_ada, b_cols, cw_pad])


def _ada_bwd(dada, jobs=()):
    def core(ins, outs, scs):
        (d_ref,) = ins
        dd_ref, gb_ref = outs
        rbuf, ssem, rsem = scs
        me = _me()
        mi = _lin(me)
        peers = [_flip(me, f) for f in FLIPS]
        rbuf[mi] = d_ref[mi]
        first = []
        for k, p in enumerate(peers):
            first.append(_remote(d_ref.at[_lin(p)], rbuf.at[mi], ssem.at[k], rsem.at[k], p))
        for cp in first:
            cp.start()
        for k, p in enumerate(peers):
            _remote(d_ref.at[mi], rbuf.at[_lin(p)], ssem.at[k], rsem.at[k], p).wait_recv()
        dd = rbuf[...].reshape(64, ADA_B)
        dd_ref[...] = dd
        gb_ref[mi] = jnp.broadcast_to(_colsum(dd), (8, ADA_B))
        second = []
        for k, p in enumerate(peers):
            second.append(_remote(gb_ref.at[mi], gb_ref.at[mi], ssem.at[7 + k], rsem.at[7 + k], p))
        for cp in second:
            cp.start()
        for k, p in enumerate(peers):
            pi = _lin(p)
            _remote(gb_ref.at[pi], gb_ref.at[pi], ssem.at[7 + k], rsem.at[7 + k], p).wait_recv()
        for cp in first + second:
            cp.wait_send()

    return _call(
        core, name="ada_bwd", grid=(), jobs=jobs, in_specs=[VM], out_specs=[VM, VM],
        out_shape=[jax.ShapeDtypeStruct((64, ADA_B), F32), jax.ShapeDtypeStruct((8, 8, ADA_B), F32)],
        scratch=[pltpu.VMEM((8, 8, ADA_B), F32), pltpu.SemaphoreType.DMA((14,)), pltpu.SemaphoreType.DMA((14,))],
        args=[dada])


SMALL_D = ("g_pre_f1", "g_post_f1", "g_pre_m", "g_post_m", "g_pre_f2", "g_post_f2")
SMALL_W = ("gmlp_norm_g", "gmlp_norm_b", "conv_b", "conv_norm_g", "conv_norm_b", "g_out_a", "g_out_b")


def kernel(x, c, w_ada, b_ada, g_pre_f1, g_post_f1, w_f1_in, w_f1_out, g_pre_m, g_post_m, w_mix_in, gmlp_norm_g, gmlp_norm_b, w_spatial, b_spatial, conv_w, conv_b, conv_norm_g, conv_norm_b, g_out_a, g_out_b, w_mix_out, g_pre_f2, g_post_f2, w_f2_in, w_f2_out, loss_target, m_w_ada, m_b_ada, m_g_pre_f1, m_g_post_f1, m_w_f1_in, m_w_f1_out, m_g_pre_m, m_g_post_m, m_w_mix_in, m_gmlp_norm_g, m_gmlp_norm_b, m_w_spatial, m_b_spatial, m_conv_w, m_conv_b, m_conv_norm_g, m_conv_norm_b, m_g_out_a, m_g_out_b, m_w_mix_out, m_g_pre_f2, m_g_post_f2, m_w_f2_in, m_w_f2_out, v_w_ada, v_b_ada, v_g_pre_f1, v_g_post_f1, v_w_f1_in, v_w_f1_out, v_g_pre_m, v_g_post_m, v_w_mix_in, v_gmlp_norm_g, v_gmlp_norm_b, v_w_spatial, v_b_spatial, v_conv_w, v_conv_b, v_conv_norm_g, v_conv_norm_b, v_g_out_a, v_g_out_b, v_w_mix_out, v_g_pre_f2, v_g_post_f2, v_w_f2_in, v_w_f2_out):
    given = dict(locals())
    bl, seq, _ = x.shape
    T = bl * seq
    tm = min(256, seq // 2)
    mi = _lin((lax.axis_index("x"), lax.axis_index("y"), lax.axis_index("c")))

    def shard_in(w):
        return jnp.pad(w[0].T.astype(BF16), ((0, FBP - FB), (0, 0)))

    zpad = jnp.zeros((max(FBP - FB, 16), D), BF16)
    g_f1 = _Gather([shard_in(w_f1_in), w_f1_out[0].astype(BF16)], ("rows", "out"), zpad)
    g_mx = _Gather([w_mix_in[0].astype(BF16), w_mix_out[0].astype(BF16), w_f2_out[0].astype(BF16)],
                   ("rows", "rows", "out"), zpad)
    g_f2 = _Gather([shard_in(w_f2_in)], ("rows",), zpad, late_mid=True)

    c_pad = jnp.pad(c, ((0, 8 - bl), (0, 0)))
    b_cols = lax.dynamic_slice(b_ada, (0, mi * ADA_B), (1, ADA_B))
    cw_pad = jnp.pad(conv_w[0], ((0, 1), (0, 0)))
    (ada_blk, sc_all, cw_all), ((wi1, wo1),) = _ada_fwd(c_pad, w_ada[0], b_cols, cw_pad, jobs=[g_f1])
    ada = ada_blk[:, 0:bl, :].transpose(1, 0, 2).reshape(bl, 9, D)
    pad5 = jnp.zeros((bl, 5, D), F32)
    mod1 = jnp.concatenate([ada[:, 0:3], pad5], axis=1)
    mod2 = jnp.concatenate([ada[:, 3:6], pad5], axis=1)
    mod3 = jnp.concatenate([ada[:, 6:9], pad5], axis=1)
    cw_full = cw_all.transpose(1, 0, 2).reshape(32, WA)

    zrow = jnp.zeros((1, D), F32)
    gv1 = jnp.concatenate([g_pre_f1, g_post_f1] + [zrow] * 6, axis=0)
    gvm = jnp.concatenate([g_pre_m, g_post_m] + [zrow] * 6, axis=0)
    gv2 = jnp.concatenate([g_pre_f2, g_post_f2] + [zrow] * 6, axis=0)
    v512 = jnp.concatenate([gmlp_norm_g, gmlp_norm_b, conv_b, conv_norm_g, conv_norm_b, g_out_a, g_out_b,
                            jnp.zeros((1, WA), F32)], axis=0)
    ws = w_spatial[0]
    bias_full = jnp.repeat(b_spatial[0].T, HD, axis=1)
    esel = (lax.broadcasted_iota(jnp.int32, (8, WA), 1) // HD == lax.broadcasted_iota(jnp.int32, (8, WA), 0)).astype(F32)

    x0 = x.reshape(T, D)
    (x1, gu1, y1), ((wmi, wmo, wo2),) = _ffn_fwd(x0, mod1, gv1, wi1, wo1, tm, "ffn1_fwd", jobs=[g_mx])
    wmo = wmo.reshape(D, D)
    (x2, proj, ym, conv), ((wi2,),) = _mixer_fwd(x1, mod2, gvm, wmi, wmo, v512, ws, bias_full, cw_full, tm, "mixer_fwd", jobs=[g_f2])
    (dx3, gu2, y2, loss_blk), _ = _ffn_fwd(x2, mod3, gv2, wi2, wo2, tm, "ffn2_fwd", target=loss_target.reshape(T, D))

    (dx2, dg2, act2, hb2, dyb2, mg3, vg3), _ = _ffn_bwd(dx3, x2, y2, gu2, mod3, gv2, wi2, wo2, tm, "ffn2_bwd")
    (g_wi2,), _ = _grad_w_in(dg2, hb2, "ffn2_gw_in")
    (g_wo2,), _ = _grad_w_out(act2, dyb2, "ffn2_gw_out")
    h_f2, tok = _chip_scatter_start([g_wi2, g_wo2], "f2_start")
    mod2b = mod2 + tok[0, 0]
    (dpart, dymb, ycat, mg2a, vgma, v5g, gws, gbs), _ = _mixer_bwd_a(
        dx2, ym, proj, conv, mod2b, gvm, wmo, v512, ws, bias_full, esel, tm, "mixer_bwd_a")
    (dx1, dproj, hbm, mg2b, vgmb, dcw), _ = _mixer_bwd_b(dx2, x1, dpart, proj, mod2b, gvm, wmi, cw_full, tm, "mixer_bwd_b")
    (g_wmi,), _ = _grad_w_mi(hbm, dproj, "mixer_gw_in")
    (g_wmo,), _ = _grad_w_mo(ycat, dymb, "mixer_gw_out")
    h_mx, tok = _chip_scatter_start([g_wmi, g_wmo], "mixer_start")
    p2 = jnp.concatenate([v5g, dcw], axis=0)
    (dx0, dg1, act1, hb1, dyb1, mg1, vg1), _ = _ffn_bwd(dx1, x0, y1, gu1, mod1 + tok[0, 0], gv1, wi1, wo1, tm, "ffn1_bwd")

    dada = jnp.concatenate([mg1[:, 0:3], mg2b[:, 0:2], mg2a[:, 2:3], mg3[:, 0:3]], axis=1)
    dada = dada.reshape(bl, NDEV, ADA_B).transpose(1, 0, 2)
    dada = jnp.pad(dada, ((0, 0), (0, 8 - bl), (0, 0)))
    p1 = jnp.concatenate([vg1[0:2], vgmb[0:1], vgma[1:2], vg3[0:2], loss_blk[0:1], zrow], axis=0)
    (dd_all, gb_all), ((a1,),) = _ada_bwd(dada, jobs=[_AllGather([p1])])
    g_bada = gb_all[:, 0, :].reshape(1, 9 * D)

    (g_wo1,), _ = _grad_w_out(act1, dyb1, "ffn1_gw_out")
    (g_wi1,), ((a2, a3, a4),) = _grad_w_in(dg1, hb1, "ffn1_gw_in", jobs=[_Gather([p2, gws, gbs], ("rows",) * 3, zpad)])
    h_f1, tok = _chip_scatter_start([g_wi1, g_wo1], "f1_start")

    res = {}

    def adamw_t(nm, own, part, after):
        quad = _adamw_reduce(part, given[nm][0].T, given["m_" + nm][0].T, given["v_" + nm][0].T, FO, "adamw_" + nm,
                             own=own, after=after)
        res[nm] = tuple(t.T[None] for t in quad)
        return quad[1]

    def adamw_n(nm, own, part, tr, after):
        quad = _adamw_reduce(part, given[nm][0], given["m_" + nm][0], given["v_" + nm][0], tr, "adamw_" + nm,
                             own=own, after=after)
        res[nm] = tuple(t[None] for t in quad)
        return quad[1]

    (g_wi2, g_wo2), (p_wi2, p_wo2) = _chip_scatter_wait(h_f2, tok, "f2_wait")
    dep = adamw_t("w_f2_in", g_wi2, p_wi2, tok)
    dep = adamw_n("w_f2_out", g_wo2, p_wo2, FO, dep)
    (g_wmi, g_wmo), (p_wmi, p_wmo) = _chip_scatter_wait(h_mx, dep, "mixer_wait")
    dep = adamw_n("w_mix_in", g_wmi, p_wmi, 256, dep)
    dep = adamw_n("w_mix_out", g_wmo, p_wmo, MO, dep)
    quad = _adamw_ada(sc_all, dd_all, w_ada[0], m_w_ada[0], v_w_ada[0], 256, "adamw_w_ada", after=dep)
    res["w_ada"] = tuple(t[None] for t in quad)
    (g_wi1, g_wo1), (p_wi1, p_wo1) = _chip_scatter_wait(h_f1, quad[1], "f1_wait")
    dep = adamw_n("w_f1_out", g_wo1, p_wo1, FO, quad[1])
    adamw_t("w_f1_in", g_wi1, p_wi1, dep)

    small = SMALL_D + SMALL_W + ("w_spatial", "b_spatial", "b_ada")
    grads = [(0, r) for r in range(6)] + [(1, r) for r in range(7)] + [(2, None), (3, None), (4, None)]
    wmv = []
    for nm in small:
        for pre in ("", "m_", "v_"):
            wmv.append(given[pre + nm][0] if nm in ("w_spatial", "b_spatial") else given[pre + nm])
    outs = _adamw_small([a1, a2, a3, a4], [g_bada], grads, wmv, (0, 1), "adamw_small")
    loss = outs[0][6, 0]
    for t, nm in enumerate(small):
        quad = outs[2 + 4 * t:6 + 4 * t]
        res[nm] = tuple(q[None] for q in quad) if nm in ("w_spatial", "b_spatial") else tuple(quad)
    g_cw = lax.dynamic_slice(outs[1], (8, mi * 64), (32, 64))
    wmv = [jnp.pad(given[pre + "conv_w"][0], ((0, 1), (0, 0)), constant_values=1.0 if pre == "v_" else 0.0)
           for pre in ("", "m_", "v_")]
    quad = _adamw_small([], [g_cw], [(0, None)], wmv, (), "adamw_conv_w")
    res["conv_w"] = tuple(q[0:CONV_K][None] for q in quad)

    order = ["w_ada", "b_ada", "g_pre_f1", "g_post_f1", "w_f1_in", "w_f1_out", "g_pre_m", "g_post_m", "w_mix_in",
             "gmlp_norm_g", "gmlp_norm_b", "w_spatial", "b_spatial", "conv_w", "conv_b", "conv_norm_g", "conv_norm_b",
             "g_out_a", "g_out_b", "w_mix_out", "g_pre_f2", "g_post_f2", "w_f2_in", "w_f2_out"]
    out = [loss, dx0.reshape(bl, seq, D)]
    for k in range(4):
        out += [res[nm][k] for nm in order]
    return tuple(out)
```

```python
import jax
import jax.numpy as jnp
from jax import lax
from jax.experimental import pallas as pl
from jax.experimental.pallas import tpu as pltpu

F32 = jnp.float32
BF16 = jnp.bfloat16

D = 1024
DFF = 2816
NDEV = 8
FB = 2 * DFF // NDEV
FBP = 704
FO = DFF // NDEV
WA = 512
NHEAD = 8
HD = 64
CHUNK = 128
CONV_K = 31
HALO = 32
MB = 2 * (WA + WA) // NDEV
MO = D // NDEV
ADA_B = 9 * D // NDEV
EPS = 1e-6
HALF = 0.5

ADAM_LR = 0.001
ADAM_B1 = 0.9
ADAM_B2 = 0.999
ADAM_EPS = 1e-08
ADAM_WD = 0.01
ADAM_STEP = 10

VMEM_LIMIT = 56 * 1024 * 1024
MESH = pl.DeviceIdType.MESH
FLIPS = ((0, 0, 1), (1, 0, 0), (0, 1, 0), (1, 1, 0), (1, 0, 1), (0, 1, 1), (1, 1, 1))
CHIP_FLIPS = ((1, 0, 0), (0, 1, 0), (1, 1, 0))
HBM = pl.BlockSpec(memory_space=pl.ANY)
VM = pl.BlockSpec(memory_space=pltpu.VMEM)


def _dot(a, b):
    return lax.dot_general(a, b, (((1,), (0,)), ((), ())), preferred_element_type=F32)


def _dot_nt(a, b):
    return lax.dot_general(a, b, (((1,), (1,)), ((), ())), preferred_element_type=F32)


def _dot_tn(a, b):
    return lax.dot_general(a, b, (((0,), (0,)), ((), ())), preferred_element_type=F32)


def _rowmean(v):
    return jnp.mean(v, axis=-1, keepdims=True)


def _colsum(v):
    return jnp.sum(v, axis=0, keepdims=True)


def _sigmoid(v):
    return 0.5 * jnp.tanh(0.5 * v) + 0.5


def _const_spec(shape):
    nd = len(shape)
    return pl.BlockSpec(shape, lambda *_: (0,) * nd, pipeline_mode=pl.Buffered(1))


def _me():
    return lax.axis_index("x"), lax.axis_index("y"), lax.axis_index("c")


def _flip(me, f):
    return tuple(1 - v if b else v for v, b in zip(me, f))


def _lin(p):
    return 4 * p[0] + 2 * p[1] + p[2]


def _remote(src, dst, send_sem, recv_sem, dev):
    return pltpu.make_async_remote_copy(src_ref=src, dst_ref=dst, send_sem=send_sem, recv_sem=recv_sem,
                                        device_id=dev, device_id_type=MESH)


def _blk(kind, ref, p):
    if kind == "out":
        return ref.at[2 * p[0] + p[1], pl.ds(p[2] * FO, FO), :]
    return ref.at[_lin(p)]


class _Gather:
    def __init__(self, shards, kinds, zpad, late_mid=False):
        self.late_mid = late_mid
        self.kinds = kinds
        self.n = len(shards)
        self.ins = list(shards) + [zpad]
        self.out_shape = [jax.ShapeDtypeStruct((4, FBP, D) if k == "out" else (NDEV,) + s.shape, s.dtype)
                          for s, k in zip(shards, kinds)]
        self.n_out = sum(k == "out" for k in kinds)
        self.sems = [pltpu.SemaphoreType.DMA((7 * self.n,)), pltpu.SemaphoreType.DMA((7 * self.n,)),
                     pltpu.SemaphoreType.DMA((self.n + 4 * max(self.n_out, 1),))]

    def _first(self, ins, outs, sems):
        ssem, rsem, lsem = sems
        me = _me()
        sib = _flip(me, (0, 0, 1))
        cps, loc = [], []
        nz = 0
        for a in range(self.n):
            mine = _blk(self.kinds[a], outs[a], me)
            loc.append(pltpu.make_async_copy(ins[a], mine, lsem.at[a]))
            if self.kinds[a] == "out" and FBP > FB:
                for q in range(4):
                    loc.append(pltpu.make_async_copy(ins[self.n], outs[a].at[q, pl.ds(FB, FBP - FB), :],
                                                     lsem.at[self.n + 4 * nz + q]))
                nz += 1
            cps.append(_remote(ins[a], mine, ssem.at[7 * a], rsem.at[7 * a], sib))
            for j, f in enumerate(CHIP_FLIPS):
                cps.append(_remote(ins[a], mine, ssem.at[7 * a + 1 + j], rsem.at[7 * a + 1 + j], _flip(me, f)))
        return cps, loc

    def _passed(self, outs, sems):
        ssem, rsem, _ = sems
        me = _me()
        sib = _flip(me, (0, 0, 1))
        cps = []
        for j, f in enumerate(CHIP_FLIPS):
            for a in range(self.n):
                blk = _blk(self.kinds[a], outs[a], _flip(me, f))
                cps.append(_remote(blk, blk, ssem.at[7 * a + 4 + j], rsem.at[7 * a + 4 + j], sib))
        return cps

    def start(self, ins, outs, sems):
        cps, loc = self._first(ins, outs, sems)
        for cp in loc + cps:
            cp.start()

    def mid(self, ins, outs, sems):
        ssem, rsem, _ = sems
        me = _me()
        passed = self._passed(outs, sems)
        t = 0
        for j, f in enumerate(CHIP_FLIPS):
            for a in range(self.n):
                blk = _blk(self.kinds[a], outs[a], _flip(me, f))
                _remote(blk, blk, ssem.at[7 * a + 1 + j], rsem.at[7 * a + 1 + j], _flip(me, f)).wait_recv()
                passed[t].start()
                t += 1

    def end(self, ins, outs, sems):
        ssem, rsem, _ = sems
        me = _me()
        sib = _flip(me, (0, 0, 1))
        for a in range(self.n):
            blk = _blk(self.kinds[a], outs[a], sib)
            _remote(blk, blk, ssem.at[7 * a], rsem.at[7 * a], sib).wait_recv()
            for j, f in enumerate(CHIP_FLIPS):
                blk = _blk(self.kinds[a], outs[a], _flip(_flip(me, f), (0, 0, 1)))
                _remote(blk, blk, ssem.at[7 * a + 4 + j], rsem.at[7 * a + 4 + j], sib).wait_recv()
        cps, loc = self._first(ins, outs, sems)
        for cp in cps + self._passed(outs, sems):
            cp.wait_send()
        for cp in loc:
            cp.wait()


class _ChipScatter:
    def __init__(self, grads):
        self.n = len(grads)
        self.ins = list(grads)
        self.out_shape = [jax.ShapeDtypeStruct(g.shape, BF16) for g in grads]
        self.sems = [pltpu.SemaphoreType.DMA((3 * self.n,)), pltpu.SemaphoreType.DMA((3 * self.n,)),
                     pltpu.SemaphoreType.DMA((self.n,))]

    def _copies(self, ins, outs, sems):
        ssem, rsem, lsem = sems
        me = _me()
        mq = 2 * me[0] + me[1]
        loc = [pltpu.make_async_copy(ins[a].at[mq], outs[a].at[mq], lsem.at[a]) for a in range(self.n)]
        cps = []
        for k, f in enumerate(CHIP_FLIPS):
            p = _flip(me, f)
            for a in range(self.n):
                cps.append(_remote(ins[a].at[2 * p[0] + p[1]], outs[a].at[mq], ssem.at[3 * a + k], rsem.at[3 * a + k], p))
        return cps, loc

    def start(self, ins, outs, sems):
        cps, loc = self._copies(ins, outs, sems)
        for cp in loc + cps:
            cp.start()

    mid = None

    def end(self, ins, outs, sems):
        ssem, rsem, _ = sems
        me = _me()
        mq = 2 * me[0] + me[1]
        for k, f in enumerate(CHIP_FLIPS):
            p = _flip(me, f)
            for a in range(self.n):
                _remote(ins[a].at[mq], outs[a].at[2 * p[0] + p[1]], ssem.at[3 * a + k], rsem.at[3 * a + k], p).wait_recv()
        cps, loc = self._copies(ins, outs, sems)
        for cp in cps:
            cp.wait_send()
        for cp in loc:
            cp.wait()


class _AllGather:
    def __init__(self, parts):
        self.n = len(parts)
        self.ins = list(parts)
        self.out_shape = [jax.ShapeDtypeStruct((NDEV,) + p.shape, p.dtype) for p in parts]
        self.sems = [pltpu.SemaphoreType.DMA((7 * self.n,)), pltpu.SemaphoreType.DMA((7 * self.n,)),
                     pltpu.SemaphoreType.DMA((self.n,))]

    def _copies(self, ins, outs, sems):
        ssem, rsem, lsem = sems
        me = _me()
        mi = _lin(me)
        loc = [pltpu.make_async_copy(ins[a], outs[a].at[mi], lsem.at[a]) for a in range(self.n)]
        cps = []
        for k, f in enumerate(FLIPS):
            for a in range(self.n):
                cps.append(_remote(ins[a], outs[a].at[mi], ssem.at[7 * a + k], rsem.at[7 * a + k], _flip(me, f)))
        return cps, loc

    def start(self, ins, outs, sems):
        cps, loc = self._copies(ins, outs, sems)
        for cp in loc + cps:
            cp.start()

    mid = None

    def end(self, ins, outs, sems):
        ssem, rsem, _ = sems
        me = _me()
        for k, f in enumerate(FLIPS):
            p = _flip(me, f)
            for a in range(self.n):
                _remote(ins[a], outs[a].at[_lin(p)], ssem.at[7 * a + k], rsem.at[7 * a + k], p).wait_recv()
        cps, loc = self._copies(ins, outs, sems)
        for cp in cps:
            cp.wait_send()
        for cp in loc:
            cp.wait()


def _call(core, *, name, grid, in_specs, out_specs, out_shape, args, scratch=(), jobs=()):
    n_in, n_out, n_sc = len(in_specs), len(out_specs), len(scratch)
    steps = 1
    for g in grid:
        steps *= g

    def body(*refs):
        pos = [0]

        def take(k):
            r = refs[pos[0]:pos[0] + k]
            pos[0] += k
            return r

        ins = take(n_in)
        j_ins = [take(len(j.ins)) for j in jobs]
        outs = take(n_out)
        j_outs = [take(len(j.out_shape)) for j in jobs]
        scs = take(n_sc)
        j_sems = [take(len(j.sems)) for j in jobs]
        if len(grid) == 2:
            step = pl.program_id(0) * grid[1] + pl.program_id(1)
        elif len(grid) == 1:
            step = pl.program_id(0)
        else:
            step = 0
        for j, ji, jo, js in zip(jobs, j_ins, j_outs, j_sems):
            if grid:
                pl.when(step == 0)(lambda j=j, ji=ji, jo=jo, js=js: j.start(ji, jo, js))
            else:
                j.start(ji, jo, js)
        for j, ji, jo, js in zip(jobs, j_ins, j_outs, j_sems):
            if j.mid is not None and grid:
                at = max(steps - 2, 0) if j.late_mid else (3 * steps) // 4
                pl.when(step == at)(lambda j=j, ji=ji, jo=jo, js=js: j.mid(ji, jo, js))
        if core is not None:
            core(ins, outs, scs)
        for j, ji, jo, js in zip(jobs, j_ins, j_outs, j_sems):
            if grid:
                pl.when(step == steps - 1)(lambda j=j, ji=ji, jo=jo, js=js: j.end(ji, jo, js))
            else:
                if j.mid is not None:
                    j.mid(ji, jo, js)
                j.end(ji, jo, js)

    all_in = list(in_specs)
    all_args = list(args)
    all_out = list(out_specs)
    all_shape = list(out_shape)
    all_sc = list(scratch)
    for j in jobs:
        all_in += [HBM] * len(j.ins)
        all_args += j.ins
    for j in jobs:
        all_out += [HBM] * len(j.out_shape)
        all_shape += j.out_shape
        all_sc += j.sems
    params = dict(vmem_limit_bytes=VMEM_LIMIT)
    if grid:
        params["dimension_semantics"] = ("arbitrary",) * len(grid)
    res = pl.pallas_call(
        body, name=name, grid=grid, in_specs=all_in, out_specs=all_out, out_shape=all_shape,
        scratch_shapes=all_sc, compiler_params=pltpu.CompilerParams(**params),
    )(*all_args)
    core_res = list(res[:n_out])
    job_res = []
    pos = n_out
    for j in jobs:
        job_res.append(list(res[pos:pos + len(j.out_shape)]))
        pos += len(j.out_shape)
    return core_res, job_res


def _ffn_fwd(x, mod, gvec, w_in, w_out, tm, name, jobs=()):
    T = x.shape[0]
    nt = T // tm
    tps = nt // mod.shape[0]

    def core(ins, outs, _):
        x_ref, mod_ref, g_ref, win_ref, wout_ref = ins
        xo_ref, gu_ref, y_ref = outs
        xv = x_ref[...]
        sh, sc, gt = mod_ref[0:1, :], mod_ref[1:2, :], mod_ref[2:3, :]
        r = lax.rsqrt(_rowmean(xv * xv) + EPS)
        h = (xv * r * g_ref[0:1, :]) * (1.0 + sc) + sh
        hb = h.astype(BF16)
        y = jnp.zeros((tm, D), F32)
        for cidx in range(4):
            gate = _dot_nt(hb, win_ref[cidx])
            up = _dot_nt(hb, win_ref[4 + cidx])
            gu_ref[cidx] = gate.astype(BF16)
            gu_ref[4 + cidx] = up.astype(BF16)
            act = gate * _sigmoid(gate) * up
            y = y + _dot(act.astype(BF16), wout_ref[cidx])
        y_ref[...] = y
        ry = lax.rsqrt(_rowmean(y * y) + EPS)
        xo_ref[...] = xv + (HALF * gt) * (y * ry * g_ref[1:2, :])

    tile = pl.BlockSpec((tm, D), lambda i: (i, 0))
    return _call(
        core, name=name, grid=(nt,), jobs=jobs,
        in_specs=[tile, pl.BlockSpec((None, 8, D), lambda i: (i // tps, 0, 0)), _const_spec((8, D)),
                  _const_spec((8, FBP, D)), _const_spec((4, FBP, D))],
        out_specs=[tile, pl.BlockSpec((8, tm, FBP), lambda i: (0, i, 0)), tile],
        out_shape=[jax.ShapeDtypeStruct((T, D), F32), jax.ShapeDtypeStruct((8, T, FBP), BF16),
                   jax.ShapeDtypeStruct((T, D), F32)],
        args=[x, mod, gvec, w_in, w_out])


def _ffn_bwd(dxo, x, y, gu, mod, gvec, w_in, w_out, tm, name, jobs=()):
    T = x.shape[0]
    nt = T // tm
    nb = mod.shape[0]
    tps = nt // nb

    def core(ins, outs, _):
        dxo_ref, x_ref, y_ref, gu_ref, mod_ref, g_ref, win_ref, wout_ref = ins
        dx_ref, dg_ref, act_ref, hb_ref, dyb_ref, mg_ref, vg_ref = outs
        i = pl.program_id(0)
        xv = x_ref[...]
        dxo_v = dxo_ref[...]
        yv = y_ref[...]
        sh, sc, gt = mod_ref[0:1, :], mod_ref[1:2, :], mod_ref[2:3, :]
        gpre, gpost = g_ref[0:1, :], g_ref[1:2, :]
        r = lax.rsqrt(_rowmean(xv * xv) + EPS)
        xh = xv * r
        n = xh * gpre
        hb = (n * (1.0 + sc) + sh).astype(BF16)
        hb_ref[...] = hb
        ry = lax.rsqrt(_rowmean(yv * yv) + EPS)
        yh = yv * ry
        d_gt = _colsum(HALF * dxo_v * (yh * gpost))
        dp = (HALF * gt) * dxo_v
        d_gpost = _colsum(dp * yh)
        dyh = dp * gpost
        dy = ry * (dyh - yh * _rowmean(dyh * yh))
        dyb = dy.astype(BF16)
        dyb_ref[...] = dyb
        dh = jnp.zeros((tm, D), F32)
        for cidx in range(4):
            gate = gu_ref[cidx].astype(F32)
            up = gu_ref[4 + cidx].astype(F32)
            sig = _sigmoid(gate)
            s = gate * sig
            act_ref[cidx] = (s * up).astype(BF16)
            d_act = _dot_nt(dyb, wout_ref[cidx])
            d_up = (d_act * s).astype(BF16)
            d_gate = (d_act * up * (sig * (1.0 + gate * (1.0 - sig)))).astype(BF16)
            dg_ref[cidx] = d_gate
            dg_ref[4 + cidx] = d_up
            dh = dh + _dot(d_gate, win_ref[cidx]) + _dot(d_up, win_ref[4 + cidx])
        d_sc = _colsum(dh * n)
        d_sh = _colsum(dh)
        dn = dh * (1.0 + sc)
        d_gpre = _colsum(dn * xh)
        dxh = dn * gpre
        dx_ref[...] = dxo_v + r * (dxh - xh * _rowmean(dxh * xh))

        @pl.when(i % tps == 0)
        def _():
            mg_ref[...] = jnp.zeros((8, D), F32)

        @pl.when(i == 0)
        def _():
            vg_ref[...] = jnp.zeros((8, D), F32)

        mg_ref[0:1, :] += d_sh
        mg_ref[1:2, :] += d_sc
        mg_ref[2:3, :] += d_gt
        vg_ref[0:1, :] += d_gpre
        vg_ref[1:2, :] += d_gpost

    tile = pl.BlockSpec((tm, D), lambda i: (i, 0))
    return _call(
        core, name=name, grid=(nt,), jobs=jobs,
        in_specs=[tile, tile, tile, pl.BlockSpec((8, tm, FBP), lambda i: (0, i, 0)),
                  pl.BlockSpec((None, 8, D), lambda i: (i // tps, 0, 0)), _const_spec((8, D)),
                  _const_spec((8, FBP, D)), _const_spec((4, FBP, D))],
        out_specs=[tile, pl.BlockSpec((8, tm, FBP), lambda i: (0, i, 0)),
                   pl.BlockSpec((4, tm, FBP), lambda i: (0, i, 0)), tile, tile,
                   pl.BlockSpec((None, 8, D), lambda i: (i // tps, 0, 0)), pl.BlockSpec((8, D), lambda i: (0, 0))],
        out_shape=[jax.ShapeDtypeStruct((T, D), F32), jax.ShapeDtypeStruct((8, T, FBP), BF16),
                   jax.ShapeDtypeStruct((4, T, FBP), BF16), jax.ShapeDtypeStruct((T, D), BF16),
                   jax.ShapeDtypeStruct((T, D), BF16), jax.ShapeDtypeStruct((nb, 8, D), F32),
                   jax.ShapeDtypeStruct((8, D), F32)],
        args=[dxo, x, y, gu, mod, gvec, w_in, w_out])


def _ffn_last(x, target, mod, gvec, w_in, w_out, tm, name, jobs=()):
    T = x.shape[0]
    nt = T // tm
    nb = mod.shape[0]
    tps = nt // nb

    def core(ins, outs, scs):
        x_ref, t_ref, mod_ref, g_ref, win_ref, wout_ref = ins
        dx_ref, dg_ref, act_ref, hb_ref, dyb_ref, mg_ref, vg_ref, loss_ref = outs
        (gu_s,) = scs
        i = pl.program_id(0)
        xv = x_ref[...]
        sh, sc, gt = mod_ref[0:1, :], mod_ref[1:2, :], mod_ref[2:3, :]
        gpre, gpost = g_ref[0:1, :], g_ref[1:2, :]
        r = lax.rsqrt(_rowmean(xv * xv) + EPS)
        xh = xv * r
        n = xh * gpre
        hb = (n * (1.0 + sc) + sh).astype(BF16)
        hb_ref[...] = hb
        yv = jnp.zeros((tm, D), F32)
        for cidx in range(4):
            gate = _dot_nt(hb, win_ref[cidx])
            up = _dot_nt(hb, win_ref[4 + cidx])
            gu_s[cidx] = gate.astype(BF16)
            gu_s[4 + cidx] = up.astype(BF16)
            act = gate * _sigmoid(gate) * up
            act_ref[cidx] = act.astype(BF16)
            yv = yv + _dot(act_ref[cidx], wout_ref[cidx])
        ry = lax.rsqrt(_rowmean(yv * yv) + EPS)
        yh = yv * ry
        pn = yh * gpost
        err = xv + (HALF * gt) * pn - t_ref[...]
        dxo_v = err * (1.0 / D)
        d_gt = _colsum(HALF * dxo_v * pn)
        dp = (HALF * gt) * dxo_v
        d_gpost = _colsum(dp * yh)
        dyh = dp * gpost
        dyb = (ry * (dyh - yh * _rowmean(dyh * yh))).astype(BF16)
        dyb_ref[...] = dyb
        dh = jnp.zeros((tm, D), F32)
        for cidx in range(4):
            gate = gu_s[cidx].astype(F32)
            up = gu_s[4 + cidx].astype(F32)
            sig = _sigmoid(gate)
            s = gate * sig
            d_act = _dot_nt(dyb, wout_ref[cidx])
            d_up = (d_act * s).astype(BF16)
            d_gate = (d_act * up * (sig * (1.0 + gate * (1.0 - sig)))).astype(BF16)
            dg_ref[cidx] = d_gate
            dg_ref[4 + cidx] = d_up
            dh = dh + _dot(d_gate, win_ref[cidx]) + _dot(d_up, win_ref[4 + cidx])
        d_sc = _colsum(dh * n)
        d_sh = _colsum(dh)
        dn = dh * (1.0 + sc)
        d_gpre = _colsum(dn * xh)
        dxh = dn * gpre
        dx_ref[...] = dxo_v + r * (dxh - xh * _rowmean(dxh * xh))

        @pl.when(i % tps == 0)
        def _():
            mg_ref[...] = jnp.zeros((8, D), F32)

        @pl.when(i == 0)
        def _():
            vg_ref[...] = jnp.zeros((8, D), F32)
            loss_ref[...] = jnp.zeros((8, D), F32)

        mg_ref[0:1, :] += d_sh
        mg_ref[1:2, :] += d_sc
        mg_ref[2:3, :] += d_gt
        vg_ref[0:1, :] += d_gpre
        vg_ref[1:2, :] += d_gpost
        loss_ref[...] += HALF * jnp.sum(_rowmean(err * err), axis=0, keepdims=True)

    tile = pl.BlockSpec((tm, D), lambda i: (i, 0))
    return _call(
        core, name=name, grid=(nt,), jobs=jobs,
        in_specs=[tile, tile, pl.BlockSpec((None, 8, D), lambda i: (i // tps, 0, 0)), _const_spec((8, D)),
                  _const_spec((8, FBP, D)), _const_spec((4, FBP, D))],
        out_specs=[tile, pl.BlockSpec((8, tm, FBP), lambda i: (0, i, 0)),
                   pl.BlockSpec((4, tm, FBP), lambda i: (0, i, 0)), tile, tile,
                   pl.BlockSpec((None, 8, D), lambda i: (i // tps, 0, 0)), pl.BlockSpec((8, D), lambda i: (0, 0)),
                   pl.BlockSpec((8, D), lambda i: (0, 0))],
        out_shape=[jax.ShapeDtypeStruct((T, D), F32), jax.ShapeDtypeStruct((8, T, FBP), BF16),
                   jax.ShapeDtypeStruct((4, T, FBP), BF16), jax.ShapeDtypeStruct((T, D), BF16),
                   jax.ShapeDtypeStruct((T, D), BF16), jax.ShapeDtypeStruct((nb, 8, D), F32),
                   jax.ShapeDtypeStruct((8, D), F32), jax.ShapeDtypeStruct((8, D), F32)],
        scratch=[pltpu.VMEM((8, tm, FBP), BF16)],
        args=[x, target, mod, gvec, w_in, w_out])


def _masked_spatial(ws_ref):
    row = lax.broadcasted_iota(jnp.int32, (CHUNK, CHUNK), 0)
    col = lax.broadcasted_iota(jnp.int32, (CHUNK, CHUNK), 1)
    keep = col <= row
    return [jnp.where(keep, ws_ref[hd], 0.0).astype(BF16) for hd in range(NHEAD)]


def _spatial_gate(wm, vb_chunk, lane_head):
    z = jnp.zeros((CHUNK, WA), F32)
    for hd in range(NHEAD):
        z = jnp.where(lane_head == hd, _dot(wm[hd], vb_chunk), z)
    return z


def _layer_norm_stats(v):
    mu = _rowmean(v)
    vc = v - mu
    rstd = lax.rsqrt(_rowmean(vc * vc) + EPS)
    return vc * rstd, rstd


def _pitch(tm):
    p = tm // 8
    while p % 8 != 4:
        p += 1
    return p


def _lanes(s):
    return slice(s * 128, (s + 1) * 128)


def _to_slabs(ref, row0, val):
    for s in range(4):
        ref[s, row0:row0 + val.shape[0], :] = val[:, _lanes(s)]


def _tap_sum(src, out, cw_ref, bias, tm, start):
    p = _pitch(tm)
    for s in range(4):
        accs = [jnp.broadcast_to(bias[:, _lanes(s)], (8, 128))] * p
        for k in range(CONV_K):
            w = jnp.broadcast_to(cw_ref[k:k + 1, _lanes(s)], (8, 128))
            for v in range(p):
                accs[v] = accs[v] + w * src[s, pl.ds(v + start(k), 8, stride=p), :]
        for v in range(p):
            out[s, pl.ds(v, 8, stride=p), :] = accs[v]
    return jnp.concatenate([out[s, 0:tm, :] for s in range(4)], axis=1)


def _mixer_fwd(x, mod, gvec, w_mi, w_mo, v512, ws, bias_full, cw, tm, name, jobs=()):
    T = x.shape[0]
    nt = T // tm
    tps = nt // mod.shape[0]
    ext_rows = 8 * _pitch(tm)

    def core(ins, outs, scs):
        x_ref, mod_ref, g_ref, wmi_ref, wmo_ref, v_ref, ws_ref, bias_ref, cw_ref = ins
        xo_ref, proj_ref, ym_ref, conv_ref = outs
        glu_ext, conv_scr = scs
        i = pl.program_id(0)
        xv = x_ref[...]
        sh, sc, gt = mod_ref[0:1, :], mod_ref[1:2, :], mod_ref[2:3, :]
        r = lax.rsqrt(_rowmean(xv * xv) + EPS)
        hb = ((xv * r * g_ref[0:1, :]) * (1.0 + sc) + sh).astype(BF16)
        for j in range(NDEV):
            proj_ref[:, j * MB:(j + 1) * MB] = _dot(hb, wmi_ref[j])
        u = proj_ref[:, 0:WA]
        v0 = proj_ref[:, WA:2 * WA]
        a = proj_ref[:, 2 * WA:3 * WA]
        g = proj_ref[:, 3 * WA:4 * WA]
        vh, _ = _layer_norm_stats(v0)
        vb = (vh * v_ref[0:1, :] + v_ref[1:2, :]).astype(BF16)
        wm = _masked_spatial(ws_ref)
        lane_head = lax.broadcasted_iota(jnp.int32, (CHUNK, WA), 1) >> 6
        ya = []
        for q in range(tm // CHUNK):
            z = _spatial_gate(wm, vb[q * CHUNK:(q + 1) * CHUNK, :], lane_head) + bias_ref[...]
            ya.append(u[q * CHUNK:(q + 1) * CHUNK, :] * z)
        ya = jnp.concatenate(ya, axis=0)
        glu = a * _sigmoid(g)

        @pl.when(i == 0)
        def _():
            glu_ext[:, HALO + tm:HALO + ext_rows, :] = jnp.zeros((4, ext_rows - tm, 128), F32)

        @pl.when(i % tps == 0)
        def _():
            glu_ext[:, 0:HALO, :] = jnp.zeros((4, HALO, 128), F32)

        _to_slabs(glu_ext, HALO, glu)
        conv = _tap_sum(glu_ext, conv_scr, cw_ref, v_ref[2:3, :], tm, lambda k: HALO - (CONV_K - 1) + k)
        conv_ref[...] = conv
        glu_ext[:, 0:HALO, :] = glu_ext[:, tm:tm + HALO, :]
        ch, _ = _layer_norm_stats(conv)
        cn = ch * v_ref[3:4, :] + v_ref[4:5, :]
        yb = cn * _sigmoid(cn)
        pa = ya * lax.rsqrt(_rowmean(ya * ya) + EPS) * v_ref[5:6, :]
        pb = yb * lax.rsqrt(_rowmean(yb * yb) + EPS) * v_ref[6:7, :]
        ycat = jnp.concatenate([pa, pb], axis=1).astype(BF16)
        ym = _dot(ycat, wmo_ref[...])
        ym_ref[...] = ym
        rm = lax.rsqrt(_rowmean(ym * ym) + EPS)
        xo_ref[...] = xv + gt * (ym * rm * g_ref[1:2, :])

    tile = pl.BlockSpec((tm, D), lambda i: (i, 0))
    return _call(
        core, name=name, grid=(nt,), jobs=jobs,
        in_specs=[tile, pl.BlockSpec((None, 8, D), lambda i: (i // tps, 0, 0)), _const_spec((8, D)),
                  _const_spec((NDEV, D, MB)), _const_spec((D, D)), _const_spec((8, WA)),
                  _const_spec((NHEAD, CHUNK, CHUNK)), _const_spec((CHUNK, WA)), _const_spec((32, WA))],
        out_specs=[tile, pl.BlockSpec((tm, 4 * WA), lambda i: (i, 0)), tile, pl.BlockSpec((tm, WA), lambda i: (i, 0))],
        out_shape=[jax.ShapeDtypeStruct((T, D), F32), jax.ShapeDtypeStruct((T, 4 * WA), F32),
                   jax.ShapeDtypeStruct((T, D), F32), jax.ShapeDtypeStruct((T, WA), F32)],
        scratch=[pltpu.VMEM((4, HALO + ext_rows, 128), F32), pltpu.VMEM((4, ext_rows, 128), F32)],
        args=[x, mod, gvec, w_mi, w_mo, v512, ws, bias_full, cw])


def _mixer_bwd_a(dxo, ym, proj, conv, mod, gvec, w_mo, v512, ws, bias_full, esel, tm, name, jobs=()):
    T = dxo.shape[0]
    nt = T // tm
    nb = mod.shape[0]
    tps = nt // nb

    def core(ins, outs, scs):
        dxo_ref, ym_ref, proj_ref, conv_ref, mod_ref, g_ref, wmo_ref, v_ref, ws_ref, bias_ref, e_ref = ins
        dpart_ref, dymb_ref, ycat_ref, mg_ref, vg_ref, v5g_ref, gws_ref, gbs_ref = outs
        (dbs_acc,) = scs
        i = pl.program_id(0)
        dxo_v = dxo_ref[...]
        ymv = ym_ref[...]
        gt = mod_ref[2:3, :]
        gpost = g_ref[1:2, :]
        rm = lax.rsqrt(_rowmean(ymv * ymv) + EPS)
        ymh = ymv * rm
        d_gt = _colsum(dxo_v * (ymh * gpost))
        dpm = gt * dxo_v
        d_gpost = _colsum(dpm * ymh)
        dymh = dpm * gpost
        dym = (rm * (dymh - ymh * _rowmean(dymh * ymh))).astype(BF16)
        dymb_ref[...] = dym
        dycat = _dot_nt(dym, wmo_ref[...])
        u = proj_ref[:, 0:WA]
        v0 = proj_ref[:, WA:2 * WA]
        vh, rv = _layer_norm_stats(v0)
        vb = (vh * v_ref[0:1, :] + v_ref[1:2, :]).astype(BF16)
        wm = _masked_spatial(ws_ref)
        lane_head = lax.broadcasted_iota(jnp.int32, (CHUNK, WA), 1) >> 6
        zs = []
        for q in range(tm // CHUNK):
            zs.append(_spatial_gate(wm, vb[q * CHUNK:(q + 1) * CHUNK, :], lane_head) + bias_ref[...])
        z = jnp.concatenate(zs, axis=0)
        ya = u * z
        ra = lax.rsqrt(_rowmean(ya * ya) + EPS)
        yah = ya * ra
        ch, rc = _layer_norm_stats(conv_ref[...])
        cn = ch * v_ref[3:4, :] + v_ref[4:5, :]
        sg = _sigmoid(cn)
        yb = cn * sg
        rb = lax.rsqrt(_rowmean(yb * yb) + EPS)
        ybh = yb * rb
        ycat_ref[...] = jnp.concatenate([yah * v_ref[5:6, :], ybh * v_ref[6:7, :]], axis=1).astype(BF16)
        dpa = dycat[:, 0:WA]
        dpb = dycat[:, WA:2 * WA]
        d_goa = _colsum(dpa * yah)
        d_gob = _colsum(dpb * ybh)
        dyah = dpa * v_ref[5:6, :]
        dybh = dpb * v_ref[6:7, :]
        dya = ra * (dyah - yah * _rowmean(dyah * yah))
        dyb = rb * (dybh - ybh * _rowmean(dybh * ybh))
        dpart_ref[:, 0:WA] = dya * z
        dz = dya * u

        @pl.when(i == 0)
        def _():
            gws_ref[...] = jnp.zeros((NHEAD, CHUNK, CHUNK), F32)
            dbs_acc[...] = jnp.zeros((CHUNK, WA), F32)
            vg_ref[...] = jnp.zeros((8, D), F32)
            v5g_ref[...] = jnp.zeros((8, WA), F32)

        dvs = []
        for q in range(tm // CHUNK):
            dz_q = dz[q * CHUNK:(q + 1) * CHUNK, :]
            vb_q = vb[q * CHUNK:(q + 1) * CHUNK, :]
            dbs_acc[...] += dz_q
            dzb = dz_q.astype(BF16)
            dv_q = jnp.zeros((CHUNK, WA), F32)
            for hd in range(NHEAD):
                dv_q = jnp.where(lane_head == hd, _dot_tn(wm[hd], dzb), dv_q)
                dz_hd = jnp.where(lane_head == hd, dz_q, 0.0).astype(BF16)
                gws_ref[hd] += _dot_nt(dz_hd, vb_q)
            dvs.append(dv_q)
        dv = jnp.concatenate(dvs, axis=0)
        d_gng = _colsum(dv * vh)
        d_gnb = _colsum(dv)
        dvh = dv * v_ref[0:1, :]
        dpart_ref[:, WA:2 * WA] = rv * (dvh - _rowmean(dvh) - vh * _rowmean(dvh * vh))
        dcn = dyb * (sg * (1.0 + cn * (1.0 - sg)))
        d_cng = _colsum(dcn * ch)
        d_cnb = _colsum(dcn)
        dch = dcn * v_ref[3:4, :]
        dconv = rc * (dch - _rowmean(dch) - ch * _rowmean(dch * ch))
        dpart_ref[:, 2 * WA:3 * WA] = dconv
        dpart_ref[:, 3 * WA:4 * WA] = jnp.zeros((tm, WA), F32)
        d_cb = _colsum(dconv)

        @pl.when(i % tps == 0)
        def _():
            mg_ref[...] = jnp.zeros((8, D), F32)

        mg_ref[2:3, :] += d_gt
        vg_ref[1:2, :] += d_gpost
        v5g_ref[0:1, :] += d_gng
        v5g_ref[1:2, :] += d_gnb
        v5g_ref[2:3, :] += d_cb
        v5g_ref[3:4, :] += d_cng
        v5g_ref[4:5, :] += d_cnb
        v5g_ref[5:6, :] += d_goa
        v5g_ref[6:7, :] += d_gob

        @pl.when(i == nt - 1)
        def _():
            row = lax.broadcasted_iota(jnp.int32, (CHUNK, CHUNK), 0)
            col = lax.broadcasted_iota(jnp.int32, (CHUNK, CHUNK), 1)
            for hd in range(NHEAD):
                gws_ref[hd] = jnp.where(col <= row, gws_ref[hd], 0.0)
            gbs_ref[...] = lax.dot_general(e_ref[...], dbs_acc[...], (((1,), (1,)), ((), ())),
                                           precision=lax.Precision.HIGHEST, preferred_element_type=F32)

    tile = pl.BlockSpec((tm, D), lambda i: (i, 0))
    ptile = pl.BlockSpec((tm, 4 * WA), lambda i: (i, 0))
    return _call(
        core, name=name, grid=(nt,), jobs=jobs,
        in_specs=[tile, tile, pl.BlockSpec((tm, 2 * WA), lambda i: (i, 0)), pl.BlockSpec((tm, WA), lambda i: (i, 0)),
                  pl.BlockSpec((None, 8, D), lambda i: (i // tps, 0, 0)), _const_spec((8, D)), _const_spec((D, D)),
                  _const_spec((8, WA)), _const_spec((NHEAD, CHUNK, CHUNK)), _const_spec((CHUNK, WA)),
                  _const_spec((8, WA))],
        out_specs=[ptile, tile, tile, pl.BlockSpec((None, 8, D), lambda i: (i // tps, 0, 0)),
                   pl.BlockSpec((8, D), lambda i: (0, 0)), pl.BlockSpec((8, WA), lambda i: (0, 0)),
                   pl.BlockSpec((NHEAD, CHUNK, CHUNK), lambda i: (0, 0, 0)), pl.BlockSpec((8, CHUNK), lambda i: (0, 0))],
        out_shape=[jax.ShapeDtypeStruct((T, 4 * WA), F32), jax.ShapeDtypeStruct((T, D), BF16),
                   jax.ShapeDtypeStruct((T, D), BF16), jax.ShapeDtypeStruct((nb, 8, D), F32),
                   jax.ShapeDtypeStruct((8, D), F32), jax.ShapeDtypeStruct((8, WA), F32),
                   jax.ShapeDtypeStruct((NHEAD, CHUNK, CHUNK), F32), jax.ShapeDtypeStruct((8, CHUNK), F32)],
        scratch=[pltpu.VMEM((CHUNK, WA), F32)],
        args=[dxo, ym, proj, conv, mod, gvec, w_mo, v512, ws, bias_full, esel])


def _mixer_bwd_b(dxo, x, dpart, proj, mod, gvec, w_mi, cw, tm, name, jobs=()):
    T = x.shape[0]
    nt = T // tm
    nb = mod.shape[0]
    tps = nt // nb
    hpt = tm // HALO
    nh = T // HALO
    off = HALO - (CONV_K - 1)
    p = _pitch(tm)
    ext_rows = 8 * p

    def core(ins, outs, scs):
        dxo_ref, x_ref, dpart_ref, dnext_ref, ag_ref, halo_ref, mod_ref, g_ref, wmi_ref, cw_ref = ins
        dx_ref, dproj_ref, hb_ref, mg_ref, vg_ref, dcw_ref = outs
        glu_ext, dconv_ext, dglu_scr, dcw_acc = scs
        i = pl.program_id(0)
        first = i % tps == 0
        last = i % tps == tps - 1
        a = ag_ref[:, 0:WA]
        g = ag_ref[:, WA:2 * WA]
        sgg = _sigmoid(g)

        @pl.when(i == 0)
        def _():
            glu_ext[:, HALO + tm:HALO + ext_rows, :] = jnp.zeros((4, ext_rows - tm, 128), F32)
            dconv_ext[:, HALO + tm:HALO + ext_rows, :] = jnp.zeros((4, ext_rows - tm, 128), F32)
            dcw_acc[...] = jnp.zeros((32, 8, WA), F32)
            vg_ref[...] = jnp.zeros((8, D), F32)

        _to_slabs(glu_ext, 0, jnp.where(first, 0.0, halo_ref[:, 0:WA] * _sigmoid(halo_ref[:, WA:2 * WA])))
        _to_slabs(glu_ext, HALO, a * sgg)
        _to_slabs(dconv_ext, 0, dpart_ref[:, 2 * WA:3 * WA])
        _to_slabs(dconv_ext, tm, jnp.where(last, 0.0, dnext_ref[...]))
        sub = lax.broadcasted_iota(jnp.int32, (8, 128), 0)
        for s in range(4):
            accs = [jnp.zeros((8, 128), F32)] * CONV_K
            for v in range(p):
                dc = jnp.where(v + p * sub < tm, dconv_ext[s, pl.ds(v, 8, stride=p), :], 0.0)
                for k in range(CONV_K):
                    accs[k] = accs[k] + dc * glu_ext[s, pl.ds(v + off + k, 8, stride=p), :]
            for k in range(CONV_K):
                dcw_acc[k, :, _lanes(s)] += accs[k]
        dglu = _tap_sum(dconv_ext, dglu_scr, cw_ref, jnp.zeros((1, WA), F32), tm, lambda k: (CONV_K - 1) - k)

        @pl.when(i == nt - 1)
        def _():
            for k in range(CONV_K):
                dcw_ref[k:k + 1, :] = jnp.sum(dcw_acc[k], axis=0, keepdims=True)
            dcw_ref[CONV_K:32, :] = jnp.zeros((32 - CONV_K, WA), F32)

        da = dglu * sgg
        dgg = dglu * a * (sgg * (1.0 - sgg))
        dproj_ref[:, 0:2 * WA] = dpart_ref[:, 0:2 * WA].astype(BF16)
        dproj_ref[:, 2 * WA:3 * WA] = da.astype(BF16)
        dproj_ref[:, 3 * WA:4 * WA] = dgg.astype(BF16)
        dh = jnp.zeros((tm, D), F32)
        for j in range(NDEV):
            dh = dh + _dot_nt(dproj_ref[:, j * MB:(j + 1) * MB], wmi_ref[j])
        xv = x_ref[...]
        sc, sh = mod_ref[1:2, :], mod_ref[0:1, :]
        gpre = g_ref[0:1, :]
        r = lax.rsqrt(_rowmean(xv * xv) + EPS)
        xh = xv * r
        n = xh * gpre
        hb_ref[...] = (n * (1.0 + sc) + sh).astype(BF16)
        d_sc = _colsum(dh * n)
        d_sh = _colsum(dh)
        dn = dh * (1.0 + sc)
        d_gpre = _colsum(dn * xh)
        dxh = dn * gpre
        dx_ref[...] = dxo_ref[...] + r * (dxh - xh * _rowmean(dxh * xh))

        @pl.when(first)
        def _():
            mg_ref[...] = jnp.zeros((8, D), F32)

        mg_ref[0:1, :] += d_sh
        mg_ref[1:2, :] += d_sc
        vg_ref[0:1, :] += d_gpre

    tile = pl.BlockSpec((tm, D), lambda i: (i, 0))
    return _call(
        core, name=name, grid=(nt,), jobs=jobs,
        in_specs=[tile, tile, pl.BlockSpec((tm, 4 * WA), lambda i: (i, 0)),
                  pl.BlockSpec((HALO, WA), lambda i: (jnp.minimum((i + 1) * hpt, nh - 1), 2)),
                  pl.BlockSpec((tm, 2 * WA), lambda i: (i, 1)),
                  pl.BlockSpec((HALO, 2 * WA), lambda i: (jnp.maximum(i * hpt - 1, 0), 1)),
                  pl.BlockSpec((None, 8, D), lambda i: (i // tps, 0, 0)), _const_spec((8, D)),
                  _const_spec((NDEV, D, MB)), _const_spec((32, WA))],
        out_specs=[tile, pl.BlockSpec((tm, 4 * WA), lambda i: (i, 0)), tile,
                   pl.BlockSpec((None, 8, D), lambda i: (i // tps, 0, 0)), pl.BlockSpec((8, D), lambda i: (0, 0)),
                   pl.BlockSpec((32, WA), lambda i: (0, 0))],
        out_shape=[jax.ShapeDtypeStruct((T, D), F32), jax.ShapeDtypeStruct((T, 4 * WA), BF16),
                   jax.ShapeDtypeStruct((T, D), BF16), jax.ShapeDtypeStruct((nb, 8, D), F32),
                   jax.ShapeDtypeStruct((8, D), F32), jax.ShapeDtypeStruct((32, WA), F32)],
        scratch=[pltpu.VMEM((4, HALO + ext_rows, 128), F32), pltpu.VMEM((4, HALO + ext_rows, 128), F32),
                 pltpu.VMEM((4, ext_rows, 128), F32), pltpu.VMEM((32, 8, WA), F32)],
        args=[dxo, x, dpart, dpart, proj, proj, mod, gvec, w_mi, cw])


def _grad_chip(a, b, a_spec, b_spec, prod_shape, half, name, jobs=()):
    steps = 8 if half is None else 4
    R = prod_shape[0] if half is None else half
    C = prod_shape[1]

    def core(ins, outs, scs):
        a_ref, b_ref = ins
        (o_ref,) = outs
        own, snd, rcv, ssem, rsem, lsem = scs
        s = pl.program_id(0)
        c = lax.axis_index("c")
        me = _me()
        sib = _flip(me, (0, 0, 1))
        prod = _dot_tn(a_ref[...], b_ref[...]).astype(BF16)
        if half is None:
            q = s // 2

            @pl.when(s % 2 == c)
            def _():
                own[q] = prod

            @pl.when(s % 2 != c)
            def _():
                snd[q] = prod
                _remote(snd.at[q], rcv.at[q], ssem.at[q], rsem.at[q], sib).start()
        else:
            lo = prod[0:half, :]
            hi = prod[half:2 * half, :]
            own[s] = jnp.where(c == 0, lo, hi)
            snd[s] = jnp.where(c == 0, hi, lo)
            _remote(snd.at[s], rcv.at[s], ssem.at[s], rsem.at[s], sib).start()

        @pl.when(s == steps - 1)
        def _():
            for q4 in range(4):
                cp = _remote(snd.at[q4], rcv.at[q4], ssem.at[q4], rsem.at[q4], sib)
                cp.wait_recv()
                cp.wait_send()
                snd[q4] = (own[q4].astype(F32) + rcv[q4].astype(F32)).astype(BF16)
            out = pltpu.make_async_copy(snd, o_ref, lsem)
            out.start()
            out.wait()

    return _call(
        core, name=name, grid=(steps,), jobs=jobs, in_specs=[a_spec, b_spec], out_specs=[HBM],
        out_shape=[jax.ShapeDtypeStruct((4, R, C), BF16)],
        scratch=[pltpu.VMEM((4, R, C), BF16), pltpu.VMEM((4, R, C), BF16), pltpu.VMEM((4, R, C), BF16),
                 pltpu.SemaphoreType.DMA((4,)), pltpu.SemaphoreType.DMA((4,)), pltpu.SemaphoreType.DMA],
        args=[a, b])


def _grad_w_in(dg, hb, name, jobs=()):
    T = hb.shape[0]
    return _grad_chip(dg, hb, pl.BlockSpec((None, T, FBP), lambda s: (s, 0, 0)), _const_spec((T, D)),
                      (FBP, D), None, name, jobs)


def _grad_w_out(act, dyb, name, jobs=()):
    T = dyb.shape[0]
    return _grad_chip(act, dyb, pl.BlockSpec((None, T, FBP), lambda s: (s, 0, 0)), _const_spec((T, D)),
                      (FBP, D), FO, name, jobs)


def _grad_w_mi(hb, dproj, name, jobs=()):
    T = hb.shape[0]
    return _grad_chip(hb, dproj, _const_spec((T, D)), pl.BlockSpec((T, MB), lambda s: (0, s)),
                      (D, MB), None, name, jobs)


def _grad_w_mo(ycat, dym, name, jobs=()):
    T = ycat.shape[0]
    return _grad_chip(ycat, dym, pl.BlockSpec((T, 2 * MO), lambda s: (0, s)), _const_spec((T, D)),
                      (2 * MO, D), MO, name, jobs)


def _adamw_math(w, g, m, v):
    m2 = ADAM_B1 * m + (1.0 - ADAM_B1) * g
    v2 = ADAM_B2 * v + (1.0 - ADAM_B2) * (g * g)
    m_hat = m2 / (1.0 - ADAM_B1 ** ADAM_STEP)
    v_hat = v2 / (1.0 - ADAM_B2 ** ADAM_STEP)
    delta = -ADAM_LR * (m_hat / (jnp.sqrt(v_hat) + ADAM_EPS) + ADAM_WD * w)
    return delta, m2, v2


def _adamw_reduce(parts, w, m, v, tr, name, own=None, after=None):
    R, C = w.shape

    def core(ins, outs, _):
        p_ref, w_ref, m_ref, v_ref = ins[:4]
        g_ref, d_ref, m2_ref, v2_ref = outs
        if own is None:
            terms = [p_ref[s].astype(F32) for s in range(4)]
        else:
            mq = 2 * lax.axis_index("x") + lax.axis_index("y")
            mine = ins[4][...].astype(F32)
            terms = [jnp.where(mq == s, mine, p_ref[s].astype(F32)) for s in range(4)]
        g = terms[0]
        for s in range(1, 4):
            g = g + terms[s]
        g_ref[...] = g
        d_ref[...], m2_ref[...], v2_ref[...] = _adamw_math(w_ref[...], g, m_ref[...], v_ref[...])

    blk = pl.BlockSpec((tr, C), lambda i: (i, 0))
    in_specs = [pl.BlockSpec((4, tr, C), lambda i: (0, i, 0)), blk, blk, blk]
    args = [parts, w, m, v]
    if own is not None:
        mq = 2 * lax.axis_index("x") + lax.axis_index("y")
        in_specs.append(pl.BlockSpec((tr, C), lambda i: (i, 0)))
        args.append(lax.dynamic_index_in_dim(own, mq, 0, keepdims=False))
    if after is not None:
        in_specs.append(HBM)
        args.append(after)
    return _call(
        core, name=name, grid=(R // tr,), in_specs=in_specs,
        out_specs=[blk, blk, blk, blk], out_shape=[jax.ShapeDtypeStruct((R, C), F32)] * 4, args=args)[0]


HBM_ONLY = pl.BlockSpec(memory_space=pltpu.HBM)
SEM = pl.BlockSpec(memory_space=pltpu.SEMAPHORE)
EFFECT = pltpu.SideEffectType.DATAFLOW_SIDE_EFFECTING


def _chip_scatter_start(gs, name):
    n = len(gs)

    def body(*refs):
        g_refs, land_refs = refs[:n], refs[n:2 * n]
        ssem, rsem = refs[2 * n:2 * n + 2]
        token = refs[-1]
        me = _me()
        mq = 2 * me[0] + me[1]
        for k, f in enumerate(CHIP_FLIPS):
            p = _flip(me, f)
            for a in range(n):
                _remote(g_refs[a].at[2 * p[0] + p[1]], land_refs[a].at[mq], ssem.at[3 * a + k], rsem.at[3 * a + k], p).start()
        token[...] = jnp.zeros_like(token)

    gs = [pltpu.with_memory_space_constraint(g, pltpu.HBM) for g in gs]
    lands = [pltpu.with_memory_space_constraint(lax.empty(g.shape, g.dtype), pltpu.HBM) for g in gs]
    res = pl.pallas_call(
        body, name=name,
        out_shape=(pltpu.SemaphoreType.DMA((3 * n,)), pltpu.SemaphoreType.DMA((3 * n,)))
        + tuple(pltpu.HBM(g.shape, g.dtype) for g in gs) * 2 + (jax.ShapeDtypeStruct((8, 128), F32),),
        in_specs=(HBM_ONLY,) * (2 * n), out_specs=(SEM, SEM) + (HBM_ONLY,) * (2 * n) + (VM,),
        input_output_aliases={a: 2 + a for a in range(2 * n)},
        compiler_params=pltpu.CompilerParams(has_side_effects=EFFECT),
    )(*gs, *lands)
    return res[:-1], res[-1]


def _chip_scatter_wait(handle, after, name):
    ssem, rsem = handle[:2]
    n = (len(handle) - 2) // 2
    thru = handle[2:]

    def body(*refs):
        g_refs, land_refs = refs[:n], refs[n:2 * n]
        ssem, rsem = refs[2 * n:2 * n + 2]
        me = _me()
        mq = 2 * me[0] + me[1]
        for k, f in enumerate(CHIP_FLIPS):
            p = _flip(me, f)
            pq = 2 * p[0] + p[1]
            for a in range(n):
                _remote(g_refs[a].at[pq], land_refs[a].at[mq], ssem.at[3 * a + k], rsem.at[3 * a + k], p).wait_send()
                _remote(g_refs[a].at[mq], land_refs[a].at[pq], ssem.at[3 * a + k], rsem.at[3 * a + k], p).wait_recv()

    res = pl.pallas_call(
        body, name=name,
        out_shape=tuple(pltpu.HBM(t.shape, t.dtype) for t in thru),
        in_specs=(HBM_ONLY,) * (2 * n) + (SEM, SEM, HBM), out_specs=(HBM_ONLY,) * (2 * n),
        input_output_aliases={a: a for a in range(2 * n)},
        compiler_params=pltpu.CompilerParams(has_side_effects=EFFECT),
    )(*thru, ssem, rsem, after)
    return list(res[:n]), list(res[n:])


def _adamw_ada(sc_all, dd, w, m, v, tr, name, after=None):
    R, C = w.shape

    def core(ins, outs, _):
        sc_ref, dd_ref, w_ref, m_ref, v_ref = ins[:5]
        g_ref, d_ref, m2_ref, v2_ref = outs
        g = _dot_tn(sc_ref[...].astype(BF16), dd_ref[...].astype(BF16))
        g_ref[...] = g
        d_ref[...], m2_ref[...], v2_ref[...] = _adamw_math(w_ref[...], g, m_ref[...], v_ref[...])

    blk = pl.BlockSpec((tr, C), lambda i: (i, 0))
    return _call(
        core, name=name, grid=(R // tr,),
        in_specs=[pl.BlockSpec((64, tr), lambda i: (0, i)), pl.BlockSpec((64, C), lambda i: (0, 0)), blk, blk, blk]
        + [HBM] * (after is not None),
        out_specs=[blk, blk, blk, blk], out_shape=[jax.ShapeDtypeStruct((R, C), F32)] * 4,
        args=[sc_all, dd, w, m, v] + [after] * (after is not None))[0]


def _adamw_small(gathered, plain, grads, wmv, emit, name):
    nw = len(grads)
    ng, npl, ne = len(gathered), len(plain), len(emit)

    def core(ins, outs, _):
        srcs = []
        for a in range(ng):
            s = ins[a][0]
            for dev in range(1, NDEV):
                s = s + ins[a][dev]
            srcs.append(s)
        srcs += [ins[ng + a][...] for a in range(npl)]
        w_refs = ins[ng + npl:]
        for e, a in enumerate(emit):
            outs[e][...] = srcs[a]
        for t in range(nw):
            src, row = grads[t]
            g = srcs[src] if row is None else srcs[src][row:row + 1, :]
            w_ref, m_ref, v_ref = w_refs[3 * t:3 * t + 3]
            g_ref, d_ref, m2_ref, v2_ref = outs[ne + 4 * t:ne + 4 * t + 4]
            g_ref[...] = g
            d_ref[...], m2_ref[...], v2_ref[...] = _adamw_math(w_ref[...], g, m_ref[...], v_ref[...])

    out_shape = [jax.ShapeDtypeStruct(gathered[a].shape[1:], F32) for a in emit]
    for t in range(nw):
        out_shape += [jax.ShapeDtypeStruct(wmv[3 * t].shape, F32)] * 4
    return _call(
        core, name=name, grid=(), in_specs=[VM] * (ng + npl + 3 * nw), out_specs=[VM] * (ne + 4 * nw),
        out_shape=out_shape, args=list(gathered) + list(plain) + list(wmv))[0]


def _ada_fwd(c_pad, w_ada, b_cols, cw_pad, jobs=()):
    def core(ins, outs, scs):
        c_ref, w_ref, b_ref, cwp_ref = ins
        ada_ref, sc_ref, cw_ref = outs
        cbuf, send_buf, ssem, rsem = scs
        me = _me()
        mi = _lin(me)
        cbuf[mi] = c_ref[...]
        cw_ref[mi] = cwp_ref[...]
        peers = [_flip(me, f) for f in FLIPS]
        first = []
        for k, p in enumerate(peers):
            first.append(_remote(cbuf.at[mi], cbuf.at[mi], ssem.at[k], rsem.at[k], p))
            first.append(_remote(cw_ref.at[mi], cw_ref.at[mi], ssem.at[7 + k], rsem.at[7 + k], p))
        for cp in first:
            cp.start()
        for k, p in enumerate(peers):
            pi = _lin(p)
            _remote(cbuf.at[pi], cbuf.at[pi], ssem.at[k], rsem.at[k], p).wait_recv()
            _remote(cw_ref.at[pi], cw_ref.at[pi], ssem.at[7 + k], rsem.at[7 + k], p).wait_recv()
        c_all = cbuf[...].reshape(8 * 8, D)
        sc = c_all * _sigmoid(c_all)
        sc_ref[...] = sc
        res = _dot(sc.astype(BF16), w_ref[...].astype(BF16)) + b_ref[...]
        send_buf[...] = res.reshape(8, 8, ADA_B)
        ada_ref[mi] = send_buf[mi]
        second = []
        for k, p in enumerate(peers):
            second.append(_remote(send_buf.at[_lin(p)], ada_ref.at[mi], ssem.at[14 + k], rsem.at[14 + k], p))
        for cp in second:
            cp.start()
        for k, p in enumerate(peers):
            _remote(send_buf.at[mi], ada_ref.at[_lin(p)], ssem.at[14 + k], rsem.at[14 + k], p).wait_recv()
        for cp in first + second:
            cp.wait_send()

    return _call(
        core, name="ada_fwd", grid=(), jobs=jobs, in_specs=[VM, VM, VM, VM], out_specs=[VM, VM, VM],
        out_shape=[jax.ShapeDtypeStruct((8, 8, ADA_B), F32), jax.ShapeDtypeStruct((64, D), F32),
                   jax.ShapeDtypeStruct((8, 32, 64), F32)],
        scratch=[pltpu.VMEM((8, 8, D), F32), pltpu.VMEM((8, 8, ADA_B), F32),
                 pltpu.SemaphoreType.DMA((21,)), pltpu.SemaphoreType.DMA((21,))],
        args=[c_pad, w_ada, b_cols, cw_pad])


def _ada_bwd(dada, jobs=()):
    def core(ins, outs, scs):
        (d_ref,) = ins
        dd_ref, gb_ref = outs
        rbuf, ssem, rsem = scs
        me = _me()
        mi = _lin(me)
        peers = [_flip(me, f) for f in FLIPS]
        rbuf[mi] = d_ref[mi]
        first = []
        for k, p in enumerate(peers):
            first.append(_remote(d_ref.at[_lin(p)], rbuf.at[mi], ssem.at[k], rsem.at[k], p))
        for cp in first:
            cp.start()
        for k, p in enumerate(peers):
            _remote(d_ref.at[mi], rbuf.at[_lin(p)], ssem.at[k], rsem.at[k], p).wait_recv()
        dd = rbuf[...].reshape(64, ADA_B)
        dd_ref[...] = dd
        gb_ref[mi] = jnp.broadcast_to(_colsum(dd), (8, ADA_B))
        second = []
        for k, p in enumerate(peers):
            second.append(_remote(gb_ref.at[mi], gb_ref.at[mi], ssem.at[7 + k], rsem.at[7 + k], p))
        for cp in second:
            cp.start()
        for k, p in enumerate(peers):
            pi = _lin(p)
            _remote(gb_ref.at[pi], gb_ref.at[pi], ssem.at[7 + k], rsem.at[7 + k], p).wait_recv()
        for cp in first + second:
            cp.wait_send()

    return _call(
        core, name="ada_bwd", grid=(), jobs=jobs, in_specs=[VM], out_specs=[VM, VM],
        out_shape=[jax.ShapeDtypeStruct((64, ADA_B), F32), jax.ShapeDtypeStruct((8, 8, ADA_B), F32)],
        scratch=[pltpu.VMEM((8, 8, ADA_B), F32), pltpu.SemaphoreType.DMA((14,)), pltpu.SemaphoreType.DMA((14,))],
        args=[dada])


SMALL_D = ("g_pre_f1", "g_post_f1", "g_pre_m", "g_post_m", "g_pre_f2", "g_post_f2")
SMALL_W = ("gmlp_norm_g", "gmlp_norm_b", "conv_b", "conv_norm_g", "conv_norm_b", "g_out_a", "g_out_b")


def kernel(x, c, w_ada, b_ada, g_pre_f1, g_post_f1, w_f1_in, w_f1_out, g_pre_m, g_post_m, w_mix_in, gmlp_norm_g, gmlp_norm_b, w_spatial, b_spatial, conv_w, conv_b, conv_norm_g, conv_norm_b, g_out_a, g_out_b, w_mix_out, g_pre_f2, g_post_f2, w_f2_in, w_f2_out, loss_target, m_w_ada, m_b_ada, m_g_pre_f1, m_g_post_f1, m_w_f1_in, m_w_f1_out, m_g_pre_m, m_g_post_m, m_w_mix_in, m_gmlp_norm_g, m_gmlp_norm_b, m_w_spatial, m_b_spatial, m_conv_w, m_conv_b, m_conv_norm_g, m_conv_norm_b, m_g_out_a, m_g_out_b, m_w_mix_out, m_g_pre_f2, m_g_post_f2, m_w_f2_in, m_w_f2_out, v_w_ada, v_b_ada, v_g_pre_f1, v_g_post_f1, v_w_f1_in, v_w_f1_out, v_g_pre_m, v_g_post_m, v_w_mix_in, v_gmlp_norm_g, v_gmlp_norm_b, v_w_spatial, v_b_spatial, v_conv_w, v_conv_b, v_conv_norm_g, v_conv_norm_b, v_g_out_a, v_g_out_b, v_w_mix_out, v_g_pre_f2, v_g_post_f2, v_w_f2_in, v_w_f2_out):
    given = dict(locals())
    bl, seq, _ = x.shape
    T = bl * seq
    tm = min(256, seq // 2)
    mi = _lin((lax.axis_index("x"), lax.axis_index("y"), lax.axis_index("c")))

    def shard_in(w):
        return jnp.pad(w[0].T.astype(BF16), ((0, FBP - FB), (0, 0)))

    zpad = jnp.zeros((max(FBP - FB, 16), D), BF16)
    g_f1 = _Gather([shard_in(w_f1_in), w_f1_out[0].astype(BF16)], ("rows", "out"), zpad)
    g_mx = _Gather([w_mix_in[0].astype(BF16), w_mix_out[0].astype(BF16), w_f2_out[0].astype(BF16)],
                   ("rows", "rows", "out"), zpad)
    g_f2 = _Gather([shard_in(w_f2_in)], ("rows",), zpad, late_mid=True)

    c_pad = jnp.pad(c, ((0, 8 - bl), (0, 0)))
    b_cols = lax.dynamic_slice(b_ada, (0, mi * ADA_B), (1, ADA_B))
    cw_pad = jnp.pad(conv_w[0], ((0, 1), (0, 0)))
    (ada_blk, sc_all, cw_all), ((wi1, wo1),) = _ada_fwd(c_pad, w_ada[0], b_cols, cw_pad, jobs=[g_f1])
    ada = ada_blk[:, 0:bl, :].transpose(1, 0, 2).reshape(bl, 9, D)
    pad5 = jnp.zeros((bl, 5, D), F32)
    mod1 = jnp.concatenate([ada[:, 0:3], pad5], axis=1)
    mod2 = jnp.concatenate([ada[:, 3:6], pad5], axis=1)
    mod3 = jnp.concatenate([ada[:, 6:9], pad5], axis=1)
    cw_full = cw_all.transpose(1, 0, 2).reshape(32, WA)

    zrow = jnp.zeros((1, D), F32)
    gv1 = jnp.concatenate([g_pre_f1, g_post_f1] + [zrow] * 6, axis=0)
    gvm = jnp.concatenate([g_pre_m, g_post_m] + [zrow] * 6, axis=0)
    gv2 = jnp.concatenate([g_pre_f2, g_post_f2] + [zrow] * 6, axis=0)
    v512 = jnp.concatenate([gmlp_norm_g, gmlp_norm_b, conv_b, conv_norm_g, conv_norm_b, g_out_a, g_out_b,
                            jnp.zeros((1, WA), F32)], axis=0)
    ws = w_spatial[0]
    bias_full = jnp.repeat(b_spatial[0].T, HD, axis=1)
    esel = (lax.broadcasted_iota(jnp.int32, (8, WA), 1) // HD == lax.broadcasted_iota(jnp.int32, (8, WA), 0)).astype(F32)

    x0 = x.reshape(T, D)
    (x1, gu1, y1), ((wmi, wmo, wo2),) = _ffn_fwd(x0, mod1, gv1, wi1, wo1, tm, "ffn1_fwd", jobs=[g_mx])
    wmo = wmo.reshape(D, D)
    (x2, proj, ym, conv), ((wi2,),) = _mixer_fwd(x1, mod2, gvm, wmi, wmo, v512, ws, bias_full, cw_full, tm, "mixer_fwd", jobs=[g_f2])

    (dx2, dg2, act2, hb2, dyb2, mg3, vg3, loss_blk), _ = _ffn_last(
        x2, loss_target.reshape(T, D), mod3, gv2, wi2, wo2, tm, "ffn2_fwd_bwd")
    (g_wi2,), _ = _grad_w_in(dg2, hb2, "ffn2_gw_in")
    (g_wo2,), _ = _grad_w_out(act2, dyb2, "ffn2_gw_out")
    (dpart, dymb, ycat, mg2a, vgma, v5g, gws, gbs), ((p_wi2,),) = _mixer_bwd_a(
        dx2, ym, proj, conv, mod2, gvm, wmo, v512, ws, bias_full, esel, tm, "mixer_bwd_a",
        jobs=[_ChipScatter([g_wi2])])
    (dx1, dproj, hbm, mg2b, vgmb, dcw), ((p_wo2,),) = _mixer_bwd_b(
        dx2, x1, dpart, proj, mod2, gvm, wmi, cw_full, tm, "mixer_bwd_b", jobs=[_ChipScatter([g_wo2])])
    (g_wmi,), _ = _grad_w_mi(hbm, dproj, "mixer_gw_in")
    (g_wmo,), _ = _grad_w_mo(ycat, dymb, "mixer_gw_out")
    p2 = jnp.concatenate([v5g, dcw], axis=0)
    (dx0, dg1, act1, hb1, dyb1, mg1, vg1), _ = _ffn_bwd(dx1, x0, y1, gu1, mod1, gv1, wi1, wo1, tm, "ffn1_bwd")

    dada = jnp.concatenate([mg1[:, 0:3], mg2b[:, 0:2], mg2a[:, 2:3], mg3[:, 0:3]], axis=1)
    dada = dada.reshape(bl, NDEV, ADA_B).transpose(1, 0, 2)
    dada = jnp.pad(dada, ((0, 0), (0, 8 - bl), (0, 0)))
    p1 = jnp.concatenate([vg1[0:2], vgmb[0:1], vgma[1:2], vg3[0:2], loss_blk[0:1], zrow], axis=0)
    (dd_all, gb_all), ((a1,),) = _ada_bwd(dada, jobs=[_AllGather([p1])])
    g_bada = gb_all[:, 0, :].reshape(1, 9 * D)

    (g_wo1,), ((p_wmi, p_wmo),) = _grad_w_out(act1, dyb1, "ffn1_gw_out", jobs=[_ChipScatter([g_wmi, g_wmo])])
    (g_wi1,), ((a2, a3, a4), (p_wo1,)) = _grad_w_in(
        dg1, hb1, "ffn1_gw_in", jobs=[_Gather([p2, gws, gbs], ("rows",) * 3, zpad), _ChipScatter([g_wo1])])

    h_f1, token = _chip_scatter_start([g_wi1], "tail_start")

    res = {}
    quad = _adamw_reduce(p_wi2, w_f2_in[0].T, m_w_f2_in[0].T, v_w_f2_in[0].T, FO, "adamw_w_f2_in", after=token)
    res["w_f2_in"] = tuple(t.T[None] for t in quad)
    for nm, part, tr in (("w_f2_out", p_wo2, FO), ("w_mix_in", p_wmi, 256), ("w_mix_out", p_wmo, MO), ("w_f1_out", p_wo1, FO)):
        quad = _adamw_reduce(part, given[nm][0], given["m_" + nm][0], given["v_" + nm][0], tr, "adamw_" + nm, after=quad[1])
        res[nm] = tuple(t[None] for t in quad)
    quad = _adamw_ada(sc_all, dd_all, w_ada[0], m_w_ada[0], v_w_ada[0], 256, "adamw_w_ada", after=quad[1])
    res["w_ada"] = tuple(t[None] for t in quad)
    (g_wi1,), (p_wi1,) = _chip_scatter_wait(h_f1, quad[1], "tail_wait")
    quad = _adamw_reduce(p_wi1, w_f1_in[0].T, m_w_f1_in[0].T, v_w_f1_in[0].T, FO, "adamw_w_f1_in", own=g_wi1)
    res["w_f1_in"] = tuple(t.T[None] for t in quad)

    small = SMALL_D + SMALL_W + ("w_spatial", "b_spatial", "b_ada")
    grads = [(0, r) for r in range(6)] + [(1, r) for r in range(7)] + [(2, None), (3, None), (4, None)]
    wmv = []
    for nm in small:
        for pre in ("", "m_", "v_"):
            wmv.append(given[pre + nm][0] if nm in ("w_spatial", "b_spatial") else given[pre + nm])
    outs = _adamw_small([a1, a2, a3, a4], [g_bada], grads, wmv, (0, 1), "adamw_small")
    loss = outs[0][6, 0]
    for t, nm in enumerate(small):
        quad = outs[2 + 4 * t:6 + 4 * t]
        res[nm] = tuple(q[None] for q in quad) if nm in ("w_spatial", "b_spatial") else tuple(quad)
    g_cw = lax.dynamic_slice(outs[1], (8, mi * 64), (32, 64))
    wmv = [jnp.pad(given[pre + "conv_w"][0], ((0, 1), (0, 0)), constant_values=1.0 if pre == "v_" else 0.0)
           for pre in ("", "m_", "v_")]
    quad = _adamw_small([], [g_cw], [(0, None)], wmv, (), "adamw_conv_w")
    res["conv_w"] = tuple(q[0:CONV_K][None] for q in quad)

    order = ["w_ada", "b_ada", "g_pre_f1", "g_post_f1", "w_f1_in", "w_f1_out", "g_pre_m", "g_post_m", "w_mix_in",
             "gmlp_norm_g", "gmlp_norm_b", "w_spatial", "b_spatial", "conv_w", "conv_b", "conv_norm_g", "conv_norm_b",
             "g_out_a", "g_out_b", "w_mix_out", "g_pre_f2", "g_post_f2", "w_f2_in", "w_f2_out"]
    out = [loss, dx0.reshape(bl, seq, D)]
    for k in range(4):
        out += [res[nm][k] for nm in order]
    return tuple(out)
```

```python
import jax
import jax.numpy as jnp
from jax import lax
from jax.experimental import pallas as pl
from jax.experimental.pallas import tpu as pltpu

F32 = jnp.float32
BF16 = jnp.bfloat16

D = 1024
DFF = 2816
NDEV = 8
FB = 2 * DFF // NDEV
FBP = 704
FO = DFF // NDEV
WA = 512
NHEAD = 8
HD = 64
CHUNK = 128
CONV_K = 31
HALO = 32
MB = 2 * (WA + WA) // NDEV
MO = D // NDEV
ADA_B = 9 * D // NDEV
EPS = 1e-6
HALF = 0.5

ADAM_LR = 0.001
ADAM_B1 = 0.9
ADAM_B2 = 0.999
ADAM_EPS = 1e-08
ADAM_WD = 0.01
ADAM_STEP = 10

VMEM_LIMIT = 56 * 1024 * 1024
MESH = pl.DeviceIdType.MESH
FLIPS = ((0, 0, 1), (1, 0, 0), (0, 1, 0), (1, 1, 0), (1, 0, 1), (0, 1, 1), (1, 1, 1))
CHIP_FLIPS = ((1, 0, 0), (0, 1, 0), (1, 1, 0))
HBM = pl.BlockSpec(memory_space=pl.ANY)
VM = pl.BlockSpec(memory_space=pltpu.VMEM)


def _dot(a, b):
    return lax.dot_general(a, b, (((1,), (0,)), ((), ())), preferred_element_type=F32)


def _dot_nt(a, b):
    return lax.dot_general(a, b, (((1,), (1,)), ((), ())), preferred_element_type=F32)


def _dot_tn(a, b):
    return lax.dot_general(a, b, (((0,), (0,)), ((), ())), preferred_element_type=F32)


def _rowmean(v):
    return jnp.mean(v, axis=-1, keepdims=True)


def _colsum(v):
    return jnp.sum(v, axis=0, keepdims=True)


def _sigmoid(v):
    return 0.5 * jnp.tanh(0.5 * v) + 0.5


def _const_spec(shape):
    nd = len(shape)
    return pl.BlockSpec(shape, lambda *_: (0,) * nd, pipeline_mode=pl.Buffered(1))


def _me():
    return lax.axis_index("x"), lax.axis_index("y"), lax.axis_index("c")


def _flip(me, f):
    return tuple(1 - v if b else v for v, b in zip(me, f))


def _lin(p):
    return 4 * p[0] + 2 * p[1] + p[2]


def _remote(src, dst, send_sem, recv_sem, dev):
    return pltpu.make_async_remote_copy(src_ref=src, dst_ref=dst, send_sem=send_sem, recv_sem=recv_sem,
                                        device_id=dev, device_id_type=MESH)


def _blk(kind, ref, p):
    if kind == "out":
        return ref.at[2 * p[0] + p[1], pl.ds(p[2] * FO, FO), :]
    return ref.at[_lin(p)]


class _Gather:
    def __init__(self, shards, kinds, zpad, late_mid=False):
        self.late_mid = late_mid
        self.kinds = kinds
        self.n = len(shards)
        self.ins = list(shards) + [zpad]
        self.out_shape = [jax.ShapeDtypeStruct((4, FBP, D) if k == "out" else (NDEV,) + s.shape, s.dtype)
                          for s, k in zip(shards, kinds)]
        self.n_out = sum(k == "out" for k in kinds)
        self.sems = [pltpu.SemaphoreType.DMA((7 * self.n,)), pltpu.SemaphoreType.DMA((7 * self.n,)),
                     pltpu.SemaphoreType.DMA((self.n + 4 * max(self.n_out, 1),))]

    def _first(self, ins, outs, sems):
        ssem, rsem, lsem = sems
        me = _me()
        sib = _flip(me, (0, 0, 1))
        cps, loc = [], []
        nz = 0
        for a in range(self.n):
            mine = _blk(self.kinds[a], outs[a], me)
            loc.append(pltpu.make_async_copy(ins[a], mine, lsem.at[a]))
            if self.kinds[a] == "out" and FBP > FB:
                for q in range(4):
                    loc.append(pltpu.make_async_copy(ins[self.n], outs[a].at[q, pl.ds(FB, FBP - FB), :],
                                                     lsem.at[self.n + 4 * nz + q]))
                nz += 1
            cps.append(_remote(ins[a], mine, ssem.at[7 * a], rsem.at[7 * a], sib))
            for j, f in enumerate(CHIP_FLIPS):
                cps.append(_remote(ins[a], mine, ssem.at[7 * a + 1 + j], rsem.at[7 * a + 1 + j], _flip(me, f)))
        return cps, loc

    def _passed(self, outs, sems):
        ssem, rsem, _ = sems
        me = _me()
        sib = _flip(me, (0, 0, 1))
        cps = []
        for j, f in enumerate(CHIP_FLIPS):
            for a in range(self.n):
                blk = _blk(self.kinds[a], outs[a], _flip(me, f))
                cps.append(_remote(blk, blk, ssem.at[7 * a + 4 + j], rsem.at[7 * a + 4 + j], sib))
        return cps

    def start(self, ins, outs, sems):
        cps, loc = self._first(ins, outs, sems)
        for cp in loc + cps:
            cp.start()

    def mid(self, ins, outs, sems):
        ssem, rsem, _ = sems
        me = _me()
        passed = self._passed(outs, sems)
        t = 0
        for j, f in enumerate(CHIP_FLIPS):
            for a in range(self.n):
                blk = _blk(self.kinds[a], outs[a], _flip(me, f))
                _remote(blk, blk, ssem.at[7 * a + 1 + j], rsem.at[7 * a + 1 + j], _flip(me, f)).wait_recv()
                passed[t].start()
                t += 1

    def end(self, ins, outs, sems):
        ssem, rsem, _ = sems
        me = _me()
        sib = _flip(me, (0, 0, 1))
        for a in range(self.n):
            blk = _blk(self.kinds[a], outs[a], sib)
            _remote(blk, blk, ssem.at[7 * a], rsem.at[7 * a], sib).wait_recv()
            for j, f in enumerate(CHIP_FLIPS):
                blk = _blk(self.kinds[a], outs[a], _flip(_flip(me, f), (0, 0, 1)))
                _remote(blk, blk, ssem.at[7 * a + 4 + j], rsem.at[7 * a + 4 + j], sib).wait_recv()
        cps, loc = self._first(ins, outs, sems)
        for cp in cps + self._passed(outs, sems):
            cp.wait_send()
        for cp in loc:
            cp.wait()


class _ChipScatter:
    def __init__(self, grads):
        self.n = len(grads)
        self.ins = list(grads)
        self.out_shape = [jax.ShapeDtypeStruct(g.shape, BF16) for g in grads]
        self.sems = [pltpu.SemaphoreType.DMA((3 * self.n,)), pltpu.SemaphoreType.DMA((3 * self.n,)),
                     pltpu.SemaphoreType.DMA((self.n,))]

    def _copies(self, ins, outs, sems):
        ssem, rsem, lsem = sems
        me = _me()
        mq = 2 * me[0] + me[1]
        loc = [pltpu.make_async_copy(ins[a].at[mq], outs[a].at[mq], lsem.at[a]) for a in range(self.n)]
        cps = []
        for k, f in enumerate(CHIP_FLIPS):
            p = _flip(me, f)
            for a in range(self.n):
                cps.append(_remote(ins[a].at[2 * p[0] + p[1]], outs[a].at[mq], ssem.at[3 * a + k], rsem.at[3 * a + k], p))
        return cps, loc

    def start(self, ins, outs, sems):
        cps, loc = self._copies(ins, outs, sems)
        for cp in loc + cps:
            cp.start()

    mid = None

    def end(self, ins, outs, sems):
        ssem, rsem, _ = sems
        me = _me()
        mq = 2 * me[0] + me[1]
        for k, f in enumerate(CHIP_FLIPS):
            p = _flip(me, f)
            for a in range(self.n):
                _remote(ins[a].at[mq], outs[a].at[2 * p[0] + p[1]], ssem.at[3 * a + k], rsem.at[3 * a + k], p).wait_recv()
        cps, loc = self._copies(ins, outs, sems)
        for cp in cps:
            cp.wait_send()
        for cp in loc:
            cp.wait()


class _AllGather:
    def __init__(self, parts):
        self.n = len(parts)
        self.ins = list(parts)
        self.out_shape = [jax.ShapeDtypeStruct((NDEV,) + p.shape, p.dtype) for p in parts]
        self.sems = [pltpu.SemaphoreType.DMA((7 * self.n,)), pltpu.SemaphoreType.DMA((7 * self.n,)),
                     pltpu.SemaphoreType.DMA((self.n,))]

    def _copies(self, ins, outs, sems):
        ssem, rsem, lsem = sems
        me = _me()
        mi = _lin(me)
        loc = [pltpu.make_async_copy(ins[a], outs[a].at[mi], lsem.at[a]) for a in range(self.n)]
        cps = []
        for k, f in enumerate(FLIPS):
            for a in range(self.n):
                cps.append(_remote(ins[a], outs[a].at[mi], ssem.at[7 * a + k], rsem.at[7 * a + k], _flip(me, f)))
        return cps, loc

    def start(self, ins, outs, sems):
        cps, loc = self._copies(ins, outs, sems)
        for cp in loc + cps:
            cp.start()

    mid = None

    def end(self, ins, outs, sems):
        ssem, rsem, _ = sems
        me = _me()
        for k, f in enumerate(FLIPS):
            p = _flip(me, f)
            for a in range(self.n):
                _remote(ins[a], outs[a].at[_lin(p)], ssem.at[7 * a + k], rsem.at[7 * a + k], p).wait_recv()
        cps, loc = self._copies(ins, outs, sems)
        for cp in cps:
            cp.wait_send()
        for cp in loc:
            cp.wait()


def _call(core, *, name, grid, in_specs, out_specs, out_shape, args, scratch=(), jobs=()):
    n_in, n_out, n_sc = len(in_specs), len(out_specs), len(scratch)
    steps = 1
    for g in grid:
        steps *= g

    def body(*refs):
        pos = [0]

        def take(k):
            r = refs[pos[0]:pos[0] + k]
            pos[0] += k
            return r

        ins = take(n_in)
        j_ins = [take(len(j.ins)) for j in jobs]
        outs = take(n_out)
        j_outs = [take(len(j.out_shape)) for j in jobs]
        scs = take(n_sc)
        j_sems = [take(len(j.sems)) for j in jobs]
        if len(grid) == 2:
            step = pl.program_id(0) * grid[1] + pl.program_id(1)
        elif len(grid) == 1:
            step = pl.program_id(0)
        else:
            step = 0
        for j, ji, jo, js in zip(jobs, j_ins, j_outs, j_sems):
            if grid:
                pl.when(step == 0)(lambda j=j, ji=ji, jo=jo, js=js: j.start(ji, jo, js))
            else:
                j.start(ji, jo, js)
        for j, ji, jo, js in zip(jobs, j_ins, j_outs, j_sems):
            if j.mid is not None and grid:
                at = max(steps - 2, 0) if j.late_mid else (3 * steps) // 4
                pl.when(step == at)(lambda j=j, ji=ji, jo=jo, js=js: j.mid(ji, jo, js))
        if core is not None:
            core(ins, outs, scs)
        for j, ji, jo, js in zip(jobs, j_ins, j_outs, j_sems):
            if grid:
                pl.when(step == steps - 1)(lambda j=j, ji=ji, jo=jo, js=js: j.end(ji, jo, js))
            else:
                if j.mid is not None:
                    j.mid(ji, jo, js)
                j.end(ji, jo, js)

    all_in = list(in_specs)
    all_args = list(args)
    all_out = list(out_specs)
    all_shape = list(out_shape)
    all_sc = list(scratch)
    for j in jobs:
        all_in += [HBM] * len(j.ins)
        all_args += j.ins
    for j in jobs:
        all_out += [HBM] * len(j.out_shape)
        all_shape += j.out_shape
        all_sc += j.sems
    params = dict(vmem_limit_bytes=VMEM_LIMIT)
    if grid:
        params["dimension_semantics"] = ("arbitrary",) * len(grid)
    res = pl.pallas_call(
        body, name=name, grid=grid, in_specs=all_in, out_specs=all_out, out_shape=all_shape,
        scratch_shapes=all_sc, compiler_params=pltpu.CompilerParams(**params),
    )(*all_args)
    core_res = list(res[:n_out])
    job_res = []
    pos = n_out
    for j in jobs:
        job_res.append(list(res[pos:pos + len(j.out_shape)]))
        pos += len(j.out_shape)
    return core_res, job_res


def _ffn_fwd(x, mod, gvec, w_in, w_out, tm, name, jobs=()):
    T = x.shape[0]
    nt = T // tm
    tps = nt // mod.shape[0]

    def core(ins, outs, _):
        x_ref, mod_ref, g_ref, win_ref, wout_ref = ins
        xo_ref, gu_ref, y_ref = outs
        xv = x_ref[...]
        sh, sc, gt = mod_ref[0:1, :], mod_ref[1:2, :], mod_ref[2:3, :]
        r = lax.rsqrt(_rowmean(xv * xv) + EPS)
        h = (xv * r * g_ref[0:1, :]) * (1.0 + sc) + sh
        hb = h.astype(BF16)
        y = jnp.zeros((tm, D), F32)
        for cidx in range(4):
            gate = _dot_nt(hb, win_ref[cidx])
            up = _dot_nt(hb, win_ref[4 + cidx])
            gu_ref[cidx] = gate.astype(BF16)
            gu_ref[4 + cidx] = up.astype(BF16)
            act = gate * _sigmoid(gate) * up
            y = y + _dot(act.astype(BF16), wout_ref[cidx])
        y_ref[...] = y
        ry = lax.rsqrt(_rowmean(y * y) + EPS)
        xo_ref[...] = xv + (HALF * gt) * (y * ry * g_ref[1:2, :])

    tile = pl.BlockSpec((tm, D), lambda i: (i, 0))
    return _call(
        core, name=name, grid=(nt,), jobs=jobs,
        in_specs=[tile, pl.BlockSpec((None, 8, D), lambda i: (i // tps, 0, 0)), _const_spec((8, D)),
                  _const_spec((8, FBP, D)), _const_spec((4, FBP, D))],
        out_specs=[tile, pl.BlockSpec((8, tm, FBP), lambda i: (0, i, 0)), tile],
        out_shape=[jax.ShapeDtypeStruct((T, D), F32), jax.ShapeDtypeStruct((8, T, FBP), BF16),
                   jax.ShapeDtypeStruct((T, D), F32)],
        args=[x, mod, gvec, w_in, w_out])


def _ffn_bwd(dxo, x, y, gu, mod, gvec, w_in, w_out, tm, name, jobs=()):
    T = x.shape[0]
    nt = T // tm
    nb = mod.shape[0]
    tps = nt // nb

    def core(ins, outs, _):
        dxo_ref, x_ref, y_ref, gu_ref, mod_ref, g_ref, win_ref, wout_ref = ins
        dx_ref, dg_ref, act_ref, hb_ref, dyb_ref, mg_ref, vg_ref = outs
        i = pl.program_id(0)
        xv = x_ref[...]
        dxo_v = dxo_ref[...]
        yv = y_ref[...]
        sh, sc, gt = mod_ref[0:1, :], mod_ref[1:2, :], mod_ref[2:3, :]
        gpre, gpost = g_ref[0:1, :], g_ref[1:2, :]
        r = lax.rsqrt(_rowmean(xv * xv) + EPS)
        xh = xv * r
        n = xh * gpre
        hb = (n * (1.0 + sc) + sh).astype(BF16)
        hb_ref[...] = hb
        ry = lax.rsqrt(_rowmean(yv * yv) + EPS)
        yh = yv * ry
        d_gt = _colsum(HALF * dxo_v * (yh * gpost))
        dp = (HALF * gt) * dxo_v
        d_gpost = _colsum(dp * yh)
        dyh = dp * gpost
        dy = ry * (dyh - yh * _rowmean(dyh * yh))
        dyb = dy.astype(BF16)
        dyb_ref[...] = dyb
        dh = jnp.zeros((tm, D), F32)
        for cidx in range(4):
            gate = gu_ref[cidx].astype(F32)
            up = gu_ref[4 + cidx].astype(F32)
            sig = _sigmoid(gate)
            s = gate * sig
            act_ref[cidx] = (s * up).astype(BF16)
            d_act = _dot_nt(dyb, wout_ref[cidx])
            d_up = (d_act * s).astype(BF16)
            d_gate = (d_act * up * (sig * (1.0 + gate * (1.0 - sig)))).astype(BF16)
            dg_ref[cidx] = d_gate
            dg_ref[4 + cidx] = d_up
            dh = dh + _dot(d_gate, win_ref[cidx]) + _dot(d_up, win_ref[4 + cidx])
        d_sc = _colsum(dh * n)
        d_sh = _colsum(dh)
        dn = dh * (1.0 + sc)
        d_gpre = _colsum(dn * xh)
        dxh = dn * gpre
        dx_ref[...] = dxo_v + r * (dxh - xh * _rowmean(dxh * xh))

        @pl.when(i % tps == 0)
        def _():
            mg_ref[...] = jnp.zeros((8, D), F32)

        @pl.when(i == 0)
        def _():
            vg_ref[...] = jnp.zeros((8, D), F32)

        mg_ref[0:1, :] += d_sh
        mg_ref[1:2, :] += d_sc
        mg_ref[2:3, :] += d_gt
        vg_ref[0:1, :] += d_gpre
        vg_ref[1:2, :] += d_gpost

    tile = pl.BlockSpec((tm, D), lambda i: (i, 0))
    return _call(
        core, name=name, grid=(nt,), jobs=jobs,
        in_specs=[tile, tile, tile, pl.BlockSpec((8, tm, FBP), lambda i: (0, i, 0)),
                  pl.BlockSpec((None, 8, D), lambda i: (i // tps, 0, 0)), _const_spec((8, D)),
                  _const_spec((8, FBP, D)), _const_spec((4, FBP, D))],
        out_specs=[tile, pl.BlockSpec((8, tm, FBP), lambda i: (0, i, 0)),
                   pl.BlockSpec((4, tm, FBP), lambda i: (0, i, 0)), tile, tile,
                   pl.BlockSpec((None, 8, D), lambda i: (i // tps, 0, 0)), pl.BlockSpec((8, D), lambda i: (0, 0))],
        out_shape=[jax.ShapeDtypeStruct((T, D), F32), jax.ShapeDtypeStruct((8, T, FBP), BF16),
                   jax.ShapeDtypeStruct((4, T, FBP), BF16), jax.ShapeDtypeStruct((T, D), BF16),
                   jax.ShapeDtypeStruct((T, D), BF16), jax.ShapeDtypeStruct((nb, 8, D), F32),
                   jax.ShapeDtypeStruct((8, D), F32)],
        args=[dxo, x, y, gu, mod, gvec, w_in, w_out])


def _ffn_last(x, target, mod, gvec, w_in, w_out, tm, name, jobs=()):
    T = x.shape[0]
    nt = T // tm
    nb = mod.shape[0]
    tps = nt // nb

    def core(ins, outs, scs):
        x_ref, t_ref, mod_ref, g_ref, win_ref, wout_ref = ins
        dx_ref, dg_ref, act_ref, hb_ref, dyb_ref, mg_ref, vg_ref, loss_ref = outs
        (gu_s,) = scs
        i = pl.program_id(0)
        xv = x_ref[...]
        sh, sc, gt = mod_ref[0:1, :], mod_ref[1:2, :], mod_ref[2:3, :]
        gpre, gpost = g_ref[0:1, :], g_ref[1:2, :]
        r = lax.rsqrt(_rowmean(xv * xv) + EPS)
        xh = xv * r
        n = xh * gpre
        hb = (n * (1.0 + sc) + sh).astype(BF16)
        hb_ref[...] = hb
        yv = jnp.zeros((tm, D), F32)
        for cidx in range(4):
            gate = _dot_nt(hb, win_ref[cidx])
            up = _dot_nt(hb, win_ref[4 + cidx])
            gu_s[cidx] = gate.astype(BF16)
            gu_s[4 + cidx] = up.astype(BF16)
            act = gate * _sigmoid(gate) * up
            act_ref[cidx] = act.astype(BF16)
            yv = yv + _dot(act_ref[cidx], wout_ref[cidx])
        ry = lax.rsqrt(_rowmean(yv * yv) + EPS)
        yh = yv * ry
        pn = yh * gpost
        err = xv + (HALF * gt) * pn - t_ref[...]
        dxo_v = err * (1.0 / D)
        d_gt = _colsum(HALF * dxo_v * pn)
        dp = (HALF * gt) * dxo_v
        d_gpost = _colsum(dp * yh)
        dyh = dp * gpost
        dyb = (ry * (dyh - yh * _rowmean(dyh * yh))).astype(BF16)
        dyb_ref[...] = dyb
        dh = jnp.zeros((tm, D), F32)
        for cidx in range(4):
            gate = gu_s[cidx].astype(F32)
            up = gu_s[4 + cidx].astype(F32)
            sig = _sigmoid(gate)
            s = gate * sig
            d_act = _dot_nt(dyb, wout_ref[cidx])
            d_up = (d_act * s).astype(BF16)
            d_gate = (d_act * up * (sig * (1.0 + gate * (1.0 - sig)))).astype(BF16)
            dg_ref[cidx] = d_gate
            dg_ref[4 + cidx] = d_up
            dh = dh + _dot(d_gate, win_ref[cidx]) + _dot(d_up, win_ref[4 + cidx])
        d_sc = _colsum(dh * n)
        d_sh = _colsum(dh)
        dn = dh * (1.0 + sc)
        d_gpre = _colsum(dn * xh)
        dxh = dn * gpre
        dx_ref[...] = dxo_v + r * (dxh - xh * _rowmean(dxh * xh))

        @pl.when(i % tps == 0)
        def _():
            mg_ref[...] = jnp.zeros((8, D), F32)

        @pl.when(i == 0)
        def _():
            vg_ref[...] = jnp.zeros((8, D), F32)
            loss_ref[...] = jnp.zeros((8, D), F32)

        mg_ref[0:1, :] += d_sh
        mg_ref[1:2, :] += d_sc
        mg_ref[2:3, :] += d_gt
        vg_ref[0:1, :] += d_gpre
        vg_ref[1:2, :] += d_gpost
        loss_ref[...] += HALF * jnp.sum(_rowmean(err * err), axis=0, keepdims=True)

    tile = pl.BlockSpec((tm, D), lambda i: (i, 0))
    return _call(
        core, name=name, grid=(nt,), jobs=jobs,
        in_specs=[tile, tile, pl.BlockSpec((None, 8, D), lambda i: (i // tps, 0, 0)), _const_spec((8, D)),
                  _const_spec((8, FBP, D)), _const_spec((4, FBP, D))],
        out_specs=[tile, pl.BlockSpec((8, tm, FBP), lambda i: (0, i, 0)),
                   pl.BlockSpec((4, tm, FBP), lambda i: (0, i, 0)), tile, tile,
                   pl.BlockSpec((None, 8, D), lambda i: (i // tps, 0, 0)), pl.BlockSpec((8, D), lambda i: (0, 0)),
                   pl.BlockSpec((8, D), lambda i: (0, 0))],
        out_shape=[jax.ShapeDtypeStruct((T, D), F32), jax.ShapeDtypeStruct((8, T, FBP), BF16),
                   jax.ShapeDtypeStruct((4, T, FBP), BF16), jax.ShapeDtypeStruct((T, D), BF16),
                   jax.ShapeDtypeStruct((T, D), BF16), jax.ShapeDtypeStruct((nb, 8, D), F32),
                   jax.ShapeDtypeStruct((8, D), F32), jax.ShapeDtypeStruct((8, D), F32)],
        scratch=[pltpu.VMEM((8, tm, FBP), BF16)],
        args=[x, target, mod, gvec, w_in, w_out])


def _masked_spatial(ws_ref):
    row = lax.broadcasted_iota(jnp.int32, (CHUNK, CHUNK), 0)
    col = lax.broadcasted_iota(jnp.int32, (CHUNK, CHUNK), 1)
    keep = col <= row
    return [jnp.where(keep, ws_ref[hd], 0.0).astype(BF16) for hd in range(NHEAD)]


def _head_pairs(mats, right, transpose=False):
    first = lax.broadcasted_iota(jnp.int32, (CHUNK, 128), 1) < HD
    op = _dot_tn if transpose else _dot
    out = []
    for p in range(NHEAD // 2):
        slab = right[:, _lanes(p)]
        out.append(jnp.where(first, op(mats[2 * p], slab), op(mats[2 * p + 1], slab)))
    return jnp.concatenate(out, axis=1)


def _spatial_gate(wm, vb_chunk):
    return _head_pairs(wm, vb_chunk)


def _layer_norm_stats(v):
    mu = _rowmean(v)
    vc = v - mu
    rstd = lax.rsqrt(_rowmean(vc * vc) + EPS)
    return vc * rstd, rstd


def _pitch(tm):
    p = tm // 8
    while p % 8 != 4:
        p += 1
    return p


def _lanes(s):
    return slice(s * 128, (s + 1) * 128)


def _to_slabs(ref, row0, val):
    for s in range(4):
        ref[s, row0:row0 + val.shape[0], :] = val[:, _lanes(s)]


def _tap_sum(src, out, cw_ref, bias, tm, start):
    p = _pitch(tm)
    for s in range(4):
        accs = [jnp.broadcast_to(bias[:, _lanes(s)], (8, 128))] * p
        for k in range(CONV_K):
            w = jnp.broadcast_to(cw_ref[k:k + 1, _lanes(s)], (8, 128))
            for v in range(p):
                accs[v] = accs[v] + w * src[s, pl.ds(v + start(k), 8, stride=p), :]
        for v in range(p):
            out[s, pl.ds(v, 8, stride=p), :] = accs[v]
    return jnp.concatenate([out[s, 0:tm, :] for s in range(4)], axis=1)


def _mixer_fwd(x, mod, gvec, w_mi, w_mo, v512, ws, bias_full, cw, tm, name, jobs=()):
    T = x.shape[0]
    nt = T // tm
    tps = nt // mod.shape[0]
    ext_rows = 8 * _pitch(tm)

    def core(ins, outs, scs):
        x_ref, mod_ref, g_ref, wmi_ref, wmo_ref, v_ref, ws_ref, bias_ref, cw_ref = ins
        xo_ref, proj_ref, ym_ref, conv_ref = outs
        glu_ext, conv_scr = scs
        i = pl.program_id(0)
        xv = x_ref[...]
        sh, sc, gt = mod_ref[0:1, :], mod_ref[1:2, :], mod_ref[2:3, :]
        r = lax.rsqrt(_rowmean(xv * xv) + EPS)
        hb = ((xv * r * g_ref[0:1, :]) * (1.0 + sc) + sh).astype(BF16)
        for j in range(NDEV):
            proj_ref[:, j * MB:(j + 1) * MB] = _dot(hb, wmi_ref[j])
        u = proj_ref[:, 0:WA]
        v0 = proj_ref[:, WA:2 * WA]
        a = proj_ref[:, 2 * WA:3 * WA]
        g = proj_ref[:, 3 * WA:4 * WA]
        vh, _ = _layer_norm_stats(v0)
        vb = (vh * v_ref[0:1, :] + v_ref[1:2, :]).astype(BF16)
        wm = _masked_spatial(ws_ref)
        ya = []
        for q in range(tm // CHUNK):
            z = _spatial_gate(wm, vb[q * CHUNK:(q + 1) * CHUNK, :]) + bias_ref[...]
            ya.append(u[q * CHUNK:(q + 1) * CHUNK, :] * z)
        ya = jnp.concatenate(ya, axis=0)
        glu = a * _sigmoid(g)

        @pl.when(i == 0)
        def _():
            glu_ext[:, HALO + tm:HALO + ext_rows, :] = jnp.zeros((4, ext_rows - tm, 128), F32)

        @pl.when(i % tps == 0)
        def _():
            glu_ext[:, 0:HALO, :] = jnp.zeros((4, HALO, 128), F32)

        _to_slabs(glu_ext, HALO, glu)
        conv = _tap_sum(glu_ext, conv_scr, cw_ref, v_ref[2:3, :], tm, lambda k: HALO - (CONV_K - 1) + k)
        conv_ref[...] = conv
        glu_ext[:, 0:HALO, :] = glu_ext[:, tm:tm + HALO, :]
        ch, _ = _layer_norm_stats(conv)
        cn = ch * v_ref[3:4, :] + v_ref[4:5, :]
        yb = cn * _sigmoid(cn)
        pa = ya * lax.rsqrt(_rowmean(ya * ya) + EPS) * v_ref[5:6, :]
        pb = yb * lax.rsqrt(_rowmean(yb * yb) + EPS) * v_ref[6:7, :]
        ycat = jnp.concatenate([pa, pb], axis=1).astype(BF16)
        ym = _dot(ycat, wmo_ref[...])
        ym_ref[...] = ym
        rm = lax.rsqrt(_rowmean(ym * ym) + EPS)
        xo_ref[...] = xv + gt * (ym * rm * g_ref[1:2, :])

    tile = pl.BlockSpec((tm, D), lambda i: (i, 0))
    return _call(
        core, name=name, grid=(nt,), jobs=jobs,
        in_specs=[tile, pl.BlockSpec((None, 8, D), lambda i: (i // tps, 0, 0)), _const_spec((8, D)),
                  _const_spec((NDEV, D, MB)), _const_spec((D, D)), _const_spec((8, WA)),
                  _const_spec((NHEAD, CHUNK, CHUNK)), _const_spec((CHUNK, WA)), _const_spec((32, WA))],
        out_specs=[tile, pl.BlockSpec((tm, 4 * WA), lambda i: (i, 0)), tile, pl.BlockSpec((tm, WA), lambda i: (i, 0))],
        out_shape=[jax.ShapeDtypeStruct((T, D), F32), jax.ShapeDtypeStruct((T, 4 * WA), F32),
                   jax.ShapeDtypeStruct((T, D), F32), jax.ShapeDtypeStruct((T, WA), F32)],
        scratch=[pltpu.VMEM((4, HALO + ext_rows, 128), F32), pltpu.VMEM((4, ext_rows, 128), F32)],
        args=[x, mod, gvec, w_mi, w_mo, v512, ws, bias_full, cw])


def _mixer_bwd_a(dxo, ym, proj, conv, mod, gvec, w_mo, v512, ws, bias_full, esel, tm, name, jobs=()):
    T = dxo.shape[0]
    nt = T // tm
    nb = mod.shape[0]
    tps = nt // nb

    def core(ins, outs, scs):
        dxo_ref, ym_ref, proj_ref, conv_ref, mod_ref, g_ref, wmo_ref, v_ref, ws_ref, bias_ref, e_ref = ins
        dpart_ref, dymb_ref, ycat_ref, mg_ref, vg_ref, v5g_ref, gws_ref, gbs_ref = outs
        (dbs_acc,) = scs
        i = pl.program_id(0)
        dxo_v = dxo_ref[...]
        ymv = ym_ref[...]
        gt = mod_ref[2:3, :]
        gpost = g_ref[1:2, :]
        rm = lax.rsqrt(_rowmean(ymv * ymv) + EPS)
        ymh = ymv * rm
        d_gt = _colsum(dxo_v * (ymh * gpost))
        dpm = gt * dxo_v
        d_gpost = _colsum(dpm * ymh)
        dymh = dpm * gpost
        dym = (rm * (dymh - ymh * _rowmean(dymh * ymh))).astype(BF16)
        dymb_ref[...] = dym
        dycat = _dot_nt(dym, wmo_ref[...])
        u = proj_ref[:, 0:WA]
        v0 = proj_ref[:, WA:2 * WA]
        vh, rv = _layer_norm_stats(v0)
        vb = (vh * v_ref[0:1, :] + v_ref[1:2, :]).astype(BF16)
        wm = _masked_spatial(ws_ref)
        zs = []
        for q in range(tm // CHUNK):
            zs.append(_spatial_gate(wm, vb[q * CHUNK:(q + 1) * CHUNK, :]) + bias_ref[...])
        z = jnp.concatenate(zs, axis=0)
        ya = u * z
        ra = lax.rsqrt(_rowmean(ya * ya) + EPS)
        yah = ya * ra
        ch, rc = _layer_norm_stats(conv_ref[...])
        cn = ch * v_ref[3:4, :] + v_ref[4:5, :]
        sg = _sigmoid(cn)
        yb = cn * sg
        rb = lax.rsqrt(_rowmean(yb * yb) + EPS)
        ybh = yb * rb
        ycat_ref[...] = jnp.concatenate([yah * v_ref[5:6, :], ybh * v_ref[6:7, :]], axis=1).astype(BF16)
        dpa = dycat[:, 0:WA]
        dpb = dycat[:, WA:2 * WA]
        d_goa = _colsum(dpa * yah)
        d_gob = _colsum(dpb * ybh)
        dyah = dpa * v_ref[5:6, :]
        dybh = dpb * v_ref[6:7, :]
        dya = ra * (dyah - yah * _rowmean(dyah * yah))
        dyb = rb * (dybh - ybh * _rowmean(dybh * ybh))
        dpart_ref[:, 0:WA] = dya * z
        dz = dya * u

        @pl.when(i == 0)
        def _():
            gws_ref[...] = jnp.zeros((NHEAD, CHUNK, CHUNK), F32)
            dbs_acc[...] = jnp.zeros((CHUNK, WA), F32)
            vg_ref[...] = jnp.zeros((8, D), F32)
            v5g_ref[...] = jnp.zeros((8, WA), F32)

        first = lax.broadcasted_iota(jnp.int32, (CHUNK, 128), 1) < HD
        dvs = []
        for q in range(tm // CHUNK):
            dz_q = dz[q * CHUNK:(q + 1) * CHUNK, :]
            vb_q = vb[q * CHUNK:(q + 1) * CHUNK, :]
            dbs_acc[...] += dz_q
            dzb = dz_q.astype(BF16)
            dvs.append(_head_pairs(wm, dzb, transpose=True))
            for hd in range(NHEAD):
                slab = dzb[:, _lanes(hd // 2)]
                dz_hd = jnp.where(first if hd % 2 == 0 else jnp.logical_not(first), slab, jnp.zeros_like(slab))
                gws_ref[hd] += _dot_nt(dz_hd, vb_q[:, _lanes(hd // 2)])
        dv = jnp.concatenate(dvs, axis=0)
        d_gng = _colsum(dv * vh)
        d_gnb = _colsum(dv)
        dvh = dv * v_ref[0:1, :]
        dpart_ref[:, WA:2 * WA] = rv * (dvh - _rowmean(dvh) - vh * _rowmean(dvh * vh))
        dcn = dyb * (sg * (1.0 + cn * (1.0 - sg)))
        d_cng = _colsum(dcn * ch)
        d_cnb = _colsum(dcn)
        dch = dcn * v_ref[3:4, :]
        dconv = rc * (dch - _rowmean(dch) - ch * _rowmean(dch * ch))
        dpart_ref[:, 2 * WA:3 * WA] = dconv
        dpart_ref[:, 3 * WA:4 * WA] = jnp.zeros((tm, WA), F32)
        d_cb = _colsum(dconv)

        @pl.when(i % tps == 0)
        def _():
            mg_ref[...] = jnp.zeros((8, D), F32)

        mg_ref[2:3, :] += d_gt
        vg_ref[1:2, :] += d_gpost
        v5g_ref[0:1, :] += d_gng
        v5g_ref[1:2, :] += d_gnb
        v5g_ref[2:3, :] += d_cb
        v5g_ref[3:4, :] += d_cng
        v5g_ref[4:5, :] += d_cnb
        v5g_ref[5:6, :] += d_goa
        v5g_ref[6:7, :] += d_gob

        @pl.when(i == nt - 1)
        def _():
            row = lax.broadcasted_iota(jnp.int32, (CHUNK, CHUNK), 0)
            col = lax.broadcasted_iota(jnp.int32, (CHUNK, CHUNK), 1)
            for hd in range(NHEAD):
                gws_ref[hd] = jnp.where(col <= row, gws_ref[hd], 0.0)
            gbs_ref[...] = lax.dot_general(e_ref[...], dbs_acc[...], (((1,), (1,)), ((), ())),
                                           precision=lax.Precision.HIGHEST, preferred_element_type=F32)

    tile = pl.BlockSpec((tm, D), lambda i: (i, 0))
    ptile = pl.BlockSpec((tm, 4 * WA), lambda i: (i, 0))
    return _call(
        core, name=name, grid=(nt,), jobs=jobs,
        in_specs=[tile, tile, pl.BlockSpec((tm, 2 * WA), lambda i: (i, 0)), pl.BlockSpec((tm, WA), lambda i: (i, 0)),
                  pl.BlockSpec((None, 8, D), lambda i: (i // tps, 0, 0)), _const_spec((8, D)), _const_spec((D, D)),
                  _const_spec((8, WA)), _const_spec((NHEAD, CHUNK, CHUNK)), _const_spec((CHUNK, WA)),
                  _const_spec((8, WA))],
        out_specs=[ptile, tile, tile, pl.BlockSpec((None, 8, D), lambda i: (i // tps, 0, 0)),
                   pl.BlockSpec((8, D), lambda i: (0, 0)), pl.BlockSpec((8, WA), lambda i: (0, 0)),
                   pl.BlockSpec((NHEAD, CHUNK, CHUNK), lambda i: (0, 0, 0)), pl.BlockSpec((8, CHUNK), lambda i: (0, 0))],
        out_shape=[jax.ShapeDtypeStruct((T, 4 * WA), F32), jax.ShapeDtypeStruct((T, D), BF16),
                   jax.ShapeDtypeStruct((T, D), BF16), jax.ShapeDtypeStruct((nb, 8, D), F32),
                   jax.ShapeDtypeStruct((8, D), F32), jax.ShapeDtypeStruct((8, WA), F32),
                   jax.ShapeDtypeStruct((NHEAD, CHUNK, CHUNK), F32), jax.ShapeDtypeStruct((8, CHUNK), F32)],
        scratch=[pltpu.VMEM((CHUNK, WA), F32)],
        args=[dxo, ym, proj, conv, mod, gvec, w_mo, v512, ws, bias_full, esel])


def _mixer_bwd_b(dxo, x, dpart, proj, mod, gvec, w_mi, cw, tm, name, jobs=()):
    T = x.shape[0]
    nt = T // tm
    nb = mod.shape[0]
    tps = nt // nb
    hpt = tm // HALO
    nh = T // HALO
    off = HALO - (CONV_K - 1)
    p = _pitch(tm)
    ext_rows = 8 * p

    def core(ins, outs, scs):
        dxo_ref, x_ref, dpart_ref, dnext_ref, ag_ref, halo_ref, mod_ref, g_ref, wmi_ref, cw_ref = ins
        dx_ref, dproj_ref, hb_ref, mg_ref, vg_ref, dcw_ref = outs
        glu_ext, dconv_ext, dglu_scr, dcw_acc = scs
        i = pl.program_id(0)
        first = i % tps == 0
        last = i % tps == tps - 1
        a = ag_ref[:, 0:WA]
        g = ag_ref[:, WA:2 * WA]
        sgg = _sigmoid(g)

        @pl.when(i == 0)
        def _():
            glu_ext[:, HALO + tm:HALO + ext_rows, :] = jnp.zeros((4, ext_rows - tm, 128), F32)
            dconv_ext[:, HALO + tm:HALO + ext_rows, :] = jnp.zeros((4, ext_rows - tm, 128), F32)
            dcw_acc[...] = jnp.zeros((32, 8, WA), F32)
            vg_ref[...] = jnp.zeros((8, D), F32)

        _to_slabs(glu_ext, 0, jnp.where(first, 0.0, halo_ref[:, 0:WA] * _sigmoid(halo_ref[:, WA:2 * WA])))
        _to_slabs(glu_ext, HALO, a * sgg)
        _to_slabs(dconv_ext, 0, dpart_ref[:, 2 * WA:3 * WA])
        _to_slabs(dconv_ext, tm, jnp.where(last, 0.0, dnext_ref[...]))
        sub = lax.broadcasted_iota(jnp.int32, (8, 128), 0)
        for s in range(4):
            accs = [jnp.zeros((8, 128), F32)] * CONV_K
            for v in range(p):
                dc = jnp.where(v + p * sub < tm, dconv_ext[s, pl.ds(v, 8, stride=p), :], 0.0)
                for k in range(CONV_K):
                    accs[k] = accs[k] + dc * glu_ext[s, pl.ds(v + off + k, 8, stride=p), :]
            for k in range(CONV_K):
                dcw_acc[k, :, _lanes(s)] += accs[k]
        dglu = _tap_sum(dconv_ext, dglu_scr, cw_ref, jnp.zeros((1, WA), F32), tm, lambda k: (CONV_K - 1) - k)

        @pl.when(i == nt - 1)
        def _():
            for k in range(CONV_K):
                dcw_ref[k:k + 1, :] = jnp.sum(dcw_acc[k], axis=0, keepdims=True)
            dcw_ref[CONV_K:32, :] = jnp.zeros((32 - CONV_K, WA), F32)

        da = dglu * sgg
        dgg = dglu * a * (sgg * (1.0 - sgg))
        dproj_ref[:, 0:2 * WA] = dpart_ref[:, 0:2 * WA].astype(BF16)
        dproj_ref[:, 2 * WA:3 * WA] = da.astype(BF16)
        dproj_ref[:, 3 * WA:4 * WA] = dgg.astype(BF16)
        dh = jnp.zeros((tm, D), F32)
        for j in range(NDEV):
            dh = dh + _dot_nt(dproj_ref[:, j * MB:(j + 1) * MB], wmi_ref[j])
        xv = x_ref[...]
        sc, sh = mod_ref[1:2, :], mod_ref[0:1, :]
        gpre = g_ref[0:1, :]
        r = lax.rsqrt(_rowmean(xv * xv) + EPS)
        xh = xv * r
        n = xh * gpre
        hb_ref[...] = (n * (1.0 + sc) + sh).astype(BF16)
        d_sc = _colsum(dh * n)
        d_sh = _colsum(dh)
        dn = dh * (1.0 + sc)
        d_gpre = _colsum(dn * xh)
        dxh = dn * gpre
        dx_ref[...] = dxo_ref[...] + r * (dxh - xh * _rowmean(dxh * xh))

        @pl.when(first)
        def _():
            mg_ref[...] = jnp.zeros((8, D), F32)

        mg_ref[0:1, :] += d_sh
        mg_ref[1:2, :] += d_sc
        vg_ref[0:1, :] += d_gpre

    tile = pl.BlockSpec((tm, D), lambda i: (i, 0))
    return _call(
        core, name=name, grid=(nt,), jobs=jobs,
        in_specs=[tile, tile, pl.BlockSpec((tm, 4 * WA), lambda i: (i, 0)),
                  pl.BlockSpec((HALO, WA), lambda i: (jnp.minimum((i + 1) * hpt, nh - 1), 2)),
                  pl.BlockSpec((tm, 2 * WA), lambda i: (i, 1)),
                  pl.BlockSpec((HALO, 2 * WA), lambda i: (jnp.maximum(i * hpt - 1, 0), 1)),
                  pl.BlockSpec((None, 8, D), lambda i: (i // tps, 0, 0)), _const_spec((8, D)),
                  _const_spec((NDEV, D, MB)), _const_spec((32, WA))],
        out_specs=[tile, pl.BlockSpec((tm, 4 * WA), lambda i: (i, 0)), tile,
                   pl.BlockSpec((None, 8, D), lambda i: (i // tps, 0, 0)), pl.BlockSpec((8, D), lambda i: (0, 0)),
                   pl.BlockSpec((32, WA), lambda i: (0, 0))],
        out_shape=[jax.ShapeDtypeStruct((T, D), F32), jax.ShapeDtypeStruct((T, 4 * WA), BF16),
                   jax.ShapeDtypeStruct((T, D), BF16), jax.ShapeDtypeStruct((nb, 8, D), F32),
                   jax.ShapeDtypeStruct((8, D), F32), jax.ShapeDtypeStruct((32, WA), F32)],
        scratch=[pltpu.VMEM((4, HALO + ext_rows, 128), F32), pltpu.VMEM((4, HALO + ext_rows, 128), F32),
                 pltpu.VMEM((4, ext_rows, 128), F32), pltpu.VMEM((32, 8, WA), F32)],
        args=[dxo, x, dpart, dpart, proj, proj, mod, gvec, w_mi, cw])


def _grad_chip(a, b, a_spec, b_spec, prod_shape, half, name, jobs=(), via_b=False):
    steps = 8 if half is None else 4
    R = prod_shape[0] if half is None else half
    C = prod_shape[1]

    def core(ins, outs, scs):
        a_ref, b_ref = ins
        (o_ref,) = outs
        own, snd, rcv, ssem, rsem, lsem = scs
        s = pl.program_id(0)
        c = lax.axis_index("c")
        me = _me()
        sib = _flip(me, (0, 0, 1))
        if via_b:
            prod = _dot_tn(b_ref[...], a_ref[...]).T.astype(BF16)
        else:
            prod = _dot_tn(a_ref[...], b_ref[...]).astype(BF16)
        if half is None:
            q = s // 2

            @pl.when(s % 2 == c)
            def _():
                own[q] = prod

            @pl.when(s % 2 != c)
            def _():
                snd[q] = prod
                _remote(snd.at[q], rcv.at[q], ssem.at[q], rsem.at[q], sib).start()
        else:
            lo = prod[0:half, :]
            hi = prod[half:2 * half, :]
            own[s] = jnp.where(c == 0, lo, hi)
            snd[s] = jnp.where(c == 0, hi, lo)
            _remote(snd.at[s], rcv.at[s], ssem.at[s], rsem.at[s], sib).start()

        @pl.when(s == steps - 1)
        def _():
            for q4 in range(4):
                cp = _remote(snd.at[q4], rcv.at[q4], ssem.at[q4], rsem.at[q4], sib)
                cp.wait_recv()
                cp.wait_send()
                snd[q4] = (own[q4].astype(F32) + rcv[q4].astype(F32)).astype(BF16)
            out = pltpu.make_async_copy(snd, o_ref, lsem)
            out.start()
            out.wait()

    return _call(
        core, name=name, grid=(steps,), jobs=jobs, in_specs=[a_spec, b_spec], out_specs=[HBM],
        out_shape=[jax.ShapeDtypeStruct((4, R, C), BF16)],
        scratch=[pltpu.VMEM((4, R, C), BF16), pltpu.VMEM((4, R, C), BF16), pltpu.VMEM((4, R, C), BF16),
                 pltpu.SemaphoreType.DMA((4,)), pltpu.SemaphoreType.DMA((4,)), pltpu.SemaphoreType.DMA],
        args=[a, b])


def _grad_w_in(dg, hb, name, jobs=()):
    T = hb.shape[0]
    return _grad_chip(dg, hb, pl.BlockSpec((None, T, FBP), lambda s: (s, 0, 0)), _const_spec((T, D)),
                      (FBP, D), None, name, jobs)


def _grad_w_out(act, dyb, name, jobs=()):
    T = dyb.shape[0]
    return _grad_chip(act, dyb, pl.BlockSpec((None, T, FBP), lambda s: (s, 0, 0)), _const_spec((T, D)),
                      (FBP, D), FO, name, jobs)


def _grad_w_mi(hb, dproj, name, jobs=()):
    T = hb.shape[0]
    return _grad_chip(hb, dproj, _const_spec((T, D)), pl.BlockSpec((T, MB), lambda s: (0, s)),
                      (D, MB), None, name, jobs, via_b=True)


def _grad_w_mo(ycat, dym, name, jobs=()):
    T = ycat.shape[0]
    return _grad_chip(ycat, dym, pl.BlockSpec((T, 2 * MO), lambda s: (0, s)), _const_spec((T, D)),
                      (2 * MO, D), MO, name, jobs)


def _adamw_math(w, g, m, v):
    m2 = ADAM_B1 * m + (1.0 - ADAM_B1) * g
    v2 = ADAM_B2 * v + (1.0 - ADAM_B2) * (g * g)
    m_hat = m2 / (1.0 - ADAM_B1 ** ADAM_STEP)
    v_hat = v2 / (1.0 - ADAM_B2 ** ADAM_STEP)
    delta = -ADAM_LR * (m_hat / (jnp.sqrt(v_hat) + ADAM_EPS) + ADAM_WD * w)
    return delta, m2, v2


def _adamw_reduce(parts, w, m, v, tr, name, own=None, after=None):
    R, C = w.shape

    def core(ins, outs, _):
        p_ref, w_ref, m_ref, v_ref = ins[:4]
        g_ref, d_ref, m2_ref, v2_ref = outs
        if own is None:
            terms = [p_ref[s].astype(F32) for s in range(4)]
        else:
            mq = 2 * lax.axis_index("x") + lax.axis_index("y")
            mine = ins[4][...].astype(F32)
            terms = [jnp.where(mq == s, mine, p_ref[s].astype(F32)) for s in range(4)]
        g = terms[0]
        for s in range(1, 4):
            g = g + terms[s]
        g_ref[...] = g
        d_ref[...], m2_ref[...], v2_ref[...] = _adamw_math(w_ref[...], g, m_ref[...], v_ref[...])

    blk = pl.BlockSpec((tr, C), lambda i: (i, 0))
    in_specs = [pl.BlockSpec((4, tr, C), lambda i: (0, i, 0)), blk, blk, blk]
    args = [parts, w, m, v]
    if own is not None:
        mq = 2 * lax.axis_index("x") + lax.axis_index("y")
        in_specs.append(pl.BlockSpec((tr, C), lambda i: (i, 0)))
        args.append(lax.dynamic_index_in_dim(own, mq, 0, keepdims=False))
    if after is not None:
        in_specs.append(HBM)
        args.append(after)
    return _call(
        core, name=name, grid=(R // tr,), in_specs=in_specs,
        out_specs=[blk, blk, blk, blk], out_shape=[jax.ShapeDtypeStruct((R, C), F32)] * 4, args=args)[0]


HBM_ONLY = pl.BlockSpec(memory_space=pltpu.HBM)
SEM = pl.BlockSpec(memory_space=pltpu.SEMAPHORE)
EFFECT = pltpu.SideEffectType.DATAFLOW_SIDE_EFFECTING


def _chip_scatter_start(gs, name):
    n = len(gs)

    def body(*refs):
        g_refs, land_refs = refs[:n], refs[n:2 * n]
        ssem, rsem = refs[2 * n:2 * n + 2]
        token = refs[-1]
        me = _me()
        mq = 2 * me[0] + me[1]
        for k, f in enumerate(CHIP_FLIPS):
            p = _flip(me, f)
            for a in range(n):
                _remote(g_refs[a].at[2 * p[0] + p[1]], land_refs[a].at[mq], ssem.at[3 * a + k], rsem.at[3 * a + k], p).start()
        token[...] = jnp.zeros_like(token)

    gs = [pltpu.with_memory_space_constraint(g, pltpu.HBM) for g in gs]
    lands = [pltpu.with_memory_space_constraint(lax.empty(g.shape, g.dtype), pltpu.HBM) for g in gs]
    res = pl.pallas_call(
        body, name=name,
        out_shape=(pltpu.SemaphoreType.DMA((3 * n,)), pltpu.SemaphoreType.DMA((3 * n,)))
        + tuple(pltpu.HBM(g.shape, g.dtype) for g in gs) * 2 + (jax.ShapeDtypeStruct((8, 128), F32),),
        in_specs=(HBM_ONLY,) * (2 * n), out_specs=(SEM, SEM) + (HBM_ONLY,) * (2 * n) + (VM,),
        input_output_aliases={a: 2 + a for a in range(2 * n)},
        compiler_params=pltpu.CompilerParams(has_side_effects=EFFECT),
    )(*gs, *lands)
    return res[:-1], res[-1]


def _chip_scatter_wait(handle, after, name):
    ssem, rsem = handle[:2]
    n = (len(handle) - 2) // 2
    thru = handle[2:]

    def body(*refs):
        g_refs, land_refs = refs[:n], refs[n:2 * n]
        ssem, rsem = refs[2 * n:2 * n + 2]
        me = _me()
        mq = 2 * me[0] + me[1]
        for k, f in enumerate(CHIP_FLIPS):
            p = _flip(me, f)
            pq = 2 * p[0] + p[1]
            for a in range(n):
                _remote(g_refs[a].at[pq], land_refs[a].at[mq], ssem.at[3 * a + k], rsem.at[3 * a + k], p).wait_send()
                _remote(g_refs[a].at[mq], land_refs[a].at[pq], ssem.at[3 * a + k], rsem.at[3 * a + k], p).wait_recv()

    res = pl.pallas_call(
        body, name=name,
        out_shape=tuple(pltpu.HBM(t.shape, t.dtype) for t in thru),
        in_specs=(HBM_ONLY,) * (2 * n) + (SEM, SEM, HBM), out_specs=(HBM_ONLY,) * (2 * n),
        input_output_aliases={a: a for a in range(2 * n)},
        compiler_params=pltpu.CompilerParams(has_side_effects=EFFECT),
    )(*thru, ssem, rsem, after)
    return list(res[:n]), list(res[n:])


def _adamw_ada(sc_all, dd, w, m, v, tr, name, after=None):
    R, C = w.shape

    def core(ins, outs, _):
        sc_ref, dd_ref, w_ref, m_ref, v_ref = ins[:5]
        g_ref, d_ref, m2_ref, v2_ref = outs
        g = _dot_tn(sc_ref[...].astype(BF16), dd_ref[...].astype(BF16))
        g_ref[...] = g
        d_ref[...], m2_ref[...], v2_ref[...] = _adamw_math(w_ref[...], g, m_ref[...], v_ref[...])

    blk = pl.BlockSpec((tr, C), lambda i: (i, 0))
    return _call(
        core, name=name, grid=(R // tr,),
        in_specs=[pl.BlockSpec((64, tr), lambda i: (0, i)), pl.BlockSpec((64, C), lambda i: (0, 0)), blk, blk, blk]
        + [HBM] * (after is not None),
        out_specs=[blk, blk, blk, blk], out_shape=[jax.ShapeDtypeStruct((R, C), F32)] * 4,
        args=[sc_all, dd, w, m, v] + [after] * (after is not None))[0]


def _adamw_small(gathered, plain, grads, wmv, emit, name):
    nw = len(grads)
    ng, npl, ne = len(gathered), len(plain), len(emit)

    def core(ins, outs, _):
        srcs = []
        for a in range(ng):
            s = ins[a][0]
            for dev in range(1, NDEV):
                s = s + ins[a][dev]
            srcs.append(s)
        srcs += [ins[ng + a][...] for a in range(npl)]
        w_refs = ins[ng + npl:]
        for e, a in enumerate(emit):
            outs[e][...] = srcs[a]
        for t in range(nw):
            src, row = grads[t]
            g = srcs[src] if row is None else srcs[src][row:row + 1, :]
            w_ref, m_ref, v_ref = w_refs[3 * t:3 * t + 3]
            g_ref, d_ref, m2_ref, v2_ref = outs[ne + 4 * t:ne + 4 * t + 4]
            g_ref[...] = g
            d_ref[...], m2_ref[...], v2_ref[...] = _adamw_math(w_ref[...], g, m_ref[...], v_ref[...])

    out_shape = [jax.ShapeDtypeStruct(gathered[a].shape[1:], F32) for a in emit]
    for t in range(nw):
        out_shape += [jax.ShapeDtypeStruct(wmv[3 * t].shape, F32)] * 4
    return _call(
        core, name=name, grid=(), in_specs=[VM] * (ng + npl + 3 * nw), out_specs=[VM] * (ne + 4 * nw),
        out_shape=out_shape, args=list(gathered) + list(plain) + list(wmv))[0]


def _ada_fwd(c_pad, w_ada, b_cols, cw_pad, jobs=()):
    def core(ins, outs, scs):
        c_ref, w_ref, b_ref, cwp_ref = ins
        ada_ref, sc_ref, cw_ref = outs
        cbuf, send_buf, ssem, rsem = scs
        me = _me()
        mi = _lin(me)
        cbuf[mi] = c_ref[...]
        cw_ref[mi] = cwp_ref[...]
        peers = [_flip(me, f) for f in FLIPS]
        first = []
        for k, p in enumerate(peers):
            first.append(_remote(cbuf.at[mi], cbuf.at[mi], ssem.at[k], rsem.at[k], p))
            first.append(_remote(cw_ref.at[mi], cw_ref.at[mi], ssem.at[7 + k], rsem.at[7 + k], p))
        for cp in first:
            cp.start()
        for k, p in enumerate(peers):
            pi = _lin(p)
            _remote(cbuf.at[pi], cbuf.at[pi], ssem.at[k], rsem.at[k], p).wait_recv()
            _remote(cw_ref.at[pi], cw_ref.at[pi], ssem.at[7 + k], rsem.at[7 + k], p).wait_recv()
        c_all = cbuf[...].reshape(8 * 8, D)
        sc = c_all * _sigmoid(c_all)
        sc_ref[...] = sc
        res = _dot(sc.astype(BF16), w_ref[...].astype(BF16)) + b_ref[...]
        send_buf[...] = res.reshape(8, 8, ADA_B)
        ada_ref[mi] = send_buf[mi]
        second = []
        for k, p in enumerate(peers):
            second.append(_remote(send_buf.at[_lin(p)], ada_ref.at[mi], ssem.at[14 + k], rsem.at[14 + k], p))
        for cp in second:
            cp.start()
        for k, p in enumerate(peers):
            _remote(send_buf.at[mi], ada_ref.at[_lin(p)], ssem.at[14 + k], rsem.at[14 + k], p).wait_recv()
        for cp in first + second:
            cp.wait_send()

    return _call(
        core, name="ada_fwd", grid=(), jobs=jobs, in_specs=[VM, VM, VM, VM], out_specs=[VM, VM, VM],
        out_shape=[jax.ShapeDtypeStruct((8, 8, ADA_B), F32), jax.ShapeDtypeStruct((64, D), F32),
                   jax.ShapeDtypeStruct((8, 32, 64), F32)],
        scratch=[pltpu.VMEM((8, 8, D), F32), pltpu.VMEM((8, 8, ADA_B), F32),
                 pltpu.SemaphoreType.DMA((21,)), pltpu.SemaphoreType.DMA((21,))],
        args=[c_pad, w_ada, b_cols, cw_pad])


def _ada_bwd(dada, jobs=()):
    def core(ins, outs, scs):
        (d_ref,) = ins
        dd_ref, gb_ref = outs
        rbuf, ssem, rsem = scs
        me = _me()
        mi = _lin(me)
        peers = [_flip(me, f) for f in FLIPS]
        rbuf[mi] = d_ref[mi]
        first = []
        for k, p in enumerate(peers):
            first.append(_remote(d_ref.at[_lin(p)], rbuf.at[mi], ssem.at[k], rsem.at[k], p))
        for cp in first:
            cp.start()
        for k, p in enumerate(peers):
            _remote(d_ref.at[mi], rbuf.at[_lin(p)], ssem.at[k], rsem.at[k], p).wait_recv()
        dd = rbuf[...].reshape(64, ADA_B)
        dd_ref[...] = dd
        gb_ref[mi] = jnp.broadcast_to(_colsum(dd), (8, ADA_B))
        second = []
        for k, p in enumerate(peers):
            second.append(_remote(gb_ref.at[mi], gb_ref.at[mi], ssem.at[7 + k], rsem.at[7 + k], p))
        for cp in second:
            cp.start()
        for k, p in enumerate(peers):
            pi = _lin(p)
            _remote(gb_ref.at[pi], gb_ref.at[pi], ssem.at[7 + k], rsem.at[7 + k], p).wait_recv()
        for cp in first + second:
            cp.wait_send()

    return _call(
        core, name="ada_bwd", grid=(), jobs=jobs, in_specs=[VM], out_specs=[VM, VM],
        out_shape=[jax.ShapeDtypeStruct((64, ADA_B), F32), jax.ShapeDtypeStruct((8, 8, ADA_B), F32)],
        scratch=[pltpu.VMEM((8, 8, ADA_B), F32), pltpu.SemaphoreType.DMA((14,)), pltpu.SemaphoreType.DMA((14,))],
        args=[dada])


SMALL_D = ("g_pre_f1", "g_post_f1", "g_pre_m", "g_post_m", "g_pre_f2", "g_post_f2")
SMALL_W = ("gmlp_norm_g", "gmlp_norm_b", "conv_b", "conv_norm_g", "conv_norm_b", "g_out_a", "g_out_b")


def kernel(x, c, w_ada, b_ada, g_pre_f1, g_post_f1, w_f1_in, w_f1_out, g_pre_m, g_post_m, w_mix_in, gmlp_norm_g, gmlp_norm_b, w_spatial, b_spatial, conv_w, conv_b, conv_norm_g, conv_norm_b, g_out_a, g_out_b, w_mix_out, g_pre_f2, g_post_f2, w_f2_in, w_f2_out, loss_target, m_w_ada, m_b_ada, m_g_pre_f1, m_g_post_f1, m_w_f1_in, m_w_f1_out, m_g_pre_m, m_g_post_m, m_w_mix_in, m_gmlp_norm_g, m_gmlp_norm_b, m_w_spatial, m_b_spatial, m_conv_w, m_conv_b, m_conv_norm_g, m_conv_norm_b, m_g_out_a, m_g_out_b, m_w_mix_out, m_g_pre_f2, m_g_post_f2, m_w_f2_in, m_w_f2_out, v_w_ada, v_b_ada, v_g_pre_f1, v_g_post_f1, v_w_f1_in, v_w_f1_out, v_g_pre_m, v_g_post_m, v_w_mix_in, v_gmlp_norm_g, v_gmlp_norm_b, v_w_spatial, v_b_spatial, v_conv_w, v_conv_b, v_conv_norm_g, v_conv_norm_b, v_g_out_a, v_g_out_b, v_w_mix_out, v_g_pre_f2, v_g_post_f2, v_w_f2_in, v_w_f2_out):
    given = dict(locals())
    bl, seq, _ = x.shape
    T = bl * seq
    tm = min(256, seq // 2)
    mi = _lin((lax.axis_index("x"), lax.axis_index("y"), lax.axis_index("c")))

    def shard_in(w):
        return jnp.pad(w[0].T.astype(BF16), ((0, FBP - FB), (0, 0)))

    zpad = jnp.zeros((max(FBP - FB, 16), D), BF16)
    g_f1 = _Gather([shard_in(w_f1_in), w_f1_out[0].astype(BF16)], ("rows", "out"), zpad)
    g_mx = _Gather([w_mix_in[0].astype(BF16), w_mix_out[0].astype(BF16), w_f2_out[0].astype(BF16)],
                   ("rows", "rows", "out"), zpad)
    g_f2 = _Gather([shard_in(w_f2_in)], ("rows",), zpad, late_mid=True)

    c_pad = jnp.pad(c, ((0, 8 - bl), (0, 0)))
    b_cols = lax.dynamic_slice(b_ada, (0, mi * ADA_B), (1, ADA_B))
    cw_pad = jnp.pad(conv_w[0], ((0, 1), (0, 0)))
    (ada_blk, sc_all, cw_all), ((wi1, wo1),) = _ada_fwd(c_pad, w_ada[0], b_cols, cw_pad, jobs=[g_f1])
    ada = ada_blk[:, 0:bl, :].transpose(1, 0, 2).reshape(bl, 9, D)
    pad5 = jnp.zeros((bl, 5, D), F32)
    mod1 = jnp.concatenate([ada[:, 0:3], pad5], axis=1)
    mod2 = jnp.concatenate([ada[:, 3:6], pad5], axis=1)
    mod3 = jnp.concatenate([ada[:, 6:9], pad5], axis=1)
    cw_full = cw_all.transpose(1, 0, 2).reshape(32, WA)

    zrow = jnp.zeros((1, D), F32)
    gv1 = jnp.concatenate([g_pre_f1, g_post_f1] + [zrow] * 6, axis=0)
    gvm = jnp.concatenate([g_pre_m, g_post_m] + [zrow] * 6, axis=0)
    gv2 = jnp.concatenate([g_pre_f2, g_post_f2] + [zrow] * 6, axis=0)
    v512 = jnp.concatenate([gmlp_norm_g, gmlp_norm_b, conv_b, conv_norm_g, conv_norm_b, g_out_a, g_out_b,
                            jnp.zeros((1, WA), F32)], axis=0)
    ws = w_spatial[0]
    bias_full = jnp.repeat(b_spatial[0].T, HD, axis=1)
    esel = (lax.broadcasted_iota(jnp.int32, (8, WA), 1) // HD == lax.broadcasted_iota(jnp.int32, (8, WA), 0)).astype(F32)

    x0 = x.reshape(T, D)
    (x1, gu1, y1), ((wmi, wmo, wo2),) = _ffn_fwd(x0, mod1, gv1, wi1, wo1, tm, "ffn1_fwd", jobs=[g_mx])
    wmo = wmo.reshape(D, D)
    (x2, proj, ym, conv), ((wi2,),) = _mixer_fwd(x1, mod2, gvm, wmi, wmo, v512, ws, bias_full, cw_full, tm, "mixer_fwd", jobs=[g_f2])

    (dx2, dg2, act2, hb2, dyb2, mg3, vg3, loss_blk), _ = _ffn_last(
        x2, loss_target.reshape(T, D), mod3, gv2, wi2, wo2, tm, "ffn2_fwd_bwd")
    (g_wi2,), _ = _grad_w_in(dg2, hb2, "ffn2_gw_in")
    (g_wo2,), _ = _grad_w_out(act2, dyb2, "ffn2_gw_out")
    (dpart, dymb, ycat, mg2a, vgma, v5g, gws, gbs), ((p_wi2,),) = _mixer_bwd_a(
        dx2, ym, proj, conv, mod2, gvm, wmo, v512, ws, bias_full, esel, tm, "mixer_bwd_a",
        jobs=[_ChipScatter([g_wi2])])
    (dx1, dproj, hbm, mg2b, vgmb, dcw), ((p_wo2,),) = _mixer_bwd_b(
        dx2, x1, dpart, proj, mod2, gvm, wmi, cw_full, tm, "mixer_bwd_b", jobs=[_ChipScatter([g_wo2])])
    (g_wmi,), _ = _grad_w_mi(hbm, dproj, "mixer_gw_in")
    (g_wmo,), _ = _grad_w_mo(ycat, dymb, "mixer_gw_out")
    p2 = jnp.concatenate([v5g, dcw], axis=0)
    (dx0, dg1, act1, hb1, dyb1, mg1, vg1), _ = _ffn_bwd(dx1, x0, y1, gu1, mod1, gv1, wi1, wo1, tm, "ffn1_bwd")

    dada = jnp.concatenate([mg1[:, 0:3], mg2b[:, 0:2], mg2a[:, 2:3], mg3[:, 0:3]], axis=1)
    dada = dada.reshape(bl, NDEV, ADA_B).transpose(1, 0, 2)
    dada = jnp.pad(dada, ((0, 0), (0, 8 - bl), (0, 0)))
    p1 = jnp.concatenate([vg1[0:2], vgmb[0:1], vgma[1:2], vg3[0:2], loss_blk[0:1], zrow], axis=0)
    (dd_all, gb_all), ((a1,),) = _ada_bwd(dada, jobs=[_AllGather([p1])])
    g_bada = gb_all[:, 0, :].reshape(1, 9 * D)

    (g_wo1,), ((p_wmi, p_wmo),) = _grad_w_out(act1, dyb1, "ffn1_gw_out", jobs=[_ChipScatter([g_wmi, g_wmo])])
    (g_wi1,), ((a2, a3, a4), (p_wo1,)) = _grad_w_in(
        dg1, hb1, "ffn1_gw_in", jobs=[_Gather([p2, gws, gbs], ("rows",) * 3, zpad), _ChipScatter([g_wo1])])

    h_f1, token = _chip_scatter_start([g_wi1], "tail_start")

    res = {}
    quad = _adamw_reduce(p_wi2, w_f2_in[0].T, m_w_f2_in[0].T, v_w_f2_in[0].T, FO, "adamw_w_f2_in", after=token)
    res["w_f2_in"] = tuple(t.T[None] for t in quad)
    for nm, part, tr in (("w_f2_out", p_wo2, FO), ("w_mix_in", p_wmi, 256), ("w_mix_out", p_wmo, MO), ("w_f1_out", p_wo1, FO)):
        quad = _adamw_reduce(part, given[nm][0], given["m_" + nm][0], given["v_" + nm][0], tr, "adamw_" + nm, after=quad[1])
        res[nm] = tuple(t[None] for t in quad)
    quad = _adamw_ada(sc_all, dd_all, w_ada[0], m_w_ada[0], v_w_ada[0], 256, "adamw_w_ada", after=quad[1])
    res["w_ada"] = tuple(t[None] for t in quad)
    (g_wi1,), (p_wi1,) = _chip_scatter_wait(h_f1, quad[1], "tail_wait")
    quad = _adamw_reduce(p_wi1, w_f1_in[0].T, m_w_f1_in[0].T, v_w_f1_in[0].T, FO, "adamw_w_f1_in", own=g_wi1)
    res["w_f1_in"] = tuple(t.T[None] for t in quad)

    small = SMALL_D + SMALL_W + ("w_spatial", "b_spatial", "b_ada")
    grads = [(0, r) for r in range(6)] + [(1, r) for r in range(7)] + [(2, None), (3, None), (4, None)]
    wmv = []
    for nm in small:
        for pre in ("", "m_", "v_"):
            wmv.append(given[pre + nm][0] if nm in ("w_spatial", "b_spatial") else given[pre + nm])
    outs = _adamw_small([a1, a2, a3, a4], [g_bada], grads, wmv, (0, 1), "adamw_small")
    loss = outs[0][6, 0]
    for t, nm in enumerate(small):
        quad = outs[2 + 4 * t:6 + 4 * t]
        res[nm] = tuple(q[None] for q in quad) if nm in ("w_spatial", "b_spatial") else tuple(quad)
    g_cw = lax.dynamic_slice(outs[1], (8, mi * 64), (32, 64))
    wmv = [jnp.pad(given[pre + "conv_w"][0], ((0, 1), (0, 0)), constant_values=1.0 if pre == "v_" else 0.0)
           for pre in ("", "m_", "v_")]
    quad = _adamw_small([], [g_cw], [(0, None)], wmv, (), "adamw_conv_w")
    res["conv_w"] = tuple(q[0:CONV_K][None] for q in quad)

    order = ["w_ada", "b_ada", "g_pre_f1", "g_post_f1", "w_f1_in", "w_f1_out", "g_pre_m", "g_post_m", "w_mix_in",
             "gmlp_norm_g", "gmlp_norm_b", "w_spatial", "b_spatial", "conv_w", "conv_b", "conv_norm_g", "conv_norm_b",
             "g_out_a", "g_out_b", "w_mix_out", "g_pre_f2", "g_post_f2", "w_f2_in", "w_f2_out"]
    out = [loss, dx0.reshape(bl, seq, D)]
    for k in range(4):
        out += [res[nm][k] for nm in order]
    return tuple(out)
```

```python
import jax
import jax.numpy as jnp
from jax import lax
from jax.experimental import pallas as pl
from jax.experimental.pallas import tpu as pltpu

F32 = jnp.float32
BF16 = jnp.bfloat16

D = 1024
DFF = 2816
NDEV = 8
FB = 2 * DFF // NDEV
FBP = 704
FO = DFF // NDEV
WA = 512
NHEAD = 8
HD = 64
CHUNK = 128
CONV_K = 31
HALO = 32
MB = 2 * (WA + WA) // NDEV
MO = D // NDEV
ADA_B = 9 * D // NDEV
EPS = 1e-6
HALF = 0.5

ADAM_LR = 0.001
ADAM_B1 = 0.9
ADAM_B2 = 0.999
ADAM_EPS = 1e-08
ADAM_WD = 0.01
ADAM_STEP = 10

VMEM_LIMIT = 56 * 1024 * 1024
MESH = pl.DeviceIdType.MESH
FLIPS = ((0, 0, 1), (1, 0, 0), (0, 1, 0), (1, 1, 0), (1, 0, 1), (0, 1, 1), (1, 1, 1))
CHIP_FLIPS = ((1, 0, 0), (0, 1, 0), (1, 1, 0))
HBM = pl.BlockSpec(memory_space=pl.ANY)
VM = pl.BlockSpec(memory_space=pltpu.VMEM)


def _dot(a, b):
    return lax.dot_general(a, b, (((1,), (0,)), ((), ())), preferred_element_type=F32)


def _dot_nt(a, b):
    return lax.dot_general(a, b, (((1,), (1,)), ((), ())), preferred_element_type=F32)


def _dot_tn(a, b):
    return lax.dot_general(a, b, (((0,), (0,)), ((), ())), preferred_element_type=F32)


def _rowmean(v):
    return jnp.mean(v, axis=-1, keepdims=True)


def _colsum(v):
    return jnp.sum(v, axis=0, keepdims=True)


def _sigmoid(v):
    return 0.5 * jnp.tanh(0.5 * v) + 0.5


def _const_spec(shape):
    nd = len(shape)
    return pl.BlockSpec(shape, lambda *_: (0,) * nd, pipeline_mode=pl.Buffered(1))


def _me():
    return lax.axis_index("x"), lax.axis_index("y"), lax.axis_index("c")


def _flip(me, f):
    return tuple(1 - v if b else v for v, b in zip(me, f))


def _lin(p):
    return 4 * p[0] + 2 * p[1] + p[2]


def _remote(src, dst, send_sem, recv_sem, dev):
    return pltpu.make_async_remote_copy(src_ref=src, dst_ref=dst, send_sem=send_sem, recv_sem=recv_sem,
                                        device_id=dev, device_id_type=MESH)


def _blk(kind, ref, p):
    if kind == "out":
        return ref.at[2 * p[0] + p[1], pl.ds(p[2] * FO, FO), :]
    return ref.at[_lin(p)]


class _Gather:
    def __init__(self, shards, kinds, zpad, late_mid=False):
        self.late_mid = late_mid
        self.kinds = kinds
        self.n = len(shards)
        self.ins = list(shards) + [zpad]
        self.out_shape = [jax.ShapeDtypeStruct((4, FBP, D) if k == "out" else (NDEV,) + s.shape, s.dtype)
                          for s, k in zip(shards, kinds)]
        self.n_out = sum(k == "out" for k in kinds)
        self.sems = [pltpu.SemaphoreType.DMA((7 * self.n,)), pltpu.SemaphoreType.DMA((7 * self.n,)),
                     pltpu.SemaphoreType.DMA((self.n + 4 * max(self.n_out, 1),))]

    def _first(self, ins, outs, sems):
        ssem, rsem, lsem = sems
        me = _me()
        sib = _flip(me, (0, 0, 1))
        cps, loc = [], []
        nz = 0
        for a in range(self.n):
            mine = _blk(self.kinds[a], outs[a], me)
            loc.append(pltpu.make_async_copy(ins[a], mine, lsem.at[a]))
            if self.kinds[a] == "out" and FBP > FB:
                for q in range(4):
                    loc.append(pltpu.make_async_copy(ins[self.n], outs[a].at[q, pl.ds(FB, FBP - FB), :],
                                                     lsem.at[self.n + 4 * nz + q]))
                nz += 1
            cps.append(_remote(ins[a], mine, ssem.at[7 * a], rsem.at[7 * a], sib))
            for j, f in enumerate(CHIP_FLIPS):
                cps.append(_remote(ins[a], mine, ssem.at[7 * a + 1 + j], rsem.at[7 * a + 1 + j], _flip(me, f)))
        return cps, loc

    def _passed(self, outs, sems):
        ssem, rsem, _ = sems
        me = _me()
        sib = _flip(me, (0, 0, 1))
        cps = []
        for j, f in enumerate(CHIP_FLIPS):
            for a in range(self.n):
                blk = _blk(self.kinds[a], outs[a], _flip(me, f))
                cps.append(_remote(blk, blk, ssem.at[7 * a + 4 + j], rsem.at[7 * a + 4 + j], sib))
        return cps

    def start(self, ins, outs, sems):
        cps, loc = self._first(ins, outs, sems)
        for cp in loc + cps:
            cp.start()

    def mid(self, ins, outs, sems):
        ssem, rsem, _ = sems
        me = _me()
        passed = self._passed(outs, sems)
        t = 0
        for j, f in enumerate(CHIP_FLIPS):
            for a in range(self.n):
                blk = _blk(self.kinds[a], outs[a], _flip(me, f))
                _remote(blk, blk, ssem.at[7 * a + 1 + j], rsem.at[7 * a + 1 + j], _flip(me, f)).wait_recv()
                passed[t].start()
                t += 1

    def end(self, ins, outs, sems):
        ssem, rsem, _ = sems
        me = _me()
        sib = _flip(me, (0, 0, 1))
        for a in range(self.n):
            blk = _blk(self.kinds[a], outs[a], sib)
            _remote(blk, blk, ssem.at[7 * a], rsem.at[7 * a], sib).wait_recv()
            for j, f in enumerate(CHIP_FLIPS):
                blk = _blk(self.kinds[a], outs[a], _flip(_flip(me, f), (0, 0, 1)))
                _remote(blk, blk, ssem.at[7 * a + 4 + j], rsem.at[7 * a + 4 + j], sib).wait_recv()
        cps, loc = self._first(ins, outs, sems)
        for cp in cps + self._passed(outs, sems):
            cp.wait_send()
        for cp in loc:
            cp.wait()


class _ChipScatter:
    def __init__(self, grads):
        self.n = len(grads)
        self.ins = list(grads)
        self.out_shape = [jax.ShapeDtypeStruct(g.shape, BF16) for g in grads]
        self.sems = [pltpu.SemaphoreType.DMA((3 * self.n,)), pltpu.SemaphoreType.DMA((3 * self.n,)),
                     pltpu.SemaphoreType.DMA((self.n,))]

    def _copies(self, ins, outs, sems):
        ssem, rsem, lsem = sems
        me = _me()
        mq = 2 * me[0] + me[1]
        loc = [pltpu.make_async_copy(ins[a].at[mq], outs[a].at[mq], lsem.at[a]) for a in range(self.n)]
        cps = []
        for k, f in enumerate(CHIP_FLIPS):
            p = _flip(me, f)
            for a in range(self.n):
                cps.append(_remote(ins[a].at[2 * p[0] + p[1]], outs[a].at[mq], ssem.at[3 * a + k], rsem.at[3 * a + k], p))
        return cps, loc

    def start(self, ins, outs, sems):
        cps, loc = self._copies(ins, outs, sems)
        for cp in loc + cps:
            cp.start()

    mid = None

    def end(self, ins, outs, sems):
        ssem, rsem, _ = sems
        me = _me()
        mq = 2 * me[0] + me[1]
        for k, f in enumerate(CHIP_FLIPS):
            p = _flip(me, f)
            for a in range(self.n):
                _remote(ins[a].at[mq], outs[a].at[2 * p[0] + p[1]], ssem.at[3 * a + k], rsem.at[3 * a + k], p).wait_recv()
        cps, loc = self._copies(ins, outs, sems)
        for cp in cps:
            cp.wait_send()
        for cp in loc:
            cp.wait()


class _AllGather:
    def __init__(self, parts):
        self.n = len(parts)
        self.ins = list(parts)
        self.out_shape = [jax.ShapeDtypeStruct((NDEV,) + p.shape, p.dtype) for p in parts]
        self.sems = [pltpu.SemaphoreType.DMA((7 * self.n,)), pltpu.SemaphoreType.DMA((7 * self.n,)),
                     pltpu.SemaphoreType.DMA((self.n,))]

    def _copies(self, ins, outs, sems):
        ssem, rsem, lsem = sems
        me = _me()
        mi = _lin(me)
        loc = [pltpu.make_async_copy(ins[a], outs[a].at[mi], lsem.at[a]) for a in range(self.n)]
        cps = []
        for k, f in enumerate(FLIPS):
            for a in range(self.n):
                cps.append(_remote(ins[a], outs[a].at[mi], ssem.at[7 * a + k], rsem.at[7 * a + k], _flip(me, f)))
        return cps, loc

    def start(self, ins, outs, sems):
        cps, loc = self._copies(ins, outs, sems)
        for cp in loc + cps:
            cp.start()

    mid = None

    def end(self, ins, outs, sems):
        ssem, rsem, _ = sems
        me = _me()
        for k, f in enumerate(FLIPS):
            p = _flip(me, f)
            for a in range(self.n):
                _remote(ins[a], outs[a].at[_lin(p)], ssem.at[7 * a + k], rsem.at[7 * a + k], p).wait_recv()
        cps, loc = self._copies(ins, outs, sems)
        for cp in cps:
            cp.wait_send()
        for cp in loc:
            cp.wait()


def _call(core, *, name, grid, in_specs, out_specs, out_shape, args, scratch=(), jobs=()):
    n_in, n_out, n_sc = len(in_specs), len(out_specs), len(scratch)
    steps = 1
    for g in grid:
        steps *= g

    def body(*refs):
        pos = [0]

        def take(k):
            r = refs[pos[0]:pos[0] + k]
            pos[0] += k
            return r

        ins = take(n_in)
        j_ins = [take(len(j.ins)) for j in jobs]
        outs = take(n_out)
        j_outs = [take(len(j.out_shape)) for j in jobs]
        scs = take(n_sc)
        j_sems = [take(len(j.sems)) for j in jobs]
        if len(grid) == 2:
            step = pl.program_id(0) * grid[1] + pl.program_id(1)
        elif len(grid) == 1:
            step = pl.program_id(0)
        else:
            step = 0
        for j, ji, jo, js in zip(jobs, j_ins, j_outs, j_sems):
            if grid:
                pl.when(step == 0)(lambda j=j, ji=ji, jo=jo, js=js: j.start(ji, jo, js))
            else:
                j.start(ji, jo, js)
        for j, ji, jo, js in zip(jobs, j_ins, j_outs, j_sems):
            if j.mid is not None and grid:
                at = max(steps - 2, 0) if j.late_mid else (3 * steps) // 4
                pl.when(step == at)(lambda j=j, ji=ji, jo=jo, js=js: j.mid(ji, jo, js))
        if core is not None:
            core(ins, outs, scs)
        for j, ji, jo, js in zip(jobs, j_ins, j_outs, j_sems):
            if grid:
                pl.when(step == steps - 1)(lambda j=j, ji=ji, jo=jo, js=js: j.end(ji, jo, js))
            else:
                if j.mid is not None:
                    j.mid(ji, jo, js)
                j.end(ji, jo, js)

    all_in = list(in_specs)
    all_args = list(args)
    all_out = list(out_specs)
    all_shape = list(out_shape)
    all_sc = list(scratch)
    for j in jobs:
        all_in += [HBM] * len(j.ins)
        all_args += j.ins
    for j in jobs:
        all_out += [HBM] * len(j.out_shape)
        all_shape += j.out_shape
        all_sc += j.sems
    params = dict(vmem_limit_bytes=VMEM_LIMIT)
    if grid:
        params["dimension_semantics"] = ("arbitrary",) * len(grid)
    res = pl.pallas_call(
        body, name=name, grid=grid, in_specs=all_in, out_specs=all_out, out_shape=all_shape,
        scratch_shapes=all_sc, compiler_params=pltpu.CompilerParams(**params),
    )(*all_args)
    core_res = list(res[:n_out])
    job_res = []
    pos = n_out
    for j in jobs:
        job_res.append(list(res[pos:pos + len(j.out_shape)]))
        pos += len(j.out_shape)
    return core_res, job_res


def _ffn_fwd(x, mod, gvec, w_in, w_out, tm, name, jobs=()):
    T = x.shape[0]
    nt = T // tm
    tps = nt // mod.shape[0]

    def core(ins, outs, _):
        x_ref, mod_ref, g_ref, win_ref, wout_ref = ins
        xo_ref, gu_ref, y_ref = outs
        xv = x_ref[...]
        sh, sc, gt = mod_ref[0:1, :], mod_ref[1:2, :], mod_ref[2:3, :]
        r = lax.rsqrt(_rowmean(xv * xv) + EPS)
        h = (xv * r * g_ref[0:1, :]) * (1.0 + sc) + sh
        hb = h.astype(BF16)
        y = jnp.zeros((tm, D), F32)
        for cidx in range(4):
            gate = _dot_nt(hb, win_ref[cidx])
            up = _dot_nt(hb, win_ref[4 + cidx])
            gu_ref[cidx] = gate.astype(BF16)
            gu_ref[4 + cidx] = up.astype(BF16)
            act = gate * _sigmoid(gate) * up
            y = y + _dot(act.astype(BF16), wout_ref[cidx])
        y_ref[...] = y
        ry = lax.rsqrt(_rowmean(y * y) + EPS)
        xo_ref[...] = xv + (HALF * gt) * (y * ry * g_ref[1:2, :])

    tile = pl.BlockSpec((tm, D), lambda i: (i, 0))
    return _call(
        core, name=name, grid=(nt,), jobs=jobs,
        in_specs=[tile, pl.BlockSpec((None, 8, D), lambda i: (i // tps, 0, 0)), _const_spec((8, D)),
                  _const_spec((8, FBP, D)), _const_spec((4, FBP, D))],
        out_specs=[tile, pl.BlockSpec((8, tm, FBP), lambda i: (0, i, 0)), tile],
        out_shape=[jax.ShapeDtypeStruct((T, D), F32), jax.ShapeDtypeStruct((8, T, FBP), BF16),
                   jax.ShapeDtypeStruct((T, D), F32)],
        args=[x, mod, gvec, w_in, w_out])


def _ffn_bwd(dxo, x, y, gu, mod, gvec, w_in, w_out, tm, name, jobs=()):
    T = x.shape[0]
    nt = T // tm
    nb = mod.shape[0]
    tps = nt // nb

    def core(ins, outs, _):
        dxo_ref, x_ref, y_ref, gu_ref, mod_ref, g_ref, win_ref, wout_ref = ins
        dx_ref, dg_ref, act_ref, hb_ref, dyb_ref, mg_ref, vg_ref = outs
        i = pl.program_id(0)
        xv = x_ref[...]
        dxo_v = dxo_ref[...]
        yv = y_ref[...]
        sh, sc, gt = mod_ref[0:1, :], mod_ref[1:2, :], mod_ref[2:3, :]
        gpre, gpost = g_ref[0:1, :], g_ref[1:2, :]
        r = lax.rsqrt(_rowmean(xv * xv) + EPS)
        xh = xv * r
        n = xh * gpre
        hb = (n * (1.0 + sc) + sh).astype(BF16)
        hb_ref[...] = hb
        ry = lax.rsqrt(_rowmean(yv * yv) + EPS)
        yh = yv * ry
        d_gt = _colsum(HALF * dxo_v * (yh * gpost))
        dp = (HALF * gt) * dxo_v
        d_gpost = _colsum(dp * yh)
        dyh = dp * gpost
        dy = ry * (dyh - yh * _rowmean(dyh * yh))
        dyb = dy.astype(BF16)
        dyb_ref[...] = dyb
        dh = jnp.zeros((tm, D), F32)
        for cidx in range(4):
            gate = gu_ref[cidx].astype(F32)
            up = gu_ref[4 + cidx].astype(F32)
            sig = _sigmoid(gate)
            s = gate * sig
            act_ref[cidx] = (s * up).astype(BF16)
            d_act = _dot_nt(dyb, wout_ref[cidx])
            d_up = (d_act * s).astype(BF16)
            d_gate = (d_act * up * (sig * (1.0 + gate * (1.0 - sig)))).astype(BF16)
            dg_ref[cidx] = d_gate
            dg_ref[4 + cidx] = d_up
            dh = dh + _dot(d_gate, win_ref[cidx]) + _dot(d_up, win_ref[4 + cidx])
        d_sc = _colsum(dh * n)
        d_sh = _colsum(dh)
        dn = dh * (1.0 + sc)
        d_gpre = _colsum(dn * xh)
        dxh = dn * gpre
        dx_ref[...] = dxo_v + r * (dxh - xh * _rowmean(dxh * xh))

        @pl.when(i % tps == 0)
        def _():
            mg_ref[...] = jnp.zeros((8, D), F32)

        @pl.when(i == 0)
        def _():
            vg_ref[...] = jnp.zeros((8, D), F32)

        mg_ref[0:1, :] += d_sh
        mg_ref[1:2, :] += d_sc
        mg_ref[2:3, :] += d_gt
        vg_ref[0:1, :] += d_gpre
        vg_ref[1:2, :] += d_gpost

    tile = pl.BlockSpec((tm, D), lambda i: (i, 0))
    return _call(
        core, name=name, grid=(nt,), jobs=jobs,
        in_specs=[tile, tile, tile, pl.BlockSpec((8, tm, FBP), lambda i: (0, i, 0)),
                  pl.BlockSpec((None, 8, D), lambda i: (i // tps, 0, 0)), _const_spec((8, D)),
                  _const_spec((8, FBP, D)), _const_spec((4, FBP, D))],
        out_specs=[tile, pl.BlockSpec((8, tm, FBP), lambda i: (0, i, 0)),
                   pl.BlockSpec((4, tm, FBP), lambda i: (0, i, 0)), tile, tile,
                   pl.BlockSpec((None, 8, D), lambda i: (i // tps, 0, 0)), pl.BlockSpec((8, D), lambda i: (0, 0))],
        out_shape=[jax.ShapeDtypeStruct((T, D), F32), jax.ShapeDtypeStruct((8, T, FBP), BF16),
                   jax.ShapeDtypeStruct((4, T, FBP), BF16), jax.ShapeDtypeStruct((T, D), BF16),
                   jax.ShapeDtypeStruct((T, D), BF16), jax.ShapeDtypeStruct((nb, 8, D), F32),
                   jax.ShapeDtypeStruct((8, D), F32)],
        args=[dxo, x, y, gu, mod, gvec, w_in, w_out])


def _ffn_last(x, target, mod, gvec, w_in, w_out, tm, name, jobs=()):
    T = x.shape[0]
    nt = T // tm
    nb = mod.shape[0]
    tps = nt // nb

    def core(ins, outs, scs):
        x_ref, t_ref, mod_ref, g_ref, win_ref, wout_ref = ins
        dx_ref, dg_ref, act_ref, hb_ref, dyb_ref, mg_ref, vg_ref, loss_ref = outs
        (gu_s,) = scs
        i = pl.program_id(0)
        xv = x_ref[...]
        sh, sc, gt = mod_ref[0:1, :], mod_ref[1:2, :], mod_ref[2:3, :]
        gpre, gpost = g_ref[0:1, :], g_ref[1:2, :]
        r = lax.rsqrt(_rowmean(xv * xv) + EPS)
        xh = xv * r
        n = xh * gpre
        hb = (n * (1.0 + sc) + sh).astype(BF16)
        hb_ref[...] = hb
        yv = jnp.zeros((tm, D), F32)
        for cidx in range(4):
            gate = _dot_nt(hb, win_ref[cidx])
            up = _dot_nt(hb, win_ref[4 + cidx])
            gu_s[cidx] = gate.astype(BF16)
            gu_s[4 + cidx] = up.astype(BF16)
            act = gate * _sigmoid(gate) * up
            act_ref[cidx] = act.astype(BF16)
            yv = yv + _dot(act_ref[cidx], wout_ref[cidx])
        ry = lax.rsqrt(_rowmean(yv * yv) + EPS)
        yh = yv * ry
        pn = yh * gpost
        err = xv + (HALF * gt) * pn - t_ref[...]
        dxo_v = err * (1.0 / D)
        d_gt = _colsum(HALF * dxo_v * pn)
        dp = (HALF * gt) * dxo_v
        d_gpost = _colsum(dp * yh)
        dyh = dp * gpost
        dyb = (ry * (dyh - yh * _rowmean(dyh * yh))).astype(BF16)
        dyb_ref[...] = dyb
        dh = jnp.zeros((tm, D), F32)
        for cidx in range(4):
            gate = gu_s[cidx].astype(F32)
            up = gu_s[4 + cidx].astype(F32)
            sig = _sigmoid(gate)
            s = gate * sig
            d_act = _dot_nt(dyb, wout_ref[cidx])
            d_up = (d_act * s).astype(BF16)
            d_gate = (d_act * up * (sig * (1.0 + gate * (1.0 - sig)))).astype(BF16)
            dg_ref[cidx] = d_gate
            dg_ref[4 + cidx] = d_up
            dh = dh + _dot(d_gate, win_ref[cidx]) + _dot(d_up, win_ref[4 + cidx])
        d_sc = _colsum(dh * n)
        d_sh = _colsum(dh)
        dn = dh * (1.0 + sc)
        d_gpre = _colsum(dn * xh)
        dxh = dn * gpre
        dx_ref[...] = dxo_v + r * (dxh - xh * _rowmean(dxh * xh))

        @pl.when(i % tps == 0)
        def _():
            mg_ref[...] = jnp.zeros((8, D), F32)

        @pl.when(i == 0)
        def _():
            vg_ref[...] = jnp.zeros((8, D), F32)
            loss_ref[...] = jnp.zeros((8, D), F32)

        mg_ref[0:1, :] += d_sh
        mg_ref[1:2, :] += d_sc
        mg_ref[2:3, :] += d_gt
        vg_ref[0:1, :] += d_gpre
        vg_ref[1:2, :] += d_gpost
        loss_ref[...] += HALF * jnp.sum(_rowmean(err * err), axis=0, keepdims=True)

    tile = pl.BlockSpec((tm, D), lambda i: (i, 0))
    return _call(
        core, name=name, grid=(nt,), jobs=jobs,
        in_specs=[tile, tile, pl.BlockSpec((None, 8, D), lambda i: (i // tps, 0, 0)), _const_spec((8, D)),
                  _const_spec((8, FBP, D)), _const_spec((4, FBP, D))],
        out_specs=[tile, pl.BlockSpec((8, tm, FBP), lambda i: (0, i, 0)),
                   pl.BlockSpec((4, tm, FBP), lambda i: (0, i, 0)), tile, tile,
                   pl.BlockSpec((None, 8, D), lambda i: (i // tps, 0, 0)), pl.BlockSpec((8, D), lambda i: (0, 0)),
                   pl.BlockSpec((8, D), lambda i: (0, 0))],
        out_shape=[jax.ShapeDtypeStruct((T, D), F32), jax.ShapeDtypeStruct((8, T, FBP), BF16),
                   jax.ShapeDtypeStruct((4, T, FBP), BF16), jax.ShapeDtypeStruct((T, D), BF16),
                   jax.ShapeDtypeStruct((T, D), BF16), jax.ShapeDtypeStruct((nb, 8, D), F32),
                   jax.ShapeDtypeStruct((8, D), F32), jax.ShapeDtypeStruct((8, D), F32)],
        scratch=[pltpu.VMEM((8, tm, FBP), BF16)],
        args=[x, target, mod, gvec, w_in, w_out])


def _masked_spatial(ws_ref):
    row = lax.broadcasted_iota(jnp.int32, (CHUNK, CHUNK), 0)
    col = lax.broadcasted_iota(jnp.int32, (CHUNK, CHUNK), 1)
    keep = col <= row
    return [jnp.where(keep, ws_ref[hd], 0.0).astype(BF16) for hd in range(NHEAD)]


def _head_pairs(mats, right, transpose=False):
    first = lax.broadcasted_iota(jnp.int32, (CHUNK, 128), 1) < HD
    op = _dot_tn if transpose else _dot
    out = []
    for p in range(NHEAD // 2):
        slab = right[:, _lanes(p)]
        out.append(jnp.where(first, op(mats[2 * p], slab), op(mats[2 * p + 1], slab)))
    return jnp.concatenate(out, axis=1)


def _spatial_gate(wm, vb_chunk):
    return _head_pairs(wm, vb_chunk)


def _layer_norm_stats(v):
    mu = _rowmean(v)
    vc = v - mu
    rstd = lax.rsqrt(_rowmean(vc * vc) + EPS)
    return vc * rstd, rstd


def _pitch(tm):
    p = tm // 8
    while p % 8 != 4:
        p += 1
    return p


def _lanes(s):
    return slice(s * 128, (s + 1) * 128)


def _to_slabs(ref, row0, val):
    for s in range(4):
        ref[s, row0:row0 + val.shape[0], :] = val[:, _lanes(s)]


def _tap_sum(src, out, cw_ref, bias, tm, start):
    p = _pitch(tm)
    for s in range(4):
        accs = [jnp.broadcast_to(bias[:, _lanes(s)], (8, 128))] * p
        for k in range(CONV_K):
            w = jnp.broadcast_to(cw_ref[k:k + 1, _lanes(s)], (8, 128))
            for v in range(p):
                accs[v] = accs[v] + w * src[s, pl.ds(v + start(k), 8, stride=p), :]
        for v in range(p):
            out[s, pl.ds(v, 8, stride=p), :] = accs[v]
    return jnp.concatenate([out[s, 0:tm, :] for s in range(4)], axis=1)


def _mixer_fwd(x, mod, gvec, w_mi, w_mo, v512, ws, bias_full, cw, tm, name, jobs=()):
    T = x.shape[0]
    nt = T // tm
    tps = nt // mod.shape[0]
    ext_rows = 8 * _pitch(tm)

    def core(ins, outs, scs):
        x_ref, mod_ref, g_ref, wmi_ref, wmo_ref, v_ref, ws_ref, bias_ref, cw_ref = ins
        xo_ref, proj_ref, ym_ref, conv_ref = outs
        glu_ext, conv_scr = scs
        i = pl.program_id(0)
        xv = x_ref[...]
        sh, sc, gt = mod_ref[0:1, :], mod_ref[1:2, :], mod_ref[2:3, :]
        r = lax.rsqrt(_rowmean(xv * xv) + EPS)
        hb = ((xv * r * g_ref[0:1, :]) * (1.0 + sc) + sh).astype(BF16)
        for j in range(NDEV):
            proj_ref[:, j * MB:(j + 1) * MB] = _dot(hb, wmi_ref[j])
        u = proj_ref[:, 0:WA]
        v0 = proj_ref[:, WA:2 * WA]
        a = proj_ref[:, 2 * WA:3 * WA]
        g = proj_ref[:, 3 * WA:4 * WA]
        vh, _ = _layer_norm_stats(v0)
        vb = (vh * v_ref[0:1, :] + v_ref[1:2, :]).astype(BF16)
        wm = _masked_spatial(ws_ref)
        ya = []
        for q in range(tm // CHUNK):
            z = _spatial_gate(wm, vb[q * CHUNK:(q + 1) * CHUNK, :]) + bias_ref[...]
            ya.append(u[q * CHUNK:(q + 1) * CHUNK, :] * z)
        ya = jnp.concatenate(ya, axis=0)
        glu = a * _sigmoid(g)

        @pl.when(i == 0)
        def _():
            glu_ext[:, HALO + tm:HALO + ext_rows, :] = jnp.zeros((4, ext_rows - tm, 128), F32)

        @pl.when(i % tps == 0)
        def _():
            glu_ext[:, 0:HALO, :] = jnp.zeros((4, HALO, 128), F32)

        _to_slabs(glu_ext, HALO, glu)
        conv = _tap_sum(glu_ext, conv_scr, cw_ref, v_ref[2:3, :], tm, lambda k: HALO - (CONV_K - 1) + k)
        conv_ref[...] = conv
        glu_ext[:, 0:HALO, :] = glu_ext[:, tm:tm + HALO, :]
        ch, _ = _layer_norm_stats(conv)
        cn = ch * v_ref[3:4, :] + v_ref[4:5, :]
        yb = cn * _sigmoid(cn)
        pa = ya * lax.rsqrt(_rowmean(ya * ya) + EPS) * v_ref[5:6, :]
        pb = yb * lax.rsqrt(_rowmean(yb * yb) + EPS) * v_ref[6:7, :]
        ycat = jnp.concatenate([pa, pb], axis=1).astype(BF16)
        ym = _dot(ycat, wmo_ref[...])
        ym_ref[...] = ym
        rm = lax.rsqrt(_rowmean(ym * ym) + EPS)
        xo_ref[...] = xv + gt * (ym * rm * g_ref[1:2, :])

    tile = pl.BlockSpec((tm, D), lambda i: (i, 0))
    return _call(
        core, name=name, grid=(nt,), jobs=jobs,
        in_specs=[tile, pl.BlockSpec((None, 8, D), lambda i: (i // tps, 0, 0)), _const_spec((8, D)),
                  _const_spec((NDEV, D, MB)), _const_spec((D, D)), _const_spec((8, WA)),
                  _const_spec((NHEAD, CHUNK, CHUNK)), _const_spec((CHUNK, WA)), _const_spec((32, WA))],
        out_specs=[tile, pl.BlockSpec((tm, 4 * WA), lambda i: (i, 0)), tile, pl.BlockSpec((tm, WA), lambda i: (i, 0))],
        out_shape=[jax.ShapeDtypeStruct((T, D), F32), jax.ShapeDtypeStruct((T, 4 * WA), F32),
                   jax.ShapeDtypeStruct((T, D), F32), jax.ShapeDtypeStruct((T, WA), F32)],
        scratch=[pltpu.VMEM((4, HALO + ext_rows, 128), F32), pltpu.VMEM((4, ext_rows, 128), F32)],
        args=[x, mod, gvec, w_mi, w_mo, v512, ws, bias_full, cw])


def _mixer_bwd_a(dxo, ym, proj, conv, mod, gvec, w_mo, v512, ws, bias_full, esel, tm, name, jobs=()):
    T = dxo.shape[0]
    nt = T // tm
    nb = mod.shape[0]
    tps = nt // nb

    def core(ins, outs, scs):
        dxo_ref, ym_ref, proj_ref, conv_ref, mod_ref, g_ref, wmo_ref, v_ref, ws_ref, bias_ref, e_ref = ins
        dpart_ref, dymb_ref, ycat_ref, mg_ref, vg_ref, v5g_ref, gws_ref, gbs_ref = outs
        (dbs_acc,) = scs
        i = pl.program_id(0)
        dxo_v = dxo_ref[...]
        ymv = ym_ref[...]
        gt = mod_ref[2:3, :]
        gpost = g_ref[1:2, :]
        rm = lax.rsqrt(_rowmean(ymv * ymv) + EPS)
        ymh = ymv * rm
        d_gt = _colsum(dxo_v * (ymh * gpost))
        dpm = gt * dxo_v
        d_gpost = _colsum(dpm * ymh)
        dymh = dpm * gpost
        dym = (rm * (dymh - ymh * _rowmean(dymh * ymh))).astype(BF16)
        dymb_ref[...] = dym
        dycat = _dot_nt(dym, wmo_ref[...])
        u = proj_ref[:, 0:WA]
        v0 = proj_ref[:, WA:2 * WA]
        vh, rv = _layer_norm_stats(v0)
        vb = (vh * v_ref[0:1, :] + v_ref[1:2, :]).astype(BF16)
        wm = _masked_spatial(ws_ref)
        zs = []
        for q in range(tm // CHUNK):
            zs.append(_spatial_gate(wm, vb[q * CHUNK:(q + 1) * CHUNK, :]) + bias_ref[...])
        z = jnp.concatenate(zs, axis=0)
        ya = u * z
        ra = lax.rsqrt(_rowmean(ya * ya) + EPS)
        yah = ya * ra
        ch, rc = _layer_norm_stats(conv_ref[...])
        cn = ch * v_ref[3:4, :] + v_ref[4:5, :]
        sg = _sigmoid(cn)
        yb = cn * sg
        rb = lax.rsqrt(_rowmean(yb * yb) + EPS)
        ybh = yb * rb
        ycat_ref[...] = jnp.concatenate([yah * v_ref[5:6, :], ybh * v_ref[6:7, :]], axis=1).astype(BF16)
        dpa = dycat[:, 0:WA]
        dpb = dycat[:, WA:2 * WA]
        d_goa = _colsum(dpa * yah)
        d_gob = _colsum(dpb * ybh)
        dyah = dpa * v_ref[5:6, :]
        dybh = dpb * v_ref[6:7, :]
        dya = ra * (dyah - yah * _rowmean(dyah * yah))
        dyb = rb * (dybh - ybh * _rowmean(dybh * ybh))
        dpart_ref[:, 0:WA] = dya * z
        dz = dya * u

        @pl.when(i == 0)
        def _():
            gws_ref[...] = jnp.zeros((NHEAD, CHUNK, CHUNK), F32)
            dbs_acc[...] = jnp.zeros((CHUNK, WA), F32)
            vg_ref[...] = jnp.zeros((8, D), F32)
            v5g_ref[...] = jnp.zeros((8, WA), F32)

        first = lax.broadcasted_iota(jnp.int32, (CHUNK, 128), 1) < HD
        dvs = []
        for q in range(tm // CHUNK):
            dz_q = dz[q * CHUNK:(q + 1) * CHUNK, :]
            vb_q = vb[q * CHUNK:(q + 1) * CHUNK, :]
            dbs_acc[...] += dz_q
            dzb = dz_q.astype(BF16)
            dvs.append(_head_pairs(wm, dzb, transpose=True))
            for hd in range(NHEAD):
                slab = dzb[:, _lanes(hd // 2)]
                dz_hd = jnp.where(first if hd % 2 == 0 else jnp.logical_not(first), slab, jnp.zeros_like(slab))
                gws_ref[hd] += _dot_nt(dz_hd, vb_q[:, _lanes(hd // 2)])
        dv = jnp.concatenate(dvs, axis=0)
        d_gng = _colsum(dv * vh)
        d_gnb = _colsum(dv)
        dvh = dv * v_ref[0:1, :]
        dpart_ref[:, WA:2 * WA] = rv * (dvh - _rowmean(dvh) - vh * _rowmean(dvh * vh))
        dcn = dyb * (sg * (1.0 + cn * (1.0 - sg)))
        d_cng = _colsum(dcn * ch)
        d_cnb = _colsum(dcn)
        dch = dcn * v_ref[3:4, :]
        dconv = rc * (dch - _rowmean(dch) - ch * _rowmean(dch * ch))
        dpart_ref[:, 2 * WA:3 * WA] = dconv
        dpart_ref[:, 3 * WA:4 * WA] = jnp.zeros((tm, WA), F32)
        d_cb = _colsum(dconv)

        @pl.when(i % tps == 0)
        def _():
            mg_ref[...] = jnp.zeros((8, D), F32)

        mg_ref[2:3, :] += d_gt
        vg_ref[1:2, :] += d_gpost
        v5g_ref[0:1, :] += d_gng
        v5g_ref[1:2, :] += d_gnb
        v5g_ref[2:3, :] += d_cb
        v5g_ref[3:4, :] += d_cng
        v5g_ref[4:5, :] += d_cnb
        v5g_ref[5:6, :] += d_goa
        v5g_ref[6:7, :] += d_gob

        @pl.when(i == nt - 1)
        def _():
            row = lax.broadcasted_iota(jnp.int32, (CHUNK, CHUNK), 0)
            col = lax.broadcasted_iota(jnp.int32, (CHUNK, CHUNK), 1)
            for hd in range(NHEAD):
                gws_ref[hd] = jnp.where(col <= row, gws_ref[hd], 0.0)
            gbs_ref[...] = lax.dot_general(e_ref[...], dbs_acc[...], (((1,), (1,)), ((), ())),
                                           precision=lax.Precision.HIGHEST, preferred_element_type=F32)

    tile = pl.BlockSpec((tm, D), lambda i: (i, 0))
    ptile = pl.BlockSpec((tm, 4 * WA), lambda i: (i, 0))
    return _call(
        core, name=name, grid=(nt,), jobs=jobs,
        in_specs=[tile, tile, pl.BlockSpec((tm, 2 * WA), lambda i: (i, 0)), pl.BlockSpec((tm, WA), lambda i: (i, 0)),
                  pl.BlockSpec((None, 8, D), lambda i: (i // tps, 0, 0)), _const_spec((8, D)), _const_spec((D, D)),
                  _const_spec((8, WA)), _const_spec((NHEAD, CHUNK, CHUNK)), _const_spec((CHUNK, WA)),
                  _const_spec((8, WA))],
        out_specs=[ptile, tile, tile, pl.BlockSpec((None, 8, D), lambda i: (i // tps, 0, 0)),
                   pl.BlockSpec((8, D), lambda i: (0, 0)), pl.BlockSpec((8, WA), lambda i: (0, 0)),
                   pl.BlockSpec((NHEAD, CHUNK, CHUNK), lambda i: (0, 0, 0)), pl.BlockSpec((8, CHUNK), lambda i: (0, 0))],
        out_shape=[jax.ShapeDtypeStruct((T, 4 * WA), F32), jax.ShapeDtypeStruct((T, D), BF16),
                   jax.ShapeDtypeStruct((T, D), BF16), jax.ShapeDtypeStruct((nb, 8, D), F32),
                   jax.ShapeDtypeStruct((8, D), F32), jax.ShapeDtypeStruct((8, WA), F32),
                   jax.ShapeDtypeStruct((NHEAD, CHUNK, CHUNK), F32), jax.ShapeDtypeStruct((8, CHUNK), F32)],
        scratch=[pltpu.VMEM((CHUNK, WA), F32)],
        args=[dxo, ym, proj, conv, mod, gvec, w_mo, v512, ws, bias_full, esel])


def _mixer_bwd_b(dxo, x, dpart, proj, mod, gvec, w_mi, cw, tm, name, jobs=()):
    T = x.shape[0]
    nt = T // tm
    nb = mod.shape[0]
    tps = nt // nb
    hpt = tm // HALO
    nh = T // HALO
    off = HALO - (CONV_K - 1)
    p = _pitch(tm)
    ext_rows = 8 * p

    def core(ins, outs, scs):
        dxo_ref, x_ref, dpart_ref, dnext_ref, ag_ref, halo_ref, mod_ref, g_ref, wmi_ref, cw_ref = ins
        dx_ref, dproj_ref, hb_ref, mg_ref, vg_ref, dcw_ref = outs
        glu_ext, dconv_ext, dglu_scr, dcw_acc = scs
        i = pl.program_id(0)
        first = i % tps == 0
        last = i % tps == tps - 1
        a = ag_ref[:, 0:WA]
        g = ag_ref[:, WA:2 * WA]
        sgg = _sigmoid(g)

        @pl.when(i == 0)
        def _():
            glu_ext[:, HALO + tm:HALO + ext_rows, :] = jnp.zeros((4, ext_rows - tm, 128), F32)
            dconv_ext[:, HALO + tm:HALO + ext_rows, :] = jnp.zeros((4, ext_rows - tm, 128), F32)
            dcw_acc[...] = jnp.zeros((32, 8, WA), F32)
            vg_ref[...] = jnp.zeros((8, D), F32)

        _to_slabs(glu_ext, 0, jnp.where(first, 0.0, halo_ref[:, 0:WA] * _sigmoid(halo_ref[:, WA:2 * WA])))
        _to_slabs(glu_ext, HALO, a * sgg)
        _to_slabs(dconv_ext, 0, dpart_ref[:, 2 * WA:3 * WA])
        _to_slabs(dconv_ext, tm, jnp.where(last, 0.0, dnext_ref[...]))
        sub = lax.broadcasted_iota(jnp.int32, (8, 128), 0)
        for s in range(4):
            accs = [jnp.zeros((8, 128), F32)] * CONV_K
            for v in range(p):
                dc = jnp.where(v + p * sub < tm, dconv_ext[s, pl.ds(v, 8, stride=p), :], 0.0)
                for k in range(CONV_K):
                    accs[k] = accs[k] + dc * glu_ext[s, pl.ds(v + off + k, 8, stride=p), :]
            for k in range(CONV_K):
                dcw_acc[k, :, _lanes(s)] += accs[k]
        dglu = _tap_sum(dconv_ext, dglu_scr, cw_ref, jnp.zeros((1, WA), F32), tm, lambda k: (CONV_K - 1) - k)

        @pl.when(i == nt - 1)
        def _():
            for k in range(CONV_K):
                dcw_ref[k:k + 1, :] = jnp.sum(dcw_acc[k], axis=0, keepdims=True)
            dcw_ref[CONV_K:32, :] = jnp.zeros((32 - CONV_K, WA), F32)

        da = dglu * sgg
        dgg = dglu * a * (sgg * (1.0 - sgg))
        dproj_ref[:, 0:2 * WA] = dpart_ref[:, 0:2 * WA].astype(BF16)
        dproj_ref[:, 2 * WA:3 * WA] = da.astype(BF16)
        dproj_ref[:, 3 * WA:4 * WA] = dgg.astype(BF16)
        dh = jnp.zeros((tm, D), F32)
        for j in range(NDEV):
            dh = dh + _dot_nt(dproj_ref[:, j * MB:(j + 1) * MB], wmi_ref[j])
        xv = x_ref[...]
        sc, sh = mod_ref[1:2, :], mod_ref[0:1, :]
        gpre = g_ref[0:1, :]
        r = lax.rsqrt(_rowmean(xv * xv) + EPS)
        xh = xv * r
        n = xh * gpre
        hb_ref[...] = (n * (1.0 + sc) + sh).astype(BF16)
        d_sc = _colsum(dh * n)
        d_sh = _colsum(dh)
        dn = dh * (1.0 + sc)
        d_gpre = _colsum(dn * xh)
        dxh = dn * gpre
        dx_ref[...] = dxo_ref[...] + r * (dxh - xh * _rowmean(dxh * xh))

        @pl.when(first)
        def _():
            mg_ref[...] = jnp.zeros((8, D), F32)

        mg_ref[0:1, :] += d_sh
        mg_ref[1:2, :] += d_sc
        vg_ref[0:1, :] += d_gpre

    tile = pl.BlockSpec((tm, D), lambda i: (i, 0))
    return _call(
        core, name=name, grid=(nt,), jobs=jobs,
        in_specs=[tile, tile, pl.BlockSpec((tm, 4 * WA), lambda i: (i, 0)),
                  pl.BlockSpec((HALO, WA), lambda i: (jnp.minimum((i + 1) * hpt, nh - 1), 2)),
                  pl.BlockSpec((tm, 2 * WA), lambda i: (i, 1)),
                  pl.BlockSpec((HALO, 2 * WA), lambda i: (jnp.maximum(i * hpt - 1, 0), 1)),
                  pl.BlockSpec((None, 8, D), lambda i: (i // tps, 0, 0)), _const_spec((8, D)),
                  _const_spec((NDEV, D, MB)), _const_spec((32, WA))],
        out_specs=[tile, pl.BlockSpec((tm, 4 * WA), lambda i: (i, 0)), tile,
                   pl.BlockSpec((None, 8, D), lambda i: (i // tps, 0, 0)), pl.BlockSpec((8, D), lambda i: (0, 0)),
                   pl.BlockSpec((32, WA), lambda i: (0, 0))],
        out_shape=[jax.ShapeDtypeStruct((T, D), F32), jax.ShapeDtypeStruct((T, 4 * WA), BF16),
                   jax.ShapeDtypeStruct((T, D), BF16), jax.ShapeDtypeStruct((nb, 8, D), F32),
                   jax.ShapeDtypeStruct((8, D), F32), jax.ShapeDtypeStruct((32, WA), F32)],
        scratch=[pltpu.VMEM((4, HALO + ext_rows, 128), F32), pltpu.VMEM((4, HALO + ext_rows, 128), F32),
                 pltpu.VMEM((4, ext_rows, 128), F32), pltpu.VMEM((32, 8, WA), F32)],
        args=[dxo, x, dpart, dpart, proj, proj, mod, gvec, w_mi, cw])


def _grad_chip(a, b, a_spec, b_spec, prod_shape, half, name, jobs=(), via_b=False):
    steps = 8 if half is None else 4
    R = prod_shape[0] if half is None else half
    C = prod_shape[1]

    def core(ins, outs, scs):
        a_ref, b_ref = ins
        (o_ref,) = outs
        own, snd, rcv, ssem, rsem, lsem = scs
        s = pl.program_id(0)
        c = lax.axis_index("c")
        me = _me()
        sib = _flip(me, (0, 0, 1))
        if via_b:
            prod = _dot_tn(b_ref[...], a_ref[...]).T.astype(BF16)
        else:
            prod = _dot_tn(a_ref[...], b_ref[...]).astype(BF16)
        if half is None:
            q = s // 2

            @pl.when(s % 2 == c)
            def _():
                own[q] = prod

            @pl.when(s % 2 != c)
            def _():
                snd[q] = prod
                _remote(snd.at[q], rcv.at[q], ssem.at[q], rsem.at[q], sib).start()
        else:
            lo = prod[0:half, :]
            hi = prod[half:2 * half, :]
            own[s] = jnp.where(c == 0, lo, hi)
            snd[s] = jnp.where(c == 0, hi, lo)
            _remote(snd.at[s], rcv.at[s], ssem.at[s], rsem.at[s], sib).start()

        @pl.when(s == steps - 1)
        def _():
            for q4 in range(4):
                cp = _remote(snd.at[q4], rcv.at[q4], ssem.at[q4], rsem.at[q4], sib)
                cp.wait_recv()
                cp.wait_send()
                snd[q4] = (own[q4].astype(F32) + rcv[q4].astype(F32)).astype(BF16)
            out = pltpu.make_async_copy(snd, o_ref, lsem)
            out.start()
            out.wait()

    return _call(
        core, name=name, grid=(steps,), jobs=jobs, in_specs=[a_spec, b_spec], out_specs=[HBM],
        out_shape=[jax.ShapeDtypeStruct((4, R, C), BF16)],
        scratch=[pltpu.VMEM((4, R, C), BF16), pltpu.VMEM((4, R, C), BF16), pltpu.VMEM((4, R, C), BF16),
                 pltpu.SemaphoreType.DMA((4,)), pltpu.SemaphoreType.DMA((4,)), pltpu.SemaphoreType.DMA],
        args=[a, b])


def _grad_w_in(dg, hb, name, jobs=()):
    T = hb.shape[0]
    return _grad_chip(dg, hb, pl.BlockSpec((None, T, FBP), lambda s: (s, 0, 0)), _const_spec((T, D)),
                      (FBP, D), None, name, jobs)


def _grad_w_out(act, dyb, name, jobs=()):
    T = dyb.shape[0]
    return _grad_chip(act, dyb, pl.BlockSpec((None, T, FBP), lambda s: (s, 0, 0)), _const_spec((T, D)),
                      (FBP, D), FO, name, jobs)


def _grad_w_mi(hb, dproj, name, jobs=()):
    T = hb.shape[0]
    return _grad_chip(hb, dproj, _const_spec((T, D)), pl.BlockSpec((T, MB), lambda s: (0, s)),
                      (D, MB), None, name, jobs, via_b=True)


def _grad_w_mo(ycat, dym, name, jobs=()):
    T = ycat.shape[0]
    return _grad_chip(ycat, dym, pl.BlockSpec((T, 2 * MO), lambda s: (0, s)), _const_spec((T, D)),
                      (2 * MO, D), MO, name, jobs)


def _adamw_math(w, g, m, v):
    m2 = ADAM_B1 * m + (1.0 - ADAM_B1) * g
    v2 = ADAM_B2 * v + (1.0 - ADAM_B2) * (g * g)
    m_hat = m2 / (1.0 - ADAM_B1 ** ADAM_STEP)
    v_hat = v2 / (1.0 - ADAM_B2 ** ADAM_STEP)
    delta = -ADAM_LR * (m_hat / (jnp.sqrt(v_hat) + ADAM_EPS) + ADAM_WD * w)
    return delta, m2, v2


def _adamw_reduce(parts, w, m, v, tr, name, own=None, after=None):
    R, C = w.shape

    def core(ins, outs, _):
        p_ref, w_ref, m_ref, v_ref = ins[:4]
        g_ref, d_ref, m2_ref, v2_ref = outs
        if own is None:
            terms = [p_ref[s].astype(F32) for s in range(4)]
        else:
            mq = 2 * lax.axis_index("x") + lax.axis_index("y")
            mine = ins[4][...].astype(F32)
            terms = [jnp.where(mq == s, mine, p_ref[s].astype(F32)) for s in range(4)]
        g = terms[0]
        for s in range(1, 4):
            g = g + terms[s]
        g_ref[...] = g
        d_ref[...], m2_ref[...], v2_ref[...] = _adamw_math(w_ref[...], g, m_ref[...], v_ref[...])

    blk = pl.BlockSpec((tr, C), lambda i: (i, 0))
    in_specs = [pl.BlockSpec((4, tr, C), lambda i: (0, i, 0)), blk, blk, blk]
    args = [parts, w, m, v]
    if own is not None:
        mq = 2 * lax.axis_index("x") + lax.axis_index("y")
        in_specs.append(pl.BlockSpec((tr, C), lambda i: (i, 0)))
        args.append(lax.dynamic_index_in_dim(own, mq, 0, keepdims=False))
    if after is not None:
        in_specs.append(HBM)
        args.append(after)
    return _call(
        core, name=name, grid=(R // tr,), in_specs=in_specs,
        out_specs=[blk, blk, blk, blk], out_shape=[jax.ShapeDtypeStruct((R, C), F32)] * 4, args=args)[0]


HBM_ONLY = pl.BlockSpec(memory_space=pltpu.HBM)
SEM = pl.BlockSpec(memory_space=pltpu.SEMAPHORE)
EFFECT = pltpu.SideEffectType.DATAFLOW_SIDE_EFFECTING


def _chip_scatter_start(gs, name):
    n = len(gs)

    def body(*refs):
        g_refs, land_refs = refs[:n], refs[n:2 * n]
        ssem, rsem = refs[2 * n:2 * n + 2]
        token = refs[-1]
        me = _me()
        mq = 2 * me[0] + me[1]
        for k, f in enumerate(CHIP_FLIPS):
            p = _flip(me, f)
            for a in range(n):
                _remote(g_refs[a].at[2 * p[0] + p[1]], land_refs[a].at[mq], ssem.at[3 * a + k], rsem.at[3 * a + k], p).start()
        token[...] = jnp.zeros_like(token)

    gs = [pltpu.with_memory_space_constraint(g, pltpu.HBM) for g in gs]
    lands = [pltpu.with_memory_space_constraint(lax.empty(g.shape, g.dtype), pltpu.HBM) for g in gs]
    res = pl.pallas_call(
        body, name=name,
        out_shape=(pltpu.SemaphoreType.DMA((3 * n,)), pltpu.SemaphoreType.DMA((3 * n,)))
        + tuple(pltpu.HBM(g.shape, g.dtype) for g in gs) * 2 + (jax.ShapeDtypeStruct((8, 128), F32),),
        in_specs=(HBM_ONLY,) * (2 * n), out_specs=(SEM, SEM) + (HBM_ONLY,) * (2 * n) + (VM,),
        input_output_aliases={a: 2 + a for a in range(2 * n)},
        compiler_params=pltpu.CompilerParams(has_side_effects=EFFECT),
    )(*gs, *lands)
    return res[:-1], res[-1]


def _chip_scatter_wait(handle, after, name):
    ssem, rsem = handle[:2]
    n = (len(handle) - 2) // 2
    thru = handle[2:]

    def body(*refs):
        g_refs, land_refs = refs[:n], refs[n:2 * n]
        ssem, rsem = refs[2 * n:2 * n + 2]
        me = _me()
        mq = 2 * me[0] + me[1]
        for k, f in enumerate(CHIP_FLIPS):
            p = _flip(me, f)
            pq = 2 * p[0] + p[1]
            for a in range(n):
                _remote(g_refs[a].at[pq], land_refs[a].at[mq], ssem.at[3 * a + k], rsem.at[3 * a + k], p).wait_send()
                _remote(g_refs[a].at[mq], land_refs[a].at[pq], ssem.at[3 * a + k], rsem.at[3 * a + k], p).wait_recv()

    res = pl.pallas_call(
        body, name=name,
        out_shape=tuple(pltpu.HBM(t.shape, t.dtype) for t in thru),
        in_specs=(HBM_ONLY,) * (2 * n) + (SEM, SEM, HBM), out_specs=(HBM_ONLY,) * (2 * n),
        input_output_aliases={a: a for a in range(2 * n)},
        compiler_params=pltpu.CompilerParams(has_side_effects=EFFECT),
    )(*thru, ssem, rsem, after)
    return list(res[:n]), list(res[n:])


def _adamw_ada(sc_all, dd, w, m, v, tr, name, after=None):
    R, C = w.shape

    def core(ins, outs, _):
        sc_ref, dd_ref, w_ref, m_ref, v_ref = ins[:5]
        g_ref, d_ref, m2_ref, v2_ref = outs
        g = _dot_tn(sc_ref[...].astype(BF16), dd_ref[...].astype(BF16))
        g_ref[...] = g
        d_ref[...], m2_ref[...], v2_ref[...] = _adamw_math(w_ref[...], g, m_ref[...], v_ref[...])

    blk = pl.BlockSpec((tr, C), lambda i: (i, 0))
    return _call(
        core, name=name, grid=(R // tr,),
        in_specs=[pl.BlockSpec((64, tr), lambda i: (0, i)), pl.BlockSpec((64, C), lambda i: (0, 0)), blk, blk, blk]
        + [HBM] * (after is not None),
        out_specs=[blk, blk, blk, blk], out_shape=[jax.ShapeDtypeStruct((R, C), F32)] * 4,
        args=[sc_all, dd, w, m, v] + [after] * (after is not None))[0]


def _adamw_small(gathered, plain, grads, wmv, emit, name):
    nw = len(grads)
    ng, npl, ne = len(gathered), len(plain), len(emit)

    def core(ins, outs, _):
        srcs = []
        for a in range(ng):
            s = ins[a][0]
            for dev in range(1, NDEV):
                s = s + ins[a][dev]
            srcs.append(s)
        srcs += [ins[ng + a][...] for a in range(npl)]
        w_refs = ins[ng + npl:]
        for e, a in enumerate(emit):
            outs[e][...] = srcs[a]
        for t in range(nw):
            src, row = grads[t]
            g = srcs[src] if row is None else srcs[src][row:row + 1, :]
            w_ref, m_ref, v_ref = w_refs[3 * t:3 * t + 3]
            g_ref, d_ref, m2_ref, v2_ref = outs[ne + 4 * t:ne + 4 * t + 4]
            g_ref[...] = g
            d_ref[...], m2_ref[...], v2_ref[...] = _adamw_math(w_ref[...], g, m_ref[...], v_ref[...])

    out_shape = [jax.ShapeDtypeStruct(gathered[a].shape[1:], F32) for a in emit]
    for t in range(nw):
        out_shape += [jax.ShapeDtypeStruct(wmv[3 * t].shape, F32)] * 4
    return _call(
        core, name=name, grid=(), in_specs=[VM] * (ng + npl + 3 * nw), out_specs=[VM] * (ne + 4 * nw),
        out_shape=out_shape, args=list(gathered) + list(plain) + list(wmv))[0]


def _ada_fwd(c_pad, w_ada, b_cols, cw_pad, jobs=()):
    def core(ins, outs, scs):
        c_ref, w_ref, b_ref, cwp_ref = ins
        ada_ref, sc_ref, cw_ref = outs
        cbuf, send_buf, ssem, rsem = scs
        me = _me()
        mi = _lin(me)
        cbuf[mi] = c_ref[...]
        cw_ref[mi] = cwp_ref[...]
        peers = [_flip(me, f) for f in FLIPS]
        first = []
        for k, p in enumerate(peers):
            first.append(_remote(cbuf.at[mi], cbuf.at[mi], ssem.at[k], rsem.at[k], p))
            first.append(_remote(cw_ref.at[mi], cw_ref.at[mi], ssem.at[7 + k], rsem.at[7 + k], p))
        for cp in first:
            cp.start()
        for k, p in enumerate(peers):
            pi = _lin(p)
            _remote(cbuf.at[pi], cbuf.at[pi], ssem.at[k], rsem.at[k], p).wait_recv()
            _remote(cw_ref.at[pi], cw_ref.at[pi], ssem.at[7 + k], rsem.at[7 + k], p).wait_recv()
        c_all = cbuf[...].reshape(8 * 8, D)
        sc = c_all * _sigmoid(c_all)
        sc_ref[...] = sc
        res = _dot(sc.astype(BF16), w_ref[...].astype(BF16)) + b_ref[...]
        send_buf[...] = res.reshape(8, 8, ADA_B)
        ada_ref[mi] = send_buf[mi]
        second = []
        for k, p in enumerate(peers):
            second.append(_remote(send_buf.at[_lin(p)], ada_ref.at[mi], ssem.at[14 + k], rsem.at[14 + k], p))
        for cp in second:
            cp.start()
        for k, p in enumerate(peers):
            _remote(send_buf.at[mi], ada_ref.at[_lin(p)], ssem.at[14 + k], rsem.at[14 + k], p).wait_recv()
        for cp in first + second:
            cp.wait_send()

    return _call(
        core, name="ada_fwd", grid=(), jobs=jobs, in_specs=[VM, VM, VM, VM], out_specs=[VM, VM, VM],
        out_shape=[jax.ShapeDtypeStruct((8, 8, ADA_B), F32), jax.ShapeDtypeStruct((64, D), F32),
                   jax.ShapeDtypeStruct((8, 32, 64), F32)],
        scratch=[pltpu.VMEM((8, 8, D), F32), pltpu.VMEM((8, 8, ADA_B), F32),
                 pltpu.SemaphoreType.DMA((21,)), pltpu.SemaphoreType.DMA((21,))],
        args=[c_pad, w_ada, b_cols, cw_pad])


def _ada_bwd(dada, jobs=()):
    def core(ins, outs, scs):
        (d_ref,) = ins
        dd_ref, gb_ref = outs
        rbuf, ssem, rsem = scs
        me = _me()
        mi = _lin(me)
        peers = [_flip(me, f) for f in FLIPS]
        rbuf[mi] = d_ref[mi]
        first = []
        for k, p in enumerate(peers):
            first.append(_remote(d_ref.at[_lin(p)], rbuf.at[mi], ssem.at[k], rsem.at[k], p))
        for cp in first:
            cp.start()
        for k, p in enumerate(peers):
            _remote(d_ref.at[mi], rbuf.at[_lin(p)], ssem.at[k], rsem.at[k], p).wait_recv()
        dd = rbuf[...].reshape(64, ADA_B)
        dd_ref[...] = dd
        gb_ref[mi] = jnp.broadcast_to(_colsum(dd), (8, ADA_B))
        second = []
        for k, p in enumerate(peers):
            second.append(_remote(gb_ref.at[mi], gb_ref.at[mi], ssem.at[7 + k], rsem.at[7 + k], p))
        for cp in second:
            cp.start()
        for k, p in enumerate(peers):
            pi = _lin(p)
            _remote(gb_ref.at[pi], gb_ref.at[pi], ssem.at[7 + k], rsem.at[7 + k], p).wait_recv()
        for cp in first + second:
            cp.wait_send()

    return _call(
        core, name="ada_bwd", grid=(), jobs=jobs, in_specs=[VM], out_specs=[VM, VM],
        out_shape=[jax.ShapeDtypeStruct((64, ADA_B), F32), jax.ShapeDtypeStruct((8, 8, ADA_B), F32)],
        scratch=[pltpu.VMEM((8, 8, ADA_B), F32), pltpu.SemaphoreType.DMA((14,)), pltpu.SemaphoreType.DMA((14,))],
        args=[dada])


SMALL_D = ("g_pre_f1", "g_post_f1", "g_pre_m", "g_post_m", "g_pre_f2", "g_post_f2")
SMALL_W = ("gmlp_norm_g", "gmlp_norm_b", "conv_b", "conv_norm_g", "conv_norm_b", "g_out_a", "g_out_b")


def kernel(x, c, w_ada, b_ada, g_pre_f1, g_post_f1, w_f1_in, w_f1_out, g_pre_m, g_post_m, w_mix_in, gmlp_norm_g, gmlp_norm_b, w_spatial, b_spatial, conv_w, conv_b, conv_norm_g, conv_norm_b, g_out_a, g_out_b, w_mix_out, g_pre_f2, g_post_f2, w_f2_in, w_f2_out, loss_target, m_w_ada, m_b_ada, m_g_pre_f1, m_g_post_f1, m_w_f1_in, m_w_f1_out, m_g_pre_m, m_g_post_m, m_w_mix_in, m_gmlp_norm_g, m_gmlp_norm_b, m_w_spatial, m_b_spatial, m_conv_w, m_conv_b, m_conv_norm_g, m_conv_norm_b, m_g_out_a, m_g_out_b, m_w_mix_out, m_g_pre_f2, m_g_post_f2, m_w_f2_in, m_w_f2_out, v_w_ada, v_b_ada, v_g_pre_f1, v_g_post_f1, v_w_f1_in, v_w_f1_out, v_g_pre_m, v_g_post_m, v_w_mix_in, v_gmlp_norm_g, v_gmlp_norm_b, v_w_spatial, v_b_spatial, v_conv_w, v_conv_b, v_conv_norm_g, v_conv_norm_b, v_g_out_a, v_g_out_b, v_w_mix_out, v_g_pre_f2, v_g_post_f2, v_w_f2_in, v_w_f2_out):
    given = dict(locals())
    bl, seq, _ = x.shape
    T = bl * seq
    tm = min(256, seq // 2)
    mi = _lin((lax.axis_index("x"), lax.axis_index("y"), lax.axis_index("c")))

    def shard_in(w):
        return jnp.pad(w[0].T.astype(BF16), ((0, FBP - FB), (0, 0)))

    zpad = jnp.zeros((max(FBP - FB, 16), D), BF16)
    g_f1 = _Gather([shard_in(w_f1_in), w_f1_out[0].astype(BF16)], ("rows", "out"), zpad)
    g_mx = _Gather([w_mix_in[0].astype(BF16), w_mix_out[0].astype(BF16), shard_in(w_f2_in)], ("rows",) * 3, zpad,
                   late_mid=True)
    g_f2 = _Gather([w_f2_out[0].astype(BF16)], ("out",), zpad)

    c_pad = jnp.pad(c, ((0, 8 - bl), (0, 0)))
    b_cols = lax.dynamic_slice(b_ada, (0, mi * ADA_B), (1, ADA_B))
    cw_pad = jnp.pad(conv_w[0], ((0, 1), (0, 0)))
    (ada_blk, sc_all, cw_all), ((wi1, wo1),) = _ada_fwd(c_pad, w_ada[0], b_cols, cw_pad, jobs=[g_f1])
    ada = ada_blk[:, 0:bl, :].transpose(1, 0, 2).reshape(bl, 9, D)
    pad5 = jnp.zeros((bl, 5, D), F32)
    mod1 = jnp.concatenate([ada[:, 0:3], pad5], axis=1)
    mod2 = jnp.concatenate([ada[:, 3:6], pad5], axis=1)
    mod3 = jnp.concatenate([ada[:, 6:9], pad5], axis=1)
    cw_full = cw_all.transpose(1, 0, 2).reshape(32, WA)

    zrow = jnp.zeros((1, D), F32)
    gv1 = jnp.concatenate([g_pre_f1, g_post_f1] + [zrow] * 6, axis=0)
    gvm = jnp.concatenate([g_pre_m, g_post_m] + [zrow] * 6, axis=0)
    gv2 = jnp.concatenate([g_pre_f2, g_post_f2] + [zrow] * 6, axis=0)
    v512 = jnp.concatenate([gmlp_norm_g, gmlp_norm_b, conv_b, conv_norm_g, conv_norm_b, g_out_a, g_out_b,
                            jnp.zeros((1, WA), F32)], axis=0)
    ws = w_spatial[0]
    bias_full = jnp.repeat(b_spatial[0].T, HD, axis=1)
    esel = (lax.broadcasted_iota(jnp.int32, (8, WA), 1) // HD == lax.broadcasted_iota(jnp.int32, (8, WA), 0)).astype(F32)

    x0 = x.reshape(T, D)
    (x1, gu1, y1), ((wmi, wmo, wi2),) = _ffn_fwd(x0, mod1, gv1, wi1, wo1, tm, "ffn1_fwd", jobs=[g_mx])
    wmo = wmo.reshape(D, D)
    (x2, proj, ym, conv), ((wo2,),) = _mixer_fwd(x1, mod2, gvm, wmi, wmo, v512, ws, bias_full, cw_full, tm, "mixer_fwd", jobs=[g_f2])

    (dx2, dg2, act2, hb2, dyb2, mg3, vg3, loss_blk), _ = _ffn_last(
        x2, loss_target.reshape(T, D), mod3, gv2, wi2, wo2, tm, "ffn2_fwd_bwd")
    (g_wi2,), _ = _grad_w_in(dg2, hb2, "ffn2_gw_in")
    (g_wo2,), _ = _grad_w_out(act2, dyb2, "ffn2_gw_out")
    (dpart, dymb, ycat, mg2a, vgma, v5g, gws, gbs), ((p_wo2,),) = _mixer_bwd_a(
        dx2, ym, proj, conv, mod2, gvm, wmo, v512, ws, bias_full, esel, tm, "mixer_bwd_a",
        jobs=[_ChipScatter([g_wo2])])
    (dx1, dproj, hbm, mg2b, vgmb, dcw), ((p_wi2,),) = _mixer_bwd_b(
        dx2, x1, dpart, proj, mod2, gvm, wmi, cw_full, tm, "mixer_bwd_b", jobs=[_ChipScatter([g_wi2])])
    (g_wmi,), _ = _grad_w_mi(hbm, dproj, "mixer_gw_in")
    (g_wmo,), _ = _grad_w_mo(ycat, dymb, "mixer_gw_out")
    p2 = jnp.concatenate([v5g, dcw], axis=0)
    (dx0, dg1, act1, hb1, dyb1, mg1, vg1), _ = _ffn_bwd(dx1, x0, y1, gu1, mod1, gv1, wi1, wo1, tm, "ffn1_bwd")

    dada = jnp.concatenate([mg1[:, 0:3], mg2b[:, 0:2], mg2a[:, 2:3], mg3[:, 0:3]], axis=1)
    dada = dada.reshape(bl, NDEV, ADA_B).transpose(1, 0, 2)
    dada = jnp.pad(dada, ((0, 0), (0, 8 - bl), (0, 0)))
    p1 = jnp.concatenate([vg1[0:2], vgmb[0:1], vgma[1:2], vg3[0:2], loss_blk[0:1], zrow], axis=0)
    (dd_all, gb_all), ((a1,),) = _ada_bwd(dada, jobs=[_AllGather([p1])])
    g_bada = gb_all[:, 0, :].reshape(1, 9 * D)

    (g_wo1,), ((p_wmi, p_wmo),) = _grad_w_out(act1, dyb1, "ffn1_gw_out", jobs=[_ChipScatter([g_wmi, g_wmo])])
    (g_wi1,), ((a2, a3, a4), (p_wo1,)) = _grad_w_in(
        dg1, hb1, "ffn1_gw_in", jobs=[_Gather([p2, gws, gbs], ("rows",) * 3, zpad), _ChipScatter([g_wo1])])

    h_f1, token = _chip_scatter_start([g_wi1], "tail_start")

    res = {}
    quad = _adamw_reduce(p_wi2, w_f2_in[0].T, m_w_f2_in[0].T, v_w_f2_in[0].T, FO, "adamw_w_f2_in", after=token)
    res["w_f2_in"] = tuple(t.T[None] for t in quad)
    for nm, part, tr in (("w_f2_out", p_wo2, FO), ("w_mix_in", p_wmi, 256), ("w_mix_out", p_wmo, MO), ("w_f1_out", p_wo1, FO)):
        quad = _adamw_reduce(part, given[nm][0], given["m_" + nm][0], given["v_" + nm][0], tr, "adamw_" + nm, after=quad[1])
        res[nm] = tuple(t[None] for t in quad)
    quad = _adamw_ada(sc_all, dd_all, w_ada[0], m_w_ada[0], v_w_ada[0], 256, "adamw_w_ada", after=quad[1])
    res["w_ada"] = tuple(t[None] for t in quad)
    (g_wi1,), (p_wi1,) = _chip_scatter_wait(h_f1, quad[1], "tail_wait")
    quad = _adamw_reduce(p_wi1, w_f1_in[0].T, m_w_f1_in[0].T, v_w_f1_in[0].T, FO, "adamw_w_f1_in", own=g_wi1)
    res["w_f1_in"] = tuple(t.T[None] for t in quad)

    small = SMALL_D + SMALL_W + ("w_spatial", "b_spatial", "b_ada")
    grads = [(0, r) for r in range(6)] + [(1, r) for r in range(7)] + [(2, None), (3, None), (4, None)]
    wmv = []
    for nm in small:
        for pre in ("", "m_", "v_"):
            wmv.append(given[pre + nm][0] if nm in ("w_spatial", "b_spatial") else given[pre + nm])
    outs = _adamw_small([a1, a2, a3, a4], [g_bada], grads, wmv, (0, 1), "adamw_small")
    loss = outs[0][6, 0]
    for t, nm in enumerate(small):
        quad = outs[2 + 4 * t:6 + 4 * t]
        res[nm] = tuple(q[None] for q in quad) if nm in ("w_spatial", "b_spatial") else tuple(quad)
    g_cw = lax.dynamic_slice(outs[1], (8, mi * 64), (32, 64))
    wmv = [jnp.pad(given[pre + "conv_w"][0], ((0, 1), (0, 0)), constant_values=1.0 if pre == "v_" else 0.0)
           for pre in ("", "m_", "v_")]
    quad = _adamw_small([], [g_cw], [(0, None)], wmv, (), "adamw_conv_w")
    res["conv_w"] = tuple(q[0:CONV_K][None] for q in quad)

    order = ["w_ada", "b_ada", "g_pre_f1", "g_post_f1", "w_f1_in", "w_f1_out", "g_pre_m", "g_post_m", "w_mix_in",
             "gmlp_norm_g", "gmlp_norm_b", "w_spatial", "b_spatial", "conv_w", "conv_b", "conv_norm_g", "conv_norm_b",
             "g_out_a", "g_out_b", "w_mix_out", "g_pre_f2", "g_post_f2", "w_f2_in", "w_f2_out"]
    out = [loss, dx0.reshape(bl, seq, D)]
    for k in range(4):
        out += [res[nm][k] for nm in order]
    return tuple(out)
```

```python
import jax
import jax.numpy as jnp
from jax import lax
from jax.experimental import pallas as pl
from jax.experimental.pallas import tpu as pltpu

F32 = jnp.float32
BF16 = jnp.bfloat16

D = 1024
DFF = 2816
NDEV = 8
FB = 2 * DFF // NDEV
FBP = 704
FO = DFF // NDEV
WA = 512
NHEAD = 8
HD = 64
CHUNK = 128
CONV_K = 31
HALO = 32
MB = 2 * (WA + WA) // NDEV
MO = D // NDEV
ADA_B = 9 * D // NDEV
EPS = 1e-6
HALF = 0.5

ADAM_LR = 0.001
ADAM_B1 = 0.9
ADAM_B2 = 0.999
ADAM_EPS = 1e-08
ADAM_WD = 0.01
ADAM_STEP = 10

VMEM_LIMIT = 56 * 1024 * 1024
MESH = pl.DeviceIdType.MESH
FLIPS = ((0, 0, 1), (1, 0, 0), (0, 1, 0), (1, 1, 0), (1, 0, 1), (0, 1, 1), (1, 1, 1))
CHIP_FLIPS = ((1, 0, 0), (0, 1, 0), (1, 1, 0))
HBM = pl.BlockSpec(memory_space=pl.ANY)
VM = pl.BlockSpec(memory_space=pltpu.VMEM)


def _dot(a, b):
    return lax.dot_general(a, b, (((1,), (0,)), ((), ())), preferred_element_type=F32)


def _dot_nt(a, b):
    return lax.dot_general(a, b, (((1,), (1,)), ((), ())), preferred_element_type=F32)


def _dot_tn(a, b):
    return lax.dot_general(a, b, (((0,), (0,)), ((), ())), preferred_element_type=F32)


def _rowmean(v):
    return jnp.mean(v, axis=-1, keepdims=True)


def _colsum(v):
    return jnp.sum(v, axis=0, keepdims=True)


def _sigmoid(v):
    return 0.5 * jnp.tanh(0.5 * v) + 0.5


def _const_spec(shape):
    nd = len(shape)
    return pl.BlockSpec(shape, lambda *_: (0,) * nd, pipeline_mode=pl.Buffered(1))


def _me():
    return lax.axis_index("x"), lax.axis_index("y"), lax.axis_index("c")


def _flip(me, f):
    return tuple(1 - v if b else v for v, b in zip(me, f))


def _lin(p):
    return 4 * p[0] + 2 * p[1] + p[2]


def _remote(src, dst, send_sem, recv_sem, dev):
    return pltpu.make_async_remote_copy(src_ref=src, dst_ref=dst, send_sem=send_sem, recv_sem=recv_sem,
                                        device_id=dev, device_id_type=MESH)


def _blk(kind, ref, p):
    if kind == "out":
        return ref.at[2 * p[0] + p[1], pl.ds(p[2] * FO, FO), :]
    return ref.at[_lin(p)]


class _Gather:
    def __init__(self, shards, kinds, zpad, late_mid=False):
        self.late_mid = late_mid
        self.kinds = kinds
        self.n = len(shards)
        self.ins = list(shards) + [zpad]
        self.out_shape = [jax.ShapeDtypeStruct((4, FBP, D) if k == "out" else (NDEV,) + s.shape, s.dtype)
                          for s, k in zip(shards, kinds)]
        self.n_out = sum(k == "out" for k in kinds)
        self.sems = [pltpu.SemaphoreType.DMA((7 * self.n,)), pltpu.SemaphoreType.DMA((7 * self.n,)),
                     pltpu.SemaphoreType.DMA((self.n + 4 * max(self.n_out, 1),))]

    def _first(self, ins, outs, sems):
        ssem, rsem, lsem = sems
        me = _me()
        sib = _flip(me, (0, 0, 1))
        cps, loc = [], []
        nz = 0
        for a in range(self.n):
            mine = _blk(self.kinds[a], outs[a], me)
            loc.append(pltpu.make_async_copy(ins[a], mine, lsem.at[a]))
            if self.kinds[a] == "out" and FBP > FB:
                for q in range(4):
                    loc.append(pltpu.make_async_copy(ins[self.n], outs[a].at[q, pl.ds(FB, FBP - FB), :],
                                                     lsem.at[self.n + 4 * nz + q]))
                nz += 1
            cps.append(_remote(ins[a], mine, ssem.at[7 * a], rsem.at[7 * a], sib))
            for j, f in enumerate(CHIP_FLIPS):
                cps.append(_remote(ins[a], mine, ssem.at[7 * a + 1 + j], rsem.at[7 * a + 1 + j], _flip(me, f)))
        return cps, loc

    def _passed(self, outs, sems):
        ssem, rsem, _ = sems
        me = _me()
        sib = _flip(me, (0, 0, 1))
        cps = []
        for j, f in enumerate(CHIP_FLIPS):
            for a in range(self.n):
                blk = _blk(self.kinds[a], outs[a], _flip(me, f))
                cps.append(_remote(blk, blk, ssem.at[7 * a + 4 + j], rsem.at[7 * a + 4 + j], sib))
        return cps

    def start(self, ins, outs, sems):
        cps, loc = self._first(ins, outs, sems)
        for cp in loc + cps:
            cp.start()

    def mid(self, ins, outs, sems):
        ssem, rsem, _ = sems
        me = _me()
        passed = self._passed(outs, sems)
        t = 0
        for j, f in enumerate(CHIP_FLIPS):
            for a in range(self.n):
                blk = _blk(self.kinds[a], outs[a], _flip(me, f))
                _remote(blk, blk, ssem.at[7 * a + 1 + j], rsem.at[7 * a + 1 + j], _flip(me, f)).wait_recv()
                passed[t].start()
                t += 1

    def end(self, ins, outs, sems):
        ssem, rsem, _ = sems
        me = _me()
        sib = _flip(me, (0, 0, 1))
        for a in range(self.n):
            blk = _blk(self.kinds[a], outs[a], sib)
            _remote(blk, blk, ssem.at[7 * a], rsem.at[7 * a], sib).wait_recv()
            for j, f in enumerate(CHIP_FLIPS):
                blk = _blk(self.kinds[a], outs[a], _flip(_flip(me, f), (0, 0, 1)))
                _remote(blk, blk, ssem.at[7 * a + 4 + j], rsem.at[7 * a + 4 + j], sib).wait_recv()
        cps, loc = self._first(ins, outs, sems)
        for cp in cps + self._passed(outs, sems):
            cp.wait_send()
        for cp in loc:
            cp.wait()


class _ChipScatter:
    def __init__(self, grads):
        self.n = len(grads)
        self.ins = list(grads)
        self.out_shape = [jax.ShapeDtypeStruct(g.shape, BF16) for g in grads]
        self.sems = [pltpu.SemaphoreType.DMA((3 * self.n,)), pltpu.SemaphoreType.DMA((3 * self.n,)),
                     pltpu.SemaphoreType.DMA((self.n,))]

    def _copies(self, ins, outs, sems):
        ssem, rsem, lsem = sems
        me = _me()
        mq = 2 * me[0] + me[1]
        loc = [pltpu.make_async_copy(ins[a].at[mq], outs[a].at[mq], lsem.at[a]) for a in range(self.n)]
        cps = []
        for k, f in enumerate(CHIP_FLIPS):
            p = _flip(me, f)
            for a in range(self.n):
                cps.append(_remote(ins[a].at[2 * p[0] + p[1]], outs[a].at[mq], ssem.at[3 * a + k], rsem.at[3 * a + k], p))
        return cps, loc

    def start(self, ins, outs, sems):
        cps, loc = self._copies(ins, outs, sems)
        for cp in loc + cps:
            cp.start()

    mid = None

    def end(self, ins, outs, sems):
        ssem, rsem, _ = sems
        me = _me()
        mq = 2 * me[0] + me[1]
        for k, f in enumerate(CHIP_FLIPS):
            p = _flip(me, f)
            for a in range(self.n):
                _remote(ins[a].at[mq], outs[a].at[2 * p[0] + p[1]], ssem.at[3 * a + k], rsem.at[3 * a + k], p).wait_recv()
        cps, loc = self._copies(ins, outs, sems)
        for cp in cps:
            cp.wait_send()
        for cp in loc:
            cp.wait()


class _AllGather:
    def __init__(self, parts):
        self.n = len(parts)
        self.ins = list(parts)
        self.out_shape = [jax.ShapeDtypeStruct((NDEV,) + p.shape, p.dtype) for p in parts]
        self.sems = [pltpu.SemaphoreType.DMA((7 * self.n,)), pltpu.SemaphoreType.DMA((7 * self.n,)),
                     pltpu.SemaphoreType.DMA((self.n,))]

    def _copies(self, ins, outs, sems):
        ssem, rsem, lsem = sems
        me = _me()
        mi = _lin(me)
        loc = [pltpu.make_async_copy(ins[a], outs[a].at[mi], lsem.at[a]) for a in range(self.n)]
        cps = []
        for k, f in enumerate(FLIPS):
            for a in range(self.n):
                cps.append(_remote(ins[a], outs[a].at[mi], ssem.at[7 * a + k], rsem.at[7 * a + k], _flip(me, f)))
        return cps, loc

    def start(self, ins, outs, sems):
        cps, loc = self._copies(ins, outs, sems)
        for cp in loc + cps:
            cp.start()

    mid = None

    def end(self, ins, outs, sems):
        ssem, rsem, _ = sems
        me = _me()
        for k, f in enumerate(FLIPS):
            p = _flip(me, f)
            for a in range(self.n):
                _remote(ins[a], outs[a].at[_lin(p)], ssem.at[7 * a + k], rsem.at[7 * a + k], p).wait_recv()
        cps, loc = self._copies(ins, outs, sems)
        for cp in cps:
            cp.wait_send()
        for cp in loc:
            cp.wait()


def _call(core, *, name, grid, in_specs, out_specs, out_shape, args, scratch=(), jobs=()):
    n_in, n_out, n_sc = len(in_specs), len(out_specs), len(scratch)
    steps = 1
    for g in grid:
        steps *= g

    def body(*refs):
        pos = [0]

        def take(k):
            r = refs[pos[0]:pos[0] + k]
            pos[0] += k
            return r

        ins = take(n_in)
        j_ins = [take(len(j.ins)) for j in jobs]
        outs = take(n_out)
        j_outs = [take(len(j.out_shape)) for j in jobs]
        scs = take(n_sc)
        j_sems = [take(len(j.sems)) for j in jobs]
        if len(grid) == 2:
            step = pl.program_id(0) * grid[1] + pl.program_id(1)
        elif len(grid) == 1:
            step = pl.program_id(0)
        else:
            step = 0
        for j, ji, jo, js in zip(jobs, j_ins, j_outs, j_sems):
            if grid:
                pl.when(step == 0)(lambda j=j, ji=ji, jo=jo, js=js: j.start(ji, jo, js))
            else:
                j.start(ji, jo, js)
        for j, ji, jo, js in zip(jobs, j_ins, j_outs, j_sems):
            if j.mid is not None and grid:
                at = max(steps - 2, 0) if j.late_mid else (3 * steps) // 4
                pl.when(step == at)(lambda j=j, ji=ji, jo=jo, js=js: j.mid(ji, jo, js))
        if core is not None:
            core(ins, outs, scs)
        for j, ji, jo, js in zip(jobs, j_ins, j_outs, j_sems):
            if grid:
                pl.when(step == steps - 1)(lambda j=j, ji=ji, jo=jo, js=js: j.end(ji, jo, js))
            else:
                if j.mid is not None:
                    j.mid(ji, jo, js)
                j.end(ji, jo, js)

    all_in = list(in_specs)
    all_args = list(args)
    all_out = list(out_specs)
    all_shape = list(out_shape)
    all_sc = list(scratch)
    for j in jobs:
        all_in += [HBM] * len(j.ins)
        all_args += j.ins
    for j in jobs:
        all_out += [HBM] * len(j.out_shape)
        all_shape += j.out_shape
        all_sc += j.sems
    params = dict(vmem_limit_bytes=VMEM_LIMIT)
    if grid:
        params["dimension_semantics"] = ("arbitrary",) * len(grid)
    res = pl.pallas_call(
        body, name=name, grid=grid, in_specs=all_in, out_specs=all_out, out_shape=all_shape,
        scratch_shapes=all_sc, compiler_params=pltpu.CompilerParams(**params),
    )(*all_args)
    core_res = list(res[:n_out])
    job_res = []
    pos = n_out
    for j in jobs:
        job_res.append(list(res[pos:pos + len(j.out_shape)]))
        pos += len(j.out_shape)
    return core_res, job_res


def _ffn_fwd(x, mod, gvec, w_in, w_out, tm, name, jobs=()):
    T = x.shape[0]
    nt = T // tm
    tps = nt // mod.shape[0]

    def core(ins, outs, _):
        x_ref, mod_ref, g_ref, win_ref, wout_ref = ins
        xo_ref, gu_ref, y_ref = outs
        xv = x_ref[...]
        sh, sc, gt = mod_ref[0:1, :], mod_ref[1:2, :], mod_ref[2:3, :]
        r = lax.rsqrt(_rowmean(xv * xv) + EPS)
        h = (xv * r * g_ref[0:1, :]) * (1.0 + sc) + sh
        hb = h.astype(BF16)
        y = jnp.zeros((tm, D), F32)
        for cidx in range(4):
            gate = _dot_nt(hb, win_ref[cidx])
            up = _dot_nt(hb, win_ref[4 + cidx])
            gu_ref[cidx] = gate.astype(BF16)
            gu_ref[4 + cidx] = up.astype(BF16)
            act = gate * _sigmoid(gate) * up
            y = y + _dot(act.astype(BF16), wout_ref[cidx])
        y_ref[...] = y
        ry = lax.rsqrt(_rowmean(y * y) + EPS)
        xo_ref[...] = xv + (HALF * gt) * (y * ry * g_ref[1:2, :])

    tile = pl.BlockSpec((tm, D), lambda i: (i, 0))
    return _call(
        core, name=name, grid=(nt,), jobs=jobs,
        in_specs=[tile, pl.BlockSpec((None, 8, D), lambda i: (i // tps, 0, 0)), _const_spec((8, D)),
                  _const_spec((8, FBP, D)), _const_spec((4, FBP, D))],
        out_specs=[tile, pl.BlockSpec((8, tm, FBP), lambda i: (0, i, 0)), tile],
        out_shape=[jax.ShapeDtypeStruct((T, D), F32), jax.ShapeDtypeStruct((8, T, FBP), BF16),
                   jax.ShapeDtypeStruct((T, D), F32)],
        args=[x, mod, gvec, w_in, w_out])


def _ffn_bwd(dxo, x, y, gu, mod, gvec, w_in, w_out, tm, name, jobs=()):
    T = x.shape[0]
    nt = T // tm
    nb = mod.shape[0]
    tps = nt // nb

    def core(ins, outs, _):
        dxo_ref, x_ref, y_ref, gu_ref, mod_ref, g_ref, win_ref, wout_ref = ins
        dx_ref, dg_ref, act_ref, hb_ref, dyb_ref, mg_ref, vg_ref = outs
        i = pl.program_id(0)
        xv = x_ref[...]
        dxo_v = dxo_ref[...]
        yv = y_ref[...]
        sh, sc, gt = mod_ref[0:1, :], mod_ref[1:2, :], mod_ref[2:3, :]
        gpre, gpost = g_ref[0:1, :], g_ref[1:2, :]
        r = lax.rsqrt(_rowmean(xv * xv) + EPS)
        xh = xv * r
        n = xh * gpre
        hb = (n * (1.0 + sc) + sh).astype(BF16)
        hb_ref[...] = hb
        ry = lax.rsqrt(_rowmean(yv * yv) + EPS)
        yh = yv * ry
        d_gt = _colsum(HALF * dxo_v * (yh * gpost))
        dp = (HALF * gt) * dxo_v
        d_gpost = _colsum(dp * yh)
        dyh = dp * gpost
        dy = ry * (dyh - yh * _rowmean(dyh * yh))
        dyb = dy.astype(BF16)
        dyb_ref[...] = dyb
        dh = jnp.zeros((tm, D), F32)
        for cidx in range(4):
            gate = gu_ref[cidx].astype(F32)
            up = gu_ref[4 + cidx].astype(F32)
            sig = _sigmoid(gate)
            s = gate * sig
            act_ref[cidx] = (s * up).astype(BF16)
            d_act = _dot_nt(dyb, wout_ref[cidx])
            d_up = (d_act * s).astype(BF16)
            d_gate = (d_act * up * (sig * (1.0 + gate * (1.0 - sig)))).astype(BF16)
            dg_ref[cidx] = d_gate
            dg_ref[4 + cidx] = d_up
            dh = dh + _dot(d_gate, win_ref[cidx]) + _dot(d_up, win_ref[4 + cidx])
        d_sc = _colsum(dh * n)
        d_sh = _colsum(dh)
        dn = dh * (1.0 + sc)
        d_gpre = _colsum(dn * xh)
        dxh = dn * gpre
        dx_ref[...] = dxo_v + r * (dxh - xh * _rowmean(dxh * xh))

        @pl.when(i % tps == 0)
        def _():
            mg_ref[...] = jnp.zeros((8, D), F32)

        @pl.when(i == 0)
        def _():
            vg_ref[...] = jnp.zeros((8, D), F32)

        mg_ref[0:1, :] += d_sh
        mg_ref[1:2, :] += d_sc
        mg_ref[2:3, :] += d_gt
        vg_ref[0:1, :] += d_gpre
        vg_ref[1:2, :] += d_gpost

    tile = pl.BlockSpec((tm, D), lambda i: (i, 0))
    return _call(
        core, name=name, grid=(nt,), jobs=jobs,
        in_specs=[tile, tile, tile, pl.BlockSpec((8, tm, FBP), lambda i: (0, i, 0)),
                  pl.BlockSpec((None, 8, D), lambda i: (i // tps, 0, 0)), _const_spec((8, D)),
                  _const_spec((8, FBP, D)), _const_spec((4, FBP, D))],
        out_specs=[tile, pl.BlockSpec((8, tm, FBP), lambda i: (0, i, 0)),
                   pl.BlockSpec((4, tm, FBP), lambda i: (0, i, 0)), tile, tile,
                   pl.BlockSpec((None, 8, D), lambda i: (i // tps, 0, 0)), pl.BlockSpec((8, D), lambda i: (0, 0))],
        out_shape=[jax.ShapeDtypeStruct((T, D), F32), jax.ShapeDtypeStruct((8, T, FBP), BF16),
                   jax.ShapeDtypeStruct((4, T, FBP), BF16), jax.ShapeDtypeStruct((T, D), BF16),
                   jax.ShapeDtypeStruct((T, D), BF16), jax.ShapeDtypeStruct((nb, 8, D), F32),
                   jax.ShapeDtypeStruct((8, D), F32)],
        args=[dxo, x, y, gu, mod, gvec, w_in, w_out])


def _ffn_last(x, target, mod, gvec, w_in, w_out, tm, name, jobs=()):
    T = x.shape[0]
    nt = T // tm
    nb = mod.shape[0]
    tps = nt // nb

    def core(ins, outs, scs):
        x_ref, t_ref, mod_ref, g_ref, wina_ref, winb_ref, wout_ref = ins
        dx_ref, dg_ref, act_ref, hb_ref, dyb_ref, mg_ref, vg_ref, loss_ref = outs
        hd2 = w_in[0].shape[2]
        (gu_s,) = scs
        i = pl.program_id(0)
        xv = x_ref[...]
        sh, sc, gt = mod_ref[0:1, :], mod_ref[1:2, :], mod_ref[2:3, :]
        gpre, gpost = g_ref[0:1, :], g_ref[1:2, :]
        r = lax.rsqrt(_rowmean(xv * xv) + EPS)
        xh = xv * r
        n = xh * gpre
        hb = (n * (1.0 + sc) + sh).astype(BF16)
        hb_ref[...] = hb
        hba, hbb = hb[:, 0:hd2], hb[:, hd2:D]
        yv = jnp.zeros((tm, D), F32)
        for cidx in range(4):
            gate = _dot_nt(hba, wina_ref[cidx]) + _dot_nt(hbb, winb_ref[cidx])
            up = _dot_nt(hba, wina_ref[4 + cidx]) + _dot_nt(hbb, winb_ref[4 + cidx])
            gu_s[cidx] = gate.astype(BF16)
            gu_s[4 + cidx] = up.astype(BF16)
            act = gate * _sigmoid(gate) * up
            act_ref[cidx] = act.astype(BF16)
            yv = yv + _dot(act_ref[cidx], wout_ref[cidx])
        ry = lax.rsqrt(_rowmean(yv * yv) + EPS)
        yh = yv * ry
        pn = yh * gpost
        err = xv + (HALF * gt) * pn - t_ref[...]
        dxo_v = err * (1.0 / D)
        d_gt = _colsum(HALF * dxo_v * pn)
        dp = (HALF * gt) * dxo_v
        d_gpost = _colsum(dp * yh)
        dyh = dp * gpost
        dyb = (ry * (dyh - yh * _rowmean(dyh * yh))).astype(BF16)
        dyb_ref[...] = dyb
        dha = jnp.zeros((tm, hd2), F32)
        dhb = jnp.zeros((tm, D - hd2), F32)
        for cidx in range(4):
            gate = gu_s[cidx].astype(F32)
            up = gu_s[4 + cidx].astype(F32)
            sig = _sigmoid(gate)
            s = gate * sig
            d_act = _dot_nt(dyb, wout_ref[cidx])
            d_up = (d_act * s).astype(BF16)
            d_gate = (d_act * up * (sig * (1.0 + gate * (1.0 - sig)))).astype(BF16)
            dg_ref[cidx] = d_gate
            dg_ref[4 + cidx] = d_up
            dha = dha + _dot(d_gate, wina_ref[cidx]) + _dot(d_up, wina_ref[4 + cidx])
            dhb = dhb + _dot(d_gate, winb_ref[cidx]) + _dot(d_up, winb_ref[4 + cidx])
        dh = jnp.concatenate([dha, dhb], axis=1)
        d_sc = _colsum(dh * n)
        d_sh = _colsum(dh)
        dn = dh * (1.0 + sc)
        d_gpre = _colsum(dn * xh)
        dxh = dn * gpre
        dx_ref[...] = dxo_v + r * (dxh - xh * _rowmean(dxh * xh))

        @pl.when(i % tps == 0)
        def _():
            mg_ref[...] = jnp.zeros((8, D), F32)

        @pl.when(i == 0)
        def _():
            vg_ref[...] = jnp.zeros((8, D), F32)
            loss_ref[...] = jnp.zeros((8, D), F32)

        mg_ref[0:1, :] += d_sh
        mg_ref[1:2, :] += d_sc
        mg_ref[2:3, :] += d_gt
        vg_ref[0:1, :] += d_gpre
        vg_ref[1:2, :] += d_gpost
        loss_ref[...] += HALF * jnp.sum(_rowmean(err * err), axis=0, keepdims=True)

    tile = pl.BlockSpec((tm, D), lambda i: (i, 0))
    return _call(
        core, name=name, grid=(nt,), jobs=jobs,
        in_specs=[tile, tile, pl.BlockSpec((None, 8, D), lambda i: (i // tps, 0, 0)), _const_spec((8, D)),
                  _const_spec(w_in[0].shape), _const_spec(w_in[1].shape), _const_spec((4, FBP, D))],
        out_specs=[tile, pl.BlockSpec((8, tm, FBP), lambda i: (0, i, 0)),
                   pl.BlockSpec((4, tm, FBP), lambda i: (0, i, 0)), tile, tile,
                   pl.BlockSpec((None, 8, D), lambda i: (i // tps, 0, 0)), pl.BlockSpec((8, D), lambda i: (0, 0)),
                   pl.BlockSpec((8, D), lambda i: (0, 0))],
        out_shape=[jax.ShapeDtypeStruct((T, D), F32), jax.ShapeDtypeStruct((8, T, FBP), BF16),
                   jax.ShapeDtypeStruct((4, T, FBP), BF16), jax.ShapeDtypeStruct((T, D), BF16),
                   jax.ShapeDtypeStruct((T, D), BF16), jax.ShapeDtypeStruct((nb, 8, D), F32),
                   jax.ShapeDtypeStruct((8, D), F32), jax.ShapeDtypeStruct((8, D), F32)],
        scratch=[pltpu.VMEM((8, tm, FBP), BF16)],
        args=[x, target, mod, gvec, w_in[0], w_in[1], w_out])


def _masked_spatial(ws_ref):
    row = lax.broadcasted_iota(jnp.int32, (CHUNK, CHUNK), 0)
    col = lax.broadcasted_iota(jnp.int32, (CHUNK, CHUNK), 1)
    keep = col <= row
    return [jnp.where(keep, ws_ref[hd], 0.0).astype(BF16) for hd in range(NHEAD)]


def _head_pairs(mats, right, transpose=False):
    first = lax.broadcasted_iota(jnp.int32, (CHUNK, 128), 1) < HD
    op = _dot_tn if transpose else _dot
    out = []
    for p in range(NHEAD // 2):
        slab = right[:, _lanes(p)]
        out.append(jnp.where(first, op(mats[2 * p], slab), op(mats[2 * p + 1], slab)))
    return jnp.concatenate(out, axis=1)


def _spatial_gate(wm, vb_chunk):
    return _head_pairs(wm, vb_chunk)


def _layer_norm_stats(v):
    mu = _rowmean(v)
    vc = v - mu
    rstd = lax.rsqrt(_rowmean(vc * vc) + EPS)
    return vc * rstd, rstd


def _pitch(tm):
    p = tm // 8
    while p % 8 != 4:
        p += 1
    return p


def _lanes(s):
    return slice(s * 128, (s + 1) * 128)


def _to_slabs(ref, row0, val):
    for s in range(4):
        ref[s, row0:row0 + val.shape[0], :] = val[:, _lanes(s)]


def _tap_sum(src, out, cw_ref, bias, tm, start):
    p = _pitch(tm)
    for s in range(4):
        accs = [jnp.broadcast_to(bias[:, _lanes(s)], (8, 128))] * p
        for k in range(CONV_K):
            w = jnp.broadcast_to(cw_ref[k:k + 1, _lanes(s)], (8, 128))
            for v in range(p):
                accs[v] = accs[v] + w * src[s, pl.ds(v + start(k), 8, stride=p), :]
        for v in range(p):
            out[s, pl.ds(v, 8, stride=p), :] = accs[v]
    return jnp.concatenate([out[s, 0:tm, :] for s in range(4)], axis=1)


def _mixer_fwd(x, mod, gvec, w_mi, w_mo, v512, ws, bias_full, cw, tm, name, jobs=()):
    T = x.shape[0]
    nt = T // tm
    tps = nt // mod.shape[0]
    ext_rows = 8 * _pitch(tm)

    def core(ins, outs, scs):
        x_ref, mod_ref, g_ref, wmi_ref, wmo_ref, v_ref, ws_ref, bias_ref, cw_ref = ins
        xo_ref, proj_ref, ym_ref, conv_ref = outs
        glu_ext, conv_scr = scs
        i = pl.program_id(0)
        xv = x_ref[...]
        sh, sc, gt = mod_ref[0:1, :], mod_ref[1:2, :], mod_ref[2:3, :]
        r = lax.rsqrt(_rowmean(xv * xv) + EPS)
        hb = ((xv * r * g_ref[0:1, :]) * (1.0 + sc) + sh).astype(BF16)
        for j in range(NDEV):
            proj_ref[:, j * MB:(j + 1) * MB] = _dot(hb, wmi_ref[j])
        u = proj_ref[:, 0:WA]
        v0 = proj_ref[:, WA:2 * WA]
        a = proj_ref[:, 2 * WA:3 * WA]
        g = proj_ref[:, 3 * WA:4 * WA]
        vh, _ = _layer_norm_stats(v0)
        vb = (vh * v_ref[0:1, :] + v_ref[1:2, :]).astype(BF16)
        wm = _masked_spatial(ws_ref)
        ya = []
        for q in range(tm // CHUNK):
            z = _spatial_gate(wm, vb[q * CHUNK:(q + 1) * CHUNK, :]) + bias_ref[...]
            ya.append(u[q * CHUNK:(q + 1) * CHUNK, :] * z)
        ya = jnp.concatenate(ya, axis=0)
        glu = a * _sigmoid(g)

        @pl.when(i == 0)
        def _():
            glu_ext[:, HALO + tm:HALO + ext_rows, :] = jnp.zeros((4, ext_rows - tm, 128), F32)

        @pl.when(i % tps == 0)
        def _():
            glu_ext[:, 0:HALO, :] = jnp.zeros((4, HALO, 128), F32)

        _to_slabs(glu_ext, HALO, glu)
        conv = _tap_sum(glu_ext, conv_scr, cw_ref, v_ref[2:3, :], tm, lambda k: HALO - (CONV_K - 1) + k)
        conv_ref[...] = conv
        glu_ext[:, 0:HALO, :] = glu_ext[:, tm:tm + HALO, :]
        ch, _ = _layer_norm_stats(conv)
        cn = ch * v_ref[3:4, :] + v_ref[4:5, :]
        yb = cn * _sigmoid(cn)
        pa = ya * lax.rsqrt(_rowmean(ya * ya) + EPS) * v_ref[5:6, :]
        pb = yb * lax.rsqrt(_rowmean(yb * yb) + EPS) * v_ref[6:7, :]
        ycat = jnp.concatenate([pa, pb], axis=1).astype(BF16)
        ym = _dot(ycat, wmo_ref[...])
        ym_ref[...] = ym
        rm = lax.rsqrt(_rowmean(ym * ym) + EPS)
        xo_ref[...] = xv + gt * (ym * rm * g_ref[1:2, :])

    tile = pl.BlockSpec((tm, D), lambda i: (i, 0))
    return _call(
        core, name=name, grid=(nt,), jobs=jobs,
        in_specs=[tile, pl.BlockSpec((None, 8, D), lambda i: (i // tps, 0, 0)), _const_spec((8, D)),
                  _const_spec((NDEV, D, MB)), _const_spec((D, D)), _const_spec((8, WA)),
                  _const_spec((NHEAD, CHUNK, CHUNK)), _const_spec((CHUNK, WA)), _const_spec((32, WA))],
        out_specs=[tile, pl.BlockSpec((tm, 4 * WA), lambda i: (i, 0)), tile, pl.BlockSpec((tm, WA), lambda i: (i, 0))],
        out_shape=[jax.ShapeDtypeStruct((T, D), F32), jax.ShapeDtypeStruct((T, 4 * WA), F32),
                   jax.ShapeDtypeStruct((T, D), F32), jax.ShapeDtypeStruct((T, WA), F32)],
        scratch=[pltpu.VMEM((4, HALO + ext_rows, 128), F32), pltpu.VMEM((4, ext_rows, 128), F32)],
        args=[x, mod, gvec, w_mi, w_mo, v512, ws, bias_full, cw])


def _mixer_bwd_a(dxo, ym, proj, conv, mod, gvec, w_mo, v512, ws, bias_full, esel, tm, name, jobs=()):
    T = dxo.shape[0]
    nt = T // tm
    nb = mod.shape[0]
    tps = nt // nb

    def core(ins, outs, scs):
        dxo_ref, ym_ref, proj_ref, conv_ref, mod_ref, g_ref, wmo_ref, v_ref, ws_ref, bias_ref, e_ref = ins
        dpart_ref, dymb_ref, ycat_ref, mg_ref, vg_ref, v5g_ref, gws_ref, gbs_ref = outs
        (dbs_acc,) = scs
        i = pl.program_id(0)
        dxo_v = dxo_ref[...]
        ymv = ym_ref[...]
        gt = mod_ref[2:3, :]
        gpost = g_ref[1:2, :]
        rm = lax.rsqrt(_rowmean(ymv * ymv) + EPS)
        ymh = ymv * rm
        d_gt = _colsum(dxo_v * (ymh * gpost))
        dpm = gt * dxo_v
        d_gpost = _colsum(dpm * ymh)
        dymh = dpm * gpost
        dym = (rm * (dymh - ymh * _rowmean(dymh * ymh))).astype(BF16)
        dymb_ref[...] = dym
        dycat = _dot_nt(dym, wmo_ref[...])
        u = proj_ref[:, 0:WA]
        v0 = proj_ref[:, WA:2 * WA]
        vh, rv = _layer_norm_stats(v0)
        vb = (vh * v_ref[0:1, :] + v_ref[1:2, :]).astype(BF16)
        wm = _masked_spatial(ws_ref)
        zs = []
        for q in range(tm // CHUNK):
            zs.append(_spatial_gate(wm, vb[q * CHUNK:(q + 1) * CHUNK, :]) + bias_ref[...])
        z = jnp.concatenate(zs, axis=0)
        ya = u * z
        ra = lax.rsqrt(_rowmean(ya * ya) + EPS)
        yah = ya * ra
        ch, rc = _layer_norm_stats(conv_ref[...])
        cn = ch * v_ref[3:4, :] + v_ref[4:5, :]
        sg = _sigmoid(cn)
        yb = cn * sg
        rb = lax.rsqrt(_rowmean(yb * yb) + EPS)
        ybh = yb * rb
        ycat_ref[...] = jnp.concatenate([yah * v_ref[5:6, :], ybh * v_ref[6:7, :]], axis=1).astype(BF16)
        dpa = dycat[:, 0:WA]
        dpb = dycat[:, WA:2 * WA]
        d_goa = _colsum(dpa * yah)
        d_gob = _colsum(dpb * ybh)
        dyah = dpa * v_ref[5:6, :]
        dybh = dpb * v_ref[6:7, :]
        dya = ra * (dyah - yah * _rowmean(dyah * yah))
        dyb = rb * (dybh - ybh * _rowmean(dybh * ybh))
        dpart_ref[:, 0:WA] = dya * z
        dz = dya * u

        @pl.when(i == 0)
        def _():
            gws_ref[...] = jnp.zeros((NHEAD, CHUNK, CHUNK), F32)
            dbs_acc[...] = jnp.zeros((CHUNK, WA), F32)
            vg_ref[...] = jnp.zeros((8, D), F32)
            v5g_ref[...] = jnp.zeros((8, WA), F32)

        first = lax.broadcasted_iota(jnp.int32, (CHUNK, 128), 1) < HD
        dvs = []
        for q in range(tm // CHUNK):
            dz_q = dz[q * CHUNK:(q + 1) * CHUNK, :]
            vb_q = vb[q * CHUNK:(q + 1) * CHUNK, :]
            dbs_acc[...] += dz_q
            dzb = dz_q.astype(BF16)
            dvs.append(_head_pairs(wm, dzb, transpose=True))
            for hd in range(NHEAD):
                slab = dzb[:, _lanes(hd // 2)]
                dz_hd = jnp.where(first if hd % 2 == 0 else jnp.logical_not(first), slab, jnp.zeros_like(slab))
                gws_ref[hd] += _dot_nt(dz_hd, vb_q[:, _lanes(hd // 2)])
        dv = jnp.concatenate(dvs, axis=0)
        d_gng = _colsum(dv * vh)
        d_gnb = _colsum(dv)
        dvh = dv * v_ref[0:1, :]
        dpart_ref[:, WA:2 * WA] = rv * (dvh - _rowmean(dvh) - vh * _rowmean(dvh * vh))
        dcn = dyb * (sg * (1.0 + cn * (1.0 - sg)))
        d_cng = _colsum(dcn * ch)
        d_cnb = _colsum(dcn)
        dch = dcn * v_ref[3:4, :]
        dconv = rc * (dch - _rowmean(dch) - ch * _rowmean(dch * ch))
        dpart_ref[:, 2 * WA:3 * WA] = dconv
        dpart_ref[:, 3 * WA:4 * WA] = jnp.zeros((tm, WA), F32)
        d_cb = _colsum(dconv)

        @pl.when(i % tps == 0)
        def _():
            mg_ref[...] = jnp.zeros((8, D), F32)

        mg_ref[2:3, :] += d_gt
        vg_ref[1:2, :] += d_gpost
        v5g_ref[0:1, :] += d_gng
        v5g_ref[1:2, :] += d_gnb
        v5g_ref[2:3, :] += d_cb
        v5g_ref[3:4, :] += d_cng
        v5g_ref[4:5, :] += d_cnb
        v5g_ref[5:6, :] += d_goa
        v5g_ref[6:7, :] += d_gob

        @pl.when(i == nt - 1)
        def _():
            row = lax.broadcasted_iota(jnp.int32, (CHUNK, CHUNK), 0)
            col = lax.broadcasted_iota(jnp.int32, (CHUNK, CHUNK), 1)
            for hd in range(NHEAD):
                gws_ref[hd] = jnp.where(col <= row, gws_ref[hd], 0.0)
            gbs_ref[...] = lax.dot_general(e_ref[...], dbs_acc[...], (((1,), (1,)), ((), ())),
                                           precision=lax.Precision.HIGHEST, preferred_element_type=F32)

    tile = pl.BlockSpec((tm, D), lambda i: (i, 0))
    ptile = pl.BlockSpec((tm, 4 * WA), lambda i: (i, 0))
    return _call(
        core, name=name, grid=(nt,), jobs=jobs,
        in_specs=[tile, tile, pl.BlockSpec((tm, 2 * WA), lambda i: (i, 0)), pl.BlockSpec((tm, WA), lambda i: (i, 0)),
                  pl.BlockSpec((None, 8, D), lambda i: (i // tps, 0, 0)), _const_spec((8, D)), _const_spec((D, D)),
                  _const_spec((8, WA)), _const_spec((NHEAD, CHUNK, CHUNK)), _const_spec((CHUNK, WA)),
                  _const_spec((8, WA))],
        out_specs=[ptile, tile, tile, pl.BlockSpec((None, 8, D), lambda i: (i // tps, 0, 0)),
                   pl.BlockSpec((8, D), lambda i: (0, 0)), pl.BlockSpec((8, WA), lambda i: (0, 0)),
                   pl.BlockSpec((NHEAD, CHUNK, CHUNK), lambda i: (0, 0, 0)), pl.BlockSpec((8, CHUNK), lambda i: (0, 0))],
        out_shape=[jax.ShapeDtypeStruct((T, 4 * WA), F32), jax.ShapeDtypeStruct((T, D), BF16),
                   jax.ShapeDtypeStruct((T, D), BF16), jax.ShapeDtypeStruct((nb, 8, D), F32),
                   jax.ShapeDtypeStruct((8, D), F32), jax.ShapeDtypeStruct((8, WA), F32),
                   jax.ShapeDtypeStruct((NHEAD, CHUNK, CHUNK), F32), jax.ShapeDtypeStruct((8, CHUNK), F32)],
        scratch=[pltpu.VMEM((CHUNK, WA), F32)],
        args=[dxo, ym, proj, conv, mod, gvec, w_mo, v512, ws, bias_full, esel])


def _mixer_bwd_b(dxo, x, dpart, proj, mod, gvec, w_mi, cw, tm, name, jobs=()):
    T = x.shape[0]
    nt = T // tm
    nb = mod.shape[0]
    tps = nt // nb
    hpt = tm // HALO
    nh = T // HALO
    off = HALO - (CONV_K - 1)
    p = _pitch(tm)
    ext_rows = 8 * p

    def core(ins, outs, scs):
        dxo_ref, x_ref, dpart_ref, dnext_ref, ag_ref, halo_ref, mod_ref, g_ref, wmi_ref, cw_ref = ins
        dx_ref, dproj_ref, hb_ref, mg_ref, vg_ref, dcw_ref = outs
        glu_ext, dconv_ext, dglu_scr, dcw_acc = scs
        i = pl.program_id(0)
        first = i % tps == 0
        last = i % tps == tps - 1
        a = ag_ref[:, 0:WA]
        g = ag_ref[:, WA:2 * WA]
        sgg = _sigmoid(g)

        @pl.when(i == 0)
        def _():
            glu_ext[:, HALO + tm:HALO + ext_rows, :] = jnp.zeros((4, ext_rows - tm, 128), F32)
            dconv_ext[:, HALO + tm:HALO + ext_rows, :] = jnp.zeros((4, ext_rows - tm, 128), F32)
            dcw_acc[...] = jnp.zeros((32, 8, WA), F32)
            vg_ref[...] = jnp.zeros((8, D), F32)

        _to_slabs(glu_ext, 0, jnp.where(first, 0.0, halo_ref[:, 0:WA] * _sigmoid(halo_ref[:, WA:2 * WA])))
        _to_slabs(glu_ext, HALO, a * sgg)
        _to_slabs(dconv_ext, 0, dpart_ref[:, 2 * WA:3 * WA])
        _to_slabs(dconv_ext, tm, jnp.where(last, 0.0, dnext_ref[...]))
        sub = lax.broadcasted_iota(jnp.int32, (8, 128), 0)
        for s in range(4):
            accs = [jnp.zeros((8, 128), F32)] * CONV_K
            for v in range(p):
                dc = jnp.where(v + p * sub < tm, dconv_ext[s, pl.ds(v, 8, stride=p), :], 0.0)
                for k in range(CONV_K):
                    accs[k] = accs[k] + dc * glu_ext[s, pl.ds(v + off + k, 8, stride=p), :]
            for k in range(CONV_K):
                dcw_acc[k, :, _lanes(s)] += accs[k]
        dglu = _tap_sum(dconv_ext, dglu_scr, cw_ref, jnp.zeros((1, WA), F32), tm, lambda k: (CONV_K - 1) - k)

        @pl.when(i == nt - 1)
        def _():
            for k in range(CONV_K):
                dcw_ref[k:k + 1, :] = jnp.sum(dcw_acc[k], axis=0, keepdims=True)
            dcw_ref[CONV_K:32, :] = jnp.zeros((32 - CONV_K, WA), F32)

        da = dglu * sgg
        dgg = dglu * a * (sgg * (1.0 - sgg))
        dproj_ref[:, 0:2 * WA] = dpart_ref[:, 0:2 * WA].astype(BF16)
        dproj_ref[:, 2 * WA:3 * WA] = da.astype(BF16)
        dproj_ref[:, 3 * WA:4 * WA] = dgg.astype(BF16)
        dh = jnp.zeros((tm, D), F32)
        for j in range(NDEV):
            dh = dh + _dot_nt(dproj_ref[:, j * MB:(j + 1) * MB], wmi_ref[j])
        xv = x_ref[...]
        sc, sh = mod_ref[1:2, :], mod_ref[0:1, :]
        gpre = g_ref[0:1, :]
        r = lax.rsqrt(_rowmean(xv * xv) + EPS)
        xh = xv * r
        n = xh * gpre
        hb_ref[...] = (n * (1.0 + sc) + sh).astype(BF16)
        d_sc = _colsum(dh * n)
        d_sh = _colsum(dh)
        dn = dh * (1.0 + sc)
        d_gpre = _colsum(dn * xh)
        dxh = dn * gpre
        dx_ref[...] = dxo_ref[...] + r * (dxh - xh * _rowmean(dxh * xh))

        @pl.when(first)
        def _():
            mg_ref[...] = jnp.zeros((8, D), F32)

        mg_ref[0:1, :] += d_sh
        mg_ref[1:2, :] += d_sc
        vg_ref[0:1, :] += d_gpre

    tile = pl.BlockSpec((tm, D), lambda i: (i, 0))
    return _call(
        core, name=name, grid=(nt,), jobs=jobs,
        in_specs=[tile, tile, pl.BlockSpec((tm, 4 * WA), lambda i: (i, 0)),
                  pl.BlockSpec((HALO, WA), lambda i: (jnp.minimum((i + 1) * hpt, nh - 1), 2)),
                  pl.BlockSpec((tm, 2 * WA), lambda i: (i, 1)),
                  pl.BlockSpec((HALO, 2 * WA), lambda i: (jnp.maximum(i * hpt - 1, 0), 1)),
                  pl.BlockSpec((None, 8, D), lambda i: (i // tps, 0, 0)), _const_spec((8, D)),
                  _const_spec((NDEV, D, MB)), _const_spec((32, WA))],
        out_specs=[tile, pl.BlockSpec((tm, 4 * WA), lambda i: (i, 0)), tile,
                   pl.BlockSpec((None, 8, D), lambda i: (i // tps, 0, 0)), pl.BlockSpec((8, D), lambda i: (0, 0)),
                   pl.BlockSpec((32, WA), lambda i: (0, 0))],
        out_shape=[jax.ShapeDtypeStruct((T, D), F32), jax.ShapeDtypeStruct((T, 4 * WA), BF16),
                   jax.ShapeDtypeStruct((T, D), BF16), jax.ShapeDtypeStruct((nb, 8, D), F32),
                   jax.ShapeDtypeStruct((8, D), F32), jax.ShapeDtypeStruct((32, WA), F32)],
        scratch=[pltpu.VMEM((4, HALO + ext_rows, 128), F32), pltpu.VMEM((4, HALO + ext_rows, 128), F32),
                 pltpu.VMEM((4, ext_rows, 128), F32), pltpu.VMEM((32, 8, WA), F32)],
        args=[dxo, x, dpart, dpart, proj, proj, mod, gvec, w_mi, cw])


def _grad_chip(a, b, a_spec, b_spec, prod_shape, half, name, jobs=(), via_b=False):
    steps = 8 if half is None else 4
    R = prod_shape[0] if half is None else half
    C = prod_shape[1]

    def core(ins, outs, scs):
        a_ref, b_ref = ins
        (o_ref,) = outs
        own, snd, rcv, ssem, rsem, lsem = scs
        s = pl.program_id(0)
        c = lax.axis_index("c")
        me = _me()
        sib = _flip(me, (0, 0, 1))
        if via_b:
            prod = _dot_tn(b_ref[...], a_ref[...]).T.astype(BF16)
        else:
            prod = _dot_tn(a_ref[...], b_ref[...]).astype(BF16)
        if half is None:
            q = s // 2

            @pl.when(s % 2 == c)
            def _():
                own[q] = prod

            @pl.when(s % 2 != c)
            def _():
                snd[q] = prod
                _remote(snd.at[q], rcv.at[q], ssem.at[q], rsem.at[q], sib).start()
        else:
            lo = prod[0:half, :]
            hi = prod[half:2 * half, :]
            own[s] = jnp.where(c == 0, lo, hi)
            snd[s] = jnp.where(c == 0, hi, lo)
            _remote(snd.at[s], rcv.at[s], ssem.at[s], rsem.at[s], sib).start()

        @pl.when(s == steps - 1)
        def _():
            for q4 in range(4):
                cp = _remote(snd.at[q4], rcv.at[q4], ssem.at[q4], rsem.at[q4], sib)
                cp.wait_recv()
                cp.wait_send()
                snd[q4] = (own[q4].astype(F32) + rcv[q4].astype(F32)).astype(BF16)
            out = pltpu.make_async_copy(snd, o_ref, lsem)
            out.start()
            out.wait()

    return _call(
        core, name=name, grid=(steps,), jobs=jobs, in_specs=[a_spec, b_spec], out_specs=[HBM],
        out_shape=[jax.ShapeDtypeStruct((4, R, C), BF16)],
        scratch=[pltpu.VMEM((4, R, C), BF16), pltpu.VMEM((4, R, C), BF16), pltpu.VMEM((4, R, C), BF16),
                 pltpu.SemaphoreType.DMA((4,)), pltpu.SemaphoreType.DMA((4,)), pltpu.SemaphoreType.DMA],
        args=[a, b])


def _grad_w_in(dg, hb, name, jobs=()):
    T = hb.shape[0]
    return _grad_chip(dg, hb, pl.BlockSpec((None, T, FBP), lambda s: (s, 0, 0)), _const_spec((T, D)),
                      (FBP, D), None, name, jobs)


def _grad_w_out(act, dyb, name, jobs=()):
    T = dyb.shape[0]
    return _grad_chip(act, dyb, pl.BlockSpec((None, T, FBP), lambda s: (s, 0, 0)), _const_spec((T, D)),
                      (FBP, D), FO, name, jobs)


def _grad_w_mi(hb, dproj, name, jobs=()):
    T = hb.shape[0]
    return _grad_chip(hb, dproj, _const_spec((T, D)), pl.BlockSpec((T, MB), lambda s: (0, s)),
                      (D, MB), None, name, jobs, via_b=True)


def _grad_w_mo(ycat, dym, name, jobs=()):
    T = ycat.shape[0]
    return _grad_chip(ycat, dym, pl.BlockSpec((T, 2 * MO), lambda s: (0, s)), _const_spec((T, D)),
                      (2 * MO, D), MO, name, jobs)


def _adamw_math(w, g, m, v):
    m2 = ADAM_B1 * m + (1.0 - ADAM_B1) * g
    v2 = ADAM_B2 * v + (1.0 - ADAM_B2) * (g * g)
    m_hat = m2 / (1.0 - ADAM_B1 ** ADAM_STEP)
    v_hat = v2 / (1.0 - ADAM_B2 ** ADAM_STEP)
    delta = -ADAM_LR * (m_hat / (jnp.sqrt(v_hat) + ADAM_EPS) + ADAM_WD * w)
    return delta, m2, v2


def _adamw_reduce(parts, w, m, v, tr, name, own=None, after=None):
    R, C = w.shape

    def core(ins, outs, _):
        p_ref, w_ref, m_ref, v_ref = ins[:4]
        g_ref, d_ref, m2_ref, v2_ref = outs
        if own is None:
            terms = [p_ref[s].astype(F32) for s in range(4)]
        else:
            mq = 2 * lax.axis_index("x") + lax.axis_index("y")
            mine = ins[4][...].astype(F32)
            terms = [jnp.where(mq == s, mine, p_ref[s].astype(F32)) for s in range(4)]
        g = terms[0]
        for s in range(1, 4):
            g = g + terms[s]
        g_ref[...] = g
        d_ref[...], m2_ref[...], v2_ref[...] = _adamw_math(w_ref[...], g, m_ref[...], v_ref[...])

    blk = pl.BlockSpec((tr, C), lambda i: (i, 0))
    in_specs = [pl.BlockSpec((4, tr, C), lambda i: (0, i, 0)), blk, blk, blk]
    args = [parts, w, m, v]
    if own is not None:
        mq = 2 * lax.axis_index("x") + lax.axis_index("y")
        in_specs.append(pl.BlockSpec((tr, C), lambda i: (i, 0)))
        args.append(lax.dynamic_index_in_dim(own, mq, 0, keepdims=False))
    if after is not None:
        in_specs.append(HBM)
        args.append(after)
    return _call(
        core, name=name, grid=(R // tr,), in_specs=in_specs,
        out_specs=[blk, blk, blk, blk], out_shape=[jax.ShapeDtypeStruct((R, C), F32)] * 4, args=args)[0]


HBM_ONLY = pl.BlockSpec(memory_space=pltpu.HBM)
SEM = pl.BlockSpec(memory_space=pltpu.SEMAPHORE)
EFFECT = pltpu.SideEffectType.DATAFLOW_SIDE_EFFECTING


def _chip_scatter_start(gs, name):
    n = len(gs)

    def body(*refs):
        g_refs, land_refs = refs[:n], refs[n:2 * n]
        ssem, rsem = refs[2 * n:2 * n + 2]
        token = refs[-1]
        me = _me()
        mq = 2 * me[0] + me[1]
        for k, f in enumerate(CHIP_FLIPS):
            p = _flip(me, f)
            for a in range(n):
                _remote(g_refs[a].at[2 * p[0] + p[1]], land_refs[a].at[mq], ssem.at[3 * a + k], rsem.at[3 * a + k], p).start()
        token[...] = jnp.zeros_like(token)

    gs = [pltpu.with_memory_space_constraint(g, pltpu.HBM) for g in gs]
    lands = [pltpu.with_memory_space_constraint(lax.empty(g.shape, g.dtype), pltpu.HBM) for g in gs]
    res = pl.pallas_call(
        body, name=name,
        out_shape=(pltpu.SemaphoreType.DMA((3 * n,)), pltpu.SemaphoreType.DMA((3 * n,)))
        + tuple(pltpu.HBM(g.shape, g.dtype) for g in gs) * 2 + (jax.ShapeDtypeStruct((8, 128), F32),),
        in_specs=(HBM_ONLY,) * (2 * n), out_specs=(SEM, SEM) + (HBM_ONLY,) * (2 * n) + (VM,),
        input_output_aliases={a: 2 + a for a in range(2 * n)},
        compiler_params=pltpu.CompilerParams(has_side_effects=EFFECT),
    )(*gs, *lands)
    return res[:-1], res[-1]


def _chip_scatter_wait(handle, after, name):
    ssem, rsem = handle[:2]
    n = (len(handle) - 2) // 2
    thru = handle[2:]

    def body(*refs):
        g_refs, land_refs = refs[:n], refs[n:2 * n]
        ssem, rsem = refs[2 * n:2 * n + 2]
        me = _me()
        mq = 2 * me[0] + me[1]
        for k, f in enumerate(CHIP_FLIPS):
            p = _flip(me, f)
            pq = 2 * p[0] + p[1]
            for a in range(n):
                _remote(g_refs[a].at[pq], land_refs[a].at[mq], ssem.at[3 * a + k], rsem.at[3 * a + k], p).wait_send()
                _remote(g_refs[a].at[mq], land_refs[a].at[pq], ssem.at[3 * a + k], rsem.at[3 * a + k], p).wait_recv()

    res = pl.pallas_call(
        body, name=name,
        out_shape=tuple(pltpu.HBM(t.shape, t.dtype) for t in thru),
        in_specs=(HBM_ONLY,) * (2 * n) + (SEM, SEM, HBM), out_specs=(HBM_ONLY,) * (2 * n),
        input_output_aliases={a: a for a in range(2 * n)},
        compiler_params=pltpu.CompilerParams(has_side_effects=EFFECT),
    )(*thru, ssem, rsem, after)
    return list(res[:n]), list(res[n:])


def _adamw_ada(sc_all, dd, w, m, v, tr, name, after=None):
    R, C = w.shape

    def core(ins, outs, _):
        sc_ref, dd_ref, w_ref, m_ref, v_ref = ins[:5]
        g_ref, d_ref, m2_ref, v2_ref = outs
        g = _dot_tn(sc_ref[...].astype(BF16), dd_ref[...].astype(BF16))
        g_ref[...] = g
        d_ref[...], m2_ref[...], v2_ref[...] = _adamw_math(w_ref[...], g, m_ref[...], v_ref[...])

    blk = pl.BlockSpec((tr, C), lambda i: (i, 0))
    return _call(
        core, name=name, grid=(R // tr,),
        in_specs=[pl.BlockSpec((64, tr), lambda i: (0, i)), pl.BlockSpec((64, C), lambda i: (0, 0)), blk, blk, blk]
        + [HBM] * (after is not None),
        out_specs=[blk, blk, blk, blk], out_shape=[jax.ShapeDtypeStruct((R, C), F32)] * 4,
        args=[sc_all, dd, w, m, v] + [after] * (after is not None))[0]


def _adamw_small(gathered, plain, grads, wmv, emit, name):
    nw = len(grads)
    ng, npl, ne = len(gathered), len(plain), len(emit)

    def core(ins, outs, _):
        srcs = []
        for a in range(ng):
            s = ins[a][0]
            for dev in range(1, NDEV):
                s = s + ins[a][dev]
            srcs.append(s)
        srcs += [ins[ng + a][...] for a in range(npl)]
        w_refs = ins[ng + npl:]
        for e, a in enumerate(emit):
            outs[e][...] = srcs[a]
        for t in range(nw):
            src, row = grads[t]
            g = srcs[src] if row is None else srcs[src][row:row + 1, :]
            w_ref, m_ref, v_ref = w_refs[3 * t:3 * t + 3]
            g_ref, d_ref, m2_ref, v2_ref = outs[ne + 4 * t:ne + 4 * t + 4]
            g_ref[...] = g
            d_ref[...], m2_ref[...], v2_ref[...] = _adamw_math(w_ref[...], g, m_ref[...], v_ref[...])

    out_shape = [jax.ShapeDtypeStruct(gathered[a].shape[1:], F32) for a in emit]
    for t in range(nw):
        out_shape += [jax.ShapeDtypeStruct(wmv[3 * t].shape, F32)] * 4
    return _call(
        core, name=name, grid=(), in_specs=[VM] * (ng + npl + 3 * nw), out_specs=[VM] * (ne + 4 * nw),
        out_shape=out_shape, args=list(gathered) + list(plain) + list(wmv))[0]


def _ada_fwd(c_pad, w_ada, b_cols, cw_pad, jobs=()):
    def core(ins, outs, scs):
        c_ref, w_ref, b_ref, cwp_ref = ins
        ada_ref, sc_ref, cw_ref = outs
        cbuf, send_buf, ssem, rsem = scs
        me = _me()
        mi = _lin(me)
        cbuf[mi] = c_ref[...]
        cw_ref[mi] = cwp_ref[...]
        peers = [_flip(me, f) for f in FLIPS]
        first = []
        for k, p in enumerate(peers):
            first.append(_remote(cbuf.at[mi], cbuf.at[mi], ssem.at[k], rsem.at[k], p))
            first.append(_remote(cw_ref.at[mi], cw_ref.at[mi], ssem.at[7 + k], rsem.at[7 + k], p))
        for cp in first:
            cp.start()
        for k, p in enumerate(peers):
            pi = _lin(p)
            _remote(cbuf.at[pi], cbuf.at[pi], ssem.at[k], rsem.at[k], p).wait_recv()
            _remote(cw_ref.at[pi], cw_ref.at[pi], ssem.at[7 + k], rsem.at[7 + k], p).wait_recv()
        c_all = cbuf[...].reshape(8 * 8, D)
        sc = c_all * _sigmoid(c_all)
        sc_ref[...] = sc
        res = _dot(sc.astype(BF16), w_ref[...].astype(BF16)) + b_ref[...]
        send_buf[...] = res.reshape(8, 8, ADA_B)
        ada_ref[mi] = send_buf[mi]
        second = []
        for k, p in enumerate(peers):
            second.append(_remote(send_buf.at[_lin(p)], ada_ref.at[mi], ssem.at[14 + k], rsem.at[14 + k], p))
        for cp in second:
            cp.start()
        for k, p in enumerate(peers):
            _remote(send_buf.at[mi], ada_ref.at[_lin(p)], ssem.at[14 + k], rsem.at[14 + k], p).wait_recv()
        for cp in first + second:
            cp.wait_send()

    return _call(
        core, name="ada_fwd", grid=(), jobs=jobs, in_specs=[VM, VM, VM, VM], out_specs=[VM, VM, VM],
        out_shape=[jax.ShapeDtypeStruct((8, 8, ADA_B), F32), jax.ShapeDtypeStruct((64, D), F32),
                   jax.ShapeDtypeStruct((8, 32, 64), F32)],
        scratch=[pltpu.VMEM((8, 8, D), F32), pltpu.VMEM((8, 8, ADA_B), F32),
                 pltpu.SemaphoreType.DMA((21,)), pltpu.SemaphoreType.DMA((21,))],
        args=[c_pad, w_ada, b_cols, cw_pad])


def _ada_bwd(dada, jobs=()):
    def core(ins, outs, scs):
        (d_ref,) = ins
        dd_ref, gb_ref = outs
        rbuf, ssem, rsem = scs
        me = _me()
        mi = _lin(me)
        peers = [_flip(me, f) for f in FLIPS]
        rbuf[mi] = d_ref[mi]
        first = []
        for k, p in enumerate(peers):
            first.append(_remote(d_ref.at[_lin(p)], rbuf.at[mi], ssem.at[k], rsem.at[k], p))
        for cp in first:
            cp.start()
        for k, p in enumerate(peers):
            _remote(d_ref.at[mi], rbuf.at[_lin(p)], ssem.at[k], rsem.at[k], p).wait_recv()
        dd = rbuf[...].reshape(64, ADA_B)
        dd_ref[...] = dd
        gb_ref[mi] = jnp.broadcast_to(_colsum(dd), (8, ADA_B))
        second = []
        for k, p in enumerate(peers):
            second.append(_remote(gb_ref.at[mi], gb_ref.at[mi], ssem.at[7 + k], rsem.at[7 + k], p))
        for cp in second:
            cp.start()
        for k, p in enumerate(peers):
            pi = _lin(p)
            _remote(gb_ref.at[pi], gb_ref.at[pi], ssem.at[7 + k], rsem.at[7 + k], p).wait_recv()
        for cp in first + second:
            cp.wait_send()

    return _call(
        core, name="ada_bwd", grid=(), jobs=jobs, in_specs=[VM], out_specs=[VM, VM],
        out_shape=[jax.ShapeDtypeStruct((64, ADA_B), F32), jax.ShapeDtypeStruct((8, 8, ADA_B), F32)],
        scratch=[pltpu.VMEM((8, 8, ADA_B), F32), pltpu.SemaphoreType.DMA((14,)), pltpu.SemaphoreType.DMA((14,))],
        args=[dada])


SMALL_D = ("g_pre_f1", "g_post_f1", "g_pre_m", "g_post_m", "g_pre_f2", "g_post_f2")
SMALL_W = ("gmlp_norm_g", "gmlp_norm_b", "conv_b", "conv_norm_g", "conv_norm_b", "g_out_a", "g_out_b")


def kernel(x, c, w_ada, b_ada, g_pre_f1, g_post_f1, w_f1_in, w_f1_out, g_pre_m, g_post_m, w_mix_in, gmlp_norm_g, gmlp_norm_b, w_spatial, b_spatial, conv_w, conv_b, conv_norm_g, conv_norm_b, g_out_a, g_out_b, w_mix_out, g_pre_f2, g_post_f2, w_f2_in, w_f2_out, loss_target, m_w_ada, m_b_ada, m_g_pre_f1, m_g_post_f1, m_w_f1_in, m_w_f1_out, m_g_pre_m, m_g_post_m, m_w_mix_in, m_gmlp_norm_g, m_gmlp_norm_b, m_w_spatial, m_b_spatial, m_conv_w, m_conv_b, m_conv_norm_g, m_conv_norm_b, m_g_out_a, m_g_out_b, m_w_mix_out, m_g_pre_f2, m_g_post_f2, m_w_f2_in, m_w_f2_out, v_w_ada, v_b_ada, v_g_pre_f1, v_g_post_f1, v_w_f1_in, v_w_f1_out, v_g_pre_m, v_g_post_m, v_w_mix_in, v_gmlp_norm_g, v_gmlp_norm_b, v_w_spatial, v_b_spatial, v_conv_w, v_conv_b, v_conv_norm_g, v_conv_norm_b, v_g_out_a, v_g_out_b, v_w_mix_out, v_g_pre_f2, v_g_post_f2, v_w_f2_in, v_w_f2_out):
    given = dict(locals())
    bl, seq, _ = x.shape
    T = bl * seq
    tm = min(256, seq // 2)
    mi = _lin((lax.axis_index("x"), lax.axis_index("y"), lax.axis_index("c")))

    def shard_in(w):
        return jnp.pad(w[0].T.astype(BF16), ((0, FBP - FB), (0, 0)))

    zpad = jnp.zeros((max(FBP - FB, 16), D), BF16)
    g_f1 = _Gather([shard_in(w_f1_in), w_f1_out[0].astype(BF16)], ("rows", "out"), zpad)
    s_f2 = shard_in(w_f2_in)
    g_mx = _Gather([w_mix_in[0].astype(BF16), w_mix_out[0].astype(BF16), w_f2_out[0].astype(BF16), s_f2[:, 0:D // 4]],
                   ("rows", "rows", "out", "rows"), zpad, late_mid=True)
    g_f2 = _Gather([s_f2[:, D // 4:D]], ("rows",), zpad)

    c_pad = jnp.pad(c, ((0, 8 - bl), (0, 0)))
    b_cols = lax.dynamic_slice(b_ada, (0, mi * ADA_B), (1, ADA_B))
    cw_pad = jnp.pad(conv_w[0], ((0, 1), (0, 0)))
    (ada_blk, sc_all, cw_all), ((wi1, wo1),) = _ada_fwd(c_pad, w_ada[0], b_cols, cw_pad, jobs=[g_f1])
    ada = ada_blk[:, 0:bl, :].transpose(1, 0, 2).reshape(bl, 9, D)
    pad5 = jnp.zeros((bl, 5, D), F32)
    mod1 = jnp.concatenate([ada[:, 0:3], pad5], axis=1)
    mod2 = jnp.concatenate([ada[:, 3:6], pad5], axis=1)
    mod3 = jnp.concatenate([ada[:, 6:9], pad5], axis=1)
    cw_full = cw_all.transpose(1, 0, 2).reshape(32, WA)

    zrow = jnp.zeros((1, D), F32)
    gv1 = jnp.concatenate([g_pre_f1, g_post_f1] + [zrow] * 6, axis=0)
    gvm = jnp.concatenate([g_pre_m, g_post_m] + [zrow] * 6, axis=0)
    gv2 = jnp.concatenate([g_pre_f2, g_post_f2] + [zrow] * 6, axis=0)
    v512 = jnp.concatenate([gmlp_norm_g, gmlp_norm_b, conv_b, conv_norm_g, conv_norm_b, g_out_a, g_out_b,
                            jnp.zeros((1, WA), F32)], axis=0)
    ws = w_spatial[0]
    bias_full = jnp.repeat(b_spatial[0].T, HD, axis=1)
    esel = (lax.broadcasted_iota(jnp.int32, (8, WA), 1) // HD == lax.broadcasted_iota(jnp.int32, (8, WA), 0)).astype(F32)

    x0 = x.reshape(T, D)
    (x1, gu1, y1), ((wmi, wmo, wo2, wi2a),) = _ffn_fwd(x0, mod1, gv1, wi1, wo1, tm, "ffn1_fwd", jobs=[g_mx])
    wmo = wmo.reshape(D, D)
    (x2, proj, ym, conv), ((wi2b,),) = _mixer_fwd(x1, mod2, gvm, wmi, wmo, v512, ws, bias_full, cw_full, tm, "mixer_fwd", jobs=[g_f2])

    (dx2, dg2, act2, hb2, dyb2, mg3, vg3, loss_blk), _ = _ffn_last(
        x2, loss_target.reshape(T, D), mod3, gv2, (wi2a, wi2b), wo2, tm, "ffn2_fwd_bwd")
    (g_wi2,), _ = _grad_w_in(dg2, hb2, "ffn2_gw_in")
    (g_wo2,), _ = _grad_w_out(act2, dyb2, "ffn2_gw_out")
    (dpart, dymb, ycat, mg2a, vgma, v5g, gws, gbs), ((p_wo2,),) = _mixer_bwd_a(
        dx2, ym, proj, conv, mod2, gvm, wmo, v512, ws, bias_full, esel, tm, "mixer_bwd_a",
        jobs=[_ChipScatter([g_wo2])])
    (dx1, dproj, hbm, mg2b, vgmb, dcw), ((p_wi2,),) = _mixer_bwd_b(
        dx2, x1, dpart, proj, mod2, gvm, wmi, cw_full, tm, "mixer_bwd_b", jobs=[_ChipScatter([g_wi2])])
    (g_wmi,), _ = _grad_w_mi(hbm, dproj, "mixer_gw_in")
    (g_wmo,), _ = _grad_w_mo(ycat, dymb, "mixer_gw_out")
    p2 = jnp.concatenate([v5g, dcw], axis=0)
    (dx0, dg1, act1, hb1, dyb1, mg1, vg1), _ = _ffn_bwd(dx1, x0, y1, gu1, mod1, gv1, wi1, wo1, tm, "ffn1_bwd")

    dada = jnp.concatenate([mg1[:, 0:3], mg2b[:, 0:2], mg2a[:, 2:3], mg3[:, 0:3]], axis=1)
    dada = dada.reshape(bl, NDEV, ADA_B).transpose(1, 0, 2)
    dada = jnp.pad(dada, ((0, 0), (0, 8 - bl), (0, 0)))
    p1 = jnp.concatenate([vg1[0:2], vgmb[0:1], vgma[1:2], vg3[0:2], loss_blk[0:1], zrow], axis=0)
    (dd_all, gb_all), ((a1,),) = _ada_bwd(dada, jobs=[_AllGather([p1])])
    g_bada = gb_all[:, 0, :].reshape(1, 9 * D)

    (g_wo1,), ((p_wmi, p_wmo),) = _grad_w_out(act1, dyb1, "ffn1_gw_out", jobs=[_ChipScatter([g_wmi, g_wmo])])
    (g_wi1,), ((a2, a3, a4), (p_wo1,)) = _grad_w_in(
        dg1, hb1, "ffn1_gw_in", jobs=[_Gather([p2, gws, gbs], ("rows",) * 3, zpad), _ChipScatter([g_wo1])])

    h_f1, token = _chip_scatter_start([g_wi1], "tail_start")

    res = {}
    quad = _adamw_reduce(p_wi2, w_f2_in[0].T, m_w_f2_in[0].T, v_w_f2_in[0].T, FO, "adamw_w_f2_in", after=token)
    res["w_f2_in"] = tuple(t.T[None] for t in quad)
    for nm, part, tr in (("w_f2_out", p_wo2, FO), ("w_mix_in", p_wmi, 256), ("w_mix_out", p_wmo, MO), ("w_f1_out", p_wo1, FO)):
        quad = _adamw_reduce(part, given[nm][0], given["m_" + nm][0], given["v_" + nm][0], tr, "adamw_" + nm, after=quad[1])
        res[nm] = tuple(t[None] for t in quad)
    quad = _adamw_ada(sc_all, dd_all, w_ada[0], m_w_ada[0], v_w_ada[0], 256, "adamw_w_ada", after=quad[1])
    res["w_ada"] = tuple(t[None] for t in quad)
    (g_wi1,), (p_wi1,) = _chip_scatter_wait(h_f1, quad[1], "tail_wait")
    quad = _adamw_reduce(p_wi1, w_f1_in[0].T, m_w_f1_in[0].T, v_w_f1_in[0].T, FO, "adamw_w_f1_in", own=g_wi1)
    res["w_f1_in"] = tuple(t.T[None] for t in quad)

    small = SMALL_D + SMALL_W + ("w_spatial", "b_spatial", "b_ada")
    grads = [(0, r) for r in range(6)] + [(1, r) for r in range(7)] + [(2, None), (3, None), (4, None)]
    wmv = []
    for nm in small:
        for pre in ("", "m_", "v_"):
            wmv.append(given[pre + nm][0] if nm in ("w_spatial", "b_spatial") else given[pre + nm])
    outs = _adamw_small([a1, a2, a3, a4], [g_bada], grads, wmv, (0, 1), "adamw_small")
    loss = outs[0][6, 0]
    for t, nm in enumerate(small):
        quad = outs[2 + 4 * t:6 + 4 * t]
        res[nm] = tuple(q[None] for q in quad) if nm in ("w_spatial", "b_spatial") else tuple(quad)
    g_cw = lax.dynamic_slice(outs[1], (8, mi * 64), (32, 64))
    wmv = [jnp.pad(given[pre + "conv_w"][0], ((0, 1), (0, 0)), constant_values=1.0 if pre == "v_" else 0.0)
           for pre in ("", "m_", "v_")]
    quad = _adamw_small([], [g_cw], [(0, None)], wmv, (), "adamw_conv_w")
    res["conv_w"] = tuple(q[0:CONV_K][None] for q in quad)

    order = ["w_ada", "b_ada", "g_pre_f1", "g_post_f1", "w_f1_in", "w_f1_out", "g_pre_m", "g_post_m", "w_mix_in",
             "gmlp_norm_g", "gmlp_norm_b", "w_spatial", "b_spatial", "conv_w", "conv_b", "conv_norm_g", "conv_norm_b",
             "g_out_a", "g_out_b", "w_mix_out", "g_pre_f2", "g_post_f2", "w_f2_in", "w_f2_out"]
    out = [loss, dx0.reshape(bl, seq, D)]
    for k in range(4):
        out += [res[nm][k] for nm in order]
    return tuple(out)
```

```python
import jax
import jax.numpy as jnp
from jax import lax
from jax.experimental import pallas as pl
from jax.experimental.pallas import tpu as pltpu

F32 = jnp.float32
BF16 = jnp.bfloat16

D = 1024
DFF = 2816
NDEV = 8
FB = 2 * DFF // NDEV
NCH = DFF // FB
LANES = 128
SUBL = 8
FO = DFF // NDEV
WA = 512
NSLAB = WA // LANES
NHEAD = 8
HD = 64
CHUNK = 128
CONV_K = 31
HALO = 32
MB = 2 * (WA + WA) // NDEV
MO = D // NDEV
ADA_B = 9 * D // NDEV
EPS = 1e-6
HALF = 0.5

ADAM_LR = 0.001
ADAM_B1 = 0.9
ADAM_B2 = 0.999
ADAM_EPS = 1e-08
ADAM_WD = 0.01
ADAM_STEP = 10

VMEM_LIMIT = 56 * 1024 * 1024
MESH = pl.DeviceIdType.MESH
FLIPS = ((0, 0, 1), (1, 0, 0), (0, 1, 0), (1, 1, 0), (1, 0, 1), (0, 1, 1), (1, 1, 1))
CHIP_FLIPS = ((1, 0, 0), (0, 1, 0), (1, 1, 0))
HBM = pl.BlockSpec(memory_space=pl.ANY)
VM = pl.BlockSpec(memory_space=pltpu.VMEM)


def _dot(a, b):
    return lax.dot_general(a, b, (((1,), (0,)), ((), ())), preferred_element_type=F32)


def _dot_nt(a, b):
    return lax.dot_general(a, b, (((1,), (1,)), ((), ())), preferred_element_type=F32)


def _dot_tn(a, b):
    return lax.dot_general(a, b, (((0,), (0,)), ((), ())), preferred_element_type=F32)


def _rowmean(v):
    return jnp.mean(v, axis=-1, keepdims=True)


def _colsum(v):
    return jnp.sum(v, axis=0, keepdims=True)


def _sigmoid(v):
    return 0.5 * jnp.tanh(0.5 * v) + 0.5


def _const_spec(shape):
    nd = len(shape)
    return pl.BlockSpec(shape, lambda *_: (0,) * nd, pipeline_mode=pl.Buffered(1))


def _me():
    return lax.axis_index("x"), lax.axis_index("y"), lax.axis_index("c")


def _flip(me, f):
    return tuple(1 - v if b else v for v, b in zip(me, f))


def _lin(p):
    return 4 * p[0] + 2 * p[1] + p[2]


def _remote(src, dst, send_sem, recv_sem, dev):
    return pltpu.make_async_remote_copy(src_ref=src, dst_ref=dst, send_sem=send_sem, recv_sem=recv_sem,
                                        device_id=dev, device_id_type=MESH)


def _blk(kind, ref, p):
    if kind == "out":
        return ref.at[2 * p[0] + p[1], pl.ds(p[2] * FO, FO), :]
    return ref.at[_lin(p)]


class _Gather:
    def __init__(self, shards, kinds, late_mid=False):
        self.late_mid = late_mid
        self.kinds = kinds
        self.n = len(shards)
        self.ins = list(shards)
        self.out_shape = [jax.ShapeDtypeStruct((4, FB, D) if k == "out" else (NDEV,) + s.shape, s.dtype)
                          for s, k in zip(shards, kinds)]
        self.sems = [pltpu.SemaphoreType.DMA((7 * self.n,)), pltpu.SemaphoreType.DMA((7 * self.n,)),
                     pltpu.SemaphoreType.DMA((self.n,))]

    def _first(self, ins, outs, sems):
        ssem, rsem, lsem = sems
        me = _me()
        sib = _flip(me, (0, 0, 1))
        cps, loc = [], []
        for a in range(self.n):
            mine = _blk(self.kinds[a], outs[a], me)
            loc.append(pltpu.make_async_copy(ins[a], mine, lsem.at[a]))
            cps.append(_remote(ins[a], mine, ssem.at[7 * a], rsem.at[7 * a], sib))
            for j, f in enumerate(CHIP_FLIPS):
                cps.append(_remote(ins[a], mine, ssem.at[7 * a + 1 + j], rsem.at[7 * a + 1 + j], _flip(me, f)))
        return cps, loc

    def _passed(self, outs, sems):
        ssem, rsem, _ = sems
        me = _me()
        sib = _flip(me, (0, 0, 1))
        cps = []
        for j, f in enumerate(CHIP_FLIPS):
            for a in range(self.n):
                blk = _blk(self.kinds[a], outs[a], _flip(me, f))
                cps.append(_remote(blk, blk, ssem.at[7 * a + 4 + j], rsem.at[7 * a + 4 + j], sib))
        return cps

    def start(self, ins, outs, sems):
        cps, loc = self._first(ins, outs, sems)
        for cp in loc + cps:
            cp.start()

    def mid(self, ins, outs, sems):
        ssem, rsem, _ = sems
        me = _me()
        passed = self._passed(outs, sems)
        t = 0
        for j, f in enumerate(CHIP_FLIPS):
            for a in range(self.n):
                blk = _blk(self.kinds[a], outs[a], _flip(me, f))
                _remote(blk, blk, ssem.at[7 * a + 1 + j], rsem.at[7 * a + 1 + j], _flip(me, f)).wait_recv()
                passed[t].start()
                t += 1

    def end(self, ins, outs, sems):
        ssem, rsem, _ = sems
        me = _me()
        sib = _flip(me, (0, 0, 1))
        for a in range(self.n):
            blk = _blk(self.kinds[a], outs[a], sib)
            _remote(blk, blk, ssem.at[7 * a], rsem.at[7 * a], sib).wait_recv()
            for j, f in enumerate(CHIP_FLIPS):
                blk = _blk(self.kinds[a], outs[a], _flip(_flip(me, f), (0, 0, 1)))
                _remote(blk, blk, ssem.at[7 * a + 4 + j], rsem.at[7 * a + 4 + j], sib).wait_recv()
        cps, loc = self._first(ins, outs, sems)
        for cp in cps + self._passed(outs, sems):
            cp.wait_send()
        for cp in loc:
            cp.wait()


class _ChipScatter:
    def __init__(self, grads):
        self.n = len(grads)
        self.ins = list(grads)
        self.out_shape = [jax.ShapeDtypeStruct(g.shape, BF16) for g in grads]
        self.sems = [pltpu.SemaphoreType.DMA((3 * self.n,)), pltpu.SemaphoreType.DMA((3 * self.n,)),
                     pltpu.SemaphoreType.DMA((self.n,))]

    def _copies(self, ins, outs, sems):
        ssem, rsem, lsem = sems
        me = _me()
        mq = 2 * me[0] + me[1]
        loc = [pltpu.make_async_copy(ins[a].at[mq], outs[a].at[mq], lsem.at[a]) for a in range(self.n)]
        cps = []
        for k, f in enumerate(CHIP_FLIPS):
            p = _flip(me, f)
            for a in range(self.n):
                cps.append(_remote(ins[a].at[2 * p[0] + p[1]], outs[a].at[mq], ssem.at[3 * a + k], rsem.at[3 * a + k], p))
        return cps, loc

    def start(self, ins, outs, sems):
        cps, loc = self._copies(ins, outs, sems)
        for cp in loc + cps:
            cp.start()

    mid = None

    def end(self, ins, outs, sems):
        ssem, rsem, _ = sems
        me = _me()
        mq = 2 * me[0] + me[1]
        for k, f in enumerate(CHIP_FLIPS):
            p = _flip(me, f)
            for a in range(self.n):
                _remote(ins[a].at[mq], outs[a].at[2 * p[0] + p[1]], ssem.at[3 * a + k], rsem.at[3 * a + k], p).wait_recv()
        cps, loc = self._copies(ins, outs, sems)
        for cp in cps:
            cp.wait_send()
        for cp in loc:
            cp.wait()


class _AllGather:
    def __init__(self, parts):
        self.n = len(parts)
        self.ins = list(parts)
        self.out_shape = [jax.ShapeDtypeStruct((NDEV,) + p.shape, p.dtype) for p in parts]
        self.sems = [pltpu.SemaphoreType.DMA((7 * self.n,)), pltpu.SemaphoreType.DMA((7 * self.n,)),
                     pltpu.SemaphoreType.DMA((self.n,))]

    def _copies(self, ins, outs, sems):
        ssem, rsem, lsem = sems
        me = _me()
        mi = _lin(me)
        loc = [pltpu.make_async_copy(ins[a], outs[a].at[mi], lsem.at[a]) for a in range(self.n)]
        cps = []
        for k, f in enumerate(FLIPS):
            for a in range(self.n):
                cps.append(_remote(ins[a], outs[a].at[mi], ssem.at[7 * a + k], rsem.at[7 * a + k], _flip(me, f)))
        return cps, loc

    def start(self, ins, outs, sems):
        cps, loc = self._copies(ins, outs, sems)
        for cp in loc + cps:
            cp.start()

    mid = None

    def end(self, ins, outs, sems):
        ssem, rsem, _ = sems
        me = _me()
        for k, f in enumerate(FLIPS):
            p = _flip(me, f)
            for a in range(self.n):
                _remote(ins[a], outs[a].at[_lin(p)], ssem.at[7 * a + k], rsem.at[7 * a + k], p).wait_recv()
        cps, loc = self._copies(ins, outs, sems)
        for cp in cps:
            cp.wait_send()
        for cp in loc:
            cp.wait()


def _call(core, *, name, grid, in_specs, out_specs, out_shape, args, scratch=(), jobs=()):
    n_in, n_out, n_sc = len(in_specs), len(out_specs), len(scratch)
    steps = 1
    for g in grid:
        steps *= g

    def body(*refs):
        pos = [0]

        def take(k):
            r = refs[pos[0]:pos[0] + k]
            pos[0] += k
            return r

        ins = take(n_in)
        j_ins = [take(len(j.ins)) for j in jobs]
        outs = take(n_out)
        j_outs = [take(len(j.out_shape)) for j in jobs]
        scs = take(n_sc)
        j_sems = [take(len(j.sems)) for j in jobs]
        if len(grid) == 2:
            step = pl.program_id(0) * grid[1] + pl.program_id(1)
        elif len(grid) == 1:
            step = pl.program_id(0)
        else:
            step = 0
        for j, ji, jo, js in zip(jobs, j_ins, j_outs, j_sems):
            if grid:
                pl.when(step == 0)(lambda j=j, ji=ji, jo=jo, js=js: j.start(ji, jo, js))
            else:
                j.start(ji, jo, js)
        for j, ji, jo, js in zip(jobs, j_ins, j_outs, j_sems):
            if j.mid is not None and grid:
                at = max(steps - 2, 0) if j.late_mid else (3 * steps) // 4
                pl.when(step == at)(lambda j=j, ji=ji, jo=jo, js=js: j.mid(ji, jo, js))
        if core is not None:
            core(ins, outs, scs)
        for j, ji, jo, js in zip(jobs, j_ins, j_outs, j_sems):
            if grid:
                pl.when(step == steps - 1)(lambda j=j, ji=ji, jo=jo, js=js: j.end(ji, jo, js))
            else:
                if j.mid is not None:
                    j.mid(ji, jo, js)
                j.end(ji, jo, js)

    all_in = list(in_specs)
    all_args = list(args)
    all_out = list(out_specs)
    all_shape = list(out_shape)
    all_sc = list(scratch)
    for j in jobs:
        all_in += [HBM] * len(j.ins)
        all_args += j.ins
    for j in jobs:
        all_out += [HBM] * len(j.out_shape)
        all_shape += j.out_shape
        all_sc += j.sems
    params = dict(vmem_limit_bytes=VMEM_LIMIT)
    if grid:
        params["dimension_semantics"] = ("arbitrary",) * len(grid)
    res = pl.pallas_call(
        body, name=name, grid=grid, in_specs=all_in, out_specs=all_out, out_shape=all_shape,
        scratch_shapes=all_sc, compiler_params=pltpu.CompilerParams(**params),
    )(*all_args)
    core_res = list(res[:n_out])
    job_res = []
    pos = n_out
    for j in jobs:
        job_res.append(list(res[pos:pos + len(j.out_shape)]))
        pos += len(j.out_shape)
    return core_res, job_res


def _ffn_fwd(x, mod, gvec, w_in, w_out, tm, name, jobs=()):
    T = x.shape[0]
    nt = T // tm
    tps = nt // mod.shape[0]

    def core(ins, outs, _):
        x_ref, mod_ref, g_ref, win_ref, wout_ref = ins
        xo_ref, gu_ref, y_ref = outs
        xv = x_ref[...]
        sh, sc, gt = mod_ref[0:1, :], mod_ref[1:2, :], mod_ref[2:3, :]
        r = lax.rsqrt(_rowmean(xv * xv) + EPS)
        h = (xv * r * g_ref[0:1, :]) * (1.0 + sc) + sh
        hb = h.astype(BF16)
        y = jnp.zeros((tm, D), F32)
        for cidx in range(NCH):
            gate = _dot_nt(hb, win_ref[cidx])
            up = _dot_nt(hb, win_ref[NCH + cidx])
            gu_ref[cidx] = gate.astype(BF16)
            gu_ref[NCH + cidx] = up.astype(BF16)
            act = gate * _sigmoid(gate) * up
            y = y + _dot(act.astype(BF16), wout_ref[cidx])
        y_ref[...] = y
        ry = lax.rsqrt(_rowmean(y * y) + EPS)
        xo_ref[...] = xv + (HALF * gt) * (y * ry * g_ref[1:2, :])

    tile = pl.BlockSpec((tm, D), lambda i: (i, 0))
    return _call(
        core, name=name, grid=(nt,), jobs=jobs,
        in_specs=[tile, pl.BlockSpec((None, 8, D), lambda i: (i // tps, 0, 0)), _const_spec((8, D)),
                  _const_spec((8, FB, D)), _const_spec((4, FB, D))],
        out_specs=[tile, pl.BlockSpec((8, tm, FB), lambda i: (0, i, 0)), tile],
        out_shape=[jax.ShapeDtypeStruct((T, D), F32), jax.ShapeDtypeStruct((8, T, FB), BF16),
                   jax.ShapeDtypeStruct((T, D), F32)],
        args=[x, mod, gvec, w_in, w_out])


def _ffn_bwd(dxo, x, y, gu, mod, gvec, w_in, w_out, tm, name, jobs=()):
    T = x.shape[0]
    nt = T // tm
    nb = mod.shape[0]
    tps = nt // nb

    def core(ins, outs, _):
        dxo_ref, x_ref, y_ref, gu_ref, mod_ref, g_ref, win_ref, wout_ref = ins
        dx_ref, dg_ref, act_ref, hb_ref, dyb_ref, mg_ref, vg_ref = outs
        i = pl.program_id(0)
        xv = x_ref[...]
        dxo_v = dxo_ref[...]
        yv = y_ref[...]
        sh, sc, gt = mod_ref[0:1, :], mod_ref[1:2, :], mod_ref[2:3, :]
        gpre, gpost = g_ref[0:1, :], g_ref[1:2, :]
        r = lax.rsqrt(_rowmean(xv * xv) + EPS)
        xh = xv * r
        n = xh * gpre
        hb = (n * (1.0 + sc) + sh).astype(BF16)
        hb_ref[...] = hb
        ry = lax.rsqrt(_rowmean(yv * yv) + EPS)
        yh = yv * ry
        d_gt = _colsum(HALF * dxo_v * (yh * gpost))
        dp = (HALF * gt) * dxo_v
        d_gpost = _colsum(dp * yh)
        dyh = dp * gpost
        dy = ry * (dyh - yh * _rowmean(dyh * yh))
        dyb = dy.astype(BF16)
        dyb_ref[...] = dyb
        dh = jnp.zeros((tm, D), F32)
        for cidx in range(NCH):
            gate = gu_ref[cidx].astype(F32)
            up = gu_ref[NCH + cidx].astype(F32)
            sig = _sigmoid(gate)
            s = gate * sig
            act_ref[cidx] = (s * up).astype(BF16)
            d_act = _dot_nt(dyb, wout_ref[cidx])
            d_up = (d_act * s).astype(BF16)
            d_gate = (d_act * up * (sig * (1.0 + gate * (1.0 - sig)))).astype(BF16)
            dg_ref[cidx] = d_gate
            dg_ref[NCH + cidx] = d_up
            dh = dh + _dot(d_gate, win_ref[cidx]) + _dot(d_up, win_ref[NCH + cidx])
        d_sc = _colsum(dh * n)
        d_sh = _colsum(dh)
        dn = dh * (1.0 + sc)
        d_gpre = _colsum(dn * xh)
        dxh = dn * gpre
        dx_ref[...] = dxo_v + r * (dxh - xh * _rowmean(dxh * xh))

        @pl.when(i % tps == 0)
        def _():
            mg_ref[...] = jnp.zeros((8, D), F32)

        @pl.when(i == 0)
        def _():
            vg_ref[...] = jnp.zeros((8, D), F32)

        mg_ref[0:1, :] += d_sh
        mg_ref[1:2, :] += d_sc
        mg_ref[2:3, :] += d_gt
        vg_ref[0:1, :] += d_gpre
        vg_ref[1:2, :] += d_gpost

    tile = pl.BlockSpec((tm, D), lambda i: (i, 0))
    return _call(
        core, name=name, grid=(nt,), jobs=jobs,
        in_specs=[tile, tile, tile, pl.BlockSpec((8, tm, FB), lambda i: (0, i, 0)),
                  pl.BlockSpec((None, 8, D), lambda i: (i // tps, 0, 0)), _const_spec((8, D)),
                  _const_spec((8, FB, D)), _const_spec((4, FB, D))],
        out_specs=[tile, pl.BlockSpec((8, tm, FB), lambda i: (0, i, 0)),
                   pl.BlockSpec((4, tm, FB), lambda i: (0, i, 0)), tile, tile,
                   pl.BlockSpec((None, 8, D), lambda i: (i // tps, 0, 0)), pl.BlockSpec((8, D), lambda i: (0, 0))],
        out_shape=[jax.ShapeDtypeStruct((T, D), F32), jax.ShapeDtypeStruct((8, T, FB), BF16),
                   jax.ShapeDtypeStruct((4, T, FB), BF16), jax.ShapeDtypeStruct((T, D), BF16),
                   jax.ShapeDtypeStruct((T, D), BF16), jax.ShapeDtypeStruct((nb, 8, D), F32),
                   jax.ShapeDtypeStruct((8, D), F32)],
        args=[dxo, x, y, gu, mod, gvec, w_in, w_out])


def _ffn_last(x, target, mod, gvec, w_in, w_out, tm, name, jobs=()):
    T = x.shape[0]
    nt = T // tm
    nb = mod.shape[0]
    tps = nt // nb

    def core(ins, outs, scs):
        x_ref, t_ref, mod_ref, g_ref, wina_ref, winb_ref, wout_ref = ins
        dx_ref, dg_ref, act_ref, hb_ref, dyb_ref, mg_ref, vg_ref, loss_ref = outs
        hd2 = w_in[0].shape[2]
        (gu_s,) = scs
        i = pl.program_id(0)
        xv = x_ref[...]
        sh, sc, gt = mod_ref[0:1, :], mod_ref[1:2, :], mod_ref[2:3, :]
        gpre, gpost = g_ref[0:1, :], g_ref[1:2, :]
        r = lax.rsqrt(_rowmean(xv * xv) + EPS)
        xh = xv * r
        n = xh * gpre
        hb = (n * (1.0 + sc) + sh).astype(BF16)
        hb_ref[...] = hb
        hba, hbb = hb[:, 0:hd2], hb[:, hd2:D]
        yv = jnp.zeros((tm, D), F32)
        for cidx in range(NCH):
            gate = _dot_nt(hba, wina_ref[cidx]) + _dot_nt(hbb, winb_ref[cidx])
            up = _dot_nt(hba, wina_ref[NCH + cidx]) + _dot_nt(hbb, winb_ref[NCH + cidx])
            gu_s[cidx] = gate.astype(BF16)
            gu_s[NCH + cidx] = up.astype(BF16)
            act = gate * _sigmoid(gate) * up
            act_ref[cidx] = act.astype(BF16)
            yv = yv + _dot(act_ref[cidx], wout_ref[cidx])
        ry = lax.rsqrt(_rowmean(yv * yv) + EPS)
        yh = yv * ry
        pn = yh * gpost
        err = xv + (HALF * gt) * pn - t_ref[...]
        dxo_v = err * (1.0 / D)
        d_gt = _colsum(HALF * dxo_v * pn)
        dp = (HALF * gt) * dxo_v
        d_gpost = _colsum(dp * yh)
        dyh = dp * gpost
        dyb = (ry * (dyh - yh * _rowmean(dyh * yh))).astype(BF16)
        dyb_ref[...] = dyb
        dha = jnp.zeros((tm, hd2), F32)
        dhb = jnp.zeros((tm, D - hd2), F32)
        for cidx in range(NCH):
            gate = gu_s[cidx].astype(F32)
            up = gu_s[NCH + cidx].astype(F32)
            sig = _sigmoid(gate)
            s = gate * sig
            d_act = _dot_nt(dyb, wout_ref[cidx])
            d_up = (d_act * s).astype(BF16)
            d_gate = (d_act * up * (sig * (1.0 + gate * (1.0 - sig)))).astype(BF16)
            dg_ref[cidx] = d_gate
            dg_ref[NCH + cidx] = d_up
            dha = dha + _dot(d_gate, wina_ref[cidx]) + _dot(d_up, wina_ref[NCH + cidx])
            dhb = dhb + _dot(d_gate, winb_ref[cidx]) + _dot(d_up, winb_ref[NCH + cidx])
        dh = jnp.concatenate([dha, dhb], axis=1)
        d_sc = _colsum(dh * n)
        d_sh = _colsum(dh)
        dn = dh * (1.0 + sc)
        d_gpre = _colsum(dn * xh)
        dxh = dn * gpre
        dx_ref[...] = dxo_v + r * (dxh - xh * _rowmean(dxh * xh))

        @pl.when(i % tps == 0)
        def _():
            mg_ref[...] = jnp.zeros((8, D), F32)

        @pl.when(i == 0)
        def _():
            vg_ref[...] = jnp.zeros((8, D), F32)
            loss_ref[...] = jnp.zeros((8, D), F32)

        mg_ref[0:1, :] += d_sh
        mg_ref[1:2, :] += d_sc
        mg_ref[2:3, :] += d_gt
        vg_ref[0:1, :] += d_gpre
        vg_ref[1:2, :] += d_gpost
        loss_ref[...] += HALF * jnp.sum(_rowmean(err * err), axis=0, keepdims=True)

    tile = pl.BlockSpec((tm, D), lambda i: (i, 0))
    return _call(
        core, name=name, grid=(nt,), jobs=jobs,
        in_specs=[tile, tile, pl.BlockSpec((None, 8, D), lambda i: (i // tps, 0, 0)), _const_spec((8, D)),
                  _const_spec(w_in[0].shape), _const_spec(w_in[1].shape), _const_spec((4, FB, D))],
        out_specs=[tile, pl.BlockSpec((8, tm, FB), lambda i: (0, i, 0)),
                   pl.BlockSpec((4, tm, FB), lambda i: (0, i, 0)), tile, tile,
                   pl.BlockSpec((None, 8, D), lambda i: (i // tps, 0, 0)), pl.BlockSpec((8, D), lambda i: (0, 0)),
                   pl.BlockSpec((8, D), lambda i: (0, 0))],
        out_shape=[jax.ShapeDtypeStruct((T, D), F32), jax.ShapeDtypeStruct((8, T, FB), BF16),
                   jax.ShapeDtypeStruct((4, T, FB), BF16), jax.ShapeDtypeStruct((T, D), BF16),
                   jax.ShapeDtypeStruct((T, D), BF16), jax.ShapeDtypeStruct((nb, 8, D), F32),
                   jax.ShapeDtypeStruct((8, D), F32), jax.ShapeDtypeStruct((8, D), F32)],
        scratch=[pltpu.VMEM((8, tm, FB), BF16)],
        args=[x, target, mod, gvec, w_in[0], w_in[1], w_out])


def _masked_spatial(ws_ref):
    row = lax.broadcasted_iota(jnp.int32, (CHUNK, CHUNK), 0)
    col = lax.broadcasted_iota(jnp.int32, (CHUNK, CHUNK), 1)
    keep = col <= row
    return [jnp.where(keep, ws_ref[hd], 0.0).astype(BF16) for hd in range(NHEAD)]


def _head_pairs(mats, right, transpose=False):
    first = lax.broadcasted_iota(jnp.int32, (CHUNK, LANES), 1) < HD
    op = _dot_tn if transpose else _dot
    out = []
    for p in range(NHEAD // 2):
        slab = right[:, _lanes(p)]
        out.append(jnp.where(first, op(mats[2 * p], slab), op(mats[2 * p + 1], slab)))
    return jnp.concatenate(out, axis=1)


def _spatial_gate(wm, vb_chunk):
    return _head_pairs(wm, vb_chunk)


def _layer_norm_stats(v):
    mu = _rowmean(v)
    vc = v - mu
    rstd = lax.rsqrt(_rowmean(vc * vc) + EPS)
    return vc * rstd, rstd


def _pitch(tm):
    p = tm // 8
    while p % 8 != 4:
        p += 1
    return p


def _lanes(s):
    return slice(s * LANES, (s + 1) * LANES)


def _to_slabs(ref, row0, val):
    for s in range(NSLAB):
        ref[s, row0:row0 + val.shape[0], :] = val[:, _lanes(s)]


def _tap_sum(src, out, cw_ref, bias, tm, start):
    p = _pitch(tm)
    for s in range(NSLAB):
        accs = [jnp.broadcast_to(bias[:, _lanes(s)], (SUBL, LANES))] * p
        for k in range(CONV_K):
            w = jnp.broadcast_to(cw_ref[k:k + 1, _lanes(s)], (SUBL, LANES))
            for v in range(p):
                accs[v] = accs[v] + w * src[s, pl.ds(v + start(k), 8, stride=p), :]
        for v in range(p):
            out[s, pl.ds(v, 8, stride=p), :] = accs[v]
    return jnp.concatenate([out[s, 0:tm, :] for s in range(NSLAB)], axis=1)


def _mixer_fwd(x, mod, gvec, w_mi, w_mo, v512, ws, bias_full, cw, tm, name, jobs=()):
    T = x.shape[0]
    nt = T // tm
    tps = nt // mod.shape[0]
    ext_rows = 8 * _pitch(tm)

    def core(ins, outs, scs):
        x_ref, mod_ref, g_ref, wmi_ref, wmo_ref, v_ref, ws_ref, bias_ref, cw_ref = ins
        xo_ref, proj_ref, ym_ref, conv_ref = outs
        glu_ext, conv_scr = scs
        i = pl.program_id(0)
        xv = x_ref[...]
        sh, sc, gt = mod_ref[0:1, :], mod_ref[1:2, :], mod_ref[2:3, :]
        r = lax.rsqrt(_rowmean(xv * xv) + EPS)
        hb = ((xv * r * g_ref[0:1, :]) * (1.0 + sc) + sh).astype(BF16)
        for j in range(NDEV):
            proj_ref[:, j * MB:(j + 1) * MB] = _dot(hb, wmi_ref[j])
        u = proj_ref[:, 0:WA]
        v0 = proj_ref[:, WA:2 * WA]
        a = proj_ref[:, 2 * WA:3 * WA]
        g = proj_ref[:, 3 * WA:4 * WA]
        vh, _ = _layer_norm_stats(v0)
        vb = (vh * v_ref[0:1, :] + v_ref[1:2, :]).astype(BF16)
        wm = _masked_spatial(ws_ref)
        ya = []
        for q in range(tm // CHUNK):
            z = _spatial_gate(wm, vb[q * CHUNK:(q + 1) * CHUNK, :]) + bias_ref[...]
            ya.append(u[q * CHUNK:(q + 1) * CHUNK, :] * z)
        ya = jnp.concatenate(ya, axis=0)
        glu = a * _sigmoid(g)

        @pl.when(i == 0)
        def _():
            glu_ext[:, HALO + tm:HALO + ext_rows, :] = jnp.zeros((NSLAB, ext_rows - tm, LANES), F32)

        @pl.when(i % tps == 0)
        def _():
            glu_ext[:, 0:HALO, :] = jnp.zeros((NSLAB, HALO, LANES), F32)

        _to_slabs(glu_ext, HALO, glu)
        conv = _tap_sum(glu_ext, conv_scr, cw_ref, v_ref[2:3, :], tm, lambda k: HALO - (CONV_K - 1) + k)
        conv_ref[...] = conv
        glu_ext[:, 0:HALO, :] = glu_ext[:, tm:tm + HALO, :]
        ch, _ = _layer_norm_stats(conv)
        cn = ch * v_ref[3:4, :] + v_ref[4:5, :]
        yb = cn * _sigmoid(cn)
        pa = ya * lax.rsqrt(_rowmean(ya * ya) + EPS) * v_ref[5:6, :]
        pb = yb * lax.rsqrt(_rowmean(yb * yb) + EPS) * v_ref[6:7, :]
        ycat = jnp.concatenate([pa, pb], axis=1).astype(BF16)
        ym = _dot(ycat, wmo_ref[...])
        ym_ref[...] = ym
        rm = lax.rsqrt(_rowmean(ym * ym) + EPS)
        xo_ref[...] = xv + gt * (ym * rm * g_ref[1:2, :])

    tile = pl.BlockSpec((tm, D), lambda i: (i, 0))
    return _call(
        core, name=name, grid=(nt,), jobs=jobs,
        in_specs=[tile, pl.BlockSpec((None, 8, D), lambda i: (i // tps, 0, 0)), _const_spec((8, D)),
                  _const_spec((NDEV, D, MB)), _const_spec((D, D)), _const_spec((8, WA)),
                  _const_spec((NHEAD, CHUNK, CHUNK)), _const_spec((CHUNK, WA)), _const_spec((32, WA))],
        out_specs=[tile, pl.BlockSpec((tm, 4 * WA), lambda i: (i, 0)), tile, pl.BlockSpec((tm, WA), lambda i: (i, 0))],
        out_shape=[jax.ShapeDtypeStruct((T, D), F32), jax.ShapeDtypeStruct((T, 4 * WA), F32),
                   jax.ShapeDtypeStruct((T, D), F32), jax.ShapeDtypeStruct((T, WA), F32)],
        scratch=[pltpu.VMEM((NSLAB, HALO + ext_rows, LANES), F32), pltpu.VMEM((NSLAB, ext_rows, LANES), F32)],
        args=[x, mod, gvec, w_mi, w_mo, v512, ws, bias_full, cw])


def _mixer_bwd_a(dxo, ym, proj, conv, mod, gvec, w_mo, v512, ws, bias_full, esel, tm, name, jobs=()):
    T = dxo.shape[0]
    nt = T // tm
    nb = mod.shape[0]
    tps = nt // nb

    def core(ins, outs, scs):
        dxo_ref, ym_ref, proj_ref, conv_ref, mod_ref, g_ref, wmo_ref, v_ref, ws_ref, bias_ref, e_ref = ins
        dpart_ref, dymb_ref, ycat_ref, mg_ref, vg_ref, v5g_ref, gws_ref, gbs_ref = outs
        (dbs_acc,) = scs
        i = pl.program_id(0)
        dxo_v = dxo_ref[...]
        ymv = ym_ref[...]
        gt = mod_ref[2:3, :]
        gpost = g_ref[1:2, :]
        rm = lax.rsqrt(_rowmean(ymv * ymv) + EPS)
        ymh = ymv * rm
        d_gt = _colsum(dxo_v * (ymh * gpost))
        dpm = gt * dxo_v
        d_gpost = _colsum(dpm * ymh)
        dymh = dpm * gpost
        dym = (rm * (dymh - ymh * _rowmean(dymh * ymh))).astype(BF16)
        dymb_ref[...] = dym
        dycat = _dot_nt(dym, wmo_ref[...])
        u = proj_ref[:, 0:WA]
        v0 = proj_ref[:, WA:2 * WA]
        vh, rv = _layer_norm_stats(v0)
        vb = (vh * v_ref[0:1, :] + v_ref[1:2, :]).astype(BF16)
        wm = _masked_spatial(ws_ref)
        zs = []
        for q in range(tm // CHUNK):
            zs.append(_spatial_gate(wm, vb[q * CHUNK:(q + 1) * CHUNK, :]) + bias_ref[...])
        z = jnp.concatenate(zs, axis=0)
        ya = u * z
        ra = lax.rsqrt(_rowmean(ya * ya) + EPS)
        yah = ya * ra
        ch, rc = _layer_norm_stats(conv_ref[...])
        cn = ch * v_ref[3:4, :] + v_ref[4:5, :]
        sg = _sigmoid(cn)
        yb = cn * sg
        rb = lax.rsqrt(_rowmean(yb * yb) + EPS)
        ybh = yb * rb
        ycat_ref[...] = jnp.concatenate([yah * v_ref[5:6, :], ybh * v_ref[6:7, :]], axis=1).astype(BF16)
        dpa = dycat[:, 0:WA]
        dpb = dycat[:, WA:2 * WA]
        d_goa = _colsum(dpa * yah)
        d_gob = _colsum(dpb * ybh)
        dyah = dpa * v_ref[5:6, :]
        dybh = dpb * v_ref[6:7, :]
        dya = ra * (dyah - yah * _rowmean(dyah * yah))
        dyb = rb * (dybh - ybh * _rowmean(dybh * ybh))
        dpart_ref[:, 0:WA] = dya * z
        dz = dya * u

        @pl.when(i == 0)
        def _():
            gws_ref[...] = jnp.zeros((NHEAD, CHUNK, CHUNK), F32)
            dbs_acc[...] = jnp.zeros((CHUNK, WA), F32)
            vg_ref[...] = jnp.zeros((8, D), F32)
            v5g_ref[...] = jnp.zeros((8, WA), F32)

        first = lax.broadcasted_iota(jnp.int32, (CHUNK, LANES), 1) < HD
        dvs = []
        for q in range(tm // CHUNK):
            dz_q = dz[q * CHUNK:(q + 1) * CHUNK, :]
            vb_q = vb[q * CHUNK:(q + 1) * CHUNK, :]
            dbs_acc[...] += dz_q
            dzb = dz_q.astype(BF16)
            dvs.append(_head_pairs(wm, dzb, transpose=True))
            for hd in range(NHEAD):
                slab = dzb[:, _lanes(hd // 2)]
                dz_hd = jnp.where(first if hd % 2 == 0 else jnp.logical_not(first), slab, jnp.zeros_like(slab))
                gws_ref[hd] += _dot_nt(dz_hd, vb_q[:, _lanes(hd // 2)])
        dv = jnp.concatenate(dvs, axis=0)
        d_gng = _colsum(dv * vh)
        d_gnb = _colsum(dv)
        dvh = dv * v_ref[0:1, :]
        dpart_ref[:, WA:2 * WA] = rv * (dvh - _rowmean(dvh) - vh * _rowmean(dvh * vh))
        dcn = dyb * (sg * (1.0 + cn * (1.0 - sg)))
        d_cng = _colsum(dcn * ch)
        d_cnb = _colsum(dcn)
        dch = dcn * v_ref[3:4, :]
        dconv = rc * (dch - _rowmean(dch) - ch * _rowmean(dch * ch))
        dpart_ref[:, 2 * WA:3 * WA] = dconv
        dpart_ref[:, 3 * WA:4 * WA] = jnp.zeros((tm, WA), F32)
        d_cb = _colsum(dconv)

        @pl.when(i % tps == 0)
        def _():
            mg_ref[...] = jnp.zeros((8, D), F32)

        mg_ref[2:3, :] += d_gt
        vg_ref[1:2, :] += d_gpost
        v5g_ref[0:1, :] += d_gng
        v5g_ref[1:2, :] += d_gnb
        v5g_ref[2:3, :] += d_cb
        v5g_ref[3:4, :] += d_cng
        v5g_ref[4:5, :] += d_cnb
        v5g_ref[5:6, :] += d_goa
        v5g_ref[6:7, :] += d_gob

        @pl.when(i == nt - 1)
        def _():
            row = lax.broadcasted_iota(jnp.int32, (CHUNK, CHUNK), 0)
            col = lax.broadcasted_iota(jnp.int32, (CHUNK, CHUNK), 1)
            for hd in range(NHEAD):
                gws_ref[hd] = jnp.where(col <= row, gws_ref[hd], 0.0)
            gbs_ref[...] = lax.dot_general(e_ref[...], dbs_acc[...], (((1,), (1,)), ((), ())),
                                           precision=lax.Precision.HIGHEST, preferred_element_type=F32)

    tile = pl.BlockSpec((tm, D), lambda i: (i, 0))
    ptile = pl.BlockSpec((tm, 4 * WA), lambda i: (i, 0))
    return _call(
        core, name=name, grid=(nt,), jobs=jobs,
        in_specs=[tile, tile, pl.BlockSpec((tm, 2 * WA), lambda i: (i, 0)), pl.BlockSpec((tm, WA), lambda i: (i, 0)),
                  pl.BlockSpec((None, 8, D), lambda i: (i // tps, 0, 0)), _const_spec((8, D)), _const_spec((D, D)),
                  _const_spec((8, WA)), _const_spec((NHEAD, CHUNK, CHUNK)), _const_spec((CHUNK, WA)),
                  _const_spec((8, WA))],
        out_specs=[ptile, tile, tile, pl.BlockSpec((None, 8, D), lambda i: (i // tps, 0, 0)),
                   pl.BlockSpec((8, D), lambda i: (0, 0)), pl.BlockSpec((8, WA), lambda i: (0, 0)),
                   pl.BlockSpec((NHEAD, CHUNK, CHUNK), lambda i: (0, 0, 0)), pl.BlockSpec((8, CHUNK), lambda i: (0, 0))],
        out_shape=[jax.ShapeDtypeStruct((T, 4 * WA), F32), jax.ShapeDtypeStruct((T, D), BF16),
                   jax.ShapeDtypeStruct((T, D), BF16), jax.ShapeDtypeStruct((nb, 8, D), F32),
                   jax.ShapeDtypeStruct((8, D), F32), jax.ShapeDtypeStruct((8, WA), F32),
                   jax.ShapeDtypeStruct((NHEAD, CHUNK, CHUNK), F32), jax.ShapeDtypeStruct((8, CHUNK), F32)],
        scratch=[pltpu.VMEM((CHUNK, WA), F32)],
        args=[dxo, ym, proj, conv, mod, gvec, w_mo, v512, ws, bias_full, esel])


def _mixer_bwd_b(dxo, x, dpart, proj, mod, gvec, w_mi, cw, tm, name, jobs=()):
    T = x.shape[0]
    nt = T // tm
    nb = mod.shape[0]
    tps = nt // nb
    hpt = tm // HALO
    nh = T // HALO
    off = HALO - (CONV_K - 1)
    p = _pitch(tm)
    ext_rows = 8 * p

    def core(ins, outs, scs):
        dxo_ref, x_ref, dpart_ref, dnext_ref, ag_ref, halo_ref, mod_ref, g_ref, wmi_ref, cw_ref = ins
        dx_ref, dproj_ref, hb_ref, mg_ref, vg_ref, dcw_ref = outs
        glu_ext, dconv_ext, dglu_scr, dcw_acc = scs
        i = pl.program_id(0)
        first = i % tps == 0
        last = i % tps == tps - 1
        a = ag_ref[:, 0:WA]
        g = ag_ref[:, WA:2 * WA]
        sgg = _sigmoid(g)

        @pl.when(i == 0)
        def _():
            glu_ext[:, HALO + tm:HALO + ext_rows, :] = jnp.zeros((NSLAB, ext_rows - tm, LANES), F32)
            dconv_ext[:, HALO + tm:HALO + ext_rows, :] = jnp.zeros((NSLAB, ext_rows - tm, LANES), F32)
            dcw_acc[...] = jnp.zeros((32, 8, WA), F32)
            vg_ref[...] = jnp.zeros((8, D), F32)

        _to_slabs(glu_ext, 0, jnp.where(first, 0.0, halo_ref[:, 0:WA] * _sigmoid(halo_ref[:, WA:2 * WA])))
        _to_slabs(glu_ext, HALO, a * sgg)
        _to_slabs(dconv_ext, 0, dpart_ref[:, 2 * WA:3 * WA])
        _to_slabs(dconv_ext, tm, jnp.where(last, 0.0, dnext_ref[...]))
        sub = lax.broadcasted_iota(jnp.int32, (SUBL, LANES), 0)
        for s in range(NSLAB):
            accs = [jnp.zeros((SUBL, LANES), F32)] * CONV_K
            for v in range(p):
                dc = jnp.where(v + p * sub < tm, dconv_ext[s, pl.ds(v, 8, stride=p), :], 0.0)
                for k in range(CONV_K):
                    accs[k] = accs[k] + dc * glu_ext[s, pl.ds(v + off + k, 8, stride=p), :]
            for k in range(CONV_K):
                dcw_acc[k, :, _lanes(s)] += accs[k]
        dglu = _tap_sum(dconv_ext, dglu_scr, cw_ref, jnp.zeros((1, WA), F32), tm, lambda k: (CONV_K - 1) - k)

        @pl.when(i == nt - 1)
        def _():
            for k in range(CONV_K):
                dcw_ref[k:k + 1, :] = jnp.sum(dcw_acc[k], axis=0, keepdims=True)
            dcw_ref[CONV_K:32, :] = jnp.zeros((32 - CONV_K, WA), F32)

        da = dglu * sgg
        dgg = dglu * a * (sgg * (1.0 - sgg))
        dproj_ref[:, 0:2 * WA] = dpart_ref[:, 0:2 * WA].astype(BF16)
        dproj_ref[:, 2 * WA:3 * WA] = da.astype(BF16)
        dproj_ref[:, 3 * WA:4 * WA] = dgg.astype(BF16)
        dh = jnp.zeros((tm, D), F32)
        for j in range(NDEV):
            dh = dh + _dot_nt(dproj_ref[:, j * MB:(j + 1) * MB], wmi_ref[j])
        xv = x_ref[...]
        sc, sh = mod_ref[1:2, :], mod_ref[0:1, :]
        gpre = g_ref[0:1, :]
        r = lax.rsqrt(_rowmean(xv * xv) + EPS)
        xh = xv * r
        n = xh * gpre
        hb_ref[...] = (n * (1.0 + sc) + sh).astype(BF16)
        d_sc = _colsum(dh * n)
        d_sh = _colsum(dh)
        dn = dh * (1.0 + sc)
        d_gpre = _colsum(dn * xh)
        dxh = dn * gpre
        dx_ref[...] = dxo_ref[...] + r * (dxh - xh * _rowmean(dxh * xh))

        @pl.when(first)
        def _():
            mg_ref[...] = jnp.zeros((8, D), F32)

        mg_ref[0:1, :] += d_sh
        mg_ref[1:2, :] += d_sc
        vg_ref[0:1, :] += d_gpre

    tile = pl.BlockSpec((tm, D), lambda i: (i, 0))
    return _call(
        core, name=name, grid=(nt,), jobs=jobs,
        in_specs=[tile, tile, pl.BlockSpec((tm, 4 * WA), lambda i: (i, 0)),
                  pl.BlockSpec((HALO, WA), lambda i: (jnp.minimum((i + 1) * hpt, nh - 1), 2)),
                  pl.BlockSpec((tm, 2 * WA), lambda i: (i, 1)),
                  pl.BlockSpec((HALO, 2 * WA), lambda i: (jnp.maximum(i * hpt - 1, 0), 1)),
                  pl.BlockSpec((None, 8, D), lambda i: (i // tps, 0, 0)), _const_spec((8, D)),
                  _const_spec((NDEV, D, MB)), _const_spec((32, WA))],
        out_specs=[tile, pl.BlockSpec((tm, 4 * WA), lambda i: (i, 0)), tile,
                   pl.BlockSpec((None, 8, D), lambda i: (i // tps, 0, 0)), pl.BlockSpec((8, D), lambda i: (0, 0)),
                   pl.BlockSpec((32, WA), lambda i: (0, 0))],
        out_shape=[jax.ShapeDtypeStruct((T, D), F32), jax.ShapeDtypeStruct((T, 4 * WA), BF16),
                   jax.ShapeDtypeStruct((T, D), BF16), jax.ShapeDtypeStruct((nb, 8, D), F32),
                   jax.ShapeDtypeStruct((8, D), F32), jax.ShapeDtypeStruct((32, WA), F32)],
        scratch=[pltpu.VMEM((NSLAB, HALO + ext_rows, LANES), F32), pltpu.VMEM((NSLAB, HALO + ext_rows, LANES), F32),
                 pltpu.VMEM((NSLAB, ext_rows, LANES), F32), pltpu.VMEM((32, 8, WA), F32)],
        args=[dxo, x, dpart, dpart, proj, proj, mod, gvec, w_mi, cw])


def _grad_chip(a, b, a_spec, b_spec, prod_shape, half, name, jobs=(), via_b=False):
    steps = 8 if half is None else 4
    R = prod_shape[0] if half is None else half
    C = prod_shape[1]

    def core(ins, outs, scs):
        a_ref, b_ref = ins
        (o_ref,) = outs
        own, snd, rcv, ssem, rsem, lsem = scs
        s = pl.program_id(0)
        c = lax.axis_index("c")
        me = _me()
        sib = _flip(me, (0, 0, 1))
        if via_b:
            prod = _dot_tn(b_ref[...], a_ref[...]).T.astype(BF16)
        else:
            prod = _dot_tn(a_ref[...], b_ref[...]).astype(BF16)
        if half is None:
            q = s // 2

            @pl.when(s % 2 == c)
            def _():
                own[q] = prod

            @pl.when(s % 2 != c)
            def _():
                snd[q] = prod
                _remote(snd.at[q], rcv.at[q], ssem.at[q], rsem.at[q], sib).start()
        else:
            lo = prod[0:half, :]
            hi = prod[half:2 * half, :]
            own[s] = jnp.where(c == 0, lo, hi)
            snd[s] = jnp.where(c == 0, hi, lo)
            _remote(snd.at[s], rcv.at[s], ssem.at[s], rsem.at[s], sib).start()

        @pl.when(s == steps - 1)
        def _():
            for q4 in range(4):
                cp = _remote(snd.at[q4], rcv.at[q4], ssem.at[q4], rsem.at[q4], sib)
                cp.wait_recv()
                cp.wait_send()
                snd[q4] = (own[q4].astype(F32) + rcv[q4].astype(F32)).astype(BF16)
            out = pltpu.make_async_copy(snd, o_ref, lsem)
            out.start()
            out.wait()

    return _call(
        core, name=name, grid=(steps,), jobs=jobs, in_specs=[a_spec, b_spec], out_specs=[HBM],
        out_shape=[jax.ShapeDtypeStruct((4, R, C), BF16)],
        scratch=[pltpu.VMEM((4, R, C), BF16), pltpu.VMEM((4, R, C), BF16), pltpu.VMEM((4, R, C), BF16),
                 pltpu.SemaphoreType.DMA((4,)), pltpu.SemaphoreType.DMA((4,)), pltpu.SemaphoreType.DMA],
        args=[a, b])


def _grad_w_in(dg, hb, name, jobs=()):
    T = hb.shape[0]
    return _grad_chip(dg, hb, pl.BlockSpec((None, T, FB), lambda s: (s, 0, 0)), _const_spec((T, D)),
                      (FB, D), None, name, jobs)


def _grad_w_out(act, dyb, name, jobs=()):
    T = dyb.shape[0]
    return _grad_chip(act, dyb, pl.BlockSpec((None, T, FB), lambda s: (s, 0, 0)), _const_spec((T, D)),
                      (FB, D), FO, name, jobs)


def _grad_w_mi(hb, dproj, name, jobs=()):
    T = hb.shape[0]
    return _grad_chip(hb, dproj, _const_spec((T, D)), pl.BlockSpec((T, MB), lambda s: (0, s)),
                      (D, MB), None, name, jobs, via_b=True)


def _grad_w_mo(ycat, dym, name, jobs=()):
    T = ycat.shape[0]
    return _grad_chip(ycat, dym, pl.BlockSpec((T, 2 * MO), lambda s: (0, s)), _const_spec((T, D)),
                      (2 * MO, D), MO, name, jobs)


def _adamw_math(w, g, m, v):
    m2 = ADAM_B1 * m + (1.0 - ADAM_B1) * g
    v2 = ADAM_B2 * v + (1.0 - ADAM_B2) * (g * g)
    m_hat = m2 / (1.0 - ADAM_B1 ** ADAM_STEP)
    v_hat = v2 / (1.0 - ADAM_B2 ** ADAM_STEP)
    delta = -ADAM_LR * (m_hat / (jnp.sqrt(v_hat) + ADAM_EPS) + ADAM_WD * w)
    return delta, m2, v2


def _adamw_reduce(parts, w, m, v, tr, name, own=None, after=None):
    R, C = w.shape

    def core(ins, outs, _):
        p_ref, w_ref, m_ref, v_ref = ins[:4]
        g_ref, d_ref, m2_ref, v2_ref = outs
        if own is None:
            terms = [p_ref[s].astype(F32) for s in range(4)]
        else:
            mq = 2 * lax.axis_index("x") + lax.axis_index("y")
            mine = ins[4][...].astype(F32)
            terms = [jnp.where(mq == s, mine, p_ref[s].astype(F32)) for s in range(4)]
        g = terms[0]
        for s in range(1, 4):
            g = g + terms[s]
        g_ref[...] = g
        d_ref[...], m2_ref[...], v2_ref[...] = _adamw_math(w_ref[...], g, m_ref[...], v_ref[...])

    blk = pl.BlockSpec((tr, C), lambda i: (i, 0))
    in_specs = [pl.BlockSpec((4, tr, C), lambda i: (0, i, 0)), blk, blk, blk]
    args = [parts, w, m, v]
    if own is not None:
        mq = 2 * lax.axis_index("x") + lax.axis_index("y")
        in_specs.append(pl.BlockSpec((tr, C), lambda i: (i, 0)))
        args.append(lax.dynamic_index_in_dim(own, mq, 0, keepdims=False))
    if after is not None:
        in_specs.append(HBM)
        args.append(after)
    return _call(
        core, name=name, grid=(R // tr,), in_specs=in_specs,
        out_specs=[blk, blk, blk, blk], out_shape=[jax.ShapeDtypeStruct((R, C), F32)] * 4, args=args)[0]


HBM_ONLY = pl.BlockSpec(memory_space=pltpu.HBM)
SEM = pl.BlockSpec(memory_space=pltpu.SEMAPHORE)
EFFECT = pltpu.SideEffectType.DATAFLOW_SIDE_EFFECTING


def _chip_scatter_start(gs, name):
    n = len(gs)

    def body(*refs):
        g_refs, land_refs = refs[:n], refs[n:2 * n]
        ssem, rsem = refs[2 * n:2 * n + 2]
        token = refs[-1]
        me = _me()
        mq = 2 * me[0] + me[1]
        for k, f in enumerate(CHIP_FLIPS):
            p = _flip(me, f)
            for a in range(n):
                _remote(g_refs[a].at[2 * p[0] + p[1]], land_refs[a].at[mq], ssem.at[3 * a + k], rsem.at[3 * a + k], p).start()
        token[...] = jnp.zeros_like(token)

    gs = [pltpu.with_memory_space_constraint(g, pltpu.HBM) for g in gs]
    lands = [pltpu.with_memory_space_constraint(lax.empty(g.shape, g.dtype), pltpu.HBM) for g in gs]
    res = pl.pallas_call(
        body, name=name,
        out_shape=(pltpu.SemaphoreType.DMA((3 * n,)), pltpu.SemaphoreType.DMA((3 * n,)))
        + tuple(pltpu.HBM(g.shape, g.dtype) for g in gs) * 2 + (jax.ShapeDtypeStruct((SUBL, LANES), F32),),
        in_specs=(HBM_ONLY,) * (2 * n), out_specs=(SEM, SEM) + (HBM_ONLY,) * (2 * n) + (VM,),
        input_output_aliases={a: 2 + a for a in range(2 * n)},
        compiler_params=pltpu.CompilerParams(has_side_effects=EFFECT),
    )(*gs, *lands)
    return res[:-1], res[-1]


def _chip_scatter_wait(handle, after, name):
    ssem, rsem = handle[:2]
    n = (len(handle) - 2) // 2
    thru = handle[2:]

    def body(*refs):
        g_refs, land_refs = refs[:n], refs[n:2 * n]
        ssem, rsem = refs[2 * n:2 * n + 2]
        me = _me()
        mq = 2 * me[0] + me[1]
        for k, f in enumerate(CHIP_FLIPS):
            p = _flip(me, f)
            pq = 2 * p[0] + p[1]
            for a in range(n):
                _remote(g_refs[a].at[pq], land_refs[a].at[mq], ssem.at[3 * a + k], rsem.at[3 * a + k], p).wait_send()
                _remote(g_refs[a].at[mq], land_refs[a].at[pq], ssem.at[3 * a + k], rsem.at[3 * a + k], p).wait_recv()

    res = pl.pallas_call(
        body, name=name,
        out_shape=tuple(pltpu.HBM(t.shape, t.dtype) for t in thru),
        in_specs=(HBM_ONLY,) * (2 * n) + (SEM, SEM, HBM), out_specs=(HBM_ONLY,) * (2 * n),
        input_output_aliases={a: a for a in range(2 * n)},
        compiler_params=pltpu.CompilerParams(has_side_effects=EFFECT),
    )(*thru, ssem, rsem, after)
    return list(res[:n]), list(res[n:])


def _adamw_ada(sc_all, dd, w, m, v, tr, name, after=None):
    R, C = w.shape

    def core(ins, outs, _):
        sc_ref, dd_ref, w_ref, m_ref, v_ref = ins[:5]
        g_ref, d_ref, m2_ref, v2_ref = outs
        g = _dot_tn(sc_ref[...].astype(BF16), dd_ref[...].astype(BF16))
        g_ref[...] = g
        d_ref[...], m2_ref[...], v2_ref[...] = _adamw_math(w_ref[...], g, m_ref[...], v_ref[...])

    blk = pl.BlockSpec((tr, C), lambda i: (i, 0))
    return _call(
        core, name=name, grid=(R // tr,),
        in_specs=[pl.BlockSpec((64, tr), lambda i: (0, i)), pl.BlockSpec((64, C), lambda i: (0, 0)), blk, blk, blk]
        + [HBM] * (after is not None),
        out_specs=[blk, blk, blk, blk], out_shape=[jax.ShapeDtypeStruct((R, C), F32)] * 4,
        args=[sc_all, dd, w, m, v] + [after] * (after is not None))[0]


def _adamw_small(gathered, plain, grads, wmv, emit, name):
    nw = len(grads)
    ng, npl, ne = len(gathered), len(plain), len(emit)

    def core(ins, outs, _):
        srcs = []
        for a in range(ng):
            s = ins[a][0]
            for dev in range(1, NDEV):
                s = s + ins[a][dev]
            srcs.append(s)
        srcs += [ins[ng + a][...] for a in range(npl)]
        w_refs = ins[ng + npl:]
        for e, a in enumerate(emit):
            outs[e][...] = srcs[a]
        for t in range(nw):
            src, row = grads[t]
            g = srcs[src] if row is None else srcs[src][row:row + 1, :]
            w_ref, m_ref, v_ref = w_refs[3 * t:3 * t + 3]
            g_ref, d_ref, m2_ref, v2_ref = outs[ne + 4 * t:ne + 4 * t + 4]
            g_ref[...] = g
            d_ref[...], m2_ref[...], v2_ref[...] = _adamw_math(w_ref[...], g, m_ref[...], v_ref[...])

    out_shape = [jax.ShapeDtypeStruct(gathered[a].shape[1:], F32) for a in emit]
    for t in range(nw):
        out_shape += [jax.ShapeDtypeStruct(wmv[3 * t].shape, F32)] * 4
    return _call(
        core, name=name, grid=(), in_specs=[VM] * (ng + npl + 3 * nw), out_specs=[VM] * (ne + 4 * nw),
        out_shape=out_shape, args=list(gathered) + list(plain) + list(wmv))[0]


def _ada_fwd(c_pad, w_ada, b_cols, cw_pad, jobs=()):
    def core(ins, outs, scs):
        c_ref, w_ref, b_ref, cwp_ref = ins
        ada_ref, sc_ref, cw_ref = outs
        cbuf, send_buf, ssem, rsem = scs
        me = _me()
        mi = _lin(me)
        cbuf[mi] = c_ref[...]
        cw_ref[mi] = cwp_ref[...]
        peers = [_flip(me, f) for f in FLIPS]
        first = []
        for k, p in enumerate(peers):
            first.append(_remote(cbuf.at[mi], cbuf.at[mi], ssem.at[k], rsem.at[k], p))
            first.append(_remote(cw_ref.at[mi], cw_ref.at[mi], ssem.at[7 + k], rsem.at[7 + k], p))
        for cp in first:
            cp.start()
        for k, p in enumerate(peers):
            pi = _lin(p)
            _remote(cbuf.at[pi], cbuf.at[pi], ssem.at[k], rsem.at[k], p).wait_recv()
            _remote(cw_ref.at[pi], cw_ref.at[pi], ssem.at[7 + k], rsem.at[7 + k], p).wait_recv()
        c_all = cbuf[...].reshape(8 * 8, D)
        sc = c_all * _sigmoid(c_all)
        sc_ref[...] = sc
        res = _dot(sc.astype(BF16), w_ref[...].astype(BF16)) + b_ref[...]
        send_buf[...] = res.reshape(8, 8, ADA_B)
        ada_ref[mi] = send_buf[mi]
        second = []
        for k, p in enumerate(peers):
            second.append(_remote(send_buf.at[_lin(p)], ada_ref.at[mi], ssem.at[14 + k], rsem.at[14 + k], p))
        for cp in second:
            cp.start()
        for k, p in enumerate(peers):
            _remote(send_buf.at[mi], ada_ref.at[_lin(p)], ssem.at[14 + k], rsem.at[14 + k], p).wait_recv()
        for cp in first + second:
            cp.wait_send()

    return _call(
        core, name="ada_fwd", grid=(), jobs=jobs, in_specs=[VM, VM, VM, VM], out_specs=[VM, VM, VM],
        out_shape=[jax.ShapeDtypeStruct((8, 8, ADA_B), F32), jax.ShapeDtypeStruct((64, D), F32),
                   jax.ShapeDtypeStruct((8, 32, 64), F32)],
        scratch=[pltpu.VMEM((8, 8, D), F32), pltpu.VMEM((8, 8, ADA_B), F32),
                 pltpu.SemaphoreType.DMA((21,)), pltpu.SemaphoreType.DMA((21,))],
        args=[c_pad, w_ada, b_cols, cw_pad])


def _ada_bwd(dada, jobs=()):
    def core(ins, outs, scs):
        (d_ref,) = ins
        dd_ref, gb_ref = outs
        rbuf, ssem, rsem = scs
        me = _me()
        mi = _lin(me)
        peers = [_flip(me, f) for f in FLIPS]
        rbuf[mi] = d_ref[mi]
        first = []
        for k, p in enumerate(peers):
            first.append(_remote(d_ref.at[_lin(p)], rbuf.at[mi], ssem.at[k], rsem.at[k], p))
        for cp in first:
            cp.start()
        for k, p in enumerate(peers):
            _remote(d_ref.at[mi], rbuf.at[_lin(p)], ssem.at[k], rsem.at[k], p).wait_recv()
        dd = rbuf[...].reshape(64, ADA_B)
        dd_ref[...] = dd
        gb_ref[mi] = jnp.broadcast_to(_colsum(dd), (8, ADA_B))
        second = []
        for k, p in enumerate(peers):
            second.append(_remote(gb_ref.at[mi], gb_ref.at[mi], ssem.at[7 + k], rsem.at[7 + k], p))
        for cp in second:
            cp.start()
        for k, p in enumerate(peers):
            pi = _lin(p)
            _remote(gb_ref.at[pi], gb_ref.at[pi], ssem.at[7 + k], rsem.at[7 + k], p).wait_recv()
        for cp in first + second:
            cp.wait_send()

    return _call(
        core, name="ada_bwd", grid=(), jobs=jobs, in_specs=[VM], out_specs=[VM, VM],
        out_shape=[jax.ShapeDtypeStruct((64, ADA_B), F32), jax.ShapeDtypeStruct((8, 8, ADA_B), F32)],
        scratch=[pltpu.VMEM((8, 8, ADA_B), F32), pltpu.SemaphoreType.DMA((14,)), pltpu.SemaphoreType.DMA((14,))],
        args=[dada])


SMALL_D = ("g_pre_f1", "g_post_f1", "g_pre_m", "g_post_m", "g_pre_f2", "g_post_f2")
SMALL_W = ("gmlp_norm_g", "gmlp_norm_b", "conv_b", "conv_norm_g", "conv_norm_b", "g_out_a", "g_out_b")


def kernel(x, c, w_ada, b_ada, g_pre_f1, g_post_f1, w_f1_in, w_f1_out, g_pre_m, g_post_m, w_mix_in, gmlp_norm_g, gmlp_norm_b, w_spatial, b_spatial, conv_w, conv_b, conv_norm_g, conv_norm_b, g_out_a, g_out_b, w_mix_out, g_pre_f2, g_post_f2, w_f2_in, w_f2_out, loss_target, m_w_ada, m_b_ada, m_g_pre_f1, m_g_post_f1, m_w_f1_in, m_w_f1_out, m_g_pre_m, m_g_post_m, m_w_mix_in, m_gmlp_norm_g, m_gmlp_norm_b, m_w_spatial, m_b_spatial, m_conv_w, m_conv_b, m_conv_norm_g, m_conv_norm_b, m_g_out_a, m_g_out_b, m_w_mix_out, m_g_pre_f2, m_g_post_f2, m_w_f2_in, m_w_f2_out, v_w_ada, v_b_ada, v_g_pre_f1, v_g_post_f1, v_w_f1_in, v_w_f1_out, v_g_pre_m, v_g_post_m, v_w_mix_in, v_gmlp_norm_g, v_gmlp_norm_b, v_w_spatial, v_b_spatial, v_conv_w, v_conv_b, v_conv_norm_g, v_conv_norm_b, v_g_out_a, v_g_out_b, v_w_mix_out, v_g_pre_f2, v_g_post_f2, v_w_f2_in, v_w_f2_out):
    given = dict(locals())
    bl, seq, _ = x.shape
    T = bl * seq
    tm = min(256, seq // 2)
    mi = _lin((lax.axis_index("x"), lax.axis_index("y"), lax.axis_index("c")))

    def shard_in(w):
        return w[0].T.astype(BF16)

    g_f1 = _Gather([shard_in(w_f1_in), w_f1_out[0].astype(BF16)], ("rows", "out"))
    s_f2 = shard_in(w_f2_in)
    g_mx = _Gather([w_mix_in[0].astype(BF16), w_mix_out[0].astype(BF16), w_f2_out[0].astype(BF16), s_f2[:, 0:D // 4]],
                   ("rows", "rows", "out", "rows"), late_mid=True)
    g_f2 = _Gather([s_f2[:, D // 4:D]], ("rows",))

    c_pad = jnp.pad(c, ((0, 8 - bl), (0, 0)))
    b_cols = lax.dynamic_slice(b_ada, (0, mi * ADA_B), (1, ADA_B))
    cw_pad = jnp.pad(conv_w[0], ((0, 1), (0, 0)))
    (ada_blk, sc_all, cw_all), ((wi1, wo1),) = _ada_fwd(c_pad, w_ada[0], b_cols, cw_pad, jobs=[g_f1])
    ada = ada_blk[:, 0:bl, :].transpose(1, 0, 2).reshape(bl, 9, D)
    pad5 = jnp.zeros((bl, 5, D), F32)
    mod1 = jnp.concatenate([ada[:, 0:3], pad5], axis=1)
    mod2 = jnp.concatenate([ada[:, 3:6], pad5], axis=1)
    mod3 = jnp.concatenate([ada[:, 6:9], pad5], axis=1)
    cw_full = cw_all.transpose(1, 0, 2).reshape(32, WA)

    zrow = jnp.zeros((1, D), F32)
    gv1 = jnp.concatenate([g_pre_f1, g_post_f1] + [zrow] * 6, axis=0)
    gvm = jnp.concatenate([g_pre_m, g_post_m] + [zrow] * 6, axis=0)
    gv2 = jnp.concatenate([g_pre_f2, g_post_f2] + [zrow] * 6, axis=0)
    v512 = jnp.concatenate([gmlp_norm_g, gmlp_norm_b, conv_b, conv_norm_g, conv_norm_b, g_out_a, g_out_b,
                            jnp.zeros((1, WA), F32)], axis=0)
    ws = w_spatial[0]
    bias_full = jnp.repeat(b_spatial[0].T, HD, axis=1)
    esel = (lax.broadcasted_iota(jnp.int32, (8, WA), 1) // HD == lax.broadcasted_iota(jnp.int32, (8, WA), 0)).astype(F32)

    x0 = x.reshape(T, D)
    (x1, gu1, y1), ((wmi, wmo, wo2, wi2a),) = _ffn_fwd(x0, mod1, gv1, wi1, wo1, tm, "ffn1_fwd", jobs=[g_mx])
    wmo = wmo.reshape(D, D)
    (x2, proj, ym, conv), ((wi2b,),) = _mixer_fwd(x1, mod2, gvm, wmi, wmo, v512, ws, bias_full, cw_full, tm, "mixer_fwd", jobs=[g_f2])

    (dx2, dg2, act2, hb2, dyb2, mg3, vg3, loss_blk), _ = _ffn_last(
        x2, loss_target.reshape(T, D), mod3, gv2, (wi2a, wi2b), wo2, tm, "ffn2_fwd_bwd")
    (g_wi2,), _ = _grad_w_in(dg2, hb2, "ffn2_gw_in")
    (g_wo2,), _ = _grad_w_out(act2, dyb2, "ffn2_gw_out")
    (dpart, dymb, ycat, mg2a, vgma, v5g, gws, gbs), ((p_wo2,),) = _mixer_bwd_a(
        dx2, ym, proj, conv, mod2, gvm, wmo, v512, ws, bias_full, esel, tm, "mixer_bwd_a",
        jobs=[_ChipScatter([g_wo2])])
    (dx1, dproj, hbm, mg2b, vgmb, dcw), ((p_wi2,),) = _mixer_bwd_b(
        dx2, x1, dpart, proj, mod2, gvm, wmi, cw_full, tm, "mixer_bwd_b", jobs=[_ChipScatter([g_wi2])])
    (g_wmi,), _ = _grad_w_mi(hbm, dproj, "mixer_gw_in")
    (g_wmo,), _ = _grad_w_mo(ycat, dymb, "mixer_gw_out")
    p2 = jnp.concatenate([v5g, dcw], axis=0)
    (dx0, dg1, act1, hb1, dyb1, mg1, vg1), _ = _ffn_bwd(dx1, x0, y1, gu1, mod1, gv1, wi1, wo1, tm, "ffn1_bwd")

    dada = jnp.concatenate([mg1[:, 0:3], mg2b[:, 0:2], mg2a[:, 2:3], mg3[:, 0:3]], axis=1)
    dada = dada.reshape(bl, NDEV, ADA_B).transpose(1, 0, 2)
    dada = jnp.pad(dada, ((0, 0), (0, 8 - bl), (0, 0)))
    p1 = jnp.concatenate([vg1[0:2], vgmb[0:1], vgma[1:2], vg3[0:2], loss_blk[0:1], zrow], axis=0)
    (dd_all, gb_all), ((a1,),) = _ada_bwd(dada, jobs=[_AllGather([p1])])
    g_bada = gb_all[:, 0, :].reshape(1, 9 * D)

    (g_wo1,), ((p_wmi, p_wmo),) = _grad_w_out(act1, dyb1, "ffn1_gw_out", jobs=[_ChipScatter([g_wmi, g_wmo])])
    (g_wi1,), ((a2, a3, a4), (p_wo1,)) = _grad_w_in(
        dg1, hb1, "ffn1_gw_in", jobs=[_Gather([p2, gws, gbs], ("rows",) * 3), _ChipScatter([g_wo1])])

    h_f1, token = _chip_scatter_start([g_wi1], "tail_start")

    res = {}
    quad = _adamw_reduce(p_wi2, w_f2_in[0].T, m_w_f2_in[0].T, v_w_f2_in[0].T, FO, "adamw_w_f2_in", after=token)
    res["w_f2_in"] = tuple(t.T[None] for t in quad)
    for nm, part, tr in (("w_f2_out", p_wo2, FO), ("w_mix_in", p_wmi, 256), ("w_mix_out", p_wmo, MO), ("w_f1_out", p_wo1, FO)):
        quad = _adamw_reduce(part, given[nm][0], given["m_" + nm][0], given["v_" + nm][0], tr, "adamw_" + nm, after=quad[1])
        res[nm] = tuple(t[None] for t in quad)
    quad = _adamw_ada(sc_all, dd_all, w_ada[0], m_w_ada[0], v_w_ada[0], 256, "adamw_w_ada", after=quad[1])
    res["w_ada"] = tuple(t[None] for t in quad)
    (g_wi1,), (p_wi1,) = _chip_scatter_wait(h_f1, quad[1], "tail_wait")
    quad = _adamw_reduce(p_wi1, w_f1_in[0].T, m_w_f1_in[0].T, v_w_f1_in[0].T, FO, "adamw_w_f1_in", own=g_wi1)
    res["w_f1_in"] = tuple(t.T[None] for t in quad)

    small = SMALL_D + SMALL_W + ("w_spatial", "b_spatial", "b_ada")
    grads = [(0, r) for r in range(6)] + [(1, r) for r in range(7)] + [(2, None), (3, None), (4, None)]
    wmv = []
    for nm in small:
        for pre in ("", "m_", "v_"):
            wmv.append(given[pre + nm][0] if nm in ("w_spatial", "b_spatial") else given[pre + nm])
    outs = _adamw_small([a1, a2, a3, a4], [g_bada], grads, wmv, (0, 1), "adamw_small")
    loss = outs[0][6, 0]
    for t, nm in enumerate(small):
        quad = outs[2 + 4 * t:6 + 4 * t]
        res[nm] = tuple(q[None] for q in quad) if nm in ("w_spatial", "b_spatial") else tuple(quad)
    g_cw = lax.dynamic_slice(outs[1], (8, mi * 64), (32, 64))
    wmv = [jnp.pad(given[pre + "conv_w"][0], ((0, 1), (0, 0)), constant_values=1.0 if pre == "v_" else 0.0)
           for pre in ("", "m_", "v_")]
    quad = _adamw_small([], [g_cw], [(0, None)], wmv, (), "adamw_conv_w")
    res["conv_w"] = tuple(q[0:CONV_K][None] for q in quad)

    order = ["w_ada", "b_ada", "g_pre_f1", "g_post_f1", "w_f1_in", "w_f1_out", "g_pre_m", "g_post_m", "w_mix_in",
             "gmlp_norm_g", "gmlp_norm_b", "w_spatial", "b_spatial", "conv_w", "conv_b", "conv_norm_g", "conv_norm_b",
             "g_out_a", "g_out_b", "w_mix_out", "g_pre_f2", "g_post_f2", "w_f2_in", "w_f2_out"]
    out = [loss, dx0.reshape(bl, seq, D)]
    for k in range(4):
        out += [res[nm][k] for nm in order]
    return tuple(out)
```

```python
import jax
import jax.numpy as jnp
from jax import lax
from jax.experimental import pallas as pl
from jax.experimental.pallas import tpu as pltpu

F32 = jnp.float32
BF16 = jnp.bfloat16

D = 1024
DFF = 2816
NDEV = 8
FB = 2 * DFF // NDEV
NCH = DFF // FB
LANES = 128
SUBL = 8
FO = DFF // NDEV
WA = 512
NSLAB = WA // LANES
NHEAD = 8
HD = 64
CHUNK = 128
CONV_K = 31
HALO = 32
MB = 2 * (WA + WA) // NDEV
MO = D // NDEV
ADA_B = 9 * D // NDEV
EPS = 1e-6
HALF = 0.5

ADAM_LR = 0.001
ADAM_B1 = 0.9
ADAM_B2 = 0.999
ADAM_EPS = 1e-08
ADAM_WD = 0.01
ADAM_STEP = 10

VMEM_LIMIT = 56 * 1024 * 1024
MESH = pl.DeviceIdType.MESH
FLIPS = ((0, 0, 1), (1, 0, 0), (0, 1, 0), (1, 1, 0), (1, 0, 1), (0, 1, 1), (1, 1, 1))
CHIP_FLIPS = ((1, 0, 0), (0, 1, 0), (1, 1, 0))
HBM = pl.BlockSpec(memory_space=pl.ANY)
VM = pl.BlockSpec(memory_space=pltpu.VMEM)


def _dot(a, b):
    return lax.dot_general(a, b, (((1,), (0,)), ((), ())), preferred_element_type=F32)


def _dot_nt(a, b):
    return lax.dot_general(a, b, (((1,), (1,)), ((), ())), preferred_element_type=F32)


def _dot_tn(a, b):
    return lax.dot_general(a, b, (((0,), (0,)), ((), ())), preferred_element_type=F32)


def _rowmean(v):
    return jnp.mean(v, axis=-1, keepdims=True)


def _colsum(v):
    return jnp.sum(v, axis=0, keepdims=True)


def _sigmoid(v):
    return 0.5 * jnp.tanh(0.5 * v) + 0.5


def _const_spec(shape):
    nd = len(shape)
    return pl.BlockSpec(shape, lambda *_: (0,) * nd, pipeline_mode=pl.Buffered(1))


def _me():
    return lax.axis_index("x"), lax.axis_index("y"), lax.axis_index("c")


def _flip(me, f):
    return tuple(1 - v if b else v for v, b in zip(me, f))


def _lin(p):
    return 4 * p[0] + 2 * p[1] + p[2]


def _remote(src, dst, send_sem, recv_sem, dev):
    return pltpu.make_async_remote_copy(src_ref=src, dst_ref=dst, send_sem=send_sem, recv_sem=recv_sem,
                                        device_id=dev, device_id_type=MESH)


def _blk(kind, ref, p):
    if kind == "out":
        return ref.at[2 * p[0] + p[1], pl.ds(p[2] * FO, FO), :]
    return ref.at[_lin(p)]


class _Gather:
    def __init__(self, shards, kinds, late_mid=False):
        self.late_mid = late_mid
        self.kinds = kinds
        self.n = len(shards)
        self.ins = list(shards)
        self.out_shape = [jax.ShapeDtypeStruct((4, FB, D) if k == "out" else (NDEV,) + s.shape, s.dtype)
                          for s, k in zip(shards, kinds)]
        self.sems = [pltpu.SemaphoreType.DMA((7 * self.n,)), pltpu.SemaphoreType.DMA((7 * self.n,)),
                     pltpu.SemaphoreType.DMA((self.n,))]

    def _first(self, ins, outs, sems):
        ssem, rsem, lsem = sems
        me = _me()
        sib = _flip(me, (0, 0, 1))
        cps, loc = [], []
        for a in range(self.n):
            mine = _blk(self.kinds[a], outs[a], me)
            loc.append(pltpu.make_async_copy(ins[a], mine, lsem.at[a]))
            cps.append(_remote(ins[a], mine, ssem.at[7 * a], rsem.at[7 * a], sib))
            for j, f in enumerate(CHIP_FLIPS):
                cps.append(_remote(ins[a], mine, ssem.at[7 * a + 1 + j], rsem.at[7 * a + 1 + j], _flip(me, f)))
        return cps, loc

    def _passed(self, outs, sems):
        ssem, rsem, _ = sems
        me = _me()
        sib = _flip(me, (0, 0, 1))
        cps = []
        for j, f in enumerate(CHIP_FLIPS):
            for a in range(self.n):
                blk = _blk(self.kinds[a], outs[a], _flip(me, f))
                cps.append(_remote(blk, blk, ssem.at[7 * a + 4 + j], rsem.at[7 * a + 4 + j], sib))
        return cps

    def start(self, ins, outs, sems):
        cps, loc = self._first(ins, outs, sems)
        for cp in loc + cps:
            cp.start()

    def mid(self, ins, outs, sems):
        ssem, rsem, _ = sems
        me = _me()
        passed = self._passed(outs, sems)
        t = 0
        for j, f in enumerate(CHIP_FLIPS):
            for a in range(self.n):
                blk = _blk(self.kinds[a], outs[a], _flip(me, f))
                _remote(blk, blk, ssem.at[7 * a + 1 + j], rsem.at[7 * a + 1 + j], _flip(me, f)).wait_recv()
                passed[t].start()
                t += 1

    def end(self, ins, outs, sems):
        ssem, rsem, _ = sems
        me = _me()
        sib = _flip(me, (0, 0, 1))
        for a in range(self.n):
            blk = _blk(self.kinds[a], outs[a], sib)
            _remote(blk, blk, ssem.at[7 * a], rsem.at[7 * a], sib).wait_recv()
            for j, f in enumerate(CHIP_FLIPS):
                blk = _blk(self.kinds[a], outs[a], _flip(_flip(me, f), (0, 0, 1)))
                _remote(blk, blk, ssem.at[7 * a + 4 + j], rsem.at[7 * a + 4 + j], sib).wait_recv()
        cps, loc = self._first(ins, outs, sems)
        for cp in cps + self._passed(outs, sems):
            cp.wait_send()
        for cp in loc:
            cp.wait()


class _ChipScatter:
    def __init__(self, grads):
        self.n = len(grads)
        self.ins = list(grads)
        self.out_shape = [jax.ShapeDtypeStruct(g.shape, BF16) for g in grads]
        self.sems = [pltpu.SemaphoreType.DMA((3 * self.n,)), pltpu.SemaphoreType.DMA((3 * self.n,)),
                     pltpu.SemaphoreType.DMA((self.n,))]

    def _copies(self, ins, outs, sems):
        ssem, rsem, lsem = sems
        me = _me()
        mq = 2 * me[0] + me[1]
        loc = [pltpu.make_async_copy(ins[a].at[mq], outs[a].at[mq], lsem.at[a]) for a in range(self.n)]
        cps = []
        for k, f in enumerate(CHIP_FLIPS):
            p = _flip(me, f)
            for a in range(self.n):
                cps.append(_remote(ins[a].at[2 * p[0] + p[1]], outs[a].at[mq], ssem.at[3 * a + k], rsem.at[3 * a + k], p))
        return cps, loc

    def start(self, ins, outs, sems):
        cps, loc = self._copies(ins, outs, sems)
        for cp in loc + cps:
            cp.start()

    mid = None

    def end(self, ins, outs, sems):
        ssem, rsem, _ = sems
        me = _me()
        mq = 2 * me[0] + me[1]
        for k, f in enumerate(CHIP_FLIPS):
            p = _flip(me, f)
            for a in range(self.n):
                _remote(ins[a].at[mq], outs[a].at[2 * p[0] + p[1]], ssem.at[3 * a + k], rsem.at[3 * a + k], p).wait_recv()
        cps, loc = self._copies(ins, outs, sems)
        for cp in cps:
            cp.wait_send()
        for cp in loc:
            cp.wait()


class _AllGather:
    def __init__(self, parts):
        self.n = len(parts)
        self.ins = list(parts)
        self.out_shape = [jax.ShapeDtypeStruct((NDEV,) + p.shape, p.dtype) for p in parts]
        self.sems = [pltpu.SemaphoreType.DMA((7 * self.n,)), pltpu.SemaphoreType.DMA((7 * self.n,)),
                     pltpu.SemaphoreType.DMA((self.n,))]

    def _copies(self, ins, outs, sems):
        ssem, rsem, lsem = sems
        me = _me()
        mi = _lin(me)
        loc = [pltpu.make_async_copy(ins[a], outs[a].at[mi], lsem.at[a]) for a in range(self.n)]
        cps = []
        for k, f in enumerate(FLIPS):
            for a in range(self.n):
                cps.append(_remote(ins[a], outs[a].at[mi], ssem.at[7 * a + k], rsem.at[7 * a + k], _flip(me, f)))
        return cps, loc

    def start(self, ins, outs, sems):
        cps, loc = self._copies(ins, outs, sems)
        for cp in loc + cps:
            cp.start()

    mid = None

    def end(self, ins, outs, sems):
        ssem, rsem, _ = sems
        me = _me()
        for k, f in enumerate(FLIPS):
            p = _flip(me, f)
            for a in range(self.n):
                _remote(ins[a], outs[a].at[_lin(p)], ssem.at[7 * a + k], rsem.at[7 * a + k], p).wait_recv()
        cps, loc = self._copies(ins, outs, sems)
        for cp in cps:
            cp.wait_send()
        for cp in loc:
            cp.wait()


def _call(core, *, name, grid, in_specs, out_specs, out_shape, args, scratch=(), jobs=(), core_starts=False):
    n_in, n_out, n_sc = len(in_specs), len(out_specs), len(scratch)
    steps = 1
    for g in grid:
        steps *= g

    def body(*refs):
        pos = [0]

        def take(k):
            r = refs[pos[0]:pos[0] + k]
            pos[0] += k
            return r

        ins = take(n_in)
        j_ins = [take(len(j.ins)) for j in jobs]
        outs = take(n_out)
        j_outs = [take(len(j.out_shape)) for j in jobs]
        scs = take(n_sc)
        j_sems = [take(len(j.sems)) for j in jobs]
        if len(grid) == 2:
            step = pl.program_id(0) * grid[1] + pl.program_id(1)
        elif len(grid) == 1:
            step = pl.program_id(0)
        else:
            step = 0
        def start_jobs():
            for j, ji, jo, js in zip(jobs, j_ins, j_outs, j_sems):
                j.start(ji, jo, js)

        if grid:
            pl.when(step == 0)(start_jobs)
        elif not core_starts:
            start_jobs()
        for j, ji, jo, js in zip(jobs, j_ins, j_outs, j_sems):
            if j.mid is not None and grid:
                at = max(steps - 2, 0) if j.late_mid else (3 * steps) // 4
                pl.when(step == at)(lambda j=j, ji=ji, jo=jo, js=js: j.mid(ji, jo, js))
        if core_starts:
            core(ins, outs, scs, start_jobs)
        elif core is not None:
            core(ins, outs, scs)
        for j, ji, jo, js in zip(jobs, j_ins, j_outs, j_sems):
            if grid:
                pl.when(step == steps - 1)(lambda j=j, ji=ji, jo=jo, js=js: j.end(ji, jo, js))
            else:
                if j.mid is not None:
                    j.mid(ji, jo, js)
                j.end(ji, jo, js)

    all_in = list(in_specs)
    all_args = list(args)
    all_out = list(out_specs)
    all_shape = list(out_shape)
    all_sc = list(scratch)
    for j in jobs:
        all_in += [HBM] * len(j.ins)
        all_args += j.ins
    for j in jobs:
        all_out += [HBM] * len(j.out_shape)
        all_shape += j.out_shape
        all_sc += j.sems
    params = dict(vmem_limit_bytes=VMEM_LIMIT)
    if grid:
        params["dimension_semantics"] = ("arbitrary",) * len(grid)
    res = pl.pallas_call(
        body, name=name, grid=grid, in_specs=all_in, out_specs=all_out, out_shape=all_shape,
        scratch_shapes=all_sc, compiler_params=pltpu.CompilerParams(**params),
    )(*all_args)
    core_res = list(res[:n_out])
    job_res = []
    pos = n_out
    for j in jobs:
        job_res.append(list(res[pos:pos + len(j.out_shape)]))
        pos += len(j.out_shape)
    return core_res, job_res


def _ffn_fwd(x, mod, gvec, w_in, w_out, tm, name, jobs=()):
    T = x.shape[0]
    nt = T // tm
    tps = nt // mod.shape[0]

    def core(ins, outs, _):
        x_ref, mod_ref, g_ref, win_ref, wout_ref = ins
        xo_ref, gu_ref, y_ref = outs
        xv = x_ref[...]
        sh, sc, gt = mod_ref[0:1, :], mod_ref[1:2, :], mod_ref[2:3, :]
        r = lax.rsqrt(_rowmean(xv * xv) + EPS)
        h = (xv * r * g_ref[0:1, :]) * (1.0 + sc) + sh
        hb = h.astype(BF16)
        y = jnp.zeros((tm, D), F32)
        for cidx in range(NCH):
            gate = _dot_nt(hb, win_ref[cidx])
            up = _dot_nt(hb, win_ref[NCH + cidx])
            gu_ref[cidx] = gate.astype(BF16)
            gu_ref[NCH + cidx] = up.astype(BF16)
            act = gate * _sigmoid(gate) * up
            y = y + _dot(act.astype(BF16), wout_ref[cidx])
        y_ref[...] = y
        ry = lax.rsqrt(_rowmean(y * y) + EPS)
        xo_ref[...] = xv + (HALF * gt) * (y * ry * g_ref[1:2, :])

    tile = pl.BlockSpec((tm, D), lambda i: (i, 0))
    return _call(
        core, name=name, grid=(nt,), jobs=jobs,
        in_specs=[tile, pl.BlockSpec((None, 8, D), lambda i: (i // tps, 0, 0)), _const_spec((8, D)),
                  _const_spec((8, FB, D)), _const_spec((4, FB, D))],
        out_specs=[tile, pl.BlockSpec((8, tm, FB), lambda i: (0, i, 0)), tile],
        out_shape=[jax.ShapeDtypeStruct((T, D), F32), jax.ShapeDtypeStruct((8, T, FB), BF16),
                   jax.ShapeDtypeStruct((T, D), F32)],
        args=[x, mod, gvec, w_in, w_out])


def _ffn_bwd(dxo, x, y, gu, mod, gvec, w_in, w_out, tm, name, jobs=()):
    T = x.shape[0]
    nt = T // tm
    nb = mod.shape[0]
    tps = nt // nb

    def core(ins, outs, _):
        dxo_ref, x_ref, y_ref, gu_ref, mod_ref, g_ref, win_ref, wout_ref = ins
        dx_ref, dg_ref, act_ref, hb_ref, dyb_ref, mg_ref, vg_ref = outs
        i = pl.program_id(0)
        xv = x_ref[...]
        dxo_v = dxo_ref[...]
        yv = y_ref[...]
        sh, sc, gt = mod_ref[0:1, :], mod_ref[1:2, :], mod_ref[2:3, :]
        gpre, gpost = g_ref[0:1, :], g_ref[1:2, :]
        r = lax.rsqrt(_rowmean(xv * xv) + EPS)
        xh = xv * r
        n = xh * gpre
        hb = (n * (1.0 + sc) + sh).astype(BF16)
        hb_ref[...] = hb
        ry = lax.rsqrt(_rowmean(yv * yv) + EPS)
        yh = yv * ry
        d_gt = _colsum(HALF * dxo_v * (yh * gpost))
        dp = (HALF * gt) * dxo_v
        d_gpost = _colsum(dp * yh)
        dyh = dp * gpost
        dy = ry * (dyh - yh * _rowmean(dyh * yh))
        dyb = dy.astype(BF16)
        dyb_ref[...] = dyb
        dh = jnp.zeros((tm, D), F32)
        for cidx in range(NCH):
            gate = gu_ref[cidx].astype(F32)
            up = gu_ref[NCH + cidx].astype(F32)
            sig = _sigmoid(gate)
            s = gate * sig
            act_ref[cidx] = (s * up).astype(BF16)
            d_act = _dot_nt(dyb, wout_ref[cidx])
            d_up = (d_act * s).astype(BF16)
            d_gate = (d_act * up * (sig * (1.0 + gate * (1.0 - sig)))).astype(BF16)
            dg_ref[cidx] = d_gate
            dg_ref[NCH + cidx] = d_up
            dh = dh + _dot(d_gate, win_ref[cidx]) + _dot(d_up, win_ref[NCH + cidx])
        d_sc = _colsum(dh * n)
        d_sh = _colsum(dh)
        dn = dh * (1.0 + sc)
        d_gpre = _colsum(dn * xh)
        dxh = dn * gpre
        dx_ref[...] = dxo_v + r * (dxh - xh * _rowmean(dxh * xh))

        @pl.when(i % tps == 0)
        def _():
            mg_ref[...] = jnp.zeros((8, D), F32)

        @pl.when(i == 0)
        def _():
            vg_ref[...] = jnp.zeros((8, D), F32)

        mg_ref[0:1, :] += d_sh
        mg_ref[1:2, :] += d_sc
        mg_ref[2:3, :] += d_gt
        vg_ref[0:1, :] += d_gpre
        vg_ref[1:2, :] += d_gpost

    tile = pl.BlockSpec((tm, D), lambda i: (i, 0))
    return _call(
        core, name=name, grid=(nt,), jobs=jobs,
        in_specs=[tile, tile, tile, pl.BlockSpec((8, tm, FB), lambda i: (0, i, 0)),
                  pl.BlockSpec((None, 8, D), lambda i: (i // tps, 0, 0)), _const_spec((8, D)),
                  _const_spec((8, FB, D)), _const_spec((4, FB, D))],
        out_specs=[tile, pl.BlockSpec((8, tm, FB), lambda i: (0, i, 0)),
                   pl.BlockSpec((4, tm, FB), lambda i: (0, i, 0)), tile, tile,
                   pl.BlockSpec((None, 8, D), lambda i: (i // tps, 0, 0)), pl.BlockSpec((8, D), lambda i: (0, 0))],
        out_shape=[jax.ShapeDtypeStruct((T, D), F32), jax.ShapeDtypeStruct((8, T, FB), BF16),
                   jax.ShapeDtypeStruct((4, T, FB), BF16), jax.ShapeDtypeStruct((T, D), BF16),
                   jax.ShapeDtypeStruct((T, D), BF16), jax.ShapeDtypeStruct((nb, 8, D), F32),
                   jax.ShapeDtypeStruct((8, D), F32)],
        args=[dxo, x, y, gu, mod, gvec, w_in, w_out])


def _ffn_last(x, target, mod, gvec, w_in, w_out, tm, name, jobs=()):
    T = x.shape[0]
    nt = T // tm
    nb = mod.shape[0]
    tps = nt // nb

    def core(ins, outs, scs):
        x_ref, t_ref, mod_ref, g_ref, wina_ref, winb_ref, wout_ref = ins
        dx_ref, dg_ref, act_ref, hb_ref, dyb_ref, mg_ref, vg_ref, loss_ref = outs
        hd2 = w_in[0].shape[2]
        (gu_s,) = scs
        i = pl.program_id(0)
        xv = x_ref[...]
        sh, sc, gt = mod_ref[0:1, :], mod_ref[1:2, :], mod_ref[2:3, :]
        gpre, gpost = g_ref[0:1, :], g_ref[1:2, :]
        r = lax.rsqrt(_rowmean(xv * xv) + EPS)
        xh = xv * r
        n = xh * gpre
        hb = (n * (1.0 + sc) + sh).astype(BF16)
        hb_ref[...] = hb
        hba, hbb = hb[:, 0:hd2], hb[:, hd2:D]
        yv = jnp.zeros((tm, D), F32)
        for cidx in range(NCH):
            gate = _dot_nt(hba, wina_ref[cidx]) + _dot_nt(hbb, winb_ref[cidx])
            up = _dot_nt(hba, wina_ref[NCH + cidx]) + _dot_nt(hbb, winb_ref[NCH + cidx])
            gu_s[cidx] = gate.astype(BF16)
            gu_s[NCH + cidx] = up.astype(BF16)
            act = gate * _sigmoid(gate) * up
            act_ref[cidx] = act.astype(BF16)
            yv = yv + _dot(act_ref[cidx], wout_ref[cidx])
        ry = lax.rsqrt(_rowmean(yv * yv) + EPS)
        yh = yv * ry
        pn = yh * gpost
        err = xv + (HALF * gt) * pn - t_ref[...]
        dxo_v = err * (1.0 / D)
        d_gt = _colsum(HALF * dxo_v * pn)
        dp = (HALF * gt) * dxo_v
        d_gpost = _colsum(dp * yh)
        dyh = dp * gpost
        dyb = (ry * (dyh - yh * _rowmean(dyh * yh))).astype(BF16)
        dyb_ref[...] = dyb
        dha = jnp.zeros((tm, hd2), F32)
        dhb = jnp.zeros((tm, D - hd2), F32)
        for cidx in range(NCH):
            gate = gu_s[cidx].astype(F32)
            up = gu_s[NCH + cidx].astype(F32)
            sig = _sigmoid(gate)
            s = gate * sig
            d_act = _dot_nt(dyb, wout_ref[cidx])
            d_up = (d_act * s).astype(BF16)
            d_gate = (d_act * up * (sig * (1.0 + gate * (1.0 - sig)))).astype(BF16)
            dg_ref[cidx] = d_gate
            dg_ref[NCH + cidx] = d_up
            dha = dha + _dot(d_gate, wina_ref[cidx]) + _dot(d_up, wina_ref[NCH + cidx])
            dhb = dhb + _dot(d_gate, winb_ref[cidx]) + _dot(d_up, winb_ref[NCH + cidx])
        dh = jnp.concatenate([dha, dhb], axis=1)
        d_sc = _colsum(dh * n)
        d_sh = _colsum(dh)
        dn = dh * (1.0 + sc)
        d_gpre = _colsum(dn * xh)
        dxh = dn * gpre
        dx_ref[...] = dxo_v + r * (dxh - xh * _rowmean(dxh * xh))

        @pl.when(i % tps == 0)
        def _():
            mg_ref[...] = jnp.zeros((8, D), F32)

        @pl.when(i == 0)
        def _():
            vg_ref[...] = jnp.zeros((8, D), F32)
            loss_ref[...] = jnp.zeros((8, D), F32)

        mg_ref[0:1, :] += d_sh
        mg_ref[1:2, :] += d_sc
        mg_ref[2:3, :] += d_gt
        vg_ref[0:1, :] += d_gpre
        vg_ref[1:2, :] += d_gpost
        loss_ref[...] += HALF * jnp.sum(_rowmean(err * err), axis=0, keepdims=True)

    tile = pl.BlockSpec((tm, D), lambda i: (i, 0))
    return _call(
        core, name=name, grid=(nt,), jobs=jobs,
        in_specs=[tile, tile, pl.BlockSpec((None, 8, D), lambda i: (i // tps, 0, 0)), _const_spec((8, D)),
                  _const_spec(w_in[0].shape), _const_spec(w_in[1].shape), _const_spec((4, FB, D))],
        out_specs=[tile, pl.BlockSpec((8, tm, FB), lambda i: (0, i, 0)),
                   pl.BlockSpec((4, tm, FB), lambda i: (0, i, 0)), tile, tile,
                   pl.BlockSpec((None, 8, D), lambda i: (i // tps, 0, 0)), pl.BlockSpec((8, D), lambda i: (0, 0)),
                   pl.BlockSpec((8, D), lambda i: (0, 0))],
        out_shape=[jax.ShapeDtypeStruct((T, D), F32), jax.ShapeDtypeStruct((8, T, FB), BF16),
                   jax.ShapeDtypeStruct((4, T, FB), BF16), jax.ShapeDtypeStruct((T, D), BF16),
                   jax.ShapeDtypeStruct((T, D), BF16), jax.ShapeDtypeStruct((nb, 8, D), F32),
                   jax.ShapeDtypeStruct((8, D), F32), jax.ShapeDtypeStruct((8, D), F32)],
        scratch=[pltpu.VMEM((8, tm, FB), BF16)],
        args=[x, target, mod, gvec, w_in[0], w_in[1], w_out])


def _masked_spatial(ws_ref):
    row = lax.broadcasted_iota(jnp.int32, (CHUNK, CHUNK), 0)
    col = lax.broadcasted_iota(jnp.int32, (CHUNK, CHUNK), 1)
    keep = col <= row
    return [jnp.where(keep, ws_ref[hd], 0.0).astype(BF16) for hd in range(NHEAD)]


def _head_pairs(mats, right, transpose=False):
    first = lax.broadcasted_iota(jnp.int32, (CHUNK, LANES), 1) < HD
    op = _dot_tn if transpose else _dot
    out = []
    for p in range(NHEAD // 2):
        slab = right[:, _lanes(p)]
        out.append(jnp.where(first, op(mats[2 * p], slab), op(mats[2 * p + 1], slab)))
    return jnp.concatenate(out, axis=1)


def _spatial_gate(wm, vb_chunk):
    return _head_pairs(wm, vb_chunk)


def _layer_norm_stats(v):
    mu = _rowmean(v)
    vc = v - mu
    rstd = lax.rsqrt(_rowmean(vc * vc) + EPS)
    return vc * rstd, rstd


def _pitch(tm):
    p = tm // 8
    while p % 8 != 4:
        p += 1
    return p


def _lanes(s):
    return slice(s * LANES, (s + 1) * LANES)


def _to_slabs(ref, row0, val):
    for s in range(NSLAB):
        ref[s, row0:row0 + val.shape[0], :] = val[:, _lanes(s)]


def _tap_sum(src, out, cw_ref, bias, tm, start):
    p = _pitch(tm)
    for s in range(NSLAB):
        accs = [jnp.broadcast_to(bias[:, _lanes(s)], (SUBL, LANES))] * p
        for k in range(CONV_K):
            w = jnp.broadcast_to(cw_ref[k:k + 1, _lanes(s)], (SUBL, LANES))
            for v in range(p):
                accs[v] = accs[v] + w * src[s, pl.ds(v + start(k), 8, stride=p), :]
        for v in range(p):
            out[s, pl.ds(v, 8, stride=p), :] = accs[v]
    return jnp.concatenate([out[s, 0:tm, :] for s in range(NSLAB)], axis=1)


def _mixer_fwd(x, mod, gvec, w_mi, w_mo, v512, ws, bias_full, cw, tm, name, jobs=()):
    T = x.shape[0]
    nt = T // tm
    tps = nt // mod.shape[0]
    ext_rows = 8 * _pitch(tm)

    def core(ins, outs, scs):
        x_ref, mod_ref, g_ref, wmi_ref, wmo_ref, v_ref, ws_ref, bias_ref, cw_ref = ins
        xo_ref, proj_ref, ym_ref, conv_ref = outs
        glu_ext, conv_scr = scs
        i = pl.program_id(0)
        xv = x_ref[...]
        sh, sc, gt = mod_ref[0:1, :], mod_ref[1:2, :], mod_ref[2:3, :]
        r = lax.rsqrt(_rowmean(xv * xv) + EPS)
        hb = ((xv * r * g_ref[0:1, :]) * (1.0 + sc) + sh).astype(BF16)
        for j in range(NDEV):
            proj_ref[:, j * MB:(j + 1) * MB] = _dot(hb, wmi_ref[j])
        u = proj_ref[:, 0:WA]
        v0 = proj_ref[:, WA:2 * WA]
        a = proj_ref[:, 2 * WA:3 * WA]
        g = proj_ref[:, 3 * WA:4 * WA]
        vh, _ = _layer_norm_stats(v0)
        vb = (vh * v_ref[0:1, :] + v_ref[1:2, :]).astype(BF16)
        wm = _masked_spatial(ws_ref)
        ya = []
        for q in range(tm // CHUNK):
            z = _spatial_gate(wm, vb[q * CHUNK:(q + 1) * CHUNK, :]) + bias_ref[...]
            ya.append(u[q * CHUNK:(q + 1) * CHUNK, :] * z)
        ya = jnp.concatenate(ya, axis=0)
        glu = a * _sigmoid(g)

        @pl.when(i == 0)
        def _():
            glu_ext[:, HALO + tm:HALO + ext_rows, :] = jnp.zeros((NSLAB, ext_rows - tm, LANES), F32)

        @pl.when(i % tps == 0)
        def _():
            glu_ext[:, 0:HALO, :] = jnp.zeros((NSLAB, HALO, LANES), F32)

        _to_slabs(glu_ext, HALO, glu)
        conv = _tap_sum(glu_ext, conv_scr, cw_ref, v_ref[2:3, :], tm, lambda k: HALO - (CONV_K - 1) + k)
        conv_ref[...] = conv
        glu_ext[:, 0:HALO, :] = glu_ext[:, tm:tm + HALO, :]
        ch, _ = _layer_norm_stats(conv)
        cn = ch * v_ref[3:4, :] + v_ref[4:5, :]
        yb = cn * _sigmoid(cn)
        pa = ya * lax.rsqrt(_rowmean(ya * ya) + EPS) * v_ref[5:6, :]
        pb = yb * lax.rsqrt(_rowmean(yb * yb) + EPS) * v_ref[6:7, :]
        ycat = jnp.concatenate([pa, pb], axis=1).astype(BF16)
        ym = _dot(ycat, wmo_ref[...])
        ym_ref[...] = ym
        rm = lax.rsqrt(_rowmean(ym * ym) + EPS)
        xo_ref[...] = xv + gt * (ym * rm * g_ref[1:2, :])

    tile = pl.BlockSpec((tm, D), lambda i: (i, 0))
    return _call(
        core, name=name, grid=(nt,), jobs=jobs,
        in_specs=[tile, pl.BlockSpec((None, 8, D), lambda i: (i // tps, 0, 0)), _const_spec((8, D)),
                  _const_spec((NDEV, D, MB)), _const_spec((D, D)), _const_spec((8, WA)),
                  _const_spec((NHEAD, CHUNK, CHUNK)), _const_spec((CHUNK, WA)), _const_spec((32, WA))],
        out_specs=[tile, pl.BlockSpec((tm, 4 * WA), lambda i: (i, 0)), tile, pl.BlockSpec((tm, WA), lambda i: (i, 0))],
        out_shape=[jax.ShapeDtypeStruct((T, D), F32), jax.ShapeDtypeStruct((T, 4 * WA), F32),
                   jax.ShapeDtypeStruct((T, D), F32), jax.ShapeDtypeStruct((T, WA), F32)],
        scratch=[pltpu.VMEM((NSLAB, HALO + ext_rows, LANES), F32), pltpu.VMEM((NSLAB, ext_rows, LANES), F32)],
        args=[x, mod, gvec, w_mi, w_mo, v512, ws, bias_full, cw])


def _mixer_bwd_a(dxo, ym, proj, conv, mod, gvec, w_mo, v512, ws, bias_full, esel, tm, name, jobs=()):
    T = dxo.shape[0]
    nt = T // tm
    nb = mod.shape[0]
    tps = nt // nb

    def core(ins, outs, scs):
        dxo_ref, ym_ref, proj_ref, conv_ref, mod_ref, g_ref, wmo_ref, v_ref, ws_ref, bias_ref, e_ref = ins
        dpart_ref, dymb_ref, ycat_ref, mg_ref, vg_ref, v5g_ref, gws_ref, gbs_ref = outs
        (dbs_acc,) = scs
        i = pl.program_id(0)
        dxo_v = dxo_ref[...]
        ymv = ym_ref[...]
        gt = mod_ref[2:3, :]
        gpost = g_ref[1:2, :]
        rm = lax.rsqrt(_rowmean(ymv * ymv) + EPS)
        ymh = ymv * rm
        d_gt = _colsum(dxo_v * (ymh * gpost))
        dpm = gt * dxo_v
        d_gpost = _colsum(dpm * ymh)
        dymh = dpm * gpost
        dym = (rm * (dymh - ymh * _rowmean(dymh * ymh))).astype(BF16)
        dymb_ref[...] = dym
        dycat = _dot_nt(dym, wmo_ref[...])
        u = proj_ref[:, 0:WA]
        v0 = proj_ref[:, WA:2 * WA]
        vh, rv = _layer_norm_stats(v0)
        vb = (vh * v_ref[0:1, :] + v_ref[1:2, :]).astype(BF16)
        wm = _masked_spatial(ws_ref)
        zs = []
        for q in range(tm // CHUNK):
            zs.append(_spatial_gate(wm, vb[q * CHUNK:(q + 1) * CHUNK, :]) + bias_ref[...])
        z = jnp.concatenate(zs, axis=0)
        ya = u * z
        ra = lax.rsqrt(_rowmean(ya * ya) + EPS)
        yah = ya * ra
        ch, rc = _layer_norm_stats(conv_ref[...])
        cn = ch * v_ref[3:4, :] + v_ref[4:5, :]
        sg = _sigmoid(cn)
        yb = cn * sg
        rb = lax.rsqrt(_rowmean(yb * yb) + EPS)
        ybh = yb * rb
        ycat_ref[...] = jnp.concatenate([yah * v_ref[5:6, :], ybh * v_ref[6:7, :]], axis=1).astype(BF16)
        dpa = dycat[:, 0:WA]
        dpb = dycat[:, WA:2 * WA]
        d_goa = _colsum(dpa * yah)
        d_gob = _colsum(dpb * ybh)
        dyah = dpa * v_ref[5:6, :]
        dybh = dpb * v_ref[6:7, :]
        dya = ra * (dyah - yah * _rowmean(dyah * yah))
        dyb = rb * (dybh - ybh * _rowmean(dybh * ybh))
        dpart_ref[:, 0:WA] = dya * z
        dz = dya * u

        @pl.when(i == 0)
        def _():
            gws_ref[...] = jnp.zeros((NHEAD, CHUNK, CHUNK), F32)
            dbs_acc[...] = jnp.zeros((CHUNK, WA), F32)
            vg_ref[...] = jnp.zeros((8, D), F32)
            v5g_ref[...] = jnp.zeros((8, WA), F32)

        first = lax.broadcasted_iota(jnp.int32, (CHUNK, LANES), 1) < HD
        dvs = []
        for q in range(tm // CHUNK):
            dz_q = dz[q * CHUNK:(q + 1) * CHUNK, :]
            vb_q = vb[q * CHUNK:(q + 1) * CHUNK, :]
            dbs_acc[...] += dz_q
            dzb = dz_q.astype(BF16)
            dvs.append(_head_pairs(wm, dzb, transpose=True))
            for hd in range(NHEAD):
                slab = dzb[:, _lanes(hd // 2)]
                dz_hd = jnp.where(first if hd % 2 == 0 else jnp.logical_not(first), slab, jnp.zeros_like(slab))
                gws_ref[hd] += _dot_nt(dz_hd, vb_q[:, _lanes(hd // 2)])
        dv = jnp.concatenate(dvs, axis=0)
        d_gng = _colsum(dv * vh)
        d_gnb = _colsum(dv)
        dvh = dv * v_ref[0:1, :]
        dpart_ref[:, WA:2 * WA] = rv * (dvh - _rowmean(dvh) - vh * _rowmean(dvh * vh))
        dcn = dyb * (sg * (1.0 + cn * (1.0 - sg)))
        d_cng = _colsum(dcn * ch)
        d_cnb = _colsum(dcn)
        dch = dcn * v_ref[3:4, :]
        dconv = rc * (dch - _rowmean(dch) - ch * _rowmean(dch * ch))
        dpart_ref[:, 2 * WA:3 * WA] = dconv
        dpart_ref[:, 3 * WA:4 * WA] = jnp.zeros((tm, WA), F32)
        d_cb = _colsum(dconv)

        @pl.when(i % tps == 0)
        def _():
            mg_ref[...] = jnp.zeros((8, D), F32)

        mg_ref[2:3, :] += d_gt
        vg_ref[1:2, :] += d_gpost
        v5g_ref[0:1, :] += d_gng
        v5g_ref[1:2, :] += d_gnb
        v5g_ref[2:3, :] += d_cb
        v5g_ref[3:4, :] += d_cng
        v5g_ref[4:5, :] += d_cnb
        v5g_ref[5:6, :] += d_goa
        v5g_ref[6:7, :] += d_gob

        @pl.when(i == nt - 1)
        def _():
            row = lax.broadcasted_iota(jnp.int32, (CHUNK, CHUNK), 0)
            col = lax.broadcasted_iota(jnp.int32, (CHUNK, CHUNK), 1)
            for hd in range(NHEAD):
                gws_ref[hd] = jnp.where(col <= row, gws_ref[hd], 0.0)
            gbs_ref[...] = lax.dot_general(e_ref[...], dbs_acc[...], (((1,), (1,)), ((), ())),
                                           precision=lax.Precision.HIGHEST, preferred_element_type=F32)

    tile = pl.BlockSpec((tm, D), lambda i: (i, 0))
    ptile = pl.BlockSpec((tm, 4 * WA), lambda i: (i, 0))
    return _call(
        core, name=name, grid=(nt,), jobs=jobs,
        in_specs=[tile, tile, pl.BlockSpec((tm, 2 * WA), lambda i: (i, 0)), pl.BlockSpec((tm, WA), lambda i: (i, 0)),
                  pl.BlockSpec((None, 8, D), lambda i: (i // tps, 0, 0)), _const_spec((8, D)), _const_spec((D, D)),
                  _const_spec((8, WA)), _const_spec((NHEAD, CHUNK, CHUNK)), _const_spec((CHUNK, WA)),
                  _const_spec((8, WA))],
        out_specs=[ptile, tile, tile, pl.BlockSpec((None, 8, D), lambda i: (i // tps, 0, 0)),
                   pl.BlockSpec((8, D), lambda i: (0, 0)), pl.BlockSpec((8, WA), lambda i: (0, 0)),
                   pl.BlockSpec((NHEAD, CHUNK, CHUNK), lambda i: (0, 0, 0)), pl.BlockSpec((8, CHUNK), lambda i: (0, 0))],
        out_shape=[jax.ShapeDtypeStruct((T, 4 * WA), F32), jax.ShapeDtypeStruct((T, D), BF16),
                   jax.ShapeDtypeStruct((T, D), BF16), jax.ShapeDtypeStruct((nb, 8, D), F32),
                   jax.ShapeDtypeStruct((8, D), F32), jax.ShapeDtypeStruct((8, WA), F32),
                   jax.ShapeDtypeStruct((NHEAD, CHUNK, CHUNK), F32), jax.ShapeDtypeStruct((8, CHUNK), F32)],
        scratch=[pltpu.VMEM((CHUNK, WA), F32)],
        args=[dxo, ym, proj, conv, mod, gvec, w_mo, v512, ws, bias_full, esel])


def _mixer_bwd_b(dxo, x, dpart, proj, mod, gvec, w_mi, cw, tm, name, jobs=()):
    T = x.shape[0]
    nt = T // tm
    nb = mod.shape[0]
    tps = nt // nb
    hpt = tm // HALO
    nh = T // HALO
    off = HALO - (CONV_K - 1)
    p = _pitch(tm)
    ext_rows = 8 * p

    def core(ins, outs, scs):
        dxo_ref, x_ref, dpart_ref, dnext_ref, ag_ref, halo_ref, mod_ref, g_ref, wmi_ref, cw_ref = ins
        dx_ref, dproj_ref, hb_ref, mg_ref, vg_ref, dcw_ref = outs
        glu_ext, dconv_ext, dglu_scr, dcw_acc = scs
        i = pl.program_id(0)
        first = i % tps == 0
        last = i % tps == tps - 1
        a = ag_ref[:, 0:WA]
        g = ag_ref[:, WA:2 * WA]
        sgg = _sigmoid(g)

        @pl.when(i == 0)
        def _():
            glu_ext[:, HALO + tm:HALO + ext_rows, :] = jnp.zeros((NSLAB, ext_rows - tm, LANES), F32)
            dconv_ext[:, HALO + tm:HALO + ext_rows, :] = jnp.zeros((NSLAB, ext_rows - tm, LANES), F32)
            dcw_acc[...] = jnp.zeros((32, 8, WA), F32)
            vg_ref[...] = jnp.zeros((8, D), F32)

        _to_slabs(glu_ext, 0, jnp.where(first, 0.0, halo_ref[:, 0:WA] * _sigmoid(halo_ref[:, WA:2 * WA])))
        _to_slabs(glu_ext, HALO, a * sgg)
        _to_slabs(dconv_ext, 0, dpart_ref[:, 2 * WA:3 * WA])
        _to_slabs(dconv_ext, tm, jnp.where(last, 0.0, dnext_ref[...]))
        sub = lax.broadcasted_iota(jnp.int32, (SUBL, LANES), 0)
        for s in range(NSLAB):
            accs = [jnp.zeros((SUBL, LANES), F32)] * CONV_K
            for v in range(p):
                dc = jnp.where(v + p * sub < tm, dconv_ext[s, pl.ds(v, 8, stride=p), :], 0.0)
                for k in range(CONV_K):
                    accs[k] = accs[k] + dc * glu_ext[s, pl.ds(v + off + k, 8, stride=p), :]
            for k in range(CONV_K):
                dcw_acc[k, :, _lanes(s)] += accs[k]
        dglu = _tap_sum(dconv_ext, dglu_scr, cw_ref, jnp.zeros((1, WA), F32), tm, lambda k: (CONV_K - 1) - k)

        @pl.when(i == nt - 1)
        def _():
            for k in range(CONV_K):
                dcw_ref[k:k + 1, :] = jnp.sum(dcw_acc[k], axis=0, keepdims=True)
            dcw_ref[CONV_K:32, :] = jnp.zeros((32 - CONV_K, WA), F32)

        da = dglu * sgg
        dgg = dglu * a * (sgg * (1.0 - sgg))
        dproj_ref[:, 0:2 * WA] = dpart_ref[:, 0:2 * WA].astype(BF16)
        dproj_ref[:, 2 * WA:3 * WA] = da.astype(BF16)
        dproj_ref[:, 3 * WA:4 * WA] = dgg.astype(BF16)
        dh = jnp.zeros((tm, D), F32)
        for j in range(NDEV):
            dh = dh + _dot_nt(dproj_ref[:, j * MB:(j + 1) * MB], wmi_ref[j])
        xv = x_ref[...]
        sc, sh = mod_ref[1:2, :], mod_ref[0:1, :]
        gpre = g_ref[0:1, :]
        r = lax.rsqrt(_rowmean(xv * xv) + EPS)
        xh = xv * r
        n = xh * gpre
        hb_ref[...] = (n * (1.0 + sc) + sh).astype(BF16)
        d_sc = _colsum(dh * n)
        d_sh = _colsum(dh)
        dn = dh * (1.0 + sc)
        d_gpre = _colsum(dn * xh)
        dxh = dn * gpre
        dx_ref[...] = dxo_ref[...] + r * (dxh - xh * _rowmean(dxh * xh))

        @pl.when(first)
        def _():
            mg_ref[...] = jnp.zeros((8, D), F32)

        mg_ref[0:1, :] += d_sh
        mg_ref[1:2, :] += d_sc
        vg_ref[0:1, :] += d_gpre

    tile = pl.BlockSpec((tm, D), lambda i: (i, 0))
    return _call(
        core, name=name, grid=(nt,), jobs=jobs,
        in_specs=[tile, tile, pl.BlockSpec((tm, 4 * WA), lambda i: (i, 0)),
                  pl.BlockSpec((HALO, WA), lambda i: (jnp.minimum((i + 1) * hpt, nh - 1), 2)),
                  pl.BlockSpec((tm, 2 * WA), lambda i: (i, 1)),
                  pl.BlockSpec((HALO, 2 * WA), lambda i: (jnp.maximum(i * hpt - 1, 0), 1)),
                  pl.BlockSpec((None, 8, D), lambda i: (i // tps, 0, 0)), _const_spec((8, D)),
                  _const_spec((NDEV, D, MB)), _const_spec((32, WA))],
        out_specs=[tile, pl.BlockSpec((tm, 4 * WA), lambda i: (i, 0)), tile,
                   pl.BlockSpec((None, 8, D), lambda i: (i // tps, 0, 0)), pl.BlockSpec((8, D), lambda i: (0, 0)),
                   pl.BlockSpec((32, WA), lambda i: (0, 0))],
        out_shape=[jax.ShapeDtypeStruct((T, D), F32), jax.ShapeDtypeStruct((T, 4 * WA), BF16),
                   jax.ShapeDtypeStruct((T, D), BF16), jax.ShapeDtypeStruct((nb, 8, D), F32),
                   jax.ShapeDtypeStruct((8, D), F32), jax.ShapeDtypeStruct((32, WA), F32)],
        scratch=[pltpu.VMEM((NSLAB, HALO + ext_rows, LANES), F32), pltpu.VMEM((NSLAB, HALO + ext_rows, LANES), F32),
                 pltpu.VMEM((NSLAB, ext_rows, LANES), F32), pltpu.VMEM((32, 8, WA), F32)],
        args=[dxo, x, dpart, dpart, proj, proj, mod, gvec, w_mi, cw])


def _grad_chip(a, b, a_spec, b_spec, prod_shape, half, name, jobs=(), via_b=False):
    steps = 8 if half is None else 4
    R = prod_shape[0] if half is None else half
    C = prod_shape[1]

    def core(ins, outs, scs):
        a_ref, b_ref = ins
        (o_ref,) = outs
        own, snd, rcv, ssem, rsem, lsem = scs
        s = pl.program_id(0)
        c = lax.axis_index("c")
        me = _me()
        sib = _flip(me, (0, 0, 1))
        if via_b:
            prod = _dot_tn(b_ref[...], a_ref[...]).T.astype(BF16)
        else:
            prod = _dot_tn(a_ref[...], b_ref[...]).astype(BF16)
        if half is None:
            q = s // 2

            @pl.when(s % 2 == c)
            def _():
                own[q] = prod

            @pl.when(s % 2 != c)
            def _():
                snd[q] = prod
                _remote(snd.at[q], rcv.at[q], ssem.at[q], rsem.at[q], sib).start()
        else:
            lo = prod[0:half, :]
            hi = prod[half:2 * half, :]
            own[s] = jnp.where(c == 0, lo, hi)
            snd[s] = jnp.where(c == 0, hi, lo)
            _remote(snd.at[s], rcv.at[s], ssem.at[s], rsem.at[s], sib).start()

        @pl.when(s == steps - 1)
        def _():
            for q4 in range(4):
                cp = _remote(snd.at[q4], rcv.at[q4], ssem.at[q4], rsem.at[q4], sib)
                cp.wait_recv()
                cp.wait_send()
                snd[q4] = (own[q4].astype(F32) + rcv[q4].astype(F32)).astype(BF16)
            out = pltpu.make_async_copy(snd, o_ref, lsem)
            out.start()
            out.wait()

    return _call(
        core, name=name, grid=(steps,), jobs=jobs, in_specs=[a_spec, b_spec], out_specs=[HBM],
        out_shape=[jax.ShapeDtypeStruct((4, R, C), BF16)],
        scratch=[pltpu.VMEM((4, R, C), BF16), pltpu.VMEM((4, R, C), BF16), pltpu.VMEM((4, R, C), BF16),
                 pltpu.SemaphoreType.DMA((4,)), pltpu.SemaphoreType.DMA((4,)), pltpu.SemaphoreType.DMA],
        args=[a, b])


def _grad_w_in(dg, hb, name, jobs=()):
    T = hb.shape[0]
    return _grad_chip(dg, hb, pl.BlockSpec((None, T, FB), lambda s: (s, 0, 0)), _const_spec((T, D)),
                      (FB, D), None, name, jobs)


def _grad_w_out(act, dyb, name, jobs=()):
    T = dyb.shape[0]
    return _grad_chip(act, dyb, pl.BlockSpec((None, T, FB), lambda s: (s, 0, 0)), _const_spec((T, D)),
                      (FB, D), FO, name, jobs)


def _grad_w_mi(hb, dproj, name, jobs=()):
    T = hb.shape[0]
    return _grad_chip(hb, dproj, _const_spec((T, D)), pl.BlockSpec((T, MB), lambda s: (0, s)),
                      (D, MB), None, name, jobs, via_b=True)


def _grad_w_mo(ycat, dym, name, jobs=()):
    T = ycat.shape[0]
    return _grad_chip(ycat, dym, pl.BlockSpec((T, 2 * MO), lambda s: (0, s)), _const_spec((T, D)),
                      (2 * MO, D), MO, name, jobs)


def _adamw_math(w, g, m, v):
    m2 = ADAM_B1 * m + (1.0 - ADAM_B1) * g
    v2 = ADAM_B2 * v + (1.0 - ADAM_B2) * (g * g)
    m_hat = m2 / (1.0 - ADAM_B1 ** ADAM_STEP)
    v_hat = v2 / (1.0 - ADAM_B2 ** ADAM_STEP)
    delta = -ADAM_LR * (m_hat / (jnp.sqrt(v_hat) + ADAM_EPS) + ADAM_WD * w)
    return delta, m2, v2


def _adamw_reduce(parts, w, m, v, tr, name, own=None, after=None):
    R, C = w.shape

    def core(ins, outs, _):
        p_ref, w_ref, m_ref, v_ref = ins[:4]
        g_ref, d_ref, m2_ref, v2_ref = outs
        if own is None:
            terms = [p_ref[s].astype(F32) for s in range(4)]
        else:
            mq = 2 * lax.axis_index("x") + lax.axis_index("y")
            mine = ins[4][...].astype(F32)
            terms = [jnp.where(mq == s, mine, p_ref[s].astype(F32)) for s in range(4)]
        g = terms[0]
        for s in range(1, 4):
            g = g + terms[s]
        g_ref[...] = g
        d_ref[...], m2_ref[...], v2_ref[...] = _adamw_math(w_ref[...], g, m_ref[...], v_ref[...])

    blk = pl.BlockSpec((tr, C), lambda i: (i, 0))
    in_specs = [pl.BlockSpec((4, tr, C), lambda i: (0, i, 0)), blk, blk, blk]
    args = [parts, w, m, v]
    if own is not None:
        mq = 2 * lax.axis_index("x") + lax.axis_index("y")
        in_specs.append(pl.BlockSpec((tr, C), lambda i: (i, 0)))
        args.append(lax.dynamic_index_in_dim(own, mq, 0, keepdims=False))
    if after is not None:
        in_specs.append(HBM)
        args.append(after)
    return _call(
        core, name=name, grid=(R // tr,), in_specs=in_specs,
        out_specs=[blk, blk, blk, blk], out_shape=[jax.ShapeDtypeStruct((R, C), F32)] * 4, args=args)[0]


HBM_ONLY = pl.BlockSpec(memory_space=pltpu.HBM)
SEM = pl.BlockSpec(memory_space=pltpu.SEMAPHORE)
EFFECT = pltpu.SideEffectType.DATAFLOW_SIDE_EFFECTING


def _chip_scatter_start(gs, name):
    n = len(gs)

    def body(*refs):
        g_refs, land_refs = refs[:n], refs[n:2 * n]
        ssem, rsem = refs[2 * n:2 * n + 2]
        token = refs[-1]
        me = _me()
        mq = 2 * me[0] + me[1]
        for k, f in enumerate(CHIP_FLIPS):
            p = _flip(me, f)
            for a in range(n):
                _remote(g_refs[a].at[2 * p[0] + p[1]], land_refs[a].at[mq], ssem.at[3 * a + k], rsem.at[3 * a + k], p).start()
        token[...] = jnp.zeros_like(token)

    gs = [pltpu.with_memory_space_constraint(g, pltpu.HBM) for g in gs]
    lands = [pltpu.with_memory_space_constraint(lax.empty(g.shape, g.dtype), pltpu.HBM) for g in gs]
    res = pl.pallas_call(
        body, name=name,
        out_shape=(pltpu.SemaphoreType.DMA((3 * n,)), pltpu.SemaphoreType.DMA((3 * n,)))
        + tuple(pltpu.HBM(g.shape, g.dtype) for g in gs) * 2 + (jax.ShapeDtypeStruct((SUBL, LANES), F32),),
        in_specs=(HBM_ONLY,) * (2 * n), out_specs=(SEM, SEM) + (HBM_ONLY,) * (2 * n) + (VM,),
        input_output_aliases={a: 2 + a for a in range(2 * n)},
        compiler_params=pltpu.CompilerParams(has_side_effects=EFFECT),
    )(*gs, *lands)
    return res[:-1], res[-1]


def _chip_scatter_wait(handle, after, name):
    ssem, rsem = handle[:2]
    n = (len(handle) - 2) // 2
    thru = handle[2:]

    def body(*refs):
        g_refs, land_refs = refs[:n], refs[n:2 * n]
        ssem, rsem = refs[2 * n:2 * n + 2]
        me = _me()
        mq = 2 * me[0] + me[1]
        for k, f in enumerate(CHIP_FLIPS):
            p = _flip(me, f)
            pq = 2 * p[0] + p[1]
            for a in range(n):
                _remote(g_refs[a].at[pq], land_refs[a].at[mq], ssem.at[3 * a + k], rsem.at[3 * a + k], p).wait_send()
                _remote(g_refs[a].at[mq], land_refs[a].at[pq], ssem.at[3 * a + k], rsem.at[3 * a + k], p).wait_recv()

    res = pl.pallas_call(
        body, name=name,
        out_shape=tuple(pltpu.HBM(t.shape, t.dtype) for t in thru),
        in_specs=(HBM_ONLY,) * (2 * n) + (SEM, SEM, HBM), out_specs=(HBM_ONLY,) * (2 * n),
        input_output_aliases={a: a for a in range(2 * n)},
        compiler_params=pltpu.CompilerParams(has_side_effects=EFFECT),
    )(*thru, ssem, rsem, after)
    return list(res[:n]), list(res[n:])


def _adamw_ada(sc_all, dd, w, m, v, tr, name, after=None):
    R, C = w.shape

    def core(ins, outs, _):
        sc_ref, dd_ref, w_ref, m_ref, v_ref = ins[:5]
        g_ref, d_ref, m2_ref, v2_ref = outs
        g = _dot_tn(sc_ref[...].astype(BF16), dd_ref[...].astype(BF16))
        g_ref[...] = g
        d_ref[...], m2_ref[...], v2_ref[...] = _adamw_math(w_ref[...], g, m_ref[...], v_ref[...])

    blk = pl.BlockSpec((tr, C), lambda i: (i, 0))
    return _call(
        core, name=name, grid=(R // tr,),
        in_specs=[pl.BlockSpec((64, tr), lambda i: (0, i)), pl.BlockSpec((64, C), lambda i: (0, 0)), blk, blk, blk]
        + [HBM] * (after is not None),
        out_specs=[blk, blk, blk, blk], out_shape=[jax.ShapeDtypeStruct((R, C), F32)] * 4,
        args=[sc_all, dd, w, m, v] + [after] * (after is not None))[0]


def _adamw_small(gathered, plain, grads, wmv, emit, name):
    nw = len(grads)
    ng, npl, ne = len(gathered), len(plain), len(emit)

    def core(ins, outs, _):
        srcs = []
        for a in range(ng):
            s = ins[a][0]
            for dev in range(1, NDEV):
                s = s + ins[a][dev]
            srcs.append(s)
        srcs += [ins[ng + a][...] for a in range(npl)]
        w_refs = ins[ng + npl:]
        for e, a in enumerate(emit):
            outs[e][...] = srcs[a]
        for t in range(nw):
            src, row = grads[t]
            g = srcs[src] if row is None else srcs[src][row:row + 1, :]
            w_ref, m_ref, v_ref = w_refs[3 * t:3 * t + 3]
            g_ref, d_ref, m2_ref, v2_ref = outs[ne + 4 * t:ne + 4 * t + 4]
            g_ref[...] = g
            d_ref[...], m2_ref[...], v2_ref[...] = _adamw_math(w_ref[...], g, m_ref[...], v_ref[...])

    out_shape = [jax.ShapeDtypeStruct(gathered[a].shape[1:], F32) for a in emit]
    for t in range(nw):
        out_shape += [jax.ShapeDtypeStruct(wmv[3 * t].shape, F32)] * 4
    return _call(
        core, name=name, grid=(), in_specs=[VM] * (ng + npl + 3 * nw), out_specs=[VM] * (ne + 4 * nw),
        out_shape=out_shape, args=list(gathered) + list(plain) + list(wmv))[0]


def _ada_fwd(c_pad, w_ada, b_cols, cw_pad, jobs=()):
    def core(ins, outs, scs, start_jobs):
        c_ref, w_ref, b_ref, cwp_ref = ins
        ada_ref, sc_ref, cw_ref = outs
        cbuf, send_buf, ssem, rsem = scs
        me = _me()
        mi = _lin(me)
        cbuf[mi] = c_ref[...]
        cw_ref[mi] = cwp_ref[...]
        peers = [_flip(me, f) for f in FLIPS]
        first = []
        for k, p in enumerate(peers):
            first.append(_remote(cbuf.at[mi], cbuf.at[mi], ssem.at[k], rsem.at[k], p))
            first.append(_remote(cw_ref.at[mi], cw_ref.at[mi], ssem.at[7 + k], rsem.at[7 + k], p))
        for cp in first:
            cp.start()
        for k, p in enumerate(peers):
            pi = _lin(p)
            _remote(cbuf.at[pi], cbuf.at[pi], ssem.at[k], rsem.at[k], p).wait_recv()
            _remote(cw_ref.at[pi], cw_ref.at[pi], ssem.at[7 + k], rsem.at[7 + k], p).wait_recv()
        c_all = cbuf[...].reshape(8 * 8, D)
        sc = c_all * _sigmoid(c_all)
        sc_ref[...] = sc
        res = _dot(sc.astype(BF16), w_ref[...].astype(BF16)) + b_ref[...]
        send_buf[...] = res.reshape(8, 8, ADA_B)
        ada_ref[mi] = send_buf[mi]
        second = []
        for k, p in enumerate(peers):
            second.append(_remote(send_buf.at[_lin(p)], ada_ref.at[mi], ssem.at[14 + k], rsem.at[14 + k], p))
        for cp in second:
            cp.start()
        start_jobs()
        for k, p in enumerate(peers):
            _remote(send_buf.at[mi], ada_ref.at[_lin(p)], ssem.at[14 + k], rsem.at[14 + k], p).wait_recv()
        for cp in first + second:
            cp.wait_send()

    return _call(
        core, name="ada_fwd", grid=(), jobs=jobs, core_starts=True, in_specs=[VM, VM, VM, VM], out_specs=[VM, VM, VM],
        out_shape=[jax.ShapeDtypeStruct((8, 8, ADA_B), F32), jax.ShapeDtypeStruct((64, D), F32),
                   jax.ShapeDtypeStruct((8, 32, 64), F32)],
        scratch=[pltpu.VMEM((8, 8, D), F32), pltpu.VMEM((8, 8, ADA_B), F32),
                 pltpu.SemaphoreType.DMA((21,)), pltpu.SemaphoreType.DMA((21,))],
        args=[c_pad, w_ada, b_cols, cw_pad])


def _ada_bwd(dada, jobs=()):
    def core(ins, outs, scs):
        (d_ref,) = ins
        dd_ref, gb_ref = outs
        rbuf, ssem, rsem = scs
        me = _me()
        mi = _lin(me)
        peers = [_flip(me, f) for f in FLIPS]
        rbuf[mi] = d_ref[mi]
        first = []
        for k, p in enumerate(peers):
            first.append(_remote(d_ref.at[_lin(p)], rbuf.at[mi], ssem.at[k], rsem.at[k], p))
        for cp in first:
            cp.start()
        for k, p in enumerate(peers):
            _remote(d_ref.at[mi], rbuf.at[_lin(p)], ssem.at[k], rsem.at[k], p).wait_recv()
        dd = rbuf[...].reshape(64, ADA_B)
        dd_ref[...] = dd
        gb_ref[mi] = jnp.broadcast_to(_colsum(dd), (8, ADA_B))
        second = []
        for k, p in enumerate(peers):
            second.append(_remote(gb_ref.at[mi], gb_ref.at[mi], ssem.at[7 + k], rsem.at[7 + k], p))
        for cp in second:
            cp.start()
        for k, p in enumerate(peers):
            pi = _lin(p)
            _remote(gb_ref.at[pi], gb_ref.at[pi], ssem.at[7 + k], rsem.at[7 + k], p).wait_recv()
        for cp in first + second:
            cp.wait_send()

    return _call(
        core, name="ada_bwd", grid=(), jobs=jobs, in_specs=[VM], out_specs=[VM, VM],
        out_shape=[jax.ShapeDtypeStruct((64, ADA_B), F32), jax.ShapeDtypeStruct((8, 8, ADA_B), F32)],
        scratch=[pltpu.VMEM((8, 8, ADA_B), F32), pltpu.SemaphoreType.DMA((14,)), pltpu.SemaphoreType.DMA((14,))],
        args=[dada])


SMALL_D = ("g_pre_f1", "g_post_f1", "g_pre_m", "g_post_m", "g_pre_f2", "g_post_f2")
SMALL_W = ("gmlp_norm_g", "gmlp_norm_b", "conv_b", "conv_norm_g", "conv_norm_b", "g_out_a", "g_out_b")


def kernel(x, c, w_ada, b_ada, g_pre_f1, g_post_f1, w_f1_in, w_f1_out, g_pre_m, g_post_m, w_mix_in, gmlp_norm_g, gmlp_norm_b, w_spatial, b_spatial, conv_w, conv_b, conv_norm_g, conv_norm_b, g_out_a, g_out_b, w_mix_out, g_pre_f2, g_post_f2, w_f2_in, w_f2_out, loss_target, m_w_ada, m_b_ada, m_g_pre_f1, m_g_post_f1, m_w_f1_in, m_w_f1_out, m_g_pre_m, m_g_post_m, m_w_mix_in, m_gmlp_norm_g, m_gmlp_norm_b, m_w_spatial, m_b_spatial, m_conv_w, m_conv_b, m_conv_norm_g, m_conv_norm_b, m_g_out_a, m_g_out_b, m_w_mix_out, m_g_pre_f2, m_g_post_f2, m_w_f2_in, m_w_f2_out, v_w_ada, v_b_ada, v_g_pre_f1, v_g_post_f1, v_w_f1_in, v_w_f1_out, v_g_pre_m, v_g_post_m, v_w_mix_in, v_gmlp_norm_g, v_gmlp_norm_b, v_w_spatial, v_b_spatial, v_conv_w, v_conv_b, v_conv_norm_g, v_conv_norm_b, v_g_out_a, v_g_out_b, v_w_mix_out, v_g_pre_f2, v_g_post_f2, v_w_f2_in, v_w_f2_out):
    given = dict(locals())
    bl, seq, _ = x.shape
    T = bl * seq
    tm = min(256, seq // 2)
    mi = _lin((lax.axis_index("x"), lax.axis_index("y"), lax.axis_index("c")))

    def shard_in(w):
        return w[0].T.astype(BF16)

    g_f1 = _Gather([shard_in(w_f1_in), w_f1_out[0].astype(BF16)], ("rows", "out"))
    s_f2 = shard_in(w_f2_in)
    g_mx = _Gather([w_mix_in[0].astype(BF16), w_mix_out[0].astype(BF16), w_f2_out[0].astype(BF16), s_f2[:, 0:D // 4]],
                   ("rows", "rows", "out", "rows"), late_mid=True)
    g_f2 = _Gather([s_f2[:, D // 4:D]], ("rows",))

    c_pad = jnp.pad(c, ((0, 8 - bl), (0, 0)))
    b_cols = lax.dynamic_slice(b_ada, (0, mi * ADA_B), (1, ADA_B))
    cw_pad = jnp.pad(conv_w[0], ((0, 1), (0, 0)))
    (ada_blk, sc_all, cw_all), ((wi1, wo1),) = _ada_fwd(c_pad, w_ada[0], b_cols, cw_pad, jobs=[g_f1])
    ada = ada_blk[:, 0:bl, :].transpose(1, 0, 2).reshape(bl, 9, D)
    pad5 = jnp.zeros((bl, 5, D), F32)
    mod1 = jnp.concatenate([ada[:, 0:3], pad5], axis=1)
    mod2 = jnp.concatenate([ada[:, 3:6], pad5], axis=1)
    mod3 = jnp.concatenate([ada[:, 6:9], pad5], axis=1)
    cw_full = cw_all.transpose(1, 0, 2).reshape(32, WA)

    zrow = jnp.zeros((1, D), F32)
    gv1 = jnp.concatenate([g_pre_f1, g_post_f1] + [zrow] * 6, axis=0)
    gvm = jnp.concatenate([g_pre_m, g_post_m] + [zrow] * 6, axis=0)
    gv2 = jnp.concatenate([g_pre_f2, g_post_f2] + [zrow] * 6, axis=0)
    v512 = jnp.concatenate([gmlp_norm_g, gmlp_norm_b, conv_b, conv_norm_g, conv_norm_b, g_out_a, g_out_b,
                            jnp.zeros((1, WA), F32)], axis=0)
    ws = w_spatial[0]
    bias_full = jnp.repeat(b_spatial[0].T, HD, axis=1)
    esel = (lax.broadcasted_iota(jnp.int32, (8, WA), 1) // HD == lax.broadcasted_iota(jnp.int32, (8, WA), 0)).astype(F32)

    x0 = x.reshape(T, D)
    (x1, gu1, y1), ((wmi, wmo, wo2, wi2a),) = _ffn_fwd(x0, mod1, gv1, wi1, wo1, tm, "ffn1_fwd", jobs=[g_mx])
    wmo = wmo.reshape(D, D)
    (x2, proj, ym, conv), ((wi2b,),) = _mixer_fwd(x1, mod2, gvm, wmi, wmo, v512, ws, bias_full, cw_full, tm, "mixer_fwd", jobs=[g_f2])

    (dx2, dg2, act2, hb2, dyb2, mg3, vg3, loss_blk), _ = _ffn_last(
        x2, loss_target.reshape(T, D), mod3, gv2, (wi2a, wi2b), wo2, tm, "ffn2_fwd_bwd")
    (g_wi2,), _ = _grad_w_in(dg2, hb2, "ffn2_gw_in")
    (g_wo2,), _ = _grad_w_out(act2, dyb2, "ffn2_gw_out")
    (dpart, dymb, ycat, mg2a, vgma, v5g, gws, gbs), ((p_wo2,),) = _mixer_bwd_a(
        dx2, ym, proj, conv, mod2, gvm, wmo, v512, ws, bias_full, esel, tm, "mixer_bwd_a",
        jobs=[_ChipScatter([g_wo2])])
    (dx1, dproj, hbm, mg2b, vgmb, dcw), ((p_wi2,),) = _mixer_bwd_b(
        dx2, x1, dpart, proj, mod2, gvm, wmi, cw_full, tm, "mixer_bwd_b", jobs=[_ChipScatter([g_wi2])])
    (g_wmi,), _ = _grad_w_mi(hbm, dproj, "mixer_gw_in")
    (g_wmo,), _ = _grad_w_mo(ycat, dymb, "mixer_gw_out")
    p2 = jnp.concatenate([v5g, dcw], axis=0)
    (dx0, dg1, act1, hb1, dyb1, mg1, vg1), _ = _ffn_bwd(dx1, x0, y1, gu1, mod1, gv1, wi1, wo1, tm, "ffn1_bwd")

    dada = jnp.concatenate([mg1[:, 0:3], mg2b[:, 0:2], mg2a[:, 2:3], mg3[:, 0:3]], axis=1)
    dada = dada.reshape(bl, NDEV, ADA_B).transpose(1, 0, 2)
    dada = jnp.pad(dada, ((0, 0), (0, 8 - bl), (0, 0)))
    p1 = jnp.concatenate([vg1[0:2], vgmb[0:1], vgma[1:2], vg3[0:2], loss_blk[0:1], zrow], axis=0)
    (dd_all, gb_all), ((a1,),) = _ada_bwd(dada, jobs=[_AllGather([p1])])
    g_bada = gb_all[:, 0, :].reshape(1, 9 * D)

    (g_wo1,), ((p_wmi, p_wmo),) = _grad_w_out(act1, dyb1, "ffn1_gw_out", jobs=[_ChipScatter([g_wmi, g_wmo])])
    (g_wi1,), ((a2, a3, a4), (p_wo1,)) = _grad_w_in(
        dg1, hb1, "ffn1_gw_in", jobs=[_Gather([p2, gws, gbs], ("rows",) * 3), _ChipScatter([g_wo1])])

    h_f1, token = _chip_scatter_start([g_wi1], "tail_start")

    res = {}
    quad = _adamw_reduce(p_wi2, w_f2_in[0].T, m_w_f2_in[0].T, v_w_f2_in[0].T, FO, "adamw_w_f2_in", after=token)
    res["w_f2_in"] = tuple(t.T[None] for t in quad)
    for nm, part, tr in (("w_f2_out", p_wo2, FO), ("w_mix_in", p_wmi, 256), ("w_mix_out", p_wmo, MO), ("w_f1_out", p_wo1, FO)):
        quad = _adamw_reduce(part, given[nm][0], given["m_" + nm][0], given["v_" + nm][0], tr, "adamw_" + nm, after=quad[1])
        res[nm] = tuple(t[None] for t in quad)
    quad = _adamw_ada(sc_all, dd_all, w_ada[0], m_w_ada[0], v_w_ada[0], 256, "adamw_w_ada", after=quad[1])
    res["w_ada"] = tuple(t[None] for t in quad)
    (g_wi1,), (p_wi1,) = _chip_scatter_wait(h_f1, quad[1], "tail_wait")
    quad = _adamw_reduce(p_wi1, w_f1_in[0].T, m_w_f1_in[0].T, v_w_f1_in[0].T, FO, "adamw_w_f1_in", own=g_wi1)
    res["w_f1_in"] = tuple(t.T[None] for t in quad)

    small = SMALL_D + SMALL_W + ("w_spatial", "b_spatial", "b_ada")
    grads = [(0, r) for r in range(6)] + [(1, r) for r in range(7)] + [(2, None), (3, None), (4, None)]
    wmv = []
    for nm in small:
        for pre in ("", "m_", "v_"):
            wmv.append(given[pre + nm][0] if nm in ("w_spatial", "b_spatial") else given[pre + nm])
    outs = _adamw_small([a1, a2, a3, a4], [g_bada], grads, wmv, (0, 1), "adamw_small")
    loss = outs[0][6, 0]
    for t, nm in enumerate(small):
        quad = outs[2 + 4 * t:6 + 4 * t]
        res[nm] = tuple(q[None] for q in quad) if nm in ("w_spatial", "b_spatial") else tuple(quad)
    g_cw = lax.dynamic_slice(outs[1], (8, mi * 64), (32, 64))
    wmv = [jnp.pad(given[pre + "conv_w"][0], ((0, 1), (0, 0)), constant_values=1.0 if pre == "v_" else 0.0)
           for pre in ("", "m_", "v_")]
    quad = _adamw_small([], [g_cw], [(0, None)], wmv, (), "adamw_conv_w")
    res["conv_w"] = tuple(q[0:CONV_K][None] for q in quad)

    order = ["w_ada", "b_ada", "g_pre_f1", "g_post_f1", "w_f1_in", "w_f1_out", "g_pre_m", "g_post_m", "w_mix_in",
             "gmlp_norm_g", "gmlp_norm_b", "w_spatial", "b_spatial", "conv_w", "conv_b", "conv_norm_g", "conv_norm_b",
             "g_out_a", "g_out_b", "w_mix_out", "g_pre_f2", "g_post_f2", "w_f2_in", "w_f2_out"]
    out = [loss, dx0.reshape(bl, seq, D)]
    for k in range(4):
        out += [res[nm][k] for nm in order]
    return tuple(out)
```

```python
import jax
import jax.numpy as jnp
from jax import lax
from jax.experimental import pallas as pl
from jax.experimental.pallas import tpu as pltpu

F32 = jnp.float32
BF16 = jnp.bfloat16

D = 1024
DFF = 2816
NDEV = 8
FB = 2 * DFF // NDEV
NCH = DFF // FB
LANES = 128
SUBL = 8
FO = DFF // NDEV
WA = 512
NSLAB = WA // LANES
NHEAD = 8
HD = 64
CHUNK = 128
CONV_K = 31
HALO = 32
MB = 2 * (WA + WA) // NDEV
MO = D // NDEV
ADA_B = 9 * D // NDEV
EPS = 1e-6
HALF = 0.5

ADAM_LR = 0.001
ADAM_B1 = 0.9
ADAM_B2 = 0.999
ADAM_EPS = 1e-08
ADAM_WD = 0.01
ADAM_STEP = 10

VMEM_LIMIT = 56 * 1024 * 1024
MESH = pl.DeviceIdType.MESH
FLIPS = ((0, 0, 1), (1, 0, 0), (0, 1, 0), (1, 1, 0), (1, 0, 1), (0, 1, 1), (1, 1, 1))
CHIP_FLIPS = ((1, 0, 0), (0, 1, 0), (1, 1, 0))
HBM = pl.BlockSpec(memory_space=pl.ANY)
VM = pl.BlockSpec(memory_space=pltpu.VMEM)


def _dot(a, b):
    return lax.dot_general(a, b, (((1,), (0,)), ((), ())), preferred_element_type=F32)


def _dot_nt(a, b):
    return lax.dot_general(a, b, (((1,), (1,)), ((), ())), preferred_element_type=F32)


def _dot_tn(a, b):
    return lax.dot_general(a, b, (((0,), (0,)), ((), ())), preferred_element_type=F32)


def _rowmean(v):
    return jnp.mean(v, axis=-1, keepdims=True)


def _colsum(v):
    return jnp.sum(v, axis=0, keepdims=True)


def _sigmoid(v):
    return 0.5 * jnp.tanh(0.5 * v) + 0.5


def _const_spec(shape):
    nd = len(shape)
    return pl.BlockSpec(shape, lambda *_: (0,) * nd, pipeline_mode=pl.Buffered(1))


def _me():
    return lax.axis_index("x"), lax.axis_index("y"), lax.axis_index("c")


def _flip(me, f):
    return tuple(1 - v if b else v for v, b in zip(me, f))


def _lin(p):
    return 4 * p[0] + 2 * p[1] + p[2]


def _remote(src, dst, send_sem, recv_sem, dev):
    return pltpu.make_async_remote_copy(src_ref=src, dst_ref=dst, send_sem=send_sem, recv_sem=recv_sem,
                                        device_id=dev, device_id_type=MESH)


def _blk(kind, ref, p):
    if kind == "out":
        return ref.at[2 * p[0] + p[1], pl.ds(p[2] * FO, FO), :]
    return ref.at[_lin(p)]


class _Gather:
    def __init__(self, shards, kinds, late_mid=False):
        self.late_mid = late_mid
        self.kinds = kinds
        self.n = len(shards)
        self.ins = list(shards)
        self.out_shape = [jax.ShapeDtypeStruct((4, FB, D) if k == "out" else (NDEV,) + s.shape, s.dtype)
                          for s, k in zip(shards, kinds)]
        self.sems = [pltpu.SemaphoreType.DMA((7 * self.n,)), pltpu.SemaphoreType.DMA((7 * self.n,)),
                     pltpu.SemaphoreType.DMA((self.n,))]

    def _first(self, ins, outs, sems):
        ssem, rsem, lsem = sems
        me = _me()
        sib = _flip(me, (0, 0, 1))
        cps, loc = [], []
        for a in range(self.n):
            mine = _blk(self.kinds[a], outs[a], me)
            loc.append(pltpu.make_async_copy(ins[a], mine, lsem.at[a]))
            cps.append(_remote(ins[a], mine, ssem.at[7 * a], rsem.at[7 * a], sib))
            for j, f in enumerate(CHIP_FLIPS):
                cps.append(_remote(ins[a], mine, ssem.at[7 * a + 1 + j], rsem.at[7 * a + 1 + j], _flip(me, f)))
        return cps, loc

    def _passed(self, outs, sems):
        ssem, rsem, _ = sems
        me = _me()
        sib = _flip(me, (0, 0, 1))
        cps = []
        for j, f in enumerate(CHIP_FLIPS):
            for a in range(self.n):
                blk = _blk(self.kinds[a], outs[a], _flip(me, f))
                cps.append(_remote(blk, blk, ssem.at[7 * a + 4 + j], rsem.at[7 * a + 4 + j], sib))
        return cps

    def start(self, ins, outs, sems):
        cps, loc = self._first(ins, outs, sems)
        for cp in loc + cps:
            cp.start()

    def mid(self, ins, outs, sems):
        ssem, rsem, _ = sems
        me = _me()
        passed = self._passed(outs, sems)
        t = 0
        for j, f in enumerate(CHIP_FLIPS):
            for a in range(self.n):
                blk = _blk(self.kinds[a], outs[a], _flip(me, f))
                _remote(blk, blk, ssem.at[7 * a + 1 + j], rsem.at[7 * a + 1 + j], _flip(me, f)).wait_recv()
                passed[t].start()
                t += 1

    def end(self, ins, outs, sems):
        ssem, rsem, _ = sems
        me = _me()
        sib = _flip(me, (0, 0, 1))
        for a in range(self.n):
            blk = _blk(self.kinds[a], outs[a], sib)
            _remote(blk, blk, ssem.at[7 * a], rsem.at[7 * a], sib).wait_recv()
            for j, f in enumerate(CHIP_FLIPS):
                blk = _blk(self.kinds[a], outs[a], _flip(_flip(me, f), (0, 0, 1)))
                _remote(blk, blk, ssem.at[7 * a + 4 + j], rsem.at[7 * a + 4 + j], sib).wait_recv()
        cps, loc = self._first(ins, outs, sems)
        for cp in cps + self._passed(outs, sems):
            cp.wait_send()
        for cp in loc:
            cp.wait()


class _ChipScatter:
    def __init__(self, grads):
        self.n = len(grads)
        self.ins = list(grads)
        self.out_shape = [jax.ShapeDtypeStruct(g.shape, BF16) for g in grads]
        self.sems = [pltpu.SemaphoreType.DMA((3 * self.n,)), pltpu.SemaphoreType.DMA((3 * self.n,)),
                     pltpu.SemaphoreType.DMA((self.n,))]

    def _copies(self, ins, outs, sems):
        ssem, rsem, lsem = sems
        me = _me()
        mq = 2 * me[0] + me[1]
        loc = [pltpu.make_async_copy(ins[a].at[mq], outs[a].at[mq], lsem.at[a]) for a in range(self.n)]
        cps = []
        for k, f in enumerate(CHIP_FLIPS):
            p = _flip(me, f)
            for a in range(self.n):
                cps.append(_remote(ins[a].at[2 * p[0] + p[1]], outs[a].at[mq], ssem.at[3 * a + k], rsem.at[3 * a + k], p))
        return cps, loc

    def start(self, ins, outs, sems):
        cps, loc = self._copies(ins, outs, sems)
        for cp in loc + cps:
            cp.start()

    mid = None

    def end(self, ins, outs, sems):
        ssem, rsem, _ = sems
        me = _me()
        mq = 2 * me[0] + me[1]
        for k, f in enumerate(CHIP_FLIPS):
            p = _flip(me, f)
            for a in range(self.n):
                _remote(ins[a].at[mq], outs[a].at[2 * p[0] + p[1]], ssem.at[3 * a + k], rsem.at[3 * a + k], p).wait_recv()
        cps, loc = self._copies(ins, outs, sems)
        for cp in cps:
            cp.wait_send()
        for cp in loc:
            cp.wait()


class _AllGather:
    def __init__(self, parts):
        self.n = len(parts)
        self.ins = list(parts)
        self.out_shape = [jax.ShapeDtypeStruct((NDEV,) + p.shape, p.dtype) for p in parts]
        self.sems = [pltpu.SemaphoreType.DMA((7 * self.n,)), pltpu.SemaphoreType.DMA((7 * self.n,)),
                     pltpu.SemaphoreType.DMA((self.n,))]

    def _copies(self, ins, outs, sems):
        ssem, rsem, lsem = sems
        me = _me()
        mi = _lin(me)
        loc = [pltpu.make_async_copy(ins[a], outs[a].at[mi], lsem.at[a]) for a in range(self.n)]
        cps = []
        for k, f in enumerate(FLIPS):
            for a in range(self.n):
                cps.append(_remote(ins[a], outs[a].at[mi], ssem.at[7 * a + k], rsem.at[7 * a + k], _flip(me, f)))
        return cps, loc

    def start(self, ins, outs, sems):
        cps, loc = self._copies(ins, outs, sems)
        for cp in loc + cps:
            cp.start()

    mid = None

    def end(self, ins, outs, sems):
        ssem, rsem, _ = sems
        me = _me()
        for k, f in enumerate(FLIPS):
            p = _flip(me, f)
            for a in range(self.n):
                _remote(ins[a], outs[a].at[_lin(p)], ssem.at[7 * a + k], rsem.at[7 * a + k], p).wait_recv()
        cps, loc = self._copies(ins, outs, sems)
        for cp in cps:
            cp.wait_send()
        for cp in loc:
            cp.wait()


def _call(core, *, name, grid, in_specs, out_specs, out_shape, args, scratch=(), jobs=(), core_starts=False):
    n_in, n_out, n_sc = len(in_specs), len(out_specs), len(scratch)
    steps = 1
    for g in grid:
        steps *= g

    def body(*refs):
        pos = [0]

        def take(k):
            r = refs[pos[0]:pos[0] + k]
            pos[0] += k
            return r

        ins = take(n_in)
        j_ins = [take(len(j.ins)) for j in jobs]
        outs = take(n_out)
        j_outs = [take(len(j.out_shape)) for j in jobs]
        scs = take(n_sc)
        j_sems = [take(len(j.sems)) for j in jobs]
        if len(grid) == 2:
            step = pl.program_id(0) * grid[1] + pl.program_id(1)
        elif len(grid) == 1:
            step = pl.program_id(0)
        else:
            step = 0
        def start_jobs():
            for j, ji, jo, js in zip(jobs, j_ins, j_outs, j_sems):
                j.start(ji, jo, js)

        if grid:
            pl.when(step == 0)(start_jobs)
        elif not core_starts:
            start_jobs()
        for j, ji, jo, js in zip(jobs, j_ins, j_outs, j_sems):
            if j.mid is not None and grid:
                at = max(steps - 2, 0) if j.late_mid else (3 * steps) // 4
                pl.when(step == at)(lambda j=j, ji=ji, jo=jo, js=js: j.mid(ji, jo, js))
        if core_starts:
            core(ins, outs, scs, start_jobs)
        elif core is not None:
            core(ins, outs, scs)
        for j, ji, jo, js in zip(jobs, j_ins, j_outs, j_sems):
            if grid:
                pl.when(step == steps - 1)(lambda j=j, ji=ji, jo=jo, js=js: j.end(ji, jo, js))
            else:
                if j.mid is not None:
                    j.mid(ji, jo, js)
                j.end(ji, jo, js)

    all_in = list(in_specs)
    all_args = list(args)
    all_out = list(out_specs)
    all_shape = list(out_shape)
    all_sc = list(scratch)
    for j in jobs:
        all_in += [HBM] * len(j.ins)
        all_args += j.ins
    for j in jobs:
        all_out += [HBM] * len(j.out_shape)
        all_shape += j.out_shape
        all_sc += j.sems
    params = dict(vmem_limit_bytes=VMEM_LIMIT)
    if grid:
        params["dimension_semantics"] = ("arbitrary",) * len(grid)
    res = pl.pallas_call(
        body, name=name, grid=grid, in_specs=all_in, out_specs=all_out, out_shape=all_shape,
        scratch_shapes=all_sc, compiler_params=pltpu.CompilerParams(**params),
    )(*all_args)
    core_res = list(res[:n_out])
    job_res = []
    pos = n_out
    for j in jobs:
        job_res.append(list(res[pos:pos + len(j.out_shape)]))
        pos += len(j.out_shape)
    return core_res, job_res


def _ffn_fwd(x, mod, gvec, w_in, w_out, tm, name, jobs=()):
    T = x.shape[0]
    nt = T // tm
    tps = nt // mod.shape[0]

    def core(ins, outs, _):
        x_ref, mod_ref, g_ref, win_ref, wout_ref = ins
        xo_ref, gu_ref, y_ref = outs
        xv = x_ref[...]
        sh, sc, gt = mod_ref[0:1, :], mod_ref[1:2, :], mod_ref[2:3, :]
        r = lax.rsqrt(_rowmean(xv * xv) + EPS)
        h = (xv * r * g_ref[0:1, :]) * (1.0 + sc) + sh
        hb = h.astype(BF16)
        y = jnp.zeros((tm, D), F32)
        for cidx in range(NCH):
            gate = _dot_nt(hb, win_ref[cidx])
            up = _dot_nt(hb, win_ref[NCH + cidx])
            gu_ref[cidx] = gate.astype(BF16)
            gu_ref[NCH + cidx] = up.astype(BF16)
            act = gate * _sigmoid(gate) * up
            y = y + _dot(act.astype(BF16), wout_ref[cidx])
        y_ref[...] = y
        ry = lax.rsqrt(_rowmean(y * y) + EPS)
        xo_ref[...] = xv + (HALF * gt) * (y * ry * g_ref[1:2, :])

    tile = pl.BlockSpec((tm, D), lambda i: (i, 0))
    return _call(
        core, name=name, grid=(nt,), jobs=jobs,
        in_specs=[tile, pl.BlockSpec((None, 8, D), lambda i: (i // tps, 0, 0)), _const_spec((8, D)),
                  _const_spec((8, FB, D)), _const_spec((4, FB, D))],
        out_specs=[tile, pl.BlockSpec((8, tm, FB), lambda i: (0, i, 0)), tile],
        out_shape=[jax.ShapeDtypeStruct((T, D), F32), jax.ShapeDtypeStruct((8, T, FB), BF16),
                   jax.ShapeDtypeStruct((T, D), F32)],
        args=[x, mod, gvec, w_in, w_out])


def _ffn_bwd(dxo, x, y, gu, mod, gvec, w_in, w_out, tm, name, jobs=()):
    T = x.shape[0]
    nt = T // tm
    nb = mod.shape[0]
    tps = nt // nb

    def core(ins, outs, _):
        dxo_ref, x_ref, y_ref, gu_ref, mod_ref, g_ref, win_ref, wout_ref = ins
        dx_ref, dg_ref, act_ref, hb_ref, dyb_ref, mg_ref, vg_ref = outs
        i = pl.program_id(0)
        xv = x_ref[...]
        dxo_v = dxo_ref[...]
        yv = y_ref[...]
        sh, sc, gt = mod_ref[0:1, :], mod_ref[1:2, :], mod_ref[2:3, :]
        gpre, gpost = g_ref[0:1, :], g_ref[1:2, :]
        r = lax.rsqrt(_rowmean(xv * xv) + EPS)
        xh = xv * r
        n = xh * gpre
        hb = (n * (1.0 + sc) + sh).astype(BF16)
        hb_ref[...] = hb
        ry = lax.rsqrt(_rowmean(yv * yv) + EPS)
        yh = yv * ry
        d_gt = _colsum(HALF * dxo_v * (yh * gpost))
        dp = (HALF * gt) * dxo_v
        d_gpost = _colsum(dp * yh)
        dyh = dp * gpost
        dy = ry * (dyh - yh * _rowmean(dyh * yh))
        dyb = dy.astype(BF16)
        dyb_ref[...] = dyb
        dh = jnp.zeros((tm, D), F32)
        for cidx in range(NCH):
            gate = gu_ref[cidx].astype(F32)
            up = gu_ref[NCH + cidx].astype(F32)
            sig = _sigmoid(gate)
            s = gate * sig
            act_ref[cidx] = (s * up).astype(BF16)
            d_act = _dot_nt(dyb, wout_ref[cidx])
            d_up = (d_act * s).astype(BF16)
            d_gate = (d_act * up * (sig * (1.0 + gate * (1.0 - sig)))).astype(BF16)
            dg_ref[cidx] = d_gate
            dg_ref[NCH + cidx] = d_up
            dh = dh + _dot(d_gate, win_ref[cidx]) + _dot(d_up, win_ref[NCH + cidx])
        d_sc = _colsum(dh * n)
        d_sh = _colsum(dh)
        dn = dh * (1.0 + sc)
        d_gpre = _colsum(dn * xh)
        dxh = dn * gpre
        dx_ref[...] = dxo_v + r * (dxh - xh * _rowmean(dxh * xh))

        @pl.when(i % tps == 0)
        def _():
            mg_ref[...] = jnp.zeros((8, D), F32)

        @pl.when(i == 0)
        def _():
            vg_ref[...] = jnp.zeros((8, D), F32)

        mg_ref[0:1, :] += d_sh
        mg_ref[1:2, :] += d_sc
        mg_ref[2:3, :] += d_gt
        vg_ref[0:1, :] += d_gpre
        vg_ref[1:2, :] += d_gpost

    tile = pl.BlockSpec((tm, D), lambda i: (i, 0))
    return _call(
        core, name=name, grid=(nt,), jobs=jobs,
        in_specs=[tile, tile, tile, pl.BlockSpec((8, tm, FB), lambda i: (0, i, 0)),
                  pl.BlockSpec((None, 8, D), lambda i: (i // tps, 0, 0)), _const_spec((8, D)),
                  _const_spec((8, FB, D)), _const_spec((4, FB, D))],
        out_specs=[tile, pl.BlockSpec((8, tm, FB), lambda i: (0, i, 0)),
                   pl.BlockSpec((4, tm, FB), lambda i: (0, i, 0)), tile, tile,
                   pl.BlockSpec((None, 8, D), lambda i: (i // tps, 0, 0)), pl.BlockSpec((8, D), lambda i: (0, 0))],
        out_shape=[jax.ShapeDtypeStruct((T, D), F32), jax.ShapeDtypeStruct((8, T, FB), BF16),
                   jax.ShapeDtypeStruct((4, T, FB), BF16), jax.ShapeDtypeStruct((T, D), BF16),
                   jax.ShapeDtypeStruct((T, D), BF16), jax.ShapeDtypeStruct((nb, 8, D), F32),
                   jax.ShapeDtypeStruct((8, D), F32)],
        args=[dxo, x, y, gu, mod, gvec, w_in, w_out])


def _ffn_last(x, target, mod, gvec, w_in, w_out, tm, name, jobs=()):
    T = x.shape[0]
    nt = T // tm
    nb = mod.shape[0]
    tps = nt // nb

    def core(ins, outs, scs):
        x_ref, t_ref, mod_ref, g_ref, wina_ref, winb_ref, wout_ref = ins
        dx_ref, dg_ref, act_ref, hb_ref, dyb_ref, mg_ref, vg_ref, loss_ref = outs
        hd2 = w_in[0].shape[2]
        (gu_s,) = scs
        i = pl.program_id(0)
        xv = x_ref[...]
        sh, sc, gt = mod_ref[0:1, :], mod_ref[1:2, :], mod_ref[2:3, :]
        gpre, gpost = g_ref[0:1, :], g_ref[1:2, :]
        r = lax.rsqrt(_rowmean(xv * xv) + EPS)
        xh = xv * r
        n = xh * gpre
        hb = (n * (1.0 + sc) + sh).astype(BF16)
        hb_ref[...] = hb
        hba, hbb = hb[:, 0:hd2], hb[:, hd2:D]
        yv = jnp.zeros((tm, D), F32)
        for cidx in range(NCH):
            gate = _dot_nt(hba, wina_ref[cidx]) + _dot_nt(hbb, winb_ref[cidx])
            up = _dot_nt(hba, wina_ref[NCH + cidx]) + _dot_nt(hbb, winb_ref[NCH + cidx])
            gu_s[cidx] = gate.astype(BF16)
            gu_s[NCH + cidx] = up.astype(BF16)
            act = gate * _sigmoid(gate) * up
            act_ref[cidx] = act.astype(BF16)
            yv = yv + _dot(act_ref[cidx], wout_ref[cidx])
        ry = lax.rsqrt(_rowmean(yv * yv) + EPS)
        yh = yv * ry
        pn = yh * gpost
        err = xv + (HALF * gt) * pn - t_ref[...]
        dxo_v = err * (1.0 / D)
        d_gt = _colsum(HALF * dxo_v * pn)
        dp = (HALF * gt) * dxo_v
        d_gpost = _colsum(dp * yh)
        dyh = dp * gpost
        dyb = (ry * (dyh - yh * _rowmean(dyh * yh))).astype(BF16)
        dyb_ref[...] = dyb
        dha = jnp.zeros((tm, hd2), F32)
        dhb = jnp.zeros((tm, D - hd2), F32)
        for cidx in range(NCH):
            gate = gu_s[cidx].astype(F32)
            up = gu_s[NCH + cidx].astype(F32)
            sig = _sigmoid(gate)
            s = gate * sig
            d_act = _dot_nt(dyb, wout_ref[cidx])
            d_up = (d_act * s).astype(BF16)
            d_gate = (d_act * up * (sig * (1.0 + gate * (1.0 - sig)))).astype(BF16)
            dg_ref[cidx] = d_gate
            dg_ref[NCH + cidx] = d_up
            dha = dha + _dot(d_gate, wina_ref[cidx]) + _dot(d_up, wina_ref[NCH + cidx])
            dhb = dhb + _dot(d_gate, winb_ref[cidx]) + _dot(d_up, winb_ref[NCH + cidx])
        dh = jnp.concatenate([dha, dhb], axis=1)
        d_sc = _colsum(dh * n)
        d_sh = _colsum(dh)
        dn = dh * (1.0 + sc)
        d_gpre = _colsum(dn * xh)
        dxh = dn * gpre
        dx_ref[...] = dxo_v + r * (dxh - xh * _rowmean(dxh * xh))

        @pl.when(i % tps == 0)
        def _():
            mg_ref[...] = jnp.zeros((8, D), F32)

        @pl.when(i == 0)
        def _():
            vg_ref[...] = jnp.zeros((8, D), F32)
            loss_ref[...] = jnp.zeros((8, D), F32)

        mg_ref[0:1, :] += d_sh
        mg_ref[1:2, :] += d_sc
        mg_ref[2:3, :] += d_gt
        vg_ref[0:1, :] += d_gpre
        vg_ref[1:2, :] += d_gpost
        loss_ref[...] += HALF * jnp.sum(_rowmean(err * err), axis=0, keepdims=True)

    tile = pl.BlockSpec((tm, D), lambda i: (i, 0))
    return _call(
        core, name=name, grid=(nt,), jobs=jobs,
        in_specs=[tile, tile, pl.BlockSpec((None, 8, D), lambda i: (i // tps, 0, 0)), _const_spec((8, D)),
                  _const_spec(w_in[0].shape), _const_spec(w_in[1].shape), _const_spec((4, FB, D))],
        out_specs=[tile, pl.BlockSpec((8, tm, FB), lambda i: (0, i, 0)),
                   pl.BlockSpec((4, tm, FB), lambda i: (0, i, 0)), tile, tile,
                   pl.BlockSpec((None, 8, D), lambda i: (i // tps, 0, 0)), pl.BlockSpec((8, D), lambda i: (0, 0)),
                   pl.BlockSpec((8, D), lambda i: (0, 0))],
        out_shape=[jax.ShapeDtypeStruct((T, D), F32), jax.ShapeDtypeStruct((8, T, FB), BF16),
                   jax.ShapeDtypeStruct((4, T, FB), BF16), jax.ShapeDtypeStruct((T, D), BF16),
                   jax.ShapeDtypeStruct((T, D), BF16), jax.ShapeDtypeStruct((nb, 8, D), F32),
                   jax.ShapeDtypeStruct((8, D), F32), jax.ShapeDtypeStruct((8, D), F32)],
        scratch=[pltpu.VMEM((8, tm, FB), BF16)],
        args=[x, target, mod, gvec, w_in[0], w_in[1], w_out])


def _masked_spatial(ws_ref):
    row = lax.broadcasted_iota(jnp.int32, (CHUNK, CHUNK), 0)
    col = lax.broadcasted_iota(jnp.int32, (CHUNK, CHUNK), 1)
    keep = col <= row
    return [jnp.where(keep, ws_ref[hd], 0.0).astype(BF16) for hd in range(NHEAD)]


def _head_pairs(mats, right, transpose=False):
    first = lax.broadcasted_iota(jnp.int32, (CHUNK, LANES), 1) < HD
    op = _dot_tn if transpose else _dot
    out = []
    for p in range(NHEAD // 2):
        slab = right[:, _lanes(p)]
        out.append(jnp.where(first, op(mats[2 * p], slab), op(mats[2 * p + 1], slab)))
    return jnp.concatenate(out, axis=1)


def _spatial_gate(wm, vb_chunk):
    return _head_pairs(wm, vb_chunk)


def _layer_norm_stats(v):
    mu = _rowmean(v)
    vc = v - mu
    rstd = lax.rsqrt(_rowmean(vc * vc) + EPS)
    return vc * rstd, rstd


def _pitch(tm):
    p = tm // 8
    while p % 8 != 4:
        p += 1
    return p


def _lanes(s):
    return slice(s * LANES, (s + 1) * LANES)


def _to_slabs(ref, row0, val):
    for s in range(NSLAB):
        ref[s, row0:row0 + val.shape[0], :] = val[:, _lanes(s)]


def _tap_sum(src, out, cw_ref, bias, tm, start):
    p = _pitch(tm)
    for s in range(NSLAB):
        accs = [jnp.broadcast_to(bias[:, _lanes(s)], (SUBL, LANES))] * p
        for k in range(CONV_K):
            w = jnp.broadcast_to(cw_ref[k:k + 1, _lanes(s)], (SUBL, LANES))
            for v in range(p):
                accs[v] = accs[v] + w * src[s, pl.ds(v + start(k), 8, stride=p), :]
        for v in range(p):
            out[s, pl.ds(v, 8, stride=p), :] = accs[v]
    return jnp.concatenate([out[s, 0:tm, :] for s in range(NSLAB)], axis=1)


def _mixer_fwd(x, mod, gvec, w_mi, w_mo, v512, ws, bias_full, cw, tm, name, jobs=()):
    T = x.shape[0]
    nt = T // tm
    tps = nt // mod.shape[0]
    ext_rows = 8 * _pitch(tm)

    def core(ins, outs, scs):
        x_ref, mod_ref, g_ref, wmi_ref, wmo_ref, v_ref, ws_ref, bias_ref, cw_ref = ins
        xo_ref, proj_ref, ym_ref, conv_ref = outs
        glu_ext, conv_scr = scs
        i = pl.program_id(0)
        xv = x_ref[...]
        sh, sc, gt = mod_ref[0:1, :], mod_ref[1:2, :], mod_ref[2:3, :]
        r = lax.rsqrt(_rowmean(xv * xv) + EPS)
        hb = ((xv * r * g_ref[0:1, :]) * (1.0 + sc) + sh).astype(BF16)
        for j in range(NDEV):
            proj_ref[:, j * MB:(j + 1) * MB] = _dot(hb, wmi_ref[j])
        u = proj_ref[:, 0:WA]
        v0 = proj_ref[:, WA:2 * WA]
        a = proj_ref[:, 2 * WA:3 * WA]
        g = proj_ref[:, 3 * WA:4 * WA]
        vh, _ = _layer_norm_stats(v0)
        vb = (vh * v_ref[0:1, :] + v_ref[1:2, :]).astype(BF16)
        wm = _masked_spatial(ws_ref)
        ya = []
        for q in range(tm // CHUNK):
            z = _spatial_gate(wm, vb[q * CHUNK:(q + 1) * CHUNK, :]) + bias_ref[...]
            ya.append(u[q * CHUNK:(q + 1) * CHUNK, :] * z)
        ya = jnp.concatenate(ya, axis=0)
        glu = a * _sigmoid(g)

        @pl.when(i == 0)
        def _():
            glu_ext[:, HALO + tm:HALO + ext_rows, :] = jnp.zeros((NSLAB, ext_rows - tm, LANES), F32)

        @pl.when(i % tps == 0)
        def _():
            glu_ext[:, 0:HALO, :] = jnp.zeros((NSLAB, HALO, LANES), F32)

        _to_slabs(glu_ext, HALO, glu)
        conv = _tap_sum(glu_ext, conv_scr, cw_ref, v_ref[2:3, :], tm, lambda k: HALO - (CONV_K - 1) + k)
        conv_ref[...] = conv
        glu_ext[:, 0:HALO, :] = glu_ext[:, tm:tm + HALO, :]
        ch, _ = _layer_norm_stats(conv)
        cn = ch * v_ref[3:4, :] + v_ref[4:5, :]
        yb = cn * _sigmoid(cn)
        pa = ya * lax.rsqrt(_rowmean(ya * ya) + EPS) * v_ref[5:6, :]
        pb = yb * lax.rsqrt(_rowmean(yb * yb) + EPS) * v_ref[6:7, :]
        ycat = jnp.concatenate([pa, pb], axis=1).astype(BF16)
        ym = _dot(ycat, wmo_ref[...])
        ym_ref[...] = ym
        rm = lax.rsqrt(_rowmean(ym * ym) + EPS)
        xo_ref[...] = xv + gt * (ym * rm * g_ref[1:2, :])

    tile = pl.BlockSpec((tm, D), lambda i: (i, 0))
    return _call(
        core, name=name, grid=(nt,), jobs=jobs,
        in_specs=[tile, pl.BlockSpec((None, 8, D), lambda i: (i // tps, 0, 0)), _const_spec((8, D)),
                  _const_spec((NDEV, D, MB)), _const_spec((D, D)), _const_spec((8, WA)),
                  _const_spec((NHEAD, CHUNK, CHUNK)), _const_spec((CHUNK, WA)), _const_spec((32, WA))],
        out_specs=[tile, pl.BlockSpec((tm, 4 * WA), lambda i: (i, 0)), tile, pl.BlockSpec((tm, WA), lambda i: (i, 0))],
        out_shape=[jax.ShapeDtypeStruct((T, D), F32), jax.ShapeDtypeStruct((T, 4 * WA), F32),
                   jax.ShapeDtypeStruct((T, D), F32), jax.ShapeDtypeStruct((T, WA), F32)],
        scratch=[pltpu.VMEM((NSLAB, HALO + ext_rows, LANES), F32), pltpu.VMEM((NSLAB, ext_rows, LANES), F32)],
        args=[x, mod, gvec, w_mi, w_mo, v512, ws, bias_full, cw])


def _mixer_bwd_a(dxo, ym, proj, conv, mod, gvec, w_mo, v512, ws, bias_full, esel, tm, name, jobs=()):
    T = dxo.shape[0]
    nt = T // tm
    nb = mod.shape[0]
    tps = nt // nb

    def core(ins, outs, scs):
        dxo_ref, ym_ref, proj_ref, conv_ref, mod_ref, g_ref, wmo_ref, v_ref, ws_ref, bias_ref, e_ref = ins
        dpart_ref, dymb_ref, ycat_ref, mg_ref, vg_ref, v5g_ref, gws_ref, gbs_ref = outs
        (dbs_acc,) = scs
        i = pl.program_id(0)
        dxo_v = dxo_ref[...]
        ymv = ym_ref[...]
        gt = mod_ref[2:3, :]
        gpost = g_ref[1:2, :]
        rm = lax.rsqrt(_rowmean(ymv * ymv) + EPS)
        ymh = ymv * rm
        d_gt = _colsum(dxo_v * (ymh * gpost))
        dpm = gt * dxo_v
        d_gpost = _colsum(dpm * ymh)
        dymh = dpm * gpost
        dym = (rm * (dymh - ymh * _rowmean(dymh * ymh))).astype(BF16)
        dymb_ref[...] = dym
        dycat = _dot_nt(dym, wmo_ref[...])
        u = proj_ref[:, 0:WA]
        v0 = proj_ref[:, WA:2 * WA]
        vh, rv = _layer_norm_stats(v0)
        vb = (vh * v_ref[0:1, :] + v_ref[1:2, :]).astype(BF16)
        wm = _masked_spatial(ws_ref)
        zs = []
        for q in range(tm // CHUNK):
            zs.append(_spatial_gate(wm, vb[q * CHUNK:(q + 1) * CHUNK, :]) + bias_ref[...])
        z = jnp.concatenate(zs, axis=0)
        ya = u * z
        ra = lax.rsqrt(_rowmean(ya * ya) + EPS)
        yah = ya * ra
        ch, rc = _layer_norm_stats(conv_ref[...])
        cn = ch * v_ref[3:4, :] + v_ref[4:5, :]
        sg = _sigmoid(cn)
        yb = cn * sg
        rb = lax.rsqrt(_rowmean(yb * yb) + EPS)
        ybh = yb * rb
        ycat_ref[...] = jnp.concatenate([yah * v_ref[5:6, :], ybh * v_ref[6:7, :]], axis=1).astype(BF16)
        dpa = dycat[:, 0:WA]
        dpb = dycat[:, WA:2 * WA]
        d_goa = _colsum(dpa * yah)
        d_gob = _colsum(dpb * ybh)
        dyah = dpa * v_ref[5:6, :]
        dybh = dpb * v_ref[6:7, :]
        dya = ra * (dyah - yah * _rowmean(dyah * yah))
        dyb = rb * (dybh - ybh * _rowmean(dybh * ybh))
        dpart_ref[:, 0:WA] = dya * z
        dz = dya * u

        @pl.when(i == 0)
        def _():
            gws_ref[...] = jnp.zeros((NHEAD, CHUNK, CHUNK), F32)
            dbs_acc[...] = jnp.zeros((CHUNK, WA), F32)
            vg_ref[...] = jnp.zeros((8, D), F32)
            v5g_ref[...] = jnp.zeros((8, WA), F32)

        first = lax.broadcasted_iota(jnp.int32, (CHUNK, LANES), 1) < HD
        dvs = []
        for q in range(tm // CHUNK):
            dz_q = dz[q * CHUNK:(q + 1) * CHUNK, :]
            vb_q = vb[q * CHUNK:(q + 1) * CHUNK, :]
            dbs_acc[...] += dz_q
            dzb = dz_q.astype(BF16)
            dvs.append(_head_pairs(wm, dzb, transpose=True))
            for hd in range(NHEAD):
                slab = dzb[:, _lanes(hd // 2)]
                dz_hd = jnp.where(first if hd % 2 == 0 else jnp.logical_not(first), slab, jnp.zeros_like(slab))
                gws_ref[hd] += _dot_nt(dz_hd, vb_q[:, _lanes(hd // 2)])
        dv = jnp.concatenate(dvs, axis=0)
        d_gng = _colsum(dv * vh)
        d_gnb = _colsum(dv)
        dvh = dv * v_ref[0:1, :]
        dpart_ref[:, WA:2 * WA] = rv * (dvh - _rowmean(dvh) - vh * _rowmean(dvh * vh))
        dcn = dyb * (sg * (1.0 + cn * (1.0 - sg)))
        d_cng = _colsum(dcn * ch)
        d_cnb = _colsum(dcn)
        dch = dcn * v_ref[3:4, :]
        dconv = rc * (dch - _rowmean(dch) - ch * _rowmean(dch * ch))
        dpart_ref[:, 2 * WA:3 * WA] = dconv
        dpart_ref[:, 3 * WA:4 * WA] = jnp.zeros((tm, WA), F32)
        d_cb = _colsum(dconv)

        @pl.when(i % tps == 0)
        def _():
            mg_ref[...] = jnp.zeros((8, D), F32)

        mg_ref[2:3, :] += d_gt
        vg_ref[1:2, :] += d_gpost
        v5g_ref[0:1, :] += d_gng
        v5g_ref[1:2, :] += d_gnb
        v5g_ref[2:3, :] += d_cb
        v5g_ref[3:4, :] += d_cng
        v5g_ref[4:5, :] += d_cnb
        v5g_ref[5:6, :] += d_goa
        v5g_ref[6:7, :] += d_gob

        @pl.when(i == nt - 1)
        def _():
            row = lax.broadcasted_iota(jnp.int32, (CHUNK, CHUNK), 0)
            col = lax.broadcasted_iota(jnp.int32, (CHUNK, CHUNK), 1)
            for hd in range(NHEAD):
                gws_ref[hd] = jnp.where(col <= row, gws_ref[hd], 0.0)
            gbs_ref[...] = lax.dot_general(e_ref[...], dbs_acc[...], (((1,), (1,)), ((), ())),
                                           precision=lax.Precision.HIGHEST, preferred_element_type=F32)

    tile = pl.BlockSpec((tm, D), lambda i: (i, 0))
    ptile = pl.BlockSpec((tm, 4 * WA), lambda i: (i, 0))
    return _call(
        core, name=name, grid=(nt,), jobs=jobs,
        in_specs=[tile, tile, pl.BlockSpec((tm, 2 * WA), lambda i: (i, 0)), pl.BlockSpec((tm, WA), lambda i: (i, 0)),
                  pl.BlockSpec((None, 8, D), lambda i: (i // tps, 0, 0)), _const_spec((8, D)), _const_spec((D, D)),
                  _const_spec((8, WA)), _const_spec((NHEAD, CHUNK, CHUNK)), _const_spec((CHUNK, WA)),
                  _const_spec((8, WA))],
        out_specs=[ptile, tile, tile, pl.BlockSpec((None, 8, D), lambda i: (i // tps, 0, 0)),
                   pl.BlockSpec((8, D), lambda i: (0, 0)), pl.BlockSpec((8, WA), lambda i: (0, 0)),
                   pl.BlockSpec((NHEAD, CHUNK, CHUNK), lambda i: (0, 0, 0)), pl.BlockSpec((8, CHUNK), lambda i: (0, 0))],
        out_shape=[jax.ShapeDtypeStruct((T, 4 * WA), F32), jax.ShapeDtypeStruct((T, D), BF16),
                   jax.ShapeDtypeStruct((T, D), BF16), jax.ShapeDtypeStruct((nb, 8, D), F32),
                   jax.ShapeDtypeStruct((8, D), F32), jax.ShapeDtypeStruct((8, WA), F32),
                   jax.ShapeDtypeStruct((NHEAD, CHUNK, CHUNK), F32), jax.ShapeDtypeStruct((8, CHUNK), F32)],
        scratch=[pltpu.VMEM((CHUNK, WA), F32)],
        args=[dxo, ym, proj, conv, mod, gvec, w_mo, v512, ws, bias_full, esel])


def _mixer_bwd_b(dxo, x, dpart, proj, mod, gvec, w_mi, cw, tm, name, jobs=()):
    T = x.shape[0]
    nt = T // tm
    nb = mod.shape[0]
    tps = nt // nb
    hpt = tm // HALO
    nh = T // HALO
    off = HALO - (CONV_K - 1)
    p = _pitch(tm)
    ext_rows = 8 * p

    def core(ins, outs, scs):
        dxo_ref, x_ref, dpart_ref, dnext_ref, ag_ref, halo_ref, mod_ref, g_ref, wmi_ref, cw_ref = ins
        dx_ref, dproj_ref, hb_ref, mg_ref, vg_ref, dcw_ref = outs
        glu_ext, dconv_ext, dglu_scr, dcw_acc = scs
        i = pl.program_id(0)
        first = i % tps == 0
        last = i % tps == tps - 1
        a = ag_ref[:, 0:WA]
        g = ag_ref[:, WA:2 * WA]
        sgg = _sigmoid(g)

        @pl.when(i == 0)
        def _():
            glu_ext[:, HALO + tm:HALO + ext_rows, :] = jnp.zeros((NSLAB, ext_rows - tm, LANES), F32)
            dconv_ext[:, HALO + tm:HALO + ext_rows, :] = jnp.zeros((NSLAB, ext_rows - tm, LANES), F32)
            dcw_acc[...] = jnp.zeros((32, 8, WA), F32)
            vg_ref[...] = jnp.zeros((8, D), F32)

        _to_slabs(glu_ext, 0, jnp.where(first, 0.0, halo_ref[:, 0:WA] * _sigmoid(halo_ref[:, WA:2 * WA])))
        _to_slabs(glu_ext, HALO, a * sgg)
        _to_slabs(dconv_ext, 0, dpart_ref[:, 2 * WA:3 * WA])
        _to_slabs(dconv_ext, tm, jnp.where(last, 0.0, dnext_ref[...]))
        sub = lax.broadcasted_iota(jnp.int32, (SUBL, LANES), 0)
        for s in range(NSLAB):
            accs = [jnp.zeros((SUBL, LANES), F32)] * CONV_K
            for v in range(p):
                dc = jnp.where(v + p * sub < tm, dconv_ext[s, pl.ds(v, 8, stride=p), :], 0.0)
                for k in range(CONV_K):
                    accs[k] = accs[k] + dc * glu_ext[s, pl.ds(v + off + k, 8, stride=p), :]
            for k in range(CONV_K):
                dcw_acc[k, :, _lanes(s)] += accs[k]
        dglu = _tap_sum(dconv_ext, dglu_scr, cw_ref, jnp.zeros((1, WA), F32), tm, lambda k: (CONV_K - 1) - k)

        @pl.when(i == nt - 1)
        def _():
            for k in range(CONV_K):
                dcw_ref[k:k + 1, :] = jnp.sum(dcw_acc[k], axis=0, keepdims=True)
            dcw_ref[CONV_K:32, :] = jnp.zeros((32 - CONV_K, WA), F32)

        da = dglu * sgg
        dgg = dglu * a * (sgg * (1.0 - sgg))
        dproj_ref[:, 0:2 * WA] = dpart_ref[:, 0:2 * WA].astype(BF16)
        dproj_ref[:, 2 * WA:3 * WA] = da.astype(BF16)
        dproj_ref[:, 3 * WA:4 * WA] = dgg.astype(BF16)
        dh = jnp.zeros((tm, D), F32)
        for j in range(NDEV):
            dh = dh + _dot_nt(dproj_ref[:, j * MB:(j + 1) * MB], wmi_ref[j])
        xv = x_ref[...]
        sc, sh = mod_ref[1:2, :], mod_ref[0:1, :]
        gpre = g_ref[0:1, :]
        r = lax.rsqrt(_rowmean(xv * xv) + EPS)
        xh = xv * r
        n = xh * gpre
        hb_ref[...] = (n * (1.0 + sc) + sh).astype(BF16)
        d_sc = _colsum(dh * n)
        d_sh = _colsum(dh)
        dn = dh * (1.0 + sc)
        d_gpre = _colsum(dn * xh)
        dxh = dn * gpre
        dx_ref[...] = dxo_ref[...] + r * (dxh - xh * _rowmean(dxh * xh))

        @pl.when(first)
        def _():
            mg_ref[...] = jnp.zeros((8, D), F32)

        mg_ref[0:1, :] += d_sh
        mg_ref[1:2, :] += d_sc
        vg_ref[0:1, :] += d_gpre

    tile = pl.BlockSpec((tm, D), lambda i: (i, 0))
    return _call(
        core, name=name, grid=(nt,), jobs=jobs,
        in_specs=[tile, tile, pl.BlockSpec((tm, 4 * WA), lambda i: (i, 0)),
                  pl.BlockSpec((HALO, WA), lambda i: (jnp.minimum((i + 1) * hpt, nh - 1), 2)),
                  pl.BlockSpec((tm, 2 * WA), lambda i: (i, 1)),
                  pl.BlockSpec((HALO, 2 * WA), lambda i: (jnp.maximum(i * hpt - 1, 0), 1)),
                  pl.BlockSpec((None, 8, D), lambda i: (i // tps, 0, 0)), _const_spec((8, D)),
                  _const_spec((NDEV, D, MB)), _const_spec((32, WA))],
        out_specs=[tile, pl.BlockSpec((tm, 4 * WA), lambda i: (i, 0)), tile,
                   pl.BlockSpec((None, 8, D), lambda i: (i // tps, 0, 0)), pl.BlockSpec((8, D), lambda i: (0, 0)),
                   pl.BlockSpec((32, WA), lambda i: (0, 0))],
        out_shape=[jax.ShapeDtypeStruct((T, D), F32), jax.ShapeDtypeStruct((T, 4 * WA), BF16),
                   jax.ShapeDtypeStruct((T, D), BF16), jax.ShapeDtypeStruct((nb, 8, D), F32),
                   jax.ShapeDtypeStruct((8, D), F32), jax.ShapeDtypeStruct((32, WA), F32)],
        scratch=[pltpu.VMEM((NSLAB, HALO + ext_rows, LANES), F32), pltpu.VMEM((NSLAB, HALO + ext_rows, LANES), F32),
                 pltpu.VMEM((NSLAB, ext_rows, LANES), F32), pltpu.VMEM((32, 8, WA), F32)],
        args=[dxo, x, dpart, dpart, proj, proj, mod, gvec, w_mi, cw])


def _grad_chip(a, b, a_spec, b_spec, prod_shape, half, name, jobs=(), via_b=False):
    steps = 8 if half is None else 4
    R = prod_shape[0] if half is None else half
    C = prod_shape[1]

    def core(ins, outs, scs):
        a_ref, b_ref = ins
        (o_ref,) = outs
        own, snd, rcv, ssem, rsem, lsem = scs
        s = pl.program_id(0)
        c = lax.axis_index("c")
        me = _me()
        sib = _flip(me, (0, 0, 1))
        def finish(q):
            cp = _remote(snd.at[q], rcv.at[q], ssem.at[q], rsem.at[q], sib)
            cp.wait_recv()
            cp.wait_send()
            snd[q] = (own[q].astype(F32) + rcv[q].astype(F32)).astype(BF16)
            pltpu.make_async_copy(snd.at[q], o_ref.at[q], lsem.at[q]).start()

        if half is None:
            pl.when(jnp.logical_and(s >= 2, s % 2 == 0))(lambda: finish(s // 2 - 1))
        else:
            pl.when(s >= 1)(lambda: finish(s - 1))
        if via_b:
            prod = _dot_tn(b_ref[...], a_ref[...]).T.astype(BF16)
        else:
            prod = _dot_tn(a_ref[...], b_ref[...]).astype(BF16)
        if half is None:
            q = s // 2

            @pl.when(s % 2 == c)
            def _():
                own[q] = prod

            @pl.when(s % 2 != c)
            def _():
                snd[q] = prod
                _remote(snd.at[q], rcv.at[q], ssem.at[q], rsem.at[q], sib).start()
        else:
            lo = prod[0:half, :]
            hi = prod[half:2 * half, :]
            own[s] = jnp.where(c == 0, lo, hi)
            snd[s] = jnp.where(c == 0, hi, lo)
            _remote(snd.at[s], rcv.at[s], ssem.at[s], rsem.at[s], sib).start()

        @pl.when(s == steps - 1)
        def _():
            finish(3)
            for q4 in range(4):
                pltpu.make_async_copy(snd.at[q4], o_ref.at[q4], lsem.at[q4]).wait()

    return _call(
        core, name=name, grid=(steps,), jobs=jobs, in_specs=[a_spec, b_spec], out_specs=[HBM],
        out_shape=[jax.ShapeDtypeStruct((4, R, C), BF16)],
        scratch=[pltpu.VMEM((4, R, C), BF16), pltpu.VMEM((4, R, C), BF16), pltpu.VMEM((4, R, C), BF16),
                 pltpu.SemaphoreType.DMA((4,)), pltpu.SemaphoreType.DMA((4,)), pltpu.SemaphoreType.DMA((4,))],
        args=[a, b])


def _grad_w_in(dg, hb, name, jobs=()):
    T = hb.shape[0]
    return _grad_chip(dg, hb, pl.BlockSpec((None, T, FB), lambda s: (s, 0, 0)), _const_spec((T, D)),
                      (FB, D), None, name, jobs)


def _grad_w_out(act, dyb, name, jobs=()):
    T = dyb.shape[0]
    return _grad_chip(act, dyb, pl.BlockSpec((None, T, FB), lambda s: (s, 0, 0)), _const_spec((T, D)),
                      (FB, D), FO, name, jobs)


def _grad_w_mi(hb, dproj, name, jobs=()):
    T = hb.shape[0]
    return _grad_chip(hb, dproj, _const_spec((T, D)), pl.BlockSpec((T, MB), lambda s: (0, s)),
                      (D, MB), None, name, jobs, via_b=True)


def _grad_w_mo(ycat, dym, name, jobs=()):
    T = ycat.shape[0]
    return _grad_chip(ycat, dym, pl.BlockSpec((T, 2 * MO), lambda s: (0, s)), _const_spec((T, D)),
                      (2 * MO, D), MO, name, jobs)


def _adamw_math(w, g, m, v):
    m2 = ADAM_B1 * m + (1.0 - ADAM_B1) * g
    v2 = ADAM_B2 * v + (1.0 - ADAM_B2) * (g * g)
    m_hat = m2 / (1.0 - ADAM_B1 ** ADAM_STEP)
    v_hat = v2 / (1.0 - ADAM_B2 ** ADAM_STEP)
    delta = -ADAM_LR * (m_hat / (jnp.sqrt(v_hat) + ADAM_EPS) + ADAM_WD * w)
    return delta, m2, v2


def _adamw_reduce(parts, w, m, v, tr, name, own=None, after=None):
    R, C = w.shape

    def core(ins, outs, _):
        p_ref, w_ref, m_ref, v_ref = ins[:4]
        g_ref, d_ref, m2_ref, v2_ref = outs
        if own is None:
            terms = [p_ref[s].astype(F32) for s in range(4)]
        else:
            mq = 2 * lax.axis_index("x") + lax.axis_index("y")
            mine = ins[4][...].astype(F32)
            terms = [jnp.where(mq == s, mine, p_ref[s].astype(F32)) for s in range(4)]
        g = terms[0]
        for s in range(1, 4):
            g = g + terms[s]
        g_ref[...] = g
        d_ref[...], m2_ref[...], v2_ref[...] = _adamw_math(w_ref[...], g, m_ref[...], v_ref[...])

    blk = pl.BlockSpec((tr, C), lambda i: (i, 0))
    in_specs = [pl.BlockSpec((4, tr, C), lambda i: (0, i, 0)), blk, blk, blk]
    args = [parts, w, m, v]
    if own is not None:
        mq = 2 * lax.axis_index("x") + lax.axis_index("y")
        in_specs.append(pl.BlockSpec((tr, C), lambda i: (i, 0)))
        args.append(lax.dynamic_index_in_dim(own, mq, 0, keepdims=False))
    if after is not None:
        in_specs.append(HBM)
        args.append(after)
    return _call(
        core, name=name, grid=(R // tr,), in_specs=in_specs,
        out_specs=[blk, blk, blk, blk], out_shape=[jax.ShapeDtypeStruct((R, C), F32)] * 4, args=args)[0]


HBM_ONLY = pl.BlockSpec(memory_space=pltpu.HBM)
SEM = pl.BlockSpec(memory_space=pltpu.SEMAPHORE)
EFFECT = pltpu.SideEffectType.DATAFLOW_SIDE_EFFECTING


def _chip_scatter_start(gs, name):
    n = len(gs)

    def body(*refs):
        g_refs, land_refs = refs[:n], refs[n:2 * n]
        ssem, rsem = refs[2 * n:2 * n + 2]
        token = refs[-1]
        me = _me()
        mq = 2 * me[0] + me[1]
        for k, f in enumerate(CHIP_FLIPS):
            p = _flip(me, f)
            for a in range(n):
                _remote(g_refs[a].at[2 * p[0] + p[1]], land_refs[a].at[mq], ssem.at[3 * a + k], rsem.at[3 * a + k], p).start()
        token[...] = jnp.zeros_like(token)

    gs = [pltpu.with_memory_space_constraint(g, pltpu.HBM) for g in gs]
    lands = [pltpu.with_memory_space_constraint(lax.empty(g.shape, g.dtype), pltpu.HBM) for g in gs]
    res = pl.pallas_call(
        body, name=name,
        out_shape=(pltpu.SemaphoreType.DMA((3 * n,)), pltpu.SemaphoreType.DMA((3 * n,)))
        + tuple(pltpu.HBM(g.shape, g.dtype) for g in gs) * 2 + (jax.ShapeDtypeStruct((SUBL, LANES), F32),),
        in_specs=(HBM_ONLY,) * (2 * n), out_specs=(SEM, SEM) + (HBM_ONLY,) * (2 * n) + (VM,),
        input_output_aliases={a: 2 + a for a in range(2 * n)},
        compiler_params=pltpu.CompilerParams(has_side_effects=EFFECT),
    )(*gs, *lands)
    return res[:-1], res[-1]


def _chip_scatter_wait(handle, after, name):
    ssem, rsem = handle[:2]
    n = (len(handle) - 2) // 2
    thru = handle[2:]

    def body(*refs):
        g_refs, land_refs = refs[:n], refs[n:2 * n]
        ssem, rsem = refs[2 * n:2 * n + 2]
        me = _me()
        mq = 2 * me[0] + me[1]
        for k, f in enumerate(CHIP_FLIPS):
            p = _flip(me, f)
            pq = 2 * p[0] + p[1]
            for a in range(n):
                _remote(g_refs[a].at[pq], land_refs[a].at[mq], ssem.at[3 * a + k], rsem.at[3 * a + k], p).wait_send()
                _remote(g_refs[a].at[mq], land_refs[a].at[pq], ssem.at[3 * a + k], rsem.at[3 * a + k], p).wait_recv()

    res = pl.pallas_call(
        body, name=name,
        out_shape=tuple(pltpu.HBM(t.shape, t.dtype) for t in thru),
        in_specs=(HBM_ONLY,) * (2 * n) + (SEM, SEM, HBM), out_specs=(HBM_ONLY,) * (2 * n),
        input_output_aliases={a: a for a in range(2 * n)},
        compiler_params=pltpu.CompilerParams(has_side_effects=EFFECT),
    )(*thru, ssem, rsem, after)
    return list(res[:n]), list(res[n:])


def _adamw_ada(sc_all, dd, w, m, v, tr, name, after=None):
    R, C = w.shape

    def core(ins, outs, _):
        sc_ref, dd_ref, w_ref, m_ref, v_ref = ins[:5]
        g_ref, d_ref, m2_ref, v2_ref = outs
        g = _dot_tn(sc_ref[...].astype(BF16), dd_ref[...].astype(BF16))
        g_ref[...] = g
        d_ref[...], m2_ref[...], v2_ref[...] = _adamw_math(w_ref[...], g, m_ref[...], v_ref[...])

    blk = pl.BlockSpec((tr, C), lambda i: (i, 0))
    return _call(
        core, name=name, grid=(R // tr,),
        in_specs=[pl.BlockSpec((64, tr), lambda i: (0, i)), pl.BlockSpec((64, C), lambda i: (0, 0)), blk, blk, blk]
        + [HBM] * (after is not None),
        out_specs=[blk, blk, blk, blk], out_shape=[jax.ShapeDtypeStruct((R, C), F32)] * 4,
        args=[sc_all, dd, w, m, v] + [after] * (after is not None))[0]


def _adamw_small(gathered, plain, grads, wmv, emit, name):
    nw = len(grads)
    ng, npl, ne = len(gathered), len(plain), len(emit)

    def core(ins, outs, _):
        srcs = []
        for a in range(ng):
            s = ins[a][0]
            for dev in range(1, NDEV):
                s = s + ins[a][dev]
            srcs.append(s)
        srcs += [ins[ng + a][...] for a in range(npl)]
        w_refs = ins[ng + npl:]
        for e, a in enumerate(emit):
            outs[e][...] = srcs[a]
        for t in range(nw):
            src, row = grads[t]
            g = srcs[src] if row is None else srcs[src][row:row + 1, :]
            w_ref, m_ref, v_ref = w_refs[3 * t:3 * t + 3]
            g_ref, d_ref, m2_ref, v2_ref = outs[ne + 4 * t:ne + 4 * t + 4]
            g_ref[...] = g
            d_ref[...], m2_ref[...], v2_ref[...] = _adamw_math(w_ref[...], g, m_ref[...], v_ref[...])

    out_shape = [jax.ShapeDtypeStruct(gathered[a].shape[1:], F32) for a in emit]
    for t in range(nw):
        out_shape += [jax.ShapeDtypeStruct(wmv[3 * t].shape, F32)] * 4
    return _call(
        core, name=name, grid=(), in_specs=[VM] * (ng + npl + 3 * nw), out_specs=[VM] * (ne + 4 * nw),
        out_shape=out_shape, args=list(gathered) + list(plain) + list(wmv))[0]


def _ada_fwd(c_pad, w_ada, b_cols, cw_pad, jobs=()):
    def core(ins, outs, scs, start_jobs):
        c_ref, w_ref, b_ref, cwp_ref = ins
        ada_ref, sc_ref, cw_ref = outs
        cbuf, send_buf, ssem, rsem = scs
        me = _me()
        mi = _lin(me)
        cbuf[mi] = c_ref[...]
        cw_ref[mi] = cwp_ref[...]
        peers = [_flip(me, f) for f in FLIPS]
        first = []
        for k, p in enumerate(peers):
            first.append(_remote(cbuf.at[mi], cbuf.at[mi], ssem.at[k], rsem.at[k], p))
            first.append(_remote(cw_ref.at[mi], cw_ref.at[mi], ssem.at[7 + k], rsem.at[7 + k], p))
        for cp in first:
            cp.start()
        for k, p in enumerate(peers):
            pi = _lin(p)
            _remote(cbuf.at[pi], cbuf.at[pi], ssem.at[k], rsem.at[k], p).wait_recv()
            _remote(cw_ref.at[pi], cw_ref.at[pi], ssem.at[7 + k], rsem.at[7 + k], p).wait_recv()
        c_all = cbuf[...].reshape(8 * 8, D)
        sc = c_all * _sigmoid(c_all)
        sc_ref[...] = sc
        res = _dot(sc.astype(BF16), w_ref[...].astype(BF16)) + b_ref[...]
        send_buf[...] = res.reshape(8, 8, ADA_B)
        ada_ref[mi] = send_buf[mi]
        second = []
        for k, p in enumerate(peers):
            second.append(_remote(send_buf.at[_lin(p)], ada_ref.at[mi], ssem.at[14 + k], rsem.at[14 + k], p))
        for cp in second:
            cp.start()
        start_jobs()
        for k, p in enumerate(peers):
            _remote(send_buf.at[mi], ada_ref.at[_lin(p)], ssem.at[14 + k], rsem.at[14 + k], p).wait_recv()
        for cp in first + second:
            cp.wait_send()

    return _call(
        core, name="ada_fwd", grid=(), jobs=jobs, core_starts=True, in_specs=[VM, VM, VM, VM], out_specs=[VM, VM, VM],
        out_shape=[jax.ShapeDtypeStruct((8, 8, ADA_B), F32), jax.ShapeDtypeStruct((64, D), F32),
                   jax.ShapeDtypeStruct((8, 32, 64), F32)],
        scratch=[pltpu.VMEM((8, 8, D), F32), pltpu.VMEM((8, 8, ADA_B), F32),
                 pltpu.SemaphoreType.DMA((21,)), pltpu.SemaphoreType.DMA((21,))],
        args=[c_pad, w_ada, b_cols, cw_pad])


def _ada_bwd(dada, jobs=()):
    def core(ins, outs, scs):
        (d_ref,) = ins
        dd_ref, gb_ref = outs
        rbuf, ssem, rsem = scs
        me = _me()
        mi = _lin(me)
        peers = [_flip(me, f) for f in FLIPS]
        rbuf[mi] = d_ref[mi]
        first = []
        for k, p in enumerate(peers):
            first.append(_remote(d_ref.at[_lin(p)], rbuf.at[mi], ssem.at[k], rsem.at[k], p))
        for cp in first:
            cp.start()
        for k, p in enumerate(peers):
            _remote(d_ref.at[mi], rbuf.at[_lin(p)], ssem.at[k], rsem.at[k], p).wait_recv()
        dd = rbuf[...].reshape(64, ADA_B)
        dd_ref[...] = dd
        gb_ref[mi] = jnp.broadcast_to(_colsum(dd), (8, ADA_B))
        second = []
        for k, p in enumerate(peers):
            second.append(_remote(gb_ref.at[mi], gb_ref.at[mi], ssem.at[7 + k], rsem.at[7 + k], p))
        for cp in second:
            cp.start()
        for k, p in enumerate(peers):
            pi = _lin(p)
            _remote(gb_ref.at[pi], gb_ref.at[pi], ssem.at[7 + k], rsem.at[7 + k], p).wait_recv()
        for cp in first + second:
            cp.wait_send()

    return _call(
        core, name="ada_bwd", grid=(), jobs=jobs, in_specs=[VM], out_specs=[VM, VM],
        out_shape=[jax.ShapeDtypeStruct((64, ADA_B), F32), jax.ShapeDtypeStruct((8, 8, ADA_B), F32)],
        scratch=[pltpu.VMEM((8, 8, ADA_B), F32), pltpu.SemaphoreType.DMA((14,)), pltpu.SemaphoreType.DMA((14,))],
        args=[dada])


SMALL_D = ("g_pre_f1", "g_post_f1", "g_pre_m", "g_post_m", "g_pre_f2", "g_post_f2")
SMALL_W = ("gmlp_norm_g", "gmlp_norm_b", "conv_b", "conv_norm_g", "conv_norm_b", "g_out_a", "g_out_b")


def kernel(x, c, w_ada, b_ada, g_pre_f1, g_post_f1, w_f1_in, w_f1_out, g_pre_m, g_post_m, w_mix_in, gmlp_norm_g, gmlp_norm_b, w_spatial, b_spatial, conv_w, conv_b, conv_norm_g, conv_norm_b, g_out_a, g_out_b, w_mix_out, g_pre_f2, g_post_f2, w_f2_in, w_f2_out, loss_target, m_w_ada, m_b_ada, m_g_pre_f1, m_g_post_f1, m_w_f1_in, m_w_f1_out, m_g_pre_m, m_g_post_m, m_w_mix_in, m_gmlp_norm_g, m_gmlp_norm_b, m_w_spatial, m_b_spatial, m_conv_w, m_conv_b, m_conv_norm_g, m_conv_norm_b, m_g_out_a, m_g_out_b, m_w_mix_out, m_g_pre_f2, m_g_post_f2, m_w_f2_in, m_w_f2_out, v_w_ada, v_b_ada, v_g_pre_f1, v_g_post_f1, v_w_f1_in, v_w_f1_out, v_g_pre_m, v_g_post_m, v_w_mix_in, v_gmlp_norm_g, v_gmlp_norm_b, v_w_spatial, v_b_spatial, v_conv_w, v_conv_b, v_conv_norm_g, v_conv_norm_b, v_g_out_a, v_g_out_b, v_w_mix_out, v_g_pre_f2, v_g_post_f2, v_w_f2_in, v_w_f2_out):
    given = dict(locals())
    bl, seq, _ = x.shape
    T = bl * seq
    tm = min(256, seq // 2)
    mi = _lin((lax.axis_index("x"), lax.axis_index("y"), lax.axis_index("c")))

    def shard_in(w):
        return w[0].T.astype(BF16)

    g_f1 = _Gather([shard_in(w_f1_in), w_f1_out[0].astype(BF16)], ("rows", "out"))
    s_f2 = shard_in(w_f2_in)
    g_mx = _Gather([w_mix_in[0].astype(BF16), w_mix_out[0].astype(BF16), w_f2_out[0].astype(BF16), s_f2[:, 0:D // 4]],
                   ("rows", "rows", "out", "rows"), late_mid=True)
    g_f2 = _Gather([s_f2[:, D // 4:D]], ("rows",))

    c_pad = jnp.pad(c, ((0, 8 - bl), (0, 0)))
    b_cols = lax.dynamic_slice(b_ada, (0, mi * ADA_B), (1, ADA_B))
    cw_pad = jnp.pad(conv_w[0], ((0, 1), (0, 0)))
    (ada_blk, sc_all, cw_all), ((wi1, wo1),) = _ada_fwd(c_pad, w_ada[0], b_cols, cw_pad, jobs=[g_f1])
    ada = ada_blk[:, 0:bl, :].transpose(1, 0, 2).reshape(bl, 9, D)
    pad5 = jnp.zeros((bl, 5, D), F32)
    mod1 = jnp.concatenate([ada[:, 0:3], pad5], axis=1)
    mod2 = jnp.concatenate([ada[:, 3:6], pad5], axis=1)
    mod3 = jnp.concatenate([ada[:, 6:9], pad5], axis=1)
    cw_full = cw_all.transpose(1, 0, 2).reshape(32, WA)

    zrow = jnp.zeros((1, D), F32)
    gv1 = jnp.concatenate([g_pre_f1, g_post_f1] + [zrow] * 6, axis=0)
    gvm = jnp.concatenate([g_pre_m, g_post_m] + [zrow] * 6, axis=0)
    gv2 = jnp.concatenate([g_pre_f2, g_post_f2] + [zrow] * 6, axis=0)
    v512 = jnp.concatenate([gmlp_norm_g, gmlp_norm_b, conv_b, conv_norm_g, conv_norm_b, g_out_a, g_out_b,
                            jnp.zeros((1, WA), F32)], axis=0)
    ws = w_spatial[0]
    bias_full = jnp.repeat(b_spatial[0].T, HD, axis=1)
    esel = (lax.broadcasted_iota(jnp.int32, (8, WA), 1) // HD == lax.broadcasted_iota(jnp.int32, (8, WA), 0)).astype(F32)

    x0 = x.reshape(T, D)
    (x1, gu1, y1), ((wmi, wmo, wo2, wi2a),) = _ffn_fwd(x0, mod1, gv1, wi1, wo1, tm, "ffn1_fwd", jobs=[g_mx])
    wmo = wmo.reshape(D, D)
    (x2, proj, ym, conv), ((wi2b,),) = _mixer_fwd(x1, mod2, gvm, wmi, wmo, v512, ws, bias_full, cw_full, tm, "mixer_fwd", jobs=[g_f2])

    (dx2, dg2, act2, hb2, dyb2, mg3, vg3, loss_blk), _ = _ffn_last(
        x2, loss_target.reshape(T, D), mod3, gv2, (wi2a, wi2b), wo2, tm, "ffn2_fwd_bwd")
    (g_wi2,), _ = _grad_w_in(dg2, hb2, "ffn2_gw_in")
    (g_wo2,), _ = _grad_w_out(act2, dyb2, "ffn2_gw_out")
    (dpart, dymb, ycat, mg2a, vgma, v5g, gws, gbs), ((p_wo2,),) = _mixer_bwd_a(
        dx2, ym, proj, conv, mod2, gvm, wmo, v512, ws, bias_full, esel, tm, "mixer_bwd_a",
        jobs=[_ChipScatter([g_wo2])])
    (dx1, dproj, hbm, mg2b, vgmb, dcw), ((p_wi2,),) = _mixer_bwd_b(
        dx2, x1, dpart, proj, mod2, gvm, wmi, cw_full, tm, "mixer_bwd_b", jobs=[_ChipScatter([g_wi2])])
    (g_wmi,), _ = _grad_w_mi(hbm, dproj, "mixer_gw_in")
    (g_wmo,), _ = _grad_w_mo(ycat, dymb, "mixer_gw_out")
    p2 = jnp.concatenate([v5g, dcw], axis=0)
    (dx0, dg1, act1, hb1, dyb1, mg1, vg1), _ = _ffn_bwd(dx1, x0, y1, gu1, mod1, gv1, wi1, wo1, tm, "ffn1_bwd")

    dada = jnp.concatenate([mg1[:, 0:3], mg2b[:, 0:2], mg2a[:, 2:3], mg3[:, 0:3]], axis=1)
    dada = dada.reshape(bl, NDEV, ADA_B).transpose(1, 0, 2)
    dada = jnp.pad(dada, ((0, 0), (0, 8 - bl), (0, 0)))
    p1 = jnp.concatenate([vg1[0:2], vgmb[0:1], vgma[1:2], vg3[0:2], loss_blk[0:1], zrow], axis=0)
    (dd_all, gb_all), ((a1,),) = _ada_bwd(dada, jobs=[_AllGather([p1])])
    g_bada = gb_all[:, 0, :].reshape(1, 9 * D)

    (g_wo1,), ((p_wmi, p_wmo),) = _grad_w_out(act1, dyb1, "ffn1_gw_out", jobs=[_ChipScatter([g_wmi, g_wmo])])
    (g_wi1,), ((a2, a3, a4), (p_wo1,)) = _grad_w_in(
        dg1, hb1, "ffn1_gw_in", jobs=[_Gather([p2, gws, gbs], ("rows",) * 3), _ChipScatter([g_wo1])])

    h_f1, token = _chip_scatter_start([g_wi1], "tail_start")

    res = {}
    quad = _adamw_reduce(p_wi2, w_f2_in[0].T, m_w_f2_in[0].T, v_w_f2_in[0].T, FO, "adamw_w_f2_in", after=token)
    res["w_f2_in"] = tuple(t.T[None] for t in quad)
    for nm, part, tr in (("w_f2_out", p_wo2, FO), ("w_mix_in", p_wmi, 256), ("w_mix_out", p_wmo, MO), ("w_f1_out", p_wo1, FO)):
        quad = _adamw_reduce(part, given[nm][0], given["m_" + nm][0], given["v_" + nm][0], tr, "adamw_" + nm, after=quad[1])
        res[nm] = tuple(t[None] for t in quad)
    quad = _adamw_ada(sc_all, dd_all, w_ada[0], m_w_ada[0], v_w_ada[0], 256, "adamw_w_ada", after=quad[1])
    res["w_ada"] = tuple(t[None] for t in quad)
    (g_wi1,), (p_wi1,) = _chip_scatter_wait(h_f1, quad[1], "tail_wait")
    quad = _adamw_reduce(p_wi1, w_f1_in[0].T, m_w_f1_in[0].T, v_w_f1_in[0].T, FO, "adamw_w_f1_in", own=g_wi1)
    res["w_f1_in"] = tuple(t.T[None] for t in quad)

    small = SMALL_D + SMALL_W + ("w_spatial", "b_spatial", "b_ada")
    grads = [(0, r) for r in range(6)] + [(1, r) for r in range(7)] + [(2, None), (3, None), (4, None)]
    wmv = []
    for nm in small:
        for pre in ("", "m_", "v_"):
            wmv.append(given[pre + nm][0] if nm in ("w_spatial", "b_spatial") else given[pre + nm])
    outs = _adamw_small([a1, a2, a3, a4], [g_bada], grads, wmv, (0, 1), "adamw_small")
    loss = outs[0][6, 0]
    for t, nm in enumerate(small):
        quad = outs[2 + 4 * t:6 + 4 * t]
        res[nm] = tuple(q[None] for q in quad) if nm in ("w_spatial", "b_spatial") else tuple(quad)
    g_cw = lax.dynamic_slice(outs[1], (8, mi * 64), (32, 64))
    wmv = [jnp.pad(given[pre + "conv_w"][0], ((0, 1), (0, 0)), constant_values=1.0 if pre == "v_" else 0.0)
           for pre in ("", "m_", "v_")]
    quad = _adamw_small([], [g_cw], [(0, None)], wmv, (), "adamw_conv_w")
    res["conv_w"] = tuple(q[0:CONV_K][None] for q in quad)

    order = ["w_ada", "b_ada", "g_pre_f1", "g_post_f1", "w_f1_in", "w_f1_out", "g_pre_m", "g_post_m", "w_mix_in",
             "gmlp_norm_g", "gmlp_norm_b", "w_spatial", "b_spatial", "conv_w", "conv_b", "conv_norm_g", "conv_norm_b",
             "g_out_a", "g_out_b", "w_mix_out", "g_pre_f2", "g_post_f2", "w_f2_in", "w_f2_out"]
    out = [loss, dx0.reshape(bl, seq, D)]
    for k in range(4):
        out += [res[nm][k] for nm in order]
    return tuple(out)
```

```python
import jax
import jax.numpy as jnp
from jax import lax
from jax.experimental import pallas as pl
from jax.experimental.pallas import tpu as pltpu

F32 = jnp.float32
BF16 = jnp.bfloat16

D = 1024
DFF = 2816
NDEV = 8
FB = 2 * DFF // NDEV
NCH = DFF // FB
LANES = 128
SUBL = 8
FO = DFF // NDEV
WA = 512
NSLAB = WA // LANES
NHEAD = 8
HD = 64
CHUNK = 128
CONV_K = 31
HALO = 32
MB = 2 * (WA + WA) // NDEV
MO = D // NDEV
ADA_B = 9 * D // NDEV
EPS = 1e-6
HALF = 0.5

ADAM_LR = 0.001
ADAM_B1 = 0.9
ADAM_B2 = 0.999
ADAM_EPS = 1e-08
ADAM_WD = 0.01
ADAM_STEP = 10

VMEM_LIMIT = 56 * 1024 * 1024
MESH = pl.DeviceIdType.MESH
FLIPS = ((0, 0, 1), (1, 0, 0), (0, 1, 0), (1, 1, 0), (1, 0, 1), (0, 1, 1), (1, 1, 1))
CHIP_FLIPS = ((1, 0, 0), (0, 1, 0), (1, 1, 0))
HBM = pl.BlockSpec(memory_space=pl.ANY)
VM = pl.BlockSpec(memory_space=pltpu.VMEM)


def _dot(a, b):
    return lax.dot_general(a, b, (((1,), (0,)), ((), ())), preferred_element_type=F32)


def _dot_nt(a, b):
    return lax.dot_general(a, b, (((1,), (1,)), ((), ())), preferred_element_type=F32)


def _dot_tn(a, b):
    return lax.dot_general(a, b, (((0,), (0,)), ((), ())), preferred_element_type=F32)


def _rowmean(v):
    return jnp.mean(v, axis=-1, keepdims=True)


def _colsum(v):
    return jnp.sum(v, axis=0, keepdims=True)


def _sigmoid(v):
    return 0.5 * jnp.tanh(0.5 * v) + 0.5


def _const_spec(shape):
    nd = len(shape)
    return pl.BlockSpec(shape, lambda *_: (0,) * nd, pipeline_mode=pl.Buffered(1))


def _me():
    return lax.axis_index("x"), lax.axis_index("y"), lax.axis_index("c")


def _flip(me, f):
    return tuple(1 - v if b else v for v, b in zip(me, f))


def _lin(p):
    return 4 * p[0] + 2 * p[1] + p[2]


def _remote(src, dst, send_sem, recv_sem, dev):
    return pltpu.make_async_remote_copy(src_ref=src, dst_ref=dst, send_sem=send_sem, recv_sem=recv_sem,
                                        device_id=dev, device_id_type=MESH)


def _blk(kind, ref, p):
    if kind == "out":
        return ref.at[2 * p[0] + p[1], pl.ds(p[2] * FO, FO), :]
    return ref.at[_lin(p)]


class _Gather:
    def __init__(self, shards, kinds, late_mid=False):
        self.late_mid = late_mid
        self.kinds = kinds
        self.n = len(shards)
        self.ins = list(shards)
        self.out_shape = [jax.ShapeDtypeStruct((4, FB, D) if k == "out" else (NDEV,) + s.shape, s.dtype)
                          for s, k in zip(shards, kinds)]
        self.sems = [pltpu.SemaphoreType.DMA((7 * self.n,)), pltpu.SemaphoreType.DMA((7 * self.n,)),
                     pltpu.SemaphoreType.DMA((self.n,))]

    def _first(self, ins, outs, sems):
        ssem, rsem, lsem = sems
        me = _me()
        sib = _flip(me, (0, 0, 1))
        cps, loc = [], []
        for a in range(self.n):
            mine = _blk(self.kinds[a], outs[a], me)
            loc.append(pltpu.make_async_copy(ins[a], mine, lsem.at[a]))
            cps.append(_remote(ins[a], mine, ssem.at[7 * a], rsem.at[7 * a], sib))
            for j, f in enumerate(CHIP_FLIPS):
                cps.append(_remote(ins[a], mine, ssem.at[7 * a + 1 + j], rsem.at[7 * a + 1 + j], _flip(me, f)))
        return cps, loc

    def _passed(self, outs, sems):
        ssem, rsem, _ = sems
        me = _me()
        sib = _flip(me, (0, 0, 1))
        cps = []
        for j, f in enumerate(CHIP_FLIPS):
            for a in range(self.n):
                blk = _blk(self.kinds[a], outs[a], _flip(me, f))
                cps.append(_remote(blk, blk, ssem.at[7 * a + 4 + j], rsem.at[7 * a + 4 + j], sib))
        return cps

    def start(self, ins, outs, sems):
        cps, loc = self._first(ins, outs, sems)
        for cp in loc + cps:
            cp.start()

    def mid(self, ins, outs, sems):
        ssem, rsem, _ = sems
        me = _me()
        passed = self._passed(outs, sems)
        t = 0
        for j, f in enumerate(CHIP_FLIPS):
            for a in range(self.n):
                blk = _blk(self.kinds[a], outs[a], _flip(me, f))
                _remote(blk, blk, ssem.at[7 * a + 1 + j], rsem.at[7 * a + 1 + j], _flip(me, f)).wait_recv()
                passed[t].start()
                t += 1

    def end(self, ins, outs, sems):
        ssem, rsem, _ = sems
        me = _me()
        sib = _flip(me, (0, 0, 1))
        for a in range(self.n):
            blk = _blk(self.kinds[a], outs[a], sib)
            _remote(blk, blk, ssem.at[7 * a], rsem.at[7 * a], sib).wait_recv()
            for j, f in enumerate(CHIP_FLIPS):
                blk = _blk(self.kinds[a], outs[a], _flip(_flip(me, f), (0, 0, 1)))
                _remote(blk, blk, ssem.at[7 * a + 4 + j], rsem.at[7 * a + 4 + j], sib).wait_recv()
        cps, loc = self._first(ins, outs, sems)
        for cp in cps + self._passed(outs, sems):
            cp.wait_send()
        for cp in loc:
            cp.wait()


class _ChipScatter:
    def __init__(self, grads):
        self.n = len(grads)
        self.ins = list(grads)
        self.out_shape = [jax.ShapeDtypeStruct(g.shape, BF16) for g in grads]
        self.sems = [pltpu.SemaphoreType.DMA((3 * self.n,)), pltpu.SemaphoreType.DMA((3 * self.n,)),
                     pltpu.SemaphoreType.DMA((self.n,))]

    def _copies(self, ins, outs, sems):
        ssem, rsem, lsem = sems
        me = _me()
        mq = 2 * me[0] + me[1]
        loc = [pltpu.make_async_copy(ins[a].at[mq], outs[a].at[mq], lsem.at[a]) for a in range(self.n)]
        cps = []
        for k, f in enumerate(CHIP_FLIPS):
            p = _flip(me, f)
            for a in range(self.n):
                cps.append(_remote(ins[a].at[2 * p[0] + p[1]], outs[a].at[mq], ssem.at[3 * a + k], rsem.at[3 * a + k], p))
        return cps, loc

    def start(self, ins, outs, sems):
        cps, loc = self._copies(ins, outs, sems)
        for cp in loc + cps:
            cp.start()

    mid = None

    def end(self, ins, outs, sems):
        ssem, rsem, _ = sems
        me = _me()
        mq = 2 * me[0] + me[1]
        for k, f in enumerate(CHIP_FLIPS):
            p = _flip(me, f)
            for a in range(self.n):
                _remote(ins[a].at[mq], outs[a].at[2 * p[0] + p[1]], ssem.at[3 * a + k], rsem.at[3 * a + k], p).wait_recv()
        cps, loc = self._copies(ins, outs, sems)
        for cp in cps:
            cp.wait_send()
        for cp in loc:
            cp.wait()


class _AllGather:
    def __init__(self, parts):
        self.n = len(parts)
        self.ins = list(parts)
        self.out_shape = [jax.ShapeDtypeStruct((NDEV,) + p.shape, p.dtype) for p in parts]
        self.sems = [pltpu.SemaphoreType.DMA((7 * self.n,)), pltpu.SemaphoreType.DMA((7 * self.n,)),
                     pltpu.SemaphoreType.DMA((self.n,))]

    def _copies(self, ins, outs, sems):
        ssem, rsem, lsem = sems
        me = _me()
        mi = _lin(me)
        loc = [pltpu.make_async_copy(ins[a], outs[a].at[mi], lsem.at[a]) for a in range(self.n)]
        cps = []
        for k, f in enumerate(FLIPS):
            for a in range(self.n):
                cps.append(_remote(ins[a], outs[a].at[mi], ssem.at[7 * a + k], rsem.at[7 * a + k], _flip(me, f)))
        return cps, loc

    def start(self, ins, outs, sems):
        cps, loc = self._copies(ins, outs, sems)
        for cp in loc + cps:
            cp.start()

    mid = None

    def end(self, ins, outs, sems):
        ssem, rsem, _ = sems
        me = _me()
        for k, f in enumerate(FLIPS):
            p = _flip(me, f)
            for a in range(self.n):
                _remote(ins[a], outs[a].at[_lin(p)], ssem.at[7 * a + k], rsem.at[7 * a + k], p).wait_recv()
        cps, loc = self._copies(ins, outs, sems)
        for cp in cps:
            cp.wait_send()
        for cp in loc:
            cp.wait()


def _call(core, *, name, grid, in_specs, out_specs, out_shape, args, scratch=(), jobs=(), core_starts=False):
    n_in, n_out, n_sc = len(in_specs), len(out_specs), len(scratch)
    steps = 1
    for g in grid:
        steps *= g

    def body(*refs):
        pos = [0]

        def take(k):
            r = refs[pos[0]:pos[0] + k]
            pos[0] += k
            return r

        ins = take(n_in)
        j_ins = [take(len(j.ins)) for j in jobs]
        outs = take(n_out)
        j_outs = [take(len(j.out_shape)) for j in jobs]
        scs = take(n_sc)
        j_sems = [take(len(j.sems)) for j in jobs]
        if len(grid) == 2:
            step = pl.program_id(0) * grid[1] + pl.program_id(1)
        elif len(grid) == 1:
            step = pl.program_id(0)
        else:
            step = 0
        def start_jobs():
            for j, ji, jo, js in zip(jobs, j_ins, j_outs, j_sems):
                j.start(ji, jo, js)

        if grid:
            pl.when(step == 0)(start_jobs)
        elif not core_starts:
            start_jobs()
        for j, ji, jo, js in zip(jobs, j_ins, j_outs, j_sems):
            if j.mid is not None and grid:
                at = max(steps - 2, 0) if j.late_mid else (3 * steps) // 4
                pl.when(step == at)(lambda j=j, ji=ji, jo=jo, js=js: j.mid(ji, jo, js))
        if core_starts:
            core(ins, outs, scs, start_jobs)
        elif core is not None:
            core(ins, outs, scs)
        for j, ji, jo, js in zip(jobs, j_ins, j_outs, j_sems):
            if grid:
                pl.when(step == steps - 1)(lambda j=j, ji=ji, jo=jo, js=js: j.end(ji, jo, js))
            else:
                if j.mid is not None:
                    j.mid(ji, jo, js)
                j.end(ji, jo, js)

    all_in = list(in_specs)
    all_args = list(args)
    all_out = list(out_specs)
    all_shape = list(out_shape)
    all_sc = list(scratch)
    for j in jobs:
        all_in += [HBM] * len(j.ins)
        all_args += j.ins
    for j in jobs:
        all_out += [HBM] * len(j.out_shape)
        all_shape += j.out_shape
        all_sc += j.sems
    params = dict(vmem_limit_bytes=VMEM_LIMIT)
    if grid:
        params["dimension_semantics"] = ("arbitrary",) * len(grid)
    res = pl.pallas_call(
        body, name=name, grid=grid, in_specs=all_in, out_specs=all_out, out_shape=all_shape,
        scratch_shapes=all_sc, compiler_params=pltpu.CompilerParams(**params),
    )(*all_args)
    core_res = list(res[:n_out])
    job_res = []
    pos = n_out
    for j in jobs:
        job_res.append(list(res[pos:pos + len(j.out_shape)]))
        pos += len(j.out_shape)
    return core_res, job_res


def _ffn_fwd(x, mod, gvec, w_in, w_out, tm, name, jobs=()):
    T = x.shape[0]
    nt = T // tm
    tps = nt // mod.shape[0]

    def core(ins, outs, _):
        x_ref, mod_ref, g_ref, win_ref, wout_ref = ins
        xo_ref, gu_ref, y_ref = outs
        xv = x_ref[...]
        sh, sc, gt = mod_ref[0:1, :], mod_ref[1:2, :], mod_ref[2:3, :]
        r = lax.rsqrt(_rowmean(xv * xv) + EPS)
        h = (xv * r * g_ref[0:1, :]) * (1.0 + sc) + sh
        hb = h.astype(BF16)
        y = jnp.zeros((tm, D), F32)
        for cidx in range(NCH):
            gate = _dot_nt(hb, win_ref[cidx])
            up = _dot_nt(hb, win_ref[NCH + cidx])
            gu_ref[cidx] = gate.astype(BF16)
            gu_ref[NCH + cidx] = up.astype(BF16)
            act = gate * _sigmoid(gate) * up
            y = y + _dot(act.astype(BF16), wout_ref[cidx])
        y_ref[...] = y
        ry = lax.rsqrt(_rowmean(y * y) + EPS)
        xo_ref[...] = xv + (HALF * gt) * (y * ry * g_ref[1:2, :])

    tile = pl.BlockSpec((tm, D), lambda i: (i, 0))
    return _call(
        core, name=name, grid=(nt,), jobs=jobs,
        in_specs=[tile, pl.BlockSpec((None, 8, D), lambda i: (i // tps, 0, 0)), _const_spec((8, D)),
                  _const_spec((8, FB, D)), _const_spec((4, FB, D))],
        out_specs=[tile, pl.BlockSpec((8, tm, FB), lambda i: (0, i, 0)), tile],
        out_shape=[jax.ShapeDtypeStruct((T, D), F32), jax.ShapeDtypeStruct((8, T, FB), BF16),
                   jax.ShapeDtypeStruct((T, D), F32)],
        args=[x, mod, gvec, w_in, w_out])


def _ffn_bwd(dxo, x, y, gu, mod, gvec, w_in, w_out, tm, name, jobs=()):
    T = x.shape[0]
    nt = T // tm
    nb = mod.shape[0]
    tps = nt // nb

    def core(ins, outs, _):
        dxo_ref, x_ref, y_ref, gu_ref, mod_ref, g_ref, win_ref, wout_ref = ins
        dx_ref, dg_ref, act_ref, hb_ref, dyb_ref, mg_ref, vg_ref = outs
        i = pl.program_id(0)
        xv = x_ref[...]
        dxo_v = dxo_ref[...]
        yv = y_ref[...]
        sh, sc, gt = mod_ref[0:1, :], mod_ref[1:2, :], mod_ref[2:3, :]
        gpre, gpost = g_ref[0:1, :], g_ref[1:2, :]
        r = lax.rsqrt(_rowmean(xv * xv) + EPS)
        xh = xv * r
        n = xh * gpre
        hb = (n * (1.0 + sc) + sh).astype(BF16)
        hb_ref[...] = hb
        ry = lax.rsqrt(_rowmean(yv * yv) + EPS)
        yh = yv * ry
        d_gt = _colsum(HALF * dxo_v * (yh * gpost))
        dp = (HALF * gt) * dxo_v
        d_gpost = _colsum(dp * yh)
        dyh = dp * gpost
        dy = ry * (dyh - yh * _rowmean(dyh * yh))
        dyb = dy.astype(BF16)
        dyb_ref[...] = dyb
        dh = jnp.zeros((tm, D), F32)
        for cidx in range(NCH):
            gate = gu_ref[cidx].astype(F32)
            up = gu_ref[NCH + cidx].astype(F32)
            sig = _sigmoid(gate)
            s = gate * sig
            act_ref[cidx] = (s * up).astype(BF16)
            d_act = _dot_nt(dyb, wout_ref[cidx])
            d_up = (d_act * s).astype(BF16)
            d_gate = (d_act * up * (sig * (1.0 + gate * (1.0 - sig)))).astype(BF16)
            dg_ref[cidx] = d_gate
            dg_ref[NCH + cidx] = d_up
            dh = dh + _dot(d_gate, win_ref[cidx]) + _dot(d_up, win_ref[NCH + cidx])
        d_sc = _colsum(dh * n)
        d_sh = _colsum(dh)
        dn = dh * (1.0 + sc)
        d_gpre = _colsum(dn * xh)
        dxh = dn * gpre
        dx_ref[...] = dxo_v + r * (dxh - xh * _rowmean(dxh * xh))

        @pl.when(i % tps == 0)
        def _():
            mg_ref[...] = jnp.zeros((8, D), F32)

        @pl.when(i == 0)
        def _():
            vg_ref[...] = jnp.zeros((8, D), F32)

        mg_ref[0:1, :] += d_sh
        mg_ref[1:2, :] += d_sc
        mg_ref[2:3, :] += d_gt
        vg_ref[0:1, :] += d_gpre
        vg_ref[1:2, :] += d_gpost

    tile = pl.BlockSpec((tm, D), lambda i: (i, 0))
    return _call(
        core, name=name, grid=(nt,), jobs=jobs,
        in_specs=[tile, tile, tile, pl.BlockSpec((8, tm, FB), lambda i: (0, i, 0)),
                  pl.BlockSpec((None, 8, D), lambda i: (i // tps, 0, 0)), _const_spec((8, D)),
                  _const_spec((8, FB, D)), _const_spec((4, FB, D))],
        out_specs=[tile, pl.BlockSpec((8, tm, FB), lambda i: (0, i, 0)),
                   pl.BlockSpec((4, tm, FB), lambda i: (0, i, 0)), tile, tile,
                   pl.BlockSpec((None, 8, D), lambda i: (i // tps, 0, 0)), pl.BlockSpec((8, D), lambda i: (0, 0))],
        out_shape=[jax.ShapeDtypeStruct((T, D), F32), jax.ShapeDtypeStruct((8, T, FB), BF16),
                   jax.ShapeDtypeStruct((4, T, FB), BF16), jax.ShapeDtypeStruct((T, D), BF16),
                   jax.ShapeDtypeStruct((T, D), BF16), jax.ShapeDtypeStruct((nb, 8, D), F32),
                   jax.ShapeDtypeStruct((8, D), F32)],
        args=[dxo, x, y, gu, mod, gvec, w_in, w_out])


def _ffn_last(x, target, mod, gvec, w_in, w_out, tm, name, jobs=()):
    T = x.shape[0]
    nt = T // tm
    nb = mod.shape[0]
    tps = nt // nb

    def core(ins, outs, scs):
        x_ref, t_ref, mod_ref, g_ref, wina_ref, winb_ref, wout_ref = ins
        dx_ref, dg_ref, act_ref, hb_ref, dyb_ref, mg_ref, vg_ref, loss_ref = outs
        hd2 = w_in[0].shape[2]
        (gu_s,) = scs
        i = pl.program_id(0)
        xv = x_ref[...]
        sh, sc, gt = mod_ref[0:1, :], mod_ref[1:2, :], mod_ref[2:3, :]
        gpre, gpost = g_ref[0:1, :], g_ref[1:2, :]
        r = lax.rsqrt(_rowmean(xv * xv) + EPS)
        xh = xv * r
        n = xh * gpre
        hb = (n * (1.0 + sc) + sh).astype(BF16)
        hb_ref[...] = hb
        hba, hbb = hb[:, 0:hd2], hb[:, hd2:D]
        yv = jnp.zeros((tm, D), F32)
        for cidx in range(NCH):
            gate = _dot_nt(hba, wina_ref[cidx]) + _dot_nt(hbb, winb_ref[cidx])
            up = _dot_nt(hba, wina_ref[NCH + cidx]) + _dot_nt(hbb, winb_ref[NCH + cidx])
            gu_s[cidx] = gate.astype(BF16)
            gu_s[NCH + cidx] = up.astype(BF16)
            act = gate * _sigmoid(gate) * up
            act_ref[cidx] = act.astype(BF16)
            yv = yv + _dot(act_ref[cidx], wout_ref[cidx])
        ry = lax.rsqrt(_rowmean(yv * yv) + EPS)
        yh = yv * ry
        pn = yh * gpost
        err = xv + (HALF * gt) * pn - t_ref[...]
        dxo_v = err * (1.0 / D)
        d_gt = _colsum(HALF * dxo_v * pn)
        dp = (HALF * gt) * dxo_v
        d_gpost = _colsum(dp * yh)
        dyh = dp * gpost
        dyb = (ry * (dyh - yh * _rowmean(dyh * yh))).astype(BF16)
        dyb_ref[...] = dyb
        dha = jnp.zeros((tm, hd2), F32)
        dhb = jnp.zeros((tm, D - hd2), F32)
        for cidx in range(NCH):
            gate = gu_s[cidx].astype(F32)
            up = gu_s[NCH + cidx].astype(F32)
            sig = _sigmoid(gate)
            s = gate * sig
            d_act = _dot_nt(dyb, wout_ref[cidx])
            d_up = (d_act * s).astype(BF16)
            d_gate = (d_act * up * (sig * (1.0 + gate * (1.0 - sig)))).astype(BF16)
            dg_ref[cidx] = d_gate
            dg_ref[NCH + cidx] = d_up
            dha = dha + _dot(d_gate, wina_ref[cidx]) + _dot(d_up, wina_ref[NCH + cidx])
            dhb = dhb + _dot(d_gate, winb_ref[cidx]) + _dot(d_up, winb_ref[NCH + cidx])
        dh = jnp.concatenate([dha, dhb], axis=1)
        d_sc = _colsum(dh * n)
        d_sh = _colsum(dh)
        dn = dh * (1.0 + sc)
        d_gpre = _colsum(dn * xh)
        dxh = dn * gpre
        dx_ref[...] = dxo_v + r * (dxh - xh * _rowmean(dxh * xh))

        @pl.when(i % tps == 0)
        def _():
            mg_ref[...] = jnp.zeros((8, D), F32)

        @pl.when(i == 0)
        def _():
            vg_ref[...] = jnp.zeros((8, D), F32)
            loss_ref[...] = jnp.zeros((8, D), F32)

        mg_ref[0:1, :] += d_sh
        mg_ref[1:2, :] += d_sc
        mg_ref[2:3, :] += d_gt
        vg_ref[0:1, :] += d_gpre
        vg_ref[1:2, :] += d_gpost
        loss_ref[...] += HALF * jnp.sum(_rowmean(err * err), axis=0, keepdims=True)

    tile = pl.BlockSpec((tm, D), lambda i: (i, 0))
    return _call(
        core, name=name, grid=(nt,), jobs=jobs,
        in_specs=[tile, tile, pl.BlockSpec((None, 8, D), lambda i: (i // tps, 0, 0)), _const_spec((8, D)),
                  _const_spec(w_in[0].shape), _const_spec(w_in[1].shape), _const_spec((4, FB, D))],
        out_specs=[tile, pl.BlockSpec((8, tm, FB), lambda i: (0, i, 0)),
                   pl.BlockSpec((4, tm, FB), lambda i: (0, i, 0)), tile, tile,
                   pl.BlockSpec((None, 8, D), lambda i: (i // tps, 0, 0)), pl.BlockSpec((8, D), lambda i: (0, 0)),
                   pl.BlockSpec((8, D), lambda i: (0, 0))],
        out_shape=[jax.ShapeDtypeStruct((T, D), F32), jax.ShapeDtypeStruct((8, T, FB), BF16),
                   jax.ShapeDtypeStruct((4, T, FB), BF16), jax.ShapeDtypeStruct((T, D), BF16),
                   jax.ShapeDtypeStruct((T, D), BF16), jax.ShapeDtypeStruct((nb, 8, D), F32),
                   jax.ShapeDtypeStruct((8, D), F32), jax.ShapeDtypeStruct((8, D), F32)],
        scratch=[pltpu.VMEM((8, tm, FB), BF16)],
        args=[x, target, mod, gvec, w_in[0], w_in[1], w_out])


def _masked_spatial(ws_ref):
    row = lax.broadcasted_iota(jnp.int32, (CHUNK, CHUNK), 0)
    col = lax.broadcasted_iota(jnp.int32, (CHUNK, CHUNK), 1)
    keep = col <= row
    return [jnp.where(keep, ws_ref[hd], 0.0).astype(BF16) for hd in range(NHEAD)]


def _head_pairs(mats, right, transpose=False):
    first = lax.broadcasted_iota(jnp.int32, (CHUNK, LANES), 1) < HD
    op = _dot_tn if transpose else _dot
    out = []
    for p in range(NHEAD // 2):
        slab = right[:, _lanes(p)]
        out.append(jnp.where(first, op(mats[2 * p], slab), op(mats[2 * p + 1], slab)))
    return jnp.concatenate(out, axis=1)


def _spatial_gate(wm, vb_chunk):
    return _head_pairs(wm, vb_chunk)


def _layer_norm_stats(v):
    mu = _rowmean(v)
    vc = v - mu
    rstd = lax.rsqrt(_rowmean(vc * vc) + EPS)
    return vc * rstd, rstd


def _pitch(tm):
    p = tm // 8
    while p % 8 != 4:
        p += 1
    return p


def _lanes(s):
    return slice(s * LANES, (s + 1) * LANES)


def _to_slabs(ref, row0, val):
    for s in range(NSLAB):
        ref[s, row0:row0 + val.shape[0], :] = val[:, _lanes(s)]


def _tap_sum(src, out, cw_ref, bias, tm, start):
    p = _pitch(tm)
    for s in range(NSLAB):
        accs = [jnp.broadcast_to(bias[:, _lanes(s)], (SUBL, LANES))] * p
        for k in range(CONV_K):
            w = jnp.broadcast_to(cw_ref[k:k + 1, _lanes(s)], (SUBL, LANES))
            for v in range(p):
                accs[v] = accs[v] + w * src[s, pl.ds(v + start(k), 8, stride=p), :]
        for v in range(p):
            out[s, pl.ds(v, 8, stride=p), :] = accs[v]
    return jnp.concatenate([out[s, 0:tm, :] for s in range(NSLAB)], axis=1)


def _mixer_fwd(x, mod, gvec, w_mi, w_mo, v512, ws, bias_full, cw, tm, name, jobs=()):
    T = x.shape[0]
    nt = T // tm
    tps = nt // mod.shape[0]
    ext_rows = 8 * _pitch(tm)

    def core(ins, outs, scs):
        x_ref, mod_ref, g_ref, wmi_ref, wmo_ref, v_ref, ws_ref, bias_ref, cw_ref = ins
        xo_ref, proj_ref, ym_ref, conv_ref = outs
        glu_ext, conv_scr = scs
        i = pl.program_id(0)
        xv = x_ref[...]
        sh, sc, gt = mod_ref[0:1, :], mod_ref[1:2, :], mod_ref[2:3, :]
        r = lax.rsqrt(_rowmean(xv * xv) + EPS)
        hb = ((xv * r * g_ref[0:1, :]) * (1.0 + sc) + sh).astype(BF16)
        for j in range(NDEV):
            proj_ref[:, j * MB:(j + 1) * MB] = _dot(hb, wmi_ref[j])
        u = proj_ref[:, 0:WA]
        v0 = proj_ref[:, WA:2 * WA]
        a = proj_ref[:, 2 * WA:3 * WA]
        g = proj_ref[:, 3 * WA:4 * WA]
        vh, _ = _layer_norm_stats(v0)
        vb = (vh * v_ref[0:1, :] + v_ref[1:2, :]).astype(BF16)
        wm = _masked_spatial(ws_ref)
        ya = []
        for q in range(tm // CHUNK):
            z = _spatial_gate(wm, vb[q * CHUNK:(q + 1) * CHUNK, :]) + bias_ref[...]
            ya.append(u[q * CHUNK:(q + 1) * CHUNK, :] * z)
        ya = jnp.concatenate(ya, axis=0)
        glu = a * _sigmoid(g)

        @pl.when(i == 0)
        def _():
            glu_ext[:, HALO + tm:HALO + ext_rows, :] = jnp.zeros((NSLAB, ext_rows - tm, LANES), F32)

        @pl.when(i % tps == 0)
        def _():
            glu_ext[:, 0:HALO, :] = jnp.zeros((NSLAB, HALO, LANES), F32)

        _to_slabs(glu_ext, HALO, glu)
        conv = _tap_sum(glu_ext, conv_scr, cw_ref, v_ref[2:3, :], tm, lambda k: HALO - (CONV_K - 1) + k)
        conv_ref[...] = conv
        glu_ext[:, 0:HALO, :] = glu_ext[:, tm:tm + HALO, :]
        ch, _ = _layer_norm_stats(conv)
        cn = ch * v_ref[3:4, :] + v_ref[4:5, :]
        yb = cn * _sigmoid(cn)
        pa = ya * lax.rsqrt(_rowmean(ya * ya) + EPS) * v_ref[5:6, :]
        pb = yb * lax.rsqrt(_rowmean(yb * yb) + EPS) * v_ref[6:7, :]
        ycat = jnp.concatenate([pa, pb], axis=1).astype(BF16)
        ym = _dot(ycat, wmo_ref[...])
        ym_ref[...] = ym
        rm = lax.rsqrt(_rowmean(ym * ym) + EPS)
        xo_ref[...] = xv + gt * (ym * rm * g_ref[1:2, :])

    tile = pl.BlockSpec((tm, D), lambda i: (i, 0))
    return _call(
        core, name=name, grid=(nt,), jobs=jobs,
        in_specs=[tile, pl.BlockSpec((None, 8, D), lambda i: (i // tps, 0, 0)), _const_spec((8, D)),
                  _const_spec((NDEV, D, MB)), _const_spec((D, D)), _const_spec((8, WA)),
                  _const_spec((NHEAD, CHUNK, CHUNK)), _const_spec((CHUNK, WA)), _const_spec((32, WA))],
        out_specs=[tile, pl.BlockSpec((tm, 4 * WA), lambda i: (i, 0)), tile, pl.BlockSpec((tm, WA), lambda i: (i, 0))],
        out_shape=[jax.ShapeDtypeStruct((T, D), F32), jax.ShapeDtypeStruct((T, 4 * WA), F32),
                   jax.ShapeDtypeStruct((T, D), F32), jax.ShapeDtypeStruct((T, WA), F32)],
        scratch=[pltpu.VMEM((NSLAB, HALO + ext_rows, LANES), F32), pltpu.VMEM((NSLAB, ext_rows, LANES), F32)],
        args=[x, mod, gvec, w_mi, w_mo, v512, ws, bias_full, cw])


def _mixer_bwd_a(dxo, ym, proj, conv, mod, gvec, w_mo, v512, ws, bias_full, esel, tm, name, jobs=()):
    T = dxo.shape[0]
    nt = T // tm
    nb = mod.shape[0]
    tps = nt // nb

    def core(ins, outs, scs):
        dxo_ref, ym_ref, proj_ref, conv_ref, mod_ref, g_ref, wmo_ref, v_ref, ws_ref, bias_ref, e_ref = ins
        dpart_ref, dymb_ref, ycat_ref, mg_ref, vg_ref, v5g_ref, gws_ref, gbs_ref = outs
        (dbs_acc,) = scs
        i = pl.program_id(0)
        dxo_v = dxo_ref[...]
        ymv = ym_ref[...]
        gt = mod_ref[2:3, :]
        gpost = g_ref[1:2, :]
        rm = lax.rsqrt(_rowmean(ymv * ymv) + EPS)
        ymh = ymv * rm
        d_gt = _colsum(dxo_v * (ymh * gpost))
        dpm = gt * dxo_v
        d_gpost = _colsum(dpm * ymh)
        dymh = dpm * gpost
        dym = (rm * (dymh - ymh * _rowmean(dymh * ymh))).astype(BF16)
        dymb_ref[...] = dym
        dycat = _dot_nt(dym, wmo_ref[...])
        u = proj_ref[:, 0:WA]
        v0 = proj_ref[:, WA:2 * WA]
        vh, rv = _layer_norm_stats(v0)
        vb = (vh * v_ref[0:1, :] + v_ref[1:2, :]).astype(BF16)
        wm = _masked_spatial(ws_ref)
        zs = []
        for q in range(tm // CHUNK):
            zs.append(_spatial_gate(wm, vb[q * CHUNK:(q + 1) * CHUNK, :]) + bias_ref[...])
        z = jnp.concatenate(zs, axis=0)
        ya = u * z
        ra = lax.rsqrt(_rowmean(ya * ya) + EPS)
        yah = ya * ra
        ch, rc = _layer_norm_stats(conv_ref[...])
        cn = ch * v_ref[3:4, :] + v_ref[4:5, :]
        sg = _sigmoid(cn)
        yb = cn * sg
        rb = lax.rsqrt(_rowmean(yb * yb) + EPS)
        ybh = yb * rb
        ycat_ref[...] = jnp.concatenate([yah * v_ref[5:6, :], ybh * v_ref[6:7, :]], axis=1).astype(BF16)
        dpa = dycat[:, 0:WA]
        dpb = dycat[:, WA:2 * WA]
        d_goa = _colsum(dpa * yah)
        d_gob = _colsum(dpb * ybh)
        dyah = dpa * v_ref[5:6, :]
        dybh = dpb * v_ref[6:7, :]
        dya = ra * (dyah - yah * _rowmean(dyah * yah))
        dyb = rb * (dybh - ybh * _rowmean(dybh * ybh))
        dpart_ref[:, 0:WA] = dya * z
        dz = dya * u

        @pl.when(i == 0)
        def _():
            gws_ref[...] = jnp.zeros((NHEAD, CHUNK, CHUNK), F32)
            dbs_acc[...] = jnp.zeros((CHUNK, WA), F32)
            vg_ref[...] = jnp.zeros((8, D), F32)
            v5g_ref[...] = jnp.zeros((8, WA), F32)

        first = lax.broadcasted_iota(jnp.int32, (CHUNK, LANES), 1) < HD
        dvs = []
        for q in range(tm // CHUNK):
            dz_q = dz[q * CHUNK:(q + 1) * CHUNK, :]
            vb_q = vb[q * CHUNK:(q + 1) * CHUNK, :]
            dbs_acc[...] += dz_q
            dzb = dz_q.astype(BF16)
            dvs.append(_head_pairs(wm, dzb, transpose=True))
            for hd in range(NHEAD):
                slab = dzb[:, _lanes(hd // 2)]
                dz_hd = jnp.where(first if hd % 2 == 0 else jnp.logical_not(first), slab, jnp.zeros_like(slab))
                gws_ref[hd] += _dot_nt(dz_hd, vb_q[:, _lanes(hd // 2)])
        dv = jnp.concatenate(dvs, axis=0)
        d_gng = _colsum(dv * vh)
        d_gnb = _colsum(dv)
        dvh = dv * v_ref[0:1, :]
        dpart_ref[:, WA:2 * WA] = rv * (dvh - _rowmean(dvh) - vh * _rowmean(dvh * vh))
        dcn = dyb * (sg * (1.0 + cn * (1.0 - sg)))
        d_cng = _colsum(dcn * ch)
        d_cnb = _colsum(dcn)
        dch = dcn * v_ref[3:4, :]
        dconv = rc * (dch - _rowmean(dch) - ch * _rowmean(dch * ch))
        dpart_ref[:, 2 * WA:3 * WA] = dconv
        dpart_ref[:, 3 * WA:4 * WA] = jnp.zeros((tm, WA), F32)
        d_cb = _colsum(dconv)

        @pl.when(i % tps == 0)
        def _():
            mg_ref[...] = jnp.zeros((8, D), F32)

        mg_ref[2:3, :] += d_gt
        vg_ref[1:2, :] += d_gpost
        v5g_ref[0:1, :] += d_gng
        v5g_ref[1:2, :] += d_gnb
        v5g_ref[2:3, :] += d_cb
        v5g_ref[3:4, :] += d_cng
        v5g_ref[4:5, :] += d_cnb
        v5g_ref[5:6, :] += d_goa
        v5g_ref[6:7, :] += d_gob

        @pl.when(i == nt - 1)
        def _():
            row = lax.broadcasted_iota(jnp.int32, (CHUNK, CHUNK), 0)
            col = lax.broadcasted_iota(jnp.int32, (CHUNK, CHUNK), 1)
            for hd in range(NHEAD):
                gws_ref[hd] = jnp.where(col <= row, gws_ref[hd], 0.0)
            gbs_ref[...] = lax.dot_general(e_ref[...], dbs_acc[...], (((1,), (1,)), ((), ())),
                                           precision=lax.Precision.HIGHEST, preferred_element_type=F32)

    tile = pl.BlockSpec((tm, D), lambda i: (i, 0))
    ptile = pl.BlockSpec((tm, 4 * WA), lambda i: (i, 0))
    return _call(
        core, name=name, grid=(nt,), jobs=jobs,
        in_specs=[tile, tile, pl.BlockSpec((tm, 2 * WA), lambda i: (i, 0)), pl.BlockSpec((tm, WA), lambda i: (i, 0)),
                  pl.BlockSpec((None, 8, D), lambda i: (i // tps, 0, 0)), _const_spec((8, D)), _const_spec((D, D)),
                  _const_spec((8, WA)), _const_spec((NHEAD, CHUNK, CHUNK)), _const_spec((CHUNK, WA)),
                  _const_spec((8, WA))],
        out_specs=[ptile, tile, tile, pl.BlockSpec((None, 8, D), lambda i: (i // tps, 0, 0)),
                   pl.BlockSpec((8, D), lambda i: (0, 0)), pl.BlockSpec((8, WA), lambda i: (0, 0)),
                   pl.BlockSpec((NHEAD, CHUNK, CHUNK), lambda i: (0, 0, 0)), pl.BlockSpec((8, CHUNK), lambda i: (0, 0))],
        out_shape=[jax.ShapeDtypeStruct((T, 4 * WA), F32), jax.ShapeDtypeStruct((T, D), BF16),
                   jax.ShapeDtypeStruct((T, D), BF16), jax.ShapeDtypeStruct((nb, 8, D), F32),
                   jax.ShapeDtypeStruct((8, D), F32), jax.ShapeDtypeStruct((8, WA), F32),
                   jax.ShapeDtypeStruct((NHEAD, CHUNK, CHUNK), F32), jax.ShapeDtypeStruct((8, CHUNK), F32)],
        scratch=[pltpu.VMEM((CHUNK, WA), F32)],
        args=[dxo, ym, proj, conv, mod, gvec, w_mo, v512, ws, bias_full, esel])


def _mixer_bwd_b(dxo, x, dpart, proj, mod, gvec, w_mi, cw, tm, name, jobs=()):
    T = x.shape[0]
    nt = T // tm
    nb = mod.shape[0]
    tps = nt // nb
    hpt = tm // HALO
    nh = T // HALO
    off = HALO - (CONV_K - 1)
    p = _pitch(tm)
    ext_rows = 8 * p

    def core(ins, outs, scs):
        dxo_ref, x_ref, dpart_ref, dnext_ref, ag_ref, halo_ref, mod_ref, g_ref, wmi_ref, cw_ref = ins
        dx_ref, dproj_ref, hb_ref, mg_ref, vg_ref, dcw_ref = outs
        glu_ext, dconv_ext, dglu_scr, dcw_acc = scs
        i = pl.program_id(0)
        first = i % tps == 0
        last = i % tps == tps - 1
        a = ag_ref[:, 0:WA]
        g = ag_ref[:, WA:2 * WA]
        sgg = _sigmoid(g)

        @pl.when(i == 0)
        def _():
            glu_ext[:, HALO + tm:HALO + ext_rows, :] = jnp.zeros((NSLAB, ext_rows - tm, LANES), F32)
            dconv_ext[:, HALO + tm:HALO + ext_rows, :] = jnp.zeros((NSLAB, ext_rows - tm, LANES), F32)
            dcw_acc[...] = jnp.zeros((32, 8, WA), F32)
            vg_ref[...] = jnp.zeros((8, D), F32)

        _to_slabs(glu_ext, 0, jnp.where(first, 0.0, halo_ref[:, 0:WA] * _sigmoid(halo_ref[:, WA:2 * WA])))
        _to_slabs(glu_ext, HALO, a * sgg)
        _to_slabs(dconv_ext, 0, dpart_ref[:, 2 * WA:3 * WA])
        _to_slabs(dconv_ext, tm, jnp.where(last, 0.0, dnext_ref[...]))
        sub = lax.broadcasted_iota(jnp.int32, (SUBL, LANES), 0)
        for s in range(NSLAB):
            accs = [jnp.zeros((SUBL, LANES), F32)] * CONV_K
            for v in range(p):
                dc = jnp.where(v + p * sub < tm, dconv_ext[s, pl.ds(v, 8, stride=p), :], 0.0)
                for k in range(CONV_K):
                    accs[k] = accs[k] + dc * glu_ext[s, pl.ds(v + off + k, 8, stride=p), :]
            for k in range(CONV_K):
                dcw_acc[k, :, _lanes(s)] += accs[k]
        dglu = _tap_sum(dconv_ext, dglu_scr, cw_ref, jnp.zeros((1, WA), F32), tm, lambda k: (CONV_K - 1) - k)

        @pl.when(i == nt - 1)
        def _():
            for k in range(CONV_K):
                dcw_ref[k:k + 1, :] = jnp.sum(dcw_acc[k], axis=0, keepdims=True)
            dcw_ref[CONV_K:32, :] = jnp.zeros((32 - CONV_K, WA), F32)

        da = dglu * sgg
        dgg = dglu * a * (sgg * (1.0 - sgg))
        dproj_ref[:, 0:2 * WA] = dpart_ref[:, 0:2 * WA].astype(BF16)
        dproj_ref[:, 2 * WA:3 * WA] = da.astype(BF16)
        dproj_ref[:, 3 * WA:4 * WA] = dgg.astype(BF16)
        dh = jnp.zeros((tm, D), F32)
        for j in range(NDEV):
            dh = dh + _dot_nt(dproj_ref[:, j * MB:(j + 1) * MB], wmi_ref[j])
        xv = x_ref[...]
        sc, sh = mod_ref[1:2, :], mod_ref[0:1, :]
        gpre = g_ref[0:1, :]
        r = lax.rsqrt(_rowmean(xv * xv) + EPS)
        xh = xv * r
        n = xh * gpre
        hb_ref[...] = (n * (1.0 + sc) + sh).astype(BF16)
        d_sc = _colsum(dh * n)
        d_sh = _colsum(dh)
        dn = dh * (1.0 + sc)
        d_gpre = _colsum(dn * xh)
        dxh = dn * gpre
        dx_ref[...] = dxo_ref[...] + r * (dxh - xh * _rowmean(dxh * xh))

        @pl.when(first)
        def _():
            mg_ref[...] = jnp.zeros((8, D), F32)

        mg_ref[0:1, :] += d_sh
        mg_ref[1:2, :] += d_sc
        vg_ref[0:1, :] += d_gpre

    tile = pl.BlockSpec((tm, D), lambda i: (i, 0))
    return _call(
        core, name=name, grid=(nt,), jobs=jobs,
        in_specs=[tile, tile, pl.BlockSpec((tm, 4 * WA), lambda i: (i, 0)),
                  pl.BlockSpec((HALO, WA), lambda i: (jnp.minimum((i + 1) * hpt, nh - 1), 2)),
                  pl.BlockSpec((tm, 2 * WA), lambda i: (i, 1)),
                  pl.BlockSpec((HALO, 2 * WA), lambda i: (jnp.maximum(i * hpt - 1, 0), 1)),
                  pl.BlockSpec((None, 8, D), lambda i: (i // tps, 0, 0)), _const_spec((8, D)),
                  _const_spec((NDEV, D, MB)), _const_spec((32, WA))],
        out_specs=[tile, pl.BlockSpec((tm, 4 * WA), lambda i: (i, 0)), tile,
                   pl.BlockSpec((None, 8, D), lambda i: (i // tps, 0, 0)), pl.BlockSpec((8, D), lambda i: (0, 0)),
                   pl.BlockSpec((32, WA), lambda i: (0, 0))],
        out_shape=[jax.ShapeDtypeStruct((T, D), F32), jax.ShapeDtypeStruct((T, 4 * WA), BF16),
                   jax.ShapeDtypeStruct((T, D), BF16), jax.ShapeDtypeStruct((nb, 8, D), F32),
                   jax.ShapeDtypeStruct((8, D), F32), jax.ShapeDtypeStruct((32, WA), F32)],
        scratch=[pltpu.VMEM((NSLAB, HALO + ext_rows, LANES), F32), pltpu.VMEM((NSLAB, HALO + ext_rows, LANES), F32),
                 pltpu.VMEM((NSLAB, ext_rows, LANES), F32), pltpu.VMEM((32, 8, WA), F32)],
        args=[dxo, x, dpart, dpart, proj, proj, mod, gvec, w_mi, cw])


def _grad_chip(a, b, a_spec, b_spec, prod_shape, half, name, jobs=(), via_b=False):
    steps = 8 if half is None else 4
    R = prod_shape[0] if half is None else half
    C = prod_shape[1]

    def core(ins, outs, scs):
        a_ref, b_ref = ins
        (o_ref,) = outs
        own, snd, rcv, ssem, rsem, lsem = scs
        s = pl.program_id(0)
        c = lax.axis_index("c")
        me = _me()
        sib = _flip(me, (0, 0, 1))
        if via_b:
            prod = _dot_tn(b_ref[...], a_ref[...]).T.astype(BF16)
        else:
            prod = _dot_tn(a_ref[...], b_ref[...]).astype(BF16)
        if half is None:
            q = s // 2

            @pl.when(s % 2 == c)
            def _():
                own[q] = prod

            @pl.when(s % 2 != c)
            def _():
                snd[q] = prod
                _remote(snd.at[q], rcv.at[q], ssem.at[q], rsem.at[q], sib).start()
        else:
            lo = prod[0:half, :]
            hi = prod[half:2 * half, :]
            own[s] = jnp.where(c == 0, lo, hi)
            snd[s] = jnp.where(c == 0, hi, lo)
            _remote(snd.at[s], rcv.at[s], ssem.at[s], rsem.at[s], sib).start()

        @pl.when(s == steps - 1)
        def _():
            for q4 in range(4):
                cp = _remote(snd.at[q4], rcv.at[q4], ssem.at[q4], rsem.at[q4], sib)
                cp.wait_recv()
                cp.wait_send()
                snd[q4] = (own[q4].astype(F32) + rcv[q4].astype(F32)).astype(BF16)
            out = pltpu.make_async_copy(snd, o_ref, lsem)
            out.start()
            out.wait()

    return _call(
        core, name=name, grid=(steps,), jobs=jobs, in_specs=[a_spec, b_spec], out_specs=[HBM],
        out_shape=[jax.ShapeDtypeStruct((4, R, C), BF16)],
        scratch=[pltpu.VMEM((4, R, C), BF16), pltpu.VMEM((4, R, C), BF16), pltpu.VMEM((4, R, C), BF16),
                 pltpu.SemaphoreType.DMA((4,)), pltpu.SemaphoreType.DMA((4,)), pltpu.SemaphoreType.DMA],
        args=[a, b])


def _grad_w_in(dg, hb, name, jobs=()):
    T = hb.shape[0]
    return _grad_chip(dg, hb, pl.BlockSpec((None, T, FB), lambda s: (s, 0, 0)), _const_spec((T, D)),
                      (FB, D), None, name, jobs)


def _grad_w_out(act, dyb, name, jobs=()):
    T = dyb.shape[0]
    return _grad_chip(act, dyb, pl.BlockSpec((None, T, FB), lambda s: (s, 0, 0)), _const_spec((T, D)),
                      (FB, D), FO, name, jobs)


def _grad_w_mi(hb, dproj, name, jobs=()):
    T = hb.shape[0]
    return _grad_chip(hb, dproj, _const_spec((T, D)), pl.BlockSpec((T, MB), lambda s: (0, s)),
                      (D, MB), None, name, jobs, via_b=True)


def _grad_w_mo(ycat, dym, name, jobs=()):
    T = ycat.shape[0]
    return _grad_chip(ycat, dym, pl.BlockSpec((T, 2 * MO), lambda s: (0, s)), _const_spec((T, D)),
                      (2 * MO, D), MO, name, jobs)


def _adamw_math(w, g, m, v):
    m2 = ADAM_B1 * m + (1.0 - ADAM_B1) * g
    v2 = ADAM_B2 * v + (1.0 - ADAM_B2) * (g * g)
    m_hat = m2 / (1.0 - ADAM_B1 ** ADAM_STEP)
    v_hat = v2 / (1.0 - ADAM_B2 ** ADAM_STEP)
    delta = -ADAM_LR * (m_hat / (jnp.sqrt(v_hat) + ADAM_EPS) + ADAM_WD * w)
    return delta, m2, v2


def _adamw_reduce(parts, w, m, v, tr, name, own=None, after=None):
    R, C = w.shape

    def core(ins, outs, _):
        p_ref, w_ref, m_ref, v_ref = ins[:4]
        g_ref, d_ref, m2_ref, v2_ref = outs
        if own is None:
            terms = [p_ref[s].astype(F32) for s in range(4)]
        else:
            mq = 2 * lax.axis_index("x") + lax.axis_index("y")
            mine = ins[4][...].astype(F32)
            terms = [jnp.where(mq == s, mine, p_ref[s].astype(F32)) for s in range(4)]
        g = terms[0]
        for s in range(1, 4):
            g = g + terms[s]
        g_ref[...] = g
        d_ref[...], m2_ref[...], v2_ref[...] = _adamw_math(w_ref[...], g, m_ref[...], v_ref[...])

    blk = pl.BlockSpec((tr, C), lambda i: (i, 0))
    in_specs = [pl.BlockSpec((4, tr, C), lambda i: (0, i, 0)), blk, blk, blk]
    args = [parts, w, m, v]
    if own is not None:
        mq = 2 * lax.axis_index("x") + lax.axis_index("y")
        in_specs.append(pl.BlockSpec((tr, C), lambda i: (i, 0)))
        args.append(lax.dynamic_index_in_dim(own, mq, 0, keepdims=False))
    if after is not None:
        in_specs.append(HBM)
        args.append(after)
    return _call(
        core, name=name, grid=(R // tr,), in_specs=in_specs,
        out_specs=[blk, blk, blk, blk], out_shape=[jax.ShapeDtypeStruct((R, C), F32)] * 4, args=args)[0]


HBM_ONLY = pl.BlockSpec(memory_space=pltpu.HBM)
SEM = pl.BlockSpec(memory_space=pltpu.SEMAPHORE)
EFFECT = pltpu.SideEffectType.DATAFLOW_SIDE_EFFECTING


def _chip_scatter_start(gs, name):
    n = len(gs)

    def body(*refs):
        g_refs, land_refs = refs[:n], refs[n:2 * n]
        ssem, rsem = refs[2 * n:2 * n + 2]
        token = refs[-1]
        me = _me()
        mq = 2 * me[0] + me[1]
        for k, f in enumerate(CHIP_FLIPS):
            p = _flip(me, f)
            for a in range(n):
                _remote(g_refs[a].at[2 * p[0] + p[1]], land_refs[a].at[mq], ssem.at[3 * a + k], rsem.at[3 * a + k], p).start()
        token[...] = jnp.zeros_like(token)

    gs = [pltpu.with_memory_space_constraint(g, pltpu.HBM) for g in gs]
    lands = [pltpu.with_memory_space_constraint(lax.empty(g.shape, g.dtype), pltpu.HBM) for g in gs]
    res = pl.pallas_call(
        body, name=name,
        out_shape=(pltpu.SemaphoreType.DMA((3 * n,)), pltpu.SemaphoreType.DMA((3 * n,)))
        + tuple(pltpu.HBM(g.shape, g.dtype) for g in gs) * 2 + (jax.ShapeDtypeStruct((SUBL, LANES), F32),),
        in_specs=(HBM_ONLY,) * (2 * n), out_specs=(SEM, SEM) + (HBM_ONLY,) * (2 * n) + (VM,),
        input_output_aliases={a: 2 + a for a in range(2 * n)},
        compiler_params=pltpu.CompilerParams(has_side_effects=EFFECT),
    )(*gs, *lands)
    return res[:-1], res[-1]


def _chip_scatter_wait(handle, after, name):
    ssem, rsem = handle[:2]
    n = (len(handle) - 2) // 2
    thru = handle[2:]

    def body(*refs):
        g_refs, land_refs = refs[:n], refs[n:2 * n]
        ssem, rsem = refs[2 * n:2 * n + 2]
        me = _me()
        mq = 2 * me[0] + me[1]
        for k, f in enumerate(CHIP_FLIPS):
            p = _flip(me, f)
            pq = 2 * p[0] + p[1]
            for a in range(n):
                _remote(g_refs[a].at[pq], land_refs[a].at[mq], ssem.at[3 * a + k], rsem.at[3 * a + k], p).wait_send()
                _remote(g_refs[a].at[mq], land_refs[a].at[pq], ssem.at[3 * a + k], rsem.at[3 * a + k], p).wait_recv()

    res = pl.pallas_call(
        body, name=name,
        out_shape=tuple(pltpu.HBM(t.shape, t.dtype) for t in thru),
        in_specs=(HBM_ONLY,) * (2 * n) + (SEM, SEM, HBM), out_specs=(HBM_ONLY,) * (2 * n),
        input_output_aliases={a: a for a in range(2 * n)},
        compiler_params=pltpu.CompilerParams(has_side_effects=EFFECT),
    )(*thru, ssem, rsem, after)
    return list(res[:n]), list(res[n:])


def _adamw_ada(sc_all, dd, w, m, v, tr, name, after=None):
    R, C = w.shape

    def core(ins, outs, _):
        sc_ref, dd_ref, w_ref, m_ref, v_ref = ins[:5]
        g_ref, d_ref, m2_ref, v2_ref = outs
        g = _dot_tn(sc_ref[...].astype(BF16), dd_ref[...].astype(BF16))
        g_ref[...] = g
        d_ref[...], m2_ref[...], v2_ref[...] = _adamw_math(w_ref[...], g, m_ref[...], v_ref[...])

    blk = pl.BlockSpec((tr, C), lambda i: (i, 0))
    return _call(
        core, name=name, grid=(R // tr,),
        in_specs=[pl.BlockSpec((64, tr), lambda i: (0, i)), pl.BlockSpec((64, C), lambda i: (0, 0)), blk, blk, blk]
        + [HBM] * (after is not None),
        out_specs=[blk, blk, blk, blk], out_shape=[jax.ShapeDtypeStruct((R, C), F32)] * 4,
        args=[sc_all, dd, w, m, v] + [after] * (after is not None))[0]


def _adamw_small(gathered, plain, grads, wmv, emit, name):
    nw = len(grads)
    ng, npl, ne = len(gathered), len(plain), len(emit)

    def core(ins, outs, _):
        srcs = []
        for a in range(ng):
            s = ins[a][0]
            for dev in range(1, NDEV):
                s = s + ins[a][dev]
            srcs.append(s)
        srcs += [ins[ng + a][...] for a in range(npl)]
        w_refs = ins[ng + npl:]
        for e, a in enumerate(emit):
            outs[e][...] = srcs[a]
        for t in range(nw):
            src, row = grads[t]
            g = srcs[src] if row is None else srcs[src][row:row + 1, :]
            w_ref, m_ref, v_ref = w_refs[3 * t:3 * t + 3]
            g_ref, d_ref, m2_ref, v2_ref = outs[ne + 4 * t:ne + 4 * t + 4]
            g_ref[...] = g
            d_ref[...], m2_ref[...], v2_ref[...] = _adamw_math(w_ref[...], g, m_ref[...], v_ref[...])

    out_shape = [jax.ShapeDtypeStruct(gathered[a].shape[1:], F32) for a in emit]
    for t in range(nw):
        out_shape += [jax.ShapeDtypeStruct(wmv[3 * t].shape, F32)] * 4
    return _call(
        core, name=name, grid=(), in_specs=[VM] * (ng + npl + 3 * nw), out_specs=[VM] * (ne + 4 * nw),
        out_shape=out_shape, args=list(gathered) + list(plain) + list(wmv))[0]


def _ada_fwd(c_pad, w_ada, b_cols, cw_pad, jobs=()):
    def core(ins, outs, scs, start_jobs):
        c_ref, w_ref, b_ref, cwp_ref = ins
        ada_ref, sc_ref, cw_ref = outs
        cbuf, send_buf, ssem, rsem = scs
        me = _me()
        mi = _lin(me)
        cbuf[mi] = c_ref[...]
        cw_ref[mi] = cwp_ref[...]
        peers = [_flip(me, f) for f in FLIPS]
        first = []
        for k, p in enumerate(peers):
            first.append(_remote(cbuf.at[mi], cbuf.at[mi], ssem.at[k], rsem.at[k], p))
            first.append(_remote(cw_ref.at[mi], cw_ref.at[mi], ssem.at[7 + k], rsem.at[7 + k], p))
        for cp in first:
            cp.start()
        for k, p in enumerate(peers):
            pi = _lin(p)
            _remote(cbuf.at[pi], cbuf.at[pi], ssem.at[k], rsem.at[k], p).wait_recv()
            _remote(cw_ref.at[pi], cw_ref.at[pi], ssem.at[7 + k], rsem.at[7 + k], p).wait_recv()
        c_all = cbuf[...].reshape(8 * 8, D)
        sc = c_all * _sigmoid(c_all)
        sc_ref[...] = sc
        res = _dot(sc.astype(BF16), w_ref[...].astype(BF16)) + b_ref[...]
        send_buf[...] = res.reshape(8, 8, ADA_B)
        ada_ref[mi] = send_buf[mi]
        second = []
        for k, p in enumerate(peers):
            second.append(_remote(send_buf.at[_lin(p)], ada_ref.at[mi], ssem.at[14 + k], rsem.at[14 + k], p))
        for cp in second:
            cp.start()
        start_jobs()
        for k, p in enumerate(peers):
            _remote(send_buf.at[mi], ada_ref.at[_lin(p)], ssem.at[14 + k], rsem.at[14 + k], p).wait_recv()
        for cp in first + second:
            cp.wait_send()

    return _call(
        core, name="ada_fwd", grid=(), jobs=jobs, core_starts=True, in_specs=[VM, VM, VM, VM], out_specs=[VM, VM, VM],
        out_shape=[jax.ShapeDtypeStruct((8, 8, ADA_B), F32), jax.ShapeDtypeStruct((64, D), F32),
                   jax.ShapeDtypeStruct((8, 32, 64), F32)],
        scratch=[pltpu.VMEM((8, 8, D), F32), pltpu.VMEM((8, 8, ADA_B), F32),
                 pltpu.SemaphoreType.DMA((21,)), pltpu.SemaphoreType.DMA((21,))],
        args=[c_pad, w_ada, b_cols, cw_pad])


def _ada_bwd(dada, jobs=()):
    def core(ins, outs, scs):
        (d_ref,) = ins
        dd_ref, gb_ref = outs
        rbuf, ssem, rsem = scs
        me = _me()
        mi = _lin(me)
        peers = [_flip(me, f) for f in FLIPS]
        rbuf[mi] = d_ref[mi]
        first = []
        for k, p in enumerate(peers):
            first.append(_remote(d_ref.at[_lin(p)], rbuf.at[mi], ssem.at[k], rsem.at[k], p))
        for cp in first:
            cp.start()
        for k, p in enumerate(peers):
            _remote(d_ref.at[mi], rbuf.at[_lin(p)], ssem.at[k], rsem.at[k], p).wait_recv()
        dd = rbuf[...].reshape(64, ADA_B)
        dd_ref[...] = dd
        gb_ref[mi] = jnp.broadcast_to(_colsum(dd), (8, ADA_B))
        second = []
        for k, p in enumerate(peers):
            second.append(_remote(gb_ref.at[mi], gb_ref.at[mi], ssem.at[7 + k], rsem.at[7 + k], p))
        for cp in second:
            cp.start()
        for k, p in enumerate(peers):
            pi = _lin(p)
            _remote(gb_ref.at[pi], gb_ref.at[pi], ssem.at[7 + k], rsem.at[7 + k], p).wait_recv()
        for cp in first + second:
            cp.wait_send()

    return _call(
        core, name="ada_bwd", grid=(), jobs=jobs, in_specs=[VM], out_specs=[VM, VM],
        out_shape=[jax.ShapeDtypeStruct((64, ADA_B), F32), jax.ShapeDtypeStruct((8, 8, ADA_B), F32)],
        scratch=[pltpu.VMEM((8, 8, ADA_B), F32), pltpu.SemaphoreType.DMA((14,)), pltpu.SemaphoreType.DMA((14,))],
        args=[dada])


SMALL_D = ("g_pre_f1", "g_post_f1", "g_pre_m", "g_post_m", "g_pre_f2", "g_post_f2")
SMALL_W = ("gmlp_norm_g", "gmlp_norm_b", "conv_b", "conv_norm_g", "conv_norm_b", "g_out_a", "g_out_b")


def kernel(x, c, w_ada, b_ada, g_pre_f1, g_post_f1, w_f1_in, w_f1_out, g_pre_m, g_post_m, w_mix_in, gmlp_norm_g, gmlp_norm_b, w_spatial, b_spatial, conv_w, conv_b, conv_norm_g, conv_norm_b, g_out_a, g_out_b, w_mix_out, g_pre_f2, g_post_f2, w_f2_in, w_f2_out, loss_target, m_w_ada, m_b_ada, m_g_pre_f1, m_g_post_f1, m_w_f1_in, m_w_f1_out, m_g_pre_m, m_g_post_m, m_w_mix_in, m_gmlp_norm_g, m_gmlp_norm_b, m_w_spatial, m_b_spatial, m_conv_w, m_conv_b, m_conv_norm_g, m_conv_norm_b, m_g_out_a, m_g_out_b, m_w_mix_out, m_g_pre_f2, m_g_post_f2, m_w_f2_in, m_w_f2_out, v_w_ada, v_b_ada, v_g_pre_f1, v_g_post_f1, v_w_f1_in, v_w_f1_out, v_g_pre_m, v_g_post_m, v_w_mix_in, v_gmlp_norm_g, v_gmlp_norm_b, v_w_spatial, v_b_spatial, v_conv_w, v_conv_b, v_conv_norm_g, v_conv_norm_b, v_g_out_a, v_g_out_b, v_w_mix_out, v_g_pre_f2, v_g_post_f2, v_w_f2_in, v_w_f2_out):
    given = dict(locals())
    bl, seq, _ = x.shape
    T = bl * seq
    tm = min(256, seq // 2)
    mi = _lin((lax.axis_index("x"), lax.axis_index("y"), lax.axis_index("c")))

    def shard_in(w):
        return w[0].T.astype(BF16)

    g_f1 = _Gather([shard_in(w_f1_in), w_f1_out[0].astype(BF16)], ("rows", "out"))
    s_f2 = shard_in(w_f2_in)
    g_mx = _Gather([w_mix_in[0].astype(BF16), w_mix_out[0].astype(BF16), w_f2_out[0].astype(BF16), s_f2[:, 0:D // 4]],
                   ("rows", "rows", "out", "rows"), late_mid=True)
    g_f2 = _Gather([s_f2[:, D // 4:D]], ("rows",))

    c_pad = jnp.pad(c, ((0, 8 - bl), (0, 0)))
    b_cols = lax.dynamic_slice(b_ada, (0, mi * ADA_B), (1, ADA_B))
    cw_pad = jnp.pad(conv_w[0], ((0, 1), (0, 0)))
    (ada_blk, sc_all, cw_all), ((wi1, wo1),) = _ada_fwd(c_pad, w_ada[0], b_cols, cw_pad, jobs=[g_f1])
    ada = ada_blk[:, 0:bl, :].transpose(1, 0, 2).reshape(bl, 9, D)
    pad5 = jnp.zeros((bl, 5, D), F32)
    mod1 = jnp.concatenate([ada[:, 0:3], pad5], axis=1)
    mod2 = jnp.concatenate([ada[:, 3:6], pad5], axis=1)
    mod3 = jnp.concatenate([ada[:, 6:9], pad5], axis=1)
    cw_full = cw_all.transpose(1, 0, 2).reshape(32, WA)

    zrow = jnp.zeros((1, D), F32)
    gv1 = jnp.concatenate([g_pre_f1, g_post_f1] + [zrow] * 6, axis=0)
    gvm = jnp.concatenate([g_pre_m, g_post_m] + [zrow] * 6, axis=0)
    gv2 = jnp.concatenate([g_pre_f2, g_post_f2] + [zrow] * 6, axis=0)
    v512 = jnp.concatenate([gmlp_norm_g, gmlp_norm_b, conv_b, conv_norm_g, conv_norm_b, g_out_a, g_out_b,
                            jnp.zeros((1, WA), F32)], axis=0)
    ws = w_spatial[0]
    bias_full = jnp.repeat(b_spatial[0].T, HD, axis=1)
    esel = (lax.broadcasted_iota(jnp.int32, (8, WA), 1) // HD == lax.broadcasted_iota(jnp.int32, (8, WA), 0)).astype(F32)

    x0 = x.reshape(T, D)
    (x1, gu1, y1), ((wmi, wmo, wo2, wi2a),) = _ffn_fwd(x0, mod1, gv1, wi1, wo1, tm, "ffn1_fwd", jobs=[g_mx])
    wmo = wmo.reshape(D, D)
    (x2, proj, ym, conv), ((wi2b,),) = _mixer_fwd(x1, mod2, gvm, wmi, wmo, v512, ws, bias_full, cw_full, tm, "mixer_fwd", jobs=[g_f2])

    (dx2, dg2, act2, hb2, dyb2, mg3, vg3, loss_blk), _ = _ffn_last(
        x2, loss_target.reshape(T, D), mod3, gv2, (wi2a, wi2b), wo2, tm, "ffn2_fwd_bwd")
    (g_wi2,), _ = _grad_w_in(dg2, hb2, "ffn2_gw_in")
    (g_wo2,), _ = _grad_w_out(act2, dyb2, "ffn2_gw_out")
    (dpart, dymb, ycat, mg2a, vgma, v5g, gws, gbs), ((p_wo2,),) = _mixer_bwd_a(
        dx2, ym, proj, conv, mod2, gvm, wmo, v512, ws, bias_full, esel, tm, "mixer_bwd_a",
        jobs=[_ChipScatter([g_wo2])])
    (dx1, dproj, hbm, mg2b, vgmb, dcw), ((p_wi2,),) = _mixer_bwd_b(
        dx2, x1, dpart, proj, mod2, gvm, wmi, cw_full, tm, "mixer_bwd_b", jobs=[_ChipScatter([g_wi2])])
    (g_wmi,), _ = _grad_w_mi(hbm, dproj, "mixer_gw_in")
    (g_wmo,), _ = _grad_w_mo(ycat, dymb, "mixer_gw_out")
    p2 = jnp.concatenate([v5g, dcw], axis=0)
    (dx0, dg1, act1, hb1, dyb1, mg1, vg1), _ = _ffn_bwd(dx1, x0, y1, gu1, mod1, gv1, wi1, wo1, tm, "ffn1_bwd")
    (g_wo1,), ((p_wmi, p_wmo),) = _grad_w_out(act1, dyb1, "ffn1_gw_out", jobs=[_ChipScatter([g_wmi, g_wmo])])

    dada = jnp.concatenate([mg1[:, 0:3], mg2b[:, 0:2], mg2a[:, 2:3], mg3[:, 0:3]], axis=1)
    dada = dada.reshape(bl, NDEV, ADA_B).transpose(1, 0, 2)
    dada = jnp.pad(dada, ((0, 0), (0, 8 - bl), (0, 0)))
    p1 = jnp.concatenate([vg1[0:2], vgmb[0:1], vgma[1:2], vg3[0:2], loss_blk[0:1], zrow], axis=0)
    (dd_all, gb_all), ((a1,),) = _ada_bwd(dada, jobs=[_AllGather([p1])])
    g_bada = gb_all[:, 0, :].reshape(1, 9 * D)

    (g_wi1,), ((a2, a3, a4), (p_wo1,)) = _grad_w_in(
        dg1, hb1, "ffn1_gw_in", jobs=[_Gather([p2, gws, gbs], ("rows",) * 3), _ChipScatter([g_wo1])])

    h_f1, token = _chip_scatter_start([g_wi1], "tail_start")

    res = {}
    quad = _adamw_reduce(p_wi2, w_f2_in[0].T, m_w_f2_in[0].T, v_w_f2_in[0].T, FO, "adamw_w_f2_in", after=token)
    res["w_f2_in"] = tuple(t.T[None] for t in quad)
    for nm, part, tr in (("w_f2_out", p_wo2, FO), ("w_mix_in", p_wmi, 256), ("w_mix_out", p_wmo, MO), ("w_f1_out", p_wo1, FO)):
        quad = _adamw_reduce(part, given[nm][0], given["m_" + nm][0], given["v_" + nm][0], tr, "adamw_" + nm, after=quad[1])
        res[nm] = tuple(t[None] for t in quad)
    quad = _adamw_ada(sc_all, dd_all, w_ada[0], m_w_ada[0], v_w_ada[0], 256, "adamw_w_ada", after=quad[1])
    res["w_ada"] = tuple(t[None] for t in quad)
    (g_wi1,), (p_wi1,) = _chip_scatter_wait(h_f1, quad[1], "tail_wait")
    quad = _adamw_reduce(p_wi1, w_f1_in[0].T, m_w_f1_in[0].T, v_w_f1_in[0].T, FO, "adamw_w_f1_in", own=g_wi1)
    res["w_f1_in"] = tuple(t.T[None] for t in quad)

    small = SMALL_D + SMALL_W + ("w_spatial", "b_spatial", "b_ada")
    grads = [(0, r) for r in range(6)] + [(1, r) for r in range(7)] + [(2, None), (3, None), (4, None)]
    wmv = []
    for nm in small:
        for pre in ("", "m_", "v_"):
            wmv.append(given[pre + nm][0] if nm in ("w_spatial", "b_spatial") else given[pre + nm])
    outs = _adamw_small([a1, a2, a3, a4], [g_bada], grads, wmv, (0, 1), "adamw_small")
    loss = outs[0][6, 0]
    for t, nm in enumerate(small):
        quad = outs[2 + 4 * t:6 + 4 * t]
        res[nm] = tuple(q[None] for q in quad) if nm in ("w_spatial", "b_spatial") else tuple(quad)
    g_cw = lax.dynamic_slice(outs[1], (8, mi * 64), (32, 64))
    wmv = [jnp.pad(given[pre + "conv_w"][0], ((0, 1), (0, 0)), constant_values=1.0 if pre == "v_" else 0.0)
           for pre in ("", "m_", "v_")]
    quad = _adamw_small([], [g_cw], [(0, None)], wmv, (), "adamw_conv_w")
    res["conv_w"] = tuple(q[0:CONV_K][None] for q in quad)

    order = ["w_ada", "b_ada", "g_pre_f1", "g_post_f1", "w_f1_in", "w_f1_out", "g_pre_m", "g_post_m", "w_mix_in",
             "gmlp_norm_g", "gmlp_norm_b", "w_spatial", "b_spatial", "conv_w", "conv_b", "conv_norm_g", "conv_norm_b",
             "g_out_a", "g_out_b", "w_mix_out", "g_pre_f2", "g_post_f2", "w_f2_in", "w_f2_out"]
    out = [loss, dx0.reshape(bl, seq, D)]
    for k in range(4):
        out += [res[nm][k] for nm in order]
    return tuple(out)
```

```python
import jax
import jax.numpy as jnp
from jax import lax
from jax.experimental import pallas as pl
from jax.experimental.pallas import tpu as pltpu

F32 = jnp.float32
BF16 = jnp.bfloat16

D = 1024
DFF = 2816
NDEV = 8
FB = 2 * DFF // NDEV
NCH = DFF // FB
LANES = 128
SUBL = 8
FO = DFF // NDEV
WA = 512
NSLAB = WA // LANES
NHEAD = 8
HD = 64
CHUNK = 128
CONV_K = 31
HALO = 32
MB = 2 * (WA + WA) // NDEV
MO = D // NDEV
ADA_B = 9 * D // NDEV
EPS = 1e-6
HALF = 0.5

ADAM_LR = 0.001
ADAM_B1 = 0.9
ADAM_B2 = 0.999
ADAM_EPS = 1e-08
ADAM_WD = 0.01
ADAM_STEP = 10

VMEM_LIMIT = 56 * 1024 * 1024
MESH = pl.DeviceIdType.MESH
FLIPS = ((0, 0, 1), (1, 0, 0), (0, 1, 0), (1, 1, 0), (1, 0, 1), (0, 1, 1), (1, 1, 1))
CHIP_FLIPS = ((1, 0, 0), (0, 1, 0), (1, 1, 0))
HBM = pl.BlockSpec(memory_space=pl.ANY)
VM = pl.BlockSpec(memory_space=pltpu.VMEM)


def _dot(a, b):
    return lax.dot_general(a, b, (((1,), (0,)), ((), ())), preferred_element_type=F32)


def _dot_nt(a, b):
    return lax.dot_general(a, b, (((1,), (1,)), ((), ())), preferred_element_type=F32)


def _dot_tn(a, b):
    return lax.dot_general(a, b, (((0,), (0,)), ((), ())), preferred_element_type=F32)


def _rowmean(v):
    return jnp.mean(v, axis=-1, keepdims=True)


def _colsum(v):
    return jnp.sum(v, axis=0, keepdims=True)


def _sigmoid(v):
    return 0.5 * jnp.tanh(0.5 * v) + 0.5


def _const_spec(shape):
    nd = len(shape)
    return pl.BlockSpec(shape, lambda *_: (0,) * nd, pipeline_mode=pl.Buffered(1))


def _me():
    return lax.axis_index("x"), lax.axis_index("y"), lax.axis_index("c")


def _flip(me, f):
    return tuple(1 - v if b else v for v, b in zip(me, f))


def _lin(p):
    return 4 * p[0] + 2 * p[1] + p[2]


def _remote(src, dst, send_sem, recv_sem, dev):
    return pltpu.make_async_remote_copy(src_ref=src, dst_ref=dst, send_sem=send_sem, recv_sem=recv_sem,
                                        device_id=dev, device_id_type=MESH)


def _blk(kind, ref, p):
    if kind == "out":
        return ref.at[2 * p[0] + p[1], pl.ds(p[2] * FO, FO), :]
    return ref.at[_lin(p)]


class _Gather:
    def __init__(self, shards, kinds, late_mid=False):
        self.late_mid = late_mid
        self.kinds = kinds
        self.n = len(shards)
        self.ins = list(shards)
        self.out_shape = [jax.ShapeDtypeStruct((4, FB, D) if k == "out" else (NDEV,) + s.shape, s.dtype)
                          for s, k in zip(shards, kinds)]
        self.sems = [pltpu.SemaphoreType.DMA((7 * self.n,)), pltpu.SemaphoreType.DMA((7 * self.n,)),
                     pltpu.SemaphoreType.DMA((self.n,))]

    def _first(self, ins, outs, sems):
        ssem, rsem, lsem = sems
        me = _me()
        sib = _flip(me, (0, 0, 1))
        cps, loc = [], []
        for a in range(self.n):
            mine = _blk(self.kinds[a], outs[a], me)
            loc.append(pltpu.make_async_copy(ins[a], mine, lsem.at[a]))
            cps.append(_remote(ins[a], mine, ssem.at[7 * a], rsem.at[7 * a], sib))
            for j, f in enumerate(CHIP_FLIPS):
                cps.append(_remote(ins[a], mine, ssem.at[7 * a + 1 + j], rsem.at[7 * a + 1 + j], _flip(me, f)))
        return cps, loc

    def _passed(self, outs, sems):
        ssem, rsem, _ = sems
        me = _me()
        sib = _flip(me, (0, 0, 1))
        cps = []
        for j, f in enumerate(CHIP_FLIPS):
            for a in range(self.n):
                blk = _blk(self.kinds[a], outs[a], _flip(me, f))
                cps.append(_remote(blk, blk, ssem.at[7 * a + 4 + j], rsem.at[7 * a + 4 + j], sib))
        return cps

    def start(self, ins, outs, sems):
        cps, loc = self._first(ins, outs, sems)
        for cp in loc + cps:
            cp.start()

    def mid(self, ins, outs, sems):
        ssem, rsem, _ = sems
        me = _me()
        passed = self._passed(outs, sems)
        t = 0
        for j, f in enumerate(CHIP_FLIPS):
            for a in range(self.n):
                blk = _blk(self.kinds[a], outs[a], _flip(me, f))
                _remote(blk, blk, ssem.at[7 * a + 1 + j], rsem.at[7 * a + 1 + j], _flip(me, f)).wait_recv()
                passed[t].start()
                t += 1

    def end(self, ins, outs, sems):
        ssem, rsem, _ = sems
        me = _me()
        sib = _flip(me, (0, 0, 1))
        for a in range(self.n):
            blk = _blk(self.kinds[a], outs[a], sib)
            _remote(blk, blk, ssem.at[7 * a], rsem.at[7 * a], sib).wait_recv()
            for j, f in enumerate(CHIP_FLIPS):
                blk = _blk(self.kinds[a], outs[a], _flip(_flip(me, f), (0, 0, 1)))
                _remote(blk, blk, ssem.at[7 * a + 4 + j], rsem.at[7 * a + 4 + j], sib).wait_recv()
        cps, loc = self._first(ins, outs, sems)
        for cp in cps + self._passed(outs, sems):
            cp.wait_send()
        for cp in loc:
            cp.wait()


class _ChipScatter:
    def __init__(self, grads):
        self.n = len(grads)
        self.ins = list(grads)
        self.out_shape = [jax.ShapeDtypeStruct(g.shape, BF16) for g in grads]
        self.sems = [pltpu.SemaphoreType.DMA((3 * self.n,)), pltpu.SemaphoreType.DMA((3 * self.n,)),
                     pltpu.SemaphoreType.DMA((self.n,))]

    def _copies(self, ins, outs, sems):
        ssem, rsem, lsem = sems
        me = _me()
        mq = 2 * me[0] + me[1]
        loc = [pltpu.make_async_copy(ins[a].at[mq], outs[a].at[mq], lsem.at[a]) for a in range(self.n)]
        cps = []
        for k, f in enumerate(CHIP_FLIPS):
            p = _flip(me, f)
            for a in range(self.n):
                cps.append(_remote(ins[a].at[2 * p[0] + p[1]], outs[a].at[mq], ssem.at[3 * a + k], rsem.at[3 * a + k], p))
        return cps, loc

    def start(self, ins, outs, sems):
        cps, loc = self._copies(ins, outs, sems)
        for cp in loc + cps:
            cp.start()

    mid = None

    def end(self, ins, outs, sems):
        ssem, rsem, _ = sems
        me = _me()
        mq = 2 * me[0] + me[1]
        for k, f in enumerate(CHIP_FLIPS):
            p = _flip(me, f)
            for a in range(self.n):
                _remote(ins[a].at[mq], outs[a].at[2 * p[0] + p[1]], ssem.at[3 * a + k], rsem.at[3 * a + k], p).wait_recv()
        cps, loc = self._copies(ins, outs, sems)
        for cp in cps:
            cp.wait_send()
        for cp in loc:
            cp.wait()


class _AllGather:
    def __init__(self, parts):
        self.n = len(parts)
        self.ins = list(parts)
        self.out_shape = [jax.ShapeDtypeStruct((NDEV,) + p.shape, p.dtype) for p in parts]
        self.sems = [pltpu.SemaphoreType.DMA((7 * self.n,)), pltpu.SemaphoreType.DMA((7 * self.n,)),
                     pltpu.SemaphoreType.DMA((self.n,))]

    def _copies(self, ins, outs, sems):
        ssem, rsem, lsem = sems
        me = _me()
        mi = _lin(me)
        loc = [pltpu.make_async_copy(ins[a], outs[a].at[mi], lsem.at[a]) for a in range(self.n)]
        cps = []
        for k, f in enumerate(FLIPS):
            for a in range(self.n):
                cps.append(_remote(ins[a], outs[a].at[mi], ssem.at[7 * a + k], rsem.at[7 * a + k], _flip(me, f)))
        return cps, loc

    def start(self, ins, outs, sems):
        cps, loc = self._copies(ins, outs, sems)
        for cp in loc + cps:
            cp.start()

    mid = None

    def end(self, ins, outs, sems):
        ssem, rsem, _ = sems
        me = _me()
        for k, f in enumerate(FLIPS):
            p = _flip(me, f)
            for a in range(self.n):
                _remote(ins[a], outs[a].at[_lin(p)], ssem.at[7 * a + k], rsem.at[7 * a + k], p).wait_recv()
        cps, loc = self._copies(ins, outs, sems)
        for cp in cps:
            cp.wait_send()
        for cp in loc:
            cp.wait()


def _call(core, *, name, grid, in_specs, out_specs, out_shape, args, scratch=(), jobs=(), core_starts=False):
    n_in, n_out, n_sc = len(in_specs), len(out_specs), len(scratch)
    steps = 1
    for g in grid:
        steps *= g

    def body(*refs):
        pos = [0]

        def take(k):
            r = refs[pos[0]:pos[0] + k]
            pos[0] += k
            return r

        ins = take(n_in)
        j_ins = [take(len(j.ins)) for j in jobs]
        outs = take(n_out)
        j_outs = [take(len(j.out_shape)) for j in jobs]
        scs = take(n_sc)
        j_sems = [take(len(j.sems)) for j in jobs]
        if len(grid) == 2:
            step = pl.program_id(0) * grid[1] + pl.program_id(1)
        elif len(grid) == 1:
            step = pl.program_id(0)
        else:
            step = 0
        def start_jobs():
            for j, ji, jo, js in zip(jobs, j_ins, j_outs, j_sems):
                j.start(ji, jo, js)

        if grid:
            pl.when(step == 0)(start_jobs)
        elif not core_starts:
            start_jobs()
        for j, ji, jo, js in zip(jobs, j_ins, j_outs, j_sems):
            if j.mid is not None and grid:
                at = max(steps - 2, 0) if j.late_mid else (3 * steps) // 4
                pl.when(step == at)(lambda j=j, ji=ji, jo=jo, js=js: j.mid(ji, jo, js))
        if core_starts:
            core(ins, outs, scs, start_jobs)
        elif core is not None:
            core(ins, outs, scs)
        for j, ji, jo, js in zip(jobs, j_ins, j_outs, j_sems):
            if grid:
                pl.when(step == steps - 1)(lambda j=j, ji=ji, jo=jo, js=js: j.end(ji, jo, js))
            else:
                if j.mid is not None:
                    j.mid(ji, jo, js)
                j.end(ji, jo, js)

    all_in = list(in_specs)
    all_args = list(args)
    all_out = list(out_specs)
    all_shape = list(out_shape)
    all_sc = list(scratch)
    for j in jobs:
        all_in += [HBM] * len(j.ins)
        all_args += j.ins
    for j in jobs:
        all_out += [HBM] * len(j.out_shape)
        all_shape += j.out_shape
        all_sc += j.sems
    params = dict(vmem_limit_bytes=VMEM_LIMIT)
    if grid:
        params["dimension_semantics"] = ("arbitrary",) * len(grid)
    res = pl.pallas_call(
        body, name=name, grid=grid, in_specs=all_in, out_specs=all_out, out_shape=all_shape,
        scratch_shapes=all_sc, compiler_params=pltpu.CompilerParams(**params),
    )(*all_args)
    core_res = list(res[:n_out])
    job_res = []
    pos = n_out
    for j in jobs:
        job_res.append(list(res[pos:pos + len(j.out_shape)]))
        pos += len(j.out_shape)
    return core_res, job_res


def _ffn_fwd(x, mod, gvec, w_in, w_out, tm, name, jobs=()):
    T = x.shape[0]
    nt = T // tm
    tps = nt // mod.shape[0]

    def core(ins, outs, _):
        x_ref, mod_ref, g_ref, win_ref, wout_ref = ins
        xo_ref, gu_ref, y_ref = outs
        xv = x_ref[...]
        sh, sc, gt = mod_ref[0:1, :], mod_ref[1:2, :], mod_ref[2:3, :]
        r = lax.rsqrt(_rowmean(xv * xv) + EPS)
        h = (xv * r * g_ref[0:1, :]) * (1.0 + sc) + sh
        hb = h.astype(BF16)
        y = jnp.zeros((tm, D), F32)
        for cidx in range(NCH):
            gate = _dot_nt(hb, win_ref[cidx])
            up = _dot_nt(hb, win_ref[NCH + cidx])
            gu_ref[cidx] = gate.astype(BF16)
            gu_ref[NCH + cidx] = up.astype(BF16)
            act = gate * _sigmoid(gate) * up
            y = y + _dot(act.astype(BF16), wout_ref[cidx])
        y_ref[...] = y
        ry = lax.rsqrt(_rowmean(y * y) + EPS)
        xo_ref[...] = xv + (HALF * gt) * (y * ry * g_ref[1:2, :])

    tile = pl.BlockSpec((tm, D), lambda i: (i, 0))
    return _call(
        core, name=name, grid=(nt,), jobs=jobs,
        in_specs=[tile, pl.BlockSpec((None, 8, D), lambda i: (i // tps, 0, 0)), _const_spec((8, D)),
                  _const_spec((8, FB, D)), _const_spec((4, FB, D))],
        out_specs=[tile, pl.BlockSpec((8, tm, FB), lambda i: (0, i, 0)), tile],
        out_shape=[jax.ShapeDtypeStruct((T, D), F32), jax.ShapeDtypeStruct((8, T, FB), BF16),
                   jax.ShapeDtypeStruct((T, D), F32)],
        args=[x, mod, gvec, w_in, w_out])


def _ffn_bwd(dxo, x, y, gu, mod, gvec, w_in, w_out, tm, name, jobs=()):
    T = x.shape[0]
    nt = T // tm
    nb = mod.shape[0]
    tps = nt // nb

    def core(ins, outs, _):
        dxo_ref, x_ref, y_ref, gu_ref, mod_ref, g_ref, win_ref, wout_ref = ins
        dx_ref, dg_ref, act_ref, hb_ref, dyb_ref, mg_ref, vg_ref = outs
        i = pl.program_id(0)
        xv = x_ref[...]
        dxo_v = dxo_ref[...]
        yv = y_ref[...]
        sh, sc, gt = mod_ref[0:1, :], mod_ref[1:2, :], mod_ref[2:3, :]
        gpre, gpost = g_ref[0:1, :], g_ref[1:2, :]
        r = lax.rsqrt(_rowmean(xv * xv) + EPS)
        xh = xv * r
        n = xh * gpre
        hb = (n * (1.0 + sc) + sh).astype(BF16)
        hb_ref[...] = hb
        ry = lax.rsqrt(_rowmean(yv * yv) + EPS)
        yh = yv * ry
        d_gt = _colsum(HALF * dxo_v * (yh * gpost))
        dp = (HALF * gt) * dxo_v
        d_gpost = _colsum(dp * yh)
        dyh = dp * gpost
        dy = ry * (dyh - yh * _rowmean(dyh * yh))
        dyb = dy.astype(BF16)
        dyb_ref[...] = dyb
        dh = jnp.zeros((tm, D), F32)
        for cidx in range(NCH):
            gate = gu_ref[cidx].astype(F32)
            up = gu_ref[NCH + cidx].astype(F32)
            sig = _sigmoid(gate)
            s = gate * sig
            act_ref[cidx] = (s * up).astype(BF16)
            d_act = _dot_nt(dyb, wout_ref[cidx])
            d_up = (d_act * s).astype(BF16)
            d_gate = (d_act * up * (sig * (1.0 + gate * (1.0 - sig)))).astype(BF16)
            dg_ref[cidx] = d_gate
            dg_ref[NCH + cidx] = d_up
            dh = dh + _dot(d_gate, win_ref[cidx]) + _dot(d_up, win_ref[NCH + cidx])
        d_sc = _colsum(dh * n)
        d_sh = _colsum(dh)
        dn = dh * (1.0 + sc)
        d_gpre = _colsum(dn * xh)
        dxh = dn * gpre
        dx_ref[...] = dxo_v + r * (dxh - xh * _rowmean(dxh * xh))

        @pl.when(i % tps == 0)
        def _():
            mg_ref[...] = jnp.zeros((8, D), F32)

        @pl.when(i == 0)
        def _():
            vg_ref[...] = jnp.zeros((8, D), F32)

        mg_ref[0:1, :] += d_sh
        mg_ref[1:2, :] += d_sc
        mg_ref[2:3, :] += d_gt
        vg_ref[0:1, :] += d_gpre
        vg_ref[1:2, :] += d_gpost

    tile = pl.BlockSpec((tm, D), lambda i: (i, 0))
    return _call(
        core, name=name, grid=(nt,), jobs=jobs,
        in_specs=[tile, tile, tile, pl.BlockSpec((8, tm, FB), lambda i: (0, i, 0)),
                  pl.BlockSpec((None, 8, D), lambda i: (i // tps, 0, 0)), _const_spec((8, D)),
                  _const_spec((8, FB, D)), _const_spec((4, FB, D))],
        out_specs=[tile, pl.BlockSpec((8, tm, FB), lambda i: (0, i, 0)),
                   pl.BlockSpec((4, tm, FB), lambda i: (0, i, 0)), tile, tile,
                   pl.BlockSpec((None, 8, D), lambda i: (i // tps, 0, 0)), pl.BlockSpec((8, D), lambda i: (0, 0))],
        out_shape=[jax.ShapeDtypeStruct((T, D), F32), jax.ShapeDtypeStruct((8, T, FB), BF16),
                   jax.ShapeDtypeStruct((4, T, FB), BF16), jax.ShapeDtypeStruct((T, D), BF16),
                   jax.ShapeDtypeStruct((T, D), BF16), jax.ShapeDtypeStruct((nb, 8, D), F32),
                   jax.ShapeDtypeStruct((8, D), F32)],
        args=[dxo, x, y, gu, mod, gvec, w_in, w_out])


def _ffn_last(x, target, mod, gvec, w_in, w_out, tm, name, jobs=()):
    T = x.shape[0]
    nt = T // tm
    nb = mod.shape[0]
    tps = nt // nb

    def core(ins, outs, scs):
        x_ref, t_ref, mod_ref, g_ref, wina_ref, winb_ref, wout_ref = ins
        dx_ref, dg_ref, act_ref, hb_ref, dyb_ref, mg_ref, vg_ref, loss_ref = outs
        hd2 = w_in[0].shape[2]
        (gu_s,) = scs
        i = pl.program_id(0)
        xv = x_ref[...]
        sh, sc, gt = mod_ref[0:1, :], mod_ref[1:2, :], mod_ref[2:3, :]
        gpre, gpost = g_ref[0:1, :], g_ref[1:2, :]
        r = lax.rsqrt(_rowmean(xv * xv) + EPS)
        xh = xv * r
        n = xh * gpre
        hb = (n * (1.0 + sc) + sh).astype(BF16)
        hb_ref[...] = hb
        hba, hbb = hb[:, 0:hd2], hb[:, hd2:D]
        yv = jnp.zeros((tm, D), F32)
        for cidx in range(NCH):
            gate = _dot_nt(hba, wina_ref[cidx]) + _dot_nt(hbb, winb_ref[cidx])
            up = _dot_nt(hba, wina_ref[NCH + cidx]) + _dot_nt(hbb, winb_ref[NCH + cidx])
            gu_s[cidx] = gate.astype(BF16)
            gu_s[NCH + cidx] = up.astype(BF16)
            act = gate * _sigmoid(gate) * up
            act_ref[cidx] = act.astype(BF16)
            yv = yv + _dot(act_ref[cidx], wout_ref[cidx])
        ry = lax.rsqrt(_rowmean(yv * yv) + EPS)
        yh = yv * ry
        pn = yh * gpost
        err = xv + (HALF * gt) * pn - t_ref[...]
        dxo_v = err * (1.0 / D)
        d_gt = _colsum(HALF * dxo_v * pn)
        dp = (HALF * gt) * dxo_v
        d_gpost = _colsum(dp * yh)
        dyh = dp * gpost
        dyb = (ry * (dyh - yh * _rowmean(dyh * yh))).astype(BF16)
        dyb_ref[...] = dyb
        dha = jnp.zeros((tm, hd2), F32)
        dhb = jnp.zeros((tm, D - hd2), F32)
        for cidx in range(NCH):
            gate = gu_s[cidx].astype(F32)
            up = gu_s[NCH + cidx].astype(F32)
            sig = _sigmoid(gate)
            s = gate * sig
            d_act = _dot_nt(dyb, wout_ref[cidx])
            d_up = (d_act * s).astype(BF16)
            d_gate = (d_act * up * (sig * (1.0 + gate * (1.0 - sig)))).astype(BF16)
            dg_ref[cidx] = d_gate
            dg_ref[NCH + cidx] = d_up
            dha = dha + _dot(d_gate, wina_ref[cidx]) + _dot(d_up, wina_ref[NCH + cidx])
            dhb = dhb + _dot(d_gate, winb_ref[cidx]) + _dot(d_up, winb_ref[NCH + cidx])
        dh = jnp.concatenate([dha, dhb], axis=1)
        d_sc = _colsum(dh * n)
        d_sh = _colsum(dh)
        dn = dh * (1.0 + sc)
        d_gpre = _colsum(dn * xh)
        dxh = dn * gpre
        dx_ref[...] = dxo_v + r * (dxh - xh * _rowmean(dxh * xh))

        @pl.when(i % tps == 0)
        def _():
            mg_ref[...] = jnp.zeros((8, D), F32)

        @pl.when(i == 0)
        def _():
            vg_ref[...] = jnp.zeros((8, D), F32)
            loss_ref[...] = jnp.zeros((8, D), F32)

        mg_ref[0:1, :] += d_sh
        mg_ref[1:2, :] += d_sc
        mg_ref[2:3, :] += d_gt
        vg_ref[0:1, :] += d_gpre
        vg_ref[1:2, :] += d_gpost
        loss_ref[...] += HALF * jnp.sum(_rowmean(err * err), axis=0, keepdims=True)

    tile = pl.BlockSpec((tm, D), lambda i: (i, 0))
    return _call(
        core, name=name, grid=(nt,), jobs=jobs,
        in_specs=[tile, tile, pl.BlockSpec((None, 8, D), lambda i: (i // tps, 0, 0)), _const_spec((8, D)),
                  _const_spec(w_in[0].shape), _const_spec(w_in[1].shape), _const_spec((4, FB, D))],
        out_specs=[tile, pl.BlockSpec((8, tm, FB), lambda i: (0, i, 0)),
                   pl.BlockSpec((4, tm, FB), lambda i: (0, i, 0)), tile, tile,
                   pl.BlockSpec((None, 8, D), lambda i: (i // tps, 0, 0)), pl.BlockSpec((8, D), lambda i: (0, 0)),
                   pl.BlockSpec((8, D), lambda i: (0, 0))],
        out_shape=[jax.ShapeDtypeStruct((T, D), F32), jax.ShapeDtypeStruct((8, T, FB), BF16),
                   jax.ShapeDtypeStruct((4, T, FB), BF16), jax.ShapeDtypeStruct((T, D), BF16),
                   jax.ShapeDtypeStruct((T, D), BF16), jax.ShapeDtypeStruct((nb, 8, D), F32),
                   jax.ShapeDtypeStruct((8, D), F32), jax.ShapeDtypeStruct((8, D), F32)],
        scratch=[pltpu.VMEM((8, tm, FB), BF16)],
        args=[x, target, mod, gvec, w_in[0], w_in[1], w_out])


def _masked_spatial(ws_ref):
    row = lax.broadcasted_iota(jnp.int32, (CHUNK, CHUNK), 0)
    col = lax.broadcasted_iota(jnp.int32, (CHUNK, CHUNK), 1)
    keep = col <= row
    return [jnp.where(keep, ws_ref[hd], 0.0).astype(BF16) for hd in range(NHEAD)]


def _head_pairs(mats, right, transpose=False):
    first = lax.broadcasted_iota(jnp.int32, (CHUNK, LANES), 1) < HD
    op = _dot_tn if transpose else _dot
    out = []
    for p in range(NHEAD // 2):
        slab = right[:, _lanes(p)]
        out.append(jnp.where(first, op(mats[2 * p], slab), op(mats[2 * p + 1], slab)))
    return jnp.concatenate(out, axis=1)


def _spatial_gate(wm, vb_chunk):
    return _head_pairs(wm, vb_chunk)


def _layer_norm_stats(v):
    mu = _rowmean(v)
    vc = v - mu
    rstd = lax.rsqrt(_rowmean(vc * vc) + EPS)
    return vc * rstd, rstd


def _pitch(tm):
    p = tm // 8
    while p % 8 != 4:
        p += 1
    return p


def _lanes(s):
    return slice(s * LANES, (s + 1) * LANES)


def _to_slabs(ref, row0, val):
    for s in range(NSLAB):
        ref[s, row0:row0 + val.shape[0], :] = val[:, _lanes(s)]


def _tap_sum(src, out, cw_ref, bias, tm, start):
    p = _pitch(tm)
    for s in range(NSLAB):
        accs = [jnp.broadcast_to(bias[:, _lanes(s)], (SUBL, LANES))] * p
        for k in range(CONV_K):
            w = jnp.broadcast_to(cw_ref[k:k + 1, _lanes(s)], (SUBL, LANES))
            for v in range(p):
                accs[v] = accs[v] + w * src[s, pl.ds(v + start(k), 8, stride=p), :]
        for v in range(p):
            out[s, pl.ds(v, 8, stride=p), :] = accs[v]
    return jnp.concatenate([out[s, 0:tm, :] for s in range(NSLAB)], axis=1)


def _mixer_fwd(x, mod, gvec, w_mi, w_mo, v512, ws, bias_full, cw, tm, name, jobs=()):
    T = x.shape[0]
    nt = T // tm
    tps = nt // mod.shape[0]
    ext_rows = 8 * _pitch(tm)

    def core(ins, outs, scs):
        x_ref, mod_ref, g_ref, wmi_ref, wmo_ref, v_ref, ws_ref, bias_ref, cw_ref = ins
        xo_ref, proj_ref, ym_ref, conv_ref = outs
        glu_ext, conv_scr = scs
        i = pl.program_id(0)
        xv = x_ref[...]
        sh, sc, gt = mod_ref[0:1, :], mod_ref[1:2, :], mod_ref[2:3, :]
        r = lax.rsqrt(_rowmean(xv * xv) + EPS)
        hb = ((xv * r * g_ref[0:1, :]) * (1.0 + sc) + sh).astype(BF16)
        for j in range(NDEV):
            proj_ref[:, j * MB:(j + 1) * MB] = _dot(hb, wmi_ref[j])
        u = proj_ref[:, 0:WA]
        v0 = proj_ref[:, WA:2 * WA]
        a = proj_ref[:, 2 * WA:3 * WA]
        g = proj_ref[:, 3 * WA:4 * WA]
        vh, _ = _layer_norm_stats(v0)
        vb = (vh * v_ref[0:1, :] + v_ref[1:2, :]).astype(BF16)
        wm = _masked_spatial(ws_ref)
        ya = []
        for q in range(tm // CHUNK):
            z = _spatial_gate(wm, vb[q * CHUNK:(q + 1) * CHUNK, :]) + bias_ref[...]
            ya.append(u[q * CHUNK:(q + 1) * CHUNK, :] * z)
        ya = jnp.concatenate(ya, axis=0)
        glu = a * _sigmoid(g)

        @pl.when(i == 0)
        def _():
            glu_ext[:, HALO + tm:HALO + ext_rows, :] = jnp.zeros((NSLAB, ext_rows - tm, LANES), F32)

        @pl.when(i % tps == 0)
        def _():
            glu_ext[:, 0:HALO, :] = jnp.zeros((NSLAB, HALO, LANES), F32)

        _to_slabs(glu_ext, HALO, glu)
        conv = _tap_sum(glu_ext, conv_scr, cw_ref, v_ref[2:3, :], tm, lambda k: HALO - (CONV_K - 1) + k)
        conv_ref[...] = conv
        glu_ext[:, 0:HALO, :] = glu_ext[:, tm:tm + HALO, :]
        ch, _ = _layer_norm_stats(conv)
        cn = ch * v_ref[3:4, :] + v_ref[4:5, :]
        yb = cn * _sigmoid(cn)
        pa = ya * lax.rsqrt(_rowmean(ya * ya) + EPS) * v_ref[5:6, :]
        pb = yb * lax.rsqrt(_rowmean(yb * yb) + EPS) * v_ref[6:7, :]
        ycat = jnp.concatenate([pa, pb], axis=1).astype(BF16)
        ym = _dot(ycat, wmo_ref[...])
        ym_ref[...] = ym
        rm = lax.rsqrt(_rowmean(ym * ym) + EPS)
        xo_ref[...] = xv + gt * (ym * rm * g_ref[1:2, :])

    tile = pl.BlockSpec((tm, D), lambda i: (i, 0))
    return _call(
        core, name=name, grid=(nt,), jobs=jobs,
        in_specs=[tile, pl.BlockSpec((None, 8, D), lambda i: (i // tps, 0, 0)), _const_spec((8, D)),
                  _const_spec((NDEV, D, MB)), _const_spec((D, D)), _const_spec((8, WA)),
                  _const_spec((NHEAD, CHUNK, CHUNK)), _const_spec((CHUNK, WA)), _const_spec((32, WA))],
        out_specs=[tile, pl.BlockSpec((tm, 4 * WA), lambda i: (i, 0)), tile, pl.BlockSpec((tm, WA), lambda i: (i, 0))],
        out_shape=[jax.ShapeDtypeStruct((T, D), F32), jax.ShapeDtypeStruct((T, 4 * WA), F32),
                   jax.ShapeDtypeStruct((T, D), F32), jax.ShapeDtypeStruct((T, WA), F32)],
        scratch=[pltpu.VMEM((NSLAB, HALO + ext_rows, LANES), F32), pltpu.VMEM((NSLAB, ext_rows, LANES), F32)],
        args=[x, mod, gvec, w_mi, w_mo, v512, ws, bias_full, cw])


def _mixer_bwd_a(dxo, ym, proj, conv, mod, gvec, w_mo, v512, ws, bias_full, esel, tm, name, jobs=()):
    T = dxo.shape[0]
    nt = T // tm
    nb = mod.shape[0]
    tps = nt // nb

    def core(ins, outs, scs):
        dxo_ref, ym_ref, proj_ref, conv_ref, mod_ref, g_ref, wmo_ref, v_ref, ws_ref, bias_ref, e_ref = ins
        dpart_ref, dymb_ref, ycat_ref, mg_ref, vg_ref, v5g_ref, gws_ref, gbs_ref = outs
        (dbs_acc,) = scs
        i = pl.program_id(0)
        dxo_v = dxo_ref[...]
        ymv = ym_ref[...]
        gt = mod_ref[2:3, :]
        gpost = g_ref[1:2, :]
        rm = lax.rsqrt(_rowmean(ymv * ymv) + EPS)
        ymh = ymv * rm
        d_gt = _colsum(dxo_v * (ymh * gpost))
        dpm = gt * dxo_v
        d_gpost = _colsum(dpm * ymh)
        dymh = dpm * gpost
        dym = (rm * (dymh - ymh * _rowmean(dymh * ymh))).astype(BF16)
        dymb_ref[...] = dym
        dycat = _dot_nt(dym, wmo_ref[...])
        u = proj_ref[:, 0:WA]
        v0 = proj_ref[:, WA:2 * WA]
        vh, rv = _layer_norm_stats(v0)
        vb = (vh * v_ref[0:1, :] + v_ref[1:2, :]).astype(BF16)
        wm = _masked_spatial(ws_ref)
        zs = []
        for q in range(tm // CHUNK):
            zs.append(_spatial_gate(wm, vb[q * CHUNK:(q + 1) * CHUNK, :]) + bias_ref[...])
        z = jnp.concatenate(zs, axis=0)
        ya = u * z
        ra = lax.rsqrt(_rowmean(ya * ya) + EPS)
        yah = ya * ra
        ch, rc = _layer_norm_stats(conv_ref[...])
        cn = ch * v_ref[3:4, :] + v_ref[4:5, :]
        sg = _sigmoid(cn)
        yb = cn * sg
        rb = lax.rsqrt(_rowmean(yb * yb) + EPS)
        ybh = yb * rb
        ycat_ref[...] = jnp.concatenate([yah * v_ref[5:6, :], ybh * v_ref[6:7, :]], axis=1).astype(BF16)
        dpa = dycat[:, 0:WA]
        dpb = dycat[:, WA:2 * WA]
        d_goa = _colsum(dpa * yah)
        d_gob = _colsum(dpb * ybh)
        dyah = dpa * v_ref[5:6, :]
        dybh = dpb * v_ref[6:7, :]
        dya = ra * (dyah - yah * _rowmean(dyah * yah))
        dyb = rb * (dybh - ybh * _rowmean(dybh * ybh))
        dpart_ref[:, 0:WA] = dya * z
        dz = dya * u

        @pl.when(i == 0)
        def _():
            gws_ref[...] = jnp.zeros((NHEAD, CHUNK, CHUNK), F32)
            dbs_acc[...] = jnp.zeros((CHUNK, WA), F32)
            vg_ref[...] = jnp.zeros((8, D), F32)
            v5g_ref[...] = jnp.zeros((8, WA), F32)

        first = lax.broadcasted_iota(jnp.int32, (CHUNK, LANES), 1) < HD
        dvs = []
        for q in range(tm // CHUNK):
            dz_q = dz[q * CHUNK:(q + 1) * CHUNK, :]
            vb_q = vb[q * CHUNK:(q + 1) * CHUNK, :]
            dbs_acc[...] += dz_q
            dzb = dz_q.astype(BF16)
            dvs.append(_head_pairs(wm, dzb, transpose=True))
            for hd in range(NHEAD):
                slab = dzb[:, _lanes(hd // 2)]
                dz_hd = jnp.where(first if hd % 2 == 0 else jnp.logical_not(first), slab, jnp.zeros_like(slab))
                gws_ref[hd] += _dot_nt(dz_hd, vb_q[:, _lanes(hd // 2)])
        dv = jnp.concatenate(dvs, axis=0)
        d_gng = _colsum(dv * vh)
        d_gnb = _colsum(dv)
        dvh = dv * v_ref[0:1, :]
        dpart_ref[:, WA:2 * WA] = rv * (dvh - _rowmean(dvh) - vh * _rowmean(dvh * vh))
        dcn = dyb * (sg * (1.0 + cn * (1.0 - sg)))
        d_cng = _colsum(dcn * ch)
        d_cnb = _colsum(dcn)
        dch = dcn * v_ref[3:4, :]
        dconv = rc * (dch - _rowmean(dch) - ch * _rowmean(dch * ch))
        dpart_ref[:, 2 * WA:3 * WA] = dconv
        dpart_ref[:, 3 * WA:4 * WA] = jnp.zeros((tm, WA), F32)
        d_cb = _colsum(dconv)

        @pl.when(i % tps == 0)
        def _():
            mg_ref[...] = jnp.zeros((8, D), F32)

        mg_ref[2:3, :] += d_gt
        vg_ref[1:2, :] += d_gpost
        v5g_ref[0:1, :] += d_gng
        v5g_ref[1:2, :] += d_gnb
        v5g_ref[2:3, :] += d_cb
        v5g_ref[3:4, :] += d_cng
        v5g_ref[4:5, :] += d_cnb
        v5g_ref[5:6, :] += d_goa
        v5g_ref[6:7, :] += d_gob

        @pl.when(i == nt - 1)
        def _():
            row = lax.broadcasted_iota(jnp.int32, (CHUNK, CHUNK), 0)
            col = lax.broadcasted_iota(jnp.int32, (CHUNK, CHUNK), 1)
            for hd in range(NHEAD):
                gws_ref[hd] = jnp.where(col <= row, gws_ref[hd], 0.0)
            gbs_ref[...] = lax.dot_general(e_ref[...], dbs_acc[...], (((1,), (1,)), ((), ())),
                                           precision=lax.Precision.HIGHEST, preferred_element_type=F32)

    tile = pl.BlockSpec((tm, D), lambda i: (i, 0))
    ptile = pl.BlockSpec((tm, 4 * WA), lambda i: (i, 0))
    return _call(
        core, name=name, grid=(nt,), jobs=jobs,
        in_specs=[tile, tile, pl.BlockSpec((tm, 2 * WA), lambda i: (i, 0)), pl.BlockSpec((tm, WA), lambda i: (i, 0)),
                  pl.BlockSpec((None, 8, D), lambda i: (i // tps, 0, 0)), _const_spec((8, D)), _const_spec((D, D)),
                  _const_spec((8, WA)), _const_spec((NHEAD, CHUNK, CHUNK)), _const_spec((CHUNK, WA)),
                  _const_spec((8, WA))],
        out_specs=[ptile, tile, tile, pl.BlockSpec((None, 8, D), lambda i: (i // tps, 0, 0)),
                   pl.BlockSpec((8, D), lambda i: (0, 0)), pl.BlockSpec((8, WA), lambda i: (0, 0)),
                   pl.BlockSpec((NHEAD, CHUNK, CHUNK), lambda i: (0, 0, 0)), pl.BlockSpec((8, CHUNK), lambda i: (0, 0))],
        out_shape=[jax.ShapeDtypeStruct((T, 4 * WA), F32), jax.ShapeDtypeStruct((T, D), BF16),
                   jax.ShapeDtypeStruct((T, D), BF16), jax.ShapeDtypeStruct((nb, 8, D), F32),
                   jax.ShapeDtypeStruct((8, D), F32), jax.ShapeDtypeStruct((8, WA), F32),
                   jax.ShapeDtypeStruct((NHEAD, CHUNK, CHUNK), F32), jax.ShapeDtypeStruct((8, CHUNK), F32)],
        scratch=[pltpu.VMEM((CHUNK, WA), F32)],
        args=[dxo, ym, proj, conv, mod, gvec, w_mo, v512, ws, bias_full, esel])


def _mixer_bwd_b(dxo, x, dpart, proj, mod, gvec, w_mi, cw, tm, name, jobs=()):
    T = x.shape[0]
    nt = T // tm
    nb = mod.shape[0]
    tps = nt // nb
    hpt = tm // HALO
    nh = T // HALO
    off = HALO - (CONV_K - 1)
    p = _pitch(tm)
    ext_rows = 8 * p

    def core(ins, outs, scs):
        dxo_ref, x_ref, dpart_ref, dnext_ref, ag_ref, halo_ref, mod_ref, g_ref, wmi_ref, cw_ref = ins
        dx_ref, dproj_ref, hb_ref, mg_ref, vg_ref, dcw_ref = outs
        glu_ext, dconv_ext, dglu_scr, dcw_acc = scs
        i = pl.program_id(0)
        first = i % tps == 0
        last = i % tps == tps - 1
        a = ag_ref[:, 0:WA]
        g = ag_ref[:, WA:2 * WA]
        sgg = _sigmoid(g)

        @pl.when(i == 0)
        def _():
            glu_ext[:, HALO + tm:HALO + ext_rows, :] = jnp.zeros((NSLAB, ext_rows - tm, LANES), F32)
            dconv_ext[:, HALO + tm:HALO + ext_rows, :] = jnp.zeros((NSLAB, ext_rows - tm, LANES), F32)
            dcw_acc[...] = jnp.zeros((32, 8, WA), F32)
            vg_ref[...] = jnp.zeros((8, D), F32)

        _to_slabs(glu_ext, 0, jnp.where(first, 0.0, halo_ref[:, 0:WA] * _sigmoid(halo_ref[:, WA:2 * WA])))
        _to_slabs(glu_ext, HALO, a * sgg)
        _to_slabs(dconv_ext, 0, dpart_ref[:, 2 * WA:3 * WA])
        _to_slabs(dconv_ext, tm, jnp.where(last, 0.0, dnext_ref[...]))
        sub = lax.broadcasted_iota(jnp.int32, (SUBL, LANES), 0)
        for s in range(NSLAB):
            accs = [jnp.zeros((SUBL, LANES), F32)] * CONV_K
            for v in range(p):
                dc = jnp.where(v + p * sub < tm, dconv_ext[s, pl.ds(v, 8, stride=p), :], 0.0)
                for k in range(CONV_K):
                    accs[k] = accs[k] + dc * glu_ext[s, pl.ds(v + off + k, 8, stride=p), :]
            for k in range(CONV_K):
                dcw_acc[k, :, _lanes(s)] += accs[k]
        dglu = _tap_sum(dconv_ext, dglu_scr, cw_ref, jnp.zeros((1, WA), F32), tm, lambda k: (CONV_K - 1) - k)

        @pl.when(i == nt - 1)
        def _():
            for k in range(CONV_K):
                dcw_ref[k:k + 1, :] = jnp.sum(dcw_acc[k], axis=0, keepdims=True)
            dcw_ref[CONV_K:32, :] = jnp.zeros((32 - CONV_K, WA), F32)

        da = dglu * sgg
        dgg = dglu * a * (sgg * (1.0 - sgg))
        dproj_ref[:, 0:2 * WA] = dpart_ref[:, 0:2 * WA].astype(BF16)
        dproj_ref[:, 2 * WA:3 * WA] = da.astype(BF16)
        dproj_ref[:, 3 * WA:4 * WA] = dgg.astype(BF16)
        dh = jnp.zeros((tm, D), F32)
        for j in range(NDEV):
            dh = dh + _dot_nt(dproj_ref[:, j * MB:(j + 1) * MB], wmi_ref[j])
        xv = x_ref[...]
        sc, sh = mod_ref[1:2, :], mod_ref[0:1, :]
        gpre = g_ref[0:1, :]
        r = lax.rsqrt(_rowmean(xv * xv) + EPS)
        xh = xv * r
        n = xh * gpre
        hb_ref[...] = (n * (1.0 + sc) + sh).astype(BF16)
        d_sc = _colsum(dh * n)
        d_sh = _colsum(dh)
        dn = dh * (1.0 + sc)
        d_gpre = _colsum(dn * xh)
        dxh = dn * gpre
        dx_ref[...] = dxo_ref[...] + r * (dxh - xh * _rowmean(dxh * xh))

        @pl.when(first)
        def _():
            mg_ref[...] = jnp.zeros((8, D), F32)

        mg_ref[0:1, :] += d_sh
        mg_ref[1:2, :] += d_sc
        vg_ref[0:1, :] += d_gpre

    tile = pl.BlockSpec((tm, D), lambda i: (i, 0))
    return _call(
        core, name=name, grid=(nt,), jobs=jobs,
        in_specs=[tile, tile, pl.BlockSpec((tm, 4 * WA), lambda i: (i, 0)),
                  pl.BlockSpec((HALO, WA), lambda i: (jnp.minimum((i + 1) * hpt, nh - 1), 2)),
                  pl.BlockSpec((tm, 2 * WA), lambda i: (i, 1)),
                  pl.BlockSpec((HALO, 2 * WA), lambda i: (jnp.maximum(i * hpt - 1, 0), 1)),
                  pl.BlockSpec((None, 8, D), lambda i: (i // tps, 0, 0)), _const_spec((8, D)),
                  _const_spec((NDEV, D, MB)), _const_spec((32, WA))],
        out_specs=[tile, pl.BlockSpec((tm, 4 * WA), lambda i: (i, 0)), tile,
                   pl.BlockSpec((None, 8, D), lambda i: (i // tps, 0, 0)), pl.BlockSpec((8, D), lambda i: (0, 0)),
                   pl.BlockSpec((32, WA), lambda i: (0, 0))],
        out_shape=[jax.ShapeDtypeStruct((T, D), F32), jax.ShapeDtypeStruct((T, 4 * WA), BF16),
                   jax.ShapeDtypeStruct((T, D), BF16), jax.ShapeDtypeStruct((nb, 8, D), F32),
                   jax.ShapeDtypeStruct((8, D), F32), jax.ShapeDtypeStruct((32, WA), F32)],
        scratch=[pltpu.VMEM((NSLAB, HALO + ext_rows, LANES), F32), pltpu.VMEM((NSLAB, HALO + ext_rows, LANES), F32),
                 pltpu.VMEM((NSLAB, ext_rows, LANES), F32), pltpu.VMEM((32, 8, WA), F32)],
        args=[dxo, x, dpart, dpart, proj, proj, mod, gvec, w_mi, cw])


def _grad_chip(a, b, a_spec, b_spec, prod_shape, half, name, jobs=(), via_b=False):
    steps = 8 if half is None else 4
    R = prod_shape[0] if half is None else half
    C = prod_shape[1]

    def core(ins, outs, scs):
        a_ref, b_ref = ins
        (o_ref,) = outs
        own, snd, rcv, ssem, rsem, lsem = scs
        s = pl.program_id(0)
        c = lax.axis_index("c")
        me = _me()
        sib = _flip(me, (0, 0, 1))
        if via_b:
            prod = _dot_tn(b_ref[...], a_ref[...]).T.astype(BF16)
        else:
            prod = _dot_tn(a_ref[...], b_ref[...]).astype(BF16)
        if half is None:
            q = s // 2

            @pl.when(s % 2 == c)
            def _():
                own[q] = prod

            @pl.when(s % 2 != c)
            def _():
                snd[q] = prod
                _remote(snd.at[q], rcv.at[q], ssem.at[q], rsem.at[q], sib).start()
        else:
            lo = prod[0:half, :]
            hi = prod[half:2 * half, :]
            own[s] = jnp.where(c == 0, lo, hi)
            snd[s] = jnp.where(c == 0, hi, lo)
            _remote(snd.at[s], rcv.at[s], ssem.at[s], rsem.at[s], sib).start()

        @pl.when(s == steps - 1)
        def _():
            for q4 in range(4):
                cp = _remote(snd.at[q4], rcv.at[q4], ssem.at[q4], rsem.at[q4], sib)
                cp.wait_recv()
                cp.wait_send()
                snd[q4] = (own[q4].astype(F32) + rcv[q4].astype(F32)).astype(BF16)
            out = pltpu.make_async_copy(snd, o_ref, lsem)
            out.start()
            out.wait()

    return _call(
        core, name=name, grid=(steps,), jobs=jobs, in_specs=[a_spec, b_spec], out_specs=[HBM],
        out_shape=[jax.ShapeDtypeStruct((4, R, C), BF16)],
        scratch=[pltpu.VMEM((4, R, C), BF16), pltpu.VMEM((4, R, C), BF16), pltpu.VMEM((4, R, C), BF16),
                 pltpu.SemaphoreType.DMA((4,)), pltpu.SemaphoreType.DMA((4,)), pltpu.SemaphoreType.DMA],
        args=[a, b])


def _grad_w_in(dg, hb, name, jobs=()):
    T = hb.shape[0]
    return _grad_chip(dg, hb, pl.BlockSpec((None, T, FB), lambda s: (s, 0, 0)), _const_spec((T, D)),
                      (FB, D), None, name, jobs)


def _grad_w_out(act, dyb, name, jobs=()):
    T = dyb.shape[0]
    return _grad_chip(act, dyb, pl.BlockSpec((None, T, FB), lambda s: (s, 0, 0)), _const_spec((T, D)),
                      (FB, D), FO, name, jobs)


def _grad_w_mi(hb, dproj, name, jobs=()):
    T = hb.shape[0]
    return _grad_chip(hb, dproj, _const_spec((T, D)), pl.BlockSpec((T, MB), lambda s: (0, s)),
                      (D, MB), None, name, jobs, via_b=True)


def _grad_w_mo(ycat, dym, name, jobs=()):
    T = ycat.shape[0]
    return _grad_chip(ycat, dym, pl.BlockSpec((T, 2 * MO), lambda s: (0, s)), _const_spec((T, D)),
                      (2 * MO, D), MO, name, jobs)


def _adamw_math(w, g, m, v):
    m2 = ADAM_B1 * m + (1.0 - ADAM_B1) * g
    v2 = ADAM_B2 * v + (1.0 - ADAM_B2) * (g * g)
    m_hat = m2 / (1.0 - ADAM_B1 ** ADAM_STEP)
    v_hat = v2 / (1.0 - ADAM_B2 ** ADAM_STEP)
    delta = -ADAM_LR * (m_hat / (jnp.sqrt(v_hat) + ADAM_EPS) + ADAM_WD * w)
    return delta, m2, v2


def _adamw_reduce(parts, w, m, v, tr, name, own=None, after=None):
    R, C = w.shape

    def core(ins, outs, _):
        p_ref, w_ref, m_ref, v_ref = ins[:4]
        g_ref, d_ref, m2_ref, v2_ref = outs
        if own is None:
            terms = [p_ref[s].astype(F32) for s in range(4)]
        else:
            mq = 2 * lax.axis_index("x") + lax.axis_index("y")
            mine = ins[4][...].astype(F32)
            terms = [jnp.where(mq == s, mine, p_ref[s].astype(F32)) for s in range(4)]
        g = terms[0]
        for s in range(1, 4):
            g = g + terms[s]
        g_ref[...] = g
        d_ref[...], m2_ref[...], v2_ref[...] = _adamw_math(w_ref[...], g, m_ref[...], v_ref[...])

    blk = pl.BlockSpec((tr, C), lambda i: (i, 0))
    in_specs = [pl.BlockSpec((4, tr, C), lambda i: (0, i, 0)), blk, blk, blk]
    args = [parts, w, m, v]
    if own is not None:
        mq = 2 * lax.axis_index("x") + lax.axis_index("y")
        in_specs.append(pl.BlockSpec((tr, C), lambda i: (i, 0)))
        args.append(lax.dynamic_index_in_dim(own, mq, 0, keepdims=False))
    if after is not None:
        in_specs.append(HBM)
        args.append(after)
    return _call(
        core, name=name, grid=(R // tr,), in_specs=in_specs,
        out_specs=[blk, blk, blk, blk], out_shape=[jax.ShapeDtypeStruct((R, C), F32)] * 4, args=args)[0]


HBM_ONLY = pl.BlockSpec(memory_space=pltpu.HBM)
SEM = pl.BlockSpec(memory_space=pltpu.SEMAPHORE)
EFFECT = pltpu.SideEffectType.DATAFLOW_SIDE_EFFECTING


def _chip_scatter_start(gs, name):
    n = len(gs)

    def body(*refs):
        g_refs, land_refs = refs[:n], refs[n:2 * n]
        ssem, rsem = refs[2 * n:2 * n + 2]
        token = refs[-1]
        me = _me()
        mq = 2 * me[0] + me[1]
        for k, f in enumerate(CHIP_FLIPS):
            p = _flip(me, f)
            for a in range(n):
                _remote(g_refs[a].at[2 * p[0] + p[1]], land_refs[a].at[mq], ssem.at[3 * a + k], rsem.at[3 * a + k], p).start()
        token[...] = jnp.zeros_like(token)

    gs = [pltpu.with_memory_space_constraint(g, pltpu.HBM) for g in gs]
    lands = [pltpu.with_memory_space_constraint(lax.empty(g.shape, g.dtype), pltpu.HBM) for g in gs]
    res = pl.pallas_call(
        body, name=name,
        out_shape=(pltpu.SemaphoreType.DMA((3 * n,)), pltpu.SemaphoreType.DMA((3 * n,)))
        + tuple(pltpu.HBM(g.shape, g.dtype) for g in gs) * 2 + (jax.ShapeDtypeStruct((SUBL, LANES), F32),),
        in_specs=(HBM_ONLY,) * (2 * n), out_specs=(SEM, SEM) + (HBM_ONLY,) * (2 * n) + (VM,),
        input_output_aliases={a: 2 + a for a in range(2 * n)},
        compiler_params=pltpu.CompilerParams(has_side_effects=EFFECT),
    )(*gs, *lands)
    return res[:-1], res[-1]


def _chip_scatter_wait(handle, after, name):
    ssem, rsem = handle[:2]
    n = (len(handle) - 2) // 2
    thru = handle[2:]

    def body(*refs):
        g_refs, land_refs = refs[:n], refs[n:2 * n]
        ssem, rsem = refs[2 * n:2 * n + 2]
        me = _me()
        mq = 2 * me[0] + me[1]
        for k, f in enumerate(CHIP_FLIPS):
            p = _flip(me, f)
            pq = 2 * p[0] + p[1]
            for a in range(n):
                _remote(g_refs[a].at[pq], land_refs[a].at[mq], ssem.at[3 * a + k], rsem.at[3 * a + k], p).wait_send()
                _remote(g_refs[a].at[mq], land_refs[a].at[pq], ssem.at[3 * a + k], rsem.at[3 * a + k], p).wait_recv()

    res = pl.pallas_call(
        body, name=name,
        out_shape=tuple(pltpu.HBM(t.shape, t.dtype) for t in thru),
        in_specs=(HBM_ONLY,) * (2 * n) + (SEM, SEM, HBM), out_specs=(HBM_ONLY,) * (2 * n),
        input_output_aliases={a: a for a in range(2 * n)},
        compiler_params=pltpu.CompilerParams(has_side_effects=EFFECT),
    )(*thru, ssem, rsem, after)
    return list(res[:n]), list(res[n:])


def _adamw_ada(sc_all, dd, w, m, v, tr, name, after=None):
    R, C = w.shape

    def core(ins, outs, _):
        sc_ref, dd_ref, w_ref, m_ref, v_ref = ins[:5]
        g_ref, d_ref, m2_ref, v2_ref = outs
        g = _dot_tn(sc_ref[...].astype(BF16), dd_ref[...].astype(BF16))
        g_ref[...] = g
        d_ref[...], m2_ref[...], v2_ref[...] = _adamw_math(w_ref[...], g, m_ref[...], v_ref[...])

    blk = pl.BlockSpec((tr, C), lambda i: (i, 0))
    return _call(
        core, name=name, grid=(R // tr,),
        in_specs=[pl.BlockSpec((64, tr), lambda i: (0, i)), pl.BlockSpec((64, C), lambda i: (0, 0)), blk, blk, blk]
        + [HBM] * (after is not None),
        out_specs=[blk, blk, blk, blk], out_shape=[jax.ShapeDtypeStruct((R, C), F32)] * 4,
        args=[sc_all, dd, w, m, v] + [after] * (after is not None))[0]


def _adamw_small(gathered, plain, grads, wmv, emit, name):
    nw = len(grads)
    ng, npl, ne = len(gathered), len(plain), len(emit)

    def core(ins, outs, _):
        srcs = []
        for a in range(ng):
            s = ins[a][0]
            for dev in range(1, NDEV):
                s = s + ins[a][dev]
            srcs.append(s)
        srcs += [ins[ng + a][...] for a in range(npl)]
        w_refs = ins[ng + npl:]
        for e, a in enumerate(emit):
            outs[e][...] = srcs[a]
        for t in range(nw):
            src, row = grads[t]
            g = srcs[src] if row is None else srcs[src][row:row + 1, :]
            w_ref, m_ref, v_ref = w_refs[3 * t:3 * t + 3]
            g_ref, d_ref, m2_ref, v2_ref = outs[ne + 4 * t:ne + 4 * t + 4]
            g_ref[...] = g
            d_ref[...], m2_ref[...], v2_ref[...] = _adamw_math(w_ref[...], g, m_ref[...], v_ref[...])

    out_shape = [jax.ShapeDtypeStruct(gathered[a].shape[1:], F32) for a in emit]
    for t in range(nw):
        out_shape += [jax.ShapeDtypeStruct(wmv[3 * t].shape, F32)] * 4
    return _call(
        core, name=name, grid=(), in_specs=[VM] * (ng + npl + 3 * nw), out_specs=[VM] * (ne + 4 * nw),
        out_shape=out_shape, args=list(gathered) + list(plain) + list(wmv))[0]


def _ada_fwd(c_pad, w_ada, b_cols, cw_pad, jobs=()):
    def core(ins, outs, scs, start_jobs):
        c_ref, w_ref, b_ref, cwp_ref = ins
        ada_ref, sc_ref, cw_ref = outs
        cbuf, send_buf, ssem, rsem = scs
        me = _me()
        mi = _lin(me)
        cbuf[mi] = c_ref[...]
        cw_ref[mi] = cwp_ref[...]
        peers = [_flip(me, f) for f in FLIPS]
        first = []
        for k, p in enumerate(peers):
            first.append(_remote(cbuf.at[mi], cbuf.at[mi], ssem.at[k], rsem.at[k], p))
            first.append(_remote(cw_ref.at[mi], cw_ref.at[mi], ssem.at[7 + k], rsem.at[7 + k], p))
        for cp in first:
            cp.start()
        for k, p in enumerate(peers):
            pi = _lin(p)
            _remote(cbuf.at[pi], cbuf.at[pi], ssem.at[k], rsem.at[k], p).wait_recv()
            _remote(cw_ref.at[pi], cw_ref.at[pi], ssem.at[7 + k], rsem.at[7 + k], p).wait_recv()
        c_all = cbuf[...].reshape(8 * 8, D)
        sc = c_all * _sigmoid(c_all)
        sc_ref[...] = sc
        res = _dot(sc.astype(BF16), w_ref[...].astype(BF16)) + b_ref[...]
        send_buf[...] = res.reshape(8, 8, ADA_B)
        ada_ref[mi] = send_buf[mi]
        second = []
        for k, p in enumerate(peers):
            second.append(_remote(send_buf.at[_lin(p)], ada_ref.at[mi], ssem.at[14 + k], rsem.at[14 + k], p))
        for cp in second:
            cp.start()
        start_jobs()
        for k, p in enumerate(peers):
            _remote(send_buf.at[mi], ada_ref.at[_lin(p)], ssem.at[14 + k], rsem.at[14 + k], p).wait_recv()
        for cp in first + second:
            cp.wait_send()

    return _call(
        core, name="ada_fwd", grid=(), jobs=jobs, core_starts=True, in_specs=[VM, VM, VM, VM], out_specs=[VM, VM, VM],
        out_shape=[jax.ShapeDtypeStruct((8, 8, ADA_B), F32), jax.ShapeDtypeStruct((64, D), F32),
                   jax.ShapeDtypeStruct((8, 32, 64), F32)],
        scratch=[pltpu.VMEM((8, 8, D), F32), pltpu.VMEM((8, 8, ADA_B), F32),
                 pltpu.SemaphoreType.DMA((21,)), pltpu.SemaphoreType.DMA((21,))],
        args=[c_pad, w_ada, b_cols, cw_pad])


def _ada_bwd(dada, jobs=()):
    def core(ins, outs, scs):
        (d_ref,) = ins
        dd_ref, gb_ref = outs
        rbuf, ssem, rsem = scs
        me = _me()
        mi = _lin(me)
        peers = [_flip(me, f) for f in FLIPS]
        rbuf[mi] = d_ref[mi]
        first = []
        for k, p in enumerate(peers):
            first.append(_remote(d_ref.at[_lin(p)], rbuf.at[mi], ssem.at[k], rsem.at[k], p))
        for cp in first:
            cp.start()
        for k, p in enumerate(peers):
            _remote(d_ref.at[mi], rbuf.at[_lin(p)], ssem.at[k], rsem.at[k], p).wait_recv()
        dd = rbuf[...].reshape(64, ADA_B)
        dd_ref[...] = dd
        gb_ref[mi] = jnp.broadcast_to(_colsum(dd), (8, ADA_B))
        second = []
        for k, p in enumerate(peers):
            second.append(_remote(gb_ref.at[mi], gb_ref.at[mi], ssem.at[7 + k], rsem.at[7 + k], p))
        for cp in second:
            cp.start()
        for k, p in enumerate(peers):
            pi = _lin(p)
            _remote(gb_ref.at[pi], gb_ref.at[pi], ssem.at[7 + k], rsem.at[7 + k], p).wait_recv()
        for cp in first + second:
            cp.wait_send()

    return _call(
        core, name="ada_bwd", grid=(), jobs=jobs, in_specs=[VM], out_specs=[VM, VM],
        out_shape=[jax.ShapeDtypeStruct((64, ADA_B), F32), jax.ShapeDtypeStruct((8, 8, ADA_B), F32)],
        scratch=[pltpu.VMEM((8, 8, ADA_B), F32), pltpu.SemaphoreType.DMA((14,)), pltpu.SemaphoreType.DMA((14,))],
        args=[dada])


SMALL_D = ("g_pre_f1", "g_post_f1", "g_pre_m", "g_post_m", "g_pre_f2", "g_post_f2")
SMALL_W = ("gmlp_norm_g", "gmlp_norm_b", "conv_b", "conv_norm_g", "conv_norm_b", "g_out_a", "g_out_b")


def kernel(x, c, w_ada, b_ada, g_pre_f1, g_post_f1, w_f1_in, w_f1_out, g_pre_m, g_post_m, w_mix_in, gmlp_norm_g, gmlp_norm_b, w_spatial, b_spatial, conv_w, conv_b, conv_norm_g, conv_norm_b, g_out_a, g_out_b, w_mix_out, g_pre_f2, g_post_f2, w_f2_in, w_f2_out, loss_target, m_w_ada, m_b_ada, m_g_pre_f1, m_g_post_f1, m_w_f1_in, m_w_f1_out, m_g_pre_m, m_g_post_m, m_w_mix_in, m_gmlp_norm_g, m_gmlp_norm_b, m_w_spatial, m_b_spatial, m_conv_w, m_conv_b, m_conv_norm_g, m_conv_norm_b, m_g_out_a, m_g_out_b, m_w_mix_out, m_g_pre_f2, m_g_post_f2, m_w_f2_in, m_w_f2_out, v_w_ada, v_b_ada, v_g_pre_f1, v_g_post_f1, v_w_f1_in, v_w_f1_out, v_g_pre_m, v_g_post_m, v_w_mix_in, v_gmlp_norm_g, v_gmlp_norm_b, v_w_spatial, v_b_spatial, v_conv_w, v_conv_b, v_conv_norm_g, v_conv_norm_b, v_g_out_a, v_g_out_b, v_w_mix_out, v_g_pre_f2, v_g_post_f2, v_w_f2_in, v_w_f2_out):
    given = dict(locals())
    bl, seq, _ = x.shape
    T = bl * seq
    tm = min(256, seq // 2)
    mi = _lin((lax.axis_index("x"), lax.axis_index("y"), lax.axis_index("c")))

    def shard_in(w):
        return w[0].T.astype(BF16)

    g_f1 = _Gather([shard_in(w_f1_in), w_f1_out[0].astype(BF16)], ("rows", "out"))
    s_f2 = shard_in(w_f2_in)
    g_mx = _Gather([w_mix_in[0].astype(BF16), w_mix_out[0].astype(BF16), w_f2_out[0].astype(BF16), s_f2[:, 0:D // 4]],
                   ("rows", "rows", "out", "rows"), late_mid=True)
    g_f2 = _Gather([s_f2[:, D // 4:D]], ("rows",))

    c_pad = jnp.pad(c, ((0, 8 - bl), (0, 0)))
    b_cols = lax.dynamic_slice(b_ada, (0, mi * ADA_B), (1, ADA_B))
    cw_pad = jnp.pad(conv_w[0], ((0, 1), (0, 0)))
    (ada_blk, sc_all, cw_all), ((wi1, wo1),) = _ada_fwd(c_pad, w_ada[0], b_cols, cw_pad, jobs=[g_f1])
    ada = ada_blk[:, 0:bl, :].transpose(1, 0, 2).reshape(bl, 9, D)
    pad5 = jnp.zeros((bl, 5, D), F32)
    mod1 = jnp.concatenate([ada[:, 0:3], pad5], axis=1)
    mod2 = jnp.concatenate([ada[:, 3:6], pad5], axis=1)
    mod3 = jnp.concatenate([ada[:, 6:9], pad5], axis=1)
    cw_full = cw_all.transpose(1, 0, 2).reshape(32, WA)

    zrow = jnp.zeros((1, D), F32)
    gv1 = jnp.concatenate([g_pre_f1, g_post_f1] + [zrow] * 6, axis=0)
    gvm = jnp.concatenate([g_pre_m, g_post_m] + [zrow] * 6, axis=0)
    gv2 = jnp.concatenate([g_pre_f2, g_post_f2] + [zrow] * 6, axis=0)
    v512 = jnp.concatenate([gmlp_norm_g, gmlp_norm_b, conv_b, conv_norm_g, conv_norm_b, g_out_a, g_out_b,
                            jnp.zeros((1, WA), F32)], axis=0)
    ws = w_spatial[0]
    bias_full = jnp.repeat(b_spatial[0].T, HD, axis=1)
    esel = (lax.broadcasted_iota(jnp.int32, (8, WA), 1) // HD == lax.broadcasted_iota(jnp.int32, (8, WA), 0)).astype(F32)

    x0 = x.reshape(T, D)
    (x1, gu1, y1), ((wmi, wmo, wo2, wi2a),) = _ffn_fwd(x0, mod1, gv1, wi1, wo1, tm, "ffn1_fwd", jobs=[g_mx])
    wmo = wmo.reshape(D, D)
    (x2, proj, ym, conv), ((wi2b,),) = _mixer_fwd(x1, mod2, gvm, wmi, wmo, v512, ws, bias_full, cw_full, tm, "mixer_fwd", jobs=[g_f2])

    (dx2, dg2, act2, hb2, dyb2, mg3, vg3, loss_blk), _ = _ffn_last(
        x2, loss_target.reshape(T, D), mod3, gv2, (wi2a, wi2b), wo2, tm, "ffn2_fwd_bwd")
    (g_wi2,), _ = _grad_w_in(dg2, hb2, "ffn2_gw_in")
    (g_wo2,), _ = _grad_w_out(act2, dyb2, "ffn2_gw_out")
    (dpart, dymb, ycat, mg2a, vgma, v5g, gws, gbs), ((p_wo2,),) = _mixer_bwd_a(
        dx2, ym, proj, conv, mod2, gvm, wmo, v512, ws, bias_full, esel, tm, "mixer_bwd_a",
        jobs=[_ChipScatter([g_wo2])])
    (dx1, dproj, hbm, mg2b, vgmb, dcw), ((p_wi2,),) = _mixer_bwd_b(
        dx2, x1, dpart, proj, mod2, gvm, wmi, cw_full, tm, "mixer_bwd_b", jobs=[_ChipScatter([g_wi2])])
    (g_wmi,), _ = _grad_w_mi(hbm, dproj, "mixer_gw_in")
    (g_wmo,), _ = _grad_w_mo(ycat, dymb, "mixer_gw_out")
    p2 = jnp.concatenate([v5g, dcw], axis=0)
    (dx0, dg1, act1, hb1, dyb1, mg1, vg1), _ = _ffn_bwd(dx1, x0, y1, gu1, mod1, gv1, wi1, wo1, tm, "ffn1_bwd")
    (g_wo1,), ((p_wmi, p_wmo),) = _grad_w_out(act1, dyb1, "ffn1_gw_out", jobs=[_ChipScatter([g_wmi, g_wmo])])

    dada = jnp.concatenate([mg1[:, 0:3], mg2b[:, 0:2], mg2a[:, 2:3], mg3[:, 0:3]], axis=1)
    dada = dada.reshape(bl, NDEV, ADA_B).transpose(1, 0, 2)
    dada = jnp.pad(dada, ((0, 0), (0, 8 - bl), (0, 0)))
    p1 = jnp.concatenate([vg1[0:2], vgmb[0:1], vgma[1:2], vg3[0:2], loss_blk[0:1], zrow], axis=0)
    (dd_all, gb_all), _ = _ada_bwd(dada)
    g_bada = gb_all[:, 0, :].reshape(1, 9 * D)

    (g_wi1,), ((a1, a2, a3, a4), (p_wo1,)) = _grad_w_in(
        dg1, hb1, "ffn1_gw_in", jobs=[_Gather([p1, p2, gws, gbs], ("rows",) * 4), _ChipScatter([g_wo1])])

    h_f1, token = _chip_scatter_start([g_wi1], "tail_start")

    res = {}
    quad = _adamw_reduce(p_wi2, w_f2_in[0].T, m_w_f2_in[0].T, v_w_f2_in[0].T, FO, "adamw_w_f2_in", after=token)
    res["w_f2_in"] = tuple(t.T[None] for t in quad)
    for nm, part, tr in (("w_f2_out", p_wo2, FO), ("w_mix_in", p_wmi, 256), ("w_mix_out", p_wmo, MO), ("w_f1_out", p_wo1, FO)):
        quad = _adamw_reduce(part, given[nm][0], given["m_" + nm][0], given["v_" + nm][0], tr, "adamw_" + nm, after=quad[1])
        res[nm] = tuple(t[None] for t in quad)
    quad = _adamw_ada(sc_all, dd_all, w_ada[0], m_w_ada[0], v_w_ada[0], 256, "adamw_w_ada", after=quad[1])
    res["w_ada"] = tuple(t[None] for t in quad)
    (g_wi1,), (p_wi1,) = _chip_scatter_wait(h_f1, quad[1], "tail_wait")
    quad = _adamw_reduce(p_wi1, w_f1_in[0].T, m_w_f1_in[0].T, v_w_f1_in[0].T, FO, "adamw_w_f1_in", own=g_wi1)
    res["w_f1_in"] = tuple(t.T[None] for t in quad)

    small = SMALL_D + SMALL_W + ("w_spatial", "b_spatial", "b_ada")
    grads = [(0, r) for r in range(6)] + [(1, r) for r in range(7)] + [(2, None), (3, None), (4, None)]
    wmv = []
    for nm in small:
        for pre in ("", "m_", "v_"):
            wmv.append(given[pre + nm][0] if nm in ("w_spatial", "b_spatial") else given[pre + nm])
    outs = _adamw_small([a1, a2, a3, a4], [g_bada], grads, wmv, (0, 1), "adamw_small")
    loss = outs[0][6, 0]
    for t, nm in enumerate(small):
        quad = outs[2 + 4 * t:6 + 4 * t]
        res[nm] = tuple(q[None] for q in quad) if nm in ("w_spatial", "b_spatial") else tuple(quad)
    g_cw = lax.dynamic_slice(outs[1], (8, mi * 64), (32, 64))
    wmv = [jnp.pad(given[pre + "conv_w"][0], ((0, 1), (0, 0)), constant_values=1.0 if pre == "v_" else 0.0)
           for pre in ("", "m_", "v_")]
    quad = _adamw_small([], [g_cw], [(0, None)], wmv, (), "adamw_conv_w")
    res["conv_w"] = tuple(q[0:CONV_K][None] for q in quad)

    order = ["w_ada", "b_ada", "g_pre_f1", "g_post_f1", "w_f1_in", "w_f1_out", "g_pre_m", "g_post_m", "w_mix_in",
             "gmlp_norm_g", "gmlp_norm_b", "w_spatial", "b_spatial", "conv_w", "conv_b", "conv_norm_g", "conv_norm_b",
             "g_out_a", "g_out_b", "w_mix_out", "g_pre_f2", "g_post_f2", "w_f2_in", "w_f2_out"]
    out = [loss, dx0.reshape(bl, seq, D)]
    for k in range(4):
        out += [res[nm][k] for nm in order]
    return tuple(out)
```

```python
import jax
import jax.numpy as jnp
from jax import lax
from jax.experimental import pallas as pl
from jax.experimental.pallas import tpu as pltpu

F32 = jnp.float32
BF16 = jnp.bfloat16

D = 1024
DFF = 2816
NDEV = 8
FB = 2 * DFF // NDEV
NCH = DFF // FB
LANES = 128
SUBL = 8
FO = DFF // NDEV
WA = 512
NSLAB = WA // LANES
NHEAD = 8
HD = 64
CHUNK = 128
CONV_K = 31
HALO = 32
MB = 2 * (WA + WA) // NDEV
MO = D // NDEV
ADA_B = 9 * D // NDEV
EPS = 1e-6
HALF = 0.5

ADAM_LR = 0.001
ADAM_B1 = 0.9
ADAM_B2 = 0.999
ADAM_EPS = 1e-08
ADAM_WD = 0.01
ADAM_STEP = 10

VMEM_LIMIT = 56 * 1024 * 1024
MESH = pl.DeviceIdType.MESH
FLIPS = ((0, 0, 1), (1, 0, 0), (0, 1, 0), (1, 1, 0), (1, 0, 1), (0, 1, 1), (1, 1, 1))
CHIP_FLIPS = ((1, 0, 0), (0, 1, 0), (1, 1, 0))
HBM = pl.BlockSpec(memory_space=pl.ANY)
VM = pl.BlockSpec(memory_space=pltpu.VMEM)


def _dot(a, b):
    return lax.dot_general(a, b, (((1,), (0,)), ((), ())), preferred_element_type=F32)


def _dot_nt(a, b):
    return lax.dot_general(a, b, (((1,), (1,)), ((), ())), preferred_element_type=F32)


def _dot_tn(a, b):
    return lax.dot_general(a, b, (((0,), (0,)), ((), ())), preferred_element_type=F32)


def _rowmean(v):
    return jnp.mean(v, axis=-1, keepdims=True)


def _colsum(v):
    return jnp.sum(v, axis=0, keepdims=True)


def _sigmoid(v):
    return 0.5 * jnp.tanh(0.5 * v) + 0.5


def _const_spec(shape):
    nd = len(shape)
    return pl.BlockSpec(shape, lambda *_: (0,) * nd, pipeline_mode=pl.Buffered(1))


def _me():
    return lax.axis_index("x"), lax.axis_index("y"), lax.axis_index("c")


def _flip(me, f):
    return tuple(1 - v if b else v for v, b in zip(me, f))


def _lin(p):
    return 4 * p[0] + 2 * p[1] + p[2]


def _remote(src, dst, send_sem, recv_sem, dev):
    return pltpu.make_async_remote_copy(src_ref=src, dst_ref=dst, send_sem=send_sem, recv_sem=recv_sem,
                                        device_id=dev, device_id_type=MESH)


def _blk(kind, ref, p):
    if kind == "out":
        return ref.at[2 * p[0] + p[1], pl.ds(p[2] * FO, FO), :]
    return ref.at[_lin(p)]


class _Gather:
    def __init__(self, shards, kinds, late_mid=False):
        self.late_mid = late_mid
        self.kinds = kinds
        self.n = len(shards)
        self.ins = list(shards)
        self.out_shape = [jax.ShapeDtypeStruct((4, FB, D) if k == "out" else (NDEV,) + s.shape, s.dtype)
                          for s, k in zip(shards, kinds)]
        self.sems = [pltpu.SemaphoreType.DMA((7 * self.n,)), pltpu.SemaphoreType.DMA((7 * self.n,)),
                     pltpu.SemaphoreType.DMA((self.n,))]

    def _first(self, ins, outs, sems):
        ssem, rsem, lsem = sems
        me = _me()
        sib = _flip(me, (0, 0, 1))
        cps, loc = [], []
        for a in range(self.n):
            mine = _blk(self.kinds[a], outs[a], me)
            loc.append(pltpu.make_async_copy(ins[a], mine, lsem.at[a]))
            cps.append(_remote(ins[a], mine, ssem.at[7 * a], rsem.at[7 * a], sib))
            for j, f in enumerate(CHIP_FLIPS):
                cps.append(_remote(ins[a], mine, ssem.at[7 * a + 1 + j], rsem.at[7 * a + 1 + j], _flip(me, f)))
        return cps, loc

    def _passed(self, outs, sems):
        ssem, rsem, _ = sems
        me = _me()
        sib = _flip(me, (0, 0, 1))
        cps = []
        for j, f in enumerate(CHIP_FLIPS):
            for a in range(self.n):
                blk = _blk(self.kinds[a], outs[a], _flip(me, f))
                cps.append(_remote(blk, blk, ssem.at[7 * a + 4 + j], rsem.at[7 * a + 4 + j], sib))
        return cps

    def start(self, ins, outs, sems):
        cps, loc = self._first(ins, outs, sems)
        for cp in loc + cps:
            cp.start()

    def mid(self, ins, outs, sems):
        ssem, rsem, _ = sems
        me = _me()
        passed = self._passed(outs, sems)
        t = 0
        for j, f in enumerate(CHIP_FLIPS):
            for a in range(self.n):
                blk = _blk(self.kinds[a], outs[a], _flip(me, f))
                _remote(blk, blk, ssem.at[7 * a + 1 + j], rsem.at[7 * a + 1 + j], _flip(me, f)).wait_recv()
                passed[t].start()
                t += 1

    def end(self, ins, outs, sems):
        ssem, rsem, _ = sems
        me = _me()
        sib = _flip(me, (0, 0, 1))
        for a in range(self.n):
            blk = _blk(self.kinds[a], outs[a], sib)
            _remote(blk, blk, ssem.at[7 * a], rsem.at[7 * a], sib).wait_recv()
            for j, f in enumerate(CHIP_FLIPS):
                blk = _blk(self.kinds[a], outs[a], _flip(_flip(me, f), (0, 0, 1)))
                _remote(blk, blk, ssem.at[7 * a + 4 + j], rsem.at[7 * a + 4 + j], sib).wait_recv()
        cps, loc = self._first(ins, outs, sems)
        for cp in cps + self._passed(outs, sems):
            cp.wait_send()
        for cp in loc:
            cp.wait()


class _ChipScatter:
    def __init__(self, grads):
        self.n = len(grads)
        self.ins = list(grads)
        self.out_shape = [jax.ShapeDtypeStruct(g.shape, BF16) for g in grads]
        self.sems = [pltpu.SemaphoreType.DMA((3 * self.n,)), pltpu.SemaphoreType.DMA((3 * self.n,)),
                     pltpu.SemaphoreType.DMA((self.n,))]

    def _copies(self, ins, outs, sems):
        ssem, rsem, lsem = sems
        me = _me()
        mq = 2 * me[0] + me[1]
        loc = [pltpu.make_async_copy(ins[a].at[mq], outs[a].at[mq], lsem.at[a]) for a in range(self.n)]
        cps = []
        for k, f in enumerate(CHIP_FLIPS):
            p = _flip(me, f)
            for a in range(self.n):
                cps.append(_remote(ins[a].at[2 * p[0] + p[1]], outs[a].at[mq], ssem.at[3 * a + k], rsem.at[3 * a + k], p))
        return cps, loc

    def start(self, ins, outs, sems):
        cps, loc = self._copies(ins, outs, sems)
        for cp in loc + cps:
            cp.start()

    mid = None

    def end(self, ins, outs, sems):
        ssem, rsem, _ = sems
        me = _me()
        mq = 2 * me[0] + me[1]
        for k, f in enumerate(CHIP_FLIPS):
            p = _flip(me, f)
            for a in range(self.n):
                _remote(ins[a].at[mq], outs[a].at[2 * p[0] + p[1]], ssem.at[3 * a + k], rsem.at[3 * a + k], p).wait_recv()
        cps, loc = self._copies(ins, outs, sems)
        for cp in cps:
            cp.wait_send()
        for cp in loc:
            cp.wait()


class _AllGather:
    def __init__(self, parts):
        self.n = len(parts)
        self.ins = list(parts)
        self.out_shape = [jax.ShapeDtypeStruct((NDEV,) + p.shape, p.dtype) for p in parts]
        self.sems = [pltpu.SemaphoreType.DMA((7 * self.n,)), pltpu.SemaphoreType.DMA((7 * self.n,)),
                     pltpu.SemaphoreType.DMA((self.n,))]

    def _copies(self, ins, outs, sems):
        ssem, rsem, lsem = sems
        me = _me()
        mi = _lin(me)
        loc = [pltpu.make_async_copy(ins[a], outs[a].at[mi], lsem.at[a]) for a in range(self.n)]
        cps = []
        for k, f in enumerate(FLIPS):
            for a in range(self.n):
                cps.append(_remote(ins[a], outs[a].at[mi], ssem.at[7 * a + k], rsem.at[7 * a + k], _flip(me, f)))
        return cps, loc

    def start(self, ins, outs, sems):
        cps, loc = self._copies(ins, outs, sems)
        for cp in loc + cps:
            cp.start()

    mid = None

    def end(self, ins, outs, sems):
        ssem, rsem, _ = sems
        me = _me()
        for k, f in enumerate(FLIPS):
            p = _flip(me, f)
            for a in range(self.n):
                _remote(ins[a], outs[a].at[_lin(p)], ssem.at[7 * a + k], rsem.at[7 * a + k], p).wait_recv()
        cps, loc = self._copies(ins, outs, sems)
        for cp in cps:
            cp.wait_send()
        for cp in loc:
            cp.wait()


def _call(core, *, name, grid, in_specs, out_specs, out_shape, args, scratch=(), jobs=(), core_starts=False):
    n_in, n_out, n_sc = len(in_specs), len(out_specs), len(scratch)
    steps = 1
    for g in grid:
        steps *= g

    def body(*refs):
        pos = [0]

        def take(k):
            r = refs[pos[0]:pos[0] + k]
            pos[0] += k
            return r

        ins = take(n_in)
        j_ins = [take(len(j.ins)) for j in jobs]
        outs = take(n_out)
        j_outs = [take(len(j.out_shape)) for j in jobs]
        scs = take(n_sc)
        j_sems = [take(len(j.sems)) for j in jobs]
        if len(grid) == 2:
            step = pl.program_id(0) * grid[1] + pl.program_id(1)
        elif len(grid) == 1:
            step = pl.program_id(0)
        else:
            step = 0
        def start_jobs():
            for j, ji, jo, js in zip(jobs, j_ins, j_outs, j_sems):
                j.start(ji, jo, js)

        if grid:
            pl.when(step == 0)(start_jobs)
        elif not core_starts:
            start_jobs()
        for j, ji, jo, js in zip(jobs, j_ins, j_outs, j_sems):
            if j.mid is not None and grid:
                at = max(steps - 2, 0) if j.late_mid else (3 * steps) // 4
                pl.when(step == at)(lambda j=j, ji=ji, jo=jo, js=js: j.mid(ji, jo, js))
        if core_starts:
            core(ins, outs, scs, start_jobs)
        elif core is not None:
            core(ins, outs, scs)
        for j, ji, jo, js in zip(jobs, j_ins, j_outs, j_sems):
            if grid:
                pl.when(step == steps - 1)(lambda j=j, ji=ji, jo=jo, js=js: j.end(ji, jo, js))
            else:
                if j.mid is not None:
                    j.mid(ji, jo, js)
                j.end(ji, jo, js)

    all_in = list(in_specs)
    all_args = list(args)
    all_out = list(out_specs)
    all_shape = list(out_shape)
    all_sc = list(scratch)
    for j in jobs:
        all_in += [HBM] * len(j.ins)
        all_args += j.ins
    for j in jobs:
        all_out += [HBM] * len(j.out_shape)
        all_shape += j.out_shape
        all_sc += j.sems
    params = dict(vmem_limit_bytes=VMEM_LIMIT)
    if grid:
        params["dimension_semantics"] = ("arbitrary",) * len(grid)
    res = pl.pallas_call(
        body, name=name, grid=grid, in_specs=all_in, out_specs=all_out, out_shape=all_shape,
        scratch_shapes=all_sc, compiler_params=pltpu.CompilerParams(**params),
    )(*all_args)
    core_res = list(res[:n_out])
    job_res = []
    pos = n_out
    for j in jobs:
        job_res.append(list(res[pos:pos + len(j.out_shape)]))
        pos += len(j.out_shape)
    return core_res, job_res


def _ffn_fwd(x, mod, gvec, w_in, w_out, tm, name, jobs=()):
    T = x.shape[0]
    nt = T // tm
    tps = nt // mod.shape[0]

    def core(ins, outs, _):
        x_ref, mod_ref, g_ref, win_ref, wout_ref = ins
        xo_ref, gu_ref, y_ref = outs
        xv = x_ref[...]
        sh, sc, gt = mod_ref[0:1, :], mod_ref[1:2, :], mod_ref[2:3, :]
        r = lax.rsqrt(_rowmean(xv * xv) + EPS)
        h = (xv * r * g_ref[0:1, :]) * (1.0 + sc) + sh
        hb = h.astype(BF16)
        y = jnp.zeros((tm, D), F32)
        for cidx in range(NCH):
            gate = _dot_nt(hb, win_ref[cidx])
            up = _dot_nt(hb, win_ref[NCH + cidx])
            gu_ref[cidx] = gate.astype(BF16)
            gu_ref[NCH + cidx] = up.astype(BF16)
            act = gate * _sigmoid(gate) * up
            y = y + _dot(act.astype(BF16), wout_ref[cidx])
        y_ref[...] = y
        ry = lax.rsqrt(_rowmean(y * y) + EPS)
        xo_ref[...] = xv + (HALF * gt) * (y * ry * g_ref[1:2, :])

    tile = pl.BlockSpec((tm, D), lambda i: (i, 0))
    return _call(
        core, name=name, grid=(nt,), jobs=jobs,
        in_specs=[tile, pl.BlockSpec((None, 8, D), lambda i: (i // tps, 0, 0)), _const_spec((8, D)),
                  _const_spec((8, FB, D)), _const_spec((4, FB, D))],
        out_specs=[tile, pl.BlockSpec((8, tm, FB), lambda i: (0, i, 0)), tile],
        out_shape=[jax.ShapeDtypeStruct((T, D), F32), jax.ShapeDtypeStruct((8, T, FB), BF16),
                   jax.ShapeDtypeStruct((T, D), F32)],
        args=[x, mod, gvec, w_in, w_out])


def _ffn_bwd(dxo, x, y, gu, mod, gvec, w_in, w_out, tm, name, jobs=()):
    T = x.shape[0]
    nt = T // tm
    nb = mod.shape[0]
    tps = nt // nb

    def core(ins, outs, _):
        dxo_ref, x_ref, y_ref, gu_ref, mod_ref, g_ref, win_ref, wout_ref = ins
        dx_ref, dg_ref, act_ref, hb_ref, dyb_ref, mg_ref, vg_ref = outs
        i = pl.program_id(0)
        xv = x_ref[...]
        dxo_v = dxo_ref[...]
        yv = y_ref[...]
        sh, sc, gt = mod_ref[0:1, :], mod_ref[1:2, :], mod_ref[2:3, :]
        gpre, gpost = g_ref[0:1, :], g_ref[1:2, :]
        r = lax.rsqrt(_rowmean(xv * xv) + EPS)
        xh = xv * r
        n = xh * gpre
        hb = (n * (1.0 + sc) + sh).astype(BF16)
        hb_ref[...] = hb
        ry = lax.rsqrt(_rowmean(yv * yv) + EPS)
        yh = yv * ry
        d_gt = _colsum(HALF * dxo_v * (yh * gpost))
        dp = (HALF * gt) * dxo_v
        d_gpost = _colsum(dp * yh)
        dyh = dp * gpost
        dy = ry * (dyh - yh * _rowmean(dyh * yh))
        dyb = dy.astype(BF16)
        dyb_ref[...] = dyb
        dh = jnp.zeros((tm, D), F32)
        for cidx in range(NCH):
            gate = gu_ref[cidx].astype(F32)
            up = gu_ref[NCH + cidx].astype(F32)
            sig = _sigmoid(gate)
            s = gate * sig
            act_ref[cidx] = (s * up).astype(BF16)
            d_act = _dot_nt(dyb, wout_ref[cidx])
            d_up = (d_act * s).astype(BF16)
            d_gate = (d_act * up * (sig * (1.0 + gate * (1.0 - sig)))).astype(BF16)
            dg_ref[cidx] = d_gate
            dg_ref[NCH + cidx] = d_up
            dh = dh + _dot(d_gate, win_ref[cidx]) + _dot(d_up, win_ref[NCH + cidx])
        d_sc = _colsum(dh * n)
        d_sh = _colsum(dh)
        dn = dh * (1.0 + sc)
        d_gpre = _colsum(dn * xh)
        dxh = dn * gpre
        dx_ref[...] = dxo_v + r * (dxh - xh * _rowmean(dxh * xh))

        @pl.when(i % tps == 0)
        def _():
            mg_ref[...] = jnp.zeros((8, D), F32)

        @pl.when(i == 0)
        def _():
            vg_ref[...] = jnp.zeros((8, D), F32)

        mg_ref[0:1, :] += d_sh
        mg_ref[1:2, :] += d_sc
        mg_ref[2:3, :] += d_gt
        vg_ref[0:1, :] += d_gpre
        vg_ref[1:2, :] += d_gpost

    tile = pl.BlockSpec((tm, D), lambda i: (i, 0))
    return _call(
        core, name=name, grid=(nt,), jobs=jobs,
        in_specs=[tile, tile, tile, pl.BlockSpec((8, tm, FB), lambda i: (0, i, 0)),
                  pl.BlockSpec((None, 8, D), lambda i: (i // tps, 0, 0)), _const_spec((8, D)),
                  _const_spec((8, FB, D)), _const_spec((4, FB, D))],
        out_specs=[tile, pl.BlockSpec((8, tm, FB), lambda i: (0, i, 0)),
                   pl.BlockSpec((4, tm, FB), lambda i: (0, i, 0)), tile, tile,
                   pl.BlockSpec((None, 8, D), lambda i: (i // tps, 0, 0)), pl.BlockSpec((8, D), lambda i: (0, 0))],
        out_shape=[jax.ShapeDtypeStruct((T, D), F32), jax.ShapeDtypeStruct((8, T, FB), BF16),
                   jax.ShapeDtypeStruct((4, T, FB), BF16), jax.ShapeDtypeStruct((T, D), BF16),
                   jax.ShapeDtypeStruct((T, D), BF16), jax.ShapeDtypeStruct((nb, 8, D), F32),
                   jax.ShapeDtypeStruct((8, D), F32)],
        args=[dxo, x, y, gu, mod, gvec, w_in, w_out])


def _ffn_last(x, target, mod, gvec, w_in, w_out, tm, name, jobs=()):
    T = x.shape[0]
    nt = T // tm
    nb = mod.shape[0]
    tps = nt // nb

    def core(ins, outs, scs):
        x_ref, t_ref, mod_ref, g_ref, wina_ref, winb_ref, wout_ref = ins
        dx_ref, dg_ref, act_ref, hb_ref, dyb_ref, mg_ref, vg_ref, loss_ref = outs
        hd2 = w_in[0].shape[2]
        (gu_s,) = scs
        i = pl.program_id(0)
        xv = x_ref[...]
        sh, sc, gt = mod_ref[0:1, :], mod_ref[1:2, :], mod_ref[2:3, :]
        gpre, gpost = g_ref[0:1, :], g_ref[1:2, :]
        r = lax.rsqrt(_rowmean(xv * xv) + EPS)
        xh = xv * r
        n = xh * gpre
        hb = (n * (1.0 + sc) + sh).astype(BF16)
        hb_ref[...] = hb
        hba, hbb = hb[:, 0:hd2], hb[:, hd2:D]
        yv = jnp.zeros((tm, D), F32)
        for cidx in range(NCH):
            gate = _dot_nt(hba, wina_ref[cidx]) + _dot_nt(hbb, winb_ref[cidx])
            up = _dot_nt(hba, wina_ref[NCH + cidx]) + _dot_nt(hbb, winb_ref[NCH + cidx])
            gu_s[cidx] = gate.astype(BF16)
            gu_s[NCH + cidx] = up.astype(BF16)
            act = gate * _sigmoid(gate) * up
            act_ref[cidx] = act.astype(BF16)
            yv = yv + _dot(act_ref[cidx], wout_ref[cidx])
        ry = lax.rsqrt(_rowmean(yv * yv) + EPS)
        yh = yv * ry
        pn = yh * gpost
        err = xv + (HALF * gt) * pn - t_ref[...]
        dxo_v = err * (1.0 / D)
        d_gt = _colsum(HALF * dxo_v * pn)
        dp = (HALF * gt) * dxo_v
        d_gpost = _colsum(dp * yh)
        dyh = dp * gpost
        dyb = (ry * (dyh - yh * _rowmean(dyh * yh))).astype(BF16)
        dyb_ref[...] = dyb
        dha = jnp.zeros((tm, hd2), F32)
        dhb = jnp.zeros((tm, D - hd2), F32)
        for cidx in range(NCH):
            gate = gu_s[cidx].astype(F32)
            up = gu_s[NCH + cidx].astype(F32)
            sig = _sigmoid(gate)
            s = gate * sig
            d_act = _dot_nt(dyb, wout_ref[cidx])
            d_up = (d_act * s).astype(BF16)
            d_gate = (d_act * up * (sig * (1.0 + gate * (1.0 - sig)))).astype(BF16)
            dg_ref[cidx] = d_gate
            dg_ref[NCH + cidx] = d_up
            dha = dha + _dot(d_gate, wina_ref[cidx]) + _dot(d_up, wina_ref[NCH + cidx])
            dhb = dhb + _dot(d_gate, winb_ref[cidx]) + _dot(d_up, winb_ref[NCH + cidx])
        dh = jnp.concatenate([dha, dhb], axis=1)
        d_sc = _colsum(dh * n)
        d_sh = _colsum(dh)
        dn = dh * (1.0 + sc)
        d_gpre = _colsum(dn * xh)
        dxh = dn * gpre
        dx_ref[...] = dxo_v + r * (dxh - xh * _rowmean(dxh * xh))

        @pl.when(i % tps == 0)
        def _():
            mg_ref[...] = jnp.zeros((8, D), F32)

        @pl.when(i == 0)
        def _():
            vg_ref[...] = jnp.zeros((8, D), F32)
            loss_ref[...] = jnp.zeros((8, D), F32)

        mg_ref[0:1, :] += d_sh
        mg_ref[1:2, :] += d_sc
        mg_ref[2:3, :] += d_gt
        vg_ref[0:1, :] += d_gpre
        vg_ref[1:2, :] += d_gpost
        loss_ref[...] += HALF * jnp.sum(_rowmean(err * err), axis=0, keepdims=True)

    tile = pl.BlockSpec((tm, D), lambda i: (i, 0))
    return _call(
        core, name=name, grid=(nt,), jobs=jobs,
        in_specs=[tile, tile, pl.BlockSpec((None, 8, D), lambda i: (i // tps, 0, 0)), _const_spec((8, D)),
                  _const_spec(w_in[0].shape), _const_spec(w_in[1].shape), _const_spec((4, FB, D))],
        out_specs=[tile, pl.BlockSpec((8, tm, FB), lambda i: (0, i, 0)),
                   pl.BlockSpec((4, tm, FB), lambda i: (0, i, 0)), tile, tile,
                   pl.BlockSpec((None, 8, D), lambda i: (i // tps, 0, 0)), pl.BlockSpec((8, D), lambda i: (0, 0)),
                   pl.BlockSpec((8, D), lambda i: (0, 0))],
        out_shape=[jax.ShapeDtypeStruct((T, D), F32), jax.ShapeDtypeStruct((8, T, FB), BF16),
                   jax.ShapeDtypeStruct((4, T, FB), BF16), jax.ShapeDtypeStruct((T, D), BF16),
                   jax.ShapeDtypeStruct((T, D), BF16), jax.ShapeDtypeStruct((nb, 8, D), F32),
                   jax.ShapeDtypeStruct((8, D), F32), jax.ShapeDtypeStruct((8, D), F32)],
        scratch=[pltpu.VMEM((8, tm, FB), BF16)],
        args=[x, target, mod, gvec, w_in[0], w_in[1], w_out])


def _masked_spatial(ws_ref):
    row = lax.broadcasted_iota(jnp.int32, (CHUNK, CHUNK), 0)
    col = lax.broadcasted_iota(jnp.int32, (CHUNK, CHUNK), 1)
    keep = col <= row
    return [jnp.where(keep, ws_ref[hd], 0.0).astype(BF16) for hd in range(NHEAD)]


def _head_pairs(mats, right, transpose=False):
    first = lax.broadcasted_iota(jnp.int32, (CHUNK, LANES), 1) < HD
    op = _dot_tn if transpose else _dot
    out = []
    for p in range(NHEAD // 2):
        slab = right[:, _lanes(p)]
        out.append(jnp.where(first, op(mats[2 * p], slab), op(mats[2 * p + 1], slab)))
    return jnp.concatenate(out, axis=1)


def _spatial_gate(wm, vb_chunk):
    return _head_pairs(wm, vb_chunk)


def _layer_norm_stats(v):
    mu = _rowmean(v)
    vc = v - mu
    rstd = lax.rsqrt(_rowmean(vc * vc) + EPS)
    return vc * rstd, rstd


def _pitch(tm):
    p = tm // 8
    while p % 8 != 4:
        p += 1
    return p


def _lanes(s):
    return slice(s * LANES, (s + 1) * LANES)


def _to_slabs(ref, row0, val):
    for s in range(NSLAB):
        ref[s, row0:row0 + val.shape[0], :] = val[:, _lanes(s)]


def _tap_sum(src, out, cw_ref, bias, tm, start):
    p = _pitch(tm)
    for s in range(NSLAB):
        accs = [jnp.broadcast_to(bias[:, _lanes(s)], (SUBL, LANES))] * p
        for k in range(CONV_K):
            w = jnp.broadcast_to(cw_ref[k:k + 1, _lanes(s)], (SUBL, LANES))
            for v in range(p):
                accs[v] = accs[v] + w * src[s, pl.ds(v + start(k), 8, stride=p), :]
        for v in range(p):
            out[s, pl.ds(v, 8, stride=p), :] = accs[v]
    return jnp.concatenate([out[s, 0:tm, :] for s in range(NSLAB)], axis=1)


def _mixer_fwd(x, mod, gvec, w_mi, w_mo, v512, ws, bias_full, cw, tm, name, jobs=()):
    T = x.shape[0]
    nt = T // tm
    tps = nt // mod.shape[0]
    ext_rows = 8 * _pitch(tm)

    def core(ins, outs, scs):
        x_ref, mod_ref, g_ref, wmi_ref, wmo_ref, v_ref, ws_ref, bias_ref, cw_ref = ins
        xo_ref, proj_ref, ym_ref, conv_ref = outs
        glu_ext, conv_scr = scs
        i = pl.program_id(0)
        xv = x_ref[...]
        sh, sc, gt = mod_ref[0:1, :], mod_ref[1:2, :], mod_ref[2:3, :]
        r = lax.rsqrt(_rowmean(xv * xv) + EPS)
        hb = ((xv * r * g_ref[0:1, :]) * (1.0 + sc) + sh).astype(BF16)
        for j in range(NDEV):
            proj_ref[:, j * MB:(j + 1) * MB] = _dot(hb, wmi_ref[j])
        u = proj_ref[:, 0:WA]
        v0 = proj_ref[:, WA:2 * WA]
        a = proj_ref[:, 2 * WA:3 * WA]
        g = proj_ref[:, 3 * WA:4 * WA]
        vh, _ = _layer_norm_stats(v0)
        vb = (vh * v_ref[0:1, :] + v_ref[1:2, :]).astype(BF16)
        wm = _masked_spatial(ws_ref)
        ya = []
        for q in range(tm // CHUNK):
            z = _spatial_gate(wm, vb[q * CHUNK:(q + 1) * CHUNK, :]) + bias_ref[...]
            ya.append(u[q * CHUNK:(q + 1) * CHUNK, :] * z)
        ya = jnp.concatenate(ya, axis=0)
        glu = a * _sigmoid(g)

        @pl.when(i == 0)
        def _():
            glu_ext[:, HALO + tm:HALO + ext_rows, :] = jnp.zeros((NSLAB, ext_rows - tm, LANES), F32)

        @pl.when(i % tps == 0)
        def _():
            glu_ext[:, 0:HALO, :] = jnp.zeros((NSLAB, HALO, LANES), F32)

        _to_slabs(glu_ext, HALO, glu)
        conv = _tap_sum(glu_ext, conv_scr, cw_ref, v_ref[2:3, :], tm, lambda k: HALO - (CONV_K - 1) + k)
        conv_ref[...] = conv
        glu_ext[:, 0:HALO, :] = glu_ext[:, tm:tm + HALO, :]
        ch, _ = _layer_norm_stats(conv)
        cn = ch * v_ref[3:4, :] + v_ref[4:5, :]
        yb = cn * _sigmoid(cn)
        pa = ya * lax.rsqrt(_rowmean(ya * ya) + EPS) * v_ref[5:6, :]
        pb = yb * lax.rsqrt(_rowmean(yb * yb) + EPS) * v_ref[6:7, :]
        ycat = jnp.concatenate([pa, pb], axis=1).astype(BF16)
        ym = _dot(ycat, wmo_ref[...])
        ym_ref[...] = ym
        rm = lax.rsqrt(_rowmean(ym * ym) + EPS)
        xo_ref[...] = xv + gt * (ym * rm * g_ref[1:2, :])

    tile = pl.BlockSpec((tm, D), lambda i: (i, 0))
    return _call(
        core, name=name, grid=(nt,), jobs=jobs,
        in_specs=[tile, pl.BlockSpec((None, 8, D), lambda i: (i // tps, 0, 0)), _const_spec((8, D)),
                  _const_spec((NDEV, D, MB)), _const_spec((D, D)), _const_spec((8, WA)),
                  _const_spec((NHEAD, CHUNK, CHUNK)), _const_spec((CHUNK, WA)), _const_spec((32, WA))],
        out_specs=[tile, pl.BlockSpec((tm, 4 * WA), lambda i: (i, 0)), tile, pl.BlockSpec((tm, WA), lambda i: (i, 0))],
        out_shape=[jax.ShapeDtypeStruct((T, D), F32), jax.ShapeDtypeStruct((T, 4 * WA), F32),
                   jax.ShapeDtypeStruct((T, D), F32), jax.ShapeDtypeStruct((T, WA), F32)],
        scratch=[pltpu.VMEM((NSLAB, HALO + ext_rows, LANES), F32), pltpu.VMEM((NSLAB, ext_rows, LANES), F32)],
        args=[x, mod, gvec, w_mi, w_mo, v512, ws, bias_full, cw])


def _mixer_bwd_a(dxo, ym, proj, conv, mod, gvec, w_mo, v512, ws, bias_full, esel, tm, name, jobs=()):
    T = dxo.shape[0]
    nt = T // tm
    nb = mod.shape[0]
    tps = nt // nb

    def core(ins, outs, scs):
        dxo_ref, ym_ref, proj_ref, conv_ref, mod_ref, g_ref, wmo_ref, v_ref, ws_ref, bias_ref, e_ref = ins
        dpart_ref, dymb_ref, ycat_ref, mg_ref, vg_ref, v5g_ref, gws_ref, gbs_ref = outs
        (dbs_acc,) = scs
        i = pl.program_id(0)
        dxo_v = dxo_ref[...]
        ymv = ym_ref[...]
        gt = mod_ref[2:3, :]
        gpost = g_ref[1:2, :]
        rm = lax.rsqrt(_rowmean(ymv * ymv) + EPS)
        ymh = ymv * rm
        d_gt = _colsum(dxo_v * (ymh * gpost))
        dpm = gt * dxo_v
        d_gpost = _colsum(dpm * ymh)
        dymh = dpm * gpost
        dym = (rm * (dymh - ymh * _rowmean(dymh * ymh))).astype(BF16)
        dymb_ref[...] = dym
        dycat = _dot_nt(dym, wmo_ref[...])
        u = proj_ref[:, 0:WA]
        v0 = proj_ref[:, WA:2 * WA]
        vh, rv = _layer_norm_stats(v0)
        vb = (vh * v_ref[0:1, :] + v_ref[1:2, :]).astype(BF16)
        wm = _masked_spatial(ws_ref)
        zs = []
        for q in range(tm // CHUNK):
            zs.append(_spatial_gate(wm, vb[q * CHUNK:(q + 1) * CHUNK, :]) + bias_ref[...])
        z = jnp.concatenate(zs, axis=0)
        ya = u * z
        ra = lax.rsqrt(_rowmean(ya * ya) + EPS)
        yah = ya * ra
        ch, rc = _layer_norm_stats(conv_ref[...])
        cn = ch * v_ref[3:4, :] + v_ref[4:5, :]
        sg = _sigmoid(cn)
        yb = cn * sg
        rb = lax.rsqrt(_rowmean(yb * yb) + EPS)
        ybh = yb * rb
        ycat_ref[...] = jnp.concatenate([yah * v_ref[5:6, :], ybh * v_ref[6:7, :]], axis=1).astype(BF16)
        dpa = dycat[:, 0:WA]
        dpb = dycat[:, WA:2 * WA]
        d_goa = _colsum(dpa * yah)
        d_gob = _colsum(dpb * ybh)
        dyah = dpa * v_ref[5:6, :]
        dybh = dpb * v_ref[6:7, :]
        dya = ra * (dyah - yah * _rowmean(dyah * yah))
        dyb = rb * (dybh - ybh * _rowmean(dybh * ybh))
        dpart_ref[:, 0:WA] = dya * z
        dz = dya * u

        @pl.when(i == 0)
        def _():
            gws_ref[...] = jnp.zeros((NHEAD, CHUNK, CHUNK), F32)
            dbs_acc[...] = jnp.zeros((CHUNK, WA), F32)
            vg_ref[...] = jnp.zeros((8, D), F32)
            v5g_ref[...] = jnp.zeros((8, WA), F32)

        first = lax.broadcasted_iota(jnp.int32, (CHUNK, LANES), 1) < HD
        dvs = []
        for q in range(tm // CHUNK):
            dz_q = dz[q * CHUNK:(q + 1) * CHUNK, :]
            vb_q = vb[q * CHUNK:(q + 1) * CHUNK, :]
            dbs_acc[...] += dz_q
            dzb = dz_q.astype(BF16)
            dvs.append(_head_pairs(wm, dzb, transpose=True))
            for hd in range(NHEAD):
                slab = dzb[:, _lanes(hd // 2)]
                dz_hd = jnp.where(first if hd % 2 == 0 else jnp.logical_not(first), slab, jnp.zeros_like(slab))
                gws_ref[hd] += _dot_nt(dz_hd, vb_q[:, _lanes(hd // 2)])
        dv = jnp.concatenate(dvs, axis=0)
        d_gng = _colsum(dv * vh)
        d_gnb = _colsum(dv)
        dvh = dv * v_ref[0:1, :]
        dpart_ref[:, WA:2 * WA] = rv * (dvh - _rowmean(dvh) - vh * _rowmean(dvh * vh))
        dcn = dyb * (sg * (1.0 + cn * (1.0 - sg)))
        d_cng = _colsum(dcn * ch)
        d_cnb = _colsum(dcn)
        dch = dcn * v_ref[3:4, :]
        dconv = rc * (dch - _rowmean(dch) - ch * _rowmean(dch * ch))
        dpart_ref[:, 2 * WA:3 * WA] = dconv
        dpart_ref[:, 3 * WA:4 * WA] = jnp.zeros((tm, WA), F32)
        d_cb = _colsum(dconv)

        @pl.when(i % tps == 0)
        def _():
            mg_ref[...] = jnp.zeros((8, D), F32)

        mg_ref[2:3, :] += d_gt
        vg_ref[1:2, :] += d_gpost
        v5g_ref[0:1, :] += d_gng
        v5g_ref[1:2, :] += d_gnb
        v5g_ref[2:3, :] += d_cb
        v5g_ref[3:4, :] += d_cng
        v5g_ref[4:5, :] += d_cnb
        v5g_ref[5:6, :] += d_goa
        v5g_ref[6:7, :] += d_gob

        @pl.when(i == nt - 1)
        def _():
            row = lax.broadcasted_iota(jnp.int32, (CHUNK, CHUNK), 0)
            col = lax.broadcasted_iota(jnp.int32, (CHUNK, CHUNK), 1)
            for hd in range(NHEAD):
                gws_ref[hd] = jnp.where(col <= row, gws_ref[hd], 0.0)
            gbs_ref[...] = lax.dot_general(e_ref[...], dbs_acc[...], (((1,), (1,)), ((), ())),
                                           precision=lax.Precision.HIGHEST, preferred_element_type=F32)

    tile = pl.BlockSpec((tm, D), lambda i: (i, 0))
    ptile = pl.BlockSpec((tm, 4 * WA), lambda i: (i, 0))
    return _call(
        core, name=name, grid=(nt,), jobs=jobs,
        in_specs=[tile, tile, pl.BlockSpec((tm, 2 * WA), lambda i: (i, 0)), pl.BlockSpec((tm, WA), lambda i: (i, 0)),
                  pl.BlockSpec((None, 8, D), lambda i: (i // tps, 0, 0)), _const_spec((8, D)), _const_spec((D, D)),
                  _const_spec((8, WA)), _const_spec((NHEAD, CHUNK, CHUNK)), _const_spec((CHUNK, WA)),
                  _const_spec((8, WA))],
        out_specs=[ptile, tile, tile, pl.BlockSpec((None, 8, D), lambda i: (i // tps, 0, 0)),
                   pl.BlockSpec((8, D), lambda i: (0, 0)), pl.BlockSpec((8, WA), lambda i: (0, 0)),
                   pl.BlockSpec((NHEAD, CHUNK, CHUNK), lambda i: (0, 0, 0)), pl.BlockSpec((8, CHUNK), lambda i: (0, 0))],
        out_shape=[jax.ShapeDtypeStruct((T, 4 * WA), F32), jax.ShapeDtypeStruct((T, D), BF16),
                   jax.ShapeDtypeStruct((T, D), BF16), jax.ShapeDtypeStruct((nb, 8, D), F32),
                   jax.ShapeDtypeStruct((8, D), F32), jax.ShapeDtypeStruct((8, WA), F32),
                   jax.ShapeDtypeStruct((NHEAD, CHUNK, CHUNK), F32), jax.ShapeDtypeStruct((8, CHUNK), F32)],
        scratch=[pltpu.VMEM((CHUNK, WA), F32)],
        args=[dxo, ym, proj, conv, mod, gvec, w_mo, v512, ws, bias_full, esel])


def _mixer_bwd_b(dxo, x, dpart, proj, mod, gvec, w_mi, cw, tm, name, jobs=()):
    T = x.shape[0]
    nt = T // tm
    nb = mod.shape[0]
    tps = nt // nb
    hpt = tm // HALO
    nh = T // HALO
    off = HALO - (CONV_K - 1)
    p = _pitch(tm)
    ext_rows = 8 * p

    def core(ins, outs, scs):
        dxo_ref, x_ref, dpart_ref, dnext_ref, ag_ref, halo_ref, mod_ref, g_ref, wmi_ref, cw_ref = ins
        dx_ref, dproj_ref, hb_ref, mg_ref, vg_ref, dcw_ref = outs
        glu_ext, dconv_ext, dglu_scr, dcw_acc = scs
        i = pl.program_id(0)
        first = i % tps == 0
        last = i % tps == tps - 1
        a = ag_ref[:, 0:WA]
        g = ag_ref[:, WA:2 * WA]
        sgg = _sigmoid(g)

        @pl.when(i == 0)
        def _():
            glu_ext[:, HALO + tm:HALO + ext_rows, :] = jnp.zeros((NSLAB, ext_rows - tm, LANES), F32)
            dconv_ext[:, HALO + tm:HALO + ext_rows, :] = jnp.zeros((NSLAB, ext_rows - tm, LANES), F32)
            dcw_acc[...] = jnp.zeros((32, 8, WA), F32)
            vg_ref[...] = jnp.zeros((8, D), F32)

        _to_slabs(glu_ext, 0, jnp.where(first, 0.0, halo_ref[:, 0:WA] * _sigmoid(halo_ref[:, WA:2 * WA])))
        _to_slabs(glu_ext, HALO, a * sgg)
        _to_slabs(dconv_ext, 0, dpart_ref[:, 2 * WA:3 * WA])
        _to_slabs(dconv_ext, tm, jnp.where(last, 0.0, dnext_ref[...]))
        sub = lax.broadcasted_iota(jnp.int32, (SUBL, LANES), 0)
        for s in range(NSLAB):
            accs = [jnp.zeros((SUBL, LANES), F32)] * CONV_K
            for v in range(p):
                dc = jnp.where(v + p * sub < tm, dconv_ext[s, pl.ds(v, 8, stride=p), :], 0.0)
                for k in range(CONV_K):
                    accs[k] = accs[k] + dc * glu_ext[s, pl.ds(v + off + k, 8, stride=p), :]
            for k in range(CONV_K):
                dcw_acc[k, :, _lanes(s)] += accs[k]
        dglu = _tap_sum(dconv_ext, dglu_scr, cw_ref, jnp.zeros((1, WA), F32), tm, lambda k: (CONV_K - 1) - k)

        @pl.when(i == nt - 1)
        def _():
            for k in range(CONV_K):
                dcw_ref[k:k + 1, :] = jnp.sum(dcw_acc[k], axis=0, keepdims=True)
            dcw_ref[CONV_K:32, :] = jnp.zeros((32 - CONV_K, WA), F32)

        da = dglu * sgg
        dgg = dglu * a * (sgg * (1.0 - sgg))
        dproj_ref[:, 0:2 * WA] = dpart_ref[:, 0:2 * WA].astype(BF16)
        dproj_ref[:, 2 * WA:3 * WA] = da.astype(BF16)
        dproj_ref[:, 3 * WA:4 * WA] = dgg.astype(BF16)
        dh = jnp.zeros((tm, D), F32)
        for j in range(NDEV):
            dh = dh + _dot_nt(dproj_ref[:, j * MB:(j + 1) * MB], wmi_ref[j])
        xv = x_ref[...]
        sc, sh = mod_ref[1:2, :], mod_ref[0:1, :]
        gpre = g_ref[0:1, :]
        r = lax.rsqrt(_rowmean(xv * xv) + EPS)
        xh = xv * r
        n = xh * gpre
        hb_ref[...] = (n * (1.0 + sc) + sh).astype(BF16)
        d_sc = _colsum(dh * n)
        d_sh = _colsum(dh)
        dn = dh * (1.0 + sc)
        d_gpre = _colsum(dn * xh)
        dxh = dn * gpre
        dx_ref[...] = dxo_ref[...] + r * (dxh - xh * _rowmean(dxh * xh))

        @pl.when(first)
        def _():
            mg_ref[...] = jnp.zeros((8, D), F32)

        mg_ref[0:1, :] += d_sh
        mg_ref[1:2, :] += d_sc
        vg_ref[0:1, :] += d_gpre

    tile = pl.BlockSpec((tm, D), lambda i: (i, 0))
    return _call(
        core, name=name, grid=(nt,), jobs=jobs,
        in_specs=[tile, tile, pl.BlockSpec((tm, 4 * WA), lambda i: (i, 0)),
                  pl.BlockSpec((HALO, WA), lambda i: (jnp.minimum((i + 1) * hpt, nh - 1), 2)),
                  pl.BlockSpec((tm, 2 * WA), lambda i: (i, 1)),
                  pl.BlockSpec((HALO, 2 * WA), lambda i: (jnp.maximum(i * hpt - 1, 0), 1)),
                  pl.BlockSpec((None, 8, D), lambda i: (i // tps, 0, 0)), _const_spec((8, D)),
                  _const_spec((NDEV, D, MB)), _const_spec((32, WA))],
        out_specs=[tile, pl.BlockSpec((tm, 4 * WA), lambda i: (i, 0)), tile,
                   pl.BlockSpec((None, 8, D), lambda i: (i // tps, 0, 0)), pl.BlockSpec((8, D), lambda i: (0, 0)),
                   pl.BlockSpec((32, WA), lambda i: (0, 0))],
        out_shape=[jax.ShapeDtypeStruct((T, D), F32), jax.ShapeDtypeStruct((T, 4 * WA), BF16),
                   jax.ShapeDtypeStruct((T, D), BF16), jax.ShapeDtypeStruct((nb, 8, D), F32),
                   jax.ShapeDtypeStruct((8, D), F32), jax.ShapeDtypeStruct((32, WA), F32)],
        scratch=[pltpu.VMEM((NSLAB, HALO + ext_rows, LANES), F32), pltpu.VMEM((NSLAB, HALO + ext_rows, LANES), F32),
                 pltpu.VMEM((NSLAB, ext_rows, LANES), F32), pltpu.VMEM((32, 8, WA), F32)],
        args=[dxo, x, dpart, dpart, proj, proj, mod, gvec, w_mi, cw])


def _grad_chip(a, b, a_spec, b_spec, prod_shape, half, name, jobs=(), via_b=False):
    steps = 8 if half is None else 4
    R = prod_shape[0] if half is None else half
    C = prod_shape[1]

    def core(ins, outs, scs):
        a_ref, b_ref = ins
        (o_ref,) = outs
        own, snd, rcv, ssem, rsem, lsem = scs
        s = pl.program_id(0)
        c = lax.axis_index("c")
        me = _me()
        sib = _flip(me, (0, 0, 1))
        if via_b:
            prod = _dot_tn(b_ref[...], a_ref[...]).T.astype(BF16)
        else:
            prod = _dot_tn(a_ref[...], b_ref[...]).astype(BF16)
        if half is None:
            q = s // 2

            @pl.when(s % 2 == c)
            def _():
                own[q] = prod

            @pl.when(s % 2 != c)
            def _():
                snd[q] = prod
                _remote(snd.at[q], rcv.at[q], ssem.at[q], rsem.at[q], sib).start()
        else:
            lo = prod[0:half, :]
            hi = prod[half:2 * half, :]
            own[s] = jnp.where(c == 0, lo, hi)
            snd[s] = jnp.where(c == 0, hi, lo)
            _remote(snd.at[s], rcv.at[s], ssem.at[s], rsem.at[s], sib).start()

        @pl.when(s == steps - 1)
        def _():
            for q4 in range(4):
                cp = _remote(snd.at[q4], rcv.at[q4], ssem.at[q4], rsem.at[q4], sib)
                cp.wait_recv()
                cp.wait_send()
                snd[q4] = (own[q4].astype(F32) + rcv[q4].astype(F32)).astype(BF16)
            out = pltpu.make_async_copy(snd, o_ref, lsem)
            out.start()
            out.wait()

    return _call(
        core, name=name, grid=(steps,), jobs=jobs, in_specs=[a_spec, b_spec], out_specs=[HBM],
        out_shape=[jax.ShapeDtypeStruct((4, R, C), BF16)],
        scratch=[pltpu.VMEM((4, R, C), BF16), pltpu.VMEM((4, R, C), BF16), pltpu.VMEM((4, R, C), BF16),
                 pltpu.SemaphoreType.DMA((4,)), pltpu.SemaphoreType.DMA((4,)), pltpu.SemaphoreType.DMA],
        args=[a, b])


def _grad_w_in(dg, hb, name, jobs=()):
    T = hb.shape[0]
    return _grad_chip(dg, hb, pl.BlockSpec((None, T, FB), lambda s: (s, 0, 0)), _const_spec((T, D)),
                      (FB, D), None, name, jobs)


def _grad_w_out(act, dyb, name, jobs=()):
    T = dyb.shape[0]
    return _grad_chip(act, dyb, pl.BlockSpec((None, T, FB), lambda s: (s, 0, 0)), _const_spec((T, D)),
                      (FB, D), FO, name, jobs)


def _grad_w_mi(hb, dproj, name, jobs=()):
    T = hb.shape[0]
    return _grad_chip(hb, dproj, _const_spec((T, D)), pl.BlockSpec((T, MB), lambda s: (0, s)),
                      (D, MB), None, name, jobs, via_b=True)


def _grad_w_mo(ycat, dym, name, jobs=()):
    T = ycat.shape[0]
    return _grad_chip(ycat, dym, pl.BlockSpec((T, 2 * MO), lambda s: (0, s)), _const_spec((T, D)),
                      (2 * MO, D), MO, name, jobs)


def _adamw_math(w, g, m, v):
    m2 = ADAM_B1 * m + (1.0 - ADAM_B1) * g
    v2 = ADAM_B2 * v + (1.0 - ADAM_B2) * (g * g)
    m_hat = m2 / (1.0 - ADAM_B1 ** ADAM_STEP)
    v_hat = v2 / (1.0 - ADAM_B2 ** ADAM_STEP)
    delta = -ADAM_LR * (m_hat / (jnp.sqrt(v_hat) + ADAM_EPS) + ADAM_WD * w)
    return delta, m2, v2


def _adamw_reduce(parts, w, m, v, tr, name, own=None, after=None):
    R, C = w.shape

    def core(ins, outs, _):
        p_ref, w_ref, m_ref, v_ref = ins[:4]
        g_ref, d_ref, m2_ref, v2_ref = outs
        if own is None:
            terms = [p_ref[s].astype(F32) for s in range(4)]
        else:
            mq = 2 * lax.axis_index("x") + lax.axis_index("y")
            mine = ins[4][...].astype(F32)
            terms = [jnp.where(mq == s, mine, p_ref[s].astype(F32)) for s in range(4)]
        g = terms[0]
        for s in range(1, 4):
            g = g + terms[s]
        g_ref[...] = g
        d_ref[...], m2_ref[...], v2_ref[...] = _adamw_math(w_ref[...], g, m_ref[...], v_ref[...])

    blk = pl.BlockSpec((tr, C), lambda i: (i, 0))
    in_specs = [pl.BlockSpec((4, tr, C), lambda i: (0, i, 0)), blk, blk, blk]
    args = [parts, w, m, v]
    if own is not None:
        mq = 2 * lax.axis_index("x") + lax.axis_index("y")
        in_specs.append(pl.BlockSpec((tr, C), lambda i: (i, 0)))
        args.append(lax.dynamic_index_in_dim(own, mq, 0, keepdims=False))
    if after is not None:
        in_specs.append(HBM)
        args.append(after)
    return _call(
        core, name=name, grid=(R // tr,), in_specs=in_specs,
        out_specs=[blk, blk, blk, blk], out_shape=[jax.ShapeDtypeStruct((R, C), F32)] * 4, args=args)[0]


HBM_ONLY = pl.BlockSpec(memory_space=pltpu.HBM)
SEM = pl.BlockSpec(memory_space=pltpu.SEMAPHORE)
EFFECT = pltpu.SideEffectType.DATAFLOW_SIDE_EFFECTING


def _chip_scatter_start(gs, name):
    n = len(gs)

    def body(*refs):
        g_refs, land_refs = refs[:n], refs[n:2 * n]
        ssem, rsem = refs[2 * n:2 * n + 2]
        token = refs[-1]
        me = _me()
        mq = 2 * me[0] + me[1]
        for k, f in enumerate(CHIP_FLIPS):
            p = _flip(me, f)
            for a in range(n):
                _remote(g_refs[a].at[2 * p[0] + p[1]], land_refs[a].at[mq], ssem.at[3 * a + k], rsem.at[3 * a + k], p).start()
        token[...] = jnp.zeros_like(token)

    gs = [pltpu.with_memory_space_constraint(g, pltpu.HBM) for g in gs]
    lands = [pltpu.with_memory_space_constraint(lax.empty(g.shape, g.dtype), pltpu.HBM) for g in gs]
    res = pl.pallas_call(
        body, name=name,
        out_shape=(pltpu.SemaphoreType.DMA((3 * n,)), pltpu.SemaphoreType.DMA((3 * n,)))
        + tuple(pltpu.HBM(g.shape, g.dtype) for g in gs) * 2 + (jax.ShapeDtypeStruct((SUBL, LANES), F32),),
        in_specs=(HBM_ONLY,) * (2 * n), out_specs=(SEM, SEM) + (HBM_ONLY,) * (2 * n) + (VM,),
        input_output_aliases={a: 2 + a for a in range(2 * n)},
        compiler_params=pltpu.CompilerParams(has_side_effects=EFFECT),
    )(*gs, *lands)
    return res[:-1], res[-1]


def _chip_scatter_wait(handle, after, name):
    ssem, rsem = handle[:2]
    n = (len(handle) - 2) // 2
    thru = handle[2:]

    def body(*refs):
        g_refs, land_refs = refs[:n], refs[n:2 * n]
        ssem, rsem = refs[2 * n:2 * n + 2]
        me = _me()
        mq = 2 * me[0] + me[1]
        for k, f in enumerate(CHIP_FLIPS):
            p = _flip(me, f)
            pq = 2 * p[0] + p[1]
            for a in range(n):
                _remote(g_refs[a].at[pq], land_refs[a].at[mq], ssem.at[3 * a + k], rsem.at[3 * a + k], p).wait_send()
                _remote(g_refs[a].at[mq], land_refs[a].at[pq], ssem.at[3 * a + k], rsem.at[3 * a + k], p).wait_recv()

    res = pl.pallas_call(
        body, name=name,
        out_shape=tuple(pltpu.HBM(t.shape, t.dtype) for t in thru),
        in_specs=(HBM_ONLY,) * (2 * n) + (SEM, SEM, HBM), out_specs=(HBM_ONLY,) * (2 * n),
        input_output_aliases={a: a for a in range(2 * n)},
        compiler_params=pltpu.CompilerParams(has_side_effects=EFFECT),
    )(*thru, ssem, rsem, after)
    return list(res[:n]), list(res[n:])


def _adamw_ada(sc_all, dd, w, m, v, tr, name, after=None):
    R, C = w.shape

    def core(ins, outs, _):
        sc_ref, dd_ref, w_ref, m_ref, v_ref = ins[:5]
        g_ref, d_ref, m2_ref, v2_ref = outs
        g = _dot_tn(sc_ref[...].astype(BF16), dd_ref[...].astype(BF16))
        g_ref[...] = g
        d_ref[...], m2_ref[...], v2_ref[...] = _adamw_math(w_ref[...], g, m_ref[...], v_ref[...])

    blk = pl.BlockSpec((tr, C), lambda i: (i, 0))
    return _call(
        core, name=name, grid=(R // tr,),
        in_specs=[pl.BlockSpec((64, tr), lambda i: (0, i)), pl.BlockSpec((64, C), lambda i: (0, 0)), blk, blk, blk]
        + [HBM] * (after is not None),
        out_specs=[blk, blk, blk, blk], out_shape=[jax.ShapeDtypeStruct((R, C), F32)] * 4,
        args=[sc_all, dd, w, m, v] + [after] * (after is not None))[0]


def _adamw_small(gathered, plain, grads, wmv, emit, name):
    nw = len(grads)
    ng, npl, ne = len(gathered), len(plain), len(emit)

    def core(ins, outs, _):
        srcs = []
        for a in range(ng):
            s = ins[a][0]
            for dev in range(1, NDEV):
                s = s + ins[a][dev]
            srcs.append(s)
        srcs += [ins[ng + a][...] for a in range(npl)]
        w_refs = ins[ng + npl:]
        for e, a in enumerate(emit):
            outs[e][...] = srcs[a]
        for t in range(nw):
            src, row = grads[t]
            g = srcs[src] if row is None else srcs[src][row:row + 1, :]
            w_ref, m_ref, v_ref = w_refs[3 * t:3 * t + 3]
            g_ref, d_ref, m2_ref, v2_ref = outs[ne + 4 * t:ne + 4 * t + 4]
            g_ref[...] = g
            d_ref[...], m2_ref[...], v2_ref[...] = _adamw_math(w_ref[...], g, m_ref[...], v_ref[...])

    out_shape = [jax.ShapeDtypeStruct(gathered[a].shape[1:], F32) for a in emit]
    for t in range(nw):
        out_shape += [jax.ShapeDtypeStruct(wmv[3 * t].shape, F32)] * 4
    return _call(
        core, name=name, grid=(), in_specs=[VM] * (ng + npl + 3 * nw), out_specs=[VM] * (ne + 4 * nw),
        out_shape=out_shape, args=list(gathered) + list(plain) + list(wmv))[0]


def _ada_fwd(c_pad, w_ada, b_cols, cw_pad, jobs=()):
    def core(ins, outs, scs, start_jobs):
        c_ref, w_ref, b_ref, cwp_ref = ins
        ada_ref, sc_ref, cw_ref = outs
        cbuf, send_buf, ssem, rsem = scs
        me = _me()
        mi = _lin(me)
        cbuf[mi] = c_ref[...]
        cw_ref[mi] = cwp_ref[...]
        peers = [_flip(me, f) for f in FLIPS]
        first = []
        for k, p in enumerate(peers):
            first.append(_remote(cbuf.at[mi], cbuf.at[mi], ssem.at[k], rsem.at[k], p))
            first.append(_remote(cw_ref.at[mi], cw_ref.at[mi], ssem.at[7 + k], rsem.at[7 + k], p))
        for cp in first:
            cp.start()
        for k, p in enumerate(peers):
            pi = _lin(p)
            _remote(cbuf.at[pi], cbuf.at[pi], ssem.at[k], rsem.at[k], p).wait_recv()
            _remote(cw_ref.at[pi], cw_ref.at[pi], ssem.at[7 + k], rsem.at[7 + k], p).wait_recv()
        c_all = cbuf[...].reshape(8 * 8, D)
        sc = c_all * _sigmoid(c_all)
        sc_ref[...] = sc
        res = _dot(sc.astype(BF16), w_ref[...].astype(BF16)) + b_ref[...]
        send_buf[...] = res.reshape(8, 8, ADA_B)
        ada_ref[mi] = send_buf[mi]
        second = []
        for k, p in enumerate(peers):
            second.append(_remote(send_buf.at[_lin(p)], ada_ref.at[mi], ssem.at[14 + k], rsem.at[14 + k], p))
        for cp in second:
            cp.start()
        start_jobs()
        for k, p in enumerate(peers):
            _remote(send_buf.at[mi], ada_ref.at[_lin(p)], ssem.at[14 + k], rsem.at[14 + k], p).wait_recv()
        for cp in first + second:
            cp.wait_send()

    return _call(
        core, name="ada_fwd", grid=(), jobs=jobs, core_starts=True, in_specs=[VM, VM, VM, VM], out_specs=[VM, VM, VM],
        out_shape=[jax.ShapeDtypeStruct((8, 8, ADA_B), F32), jax.ShapeDtypeStruct((64, D), F32),
                   jax.ShapeDtypeStruct((8, 32, 64), F32)],
        scratch=[pltpu.VMEM((8, 8, D), F32), pltpu.VMEM((8, 8, ADA_B), F32),
                 pltpu.SemaphoreType.DMA((21,)), pltpu.SemaphoreType.DMA((21,))],
        args=[c_pad, w_ada, b_cols, cw_pad])


def _ada_bwd(dada, jobs=()):
    def core(ins, outs, scs):
        (d_ref,) = ins
        dd_ref, gb_ref = outs
        rbuf, ssem, rsem = scs
        me = _me()
        mi = _lin(me)
        peers = [_flip(me, f) for f in FLIPS]
        rbuf[mi] = d_ref[mi]
        first = []
        for k, p in enumerate(peers):
            first.append(_remote(d_ref.at[_lin(p)], rbuf.at[mi], ssem.at[k], rsem.at[k], p))
        for cp in first:
            cp.start()
        for k, p in enumerate(peers):
            _remote(d_ref.at[mi], rbuf.at[_lin(p)], ssem.at[k], rsem.at[k], p).wait_recv()
        dd = rbuf[...].reshape(64, ADA_B)
        dd_ref[...] = dd
        gb_ref[mi] = jnp.broadcast_to(_colsum(dd), (8, ADA_B))
        second = []
        for k, p in enumerate(peers):
            second.append(_remote(gb_ref.at[mi], gb_ref.at[mi], ssem.at[7 + k], rsem.at[7 + k], p))
        for cp in second:
            cp.start()
        for k, p in enumerate(peers):
            pi = _lin(p)
            _remote(gb_ref.at[pi], gb_ref.at[pi], ssem.at[7 + k], rsem.at[7 + k], p).wait_recv()
        for cp in first + second:
            cp.wait_send()

    return _call(
        core, name="ada_bwd", grid=(), jobs=jobs, in_specs=[VM], out_specs=[VM, VM],
        out_shape=[jax.ShapeDtypeStruct((64, ADA_B), F32), jax.ShapeDtypeStruct((8, 8, ADA_B), F32)],
        scratch=[pltpu.VMEM((8, 8, ADA_B), F32), pltpu.SemaphoreType.DMA((14,)), pltpu.SemaphoreType.DMA((14,))],
        args=[dada])


SMALL_D = ("g_pre_f1", "g_post_f1", "g_pre_m", "g_post_m", "g_pre_f2", "g_post_f2")
SMALL_W = ("gmlp_norm_g", "gmlp_norm_b", "conv_b", "conv_norm_g", "conv_norm_b", "g_out_a", "g_out_b")


def kernel(x, c, w_ada, b_ada, g_pre_f1, g_post_f1, w_f1_in, w_f1_out, g_pre_m, g_post_m, w_mix_in, gmlp_norm_g, gmlp_norm_b, w_spatial, b_spatial, conv_w, conv_b, conv_norm_g, conv_norm_b, g_out_a, g_out_b, w_mix_out, g_pre_f2, g_post_f2, w_f2_in, w_f2_out, loss_target, m_w_ada, m_b_ada, m_g_pre_f1, m_g_post_f1, m_w_f1_in, m_w_f1_out, m_g_pre_m, m_g_post_m, m_w_mix_in, m_gmlp_norm_g, m_gmlp_norm_b, m_w_spatial, m_b_spatial, m_conv_w, m_conv_b, m_conv_norm_g, m_conv_norm_b, m_g_out_a, m_g_out_b, m_w_mix_out, m_g_pre_f2, m_g_post_f2, m_w_f2_in, m_w_f2_out, v_w_ada, v_b_ada, v_g_pre_f1, v_g_post_f1, v_w_f1_in, v_w_f1_out, v_g_pre_m, v_g_post_m, v_w_mix_in, v_gmlp_norm_g, v_gmlp_norm_b, v_w_spatial, v_b_spatial, v_conv_w, v_conv_b, v_conv_norm_g, v_conv_norm_b, v_g_out_a, v_g_out_b, v_w_mix_out, v_g_pre_f2, v_g_post_f2, v_w_f2_in, v_w_f2_out):
    given = dict(locals())
    bl, seq, _ = x.shape
    T = bl * seq
    tm = min(256, seq // 2)
    mi = _lin((lax.axis_index("x"), lax.axis_index("y"), lax.axis_index("c")))

    def shard_in(w):
        return w[0].T.astype(BF16)

    g_f1 = _Gather([shard_in(w_f1_in), w_f1_out[0].astype(BF16)], ("rows", "out"))
    s_f2 = shard_in(w_f2_in)
    g_mx = _Gather([w_mix_in[0].astype(BF16), w_mix_out[0].astype(BF16), w_f2_out[0].astype(BF16), s_f2[:, 0:D // 4]],
                   ("rows", "rows", "out", "rows"), late_mid=True)
    g_f2 = _Gather([s_f2[:, D // 4:D]], ("rows",))

    c_pad = jnp.pad(c, ((0, 8 - bl), (0, 0)))
    b_cols = lax.dynamic_slice(b_ada, (0, mi * ADA_B), (1, ADA_B))
    cw_pad = jnp.pad(conv_w[0], ((0, 1), (0, 0)))
    (ada_blk, sc_all, cw_all), ((wi1, wo1),) = _ada_fwd(c_pad, w_ada[0], b_cols, cw_pad, jobs=[g_f1])
    ada = ada_blk[:, 0:bl, :].transpose(1, 0, 2).reshape(bl, 9, D)
    pad5 = jnp.zeros((bl, 5, D), F32)
    mod1 = jnp.concatenate([ada[:, 0:3], pad5], axis=1)
    mod2 = jnp.concatenate([ada[:, 3:6], pad5], axis=1)
    mod3 = jnp.concatenate([ada[:, 6:9], pad5], axis=1)
    cw_full = cw_all.transpose(1, 0, 2).reshape(32, WA)

    zrow = jnp.zeros((1, D), F32)
    gv1 = jnp.concatenate([g_pre_f1, g_post_f1] + [zrow] * 6, axis=0)
    gvm = jnp.concatenate([g_pre_m, g_post_m] + [zrow] * 6, axis=0)
    gv2 = jnp.concatenate([g_pre_f2, g_post_f2] + [zrow] * 6, axis=0)
    v512 = jnp.concatenate([gmlp_norm_g, gmlp_norm_b, conv_b, conv_norm_g, conv_norm_b, g_out_a, g_out_b,
                            jnp.zeros((1, WA), F32)], axis=0)
    ws = w_spatial[0]
    bias_full = jnp.repeat(b_spatial[0].T, HD, axis=1)
    esel = (lax.broadcasted_iota(jnp.int32, (8, WA), 1) // HD == lax.broadcasted_iota(jnp.int32, (8, WA), 0)).astype(F32)

    x0 = x.reshape(T, D)
    (x1, gu1, y1), ((wmi, wmo, wo2, wi2a),) = _ffn_fwd(x0, mod1, gv1, wi1, wo1, tm, "ffn1_fwd", jobs=[g_mx])
    wmo = wmo.reshape(D, D)
    (x2, proj, ym, conv), ((wi2b,),) = _mixer_fwd(x1, mod2, gvm, wmi, wmo, v512, ws, bias_full, cw_full, tm, "mixer_fwd", jobs=[g_f2])

    (dx2, dg2, act2, hb2, dyb2, mg3, vg3, loss_blk), _ = _ffn_last(
        x2, loss_target.reshape(T, D), mod3, gv2, (wi2a, wi2b), wo2, tm, "ffn2_fwd_bwd")
    (g_wi2,), _ = _grad_w_in(dg2, hb2, "ffn2_gw_in")
    (g_wo2,), _ = _grad_w_out(act2, dyb2, "ffn2_gw_out")
    (dpart, dymb, ycat, mg2a, vgma, v5g, gws, gbs), ((p_wo2,),) = _mixer_bwd_a(
        dx2, ym, proj, conv, mod2, gvm, wmo, v512, ws, bias_full, esel, tm, "mixer_bwd_a",
        jobs=[_ChipScatter([g_wo2])])
    (dx1, dproj, hbm, mg2b, vgmb, dcw), ((p_wi2,),) = _mixer_bwd_b(
        dx2, x1, dpart, proj, mod2, gvm, wmi, cw_full, tm, "mixer_bwd_b", jobs=[_ChipScatter([g_wi2])])
    (g_wmi,), _ = _grad_w_mi(hbm, dproj, "mixer_gw_in")
    (g_wmo,), ((p_wmi,),) = _grad_w_mo(ycat, dymb, "mixer_gw_out", jobs=[_ChipScatter([g_wmi])])
    p2 = jnp.concatenate([v5g, dcw], axis=0)
    (dx0, dg1, act1, hb1, dyb1, mg1, vg1), _ = _ffn_bwd(dx1, x0, y1, gu1, mod1, gv1, wi1, wo1, tm, "ffn1_bwd")

    dada = jnp.concatenate([mg1[:, 0:3], mg2b[:, 0:2], mg2a[:, 2:3], mg3[:, 0:3]], axis=1)
    dada = dada.reshape(bl, NDEV, ADA_B).transpose(1, 0, 2)
    dada = jnp.pad(dada, ((0, 0), (0, 8 - bl), (0, 0)))
    p1 = jnp.concatenate([vg1[0:2], vgmb[0:1], vgma[1:2], vg3[0:2], loss_blk[0:1], zrow], axis=0)
    (dd_all, gb_all), ((a1,),) = _ada_bwd(dada, jobs=[_AllGather([p1])])
    g_bada = gb_all[:, 0, :].reshape(1, 9 * D)

    (g_wo1,), ((p_wmo,),) = _grad_w_out(act1, dyb1, "ffn1_gw_out", jobs=[_ChipScatter([g_wmo])])
    (g_wi1,), ((a2, a3, a4), (p_wo1,)) = _grad_w_in(
        dg1, hb1, "ffn1_gw_in", jobs=[_Gather([p2, gws, gbs], ("rows",) * 3), _ChipScatter([g_wo1])])

    h_f1, token = _chip_scatter_start([g_wi1], "tail_start")

    res = {}
    quad = _adamw_reduce(p_wi2, w_f2_in[0].T, m_w_f2_in[0].T, v_w_f2_in[0].T, FO // 2, "adamw_w_f2_in", after=token)
    res["w_f2_in"] = tuple(t.T[None] for t in quad)
    for nm, part, tr in (("w_f2_out", p_wo2, FO // 2), ("w_mix_in", p_wmi, 256), ("w_mix_out", p_wmo, MO // 2),
                         ("w_f1_out", p_wo1, FO // 2)):
        quad = _adamw_reduce(part, given[nm][0], given["m_" + nm][0], given["v_" + nm][0], tr, "adamw_" + nm, after=quad[1])
        res[nm] = tuple(t[None] for t in quad)
    quad = _adamw_ada(sc_all, dd_all, w_ada[0], m_w_ada[0], v_w_ada[0], 128, "adamw_w_ada", after=quad[1])
    res["w_ada"] = tuple(t[None] for t in quad)
    (g_wi1,), (p_wi1,) = _chip_scatter_wait(h_f1, quad[1], "tail_wait")
    quad = _adamw_reduce(p_wi1, w_f1_in[0].T, m_w_f1_in[0].T, v_w_f1_in[0].T, FO // 2, "adamw_w_f1_in", own=g_wi1)
    res["w_f1_in"] = tuple(t.T[None] for t in quad)

    small = SMALL_D + SMALL_W + ("w_spatial", "b_spatial", "b_ada")
    grads = [(0, r) for r in range(6)] + [(1, r) for r in range(7)] + [(2, None), (3, None), (4, None)]
    wmv = []
    for nm in small:
        for pre in ("", "m_", "v_"):
            wmv.append(given[pre + nm][0] if nm in ("w_spatial", "b_spatial") else given[pre + nm])
    outs = _adamw_small([a1, a2, a3, a4], [g_bada], grads, wmv, (0, 1), "adamw_small")
    loss = outs[0][6, 0]
    for t, nm in enumerate(small):
        quad = outs[2 + 4 * t:6 + 4 * t]
        res[nm] = tuple(q[None] for q in quad) if nm in ("w_spatial", "b_spatial") else tuple(quad)
    g_cw = lax.dynamic_slice(outs[1], (8, mi * 64), (32, 64))
    wmv = [jnp.pad(given[pre + "conv_w"][0], ((0, 1), (0, 0)), constant_values=1.0 if pre == "v_" else 0.0)
           for pre in ("", "m_", "v_")]
    quad = _adamw_small([], [g_cw], [(0, None)], wmv, (), "adamw_conv_w")
    res["conv_w"] = tuple(q[0:CONV_K][None] for q in quad)

    order = ["w_ada", "b_ada", "g_pre_f1", "g_post_f1", "w_f1_in", "w_f1_out", "g_pre_m", "g_post_m", "w_mix_in",
             "gmlp_norm_g", "gmlp_norm_b", "w_spatial", "b_spatial", "conv_w", "conv_b", "conv_norm_g", "conv_norm_b",
             "g_out_a", "g_out_b", "w_mix_out", "g_pre_f2", "g_post_f2", "w_f2_in", "w_f2_out"]
    out = [loss, dx0.reshape(bl, seq, D)]
    for k in range(4):
        out += [res[nm][k] for nm in order]
    return tuple(out)
```

```python
import jax
import jax.numpy as jnp
from jax import lax
from jax.experimental import pallas as pl
from jax.experimental.pallas import tpu as pltpu

F32 = jnp.float32
BF16 = jnp.bfloat16

D = 1024
DFF = 2816
NDEV = 8
FB = 2 * DFF // NDEV
NCH = DFF // FB
LANES = 128
SUBL = 8
FO = DFF // NDEV
WA = 512
NSLAB = WA // LANES
NHEAD = 8
HD = 64
CHUNK = 128
CONV_K = 31
HALO = 32
MB = 2 * (WA + WA) // NDEV
MO = D // NDEV
ADA_B = 9 * D // NDEV
EPS = 1e-6
HALF = 0.5

ADAM_LR = 0.001
ADAM_B1 = 0.9
ADAM_B2 = 0.999
ADAM_EPS = 1e-08
ADAM_WD = 0.01
ADAM_STEP = 10

VMEM_LIMIT = 56 * 1024 * 1024
MESH = pl.DeviceIdType.MESH
FLIPS = ((0, 0, 1), (1, 0, 0), (0, 1, 0), (1, 1, 0), (1, 0, 1), (0, 1, 1), (1, 1, 1))
CHIP_FLIPS = ((1, 0, 0), (0, 1, 0), (1, 1, 0))
HBM = pl.BlockSpec(memory_space=pl.ANY)
VM = pl.BlockSpec(memory_space=pltpu.VMEM)


def _dot(a, b):
    return lax.dot_general(a, b, (((1,), (0,)), ((), ())), preferred_element_type=F32)


def _dot_nt(a, b):
    return lax.dot_general(a, b, (((1,), (1,)), ((), ())), preferred_element_type=F32)


def _dot_tn(a, b):
    return lax.dot_general(a, b, (((0,), (0,)), ((), ())), preferred_element_type=F32)


def _rowmean(v):
    return jnp.mean(v, axis=-1, keepdims=True)


def _colsum(v):
    return jnp.sum(v, axis=0, keepdims=True)


def _sigmoid(v):
    return 0.5 * jnp.tanh(0.5 * v) + 0.5


def _const_spec(shape):
    nd = len(shape)
    return pl.BlockSpec(shape, lambda *_: (0,) * nd, pipeline_mode=pl.Buffered(1))


def _me():
    return lax.axis_index("x"), lax.axis_index("y"), lax.axis_index("c")


def _flip(me, f):
    return tuple(1 - v if b else v for v, b in zip(me, f))


def _lin(p):
    return 4 * p[0] + 2 * p[1] + p[2]


def _remote(src, dst, send_sem, recv_sem, dev):
    return pltpu.make_async_remote_copy(src_ref=src, dst_ref=dst, send_sem=send_sem, recv_sem=recv_sem,
                                        device_id=dev, device_id_type=MESH)


def _blk(kind, ref, p):
    if kind == "out":
        return ref.at[2 * p[0] + p[1], pl.ds(p[2] * FO, FO), :]
    return ref.at[_lin(p)]


class _Gather:
    def __init__(self, shards, kinds, late_mid=False):
        self.late_mid = late_mid
        self.kinds = kinds
        self.n = len(shards)
        self.ins = list(shards)
        self.out_shape = [jax.ShapeDtypeStruct((4, FB, D) if k == "out" else (NDEV,) + s.shape, s.dtype)
                          for s, k in zip(shards, kinds)]
        self.sems = [pltpu.SemaphoreType.DMA((7 * self.n,)), pltpu.SemaphoreType.DMA((7 * self.n,)),
                     pltpu.SemaphoreType.DMA((self.n,))]

    def _first(self, ins, outs, sems):
        ssem, rsem, lsem = sems
        me = _me()
        sib = _flip(me, (0, 0, 1))
        cps, loc = [], []
        for a in range(self.n):
            mine = _blk(self.kinds[a], outs[a], me)
            loc.append(pltpu.make_async_copy(ins[a], mine, lsem.at[a]))
            cps.append(_remote(ins[a], mine, ssem.at[7 * a], rsem.at[7 * a], sib))
            for j, f in enumerate(CHIP_FLIPS):
                cps.append(_remote(ins[a], mine, ssem.at[7 * a + 1 + j], rsem.at[7 * a + 1 + j], _flip(me, f)))
        return cps, loc

    def _passed(self, outs, sems):
        ssem, rsem, _ = sems
        me = _me()
        sib = _flip(me, (0, 0, 1))
        cps = []
        for j, f in enumerate(CHIP_FLIPS):
            for a in range(self.n):
                blk = _blk(self.kinds[a], outs[a], _flip(me, f))
                cps.append(_remote(blk, blk, ssem.at[7 * a + 4 + j], rsem.at[7 * a + 4 + j], sib))
        return cps

    def start(self, ins, outs, sems):
        cps, loc = self._first(ins, outs, sems)
        for cp in loc + cps:
            cp.start()

    def mid(self, ins, outs, sems):
        ssem, rsem, _ = sems
        me = _me()
        passed = self._passed(outs, sems)
        t = 0
        for j, f in enumerate(CHIP_FLIPS):
            for a in range(self.n):
                blk = _blk(self.kinds[a], outs[a], _flip(me, f))
                _remote(blk, blk, ssem.at[7 * a + 1 + j], rsem.at[7 * a + 1 + j], _flip(me, f)).wait_recv()
                passed[t].start()
                t += 1

    def end(self, ins, outs, sems):
        ssem, rsem, _ = sems
        me = _me()
        sib = _flip(me, (0, 0, 1))
        for a in range(self.n):
            blk = _blk(self.kinds[a], outs[a], sib)
            _remote(blk, blk, ssem.at[7 * a], rsem.at[7 * a], sib).wait_recv()
            for j, f in enumerate(CHIP_FLIPS):
                blk = _blk(self.kinds[a], outs[a], _flip(_flip(me, f), (0, 0, 1)))
                _remote(blk, blk, ssem.at[7 * a + 4 + j], rsem.at[7 * a + 4 + j], sib).wait_recv()
        cps, loc = self._first(ins, outs, sems)
        for cp in cps + self._passed(outs, sems):
            cp.wait_send()
        for cp in loc:
            cp.wait()


class _RelayGather(_Gather):
    def _peers(self):
        me = _me()
        c = me[2]
        to = (me[0] + (1 - c) - 2 * me[0] * (1 - c), me[1] + c - 2 * me[1] * c, c)
        frm = (me[0] + c - 2 * me[0] * c, me[1] + (1 - c) - 2 * me[1] * (1 - c), c)
        return me, _flip(me, (0, 0, 1)), to, frm, _flip(me, (1, 1, 0))

    def _first(self, ins, outs, sems):
        ssem, rsem, lsem = sems
        me, sib, to, frm, _ = self._peers()
        cps, loc = [], []
        for a in range(self.n):
            mine = _blk(self.kinds[a], outs[a], me)
            loc.append(pltpu.make_async_copy(ins[a], mine, lsem.at[a]))
            for slot, dev in ((0, sib), (1, to), (2, frm)):
                cps.append(_remote(ins[a], mine, ssem.at[7 * a + slot], rsem.at[7 * a + slot], dev))
        return cps, loc

    def _block_copy(self, outs, sems, a, slot, owner, dev):
        ssem, rsem, _ = sems
        blk = _blk(self.kinds[a], outs[a], owner)
        return _remote(blk, blk, ssem.at[7 * a + slot], rsem.at[7 * a + slot], dev)

    def mid(self, ins, outs, sems):
        me, sib, to, frm, _ = self._peers()
        for a in range(self.n):
            self._block_copy(outs, sems, a, 2, frm, frm).wait_recv()
            self._block_copy(outs, sems, a, 3, frm, to).start()
            self._block_copy(outs, sems, a, 5, frm, sib).start()
        for a in range(self.n):
            self._block_copy(outs, sems, a, 1, to, to).wait_recv()
            self._block_copy(outs, sems, a, 4, to, sib).start()

    def end(self, ins, outs, sems):
        me, sib, to, frm, far = self._peers()
        up = (0, 0, 1)
        for a in range(self.n):
            self._block_copy(outs, sems, a, 3, far, to).wait_recv()
            self._block_copy(outs, sems, a, 6, far, sib).start()
        for a in range(self.n):
            for slot, owner in ((0, sib), (4, _flip(frm, up)), (5, _flip(to, up)), (6, _flip(far, up))):
                self._block_copy(outs, sems, a, slot, owner, sib).wait_recv()
        cps, loc = self._first(ins, outs, sems)
        for a in range(self.n):
            cps += [self._block_copy(outs, sems, a, 3, frm, to), self._block_copy(outs, sems, a, 4, to, sib),
                    self._block_copy(outs, sems, a, 5, frm, sib), self._block_copy(outs, sems, a, 6, far, sib)]
        for cp in cps:
            cp.wait_send()
        for cp in loc:
            cp.wait()


class _ChipScatter:
    def __init__(self, grads):
        self.n = len(grads)
        self.ins = list(grads)
        self.out_shape = [jax.ShapeDtypeStruct(g.shape, BF16) for g in grads]
        self.sems = [pltpu.SemaphoreType.DMA((3 * self.n,)), pltpu.SemaphoreType.DMA((3 * self.n,)),
                     pltpu.SemaphoreType.DMA((self.n,))]

    def _copies(self, ins, outs, sems):
        ssem, rsem, lsem = sems
        me = _me()
        mq = 2 * me[0] + me[1]
        loc = [pltpu.make_async_copy(ins[a].at[mq], outs[a].at[mq], lsem.at[a]) for a in range(self.n)]
        cps = []
        for k, f in enumerate(CHIP_FLIPS):
            p = _flip(me, f)
            for a in range(self.n):
                cps.append(_remote(ins[a].at[2 * p[0] + p[1]], outs[a].at[mq], ssem.at[3 * a + k], rsem.at[3 * a + k], p))
        return cps, loc

    def start(self, ins, outs, sems):
        cps, loc = self._copies(ins, outs, sems)
        for cp in loc + cps:
            cp.start()

    mid = None

    def end(self, ins, outs, sems):
        ssem, rsem, _ = sems
        me = _me()
        mq = 2 * me[0] + me[1]
        for k, f in enumerate(CHIP_FLIPS):
            p = _flip(me, f)
            for a in range(self.n):
                _remote(ins[a].at[mq], outs[a].at[2 * p[0] + p[1]], ssem.at[3 * a + k], rsem.at[3 * a + k], p).wait_recv()
        cps, loc = self._copies(ins, outs, sems)
        for cp in cps:
            cp.wait_send()
        for cp in loc:
            cp.wait()


class _AllGather:
    def __init__(self, parts):
        self.n = len(parts)
        self.ins = list(parts)
        self.out_shape = [jax.ShapeDtypeStruct((NDEV,) + p.shape, p.dtype) for p in parts]
        self.sems = [pltpu.SemaphoreType.DMA((7 * self.n,)), pltpu.SemaphoreType.DMA((7 * self.n,)),
                     pltpu.SemaphoreType.DMA((self.n,))]

    def _copies(self, ins, outs, sems):
        ssem, rsem, lsem = sems
        me = _me()
        mi = _lin(me)
        loc = [pltpu.make_async_copy(ins[a], outs[a].at[mi], lsem.at[a]) for a in range(self.n)]
        cps = []
        for k, f in enumerate(FLIPS):
            for a in range(self.n):
                cps.append(_remote(ins[a], outs[a].at[mi], ssem.at[7 * a + k], rsem.at[7 * a + k], _flip(me, f)))
        return cps, loc

    def start(self, ins, outs, sems):
        cps, loc = self._copies(ins, outs, sems)
        for cp in loc + cps:
            cp.start()

    mid = None

    def end(self, ins, outs, sems):
        ssem, rsem, _ = sems
        me = _me()
        for k, f in enumerate(FLIPS):
            p = _flip(me, f)
            for a in range(self.n):
                _remote(ins[a], outs[a].at[_lin(p)], ssem.at[7 * a + k], rsem.at[7 * a + k], p).wait_recv()
        cps, loc = self._copies(ins, outs, sems)
        for cp in cps:
            cp.wait_send()
        for cp in loc:
            cp.wait()


def _call(core, *, name, grid, in_specs, out_specs, out_shape, args, scratch=(), jobs=(), core_starts=False):
    n_in, n_out, n_sc = len(in_specs), len(out_specs), len(scratch)
    steps = 1
    for g in grid:
        steps *= g

    def body(*refs):
        pos = [0]

        def take(k):
            r = refs[pos[0]:pos[0] + k]
            pos[0] += k
            return r

        ins = take(n_in)
        j_ins = [take(len(j.ins)) for j in jobs]
        outs = take(n_out)
        j_outs = [take(len(j.out_shape)) for j in jobs]
        scs = take(n_sc)
        j_sems = [take(len(j.sems)) for j in jobs]
        if len(grid) == 2:
            step = pl.program_id(0) * grid[1] + pl.program_id(1)
        elif len(grid) == 1:
            step = pl.program_id(0)
        else:
            step = 0
        def start_jobs():
            for j, ji, jo, js in zip(jobs, j_ins, j_outs, j_sems):
                j.start(ji, jo, js)

        if grid:
            pl.when(step == 0)(start_jobs)
        elif not core_starts:
            start_jobs()
        for j, ji, jo, js in zip(jobs, j_ins, j_outs, j_sems):
            if j.mid is not None and grid:
                at = max(steps - 2, 0) if j.late_mid else (3 * steps) // 4
                pl.when(step == at)(lambda j=j, ji=ji, jo=jo, js=js: j.mid(ji, jo, js))
        if core_starts:
            core(ins, outs, scs, start_jobs)
        elif core is not None:
            core(ins, outs, scs)
        for j, ji, jo, js in zip(jobs, j_ins, j_outs, j_sems):
            if grid:
                pl.when(step == steps - 1)(lambda j=j, ji=ji, jo=jo, js=js: j.end(ji, jo, js))
            else:
                if j.mid is not None:
                    j.mid(ji, jo, js)
                j.end(ji, jo, js)

    all_in = list(in_specs)
    all_args = list(args)
    all_out = list(out_specs)
    all_shape = list(out_shape)
    all_sc = list(scratch)
    for j in jobs:
        all_in += [HBM] * len(j.ins)
        all_args += j.ins
    for j in jobs:
        all_out += [HBM] * len(j.out_shape)
        all_shape += j.out_shape
        all_sc += j.sems
    params = dict(vmem_limit_bytes=VMEM_LIMIT)
    if grid:
        params["dimension_semantics"] = ("arbitrary",) * len(grid)
    res = pl.pallas_call(
        body, name=name, grid=grid, in_specs=all_in, out_specs=all_out, out_shape=all_shape,
        scratch_shapes=all_sc, compiler_params=pltpu.CompilerParams(**params),
    )(*all_args)
    core_res = list(res[:n_out])
    job_res = []
    pos = n_out
    for j in jobs:
        job_res.append(list(res[pos:pos + len(j.out_shape)]))
        pos += len(j.out_shape)
    return core_res, job_res


def _ffn_fwd(x, mod, gvec, w_in, w_out, tm, name, jobs=()):
    T = x.shape[0]
    nt = T // tm
    tps = nt // mod.shape[0]

    def core(ins, outs, _):
        x_ref, mod_ref, g_ref, win_ref, wout_ref = ins
        xo_ref, gu_ref, y_ref = outs
        xv = x_ref[...]
        sh, sc, gt = mod_ref[0:1, :], mod_ref[1:2, :], mod_ref[2:3, :]
        r = lax.rsqrt(_rowmean(xv * xv) + EPS)
        h = (xv * r * g_ref[0:1, :]) * (1.0 + sc) + sh
        hb = h.astype(BF16)
        y = jnp.zeros((tm, D), F32)
        for cidx in range(NCH):
            gate = _dot_nt(hb, win_ref[cidx])
            up = _dot_nt(hb, win_ref[NCH + cidx])
            gu_ref[cidx] = gate.astype(BF16)
            gu_ref[NCH + cidx] = up.astype(BF16)
            act = gate * _sigmoid(gate) * up
            y = y + _dot(act.astype(BF16), wout_ref[cidx])
        y_ref[...] = y
        ry = lax.rsqrt(_rowmean(y * y) + EPS)
        xo_ref[...] = xv + (HALF * gt) * (y * ry * g_ref[1:2, :])

    tile = pl.BlockSpec((tm, D), lambda i: (i, 0))
    return _call(
        core, name=name, grid=(nt,), jobs=jobs,
        in_specs=[tile, pl.BlockSpec((None, 8, D), lambda i: (i // tps, 0, 0)), _const_spec((8, D)),
                  _const_spec((8, FB, D)), _const_spec((4, FB, D))],
        out_specs=[tile, pl.BlockSpec((8, tm, FB), lambda i: (0, i, 0)), tile],
        out_shape=[jax.ShapeDtypeStruct((T, D), F32), jax.ShapeDtypeStruct((8, T, FB), BF16),
                   jax.ShapeDtypeStruct((T, D), F32)],
        args=[x, mod, gvec, w_in, w_out])


def _ffn_bwd(dxo, x, y, gu, mod, gvec, w_in, w_out, tm, name, jobs=()):
    T = x.shape[0]
    nt = T // tm
    nb = mod.shape[0]
    tps = nt // nb

    def core(ins, outs, _):
        dxo_ref, x_ref, y_ref, gu_ref, mod_ref, g_ref, win_ref, wout_ref = ins
        dx_ref, dg_ref, act_ref, hb_ref, dyb_ref, mg_ref, vg_ref = outs
        i = pl.program_id(0)
        xv = x_ref[...]
        dxo_v = dxo_ref[...]
        yv = y_ref[...]
        sh, sc, gt = mod_ref[0:1, :], mod_ref[1:2, :], mod_ref[2:3, :]
        gpre, gpost = g_ref[0:1, :], g_ref[1:2, :]
        r = lax.rsqrt(_rowmean(xv * xv) + EPS)
        xh = xv * r
        n = xh * gpre
        hb = (n * (1.0 + sc) + sh).astype(BF16)
        hb_ref[...] = hb
        ry = lax.rsqrt(_rowmean(yv * yv) + EPS)
        yh = yv * ry
        d_gt = _colsum(HALF * dxo_v * (yh * gpost))
        dp = (HALF * gt) * dxo_v
        d_gpost = _colsum(dp * yh)
        dyh = dp * gpost
        dy = ry * (dyh - yh * _rowmean(dyh * yh))
        dyb = dy.astype(BF16)
        dyb_ref[...] = dyb
        dh = jnp.zeros((tm, D), F32)
        for cidx in range(NCH):
            gate = gu_ref[cidx].astype(F32)
            up = gu_ref[NCH + cidx].astype(F32)
            sig = _sigmoid(gate)
            s = gate * sig
            act_ref[cidx] = (s * up).astype(BF16)
            d_act = _dot_nt(dyb, wout_ref[cidx])
            d_up = (d_act * s).astype(BF16)
            d_gate = (d_act * up * (sig * (1.0 + gate * (1.0 - sig)))).astype(BF16)
            dg_ref[cidx] = d_gate
            dg_ref[NCH + cidx] = d_up
            dh = dh + _dot(d_gate, win_ref[cidx]) + _dot(d_up, win_ref[NCH + cidx])
        d_sc = _colsum(dh * n)
        d_sh = _colsum(dh)
        dn = dh * (1.0 + sc)
        d_gpre = _colsum(dn * xh)
        dxh = dn * gpre
        dx_ref[...] = dxo_v + r * (dxh - xh * _rowmean(dxh * xh))

        @pl.when(i % tps == 0)
        def _():
            mg_ref[...] = jnp.zeros((8, D), F32)

        @pl.when(i == 0)
        def _():
            vg_ref[...] = jnp.zeros((8, D), F32)

        mg_ref[0:1, :] += d_sh
        mg_ref[1:2, :] += d_sc
        mg_ref[2:3, :] += d_gt
        vg_ref[0:1, :] += d_gpre
        vg_ref[1:2, :] += d_gpost

    tile = pl.BlockSpec((tm, D), lambda i: (i, 0))
    return _call(
        core, name=name, grid=(nt,), jobs=jobs,
        in_specs=[tile, tile, tile, pl.BlockSpec((8, tm, FB), lambda i: (0, i, 0)),
                  pl.BlockSpec((None, 8, D), lambda i: (i // tps, 0, 0)), _const_spec((8, D)),
                  _const_spec((8, FB, D)), _const_spec((4, FB, D))],
        out_specs=[tile, pl.BlockSpec((8, tm, FB), lambda i: (0, i, 0)),
                   pl.BlockSpec((4, tm, FB), lambda i: (0, i, 0)), tile, tile,
                   pl.BlockSpec((None, 8, D), lambda i: (i // tps, 0, 0)), pl.BlockSpec((8, D), lambda i: (0, 0))],
        out_shape=[jax.ShapeDtypeStruct((T, D), F32), jax.ShapeDtypeStruct((8, T, FB), BF16),
                   jax.ShapeDtypeStruct((4, T, FB), BF16), jax.ShapeDtypeStruct((T, D), BF16),
                   jax.ShapeDtypeStruct((T, D), BF16), jax.ShapeDtypeStruct((nb, 8, D), F32),
                   jax.ShapeDtypeStruct((8, D), F32)],
        args=[dxo, x, y, gu, mod, gvec, w_in, w_out])


def _ffn_last(x, target, mod, gvec, w_in, w_out, tm, name, jobs=()):
    T = x.shape[0]
    nt = T // tm
    nb = mod.shape[0]
    tps = nt // nb

    def core(ins, outs, scs):
        x_ref, t_ref, mod_ref, g_ref, wina_ref, winb_ref, wout_ref = ins
        dx_ref, dg_ref, act_ref, hb_ref, dyb_ref, mg_ref, vg_ref, loss_ref = outs
        hd2 = w_in[0].shape[2]
        (gu_s,) = scs
        i = pl.program_id(0)
        xv = x_ref[...]
        sh, sc, gt = mod_ref[0:1, :], mod_ref[1:2, :], mod_ref[2:3, :]
        gpre, gpost = g_ref[0:1, :], g_ref[1:2, :]
        r = lax.rsqrt(_rowmean(xv * xv) + EPS)
        xh = xv * r
        n = xh * gpre
        hb = (n * (1.0 + sc) + sh).astype(BF16)
        hb_ref[...] = hb
        hba, hbb = hb[:, 0:hd2], hb[:, hd2:D]
        yv = jnp.zeros((tm, D), F32)
        for cidx in range(NCH):
            gate = _dot_nt(hba, wina_ref[cidx]) + _dot_nt(hbb, winb_ref[cidx])
            up = _dot_nt(hba, wina_ref[NCH + cidx]) + _dot_nt(hbb, winb_ref[NCH + cidx])
            gu_s[cidx] = gate.astype(BF16)
            gu_s[NCH + cidx] = up.astype(BF16)
            act = gate * _sigmoid(gate) * up
            act_ref[cidx] = act.astype(BF16)
            yv = yv + _dot(act_ref[cidx], wout_ref[cidx])
        ry = lax.rsqrt(_rowmean(yv * yv) + EPS)
        yh = yv * ry
        pn = yh * gpost
        err = xv + (HALF * gt) * pn - t_ref[...]
        dxo_v = err * (1.0 / D)
        d_gt = _colsum(HALF * dxo_v * pn)
        dp = (HALF * gt) * dxo_v
        d_gpost = _colsum(dp * yh)
        dyh = dp * gpost
        dyb = (ry * (dyh - yh * _rowmean(dyh * yh))).astype(BF16)
        dyb_ref[...] = dyb
        dha = jnp.zeros((tm, hd2), F32)
        dhb = jnp.zeros((tm, D - hd2), F32)
        for cidx in range(NCH):
            gate = gu_s[cidx].astype(F32)
            up = gu_s[NCH + cidx].astype(F32)
            sig = _sigmoid(gate)
            s = gate * sig
            d_act = _dot_nt(dyb, wout_ref[cidx])
            d_up = (d_act * s).astype(BF16)
            d_gate = (d_act * up * (sig * (1.0 + gate * (1.0 - sig)))).astype(BF16)
            dg_ref[cidx] = d_gate
            dg_ref[NCH + cidx] = d_up
            dha = dha + _dot(d_gate, wina_ref[cidx]) + _dot(d_up, wina_ref[NCH + cidx])
            dhb = dhb + _dot(d_gate, winb_ref[cidx]) + _dot(d_up, winb_ref[NCH + cidx])
        dh = jnp.concatenate([dha, dhb], axis=1)
        d_sc = _colsum(dh * n)
        d_sh = _colsum(dh)
        dn = dh * (1.0 + sc)
        d_gpre = _colsum(dn * xh)
        dxh = dn * gpre
        dx_ref[...] = dxo_v + r * (dxh - xh * _rowmean(dxh * xh))

        @pl.when(i % tps == 0)
        def _():
            mg_ref[...] = jnp.zeros((8, D), F32)

        @pl.when(i == 0)
        def _():
            vg_ref[...] = jnp.zeros((8, D), F32)
            loss_ref[...] = jnp.zeros((8, D), F32)

        mg_ref[0:1, :] += d_sh
        mg_ref[1:2, :] += d_sc
        mg_ref[2:3, :] += d_gt
        vg_ref[0:1, :] += d_gpre
        vg_ref[1:2, :] += d_gpost
        loss_ref[...] += HALF * jnp.sum(_rowmean(err * err), axis=0, keepdims=True)

    tile = pl.BlockSpec((tm, D), lambda i: (i, 0))
    return _call(
        core, name=name, grid=(nt,), jobs=jobs,
        in_specs=[tile, tile, pl.BlockSpec((None, 8, D), lambda i: (i // tps, 0, 0)), _const_spec((8, D)),
                  _const_spec(w_in[0].shape), _const_spec(w_in[1].shape), _const_spec((4, FB, D))],
        out_specs=[tile, pl.BlockSpec((8, tm, FB), lambda i: (0, i, 0)),
                   pl.BlockSpec((4, tm, FB), lambda i: (0, i, 0)), tile, tile,
                   pl.BlockSpec((None, 8, D), lambda i: (i // tps, 0, 0)), pl.BlockSpec((8, D), lambda i: (0, 0)),
                   pl.BlockSpec((8, D), lambda i: (0, 0))],
        out_shape=[jax.ShapeDtypeStruct((T, D), F32), jax.ShapeDtypeStruct((8, T, FB), BF16),
                   jax.ShapeDtypeStruct((4, T, FB), BF16), jax.ShapeDtypeStruct((T, D), BF16),
                   jax.ShapeDtypeStruct((T, D), BF16), jax.ShapeDtypeStruct((nb, 8, D), F32),
                   jax.ShapeDtypeStruct((8, D), F32), jax.ShapeDtypeStruct((8, D), F32)],
        scratch=[pltpu.VMEM((8, tm, FB), BF16)],
        args=[x, target, mod, gvec, w_in[0], w_in[1], w_out])


def _masked_spatial(ws_ref):
    row = lax.broadcasted_iota(jnp.int32, (CHUNK, CHUNK), 0)
    col = lax.broadcasted_iota(jnp.int32, (CHUNK, CHUNK), 1)
    keep = col <= row
    return [jnp.where(keep, ws_ref[hd], 0.0).astype(BF16) for hd in range(NHEAD)]


def _head_pairs(mats, right, transpose=False):
    first = lax.broadcasted_iota(jnp.int32, (CHUNK, LANES), 1) < HD
    op = _dot_tn if transpose else _dot
    out = []
    for p in range(NHEAD // 2):
        slab = right[:, _lanes(p)]
        out.append(jnp.where(first, op(mats[2 * p], slab), op(mats[2 * p + 1], slab)))
    return jnp.concatenate(out, axis=1)


def _spatial_gate(wm, vb_chunk):
    return _head_pairs(wm, vb_chunk)


def _layer_norm_stats(v):
    mu = _rowmean(v)
    vc = v - mu
    rstd = lax.rsqrt(_rowmean(vc * vc) + EPS)
    return vc * rstd, rstd


def _pitch(tm):
    p = tm // 8
    while p % 8 != 4:
        p += 1
    return p


def _lanes(s):
    return slice(s * LANES, (s + 1) * LANES)


def _to_slabs(ref, row0, val):
    for s in range(NSLAB):
        ref[s, row0:row0 + val.shape[0], :] = val[:, _lanes(s)]


def _tap_sum(src, out, cw_ref, bias, tm, start):
    p = _pitch(tm)
    for s in range(NSLAB):
        accs = [jnp.broadcast_to(bias[:, _lanes(s)], (SUBL, LANES))] * p
        for k in range(CONV_K):
            w = jnp.broadcast_to(cw_ref[k:k + 1, _lanes(s)], (SUBL, LANES))
            for v in range(p):
                accs[v] = accs[v] + w * src[s, pl.ds(v + start(k), 8, stride=p), :]
        for v in range(p):
            out[s, pl.ds(v, 8, stride=p), :] = accs[v]
    return jnp.concatenate([out[s, 0:tm, :] for s in range(NSLAB)], axis=1)


def _mixer_fwd(x, mod, gvec, w_mi, w_mo, v512, ws, bias_full, cw, tm, name, jobs=()):
    T = x.shape[0]
    nt = T // tm
    tps = nt // mod.shape[0]
    ext_rows = 8 * _pitch(tm)

    def core(ins, outs, scs):
        x_ref, mod_ref, g_ref, wmi_ref, wmo_ref, v_ref, ws_ref, bias_ref, cw_ref = ins
        xo_ref, proj_ref, ym_ref, conv_ref = outs
        glu_ext, conv_scr = scs
        i = pl.program_id(0)
        xv = x_ref[...]
        sh, sc, gt = mod_ref[0:1, :], mod_ref[1:2, :], mod_ref[2:3, :]
        r = lax.rsqrt(_rowmean(xv * xv) + EPS)
        hb = ((xv * r * g_ref[0:1, :]) * (1.0 + sc) + sh).astype(BF16)
        for j in range(NDEV):
            proj_ref[:, j * MB:(j + 1) * MB] = _dot(hb, wmi_ref[j])
        u = proj_ref[:, 0:WA]
        v0 = proj_ref[:, WA:2 * WA]
        a = proj_ref[:, 2 * WA:3 * WA]
        g = proj_ref[:, 3 * WA:4 * WA]
        vh, _ = _layer_norm_stats(v0)
        vb = (vh * v_ref[0:1, :] + v_ref[1:2, :]).astype(BF16)
        wm = _masked_spatial(ws_ref)
        ya = []
        for q in range(tm // CHUNK):
            z = _spatial_gate(wm, vb[q * CHUNK:(q + 1) * CHUNK, :]) + bias_ref[...]
            ya.append(u[q * CHUNK:(q + 1) * CHUNK, :] * z)
        ya = jnp.concatenate(ya, axis=0)
        glu = a * _sigmoid(g)

        @pl.when(i == 0)
        def _():
            glu_ext[:, HALO + tm:HALO + ext_rows, :] = jnp.zeros((NSLAB, ext_rows - tm, LANES), F32)

        @pl.when(i % tps == 0)
        def _():
            glu_ext[:, 0:HALO, :] = jnp.zeros((NSLAB, HALO, LANES), F32)

        _to_slabs(glu_ext, HALO, glu)
        conv = _tap_sum(glu_ext, conv_scr, cw_ref, v_ref[2:3, :], tm, lambda k: HALO - (CONV_K - 1) + k)
        conv_ref[...] = conv
        glu_ext[:, 0:HALO, :] = glu_ext[:, tm:tm + HALO, :]
        ch, _ = _layer_norm_stats(conv)
        cn = ch * v_ref[3:4, :] + v_ref[4:5, :]
        yb = cn * _sigmoid(cn)
        pa = ya * lax.rsqrt(_rowmean(ya * ya) + EPS) * v_ref[5:6, :]
        pb = yb * lax.rsqrt(_rowmean(yb * yb) + EPS) * v_ref[6:7, :]
        ycat = jnp.concatenate([pa, pb], axis=1).astype(BF16)
        ym = _dot(ycat, wmo_ref[...])
        ym_ref[...] = ym
        rm = lax.rsqrt(_rowmean(ym * ym) + EPS)
        xo_ref[...] = xv + gt * (ym * rm * g_ref[1:2, :])

    tile = pl.BlockSpec((tm, D), lambda i: (i, 0))
    return _call(
        core, name=name, grid=(nt,), jobs=jobs,
        in_specs=[tile, pl.BlockSpec((None, 8, D), lambda i: (i // tps, 0, 0)), _const_spec((8, D)),
                  _const_spec((NDEV, D, MB)), _const_spec((D, D)), _const_spec((8, WA)),
                  _const_spec((NHEAD, CHUNK, CHUNK)), _const_spec((CHUNK, WA)), _const_spec((32, WA))],
        out_specs=[tile, pl.BlockSpec((tm, 4 * WA), lambda i: (i, 0)), tile, pl.BlockSpec((tm, WA), lambda i: (i, 0))],
        out_shape=[jax.ShapeDtypeStruct((T, D), F32), jax.ShapeDtypeStruct((T, 4 * WA), F32),
                   jax.ShapeDtypeStruct((T, D), F32), jax.ShapeDtypeStruct((T, WA), F32)],
        scratch=[pltpu.VMEM((NSLAB, HALO + ext_rows, LANES), F32), pltpu.VMEM((NSLAB, ext_rows, LANES), F32)],
        args=[x, mod, gvec, w_mi, w_mo, v512, ws, bias_full, cw])


def _mixer_bwd_a(dxo, ym, proj, conv, mod, gvec, w_mo, v512, ws, bias_full, esel, tm, name, jobs=()):
    T = dxo.shape[0]
    nt = T // tm
    nb = mod.shape[0]
    tps = nt // nb

    def core(ins, outs, scs):
        dxo_ref, ym_ref, proj_ref, conv_ref, mod_ref, g_ref, wmo_ref, v_ref, ws_ref, bias_ref, e_ref = ins
        dpart_ref, dymb_ref, ycat_ref, mg_ref, vg_ref, v5g_ref, gws_ref, gbs_ref = outs
        (dbs_acc,) = scs
        i = pl.program_id(0)
        dxo_v = dxo_ref[...]
        ymv = ym_ref[...]
        gt = mod_ref[2:3, :]
        gpost = g_ref[1:2, :]
        rm = lax.rsqrt(_rowmean(ymv * ymv) + EPS)
        ymh = ymv * rm
        d_gt = _colsum(dxo_v * (ymh * gpost))
        dpm = gt * dxo_v
        d_gpost = _colsum(dpm * ymh)
        dymh = dpm * gpost
        dym = (rm * (dymh - ymh * _rowmean(dymh * ymh))).astype(BF16)
        dymb_ref[...] = dym
        dycat = _dot_nt(dym, wmo_ref[...])
        u = proj_ref[:, 0:WA]
        v0 = proj_ref[:, WA:2 * WA]
        vh, rv = _layer_norm_stats(v0)
        vb = (vh * v_ref[0:1, :] + v_ref[1:2, :]).astype(BF16)
        wm = _masked_spatial(ws_ref)
        zs = []
        for q in range(tm // CHUNK):
            zs.append(_spatial_gate(wm, vb[q * CHUNK:(q + 1) * CHUNK, :]) + bias_ref[...])
        z = jnp.concatenate(zs, axis=0)
        ya = u * z
        ra = lax.rsqrt(_rowmean(ya * ya) + EPS)
        yah = ya * ra
        ch, rc = _layer_norm_stats(conv_ref[...])
        cn = ch * v_ref[3:4, :] + v_ref[4:5, :]
        sg = _sigmoid(cn)
        yb = cn * sg
        rb = lax.rsqrt(_rowmean(yb * yb) + EPS)
        ybh = yb * rb
        ycat_ref[...] = jnp.concatenate([yah * v_ref[5:6, :], ybh * v_ref[6:7, :]], axis=1).astype(BF16)
        dpa = dycat[:, 0:WA]
        dpb = dycat[:, WA:2 * WA]
        d_goa = _colsum(dpa * yah)
        d_gob = _colsum(dpb * ybh)
        dyah = dpa * v_ref[5:6, :]
        dybh = dpb * v_ref[6:7, :]
        dya = ra * (dyah - yah * _rowmean(dyah * yah))
        dyb = rb * (dybh - ybh * _rowmean(dybh * ybh))
        dpart_ref[:, 0:WA] = dya * z
        dz = dya * u

        @pl.when(i == 0)
        def _():
            gws_ref[...] = jnp.zeros((NHEAD, CHUNK, CHUNK), F32)
            dbs_acc[...] = jnp.zeros((CHUNK, WA), F32)
            vg_ref[...] = jnp.zeros((8, D), F32)
            v5g_ref[...] = jnp.zeros((8, WA), F32)

        first = lax.broadcasted_iota(jnp.int32, (CHUNK, LANES), 1) < HD
        dvs = []
        for q in range(tm // CHUNK):
            dz_q = dz[q * CHUNK:(q + 1) * CHUNK, :]
            vb_q = vb[q * CHUNK:(q + 1) * CHUNK, :]
            dbs_acc[...] += dz_q
            dzb = dz_q.astype(BF16)
            dvs.append(_head_pairs(wm, dzb, transpose=True))
            for hd in range(NHEAD):
                slab = dzb[:, _lanes(hd // 2)]
                dz_hd = jnp.where(first if hd % 2 == 0 else jnp.logical_not(first), slab, jnp.zeros_like(slab))
                gws_ref[hd] += _dot_nt(dz_hd, vb_q[:, _lanes(hd // 2)])
        dv = jnp.concatenate(dvs, axis=0)
        d_gng = _colsum(dv * vh)
        d_gnb = _colsum(dv)
        dvh = dv * v_ref[0:1, :]
        dpart_ref[:, WA:2 * WA] = rv * (dvh - _rowmean(dvh) - vh * _rowmean(dvh * vh))
        dcn = dyb * (sg * (1.0 + cn * (1.0 - sg)))
        d_cng = _colsum(dcn * ch)
        d_cnb = _colsum(dcn)
        dch = dcn * v_ref[3:4, :]
        dconv = rc * (dch - _rowmean(dch) - ch * _rowmean(dch * ch))
        dpart_ref[:, 2 * WA:3 * WA] = dconv
        dpart_ref[:, 3 * WA:4 * WA] = jnp.zeros((tm, WA), F32)
        d_cb = _colsum(dconv)

        @pl.when(i % tps == 0)
        def _():
            mg_ref[...] = jnp.zeros((8, D), F32)

        mg_ref[2:3, :] += d_gt
        vg_ref[1:2, :] += d_gpost
        v5g_ref[0:1, :] += d_gng
        v5g_ref[1:2, :] += d_gnb
        v5g_ref[2:3, :] += d_cb
        v5g_ref[3:4, :] += d_cng
        v5g_ref[4:5, :] += d_cnb
        v5g_ref[5:6, :] += d_goa
        v5g_ref[6:7, :] += d_gob

        @pl.when(i == nt - 1)
        def _():
            row = lax.broadcasted_iota(jnp.int32, (CHUNK, CHUNK), 0)
            col = lax.broadcasted_iota(jnp.int32, (CHUNK, CHUNK), 1)
            for hd in range(NHEAD):
                gws_ref[hd] = jnp.where(col <= row, gws_ref[hd], 0.0)
            gbs_ref[...] = lax.dot_general(e_ref[...], dbs_acc[...], (((1,), (1,)), ((), ())),
                                           precision=lax.Precision.HIGHEST, preferred_element_type=F32)

    tile = pl.BlockSpec((tm, D), lambda i: (i, 0))
    ptile = pl.BlockSpec((tm, 4 * WA), lambda i: (i, 0))
    return _call(
        core, name=name, grid=(nt,), jobs=jobs,
        in_specs=[tile, tile, pl.BlockSpec((tm, 2 * WA), lambda i: (i, 0)), pl.BlockSpec((tm, WA), lambda i: (i, 0)),
                  pl.BlockSpec((None, 8, D), lambda i: (i // tps, 0, 0)), _const_spec((8, D)), _const_spec((D, D)),
                  _const_spec((8, WA)), _const_spec((NHEAD, CHUNK, CHUNK)), _const_spec((CHUNK, WA)),
                  _const_spec((8, WA))],
        out_specs=[ptile, tile, tile, pl.BlockSpec((None, 8, D), lambda i: (i // tps, 0, 0)),
                   pl.BlockSpec((8, D), lambda i: (0, 0)), pl.BlockSpec((8, WA), lambda i: (0, 0)),
                   pl.BlockSpec((NHEAD, CHUNK, CHUNK), lambda i: (0, 0, 0)), pl.BlockSpec((8, CHUNK), lambda i: (0, 0))],
        out_shape=[jax.ShapeDtypeStruct((T, 4 * WA), F32), jax.ShapeDtypeStruct((T, D), BF16),
                   jax.ShapeDtypeStruct((T, D), BF16), jax.ShapeDtypeStruct((nb, 8, D), F32),
                   jax.ShapeDtypeStruct((8, D), F32), jax.ShapeDtypeStruct((8, WA), F32),
                   jax.ShapeDtypeStruct((NHEAD, CHUNK, CHUNK), F32), jax.ShapeDtypeStruct((8, CHUNK), F32)],
        scratch=[pltpu.VMEM((CHUNK, WA), F32)],
        args=[dxo, ym, proj, conv, mod, gvec, w_mo, v512, ws, bias_full, esel])


def _mixer_bwd_b(dxo, x, dpart, proj, mod, gvec, w_mi, cw, tm, name, jobs=()):
    T = x.shape[0]
    nt = T // tm
    nb = mod.shape[0]
    tps = nt // nb
    hpt = tm // HALO
    nh = T // HALO
    off = HALO - (CONV_K - 1)
    p = _pitch(tm)
    ext_rows = 8 * p

    def core(ins, outs, scs):
        dxo_ref, x_ref, dpart_ref, dnext_ref, ag_ref, halo_ref, mod_ref, g_ref, wmi_ref, cw_ref = ins
        dx_ref, dproj_ref, hb_ref, mg_ref, vg_ref, dcw_ref = outs
        glu_ext, dconv_ext, dglu_scr, dcw_acc = scs
        i = pl.program_id(0)
        first = i % tps == 0
        last = i % tps == tps - 1
        a = ag_ref[:, 0:WA]
        g = ag_ref[:, WA:2 * WA]
        sgg = _sigmoid(g)

        @pl.when(i == 0)
        def _():
            glu_ext[:, HALO + tm:HALO + ext_rows, :] = jnp.zeros((NSLAB, ext_rows - tm, LANES), F32)
            dconv_ext[:, HALO + tm:HALO + ext_rows, :] = jnp.zeros((NSLAB, ext_rows - tm, LANES), F32)
            dcw_acc[...] = jnp.zeros((32, 8, WA), F32)
            vg_ref[...] = jnp.zeros((8, D), F32)

        _to_slabs(glu_ext, 0, jnp.where(first, 0.0, halo_ref[:, 0:WA] * _sigmoid(halo_ref[:, WA:2 * WA])))
        _to_slabs(glu_ext, HALO, a * sgg)
        _to_slabs(dconv_ext, 0, dpart_ref[:, 2 * WA:3 * WA])
        _to_slabs(dconv_ext, tm, jnp.where(last, 0.0, dnext_ref[...]))
        sub = lax.broadcasted_iota(jnp.int32, (SUBL, LANES), 0)
        for s in range(NSLAB):
            accs = [jnp.zeros((SUBL, LANES), F32)] * CONV_K
            for v in range(p):
                dc = jnp.where(v + p * sub < tm, dconv_ext[s, pl.ds(v, 8, stride=p), :], 0.0)
                for k in range(CONV_K):
                    accs[k] = accs[k] + dc * glu_ext[s, pl.ds(v + off + k, 8, stride=p), :]
            for k in range(CONV_K):
                dcw_acc[k, :, _lanes(s)] += accs[k]
        dglu = _tap_sum(dconv_ext, dglu_scr, cw_ref, jnp.zeros((1, WA), F32), tm, lambda k: (CONV_K - 1) - k)

        @pl.when(i == nt - 1)
        def _():
            for k in range(CONV_K):
                dcw_ref[k:k + 1, :] = jnp.sum(dcw_acc[k], axis=0, keepdims=True)
            dcw_ref[CONV_K:32, :] = jnp.zeros((32 - CONV_K, WA), F32)

        da = dglu * sgg
        dgg = dglu * a * (sgg * (1.0 - sgg))
        dproj_ref[:, 0:2 * WA] = dpart_ref[:, 0:2 * WA].astype(BF16)
        dproj_ref[:, 2 * WA:3 * WA] = da.astype(BF16)
        dproj_ref[:, 3 * WA:4 * WA] = dgg.astype(BF16)
        dh = jnp.zeros((tm, D), F32)
        for j in range(NDEV):
            dh = dh + _dot_nt(dproj_ref[:, j * MB:(j + 1) * MB], wmi_ref[j])
        xv = x_ref[...]
        sc, sh = mod_ref[1:2, :], mod_ref[0:1, :]
        gpre = g_ref[0:1, :]
        r = lax.rsqrt(_rowmean(xv * xv) + EPS)
        xh = xv * r
        n = xh * gpre
        hb_ref[...] = (n * (1.0 + sc) + sh).astype(BF16)
        d_sc = _colsum(dh * n)
        d_sh = _colsum(dh)
        dn = dh * (1.0 + sc)
        d_gpre = _colsum(dn * xh)
        dxh = dn * gpre
        dx_ref[...] = dxo_ref[...] + r * (dxh - xh * _rowmean(dxh * xh))

        @pl.when(first)
        def _():
            mg_ref[...] = jnp.zeros((8, D), F32)

        mg_ref[0:1, :] += d_sh
        mg_ref[1:2, :] += d_sc
        vg_ref[0:1, :] += d_gpre

    tile = pl.BlockSpec((tm, D), lambda i: (i, 0))
    return _call(
        core, name=name, grid=(nt,), jobs=jobs,
        in_specs=[tile, tile, pl.BlockSpec((tm, 4 * WA), lambda i: (i, 0)),
                  pl.BlockSpec((HALO, WA), lambda i: (jnp.minimum((i + 1) * hpt, nh - 1), 2)),
                  pl.BlockSpec((tm, 2 * WA), lambda i: (i, 1)),
                  pl.BlockSpec((HALO, 2 * WA), lambda i: (jnp.maximum(i * hpt - 1, 0), 1)),
                  pl.BlockSpec((None, 8, D), lambda i: (i // tps, 0, 0)), _const_spec((8, D)),
                  _const_spec((NDEV, D, MB)), _const_spec((32, WA))],
        out_specs=[tile, pl.BlockSpec((tm, 4 * WA), lambda i: (i, 0)), tile,
                   pl.BlockSpec((None, 8, D), lambda i: (i // tps, 0, 0)), pl.BlockSpec((8, D), lambda i: (0, 0)),
                   pl.BlockSpec((32, WA), lambda i: (0, 0))],
        out_shape=[jax.ShapeDtypeStruct((T, D), F32), jax.ShapeDtypeStruct((T, 4 * WA), BF16),
                   jax.ShapeDtypeStruct((T, D), BF16), jax.ShapeDtypeStruct((nb, 8, D), F32),
                   jax.ShapeDtypeStruct((8, D), F32), jax.ShapeDtypeStruct((32, WA), F32)],
        scratch=[pltpu.VMEM((NSLAB, HALO + ext_rows, LANES), F32), pltpu.VMEM((NSLAB, HALO + ext_rows, LANES), F32),
                 pltpu.VMEM((NSLAB, ext_rows, LANES), F32), pltpu.VMEM((32, 8, WA), F32)],
        args=[dxo, x, dpart, dpart, proj, proj, mod, gvec, w_mi, cw])


def _grad_chip(a, b, a_spec, b_spec, prod_shape, half, name, jobs=(), via_b=False):
    steps = 8 if half is None else 4
    R = prod_shape[0] if half is None else half
    C = prod_shape[1]

    def core(ins, outs, scs):
        a_ref, b_ref = ins
        (o_ref,) = outs
        own, snd, rcv, ssem, rsem, lsem = scs
        s = pl.program_id(0)
        c = lax.axis_index("c")
        me = _me()
        sib = _flip(me, (0, 0, 1))
        if via_b:
            prod = _dot_tn(b_ref[...], a_ref[...]).T.astype(BF16)
        else:
            prod = _dot_tn(a_ref[...], b_ref[...]).astype(BF16)
        if half is None:
            q = s // 2

            @pl.when(s % 2 == c)
            def _():
                own[q] = prod

            @pl.when(s % 2 != c)
            def _():
                snd[q] = prod
                _remote(snd.at[q], rcv.at[q], ssem.at[q], rsem.at[q], sib).start()
        else:
            lo = prod[0:half, :]
            hi = prod[half:2 * half, :]
            own[s] = jnp.where(c == 0, lo, hi)
            snd[s] = jnp.where(c == 0, hi, lo)
            _remote(snd.at[s], rcv.at[s], ssem.at[s], rsem.at[s], sib).start()

        @pl.when(s == steps - 1)
        def _():
            for q4 in range(4):
                cp = _remote(snd.at[q4], rcv.at[q4], ssem.at[q4], rsem.at[q4], sib)
                cp.wait_recv()
                cp.wait_send()
                snd[q4] = (own[q4].astype(F32) + rcv[q4].astype(F32)).astype(BF16)
            out = pltpu.make_async_copy(snd, o_ref, lsem)
            out.start()
            out.wait()

    return _call(
        core, name=name, grid=(steps,), jobs=jobs, in_specs=[a_spec, b_spec], out_specs=[HBM],
        out_shape=[jax.ShapeDtypeStruct((4, R, C), BF16)],
        scratch=[pltpu.VMEM((4, R, C), BF16), pltpu.VMEM((4, R, C), BF16), pltpu.VMEM((4, R, C), BF16),
                 pltpu.SemaphoreType.DMA((4,)), pltpu.SemaphoreType.DMA((4,)), pltpu.SemaphoreType.DMA],
        args=[a, b])


def _grad_w_in(dg, hb, name, jobs=()):
    T = hb.shape[0]
    return _grad_chip(dg, hb, pl.BlockSpec((None, T, FB), lambda s: (s, 0, 0)), _const_spec((T, D)),
                      (FB, D), None, name, jobs)


def _grad_w_out(act, dyb, name, jobs=()):
    T = dyb.shape[0]
    return _grad_chip(act, dyb, pl.BlockSpec((None, T, FB), lambda s: (s, 0, 0)), _const_spec((T, D)),
                      (FB, D), FO, name, jobs)


def _grad_w_mi(hb, dproj, name, jobs=()):
    T = hb.shape[0]
    return _grad_chip(hb, dproj, _const_spec((T, D)), pl.BlockSpec((T, MB), lambda s: (0, s)),
                      (D, MB), None, name, jobs, via_b=True)


def _grad_w_mo(ycat, dym, name, jobs=()):
    T = ycat.shape[0]
    return _grad_chip(ycat, dym, pl.BlockSpec((T, 2 * MO), lambda s: (0, s)), _const_spec((T, D)),
                      (2 * MO, D), MO, name, jobs)


def _adamw_math(w, g, m, v):
    m2 = ADAM_B1 * m + (1.0 - ADAM_B1) * g
    v2 = ADAM_B2 * v + (1.0 - ADAM_B2) * (g * g)
    m_hat = m2 / (1.0 - ADAM_B1 ** ADAM_STEP)
    v_hat = v2 / (1.0 - ADAM_B2 ** ADAM_STEP)
    delta = -ADAM_LR * (m_hat / (jnp.sqrt(v_hat) + ADAM_EPS) + ADAM_WD * w)
    return delta, m2, v2


def _adamw_reduce(parts, w, m, v, tr, name, own=None, after=None):
    R, C = w.shape

    def core(ins, outs, _):
        p_ref, w_ref, m_ref, v_ref = ins[:4]
        g_ref, d_ref, m2_ref, v2_ref = outs
        if own is None:
            terms = [p_ref[s].astype(F32) for s in range(4)]
        else:
            mq = 2 * lax.axis_index("x") + lax.axis_index("y")
            mine = ins[4][...].astype(F32)
            terms = [jnp.where(mq == s, mine, p_ref[s].astype(F32)) for s in range(4)]
        g = terms[0]
        for s in range(1, 4):
            g = g + terms[s]
        g_ref[...] = g
        d_ref[...], m2_ref[...], v2_ref[...] = _adamw_math(w_ref[...], g, m_ref[...], v_ref[...])

    blk = pl.BlockSpec((tr, C), lambda i: (i, 0))
    in_specs = [pl.BlockSpec((4, tr, C), lambda i: (0, i, 0)), blk, blk, blk]
    args = [parts, w, m, v]
    if own is not None:
        mq = 2 * lax.axis_index("x") + lax.axis_index("y")
        in_specs.append(pl.BlockSpec((tr, C), lambda i: (i, 0)))
        args.append(lax.dynamic_index_in_dim(own, mq, 0, keepdims=False))
    if after is not None:
        in_specs.append(HBM)
        args.append(after)
    return _call(
        core, name=name, grid=(R // tr,), in_specs=in_specs,
        out_specs=[blk, blk, blk, blk], out_shape=[jax.ShapeDtypeStruct((R, C), F32)] * 4, args=args)[0]


HBM_ONLY = pl.BlockSpec(memory_space=pltpu.HBM)
SEM = pl.BlockSpec(memory_space=pltpu.SEMAPHORE)
EFFECT = pltpu.SideEffectType.DATAFLOW_SIDE_EFFECTING


def _chip_scatter_start(gs, name):
    n = len(gs)

    def body(*refs):
        g_refs, land_refs = refs[:n], refs[n:2 * n]
        ssem, rsem = refs[2 * n:2 * n + 2]
        token = refs[-1]
        me = _me()
        mq = 2 * me[0] + me[1]
        for k, f in enumerate(CHIP_FLIPS):
            p = _flip(me, f)
            for a in range(n):
                _remote(g_refs[a].at[2 * p[0] + p[1]], land_refs[a].at[mq], ssem.at[3 * a + k], rsem.at[3 * a + k], p).start()
        token[...] = jnp.zeros_like(token)

    gs = [pltpu.with_memory_space_constraint(g, pltpu.HBM) for g in gs]
    lands = [pltpu.with_memory_space_constraint(lax.empty(g.shape, g.dtype), pltpu.HBM) for g in gs]
    res = pl.pallas_call(
        body, name=name,
        out_shape=(pltpu.SemaphoreType.DMA((3 * n,)), pltpu.SemaphoreType.DMA((3 * n,)))
        + tuple(pltpu.HBM(g.shape, g.dtype) for g in gs) * 2 + (jax.ShapeDtypeStruct((SUBL, LANES), F32),),
        in_specs=(HBM_ONLY,) * (2 * n), out_specs=(SEM, SEM) + (HBM_ONLY,) * (2 * n) + (VM,),
        input_output_aliases={a: 2 + a for a in range(2 * n)},
        compiler_params=pltpu.CompilerParams(has_side_effects=EFFECT),
    )(*gs, *lands)
    return res[:-1], res[-1]


def _chip_scatter_wait(handle, after, name):
    ssem, rsem = handle[:2]
    n = (len(handle) - 2) // 2
    thru = handle[2:]

    def body(*refs):
        g_refs, land_refs = refs[:n], refs[n:2 * n]
        ssem, rsem = refs[2 * n:2 * n + 2]
        me = _me()
        mq = 2 * me[0] + me[1]
        for k, f in enumerate(CHIP_FLIPS):
            p = _flip(me, f)
            pq = 2 * p[0] + p[1]
            for a in range(n):
                _remote(g_refs[a].at[pq], land_refs[a].at[mq], ssem.at[3 * a + k], rsem.at[3 * a + k], p).wait_send()
                _remote(g_refs[a].at[mq], land_refs[a].at[pq], ssem.at[3 * a + k], rsem.at[3 * a + k], p).wait_recv()

    res = pl.pallas_call(
        body, name=name,
        out_shape=tuple(pltpu.HBM(t.shape, t.dtype) for t in thru),
        in_specs=(HBM_ONLY,) * (2 * n) + (SEM, SEM, HBM), out_specs=(HBM_ONLY,) * (2 * n),
        input_output_aliases={a: a for a in range(2 * n)},
        compiler_params=pltpu.CompilerParams(has_side_effects=EFFECT),
    )(*thru, ssem, rsem, after)
    return list(res[:n]), list(res[n:])


def _adamw_ada(sc_all, dd, w, m, v, tr, name, after=None):
    R, C = w.shape

    def core(ins, outs, _):
        sc_ref, dd_ref, w_ref, m_ref, v_ref = ins[:5]
        g_ref, d_ref, m2_ref, v2_ref = outs
        g = _dot_tn(sc_ref[...].astype(BF16), dd_ref[...].astype(BF16))
        g_ref[...] = g
        d_ref[...], m2_ref[...], v2_ref[...] = _adamw_math(w_ref[...], g, m_ref[...], v_ref[...])

    blk = pl.BlockSpec((tr, C), lambda i: (i, 0))
    return _call(
        core, name=name, grid=(R // tr,),
        in_specs=[pl.BlockSpec((64, tr), lambda i: (0, i)), pl.BlockSpec((64, C), lambda i: (0, 0)), blk, blk, blk]
        + [HBM] * (after is not None),
        out_specs=[blk, blk, blk, blk], out_shape=[jax.ShapeDtypeStruct((R, C), F32)] * 4,
        args=[sc_all, dd, w, m, v] + [after] * (after is not None))[0]


def _adamw_small(gathered, plain, grads, wmv, emit, name):
    nw = len(grads)
    ng, npl, ne = len(gathered), len(plain), len(emit)

    def core(ins, outs, _):
        srcs = []
        for a in range(ng):
            s = ins[a][0]
            for dev in range(1, NDEV):
                s = s + ins[a][dev]
            srcs.append(s)
        srcs += [ins[ng + a][...] for a in range(npl)]
        w_refs = ins[ng + npl:]
        for e, a in enumerate(emit):
            outs[e][...] = srcs[a]
        for t in range(nw):
            src, row = grads[t]
            g = srcs[src] if row is None else srcs[src][row:row + 1, :]
            w_ref, m_ref, v_ref = w_refs[3 * t:3 * t + 3]
            g_ref, d_ref, m2_ref, v2_ref = outs[ne + 4 * t:ne + 4 * t + 4]
            g_ref[...] = g
            d_ref[...], m2_ref[...], v2_ref[...] = _adamw_math(w_ref[...], g, m_ref[...], v_ref[...])

    out_shape = [jax.ShapeDtypeStruct(gathered[a].shape[1:], F32) for a in emit]
    for t in range(nw):
        out_shape += [jax.ShapeDtypeStruct(wmv[3 * t].shape, F32)] * 4
    return _call(
        core, name=name, grid=(), in_specs=[VM] * (ng + npl + 3 * nw), out_specs=[VM] * (ne + 4 * nw),
        out_shape=out_shape, args=list(gathered) + list(plain) + list(wmv))[0]


def _ada_fwd(c_pad, w_ada, b_cols, cw_pad, jobs=()):
    def core(ins, outs, scs, start_jobs):
        c_ref, w_ref, b_ref, cwp_ref = ins
        ada_ref, sc_ref, cw_ref = outs
        cbuf, send_buf, ssem, rsem = scs
        me = _me()
        mi = _lin(me)
        cbuf[mi] = c_ref[...]
        cw_ref[mi] = cwp_ref[...]
        peers = [_flip(me, f) for f in FLIPS]
        first = []
        for k, p in enumerate(peers):
            first.append(_remote(cbuf.at[mi], cbuf.at[mi], ssem.at[k], rsem.at[k], p))
            first.append(_remote(cw_ref.at[mi], cw_ref.at[mi], ssem.at[7 + k], rsem.at[7 + k], p))
        for cp in first:
            cp.start()
        for k, p in enumerate(peers):
            pi = _lin(p)
            _remote(cbuf.at[pi], cbuf.at[pi], ssem.at[k], rsem.at[k], p).wait_recv()
            _remote(cw_ref.at[pi], cw_ref.at[pi], ssem.at[7 + k], rsem.at[7 + k], p).wait_recv()
        c_all = cbuf[...].reshape(8 * 8, D)
        sc = c_all * _sigmoid(c_all)
        sc_ref[...] = sc
        res = _dot(sc.astype(BF16), w_ref[...].astype(BF16)) + b_ref[...]
        send_buf[...] = res.reshape(8, 8, ADA_B)
        ada_ref[mi] = send_buf[mi]
        second = []
        for k, p in enumerate(peers):
            second.append(_remote(send_buf.at[_lin(p)], ada_ref.at[mi], ssem.at[14 + k], rsem.at[14 + k], p))
        for cp in second:
            cp.start()
        start_jobs()
        for k, p in enumerate(peers):
            _remote(send_buf.at[mi], ada_ref.at[_lin(p)], ssem.at[14 + k], rsem.at[14 + k], p).wait_recv()
        for cp in first + second:
            cp.wait_send()

    return _call(
        core, name="ada_fwd", grid=(), jobs=jobs, core_starts=True, in_specs=[VM, VM, VM, VM], out_specs=[VM, VM, VM],
        out_shape=[jax.ShapeDtypeStruct((8, 8, ADA_B), F32), jax.ShapeDtypeStruct((64, D), F32),
                   jax.ShapeDtypeStruct((8, 32, 64), F32)],
        scratch=[pltpu.VMEM((8, 8, D), F32), pltpu.VMEM((8, 8, ADA_B), F32),
                 pltpu.SemaphoreType.DMA((21,)), pltpu.SemaphoreType.DMA((21,))],
        args=[c_pad, w_ada, b_cols, cw_pad])


def _ada_bwd(dada, jobs=()):
    def core(ins, outs, scs):
        (d_ref,) = ins
        dd_ref, gb_ref = outs
        rbuf, ssem, rsem = scs
        me = _me()
        mi = _lin(me)
        peers = [_flip(me, f) for f in FLIPS]
        rbuf[mi] = d_ref[mi]
        first = []
        for k, p in enumerate(peers):
            first.append(_remote(d_ref.at[_lin(p)], rbuf.at[mi], ssem.at[k], rsem.at[k], p))
        for cp in first:
            cp.start()
        for k, p in enumerate(peers):
            _remote(d_ref.at[mi], rbuf.at[_lin(p)], ssem.at[k], rsem.at[k], p).wait_recv()
        dd = rbuf[...].reshape(64, ADA_B)
        dd_ref[...] = dd
        gb_ref[mi] = jnp.broadcast_to(_colsum(dd), (8, ADA_B))
        second = []
        for k, p in enumerate(peers):
            second.append(_remote(gb_ref.at[mi], gb_ref.at[mi], ssem.at[7 + k], rsem.at[7 + k], p))
        for cp in second:
            cp.start()
        for k, p in enumerate(peers):
            pi = _lin(p)
            _remote(gb_ref.at[pi], gb_ref.at[pi], ssem.at[7 + k], rsem.at[7 + k], p).wait_recv()
        for cp in first + second:
            cp.wait_send()

    return _call(
        core, name="ada_bwd", grid=(), jobs=jobs, in_specs=[VM], out_specs=[VM, VM],
        out_shape=[jax.ShapeDtypeStruct((64, ADA_B), F32), jax.ShapeDtypeStruct((8, 8, ADA_B), F32)],
        scratch=[pltpu.VMEM((8, 8, ADA_B), F32), pltpu.SemaphoreType.DMA((14,)), pltpu.SemaphoreType.DMA((14,))],
        args=[dada])


SMALL_D = ("g_pre_f1", "g_post_f1", "g_pre_m", "g_post_m", "g_pre_f2", "g_post_f2")
SMALL_W = ("gmlp_norm_g", "gmlp_norm_b", "conv_b", "conv_norm_g", "conv_norm_b", "g_out_a", "g_out_b")


def kernel(x, c, w_ada, b_ada, g_pre_f1, g_post_f1, w_f1_in, w_f1_out, g_pre_m, g_post_m, w_mix_in, gmlp_norm_g, gmlp_norm_b, w_spatial, b_spatial, conv_w, conv_b, conv_norm_g, conv_norm_b, g_out_a, g_out_b, w_mix_out, g_pre_f2, g_post_f2, w_f2_in, w_f2_out, loss_target, m_w_ada, m_b_ada, m_g_pre_f1, m_g_post_f1, m_w_f1_in, m_w_f1_out, m_g_pre_m, m_g_post_m, m_w_mix_in, m_gmlp_norm_g, m_gmlp_norm_b, m_w_spatial, m_b_spatial, m_conv_w, m_conv_b, m_conv_norm_g, m_conv_norm_b, m_g_out_a, m_g_out_b, m_w_mix_out, m_g_pre_f2, m_g_post_f2, m_w_f2_in, m_w_f2_out, v_w_ada, v_b_ada, v_g_pre_f1, v_g_post_f1, v_w_f1_in, v_w_f1_out, v_g_pre_m, v_g_post_m, v_w_mix_in, v_gmlp_norm_g, v_gmlp_norm_b, v_w_spatial, v_b_spatial, v_conv_w, v_conv_b, v_conv_norm_g, v_conv_norm_b, v_g_out_a, v_g_out_b, v_w_mix_out, v_g_pre_f2, v_g_post_f2, v_w_f2_in, v_w_f2_out):
    given = dict(locals())
    bl, seq, _ = x.shape
    T = bl * seq
    tm = min(256, seq // 2)
    mi = _lin((lax.axis_index("x"), lax.axis_index("y"), lax.axis_index("c")))

    def shard_in(w):
        return w[0].T.astype(BF16)

    g_f1 = _RelayGather([shard_in(w_f1_in), w_f1_out[0].astype(BF16)], ("rows", "out"))
    s_f2 = shard_in(w_f2_in)
    g_mx = _Gather([w_mix_in[0].astype(BF16), w_mix_out[0].astype(BF16), w_f2_out[0].astype(BF16), s_f2[:, 0:D // 4]],
                   ("rows", "rows", "out", "rows"), late_mid=True)
    g_f2 = _Gather([s_f2[:, D // 4:D]], ("rows",))

    c_pad = jnp.pad(c, ((0, 8 - bl), (0, 0)))
    b_cols = lax.dynamic_slice(b_ada, (0, mi * ADA_B), (1, ADA_B))
    cw_pad = jnp.pad(conv_w[0], ((0, 1), (0, 0)))
    (ada_blk, sc_all, cw_all), ((wi1, wo1),) = _ada_fwd(c_pad, w_ada[0], b_cols, cw_pad, jobs=[g_f1])
    ada = ada_blk[:, 0:bl, :].transpose(1, 0, 2).reshape(bl, 9, D)
    pad5 = jnp.zeros((bl, 5, D), F32)
    mod1 = jnp.concatenate([ada[:, 0:3], pad5], axis=1)
    mod2 = jnp.concatenate([ada[:, 3:6], pad5], axis=1)
    mod3 = jnp.concatenate([ada[:, 6:9], pad5], axis=1)
    cw_full = cw_all.transpose(1, 0, 2).reshape(32, WA)

    zrow = jnp.zeros((1, D), F32)
    gv1 = jnp.concatenate([g_pre_f1, g_post_f1] + [zrow] * 6, axis=0)
    gvm = jnp.concatenate([g_pre_m, g_post_m] + [zrow] * 6, axis=0)
    gv2 = jnp.concatenate([g_pre_f2, g_post_f2] + [zrow] * 6, axis=0)
    v512 = jnp.concatenate([gmlp_norm_g, gmlp_norm_b, conv_b, conv_norm_g, conv_norm_b, g_out_a, g_out_b,
                            jnp.zeros((1, WA), F32)], axis=0)
    ws = w_spatial[0]
    bias_full = jnp.repeat(b_spatial[0].T, HD, axis=1)
    esel = (lax.broadcasted_iota(jnp.int32, (8, WA), 1) // HD == lax.broadcasted_iota(jnp.int32, (8, WA), 0)).astype(F32)

    x0 = x.reshape(T, D)
    (x1, gu1, y1), ((wmi, wmo, wo2, wi2a),) = _ffn_fwd(x0, mod1, gv1, wi1, wo1, tm, "ffn1_fwd", jobs=[g_mx])
    wmo = wmo.reshape(D, D)
    (x2, proj, ym, conv), ((wi2b,),) = _mixer_fwd(x1, mod2, gvm, wmi, wmo, v512, ws, bias_full, cw_full, tm, "mixer_fwd", jobs=[g_f2])

    (dx2, dg2, act2, hb2, dyb2, mg3, vg3, loss_blk), _ = _ffn_last(
        x2, loss_target.reshape(T, D), mod3, gv2, (wi2a, wi2b), wo2, tm, "ffn2_fwd_bwd")
    (g_wi2,), _ = _grad_w_in(dg2, hb2, "ffn2_gw_in")
    (g_wo2,), _ = _grad_w_out(act2, dyb2, "ffn2_gw_out")
    (dpart, dymb, ycat, mg2a, vgma, v5g, gws, gbs), ((p_wo2,),) = _mixer_bwd_a(
        dx2, ym, proj, conv, mod2, gvm, wmo, v512, ws, bias_full, esel, tm, "mixer_bwd_a",
        jobs=[_ChipScatter([g_wo2])])
    (dx1, dproj, hbm, mg2b, vgmb, dcw), ((p_wi2,),) = _mixer_bwd_b(
        dx2, x1, dpart, proj, mod2, gvm, wmi, cw_full, tm, "mixer_bwd_b", jobs=[_ChipScatter([g_wi2])])
    (g_wmi,), _ = _grad_w_mi(hbm, dproj, "mixer_gw_in")
    (g_wmo,), _ = _grad_w_mo(ycat, dymb, "mixer_gw_out")
    p2 = jnp.concatenate([v5g, dcw], axis=0)
    (dx0, dg1, act1, hb1, dyb1, mg1, vg1), _ = _ffn_bwd(dx1, x0, y1, gu1, mod1, gv1, wi1, wo1, tm, "ffn1_bwd")

    dada = jnp.concatenate([mg1[:, 0:3], mg2b[:, 0:2], mg2a[:, 2:3], mg3[:, 0:3]], axis=1)
    dada = dada.reshape(bl, NDEV, ADA_B).transpose(1, 0, 2)
    dada = jnp.pad(dada, ((0, 0), (0, 8 - bl), (0, 0)))
    p1 = jnp.concatenate([vg1[0:2], vgmb[0:1], vgma[1:2], vg3[0:2], loss_blk[0:1], zrow], axis=0)
    (dd_all, gb_all), ((a1,),) = _ada_bwd(dada, jobs=[_AllGather([p1])])
    g_bada = gb_all[:, 0, :].reshape(1, 9 * D)

    (g_wo1,), ((p_wmi, p_wmo),) = _grad_w_out(act1, dyb1, "ffn1_gw_out", jobs=[_ChipScatter([g_wmi, g_wmo])])
    (g_wi1,), ((a2, a3, a4), (p_wo1,)) = _grad_w_in(
        dg1, hb1, "ffn1_gw_in", jobs=[_Gather([p2, gws, gbs], ("rows",) * 3), _ChipScatter([g_wo1])])

    h_f1, token = _chip_scatter_start([g_wi1], "tail_start")

    res = {}
    quad = _adamw_reduce(p_wi2, w_f2_in[0].T, m_w_f2_in[0].T, v_w_f2_in[0].T, FO, "adamw_w_f2_in", after=token)
    res["w_f2_in"] = tuple(t.T[None] for t in quad)
    for nm, part, tr in (("w_f2_out", p_wo2, FO), ("w_mix_in", p_wmi, 256), ("w_mix_out", p_wmo, MO), ("w_f1_out", p_wo1, FO)):
        quad = _adamw_reduce(part, given[nm][0], given["m_" + nm][0], given["v_" + nm][0], tr, "adamw_" + nm, after=quad[1])
        res[nm] = tuple(t[None] for t in quad)
    quad = _adamw_ada(sc_all, dd_all, w_ada[0], m_w_ada[0], v_w_ada[0], 256, "adamw_w_ada", after=quad[1])
    res["w_ada"] = tuple(t[None] for t in quad)
    (g_wi1,), (p_wi1,) = _chip_scatter_wait(h_f1, quad[1], "tail_wait")
    quad = _adamw_reduce(p_wi1, w_f1_in[0].T, m_w_f1_in[0].T, v_w_f1_in[0].T, FO, "adamw_w_f1_in", own=g_wi1)
    res["w_f1_in"] = tuple(t.T[None] for t in quad)

    small = SMALL_D + SMALL_W + ("w_spatial", "b_spatial", "b_ada")
    grads = [(0, r) for r in range(6)] + [(1, r) for r in range(7)] + [(2, None), (3, None), (4, None)]
    wmv = []
    for nm in small:
        for pre in ("", "m_", "v_"):
            wmv.append(given[pre + nm][0] if nm in ("w_spatial", "b_spatial") else given[pre + nm])
    outs = _adamw_small([a1, a2, a3, a4], [g_bada], grads, wmv, (0, 1), "adamw_small")
    loss = outs[0][6, 0]
    for t, nm in enumerate(small):
        quad = outs[2 + 4 * t:6 + 4 * t]
        res[nm] = tuple(q[None] for q in quad) if nm in ("w_spatial", "b_spatial") else tuple(quad)
    g_cw = lax.dynamic_slice(outs[1], (8, mi * 64), (32, 64))
    wmv = [jnp.pad(given[pre + "conv_w"][0], ((0, 1), (0, 0)), constant_values=1.0 if pre == "v_" else 0.0)
           for pre in ("", "m_", "v_")]
    quad = _adamw_small([], [g_cw], [(0, None)], wmv, (), "adamw_conv_w")
    res["conv_w"] = tuple(q[0:CONV_K][None] for q in quad)

    order = ["w_ada", "b_ada", "g_pre_f1", "g_post_f1", "w_f1_in", "w_f1_out", "g_pre_m", "g_post_m", "w_mix_in",
             "gmlp_norm_g", "gmlp_norm_b", "w_spatial", "b_spatial", "conv_w", "conv_b", "conv_norm_g", "conv_norm_b",
             "g_out_a", "g_out_b", "w_mix_out", "g_pre_f2", "g_post_f2", "w_f2_in", "w_f2_out"]
    out = [loss, dx0.reshape(bl, seq, D)]
    for k in range(4):
        out += [res[nm][k] for nm in order]
    return tuple(out)
```

```python
import jax
import jax.numpy as jnp
from jax import lax
from jax.experimental import pallas as pl
from jax.experimental.pallas import tpu as pltpu

F32 = jnp.float32
BF16 = jnp.bfloat16

D = 1024
DFF = 2816
NDEV = 8
FB = 2 * DFF // NDEV
NCH = DFF // FB
LANES = 128
SUBL = 8
FO = DFF // NDEV
WA = 512
NSLAB = WA // LANES
NHEAD = 8
HD = 64
CHUNK = 128
CONV_K = 31
HALO = 32
MB = 2 * (WA + WA) // NDEV
MO = D // NDEV
ADA_B = 9 * D // NDEV
EPS = 1e-6
HALF = 0.5

ADAM_LR = 0.001
ADAM_B1 = 0.9
ADAM_B2 = 0.999
ADAM_EPS = 1e-08
ADAM_WD = 0.01
ADAM_STEP = 10

VMEM_LIMIT = 56 * 1024 * 1024
MESH = pl.DeviceIdType.MESH
FLIPS = ((0, 0, 1), (1, 0, 0), (0, 1, 0), (1, 1, 0), (1, 0, 1), (0, 1, 1), (1, 1, 1))
CHIP_FLIPS = ((1, 0, 0), (0, 1, 0), (1, 1, 0))
HBM = pl.BlockSpec(memory_space=pl.ANY)
VM = pl.BlockSpec(memory_space=pltpu.VMEM)


def _dot(a, b):
    return lax.dot_general(a, b, (((1,), (0,)), ((), ())), preferred_element_type=F32)


def _dot_nt(a, b):
    return lax.dot_general(a, b, (((1,), (1,)), ((), ())), preferred_element_type=F32)


def _dot_tn(a, b):
    return lax.dot_general(a, b, (((0,), (0,)), ((), ())), preferred_element_type=F32)


def _rowmean(v):
    return jnp.mean(v, axis=-1, keepdims=True)


def _colsum(v):
    return jnp.sum(v, axis=0, keepdims=True)


def _sigmoid(v):
    return 0.5 * jnp.tanh(0.5 * v) + 0.5


def _const_spec(shape):
    nd = len(shape)
    return pl.BlockSpec(shape, lambda *_: (0,) * nd, pipeline_mode=pl.Buffered(1))


def _me():
    return lax.axis_index("x"), lax.axis_index("y"), lax.axis_index("c")


def _flip(me, f):
    return tuple(1 - v if b else v for v, b in zip(me, f))


def _lin(p):
    return 4 * p[0] + 2 * p[1] + p[2]


def _remote(src, dst, send_sem, recv_sem, dev):
    return pltpu.make_async_remote_copy(src_ref=src, dst_ref=dst, send_sem=send_sem, recv_sem=recv_sem,
                                        device_id=dev, device_id_type=MESH)


def _blk(kind, ref, p):
    if kind == "out":
        return ref.at[2 * p[0] + p[1], pl.ds(p[2] * FO, FO), :]
    return ref.at[_lin(p)]


class _Gather:
    def __init__(self, shards, kinds, late_mid=False):
        self.late_mid = late_mid
        self.kinds = kinds
        self.n = len(shards)
        self.ins = list(shards)
        self.out_shape = [jax.ShapeDtypeStruct((4, FB, D) if k == "out" else (NDEV,) + s.shape, s.dtype)
                          for s, k in zip(shards, kinds)]
        self.sems = [pltpu.SemaphoreType.DMA((7 * self.n,)), pltpu.SemaphoreType.DMA((7 * self.n,)),
                     pltpu.SemaphoreType.DMA((self.n,))]

    def _first(self, ins, outs, sems):
        ssem, rsem, lsem = sems
        me = _me()
        sib = _flip(me, (0, 0, 1))
        cps, loc = [], []
        for a in range(self.n):
            mine = _blk(self.kinds[a], outs[a], me)
            loc.append(pltpu.make_async_copy(ins[a], mine, lsem.at[a]))
            cps.append(_remote(ins[a], mine, ssem.at[7 * a], rsem.at[7 * a], sib))
            for j, f in enumerate(CHIP_FLIPS):
                cps.append(_remote(ins[a], mine, ssem.at[7 * a + 1 + j], rsem.at[7 * a + 1 + j], _flip(me, f)))
        return cps, loc

    def _passed(self, outs, sems):
        ssem, rsem, _ = sems
        me = _me()
        sib = _flip(me, (0, 0, 1))
        cps = []
        for j, f in enumerate(CHIP_FLIPS):
            for a in range(self.n):
                blk = _blk(self.kinds[a], outs[a], _flip(me, f))
                cps.append(_remote(blk, blk, ssem.at[7 * a + 4 + j], rsem.at[7 * a + 4 + j], sib))
        return cps

    def start(self, ins, outs, sems):
        cps, loc = self._first(ins, outs, sems)
        for cp in loc + cps:
            cp.start()

    def mid(self, ins, outs, sems):
        ssem, rsem, _ = sems
        me = _me()
        passed = self._passed(outs, sems)
        t = 0
        for j, f in enumerate(CHIP_FLIPS):
            for a in range(self.n):
                blk = _blk(self.kinds[a], outs[a], _flip(me, f))
                _remote(blk, blk, ssem.at[7 * a + 1 + j], rsem.at[7 * a + 1 + j], _flip(me, f)).wait_recv()
                passed[t].start()
                t += 1

    def end(self, ins, outs, sems):
        ssem, rsem, _ = sems
        me = _me()
        sib = _flip(me, (0, 0, 1))
        for a in range(self.n):
            blk = _blk(self.kinds[a], outs[a], sib)
            _remote(blk, blk, ssem.at[7 * a], rsem.at[7 * a], sib).wait_recv()
            for j, f in enumerate(CHIP_FLIPS):
                blk = _blk(self.kinds[a], outs[a], _flip(_flip(me, f), (0, 0, 1)))
                _remote(blk, blk, ssem.at[7 * a + 4 + j], rsem.at[7 * a + 4 + j], sib).wait_recv()
        cps, loc = self._first(ins, outs, sems)
        for cp in cps + self._passed(outs, sems):
            cp.wait_send()
        for cp in loc:
            cp.wait()


class _RelayGather(_Gather):
    def _peers(self):
        me = _me()
        c = me[2]
        to = (me[0] + (1 - c) - 2 * me[0] * (1 - c), me[1] + c - 2 * me[1] * c, c)
        frm = (me[0] + c - 2 * me[0] * c, me[1] + (1 - c) - 2 * me[1] * (1 - c), c)
        return me, _flip(me, (0, 0, 1)), to, frm, _flip(me, (1, 1, 0))

    def _first(self, ins, outs, sems):
        ssem, rsem, lsem = sems
        me, sib, to, frm, _ = self._peers()
        cps, loc = [], []
        for a in range(self.n):
            mine = _blk(self.kinds[a], outs[a], me)
            loc.append(pltpu.make_async_copy(ins[a], mine, lsem.at[a]))
            for slot, dev in ((0, sib), (1, to), (2, frm)):
                cps.append(_remote(ins[a], mine, ssem.at[7 * a + slot], rsem.at[7 * a + slot], dev))
        return cps, loc

    def _block_copy(self, outs, sems, a, slot, owner, dev):
        ssem, rsem, _ = sems
        blk = _blk(self.kinds[a], outs[a], owner)
        return _remote(blk, blk, ssem.at[7 * a + slot], rsem.at[7 * a + slot], dev)

    def mid(self, ins, outs, sems):
        me, sib, to, frm, _ = self._peers()
        for a in range(self.n):
            self._block_copy(outs, sems, a, 2, frm, frm).wait_recv()
            self._block_copy(outs, sems, a, 3, frm, to).start()
            self._block_copy(outs, sems, a, 5, frm, sib).start()
        for a in range(self.n):
            self._block_copy(outs, sems, a, 1, to, to).wait_recv()
            self._block_copy(outs, sems, a, 4, to, sib).start()

    def end(self, ins, outs, sems):
        me, sib, to, frm, far = self._peers()
        up = (0, 0, 1)
        for a in range(self.n):
            self._block_copy(outs, sems, a, 3, far, to).wait_recv()
            self._block_copy(outs, sems, a, 6, far, sib).start()
        for a in range(self.n):
            for slot, owner in ((0, sib), (4, _flip(frm, up)), (5, _flip(to, up)), (6, _flip(far, up))):
                self._block_copy(outs, sems, a, slot, owner, sib).wait_recv()
        cps, loc = self._first(ins, outs, sems)
        for a in range(self.n):
            cps += [self._block_copy(outs, sems, a, 3, frm, to), self._block_copy(outs, sems, a, 4, to, sib),
                    self._block_copy(outs, sems, a, 5, frm, sib), self._block_copy(outs, sems, a, 6, far, sib)]
        for cp in cps:
            cp.wait_send()
        for cp in loc:
            cp.wait()


class _ChipScatter:
    def __init__(self, grads):
        self.n = len(grads)
        self.ins = list(grads)
        self.out_shape = [jax.ShapeDtypeStruct(g.shape, BF16) for g in grads]
        self.sems = [pltpu.SemaphoreType.DMA((3 * self.n,)), pltpu.SemaphoreType.DMA((3 * self.n,)),
                     pltpu.SemaphoreType.DMA((self.n,))]

    def _copies(self, ins, outs, sems):
        ssem, rsem, lsem = sems
        me = _me()
        mq = 2 * me[0] + me[1]
        loc = [pltpu.make_async_copy(ins[a].at[mq], outs[a].at[mq], lsem.at[a]) for a in range(self.n)]
        cps = []
        for k, f in enumerate(CHIP_FLIPS):
            p = _flip(me, f)
            for a in range(self.n):
                cps.append(_remote(ins[a].at[2 * p[0] + p[1]], outs[a].at[mq], ssem.at[3 * a + k], rsem.at[3 * a + k], p))
        return cps, loc

    def start(self, ins, outs, sems):
        cps, loc = self._copies(ins, outs, sems)
        for cp in loc + cps:
            cp.start()

    mid = None

    def end(self, ins, outs, sems):
        ssem, rsem, _ = sems
        me = _me()
        mq = 2 * me[0] + me[1]
        for k, f in enumerate(CHIP_FLIPS):
            p = _flip(me, f)
            for a in range(self.n):
                _remote(ins[a].at[mq], outs[a].at[2 * p[0] + p[1]], ssem.at[3 * a + k], rsem.at[3 * a + k], p).wait_recv()
        cps, loc = self._copies(ins, outs, sems)
        for cp in cps:
            cp.wait_send()
        for cp in loc:
            cp.wait()


class _AllGather:
    def __init__(self, parts):
        self.n = len(parts)
        self.ins = list(parts)
        self.out_shape = [jax.ShapeDtypeStruct((NDEV,) + p.shape, p.dtype) for p in parts]
        self.sems = [pltpu.SemaphoreType.DMA((7 * self.n,)), pltpu.SemaphoreType.DMA((7 * self.n,)),
                     pltpu.SemaphoreType.DMA((self.n,))]

    def _copies(self, ins, outs, sems):
        ssem, rsem, lsem = sems
        me = _me()
        mi = _lin(me)
        loc = [pltpu.make_async_copy(ins[a], outs[a].at[mi], lsem.at[a]) for a in range(self.n)]
        cps = []
        for k, f in enumerate(FLIPS):
            for a in range(self.n):
                cps.append(_remote(ins[a], outs[a].at[mi], ssem.at[7 * a + k], rsem.at[7 * a + k], _flip(me, f)))
        return cps, loc

    def start(self, ins, outs, sems):
        cps, loc = self._copies(ins, outs, sems)
        for cp in loc + cps:
            cp.start()

    mid = None

    def end(self, ins, outs, sems):
        ssem, rsem, _ = sems
        me = _me()
        for k, f in enumerate(FLIPS):
            p = _flip(me, f)
            for a in range(self.n):
                _remote(ins[a], outs[a].at[_lin(p)], ssem.at[7 * a + k], rsem.at[7 * a + k], p).wait_recv()
        cps, loc = self._copies(ins, outs, sems)
        for cp in cps:
            cp.wait_send()
        for cp in loc:
            cp.wait()


def _call(core, *, name, grid, in_specs, out_specs, out_shape, args, scratch=(), jobs=(), core_starts=False):
    n_in, n_out, n_sc = len(in_specs), len(out_specs), len(scratch)
    steps = 1
    for g in grid:
        steps *= g

    def body(*refs):
        pos = [0]

        def take(k):
            r = refs[pos[0]:pos[0] + k]
            pos[0] += k
            return r

        ins = take(n_in)
        j_ins = [take(len(j.ins)) for j in jobs]
        outs = take(n_out)
        j_outs = [take(len(j.out_shape)) for j in jobs]
        scs = take(n_sc)
        j_sems = [take(len(j.sems)) for j in jobs]
        if len(grid) == 2:
            step = pl.program_id(0) * grid[1] + pl.program_id(1)
        elif len(grid) == 1:
            step = pl.program_id(0)
        else:
            step = 0
        def start_jobs():
            for j, ji, jo, js in zip(jobs, j_ins, j_outs, j_sems):
                j.start(ji, jo, js)

        if grid:
            pl.when(step == 0)(start_jobs)
        elif not core_starts:
            start_jobs()
        for j, ji, jo, js in zip(jobs, j_ins, j_outs, j_sems):
            if j.mid is not None and grid:
                at = max(steps - 2, 0) if j.late_mid else (3 * steps) // 4
                pl.when(step == at)(lambda j=j, ji=ji, jo=jo, js=js: j.mid(ji, jo, js))
        def finish_jobs():
            for j, ji, jo, js in zip(jobs, j_ins, j_outs, j_sems):
                if j.mid is not None:
                    j.mid(ji, jo, js)
                j.end(ji, jo, js)

        if core_starts:
            core(ins, outs, scs, start_jobs, finish_jobs)
        elif core is not None:
            core(ins, outs, scs)
        if grid:
            for j, ji, jo, js in zip(jobs, j_ins, j_outs, j_sems):
                pl.when(step == steps - 1)(lambda j=j, ji=ji, jo=jo, js=js: j.end(ji, jo, js))
        elif not core_starts:
            finish_jobs()

    all_in = list(in_specs)
    all_args = list(args)
    all_out = list(out_specs)
    all_shape = list(out_shape)
    all_sc = list(scratch)
    for j in jobs:
        all_in += [HBM] * len(j.ins)
        all_args += j.ins
    for j in jobs:
        all_out += [HBM] * len(j.out_shape)
        all_shape += j.out_shape
        all_sc += j.sems
    params = dict(vmem_limit_bytes=VMEM_LIMIT)
    if grid:
        params["dimension_semantics"] = ("arbitrary",) * len(grid)
    res = pl.pallas_call(
        body, name=name, grid=grid, in_specs=all_in, out_specs=all_out, out_shape=all_shape,
        scratch_shapes=all_sc, compiler_params=pltpu.CompilerParams(**params),
    )(*all_args)
    core_res = list(res[:n_out])
    job_res = []
    pos = n_out
    for j in jobs:
        job_res.append(list(res[pos:pos + len(j.out_shape)]))
        pos += len(j.out_shape)
    return core_res, job_res


def _ffn_fwd(x, mod, gvec, w_in, w_out, tm, name, jobs=()):
    T = x.shape[0]
    nt = T // tm
    tps = nt // mod.shape[0]

    def core(ins, outs, _):
        x_ref, mod_ref, g_ref, win_ref, wout_ref = ins
        xo_ref, gu_ref, y_ref = outs
        xv = x_ref[...]
        sh, sc, gt = mod_ref[0:1, :], mod_ref[1:2, :], mod_ref[2:3, :]
        r = lax.rsqrt(_rowmean(xv * xv) + EPS)
        h = (xv * r * g_ref[0:1, :]) * (1.0 + sc) + sh
        hb = h.astype(BF16)
        y = jnp.zeros((tm, D), F32)
        for cidx in range(NCH):
            gate = _dot_nt(hb, win_ref[cidx])
            up = _dot_nt(hb, win_ref[NCH + cidx])
            gu_ref[cidx] = gate.astype(BF16)
            gu_ref[NCH + cidx] = up.astype(BF16)
            act = gate * _sigmoid(gate) * up
            y = y + _dot(act.astype(BF16), wout_ref[cidx])
        y_ref[...] = y
        ry = lax.rsqrt(_rowmean(y * y) + EPS)
        xo_ref[...] = xv + (HALF * gt) * (y * ry * g_ref[1:2, :])

    tile = pl.BlockSpec((tm, D), lambda i: (i, 0))
    return _call(
        core, name=name, grid=(nt,), jobs=jobs,
        in_specs=[tile, pl.BlockSpec((None, 8, D), lambda i: (i // tps, 0, 0)), _const_spec((8, D)),
                  _const_spec((8, FB, D)), _const_spec((4, FB, D))],
        out_specs=[tile, pl.BlockSpec((8, tm, FB), lambda i: (0, i, 0)), tile],
        out_shape=[jax.ShapeDtypeStruct((T, D), F32), jax.ShapeDtypeStruct((8, T, FB), BF16),
                   jax.ShapeDtypeStruct((T, D), F32)],
        args=[x, mod, gvec, w_in, w_out])


def _ffn_bwd(dxo, x, y, gu, mod, gvec, w_in, w_out, tm, name, jobs=()):
    T = x.shape[0]
    nt = T // tm
    nb = mod.shape[0]
    tps = nt // nb

    def core(ins, outs, _):
        dxo_ref, x_ref, y_ref, gu_ref, mod_ref, g_ref, win_ref, wout_ref = ins
        dx_ref, dg_ref, act_ref, hb_ref, dyb_ref, mg_ref, vg_ref = outs
        i = pl.program_id(0)
        xv = x_ref[...]
        dxo_v = dxo_ref[...]
        yv = y_ref[...]
        sh, sc, gt = mod_ref[0:1, :], mod_ref[1:2, :], mod_ref[2:3, :]
        gpre, gpost = g_ref[0:1, :], g_ref[1:2, :]
        r = lax.rsqrt(_rowmean(xv * xv) + EPS)
        xh = xv * r
        n = xh * gpre
        hb = (n * (1.0 + sc) + sh).astype(BF16)
        hb_ref[...] = hb
        ry = lax.rsqrt(_rowmean(yv * yv) + EPS)
        yh = yv * ry
        d_gt = _colsum(HALF * dxo_v * (yh * gpost))
        dp = (HALF * gt) * dxo_v
        d_gpost = _colsum(dp * yh)
        dyh = dp * gpost
        dy = ry * (dyh - yh * _rowmean(dyh * yh))
        dyb = dy.astype(BF16)
        dyb_ref[...] = dyb
        dh = jnp.zeros((tm, D), F32)
        for cidx in range(NCH):
            gate = gu_ref[cidx].astype(F32)
            up = gu_ref[NCH + cidx].astype(F32)
            sig = _sigmoid(gate)
            s = gate * sig
            act_ref[cidx] = (s * up).astype(BF16)
            d_act = _dot_nt(dyb, wout_ref[cidx])
            d_up = (d_act * s).astype(BF16)
            d_gate = (d_act * up * (sig * (1.0 + gate * (1.0 - sig)))).astype(BF16)
            dg_ref[cidx] = d_gate
            dg_ref[NCH + cidx] = d_up
            dh = dh + _dot(d_gate, win_ref[cidx]) + _dot(d_up, win_ref[NCH + cidx])
        d_sc = _colsum(dh * n)
        d_sh = _colsum(dh)
        dn = dh * (1.0 + sc)
        d_gpre = _colsum(dn * xh)
        dxh = dn * gpre
        dx_ref[...] = dxo_v + r * (dxh - xh * _rowmean(dxh * xh))

        @pl.when(i % tps == 0)
        def _():
            mg_ref[...] = jnp.zeros((8, D), F32)

        @pl.when(i == 0)
        def _():
            vg_ref[...] = jnp.zeros((8, D), F32)

        mg_ref[0:1, :] += d_sh
        mg_ref[1:2, :] += d_sc
        mg_ref[2:3, :] += d_gt
        vg_ref[0:1, :] += d_gpre
        vg_ref[1:2, :] += d_gpost

    tile = pl.BlockSpec((tm, D), lambda i: (i, 0))
    return _call(
        core, name=name, grid=(nt,), jobs=jobs,
        in_specs=[tile, tile, tile, pl.BlockSpec((8, tm, FB), lambda i: (0, i, 0)),
                  pl.BlockSpec((None, 8, D), lambda i: (i // tps, 0, 0)), _const_spec((8, D)),
                  _const_spec((8, FB, D)), _const_spec((4, FB, D))],
        out_specs=[tile, pl.BlockSpec((8, tm, FB), lambda i: (0, i, 0)),
                   pl.BlockSpec((4, tm, FB), lambda i: (0, i, 0)), tile, tile,
                   pl.BlockSpec((None, 8, D), lambda i: (i // tps, 0, 0)), pl.BlockSpec((8, D), lambda i: (0, 0))],
        out_shape=[jax.ShapeDtypeStruct((T, D), F32), jax.ShapeDtypeStruct((8, T, FB), BF16),
                   jax.ShapeDtypeStruct((4, T, FB), BF16), jax.ShapeDtypeStruct((T, D), BF16),
                   jax.ShapeDtypeStruct((T, D), BF16), jax.ShapeDtypeStruct((nb, 8, D), F32),
                   jax.ShapeDtypeStruct((8, D), F32)],
        args=[dxo, x, y, gu, mod, gvec, w_in, w_out])


def _ffn_last(x, target, mod, gvec, w_in, w_out, tm, name, jobs=()):
    T = x.shape[0]
    nt = T // tm
    nb = mod.shape[0]
    tps = nt // nb

    def core(ins, outs, scs):
        x_ref, t_ref, mod_ref, g_ref, wina_ref, winb_ref, wout_ref = ins
        dx_ref, dg_ref, act_ref, hb_ref, dyb_ref, mg_ref, vg_ref, loss_ref = outs
        hd2 = w_in[0].shape[2]
        (gu_s,) = scs
        i = pl.program_id(0)
        xv = x_ref[...]
        sh, sc, gt = mod_ref[0:1, :], mod_ref[1:2, :], mod_ref[2:3, :]
        gpre, gpost = g_ref[0:1, :], g_ref[1:2, :]
        r = lax.rsqrt(_rowmean(xv * xv) + EPS)
        xh = xv * r
        n = xh * gpre
        hb = (n * (1.0 + sc) + sh).astype(BF16)
        hb_ref[...] = hb
        hba, hbb = hb[:, 0:hd2], hb[:, hd2:D]
        yv = jnp.zeros((tm, D), F32)
        for cidx in range(NCH):
            gate = _dot_nt(hba, wina_ref[cidx]) + _dot_nt(hbb, winb_ref[cidx])
            up = _dot_nt(hba, wina_ref[NCH + cidx]) + _dot_nt(hbb, winb_ref[NCH + cidx])
            gu_s[cidx] = gate.astype(BF16)
            gu_s[NCH + cidx] = up.astype(BF16)
            act = gate * _sigmoid(gate) * up
            act_ref[cidx] = act.astype(BF16)
            yv = yv + _dot(act_ref[cidx], wout_ref[cidx])
        ry = lax.rsqrt(_rowmean(yv * yv) + EPS)
        yh = yv * ry
        pn = yh * gpost
        err = xv + (HALF * gt) * pn - t_ref[...]
        dxo_v = err * (1.0 / D)
        d_gt = _colsum(HALF * dxo_v * pn)
        dp = (HALF * gt) * dxo_v
        d_gpost = _colsum(dp * yh)
        dyh = dp * gpost
        dyb = (ry * (dyh - yh * _rowmean(dyh * yh))).astype(BF16)
        dyb_ref[...] = dyb
        dha = jnp.zeros((tm, hd2), F32)
        dhb = jnp.zeros((tm, D - hd2), F32)
        for cidx in range(NCH):
            gate = gu_s[cidx].astype(F32)
            up = gu_s[NCH + cidx].astype(F32)
            sig = _sigmoid(gate)
            s = gate * sig
            d_act = _dot_nt(dyb, wout_ref[cidx])
            d_up = (d_act * s).astype(BF16)
            d_gate = (d_act * up * (sig * (1.0 + gate * (1.0 - sig)))).astype(BF16)
            dg_ref[cidx] = d_gate
            dg_ref[NCH + cidx] = d_up
            dha = dha + _dot(d_gate, wina_ref[cidx]) + _dot(d_up, wina_ref[NCH + cidx])
            dhb = dhb + _dot(d_gate, winb_ref[cidx]) + _dot(d_up, winb_ref[NCH + cidx])
        dh = jnp.concatenate([dha, dhb], axis=1)
        d_sc = _colsum(dh * n)
        d_sh = _colsum(dh)
        dn = dh * (1.0 + sc)
        d_gpre = _colsum(dn * xh)
        dxh = dn * gpre
        dx_ref[...] = dxo_v + r * (dxh - xh * _rowmean(dxh * xh))

        @pl.when(i % tps == 0)
        def _():
            mg_ref[...] = jnp.zeros((8, D), F32)

        @pl.when(i == 0)
        def _():
            vg_ref[...] = jnp.zeros((8, D), F32)
            loss_ref[...] = jnp.zeros((8, D), F32)

        mg_ref[0:1, :] += d_sh
        mg_ref[1:2, :] += d_sc
        mg_ref[2:3, :] += d_gt
        vg_ref[0:1, :] += d_gpre
        vg_ref[1:2, :] += d_gpost
        loss_ref[...] += HALF * jnp.sum(_rowmean(err * err), axis=0, keepdims=True)

    tile = pl.BlockSpec((tm, D), lambda i: (i, 0))
    return _call(
        core, name=name, grid=(nt,), jobs=jobs,
        in_specs=[tile, tile, pl.BlockSpec((None, 8, D), lambda i: (i // tps, 0, 0)), _const_spec((8, D)),
                  _const_spec(w_in[0].shape), _const_spec(w_in[1].shape), _const_spec((4, FB, D))],
        out_specs=[tile, pl.BlockSpec((8, tm, FB), lambda i: (0, i, 0)),
                   pl.BlockSpec((4, tm, FB), lambda i: (0, i, 0)), tile, tile,
                   pl.BlockSpec((None, 8, D), lambda i: (i // tps, 0, 0)), pl.BlockSpec((8, D), lambda i: (0, 0)),
                   pl.BlockSpec((8, D), lambda i: (0, 0))],
        out_shape=[jax.ShapeDtypeStruct((T, D), F32), jax.ShapeDtypeStruct((8, T, FB), BF16),
                   jax.ShapeDtypeStruct((4, T, FB), BF16), jax.ShapeDtypeStruct((T, D), BF16),
                   jax.ShapeDtypeStruct((T, D), BF16), jax.ShapeDtypeStruct((nb, 8, D), F32),
                   jax.ShapeDtypeStruct((8, D), F32), jax.ShapeDtypeStruct((8, D), F32)],
        scratch=[pltpu.VMEM((8, tm, FB), BF16)],
        args=[x, target, mod, gvec, w_in[0], w_in[1], w_out])


def _masked_spatial(ws_ref):
    row = lax.broadcasted_iota(jnp.int32, (CHUNK, CHUNK), 0)
    col = lax.broadcasted_iota(jnp.int32, (CHUNK, CHUNK), 1)
    keep = col <= row
    return [jnp.where(keep, ws_ref[hd], 0.0).astype(BF16) for hd in range(NHEAD)]


def _head_pairs(mats, right, transpose=False):
    first = lax.broadcasted_iota(jnp.int32, (CHUNK, LANES), 1) < HD
    op = _dot_tn if transpose else _dot
    out = []
    for p in range(NHEAD // 2):
        slab = right[:, _lanes(p)]
        out.append(jnp.where(first, op(mats[2 * p], slab), op(mats[2 * p + 1], slab)))
    return jnp.concatenate(out, axis=1)


def _spatial_gate(wm, vb_chunk):
    return _head_pairs(wm, vb_chunk)


def _layer_norm_stats(v):
    mu = _rowmean(v)
    vc = v - mu
    rstd = lax.rsqrt(_rowmean(vc * vc) + EPS)
    return vc * rstd, rstd


def _pitch(tm):
    p = tm // 8
    while p % 8 != 4:
        p += 1
    return p


def _lanes(s):
    return slice(s * LANES, (s + 1) * LANES)


def _to_slabs(ref, row0, val):
    for s in range(NSLAB):
        ref[s, row0:row0 + val.shape[0], :] = val[:, _lanes(s)]


def _tap_sum(src, out, cw_ref, bias, tm, start):
    p = _pitch(tm)
    for s in range(NSLAB):
        accs = [jnp.broadcast_to(bias[:, _lanes(s)], (SUBL, LANES))] * p
        for k in range(CONV_K):
            w = jnp.broadcast_to(cw_ref[k:k + 1, _lanes(s)], (SUBL, LANES))
            for v in range(p):
                accs[v] = accs[v] + w * src[s, pl.ds(v + start(k), 8, stride=p), :]
        for v in range(p):
            out[s, pl.ds(v, 8, stride=p), :] = accs[v]
    return jnp.concatenate([out[s, 0:tm, :] for s in range(NSLAB)], axis=1)


def _mixer_fwd(x, mod, gvec, w_mi, w_mo, v512, ws, bias_full, cw, tm, name, jobs=()):
    T = x.shape[0]
    nt = T // tm
    tps = nt // mod.shape[0]
    ext_rows = 8 * _pitch(tm)

    def core(ins, outs, scs):
        x_ref, mod_ref, g_ref, wmi_ref, wmo_ref, v_ref, ws_ref, bias_ref, cw_ref = ins
        xo_ref, proj_ref, ym_ref, conv_ref = outs
        glu_ext, conv_scr = scs
        i = pl.program_id(0)
        xv = x_ref[...]
        sh, sc, gt = mod_ref[0:1, :], mod_ref[1:2, :], mod_ref[2:3, :]
        r = lax.rsqrt(_rowmean(xv * xv) + EPS)
        hb = ((xv * r * g_ref[0:1, :]) * (1.0 + sc) + sh).astype(BF16)
        for j in range(NDEV):
            proj_ref[:, j * MB:(j + 1) * MB] = _dot(hb, wmi_ref[j])
        u = proj_ref[:, 0:WA]
        v0 = proj_ref[:, WA:2 * WA]
        a = proj_ref[:, 2 * WA:3 * WA]
        g = proj_ref[:, 3 * WA:4 * WA]
        vh, _ = _layer_norm_stats(v0)
        vb = (vh * v_ref[0:1, :] + v_ref[1:2, :]).astype(BF16)
        wm = _masked_spatial(ws_ref)
        ya = []
        for q in range(tm // CHUNK):
            z = _spatial_gate(wm, vb[q * CHUNK:(q + 1) * CHUNK, :]) + bias_ref[...]
            ya.append(u[q * CHUNK:(q + 1) * CHUNK, :] * z)
        ya = jnp.concatenate(ya, axis=0)
        glu = a * _sigmoid(g)

        @pl.when(i == 0)
        def _():
            glu_ext[:, HALO + tm:HALO + ext_rows, :] = jnp.zeros((NSLAB, ext_rows - tm, LANES), F32)

        @pl.when(i % tps == 0)
        def _():
            glu_ext[:, 0:HALO, :] = jnp.zeros((NSLAB, HALO, LANES), F32)

        _to_slabs(glu_ext, HALO, glu)
        conv = _tap_sum(glu_ext, conv_scr, cw_ref, v_ref[2:3, :], tm, lambda k: HALO - (CONV_K - 1) + k)
        conv_ref[...] = conv
        glu_ext[:, 0:HALO, :] = glu_ext[:, tm:tm + HALO, :]
        ch, _ = _layer_norm_stats(conv)
        cn = ch * v_ref[3:4, :] + v_ref[4:5, :]
        yb = cn * _sigmoid(cn)
        pa = ya * lax.rsqrt(_rowmean(ya * ya) + EPS) * v_ref[5:6, :]
        pb = yb * lax.rsqrt(_rowmean(yb * yb) + EPS) * v_ref[6:7, :]
        ycat = jnp.concatenate([pa, pb], axis=1).astype(BF16)
        ym = _dot(ycat, wmo_ref[...])
        ym_ref[...] = ym
        rm = lax.rsqrt(_rowmean(ym * ym) + EPS)
        xo_ref[...] = xv + gt * (ym * rm * g_ref[1:2, :])

    tile = pl.BlockSpec((tm, D), lambda i: (i, 0))
    return _call(
        core, name=name, grid=(nt,), jobs=jobs,
        in_specs=[tile, pl.BlockSpec((None, 8, D), lambda i: (i // tps, 0, 0)), _const_spec((8, D)),
                  _const_spec((NDEV, D, MB)), _const_spec((D, D)), _const_spec((8, WA)),
                  _const_spec((NHEAD, CHUNK, CHUNK)), _const_spec((CHUNK, WA)), _const_spec((32, WA))],
        out_specs=[tile, pl.BlockSpec((tm, 4 * WA), lambda i: (i, 0)), tile, pl.BlockSpec((tm, WA), lambda i: (i, 0))],
        out_shape=[jax.ShapeDtypeStruct((T, D), F32), jax.ShapeDtypeStruct((T, 4 * WA), F32),
                   jax.ShapeDtypeStruct((T, D), F32), jax.ShapeDtypeStruct((T, WA), F32)],
        scratch=[pltpu.VMEM((NSLAB, HALO + ext_rows, LANES), F32), pltpu.VMEM((NSLAB, ext_rows, LANES), F32)],
        args=[x, mod, gvec, w_mi, w_mo, v512, ws, bias_full, cw])


def _mixer_bwd_a(dxo, ym, proj, conv, mod, gvec, w_mo, v512, ws, bias_full, esel, tm, name, jobs=()):
    T = dxo.shape[0]
    nt = T // tm
    nb = mod.shape[0]
    tps = nt // nb

    def core(ins, outs, scs):
        dxo_ref, ym_ref, proj_ref, conv_ref, mod_ref, g_ref, wmo_ref, v_ref, ws_ref, bias_ref, e_ref = ins
        dpart_ref, dymb_ref, ycat_ref, mg_ref, vg_ref, v5g_ref, gws_ref, gbs_ref = outs
        (dbs_acc,) = scs
        i = pl.program_id(0)
        dxo_v = dxo_ref[...]
        ymv = ym_ref[...]
        gt = mod_ref[2:3, :]
        gpost = g_ref[1:2, :]
        rm = lax.rsqrt(_rowmean(ymv * ymv) + EPS)
        ymh = ymv * rm
        d_gt = _colsum(dxo_v * (ymh * gpost))
        dpm = gt * dxo_v
        d_gpost = _colsum(dpm * ymh)
        dymh = dpm * gpost
        dym = (rm * (dymh - ymh * _rowmean(dymh * ymh))).astype(BF16)
        dymb_ref[...] = dym
        dycat = _dot_nt(dym, wmo_ref[...])
        u = proj_ref[:, 0:WA]
        v0 = proj_ref[:, WA:2 * WA]
        vh, rv = _layer_norm_stats(v0)
        vb = (vh * v_ref[0:1, :] + v_ref[1:2, :]).astype(BF16)
        wm = _masked_spatial(ws_ref)
        zs = []
        for q in range(tm // CHUNK):
            zs.append(_spatial_gate(wm, vb[q * CHUNK:(q + 1) * CHUNK, :]) + bias_ref[...])
        z = jnp.concatenate(zs, axis=0)
        ya = u * z
        ra = lax.rsqrt(_rowmean(ya * ya) + EPS)
        yah = ya * ra
        ch, rc = _layer_norm_stats(conv_ref[...])
        cn = ch * v_ref[3:4, :] + v_ref[4:5, :]
        sg = _sigmoid(cn)
        yb = cn * sg
        rb = lax.rsqrt(_rowmean(yb * yb) + EPS)
        ybh = yb * rb
        ycat_ref[...] = jnp.concatenate([yah * v_ref[5:6, :], ybh * v_ref[6:7, :]], axis=1).astype(BF16)
        dpa = dycat[:, 0:WA]
        dpb = dycat[:, WA:2 * WA]
        d_goa = _colsum(dpa * yah)
        d_gob = _colsum(dpb * ybh)
        dyah = dpa * v_ref[5:6, :]
        dybh = dpb * v_ref[6:7, :]
        dya = ra * (dyah - yah * _rowmean(dyah * yah))
        dyb = rb * (dybh - ybh * _rowmean(dybh * ybh))
        dpart_ref[:, 0:WA] = dya * z
        dz = dya * u

        @pl.when(i == 0)
        def _():
            gws_ref[...] = jnp.zeros((NHEAD, CHUNK, CHUNK), F32)
            dbs_acc[...] = jnp.zeros((CHUNK, WA), F32)
            vg_ref[...] = jnp.zeros((8, D), F32)
            v5g_ref[...] = jnp.zeros((8, WA), F32)

        first = lax.broadcasted_iota(jnp.int32, (CHUNK, LANES), 1) < HD
        dvs = []
        for q in range(tm // CHUNK):
            dz_q = dz[q * CHUNK:(q + 1) * CHUNK, :]
            vb_q = vb[q * CHUNK:(q + 1) * CHUNK, :]
            dbs_acc[...] += dz_q
            dzb = dz_q.astype(BF16)
            dvs.append(_head_pairs(wm, dzb, transpose=True))
            for hd in range(NHEAD):
                slab = dzb[:, _lanes(hd // 2)]
                dz_hd = jnp.where(first if hd % 2 == 0 else jnp.logical_not(first), slab, jnp.zeros_like(slab))
                gws_ref[hd] += _dot_nt(dz_hd, vb_q[:, _lanes(hd // 2)])
        dv = jnp.concatenate(dvs, axis=0)
        d_gng = _colsum(dv * vh)
        d_gnb = _colsum(dv)
        dvh = dv * v_ref[0:1, :]
        dpart_ref[:, WA:2 * WA] = rv * (dvh - _rowmean(dvh) - vh * _rowmean(dvh * vh))
        dcn = dyb * (sg * (1.0 + cn * (1.0 - sg)))
        d_cng = _colsum(dcn * ch)
        d_cnb = _colsum(dcn)
        dch = dcn * v_ref[3:4, :]
        dconv = rc * (dch - _rowmean(dch) - ch * _rowmean(dch * ch))
        dpart_ref[:, 2 * WA:3 * WA] = dconv
        dpart_ref[:, 3 * WA:4 * WA] = jnp.zeros((tm, WA), F32)
        d_cb = _colsum(dconv)

        @pl.when(i % tps == 0)
        def _():
            mg_ref[...] = jnp.zeros((8, D), F32)

        mg_ref[2:3, :] += d_gt
        vg_ref[1:2, :] += d_gpost
        v5g_ref[0:1, :] += d_gng
        v5g_ref[1:2, :] += d_gnb
        v5g_ref[2:3, :] += d_cb
        v5g_ref[3:4, :] += d_cng
        v5g_ref[4:5, :] += d_cnb
        v5g_ref[5:6, :] += d_goa
        v5g_ref[6:7, :] += d_gob

        @pl.when(i == nt - 1)
        def _():
            row = lax.broadcasted_iota(jnp.int32, (CHUNK, CHUNK), 0)
            col = lax.broadcasted_iota(jnp.int32, (CHUNK, CHUNK), 1)
            for hd in range(NHEAD):
                gws_ref[hd] = jnp.where(col <= row, gws_ref[hd], 0.0)
            gbs_ref[...] = lax.dot_general(e_ref[...], dbs_acc[...], (((1,), (1,)), ((), ())),
                                           precision=lax.Precision.HIGHEST, preferred_element_type=F32)

    tile = pl.BlockSpec((tm, D), lambda i: (i, 0))
    ptile = pl.BlockSpec((tm, 4 * WA), lambda i: (i, 0))
    return _call(
        core, name=name, grid=(nt,), jobs=jobs,
        in_specs=[tile, tile, pl.BlockSpec((tm, 2 * WA), lambda i: (i, 0)), pl.BlockSpec((tm, WA), lambda i: (i, 0)),
                  pl.BlockSpec((None, 8, D), lambda i: (i // tps, 0, 0)), _const_spec((8, D)), _const_spec((D, D)),
                  _const_spec((8, WA)), _const_spec((NHEAD, CHUNK, CHUNK)), _const_spec((CHUNK, WA)),
                  _const_spec((8, WA))],
        out_specs=[ptile, tile, tile, pl.BlockSpec((None, 8, D), lambda i: (i // tps, 0, 0)),
                   pl.BlockSpec((8, D), lambda i: (0, 0)), pl.BlockSpec((8, WA), lambda i: (0, 0)),
                   pl.BlockSpec((NHEAD, CHUNK, CHUNK), lambda i: (0, 0, 0)), pl.BlockSpec((8, CHUNK), lambda i: (0, 0))],
        out_shape=[jax.ShapeDtypeStruct((T, 4 * WA), F32), jax.ShapeDtypeStruct((T, D), BF16),
                   jax.ShapeDtypeStruct((T, D), BF16), jax.ShapeDtypeStruct((nb, 8, D), F32),
                   jax.ShapeDtypeStruct((8, D), F32), jax.ShapeDtypeStruct((8, WA), F32),
                   jax.ShapeDtypeStruct((NHEAD, CHUNK, CHUNK), F32), jax.ShapeDtypeStruct((8, CHUNK), F32)],
        scratch=[pltpu.VMEM((CHUNK, WA), F32)],
        args=[dxo, ym, proj, conv, mod, gvec, w_mo, v512, ws, bias_full, esel])


def _mixer_bwd_b(dxo, x, dpart, proj, mod, gvec, w_mi, cw, tm, name, jobs=()):
    T = x.shape[0]
    nt = T // tm
    nb = mod.shape[0]
    tps = nt // nb
    hpt = tm // HALO
    nh = T // HALO
    off = HALO - (CONV_K - 1)
    p = _pitch(tm)
    ext_rows = 8 * p

    def core(ins, outs, scs):
        dxo_ref, x_ref, dpart_ref, dnext_ref, ag_ref, halo_ref, mod_ref, g_ref, wmi_ref, cw_ref = ins
        dx_ref, dproj_ref, hb_ref, mg_ref, vg_ref, dcw_ref = outs
        glu_ext, dconv_ext, dglu_scr, dcw_acc = scs
        i = pl.program_id(0)
        first = i % tps == 0
        last = i % tps == tps - 1
        a = ag_ref[:, 0:WA]
        g = ag_ref[:, WA:2 * WA]
        sgg = _sigmoid(g)

        @pl.when(i == 0)
        def _():
            glu_ext[:, HALO + tm:HALO + ext_rows, :] = jnp.zeros((NSLAB, ext_rows - tm, LANES), F32)
            dconv_ext[:, HALO + tm:HALO + ext_rows, :] = jnp.zeros((NSLAB, ext_rows - tm, LANES), F32)
            dcw_acc[...] = jnp.zeros((32, 8, WA), F32)
            vg_ref[...] = jnp.zeros((8, D), F32)

        _to_slabs(glu_ext, 0, jnp.where(first, 0.0, halo_ref[:, 0:WA] * _sigmoid(halo_ref[:, WA:2 * WA])))
        _to_slabs(glu_ext, HALO, a * sgg)
        _to_slabs(dconv_ext, 0, dpart_ref[:, 2 * WA:3 * WA])
        _to_slabs(dconv_ext, tm, jnp.where(last, 0.0, dnext_ref[...]))
        sub = lax.broadcasted_iota(jnp.int32, (SUBL, LANES), 0)
        for s in range(NSLAB):
            accs = [jnp.zeros((SUBL, LANES), F32)] * CONV_K
            for v in range(p):
                dc = jnp.where(v + p * sub < tm, dconv_ext[s, pl.ds(v, 8, stride=p), :], 0.0)
                for k in range(CONV_K):
                    accs[k] = accs[k] + dc * glu_ext[s, pl.ds(v + off + k, 8, stride=p), :]
            for k in range(CONV_K):
                dcw_acc[k, :, _lanes(s)] += accs[k]
        dglu = _tap_sum(dconv_ext, dglu_scr, cw_ref, jnp.zeros((1, WA), F32), tm, lambda k: (CONV_K - 1) - k)

        @pl.when(i == nt - 1)
        def _():
            for k in range(CONV_K):
                dcw_ref[k:k + 1, :] = jnp.sum(dcw_acc[k], axis=0, keepdims=True)
            dcw_ref[CONV_K:32, :] = jnp.zeros((32 - CONV_K, WA), F32)

        da = dglu * sgg
        dgg = dglu * a * (sgg * (1.0 - sgg))
        dproj_ref[:, 0:2 * WA] = dpart_ref[:, 0:2 * WA].astype(BF16)
        dproj_ref[:, 2 * WA:3 * WA] = da.astype(BF16)
        dproj_ref[:, 3 * WA:4 * WA] = dgg.astype(BF16)
        dh = jnp.zeros((tm, D), F32)
        for j in range(NDEV):
            dh = dh + _dot_nt(dproj_ref[:, j * MB:(j + 1) * MB], wmi_ref[j])
        xv = x_ref[...]
        sc, sh = mod_ref[1:2, :], mod_ref[0:1, :]
        gpre = g_ref[0:1, :]
        r = lax.rsqrt(_rowmean(xv * xv) + EPS)
        xh = xv * r
        n = xh * gpre
        hb_ref[...] = (n * (1.0 + sc) + sh).astype(BF16)
        d_sc = _colsum(dh * n)
        d_sh = _colsum(dh)
        dn = dh * (1.0 + sc)
        d_gpre = _colsum(dn * xh)
        dxh = dn * gpre
        dx_ref[...] = dxo_ref[...] + r * (dxh - xh * _rowmean(dxh * xh))

        @pl.when(first)
        def _():
            mg_ref[...] = jnp.zeros((8, D), F32)

        mg_ref[0:1, :] += d_sh
        mg_ref[1:2, :] += d_sc
        vg_ref[0:1, :] += d_gpre

    tile = pl.BlockSpec((tm, D), lambda i: (i, 0))
    return _call(
        core, name=name, grid=(nt,), jobs=jobs,
        in_specs=[tile, tile, pl.BlockSpec((tm, 4 * WA), lambda i: (i, 0)),
                  pl.BlockSpec((HALO, WA), lambda i: (jnp.minimum((i + 1) * hpt, nh - 1), 2)),
                  pl.BlockSpec((tm, 2 * WA), lambda i: (i, 1)),
                  pl.BlockSpec((HALO, 2 * WA), lambda i: (jnp.maximum(i * hpt - 1, 0), 1)),
                  pl.BlockSpec((None, 8, D), lambda i: (i // tps, 0, 0)), _const_spec((8, D)),
                  _const_spec((NDEV, D, MB)), _const_spec((32, WA))],
        out_specs=[tile, pl.BlockSpec((tm, 4 * WA), lambda i: (i, 0)), tile,
                   pl.BlockSpec((None, 8, D), lambda i: (i // tps, 0, 0)), pl.BlockSpec((8, D), lambda i: (0, 0)),
                   pl.BlockSpec((32, WA), lambda i: (0, 0))],
        out_shape=[jax.ShapeDtypeStruct((T, D), F32), jax.ShapeDtypeStruct((T, 4 * WA), BF16),
                   jax.ShapeDtypeStruct((T, D), BF16), jax.ShapeDtypeStruct((nb, 8, D), F32),
                   jax.ShapeDtypeStruct((8, D), F32), jax.ShapeDtypeStruct((32, WA), F32)],
        scratch=[pltpu.VMEM((NSLAB, HALO + ext_rows, LANES), F32), pltpu.VMEM((NSLAB, HALO + ext_rows, LANES), F32),
                 pltpu.VMEM((NSLAB, ext_rows, LANES), F32), pltpu.VMEM((32, 8, WA), F32)],
        args=[dxo, x, dpart, dpart, proj, proj, mod, gvec, w_mi, cw])


def _grad_chip(a, b, a_spec, b_spec, prod_shape, half, name, jobs=(), via_b=False):
    steps = 8 if half is None else 4
    R = prod_shape[0] if half is None else half
    C = prod_shape[1]

    def core(ins, outs, scs):
        a_ref, b_ref = ins
        (o_ref,) = outs
        own, snd, rcv, ssem, rsem, lsem = scs
        s = pl.program_id(0)
        c = lax.axis_index("c")
        me = _me()
        sib = _flip(me, (0, 0, 1))
        if via_b:
            prod = _dot_tn(b_ref[...], a_ref[...]).T.astype(BF16)
        else:
            prod = _dot_tn(a_ref[...], b_ref[...]).astype(BF16)
        if half is None:
            q = s // 2

            @pl.when(s % 2 == c)
            def _():
                own[q] = prod

            @pl.when(s % 2 != c)
            def _():
                snd[q] = prod
                _remote(snd.at[q], rcv.at[q], ssem.at[q], rsem.at[q], sib).start()
        else:
            lo = prod[0:half, :]
            hi = prod[half:2 * half, :]
            own[s] = jnp.where(c == 0, lo, hi)
            snd[s] = jnp.where(c == 0, hi, lo)
            _remote(snd.at[s], rcv.at[s], ssem.at[s], rsem.at[s], sib).start()

        @pl.when(s == steps - 1)
        def _():
            for q4 in range(4):
                cp = _remote(snd.at[q4], rcv.at[q4], ssem.at[q4], rsem.at[q4], sib)
                cp.wait_recv()
                cp.wait_send()
                snd[q4] = (own[q4].astype(F32) + rcv[q4].astype(F32)).astype(BF16)
            out = pltpu.make_async_copy(snd, o_ref, lsem)
            out.start()
            out.wait()

    return _call(
        core, name=name, grid=(steps,), jobs=jobs, in_specs=[a_spec, b_spec], out_specs=[HBM],
        out_shape=[jax.ShapeDtypeStruct((4, R, C), BF16)],
        scratch=[pltpu.VMEM((4, R, C), BF16), pltpu.VMEM((4, R, C), BF16), pltpu.VMEM((4, R, C), BF16),
                 pltpu.SemaphoreType.DMA((4,)), pltpu.SemaphoreType.DMA((4,)), pltpu.SemaphoreType.DMA],
        args=[a, b])


def _grad_w_in(dg, hb, name, jobs=()):
    T = hb.shape[0]
    return _grad_chip(dg, hb, pl.BlockSpec((None, T, FB), lambda s: (s, 0, 0)), _const_spec((T, D)),
                      (FB, D), None, name, jobs)


def _grad_w_out(act, dyb, name, jobs=()):
    T = dyb.shape[0]
    return _grad_chip(act, dyb, pl.BlockSpec((None, T, FB), lambda s: (s, 0, 0)), _const_spec((T, D)),
                      (FB, D), FO, name, jobs)


def _grad_w_mi(hb, dproj, name, jobs=()):
    T = hb.shape[0]
    return _grad_chip(hb, dproj, _const_spec((T, D)), pl.BlockSpec((T, MB), lambda s: (0, s)),
                      (D, MB), None, name, jobs, via_b=True)


def _grad_w_mo(ycat, dym, name, jobs=()):
    T = ycat.shape[0]
    return _grad_chip(ycat, dym, pl.BlockSpec((T, 2 * MO), lambda s: (0, s)), _const_spec((T, D)),
                      (2 * MO, D), MO, name, jobs)


def _adamw_math(w, g, m, v):
    m2 = ADAM_B1 * m + (1.0 - ADAM_B1) * g
    v2 = ADAM_B2 * v + (1.0 - ADAM_B2) * (g * g)
    m_hat = m2 / (1.0 - ADAM_B1 ** ADAM_STEP)
    v_hat = v2 / (1.0 - ADAM_B2 ** ADAM_STEP)
    delta = -ADAM_LR * (m_hat / (jnp.sqrt(v_hat) + ADAM_EPS) + ADAM_WD * w)
    return delta, m2, v2


def _adamw_reduce(parts, w, m, v, tr, name, own=None, after=None):
    R, C = w.shape

    def core(ins, outs, _):
        p_ref, w_ref, m_ref, v_ref = ins[:4]
        g_ref, d_ref, m2_ref, v2_ref = outs
        if own is None:
            terms = [p_ref[s].astype(F32) for s in range(4)]
        else:
            mq = 2 * lax.axis_index("x") + lax.axis_index("y")
            mine = ins[4][...].astype(F32)
            terms = [jnp.where(mq == s, mine, p_ref[s].astype(F32)) for s in range(4)]
        g = terms[0]
        for s in range(1, 4):
            g = g + terms[s]
        g_ref[...] = g
        d_ref[...], m2_ref[...], v2_ref[...] = _adamw_math(w_ref[...], g, m_ref[...], v_ref[...])

    blk = pl.BlockSpec((tr, C), lambda i: (i, 0))
    in_specs = [pl.BlockSpec((4, tr, C), lambda i: (0, i, 0)), blk, blk, blk]
    args = [parts, w, m, v]
    if own is not None:
        mq = 2 * lax.axis_index("x") + lax.axis_index("y")
        in_specs.append(pl.BlockSpec((tr, C), lambda i: (i, 0)))
        args.append(lax.dynamic_index_in_dim(own, mq, 0, keepdims=False))
    if after is not None:
        in_specs.append(HBM)
        args.append(after)
    return _call(
        core, name=name, grid=(R // tr,), in_specs=in_specs,
        out_specs=[blk, blk, blk, blk], out_shape=[jax.ShapeDtypeStruct((R, C), F32)] * 4, args=args)[0]


HBM_ONLY = pl.BlockSpec(memory_space=pltpu.HBM)
SEM = pl.BlockSpec(memory_space=pltpu.SEMAPHORE)
EFFECT = pltpu.SideEffectType.DATAFLOW_SIDE_EFFECTING


def _chip_scatter_start(gs, name):
    n = len(gs)

    def body(*refs):
        g_refs, land_refs = refs[:n], refs[n:2 * n]
        ssem, rsem = refs[2 * n:2 * n + 2]
        token = refs[-1]
        me = _me()
        mq = 2 * me[0] + me[1]
        for k, f in enumerate(CHIP_FLIPS):
            p = _flip(me, f)
            for a in range(n):
                _remote(g_refs[a].at[2 * p[0] + p[1]], land_refs[a].at[mq], ssem.at[3 * a + k], rsem.at[3 * a + k], p).start()
        token[...] = jnp.zeros_like(token)

    gs = [pltpu.with_memory_space_constraint(g, pltpu.HBM) for g in gs]
    lands = [pltpu.with_memory_space_constraint(lax.empty(g.shape, g.dtype), pltpu.HBM) for g in gs]
    res = pl.pallas_call(
        body, name=name,
        out_shape=(pltpu.SemaphoreType.DMA((3 * n,)), pltpu.SemaphoreType.DMA((3 * n,)))
        + tuple(pltpu.HBM(g.shape, g.dtype) for g in gs) * 2 + (jax.ShapeDtypeStruct((SUBL, LANES), F32),),
        in_specs=(HBM_ONLY,) * (2 * n), out_specs=(SEM, SEM) + (HBM_ONLY,) * (2 * n) + (VM,),
        input_output_aliases={a: 2 + a for a in range(2 * n)},
        compiler_params=pltpu.CompilerParams(has_side_effects=EFFECT),
    )(*gs, *lands)
    return res[:-1], res[-1]


def _chip_scatter_wait(handle, after, name):
    ssem, rsem = handle[:2]
    n = (len(handle) - 2) // 2
    thru = handle[2:]

    def body(*refs):
        g_refs, land_refs = refs[:n], refs[n:2 * n]
        ssem, rsem = refs[2 * n:2 * n + 2]
        me = _me()
        mq = 2 * me[0] + me[1]
        for k, f in enumerate(CHIP_FLIPS):
            p = _flip(me, f)
            pq = 2 * p[0] + p[1]
            for a in range(n):
                _remote(g_refs[a].at[pq], land_refs[a].at[mq], ssem.at[3 * a + k], rsem.at[3 * a + k], p).wait_send()
                _remote(g_refs[a].at[mq], land_refs[a].at[pq], ssem.at[3 * a + k], rsem.at[3 * a + k], p).wait_recv()

    res = pl.pallas_call(
        body, name=name,
        out_shape=tuple(pltpu.HBM(t.shape, t.dtype) for t in thru),
        in_specs=(HBM_ONLY,) * (2 * n) + (SEM, SEM, HBM), out_specs=(HBM_ONLY,) * (2 * n),
        input_output_aliases={a: a for a in range(2 * n)},
        compiler_params=pltpu.CompilerParams(has_side_effects=EFFECT),
    )(*thru, ssem, rsem, after)
    return list(res[:n]), list(res[n:])


def _adamw_ada(sc_all, dd, w, m, v, tr, name, after=None):
    R, C = w.shape

    def core(ins, outs, _):
        sc_ref, dd_ref, w_ref, m_ref, v_ref = ins[:5]
        g_ref, d_ref, m2_ref, v2_ref = outs
        g = _dot_tn(sc_ref[...].astype(BF16), dd_ref[...].astype(BF16))
        g_ref[...] = g
        d_ref[...], m2_ref[...], v2_ref[...] = _adamw_math(w_ref[...], g, m_ref[...], v_ref[...])

    blk = pl.BlockSpec((tr, C), lambda i: (i, 0))
    return _call(
        core, name=name, grid=(R // tr,),
        in_specs=[pl.BlockSpec((64, tr), lambda i: (0, i)), pl.BlockSpec((64, C), lambda i: (0, 0)), blk, blk, blk]
        + [HBM] * (after is not None),
        out_specs=[blk, blk, blk, blk], out_shape=[jax.ShapeDtypeStruct((R, C), F32)] * 4,
        args=[sc_all, dd, w, m, v] + [after] * (after is not None))[0]


def _adamw_small(gathered, plain, grads, wmv, emit, name):
    nw = len(grads)
    ng, npl, ne = len(gathered), len(plain), len(emit)

    def core(ins, outs, _):
        srcs = []
        for a in range(ng):
            s = ins[a][0]
            for dev in range(1, NDEV):
                s = s + ins[a][dev]
            srcs.append(s)
        srcs += [ins[ng + a][...] for a in range(npl)]
        w_refs = ins[ng + npl:]
        for e, a in enumerate(emit):
            outs[e][...] = srcs[a]
        for t in range(nw):
            src, row = grads[t]
            g = srcs[src] if row is None else srcs[src][row:row + 1, :]
            w_ref, m_ref, v_ref = w_refs[3 * t:3 * t + 3]
            g_ref, d_ref, m2_ref, v2_ref = outs[ne + 4 * t:ne + 4 * t + 4]
            g_ref[...] = g
            d_ref[...], m2_ref[...], v2_ref[...] = _adamw_math(w_ref[...], g, m_ref[...], v_ref[...])

    out_shape = [jax.ShapeDtypeStruct(gathered[a].shape[1:], F32) for a in emit]
    for t in range(nw):
        out_shape += [jax.ShapeDtypeStruct(wmv[3 * t].shape, F32)] * 4
    return _call(
        core, name=name, grid=(), in_specs=[VM] * (ng + npl + 3 * nw), out_specs=[VM] * (ne + 4 * nw),
        out_shape=out_shape, args=list(gathered) + list(plain) + list(wmv))[0]


def _ada_fwd(c_pad, w_ada, b_cols, cw_pad, jobs=()):
    def core(ins, outs, scs, start_jobs, finish_jobs):
        c_ref, w_ref, b_ref, cwp_ref = ins
        ada_ref, sc_ref, cw_ref = outs
        cbuf, send_buf, ssem, rsem = scs
        me = _me()
        mi = _lin(me)
        cbuf[mi] = c_ref[...]
        cw_ref[mi] = cwp_ref[...]
        peers = [_flip(me, f) for f in FLIPS]
        first = []
        for k, p in enumerate(peers):
            first.append(_remote(cbuf.at[mi], cbuf.at[mi], ssem.at[k], rsem.at[k], p))
            first.append(_remote(cw_ref.at[mi], cw_ref.at[mi], ssem.at[7 + k], rsem.at[7 + k], p))
        for cp in first:
            cp.start()
        start_jobs()
        for k, p in enumerate(peers):
            pi = _lin(p)
            _remote(cbuf.at[pi], cbuf.at[pi], ssem.at[k], rsem.at[k], p).wait_recv()
            _remote(cw_ref.at[pi], cw_ref.at[pi], ssem.at[7 + k], rsem.at[7 + k], p).wait_recv()
        c_all = cbuf[...].reshape(8 * 8, D)
        sc = c_all * _sigmoid(c_all)
        sc_ref[...] = sc
        res = _dot(sc.astype(BF16), w_ref[...].astype(BF16)) + b_ref[...]
        send_buf[...] = res.reshape(8, 8, ADA_B)
        ada_ref[mi] = send_buf[mi]
        second = []
        for k, p in enumerate(peers):
            second.append(_remote(send_buf.at[_lin(p)], ada_ref.at[mi], ssem.at[14 + k], rsem.at[14 + k], p))
        for cp in second:
            cp.start()
        finish_jobs()
        for k, p in enumerate(peers):
            _remote(send_buf.at[mi], ada_ref.at[_lin(p)], ssem.at[14 + k], rsem.at[14 + k], p).wait_recv()
        for cp in first + second:
            cp.wait_send()

    return _call(
        core, name="ada_fwd", grid=(), jobs=jobs, core_starts=True, in_specs=[VM, VM, VM, VM], out_specs=[VM, VM, VM],
        out_shape=[jax.ShapeDtypeStruct((8, 8, ADA_B), F32), jax.ShapeDtypeStruct((64, D), F32),
                   jax.ShapeDtypeStruct((8, 32, 64), F32)],
        scratch=[pltpu.VMEM((8, 8, D), F32), pltpu.VMEM((8, 8, ADA_B), F32),
                 pltpu.SemaphoreType.DMA((21,)), pltpu.SemaphoreType.DMA((21,))],
        args=[c_pad, w_ada, b_cols, cw_pad])


def _ada_bwd(dada, jobs=()):
    def core(ins, outs, scs):
        (d_ref,) = ins
        dd_ref, gb_ref = outs
        rbuf, ssem, rsem = scs
        me = _me()
        mi = _lin(me)
        peers = [_flip(me, f) for f in FLIPS]
        rbuf[mi] = d_ref[mi]
        first = []
        for k, p in enumerate(peers):
            first.append(_remote(d_ref.at[_lin(p)], rbuf.at[mi], ssem.at[k], rsem.at[k], p))
        for cp in first:
            cp.start()
        for k, p in enumerate(peers):
            _remote(d_ref.at[mi], rbuf.at[_lin(p)], ssem.at[k], rsem.at[k], p).wait_recv()
        dd = rbuf[...].reshape(64, ADA_B)
        dd_ref[...] = dd
        gb_ref[mi] = jnp.broadcast_to(_colsum(dd), (8, ADA_B))
        second = []
        for k, p in enumerate(peers):
            second.append(_remote(gb_ref.at[mi], gb_ref.at[mi], ssem.at[7 + k], rsem.at[7 + k], p))
        for cp in second:
            cp.start()
        for k, p in enumerate(peers):
            pi = _lin(p)
            _remote(gb_ref.at[pi], gb_ref.at[pi], ssem.at[7 + k], rsem.at[7 + k], p).wait_recv()
        for cp in first + second:
            cp.wait_send()

    return _call(
        core, name="ada_bwd", grid=(), jobs=jobs, in_specs=[VM], out_specs=[VM, VM],
        out_shape=[jax.ShapeDtypeStruct((64, ADA_B), F32), jax.ShapeDtypeStruct((8, 8, ADA_B), F32)],
        scratch=[pltpu.VMEM((8, 8, ADA_B), F32), pltpu.SemaphoreType.DMA((14,)), pltpu.SemaphoreType.DMA((14,))],
        args=[dada])


SMALL_D = ("g_pre_f1", "g_post_f1", "g_pre_m", "g_post_m", "g_pre_f2", "g_post_f2")
SMALL_W = ("gmlp_norm_g", "gmlp_norm_b", "conv_b", "conv_norm_g", "conv_norm_b", "g_out_a", "g_out_b")


def kernel(x, c, w_ada, b_ada, g_pre_f1, g_post_f1, w_f1_in, w_f1_out, g_pre_m, g_post_m, w_mix_in, gmlp_norm_g, gmlp_norm_b, w_spatial, b_spatial, conv_w, conv_b, conv_norm_g, conv_norm_b, g_out_a, g_out_b, w_mix_out, g_pre_f2, g_post_f2, w_f2_in, w_f2_out, loss_target, m_w_ada, m_b_ada, m_g_pre_f1, m_g_post_f1, m_w_f1_in, m_w_f1_out, m_g_pre_m, m_g_post_m, m_w_mix_in, m_gmlp_norm_g, m_gmlp_norm_b, m_w_spatial, m_b_spatial, m_conv_w, m_conv_b, m_conv_norm_g, m_conv_norm_b, m_g_out_a, m_g_out_b, m_w_mix_out, m_g_pre_f2, m_g_post_f2, m_w_f2_in, m_w_f2_out, v_w_ada, v_b_ada, v_g_pre_f1, v_g_post_f1, v_w_f1_in, v_w_f1_out, v_g_pre_m, v_g_post_m, v_w_mix_in, v_gmlp_norm_g, v_gmlp_norm_b, v_w_spatial, v_b_spatial, v_conv_w, v_conv_b, v_conv_norm_g, v_conv_norm_b, v_g_out_a, v_g_out_b, v_w_mix_out, v_g_pre_f2, v_g_post_f2, v_w_f2_in, v_w_f2_out):
    given = dict(locals())
    bl, seq, _ = x.shape
    T = bl * seq
    tm = min(256, seq // 2)
    mi = _lin((lax.axis_index("x"), lax.axis_index("y"), lax.axis_index("c")))

    def shard_in(w):
        return w[0].T.astype(BF16)

    g_f1 = _RelayGather([shard_in(w_f1_in), w_f1_out[0].astype(BF16)], ("rows", "out"))
    s_f2 = shard_in(w_f2_in)
    g_mx = _Gather([w_mix_in[0].astype(BF16), w_mix_out[0].astype(BF16), w_f2_out[0].astype(BF16), s_f2[:, 0:D // 4]],
                   ("rows", "rows", "out", "rows"), late_mid=True)
    g_f2 = _Gather([s_f2[:, D // 4:D]], ("rows",))

    c_pad = jnp.pad(c, ((0, 8 - bl), (0, 0)))
    b_cols = lax.dynamic_slice(b_ada, (0, mi * ADA_B), (1, ADA_B))
    cw_pad = jnp.pad(conv_w[0], ((0, 1), (0, 0)))
    (ada_blk, sc_all, cw_all), ((wi1, wo1),) = _ada_fwd(c_pad, w_ada[0], b_cols, cw_pad, jobs=[g_f1])
    ada = ada_blk[:, 0:bl, :].transpose(1, 0, 2).reshape(bl, 9, D)
    pad5 = jnp.zeros((bl, 5, D), F32)
    mod1 = jnp.concatenate([ada[:, 0:3], pad5], axis=1)
    mod2 = jnp.concatenate([ada[:, 3:6], pad5], axis=1)
    mod3 = jnp.concatenate([ada[:, 6:9], pad5], axis=1)
    cw_full = cw_all.transpose(1, 0, 2).reshape(32, WA)

    zrow = jnp.zeros((1, D), F32)
    gv1 = jnp.concatenate([g_pre_f1, g_post_f1] + [zrow] * 6, axis=0)
    gvm = jnp.concatenate([g_pre_m, g_post_m] + [zrow] * 6, axis=0)
    gv2 = jnp.concatenate([g_pre_f2, g_post_f2] + [zrow] * 6, axis=0)
    v512 = jnp.concatenate([gmlp_norm_g, gmlp_norm_b, conv_b, conv_norm_g, conv_norm_b, g_out_a, g_out_b,
                            jnp.zeros((1, WA), F32)], axis=0)
    ws = w_spatial[0]
    bias_full = jnp.repeat(b_spatial[0].T, HD, axis=1)
    esel = (lax.broadcasted_iota(jnp.int32, (8, WA), 1) // HD == lax.broadcasted_iota(jnp.int32, (8, WA), 0)).astype(F32)

    x0 = x.reshape(T, D)
    (x1, gu1, y1), ((wmi, wmo, wo2, wi2a),) = _ffn_fwd(x0, mod1, gv1, wi1, wo1, tm, "ffn1_fwd", jobs=[g_mx])
    wmo = wmo.reshape(D, D)
    (x2, proj, ym, conv), ((wi2b,),) = _mixer_fwd(x1, mod2, gvm, wmi, wmo, v512, ws, bias_full, cw_full, tm, "mixer_fwd", jobs=[g_f2])

    (dx2, dg2, act2, hb2, dyb2, mg3, vg3, loss_blk), _ = _ffn_last(
        x2, loss_target.reshape(T, D), mod3, gv2, (wi2a, wi2b), wo2, tm, "ffn2_fwd_bwd")
    (g_wi2,), _ = _grad_w_in(dg2, hb2, "ffn2_gw_in")
    (g_wo2,), _ = _grad_w_out(act2, dyb2, "ffn2_gw_out")
    (dpart, dymb, ycat, mg2a, vgma, v5g, gws, gbs), ((p_wo2,),) = _mixer_bwd_a(
        dx2, ym, proj, conv, mod2, gvm, wmo, v512, ws, bias_full, esel, tm, "mixer_bwd_a",
        jobs=[_ChipScatter([g_wo2])])
    (dx1, dproj, hbm, mg2b, vgmb, dcw), ((p_wi2,),) = _mixer_bwd_b(
        dx2, x1, dpart, proj, mod2, gvm, wmi, cw_full, tm, "mixer_bwd_b", jobs=[_ChipScatter([g_wi2])])
    (g_wmi,), _ = _grad_w_mi(hbm, dproj, "mixer_gw_in")
    (g_wmo,), _ = _grad_w_mo(ycat, dymb, "mixer_gw_out")
    p2 = jnp.concatenate([v5g, dcw], axis=0)
    (dx0, dg1, act1, hb1, dyb1, mg1, vg1), _ = _ffn_bwd(dx1, x0, y1, gu1, mod1, gv1, wi1, wo1, tm, "ffn1_bwd")

    dada = jnp.concatenate([mg1[:, 0:3], mg2b[:, 0:2], mg2a[:, 2:3], mg3[:, 0:3]], axis=1)
    dada = dada.reshape(bl, NDEV, ADA_B).transpose(1, 0, 2)
    dada = jnp.pad(dada, ((0, 0), (0, 8 - bl), (0, 0)))
    p1 = jnp.concatenate([vg1[0:2], vgmb[0:1], vgma[1:2], vg3[0:2], loss_blk[0:1], zrow], axis=0)
    (dd_all, gb_all), ((a1,),) = _ada_bwd(dada, jobs=[_AllGather([p1])])
    g_bada = gb_all[:, 0, :].reshape(1, 9 * D)

    (g_wo1,), ((p_wmi, p_wmo),) = _grad_w_out(act1, dyb1, "ffn1_gw_out", jobs=[_ChipScatter([g_wmi, g_wmo])])
    (g_wi1,), ((a2, a3, a4), (p_wo1,)) = _grad_w_in(
        dg1, hb1, "ffn1_gw_in", jobs=[_Gather([p2, gws, gbs], ("rows",) * 3), _ChipScatter([g_wo1])])

    h_f1, token = _chip_scatter_start([g_wi1], "tail_start")

    res = {}
    quad = _adamw_reduce(p_wi2, w_f2_in[0].T, m_w_f2_in[0].T, v_w_f2_in[0].T, FO, "adamw_w_f2_in", after=token)
    res["w_f2_in"] = tuple(t.T[None] for t in quad)
    for nm, part, tr in (("w_f2_out", p_wo2, FO), ("w_mix_in", p_wmi, 256), ("w_mix_out", p_wmo, MO), ("w_f1_out", p_wo1, FO)):
        quad = _adamw_reduce(part, given[nm][0], given["m_" + nm][0], given["v_" + nm][0], tr, "adamw_" + nm, after=quad[1])
        res[nm] = tuple(t[None] for t in quad)
    quad = _adamw_ada(sc_all, dd_all, w_ada[0], m_w_ada[0], v_w_ada[0], 256, "adamw_w_ada", after=quad[1])
    res["w_ada"] = tuple(t[None] for t in quad)
    (g_wi1,), (p_wi1,) = _chip_scatter_wait(h_f1, quad[1], "tail_wait")
    quad = _adamw_reduce(p_wi1, w_f1_in[0].T, m_w_f1_in[0].T, v_w_f1_in[0].T, FO, "adamw_w_f1_in", own=g_wi1)
    res["w_f1_in"] = tuple(t.T[None] for t in quad)

    small = SMALL_D + SMALL_W + ("w_spatial", "b_spatial", "b_ada")
    grads = [(0, r) for r in range(6)] + [(1, r) for r in range(7)] + [(2, None), (3, None), (4, None)]
    wmv = []
    for nm in small:
        for pre in ("", "m_", "v_"):
            wmv.append(given[pre + nm][0] if nm in ("w_spatial", "b_spatial") else given[pre + nm])
    outs = _adamw_small([a1, a2, a3, a4], [g_bada], grads, wmv, (0, 1), "adamw_small")
    loss = outs[0][6, 0]
    for t, nm in enumerate(small):
        quad = outs[2 + 4 * t:6 + 4 * t]
        res[nm] = tuple(q[None] for q in quad) if nm in ("w_spatial", "b_spatial") else tuple(quad)
    g_cw = lax.dynamic_slice(outs[1], (8, mi * 64), (32, 64))
    wmv = [jnp.pad(given[pre + "conv_w"][0], ((0, 1), (0, 0)), constant_values=1.0 if pre == "v_" else 0.0)
           for pre in ("", "m_", "v_")]
    quad = _adamw_small([], [g_cw], [(0, None)], wmv, (), "adamw_conv_w")
    res["conv_w"] = tuple(q[0:CONV_K][None] for q in quad)

    order = ["w_ada", "b_ada", "g_pre_f1", "g_post_f1", "w_f1_in", "w_f1_out", "g_pre_m", "g_post_m", "w_mix_in",
             "gmlp_norm_g", "gmlp_norm_b", "w_spatial", "b_spatial", "conv_w", "conv_b", "conv_norm_g", "conv_norm_b",
             "g_out_a", "g_out_b", "w_mix_out", "g_pre_f2", "g_post_f2", "w_f2_in", "w_f2_out"]
    out = [loss, dx0.reshape(bl, seq, D)]
    for k in range(4):
        out += [res[nm][k] for nm in order]
    return tuple(out)
```

```python
import jax
import jax.numpy as jnp
from jax import lax
from jax.experimental import pallas as pl
from jax.experimental.pallas import tpu as pltpu

F32 = jnp.float32
BF16 = jnp.bfloat16

D = 1024
DFF = 2816
NDEV = 8
FB = 2 * DFF // NDEV
NCH = DFF // FB
LANES = 128
SUBL = 8
FO = DFF // NDEV
WA = 512
NSLAB = WA // LANES
NHEAD = 8
HD = 64
CHUNK = 128
CONV_K = 31
HALO = 32
MB = 2 * (WA + WA) // NDEV
MO = D // NDEV
ADA_B = 9 * D // NDEV
EPS = 1e-6
HALF = 0.5

ADAM_LR = 0.001
ADAM_B1 = 0.9
ADAM_B2 = 0.999
ADAM_EPS = 1e-08
ADAM_WD = 0.01
ADAM_STEP = 10

VMEM_LIMIT = 56 * 1024 * 1024
MESH = pl.DeviceIdType.MESH
FLIPS = ((0, 0, 1), (1, 0, 0), (0, 1, 0), (1, 1, 0), (1, 0, 1), (0, 1, 1), (1, 1, 1))
CHIP_FLIPS = ((1, 0, 0), (0, 1, 0), (1, 1, 0))
HBM = pl.BlockSpec(memory_space=pl.ANY)
VM = pl.BlockSpec(memory_space=pltpu.VMEM)


def _dot(a, b):
    return lax.dot_general(a, b, (((1,), (0,)), ((), ())), preferred_element_type=F32)


def _dot_nt(a, b):
    return lax.dot_general(a, b, (((1,), (1,)), ((), ())), preferred_element_type=F32)


def _dot_tn(a, b):
    return lax.dot_general(a, b, (((0,), (0,)), ((), ())), preferred_element_type=F32)


def _rowmean(v):
    return jnp.mean(v, axis=-1, keepdims=True)


def _colsum(v):
    return jnp.sum(v, axis=0, keepdims=True)


def _sigmoid(v):
    return 0.5 * jnp.tanh(0.5 * v) + 0.5


def _const_spec(shape):
    nd = len(shape)
    return pl.BlockSpec(shape, lambda *_: (0,) * nd, pipeline_mode=pl.Buffered(1))


def _me():
    return lax.axis_index("x"), lax.axis_index("y"), lax.axis_index("c")


def _flip(me, f):
    return tuple(1 - v if b else v for v, b in zip(me, f))


def _lin(p):
    return 4 * p[0] + 2 * p[1] + p[2]


def _remote(src, dst, send_sem, recv_sem, dev):
    return pltpu.make_async_remote_copy(src_ref=src, dst_ref=dst, send_sem=send_sem, recv_sem=recv_sem,
                                        device_id=dev, device_id_type=MESH)


def _blk(kind, ref, p):
    if kind == "out":
        return ref.at[2 * p[0] + p[1], pl.ds(p[2] * FO, FO), :]
    return ref.at[_lin(p)]


class _Gather:
    def __init__(self, shards, kinds, late_mid=False):
        self.late_mid = late_mid
        self.kinds = kinds
        self.n = len(shards)
        self.ins = list(shards)
        self.out_shape = [jax.ShapeDtypeStruct((4, FB, D) if k == "out" else (NDEV,) + s.shape, s.dtype)
                          for s, k in zip(shards, kinds)]
        self.sems = [pltpu.SemaphoreType.DMA((7 * self.n,)), pltpu.SemaphoreType.DMA((7 * self.n,)),
                     pltpu.SemaphoreType.DMA((self.n,))]

    def _first(self, ins, outs, sems):
        ssem, rsem, lsem = sems
        me = _me()
        sib = _flip(me, (0, 0, 1))
        cps, loc = [], []
        for a in range(self.n):
            mine = _blk(self.kinds[a], outs[a], me)
            loc.append(pltpu.make_async_copy(ins[a], mine, lsem.at[a]))
            cps.append(_remote(ins[a], mine, ssem.at[7 * a], rsem.at[7 * a], sib))
            for j, f in enumerate(CHIP_FLIPS):
                cps.append(_remote(ins[a], mine, ssem.at[7 * a + 1 + j], rsem.at[7 * a + 1 + j], _flip(me, f)))
        return cps, loc

    def _passed(self, outs, sems):
        ssem, rsem, _ = sems
        me = _me()
        sib = _flip(me, (0, 0, 1))
        cps = []
        for j, f in enumerate(CHIP_FLIPS):
            for a in range(self.n):
                blk = _blk(self.kinds[a], outs[a], _flip(me, f))
                cps.append(_remote(blk, blk, ssem.at[7 * a + 4 + j], rsem.at[7 * a + 4 + j], sib))
        return cps

    def start(self, ins, outs, sems):
        cps, loc = self._first(ins, outs, sems)
        for cp in loc + cps:
            cp.start()

    def mid(self, ins, outs, sems):
        ssem, rsem, _ = sems
        me = _me()
        passed = self._passed(outs, sems)
        t = 0
        for j, f in enumerate(CHIP_FLIPS):
            for a in range(self.n):
                blk = _blk(self.kinds[a], outs[a], _flip(me, f))
                _remote(blk, blk, ssem.at[7 * a + 1 + j], rsem.at[7 * a + 1 + j], _flip(me, f)).wait_recv()
                passed[t].start()
                t += 1

    def end(self, ins, outs, sems):
        ssem, rsem, _ = sems
        me = _me()
        sib = _flip(me, (0, 0, 1))
        for a in range(self.n):
            blk = _blk(self.kinds[a], outs[a], sib)
            _remote(blk, blk, ssem.at[7 * a], rsem.at[7 * a], sib).wait_recv()
            for j, f in enumerate(CHIP_FLIPS):
                blk = _blk(self.kinds[a], outs[a], _flip(_flip(me, f), (0, 0, 1)))
                _remote(blk, blk, ssem.at[7 * a + 4 + j], rsem.at[7 * a + 4 + j], sib).wait_recv()
        cps, loc = self._first(ins, outs, sems)
        for cp in cps + self._passed(outs, sems):
            cp.wait_send()
        for cp in loc:
            cp.wait()


class _RelayGather(_Gather):
    def _peers(self):
        me = _me()
        c = me[2]
        to = (me[0] + (1 - c) - 2 * me[0] * (1 - c), me[1] + c - 2 * me[1] * c, c)
        frm = (me[0] + c - 2 * me[0] * c, me[1] + (1 - c) - 2 * me[1] * (1 - c), c)
        return me, _flip(me, (0, 0, 1)), to, frm, _flip(me, (1, 1, 0))

    def _first(self, ins, outs, sems):
        ssem, rsem, lsem = sems
        me, sib, to, frm, _ = self._peers()
        cps, loc = [], []
        for a in range(self.n):
            mine = _blk(self.kinds[a], outs[a], me)
            loc.append(pltpu.make_async_copy(ins[a], mine, lsem.at[a]))
            for slot, dev in ((0, sib), (1, to), (2, frm)):
                cps.append(_remote(ins[a], mine, ssem.at[7 * a + slot], rsem.at[7 * a + slot], dev))
        return cps, loc

    def _block_copy(self, outs, sems, a, slot, owner, dev):
        ssem, rsem, _ = sems
        blk = _blk(self.kinds[a], outs[a], owner)
        return _remote(blk, blk, ssem.at[7 * a + slot], rsem.at[7 * a + slot], dev)

    def mid(self, ins, outs, sems):
        me, sib, to, frm, _ = self._peers()
        for a in range(self.n):
            self._block_copy(outs, sems, a, 2, frm, frm).wait_recv()
            self._block_copy(outs, sems, a, 3, frm, to).start()
            self._block_copy(outs, sems, a, 5, frm, sib).start()
        for a in range(self.n):
            self._block_copy(outs, sems, a, 1, to, to).wait_recv()
            self._block_copy(outs, sems, a, 4, to, sib).start()

    def end(self, ins, outs, sems):
        me, sib, to, frm, far = self._peers()
        up = (0, 0, 1)
        for a in range(self.n):
            self._block_copy(outs, sems, a, 3, far, to).wait_recv()
            self._block_copy(outs, sems, a, 6, far, sib).start()
        for a in range(self.n):
            for slot, owner in ((0, sib), (4, _flip(frm, up)), (5, _flip(to, up)), (6, _flip(far, up))):
                self._block_copy(outs, sems, a, slot, owner, sib).wait_recv()
        cps, loc = self._first(ins, outs, sems)
        for a in range(self.n):
            cps += [self._block_copy(outs, sems, a, 3, frm, to), self._block_copy(outs, sems, a, 4, to, sib),
                    self._block_copy(outs, sems, a, 5, frm, sib), self._block_copy(outs, sems, a, 6, far, sib)]
        for cp in cps:
            cp.wait_send()
        for cp in loc:
            cp.wait()


class _ChipScatter:
    def __init__(self, grads):
        self.n = len(grads)
        self.ins = list(grads)
        self.out_shape = [jax.ShapeDtypeStruct(g.shape, BF16) for g in grads]
        self.sems = [pltpu.SemaphoreType.DMA((3 * self.n,)), pltpu.SemaphoreType.DMA((3 * self.n,)),
                     pltpu.SemaphoreType.DMA((self.n,))]

    def _copies(self, ins, outs, sems):
        ssem, rsem, lsem = sems
        me = _me()
        mq = 2 * me[0] + me[1]
        loc = [pltpu.make_async_copy(ins[a].at[mq], outs[a].at[mq], lsem.at[a]) for a in range(self.n)]
        cps = []
        for k, f in enumerate(CHIP_FLIPS):
            p = _flip(me, f)
            for a in range(self.n):
                cps.append(_remote(ins[a].at[2 * p[0] + p[1]], outs[a].at[mq], ssem.at[3 * a + k], rsem.at[3 * a + k], p))
        return cps, loc

    def start(self, ins, outs, sems):
        cps, loc = self._copies(ins, outs, sems)
        for cp in loc + cps:
            cp.start()

    mid = None

    def end(self, ins, outs, sems):
        ssem, rsem, _ = sems
        me = _me()
        mq = 2 * me[0] + me[1]
        for k, f in enumerate(CHIP_FLIPS):
            p = _flip(me, f)
            for a in range(self.n):
                _remote(ins[a].at[mq], outs[a].at[2 * p[0] + p[1]], ssem.at[3 * a + k], rsem.at[3 * a + k], p).wait_recv()
        cps, loc = self._copies(ins, outs, sems)
        for cp in cps:
            cp.wait_send()
        for cp in loc:
            cp.wait()


class _AllGather:
    def __init__(self, parts):
        self.n = len(parts)
        self.ins = list(parts)
        self.out_shape = [jax.ShapeDtypeStruct((NDEV,) + p.shape, p.dtype) for p in parts]
        self.sems = [pltpu.SemaphoreType.DMA((7 * self.n,)), pltpu.SemaphoreType.DMA((7 * self.n,)),
                     pltpu.SemaphoreType.DMA((self.n,))]

    def _copies(self, ins, outs, sems):
        ssem, rsem, lsem = sems
        me = _me()
        mi = _lin(me)
        loc = [pltpu.make_async_copy(ins[a], outs[a].at[mi], lsem.at[a]) for a in range(self.n)]
        cps = []
        for k, f in enumerate(FLIPS):
            for a in range(self.n):
                cps.append(_remote(ins[a], outs[a].at[mi], ssem.at[7 * a + k], rsem.at[7 * a + k], _flip(me, f)))
        return cps, loc

    def start(self, ins, outs, sems):
        cps, loc = self._copies(ins, outs, sems)
        for cp in loc + cps:
            cp.start()

    mid = None

    def end(self, ins, outs, sems):
        ssem, rsem, _ = sems
        me = _me()
        for k, f in enumerate(FLIPS):
            p = _flip(me, f)
            for a in range(self.n):
                _remote(ins[a], outs[a].at[_lin(p)], ssem.at[7 * a + k], rsem.at[7 * a + k], p).wait_recv()
        cps, loc = self._copies(ins, outs, sems)
        for cp in cps:
            cp.wait_send()
        for cp in loc:
            cp.wait()


def _call(core, *, name, grid, in_specs, out_specs, out_shape, args, scratch=(), jobs=(), core_starts=False):
    n_in, n_out, n_sc = len(in_specs), len(out_specs), len(scratch)
    steps = 1
    for g in grid:
        steps *= g

    def body(*refs):
        pos = [0]

        def take(k):
            r = refs[pos[0]:pos[0] + k]
            pos[0] += k
            return r

        ins = take(n_in)
        j_ins = [take(len(j.ins)) for j in jobs]
        outs = take(n_out)
        j_outs = [take(len(j.out_shape)) for j in jobs]
        scs = take(n_sc)
        j_sems = [take(len(j.sems)) for j in jobs]
        if len(grid) == 2:
            step = pl.program_id(0) * grid[1] + pl.program_id(1)
        elif len(grid) == 1:
            step = pl.program_id(0)
        else:
            step = 0
        def start_jobs():
            for j, ji, jo, js in zip(jobs, j_ins, j_outs, j_sems):
                j.start(ji, jo, js)

        if grid:
            pl.when(step == 0)(start_jobs)
        elif not core_starts:
            start_jobs()
        for j, ji, jo, js in zip(jobs, j_ins, j_outs, j_sems):
            if j.mid is not None and grid:
                at = max(steps - 2, 0) if j.late_mid else (3 * steps) // 4
                pl.when(step == at)(lambda j=j, ji=ji, jo=jo, js=js: j.mid(ji, jo, js))
        def finish_jobs():
            for j, ji, jo, js in zip(jobs, j_ins, j_outs, j_sems):
                if j.mid is not None:
                    j.mid(ji, jo, js)
                j.end(ji, jo, js)

        if core_starts:
            core(ins, outs, scs, start_jobs, finish_jobs)
        elif core is not None:
            core(ins, outs, scs)
        if grid:
            for j, ji, jo, js in zip(jobs, j_ins, j_outs, j_sems):
                pl.when(step == steps - 1)(lambda j=j, ji=ji, jo=jo, js=js: j.end(ji, jo, js))
        elif not core_starts:
            finish_jobs()

    all_in = list(in_specs)
    all_args = list(args)
    all_out = list(out_specs)
    all_shape = list(out_shape)
    all_sc = list(scratch)
    for j in jobs:
        all_in += [HBM] * len(j.ins)
        all_args += j.ins
    for j in jobs:
        all_out += [HBM] * len(j.out_shape)
        all_shape += j.out_shape
        all_sc += j.sems
    params = dict(vmem_limit_bytes=VMEM_LIMIT)
    if grid:
        params["dimension_semantics"] = ("arbitrary",) * len(grid)
    res = pl.pallas_call(
        body, name=name, grid=grid, in_specs=all_in, out_specs=all_out, out_shape=all_shape,
        scratch_shapes=all_sc, compiler_params=pltpu.CompilerParams(**params),
    )(*all_args)
    core_res = list(res[:n_out])
    job_res = []
    pos = n_out
    for j in jobs:
        job_res.append(list(res[pos:pos + len(j.out_shape)]))
        pos += len(j.out_shape)
    return core_res, job_res


def _ffn_fwd(x, mod, gvec, w_in, w_out, tm, name, jobs=()):
    T = x.shape[0]
    nt = T // tm
    tps = nt // mod.shape[0]

    def core(ins, outs, _):
        x_ref, mod_ref, g_ref, win_ref, wout_ref = ins
        xo_ref, gu_ref, y_ref = outs
        xv = x_ref[...]
        sh, sc, gt = mod_ref[0:1, :], mod_ref[1:2, :], mod_ref[2:3, :]
        r = lax.rsqrt(_rowmean(xv * xv) + EPS)
        h = (xv * r * g_ref[0:1, :]) * (1.0 + sc) + sh
        hb = h.astype(BF16)
        y = jnp.zeros((tm, D), F32)
        for cidx in range(NCH):
            gate = _dot_nt(hb, win_ref[cidx])
            up = _dot_nt(hb, win_ref[NCH + cidx])
            gu_ref[cidx] = gate.astype(BF16)
            gu_ref[NCH + cidx] = up.astype(BF16)
            act = gate * _sigmoid(gate) * up
            y = y + _dot(act.astype(BF16), wout_ref[cidx])
        y_ref[...] = y
        ry = lax.rsqrt(_rowmean(y * y) + EPS)
        xo_ref[...] = xv + (HALF * gt) * (y * ry * g_ref[1:2, :])

    tile = pl.BlockSpec((tm, D), lambda i: (i, 0))
    return _call(
        core, name=name, grid=(nt,), jobs=jobs,
        in_specs=[tile, pl.BlockSpec((None, 8, D), lambda i: (i // tps, 0, 0)), _const_spec((8, D)),
                  _const_spec((8, FB, D)), _const_spec((4, FB, D))],
        out_specs=[tile, pl.BlockSpec((8, tm, FB), lambda i: (0, i, 0)), tile],
        out_shape=[jax.ShapeDtypeStruct((T, D), F32), jax.ShapeDtypeStruct((8, T, FB), BF16),
                   jax.ShapeDtypeStruct((T, D), F32)],
        args=[x, mod, gvec, w_in, w_out])


def _ffn_bwd(dxo, x, y, gu, mod, gvec, w_in, w_out, tm, name, jobs=()):
    T = x.shape[0]
    nt = T // tm
    nb = mod.shape[0]
    tps = nt // nb

    def core(ins, outs, _):
        dxo_ref, x_ref, y_ref, gu_ref, mod_ref, g_ref, win_ref, wout_ref = ins
        dx_ref, dg_ref, act_ref, hb_ref, dyb_ref, mg_ref, vg_ref = outs
        i = pl.program_id(0)
        xv = x_ref[...]
        dxo_v = dxo_ref[...]
        yv = y_ref[...]
        sh, sc, gt = mod_ref[0:1, :], mod_ref[1:2, :], mod_ref[2:3, :]
        gpre, gpost = g_ref[0:1, :], g_ref[1:2, :]
        r = lax.rsqrt(_rowmean(xv * xv) + EPS)
        xh = xv * r
        n = xh * gpre
        hb = (n * (1.0 + sc) + sh).astype(BF16)
        hb_ref[...] = hb
        ry = lax.rsqrt(_rowmean(yv * yv) + EPS)
        yh = yv * ry
        d_gt = _colsum(HALF * dxo_v * (yh * gpost))
        dp = (HALF * gt) * dxo_v
        d_gpost = _colsum(dp * yh)
        dyh = dp * gpost
        dy = ry * (dyh - yh * _rowmean(dyh * yh))
        dyb = dy.astype(BF16)
        dyb_ref[...] = dyb
        dh = jnp.zeros((tm, D), F32)
        for cidx in range(NCH):
            gate = gu_ref[cidx].astype(F32)
            up = gu_ref[NCH + cidx].astype(F32)
            sig = _sigmoid(gate)
            s = gate * sig
            act_ref[cidx] = (s * up).astype(BF16)
            d_act = _dot_nt(dyb, wout_ref[cidx])
            d_up = (d_act * s).astype(BF16)
            d_gate = (d_act * up * (sig * (1.0 + gate * (1.0 - sig)))).astype(BF16)
            dg_ref[cidx] = d_gate
            dg_ref[NCH + cidx] = d_up
            dh = dh + _dot(d_gate, win_ref[cidx]) + _dot(d_up, win_ref[NCH + cidx])
        d_sc = _colsum(dh * n)
        d_sh = _colsum(dh)
        dn = dh * (1.0 + sc)
        d_gpre = _colsum(dn * xh)
        dxh = dn * gpre
        dx_ref[...] = dxo_v + r * (dxh - xh * _rowmean(dxh * xh))

        @pl.when(i % tps == 0)
        def _():
            mg_ref[...] = jnp.zeros((8, D), F32)

        @pl.when(i == 0)
        def _():
            vg_ref[...] = jnp.zeros((8, D), F32)

        mg_ref[0:1, :] += d_sh
        mg_ref[1:2, :] += d_sc
        mg_ref[2:3, :] += d_gt
        vg_ref[0:1, :] += d_gpre
        vg_ref[1:2, :] += d_gpost

    tile = pl.BlockSpec((tm, D), lambda i: (i, 0))
    return _call(
        core, name=name, grid=(nt,), jobs=jobs,
        in_specs=[tile, tile, tile, pl.BlockSpec((8, tm, FB), lambda i: (0, i, 0)),
                  pl.BlockSpec((None, 8, D), lambda i: (i // tps, 0, 0)), _const_spec((8, D)),
                  _const_spec((8, FB, D)), _const_spec((4, FB, D))],
        out_specs=[tile, pl.BlockSpec((8, tm, FB), lambda i: (0, i, 0)),
                   pl.BlockSpec((4, tm, FB), lambda i: (0, i, 0)), tile, tile,
                   pl.BlockSpec((None, 8, D), lambda i: (i // tps, 0, 0)), pl.BlockSpec((8, D), lambda i: (0, 0))],
        out_shape=[jax.ShapeDtypeStruct((T, D), F32), jax.ShapeDtypeStruct((8, T, FB), BF16),
                   jax.ShapeDtypeStruct((4, T, FB), BF16), jax.ShapeDtypeStruct((T, D), BF16),
                   jax.ShapeDtypeStruct((T, D), BF16), jax.ShapeDtypeStruct((nb, 8, D), F32),
                   jax.ShapeDtypeStruct((8, D), F32)],
        args=[dxo, x, y, gu, mod, gvec, w_in, w_out])


def _ffn_last(x, target, mod, gvec, w_in, w_out, tm, name, jobs=()):
    T = x.shape[0]
    nt = T // tm
    nb = mod.shape[0]
    tps = nt // nb

    def core(ins, outs, scs):
        x_ref, t_ref, mod_ref, g_ref, wina_ref, winb_ref, wout_ref = ins
        dx_ref, dg_ref, act_ref, hb_ref, dyb_ref, mg_ref, vg_ref, loss_ref = outs
        hd2 = w_in[0].shape[2]
        (gu_s,) = scs
        i = pl.program_id(0)
        xv = x_ref[...]
        sh, sc, gt = mod_ref[0:1, :], mod_ref[1:2, :], mod_ref[2:3, :]
        gpre, gpost = g_ref[0:1, :], g_ref[1:2, :]
        r = lax.rsqrt(_rowmean(xv * xv) + EPS)
        xh = xv * r
        n = xh * gpre
        hb = (n * (1.0 + sc) + sh).astype(BF16)
        hb_ref[...] = hb
        hba, hbb = hb[:, 0:hd2], hb[:, hd2:D]
        yv = jnp.zeros((tm, D), F32)
        for cidx in range(NCH):
            gate = _dot_nt(hba, wina_ref[cidx]) + _dot_nt(hbb, winb_ref[cidx])
            up = _dot_nt(hba, wina_ref[NCH + cidx]) + _dot_nt(hbb, winb_ref[NCH + cidx])
            gu_s[cidx] = gate.astype(BF16)
            gu_s[NCH + cidx] = up.astype(BF16)
            act = gate * _sigmoid(gate) * up
            act_ref[cidx] = act.astype(BF16)
            yv = yv + _dot(act_ref[cidx], wout_ref[cidx])
        ry = lax.rsqrt(_rowmean(yv * yv) + EPS)
        yh = yv * ry
        pn = yh * gpost
        err = xv + (HALF * gt) * pn - t_ref[...]
        dxo_v = err * (1.0 / D)
        d_gt = _colsum(HALF * dxo_v * pn)
        dp = (HALF * gt) * dxo_v
        d_gpost = _colsum(dp * yh)
        dyh = dp * gpost
        dyb = (ry * (dyh - yh * _rowmean(dyh * yh))).astype(BF16)
        dyb_ref[...] = dyb
        dha = jnp.zeros((tm, hd2), F32)
        dhb = jnp.zeros((tm, D - hd2), F32)
        for cidx in range(NCH):
            gate = gu_s[cidx].astype(F32)
            up = gu_s[NCH + cidx].astype(F32)
            sig = _sigmoid(gate)
            s = gate * sig
            d_act = _dot_nt(dyb, wout_ref[cidx])
            d_up = (d_act * s).astype(BF16)
            d_gate = (d_act * up * (sig * (1.0 + gate * (1.0 - sig)))).astype(BF16)
            dg_ref[cidx] = d_gate
            dg_ref[NCH + cidx] = d_up
            dha = dha + _dot(d_gate, wina_ref[cidx]) + _dot(d_up, wina_ref[NCH + cidx])
            dhb = dhb + _dot(d_gate, winb_ref[cidx]) + _dot(d_up, winb_ref[NCH + cidx])
        dh = jnp.concatenate([dha, dhb], axis=1)
        d_sc = _colsum(dh * n)
        d_sh = _colsum(dh)
        dn = dh * (1.0 + sc)
        d_gpre = _colsum(dn * xh)
        dxh = dn * gpre
        dx_ref[...] = dxo_v + r * (dxh - xh * _rowmean(dxh * xh))

        @pl.when(i % tps == 0)
        def _():
            mg_ref[...] = jnp.zeros((8, D), F32)

        @pl.when(i == 0)
        def _():
            vg_ref[...] = jnp.zeros((8, D), F32)
            loss_ref[...] = jnp.zeros((8, D), F32)

        mg_ref[0:1, :] += d_sh
        mg_ref[1:2, :] += d_sc
        mg_ref[2:3, :] += d_gt
        vg_ref[0:1, :] += d_gpre
        vg_ref[1:2, :] += d_gpost
        loss_ref[...] += HALF * jnp.sum(_rowmean(err * err), axis=0, keepdims=True)

    tile = pl.BlockSpec((tm, D), lambda i: (i, 0))
    return _call(
        core, name=name, grid=(nt,), jobs=jobs,
        in_specs=[tile, tile, pl.BlockSpec((None, 8, D), lambda i: (i // tps, 0, 0)), _const_spec((8, D)),
                  _const_spec(w_in[0].shape), _const_spec(w_in[1].shape), _const_spec((4, FB, D))],
        out_specs=[tile, pl.BlockSpec((8, tm, FB), lambda i: (0, i, 0)),
                   pl.BlockSpec((4, tm, FB), lambda i: (0, i, 0)), tile, tile,
                   pl.BlockSpec((None, 8, D), lambda i: (i // tps, 0, 0)), pl.BlockSpec((8, D), lambda i: (0, 0)),
                   pl.BlockSpec((8, D), lambda i: (0, 0))],
        out_shape=[jax.ShapeDtypeStruct((T, D), F32), jax.ShapeDtypeStruct((8, T, FB), BF16),
                   jax.ShapeDtypeStruct((4, T, FB), BF16), jax.ShapeDtypeStruct((T, D), BF16),
                   jax.ShapeDtypeStruct((T, D), BF16), jax.ShapeDtypeStruct((nb, 8, D), F32),
                   jax.ShapeDtypeStruct((8, D), F32), jax.ShapeDtypeStruct((8, D), F32)],
        scratch=[pltpu.VMEM((8, tm, FB), BF16)],
        args=[x, target, mod, gvec, w_in[0], w_in[1], w_out])


def _masked_spatial(ws_ref):
    row = lax.broadcasted_iota(jnp.int32, (CHUNK, CHUNK), 0)
    col = lax.broadcasted_iota(jnp.int32, (CHUNK, CHUNK), 1)
    keep = col <= row
    return [jnp.where(keep, ws_ref[hd], 0.0).astype(BF16) for hd in range(NHEAD)]


def _head_pairs(mats, right, transpose=False):
    first = lax.broadcasted_iota(jnp.int32, (CHUNK, LANES), 1) < HD
    op = _dot_tn if transpose else _dot
    out = []
    for p in range(NHEAD // 2):
        slab = right[:, _lanes(p)]
        out.append(jnp.where(first, op(mats[2 * p], slab), op(mats[2 * p + 1], slab)))
    return jnp.concatenate(out, axis=1)


def _spatial_gate(wm, vb_chunk):
    return _head_pairs(wm, vb_chunk)


def _layer_norm_stats(v):
    mu = _rowmean(v)
    vc = v - mu
    rstd = lax.rsqrt(_rowmean(vc * vc) + EPS)
    return vc * rstd, rstd


def _pitch(tm):
    p = tm // 8
    while p % 8 != 4:
        p += 1
    return p


def _lanes(s):
    return slice(s * LANES, (s + 1) * LANES)


def _to_slabs(ref, row0, val):
    for s in range(NSLAB):
        ref[s, row0:row0 + val.shape[0], :] = val[:, _lanes(s)]


def _tap_sum(src, out, cw_ref, bias, tm, start):
    p = _pitch(tm)
    for s in range(NSLAB):
        accs = [jnp.broadcast_to(bias[:, _lanes(s)], (SUBL, LANES))] * p
        for k in range(CONV_K):
            w = jnp.broadcast_to(cw_ref[k:k + 1, _lanes(s)], (SUBL, LANES))
            for v in range(p):
                accs[v] = accs[v] + w * src[s, pl.ds(v + start(k), 8, stride=p), :]
        for v in range(p):
            out[s, pl.ds(v, 8, stride=p), :] = accs[v]
    return jnp.concatenate([out[s, 0:tm, :] for s in range(NSLAB)], axis=1)


def _mixer_fwd(x, mod, gvec, w_mi, w_mo, v512, ws, bias_full, cw, tm, name, jobs=()):
    T = x.shape[0]
    nt = T // tm
    tps = nt // mod.shape[0]
    ext_rows = 8 * _pitch(tm)

    def core(ins, outs, scs):
        x_ref, mod_ref, g_ref, wmi_ref, wmo_ref, v_ref, ws_ref, bias_ref, cw_ref = ins
        xo_ref, proj_ref, ym_ref, conv_ref = outs
        glu_ext, conv_scr = scs
        i = pl.program_id(0)
        xv = x_ref[...]
        sh, sc, gt = mod_ref[0:1, :], mod_ref[1:2, :], mod_ref[2:3, :]
        r = lax.rsqrt(_rowmean(xv * xv) + EPS)
        hb = ((xv * r * g_ref[0:1, :]) * (1.0 + sc) + sh).astype(BF16)
        for j in range(NDEV):
            proj_ref[:, j * MB:(j + 1) * MB] = _dot(hb, wmi_ref[j])
        u = proj_ref[:, 0:WA]
        v0 = proj_ref[:, WA:2 * WA]
        a = proj_ref[:, 2 * WA:3 * WA]
        g = proj_ref[:, 3 * WA:4 * WA]
        vh, _ = _layer_norm_stats(v0)
        vb = (vh * v_ref[0:1, :] + v_ref[1:2, :]).astype(BF16)
        wm = _masked_spatial(ws_ref)
        ya = []
        for q in range(tm // CHUNK):
            z = _spatial_gate(wm, vb[q * CHUNK:(q + 1) * CHUNK, :]) + bias_ref[...]
            ya.append(u[q * CHUNK:(q + 1) * CHUNK, :] * z)
        ya = jnp.concatenate(ya, axis=0)
        glu = a * _sigmoid(g)

        @pl.when(i == 0)
        def _():
            glu_ext[:, HALO + tm:HALO + ext_rows, :] = jnp.zeros((NSLAB, ext_rows - tm, LANES), F32)

        @pl.when(i % tps == 0)
        def _():
            glu_ext[:, 0:HALO, :] = jnp.zeros((NSLAB, HALO, LANES), F32)

        _to_slabs(glu_ext, HALO, glu)
        conv = _tap_sum(glu_ext, conv_scr, cw_ref, v_ref[2:3, :], tm, lambda k: HALO - (CONV_K - 1) + k)
        conv_ref[...] = conv
        glu_ext[:, 0:HALO, :] = glu_ext[:, tm:tm + HALO, :]
        ch, _ = _layer_norm_stats(conv)
        cn = ch * v_ref[3:4, :] + v_ref[4:5, :]
        yb = cn * _sigmoid(cn)
        pa = ya * lax.rsqrt(_rowmean(ya * ya) + EPS) * v_ref[5:6, :]
        pb = yb * lax.rsqrt(_rowmean(yb * yb) + EPS) * v_ref[6:7, :]
        ycat = jnp.concatenate([pa, pb], axis=1).astype(BF16)
        ym = _dot(ycat, wmo_ref[...])
        ym_ref[...] = ym
        rm = lax.rsqrt(_rowmean(ym * ym) + EPS)
        xo_ref[...] = xv + gt * (ym * rm * g_ref[1:2, :])

    tile = pl.BlockSpec((tm, D), lambda i: (i, 0))
    return _call(
        core, name=name, grid=(nt,), jobs=jobs,
        in_specs=[tile, pl.BlockSpec((None, 8, D), lambda i: (i // tps, 0, 0)), _const_spec((8, D)),
                  _const_spec((NDEV, D, MB)), _const_spec((D, D)), _const_spec((8, WA)),
                  _const_spec((NHEAD, CHUNK, CHUNK)), _const_spec((CHUNK, WA)), _const_spec((32, WA))],
        out_specs=[tile, pl.BlockSpec((tm, 4 * WA), lambda i: (i, 0)), tile, pl.BlockSpec((tm, WA), lambda i: (i, 0))],
        out_shape=[jax.ShapeDtypeStruct((T, D), F32), jax.ShapeDtypeStruct((T, 4 * WA), F32),
                   jax.ShapeDtypeStruct((T, D), F32), jax.ShapeDtypeStruct((T, WA), F32)],
        scratch=[pltpu.VMEM((NSLAB, HALO + ext_rows, LANES), F32), pltpu.VMEM((NSLAB, ext_rows, LANES), F32)],
        args=[x, mod, gvec, w_mi, w_mo, v512, ws, bias_full, cw])


def _mixer_bwd_a(dxo, ym, proj, conv, mod, gvec, w_mo, v512, ws, bias_full, esel, tm, name, jobs=()):
    T = dxo.shape[0]
    nt = T // tm
    nb = mod.shape[0]
    tps = nt // nb

    def core(ins, outs, scs):
        dxo_ref, ym_ref, proj_ref, conv_ref, mod_ref, g_ref, wmo_ref, v_ref, ws_ref, bias_ref, e_ref = ins
        dpart_ref, dymb_ref, ycat_ref, mg_ref, vg_ref, v5g_ref, gws_ref, gbs_ref = outs
        (dbs_acc,) = scs
        i = pl.program_id(0)
        dxo_v = dxo_ref[...]
        ymv = ym_ref[...]
        gt = mod_ref[2:3, :]
        gpost = g_ref[1:2, :]
        rm = lax.rsqrt(_rowmean(ymv * ymv) + EPS)
        ymh = ymv * rm
        d_gt = _colsum(dxo_v * (ymh * gpost))
        dpm = gt * dxo_v
        d_gpost = _colsum(dpm * ymh)
        dymh = dpm * gpost
        dym = (rm * (dymh - ymh * _rowmean(dymh * ymh))).astype(BF16)
        dymb_ref[...] = dym
        dycat = _dot_nt(dym, wmo_ref[...])
        u = proj_ref[:, 0:WA]
        v0 = proj_ref[:, WA:2 * WA]
        vh, rv = _layer_norm_stats(v0)
        vb = (vh * v_ref[0:1, :] + v_ref[1:2, :]).astype(BF16)
        wm = _masked_spatial(ws_ref)
        zs = []
        for q in range(tm // CHUNK):
            zs.append(_spatial_gate(wm, vb[q * CHUNK:(q + 1) * CHUNK, :]) + bias_ref[...])
        z = jnp.concatenate(zs, axis=0)
        ya = u * z
        ra = lax.rsqrt(_rowmean(ya * ya) + EPS)
        yah = ya * ra
        ch, rc = _layer_norm_stats(conv_ref[...])
        cn = ch * v_ref[3:4, :] + v_ref[4:5, :]
        sg = _sigmoid(cn)
        yb = cn * sg
        rb = lax.rsqrt(_rowmean(yb * yb) + EPS)
        ybh = yb * rb
        ycat_ref[...] = jnp.concatenate([yah * v_ref[5:6, :], ybh * v_ref[6:7, :]], axis=1).astype(BF16)
        dpa = dycat[:, 0:WA]
        dpb = dycat[:, WA:2 * WA]
        d_goa = _colsum(dpa * yah)
        d_gob = _colsum(dpb * ybh)
        dyah = dpa * v_ref[5:6, :]
        dybh = dpb * v_ref[6:7, :]
        dya = ra * (dyah - yah * _rowmean(dyah * yah))
        dyb = rb * (dybh - ybh * _rowmean(dybh * ybh))
        dpart_ref[:, 0:WA] = dya * z
        dz = dya * u

        @pl.when(i == 0)
        def _():
            gws_ref[...] = jnp.zeros((NHEAD, CHUNK, CHUNK), F32)
            dbs_acc[...] = jnp.zeros((CHUNK, WA), F32)
            vg_ref[...] = jnp.zeros((8, D), F32)
            v5g_ref[...] = jnp.zeros((8, WA), F32)

        first = lax.broadcasted_iota(jnp.int32, (CHUNK, LANES), 1) < HD
        dvs = []
        for q in range(tm // CHUNK):
            dz_q = dz[q * CHUNK:(q + 1) * CHUNK, :]
            vb_q = vb[q * CHUNK:(q + 1) * CHUNK, :]
            dbs_acc[...] += dz_q
            dzb = dz_q.astype(BF16)
            dvs.append(_head_pairs(wm, dzb, transpose=True))
            for hd in range(NHEAD):
                slab = dzb[:, _lanes(hd // 2)]
                dz_hd = jnp.where(first if hd % 2 == 0 else jnp.logical_not(first), slab, jnp.zeros_like(slab))
                gws_ref[hd] += _dot_nt(dz_hd, vb_q[:, _lanes(hd // 2)])
        dv = jnp.concatenate(dvs, axis=0)
        d_gng = _colsum(dv * vh)
        d_gnb = _colsum(dv)
        dvh = dv * v_ref[0:1, :]
        dpart_ref[:, WA:2 * WA] = rv * (dvh - _rowmean(dvh) - vh * _rowmean(dvh * vh))
        dcn = dyb * (sg * (1.0 + cn * (1.0 - sg)))
        d_cng = _colsum(dcn * ch)
        d_cnb = _colsum(dcn)
        dch = dcn * v_ref[3:4, :]
        dconv = rc * (dch - _rowmean(dch) - ch * _rowmean(dch * ch))
        dpart_ref[:, 2 * WA:3 * WA] = dconv
        dpart_ref[:, 3 * WA:4 * WA] = jnp.zeros((tm, WA), F32)
        d_cb = _colsum(dconv)

        @pl.when(i % tps == 0)
        def _():
            mg_ref[...] = jnp.zeros((8, D), F32)

        mg_ref[2:3, :] += d_gt
        vg_ref[1:2, :] += d_gpost
        v5g_ref[0:1, :] += d_gng
        v5g_ref[1:2, :] += d_gnb
        v5g_ref[2:3, :] += d_cb
        v5g_ref[3:4, :] += d_cng
        v5g_ref[4:5, :] += d_cnb
        v5g_ref[5:6, :] += d_goa
        v5g_ref[6:7, :] += d_gob

        @pl.when(i == nt - 1)
        def _():
            row = lax.broadcasted_iota(jnp.int32, (CHUNK, CHUNK), 0)
            col = lax.broadcasted_iota(jnp.int32, (CHUNK, CHUNK), 1)
            for hd in range(NHEAD):
                gws_ref[hd] = jnp.where(col <= row, gws_ref[hd], 0.0)
            gbs_ref[...] = lax.dot_general(e_ref[...], dbs_acc[...], (((1,), (1,)), ((), ())),
                                           precision=lax.Precision.HIGHEST, preferred_element_type=F32)

    tile = pl.BlockSpec((tm, D), lambda i: (i, 0))
    ptile = pl.BlockSpec((tm, 4 * WA), lambda i: (i, 0))
    return _call(
        core, name=name, grid=(nt,), jobs=jobs,
        in_specs=[tile, tile, pl.BlockSpec((tm, 2 * WA), lambda i: (i, 0)), pl.BlockSpec((tm, WA), lambda i: (i, 0)),
                  pl.BlockSpec((None, 8, D), lambda i: (i // tps, 0, 0)), _const_spec((8, D)), _const_spec((D, D)),
                  _const_spec((8, WA)), _const_spec((NHEAD, CHUNK, CHUNK)), _const_spec((CHUNK, WA)),
                  _const_spec((8, WA))],
        out_specs=[ptile, tile, tile, pl.BlockSpec((None, 8, D), lambda i: (i // tps, 0, 0)),
                   pl.BlockSpec((8, D), lambda i: (0, 0)), pl.BlockSpec((8, WA), lambda i: (0, 0)),
                   pl.BlockSpec((NHEAD, CHUNK, CHUNK), lambda i: (0, 0, 0)), pl.BlockSpec((8, CHUNK), lambda i: (0, 0))],
        out_shape=[jax.ShapeDtypeStruct((T, 4 * WA), F32), jax.ShapeDtypeStruct((T, D), BF16),
                   jax.ShapeDtypeStruct((T, D), BF16), jax.ShapeDtypeStruct((nb, 8, D), F32),
                   jax.ShapeDtypeStruct((8, D), F32), jax.ShapeDtypeStruct((8, WA), F32),
                   jax.ShapeDtypeStruct((NHEAD, CHUNK, CHUNK), F32), jax.ShapeDtypeStruct((8, CHUNK), F32)],
        scratch=[pltpu.VMEM((CHUNK, WA), F32)],
        args=[dxo, ym, proj, conv, mod, gvec, w_mo, v512, ws, bias_full, esel])


def _mixer_bwd_b(dxo, x, dpart, proj, mod, gvec, w_mi, cw, tm, name, jobs=()):
    T = x.shape[0]
    nt = T // tm
    nb = mod.shape[0]
    tps = nt // nb
    hpt = tm // HALO
    nh = T // HALO
    off = HALO - (CONV_K - 1)
    p = _pitch(tm)
    ext_rows = 8 * p

    def core(ins, outs, scs):
        dxo_ref, x_ref, dpart_ref, dnext_ref, ag_ref, halo_ref, mod_ref, g_ref, wmi_ref, cw_ref = ins
        dx_ref, dproj_ref, hb_ref, mg_ref, vg_ref, dcw_ref = outs
        glu_ext, dconv_ext, dglu_scr, dcw_acc = scs
        i = pl.program_id(0)
        first = i % tps == 0
        last = i % tps == tps - 1
        a = ag_ref[:, 0:WA]
        g = ag_ref[:, WA:2 * WA]
        sgg = _sigmoid(g)

        @pl.when(i == 0)
        def _():
            glu_ext[:, HALO + tm:HALO + ext_rows, :] = jnp.zeros((NSLAB, ext_rows - tm, LANES), F32)
            dconv_ext[:, HALO + tm:HALO + ext_rows, :] = jnp.zeros((NSLAB, ext_rows - tm, LANES), F32)
            dcw_acc[...] = jnp.zeros((32, 8, WA), F32)
            vg_ref[...] = jnp.zeros((8, D), F32)

        _to_slabs(glu_ext, 0, jnp.where(first, 0.0, halo_ref[:, 0:WA] * _sigmoid(halo_ref[:, WA:2 * WA])))
        _to_slabs(glu_ext, HALO, a * sgg)
        _to_slabs(dconv_ext, 0, dpart_ref[:, 2 * WA:3 * WA])
        _to_slabs(dconv_ext, tm, jnp.where(last, 0.0, dnext_ref[...]))
        sub = lax.broadcasted_iota(jnp.int32, (SUBL, LANES), 0)
        for s in range(NSLAB):
            accs = [jnp.zeros((SUBL, LANES), F32)] * CONV_K
            for v in range(p):
                dc = jnp.where(v + p * sub < tm, dconv_ext[s, pl.ds(v, 8, stride=p), :], 0.0)
                for k in range(CONV_K):
                    accs[k] = accs[k] + dc * glu_ext[s, pl.ds(v + off + k, 8, stride=p), :]
            for k in range(CONV_K):
                dcw_acc[k, :, _lanes(s)] += accs[k]
        dglu = _tap_sum(dconv_ext, dglu_scr, cw_ref, jnp.zeros((1, WA), F32), tm, lambda k: (CONV_K - 1) - k)

        @pl.when(i == nt - 1)
        def _():
            for k in range(CONV_K):
                dcw_ref[k:k + 1, :] = jnp.sum(dcw_acc[k], axis=0, keepdims=True)
            dcw_ref[CONV_K:32, :] = jnp.zeros((32 - CONV_K, WA), F32)

        da = dglu * sgg
        dgg = dglu * a * (sgg * (1.0 - sgg))
        dproj_ref[:, 0:2 * WA] = dpart_ref[:, 0:2 * WA].astype(BF16)
        dproj_ref[:, 2 * WA:3 * WA] = da.astype(BF16)
        dproj_ref[:, 3 * WA:4 * WA] = dgg.astype(BF16)
        dh = jnp.zeros((tm, D), F32)
        for j in range(NDEV):
            dh = dh + _dot_nt(dproj_ref[:, j * MB:(j + 1) * MB], wmi_ref[j])
        xv = x_ref[...]
        sc, sh = mod_ref[1:2, :], mod_ref[0:1, :]
        gpre = g_ref[0:1, :]
        r = lax.rsqrt(_rowmean(xv * xv) + EPS)
        xh = xv * r
        n = xh * gpre
        hb_ref[...] = (n * (1.0 + sc) + sh).astype(BF16)
        d_sc = _colsum(dh * n)
        d_sh = _colsum(dh)
        dn = dh * (1.0 + sc)
        d_gpre = _colsum(dn * xh)
        dxh = dn * gpre
        dx_ref[...] = dxo_ref[...] + r * (dxh - xh * _rowmean(dxh * xh))

        @pl.when(first)
        def _():
            mg_ref[...] = jnp.zeros((8, D), F32)

        mg_ref[0:1, :] += d_sh
        mg_ref[1:2, :] += d_sc
        vg_ref[0:1, :] += d_gpre

    tile = pl.BlockSpec((tm, D), lambda i: (i, 0))
    return _call(
        core, name=name, grid=(nt,), jobs=jobs,
        in_specs=[tile, tile, pl.BlockSpec((tm, 4 * WA), lambda i: (i, 0)),
                  pl.BlockSpec((HALO, WA), lambda i: (jnp.minimum((i + 1) * hpt, nh - 1), 2)),
                  pl.BlockSpec((tm, 2 * WA), lambda i: (i, 1)),
                  pl.BlockSpec((HALO, 2 * WA), lambda i: (jnp.maximum(i * hpt - 1, 0), 1)),
                  pl.BlockSpec((None, 8, D), lambda i: (i // tps, 0, 0)), _const_spec((8, D)),
                  _const_spec((NDEV, D, MB)), _const_spec((32, WA))],
        out_specs=[tile, pl.BlockSpec((tm, 4 * WA), lambda i: (i, 0)), tile,
                   pl.BlockSpec((None, 8, D), lambda i: (i // tps, 0, 0)), pl.BlockSpec((8, D), lambda i: (0, 0)),
                   pl.BlockSpec((32, WA), lambda i: (0, 0))],
        out_shape=[jax.ShapeDtypeStruct((T, D), F32), jax.ShapeDtypeStruct((T, 4 * WA), BF16),
                   jax.ShapeDtypeStruct((T, D), BF16), jax.ShapeDtypeStruct((nb, 8, D), F32),
                   jax.ShapeDtypeStruct((8, D), F32), jax.ShapeDtypeStruct((32, WA), F32)],
        scratch=[pltpu.VMEM((NSLAB, HALO + ext_rows, LANES), F32), pltpu.VMEM((NSLAB, HALO + ext_rows, LANES), F32),
                 pltpu.VMEM((NSLAB, ext_rows, LANES), F32), pltpu.VMEM((32, 8, WA), F32)],
        args=[dxo, x, dpart, dpart, proj, proj, mod, gvec, w_mi, cw])


def _grad_chip(a, b, a_spec, b_spec, prod_shape, half, name, jobs=(), via_b=False):
    steps = 8 if half is None else 4
    R = prod_shape[0] if half is None else half
    C = prod_shape[1]

    def core(ins, outs, scs):
        a_ref, b_ref = ins
        (o_ref,) = outs
        own, snd, rcv, ssem, rsem, lsem = scs
        s = pl.program_id(0)
        c = lax.axis_index("c")
        me = _me()
        sib = _flip(me, (0, 0, 1))
        if via_b:
            prod = _dot_tn(b_ref[...], a_ref[...]).T.astype(BF16)
        else:
            prod = _dot_tn(a_ref[...], b_ref[...]).astype(BF16)
        if half is None:
            q = s // 2

            @pl.when(s % 2 == c)
            def _():
                own[q] = prod

            @pl.when(s % 2 != c)
            def _():
                snd[q] = prod
                _remote(snd.at[q], rcv.at[q], ssem.at[q], rsem.at[q], sib).start()
        else:
            lo = prod[0:half, :]
            hi = prod[half:2 * half, :]
            own[s] = jnp.where(c == 0, lo, hi)
            snd[s] = jnp.where(c == 0, hi, lo)
            _remote(snd.at[s], rcv.at[s], ssem.at[s], rsem.at[s], sib).start()

        @pl.when(s == steps - 1)
        def _():
            for q4 in range(4):
                cp = _remote(snd.at[q4], rcv.at[q4], ssem.at[q4], rsem.at[q4], sib)
                cp.wait_recv()
                cp.wait_send()
                snd[q4] = (own[q4].astype(F32) + rcv[q4].astype(F32)).astype(BF16)
            out = pltpu.make_async_copy(snd, o_ref, lsem)
            out.start()
            out.wait()

    return _call(
        core, name=name, grid=(steps,), jobs=jobs, in_specs=[a_spec, b_spec], out_specs=[HBM],
        out_shape=[jax.ShapeDtypeStruct((4, R, C), BF16)],
        scratch=[pltpu.VMEM((4, R, C), BF16), pltpu.VMEM((4, R, C), BF16), pltpu.VMEM((4, R, C), BF16),
                 pltpu.SemaphoreType.DMA((4,)), pltpu.SemaphoreType.DMA((4,)), pltpu.SemaphoreType.DMA],
        args=[a, b])


def _grad_w_in(dg, hb, name, jobs=()):
    T = hb.shape[0]
    return _grad_chip(dg, hb, pl.BlockSpec((None, T, FB), lambda s: (s, 0, 0)), _const_spec((T, D)),
                      (FB, D), None, name, jobs)


def _grad_w_out(act, dyb, name, jobs=()):
    T = dyb.shape[0]
    return _grad_chip(act, dyb, pl.BlockSpec((None, T, FB), lambda s: (s, 0, 0)), _const_spec((T, D)),
                      (FB, D), FO, name, jobs)


def _grad_w_mi(hb, dproj, name, jobs=()):
    T = hb.shape[0]
    return _grad_chip(hb, dproj, _const_spec((T, D)), pl.BlockSpec((T, MB), lambda s: (0, s)),
                      (D, MB), None, name, jobs, via_b=True)


def _grad_w_mo(ycat, dym, name, jobs=()):
    T = ycat.shape[0]
    return _grad_chip(ycat, dym, pl.BlockSpec((T, 2 * MO), lambda s: (0, s)), _const_spec((T, D)),
                      (2 * MO, D), MO, name, jobs)


def _adamw_math(w, g, m, v):
    m2 = ADAM_B1 * m + (1.0 - ADAM_B1) * g
    v2 = ADAM_B2 * v + (1.0 - ADAM_B2) * (g * g)
    m_hat = m2 / (1.0 - ADAM_B1 ** ADAM_STEP)
    v_hat = v2 / (1.0 - ADAM_B2 ** ADAM_STEP)
    delta = -ADAM_LR * (m_hat / (jnp.sqrt(v_hat) + ADAM_EPS) + ADAM_WD * w)
    return delta, m2, v2


def _adamw_reduce(parts, w, m, v, tr, name, own=None, after=None):
    R, C = w.shape

    def core(ins, outs, _):
        p_ref, w_ref, m_ref, v_ref = ins[:4]
        g_ref, d_ref, m2_ref, v2_ref = outs
        if own is None:
            terms = [p_ref[s].astype(F32) for s in range(4)]
        else:
            mq = 2 * lax.axis_index("x") + lax.axis_index("y")
            mine = ins[4][...].astype(F32)
            terms = [jnp.where(mq == s, mine, p_ref[s].astype(F32)) for s in range(4)]
        g = terms[0]
        for s in range(1, 4):
            g = g + terms[s]
        g_ref[...] = g
        d_ref[...], m2_ref[...], v2_ref[...] = _adamw_math(w_ref[...], g, m_ref[...], v_ref[...])

    blk = pl.BlockSpec((tr, C), lambda i: (i, 0))
    in_specs = [pl.BlockSpec((4, tr, C), lambda i: (0, i, 0)), blk, blk, blk]
    args = [parts, w, m, v]
    if own is not None:
        mq = 2 * lax.axis_index("x") + lax.axis_index("y")
        in_specs.append(pl.BlockSpec((tr, C), lambda i: (i, 0)))
        args.append(lax.dynamic_index_in_dim(own, mq, 0, keepdims=False))
    if after is not None:
        in_specs.append(HBM)
        args.append(after)
    return _call(
        core, name=name, grid=(R // tr,), in_specs=in_specs,
        out_specs=[blk, blk, blk, blk], out_shape=[jax.ShapeDtypeStruct((R, C), F32)] * 4, args=args)[0]


HBM_ONLY = pl.BlockSpec(memory_space=pltpu.HBM)
SEM = pl.BlockSpec(memory_space=pltpu.SEMAPHORE)
EFFECT = pltpu.SideEffectType.DATAFLOW_SIDE_EFFECTING


def _chip_scatter_start(gs, name):
    n = len(gs)

    def body(*refs):
        g_refs, land_refs = refs[:n], refs[n:2 * n]
        ssem, rsem = refs[2 * n:2 * n + 2]
        token = refs[-1]
        me = _me()
        mq = 2 * me[0] + me[1]
        for k, f in enumerate(CHIP_FLIPS):
            p = _flip(me, f)
            for a in range(n):
                _remote(g_refs[a].at[2 * p[0] + p[1]], land_refs[a].at[mq], ssem.at[3 * a + k], rsem.at[3 * a + k], p).start()
        token[...] = jnp.zeros_like(token)

    gs = [pltpu.with_memory_space_constraint(g, pltpu.HBM) for g in gs]
    lands = [pltpu.with_memory_space_constraint(lax.empty(g.shape, g.dtype), pltpu.HBM) for g in gs]
    res = pl.pallas_call(
        body, name=name,
        out_shape=(pltpu.SemaphoreType.DMA((3 * n,)), pltpu.SemaphoreType.DMA((3 * n,)))
        + tuple(pltpu.HBM(g.shape, g.dtype) for g in gs) * 2 + (jax.ShapeDtypeStruct((SUBL, LANES), F32),),
        in_specs=(HBM_ONLY,) * (2 * n), out_specs=(SEM, SEM) + (HBM_ONLY,) * (2 * n) + (VM,),
        input_output_aliases={a: 2 + a for a in range(2 * n)},
        compiler_params=pltpu.CompilerParams(has_side_effects=EFFECT),
    )(*gs, *lands)
    return res[:-1], res[-1]


def _chip_scatter_wait(handle, after, name):
    ssem, rsem = handle[:2]
    n = (len(handle) - 2) // 2
    thru = handle[2:]

    def body(*refs):
        g_refs, land_refs = refs[:n], refs[n:2 * n]
        ssem, rsem = refs[2 * n:2 * n + 2]
        me = _me()
        mq = 2 * me[0] + me[1]
        for k, f in enumerate(CHIP_FLIPS):
            p = _flip(me, f)
            pq = 2 * p[0] + p[1]
            for a in range(n):
                _remote(g_refs[a].at[pq], land_refs[a].at[mq], ssem.at[3 * a + k], rsem.at[3 * a + k], p).wait_send()
                _remote(g_refs[a].at[mq], land_refs[a].at[pq], ssem.at[3 * a + k], rsem.at[3 * a + k], p).wait_recv()

    res = pl.pallas_call(
        body, name=name,
        out_shape=tuple(pltpu.HBM(t.shape, t.dtype) for t in thru),
        in_specs=(HBM_ONLY,) * (2 * n) + (SEM, SEM, HBM), out_specs=(HBM_ONLY,) * (2 * n),
        input_output_aliases={a: a for a in range(2 * n)},
        compiler_params=pltpu.CompilerParams(has_side_effects=EFFECT),
    )(*thru, ssem, rsem, after)
    return list(res[:n]), list(res[n:])


def _adamw_ada(sc_all, dd, w, m, v, tr, name, after=None):
    R, C = w.shape

    def core(ins, outs, _):
        sc_ref, dd_ref, w_ref, m_ref, v_ref = ins[:5]
        g_ref, d_ref, m2_ref, v2_ref = outs
        g = _dot_tn(sc_ref[...].astype(BF16), dd_ref[...].astype(BF16))
        g_ref[...] = g
        d_ref[...], m2_ref[...], v2_ref[...] = _adamw_math(w_ref[...], g, m_ref[...], v_ref[...])

    blk = pl.BlockSpec((tr, C), lambda i: (i, 0))
    return _call(
        core, name=name, grid=(R // tr,),
        in_specs=[pl.BlockSpec((64, tr), lambda i: (0, i)), pl.BlockSpec((64, C), lambda i: (0, 0)), blk, blk, blk]
        + [HBM] * (after is not None),
        out_specs=[blk, blk, blk, blk], out_shape=[jax.ShapeDtypeStruct((R, C), F32)] * 4,
        args=[sc_all, dd, w, m, v] + [after] * (after is not None))[0]


def _adamw_small(gathered, plain, grads, wmv, emit, name):
    nw = len(grads)
    ng, npl, ne = len(gathered), len(plain), len(emit)

    def core(ins, outs, _):
        srcs = []
        for a in range(ng):
            s = ins[a][0]
            for dev in range(1, NDEV):
                s = s + ins[a][dev]
            srcs.append(s)
        srcs += [ins[ng + a][...] for a in range(npl)]
        w_refs = ins[ng + npl:]
        for e, a in enumerate(emit):
            outs[e][...] = srcs[a]
        for t in range(nw):
            src, row = grads[t]
            g = srcs[src] if row is None else srcs[src][row:row + 1, :]
            w_ref, m_ref, v_ref = w_refs[3 * t:3 * t + 3]
            g_ref, d_ref, m2_ref, v2_ref = outs[ne + 4 * t:ne + 4 * t + 4]
            g_ref[...] = g
            d_ref[...], m2_ref[...], v2_ref[...] = _adamw_math(w_ref[...], g, m_ref[...], v_ref[...])

    out_shape = [jax.ShapeDtypeStruct(gathered[a].shape[1:], F32) for a in emit]
    for t in range(nw):
        out_shape += [jax.ShapeDtypeStruct(wmv[3 * t].shape, F32)] * 4
    return _call(
        core, name=name, grid=(), in_specs=[VM] * (ng + npl + 3 * nw), out_specs=[VM] * (ne + 4 * nw),
        out_shape=out_shape, args=list(gathered) + list(plain) + list(wmv))[0]


def _ada_fwd(c_pad, w_ada, b_cols, cw_pad, jobs=()):
    def core(ins, outs, scs, start_jobs, finish_jobs):
        c_ref, w_ref, b_ref, cwp_ref = ins
        ada_ref, sc_ref, cw_ref = outs
        cbuf, send_buf, ssem, rsem = scs
        me = _me()
        mi = _lin(me)
        cbuf[mi] = c_ref[...]
        cw_ref[mi] = cwp_ref[...]
        peers = [_flip(me, f) for f in FLIPS]
        first = []
        for k, p in enumerate(peers):
            first.append(_remote(cbuf.at[mi], cbuf.at[mi], ssem.at[k], rsem.at[k], p))
            first.append(_remote(cw_ref.at[mi], cw_ref.at[mi], ssem.at[7 + k], rsem.at[7 + k], p))
        for cp in first:
            cp.start()
        start_jobs()
        for k, p in enumerate(peers):
            pi = _lin(p)
            _remote(cbuf.at[pi], cbuf.at[pi], ssem.at[k], rsem.at[k], p).wait_recv()
            _remote(cw_ref.at[pi], cw_ref.at[pi], ssem.at[7 + k], rsem.at[7 + k], p).wait_recv()
        c_all = cbuf[...].reshape(8 * 8, D)
        sc = c_all * _sigmoid(c_all)
        sc_ref[...] = sc
        res = _dot(sc.astype(BF16), w_ref[...].astype(BF16)) + b_ref[...]
        send_buf[...] = res.reshape(8, 8, ADA_B)
        ada_ref[mi] = send_buf[mi]
        second = []
        for k, p in enumerate(peers):
            second.append(_remote(send_buf.at[_lin(p)], ada_ref.at[mi], ssem.at[14 + k], rsem.at[14 + k], p))
        for cp in second:
            cp.start()
        finish_jobs()
        for k, p in enumerate(peers):
            _remote(send_buf.at[mi], ada_ref.at[_lin(p)], ssem.at[14 + k], rsem.at[14 + k], p).wait_recv()
        for cp in first + second:
            cp.wait_send()

    return _call(
        core, name="ada_fwd", grid=(), jobs=jobs, core_starts=True, in_specs=[VM, VM, VM, VM], out_specs=[VM, VM, VM],
        out_shape=[jax.ShapeDtypeStruct((8, 8, ADA_B), F32), jax.ShapeDtypeStruct((64, D), F32),
                   jax.ShapeDtypeStruct((8, 32, 64), F32)],
        scratch=[pltpu.VMEM((8, 8, D), F32), pltpu.VMEM((8, 8, ADA_B), F32),
                 pltpu.SemaphoreType.DMA((21,)), pltpu.SemaphoreType.DMA((21,))],
        args=[c_pad, w_ada, b_cols, cw_pad])


def _ada_bwd(dada, jobs=()):
    def core(ins, outs, scs):
        (d_ref,) = ins
        dd_ref, gb_ref = outs
        rbuf, ssem, rsem = scs
        me = _me()
        mi = _lin(me)
        peers = [_flip(me, f) for f in FLIPS]
        rbuf[mi] = d_ref[mi]
        first = []
        for k, p in enumerate(peers):
            first.append(_remote(d_ref.at[_lin(p)], rbuf.at[mi], ssem.at[k], rsem.at[k], p))
        for cp in first:
            cp.start()
        for k, p in enumerate(peers):
            _remote(d_ref.at[mi], rbuf.at[_lin(p)], ssem.at[k], rsem.at[k], p).wait_recv()
        dd = rbuf[...].reshape(64, ADA_B)
        dd_ref[...] = dd
        gb_ref[...] = jnp.broadcast_to(_colsum(dd), (8, ADA_B))
        for cp in first:
            cp.wait_send()

    return _call(
        core, name="ada_bwd", grid=(), jobs=jobs, in_specs=[VM], out_specs=[VM, VM],
        out_shape=[jax.ShapeDtypeStruct((64, ADA_B), F32), jax.ShapeDtypeStruct((8, ADA_B), F32)],
        scratch=[pltpu.VMEM((8, 8, ADA_B), F32), pltpu.SemaphoreType.DMA((7,)), pltpu.SemaphoreType.DMA((7,))],
        args=[dada])


SMALL_D = ("g_pre_f1", "g_post_f1", "g_pre_m", "g_post_m", "g_pre_f2", "g_post_f2")
SMALL_W = ("gmlp_norm_g", "gmlp_norm_b", "conv_b", "conv_norm_g", "conv_norm_b", "g_out_a", "g_out_b")


def kernel(x, c, w_ada, b_ada, g_pre_f1, g_post_f1, w_f1_in, w_f1_out, g_pre_m, g_post_m, w_mix_in, gmlp_norm_g, gmlp_norm_b, w_spatial, b_spatial, conv_w, conv_b, conv_norm_g, conv_norm_b, g_out_a, g_out_b, w_mix_out, g_pre_f2, g_post_f2, w_f2_in, w_f2_out, loss_target, m_w_ada, m_b_ada, m_g_pre_f1, m_g_post_f1, m_w_f1_in, m_w_f1_out, m_g_pre_m, m_g_post_m, m_w_mix_in, m_gmlp_norm_g, m_gmlp_norm_b, m_w_spatial, m_b_spatial, m_conv_w, m_conv_b, m_conv_norm_g, m_conv_norm_b, m_g_out_a, m_g_out_b, m_w_mix_out, m_g_pre_f2, m_g_post_f2, m_w_f2_in, m_w_f2_out, v_w_ada, v_b_ada, v_g_pre_f1, v_g_post_f1, v_w_f1_in, v_w_f1_out, v_g_pre_m, v_g_post_m, v_w_mix_in, v_gmlp_norm_g, v_gmlp_norm_b, v_w_spatial, v_b_spatial, v_conv_w, v_conv_b, v_conv_norm_g, v_conv_norm_b, v_g_out_a, v_g_out_b, v_w_mix_out, v_g_pre_f2, v_g_post_f2, v_w_f2_in, v_w_f2_out):
    given = dict(locals())
    bl, seq, _ = x.shape
    T = bl * seq
    tm = min(256, seq // 2)
    mi = _lin((lax.axis_index("x"), lax.axis_index("y"), lax.axis_index("c")))

    def shard_in(w):
        return w[0].T.astype(BF16)

    g_f1 = _RelayGather([shard_in(w_f1_in), w_f1_out[0].astype(BF16)], ("rows", "out"))
    s_f2 = shard_in(w_f2_in)
    g_mx = _Gather([w_mix_in[0].astype(BF16), w_mix_out[0].astype(BF16), w_f2_out[0].astype(BF16), s_f2[:, 0:D // 4]],
                   ("rows", "rows", "out", "rows"), late_mid=True)
    g_f2 = _Gather([s_f2[:, D // 4:D]], ("rows",))

    c_pad = jnp.pad(c, ((0, 8 - bl), (0, 0)))
    b_cols = lax.dynamic_slice(b_ada, (0, mi * ADA_B), (1, ADA_B))
    cw_pad = jnp.pad(conv_w[0], ((0, 1), (0, 0)))
    (ada_blk, sc_all, cw_all), ((wi1, wo1),) = _ada_fwd(c_pad, w_ada[0], b_cols, cw_pad, jobs=[g_f1])
    ada = ada_blk[:, 0:bl, :].transpose(1, 0, 2).reshape(bl, 9, D)
    pad5 = jnp.zeros((bl, 5, D), F32)
    mod1 = jnp.concatenate([ada[:, 0:3], pad5], axis=1)
    mod2 = jnp.concatenate([ada[:, 3:6], pad5], axis=1)
    mod3 = jnp.concatenate([ada[:, 6:9], pad5], axis=1)
    cw_full = cw_all.transpose(1, 0, 2).reshape(32, WA)

    zrow = jnp.zeros((1, D), F32)
    gv1 = jnp.concatenate([g_pre_f1, g_post_f1] + [zrow] * 6, axis=0)
    gvm = jnp.concatenate([g_pre_m, g_post_m] + [zrow] * 6, axis=0)
    gv2 = jnp.concatenate([g_pre_f2, g_post_f2] + [zrow] * 6, axis=0)
    v512 = jnp.concatenate([gmlp_norm_g, gmlp_norm_b, conv_b, conv_norm_g, conv_norm_b, g_out_a, g_out_b,
                            jnp.zeros((1, WA), F32)], axis=0)
    ws = w_spatial[0]
    bias_full = jnp.repeat(b_spatial[0].T, HD, axis=1)
    esel = (lax.broadcasted_iota(jnp.int32, (8, WA), 1) // HD == lax.broadcasted_iota(jnp.int32, (8, WA), 0)).astype(F32)

    x0 = x.reshape(T, D)
    (x1, gu1, y1), ((wmi, wmo, wo2, wi2a),) = _ffn_fwd(x0, mod1, gv1, wi1, wo1, tm, "ffn1_fwd", jobs=[g_mx])
    wmo = wmo.reshape(D, D)
    (x2, proj, ym, conv), ((wi2b,),) = _mixer_fwd(x1, mod2, gvm, wmi, wmo, v512, ws, bias_full, cw_full, tm, "mixer_fwd", jobs=[g_f2])

    (dx2, dg2, act2, hb2, dyb2, mg3, vg3, loss_blk), _ = _ffn_last(
        x2, loss_target.reshape(T, D), mod3, gv2, (wi2a, wi2b), wo2, tm, "ffn2_fwd_bwd")
    (g_wi2,), _ = _grad_w_in(dg2, hb2, "ffn2_gw_in")
    (g_wo2,), _ = _grad_w_out(act2, dyb2, "ffn2_gw_out")
    (dpart, dymb, ycat, mg2a, vgma, v5g, gws, gbs), ((p_wo2,),) = _mixer_bwd_a(
        dx2, ym, proj, conv, mod2, gvm, wmo, v512, ws, bias_full, esel, tm, "mixer_bwd_a",
        jobs=[_ChipScatter([g_wo2])])
    (dx1, dproj, hbm, mg2b, vgmb, dcw), ((p_wi2,),) = _mixer_bwd_b(
        dx2, x1, dpart, proj, mod2, gvm, wmi, cw_full, tm, "mixer_bwd_b", jobs=[_ChipScatter([g_wi2])])
    (g_wmi,), _ = _grad_w_mi(hbm, dproj, "mixer_gw_in")
    (g_wmo,), _ = _grad_w_mo(ycat, dymb, "mixer_gw_out")
    p2 = jnp.concatenate([v5g, dcw], axis=0)
    (dx0, dg1, act1, hb1, dyb1, mg1, vg1), _ = _ffn_bwd(dx1, x0, y1, gu1, mod1, gv1, wi1, wo1, tm, "ffn1_bwd")

    dada = jnp.concatenate([mg1[:, 0:3], mg2b[:, 0:2], mg2a[:, 2:3], mg3[:, 0:3]], axis=1)
    dada = dada.reshape(bl, NDEV, ADA_B).transpose(1, 0, 2)
    dada = jnp.pad(dada, ((0, 0), (0, 8 - bl), (0, 0)))
    p1 = jnp.concatenate([vg1[0:2], vgmb[0:1], vgma[1:2], vg3[0:2], loss_blk[0:1], zrow], axis=0)
    (dd_all, gb_own), ((a1,),) = _ada_bwd(dada, jobs=[_AllGather([p1])])

    (g_wo1,), ((p_wmi, p_wmo),) = _grad_w_out(act1, dyb1, "ffn1_gw_out", jobs=[_ChipScatter([g_wmi, g_wmo])])
    (g_wi1,), ((a2, a3, a4, gb_all), (p_wo1,)) = _grad_w_in(
        dg1, hb1, "ffn1_gw_in", jobs=[_Gather([p2, gws, gbs, gb_own], ("rows",) * 4), _ChipScatter([g_wo1])])
    g_bada = gb_all[:, 0, :].reshape(1, 9 * D)

    h_f1, token = _chip_scatter_start([g_wi1], "tail_start")

    res = {}
    quad = _adamw_reduce(p_wi2, w_f2_in[0].T, m_w_f2_in[0].T, v_w_f2_in[0].T, FO, "adamw_w_f2_in", after=token)
    res["w_f2_in"] = tuple(t.T[None] for t in quad)
    for nm, part, tr in (("w_f2_out", p_wo2, FO), ("w_mix_in", p_wmi, 256), ("w_mix_out", p_wmo, MO), ("w_f1_out", p_wo1, FO)):
        quad = _adamw_reduce(part, given[nm][0], given["m_" + nm][0], given["v_" + nm][0], tr, "adamw_" + nm, after=quad[1])
        res[nm] = tuple(t[None] for t in quad)
    quad = _adamw_ada(sc_all, dd_all, w_ada[0], m_w_ada[0], v_w_ada[0], 256, "adamw_w_ada", after=quad[1])
    res["w_ada"] = tuple(t[None] for t in quad)
    (g_wi1,), (p_wi1,) = _chip_scatter_wait(h_f1, quad[1], "tail_wait")
    quad = _adamw_reduce(p_wi1, w_f1_in[0].T, m_w_f1_in[0].T, v_w_f1_in[0].T, FO, "adamw_w_f1_in", own=g_wi1)
    res["w_f1_in"] = tuple(t.T[None] for t in quad)

    small = SMALL_D + SMALL_W + ("w_spatial", "b_spatial", "b_ada")
    grads = [(0, r) for r in range(6)] + [(1, r) for r in range(7)] + [(2, None), (3, None), (4, None)]
    wmv = []
    for nm in small:
        for pre in ("", "m_", "v_"):
            wmv.append(given[pre + nm][0] if nm in ("w_spatial", "b_spatial") else given[pre + nm])
    outs = _adamw_small([a1, a2, a3, a4], [g_bada], grads, wmv, (0, 1), "adamw_small")
    loss = outs[0][6, 0]
    for t, nm in enumerate(small):
        quad = outs[2 + 4 * t:6 + 4 * t]
        res[nm] = tuple(q[None] for q in quad) if nm in ("w_spatial", "b_spatial") else tuple(quad)
    g_cw = lax.dynamic_slice(outs[1], (8, mi * 64), (32, 64))
    wmv = [jnp.pad(given[pre + "conv_w"][0], ((0, 1), (0, 0)), constant_values=1.0 if pre == "v_" else 0.0)
           for pre in ("", "m_", "v_")]
    quad = _adamw_small([], [g_cw], [(0, None)], wmv, (), "adamw_conv_w")
    res["conv_w"] = tuple(q[0:CONV_K][None] for q in quad)

    order = ["w_ada", "b_ada", "g_pre_f1", "g_post_f1", "w_f1_in", "w_f1_out", "g_pre_m", "g_post_m", "w_mix_in",
             "gmlp_norm_g", "gmlp_norm_b", "w_spatial", "b_spatial", "conv_w", "conv_b", "conv_norm_g", "conv_norm_b",
             "g_out_a", "g_out_b", "w_mix_out", "g_pre_f2", "g_post_f2", "w_f2_in", "w_f2_out"]
    out = [loss, dx0.reshape(bl, seq, D)]
    for k in range(4):
        out += [res[nm][k] for nm in order]
    return tuple(out)
```

```python
import jax
import jax.numpy as jnp
from jax import lax
from jax.experimental import pallas as pl
from jax.experimental.pallas import tpu as pltpu

F32 = jnp.float32
BF16 = jnp.bfloat16

D = 1024
DFF = 2816
NDEV = 8
FB = 2 * DFF // NDEV
NCH = DFF // FB
LANES = 128
SUBL = 8
FO = DFF // NDEV
WA = 512
NSLAB = WA // LANES
NHEAD = 8
HD = 64
CHUNK = 128
CONV_K = 31
HALO = 32
MB = 2 * (WA + WA) // NDEV
MO = D // NDEV
ADA_B = 9 * D // NDEV
EPS = 1e-6
HALF = 0.5

ADAM_LR = 0.001
ADAM_B1 = 0.9
ADAM_B2 = 0.999
ADAM_EPS = 1e-08
ADAM_WD = 0.01
ADAM_STEP = 10

VMEM_LIMIT = 56 * 1024 * 1024
MESH = pl.DeviceIdType.MESH
FLIPS = ((0, 0, 1), (1, 0, 0), (0, 1, 0), (1, 1, 0), (1, 0, 1), (0, 1, 1), (1, 1, 1))
CHIP_FLIPS = ((1, 0, 0), (0, 1, 0), (1, 1, 0))
HBM = pl.BlockSpec(memory_space=pl.ANY)
VM = pl.BlockSpec(memory_space=pltpu.VMEM)


def _dot(a, b):
    return lax.dot_general(a, b, (((1,), (0,)), ((), ())), preferred_element_type=F32)


def _dot_nt(a, b):
    return lax.dot_general(a, b, (((1,), (1,)), ((), ())), preferred_element_type=F32)


def _dot_tn(a, b):
    return lax.dot_general(a, b, (((0,), (0,)), ((), ())), preferred_element_type=F32)


def _rowmean(v):
    return jnp.mean(v, axis=-1, keepdims=True)


def _colsum(v):
    return jnp.sum(v, axis=0, keepdims=True)


def _sigmoid(v):
    return 0.5 * jnp.tanh(0.5 * v) + 0.5


def _const_spec(shape):
    nd = len(shape)
    return pl.BlockSpec(shape, lambda *_: (0,) * nd, pipeline_mode=pl.Buffered(1))


def _me():
    return lax.axis_index("x"), lax.axis_index("y"), lax.axis_index("c")


def _flip(me, f):
    return tuple(1 - v if b else v for v, b in zip(me, f))


def _lin(p):
    return 4 * p[0] + 2 * p[1] + p[2]


def _remote(src, dst, send_sem, recv_sem, dev):
    return pltpu.make_async_remote_copy(src_ref=src, dst_ref=dst, send_sem=send_sem, recv_sem=recv_sem,
                                        device_id=dev, device_id_type=MESH)


def _blk(kind, ref, p):
    if kind == "out":
        return ref.at[2 * p[0] + p[1], pl.ds(p[2] * FO, FO), :]
    return ref.at[_lin(p)]


class _Gather:
    def __init__(self, shards, kinds, late_mid=False):
        self.late_mid = late_mid
        self.kinds = kinds
        self.n = len(shards)
        self.ins = list(shards)
        self.out_shape = [jax.ShapeDtypeStruct((4, FB, D) if k == "out" else (NDEV,) + s.shape, s.dtype)
                          for s, k in zip(shards, kinds)]
        self.sems = [pltpu.SemaphoreType.DMA((7 * self.n,)), pltpu.SemaphoreType.DMA((7 * self.n,)),
                     pltpu.SemaphoreType.DMA((self.n,))]

    def _first(self, ins, outs, sems):
        ssem, rsem, lsem = sems
        me = _me()
        sib = _flip(me, (0, 0, 1))
        cps, loc = [], []
        for a in range(self.n):
            mine = _blk(self.kinds[a], outs[a], me)
            loc.append(pltpu.make_async_copy(ins[a], mine, lsem.at[a]))
            cps.append(_remote(ins[a], mine, ssem.at[7 * a], rsem.at[7 * a], sib))
            for j, f in enumerate(CHIP_FLIPS):
                cps.append(_remote(ins[a], mine, ssem.at[7 * a + 1 + j], rsem.at[7 * a + 1 + j], _flip(me, f)))
        return cps, loc

    def _passed(self, outs, sems):
        ssem, rsem, _ = sems
        me = _me()
        sib = _flip(me, (0, 0, 1))
        cps = []
        for j, f in enumerate(CHIP_FLIPS):
            for a in range(self.n):
                blk = _blk(self.kinds[a], outs[a], _flip(me, f))
                cps.append(_remote(blk, blk, ssem.at[7 * a + 4 + j], rsem.at[7 * a + 4 + j], sib))
        return cps

    def start(self, ins, outs, sems):
        cps, loc = self._first(ins, outs, sems)
        for cp in loc + cps:
            cp.start()

    def mid(self, ins, outs, sems):
        ssem, rsem, _ = sems
        me = _me()
        passed = self._passed(outs, sems)
        t = 0
        for j, f in enumerate(CHIP_FLIPS):
            for a in range(self.n):
                blk = _blk(self.kinds[a], outs[a], _flip(me, f))
                _remote(blk, blk, ssem.at[7 * a + 1 + j], rsem.at[7 * a + 1 + j], _flip(me, f)).wait_recv()
                passed[t].start()
                t += 1

    def end(self, ins, outs, sems):
        ssem, rsem, _ = sems
        me = _me()
        sib = _flip(me, (0, 0, 1))
        for a in range(self.n):
            blk = _blk(self.kinds[a], outs[a], sib)
            _remote(blk, blk, ssem.at[7 * a], rsem.at[7 * a], sib).wait_recv()
            for j, f in enumerate(CHIP_FLIPS):
                blk = _blk(self.kinds[a], outs[a], _flip(_flip(me, f), (0, 0, 1)))
                _remote(blk, blk, ssem.at[7 * a + 4 + j], rsem.at[7 * a + 4 + j], sib).wait_recv()
        cps, loc = self._first(ins, outs, sems)
        for cp in cps + self._passed(outs, sems):
            cp.wait_send()
        for cp in loc:
            cp.wait()


class _RelayGather(_Gather):
    def _peers(self):
        me = _me()
        c = me[2]
        to = (me[0] + (1 - c) - 2 * me[0] * (1 - c), me[1] + c - 2 * me[1] * c, c)
        frm = (me[0] + c - 2 * me[0] * c, me[1] + (1 - c) - 2 * me[1] * (1 - c), c)
        return me, _flip(me, (0, 0, 1)), to, frm, _flip(me, (1, 1, 0))

    def _first(self, ins, outs, sems):
        ssem, rsem, lsem = sems
        me, sib, to, frm, _ = self._peers()
        cps, loc = [], []
        for a in range(self.n):
            mine = _blk(self.kinds[a], outs[a], me)
            loc.append(pltpu.make_async_copy(ins[a], mine, lsem.at[a]))
            for slot, dev in ((0, sib), (1, to), (2, frm)):
                cps.append(_remote(ins[a], mine, ssem.at[7 * a + slot], rsem.at[7 * a + slot], dev))
        return cps, loc

    def _block_copy(self, outs, sems, a, slot, owner, dev):
        ssem, rsem, _ = sems
        blk = _blk(self.kinds[a], outs[a], owner)
        return _remote(blk, blk, ssem.at[7 * a + slot], rsem.at[7 * a + slot], dev)

    def mid(self, ins, outs, sems):
        me, sib, to, frm, _ = self._peers()
        for a in range(self.n):
            self._block_copy(outs, sems, a, 2, frm, frm).wait_recv()
            self._block_copy(outs, sems, a, 3, frm, to).start()
            self._block_copy(outs, sems, a, 5, frm, sib).start()
        for a in range(self.n):
            self._block_copy(outs, sems, a, 1, to, to).wait_recv()
            self._block_copy(outs, sems, a, 4, to, sib).start()

    def end(self, ins, outs, sems):
        me, sib, to, frm, far = self._peers()
        up = (0, 0, 1)
        for a in range(self.n):
            self._block_copy(outs, sems, a, 3, far, to).wait_recv()
            self._block_copy(outs, sems, a, 6, far, sib).start()
        for a in range(self.n):
            for slot, owner in ((0, sib), (4, _flip(frm, up)), (5, _flip(to, up)), (6, _flip(far, up))):
                self._block_copy(outs, sems, a, slot, owner, sib).wait_recv()
        cps, loc = self._first(ins, outs, sems)
        for a in range(self.n):
            cps += [self._block_copy(outs, sems, a, 3, frm, to), self._block_copy(outs, sems, a, 4, to, sib),
                    self._block_copy(outs, sems, a, 5, frm, sib), self._block_copy(outs, sems, a, 6, far, sib)]
        for cp in cps:
            cp.wait_send()
        for cp in loc:
            cp.wait()


class _ChipScatter:
    def __init__(self, grads):
        self.n = len(grads)
        self.ins = list(grads)
        self.out_shape = [jax.ShapeDtypeStruct(g.shape, BF16) for g in grads]
        self.sems = [pltpu.SemaphoreType.DMA((3 * self.n,)), pltpu.SemaphoreType.DMA((3 * self.n,)),
                     pltpu.SemaphoreType.DMA((self.n,))]

    def _copies(self, ins, outs, sems):
        ssem, rsem, lsem = sems
        me = _me()
        mq = 2 * me[0] + me[1]
        loc = [pltpu.make_async_copy(ins[a].at[mq], outs[a].at[mq], lsem.at[a]) for a in range(self.n)]
        cps = []
        for k, f in enumerate(CHIP_FLIPS):
            p = _flip(me, f)
            for a in range(self.n):
                cps.append(_remote(ins[a].at[2 * p[0] + p[1]], outs[a].at[mq], ssem.at[3 * a + k], rsem.at[3 * a + k], p))
        return cps, loc

    def start(self, ins, outs, sems):
        cps, loc = self._copies(ins, outs, sems)
        for cp in loc + cps:
            cp.start()

    mid = None

    def end(self, ins, outs, sems):
        ssem, rsem, _ = sems
        me = _me()
        mq = 2 * me[0] + me[1]
        for k, f in enumerate(CHIP_FLIPS):
            p = _flip(me, f)
            for a in range(self.n):
                _remote(ins[a].at[mq], outs[a].at[2 * p[0] + p[1]], ssem.at[3 * a + k], rsem.at[3 * a + k], p).wait_recv()
        cps, loc = self._copies(ins, outs, sems)
        for cp in cps:
            cp.wait_send()
        for cp in loc:
            cp.wait()


class _AllGather:
    def __init__(self, parts):
        self.n = len(parts)
        self.ins = list(parts)
        self.out_shape = [jax.ShapeDtypeStruct((NDEV,) + p.shape, p.dtype) for p in parts]
        self.sems = [pltpu.SemaphoreType.DMA((7 * self.n,)), pltpu.SemaphoreType.DMA((7 * self.n,)),
                     pltpu.SemaphoreType.DMA((self.n,))]

    def _copies(self, ins, outs, sems):
        ssem, rsem, lsem = sems
        me = _me()
        mi = _lin(me)
        loc = [pltpu.make_async_copy(ins[a], outs[a].at[mi], lsem.at[a]) for a in range(self.n)]
        cps = []
        for k, f in enumerate(FLIPS):
            for a in range(self.n):
                cps.append(_remote(ins[a], outs[a].at[mi], ssem.at[7 * a + k], rsem.at[7 * a + k], _flip(me, f)))
        return cps, loc

    def start(self, ins, outs, sems):
        cps, loc = self._copies(ins, outs, sems)
        for cp in loc + cps:
            cp.start()

    mid = None

    def end(self, ins, outs, sems):
        ssem, rsem, _ = sems
        me = _me()
        for k, f in enumerate(FLIPS):
            p = _flip(me, f)
            for a in range(self.n):
                _remote(ins[a], outs[a].at[_lin(p)], ssem.at[7 * a + k], rsem.at[7 * a + k], p).wait_recv()
        cps, loc = self._copies(ins, outs, sems)
        for cp in cps:
            cp.wait_send()
        for cp in loc:
            cp.wait()


def _call(core, *, name, grid, in_specs, out_specs, out_shape, args, scratch=(), jobs=(), core_starts=False):
    n_in, n_out, n_sc = len(in_specs), len(out_specs), len(scratch)
    steps = 1
    for g in grid:
        steps *= g

    def body(*refs):
        pos = [0]

        def take(k):
            r = refs[pos[0]:pos[0] + k]
            pos[0] += k
            return r

        ins = take(n_in)
        j_ins = [take(len(j.ins)) for j in jobs]
        outs = take(n_out)
        j_outs = [take(len(j.out_shape)) for j in jobs]
        scs = take(n_sc)
        j_sems = [take(len(j.sems)) for j in jobs]
        if len(grid) == 2:
            step = pl.program_id(0) * grid[1] + pl.program_id(1)
        elif len(grid) == 1:
            step = pl.program_id(0)
        else:
            step = 0
        def start_jobs():
            for j, ji, jo, js in zip(jobs, j_ins, j_outs, j_sems):
                j.start(ji, jo, js)

        if grid:
            pl.when(step == 0)(start_jobs)
        elif not core_starts:
            start_jobs()
        for j, ji, jo, js in zip(jobs, j_ins, j_outs, j_sems):
            if j.mid is not None and grid:
                at = max(steps - 2, 0) if j.late_mid else (3 * steps) // 4
                pl.when(step == at)(lambda j=j, ji=ji, jo=jo, js=js: j.mid(ji, jo, js))
        def finish_jobs():
            for j, ji, jo, js in zip(jobs, j_ins, j_outs, j_sems):
                if j.mid is not None:
                    j.mid(ji, jo, js)
                j.end(ji, jo, js)

        if core_starts:
            core(ins, outs, scs, start_jobs, finish_jobs)
        elif core is not None:
            core(ins, outs, scs)
        if grid:
            for j, ji, jo, js in zip(jobs, j_ins, j_outs, j_sems):
                pl.when(step == steps - 1)(lambda j=j, ji=ji, jo=jo, js=js: j.end(ji, jo, js))
        elif not core_starts:
            finish_jobs()

    all_in = list(in_specs)
    all_args = list(args)
    all_out = list(out_specs)
    all_shape = list(out_shape)
    all_sc = list(scratch)
    for j in jobs:
        all_in += [HBM] * len(j.ins)
        all_args += j.ins
    for j in jobs:
        all_out += [HBM] * len(j.out_shape)
        all_shape += j.out_shape
        all_sc += j.sems
    params = dict(vmem_limit_bytes=VMEM_LIMIT)
    if grid:
        params["dimension_semantics"] = ("arbitrary",) * len(grid)
    res = pl.pallas_call(
        body, name=name, grid=grid, in_specs=all_in, out_specs=all_out, out_shape=all_shape,
        scratch_shapes=all_sc, compiler_params=pltpu.CompilerParams(**params),
    )(*all_args)
    core_res = list(res[:n_out])
    job_res = []
    pos = n_out
    for j in jobs:
        job_res.append(list(res[pos:pos + len(j.out_shape)]))
        pos += len(j.out_shape)
    return core_res, job_res


def _ffn_fwd(x, mod, gvec, w_in, w_out, tm, name, jobs=()):
    T = x.shape[0]
    nt = T // tm
    tps = nt // mod.shape[0]

    def core(ins, outs, _):
        x_ref, mod_ref, g_ref, win_ref, wout_ref = ins
        xo_ref, gu_ref, y_ref = outs
        xv = x_ref[...]
        sh, sc, gt = mod_ref[0:1, :], mod_ref[1:2, :], mod_ref[2:3, :]
        r = lax.rsqrt(_rowmean(xv * xv) + EPS)
        h = (xv * r * g_ref[0:1, :]) * (1.0 + sc) + sh
        hb = h.astype(BF16)
        y = jnp.zeros((tm, D), F32)
        for cidx in range(NCH):
            gate = _dot_nt(hb, win_ref[cidx])
            up = _dot_nt(hb, win_ref[NCH + cidx])
            gu_ref[cidx] = gate.astype(BF16)
            gu_ref[NCH + cidx] = up.astype(BF16)
            act = gate * _sigmoid(gate) * up
            y = y + _dot(act.astype(BF16), wout_ref[cidx])
        y_ref[...] = y
        ry = lax.rsqrt(_rowmean(y * y) + EPS)
        xo_ref[...] = xv + (HALF * gt) * (y * ry * g_ref[1:2, :])

    tile = pl.BlockSpec((tm, D), lambda i: (i, 0))
    return _call(
        core, name=name, grid=(nt,), jobs=jobs,
        in_specs=[tile, pl.BlockSpec((None, 8, D), lambda i: (i // tps, 0, 0)), _const_spec((8, D)),
                  _const_spec((8, FB, D)), _const_spec((4, FB, D))],
        out_specs=[tile, pl.BlockSpec((8, tm, FB), lambda i: (0, i, 0)), tile],
        out_shape=[jax.ShapeDtypeStruct((T, D), F32), jax.ShapeDtypeStruct((8, T, FB), BF16),
                   jax.ShapeDtypeStruct((T, D), F32)],
        args=[x, mod, gvec, w_in, w_out])


def _ffn_bwd(dxo, x, y, gu, mod, gvec, w_in, w_out, tm, name, jobs=()):
    T = x.shape[0]
    nt = T // tm
    nb = mod.shape[0]
    tps = nt // nb

    def core(ins, outs, _):
        dxo_ref, x_ref, y_ref, gu_ref, mod_ref, g_ref, win_ref, wout_ref = ins
        dx_ref, dg_ref, act_ref, hb_ref, dyb_ref, mg_ref, vg_ref = outs
        i = pl.program_id(0)
        xv = x_ref[...]
        dxo_v = dxo_ref[...]
        yv = y_ref[...]
        sh, sc, gt = mod_ref[0:1, :], mod_ref[1:2, :], mod_ref[2:3, :]
        gpre, gpost = g_ref[0:1, :], g_ref[1:2, :]
        r = lax.rsqrt(_rowmean(xv * xv) + EPS)
        xh = xv * r
        n = xh * gpre
        hb = (n * (1.0 + sc) + sh).astype(BF16)
        hb_ref[...] = hb
        ry = lax.rsqrt(_rowmean(yv * yv) + EPS)
        yh = yv * ry
        d_gt = _colsum(HALF * dxo_v * (yh * gpost))
        dp = (HALF * gt) * dxo_v
        d_gpost = _colsum(dp * yh)
        dyh = dp * gpost
        dy = ry * (dyh - yh * _rowmean(dyh * yh))
        dyb = dy.astype(BF16)
        dyb_ref[...] = dyb
        dh = jnp.zeros((tm, D), F32)
        for cidx in range(NCH):
            gate = gu_ref[cidx].astype(F32)
            up = gu_ref[NCH + cidx].astype(F32)
            sig = _sigmoid(gate)
            s = gate * sig
            act_ref[cidx] = (s * up).astype(BF16)
            d_act = _dot_nt(dyb, wout_ref[cidx])
            d_up = (d_act * s).astype(BF16)
            d_gate = (d_act * up * (sig * (1.0 + gate * (1.0 - sig)))).astype(BF16)
            dg_ref[cidx] = d_gate
            dg_ref[NCH + cidx] = d_up
            dh = dh + _dot(d_gate, win_ref[cidx]) + _dot(d_up, win_ref[NCH + cidx])
        d_sc = _colsum(dh * n)
        d_sh = _colsum(dh)
        dn = dh * (1.0 + sc)
        d_gpre = _colsum(dn * xh)
        dxh = dn * gpre
        dx_ref[...] = dxo_v + r * (dxh - xh * _rowmean(dxh * xh))

        @pl.when(i % tps == 0)
        def _():
            mg_ref[...] = jnp.zeros((8, D), F32)

        @pl.when(i == 0)
        def _():
            vg_ref[...] = jnp.zeros((8, D), F32)

        mg_ref[0:1, :] += d_sh
        mg_ref[1:2, :] += d_sc
        mg_ref[2:3, :] += d_gt
        vg_ref[0:1, :] += d_gpre
        vg_ref[1:2, :] += d_gpost

    tile = pl.BlockSpec((tm, D), lambda i: (i, 0))
    return _call(
        core, name=name, grid=(nt,), jobs=jobs,
        in_specs=[tile, tile, tile, pl.BlockSpec((8, tm, FB), lambda i: (0, i, 0)),
                  pl.BlockSpec((None, 8, D), lambda i: (i // tps, 0, 0)), _const_spec((8, D)),
                  _const_spec((8, FB, D)), _const_spec((4, FB, D))],
        out_specs=[tile, pl.BlockSpec((8, tm, FB), lambda i: (0, i, 0)),
                   pl.BlockSpec((4, tm, FB), lambda i: (0, i, 0)), tile, tile,
                   pl.BlockSpec((None, 8, D), lambda i: (i // tps, 0, 0)), pl.BlockSpec((8, D), lambda i: (0, 0))],
        out_shape=[jax.ShapeDtypeStruct((T, D), F32), jax.ShapeDtypeStruct((8, T, FB), BF16),
                   jax.ShapeDtypeStruct((4, T, FB), BF16), jax.ShapeDtypeStruct((T, D), BF16),
                   jax.ShapeDtypeStruct((T, D), BF16), jax.ShapeDtypeStruct((nb, 8, D), F32),
                   jax.ShapeDtypeStruct((8, D), F32)],
        args=[dxo, x, y, gu, mod, gvec, w_in, w_out])


def _ffn_last(x, target, mod, gvec, w_in, w_out, tm, name, jobs=()):
    T = x.shape[0]
    nt = T // tm
    nb = mod.shape[0]
    tps = nt // nb

    def core(ins, outs, scs):
        x_ref, t_ref, mod_ref, g_ref, wina_ref, winb_ref, wout_ref = ins
        dx_ref, dg_ref, act_ref, hb_ref, dyb_ref, mg_ref, vg_ref, loss_ref = outs
        hd2 = w_in[0].shape[2]
        (gu_s,) = scs
        i = pl.program_id(0)
        xv = x_ref[...]
        sh, sc, gt = mod_ref[0:1, :], mod_ref[1:2, :], mod_ref[2:3, :]
        gpre, gpost = g_ref[0:1, :], g_ref[1:2, :]
        r = lax.rsqrt(_rowmean(xv * xv) + EPS)
        xh = xv * r
        n = xh * gpre
        hb = (n * (1.0 + sc) + sh).astype(BF16)
        hb_ref[...] = hb
        hba, hbb = hb[:, 0:hd2], hb[:, hd2:D]
        yv = jnp.zeros((tm, D), F32)
        for cidx in range(NCH):
            gate = _dot_nt(hba, wina_ref[cidx]) + _dot_nt(hbb, winb_ref[cidx])
            up = _dot_nt(hba, wina_ref[NCH + cidx]) + _dot_nt(hbb, winb_ref[NCH + cidx])
            gu_s[cidx] = gate.astype(BF16)
            gu_s[NCH + cidx] = up.astype(BF16)
            act = gate * _sigmoid(gate) * up
            act_ref[cidx] = act.astype(BF16)
            yv = yv + _dot(act_ref[cidx], wout_ref[cidx])
        ry = lax.rsqrt(_rowmean(yv * yv) + EPS)
        yh = yv * ry
        pn = yh * gpost
        err = xv + (HALF * gt) * pn - t_ref[...]
        dxo_v = err * (1.0 / D)
        d_gt = _colsum(HALF * dxo_v * pn)
        dp = (HALF * gt) * dxo_v
        d_gpost = _colsum(dp * yh)
        dyh = dp * gpost
        dyb = (ry * (dyh - yh * _rowmean(dyh * yh))).astype(BF16)
        dyb_ref[...] = dyb
        dha = jnp.zeros((tm, hd2), F32)
        dhb = jnp.zeros((tm, D - hd2), F32)
        for cidx in range(NCH):
            gate = gu_s[cidx].astype(F32)
            up = gu_s[NCH + cidx].astype(F32)
            sig = _sigmoid(gate)
            s = gate * sig
            d_act = _dot_nt(dyb, wout_ref[cidx])
            d_up = (d_act * s).astype(BF16)
            d_gate = (d_act * up * (sig * (1.0 + gate * (1.0 - sig)))).astype(BF16)
            dg_ref[cidx] = d_gate
            dg_ref[NCH + cidx] = d_up
            dha = dha + _dot(d_gate, wina_ref[cidx]) + _dot(d_up, wina_ref[NCH + cidx])
            dhb = dhb + _dot(d_gate, winb_ref[cidx]) + _dot(d_up, winb_ref[NCH + cidx])
        dh = jnp.concatenate([dha, dhb], axis=1)
        d_sc = _colsum(dh * n)
        d_sh = _colsum(dh)
        dn = dh * (1.0 + sc)
        d_gpre = _colsum(dn * xh)
        dxh = dn * gpre
        dx_ref[...] = dxo_v + r * (dxh - xh * _rowmean(dxh * xh))

        @pl.when(i % tps == 0)
        def _():
            mg_ref[...] = jnp.zeros((8, D), F32)

        @pl.when(i == 0)
        def _():
            vg_ref[...] = jnp.zeros((8, D), F32)
            loss_ref[...] = jnp.zeros((8, D), F32)

        mg_ref[0:1, :] += d_sh
        mg_ref[1:2, :] += d_sc
        mg_ref[2:3, :] += d_gt
        vg_ref[0:1, :] += d_gpre
        vg_ref[1:2, :] += d_gpost
        loss_ref[...] += HALF * jnp.sum(_rowmean(err * err), axis=0, keepdims=True)

    tile = pl.BlockSpec((tm, D), lambda i: (i, 0))
    return _call(
        core, name=name, grid=(nt,), jobs=jobs,
        in_specs=[tile, tile, pl.BlockSpec((None, 8, D), lambda i: (i // tps, 0, 0)), _const_spec((8, D)),
                  _const_spec(w_in[0].shape), _const_spec(w_in[1].shape), _const_spec((4, FB, D))],
        out_specs=[tile, pl.BlockSpec((8, tm, FB), lambda i: (0, i, 0)),
                   pl.BlockSpec((4, tm, FB), lambda i: (0, i, 0)), tile, tile,
                   pl.BlockSpec((None, 8, D), lambda i: (i // tps, 0, 0)), pl.BlockSpec((8, D), lambda i: (0, 0)),
                   pl.BlockSpec((8, D), lambda i: (0, 0))],
        out_shape=[jax.ShapeDtypeStruct((T, D), F32), jax.ShapeDtypeStruct((8, T, FB), BF16),
                   jax.ShapeDtypeStruct((4, T, FB), BF16), jax.ShapeDtypeStruct((T, D), BF16),
                   jax.ShapeDtypeStruct((T, D), BF16), jax.ShapeDtypeStruct((nb, 8, D), F32),
                   jax.ShapeDtypeStruct((8, D), F32), jax.ShapeDtypeStruct((8, D), F32)],
        scratch=[pltpu.VMEM((8, tm, FB), BF16)],
        args=[x, target, mod, gvec, w_in[0], w_in[1], w_out])


def _masked_spatial(ws_ref):
    row = lax.broadcasted_iota(jnp.int32, (CHUNK, CHUNK), 0)
    col = lax.broadcasted_iota(jnp.int32, (CHUNK, CHUNK), 1)
    keep = col <= row
    return [jnp.where(keep, ws_ref[hd], 0.0).astype(BF16) for hd in range(NHEAD)]


def _head_pairs(mats, right, transpose=False):
    first = lax.broadcasted_iota(jnp.int32, (CHUNK, LANES), 1) < HD
    op = _dot_tn if transpose else _dot
    out = []
    for p in range(NHEAD // 2):
        slab = right[:, _lanes(p)]
        out.append(jnp.where(first, op(mats[2 * p], slab), op(mats[2 * p + 1], slab)))
    return jnp.concatenate(out, axis=1)


def _spatial_gate(wm, vb_chunk):
    return _head_pairs(wm, vb_chunk)


def _layer_norm_stats(v):
    mu = _rowmean(v)
    vc = v - mu
    rstd = lax.rsqrt(_rowmean(vc * vc) + EPS)
    return vc * rstd, rstd


def _pitch(tm):
    p = tm // 8
    while p % 8 != 4:
        p += 1
    return p


def _lanes(s):
    return slice(s * LANES, (s + 1) * LANES)


def _to_slabs(ref, row0, val):
    for s in range(NSLAB):
        ref[s, row0:row0 + val.shape[0], :] = val[:, _lanes(s)]


def _tap_sum(src, out, cw_ref, bias, tm, start):
    p = _pitch(tm)
    for s in range(NSLAB):
        accs = [jnp.broadcast_to(bias[:, _lanes(s)], (SUBL, LANES))] * p
        for k in range(CONV_K):
            w = jnp.broadcast_to(cw_ref[k:k + 1, _lanes(s)], (SUBL, LANES))
            for v in range(p):
                accs[v] = accs[v] + w * src[s, pl.ds(v + start(k), 8, stride=p), :]
        for v in range(p):
            out[s, pl.ds(v, 8, stride=p), :] = accs[v]
    return jnp.concatenate([out[s, 0:tm, :] for s in range(NSLAB)], axis=1)


def _mixer_fwd(x, mod, gvec, w_mi, w_mo, v512, ws, bias_full, cw, tm, name, jobs=()):
    T = x.shape[0]
    nt = T // tm
    tps = nt // mod.shape[0]
    ext_rows = 8 * _pitch(tm)

    def core(ins, outs, scs):
        x_ref, mod_ref, g_ref, wmi_ref, wmo_ref, v_ref, ws_ref, bias_ref, cw_ref = ins
        xo_ref, proj_ref, ym_ref, conv_ref = outs
        glu_ext, conv_scr = scs
        i = pl.program_id(0)
        xv = x_ref[...]
        sh, sc, gt = mod_ref[0:1, :], mod_ref[1:2, :], mod_ref[2:3, :]
        r = lax.rsqrt(_rowmean(xv * xv) + EPS)
        hb = ((xv * r * g_ref[0:1, :]) * (1.0 + sc) + sh).astype(BF16)
        for j in range(NDEV):
            proj_ref[:, j * MB:(j + 1) * MB] = _dot(hb, wmi_ref[j])
        u = proj_ref[:, 0:WA]
        v0 = proj_ref[:, WA:2 * WA]
        a = proj_ref[:, 2 * WA:3 * WA]
        g = proj_ref[:, 3 * WA:4 * WA]
        vh, _ = _layer_norm_stats(v0)
        vb = (vh * v_ref[0:1, :] + v_ref[1:2, :]).astype(BF16)
        wm = _masked_spatial(ws_ref)
        ya = []
        for q in range(tm // CHUNK):
            z = _spatial_gate(wm, vb[q * CHUNK:(q + 1) * CHUNK, :]) + bias_ref[...]
            ya.append(u[q * CHUNK:(q + 1) * CHUNK, :] * z)
        ya = jnp.concatenate(ya, axis=0)
        glu = a * _sigmoid(g)

        @pl.when(i == 0)
        def _():
            glu_ext[:, HALO + tm:HALO + ext_rows, :] = jnp.zeros((NSLAB, ext_rows - tm, LANES), F32)

        @pl.when(i % tps == 0)
        def _():
            glu_ext[:, 0:HALO, :] = jnp.zeros((NSLAB, HALO, LANES), F32)

        _to_slabs(glu_ext, HALO, glu)
        conv = _tap_sum(glu_ext, conv_scr, cw_ref, v_ref[2:3, :], tm, lambda k: HALO - (CONV_K - 1) + k)
        conv_ref[...] = conv
        glu_ext[:, 0:HALO, :] = glu_ext[:, tm:tm + HALO, :]
        ch, _ = _layer_norm_stats(conv)
        cn = ch * v_ref[3:4, :] + v_ref[4:5, :]
        yb = cn * _sigmoid(cn)
        pa = ya * lax.rsqrt(_rowmean(ya * ya) + EPS) * v_ref[5:6, :]
        pb = yb * lax.rsqrt(_rowmean(yb * yb) + EPS) * v_ref[6:7, :]
        ycat = jnp.concatenate([pa, pb], axis=1).astype(BF16)
        ym = _dot(ycat, wmo_ref[...])
        ym_ref[...] = ym
        rm = lax.rsqrt(_rowmean(ym * ym) + EPS)
        xo_ref[...] = xv + gt * (ym * rm * g_ref[1:2, :])

    tile = pl.BlockSpec((tm, D), lambda i: (i, 0))
    return _call(
        core, name=name, grid=(nt,), jobs=jobs,
        in_specs=[tile, pl.BlockSpec((None, 8, D), lambda i: (i // tps, 0, 0)), _const_spec((8, D)),
                  _const_spec((NDEV, D, MB)), _const_spec((D, D)), _const_spec((8, WA)),
                  _const_spec((NHEAD, CHUNK, CHUNK)), _const_spec((CHUNK, WA)), _const_spec((32, WA))],
        out_specs=[tile, pl.BlockSpec((tm, 4 * WA), lambda i: (i, 0)), tile, pl.BlockSpec((tm, WA), lambda i: (i, 0))],
        out_shape=[jax.ShapeDtypeStruct((T, D), F32), jax.ShapeDtypeStruct((T, 4 * WA), F32),
                   jax.ShapeDtypeStruct((T, D), F32), jax.ShapeDtypeStruct((T, WA), F32)],
        scratch=[pltpu.VMEM((NSLAB, HALO + ext_rows, LANES), F32), pltpu.VMEM((NSLAB, ext_rows, LANES), F32)],
        args=[x, mod, gvec, w_mi, w_mo, v512, ws, bias_full, cw])


def _mixer_bwd_a(dxo, ym, proj, conv, mod, gvec, w_mo, v512, ws, bias_full, esel, tm, name, jobs=()):
    T = dxo.shape[0]
    nt = T // tm
    nb = mod.shape[0]
    tps = nt // nb

    def core(ins, outs, scs):
        dxo_ref, ym_ref, proj_ref, conv_ref, mod_ref, g_ref, wmo_ref, v_ref, ws_ref, bias_ref, e_ref = ins
        dpart_ref, dymb_ref, ycat_ref, mg_ref, vg_ref, v5g_ref, gws_ref, gbs_ref = outs
        (dbs_acc,) = scs
        i = pl.program_id(0)
        dxo_v = dxo_ref[...]
        ymv = ym_ref[...]
        gt = mod_ref[2:3, :]
        gpost = g_ref[1:2, :]
        rm = lax.rsqrt(_rowmean(ymv * ymv) + EPS)
        ymh = ymv * rm
        d_gt = _colsum(dxo_v * (ymh * gpost))
        dpm = gt * dxo_v
        d_gpost = _colsum(dpm * ymh)
        dymh = dpm * gpost
        dym = (rm * (dymh - ymh * _rowmean(dymh * ymh))).astype(BF16)
        dymb_ref[...] = dym
        dycat = _dot_nt(dym, wmo_ref[...])
        u = proj_ref[:, 0:WA]
        v0 = proj_ref[:, WA:2 * WA]
        vh, rv = _layer_norm_stats(v0)
        vb = (vh * v_ref[0:1, :] + v_ref[1:2, :]).astype(BF16)
        wm = _masked_spatial(ws_ref)
        zs = []
        for q in range(tm // CHUNK):
            zs.append(_spatial_gate(wm, vb[q * CHUNK:(q + 1) * CHUNK, :]) + bias_ref[...])
        z = jnp.concatenate(zs, axis=0)
        ya = u * z
        ra = lax.rsqrt(_rowmean(ya * ya) + EPS)
        yah = ya * ra
        ch, rc = _layer_norm_stats(conv_ref[...])
        cn = ch * v_ref[3:4, :] + v_ref[4:5, :]
        sg = _sigmoid(cn)
        yb = cn * sg
        rb = lax.rsqrt(_rowmean(yb * yb) + EPS)
        ybh = yb * rb
        ycat_ref[...] = jnp.concatenate([yah * v_ref[5:6, :], ybh * v_ref[6:7, :]], axis=1).astype(BF16)
        dpa = dycat[:, 0:WA]
        dpb = dycat[:, WA:2 * WA]
        d_goa = _colsum(dpa * yah)
        d_gob = _colsum(dpb * ybh)
        dyah = dpa * v_ref[5:6, :]
        dybh = dpb * v_ref[6:7, :]
        dya = ra * (dyah - yah * _rowmean(dyah * yah))
        dyb = rb * (dybh - ybh * _rowmean(dybh * ybh))
        dpart_ref[:, 0:WA] = dya * z
        dz = dya * u

        @pl.when(i == 0)
        def _():
            gws_ref[...] = jnp.zeros((NHEAD, CHUNK, CHUNK), F32)
            dbs_acc[...] = jnp.zeros((CHUNK, WA), F32)
            vg_ref[...] = jnp.zeros((8, D), F32)
            v5g_ref[...] = jnp.zeros((8, WA), F32)

        first = lax.broadcasted_iota(jnp.int32, (CHUNK, LANES), 1) < HD
        dvs = []
        for q in range(tm // CHUNK):
            dz_q = dz[q * CHUNK:(q + 1) * CHUNK, :]
            vb_q = vb[q * CHUNK:(q + 1) * CHUNK, :]
            dbs_acc[...] += dz_q
            dzb = dz_q.astype(BF16)
            dvs.append(_head_pairs(wm, dzb, transpose=True))
            for hd in range(NHEAD):
                slab = dzb[:, _lanes(hd // 2)]
                dz_hd = jnp.where(first if hd % 2 == 0 else jnp.logical_not(first), slab, jnp.zeros_like(slab))
                gws_ref[hd] += _dot_nt(dz_hd, vb_q[:, _lanes(hd // 2)])
        dv = jnp.concatenate(dvs, axis=0)
        d_gng = _colsum(dv * vh)
        d_gnb = _colsum(dv)
        dvh = dv * v_ref[0:1, :]
        dpart_ref[:, WA:2 * WA] = rv * (dvh - _rowmean(dvh) - vh * _rowmean(dvh * vh))
        dcn = dyb * (sg * (1.0 + cn * (1.0 - sg)))
        d_cng = _colsum(dcn * ch)
        d_cnb = _colsum(dcn)
        dch = dcn * v_ref[3:4, :]
        dconv = rc * (dch - _rowmean(dch) - ch * _rowmean(dch * ch))
        dpart_ref[:, 2 * WA:3 * WA] = dconv
        dpart_ref[:, 3 * WA:4 * WA] = jnp.zeros((tm, WA), F32)
        d_cb = _colsum(dconv)

        @pl.when(i % tps == 0)
        def _():
            mg_ref[...] = jnp.zeros((8, D), F32)

        mg_ref[2:3, :] += d_gt
        vg_ref[1:2, :] += d_gpost
        v5g_ref[0:1, :] += d_gng
        v5g_ref[1:2, :] += d_gnb
        v5g_ref[2:3, :] += d_cb
        v5g_ref[3:4, :] += d_cng
        v5g_ref[4:5, :] += d_cnb
        v5g_ref[5:6, :] += d_goa
        v5g_ref[6:7, :] += d_gob

        @pl.when(i == nt - 1)
        def _():
            row = lax.broadcasted_iota(jnp.int32, (CHUNK, CHUNK), 0)
            col = lax.broadcasted_iota(jnp.int32, (CHUNK, CHUNK), 1)
            for hd in range(NHEAD):
                gws_ref[hd] = jnp.where(col <= row, gws_ref[hd], 0.0)
            gbs_ref[...] = lax.dot_general(e_ref[...], dbs_acc[...], (((1,), (1,)), ((), ())),
                                           precision=lax.Precision.HIGHEST, preferred_element_type=F32)

    tile = pl.BlockSpec((tm, D), lambda i: (i, 0))
    ptile = pl.BlockSpec((tm, 4 * WA), lambda i: (i, 0))
    return _call(
        core, name=name, grid=(nt,), jobs=jobs,
        in_specs=[tile, tile, pl.BlockSpec((tm, 2 * WA), lambda i: (i, 0)), pl.BlockSpec((tm, WA), lambda i: (i, 0)),
                  pl.BlockSpec((None, 8, D), lambda i: (i // tps, 0, 0)), _const_spec((8, D)), _const_spec((D, D)),
                  _const_spec((8, WA)), _const_spec((NHEAD, CHUNK, CHUNK)), _const_spec((CHUNK, WA)),
                  _const_spec((8, WA))],
        out_specs=[ptile, tile, tile, pl.BlockSpec((None, 8, D), lambda i: (i // tps, 0, 0)),
                   pl.BlockSpec((8, D), lambda i: (0, 0)), pl.BlockSpec((8, WA), lambda i: (0, 0)),
                   pl.BlockSpec((NHEAD, CHUNK, CHUNK), lambda i: (0, 0, 0)), pl.BlockSpec((8, CHUNK), lambda i: (0, 0))],
        out_shape=[jax.ShapeDtypeStruct((T, 4 * WA), F32), jax.ShapeDtypeStruct((T, D), BF16),
                   jax.ShapeDtypeStruct((T, D), BF16), jax.ShapeDtypeStruct((nb, 8, D), F32),
                   jax.ShapeDtypeStruct((8, D), F32), jax.ShapeDtypeStruct((8, WA), F32),
                   jax.ShapeDtypeStruct((NHEAD, CHUNK, CHUNK), F32), jax.ShapeDtypeStruct((8, CHUNK), F32)],
        scratch=[pltpu.VMEM((CHUNK, WA), F32)],
        args=[dxo, ym, proj, conv, mod, gvec, w_mo, v512, ws, bias_full, esel])


def _mixer_bwd_b(dxo, x, dpart, proj, mod, gvec, w_mi, cw, tm, name, jobs=()):
    T = x.shape[0]
    nt = T // tm
    nb = mod.shape[0]
    tps = nt // nb
    hpt = tm // HALO
    nh = T // HALO
    off = HALO - (CONV_K - 1)
    p = _pitch(tm)
    ext_rows = 8 * p

    def core(ins, outs, scs):
        dxo_ref, x_ref, dpart_ref, dnext_ref, ag_ref, halo_ref, mod_ref, g_ref, wmi_ref, cw_ref = ins
        dx_ref, dproj_ref, hb_ref, mg_ref, vg_ref, dcw_ref = outs
        glu_ext, dconv_ext, dglu_scr, dcw_acc = scs
        i = pl.program_id(0)
        first = i % tps == 0
        last = i % tps == tps - 1
        a = ag_ref[:, 0:WA]
        g = ag_ref[:, WA:2 * WA]
        sgg = _sigmoid(g)

        @pl.when(i == 0)
        def _():
            glu_ext[:, HALO + tm:HALO + ext_rows, :] = jnp.zeros((NSLAB, ext_rows - tm, LANES), F32)
            dconv_ext[:, HALO + tm:HALO + ext_rows, :] = jnp.zeros((NSLAB, ext_rows - tm, LANES), F32)
            dcw_acc[...] = jnp.zeros((32, 8, WA), F32)
            vg_ref[...] = jnp.zeros((8, D), F32)

        _to_slabs(glu_ext, 0, jnp.where(first, 0.0, halo_ref[:, 0:WA] * _sigmoid(halo_ref[:, WA:2 * WA])))
        _to_slabs(glu_ext, HALO, a * sgg)
        _to_slabs(dconv_ext, 0, dpart_ref[:, 2 * WA:3 * WA])
        _to_slabs(dconv_ext, tm, jnp.where(last, 0.0, dnext_ref[...]))
        sub = lax.broadcasted_iota(jnp.int32, (SUBL, LANES), 0)
        for s in range(NSLAB):
            accs = [jnp.zeros((SUBL, LANES), F32)] * CONV_K
            for v in range(p):
                dc = jnp.where(v + p * sub < tm, dconv_ext[s, pl.ds(v, 8, stride=p), :], 0.0)
                for k in range(CONV_K):
                    accs[k] = accs[k] + dc * glu_ext[s, pl.ds(v + off + k, 8, stride=p), :]
            for k in range(CONV_K):
                dcw_acc[k, :, _lanes(s)] += accs[k]
        dglu = _tap_sum(dconv_ext, dglu_scr, cw_ref, jnp.zeros((1, WA), F32), tm, lambda k: (CONV_K - 1) - k)

        @pl.when(i == nt - 1)
        def _():
            for k in range(CONV_K):
                dcw_ref[k:k + 1, :] = jnp.sum(dcw_acc[k], axis=0, keepdims=True)
            dcw_ref[CONV_K:32, :] = jnp.zeros((32 - CONV_K, WA), F32)

        da = dglu * sgg
        dgg = dglu * a * (sgg * (1.0 - sgg))
        dproj_ref[:, 0:2 * WA] = dpart_ref[:, 0:2 * WA].astype(BF16)
        dproj_ref[:, 2 * WA:3 * WA] = da.astype(BF16)
        dproj_ref[:, 3 * WA:4 * WA] = dgg.astype(BF16)
        dh = jnp.zeros((tm, D), F32)
        for j in range(NDEV):
            dh = dh + _dot_nt(dproj_ref[:, j * MB:(j + 1) * MB], wmi_ref[j])
        xv = x_ref[...]
        sc, sh = mod_ref[1:2, :], mod_ref[0:1, :]
        gpre = g_ref[0:1, :]
        r = lax.rsqrt(_rowmean(xv * xv) + EPS)
        xh = xv * r
        n = xh * gpre
        hb_ref[...] = (n * (1.0 + sc) + sh).astype(BF16)
        d_sc = _colsum(dh * n)
        d_sh = _colsum(dh)
        dn = dh * (1.0 + sc)
        d_gpre = _colsum(dn * xh)
        dxh = dn * gpre
        dx_ref[...] = dxo_ref[...] + r * (dxh - xh * _rowmean(dxh * xh))

        @pl.when(first)
        def _():
            mg_ref[...] = jnp.zeros((8, D), F32)

        mg_ref[0:1, :] += d_sh
        mg_ref[1:2, :] += d_sc
        vg_ref[0:1, :] += d_gpre

    tile = pl.BlockSpec((tm, D), lambda i: (i, 0))
    return _call(
        core, name=name, grid=(nt,), jobs=jobs,
        in_specs=[tile, tile, pl.BlockSpec((tm, 4 * WA), lambda i: (i, 0)),
                  pl.BlockSpec((HALO, WA), lambda i: (jnp.minimum((i + 1) * hpt, nh - 1), 2)),
                  pl.BlockSpec((tm, 2 * WA), lambda i: (i, 1)),
                  pl.BlockSpec((HALO, 2 * WA), lambda i: (jnp.maximum(i * hpt - 1, 0), 1)),
                  pl.BlockSpec((None, 8, D), lambda i: (i // tps, 0, 0)), _const_spec((8, D)),
                  _const_spec((NDEV, D, MB)), _const_spec((32, WA))],
        out_specs=[tile, pl.BlockSpec((tm, 4 * WA), lambda i: (i, 0)), tile,
                   pl.BlockSpec((None, 8, D), lambda i: (i // tps, 0, 0)), pl.BlockSpec((8, D), lambda i: (0, 0)),
                   pl.BlockSpec((32, WA), lambda i: (0, 0))],
        out_shape=[jax.ShapeDtypeStruct((T, D), F32), jax.ShapeDtypeStruct((T, 4 * WA), BF16),
                   jax.ShapeDtypeStruct((T, D), BF16), jax.ShapeDtypeStruct((nb, 8, D), F32),
                   jax.ShapeDtypeStruct((8, D), F32), jax.ShapeDtypeStruct((32, WA), F32)],
        scratch=[pltpu.VMEM((NSLAB, HALO + ext_rows, LANES), F32), pltpu.VMEM((NSLAB, HALO + ext_rows, LANES), F32),
                 pltpu.VMEM((NSLAB, ext_rows, LANES), F32), pltpu.VMEM((32, 8, WA), F32)],
        args=[dxo, x, dpart, dpart, proj, proj, mod, gvec, w_mi, cw])


def _grad_chip(a, b, a_spec, b_spec, prod_shape, half, name, jobs=(), via_b=False, after=None):
    steps = 8 if half is None else 4
    R = prod_shape[0] if half is None else half
    C = prod_shape[1]

    def core(ins, outs, scs):
        a_ref, b_ref = ins[:2]
        (o_ref,) = outs
        own, snd, rcv, ssem, rsem, lsem = scs
        s = pl.program_id(0)
        c = lax.axis_index("c")
        me = _me()
        sib = _flip(me, (0, 0, 1))
        if via_b:
            prod = _dot_tn(b_ref[...], a_ref[...]).T.astype(BF16)
        else:
            prod = _dot_tn(a_ref[...], b_ref[...]).astype(BF16)
        if half is None:
            q = s // 2

            @pl.when(s % 2 == c)
            def _():
                own[q] = prod

            @pl.when(s % 2 != c)
            def _():
                snd[q] = prod
                _remote(snd.at[q], rcv.at[q], ssem.at[q], rsem.at[q], sib).start()
        else:
            lo = prod[0:half, :]
            hi = prod[half:2 * half, :]
            own[s] = jnp.where(c == 0, lo, hi)
            snd[s] = jnp.where(c == 0, hi, lo)
            _remote(snd.at[s], rcv.at[s], ssem.at[s], rsem.at[s], sib).start()

        @pl.when(s == steps - 1)
        def _():
            for q4 in range(4):
                cp = _remote(snd.at[q4], rcv.at[q4], ssem.at[q4], rsem.at[q4], sib)
                cp.wait_recv()
                cp.wait_send()
                snd[q4] = (own[q4].astype(F32) + rcv[q4].astype(F32)).astype(BF16)
            out = pltpu.make_async_copy(snd, o_ref, lsem)
            out.start()
            out.wait()

    return _call(
        core, name=name, grid=(steps,), jobs=jobs, in_specs=[a_spec, b_spec] + [HBM] * (after is not None),
        out_specs=[HBM], out_shape=[jax.ShapeDtypeStruct((4, R, C), BF16)],
        scratch=[pltpu.VMEM((4, R, C), BF16), pltpu.VMEM((4, R, C), BF16), pltpu.VMEM((4, R, C), BF16),
                 pltpu.SemaphoreType.DMA((4,)), pltpu.SemaphoreType.DMA((4,)), pltpu.SemaphoreType.DMA],
        args=[a, b] + [after] * (after is not None))


def _grad_w_in(dg, hb, name, jobs=()):
    T = hb.shape[0]
    return _grad_chip(dg, hb, pl.BlockSpec((None, T, FB), lambda s: (s, 0, 0)), _const_spec((T, D)),
                      (FB, D), None, name, jobs)


def _grad_w_out(act, dyb, name, jobs=(), after=None):
    T = dyb.shape[0]
    return _grad_chip(act, dyb, pl.BlockSpec((None, T, FB), lambda s: (s, 0, 0)), _const_spec((T, D)),
                      (FB, D), FO, name, jobs, after=after)


def _grad_w_mi(hb, dproj, name, jobs=()):
    T = hb.shape[0]
    return _grad_chip(hb, dproj, _const_spec((T, D)), pl.BlockSpec((T, MB), lambda s: (0, s)),
                      (D, MB), None, name, jobs, via_b=True)


def _grad_w_mo(ycat, dym, name, jobs=()):
    T = ycat.shape[0]
    return _grad_chip(ycat, dym, pl.BlockSpec((T, 2 * MO), lambda s: (0, s)), _const_spec((T, D)),
                      (2 * MO, D), MO, name, jobs)


def _adamw_math(w, g, m, v):
    m2 = ADAM_B1 * m + (1.0 - ADAM_B1) * g
    v2 = ADAM_B2 * v + (1.0 - ADAM_B2) * (g * g)
    m_hat = m2 / (1.0 - ADAM_B1 ** ADAM_STEP)
    v_hat = v2 / (1.0 - ADAM_B2 ** ADAM_STEP)
    delta = -ADAM_LR * (m_hat / (jnp.sqrt(v_hat) + ADAM_EPS) + ADAM_WD * w)
    return delta, m2, v2


def _adamw_reduce(parts, w, m, v, tr, name, own=None, after=None):
    R, C = w.shape

    def core(ins, outs, _):
        p_ref, w_ref, m_ref, v_ref = ins[:4]
        g_ref, d_ref, m2_ref, v2_ref = outs
        if own is None:
            terms = [p_ref[s].astype(F32) for s in range(4)]
        else:
            mq = 2 * lax.axis_index("x") + lax.axis_index("y")
            mine = ins[4][...].astype(F32)
            terms = [jnp.where(mq == s, mine, p_ref[s].astype(F32)) for s in range(4)]
        g = terms[0]
        for s in range(1, 4):
            g = g + terms[s]
        g_ref[...] = g
        d_ref[...], m2_ref[...], v2_ref[...] = _adamw_math(w_ref[...], g, m_ref[...], v_ref[...])

    blk = pl.BlockSpec((tr, C), lambda i: (i, 0))
    in_specs = [pl.BlockSpec((4, tr, C), lambda i: (0, i, 0)), blk, blk, blk]
    args = [parts, w, m, v]
    if own is not None:
        mq = 2 * lax.axis_index("x") + lax.axis_index("y")
        in_specs.append(pl.BlockSpec((tr, C), lambda i: (i, 0)))
        args.append(lax.dynamic_index_in_dim(own, mq, 0, keepdims=False))
    if after is not None:
        in_specs.append(HBM)
        args.append(after)
    return _call(
        core, name=name, grid=(R // tr,), in_specs=in_specs,
        out_specs=[blk, blk, blk, blk], out_shape=[jax.ShapeDtypeStruct((R, C), F32)] * 4, args=args)[0]


HBM_ONLY = pl.BlockSpec(memory_space=pltpu.HBM)
SEM = pl.BlockSpec(memory_space=pltpu.SEMAPHORE)
EFFECT = pltpu.SideEffectType.DATAFLOW_SIDE_EFFECTING


def _chip_scatter_start(gs, name):
    n = len(gs)

    def body(*refs):
        g_refs, land_refs = refs[:n], refs[n:2 * n]
        ssem, rsem = refs[2 * n:2 * n + 2]
        token = refs[-1]
        me = _me()
        mq = 2 * me[0] + me[1]
        for k, f in enumerate(CHIP_FLIPS):
            p = _flip(me, f)
            for a in range(n):
                _remote(g_refs[a].at[2 * p[0] + p[1]], land_refs[a].at[mq], ssem.at[3 * a + k], rsem.at[3 * a + k], p).start()
        token[...] = jnp.zeros_like(token)

    gs = [pltpu.with_memory_space_constraint(g, pltpu.HBM) for g in gs]
    lands = [pltpu.with_memory_space_constraint(lax.empty(g.shape, g.dtype), pltpu.HBM) for g in gs]
    res = pl.pallas_call(
        body, name=name,
        out_shape=(pltpu.SemaphoreType.DMA((3 * n,)), pltpu.SemaphoreType.DMA((3 * n,)))
        + tuple(pltpu.HBM(g.shape, g.dtype) for g in gs) * 2 + (jax.ShapeDtypeStruct((SUBL, LANES), F32),),
        in_specs=(HBM_ONLY,) * (2 * n), out_specs=(SEM, SEM) + (HBM_ONLY,) * (2 * n) + (VM,),
        input_output_aliases={a: 2 + a for a in range(2 * n)},
        compiler_params=pltpu.CompilerParams(has_side_effects=EFFECT),
    )(*gs, *lands)
    return res[:-1], res[-1]


def _chip_scatter_wait(handle, after, name):
    ssem, rsem = handle[:2]
    n = (len(handle) - 2) // 2
    thru = handle[2:]

    def body(*refs):
        g_refs, land_refs = refs[:n], refs[n:2 * n]
        ssem, rsem = refs[2 * n:2 * n + 2]
        me = _me()
        mq = 2 * me[0] + me[1]
        for k, f in enumerate(CHIP_FLIPS):
            p = _flip(me, f)
            pq = 2 * p[0] + p[1]
            for a in range(n):
                _remote(g_refs[a].at[pq], land_refs[a].at[mq], ssem.at[3 * a + k], rsem.at[3 * a + k], p).wait_send()
                _remote(g_refs[a].at[mq], land_refs[a].at[pq], ssem.at[3 * a + k], rsem.at[3 * a + k], p).wait_recv()

    res = pl.pallas_call(
        body, name=name,
        out_shape=tuple(pltpu.HBM(t.shape, t.dtype) for t in thru),
        in_specs=(HBM_ONLY,) * (2 * n) + (SEM, SEM, HBM), out_specs=(HBM_ONLY,) * (2 * n),
        input_output_aliases={a: a for a in range(2 * n)},
        compiler_params=pltpu.CompilerParams(has_side_effects=EFFECT),
    )(*thru, ssem, rsem, after)
    return list(res[:n]), list(res[n:])


def _adamw_ada(sc_all, dd, w, m, v, tr, name, after=None):
    R, C = w.shape

    def core(ins, outs, _):
        sc_ref, dd_ref, w_ref, m_ref, v_ref = ins[:5]
        g_ref, d_ref, m2_ref, v2_ref = outs
        g = _dot_tn(sc_ref[...].astype(BF16), dd_ref[...].astype(BF16))
        g_ref[...] = g
        d_ref[...], m2_ref[...], v2_ref[...] = _adamw_math(w_ref[...], g, m_ref[...], v_ref[...])

    blk = pl.BlockSpec((tr, C), lambda i: (i, 0))
    return _call(
        core, name=name, grid=(R // tr,),
        in_specs=[pl.BlockSpec((64, tr), lambda i: (0, i)), pl.BlockSpec((64, C), lambda i: (0, 0)), blk, blk, blk]
        + [HBM] * (after is not None),
        out_specs=[blk, blk, blk, blk], out_shape=[jax.ShapeDtypeStruct((R, C), F32)] * 4,
        args=[sc_all, dd, w, m, v] + [after] * (after is not None))[0]


def _adamw_small(gathered, plain, grads, wmv, emit, name, after=None):
    nw = len(grads)
    ng, npl, ne = len(gathered), len(plain), len(emit)

    def core(ins, outs, _):
        srcs = []
        for a in range(ng):
            s = ins[a][0]
            for dev in range(1, NDEV):
                s = s + ins[a][dev]
            srcs.append(s)
        srcs += [ins[ng + a][...] for a in range(npl)]
        w_refs = ins[ng + npl:]
        for e, a in enumerate(emit):
            outs[e][...] = srcs[a]
        for t in range(nw):
            src, row = grads[t]
            g = srcs[src] if row is None else srcs[src][row:row + 1, :]
            w_ref, m_ref, v_ref = w_refs[3 * t:3 * t + 3]
            g_ref, d_ref, m2_ref, v2_ref = outs[ne + 4 * t:ne + 4 * t + 4]
            g_ref[...] = g
            d_ref[...], m2_ref[...], v2_ref[...] = _adamw_math(w_ref[...], g, m_ref[...], v_ref[...])

    out_shape = [jax.ShapeDtypeStruct(gathered[a].shape[1:], F32) for a in emit]
    for t in range(nw):
        out_shape += [jax.ShapeDtypeStruct(wmv[3 * t].shape, F32)] * 4
    return _call(
        core, name=name, grid=(), in_specs=[VM] * (ng + npl + 3 * nw) + [HBM] * (after is not None),
        out_specs=[VM] * (ne + 4 * nw), out_shape=out_shape,
        args=list(gathered) + list(plain) + list(wmv) + [after] * (after is not None))[0]


def _ada_fwd(c_pad, w_ada, b_cols, cw_pad, jobs=()):
    def core(ins, outs, scs, start_jobs, finish_jobs):
        c_ref, w_ref, b_ref, cwp_ref = ins
        ada_ref, sc_ref, cw_ref = outs
        cbuf, send_buf, ssem, rsem = scs
        me = _me()
        mi = _lin(me)
        cbuf[mi] = c_ref[...]
        cw_ref[mi] = cwp_ref[...]
        peers = [_flip(me, f) for f in FLIPS]
        first = []
        for k, p in enumerate(peers):
            first.append(_remote(cbuf.at[mi], cbuf.at[mi], ssem.at[k], rsem.at[k], p))
            first.append(_remote(cw_ref.at[mi], cw_ref.at[mi], ssem.at[7 + k], rsem.at[7 + k], p))
        for cp in first:
            cp.start()
        start_jobs()
        for k, p in enumerate(peers):
            pi = _lin(p)
            _remote(cbuf.at[pi], cbuf.at[pi], ssem.at[k], rsem.at[k], p).wait_recv()
            _remote(cw_ref.at[pi], cw_ref.at[pi], ssem.at[7 + k], rsem.at[7 + k], p).wait_recv()
        c_all = cbuf[...].reshape(8 * 8, D)
        sc = c_all * _sigmoid(c_all)
        sc_ref[...] = sc
        res = _dot(sc.astype(BF16), w_ref[...].astype(BF16)) + b_ref[...]
        send_buf[...] = res.reshape(8, 8, ADA_B)
        ada_ref[mi] = send_buf[mi]
        second = []
        for k, p in enumerate(peers):
            second.append(_remote(send_buf.at[_lin(p)], ada_ref.at[mi], ssem.at[14 + k], rsem.at[14 + k], p))
        for cp in second:
            cp.start()
        finish_jobs()
        for k, p in enumerate(peers):
            _remote(send_buf.at[mi], ada_ref.at[_lin(p)], ssem.at[14 + k], rsem.at[14 + k], p).wait_recv()
        for cp in first + second:
            cp.wait_send()

    return _call(
        core, name="ada_fwd", grid=(), jobs=jobs, core_starts=True, in_specs=[VM, VM, VM, VM], out_specs=[VM, VM, VM],
        out_shape=[jax.ShapeDtypeStruct((8, 8, ADA_B), F32), jax.ShapeDtypeStruct((64, D), F32),
                   jax.ShapeDtypeStruct((8, 32, 64), F32)],
        scratch=[pltpu.VMEM((8, 8, D), F32), pltpu.VMEM((8, 8, ADA_B), F32),
                 pltpu.SemaphoreType.DMA((21,)), pltpu.SemaphoreType.DMA((21,))],
        args=[c_pad, w_ada, b_cols, cw_pad])


def _ada_bwd(dada, jobs=()):
    def core(ins, outs, scs):
        (d_ref,) = ins
        dd_ref, gb_ref = outs
        rbuf, ssem, rsem = scs
        me = _me()
        mi = _lin(me)
        peers = [_flip(me, f) for f in FLIPS]
        rbuf[mi] = d_ref[mi]
        first = []
        for k, p in enumerate(peers):
            first.append(_remote(d_ref.at[_lin(p)], rbuf.at[mi], ssem.at[k], rsem.at[k], p))
        for cp in first:
            cp.start()
        for k, p in enumerate(peers):
            _remote(d_ref.at[mi], rbuf.at[_lin(p)], ssem.at[k], rsem.at[k], p).wait_recv()
        dd = rbuf[...].reshape(64, ADA_B)
        dd_ref[...] = dd
        gb_ref[...] = jnp.broadcast_to(_colsum(dd), (8, ADA_B))
        for cp in first:
            cp.wait_send()

    return _call(
        core, name="ada_bwd", grid=(), jobs=jobs, in_specs=[VM], out_specs=[VM, VM],
        out_shape=[jax.ShapeDtypeStruct((64, ADA_B), F32), jax.ShapeDtypeStruct((8, ADA_B), F32)],
        scratch=[pltpu.VMEM((8, 8, ADA_B), F32), pltpu.SemaphoreType.DMA((7,)), pltpu.SemaphoreType.DMA((7,))],
        args=[dada])


SMALL_D = ("g_pre_f1", "g_post_f1", "g_pre_m", "g_post_m", "g_pre_f2", "g_post_f2")
SMALL_W = ("gmlp_norm_g", "gmlp_norm_b", "conv_b", "conv_norm_g", "conv_norm_b", "g_out_a", "g_out_b")


def kernel(x, c, w_ada, b_ada, g_pre_f1, g_post_f1, w_f1_in, w_f1_out, g_pre_m, g_post_m, w_mix_in, gmlp_norm_g, gmlp_norm_b, w_spatial, b_spatial, conv_w, conv_b, conv_norm_g, conv_norm_b, g_out_a, g_out_b, w_mix_out, g_pre_f2, g_post_f2, w_f2_in, w_f2_out, loss_target, m_w_ada, m_b_ada, m_g_pre_f1, m_g_post_f1, m_w_f1_in, m_w_f1_out, m_g_pre_m, m_g_post_m, m_w_mix_in, m_gmlp_norm_g, m_gmlp_norm_b, m_w_spatial, m_b_spatial, m_conv_w, m_conv_b, m_conv_norm_g, m_conv_norm_b, m_g_out_a, m_g_out_b, m_w_mix_out, m_g_pre_f2, m_g_post_f2, m_w_f2_in, m_w_f2_out, v_w_ada, v_b_ada, v_g_pre_f1, v_g_post_f1, v_w_f1_in, v_w_f1_out, v_g_pre_m, v_g_post_m, v_w_mix_in, v_gmlp_norm_g, v_gmlp_norm_b, v_w_spatial, v_b_spatial, v_conv_w, v_conv_b, v_conv_norm_g, v_conv_norm_b, v_g_out_a, v_g_out_b, v_w_mix_out, v_g_pre_f2, v_g_post_f2, v_w_f2_in, v_w_f2_out):
    given = dict(locals())
    bl, seq, _ = x.shape
    T = bl * seq
    tm = min(256, seq // 2)
    mi = _lin((lax.axis_index("x"), lax.axis_index("y"), lax.axis_index("c")))

    def shard_in(w):
        return w[0].T.astype(BF16)

    g_f1 = _RelayGather([shard_in(w_f1_in), w_f1_out[0].astype(BF16)], ("rows", "out"))
    s_f2 = shard_in(w_f2_in)
    g_mx = _Gather([w_mix_in[0].astype(BF16), w_mix_out[0].astype(BF16), w_f2_out[0].astype(BF16), s_f2[:, 0:D // 4]],
                   ("rows", "rows", "out", "rows"), late_mid=True)
    g_f2 = _Gather([s_f2[:, D // 4:D]], ("rows",))

    c_pad = jnp.pad(c, ((0, 8 - bl), (0, 0)))
    b_cols = lax.dynamic_slice(b_ada, (0, mi * ADA_B), (1, ADA_B))
    cw_pad = jnp.pad(conv_w[0], ((0, 1), (0, 0)))
    (ada_blk, sc_all, cw_all), ((wi1, wo1),) = _ada_fwd(c_pad, w_ada[0], b_cols, cw_pad, jobs=[g_f1])
    ada = ada_blk[:, 0:bl, :].transpose(1, 0, 2).reshape(bl, 9, D)
    pad5 = jnp.zeros((bl, 5, D), F32)
    mod1 = jnp.concatenate([ada[:, 0:3], pad5], axis=1)
    mod2 = jnp.concatenate([ada[:, 3:6], pad5], axis=1)
    mod3 = jnp.concatenate([ada[:, 6:9], pad5], axis=1)
    cw_full = cw_all.transpose(1, 0, 2).reshape(32, WA)

    zrow = jnp.zeros((1, D), F32)
    gv1 = jnp.concatenate([g_pre_f1, g_post_f1] + [zrow] * 6, axis=0)
    gvm = jnp.concatenate([g_pre_m, g_post_m] + [zrow] * 6, axis=0)
    gv2 = jnp.concatenate([g_pre_f2, g_post_f2] + [zrow] * 6, axis=0)
    v512 = jnp.concatenate([gmlp_norm_g, gmlp_norm_b, conv_b, conv_norm_g, conv_norm_b, g_out_a, g_out_b,
                            jnp.zeros((1, WA), F32)], axis=0)
    ws = w_spatial[0]
    bias_full = jnp.repeat(b_spatial[0].T, HD, axis=1)
    esel = (lax.broadcasted_iota(jnp.int32, (8, WA), 1) // HD == lax.broadcasted_iota(jnp.int32, (8, WA), 0)).astype(F32)

    x0 = x.reshape(T, D)
    (x1, gu1, y1), ((wmi, wmo, wo2, wi2a),) = _ffn_fwd(x0, mod1, gv1, wi1, wo1, tm, "ffn1_fwd", jobs=[g_mx])
    wmo = wmo.reshape(D, D)
    (x2, proj, ym, conv), ((wi2b,),) = _mixer_fwd(x1, mod2, gvm, wmi, wmo, v512, ws, bias_full, cw_full, tm, "mixer_fwd", jobs=[g_f2])

    (dx2, dg2, act2, hb2, dyb2, mg3, vg3, loss_blk), _ = _ffn_last(
        x2, loss_target.reshape(T, D), mod3, gv2, (wi2a, wi2b), wo2, tm, "ffn2_fwd_bwd")
    (g_wi2,), _ = _grad_w_in(dg2, hb2, "ffn2_gw_in")
    (g_wo2,), _ = _grad_w_out(act2, dyb2, "ffn2_gw_out")
    (dpart, dymb, ycat, mg2a, vgma, v5g, gws, gbs), ((p_wo2,),) = _mixer_bwd_a(
        dx2, ym, proj, conv, mod2, gvm, wmo, v512, ws, bias_full, esel, tm, "mixer_bwd_a",
        jobs=[_ChipScatter([g_wo2])])
    (dx1, dproj, hbm, mg2b, vgmb, dcw), ((p_wi2,),) = _mixer_bwd_b(
        dx2, x1, dpart, proj, mod2, gvm, wmi, cw_full, tm, "mixer_bwd_b", jobs=[_ChipScatter([g_wi2])])
    (g_wmi,), _ = _grad_w_mi(hbm, dproj, "mixer_gw_in")
    (g_wmo,), _ = _grad_w_mo(ycat, dymb, "mixer_gw_out")
    p2 = jnp.concatenate([v5g, dcw], axis=0)
    (dx0, dg1, act1, hb1, dyb1, mg1, vg1), _ = _ffn_bwd(dx1, x0, y1, gu1, mod1, gv1, wi1, wo1, tm, "ffn1_bwd")

    dada = jnp.concatenate([mg1[:, 0:3], mg2b[:, 0:2], mg2a[:, 2:3], mg3[:, 0:3]], axis=1)
    dada = dada.reshape(bl, NDEV, ADA_B).transpose(1, 0, 2)
    dada = jnp.pad(dada, ((0, 0), (0, 8 - bl), (0, 0)))
    p1 = jnp.concatenate([vg1[0:2], vgmb[0:1], vgma[1:2], vg3[0:2], loss_blk[0:1], zrow], axis=0)
    (dd_all, gb_own), ((a1,),) = _ada_bwd(dada, jobs=[_AllGather([p1])])

    (g_wi1,), ((a2, a3, a4, gb_all), (p_wmi, p_wmo)) = _grad_w_in(
        dg1, hb1, "ffn1_gw_in", jobs=[_Gather([p2, gws, gbs, gb_own], ("rows",) * 4), _ChipScatter([g_wmi, g_wmo])])
    g_bada = gb_all[:, 0, :].reshape(1, 9 * D)

    h_i1, token = _chip_scatter_start([g_wi1], "tail_start")
    (g_wo1,), _ = _grad_w_out(act1, dyb1, "ffn1_gw_out", after=token)
    h_o1, token = _chip_scatter_start([g_wo1], "tail2_start")

    res = {}
    quad = _adamw_reduce(p_wi2, w_f2_in[0].T, m_w_f2_in[0].T, v_w_f2_in[0].T, FO, "adamw_w_f2_in", after=token)
    res["w_f2_in"] = tuple(t.T[None] for t in quad)
    for nm, part, tr in (("w_f2_out", p_wo2, FO), ("w_mix_in", p_wmi, 256), ("w_mix_out", p_wmo, MO)):
        quad = _adamw_reduce(part, given[nm][0], given["m_" + nm][0], given["v_" + nm][0], tr, "adamw_" + nm, after=quad[1])
        res[nm] = tuple(t[None] for t in quad)
    quad = _adamw_ada(sc_all, dd_all, w_ada[0], m_w_ada[0], v_w_ada[0], 256, "adamw_w_ada", after=quad[1])
    res["w_ada"] = tuple(t[None] for t in quad)

    small = SMALL_D + SMALL_W + ("w_spatial", "b_spatial", "b_ada")
    grads = [(0, r) for r in range(6)] + [(1, r) for r in range(7)] + [(2, None), (3, None), (4, None)]
    wmv = []
    for nm in small:
        for pre in ("", "m_", "v_"):
            wmv.append(given[pre + nm][0] if nm in ("w_spatial", "b_spatial") else given[pre + nm])
    outs = _adamw_small([a1, a2, a3, a4], [g_bada], grads, wmv, (0, 1), "adamw_small", after=quad[1])

    (g_wi1,), (p_wi1,) = _chip_scatter_wait(h_i1, outs[0], "tail_wait")
    quad = _adamw_reduce(p_wi1, w_f1_in[0].T, m_w_f1_in[0].T, v_w_f1_in[0].T, FO, "adamw_w_f1_in", own=g_wi1)
    res["w_f1_in"] = tuple(t.T[None] for t in quad)
    (g_wo1,), (p_wo1,) = _chip_scatter_wait(h_o1, quad[1], "tail2_wait")
    quad = _adamw_reduce(p_wo1, w_f1_out[0], m_w_f1_out[0], v_w_f1_out[0], FO, "adamw_w_f1_out", own=g_wo1)
    res["w_f1_out"] = tuple(t[None] for t in quad)
    loss = outs[0][6, 0]
    for t, nm in enumerate(small):
        quad = outs[2 + 4 * t:6 + 4 * t]
        res[nm] = tuple(q[None] for q in quad) if nm in ("w_spatial", "b_spatial") else tuple(quad)
    g_cw = lax.dynamic_slice(outs[1], (8, mi * 64), (32, 64))
    wmv = [jnp.pad(given[pre + "conv_w"][0], ((0, 1), (0, 0)), constant_values=1.0 if pre == "v_" else 0.0)
           for pre in ("", "m_", "v_")]
    quad = _adamw_small([], [g_cw], [(0, None)], wmv, (), "adamw_conv_w")
    res["conv_w"] = tuple(q[0:CONV_K][None] for q in quad)

    order = ["w_ada", "b_ada", "g_pre_f1", "g_post_f1", "w_f1_in", "w_f1_out", "g_pre_m", "g_post_m", "w_mix_in",
             "gmlp_norm_g", "gmlp_norm_b", "w_spatial", "b_spatial", "conv_w", "conv_b", "conv_norm_g", "conv_norm_b",
             "g_out_a", "g_out_b", "w_mix_out", "g_pre_f2", "g_post_f2", "w_f2_in", "w_f2_out"]
    out = [loss, dx0.reshape(bl, seq, D)]
    for k in range(4):
        out += [res[nm][k] for nm in order]
    return tuple(out)
```

```python
import jax
import jax.numpy as jnp
from jax import lax
from jax.experimental import pallas as pl
from jax.experimental.pallas import tpu as pltpu

F32 = jnp.float32
BF16 = jnp.bfloat16

D = 1024
DFF = 2816
NDEV = 8
FB = 2 * DFF // NDEV
NCH = DFF // FB
LANES = 128
SUBL = 8
FO = DFF // NDEV
WA = 512
NSLAB = WA // LANES
NHEAD = 8
HD = 64
CHUNK = 128
CONV_K = 31
HALO = 32
MB = 2 * (WA + WA) // NDEV
MO = D // NDEV
ADA_B = 9 * D // NDEV
EPS = 1e-6
HALF = 0.5

ADAM_LR = 0.001
ADAM_B1 = 0.9
ADAM_B2 = 0.999
ADAM_EPS = 1e-08
ADAM_WD = 0.01
ADAM_STEP = 10

VMEM_LIMIT = 56 * 1024 * 1024
MESH = pl.DeviceIdType.MESH
FLIPS = ((0, 0, 1), (1, 0, 0), (0, 1, 0), (1, 1, 0), (1, 0, 1), (0, 1, 1), (1, 1, 1))
CHIP_FLIPS = ((1, 0, 0), (0, 1, 0), (1, 1, 0))
HBM = pl.BlockSpec(memory_space=pl.ANY)
VM = pl.BlockSpec(memory_space=pltpu.VMEM)


def _dot(a, b):
    return lax.dot_general(a, b, (((1,), (0,)), ((), ())), preferred_element_type=F32)


def _dot_nt(a, b):
    return lax.dot_general(a, b, (((1,), (1,)), ((), ())), preferred_element_type=F32)


def _dot_tn(a, b):
    return lax.dot_general(a, b, (((0,), (0,)), ((), ())), preferred_element_type=F32)


def _rowmean(v):
    return jnp.mean(v, axis=-1, keepdims=True)


def _colsum(v):
    return jnp.sum(v, axis=0, keepdims=True)


def _sigmoid(v):
    return 0.5 * jnp.tanh(0.5 * v) + 0.5


def _const_spec(shape):
    nd = len(shape)
    return pl.BlockSpec(shape, lambda *_: (0,) * nd, pipeline_mode=pl.Buffered(1))


def _me():
    return lax.axis_index("x"), lax.axis_index("y"), lax.axis_index("c")


def _flip(me, f):
    return tuple(1 - v if b else v for v, b in zip(me, f))


def _lin(p):
    return 4 * p[0] + 2 * p[1] + p[2]


def _remote(src, dst, send_sem, recv_sem, dev):
    return pltpu.make_async_remote_copy(src_ref=src, dst_ref=dst, send_sem=send_sem, recv_sem=recv_sem,
                                        device_id=dev, device_id_type=MESH)


def _blk(kind, ref, p):
    if kind == "out":
        return ref.at[2 * p[0] + p[1], pl.ds(p[2] * FO, FO), :]
    return ref.at[_lin(p)]


class _Gather:
    def __init__(self, shards, kinds, late_mid=False):
        self.late_mid = late_mid
        self.kinds = kinds
        self.n = len(shards)
        self.ins = list(shards)
        self.out_shape = [jax.ShapeDtypeStruct((4, FB, D) if k == "out" else (NDEV,) + s.shape, s.dtype)
                          for s, k in zip(shards, kinds)]
        self.sems = [pltpu.SemaphoreType.DMA((7 * self.n,)), pltpu.SemaphoreType.DMA((7 * self.n,)),
                     pltpu.SemaphoreType.DMA((self.n,))]

    def _first(self, ins, outs, sems):
        ssem, rsem, lsem = sems
        me = _me()
        sib = _flip(me, (0, 0, 1))
        cps, loc = [], []
        for a in range(self.n):
            mine = _blk(self.kinds[a], outs[a], me)
            loc.append(pltpu.make_async_copy(ins[a], mine, lsem.at[a]))
            cps.append(_remote(ins[a], mine, ssem.at[7 * a], rsem.at[7 * a], sib))
            for j, f in enumerate(CHIP_FLIPS):
                cps.append(_remote(ins[a], mine, ssem.at[7 * a + 1 + j], rsem.at[7 * a + 1 + j], _flip(me, f)))
        return cps, loc

    def _passed(self, outs, sems):
        ssem, rsem, _ = sems
        me = _me()
        sib = _flip(me, (0, 0, 1))
        cps = []
        for j, f in enumerate(CHIP_FLIPS):
            for a in range(self.n):
                blk = _blk(self.kinds[a], outs[a], _flip(me, f))
                cps.append(_remote(blk, blk, ssem.at[7 * a + 4 + j], rsem.at[7 * a + 4 + j], sib))
        return cps

    def start(self, ins, outs, sems):
        cps, loc = self._first(ins, outs, sems)
        for cp in loc + cps:
            cp.start()

    def mid(self, ins, outs, sems):
        ssem, rsem, _ = sems
        me = _me()
        passed = self._passed(outs, sems)
        t = 0
        for j, f in enumerate(CHIP_FLIPS):
            for a in range(self.n):
                blk = _blk(self.kinds[a], outs[a], _flip(me, f))
                _remote(blk, blk, ssem.at[7 * a + 1 + j], rsem.at[7 * a + 1 + j], _flip(me, f)).wait_recv()
                passed[t].start()
                t += 1

    def end(self, ins, outs, sems):
        ssem, rsem, _ = sems
        me = _me()
        sib = _flip(me, (0, 0, 1))
        for a in range(self.n):
            blk = _blk(self.kinds[a], outs[a], sib)
            _remote(blk, blk, ssem.at[7 * a], rsem.at[7 * a], sib).wait_recv()
            for j, f in enumerate(CHIP_FLIPS):
                blk = _blk(self.kinds[a], outs[a], _flip(_flip(me, f), (0, 0, 1)))
                _remote(blk, blk, ssem.at[7 * a + 4 + j], rsem.at[7 * a + 4 + j], sib).wait_recv()
        cps, loc = self._first(ins, outs, sems)
        for cp in cps + self._passed(outs, sems):
            cp.wait_send()
        for cp in loc:
            cp.wait()


class _RelayGather(_Gather):
    def _peers(self):
        me = _me()
        c = me[2]
        to = (me[0] + (1 - c) - 2 * me[0] * (1 - c), me[1] + c - 2 * me[1] * c, c)
        frm = (me[0] + c - 2 * me[0] * c, me[1] + (1 - c) - 2 * me[1] * (1 - c), c)
        return me, _flip(me, (0, 0, 1)), to, frm, _flip(me, (1, 1, 0))

    def _first(self, ins, outs, sems):
        ssem, rsem, lsem = sems
        me, sib, to, frm, _ = self._peers()
        cps, loc = [], []
        for a in range(self.n):
            mine = _blk(self.kinds[a], outs[a], me)
            loc.append(pltpu.make_async_copy(ins[a], mine, lsem.at[a]))
            for slot, dev in ((0, sib), (1, to), (2, frm)):
                cps.append(_remote(ins[a], mine, ssem.at[7 * a + slot], rsem.at[7 * a + slot], dev))
        return cps, loc

    def _block_copy(self, outs, sems, a, slot, owner, dev):
        ssem, rsem, _ = sems
        blk = _blk(self.kinds[a], outs[a], owner)
        return _remote(blk, blk, ssem.at[7 * a + slot], rsem.at[7 * a + slot], dev)

    def mid(self, ins, outs, sems):
        me, sib, to, frm, _ = self._peers()
        for a in range(self.n):
            self._block_copy(outs, sems, a, 2, frm, frm).wait_recv()
            self._block_copy(outs, sems, a, 3, frm, to).start()
            self._block_copy(outs, sems, a, 5, frm, sib).start()
        for a in range(self.n):
            self._block_copy(outs, sems, a, 1, to, to).wait_recv()
            self._block_copy(outs, sems, a, 4, to, sib).start()

    def end(self, ins, outs, sems):
        me, sib, to, frm, far = self._peers()
        up = (0, 0, 1)
        for a in range(self.n):
            self._block_copy(outs, sems, a, 3, far, to).wait_recv()
            self._block_copy(outs, sems, a, 6, far, sib).start()
        for a in range(self.n):
            for slot, owner in ((0, sib), (4, _flip(frm, up)), (5, _flip(to, up)), (6, _flip(far, up))):
                self._block_copy(outs, sems, a, slot, owner, sib).wait_recv()
        cps, loc = self._first(ins, outs, sems)
        for a in range(self.n):
            cps += [self._block_copy(outs, sems, a, 3, frm, to), self._block_copy(outs, sems, a, 4, to, sib),
                    self._block_copy(outs, sems, a, 5, frm, sib), self._block_copy(outs, sems, a, 6, far, sib)]
        for cp in cps:
            cp.wait_send()
        for cp in loc:
            cp.wait()


class _ChipScatter:
    def __init__(self, grads):
        self.n = len(grads)
        self.ins = list(grads)
        self.out_shape = [jax.ShapeDtypeStruct(g.shape, BF16) for g in grads]
        self.sems = [pltpu.SemaphoreType.DMA((3 * self.n,)), pltpu.SemaphoreType.DMA((3 * self.n,)),
                     pltpu.SemaphoreType.DMA((self.n,))]

    def _copies(self, ins, outs, sems):
        ssem, rsem, lsem = sems
        me = _me()
        mq = 2 * me[0] + me[1]
        loc = [pltpu.make_async_copy(ins[a].at[mq], outs[a].at[mq], lsem.at[a]) for a in range(self.n)]
        cps = []
        for k, f in enumerate(CHIP_FLIPS):
            p = _flip(me, f)
            for a in range(self.n):
                cps.append(_remote(ins[a].at[2 * p[0] + p[1]], outs[a].at[mq], ssem.at[3 * a + k], rsem.at[3 * a + k], p))
        return cps, loc

    def start(self, ins, outs, sems):
        cps, loc = self._copies(ins, outs, sems)
        for cp in loc + cps:
            cp.start()

    mid = None

    def end(self, ins, outs, sems):
        ssem, rsem, _ = sems
        me = _me()
        mq = 2 * me[0] + me[1]
        for k, f in enumerate(CHIP_FLIPS):
            p = _flip(me, f)
            for a in range(self.n):
                _remote(ins[a].at[mq], outs[a].at[2 * p[0] + p[1]], ssem.at[3 * a + k], rsem.at[3 * a + k], p).wait_recv()
        cps, loc = self._copies(ins, outs, sems)
        for cp in cps:
            cp.wait_send()
        for cp in loc:
            cp.wait()


class _AllGather:
    def __init__(self, parts):
        self.n = len(parts)
        self.ins = list(parts)
        self.out_shape = [jax.ShapeDtypeStruct((NDEV,) + p.shape, p.dtype) for p in parts]
        self.sems = [pltpu.SemaphoreType.DMA((7 * self.n,)), pltpu.SemaphoreType.DMA((7 * self.n,)),
                     pltpu.SemaphoreType.DMA((self.n,))]

    def _copies(self, ins, outs, sems):
        ssem, rsem, lsem = sems
        me = _me()
        mi = _lin(me)
        loc = [pltpu.make_async_copy(ins[a], outs[a].at[mi], lsem.at[a]) for a in range(self.n)]
        cps = []
        for k, f in enumerate(FLIPS):
            for a in range(self.n):
                cps.append(_remote(ins[a], outs[a].at[mi], ssem.at[7 * a + k], rsem.at[7 * a + k], _flip(me, f)))
        return cps, loc

    def start(self, ins, outs, sems):
        cps, loc = self._copies(ins, outs, sems)
        for cp in loc + cps:
            cp.start()

    mid = None

    def end(self, ins, outs, sems):
        ssem, rsem, _ = sems
        me = _me()
        for k, f in enumerate(FLIPS):
            p = _flip(me, f)
            for a in range(self.n):
                _remote(ins[a], outs[a].at[_lin(p)], ssem.at[7 * a + k], rsem.at[7 * a + k], p).wait_recv()
        cps, loc = self._copies(ins, outs, sems)
        for cp in cps:
            cp.wait_send()
        for cp in loc:
            cp.wait()


def _call(core, *, name, grid, in_specs, out_specs, out_shape, args, scratch=(), jobs=(), core_starts=False):
    n_in, n_out, n_sc = len(in_specs), len(out_specs), len(scratch)
    steps = 1
    for g in grid:
        steps *= g

    def body(*refs):
        pos = [0]

        def take(k):
            r = refs[pos[0]:pos[0] + k]
            pos[0] += k
            return r

        ins = take(n_in)
        j_ins = [take(len(j.ins)) for j in jobs]
        outs = take(n_out)
        j_outs = [take(len(j.out_shape)) for j in jobs]
        scs = take(n_sc)
        j_sems = [take(len(j.sems)) for j in jobs]
        if len(grid) == 2:
            step = pl.program_id(0) * grid[1] + pl.program_id(1)
        elif len(grid) == 1:
            step = pl.program_id(0)
        else:
            step = 0
        def start_jobs():
            for j, ji, jo, js in zip(jobs, j_ins, j_outs, j_sems):
                j.start(ji, jo, js)

        if grid:
            pl.when(step == 0)(start_jobs)
        elif not core_starts:
            start_jobs()
        for j, ji, jo, js in zip(jobs, j_ins, j_outs, j_sems):
            if j.mid is not None and grid:
                at = max(steps - 2, 0) if j.late_mid else (3 * steps) // 4
                pl.when(step == at)(lambda j=j, ji=ji, jo=jo, js=js: j.mid(ji, jo, js))
        def finish_jobs():
            for j, ji, jo, js in zip(jobs, j_ins, j_outs, j_sems):
                if j.mid is not None:
                    j.mid(ji, jo, js)
                j.end(ji, jo, js)

        if core_starts:
            core(ins, outs, scs, start_jobs, finish_jobs)
        elif core is not None:
            core(ins, outs, scs)
        if grid:
            for j, ji, jo, js in zip(jobs, j_ins, j_outs, j_sems):
                pl.when(step == steps - 1)(lambda j=j, ji=ji, jo=jo, js=js: j.end(ji, jo, js))
        elif not core_starts:
            finish_jobs()

    all_in = list(in_specs)
    all_args = list(args)
    all_out = list(out_specs)
    all_shape = list(out_shape)
    all_sc = list(scratch)
    for j in jobs:
        all_in += [HBM] * len(j.ins)
        all_args += j.ins
    for j in jobs:
        all_out += [HBM] * len(j.out_shape)
        all_shape += j.out_shape
        all_sc += j.sems
    params = dict(vmem_limit_bytes=VMEM_LIMIT)
    if grid:
        params["dimension_semantics"] = ("arbitrary",) * len(grid)
    res = pl.pallas_call(
        body, name=name, grid=grid, in_specs=all_in, out_specs=all_out, out_shape=all_shape,
        scratch_shapes=all_sc, compiler_params=pltpu.CompilerParams(**params),
    )(*all_args)
    core_res = list(res[:n_out])
    job_res = []
    pos = n_out
    for j in jobs:
        job_res.append(list(res[pos:pos + len(j.out_shape)]))
        pos += len(j.out_shape)
    return core_res, job_res


def _ffn_fwd(x, mod, gvec, w_in, w_out, tm, name, jobs=()):
    T = x.shape[0]
    nt = T // tm
    tps = nt // mod.shape[0]

    def core(ins, outs, _):
        x_ref, mod_ref, g_ref, win_ref, wout_ref = ins
        xo_ref, gu_ref, y_ref = outs
        xv = x_ref[...]
        sh, sc, gt = mod_ref[0:1, :], mod_ref[1:2, :], mod_ref[2:3, :]
        r = lax.rsqrt(_rowmean(xv * xv) + EPS)
        h = (xv * r * g_ref[0:1, :]) * (1.0 + sc) + sh
        hb = h.astype(BF16)
        y = jnp.zeros((tm, D), F32)
        for cidx in range(NCH):
            gate = _dot_nt(hb, win_ref[cidx])
            up = _dot_nt(hb, win_ref[NCH + cidx])
            gu_ref[cidx] = gate.astype(BF16)
            gu_ref[NCH + cidx] = up.astype(BF16)
            act = gate * _sigmoid(gate) * up
            y = y + _dot(act.astype(BF16), wout_ref[cidx])
        y_ref[...] = y
        ry = lax.rsqrt(_rowmean(y * y) + EPS)
        xo_ref[...] = xv + (HALF * gt) * (y * ry * g_ref[1:2, :])

    tile = pl.BlockSpec((tm, D), lambda i: (i, 0))
    return _call(
        core, name=name, grid=(nt,), jobs=jobs,
        in_specs=[tile, pl.BlockSpec((None, 8, D), lambda i: (i // tps, 0, 0)), _const_spec((8, D)),
                  _const_spec((8, FB, D)), _const_spec((4, FB, D))],
        out_specs=[tile, pl.BlockSpec((8, tm, FB), lambda i: (0, i, 0)), tile],
        out_shape=[jax.ShapeDtypeStruct((T, D), F32), jax.ShapeDtypeStruct((8, T, FB), BF16),
                   jax.ShapeDtypeStruct((T, D), F32)],
        args=[x, mod, gvec, w_in, w_out])


def _ffn_bwd(dxo, x, y, gu, mod, gvec, w_in, w_out, tm, name, jobs=()):
    T = x.shape[0]
    nt = T // tm
    nb = mod.shape[0]
    tps = nt // nb

    def core(ins, outs, _):
        dxo_ref, x_ref, y_ref, gu_ref, mod_ref, g_ref, win_ref, wout_ref = ins
        dx_ref, dg_ref, act_ref, hb_ref, dyb_ref, mg_ref, vg_ref = outs
        i = pl.program_id(0)
        xv = x_ref[...]
        dxo_v = dxo_ref[...]
        yv = y_ref[...]
        sh, sc, gt = mod_ref[0:1, :], mod_ref[1:2, :], mod_ref[2:3, :]
        gpre, gpost = g_ref[0:1, :], g_ref[1:2, :]
        r = lax.rsqrt(_rowmean(xv * xv) + EPS)
        xh = xv * r
        n = xh * gpre
        hb = (n * (1.0 + sc) + sh).astype(BF16)
        hb_ref[...] = hb
        ry = lax.rsqrt(_rowmean(yv * yv) + EPS)
        yh = yv * ry
        d_gt = _colsum(HALF * dxo_v * (yh * gpost))
        dp = (HALF * gt) * dxo_v
        d_gpost = _colsum(dp * yh)
        dyh = dp * gpost
        dy = ry * (dyh - yh * _rowmean(dyh * yh))
        dyb = dy.astype(BF16)
        dyb_ref[...] = dyb
        dh = jnp.zeros((tm, D), F32)
        for cidx in range(NCH):
            gate = gu_ref[cidx].astype(F32)
            up = gu_ref[NCH + cidx].astype(F32)
            sig = _sigmoid(gate)
            s = gate * sig
            act_ref[cidx] = (s * up).astype(BF16)
            d_act = _dot_nt(dyb, wout_ref[cidx])
            d_up = (d_act * s).astype(BF16)
            d_gate = (d_act * up * (sig * (1.0 + gate * (1.0 - sig)))).astype(BF16)
            dg_ref[cidx] = d_gate
            dg_ref[NCH + cidx] = d_up
            dh = dh + _dot(d_gate, win_ref[cidx]) + _dot(d_up, win_ref[NCH + cidx])
        d_sc = _colsum(dh * n)
        d_sh = _colsum(dh)
        dn = dh * (1.0 + sc)
        d_gpre = _colsum(dn * xh)
        dxh = dn * gpre
        dx_ref[...] = dxo_v + r * (dxh - xh * _rowmean(dxh * xh))

        @pl.when(i % tps == 0)
        def _():
            mg_ref[...] = jnp.zeros((8, D), F32)

        @pl.when(i == 0)
        def _():
            vg_ref[...] = jnp.zeros((8, D), F32)

        mg_ref[0:1, :] += d_sh
        mg_ref[1:2, :] += d_sc
        mg_ref[2:3, :] += d_gt
        vg_ref[0:1, :] += d_gpre
        vg_ref[1:2, :] += d_gpost

    tile = pl.BlockSpec((tm, D), lambda i: (i, 0))
    return _call(
        core, name=name, grid=(nt,), jobs=jobs,
        in_specs=[tile, tile, tile, pl.BlockSpec((8, tm, FB), lambda i: (0, i, 0)),
                  pl.BlockSpec((None, 8, D), lambda i: (i // tps, 0, 0)), _const_spec((8, D)),
                  _const_spec((8, FB, D)), _const_spec((4, FB, D))],
        out_specs=[tile, pl.BlockSpec((8, tm, FB), lambda i: (0, i, 0)),
                   pl.BlockSpec((4, tm, FB), lambda i: (0, i, 0)), tile, tile,
                   pl.BlockSpec((None, 8, D), lambda i: (i // tps, 0, 0)), pl.BlockSpec((8, D), lambda i: (0, 0))],
        out_shape=[jax.ShapeDtypeStruct((T, D), F32), jax.ShapeDtypeStruct((8, T, FB), BF16),
                   jax.ShapeDtypeStruct((4, T, FB), BF16), jax.ShapeDtypeStruct((T, D), BF16),
                   jax.ShapeDtypeStruct((T, D), BF16), jax.ShapeDtypeStruct((nb, 8, D), F32),
                   jax.ShapeDtypeStruct((8, D), F32)],
        args=[dxo, x, y, gu, mod, gvec, w_in, w_out])


def _ffn_last(x, target, mod, gvec, w_in, w_out, tm, name, jobs=()):
    T = x.shape[0]
    nt = T // tm
    nb = mod.shape[0]
    tps = nt // nb

    def core(ins, outs, scs):
        x_ref, t_ref, mod_ref, g_ref, wina_ref, winb_ref, wout_ref = ins
        dx_ref, dg_ref, act_ref, hb_ref, dyb_ref, mg_ref, vg_ref, loss_ref = outs
        hd2 = w_in[0].shape[2]
        (gu_s,) = scs
        i = pl.program_id(0)
        xv = x_ref[...]
        sh, sc, gt = mod_ref[0:1, :], mod_ref[1:2, :], mod_ref[2:3, :]
        gpre, gpost = g_ref[0:1, :], g_ref[1:2, :]
        r = lax.rsqrt(_rowmean(xv * xv) + EPS)
        xh = xv * r
        n = xh * gpre
        hb = (n * (1.0 + sc) + sh).astype(BF16)
        hb_ref[...] = hb
        hba, hbb = hb[:, 0:hd2], hb[:, hd2:D]
        yv = jnp.zeros((tm, D), F32)
        for cidx in range(NCH):
            gate = _dot_nt(hba, wina_ref[cidx]) + _dot_nt(hbb, winb_ref[cidx])
            up = _dot_nt(hba, wina_ref[NCH + cidx]) + _dot_nt(hbb, winb_ref[NCH + cidx])
            gu_s[cidx] = gate.astype(BF16)
            gu_s[NCH + cidx] = up.astype(BF16)
            act = gate * _sigmoid(gate) * up
            act_ref[cidx] = act.astype(BF16)
            yv = yv + _dot(act_ref[cidx], wout_ref[cidx])
        ry = lax.rsqrt(_rowmean(yv * yv) + EPS)
        yh = yv * ry
        pn = yh * gpost
        err = xv + (HALF * gt) * pn - t_ref[...]
        dxo_v = err * (1.0 / D)
        d_gt = _colsum(HALF * dxo_v * pn)
        dp = (HALF * gt) * dxo_v
        d_gpost = _colsum(dp * yh)
        dyh = dp * gpost
        dyb = (ry * (dyh - yh * _rowmean(dyh * yh))).astype(BF16)
        dyb_ref[...] = dyb
        dha = jnp.zeros((tm, hd2), F32)
        dhb = jnp.zeros((tm, D - hd2), F32)
        for cidx in range(NCH):
            gate = gu_s[cidx].astype(F32)
            up = gu_s[NCH + cidx].astype(F32)
            sig = _sigmoid(gate)
            s = gate * sig
            d_act = _dot_nt(dyb, wout_ref[cidx])
            d_up = (d_act * s).astype(BF16)
            d_gate = (d_act * up * (sig * (1.0 + gate * (1.0 - sig)))).astype(BF16)
            dg_ref[cidx] = d_gate
            dg_ref[NCH + cidx] = d_up
            dha = dha + _dot(d_gate, wina_ref[cidx]) + _dot(d_up, wina_ref[NCH + cidx])
            dhb = dhb + _dot(d_gate, winb_ref[cidx]) + _dot(d_up, winb_ref[NCH + cidx])
        dh = jnp.concatenate([dha, dhb], axis=1)
        d_sc = _colsum(dh * n)
        d_sh = _colsum(dh)
        dn = dh * (1.0 + sc)
        d_gpre = _colsum(dn * xh)
        dxh = dn * gpre
        dx_ref[...] = dxo_v + r * (dxh - xh * _rowmean(dxh * xh))

        @pl.when(i % tps == 0)
        def _():
            mg_ref[...] = jnp.zeros((8, D), F32)

        @pl.when(i == 0)
        def _():
            vg_ref[...] = jnp.zeros((8, D), F32)
            loss_ref[...] = jnp.zeros((8, D), F32)

        mg_ref[0:1, :] += d_sh
        mg_ref[1:2, :] += d_sc
        mg_ref[2:3, :] += d_gt
        vg_ref[0:1, :] += d_gpre
        vg_ref[1:2, :] += d_gpost
        loss_ref[...] += HALF * jnp.sum(_rowmean(err * err), axis=0, keepdims=True)

    tile = pl.BlockSpec((tm, D), lambda i: (i, 0))
    return _call(
        core, name=name, grid=(nt,), jobs=jobs,
        in_specs=[tile, tile, pl.BlockSpec((None, 8, D), lambda i: (i // tps, 0, 0)), _const_spec((8, D)),
                  _const_spec(w_in[0].shape), _const_spec(w_in[1].shape), _const_spec((4, FB, D))],
        out_specs=[tile, pl.BlockSpec((8, tm, FB), lambda i: (0, i, 0)),
                   pl.BlockSpec((4, tm, FB), lambda i: (0, i, 0)), tile, tile,
                   pl.BlockSpec((None, 8, D), lambda i: (i // tps, 0, 0)), pl.BlockSpec((8, D), lambda i: (0, 0)),
                   pl.BlockSpec((8, D), lambda i: (0, 0))],
        out_shape=[jax.ShapeDtypeStruct((T, D), F32), jax.ShapeDtypeStruct((8, T, FB), BF16),
                   jax.ShapeDtypeStruct((4, T, FB), BF16), jax.ShapeDtypeStruct((T, D), BF16),
                   jax.ShapeDtypeStruct((T, D), BF16), jax.ShapeDtypeStruct((nb, 8, D), F32),
                   jax.ShapeDtypeStruct((8, D), F32), jax.ShapeDtypeStruct((8, D), F32)],
        scratch=[pltpu.VMEM((8, tm, FB), BF16)],
        args=[x, target, mod, gvec, w_in[0], w_in[1], w_out])


def _masked_spatial(ws_ref):
    row = lax.broadcasted_iota(jnp.int32, (CHUNK, CHUNK), 0)
    col = lax.broadcasted_iota(jnp.int32, (CHUNK, CHUNK), 1)
    keep = col <= row
    return [jnp.where(keep, ws_ref[hd], 0.0).astype(BF16) for hd in range(NHEAD)]


def _head_pairs(mats, right, transpose=False):
    first = lax.broadcasted_iota(jnp.int32, (CHUNK, LANES), 1) < HD
    op = _dot_tn if transpose else _dot
    out = []
    for p in range(NHEAD // 2):
        slab = right[:, _lanes(p)]
        out.append(jnp.where(first, op(mats[2 * p], slab), op(mats[2 * p + 1], slab)))
    return jnp.concatenate(out, axis=1)


def _spatial_gate(wm, vb_chunk):
    return _head_pairs(wm, vb_chunk)


def _layer_norm_stats(v):
    mu = _rowmean(v)
    vc = v - mu
    rstd = lax.rsqrt(_rowmean(vc * vc) + EPS)
    return vc * rstd, rstd


def _pitch(tm):
    p = tm // 8
    while p % 8 != 4:
        p += 1
    return p


def _lanes(s):
    return slice(s * LANES, (s + 1) * LANES)


def _to_slabs(ref, row0, val):
    for s in range(NSLAB):
        ref[s, row0:row0 + val.shape[0], :] = val[:, _lanes(s)]


def _tap_sum(src, out, cw_ref, bias, tm, start):
    p = _pitch(tm)
    for s in range(NSLAB):
        accs = [jnp.broadcast_to(bias[:, _lanes(s)], (SUBL, LANES))] * p
        for k in range(CONV_K):
            w = jnp.broadcast_to(cw_ref[k:k + 1, _lanes(s)], (SUBL, LANES))
            for v in range(p):
                accs[v] = accs[v] + w * src[s, pl.ds(v + start(k), 8, stride=p), :]
        for v in range(p):
            out[s, pl.ds(v, 8, stride=p), :] = accs[v]
    return jnp.concatenate([out[s, 0:tm, :] for s in range(NSLAB)], axis=1)


def _mixer_fwd(x, mod, gvec, w_mi, w_mo, v512, ws, bias_full, cw, tm, name, jobs=()):
    T = x.shape[0]
    nt = T // tm
    tps = nt // mod.shape[0]
    ext_rows = 8 * _pitch(tm)

    def core(ins, outs, scs):
        x_ref, mod_ref, g_ref, wmi_ref, wmo_ref, v_ref, ws_ref, bias_ref, cw_ref = ins
        xo_ref, proj_ref, ym_ref, conv_ref = outs
        glu_ext, conv_scr = scs
        i = pl.program_id(0)
        xv = x_ref[...]
        sh, sc, gt = mod_ref[0:1, :], mod_ref[1:2, :], mod_ref[2:3, :]
        r = lax.rsqrt(_rowmean(xv * xv) + EPS)
        hb = ((xv * r * g_ref[0:1, :]) * (1.0 + sc) + sh).astype(BF16)
        for j in range(NDEV):
            proj_ref[:, j * MB:(j + 1) * MB] = _dot(hb, wmi_ref[j])
        u = proj_ref[:, 0:WA]
        v0 = proj_ref[:, WA:2 * WA]
        a = proj_ref[:, 2 * WA:3 * WA]
        g = proj_ref[:, 3 * WA:4 * WA]
        vh, _ = _layer_norm_stats(v0)
        vb = (vh * v_ref[0:1, :] + v_ref[1:2, :]).astype(BF16)
        wm = _masked_spatial(ws_ref)
        ya = []
        for q in range(tm // CHUNK):
            z = _spatial_gate(wm, vb[q * CHUNK:(q + 1) * CHUNK, :]) + bias_ref[...]
            ya.append(u[q * CHUNK:(q + 1) * CHUNK, :] * z)
        ya = jnp.concatenate(ya, axis=0)
        glu = a * _sigmoid(g)

        @pl.when(i == 0)
        def _():
            glu_ext[:, HALO + tm:HALO + ext_rows, :] = jnp.zeros((NSLAB, ext_rows - tm, LANES), F32)

        @pl.when(i % tps == 0)
        def _():
            glu_ext[:, 0:HALO, :] = jnp.zeros((NSLAB, HALO, LANES), F32)

        _to_slabs(glu_ext, HALO, glu)
        conv = _tap_sum(glu_ext, conv_scr, cw_ref, v_ref[2:3, :], tm, lambda k: HALO - (CONV_K - 1) + k)
        conv_ref[...] = conv
        glu_ext[:, 0:HALO, :] = glu_ext[:, tm:tm + HALO, :]
        ch, _ = _layer_norm_stats(conv)
        cn = ch * v_ref[3:4, :] + v_ref[4:5, :]
        yb = cn * _sigmoid(cn)
        pa = ya * lax.rsqrt(_rowmean(ya * ya) + EPS) * v_ref[5:6, :]
        pb = yb * lax.rsqrt(_rowmean(yb * yb) + EPS) * v_ref[6:7, :]
        ycat = jnp.concatenate([pa, pb], axis=1).astype(BF16)
        ym = _dot(ycat, wmo_ref[...])
        ym_ref[...] = ym
        rm = lax.rsqrt(_rowmean(ym * ym) + EPS)
        xo_ref[...] = xv + gt * (ym * rm * g_ref[1:2, :])

    tile = pl.BlockSpec((tm, D), lambda i: (i, 0))
    return _call(
        core, name=name, grid=(nt,), jobs=jobs,
        in_specs=[tile, pl.BlockSpec((None, 8, D), lambda i: (i // tps, 0, 0)), _const_spec((8, D)),
                  _const_spec((NDEV, D, MB)), _const_spec((D, D)), _const_spec((8, WA)),
                  _const_spec((NHEAD, CHUNK, CHUNK)), _const_spec((CHUNK, WA)), _const_spec((32, WA))],
        out_specs=[tile, pl.BlockSpec((tm, 4 * WA), lambda i: (i, 0)), tile, pl.BlockSpec((tm, WA), lambda i: (i, 0))],
        out_shape=[jax.ShapeDtypeStruct((T, D), F32), jax.ShapeDtypeStruct((T, 4 * WA), F32),
                   jax.ShapeDtypeStruct((T, D), F32), jax.ShapeDtypeStruct((T, WA), F32)],
        scratch=[pltpu.VMEM((NSLAB, HALO + ext_rows, LANES), F32), pltpu.VMEM((NSLAB, ext_rows, LANES), F32)],
        args=[x, mod, gvec, w_mi, w_mo, v512, ws, bias_full, cw])


def _mixer_bwd_a(dxo, ym, proj, conv, mod, gvec, w_mo, v512, ws, bias_full, esel, tm, name, jobs=()):
    T = dxo.shape[0]
    nt = T // tm
    nb = mod.shape[0]
    tps = nt // nb

    def core(ins, outs, scs):
        dxo_ref, ym_ref, proj_ref, conv_ref, mod_ref, g_ref, wmo_ref, v_ref, ws_ref, bias_ref, e_ref = ins
        dpart_ref, dymb_ref, ycat_ref, mg_ref, vg_ref, v5g_ref, gws_ref, gbs_ref = outs
        (dbs_acc,) = scs
        i = pl.program_id(0)
        dxo_v = dxo_ref[...]
        ymv = ym_ref[...]
        gt = mod_ref[2:3, :]
        gpost = g_ref[1:2, :]
        rm = lax.rsqrt(_rowmean(ymv * ymv) + EPS)
        ymh = ymv * rm
        d_gt = _colsum(dxo_v * (ymh * gpost))
        dpm = gt * dxo_v
        d_gpost = _colsum(dpm * ymh)
        dymh = dpm * gpost
        dym = (rm * (dymh - ymh * _rowmean(dymh * ymh))).astype(BF16)
        dymb_ref[...] = dym
        dycat = _dot_nt(dym, wmo_ref[...])
        u = proj_ref[:, 0:WA]
        v0 = proj_ref[:, WA:2 * WA]
        vh, rv = _layer_norm_stats(v0)
        vb = (vh * v_ref[0:1, :] + v_ref[1:2, :]).astype(BF16)
        wm = _masked_spatial(ws_ref)
        zs = []
        for q in range(tm // CHUNK):
            zs.append(_spatial_gate(wm, vb[q * CHUNK:(q + 1) * CHUNK, :]) + bias_ref[...])
        z = jnp.concatenate(zs, axis=0)
        ya = u * z
        ra = lax.rsqrt(_rowmean(ya * ya) + EPS)
        yah = ya * ra
        ch, rc = _layer_norm_stats(conv_ref[...])
        cn = ch * v_ref[3:4, :] + v_ref[4:5, :]
        sg = _sigmoid(cn)
        yb = cn * sg
        rb = lax.rsqrt(_rowmean(yb * yb) + EPS)
        ybh = yb * rb
        ycat_ref[...] = jnp.concatenate([yah * v_ref[5:6, :], ybh * v_ref[6:7, :]], axis=1).astype(BF16)
        dpa = dycat[:, 0:WA]
        dpb = dycat[:, WA:2 * WA]
        d_goa = _colsum(dpa * yah)
        d_gob = _colsum(dpb * ybh)
        dyah = dpa * v_ref[5:6, :]
        dybh = dpb * v_ref[6:7, :]
        dya = ra * (dyah - yah * _rowmean(dyah * yah))
        dyb = rb * (dybh - ybh * _rowmean(dybh * ybh))
        dpart_ref[:, 0:WA] = dya * z
        dz = dya * u

        @pl.when(i == 0)
        def _():
            gws_ref[...] = jnp.zeros((NHEAD, CHUNK, CHUNK), F32)
            dbs_acc[...] = jnp.zeros((CHUNK, WA), F32)
            vg_ref[...] = jnp.zeros((8, D), F32)
            v5g_ref[...] = jnp.zeros((8, WA), F32)

        first = lax.broadcasted_iota(jnp.int32, (CHUNK, LANES), 1) < HD
        dvs = []
        for q in range(tm // CHUNK):
            dz_q = dz[q * CHUNK:(q + 1) * CHUNK, :]
            vb_q = vb[q * CHUNK:(q + 1) * CHUNK, :]
            dbs_acc[...] += dz_q
            dzb = dz_q.astype(BF16)
            dvs.append(_head_pairs(wm, dzb, transpose=True))
            for hd in range(NHEAD):
                slab = dzb[:, _lanes(hd // 2)]
                dz_hd = jnp.where(first if hd % 2 == 0 else jnp.logical_not(first), slab, jnp.zeros_like(slab))
                gws_ref[hd] += _dot_nt(dz_hd, vb_q[:, _lanes(hd // 2)])
        dv = jnp.concatenate(dvs, axis=0)
        d_gng = _colsum(dv * vh)
        d_gnb = _colsum(dv)
        dvh = dv * v_ref[0:1, :]
        dpart_ref[:, WA:2 * WA] = rv * (dvh - _rowmean(dvh) - vh * _rowmean(dvh * vh))
        dcn = dyb * (sg * (1.0 + cn * (1.0 - sg)))
        d_cng = _colsum(dcn * ch)
        d_cnb = _colsum(dcn)
        dch = dcn * v_ref[3:4, :]
        dconv = rc * (dch - _rowmean(dch) - ch * _rowmean(dch * ch))
        dpart_ref[:, 2 * WA:3 * WA] = dconv
        dpart_ref[:, 3 * WA:4 * WA] = jnp.zeros((tm, WA), F32)
        d_cb = _colsum(dconv)

        @pl.when(i % tps == 0)
        def _():
            mg_ref[...] = jnp.zeros((8, D), F32)

        mg_ref[2:3, :] += d_gt
        vg_ref[1:2, :] += d_gpost
        v5g_ref[0:1, :] += d_gng
        v5g_ref[1:2, :] += d_gnb
        v5g_ref[2:3, :] += d_cb
        v5g_ref[3:4, :] += d_cng
        v5g_ref[4:5, :] += d_cnb
        v5g_ref[5:6, :] += d_goa
        v5g_ref[6:7, :] += d_gob

        @pl.when(i == nt - 1)
        def _():
            row = lax.broadcasted_iota(jnp.int32, (CHUNK, CHUNK), 0)
            col = lax.broadcasted_iota(jnp.int32, (CHUNK, CHUNK), 1)
            for hd in range(NHEAD):
                gws_ref[hd] = jnp.where(col <= row, gws_ref[hd], 0.0)
            gbs_ref[...] = lax.dot_general(e_ref[...], dbs_acc[...], (((1,), (1,)), ((), ())),
                                           precision=lax.Precision.HIGHEST, preferred_element_type=F32)

    tile = pl.BlockSpec((tm, D), lambda i: (i, 0))
    ptile = pl.BlockSpec((tm, 4 * WA), lambda i: (i, 0))
    return _call(
        core, name=name, grid=(nt,), jobs=jobs,
        in_specs=[tile, tile, pl.BlockSpec((tm, 2 * WA), lambda i: (i, 0)), pl.BlockSpec((tm, WA), lambda i: (i, 0)),
                  pl.BlockSpec((None, 8, D), lambda i: (i // tps, 0, 0)), _const_spec((8, D)), _const_spec((D, D)),
                  _const_spec((8, WA)), _const_spec((NHEAD, CHUNK, CHUNK)), _const_spec((CHUNK, WA)),
                  _const_spec((8, WA))],
        out_specs=[ptile, tile, tile, pl.BlockSpec((None, 8, D), lambda i: (i // tps, 0, 0)),
                   pl.BlockSpec((8, D), lambda i: (0, 0)), pl.BlockSpec((8, WA), lambda i: (0, 0)),
                   pl.BlockSpec((NHEAD, CHUNK, CHUNK), lambda i: (0, 0, 0)), pl.BlockSpec((8, CHUNK), lambda i: (0, 0))],
        out_shape=[jax.ShapeDtypeStruct((T, 4 * WA), F32), jax.ShapeDtypeStruct((T, D), BF16),
                   jax.ShapeDtypeStruct((T, D), BF16), jax.ShapeDtypeStruct((nb, 8, D), F32),
                   jax.ShapeDtypeStruct((8, D), F32), jax.ShapeDtypeStruct((8, WA), F32),
                   jax.ShapeDtypeStruct((NHEAD, CHUNK, CHUNK), F32), jax.ShapeDtypeStruct((8, CHUNK), F32)],
        scratch=[pltpu.VMEM((CHUNK, WA), F32)],
        args=[dxo, ym, proj, conv, mod, gvec, w_mo, v512, ws, bias_full, esel])


def _mixer_bwd_b(dxo, x, dpart, proj, mod, gvec, w_mi, cw, tm, name, jobs=()):
    T = x.shape[0]
    nt = T // tm
    nb = mod.shape[0]
    tps = nt // nb
    hpt = tm // HALO
    nh = T // HALO
    off = HALO - (CONV_K - 1)
    p = _pitch(tm)
    ext_rows = 8 * p

    def core(ins, outs, scs):
        dxo_ref, x_ref, dpart_ref, dnext_ref, ag_ref, halo_ref, mod_ref, g_ref, wmi_ref, cw_ref = ins
        dx_ref, dproj_ref, hb_ref, mg_ref, vg_ref, dcw_ref = outs
        glu_ext, dconv_ext, dglu_scr, dcw_acc = scs
        i = pl.program_id(0)
        first = i % tps == 0
        last = i % tps == tps - 1
        a = ag_ref[:, 0:WA]
        g = ag_ref[:, WA:2 * WA]
        sgg = _sigmoid(g)

        @pl.when(i == 0)
        def _():
            glu_ext[:, HALO + tm:HALO + ext_rows, :] = jnp.zeros((NSLAB, ext_rows - tm, LANES), F32)
            dconv_ext[:, HALO + tm:HALO + ext_rows, :] = jnp.zeros((NSLAB, ext_rows - tm, LANES), F32)
            dcw_acc[...] = jnp.zeros((32, 8, WA), F32)
            vg_ref[...] = jnp.zeros((8, D), F32)

        _to_slabs(glu_ext, 0, jnp.where(first, 0.0, halo_ref[:, 0:WA] * _sigmoid(halo_ref[:, WA:2 * WA])))
        _to_slabs(glu_ext, HALO, a * sgg)
        _to_slabs(dconv_ext, 0, dpart_ref[:, 2 * WA:3 * WA])
        _to_slabs(dconv_ext, tm, jnp.where(last, 0.0, dnext_ref[...]))
        sub = lax.broadcasted_iota(jnp.int32, (SUBL, LANES), 0)
        for s in range(NSLAB):
            accs = [jnp.zeros((SUBL, LANES), F32)] * CONV_K
            for v in range(p):
                dc = jnp.where(v + p * sub < tm, dconv_ext[s, pl.ds(v, 8, stride=p), :], 0.0)
                for k in range(CONV_K):
                    accs[k] = accs[k] + dc * glu_ext[s, pl.ds(v + off + k, 8, stride=p), :]
            for k in range(CONV_K):
                dcw_acc[k, :, _lanes(s)] += accs[k]
        dglu = _tap_sum(dconv_ext, dglu_scr, cw_ref, jnp.zeros((1, WA), F32), tm, lambda k: (CONV_K - 1) - k)

        @pl.when(i == nt - 1)
        def _():
            for k in range(CONV_K):
                dcw_ref[k:k + 1, :] = jnp.sum(dcw_acc[k], axis=0, keepdims=True)
            dcw_ref[CONV_K:32, :] = jnp.zeros((32 - CONV_K, WA), F32)

        da = dglu * sgg
        dgg = dglu * a * (sgg * (1.0 - sgg))
        dproj_ref[:, 0:2 * WA] = dpart_ref[:, 0:2 * WA].astype(BF16)
        dproj_ref[:, 2 * WA:3 * WA] = da.astype(BF16)
        dproj_ref[:, 3 * WA:4 * WA] = dgg.astype(BF16)
        dh = jnp.zeros((tm, D), F32)
        for j in range(NDEV):
            dh = dh + _dot_nt(dproj_ref[:, j * MB:(j + 1) * MB], wmi_ref[j])
        xv = x_ref[...]
        sc, sh = mod_ref[1:2, :], mod_ref[0:1, :]
        gpre = g_ref[0:1, :]
        r = lax.rsqrt(_rowmean(xv * xv) + EPS)
        xh = xv * r
        n = xh * gpre
        hb_ref[...] = (n * (1.0 + sc) + sh).astype(BF16)
        d_sc = _colsum(dh * n)
        d_sh = _colsum(dh)
        dn = dh * (1.0 + sc)
        d_gpre = _colsum(dn * xh)
        dxh = dn * gpre
        dx_ref[...] = dxo_ref[...] + r * (dxh - xh * _rowmean(dxh * xh))

        @pl.when(first)
        def _():
            mg_ref[...] = jnp.zeros((8, D), F32)

        mg_ref[0:1, :] += d_sh
        mg_ref[1:2, :] += d_sc
        vg_ref[0:1, :] += d_gpre

    tile = pl.BlockSpec((tm, D), lambda i: (i, 0))
    return _call(
        core, name=name, grid=(nt,), jobs=jobs,
        in_specs=[tile, tile, pl.BlockSpec((tm, 4 * WA), lambda i: (i, 0)),
                  pl.BlockSpec((HALO, WA), lambda i: (jnp.minimum((i + 1) * hpt, nh - 1), 2)),
                  pl.BlockSpec((tm, 2 * WA), lambda i: (i, 1)),
                  pl.BlockSpec((HALO, 2 * WA), lambda i: (jnp.maximum(i * hpt - 1, 0), 1)),
                  pl.BlockSpec((None, 8, D), lambda i: (i // tps, 0, 0)), _const_spec((8, D)),
                  _const_spec((NDEV, D, MB)), _const_spec((32, WA))],
        out_specs=[tile, pl.BlockSpec((tm, 4 * WA), lambda i: (i, 0)), tile,
                   pl.BlockSpec((None, 8, D), lambda i: (i // tps, 0, 0)), pl.BlockSpec((8, D), lambda i: (0, 0)),
                   pl.BlockSpec((32, WA), lambda i: (0, 0))],
        out_shape=[jax.ShapeDtypeStruct((T, D), F32), jax.ShapeDtypeStruct((T, 4 * WA), BF16),
                   jax.ShapeDtypeStruct((T, D), BF16), jax.ShapeDtypeStruct((nb, 8, D), F32),
                   jax.ShapeDtypeStruct((8, D), F32), jax.ShapeDtypeStruct((32, WA), F32)],
        scratch=[pltpu.VMEM((NSLAB, HALO + ext_rows, LANES), F32), pltpu.VMEM((NSLAB, HALO + ext_rows, LANES), F32),
                 pltpu.VMEM((NSLAB, ext_rows, LANES), F32), pltpu.VMEM((32, 8, WA), F32)],
        args=[dxo, x, dpart, dpart, proj, proj, mod, gvec, w_mi, cw])


def _grad_chip(a, b, a_spec, b_spec, prod_shape, half, name, jobs=(), via_b=False, after=None):
    steps = 8 if half is None else 4
    R = prod_shape[0] if half is None else half
    C = prod_shape[1]

    def core(ins, outs, scs):
        a_ref, b_ref = ins[:2]
        (o_ref,) = outs
        own, snd, rcv, ssem, rsem, lsem = scs
        s = pl.program_id(0)
        c = lax.axis_index("c")
        me = _me()
        sib = _flip(me, (0, 0, 1))
        if via_b:
            prod = _dot_tn(b_ref[...], a_ref[...]).T.astype(BF16)
        else:
            prod = _dot_tn(a_ref[...], b_ref[...]).astype(BF16)
        if half is None:
            q = s // 2

            @pl.when(s % 2 == c)
            def _():
                own[q] = prod

            @pl.when(s % 2 != c)
            def _():
                snd[q] = prod
                _remote(snd.at[q], rcv.at[q], ssem.at[q], rsem.at[q], sib).start()
        else:
            lo = prod[0:half, :]
            hi = prod[half:2 * half, :]
            own[s] = jnp.where(c == 0, lo, hi)
            snd[s] = jnp.where(c == 0, hi, lo)
            _remote(snd.at[s], rcv.at[s], ssem.at[s], rsem.at[s], sib).start()

        @pl.when(s == steps - 1)
        def _():
            for q4 in range(4):
                cp = _remote(snd.at[q4], rcv.at[q4], ssem.at[q4], rsem.at[q4], sib)
                cp.wait_recv()
                cp.wait_send()
                snd[q4] = (own[q4].astype(F32) + rcv[q4].astype(F32)).astype(BF16)
            out = pltpu.make_async_copy(snd, o_ref, lsem)
            out.start()
            out.wait()

    return _call(
        core, name=name, grid=(steps,), jobs=jobs, in_specs=[a_spec, b_spec] + [HBM] * (after is not None),
        out_specs=[HBM], out_shape=[jax.ShapeDtypeStruct((4, R, C), BF16)],
        scratch=[pltpu.VMEM((4, R, C), BF16), pltpu.VMEM((4, R, C), BF16), pltpu.VMEM((4, R, C), BF16),
                 pltpu.SemaphoreType.DMA((4,)), pltpu.SemaphoreType.DMA((4,)), pltpu.SemaphoreType.DMA],
        args=[a, b] + [after] * (after is not None))


def _grad_w_in(dg, hb, name, jobs=()):
    T = hb.shape[0]
    return _grad_chip(dg, hb, pl.BlockSpec((None, T, FB), lambda s: (s, 0, 0)), _const_spec((T, D)),
                      (FB, D), None, name, jobs)


def _grad_w_out(act, dyb, name, jobs=(), after=None):
    T = dyb.shape[0]
    return _grad_chip(act, dyb, pl.BlockSpec((None, T, FB), lambda s: (s, 0, 0)), _const_spec((T, D)),
                      (FB, D), FO, name, jobs, after=after)


def _grad_w_mi(hb, dproj, name, jobs=()):
    T = hb.shape[0]
    return _grad_chip(hb, dproj, _const_spec((T, D)), pl.BlockSpec((T, MB), lambda s: (0, s)),
                      (D, MB), None, name, jobs, via_b=True)


def _grad_w_mo(ycat, dym, name, jobs=()):
    T = ycat.shape[0]
    return _grad_chip(ycat, dym, pl.BlockSpec((T, 2 * MO), lambda s: (0, s)), _const_spec((T, D)),
                      (2 * MO, D), MO, name, jobs)


def _adamw_math(w, g, m, v):
    m2 = ADAM_B1 * m + (1.0 - ADAM_B1) * g
    v2 = ADAM_B2 * v + (1.0 - ADAM_B2) * (g * g)
    m_hat = m2 / (1.0 - ADAM_B1 ** ADAM_STEP)
    v_hat = v2 / (1.0 - ADAM_B2 ** ADAM_STEP)
    delta = -ADAM_LR * (m_hat / (jnp.sqrt(v_hat) + ADAM_EPS) + ADAM_WD * w)
    return delta, m2, v2


def _adamw_reduce(parts, w, m, v, tr, name, own=None, after=None):
    R, C = w.shape

    def core(ins, outs, _):
        p_ref, w_ref, m_ref, v_ref = ins[:4]
        g_ref, d_ref, m2_ref, v2_ref = outs
        if own is None:
            terms = [p_ref[s].astype(F32) for s in range(4)]
        else:
            mq = 2 * lax.axis_index("x") + lax.axis_index("y")
            mine = ins[4][...].astype(F32)
            terms = [jnp.where(mq == s, mine, p_ref[s].astype(F32)) for s in range(4)]
        g = terms[0]
        for s in range(1, 4):
            g = g + terms[s]
        g_ref[...] = g
        d_ref[...], m2_ref[...], v2_ref[...] = _adamw_math(w_ref[...], g, m_ref[...], v_ref[...])

    blk = pl.BlockSpec((tr, C), lambda i: (i, 0))
    in_specs = [pl.BlockSpec((4, tr, C), lambda i: (0, i, 0)), blk, blk, blk]
    args = [parts, w, m, v]
    if own is not None:
        mq = 2 * lax.axis_index("x") + lax.axis_index("y")
        in_specs.append(pl.BlockSpec((tr, C), lambda i: (i, 0)))
        args.append(lax.dynamic_index_in_dim(own, mq, 0, keepdims=False))
    if after is not None:
        in_specs.append(HBM)
        args.append(after)
    return _call(
        core, name=name, grid=(R // tr,), in_specs=in_specs,
        out_specs=[blk, blk, blk, blk], out_shape=[jax.ShapeDtypeStruct((R, C), F32)] * 4, args=args)[0]


HBM_ONLY = pl.BlockSpec(memory_space=pltpu.HBM)
SEM = pl.BlockSpec(memory_space=pltpu.SEMAPHORE)
EFFECT = pltpu.SideEffectType.DATAFLOW_SIDE_EFFECTING


def _chip_scatter_start(gs, name):
    n = len(gs)

    def body(*refs):
        g_refs, land_refs = refs[:n], refs[n:2 * n]
        ssem, rsem = refs[2 * n:2 * n + 2]
        token = refs[-1]
        me = _me()
        mq = 2 * me[0] + me[1]
        for k, f in enumerate(CHIP_FLIPS):
            p = _flip(me, f)
            for a in range(n):
                _remote(g_refs[a].at[2 * p[0] + p[1]], land_refs[a].at[mq], ssem.at[3 * a + k], rsem.at[3 * a + k], p).start()
        token[...] = jnp.zeros_like(token)

    gs = [pltpu.with_memory_space_constraint(g, pltpu.HBM) for g in gs]
    lands = [pltpu.with_memory_space_constraint(lax.empty(g.shape, g.dtype), pltpu.HBM) for g in gs]
    res = pl.pallas_call(
        body, name=name,
        out_shape=(pltpu.SemaphoreType.DMA((3 * n,)), pltpu.SemaphoreType.DMA((3 * n,)))
        + tuple(pltpu.HBM(g.shape, g.dtype) for g in gs) * 2 + (jax.ShapeDtypeStruct((SUBL, LANES), F32),),
        in_specs=(HBM_ONLY,) * (2 * n), out_specs=(SEM, SEM) + (HBM_ONLY,) * (2 * n) + (VM,),
        input_output_aliases={a: 2 + a for a in range(2 * n)},
        compiler_params=pltpu.CompilerParams(has_side_effects=EFFECT),
    )(*gs, *lands)
    return res[:-1], res[-1]


def _chip_scatter_wait(handle, after, name):
    ssem, rsem = handle[:2]
    n = (len(handle) - 2) // 2
    thru = handle[2:]

    def body(*refs):
        g_refs, land_refs = refs[:n], refs[n:2 * n]
        ssem, rsem = refs[2 * n:2 * n + 2]
        me = _me()
        mq = 2 * me[0] + me[1]
        for k, f in enumerate(CHIP_FLIPS):
            p = _flip(me, f)
            pq = 2 * p[0] + p[1]
            for a in range(n):
                _remote(g_refs[a].at[pq], land_refs[a].at[mq], ssem.at[3 * a + k], rsem.at[3 * a + k], p).wait_send()
                _remote(g_refs[a].at[mq], land_refs[a].at[pq], ssem.at[3 * a + k], rsem.at[3 * a + k], p).wait_recv()

    res = pl.pallas_call(
        body, name=name,
        out_shape=tuple(pltpu.HBM(t.shape, t.dtype) for t in thru),
        in_specs=(HBM_ONLY,) * (2 * n) + (SEM, SEM, HBM), out_specs=(HBM_ONLY,) * (2 * n),
        input_output_aliases={a: a for a in range(2 * n)},
        compiler_params=pltpu.CompilerParams(has_side_effects=EFFECT),
    )(*thru, ssem, rsem, after)
    return list(res[:n]), list(res[n:])


def _adamw_ada(sc_all, dd, w, m, v, tr, name, after=None):
    R, C = w.shape

    def core(ins, outs, _):
        sc_ref, dd_ref, w_ref, m_ref, v_ref = ins[:5]
        g_ref, d_ref, m2_ref, v2_ref = outs
        g = _dot_tn(sc_ref[...].astype(BF16), dd_ref[...].astype(BF16))
        g_ref[...] = g
        d_ref[...], m2_ref[...], v2_ref[...] = _adamw_math(w_ref[...], g, m_ref[...], v_ref[...])

    blk = pl.BlockSpec((tr, C), lambda i: (i, 0))
    return _call(
        core, name=name, grid=(R // tr,),
        in_specs=[pl.BlockSpec((64, tr), lambda i: (0, i)), pl.BlockSpec((64, C), lambda i: (0, 0)), blk, blk, blk]
        + [HBM] * (after is not None),
        out_specs=[blk, blk, blk, blk], out_shape=[jax.ShapeDtypeStruct((R, C), F32)] * 4,
        args=[sc_all, dd, w, m, v] + [after] * (after is not None))[0]


def _adamw_small(gathered, plain, grads, wmv, emit, name, after=None):
    nw = len(grads)
    ng, npl, ne = len(gathered), len(plain), len(emit)

    def core(ins, outs, _):
        srcs = []
        for a in range(ng):
            s = ins[a][0]
            for dev in range(1, NDEV):
                s = s + ins[a][dev]
            srcs.append(s)
        srcs += [ins[ng + a][...] for a in range(npl)]
        w_refs = ins[ng + npl:]
        for e, a in enumerate(emit):
            outs[e][...] = srcs[a]
        for t in range(nw):
            src, row = grads[t]
            g = srcs[src] if row is None else srcs[src][row:row + 1, :]
            w_ref, m_ref, v_ref = w_refs[3 * t:3 * t + 3]
            g_ref, d_ref, m2_ref, v2_ref = outs[ne + 4 * t:ne + 4 * t + 4]
            g_ref[...] = g
            d_ref[...], m2_ref[...], v2_ref[...] = _adamw_math(w_ref[...], g, m_ref[...], v_ref[...])

    out_shape = [jax.ShapeDtypeStruct(gathered[a].shape[1:], F32) for a in emit]
    for t in range(nw):
        out_shape += [jax.ShapeDtypeStruct(wmv[3 * t].shape, F32)] * 4
    return _call(
        core, name=name, grid=(), in_specs=[VM] * (ng + npl + 3 * nw) + [HBM] * (after is not None),
        out_specs=[VM] * (ne + 4 * nw), out_shape=out_shape,
        args=list(gathered) + list(plain) + list(wmv) + [after] * (after is not None))[0]


def _ada_fwd(c_pad, w_ada, b_cols, cw_pad, jobs=()):
    def core(ins, outs, scs, start_jobs, finish_jobs):
        c_ref, w_ref, b_ref, cwp_ref = ins
        ada_ref, sc_ref, cw_ref = outs
        cbuf, send_buf, ssem, rsem = scs
        me = _me()
        mi = _lin(me)
        cbuf[mi] = c_ref[...]
        cw_ref[mi] = cwp_ref[...]
        peers = [_flip(me, f) for f in FLIPS]
        first = []
        for k, p in enumerate(peers):
            first.append(_remote(cbuf.at[mi], cbuf.at[mi], ssem.at[k], rsem.at[k], p))
            first.append(_remote(cw_ref.at[mi], cw_ref.at[mi], ssem.at[7 + k], rsem.at[7 + k], p))
        for cp in first:
            cp.start()
        start_jobs()
        for k, p in enumerate(peers):
            pi = _lin(p)
            _remote(cbuf.at[pi], cbuf.at[pi], ssem.at[k], rsem.at[k], p).wait_recv()
            _remote(cw_ref.at[pi], cw_ref.at[pi], ssem.at[7 + k], rsem.at[7 + k], p).wait_recv()
        c_all = cbuf[...].reshape(8 * 8, D)
        sc = c_all * _sigmoid(c_all)
        sc_ref[...] = sc
        res = _dot(sc.astype(BF16), w_ref[...].astype(BF16)) + b_ref[...]
        send_buf[...] = res.reshape(8, 8, ADA_B)
        ada_ref[mi] = send_buf[mi]
        second = []
        for k, p in enumerate(peers):
            second.append(_remote(send_buf.at[_lin(p)], ada_ref.at[mi], ssem.at[14 + k], rsem.at[14 + k], p))
        for cp in second:
            cp.start()
        finish_jobs()
        for k, p in enumerate(peers):
            _remote(send_buf.at[mi], ada_ref.at[_lin(p)], ssem.at[14 + k], rsem.at[14 + k], p).wait_recv()
        for cp in first + second:
            cp.wait_send()

    return _call(
        core, name="ada_fwd", grid=(), jobs=jobs, core_starts=True, in_specs=[VM, VM, VM, VM], out_specs=[VM, VM, VM],
        out_shape=[jax.ShapeDtypeStruct((8, 8, ADA_B), F32), jax.ShapeDtypeStruct((64, D), F32),
                   jax.ShapeDtypeStruct((8, 32, 64), F32)],
        scratch=[pltpu.VMEM((8, 8, D), F32), pltpu.VMEM((8, 8, ADA_B), F32),
                 pltpu.SemaphoreType.DMA((21,)), pltpu.SemaphoreType.DMA((21,))],
        args=[c_pad, w_ada, b_cols, cw_pad])


def _ada_bwd(dada, jobs=()):
    def core(ins, outs, scs):
        (d_ref,) = ins
        dd_ref, gb_ref = outs
        rbuf, ssem, rsem = scs
        me = _me()
        mi = _lin(me)
        peers = [_flip(me, f) for f in FLIPS]
        rbuf[mi] = d_ref[mi]
        first = []
        for k, p in enumerate(peers):
            first.append(_remote(d_ref.at[_lin(p)], rbuf.at[mi], ssem.at[k], rsem.at[k], p))
        for cp in first:
            cp.start()
        for k, p in enumerate(peers):
            _remote(d_ref.at[mi], rbuf.at[_lin(p)], ssem.at[k], rsem.at[k], p).wait_recv()
        dd = rbuf[...].reshape(64, ADA_B)
        dd_ref[...] = dd
        gb_ref[...] = jnp.broadcast_to(_colsum(dd), (8, ADA_B))
        for cp in first:
            cp.wait_send()

    return _call(
        core, name="ada_bwd", grid=(), jobs=jobs, in_specs=[VM], out_specs=[VM, VM],
        out_shape=[jax.ShapeDtypeStruct((64, ADA_B), F32), jax.ShapeDtypeStruct((8, ADA_B), F32)],
        scratch=[pltpu.VMEM((8, 8, ADA_B), F32), pltpu.SemaphoreType.DMA((7,)), pltpu.SemaphoreType.DMA((7,))],
        args=[dada])


SMALL_D = ("g_pre_f1", "g_post_f1", "g_pre_m", "g_post_m", "g_pre_f2", "g_post_f2")
SMALL_W = ("gmlp_norm_g", "gmlp_norm_b", "conv_b", "conv_norm_g", "conv_norm_b", "g_out_a", "g_out_b")


def kernel(x, c, w_ada, b_ada, g_pre_f1, g_post_f1, w_f1_in, w_f1_out, g_pre_m, g_post_m, w_mix_in, gmlp_norm_g, gmlp_norm_b, w_spatial, b_spatial, conv_w, conv_b, conv_norm_g, conv_norm_b, g_out_a, g_out_b, w_mix_out, g_pre_f2, g_post_f2, w_f2_in, w_f2_out, loss_target, m_w_ada, m_b_ada, m_g_pre_f1, m_g_post_f1, m_w_f1_in, m_w_f1_out, m_g_pre_m, m_g_post_m, m_w_mix_in, m_gmlp_norm_g, m_gmlp_norm_b, m_w_spatial, m_b_spatial, m_conv_w, m_conv_b, m_conv_norm_g, m_conv_norm_b, m_g_out_a, m_g_out_b, m_w_mix_out, m_g_pre_f2, m_g_post_f2, m_w_f2_in, m_w_f2_out, v_w_ada, v_b_ada, v_g_pre_f1, v_g_post_f1, v_w_f1_in, v_w_f1_out, v_g_pre_m, v_g_post_m, v_w_mix_in, v_gmlp_norm_g, v_gmlp_norm_b, v_w_spatial, v_b_spatial, v_conv_w, v_conv_b, v_conv_norm_g, v_conv_norm_b, v_g_out_a, v_g_out_b, v_w_mix_out, v_g_pre_f2, v_g_post_f2, v_w_f2_in, v_w_f2_out):
    given = dict(locals())
    bl, seq, _ = x.shape
    T = bl * seq
    tm = min(256, seq // 2)
    mi = _lin((lax.axis_index("x"), lax.axis_index("y"), lax.axis_index("c")))

    def shard_in(w):
        return w[0].T.astype(BF16)

    g_f1 = _RelayGather([shard_in(w_f1_in), w_f1_out[0].astype(BF16)], ("rows", "out"))
    s_f2 = shard_in(w_f2_in)
    g_mx = _Gather([w_mix_in[0].astype(BF16), w_mix_out[0].astype(BF16), w_f2_out[0].astype(BF16), s_f2[:, 0:D // 4]],
                   ("rows", "rows", "out", "rows"), late_mid=True)
    g_f2 = _Gather([s_f2[:, D // 4:D]], ("rows",))

    c_pad = jnp.pad(c, ((0, 8 - bl), (0, 0)))
    b_cols = lax.dynamic_slice(b_ada, (0, mi * ADA_B), (1, ADA_B))
    cw_pad = jnp.pad(conv_w[0], ((0, 1), (0, 0)))
    (ada_blk, sc_all, cw_all), ((wi1, wo1),) = _ada_fwd(c_pad, w_ada[0], b_cols, cw_pad, jobs=[g_f1])
    ada = ada_blk[:, 0:bl, :].transpose(1, 0, 2).reshape(bl, 9, D)
    pad5 = jnp.zeros((bl, 5, D), F32)
    mod1 = jnp.concatenate([ada[:, 0:3], pad5], axis=1)
    mod2 = jnp.concatenate([ada[:, 3:6], pad5], axis=1)
    mod3 = jnp.concatenate([ada[:, 6:9], pad5], axis=1)
    cw_full = cw_all.transpose(1, 0, 2).reshape(32, WA)

    zrow = jnp.zeros((1, D), F32)
    gv1 = jnp.concatenate([g_pre_f1, g_post_f1] + [zrow] * 6, axis=0)
    gvm = jnp.concatenate([g_pre_m, g_post_m] + [zrow] * 6, axis=0)
    gv2 = jnp.concatenate([g_pre_f2, g_post_f2] + [zrow] * 6, axis=0)
    v512 = jnp.concatenate([gmlp_norm_g, gmlp_norm_b, conv_b, conv_norm_g, conv_norm_b, g_out_a, g_out_b,
                            jnp.zeros((1, WA), F32)], axis=0)
    ws = w_spatial[0]
    bias_full = jnp.repeat(b_spatial[0].T, HD, axis=1)
    esel = (lax.broadcasted_iota(jnp.int32, (8, WA), 1) // HD == lax.broadcasted_iota(jnp.int32, (8, WA), 0)).astype(F32)

    x0 = x.reshape(T, D)
    (x1, gu1, y1), ((wmi, wmo, wo2, wi2a),) = _ffn_fwd(x0, mod1, gv1, wi1, wo1, tm, "ffn1_fwd", jobs=[g_mx])
    wmo = wmo.reshape(D, D)
    (x2, proj, ym, conv), ((wi2b,),) = _mixer_fwd(x1, mod2, gvm, wmi, wmo, v512, ws, bias_full, cw_full, tm, "mixer_fwd", jobs=[g_f2])

    (dx2, dg2, act2, hb2, dyb2, mg3, vg3, loss_blk), _ = _ffn_last(
        x2, loss_target.reshape(T, D), mod3, gv2, (wi2a, wi2b), wo2, tm, "ffn2_fwd_bwd")
    (g_wi2,), _ = _grad_w_in(dg2, hb2, "ffn2_gw_in")
    (g_wo2,), _ = _grad_w_out(act2, dyb2, "ffn2_gw_out")
    (dpart, dymb, ycat, mg2a, vgma, v5g, gws, gbs), ((p_wo2,),) = _mixer_bwd_a(
        dx2, ym, proj, conv, mod2, gvm, wmo, v512, ws, bias_full, esel, tm, "mixer_bwd_a",
        jobs=[_ChipScatter([g_wo2])])
    (dx1, dproj, hbm, mg2b, vgmb, dcw), ((p_wi2,),) = _mixer_bwd_b(
        dx2, x1, dpart, proj, mod2, gvm, wmi, cw_full, tm, "mixer_bwd_b", jobs=[_ChipScatter([g_wi2])])
    (g_wmi,), _ = _grad_w_mi(hbm, dproj, "mixer_gw_in")
    (g_wmo,), _ = _grad_w_mo(ycat, dymb, "mixer_gw_out")
    p2 = jnp.concatenate([v5g, dcw], axis=0)
    (dx0, dg1, act1, hb1, dyb1, mg1, vg1), ((p_wmi, p_wmo),) = _ffn_bwd(
        dx1, x0, y1, gu1, mod1, gv1, wi1, wo1, tm, "ffn1_bwd", jobs=[_ChipScatter([g_wmi, g_wmo])])

    dada = jnp.concatenate([mg1[:, 0:3], mg2b[:, 0:2], mg2a[:, 2:3], mg3[:, 0:3]], axis=1)
    dada = dada.reshape(bl, NDEV, ADA_B).transpose(1, 0, 2)
    dada = jnp.pad(dada, ((0, 0), (0, 8 - bl), (0, 0)))
    p1 = jnp.concatenate([vg1[0:2], vgmb[0:1], vgma[1:2], vg3[0:2], loss_blk[0:1], zrow], axis=0)
    (dd_all, gb_own), ((a1,),) = _ada_bwd(dada, jobs=[_AllGather([p1])])

    (g_wi1,), ((a2, a3, a4, gb_all),) = _grad_w_in(
        dg1, hb1, "ffn1_gw_in", jobs=[_Gather([p2, gws, gbs, gb_own], ("rows",) * 4)])
    g_bada = gb_all[:, 0, :].reshape(1, 9 * D)

    h_i1, token = _chip_scatter_start([g_wi1], "tail_start")
    (g_wo1,), _ = _grad_w_out(act1, dyb1, "ffn1_gw_out", after=token)
    h_o1, token = _chip_scatter_start([g_wo1], "tail2_start")

    res = {}
    quad = _adamw_reduce(p_wi2, w_f2_in[0].T, m_w_f2_in[0].T, v_w_f2_in[0].T, FO, "adamw_w_f2_in", after=token)
    res["w_f2_in"] = tuple(t.T[None] for t in quad)
    for nm, part, tr in (("w_f2_out", p_wo2, FO), ("w_mix_in", p_wmi, 256), ("w_mix_out", p_wmo, MO)):
        quad = _adamw_reduce(part, given[nm][0], given["m_" + nm][0], given["v_" + nm][0], tr, "adamw_" + nm, after=quad[1])
        res[nm] = tuple(t[None] for t in quad)
    quad = _adamw_ada(sc_all, dd_all, w_ada[0], m_w_ada[0], v_w_ada[0], 256, "adamw_w_ada", after=quad[1])
    res["w_ada"] = tuple(t[None] for t in quad)

    small = SMALL_D + SMALL_W + ("w_spatial", "b_spatial", "b_ada")
    grads = [(0, r) for r in range(6)] + [(1, r) for r in range(7)] + [(2, None), (3, None), (4, None)]
    wmv = []
    for nm in small:
        for pre in ("", "m_", "v_"):
            wmv.append(given[pre + nm][0] if nm in ("w_spatial", "b_spatial") else given[pre + nm])
    outs = _adamw_small([a1, a2, a3, a4], [g_bada], grads, wmv, (0, 1), "adamw_small", after=quad[1])

    (g_wi1,), (p_wi1,) = _chip_scatter_wait(h_i1, outs[0], "tail_wait")
    quad = _adamw_reduce(p_wi1, w_f1_in[0].T, m_w_f1_in[0].T, v_w_f1_in[0].T, FO, "adamw_w_f1_in", own=g_wi1)
    res["w_f1_in"] = tuple(t.T[None] for t in quad)
    (g_wo1,), (p_wo1,) = _chip_scatter_wait(h_o1, quad[1], "tail2_wait")
    quad = _adamw_reduce(p_wo1, w_f1_out[0], m_w_f1_out[0], v_w_f1_out[0], FO, "adamw_w_f1_out", own=g_wo1)
    res["w_f1_out"] = tuple(t[None] for t in quad)
    loss = outs[0][6, 0]
    for t, nm in enumerate(small):
        quad = outs[2 + 4 * t:6 + 4 * t]
        res[nm] = tuple(q[None] for q in quad) if nm in ("w_spatial", "b_spatial") else tuple(quad)
    g_cw = lax.dynamic_slice(outs[1], (8, mi * 64), (32, 64))
    wmv = [jnp.pad(given[pre + "conv_w"][0], ((0, 1), (0, 0)), constant_values=1.0 if pre == "v_" else 0.0)
           for pre in ("", "m_", "v_")]
    quad = _adamw_small([], [g_cw], [(0, None)], wmv, (), "adamw_conv_w")
    res["conv_w"] = tuple(q[0:CONV_K][None] for q in quad)

    order = ["w_ada", "b_ada", "g_pre_f1", "g_post_f1", "w_f1_in", "w_f1_out", "g_pre_m", "g_post_m", "w_mix_in",
             "gmlp_norm_g", "gmlp_norm_b", "w_spatial", "b_spatial", "conv_w", "conv_b", "conv_norm_g", "conv_norm_b",
             "g_out_a", "g_out_b", "w_mix_out", "g_pre_f2", "g_post_f2", "w_f2_in", "w_f2_out"]
    out = [loss, dx0.reshape(bl, seq, D)]
    for k in range(4):
        out += [res[nm][k] for nm in order]
    return tuple(out)
```

```python
import jax
import jax.numpy as jnp
from jax import lax
from jax.experimental import pallas as pl
from jax.experimental.pallas import tpu as pltpu

F32 = jnp.float32
BF16 = jnp.bfloat16

D = 1024
DFF = 2816
NDEV = 8
FB = 2 * DFF // NDEV
NCH = DFF // FB
LANES = 128
SUBL = 8
FO = DFF // NDEV
WA = 512
NSLAB = WA // LANES
NHEAD = 8
HD = 64
CHUNK = 128
CONV_K = 31
HALO = 32
MB = 2 * (WA + WA) // NDEV
MO = D // NDEV
ADA_B = 9 * D // NDEV
EPS = 1e-6
HALF = 0.5

ADAM_LR = 0.001
ADAM_B1 = 0.9
ADAM_B2 = 0.999
ADAM_EPS = 1e-08
ADAM_WD = 0.01
ADAM_STEP = 10

VMEM_LIMIT = 56 * 1024 * 1024
MESH = pl.DeviceIdType.MESH
FLIPS = ((0, 0, 1), (1, 0, 0), (0, 1, 0), (1, 1, 0), (1, 0, 1), (0, 1, 1), (1, 1, 1))
CHIP_FLIPS = ((1, 0, 0), (0, 1, 0), (1, 1, 0))
HBM = pl.BlockSpec(memory_space=pl.ANY)
VM = pl.BlockSpec(memory_space=pltpu.VMEM)


def _dot(a, b):
    return lax.dot_general(a, b, (((1,), (0,)), ((), ())), preferred_element_type=F32)


def _dot_nt(a, b):
    return lax.dot_general(a, b, (((1,), (1,)), ((), ())), preferred_element_type=F32)


def _dot_tn(a, b):
    return lax.dot_general(a, b, (((0,), (0,)), ((), ())), preferred_element_type=F32)


def _rowmean(v):
    return jnp.mean(v, axis=-1, keepdims=True)


def _colsum(v):
    return jnp.sum(v, axis=0, keepdims=True)


def _sigmoid(v):
    return 0.5 * jnp.tanh(0.5 * v) + 0.5


def _const_spec(shape):
    nd = len(shape)
    return pl.BlockSpec(shape, lambda *_: (0,) * nd, pipeline_mode=pl.Buffered(1))


def _me():
    return lax.axis_index("x"), lax.axis_index("y"), lax.axis_index("c")


def _flip(me, f):
    return tuple(1 - v if b else v for v, b in zip(me, f))


def _lin(p):
    return 4 * p[0] + 2 * p[1] + p[2]


def _remote(src, dst, send_sem, recv_sem, dev):
    return pltpu.make_async_remote_copy(src_ref=src, dst_ref=dst, send_sem=send_sem, recv_sem=recv_sem,
                                        device_id=dev, device_id_type=MESH)


def _blk(kind, ref, p):
    if kind == "out":
        return ref.at[2 * p[0] + p[1], pl.ds(p[2] * FO, FO), :]
    return ref.at[_lin(p)]


class _Gather:
    def __init__(self, shards, kinds, late_mid=False):
        self.late_mid = late_mid
        self.kinds = kinds
        self.n = len(shards)
        self.ins = list(shards)
        self.out_shape = [jax.ShapeDtypeStruct((4, FB, D) if k == "out" else (NDEV,) + s.shape, s.dtype)
                          for s, k in zip(shards, kinds)]
        self.sems = [pltpu.SemaphoreType.DMA((7 * self.n,)), pltpu.SemaphoreType.DMA((7 * self.n,)),
                     pltpu.SemaphoreType.DMA((self.n,))]

    def _first(self, ins, outs, sems):
        ssem, rsem, lsem = sems
        me = _me()
        sib = _flip(me, (0, 0, 1))
        cps, loc = [], []
        for a in range(self.n):
            mine = _blk(self.kinds[a], outs[a], me)
            loc.append(pltpu.make_async_copy(ins[a], mine, lsem.at[a]))
            cps.append(_remote(ins[a], mine, ssem.at[7 * a], rsem.at[7 * a], sib))
            for j, f in enumerate(CHIP_FLIPS):
                cps.append(_remote(ins[a], mine, ssem.at[7 * a + 1 + j], rsem.at[7 * a + 1 + j], _flip(me, f)))
        return cps, loc

    def _passed(self, outs, sems):
        ssem, rsem, _ = sems
        me = _me()
        sib = _flip(me, (0, 0, 1))
        cps = []
        for j, f in enumerate(CHIP_FLIPS):
            for a in range(self.n):
                blk = _blk(self.kinds[a], outs[a], _flip(me, f))
                cps.append(_remote(blk, blk, ssem.at[7 * a + 4 + j], rsem.at[7 * a + 4 + j], sib))
        return cps

    def start(self, ins, outs, sems):
        cps, loc = self._first(ins, outs, sems)
        for cp in loc + cps:
            cp.start()

    def mid(self, ins, outs, sems):
        ssem, rsem, _ = sems
        me = _me()
        passed = self._passed(outs, sems)
        t = 0
        for j, f in enumerate(CHIP_FLIPS):
            for a in range(self.n):
                blk = _blk(self.kinds[a], outs[a], _flip(me, f))
                _remote(blk, blk, ssem.at[7 * a + 1 + j], rsem.at[7 * a + 1 + j], _flip(me, f)).wait_recv()
                passed[t].start()
                t += 1

    def end(self, ins, outs, sems):
        ssem, rsem, _ = sems
        me = _me()
        sib = _flip(me, (0, 0, 1))
        for a in range(self.n):
            blk = _blk(self.kinds[a], outs[a], sib)
            _remote(blk, blk, ssem.at[7 * a], rsem.at[7 * a], sib).wait_recv()
            for j, f in enumerate(CHIP_FLIPS):
                blk = _blk(self.kinds[a], outs[a], _flip(_flip(me, f), (0, 0, 1)))
                _remote(blk, blk, ssem.at[7 * a + 4 + j], rsem.at[7 * a + 4 + j], sib).wait_recv()
        cps, loc = self._first(ins, outs, sems)
        for cp in cps + self._passed(outs, sems):
            cp.wait_send()
        for cp in loc:
            cp.wait()


class _RelayGather(_Gather):
    def _peers(self):
        me = _me()
        c = me[2]
        to = (me[0] + (1 - c) - 2 * me[0] * (1 - c), me[1] + c - 2 * me[1] * c, c)
        frm = (me[0] + c - 2 * me[0] * c, me[1] + (1 - c) - 2 * me[1] * (1 - c), c)
        return me, _flip(me, (0, 0, 1)), to, frm, _flip(me, (1, 1, 0))

    def _first(self, ins, outs, sems):
        ssem, rsem, lsem = sems
        me, sib, to, frm, _ = self._peers()
        cps, loc = [], []
        for a in range(self.n):
            mine = _blk(self.kinds[a], outs[a], me)
            loc.append(pltpu.make_async_copy(ins[a], mine, lsem.at[a]))
            for slot, dev in ((0, sib), (1, to), (2, frm)):
                cps.append(_remote(ins[a], mine, ssem.at[7 * a + slot], rsem.at[7 * a + slot], dev))
        return cps, loc

    def _block_copy(self, outs, sems, a, slot, owner, dev):
        ssem, rsem, _ = sems
        blk = _blk(self.kinds[a], outs[a], owner)
        return _remote(blk, blk, ssem.at[7 * a + slot], rsem.at[7 * a + slot], dev)

    def mid(self, ins, outs, sems):
        me, sib, to, frm, _ = self._peers()
        for a in range(self.n):
            self._block_copy(outs, sems, a, 2, frm, frm).wait_recv()
            self._block_copy(outs, sems, a, 3, frm, to).start()
            self._block_copy(outs, sems, a, 5, frm, sib).start()
        for a in range(self.n):
            self._block_copy(outs, sems, a, 1, to, to).wait_recv()
            self._block_copy(outs, sems, a, 4, to, sib).start()

    def end(self, ins, outs, sems):
        me, sib, to, frm, far = self._peers()
        up = (0, 0, 1)
        for a in range(self.n):
            self._block_copy(outs, sems, a, 3, far, to).wait_recv()
            self._block_copy(outs, sems, a, 6, far, sib).start()
        for a in range(self.n):
            for slot, owner in ((0, sib), (4, _flip(frm, up)), (5, _flip(to, up)), (6, _flip(far, up))):
                self._block_copy(outs, sems, a, slot, owner, sib).wait_recv()
        cps, loc = self._first(ins, outs, sems)
        for a in range(self.n):
            cps += [self._block_copy(outs, sems, a, 3, frm, to), self._block_copy(outs, sems, a, 4, to, sib),
                    self._block_copy(outs, sems, a, 5, frm, sib), self._block_copy(outs, sems, a, 6, far, sib)]
        for cp in cps:
            cp.wait_send()
        for cp in loc:
            cp.wait()


class _ChipScatter:
    def __init__(self, grads):
        self.n = len(grads)
        self.ins = list(grads)
        self.out_shape = [jax.ShapeDtypeStruct(g.shape, BF16) for g in grads]
        self.sems = [pltpu.SemaphoreType.DMA((3 * self.n,)), pltpu.SemaphoreType.DMA((3 * self.n,)),
                     pltpu.SemaphoreType.DMA((self.n,))]

    def _copies(self, ins, outs, sems):
        ssem, rsem, lsem = sems
        me = _me()
        mq = 2 * me[0] + me[1]
        loc = [pltpu.make_async_copy(ins[a].at[mq], outs[a].at[mq], lsem.at[a]) for a in range(self.n)]
        cps = []
        for k, f in enumerate(CHIP_FLIPS):
            p = _flip(me, f)
            for a in range(self.n):
                cps.append(_remote(ins[a].at[2 * p[0] + p[1]], outs[a].at[mq], ssem.at[3 * a + k], rsem.at[3 * a + k], p))
        return cps, loc

    def start(self, ins, outs, sems):
        cps, loc = self._copies(ins, outs, sems)
        for cp in loc + cps:
            cp.start()

    mid = None

    def end(self, ins, outs, sems):
        ssem, rsem, _ = sems
        me = _me()
        mq = 2 * me[0] + me[1]
        for k, f in enumerate(CHIP_FLIPS):
            p = _flip(me, f)
            for a in range(self.n):
                _remote(ins[a].at[mq], outs[a].at[2 * p[0] + p[1]], ssem.at[3 * a + k], rsem.at[3 * a + k], p).wait_recv()
        cps, loc = self._copies(ins, outs, sems)
        for cp in cps:
            cp.wait_send()
        for cp in loc:
            cp.wait()


class _AllGather:
    def __init__(self, parts):
        self.n = len(parts)
        self.ins = list(parts)
        self.out_shape = [jax.ShapeDtypeStruct((NDEV,) + p.shape, p.dtype) for p in parts]
        self.sems = [pltpu.SemaphoreType.DMA((7 * self.n,)), pltpu.SemaphoreType.DMA((7 * self.n,)),
                     pltpu.SemaphoreType.DMA((self.n,))]

    def _copies(self, ins, outs, sems):
        ssem, rsem, lsem = sems
        me = _me()
        mi = _lin(me)
        loc = [pltpu.make_async_copy(ins[a], outs[a].at[mi], lsem.at[a]) for a in range(self.n)]
        cps = []
        for k, f in enumerate(FLIPS):
            for a in range(self.n):
                cps.append(_remote(ins[a], outs[a].at[mi], ssem.at[7 * a + k], rsem.at[7 * a + k], _flip(me, f)))
        return cps, loc

    def start(self, ins, outs, sems):
        cps, loc = self._copies(ins, outs, sems)
        for cp in loc + cps:
            cp.start()

    mid = None

    def end(self, ins, outs, sems):
        ssem, rsem, _ = sems
        me = _me()
        for k, f in enumerate(FLIPS):
            p = _flip(me, f)
            for a in range(self.n):
                _remote(ins[a], outs[a].at[_lin(p)], ssem.at[7 * a + k], rsem.at[7 * a + k], p).wait_recv()
        cps, loc = self._copies(ins, outs, sems)
        for cp in cps:
            cp.wait_send()
        for cp in loc:
            cp.wait()


def _call(core, *, name, grid, in_specs, out_specs, out_shape, args, scratch=(), jobs=(), core_starts=False):
    n_in, n_out, n_sc = len(in_specs), len(out_specs), len(scratch)
    steps = 1
    for g in grid:
        steps *= g

    def body(*refs):
        pos = [0]

        def take(k):
            r = refs[pos[0]:pos[0] + k]
            pos[0] += k
            return r

        ins = take(n_in)
        j_ins = [take(len(j.ins)) for j in jobs]
        outs = take(n_out)
        j_outs = [take(len(j.out_shape)) for j in jobs]
        scs = take(n_sc)
        j_sems = [take(len(j.sems)) for j in jobs]
        if len(grid) == 2:
            step = pl.program_id(0) * grid[1] + pl.program_id(1)
        elif len(grid) == 1:
            step = pl.program_id(0)
        else:
            step = 0
        def start_jobs():
            for j, ji, jo, js in zip(jobs, j_ins, j_outs, j_sems):
                j.start(ji, jo, js)

        if grid:
            pl.when(step == 0)(start_jobs)
        elif not core_starts:
            start_jobs()
        for j, ji, jo, js in zip(jobs, j_ins, j_outs, j_sems):
            if j.mid is not None and grid:
                at = max(steps - 2, 0) if j.late_mid else (3 * steps) // 4
                pl.when(step == at)(lambda j=j, ji=ji, jo=jo, js=js: j.mid(ji, jo, js))
        def finish_jobs():
            for j, ji, jo, js in zip(jobs, j_ins, j_outs, j_sems):
                if j.mid is not None:
                    j.mid(ji, jo, js)
                j.end(ji, jo, js)

        if core_starts:
            core(ins, outs, scs, start_jobs, finish_jobs)
        elif core is not None:
            core(ins, outs, scs)
        if grid:
            for j, ji, jo, js in zip(jobs, j_ins, j_outs, j_sems):
                pl.when(step == steps - 1)(lambda j=j, ji=ji, jo=jo, js=js: j.end(ji, jo, js))
        elif not core_starts:
            finish_jobs()

    all_in = list(in_specs)
    all_args = list(args)
    all_out = list(out_specs)
    all_shape = list(out_shape)
    all_sc = list(scratch)
    for j in jobs:
        all_in += [HBM] * len(j.ins)
        all_args += j.ins
    for j in jobs:
        all_out += [HBM] * len(j.out_shape)
        all_shape += j.out_shape
        all_sc += j.sems
    params = dict(vmem_limit_bytes=VMEM_LIMIT)
    if grid:
        params["dimension_semantics"] = ("arbitrary",) * len(grid)
    res = pl.pallas_call(
        body, name=name, grid=grid, in_specs=all_in, out_specs=all_out, out_shape=all_shape,
        scratch_shapes=all_sc, compiler_params=pltpu.CompilerParams(**params),
    )(*all_args)
    core_res = list(res[:n_out])
    job_res = []
    pos = n_out
    for j in jobs:
        job_res.append(list(res[pos:pos + len(j.out_shape)]))
        pos += len(j.out_shape)
    return core_res, job_res


def _ffn_fwd(x, mod, gvec, w_in, w_out, tm, name, jobs=()):
    T = x.shape[0]
    nt = T // tm
    tps = nt // mod.shape[0]

    def core(ins, outs, _):
        x_ref, mod_ref, g_ref, win_ref, wout_ref = ins
        xo_ref, gu_ref, y_ref = outs
        xv = x_ref[...]
        sh, sc, gt = mod_ref[0:1, :], mod_ref[1:2, :], mod_ref[2:3, :]
        r = lax.rsqrt(_rowmean(xv * xv) + EPS)
        h = (xv * r * g_ref[0:1, :]) * (1.0 + sc) + sh
        hb = h.astype(BF16)
        y = jnp.zeros((tm, D), F32)
        for cidx in range(NCH):
            gate = _dot_nt(hb, win_ref[cidx])
            up = _dot_nt(hb, win_ref[NCH + cidx])
            gu_ref[cidx] = gate.astype(BF16)
            gu_ref[NCH + cidx] = up.astype(BF16)
            act = gate * _sigmoid(gate) * up
            y = y + _dot(act.astype(BF16), wout_ref[cidx])
        y_ref[...] = y
        ry = lax.rsqrt(_rowmean(y * y) + EPS)
        xo_ref[...] = xv + (HALF * gt) * (y * ry * g_ref[1:2, :])

    tile = pl.BlockSpec((tm, D), lambda i: (i, 0))
    return _call(
        core, name=name, grid=(nt,), jobs=jobs,
        in_specs=[tile, pl.BlockSpec((None, 8, D), lambda i: (i // tps, 0, 0)), _const_spec((8, D)),
                  _const_spec((8, FB, D)), _const_spec((4, FB, D))],
        out_specs=[tile, pl.BlockSpec((8, tm, FB), lambda i: (0, i, 0)), tile],
        out_shape=[jax.ShapeDtypeStruct((T, D), F32), jax.ShapeDtypeStruct((8, T, FB), BF16),
                   jax.ShapeDtypeStruct((T, D), F32)],
        args=[x, mod, gvec, w_in, w_out])


def _ffn_bwd(dxo, x, y, gu, mod, gvec, w_in, w_out, tm, name, jobs=()):
    T = x.shape[0]
    nt = T // tm
    nb = mod.shape[0]
    tps = nt // nb

    def core(ins, outs, _):
        dxo_ref, x_ref, y_ref, gu_ref, mod_ref, g_ref, win_ref, wout_ref = ins
        dx_ref, dg_ref, act_ref, hb_ref, dyb_ref, mg_ref, vg_ref = outs
        i = pl.program_id(0)
        xv = x_ref[...]
        dxo_v = dxo_ref[...]
        yv = y_ref[...]
        sh, sc, gt = mod_ref[0:1, :], mod_ref[1:2, :], mod_ref[2:3, :]
        gpre, gpost = g_ref[0:1, :], g_ref[1:2, :]
        r = lax.rsqrt(_rowmean(xv * xv) + EPS)
        xh = xv * r
        n = xh * gpre
        hb = (n * (1.0 + sc) + sh).astype(BF16)
        hb_ref[...] = hb
        ry = lax.rsqrt(_rowmean(yv * yv) + EPS)
        yh = yv * ry
        d_gt = _colsum(HALF * dxo_v * (yh * gpost))
        dp = (HALF * gt) * dxo_v
        d_gpost = _colsum(dp * yh)
        dyh = dp * gpost
        dy = ry * (dyh - yh * _rowmean(dyh * yh))
        dyb = dy.astype(BF16)
        dyb_ref[...] = dyb
        dh = jnp.zeros((tm, D), F32)
        for cidx in range(NCH):
            gate = gu_ref[cidx].astype(F32)
            up = gu_ref[NCH + cidx].astype(F32)
            sig = _sigmoid(gate)
            s = gate * sig
            act_ref[cidx] = (s * up).astype(BF16)
            d_act = _dot_nt(dyb, wout_ref[cidx])
            d_up = (d_act * s).astype(BF16)
            d_gate = (d_act * up * (sig * (1.0 + gate * (1.0 - sig)))).astype(BF16)
            dg_ref[cidx] = d_gate
            dg_ref[NCH + cidx] = d_up
            dh = dh + _dot(d_gate, win_ref[cidx]) + _dot(d_up, win_ref[NCH + cidx])
        d_sc = _colsum(dh * n)
        d_sh = _colsum(dh)
        dn = dh * (1.0 + sc)
        d_gpre = _colsum(dn * xh)
        dxh = dn * gpre
        dx_ref[...] = dxo_v + r * (dxh - xh * _rowmean(dxh * xh))

        @pl.when(i % tps == 0)
        def _():
            mg_ref[...] = jnp.zeros((8, D), F32)

        @pl.when(i == 0)
        def _():
            vg_ref[...] = jnp.zeros((8, D), F32)

        mg_ref[0:1, :] += d_sh
        mg_ref[1:2, :] += d_sc
        mg_ref[2:3, :] += d_gt
        vg_ref[0:1, :] += d_gpre
        vg_ref[1:2, :] += d_gpost

    tile = pl.BlockSpec((tm, D), lambda i: (i, 0))
    return _call(
        core, name=name, grid=(nt,), jobs=jobs,
        in_specs=[tile, tile, tile, pl.BlockSpec((8, tm, FB), lambda i: (0, i, 0)),
                  pl.BlockSpec((None, 8, D), lambda i: (i // tps, 0, 0)), _const_spec((8, D)),
                  _const_spec((8, FB, D)), _const_spec((4, FB, D))],
        out_specs=[tile, pl.BlockSpec((8, tm, FB), lambda i: (0, i, 0)),
                   pl.BlockSpec((4, tm, FB), lambda i: (0, i, 0)), tile, tile,
                   pl.BlockSpec((None, 8, D), lambda i: (i // tps, 0, 0)), pl.BlockSpec((8, D), lambda i: (0, 0))],
        out_shape=[jax.ShapeDtypeStruct((T, D), F32), jax.ShapeDtypeStruct((8, T, FB), BF16),
                   jax.ShapeDtypeStruct((4, T, FB), BF16), jax.ShapeDtypeStruct((T, D), BF16),
                   jax.ShapeDtypeStruct((T, D), BF16), jax.ShapeDtypeStruct((nb, 8, D), F32),
                   jax.ShapeDtypeStruct((8, D), F32)],
        args=[dxo, x, y, gu, mod, gvec, w_in, w_out])


def _ffn_last(x, target, mod, gvec, w_in, w_out, tm, name, jobs=()):
    T = x.shape[0]
    nt = T // tm
    nb = mod.shape[0]
    tps = nt // nb

    def core(ins, outs, scs):
        x_ref, t_ref, mod_ref, g_ref, wina_ref, winb_ref, wout_ref = ins
        dx_ref, dg_ref, act_ref, hb_ref, dyb_ref, mg_ref, vg_ref, loss_ref = outs
        hd2 = w_in[0].shape[2]
        (gu_s,) = scs
        i = pl.program_id(0)
        xv = x_ref[...]
        sh, sc, gt = mod_ref[0:1, :], mod_ref[1:2, :], mod_ref[2:3, :]
        gpre, gpost = g_ref[0:1, :], g_ref[1:2, :]
        r = lax.rsqrt(_rowmean(xv * xv) + EPS)
        xh = xv * r
        n = xh * gpre
        hb = (n * (1.0 + sc) + sh).astype(BF16)
        hb_ref[...] = hb
        hba, hbb = hb[:, 0:hd2], hb[:, hd2:D]
        yv = jnp.zeros((tm, D), F32)
        for cidx in range(NCH):
            gate = _dot_nt(hba, wina_ref[cidx]) + _dot_nt(hbb, winb_ref[cidx])
            up = _dot_nt(hba, wina_ref[NCH + cidx]) + _dot_nt(hbb, winb_ref[NCH + cidx])
            gu_s[cidx] = gate.astype(BF16)
            gu_s[NCH + cidx] = up.astype(BF16)
            act = gate * _sigmoid(gate) * up
            act_ref[cidx] = act.astype(BF16)
            yv = yv + _dot(act_ref[cidx], wout_ref[cidx])
        ry = lax.rsqrt(_rowmean(yv * yv) + EPS)
        yh = yv * ry
        pn = yh * gpost
        err = xv + (HALF * gt) * pn - t_ref[...]
        dxo_v = err * (1.0 / D)
        d_gt = _colsum(HALF * dxo_v * pn)
        dp = (HALF * gt) * dxo_v
        d_gpost = _colsum(dp * yh)
        dyh = dp * gpost
        dyb = (ry * (dyh - yh * _rowmean(dyh * yh))).astype(BF16)
        dyb_ref[...] = dyb
        dha = jnp.zeros((tm, hd2), F32)
        dhb = jnp.zeros((tm, D - hd2), F32)
        for cidx in range(NCH):
            gate = gu_s[cidx].astype(F32)
            up = gu_s[NCH + cidx].astype(F32)
            sig = _sigmoid(gate)
            s = gate * sig
            d_act = _dot_nt(dyb, wout_ref[cidx])
            d_up = (d_act * s).astype(BF16)
            d_gate = (d_act * up * (sig * (1.0 + gate * (1.0 - sig)))).astype(BF16)
            dg_ref[cidx] = d_gate
            dg_ref[NCH + cidx] = d_up
            dha = dha + _dot(d_gate, wina_ref[cidx]) + _dot(d_up, wina_ref[NCH + cidx])
            dhb = dhb + _dot(d_gate, winb_ref[cidx]) + _dot(d_up, winb_ref[NCH + cidx])
        dh = jnp.concatenate([dha, dhb], axis=1)
        d_sc = _colsum(dh * n)
        d_sh = _colsum(dh)
        dn = dh * (1.0 + sc)
        d_gpre = _colsum(dn * xh)
        dxh = dn * gpre
        dx_ref[...] = dxo_v + r * (dxh - xh * _rowmean(dxh * xh))

        @pl.when(i % tps == 0)
        def _():
            mg_ref[...] = jnp.zeros((8, D), F32)

        @pl.when(i == 0)
        def _():
            vg_ref[...] = jnp.zeros((8, D), F32)
            loss_ref[...] = jnp.zeros((8, D), F32)

        mg_ref[0:1, :] += d_sh
        mg_ref[1:2, :] += d_sc
        mg_ref[2:3, :] += d_gt
        vg_ref[0:1, :] += d_gpre
        vg_ref[1:2, :] += d_gpost
        loss_ref[...] += HALF * jnp.sum(_rowmean(err * err), axis=0, keepdims=True)

    tile = pl.BlockSpec((tm, D), lambda i: (i, 0))
    return _call(
        core, name=name, grid=(nt,), jobs=jobs,
        in_specs=[tile, tile, pl.BlockSpec((None, 8, D), lambda i: (i // tps, 0, 0)), _const_spec((8, D)),
                  _const_spec(w_in[0].shape), _const_spec(w_in[1].shape), _const_spec((4, FB, D))],
        out_specs=[tile, pl.BlockSpec((8, tm, FB), lambda i: (0, i, 0)),
                   pl.BlockSpec((4, tm, FB), lambda i: (0, i, 0)), tile, tile,
                   pl.BlockSpec((None, 8, D), lambda i: (i // tps, 0, 0)), pl.BlockSpec((8, D), lambda i: (0, 0)),
                   pl.BlockSpec((8, D), lambda i: (0, 0))],
        out_shape=[jax.ShapeDtypeStruct((T, D), F32), jax.ShapeDtypeStruct((8, T, FB), BF16),
                   jax.ShapeDtypeStruct((4, T, FB), BF16), jax.ShapeDtypeStruct((T, D), BF16),
                   jax.ShapeDtypeStruct((T, D), BF16), jax.ShapeDtypeStruct((nb, 8, D), F32),
                   jax.ShapeDtypeStruct((8, D), F32), jax.ShapeDtypeStruct((8, D), F32)],
        scratch=[pltpu.VMEM((8, tm, FB), BF16)],
        args=[x, target, mod, gvec, w_in[0], w_in[1], w_out])


def _masked_spatial(ws_ref):
    row = lax.broadcasted_iota(jnp.int32, (CHUNK, CHUNK), 0)
    col = lax.broadcasted_iota(jnp.int32, (CHUNK, CHUNK), 1)
    keep = col <= row
    return [jnp.where(keep, ws_ref[hd], 0.0).astype(BF16) for hd in range(NHEAD)]


def _head_pairs(mats, right, transpose=False):
    first = lax.broadcasted_iota(jnp.int32, (CHUNK, LANES), 1) < HD
    op = _dot_tn if transpose else _dot
    out = []
    for p in range(NHEAD // 2):
        slab = right[:, _lanes(p)]
        out.append(jnp.where(first, op(mats[2 * p], slab), op(mats[2 * p + 1], slab)))
    return jnp.concatenate(out, axis=1)


def _spatial_gate(wm, vb_chunk):
    return _head_pairs(wm, vb_chunk)


def _layer_norm_stats(v):
    mu = _rowmean(v)
    vc = v - mu
    rstd = lax.rsqrt(_rowmean(vc * vc) + EPS)
    return vc * rstd, rstd


def _pitch(tm):
    p = tm // 8
    while p % 8 != 4:
        p += 1
    return p


def _lanes(s):
    return slice(s * LANES, (s + 1) * LANES)


def _to_slabs(ref, row0, val):
    for s in range(NSLAB):
        ref[s, row0:row0 + val.shape[0], :] = val[:, _lanes(s)]


def _tap_sum(src, out, cw_ref, bias, tm, start):
    p = _pitch(tm)
    for s in range(NSLAB):
        accs = [jnp.broadcast_to(bias[:, _lanes(s)], (SUBL, LANES))] * p
        for k in range(CONV_K):
            w = jnp.broadcast_to(cw_ref[k:k + 1, _lanes(s)], (SUBL, LANES))
            for v in range(p):
                accs[v] = accs[v] + w * src[s, pl.ds(v + start(k), 8, stride=p), :]
        for v in range(p):
            out[s, pl.ds(v, 8, stride=p), :] = accs[v]
    return jnp.concatenate([out[s, 0:tm, :] for s in range(NSLAB)], axis=1)


def _mixer_fwd(x, mod, gvec, w_mi, w_mo, v512, ws, bias_full, cw, tm, name, jobs=()):
    T = x.shape[0]
    nt = T // tm
    tps = nt // mod.shape[0]
    ext_rows = 8 * _pitch(tm)

    def core(ins, outs, scs):
        x_ref, mod_ref, g_ref, wmi_ref, wmo_ref, v_ref, ws_ref, bias_ref, cw_ref = ins
        xo_ref, proj_ref, ym_ref, conv_ref = outs
        glu_ext, conv_scr = scs
        i = pl.program_id(0)
        xv = x_ref[...]
        sh, sc, gt = mod_ref[0:1, :], mod_ref[1:2, :], mod_ref[2:3, :]
        r = lax.rsqrt(_rowmean(xv * xv) + EPS)
        hb = ((xv * r * g_ref[0:1, :]) * (1.0 + sc) + sh).astype(BF16)
        for j in range(NDEV):
            proj_ref[:, j * MB:(j + 1) * MB] = _dot(hb, wmi_ref[j])
        u = proj_ref[:, 0:WA]
        v0 = proj_ref[:, WA:2 * WA]
        a = proj_ref[:, 2 * WA:3 * WA]
        g = proj_ref[:, 3 * WA:4 * WA]
        vh, _ = _layer_norm_stats(v0)
        vb = (vh * v_ref[0:1, :] + v_ref[1:2, :]).astype(BF16)
        wm = _masked_spatial(ws_ref)
        ya = []
        for q in range(tm // CHUNK):
            z = _spatial_gate(wm, vb[q * CHUNK:(q + 1) * CHUNK, :]) + bias_ref[...]
            ya.append(u[q * CHUNK:(q + 1) * CHUNK, :] * z)
        ya = jnp.concatenate(ya, axis=0)
        glu = a * _sigmoid(g)

        @pl.when(i == 0)
        def _():
            glu_ext[:, HALO + tm:HALO + ext_rows, :] = jnp.zeros((NSLAB, ext_rows - tm, LANES), F32)

        @pl.when(i % tps == 0)
        def _():
            glu_ext[:, 0:HALO, :] = jnp.zeros((NSLAB, HALO, LANES), F32)

        _to_slabs(glu_ext, HALO, glu)
        conv = _tap_sum(glu_ext, conv_scr, cw_ref, v_ref[2:3, :], tm, lambda k: HALO - (CONV_K - 1) + k)
        conv_ref[...] = conv
        glu_ext[:, 0:HALO, :] = glu_ext[:, tm:tm + HALO, :]
        ch, _ = _layer_norm_stats(conv)
        cn = ch * v_ref[3:4, :] + v_ref[4:5, :]
        yb = cn * _sigmoid(cn)
        pa = ya * lax.rsqrt(_rowmean(ya * ya) + EPS) * v_ref[5:6, :]
        pb = yb * lax.rsqrt(_rowmean(yb * yb) + EPS) * v_ref[6:7, :]
        ycat = jnp.concatenate([pa, pb], axis=1).astype(BF16)
        ym = _dot(ycat, wmo_ref[...])
        ym_ref[...] = ym
        rm = lax.rsqrt(_rowmean(ym * ym) + EPS)
        xo_ref[...] = xv + gt * (ym * rm * g_ref[1:2, :])

    tile = pl.BlockSpec((tm, D), lambda i: (i, 0))
    return _call(
        core, name=name, grid=(nt,), jobs=jobs,
        in_specs=[tile, pl.BlockSpec((None, 8, D), lambda i: (i // tps, 0, 0)), _const_spec((8, D)),
                  _const_spec((NDEV, D, MB)), _const_spec((D, D)), _const_spec((8, WA)),
                  _const_spec((NHEAD, CHUNK, CHUNK)), _const_spec((CHUNK, WA)), _const_spec((32, WA))],
        out_specs=[tile, pl.BlockSpec((tm, 4 * WA), lambda i: (i, 0)), tile, pl.BlockSpec((tm, WA), lambda i: (i, 0))],
        out_shape=[jax.ShapeDtypeStruct((T, D), F32), jax.ShapeDtypeStruct((T, 4 * WA), F32),
                   jax.ShapeDtypeStruct((T, D), F32), jax.ShapeDtypeStruct((T, WA), F32)],
        scratch=[pltpu.VMEM((NSLAB, HALO + ext_rows, LANES), F32), pltpu.VMEM((NSLAB, ext_rows, LANES), F32)],
        args=[x, mod, gvec, w_mi, w_mo, v512, ws, bias_full, cw])


def _mixer_bwd_a(dxo, ym, proj, conv, mod, gvec, w_mo, v512, ws, bias_full, esel, tm, name, jobs=()):
    T = dxo.shape[0]
    nt = T // tm
    nb = mod.shape[0]
    tps = nt // nb

    def core(ins, outs, scs):
        dxo_ref, ym_ref, proj_ref, conv_ref, mod_ref, g_ref, wmo_ref, v_ref, ws_ref, bias_ref, e_ref = ins
        dpart_ref, dymb_ref, ycat_ref, mg_ref, vg_ref, v5g_ref, gws_ref, gbs_ref = outs
        (dbs_acc,) = scs
        i = pl.program_id(0)
        dxo_v = dxo_ref[...]
        ymv = ym_ref[...]
        gt = mod_ref[2:3, :]
        gpost = g_ref[1:2, :]
        rm = lax.rsqrt(_rowmean(ymv * ymv) + EPS)
        ymh = ymv * rm
        d_gt = _colsum(dxo_v * (ymh * gpost))
        dpm = gt * dxo_v
        d_gpost = _colsum(dpm * ymh)
        dymh = dpm * gpost
        dym = (rm * (dymh - ymh * _rowmean(dymh * ymh))).astype(BF16)
        dymb_ref[...] = dym
        dycat = _dot_nt(dym, wmo_ref[...])
        u = proj_ref[:, 0:WA]
        v0 = proj_ref[:, WA:2 * WA]
        vh, rv = _layer_norm_stats(v0)
        vb = (vh * v_ref[0:1, :] + v_ref[1:2, :]).astype(BF16)
        wm = _masked_spatial(ws_ref)
        zs = []
        for q in range(tm // CHUNK):
            zs.append(_spatial_gate(wm, vb[q * CHUNK:(q + 1) * CHUNK, :]) + bias_ref[...])
        z = jnp.concatenate(zs, axis=0)
        ya = u * z
        ra = lax.rsqrt(_rowmean(ya * ya) + EPS)
        yah = ya * ra
        ch, rc = _layer_norm_stats(conv_ref[...])
        cn = ch * v_ref[3:4, :] + v_ref[4:5, :]
        sg = _sigmoid(cn)
        yb = cn * sg
        rb = lax.rsqrt(_rowmean(yb * yb) + EPS)
        ybh = yb * rb
        ycat_ref[...] = jnp.concatenate([yah * v_ref[5:6, :], ybh * v_ref[6:7, :]], axis=1).astype(BF16)
        dpa = dycat[:, 0:WA]
        dpb = dycat[:, WA:2 * WA]
        d_goa = _colsum(dpa * yah)
        d_gob = _colsum(dpb * ybh)
        dyah = dpa * v_ref[5:6, :]
        dybh = dpb * v_ref[6:7, :]
        dya = ra * (dyah - yah * _rowmean(dyah * yah))
        dyb = rb * (dybh - ybh * _rowmean(dybh * ybh))
        dpart_ref[:, 0:WA] = dya * z
        dz = dya * u

        @pl.when(i == 0)
        def _():
            gws_ref[...] = jnp.zeros((NHEAD, CHUNK, CHUNK), F32)
            dbs_acc[...] = jnp.zeros((CHUNK, WA), F32)
            vg_ref[...] = jnp.zeros((8, D), F32)
            v5g_ref[...] = jnp.zeros((8, WA), F32)

        first = lax.broadcasted_iota(jnp.int32, (CHUNK, LANES), 1) < HD
        dvs = []
        for q in range(tm // CHUNK):
            dz_q = dz[q * CHUNK:(q + 1) * CHUNK, :]
            vb_q = vb[q * CHUNK:(q + 1) * CHUNK, :]
            dbs_acc[...] += dz_q
            dzb = dz_q.astype(BF16)
            dvs.append(_head_pairs(wm, dzb, transpose=True))
            for hd in range(NHEAD):
                slab = dzb[:, _lanes(hd // 2)]
                dz_hd = jnp.where(first if hd % 2 == 0 else jnp.logical_not(first), slab, jnp.zeros_like(slab))
                gws_ref[hd] += _dot_nt(dz_hd, vb_q[:, _lanes(hd // 2)])
        dv = jnp.concatenate(dvs, axis=0)
        d_gng = _colsum(dv * vh)
        d_gnb = _colsum(dv)
        dvh = dv * v_ref[0:1, :]
        dpart_ref[:, WA:2 * WA] = rv * (dvh - _rowmean(dvh) - vh * _rowmean(dvh * vh))
        dcn = dyb * (sg * (1.0 + cn * (1.0 - sg)))
        d_cng = _colsum(dcn * ch)
        d_cnb = _colsum(dcn)
        dch = dcn * v_ref[3:4, :]
        dconv = rc * (dch - _rowmean(dch) - ch * _rowmean(dch * ch))
        dpart_ref[:, 2 * WA:3 * WA] = dconv
        dpart_ref[:, 3 * WA:4 * WA] = jnp.zeros((tm, WA), F32)
        d_cb = _colsum(dconv)

        @pl.when(i % tps == 0)
        def _():
            mg_ref[...] = jnp.zeros((8, D), F32)

        mg_ref[2:3, :] += d_gt
        vg_ref[1:2, :] += d_gpost
        v5g_ref[0:1, :] += d_gng
        v5g_ref[1:2, :] += d_gnb
        v5g_ref[2:3, :] += d_cb
        v5g_ref[3:4, :] += d_cng
        v5g_ref[4:5, :] += d_cnb
        v5g_ref[5:6, :] += d_goa
        v5g_ref[6:7, :] += d_gob

        @pl.when(i == nt - 1)
        def _():
            row = lax.broadcasted_iota(jnp.int32, (CHUNK, CHUNK), 0)
            col = lax.broadcasted_iota(jnp.int32, (CHUNK, CHUNK), 1)
            for hd in range(NHEAD):
                gws_ref[hd] = jnp.where(col <= row, gws_ref[hd], 0.0)
            gbs_ref[...] = lax.dot_general(e_ref[...], dbs_acc[...], (((1,), (1,)), ((), ())),
                                           precision=lax.Precision.HIGHEST, preferred_element_type=F32)

    tile = pl.BlockSpec((tm, D), lambda i: (i, 0))
    ptile = pl.BlockSpec((tm, 4 * WA), lambda i: (i, 0))
    return _call(
        core, name=name, grid=(nt,), jobs=jobs,
        in_specs=[tile, tile, pl.BlockSpec((tm, 2 * WA), lambda i: (i, 0)), pl.BlockSpec((tm, WA), lambda i: (i, 0)),
                  pl.BlockSpec((None, 8, D), lambda i: (i // tps, 0, 0)), _const_spec((8, D)), _const_spec((D, D)),
                  _const_spec((8, WA)), _const_spec((NHEAD, CHUNK, CHUNK)), _const_spec((CHUNK, WA)),
                  _const_spec((8, WA))],
        out_specs=[ptile, tile, tile, pl.BlockSpec((None, 8, D), lambda i: (i // tps, 0, 0)),
                   pl.BlockSpec((8, D), lambda i: (0, 0)), pl.BlockSpec((8, WA), lambda i: (0, 0)),
                   pl.BlockSpec((NHEAD, CHUNK, CHUNK), lambda i: (0, 0, 0)), pl.BlockSpec((8, CHUNK), lambda i: (0, 0))],
        out_shape=[jax.ShapeDtypeStruct((T, 4 * WA), F32), jax.ShapeDtypeStruct((T, D), BF16),
                   jax.ShapeDtypeStruct((T, D), BF16), jax.ShapeDtypeStruct((nb, 8, D), F32),
                   jax.ShapeDtypeStruct((8, D), F32), jax.ShapeDtypeStruct((8, WA), F32),
                   jax.ShapeDtypeStruct((NHEAD, CHUNK, CHUNK), F32), jax.ShapeDtypeStruct((8, CHUNK), F32)],
        scratch=[pltpu.VMEM((CHUNK, WA), F32)],
        args=[dxo, ym, proj, conv, mod, gvec, w_mo, v512, ws, bias_full, esel])


def _mixer_bwd_b(dxo, x, dpart, proj, mod, gvec, w_mi, cw, tm, name, jobs=()):
    T = x.shape[0]
    nt = T // tm
    nb = mod.shape[0]
    tps = nt // nb
    hpt = tm // HALO
    nh = T // HALO
    off = HALO - (CONV_K - 1)
    p = _pitch(tm)
    ext_rows = 8 * p

    def core(ins, outs, scs):
        dxo_ref, x_ref, dpart_ref, dnext_ref, ag_ref, halo_ref, mod_ref, g_ref, wmi_ref, cw_ref = ins
        dx_ref, dproj_ref, hb_ref, mg_ref, vg_ref, dcw_ref = outs
        glu_ext, dconv_ext, dglu_scr, dcw_acc = scs
        i = pl.program_id(0)
        first = i % tps == 0
        last = i % tps == tps - 1
        a = ag_ref[:, 0:WA]
        g = ag_ref[:, WA:2 * WA]
        sgg = _sigmoid(g)

        @pl.when(i == 0)
        def _():
            glu_ext[:, HALO + tm:HALO + ext_rows, :] = jnp.zeros((NSLAB, ext_rows - tm, LANES), F32)
            dconv_ext[:, HALO + tm:HALO + ext_rows, :] = jnp.zeros((NSLAB, ext_rows - tm, LANES), F32)
            dcw_acc[...] = jnp.zeros((32, 8, WA), F32)
            vg_ref[...] = jnp.zeros((8, D), F32)

        _to_slabs(glu_ext, 0, jnp.where(first, 0.0, halo_ref[:, 0:WA] * _sigmoid(halo_ref[:, WA:2 * WA])))
        _to_slabs(glu_ext, HALO, a * sgg)
        _to_slabs(dconv_ext, 0, dpart_ref[:, 2 * WA:3 * WA])
        _to_slabs(dconv_ext, tm, jnp.where(last, 0.0, dnext_ref[...]))
        sub = lax.broadcasted_iota(jnp.int32, (SUBL, LANES), 0)
        for s in range(NSLAB):
            accs = [jnp.zeros((SUBL, LANES), F32)] * CONV_K
            for v in range(p):
                dc = jnp.where(v + p * sub < tm, dconv_ext[s, pl.ds(v, 8, stride=p), :], 0.0)
                for k in range(CONV_K):
                    accs[k] = accs[k] + dc * glu_ext[s, pl.ds(v + off + k, 8, stride=p), :]
            for k in range(CONV_K):
                dcw_acc[k, :, _lanes(s)] += accs[k]
        dglu = _tap_sum(dconv_ext, dglu_scr, cw_ref, jnp.zeros((1, WA), F32), tm, lambda k: (CONV_K - 1) - k)

        @pl.when(i == nt - 1)
        def _():
            for k in range(CONV_K):
                dcw_ref[k:k + 1, :] = jnp.sum(dcw_acc[k], axis=0, keepdims=True)
            dcw_ref[CONV_K:32, :] = jnp.zeros((32 - CONV_K, WA), F32)

        da = dglu * sgg
        dgg = dglu * a * (sgg * (1.0 - sgg))
        dproj_ref[:, 0:2 * WA] = dpart_ref[:, 0:2 * WA].astype(BF16)
        dproj_ref[:, 2 * WA:3 * WA] = da.astype(BF16)
        dproj_ref[:, 3 * WA:4 * WA] = dgg.astype(BF16)
        dh = jnp.zeros((tm, D), F32)
        for j in range(NDEV):
            dh = dh + _dot_nt(dproj_ref[:, j * MB:(j + 1) * MB], wmi_ref[j])
        xv = x_ref[...]
        sc, sh = mod_ref[1:2, :], mod_ref[0:1, :]
        gpre = g_ref[0:1, :]
        r = lax.rsqrt(_rowmean(xv * xv) + EPS)
        xh = xv * r
        n = xh * gpre
        hb_ref[...] = (n * (1.0 + sc) + sh).astype(BF16)
        d_sc = _colsum(dh * n)
        d_sh = _colsum(dh)
        dn = dh * (1.0 + sc)
        d_gpre = _colsum(dn * xh)
        dxh = dn * gpre
        dx_ref[...] = dxo_ref[...] + r * (dxh - xh * _rowmean(dxh * xh))

        @pl.when(first)
        def _():
            mg_ref[...] = jnp.zeros((8, D), F32)

        mg_ref[0:1, :] += d_sh
        mg_ref[1:2, :] += d_sc
        vg_ref[0:1, :] += d_gpre

    tile = pl.BlockSpec((tm, D), lambda i: (i, 0))
    return _call(
        core, name=name, grid=(nt,), jobs=jobs,
        in_specs=[tile, tile, pl.BlockSpec((tm, 4 * WA), lambda i: (i, 0)),
                  pl.BlockSpec((HALO, WA), lambda i: (jnp.minimum((i + 1) * hpt, nh - 1), 2)),
                  pl.BlockSpec((tm, 2 * WA), lambda i: (i, 1)),
                  pl.BlockSpec((HALO, 2 * WA), lambda i: (jnp.maximum(i * hpt - 1, 0), 1)),
                  pl.BlockSpec((None, 8, D), lambda i: (i // tps, 0, 0)), _const_spec((8, D)),
                  _const_spec((NDEV, D, MB)), _const_spec((32, WA))],
        out_specs=[tile, pl.BlockSpec((tm, 4 * WA), lambda i: (i, 0)), tile,
                   pl.BlockSpec((None, 8, D), lambda i: (i // tps, 0, 0)), pl.BlockSpec((8, D), lambda i: (0, 0)),
                   pl.BlockSpec((32, WA), lambda i: (0, 0))],
        out_shape=[jax.ShapeDtypeStruct((T, D), F32), jax.ShapeDtypeStruct((T, 4 * WA), BF16),
                   jax.ShapeDtypeStruct((T, D), BF16), jax.ShapeDtypeStruct((nb, 8, D), F32),
                   jax.ShapeDtypeStruct((8, D), F32), jax.ShapeDtypeStruct((32, WA), F32)],
        scratch=[pltpu.VMEM((NSLAB, HALO + ext_rows, LANES), F32), pltpu.VMEM((NSLAB, HALO + ext_rows, LANES), F32),
                 pltpu.VMEM((NSLAB, ext_rows, LANES), F32), pltpu.VMEM((32, 8, WA), F32)],
        args=[dxo, x, dpart, dpart, proj, proj, mod, gvec, w_mi, cw])


def _grad_chip(a, b, a_spec, b_spec, prod_shape, half, name, jobs=(), via_b=False, after=None):
    steps = 8 if half is None else 4
    R = prod_shape[0] if half is None else half
    C = prod_shape[1]

    def core(ins, outs, scs):
        a_ref, b_ref = ins[:2]
        (o_ref,) = outs
        own, snd, rcv, ssem, rsem, lsem = scs
        s = pl.program_id(0)
        c = lax.axis_index("c")
        me = _me()
        sib = _flip(me, (0, 0, 1))
        if via_b:
            prod = _dot_tn(b_ref[...], a_ref[...]).T.astype(BF16)
        else:
            prod = _dot_tn(a_ref[...], b_ref[...]).astype(BF16)
        if half is None:
            q = s // 2

            @pl.when(s % 2 == c)
            def _():
                own[q] = prod

            @pl.when(s % 2 != c)
            def _():
                snd[q] = prod
                _remote(snd.at[q], rcv.at[q], ssem.at[q], rsem.at[q], sib).start()
        else:
            lo = prod[0:half, :]
            hi = prod[half:2 * half, :]
            own[s] = jnp.where(c == 0, lo, hi)
            snd[s] = jnp.where(c == 0, hi, lo)
            _remote(snd.at[s], rcv.at[s], ssem.at[s], rsem.at[s], sib).start()

        @pl.when(s == steps - 1)
        def _():
            for q4 in range(4):
                cp = _remote(snd.at[q4], rcv.at[q4], ssem.at[q4], rsem.at[q4], sib)
                cp.wait_recv()
                cp.wait_send()
                snd[q4] = (own[q4].astype(F32) + rcv[q4].astype(F32)).astype(BF16)
            out = pltpu.make_async_copy(snd, o_ref, lsem)
            out.start()
            out.wait()

    return _call(
        core, name=name, grid=(steps,), jobs=jobs, in_specs=[a_spec, b_spec] + [HBM] * (after is not None),
        out_specs=[HBM], out_shape=[jax.ShapeDtypeStruct((4, R, C), BF16)],
        scratch=[pltpu.VMEM((4, R, C), BF16), pltpu.VMEM((4, R, C), BF16), pltpu.VMEM((4, R, C), BF16),
                 pltpu.SemaphoreType.DMA((4,)), pltpu.SemaphoreType.DMA((4,)), pltpu.SemaphoreType.DMA],
        args=[a, b] + [after] * (after is not None))


def _grad_w_in(dg, hb, name, jobs=()):
    T = hb.shape[0]
    return _grad_chip(dg, hb, pl.BlockSpec((None, T, FB), lambda s: (s, 0, 0)), _const_spec((T, D)),
                      (FB, D), None, name, jobs)


def _grad_w_out(act, dyb, name, jobs=(), after=None):
    T = dyb.shape[0]
    return _grad_chip(act, dyb, pl.BlockSpec((None, T, FB), lambda s: (s, 0, 0)), _const_spec((T, D)),
                      (FB, D), FO, name, jobs, after=after)


def _grad_w_mi(hb, dproj, name, jobs=()):
    T = hb.shape[0]
    return _grad_chip(hb, dproj, _const_spec((T, D)), pl.BlockSpec((T, MB), lambda s: (0, s)),
                      (D, MB), None, name, jobs, via_b=True)


def _grad_w_mo(ycat, dym, name, jobs=()):
    T = ycat.shape[0]
    return _grad_chip(ycat, dym, pl.BlockSpec((T, 2 * MO), lambda s: (0, s)), _const_spec((T, D)),
                      (2 * MO, D), MO, name, jobs)


def _adamw_math(w, g, m, v):
    m2 = ADAM_B1 * m + (1.0 - ADAM_B1) * g
    v2 = ADAM_B2 * v + (1.0 - ADAM_B2) * (g * g)
    m_hat = m2 / (1.0 - ADAM_B1 ** ADAM_STEP)
    v_hat = v2 / (1.0 - ADAM_B2 ** ADAM_STEP)
    delta = -ADAM_LR * (m_hat / (jnp.sqrt(v_hat) + ADAM_EPS) + ADAM_WD * w)
    return delta, m2, v2


def _adamw_reduce(parts, w, m, v, tr, name, after=None):
    R, C = w.shape

    def core(ins, outs, _):
        p_ref, w_ref, m_ref, v_ref = ins[:4]
        g_ref, d_ref, m2_ref, v2_ref = outs
        g = p_ref[0].astype(F32)
        for s in range(1, 4):
            g = g + p_ref[s].astype(F32)
        g_ref[...] = g
        d_ref[...], m2_ref[...], v2_ref[...] = _adamw_math(w_ref[...], g, m_ref[...], v_ref[...])

    blk = pl.BlockSpec((tr, C), lambda i: (i, 0))
    in_specs = [pl.BlockSpec((4, tr, C), lambda i: (0, i, 0)), blk, blk, blk]
    args = [parts, w, m, v]
    if after is not None:
        in_specs.append(HBM)
        args.append(after)
    return _call(
        core, name=name, grid=(R // tr,), in_specs=in_specs,
        out_specs=[blk, blk, blk, blk], out_shape=[jax.ShapeDtypeStruct((R, C), F32)] * 4, args=args)[0]


HBM_ONLY = pl.BlockSpec(memory_space=pltpu.HBM)
SEM = pl.BlockSpec(memory_space=pltpu.SEMAPHORE)
EFFECT = pltpu.SideEffectType.DATAFLOW_SIDE_EFFECTING


def _chip_scatter_start(gs, name):
    n = len(gs)

    def body(*refs):
        g_refs, land_refs = refs[:n], refs[n:2 * n]
        ssem, rsem = refs[2 * n:2 * n + 2]
        token = refs[-1]
        me = _me()
        mq = 2 * me[0] + me[1]
        for k, f in enumerate(CHIP_FLIPS):
            p = _flip(me, f)
            for a in range(n):
                _remote(g_refs[a].at[2 * p[0] + p[1]], land_refs[a].at[mq], ssem.at[4 * a + k], rsem.at[3 * a + k], p).start()
        for a in range(n):
            pltpu.make_async_copy(g_refs[a].at[mq], land_refs[a].at[mq], ssem.at[4 * a + 3]).start()
        token[...] = jnp.zeros_like(token)

    gs = [pltpu.with_memory_space_constraint(g, pltpu.HBM) for g in gs]
    lands = [pltpu.with_memory_space_constraint(lax.empty(g.shape, g.dtype), pltpu.HBM) for g in gs]
    res = pl.pallas_call(
        body, name=name,
        out_shape=(pltpu.SemaphoreType.DMA((4 * n,)), pltpu.SemaphoreType.DMA((3 * n,)))
        + tuple(pltpu.HBM(g.shape, g.dtype) for g in gs) * 2 + (jax.ShapeDtypeStruct((SUBL, LANES), F32),),
        in_specs=(HBM_ONLY,) * (2 * n), out_specs=(SEM, SEM) + (HBM_ONLY,) * (2 * n) + (VM,),
        input_output_aliases={a: 2 + a for a in range(2 * n)},
        compiler_params=pltpu.CompilerParams(has_side_effects=EFFECT),
    )(*gs, *lands)
    return res[:-1], res[-1]


def _chip_scatter_wait(handle, after, name):
    ssem, rsem = handle[:2]
    n = (len(handle) - 2) // 2
    thru = handle[2:]

    def body(*refs):
        g_refs, land_refs = refs[:n], refs[n:2 * n]
        ssem, rsem = refs[2 * n:2 * n + 2]
        me = _me()
        mq = 2 * me[0] + me[1]
        for k, f in enumerate(CHIP_FLIPS):
            p = _flip(me, f)
            pq = 2 * p[0] + p[1]
            for a in range(n):
                _remote(g_refs[a].at[pq], land_refs[a].at[mq], ssem.at[4 * a + k], rsem.at[3 * a + k], p).wait_send()
                _remote(g_refs[a].at[mq], land_refs[a].at[pq], ssem.at[4 * a + k], rsem.at[3 * a + k], p).wait_recv()
        for a in range(n):
            pltpu.make_async_copy(g_refs[a].at[mq], land_refs[a].at[mq], ssem.at[4 * a + 3]).wait()

    res = pl.pallas_call(
        body, name=name,
        out_shape=tuple(pltpu.HBM(t.shape, t.dtype) for t in thru),
        in_specs=(HBM_ONLY,) * (2 * n) + (SEM, SEM, HBM), out_specs=(HBM_ONLY,) * (2 * n),
        input_output_aliases={a: a for a in range(2 * n)},
        compiler_params=pltpu.CompilerParams(has_side_effects=EFFECT),
    )(*thru, ssem, rsem, after)
    return list(res[:n]), list(res[n:])


def _adamw_ada(sc_all, dd, w, m, v, tr, name, after=None):
    R, C = w.shape

    def core(ins, outs, _):
        sc_ref, dd_ref, w_ref, m_ref, v_ref = ins[:5]
        g_ref, d_ref, m2_ref, v2_ref = outs
        g = _dot_tn(sc_ref[...].astype(BF16), dd_ref[...].astype(BF16))
        g_ref[...] = g
        d_ref[...], m2_ref[...], v2_ref[...] = _adamw_math(w_ref[...], g, m_ref[...], v_ref[...])

    blk = pl.BlockSpec((tr, C), lambda i: (i, 0))
    return _call(
        core, name=name, grid=(R // tr,),
        in_specs=[pl.BlockSpec((64, tr), lambda i: (0, i)), pl.BlockSpec((64, C), lambda i: (0, 0)), blk, blk, blk]
        + [HBM] * (after is not None),
        out_specs=[blk, blk, blk, blk], out_shape=[jax.ShapeDtypeStruct((R, C), F32)] * 4,
        args=[sc_all, dd, w, m, v] + [after] * (after is not None))[0]


def _adamw_small(gathered, plain, grads, wmv, emit, name, after=None):
    nw = len(grads)
    ng, npl, ne = len(gathered), len(plain), len(emit)

    def core(ins, outs, _):
        srcs = []
        for a in range(ng):
            s = ins[a][0]
            for dev in range(1, NDEV):
                s = s + ins[a][dev]
            srcs.append(s)
        srcs += [ins[ng + a][...] for a in range(npl)]
        w_refs = ins[ng + npl:]
        for e, a in enumerate(emit):
            outs[e][...] = srcs[a]
        for t in range(nw):
            src, row = grads[t]
            g = srcs[src] if row is None else srcs[src][row:row + 1, :]
            w_ref, m_ref, v_ref = w_refs[3 * t:3 * t + 3]
            g_ref, d_ref, m2_ref, v2_ref = outs[ne + 4 * t:ne + 4 * t + 4]
            g_ref[...] = g
            d_ref[...], m2_ref[...], v2_ref[...] = _adamw_math(w_ref[...], g, m_ref[...], v_ref[...])

    out_shape = [jax.ShapeDtypeStruct(gathered[a].shape[1:], F32) for a in emit]
    for t in range(nw):
        out_shape += [jax.ShapeDtypeStruct(wmv[3 * t].shape, F32)] * 4
    return _call(
        core, name=name, grid=(), in_specs=[VM] * (ng + npl + 3 * nw) + [HBM] * (after is not None),
        out_specs=[VM] * (ne + 4 * nw), out_shape=out_shape,
        args=list(gathered) + list(plain) + list(wmv) + [after] * (after is not None))[0]


def _ada_fwd(c_pad, w_ada, b_cols, cw_pad, jobs=()):
    def core(ins, outs, scs, start_jobs, finish_jobs):
        c_ref, w_ref, b_ref, cwp_ref = ins
        ada_ref, sc_ref, cw_ref = outs
        cbuf, send_buf, ssem, rsem = scs
        me = _me()
        mi = _lin(me)
        cbuf[mi] = c_ref[...]
        cw_ref[mi] = cwp_ref[...]
        peers = [_flip(me, f) for f in FLIPS]
        first = []
        for k, p in enumerate(peers):
            first.append(_remote(cbuf.at[mi], cbuf.at[mi], ssem.at[k], rsem.at[k], p))
            first.append(_remote(cw_ref.at[mi], cw_ref.at[mi], ssem.at[7 + k], rsem.at[7 + k], p))
        for cp in first:
            cp.start()
        start_jobs()
        for k, p in enumerate(peers):
            pi = _lin(p)
            _remote(cbuf.at[pi], cbuf.at[pi], ssem.at[k], rsem.at[k], p).wait_recv()
            _remote(cw_ref.at[pi], cw_ref.at[pi], ssem.at[7 + k], rsem.at[7 + k], p).wait_recv()
        c_all = cbuf[...].reshape(8 * 8, D)
        sc = c_all * _sigmoid(c_all)
        sc_ref[...] = sc
        res = _dot(sc.astype(BF16), w_ref[...].astype(BF16)) + b_ref[...]
        send_buf[...] = res.reshape(8, 8, ADA_B)
        ada_ref[mi] = send_buf[mi]
        second = []
        for k, p in enumerate(peers):
            second.append(_remote(send_buf.at[_lin(p)], ada_ref.at[mi], ssem.at[14 + k], rsem.at[14 + k], p))
        for cp in second:
            cp.start()
        finish_jobs()
        for k, p in enumerate(peers):
            _remote(send_buf.at[mi], ada_ref.at[_lin(p)], ssem.at[14 + k], rsem.at[14 + k], p).wait_recv()
        for cp in first + second:
            cp.wait_send()

    return _call(
        core, name="ada_fwd", grid=(), jobs=jobs, core_starts=True, in_specs=[VM, VM, VM, VM], out_specs=[VM, VM, VM],
        out_shape=[jax.ShapeDtypeStruct((8, 8, ADA_B), F32), jax.ShapeDtypeStruct((64, D), F32),
                   jax.ShapeDtypeStruct((8, 32, 64), F32)],
        scratch=[pltpu.VMEM((8, 8, D), F32), pltpu.VMEM((8, 8, ADA_B), F32),
                 pltpu.SemaphoreType.DMA((21,)), pltpu.SemaphoreType.DMA((21,))],
        args=[c_pad, w_ada, b_cols, cw_pad])


def _ada_bwd(dada, jobs=()):
    def core(ins, outs, scs):
        (d_ref,) = ins
        dd_ref, gb_ref = outs
        rbuf, ssem, rsem = scs
        me = _me()
        mi = _lin(me)
        peers = [_flip(me, f) for f in FLIPS]
        rbuf[mi] = d_ref[mi]
        first = []
        for k, p in enumerate(peers):
            first.append(_remote(d_ref.at[_lin(p)], rbuf.at[mi], ssem.at[k], rsem.at[k], p))
        for cp in first:
            cp.start()
        for k, p in enumerate(peers):
            _remote(d_ref.at[mi], rbuf.at[_lin(p)], ssem.at[k], rsem.at[k], p).wait_recv()
        dd = rbuf[...].reshape(64, ADA_B)
        dd_ref[...] = dd
        gb_ref[...] = jnp.broadcast_to(_colsum(dd), (8, ADA_B))
        for cp in first:
            cp.wait_send()

    return _call(
        core, name="ada_bwd", grid=(), jobs=jobs, in_specs=[VM], out_specs=[VM, VM],
        out_shape=[jax.ShapeDtypeStruct((64, ADA_B), F32), jax.ShapeDtypeStruct((8, ADA_B), F32)],
        scratch=[pltpu.VMEM((8, 8, ADA_B), F32), pltpu.SemaphoreType.DMA((7,)), pltpu.SemaphoreType.DMA((7,))],
        args=[dada])


SMALL_D = ("g_pre_f1", "g_post_f1", "g_pre_m", "g_post_m", "g_pre_f2", "g_post_f2")
SMALL_W = ("gmlp_norm_g", "gmlp_norm_b", "conv_b", "conv_norm_g", "conv_norm_b", "g_out_a", "g_out_b")


def kernel(x, c, w_ada, b_ada, g_pre_f1, g_post_f1, w_f1_in, w_f1_out, g_pre_m, g_post_m, w_mix_in, gmlp_norm_g, gmlp_norm_b, w_spatial, b_spatial, conv_w, conv_b, conv_norm_g, conv_norm_b, g_out_a, g_out_b, w_mix_out, g_pre_f2, g_post_f2, w_f2_in, w_f2_out, loss_target, m_w_ada, m_b_ada, m_g_pre_f1, m_g_post_f1, m_w_f1_in, m_w_f1_out, m_g_pre_m, m_g_post_m, m_w_mix_in, m_gmlp_norm_g, m_gmlp_norm_b, m_w_spatial, m_b_spatial, m_conv_w, m_conv_b, m_conv_norm_g, m_conv_norm_b, m_g_out_a, m_g_out_b, m_w_mix_out, m_g_pre_f2, m_g_post_f2, m_w_f2_in, m_w_f2_out, v_w_ada, v_b_ada, v_g_pre_f1, v_g_post_f1, v_w_f1_in, v_w_f1_out, v_g_pre_m, v_g_post_m, v_w_mix_in, v_gmlp_norm_g, v_gmlp_norm_b, v_w_spatial, v_b_spatial, v_conv_w, v_conv_b, v_conv_norm_g, v_conv_norm_b, v_g_out_a, v_g_out_b, v_w_mix_out, v_g_pre_f2, v_g_post_f2, v_w_f2_in, v_w_f2_out):
    given = dict(locals())
    bl, seq, _ = x.shape
    T = bl * seq
    tm = min(256, seq // 2)
    mi = _lin((lax.axis_index("x"), lax.axis_index("y"), lax.axis_index("c")))

    def shard_in(w):
        return w[0].T.astype(BF16)

    g_f1 = _RelayGather([shard_in(w_f1_in), w_f1_out[0].astype(BF16)], ("rows", "out"))
    s_f2 = shard_in(w_f2_in)
    g_mx = _Gather([w_mix_in[0].astype(BF16), w_mix_out[0].astype(BF16), w_f2_out[0].astype(BF16), s_f2[:, 0:D // 4]],
                   ("rows", "rows", "out", "rows"), late_mid=True)
    g_f2 = _Gather([s_f2[:, D // 4:D]], ("rows",))

    c_pad = jnp.pad(c, ((0, 8 - bl), (0, 0)))
    b_cols = lax.dynamic_slice(b_ada, (0, mi * ADA_B), (1, ADA_B))
    cw_pad = jnp.pad(conv_w[0], ((0, 1), (0, 0)))
    (ada_blk, sc_all, cw_all), ((wi1, wo1),) = _ada_fwd(c_pad, w_ada[0], b_cols, cw_pad, jobs=[g_f1])
    ada = ada_blk[:, 0:bl, :].transpose(1, 0, 2).reshape(bl, 9, D)
    pad5 = jnp.zeros((bl, 5, D), F32)
    mod1 = jnp.concatenate([ada[:, 0:3], pad5], axis=1)
    mod2 = jnp.concatenate([ada[:, 3:6], pad5], axis=1)
    mod3 = jnp.concatenate([ada[:, 6:9], pad5], axis=1)
    cw_full = cw_all.transpose(1, 0, 2).reshape(32, WA)

    zrow = jnp.zeros((1, D), F32)
    gv1 = jnp.concatenate([g_pre_f1, g_post_f1] + [zrow] * 6, axis=0)
    gvm = jnp.concatenate([g_pre_m, g_post_m] + [zrow] * 6, axis=0)
    gv2 = jnp.concatenate([g_pre_f2, g_post_f2] + [zrow] * 6, axis=0)
    v512 = jnp.concatenate([gmlp_norm_g, gmlp_norm_b, conv_b, conv_norm_g, conv_norm_b, g_out_a, g_out_b,
                            jnp.zeros((1, WA), F32)], axis=0)
    ws = w_spatial[0]
    bias_full = jnp.repeat(b_spatial[0].T, HD, axis=1)
    esel = (lax.broadcasted_iota(jnp.int32, (8, WA), 1) // HD == lax.broadcasted_iota(jnp.int32, (8, WA), 0)).astype(F32)

    x0 = x.reshape(T, D)
    (x1, gu1, y1), ((wmi, wmo, wo2, wi2a),) = _ffn_fwd(x0, mod1, gv1, wi1, wo1, tm, "ffn1_fwd", jobs=[g_mx])
    wmo = wmo.reshape(D, D)
    (x2, proj, ym, conv), ((wi2b,),) = _mixer_fwd(x1, mod2, gvm, wmi, wmo, v512, ws, bias_full, cw_full, tm, "mixer_fwd", jobs=[g_f2])

    (dx2, dg2, act2, hb2, dyb2, mg3, vg3, loss_blk), _ = _ffn_last(
        x2, loss_target.reshape(T, D), mod3, gv2, (wi2a, wi2b), wo2, tm, "ffn2_fwd_bwd")
    (g_wi2,), _ = _grad_w_in(dg2, hb2, "ffn2_gw_in")
    (g_wo2,), _ = _grad_w_out(act2, dyb2, "ffn2_gw_out")
    (dpart, dymb, ycat, mg2a, vgma, v5g, gws, gbs), ((p_wo2,),) = _mixer_bwd_a(
        dx2, ym, proj, conv, mod2, gvm, wmo, v512, ws, bias_full, esel, tm, "mixer_bwd_a",
        jobs=[_ChipScatter([g_wo2])])
    (dx1, dproj, hbm, mg2b, vgmb, dcw), ((p_wi2,),) = _mixer_bwd_b(
        dx2, x1, dpart, proj, mod2, gvm, wmi, cw_full, tm, "mixer_bwd_b", jobs=[_ChipScatter([g_wi2])])
    (g_wmi,), _ = _grad_w_mi(hbm, dproj, "mixer_gw_in")
    (g_wmo,), _ = _grad_w_mo(ycat, dymb, "mixer_gw_out")
    p2 = jnp.concatenate([v5g, dcw], axis=0)
    (dx0, dg1, act1, hb1, dyb1, mg1, vg1), _ = _ffn_bwd(dx1, x0, y1, gu1, mod1, gv1, wi1, wo1, tm, "ffn1_bwd")

    dada = jnp.concatenate([mg1[:, 0:3], mg2b[:, 0:2], mg2a[:, 2:3], mg3[:, 0:3]], axis=1)
    dada = dada.reshape(bl, NDEV, ADA_B).transpose(1, 0, 2)
    dada = jnp.pad(dada, ((0, 0), (0, 8 - bl), (0, 0)))
    p1 = jnp.concatenate([vg1[0:2], vgmb[0:1], vgma[1:2], vg3[0:2], loss_blk[0:1], zrow], axis=0)
    (dd_all, gb_own), ((a1,),) = _ada_bwd(dada, jobs=[_AllGather([p1])])

    (g_wi1,), ((a2, a3, a4, gb_all), (p_wmi, p_wmo)) = _grad_w_in(
        dg1, hb1, "ffn1_gw_in", jobs=[_Gather([p2, gws, gbs, gb_own], ("rows",) * 4), _ChipScatter([g_wmi, g_wmo])])
    g_bada = gb_all[:, 0, :].reshape(1, 9 * D)

    h_i1, token = _chip_scatter_start([g_wi1], "tail_start")
    (g_wo1,), _ = _grad_w_out(act1, dyb1, "ffn1_gw_out", after=token)
    h_o1, token = _chip_scatter_start([g_wo1], "tail2_start")

    res = {}
    quad = _adamw_reduce(p_wi2, w_f2_in[0].T, m_w_f2_in[0].T, v_w_f2_in[0].T, FO, "adamw_w_f2_in", after=token)
    res["w_f2_in"] = tuple(t.T[None] for t in quad)
    for nm, part, tr in (("w_f2_out", p_wo2, FO), ("w_mix_in", p_wmi, 256), ("w_mix_out", p_wmo, MO)):
        quad = _adamw_reduce(part, given[nm][0], given["m_" + nm][0], given["v_" + nm][0], tr, "adamw_" + nm, after=quad[1])
        res[nm] = tuple(t[None] for t in quad)
    quad = _adamw_ada(sc_all, dd_all, w_ada[0], m_w_ada[0], v_w_ada[0], 256, "adamw_w_ada", after=quad[1])
    res["w_ada"] = tuple(t[None] for t in quad)

    small = SMALL_D + SMALL_W + ("w_spatial", "b_spatial", "b_ada")
    grads = [(0, r) for r in range(6)] + [(1, r) for r in range(7)] + [(2, None), (3, None), (4, None)]
    wmv = []
    for nm in small:
        for pre in ("", "m_", "v_"):
            wmv.append(given[pre + nm][0] if nm in ("w_spatial", "b_spatial") else given[pre + nm])
    outs = _adamw_small([a1, a2, a3, a4], [g_bada], grads, wmv, (0, 1), "adamw_small", after=quad[1])

    _, (p_wi1,) = _chip_scatter_wait(h_i1, outs[0], "tail_wait")
    quad = _adamw_reduce(p_wi1, w_f1_in[0].T, m_w_f1_in[0].T, v_w_f1_in[0].T, FO, "adamw_w_f1_in")
    res["w_f1_in"] = tuple(t.T[None] for t in quad)
    _, (p_wo1,) = _chip_scatter_wait(h_o1, quad[1], "tail2_wait")
    quad = _adamw_reduce(p_wo1, w_f1_out[0], m_w_f1_out[0], v_w_f1_out[0], FO, "adamw_w_f1_out")
    res["w_f1_out"] = tuple(t[None] for t in quad)
    loss = outs[0][6, 0]
    for t, nm in enumerate(small):
        quad = outs[2 + 4 * t:6 + 4 * t]
        res[nm] = tuple(q[None] for q in quad) if nm in ("w_spatial", "b_spatial") else tuple(quad)
    g_cw = lax.dynamic_slice(outs[1], (8, mi * 64), (32, 64))
    wmv = [jnp.pad(given[pre + "conv_w"][0], ((0, 1), (0, 0)), constant_values=1.0 if pre == "v_" else 0.0)
           for pre in ("", "m_", "v_")]
    quad = _adamw_small([], [g_cw], [(0, None)], wmv, (), "adamw_conv_w")
    res["conv_w"] = tuple(q[0:CONV_K][None] for q in quad)

    order = ["w_ada", "b_ada", "g_pre_f1", "g_post_f1", "w_f1_in", "w_f1_out", "g_pre_m", "g_post_m", "w_mix_in",
             "gmlp_norm_g", "gmlp_norm_b", "w_spatial", "b_spatial", "conv_w", "conv_b", "conv_norm_g", "conv_norm_b",
             "g_out_a", "g_out_b", "w_mix_out", "g_pre_f2", "g_post_f2", "w_f2_in", "w_f2_out"]
    out = [loss, dx0.reshape(bl, seq, D)]
    for k in range(4):
        out += [res[nm][k] for nm in order]
    return tuple(out)
```

```python
import jax
import jax.numpy as jnp
from jax import lax
from jax.experimental import pallas as pl
from jax.experimental.pallas import tpu as pltpu

F32 = jnp.float32
BF16 = jnp.bfloat16

D = 1024
DFF = 2816
NDEV = 8
FB = 2 * DFF // NDEV
NCH = DFF // FB
LANES = 128
SUBL = 8
FO = DFF // NDEV
WA = 512
NSLAB = WA // LANES
NHEAD = 8
HD = 64
CHUNK = 128
CONV_K = 31
HALO = 32
MB = 2 * (WA + WA) // NDEV
MO = D // NDEV
ADA_B = 9 * D // NDEV
EPS = 1e-6
HALF = 0.5

ADAM_LR = 0.001
ADAM_B1 = 0.9
ADAM_B2 = 0.999
ADAM_EPS = 1e-08
ADAM_WD = 0.01
ADAM_STEP = 10

VMEM_LIMIT = 56 * 1024 * 1024
MESH = pl.DeviceIdType.MESH
FLIPS = ((0, 0, 1), (1, 0, 0), (0, 1, 0), (1, 1, 0), (1, 0, 1), (0, 1, 1), (1, 1, 1))
CHIP_FLIPS = ((1, 0, 0), (0, 1, 0), (1, 1, 0))
HBM = pl.BlockSpec(memory_space=pl.ANY)
VM = pl.BlockSpec(memory_space=pltpu.VMEM)


def _dot(a, b):
    return lax.dot_general(a, b, (((1,), (0,)), ((), ())), preferred_element_type=F32)


def _dot_nt(a, b):
    return lax.dot_general(a, b, (((1,), (1,)), ((), ())), preferred_element_type=F32)


def _dot_tn(a, b):
    return lax.dot_general(a, b, (((0,), (0,)), ((), ())), preferred_element_type=F32)


def _rowmean(v):
    return jnp.mean(v, axis=-1, keepdims=True)


def _colsum(v):
    return jnp.sum(v, axis=0, keepdims=True)


def _sigmoid(v):
    return 0.5 * jnp.tanh(0.5 * v) + 0.5


def _const_spec(shape):
    nd = len(shape)
    return pl.BlockSpec(shape, lambda *_: (0,) * nd, pipeline_mode=pl.Buffered(1))


def _me():
    return lax.axis_index("x"), lax.axis_index("y"), lax.axis_index("c")


def _flip(me, f):
    return tuple(1 - v if b else v for v, b in zip(me, f))


def _lin(p):
    return 4 * p[0] + 2 * p[1] + p[2]


def _remote(src, dst, send_sem, recv_sem, dev):
    return pltpu.make_async_remote_copy(src_ref=src, dst_ref=dst, send_sem=send_sem, recv_sem=recv_sem,
                                        device_id=dev, device_id_type=MESH)


def _blk(kind, ref, p):
    if kind == "out":
        return ref.at[2 * p[0] + p[1], pl.ds(p[2] * FO, FO), :]
    return ref.at[_lin(p)]


class _Gather:
    def __init__(self, shards, kinds, late_mid=False):
        self.late_mid = late_mid
        self.kinds = kinds
        self.n = len(shards)
        self.ins = list(shards)
        self.out_shape = [jax.ShapeDtypeStruct((4, FB, D) if k == "out" else (NDEV,) + s.shape, s.dtype)
                          for s, k in zip(shards, kinds)]
        self.sems = [pltpu.SemaphoreType.DMA((7 * self.n,)), pltpu.SemaphoreType.DMA((7 * self.n,)),
                     pltpu.SemaphoreType.DMA((self.n,))]

    def _first(self, ins, outs, sems):
        ssem, rsem, lsem = sems
        me = _me()
        sib = _flip(me, (0, 0, 1))
        cps, loc = [], []
        for a in range(self.n):
            mine = _blk(self.kinds[a], outs[a], me)
            loc.append(pltpu.make_async_copy(ins[a], mine, lsem.at[a]))
            cps.append(_remote(ins[a], mine, ssem.at[7 * a], rsem.at[7 * a], sib))
            for j, f in enumerate(CHIP_FLIPS):
                cps.append(_remote(ins[a], mine, ssem.at[7 * a + 1 + j], rsem.at[7 * a + 1 + j], _flip(me, f)))
        return cps, loc

    def _passed(self, outs, sems):
        ssem, rsem, _ = sems
        me = _me()
        sib = _flip(me, (0, 0, 1))
        cps = []
        for j, f in enumerate(CHIP_FLIPS):
            for a in range(self.n):
                blk = _blk(self.kinds[a], outs[a], _flip(me, f))
                cps.append(_remote(blk, blk, ssem.at[7 * a + 4 + j], rsem.at[7 * a + 4 + j], sib))
        return cps

    def start(self, ins, outs, sems):
        cps, loc = self._first(ins, outs, sems)
        for cp in loc + cps:
            cp.start()

    def mid(self, ins, outs, sems):
        ssem, rsem, _ = sems
        me = _me()
        passed = self._passed(outs, sems)
        t = 0
        for j, f in enumerate(CHIP_FLIPS):
            for a in range(self.n):
                blk = _blk(self.kinds[a], outs[a], _flip(me, f))
                _remote(blk, blk, ssem.at[7 * a + 1 + j], rsem.at[7 * a + 1 + j], _flip(me, f)).wait_recv()
                passed[t].start()
                t += 1

    def end(self, ins, outs, sems):
        ssem, rsem, _ = sems
        me = _me()
        sib = _flip(me, (0, 0, 1))
        for a in range(self.n):
            blk = _blk(self.kinds[a], outs[a], sib)
            _remote(blk, blk, ssem.at[7 * a], rsem.at[7 * a], sib).wait_recv()
            for j, f in enumerate(CHIP_FLIPS):
                blk = _blk(self.kinds[a], outs[a], _flip(_flip(me, f), (0, 0, 1)))
                _remote(blk, blk, ssem.at[7 * a + 4 + j], rsem.at[7 * a + 4 + j], sib).wait_recv()
        cps, loc = self._first(ins, outs, sems)
        for cp in cps + self._passed(outs, sems):
            cp.wait_send()
        for cp in loc:
            cp.wait()


class _RelayGather(_Gather):
    def _peers(self):
        me = _me()
        c = me[2]
        to = (me[0] + (1 - c) - 2 * me[0] * (1 - c), me[1] + c - 2 * me[1] * c, c)
        frm = (me[0] + c - 2 * me[0] * c, me[1] + (1 - c) - 2 * me[1] * (1 - c), c)
        return me, _flip(me, (0, 0, 1)), to, frm, _flip(me, (1, 1, 0))

    def _first(self, ins, outs, sems):
        ssem, rsem, lsem = sems
        me, sib, to, frm, _ = self._peers()
        cps, loc = [], []
        for a in range(self.n):
            mine = _blk(self.kinds[a], outs[a], me)
            loc.append(pltpu.make_async_copy(ins[a], mine, lsem.at[a]))
            for slot, dev in ((0, sib), (1, to), (2, frm)):
                cps.append(_remote(ins[a], mine, ssem.at[7 * a + slot], rsem.at[7 * a + slot], dev))
        return cps, loc

    def _block_copy(self, outs, sems, a, slot, owner, dev):
        ssem, rsem, _ = sems
        blk = _blk(self.kinds[a], outs[a], owner)
        return _remote(blk, blk, ssem.at[7 * a + slot], rsem.at[7 * a + slot], dev)

    def mid(self, ins, outs, sems):
        me, sib, to, frm, _ = self._peers()
        for a in range(self.n):
            self._block_copy(outs, sems, a, 2, frm, frm).wait_recv()
            self._block_copy(outs, sems, a, 3, frm, to).start()
            self._block_copy(outs, sems, a, 5, frm, sib).start()
        for a in range(self.n):
            self._block_copy(outs, sems, a, 1, to, to).wait_recv()
            self._block_copy(outs, sems, a, 4, to, sib).start()

    def end(self, ins, outs, sems):
        me, sib, to, frm, far = self._peers()
        up = (0, 0, 1)
        for a in range(self.n):
            self._block_copy(outs, sems, a, 3, far, to).wait_recv()
            self._block_copy(outs, sems, a, 6, far, sib).start()
        for a in range(self.n):
            for slot, owner in ((0, sib), (4, _flip(frm, up)), (5, _flip(to, up)), (6, _flip(far, up))):
                self._block_copy(outs, sems, a, slot, owner, sib).wait_recv()
        cps, loc = self._first(ins, outs, sems)
        for a in range(self.n):
            cps += [self._block_copy(outs, sems, a, 3, frm, to), self._block_copy(outs, sems, a, 4, to, sib),
                    self._block_copy(outs, sems, a, 5, frm, sib), self._block_copy(outs, sems, a, 6, far, sib)]
        for cp in cps:
            cp.wait_send()
        for cp in loc:
            cp.wait()


class _ChipScatter:
    def __init__(self, grads):
        self.n = len(grads)
        self.ins = list(grads)
        self.out_shape = [jax.ShapeDtypeStruct(g.shape, BF16) for g in grads]
        self.sems = [pltpu.SemaphoreType.DMA((3 * self.n,)), pltpu.SemaphoreType.DMA((3 * self.n,)),
                     pltpu.SemaphoreType.DMA((self.n,))]

    def _copies(self, ins, outs, sems):
        ssem, rsem, lsem = sems
        me = _me()
        mq = 2 * me[0] + me[1]
        loc = [pltpu.make_async_copy(ins[a].at[mq], outs[a].at[mq], lsem.at[a]) for a in range(self.n)]
        cps = []
        for k, f in enumerate(CHIP_FLIPS):
            p = _flip(me, f)
            for a in range(self.n):
                cps.append(_remote(ins[a].at[2 * p[0] + p[1]], outs[a].at[mq], ssem.at[3 * a + k], rsem.at[3 * a + k], p))
        return cps, loc

    def start(self, ins, outs, sems):
        cps, loc = self._copies(ins, outs, sems)
        for cp in loc + cps:
            cp.start()

    mid = None

    def end(self, ins, outs, sems):
        ssem, rsem, _ = sems
        me = _me()
        mq = 2 * me[0] + me[1]
        for k, f in enumerate(CHIP_FLIPS):
            p = _flip(me, f)
            for a in range(self.n):
                _remote(ins[a].at[mq], outs[a].at[2 * p[0] + p[1]], ssem.at[3 * a + k], rsem.at[3 * a + k], p).wait_recv()
        cps, loc = self._copies(ins, outs, sems)
        for cp in cps:
            cp.wait_send()
        for cp in loc:
            cp.wait()


class _AllGather:
    def __init__(self, parts):
        self.n = len(parts)
        self.ins = list(parts)
        self.out_shape = [jax.ShapeDtypeStruct((NDEV,) + p.shape, p.dtype) for p in parts]
        self.sems = [pltpu.SemaphoreType.DMA((7 * self.n,)), pltpu.SemaphoreType.DMA((7 * self.n,)),
                     pltpu.SemaphoreType.DMA((self.n,))]

    def _copies(self, ins, outs, sems):
        ssem, rsem, lsem = sems
        me = _me()
        mi = _lin(me)
        loc = [pltpu.make_async_copy(ins[a], outs[a].at[mi], lsem.at[a]) for a in range(self.n)]
        cps = []
        for k, f in enumerate(FLIPS):
            for a in range(self.n):
                cps.append(_remote(ins[a], outs[a].at[mi], ssem.at[7 * a + k], rsem.at[7 * a + k], _flip(me, f)))
        return cps, loc

    def start(self, ins, outs, sems):
        cps, loc = self._copies(ins, outs, sems)
        for cp in loc + cps:
            cp.start()

    mid = None

    def end(self, ins, outs, sems):
        ssem, rsem, _ = sems
        me = _me()
        for k, f in enumerate(FLIPS):
            p = _flip(me, f)
            for a in range(self.n):
                _remote(ins[a], outs[a].at[_lin(p)], ssem.at[7 * a + k], rsem.at[7 * a + k], p).wait_recv()
        cps, loc = self._copies(ins, outs, sems)
        for cp in cps:
            cp.wait_send()
        for cp in loc:
            cp.wait()


def _call(core, *, name, grid, in_specs, out_specs, out_shape, args, scratch=(), jobs=(), core_starts=False):
    n_in, n_out, n_sc = len(in_specs), len(out_specs), len(scratch)
    steps = 1
    for g in grid:
        steps *= g

    def body(*refs):
        pos = [0]

        def take(k):
            r = refs[pos[0]:pos[0] + k]
            pos[0] += k
            return r

        ins = take(n_in)
        j_ins = [take(len(j.ins)) for j in jobs]
        outs = take(n_out)
        j_outs = [take(len(j.out_shape)) for j in jobs]
        scs = take(n_sc)
        j_sems = [take(len(j.sems)) for j in jobs]
        if len(grid) == 2:
            step = pl.program_id(0) * grid[1] + pl.program_id(1)
        elif len(grid) == 1:
            step = pl.program_id(0)
        else:
            step = 0
        def start_jobs():
            for j, ji, jo, js in zip(jobs, j_ins, j_outs, j_sems):
                j.start(ji, jo, js)

        if grid:
            pl.when(step == 0)(start_jobs)
        elif not core_starts:
            start_jobs()
        for j, ji, jo, js in zip(jobs, j_ins, j_outs, j_sems):
            if j.mid is not None and grid:
                at = max(steps - 2, 0) if j.late_mid else (3 * steps) // 4
                pl.when(step == at)(lambda j=j, ji=ji, jo=jo, js=js: j.mid(ji, jo, js))
        def finish_jobs():
            for j, ji, jo, js in zip(jobs, j_ins, j_outs, j_sems):
                if j.mid is not None:
                    j.mid(ji, jo, js)
                j.end(ji, jo, js)

        if core_starts:
            core(ins, outs, scs, start_jobs, finish_jobs)
        elif core is not None:
            core(ins, outs, scs)
        if grid:
            for j, ji, jo, js in zip(jobs, j_ins, j_outs, j_sems):
                pl.when(step == steps - 1)(lambda j=j, ji=ji, jo=jo, js=js: j.end(ji, jo, js))
        elif not core_starts:
            finish_jobs()

    all_in = list(in_specs)
    all_args = list(args)
    all_out = list(out_specs)
    all_shape = list(out_shape)
    all_sc = list(scratch)
    for j in jobs:
        all_in += [HBM] * len(j.ins)
        all_args += j.ins
    for j in jobs:
        all_out += [HBM] * len(j.out_shape)
        all_shape += j.out_shape
        all_sc += j.sems
    params = dict(vmem_limit_bytes=VMEM_LIMIT)
    if grid:
        params["dimension_semantics"] = ("arbitrary",) * len(grid)
    res = pl.pallas_call(
        body, name=name, grid=grid, in_specs=all_in, out_specs=all_out, out_shape=all_shape,
        scratch_shapes=all_sc, compiler_params=pltpu.CompilerParams(**params),
    )(*all_args)
    core_res = list(res[:n_out])
    job_res = []
    pos = n_out
    for j in jobs:
        job_res.append(list(res[pos:pos + len(j.out_shape)]))
        pos += len(j.out_shape)
    return core_res, job_res


def _ffn_fwd(x, mod, gvec, w_in, w_out, tm, name, jobs=()):
    T = x.shape[0]
    nt = T // tm
    tps = nt // mod.shape[0]

    def core(ins, outs, _):
        x_ref, mod_ref, g_ref, win_ref, wout_ref = ins
        xo_ref, gu_ref, y_ref = outs
        xv = x_ref[...]
        sh, sc, gt = mod_ref[0:1, :], mod_ref[1:2, :], mod_ref[2:3, :]
        r = lax.rsqrt(_rowmean(xv * xv) + EPS)
        h = (xv * r * g_ref[0:1, :]) * (1.0 + sc) + sh
        hb = h.astype(BF16)
        y = jnp.zeros((tm, D), F32)
        for cidx in range(NCH):
            gate = _dot_nt(hb, win_ref[cidx])
            up = _dot_nt(hb, win_ref[NCH + cidx])
            gu_ref[cidx] = gate.astype(BF16)
            gu_ref[NCH + cidx] = up.astype(BF16)
            act = gate * _sigmoid(gate) * up
            y = y + _dot(act.astype(BF16), wout_ref[cidx])
        y_ref[...] = y
        ry = lax.rsqrt(_rowmean(y * y) + EPS)
        xo_ref[...] = xv + (HALF * gt) * (y * ry * g_ref[1:2, :])

    tile = pl.BlockSpec((tm, D), lambda i: (i, 0))
    return _call(
        core, name=name, grid=(nt,), jobs=jobs,
        in_specs=[tile, pl.BlockSpec((None, 8, D), lambda i: (i // tps, 0, 0)), _const_spec((8, D)),
                  _const_spec((8, FB, D)), _const_spec((4, FB, D))],
        out_specs=[tile, pl.BlockSpec((8, tm, FB), lambda i: (0, i, 0)), tile],
        out_shape=[jax.ShapeDtypeStruct((T, D), F32), jax.ShapeDtypeStruct((8, T, FB), BF16),
                   jax.ShapeDtypeStruct((T, D), F32)],
        args=[x, mod, gvec, w_in, w_out])


def _ffn_bwd(dxo, x, y, gu, mod, gvec, w_in, w_out, tm, name, jobs=()):
    T = x.shape[0]
    nt = T // tm
    nb = mod.shape[0]
    tps = nt // nb

    def core(ins, outs, _):
        dxo_ref, x_ref, y_ref, gu_ref, mod_ref, g_ref, win_ref, wout_ref = ins
        dx_ref, dg_ref, act_ref, hb_ref, dyb_ref, mg_ref, vg_ref = outs
        i = pl.program_id(0)
        xv = x_ref[...]
        dxo_v = dxo_ref[...]
        yv = y_ref[...]
        sh, sc, gt = mod_ref[0:1, :], mod_ref[1:2, :], mod_ref[2:3, :]
        gpre, gpost = g_ref[0:1, :], g_ref[1:2, :]
        r = lax.rsqrt(_rowmean(xv * xv) + EPS)
        xh = xv * r
        n = xh * gpre
        hb = (n * (1.0 + sc) + sh).astype(BF16)
        hb_ref[...] = hb
        ry = lax.rsqrt(_rowmean(yv * yv) + EPS)
        yh = yv * ry
        d_gt = _colsum(HALF * dxo_v * (yh * gpost))
        dp = (HALF * gt) * dxo_v
        d_gpost = _colsum(dp * yh)
        dyh = dp * gpost
        dy = ry * (dyh - yh * _rowmean(dyh * yh))
        dyb = dy.astype(BF16)
        dyb_ref[...] = dyb
        dh = jnp.zeros((tm, D), F32)
        for cidx in range(NCH):
            gate = gu_ref[cidx].astype(F32)
            up = gu_ref[NCH + cidx].astype(F32)
            sig = _sigmoid(gate)
            s = gate * sig
            act_ref[cidx] = (s * up).astype(BF16)
            d_act = _dot_nt(dyb, wout_ref[cidx])
            d_up = (d_act * s).astype(BF16)
            d_gate = (d_act * up * (sig * (1.0 + gate * (1.0 - sig)))).astype(BF16)
            dg_ref[cidx] = d_gate
            dg_ref[NCH + cidx] = d_up
            dh = dh + _dot(d_gate, win_ref[cidx]) + _dot(d_up, win_ref[NCH + cidx])
        d_sc = _colsum(dh * n)
        d_sh = _colsum(dh)
        dn = dh * (1.0 + sc)
        d_gpre = _colsum(dn * xh)
        dxh = dn * gpre
        dx_ref[...] = dxo_v + r * (dxh - xh * _rowmean(dxh * xh))

        @pl.when(i % tps == 0)
        def _():
            mg_ref[...] = jnp.zeros((8, D), F32)

        @pl.when(i == 0)
        def _():
            vg_ref[...] = jnp.zeros((8, D), F32)

        mg_ref[0:1, :] += d_sh
        mg_ref[1:2, :] += d_sc
        mg_ref[2:3, :] += d_gt
        vg_ref[0:1, :] += d_gpre
        vg_ref[1:2, :] += d_gpost

    tile = pl.BlockSpec((tm, D), lambda i: (i, 0))
    return _call(
        core, name=name, grid=(nt,), jobs=jobs,
        in_specs=[tile, tile, tile, pl.BlockSpec((8, tm, FB), lambda i: (0, i, 0)),
                  pl.BlockSpec((None, 8, D), lambda i: (i // tps, 0, 0)), _const_spec((8, D)),
                  _const_spec((8, FB, D)), _const_spec((4, FB, D))],
        out_specs=[tile, pl.BlockSpec((8, tm, FB), lambda i: (0, i, 0)),
                   pl.BlockSpec((4, tm, FB), lambda i: (0, i, 0)), tile, tile,
                   pl.BlockSpec((None, 8, D), lambda i: (i // tps, 0, 0)), pl.BlockSpec((8, D), lambda i: (0, 0))],
        out_shape=[jax.ShapeDtypeStruct((T, D), F32), jax.ShapeDtypeStruct((8, T, FB), BF16),
                   jax.ShapeDtypeStruct((4, T, FB), BF16), jax.ShapeDtypeStruct((T, D), BF16),
                   jax.ShapeDtypeStruct((T, D), BF16), jax.ShapeDtypeStruct((nb, 8, D), F32),
                   jax.ShapeDtypeStruct((8, D), F32)],
        args=[dxo, x, y, gu, mod, gvec, w_in, w_out])


def _ffn_last(x, target, mod, gvec, w_in, w_out, tm, name, jobs=()):
    T = x.shape[0]
    nt = T // tm
    nb = mod.shape[0]
    tps = nt // nb

    def core(ins, outs, scs):
        x_ref, t_ref, mod_ref, g_ref, wina_ref, winb_ref, wout_ref = ins
        dx_ref, dg_ref, act_ref, hb_ref, dyb_ref, mg_ref, vg_ref, loss_ref = outs
        hd2 = w_in[0].shape[2]
        (gu_s,) = scs
        i = pl.program_id(0)
        xv = x_ref[...]
        sh, sc, gt = mod_ref[0:1, :], mod_ref[1:2, :], mod_ref[2:3, :]
        gpre, gpost = g_ref[0:1, :], g_ref[1:2, :]
        r = lax.rsqrt(_rowmean(xv * xv) + EPS)
        xh = xv * r
        n = xh * gpre
        hb = (n * (1.0 + sc) + sh).astype(BF16)
        hb_ref[...] = hb
        hba, hbb = hb[:, 0:hd2], hb[:, hd2:D]
        yv = jnp.zeros((tm, D), F32)
        for cidx in range(NCH):
            gate = _dot_nt(hba, wina_ref[cidx]) + _dot_nt(hbb, winb_ref[cidx])
            up = _dot_nt(hba, wina_ref[NCH + cidx]) + _dot_nt(hbb, winb_ref[NCH + cidx])
            gu_s[cidx] = gate.astype(BF16)
            gu_s[NCH + cidx] = up.astype(BF16)
            act = gate * _sigmoid(gate) * up
            act_ref[cidx] = act.astype(BF16)
            yv = yv + _dot(act_ref[cidx], wout_ref[cidx])
        ry = lax.rsqrt(_rowmean(yv * yv) + EPS)
        yh = yv * ry
        pn = yh * gpost
        err = xv + (HALF * gt) * pn - t_ref[...]
        dxo_v = err * (1.0 / D)
        d_gt = _colsum(HALF * dxo_v * pn)
        dp = (HALF * gt) * dxo_v
        d_gpost = _colsum(dp * yh)
        dyh = dp * gpost
        dyb = (ry * (dyh - yh * _rowmean(dyh * yh))).astype(BF16)
        dyb_ref[...] = dyb
        dha = jnp.zeros((tm, hd2), F32)
        dhb = jnp.zeros((tm, D - hd2), F32)
        for cidx in range(NCH):
            gate = gu_s[cidx].astype(F32)
            up = gu_s[NCH + cidx].astype(F32)
            sig = _sigmoid(gate)
            s = gate * sig
            d_act = _dot_nt(dyb, wout_ref[cidx])
            d_up = (d_act * s).astype(BF16)
            d_gate = (d_act * up * (sig * (1.0 + gate * (1.0 - sig)))).astype(BF16)
            dg_ref[cidx] = d_gate
            dg_ref[NCH + cidx] = d_up
            dha = dha + _dot(d_gate, wina_ref[cidx]) + _dot(d_up, wina_ref[NCH + cidx])
            dhb = dhb + _dot(d_gate, winb_ref[cidx]) + _dot(d_up, winb_ref[NCH + cidx])
        dh = jnp.concatenate([dha, dhb], axis=1)
        d_sc = _colsum(dh * n)
        d_sh = _colsum(dh)
        dn = dh * (1.0 + sc)
        d_gpre = _colsum(dn * xh)
        dxh = dn * gpre
        dx_ref[...] = dxo_v + r * (dxh - xh * _rowmean(dxh * xh))

        @pl.when(i % tps == 0)
        def _():
            mg_ref[...] = jnp.zeros((8, D), F32)

        @pl.when(i == 0)
        def _():
            vg_ref[...] = jnp.zeros((8, D), F32)
            loss_ref[...] = jnp.zeros((8, D), F32)

        mg_ref[0:1, :] += d_sh
        mg_ref[1:2, :] += d_sc
        mg_ref[2:3, :] += d_gt
        vg_ref[0:1, :] += d_gpre
        vg_ref[1:2, :] += d_gpost
        loss_ref[...] += HALF * jnp.sum(_rowmean(err * err), axis=0, keepdims=True)

    tile = pl.BlockSpec((tm, D), lambda i: (i, 0))
    return _call(
        core, name=name, grid=(nt,), jobs=jobs,
        in_specs=[tile, tile, pl.BlockSpec((None, 8, D), lambda i: (i // tps, 0, 0)), _const_spec((8, D)),
                  _const_spec(w_in[0].shape), _const_spec(w_in[1].shape), _const_spec((4, FB, D))],
        out_specs=[tile, pl.BlockSpec((8, tm, FB), lambda i: (0, i, 0)),
                   pl.BlockSpec((4, tm, FB), lambda i: (0, i, 0)), tile, tile,
                   pl.BlockSpec((None, 8, D), lambda i: (i // tps, 0, 0)), pl.BlockSpec((8, D), lambda i: (0, 0)),
                   pl.BlockSpec((8, D), lambda i: (0, 0))],
        out_shape=[jax.ShapeDtypeStruct((T, D), F32), jax.ShapeDtypeStruct((8, T, FB), BF16),
                   jax.ShapeDtypeStruct((4, T, FB), BF16), jax.ShapeDtypeStruct((T, D), BF16),
                   jax.ShapeDtypeStruct((T, D), BF16), jax.ShapeDtypeStruct((nb, 8, D), F32),
                   jax.ShapeDtypeStruct((8, D), F32), jax.ShapeDtypeStruct((8, D), F32)],
        scratch=[pltpu.VMEM((8, tm, FB), BF16)],
        args=[x, target, mod, gvec, w_in[0], w_in[1], w_out])


def _masked_spatial(ws_ref):
    row = lax.broadcasted_iota(jnp.int32, (CHUNK, CHUNK), 0)
    col = lax.broadcasted_iota(jnp.int32, (CHUNK, CHUNK), 1)
    keep = col <= row
    return [jnp.where(keep, ws_ref[hd], 0.0).astype(BF16) for hd in range(NHEAD)]


def _head_pairs(mats, right, transpose=False):
    first = lax.broadcasted_iota(jnp.int32, (CHUNK, LANES), 1) < HD
    op = _dot_tn if transpose else _dot
    out = []
    for p in range(NHEAD // 2):
        slab = right[:, _lanes(p)]
        out.append(jnp.where(first, op(mats[2 * p], slab), op(mats[2 * p + 1], slab)))
    return jnp.concatenate(out, axis=1)


def _spatial_gate(wm, vb_chunk):
    return _head_pairs(wm, vb_chunk)


def _layer_norm_stats(v):
    mu = _rowmean(v)
    vc = v - mu
    rstd = lax.rsqrt(_rowmean(vc * vc) + EPS)
    return vc * rstd, rstd


def _pitch(tm):
    p = tm // 8
    while p % 8 != 4:
        p += 1
    return p


def _lanes(s):
    return slice(s * LANES, (s + 1) * LANES)


def _to_slabs(ref, row0, val):
    for s in range(NSLAB):
        ref[s, row0:row0 + val.shape[0], :] = val[:, _lanes(s)]


def _tap_sum(src, out, cw_ref, bias, tm, start):
    p = _pitch(tm)
    for s in range(NSLAB):
        accs = [jnp.broadcast_to(bias[:, _lanes(s)], (SUBL, LANES))] * p
        for k in range(CONV_K):
            w = jnp.broadcast_to(cw_ref[k:k + 1, _lanes(s)], (SUBL, LANES))
            for v in range(p):
                accs[v] = accs[v] + w * src[s, pl.ds(v + start(k), 8, stride=p), :]
        for v in range(p):
            out[s, pl.ds(v, 8, stride=p), :] = accs[v]
    return jnp.concatenate([out[s, 0:tm, :] for s in range(NSLAB)], axis=1)


def _mixer_fwd(x, mod, gvec, w_mi, w_mo, v512, ws, bias_full, cw, tm, name, jobs=()):
    T = x.shape[0]
    nt = T // tm
    tps = nt // mod.shape[0]
    ext_rows = 8 * _pitch(tm)

    def core(ins, outs, scs):
        x_ref, mod_ref, g_ref, wmi_ref, wmo_ref, v_ref, ws_ref, bias_ref, cw_ref = ins
        xo_ref, proj_ref, ym_ref, conv_ref = outs
        glu_ext, conv_scr = scs
        i = pl.program_id(0)
        xv = x_ref[...]
        sh, sc, gt = mod_ref[0:1, :], mod_ref[1:2, :], mod_ref[2:3, :]
        r = lax.rsqrt(_rowmean(xv * xv) + EPS)
        hb = ((xv * r * g_ref[0:1, :]) * (1.0 + sc) + sh).astype(BF16)
        for j in range(NDEV):
            proj_ref[:, j * MB:(j + 1) * MB] = _dot(hb, wmi_ref[j])
        u = proj_ref[:, 0:WA]
        v0 = proj_ref[:, WA:2 * WA]
        a = proj_ref[:, 2 * WA:3 * WA]
        g = proj_ref[:, 3 * WA:4 * WA]
        vh, _ = _layer_norm_stats(v0)
        vb = (vh * v_ref[0:1, :] + v_ref[1:2, :]).astype(BF16)
        wm = _masked_spatial(ws_ref)
        ya = []
        for q in range(tm // CHUNK):
            z = _spatial_gate(wm, vb[q * CHUNK:(q + 1) * CHUNK, :]) + bias_ref[...]
            ya.append(u[q * CHUNK:(q + 1) * CHUNK, :] * z)
        ya = jnp.concatenate(ya, axis=0)
        glu = a * _sigmoid(g)

        @pl.when(i == 0)
        def _():
            glu_ext[:, HALO + tm:HALO + ext_rows, :] = jnp.zeros((NSLAB, ext_rows - tm, LANES), F32)

        @pl.when(i % tps == 0)
        def _():
            glu_ext[:, 0:HALO, :] = jnp.zeros((NSLAB, HALO, LANES), F32)

        _to_slabs(glu_ext, HALO, glu)
        conv = _tap_sum(glu_ext, conv_scr, cw_ref, v_ref[2:3, :], tm, lambda k: HALO - (CONV_K - 1) + k)
        conv_ref[...] = conv
        glu_ext[:, 0:HALO, :] = glu_ext[:, tm:tm + HALO, :]
        ch, _ = _layer_norm_stats(conv)
        cn = ch * v_ref[3:4, :] + v_ref[4:5, :]
        yb = cn * _sigmoid(cn)
        pa = ya * lax.rsqrt(_rowmean(ya * ya) + EPS) * v_ref[5:6, :]
        pb = yb * lax.rsqrt(_rowmean(yb * yb) + EPS) * v_ref[6:7, :]
        ycat = jnp.concatenate([pa, pb], axis=1).astype(BF16)
        ym = _dot(ycat, wmo_ref[...])
        ym_ref[...] = ym
        rm = lax.rsqrt(_rowmean(ym * ym) + EPS)
        xo_ref[...] = xv + gt * (ym * rm * g_ref[1:2, :])

    tile = pl.BlockSpec((tm, D), lambda i: (i, 0))
    return _call(
        core, name=name, grid=(nt,), jobs=jobs,
        in_specs=[tile, pl.BlockSpec((None, 8, D), lambda i: (i // tps, 0, 0)), _const_spec((8, D)),
                  _const_spec((NDEV, D, MB)), _const_spec((D, D)), _const_spec((8, WA)),
                  _const_spec((NHEAD, CHUNK, CHUNK)), _const_spec((CHUNK, WA)), _const_spec((32, WA))],
        out_specs=[tile, pl.BlockSpec((tm, 4 * WA), lambda i: (i, 0)), tile, pl.BlockSpec((tm, WA), lambda i: (i, 0))],
        out_shape=[jax.ShapeDtypeStruct((T, D), F32), jax.ShapeDtypeStruct((T, 4 * WA), F32),
                   jax.ShapeDtypeStruct((T, D), F32), jax.ShapeDtypeStruct((T, WA), F32)],
        scratch=[pltpu.VMEM((NSLAB, HALO + ext_rows, LANES), F32), pltpu.VMEM((NSLAB, ext_rows, LANES), F32)],
        args=[x, mod, gvec, w_mi, w_mo, v512, ws, bias_full, cw])


def _mixer_bwd_a(dxo, ym, proj, conv, mod, gvec, w_mo, v512, ws, bias_full, esel, tm, name, jobs=()):
    T = dxo.shape[0]
    nt = T // tm
    nb = mod.shape[0]
    tps = nt // nb

    def core(ins, outs, scs):
        dxo_ref, ym_ref, proj_ref, conv_ref, mod_ref, g_ref, wmo_ref, v_ref, ws_ref, bias_ref, e_ref = ins
        dpart_ref, dymb_ref, ycat_ref, mg_ref, vg_ref, v5g_ref, gws_ref, gbs_ref = outs
        (dbs_acc,) = scs
        i = pl.program_id(0)
        dxo_v = dxo_ref[...]
        ymv = ym_ref[...]
        gt = mod_ref[2:3, :]
        gpost = g_ref[1:2, :]
        rm = lax.rsqrt(_rowmean(ymv * ymv) + EPS)
        ymh = ymv * rm
        d_gt = _colsum(dxo_v * (ymh * gpost))
        dpm = gt * dxo_v
        d_gpost = _colsum(dpm * ymh)
        dymh = dpm * gpost
        dym = (rm * (dymh - ymh * _rowmean(dymh * ymh))).astype(BF16)
        dymb_ref[...] = dym
        dycat = _dot_nt(dym, wmo_ref[...])
        u = proj_ref[:, 0:WA]
        v0 = proj_ref[:, WA:2 * WA]
        vh, rv = _layer_norm_stats(v0)
        vb = (vh * v_ref[0:1, :] + v_ref[1:2, :]).astype(BF16)
        wm = _masked_spatial(ws_ref)
        zs = []
        for q in range(tm // CHUNK):
            zs.append(_spatial_gate(wm, vb[q * CHUNK:(q + 1) * CHUNK, :]) + bias_ref[...])
        z = jnp.concatenate(zs, axis=0)
        ya = u * z
        ra = lax.rsqrt(_rowmean(ya * ya) + EPS)
        yah = ya * ra
        ch, rc = _layer_norm_stats(conv_ref[...])
        cn = ch * v_ref[3:4, :] + v_ref[4:5, :]
        sg = _sigmoid(cn)
        yb = cn * sg
        rb = lax.rsqrt(_rowmean(yb * yb) + EPS)
        ybh = yb * rb
        ycat_ref[...] = jnp.concatenate([yah * v_ref[5:6, :], ybh * v_ref[6:7, :]], axis=1).astype(BF16)
        dpa = dycat[:, 0:WA]
        dpb = dycat[:, WA:2 * WA]
        d_goa = _colsum(dpa * yah)
        d_gob = _colsum(dpb * ybh)
        dyah = dpa * v_ref[5:6, :]
        dybh = dpb * v_ref[6:7, :]
        dya = ra * (dyah - yah * _rowmean(dyah * yah))
        dyb = rb * (dybh - ybh * _rowmean(dybh * ybh))
        dpart_ref[:, 0:WA] = dya * z
        dz = dya * u

        @pl.when(i == 0)
        def _():
            gws_ref[...] = jnp.zeros((NHEAD, CHUNK, CHUNK), F32)
            dbs_acc[...] = jnp.zeros((CHUNK, WA), F32)
            vg_ref[...] = jnp.zeros((8, D), F32)
            v5g_ref[...] = jnp.zeros((8, WA), F32)

        first = lax.broadcasted_iota(jnp.int32, (CHUNK, LANES), 1) < HD
        dvs = []
        for q in range(tm // CHUNK):
            dz_q = dz[q * CHUNK:(q + 1) * CHUNK, :]
            vb_q = vb[q * CHUNK:(q + 1) * CHUNK, :]
            dbs_acc[...] += dz_q
            dzb = dz_q.astype(BF16)
            dvs.append(_head_pairs(wm, dzb, transpose=True))
            for hd in range(NHEAD):
                slab = dzb[:, _lanes(hd // 2)]
                dz_hd = jnp.where(first if hd % 2 == 0 else jnp.logical_not(first), slab, jnp.zeros_like(slab))
                gws_ref[hd] += _dot_nt(dz_hd, vb_q[:, _lanes(hd // 2)])
        dv = jnp.concatenate(dvs, axis=0)
        d_gng = _colsum(dv * vh)
        d_gnb = _colsum(dv)
        dvh = dv * v_ref[0:1, :]
        dpart_ref[:, WA:2 * WA] = rv * (dvh - _rowmean(dvh) - vh * _rowmean(dvh * vh))
        dcn = dyb * (sg * (1.0 + cn * (1.0 - sg)))
        d_cng = _colsum(dcn * ch)
        d_cnb = _colsum(dcn)
        dch = dcn * v_ref[3:4, :]
        dconv = rc * (dch - _rowmean(dch) - ch * _rowmean(dch * ch))
        dpart_ref[:, 2 * WA:3 * WA] = dconv
        dpart_ref[:, 3 * WA:4 * WA] = jnp.zeros((tm, WA), F32)
        d_cb = _colsum(dconv)

        @pl.when(i % tps == 0)
        def _():
            mg_ref[...] = jnp.zeros((8, D), F32)

        mg_ref[2:3, :] += d_gt
        vg_ref[1:2, :] += d_gpost
        v5g_ref[0:1, :] += d_gng
        v5g_ref[1:2, :] += d_gnb
        v5g_ref[2:3, :] += d_cb
        v5g_ref[3:4, :] += d_cng
        v5g_ref[4:5, :] += d_cnb
        v5g_ref[5:6, :] += d_goa
        v5g_ref[6:7, :] += d_gob

        @pl.when(i == nt - 1)
        def _():
            row = lax.broadcasted_iota(jnp.int32, (CHUNK, CHUNK), 0)
            col = lax.broadcasted_iota(jnp.int32, (CHUNK, CHUNK), 1)
            for hd in range(NHEAD):
                gws_ref[hd] = jnp.where(col <= row, gws_ref[hd], 0.0)
            gbs_ref[...] = lax.dot_general(e_ref[...], dbs_acc[...], (((1,), (1,)), ((), ())),
                                           precision=lax.Precision.HIGHEST, preferred_element_type=F32)

    tile = pl.BlockSpec((tm, D), lambda i: (i, 0))
    ptile = pl.BlockSpec((tm, 4 * WA), lambda i: (i, 0))
    return _call(
        core, name=name, grid=(nt,), jobs=jobs,
        in_specs=[tile, tile, pl.BlockSpec((tm, 2 * WA), lambda i: (i, 0)), pl.BlockSpec((tm, WA), lambda i: (i, 0)),
                  pl.BlockSpec((None, 8, D), lambda i: (i // tps, 0, 0)), _const_spec((8, D)), _const_spec((D, D)),
                  _const_spec((8, WA)), _const_spec((NHEAD, CHUNK, CHUNK)), _const_spec((CHUNK, WA)),
                  _const_spec((8, WA))],
        out_specs=[ptile, tile, tile, pl.BlockSpec((None, 8, D), lambda i: (i // tps, 0, 0)),
                   pl.BlockSpec((8, D), lambda i: (0, 0)), pl.BlockSpec((8, WA), lambda i: (0, 0)),
                   pl.BlockSpec((NHEAD, CHUNK, CHUNK), lambda i: (0, 0, 0)), pl.BlockSpec((8, CHUNK), lambda i: (0, 0))],
        out_shape=[jax.ShapeDtypeStruct((T, 4 * WA), F32), jax.ShapeDtypeStruct((T, D), BF16),
                   jax.ShapeDtypeStruct((T, D), BF16), jax.ShapeDtypeStruct((nb, 8, D), F32),
                   jax.ShapeDtypeStruct((8, D), F32), jax.ShapeDtypeStruct((8, WA), F32),
                   jax.ShapeDtypeStruct((NHEAD, CHUNK, CHUNK), F32), jax.ShapeDtypeStruct((8, CHUNK), F32)],
        scratch=[pltpu.VMEM((CHUNK, WA), F32)],
        args=[dxo, ym, proj, conv, mod, gvec, w_mo, v512, ws, bias_full, esel])


def _mixer_bwd_b(dxo, x, dpart, proj, mod, gvec, w_mi, cw, tm, name, jobs=()):
    T = x.shape[0]
    nt = T // tm
    nb = mod.shape[0]
    tps = nt // nb
    hpt = tm // HALO
    nh = T // HALO
    off = HALO - (CONV_K - 1)
    p = _pitch(tm)
    ext_rows = 8 * p

    def core(ins, outs, scs):
        dxo_ref, x_ref, dpart_ref, dnext_ref, ag_ref, halo_ref, mod_ref, g_ref, wmi_ref, cw_ref = ins
        dx_ref, dproj_ref, hb_ref, mg_ref, vg_ref, dcw_ref = outs
        glu_ext, dconv_ext, dglu_scr, dcw_acc = scs
        i = pl.program_id(0)
        first = i % tps == 0
        last = i % tps == tps - 1
        a = ag_ref[:, 0:WA]
        g = ag_ref[:, WA:2 * WA]
        sgg = _sigmoid(g)

        @pl.when(i == 0)
        def _():
            glu_ext[:, HALO + tm:HALO + ext_rows, :] = jnp.zeros((NSLAB, ext_rows - tm, LANES), F32)
            dconv_ext[:, HALO + tm:HALO + ext_rows, :] = jnp.zeros((NSLAB, ext_rows - tm, LANES), F32)
            dcw_acc[...] = jnp.zeros((32, 8, WA), F32)
            vg_ref[...] = jnp.zeros((8, D), F32)

        _to_slabs(glu_ext, 0, jnp.where(first, 0.0, halo_ref[:, 0:WA] * _sigmoid(halo_ref[:, WA:2 * WA])))
        _to_slabs(glu_ext, HALO, a * sgg)
        _to_slabs(dconv_ext, 0, dpart_ref[:, 2 * WA:3 * WA])
        _to_slabs(dconv_ext, tm, jnp.where(last, 0.0, dnext_ref[...]))
        sub = lax.broadcasted_iota(jnp.int32, (SUBL, LANES), 0)
        for s in range(NSLAB):
            accs = [jnp.zeros((SUBL, LANES), F32)] * CONV_K
            for v in range(p):
                dc = jnp.where(v + p * sub < tm, dconv_ext[s, pl.ds(v, 8, stride=p), :], 0.0)
                for k in range(CONV_K):
                    accs[k] = accs[k] + dc * glu_ext[s, pl.ds(v + off + k, 8, stride=p), :]
            for k in range(CONV_K):
                dcw_acc[k, :, _lanes(s)] += accs[k]
        dglu = _tap_sum(dconv_ext, dglu_scr, cw_ref, jnp.zeros((1, WA), F32), tm, lambda k: (CONV_K - 1) - k)

        @pl.when(i == nt - 1)
        def _():
            for k in range(CONV_K):
                dcw_ref[k:k + 1, :] = jnp.sum(dcw_acc[k], axis=0, keepdims=True)
            dcw_ref[CONV_K:32, :] = jnp.zeros((32 - CONV_K, WA), F32)

        da = dglu * sgg
        dgg = dglu * a * (sgg * (1.0 - sgg))
        dproj_ref[:, 0:2 * WA] = dpart_ref[:, 0:2 * WA].astype(BF16)
        dproj_ref[:, 2 * WA:3 * WA] = da.astype(BF16)
        dproj_ref[:, 3 * WA:4 * WA] = dgg.astype(BF16)
        dh = jnp.zeros((tm, D), F32)
        for j in range(NDEV):
            dh = dh + _dot_nt(dproj_ref[:, j * MB:(j + 1) * MB], wmi_ref[j])
        xv = x_ref[...]
        sc, sh = mod_ref[1:2, :], mod_ref[0:1, :]
        gpre = g_ref[0:1, :]
        r = lax.rsqrt(_rowmean(xv * xv) + EPS)
        xh = xv * r
        n = xh * gpre
        hb_ref[...] = (n * (1.0 + sc) + sh).astype(BF16)
        d_sc = _colsum(dh * n)
        d_sh = _colsum(dh)
        dn = dh * (1.0 + sc)
        d_gpre = _colsum(dn * xh)
        dxh = dn * gpre
        dx_ref[...] = dxo_ref[...] + r * (dxh - xh * _rowmean(dxh * xh))

        @pl.when(first)
        def _():
            mg_ref[...] = jnp.zeros((8, D), F32)

        mg_ref[0:1, :] += d_sh
        mg_ref[1:2, :] += d_sc
        vg_ref[0:1, :] += d_gpre

    tile = pl.BlockSpec((tm, D), lambda i: (i, 0))
    return _call(
        core, name=name, grid=(nt,), jobs=jobs,
        in_specs=[tile, tile, pl.BlockSpec((tm, 4 * WA), lambda i: (i, 0)),
                  pl.BlockSpec((HALO, WA), lambda i: (jnp.minimum((i + 1) * hpt, nh - 1), 2)),
                  pl.BlockSpec((tm, 2 * WA), lambda i: (i, 1)),
                  pl.BlockSpec((HALO, 2 * WA), lambda i: (jnp.maximum(i * hpt - 1, 0), 1)),
                  pl.BlockSpec((None, 8, D), lambda i: (i // tps, 0, 0)), _const_spec((8, D)),
                  _const_spec((NDEV, D, MB)), _const_spec((32, WA))],
        out_specs=[tile, pl.BlockSpec((tm, 4 * WA), lambda i: (i, 0)), tile,
                   pl.BlockSpec((None, 8, D), lambda i: (i // tps, 0, 0)), pl.BlockSpec((8, D), lambda i: (0, 0)),
                   pl.BlockSpec((32, WA), lambda i: (0, 0))],
        out_shape=[jax.ShapeDtypeStruct((T, D), F32), jax.ShapeDtypeStruct((T, 4 * WA), BF16),
                   jax.ShapeDtypeStruct((T, D), BF16), jax.ShapeDtypeStruct((nb, 8, D), F32),
                   jax.ShapeDtypeStruct((8, D), F32), jax.ShapeDtypeStruct((32, WA), F32)],
        scratch=[pltpu.VMEM((NSLAB, HALO + ext_rows, LANES), F32), pltpu.VMEM((NSLAB, HALO + ext_rows, LANES), F32),
                 pltpu.VMEM((NSLAB, ext_rows, LANES), F32), pltpu.VMEM((32, 8, WA), F32)],
        args=[dxo, x, dpart, dpart, proj, proj, mod, gvec, w_mi, cw])


def _grad_chip(a, b, a_spec, b_spec, prod_shape, half, name, jobs=(), via_b=False, after=None):
    steps = 8 if half is None else 4
    R = prod_shape[0] if half is None else half
    C = prod_shape[1]

    def core(ins, outs, scs):
        a_ref, b_ref = ins[:2]
        (o_ref,) = outs
        own, snd, rcv, ssem, rsem, lsem = scs
        s = pl.program_id(0)
        c = lax.axis_index("c")
        me = _me()
        sib = _flip(me, (0, 0, 1))
        if via_b:
            prod = _dot_tn(b_ref[...], a_ref[...]).T.astype(BF16)
        else:
            prod = _dot_tn(a_ref[...], b_ref[...]).astype(BF16)
        if half is None:
            q = s // 2

            @pl.when(s % 2 == c)
            def _():
                own[q] = prod

            @pl.when(s % 2 != c)
            def _():
                snd[q] = prod
                _remote(snd.at[q], rcv.at[q], ssem.at[q], rsem.at[q], sib).start()
        else:
            lo = prod[0:half, :]
            hi = prod[half:2 * half, :]
            own[s] = jnp.where(c == 0, lo, hi)
            snd[s] = jnp.where(c == 0, hi, lo)
            _remote(snd.at[s], rcv.at[s], ssem.at[s], rsem.at[s], sib).start()

        @pl.when(s == steps - 1)
        def _():
            for q4 in range(4):
                cp = _remote(snd.at[q4], rcv.at[q4], ssem.at[q4], rsem.at[q4], sib)
                cp.wait_recv()
                cp.wait_send()
                snd[q4] = (own[q4].astype(F32) + rcv[q4].astype(F32)).astype(BF16)
            out = pltpu.make_async_copy(snd, o_ref, lsem)
            out.start()
            out.wait()

    return _call(
        core, name=name, grid=(steps,), jobs=jobs, in_specs=[a_spec, b_spec] + [HBM] * (after is not None),
        out_specs=[HBM], out_shape=[jax.ShapeDtypeStruct((4, R, C), BF16)],
        scratch=[pltpu.VMEM((4, R, C), BF16), pltpu.VMEM((4, R, C), BF16), pltpu.VMEM((4, R, C), BF16),
                 pltpu.SemaphoreType.DMA((4,)), pltpu.SemaphoreType.DMA((4,)), pltpu.SemaphoreType.DMA],
        args=[a, b] + [after] * (after is not None))


def _grad_w_in(dg, hb, name, jobs=()):
    T = hb.shape[0]
    return _grad_chip(dg, hb, pl.BlockSpec((None, T, FB), lambda s: (s, 0, 0)), _const_spec((T, D)),
                      (FB, D), None, name, jobs)


def _grad_w_out(act, dyb, name, jobs=(), after=None):
    T = dyb.shape[0]
    return _grad_chip(act, dyb, pl.BlockSpec((None, T, FB), lambda s: (s, 0, 0)), _const_spec((T, D)),
                      (FB, D), FO, name, jobs, after=after)


def _grad_w_mi(hb, dproj, name, jobs=()):
    T = hb.shape[0]
    return _grad_chip(hb, dproj, _const_spec((T, D)), pl.BlockSpec((T, MB), lambda s: (0, s)),
                      (D, MB), None, name, jobs, via_b=True)


def _grad_w_mo(ycat, dym, name, jobs=()):
    T = ycat.shape[0]
    return _grad_chip(ycat, dym, pl.BlockSpec((T, 2 * MO), lambda s: (0, s)), _const_spec((T, D)),
                      (2 * MO, D), MO, name, jobs)


def _adamw_math(w, g, m, v):
    m2 = ADAM_B1 * m + (1.0 - ADAM_B1) * g
    v2 = ADAM_B2 * v + (1.0 - ADAM_B2) * (g * g)
    m_hat = m2 / (1.0 - ADAM_B1 ** ADAM_STEP)
    v_hat = v2 / (1.0 - ADAM_B2 ** ADAM_STEP)
    delta = -ADAM_LR * (m_hat / (jnp.sqrt(v_hat) + ADAM_EPS) + ADAM_WD * w)
    return delta, m2, v2


def _adamw_reduce(parts, w, m, v, tr, name, after=None):
    R, C = w.shape

    def core(ins, outs, _):
        p_ref, w_ref, m_ref, v_ref = ins[:4]
        g_ref, d_ref, m2_ref, v2_ref = outs
        g = p_ref[0].astype(F32)
        for s in range(1, 4):
            g = g + p_ref[s].astype(F32)
        g_ref[...] = g
        d_ref[...], m2_ref[...], v2_ref[...] = _adamw_math(w_ref[...], g, m_ref[...], v_ref[...])

    blk = pl.BlockSpec((tr, C), lambda i: (i, 0))
    in_specs = [pl.BlockSpec((4, tr, C), lambda i: (0, i, 0)), blk, blk, blk]
    args = [parts, w, m, v]
    if after is not None:
        in_specs.append(HBM)
        args.append(after)
    return _call(
        core, name=name, grid=(R // tr,), in_specs=in_specs,
        out_specs=[blk, blk, blk, blk], out_shape=[jax.ShapeDtypeStruct((R, C), F32)] * 4, args=args)[0]


HBM_ONLY = pl.BlockSpec(memory_space=pltpu.HBM)
SEM = pl.BlockSpec(memory_space=pltpu.SEMAPHORE)
EFFECT = pltpu.SideEffectType.DATAFLOW_SIDE_EFFECTING


def _chip_scatter_start(gs, name):
    n = len(gs)

    def body(*refs):
        g_refs, land_refs = refs[:n], refs[n:2 * n]
        ssem, rsem = refs[2 * n:2 * n + 2]
        token = refs[-1]
        me = _me()
        mq = 2 * me[0] + me[1]
        for k, f in enumerate(CHIP_FLIPS):
            p = _flip(me, f)
            for a in range(n):
                _remote(g_refs[a].at[2 * p[0] + p[1]], land_refs[a].at[mq], ssem.at[4 * a + k], rsem.at[3 * a + k], p).start()
        for a in range(n):
            pltpu.make_async_copy(g_refs[a].at[mq], land_refs[a].at[mq], ssem.at[4 * a + 3]).start()
        token[...] = jnp.zeros_like(token)

    gs = [pltpu.with_memory_space_constraint(g, pltpu.HBM) for g in gs]
    lands = [pltpu.with_memory_space_constraint(lax.empty(g.shape, g.dtype), pltpu.HBM) for g in gs]
    res = pl.pallas_call(
        body, name=name,
        out_shape=(pltpu.SemaphoreType.DMA((4 * n,)), pltpu.SemaphoreType.DMA((3 * n,)))
        + tuple(pltpu.HBM(g.shape, g.dtype) for g in gs) * 2 + (jax.ShapeDtypeStruct((SUBL, LANES), F32),),
        in_specs=(HBM_ONLY,) * (2 * n), out_specs=(SEM, SEM) + (HBM_ONLY,) * (2 * n) + (VM,),
        input_output_aliases={a: 2 + a for a in range(2 * n)},
        compiler_params=pltpu.CompilerParams(has_side_effects=EFFECT),
    )(*gs, *lands)
    return res[:-1], res[-1]


def _chip_scatter_wait(handle, after, name):
    ssem, rsem = handle[:2]
    n = (len(handle) - 2) // 2
    thru = handle[2:]

    def body(*refs):
        g_refs, land_refs = refs[:n], refs[n:2 * n]
        ssem, rsem = refs[2 * n:2 * n + 2]
        me = _me()
        mq = 2 * me[0] + me[1]
        for k, f in enumerate(CHIP_FLIPS):
            p = _flip(me, f)
            pq = 2 * p[0] + p[1]
            for a in range(n):
                _remote(g_refs[a].at[pq], land_refs[a].at[mq], ssem.at[4 * a + k], rsem.at[3 * a + k], p).wait_send()
                _remote(g_refs[a].at[mq], land_refs[a].at[pq], ssem.at[4 * a + k], rsem.at[3 * a + k], p).wait_recv()
        for a in range(n):
            pltpu.make_async_copy(g_refs[a].at[mq], land_refs[a].at[mq], ssem.at[4 * a + 3]).wait()

    res = pl.pallas_call(
        body, name=name,
        out_shape=tuple(pltpu.HBM(t.shape, t.dtype) for t in thru),
        in_specs=(HBM_ONLY,) * (2 * n) + (SEM, SEM, HBM), out_specs=(HBM_ONLY,) * (2 * n),
        input_output_aliases={a: a for a in range(2 * n)},
        compiler_params=pltpu.CompilerParams(has_side_effects=EFFECT),
    )(*thru, ssem, rsem, after)
    return list(res[:n]), list(res[n:])


def _adamw_ada(sc_all, dd, w, m, v, tr, name, after=None):
    R, C = w.shape

    def core(ins, outs, _):
        sc_ref, dd_ref, w_ref, m_ref, v_ref = ins[:5]
        g_ref, d_ref, m2_ref, v2_ref = outs
        g = _dot_tn(sc_ref[...].astype(BF16), dd_ref[...].astype(BF16))
        g_ref[...] = g
        d_ref[...], m2_ref[...], v2_ref[...] = _adamw_math(w_ref[...], g, m_ref[...], v_ref[...])

    blk = pl.BlockSpec((tr, C), lambda i: (i, 0))
    return _call(
        core, name=name, grid=(R // tr,),
        in_specs=[pl.BlockSpec((64, tr), lambda i: (0, i)), pl.BlockSpec((64, C), lambda i: (0, 0)), blk, blk, blk]
        + [HBM] * (after is not None),
        out_specs=[blk, blk, blk, blk], out_shape=[jax.ShapeDtypeStruct((R, C), F32)] * 4,
        args=[sc_all, dd, w, m, v] + [after] * (after is not None))[0]


def _adamw_small(gathered, plain, grads, wmv, emit, name, after=None):
    nw = len(grads)
    ng, npl, ne = len(gathered), len(plain), len(emit)

    def core(ins, outs, _):
        srcs = []
        for a in range(ng):
            s = ins[a][0]
            for dev in range(1, NDEV):
                s = s + ins[a][dev]
            srcs.append(s)
        srcs += [ins[ng + a][...] for a in range(npl)]
        w_refs = ins[ng + npl:]
        for e, a in enumerate(emit):
            outs[e][...] = srcs[a]
        for t in range(nw):
            src, row = grads[t]
            g = srcs[src] if row is None else srcs[src][row:row + 1, :]
            w_ref, m_ref, v_ref = w_refs[3 * t:3 * t + 3]
            g_ref, d_ref, m2_ref, v2_ref = outs[ne + 4 * t:ne + 4 * t + 4]
            g_ref[...] = g
            d_ref[...], m2_ref[...], v2_ref[...] = _adamw_math(w_ref[...], g, m_ref[...], v_ref[...])

    out_shape = [jax.ShapeDtypeStruct(gathered[a].shape[1:], F32) for a in emit]
    for t in range(nw):
        out_shape += [jax.ShapeDtypeStruct(wmv[3 * t].shape, F32)] * 4
    return _call(
        core, name=name, grid=(), in_specs=[VM] * (ng + npl + 3 * nw) + [HBM] * (after is not None),
        out_specs=[VM] * (ne + 4 * nw), out_shape=out_shape,
        args=list(gathered) + list(plain) + list(wmv) + [after] * (after is not None))[0]


def _ada_fwd(c_pad, w_ada, b_cols, cw_pad, jobs=()):
    def core(ins, outs, scs, start_jobs, finish_jobs):
        c_ref, w_ref, b_ref, cwp_ref = ins
        ada_ref, sc_ref, cw_ref = outs
        cbuf, send_buf, ssem, rsem = scs
        me = _me()
        mi = _lin(me)
        cbuf[mi] = c_ref[...]
        cw_ref[mi] = cwp_ref[...]
        peers = [_flip(me, f) for f in FLIPS]
        first = []
        for k, p in enumerate(peers):
            first.append(_remote(cbuf.at[mi], cbuf.at[mi], ssem.at[k], rsem.at[k], p))
            first.append(_remote(cw_ref.at[mi], cw_ref.at[mi], ssem.at[7 + k], rsem.at[7 + k], p))
        for cp in first:
            cp.start()
        start_jobs()
        for k, p in enumerate(peers):
            pi = _lin(p)
            _remote(cbuf.at[pi], cbuf.at[pi], ssem.at[k], rsem.at[k], p).wait_recv()
            _remote(cw_ref.at[pi], cw_ref.at[pi], ssem.at[7 + k], rsem.at[7 + k], p).wait_recv()
        c_all = cbuf[...].reshape(8 * 8, D)
        sc = c_all * _sigmoid(c_all)
        sc_ref[...] = sc
        res = _dot(sc.astype(BF16), w_ref[...].astype(BF16)) + b_ref[...]
        send_buf[...] = res.reshape(8, 8, ADA_B)
        ada_ref[mi] = send_buf[mi]
        second = []
        for k, p in enumerate(peers):
            second.append(_remote(send_buf.at[_lin(p)], ada_ref.at[mi], ssem.at[14 + k], rsem.at[14 + k], p))
        for cp in second:
            cp.start()
        finish_jobs()
        for k, p in enumerate(peers):
            _remote(send_buf.at[mi], ada_ref.at[_lin(p)], ssem.at[14 + k], rsem.at[14 + k], p).wait_recv()
        for cp in first + second:
            cp.wait_send()

    return _call(
        core, name="ada_fwd", grid=(), jobs=jobs, core_starts=True, in_specs=[VM, VM, VM, VM], out_specs=[VM, VM, VM],
        out_shape=[jax.ShapeDtypeStruct((8, 8, ADA_B), F32), jax.ShapeDtypeStruct((64, D), F32),
                   jax.ShapeDtypeStruct((8, 32, 64), F32)],
        scratch=[pltpu.VMEM((8, 8, D), F32), pltpu.VMEM((8, 8, ADA_B), F32),
                 pltpu.SemaphoreType.DMA((21,)), pltpu.SemaphoreType.DMA((21,))],
        args=[c_pad, w_ada, b_cols, cw_pad])


def _ada_bwd(mods, rows, jobs=()):
    bl = mods[0].shape[0]

    def core(ins, outs, scs):
        dd_ref, gb_ref = outs
        d_ref, rbuf, ssem, rsem = scs
        me = _me()
        mi = _lin(me)
        peers = [_flip(me, f) for f in FLIPS]
        d_ref[...] = jnp.zeros_like(d_ref)
        for r, (a, row) in enumerate(rows):
            pos = r * D
            while pos < (r + 1) * D:
                j = pos // ADA_B
                nxt = min((r + 1) * D, (j + 1) * ADA_B)
                for b in range(bl):
                    d_ref[j, b:b + 1, pos - j * ADA_B:nxt - j * ADA_B] = ins[a][b, row:row + 1, pos - r * D:nxt - r * D]
                pos = nxt
        rbuf[mi] = d_ref[mi]
        first = []
        for k, p in enumerate(peers):
            first.append(_remote(d_ref.at[_lin(p)], rbuf.at[mi], ssem.at[k], rsem.at[k], p))
        for cp in first:
            cp.start()
        for k, p in enumerate(peers):
            _remote(d_ref.at[mi], rbuf.at[_lin(p)], ssem.at[k], rsem.at[k], p).wait_recv()
        dd = rbuf[...].reshape(64, ADA_B)
        dd_ref[...] = dd
        gb_ref[...] = jnp.broadcast_to(_colsum(dd), (8, ADA_B))
        for cp in first:
            cp.wait_send()

    return _call(
        core, name="ada_bwd", grid=(), jobs=jobs, in_specs=[VM] * len(mods), out_specs=[VM, VM],
        out_shape=[jax.ShapeDtypeStruct((64, ADA_B), F32), jax.ShapeDtypeStruct((8, ADA_B), F32)],
        scratch=[pltpu.VMEM((8, 8, ADA_B), F32), pltpu.VMEM((8, 8, ADA_B), F32),
                 pltpu.SemaphoreType.DMA((7,)), pltpu.SemaphoreType.DMA((7,))],
        args=list(mods))


SMALL_D = ("g_pre_f1", "g_post_f1", "g_pre_m", "g_post_m", "g_pre_f2", "g_post_f2")
SMALL_W = ("gmlp_norm_g", "gmlp_norm_b", "conv_b", "conv_norm_g", "conv_norm_b", "g_out_a", "g_out_b")


def kernel(x, c, w_ada, b_ada, g_pre_f1, g_post_f1, w_f1_in, w_f1_out, g_pre_m, g_post_m, w_mix_in, gmlp_norm_g, gmlp_norm_b, w_spatial, b_spatial, conv_w, conv_b, conv_norm_g, conv_norm_b, g_out_a, g_out_b, w_mix_out, g_pre_f2, g_post_f2, w_f2_in, w_f2_out, loss_target, m_w_ada, m_b_ada, m_g_pre_f1, m_g_post_f1, m_w_f1_in, m_w_f1_out, m_g_pre_m, m_g_post_m, m_w_mix_in, m_gmlp_norm_g, m_gmlp_norm_b, m_w_spatial, m_b_spatial, m_conv_w, m_conv_b, m_conv_norm_g, m_conv_norm_b, m_g_out_a, m_g_out_b, m_w_mix_out, m_g_pre_f2, m_g_post_f2, m_w_f2_in, m_w_f2_out, v_w_ada, v_b_ada, v_g_pre_f1, v_g_post_f1, v_w_f1_in, v_w_f1_out, v_g_pre_m, v_g_post_m, v_w_mix_in, v_gmlp_norm_g, v_gmlp_norm_b, v_w_spatial, v_b_spatial, v_conv_w, v_conv_b, v_conv_norm_g, v_conv_norm_b, v_g_out_a, v_g_out_b, v_w_mix_out, v_g_pre_f2, v_g_post_f2, v_w_f2_in, v_w_f2_out):
    given = dict(locals())
    bl, seq, _ = x.shape
    T = bl * seq
    tm = min(256, seq // 2)
    mi = _lin((lax.axis_index("x"), lax.axis_index("y"), lax.axis_index("c")))

    def shard_in(w):
        return w[0].T.astype(BF16)

    g_f1 = _RelayGather([shard_in(w_f1_in), w_f1_out[0].astype(BF16)], ("rows", "out"))
    s_f2 = shard_in(w_f2_in)
    g_mx = _Gather([w_mix_in[0].astype(BF16), w_mix_out[0].astype(BF16), w_f2_out[0].astype(BF16), s_f2[:, 0:D // 4]],
                   ("rows", "rows", "out", "rows"), late_mid=True)
    g_f2 = _Gather([s_f2[:, D // 4:D]], ("rows",))

    c_pad = jnp.pad(c, ((0, 8 - bl), (0, 0)))
    b_cols = lax.dynamic_slice(b_ada, (0, mi * ADA_B), (1, ADA_B))
    cw_pad = jnp.pad(conv_w[0], ((0, 1), (0, 0)))
    (ada_blk, sc_all, cw_all), ((wi1, wo1),) = _ada_fwd(c_pad, w_ada[0], b_cols, cw_pad, jobs=[g_f1])
    ada = ada_blk[:, 0:bl, :].transpose(1, 0, 2).reshape(bl, 9, D)
    pad5 = jnp.zeros((bl, 5, D), F32)
    mod1 = jnp.concatenate([ada[:, 0:3], pad5], axis=1)
    mod2 = jnp.concatenate([ada[:, 3:6], pad5], axis=1)
    mod3 = jnp.concatenate([ada[:, 6:9], pad5], axis=1)
    cw_full = cw_all.transpose(1, 0, 2).reshape(32, WA)

    zrow = jnp.zeros((1, D), F32)
    gv1 = jnp.concatenate([g_pre_f1, g_post_f1] + [zrow] * 6, axis=0)
    gvm = jnp.concatenate([g_pre_m, g_post_m] + [zrow] * 6, axis=0)
    gv2 = jnp.concatenate([g_pre_f2, g_post_f2] + [zrow] * 6, axis=0)
    v512 = jnp.concatenate([gmlp_norm_g, gmlp_norm_b, conv_b, conv_norm_g, conv_norm_b, g_out_a, g_out_b,
                            jnp.zeros((1, WA), F32)], axis=0)
    ws = w_spatial[0]
    bias_full = jnp.repeat(b_spatial[0].T, HD, axis=1)
    esel = (lax.broadcasted_iota(jnp.int32, (8, WA), 1) // HD == lax.broadcasted_iota(jnp.int32, (8, WA), 0)).astype(F32)

    x0 = x.reshape(T, D)
    (x1, gu1, y1), ((wmi, wmo, wo2, wi2a),) = _ffn_fwd(x0, mod1, gv1, wi1, wo1, tm, "ffn1_fwd", jobs=[g_mx])
    wmo = wmo.reshape(D, D)
    (x2, proj, ym, conv), ((wi2b,),) = _mixer_fwd(x1, mod2, gvm, wmi, wmo, v512, ws, bias_full, cw_full, tm, "mixer_fwd", jobs=[g_f2])

    (dx2, dg2, act2, hb2, dyb2, mg3, vg3, loss_blk), _ = _ffn_last(
        x2, loss_target.reshape(T, D), mod3, gv2, (wi2a, wi2b), wo2, tm, "ffn2_fwd_bwd")
    (g_wi2,), _ = _grad_w_in(dg2, hb2, "ffn2_gw_in")
    (g_wo2,), _ = _grad_w_out(act2, dyb2, "ffn2_gw_out")
    (dpart, dymb, ycat, mg2a, vgma, v5g, gws, gbs), ((p_wo2,),) = _mixer_bwd_a(
        dx2, ym, proj, conv, mod2, gvm, wmo, v512, ws, bias_full, esel, tm, "mixer_bwd_a",
        jobs=[_ChipScatter([g_wo2])])
    (dx1, dproj, hbm, mg2b, vgmb, dcw), ((p_wi2,),) = _mixer_bwd_b(
        dx2, x1, dpart, proj, mod2, gvm, wmi, cw_full, tm, "mixer_bwd_b", jobs=[_ChipScatter([g_wi2])])
    (g_wmi,), _ = _grad_w_mi(hbm, dproj, "mixer_gw_in")
    (g_wmo,), _ = _grad_w_mo(ycat, dymb, "mixer_gw_out")
    p2 = jnp.concatenate([v5g, dcw], axis=0)
    (dx0, dg1, act1, hb1, dyb1, mg1, vg1), _ = _ffn_bwd(dx1, x0, y1, gu1, mod1, gv1, wi1, wo1, tm, "ffn1_bwd")

    ada_rows = [(0, 0), (0, 1), (0, 2), (1, 0), (1, 1), (2, 2), (3, 0), (3, 1), (3, 2)]
    p1 = jnp.concatenate([vg1[0:2], vgmb[0:1], vgma[1:2], vg3[0:2], loss_blk[0:1], zrow], axis=0)
    (dd_all, gb_own), ((a1,),) = _ada_bwd([mg1, mg2b, mg2a, mg3], ada_rows, jobs=[_AllGather([p1])])

    (g_wi1,), ((a2, a3, a4, gb_all), (p_wmi, p_wmo)) = _grad_w_in(
        dg1, hb1, "ffn1_gw_in", jobs=[_Gather([p2, gws, gbs, gb_own], ("rows",) * 4), _ChipScatter([g_wmi, g_wmo])])
    g_bada = gb_all[:, 0, :].reshape(1, 9 * D)

    h_i1, token = _chip_scatter_start([g_wi1], "tail_start")
    (g_wo1,), _ = _grad_w_out(act1, dyb1, "ffn1_gw_out", after=token)
    h_o1, token = _chip_scatter_start([g_wo1], "tail2_start")

    res = {}
    quad = _adamw_reduce(p_wi2, w_f2_in[0].T, m_w_f2_in[0].T, v_w_f2_in[0].T, FO, "adamw_w_f2_in", after=token)
    res["w_f2_in"] = tuple(t.T[None] for t in quad)
    for nm, part, tr in (("w_f2_out", p_wo2, FO), ("w_mix_in", p_wmi, 256), ("w_mix_out", p_wmo, MO)):
        quad = _adamw_reduce(part, given[nm][0], given["m_" + nm][0], given["v_" + nm][0], tr, "adamw_" + nm, after=quad[1])
        res[nm] = tuple(t[None] for t in quad)
    quad = _adamw_ada(sc_all, dd_all, w_ada[0], m_w_ada[0], v_w_ada[0], 256, "adamw_w_ada", after=quad[1])
    res["w_ada"] = tuple(t[None] for t in quad)

    small = SMALL_D + SMALL_W + ("w_spatial", "b_spatial", "b_ada")
    grads = [(0, r) for r in range(6)] + [(1, r) for r in range(7)] + [(2, None), (3, None), (4, None)]
    wmv = []
    for nm in small:
        for pre in ("", "m_", "v_"):
            wmv.append(given[pre + nm][0] if nm in ("w_spatial", "b_spatial") else given[pre + nm])
    outs = _adamw_small([a1, a2, a3, a4], [g_bada], grads, wmv, (0, 1), "adamw_small", after=quad[1])

    _, (p_wi1,) = _chip_scatter_wait(h_i1, outs[0], "tail_wait")
    quad = _adamw_reduce(p_wi1, w_f1_in[0].T, m_w_f1_in[0].T, v_w_f1_in[0].T, FO, "adamw_w_f1_in")
    res["w_f1_in"] = tuple(t.T[None] for t in quad)
    _, (p_wo1,) = _chip_scatter_wait(h_o1, quad[1], "tail2_wait")
    quad = _adamw_reduce(p_wo1, w_f1_out[0], m_w_f1_out[0], v_w_f1_out[0], FO, "adamw_w_f1_out")
    res["w_f1_out"] = tuple(t[None] for t in quad)
    loss = outs[0][6, 0]
    for t, nm in enumerate(small):
        quad = outs[2 + 4 * t:6 + 4 * t]
        res[nm] = tuple(q[None] for q in quad) if nm in ("w_spatial", "b_spatial") else tuple(quad)
    g_cw = lax.dynamic_slice(outs[1], (8, mi * 64), (32, 64))
    wmv = [jnp.pad(given[pre + "conv_w"][0], ((0, 1), (0, 0)), constant_values=1.0 if pre == "v_" else 0.0)
           for pre in ("", "m_", "v_")]
    quad = _adamw_small([], [g_cw], [(0, None)], wmv, (), "adamw_conv_w")
    res["conv_w"] = tuple(q[0:CONV_K][None] for q in quad)

    order = ["w_ada", "b_ada", "g_pre_f1", "g_post_f1", "w_f1_in", "w_f1_out", "g_pre_m", "g_post_m", "w_mix_in",
             "gmlp_norm_g", "gmlp_norm_b", "w_spatial", "b_spatial", "conv_w", "conv_b", "conv_norm_g", "conv_norm_b",
             "g_out_a", "g_out_b", "w_mix_out", "g_pre_f2", "g_post_f2", "w_f2_in", "w_f2_out"]
    out = [loss, dx0.reshape(bl, seq, D)]
    for k in range(4):
        out += [res[nm][k] for nm in order]
    return tuple(out)
```

```python
import jax
import jax.numpy as jnp
from jax import lax
from jax.experimental import pallas as pl
from jax.experimental.pallas import tpu as pltpu

F32 = jnp.float32
BF16 = jnp.bfloat16

D = 1024
DFF = 2816
NDEV = 8
FB = 2 * DFF // NDEV
NCH = DFF // FB
LANES = 128
SUBL = 8
FO = DFF // NDEV
WA = 512
NSLAB = WA // LANES
NHEAD = 8
HD = 64
CHUNK = 128
CONV_K = 31
HALO = 32
MB = 2 * (WA + WA) // NDEV
MO = D // NDEV
ADA_B = 9 * D // NDEV
EPS = 1e-6
HALF = 0.5

ADAM_LR = 0.001
ADAM_B1 = 0.9
ADAM_B2 = 0.999
ADAM_EPS = 1e-08
ADAM_WD = 0.01
ADAM_STEP = 10

VMEM_LIMIT = 56 * 1024 * 1024
MESH = pl.DeviceIdType.MESH
FLIPS = ((0, 0, 1), (1, 0, 0), (0, 1, 0), (1, 1, 0), (1, 0, 1), (0, 1, 1), (1, 1, 1))
CHIP_FLIPS = ((1, 0, 0), (0, 1, 0), (1, 1, 0))
HBM = pl.BlockSpec(memory_space=pl.ANY)
VM = pl.BlockSpec(memory_space=pltpu.VMEM)


def _dot(a, b):
    return lax.dot_general(a, b, (((1,), (0,)), ((), ())), preferred_element_type=F32)


def _dot_nt(a, b):
    return lax.dot_general(a, b, (((1,), (1,)), ((), ())), preferred_element_type=F32)


def _dot_tn(a, b):
    return lax.dot_general(a, b, (((0,), (0,)), ((), ())), preferred_element_type=F32)


def _rowmean(v):
    return jnp.mean(v, axis=-1, keepdims=True)


def _colsum(v):
    return jnp.sum(v, axis=0, keepdims=True)


def _sigmoid(v):
    return 0.5 * jnp.tanh(0.5 * v) + 0.5


def _const_spec(shape):
    nd = len(shape)
    return pl.BlockSpec(shape, lambda *_: (0,) * nd, pipeline_mode=pl.Buffered(1))


def _me():
    return lax.axis_index("x"), lax.axis_index("y"), lax.axis_index("c")


def _flip(me, f):
    return tuple(1 - v if b else v for v, b in zip(me, f))


def _lin(p):
    return 4 * p[0] + 2 * p[1] + p[2]


def _remote(src, dst, send_sem, recv_sem, dev):
    return pltpu.make_async_remote_copy(src_ref=src, dst_ref=dst, send_sem=send_sem, recv_sem=recv_sem,
                                        device_id=dev, device_id_type=MESH)


def _blk(kind, ref, p):
    if kind == "out":
        return ref.at[2 * p[0] + p[1], pl.ds(p[2] * FO, FO), :]
    return ref.at[_lin(p)]


class _Gather:
    def __init__(self, shards, kinds, late_mid=False):
        self.late_mid = late_mid
        self.kinds = kinds
        self.n = len(shards)
        self.ins = list(shards)
        self.out_shape = [jax.ShapeDtypeStruct((4, FB, D) if k == "out" else (NDEV,) + s.shape, s.dtype)
                          for s, k in zip(shards, kinds)]
        self.sems = [pltpu.SemaphoreType.DMA((7 * self.n,)), pltpu.SemaphoreType.DMA((7 * self.n,)),
                     pltpu.SemaphoreType.DMA((self.n,))]

    def _first(self, ins, outs, sems):
        ssem, rsem, lsem = sems
        me = _me()
        sib = _flip(me, (0, 0, 1))
        cps, loc = [], []
        for a in range(self.n):
            mine = _blk(self.kinds[a], outs[a], me)
            loc.append(pltpu.make_async_copy(ins[a], mine, lsem.at[a]))
            cps.append(_remote(ins[a], mine, ssem.at[7 * a], rsem.at[7 * a], sib))
            for j, f in enumerate(CHIP_FLIPS):
                cps.append(_remote(ins[a], mine, ssem.at[7 * a + 1 + j], rsem.at[7 * a + 1 + j], _flip(me, f)))
        return cps, loc

    def _passed(self, outs, sems):
        ssem, rsem, _ = sems
        me = _me()
        sib = _flip(me, (0, 0, 1))
        cps = []
        for j, f in enumerate(CHIP_FLIPS):
            for a in range(self.n):
                blk = _blk(self.kinds[a], outs[a], _flip(me, f))
                cps.append(_remote(blk, blk, ssem.at[7 * a + 4 + j], rsem.at[7 * a + 4 + j], sib))
        return cps

    def start(self, ins, outs, sems):
        cps, loc = self._first(ins, outs, sems)
        for cp in loc + cps:
            cp.start()

    def mid(self, ins, outs, sems):
        ssem, rsem, _ = sems
        me = _me()
        passed = self._passed(outs, sems)
        t = 0
        for j, f in enumerate(CHIP_FLIPS):
            for a in range(self.n):
                blk = _blk(self.kinds[a], outs[a], _flip(me, f))
                _remote(blk, blk, ssem.at[7 * a + 1 + j], rsem.at[7 * a + 1 + j], _flip(me, f)).wait_recv()
                passed[t].start()
                t += 1

    def end(self, ins, outs, sems):
        ssem, rsem, _ = sems
        me = _me()
        sib = _flip(me, (0, 0, 1))
        for a in range(self.n):
            blk = _blk(self.kinds[a], outs[a], sib)
            _remote(blk, blk, ssem.at[7 * a], rsem.at[7 * a], sib).wait_recv()
            for j, f in enumerate(CHIP_FLIPS):
                blk = _blk(self.kinds[a], outs[a], _flip(_flip(me, f), (0, 0, 1)))
                _remote(blk, blk, ssem.at[7 * a + 4 + j], rsem.at[7 * a + 4 + j], sib).wait_recv()
        cps, loc = self._first(ins, outs, sems)
        for cp in cps + self._passed(outs, sems):
            cp.wait_send()
        for cp in loc:
            cp.wait()


class _RelayGather(_Gather):
    def _peers(self):
        me = _me()
        c = me[2]
        to = (me[0] + (1 - c) - 2 * me[0] * (1 - c), me[1] + c - 2 * me[1] * c, c)
        frm = (me[0] + c - 2 * me[0] * c, me[1] + (1 - c) - 2 * me[1] * (1 - c), c)
        return me, _flip(me, (0, 0, 1)), to, frm, _flip(me, (1, 1, 0))

    def _first(self, ins, outs, sems):
        ssem, rsem, lsem = sems
        me, sib, to, frm, _ = self._peers()
        cps, loc = [], []
        for a in range(self.n):
            mine = _blk(self.kinds[a], outs[a], me)
            loc.append(pltpu.make_async_copy(ins[a], mine, lsem.at[a]))
            for slot, dev in ((0, sib), (1, to), (2, frm)):
                cps.append(_remote(ins[a], mine, ssem.at[7 * a + slot], rsem.at[7 * a + slot], dev))
        return cps, loc

    def _block_copy(self, outs, sems, a, slot, owner, dev):
        ssem, rsem, _ = sems
        blk = _blk(self.kinds[a], outs[a], owner)
        return _remote(blk, blk, ssem.at[7 * a + slot], rsem.at[7 * a + slot], dev)

    def mid(self, ins, outs, sems):
        me, sib, to, frm, _ = self._peers()
        for a in range(self.n):
            self._block_copy(outs, sems, a, 2, frm, frm).wait_recv()
            self._block_copy(outs, sems, a, 3, frm, to).start()
            self._block_copy(outs, sems, a, 5, frm, sib).start()
        for a in range(self.n):
            self._block_copy(outs, sems, a, 1, to, to).wait_recv()
            self._block_copy(outs, sems, a, 4, to, sib).start()

    def end(self, ins, outs, sems):
        me, sib, to, frm, far = self._peers()
        up = (0, 0, 1)
        for a in range(self.n):
            self._block_copy(outs, sems, a, 3, far, to).wait_recv()
            self._block_copy(outs, sems, a, 6, far, sib).start()
        for a in range(self.n):
            for slot, owner in ((0, sib), (4, _flip(frm, up)), (5, _flip(to, up)), (6, _flip(far, up))):
                self._block_copy(outs, sems, a, slot, owner, sib).wait_recv()
        cps, loc = self._first(ins, outs, sems)
        for a in range(self.n):
            cps += [self._block_copy(outs, sems, a, 3, frm, to), self._block_copy(outs, sems, a, 4, to, sib),
                    self._block_copy(outs, sems, a, 5, frm, sib), self._block_copy(outs, sems, a, 6, far, sib)]
        for cp in cps:
            cp.wait_send()
        for cp in loc:
            cp.wait()


class _ChipScatter:
    def __init__(self, grads):
        self.n = len(grads)
        self.ins = list(grads)
        self.out_shape = [jax.ShapeDtypeStruct(g.shape, BF16) for g in grads]
        self.sems = [pltpu.SemaphoreType.DMA((3 * self.n,)), pltpu.SemaphoreType.DMA((3 * self.n,)),
                     pltpu.SemaphoreType.DMA((self.n,))]

    def _copies(self, ins, outs, sems):
        ssem, rsem, lsem = sems
        me = _me()
        mq = 2 * me[0] + me[1]
        loc = [pltpu.make_async_copy(ins[a].at[mq], outs[a].at[mq], lsem.at[a]) for a in range(self.n)]
        cps = []
        for k, f in enumerate(CHIP_FLIPS):
            p = _flip(me, f)
            for a in range(self.n):
                cps.append(_remote(ins[a].at[2 * p[0] + p[1]], outs[a].at[mq], ssem.at[3 * a + k], rsem.at[3 * a + k], p))
        return cps, loc

    def start(self, ins, outs, sems):
        cps, loc = self._copies(ins, outs, sems)
        for cp in loc + cps:
            cp.start()

    mid = None

    def end(self, ins, outs, sems):
        ssem, rsem, _ = sems
        me = _me()
        mq = 2 * me[0] + me[1]
        for k, f in enumerate(CHIP_FLIPS):
            p = _flip(me, f)
            for a in range(self.n):
                _remote(ins[a].at[mq], outs[a].at[2 * p[0] + p[1]], ssem.at[3 * a + k], rsem.at[3 * a + k], p).wait_recv()
        cps, loc = self._copies(ins, outs, sems)
        for cp in cps:
            cp.wait_send()
        for cp in loc:
            cp.wait()


class _AllGather:
    def __init__(self, parts):
        self.n = len(parts)
        self.ins = list(parts)
        self.out_shape = [jax.ShapeDtypeStruct((NDEV,) + p.shape, p.dtype) for p in parts]
        self.sems = [pltpu.SemaphoreType.DMA((7 * self.n,)), pltpu.SemaphoreType.DMA((7 * self.n,)),
                     pltpu.SemaphoreType.DMA((self.n,))]

    def _copies(self, ins, outs, sems):
        ssem, rsem, lsem = sems
        me = _me()
        mi = _lin(me)
        loc = [pltpu.make_async_copy(ins[a], outs[a].at[mi], lsem.at[a]) for a in range(self.n)]
        cps = []
        for k, f in enumerate(FLIPS):
            for a in range(self.n):
                cps.append(_remote(ins[a], outs[a].at[mi], ssem.at[7 * a + k], rsem.at[7 * a + k], _flip(me, f)))
        return cps, loc

    def start(self, ins, outs, sems):
        cps, loc = self._copies(ins, outs, sems)
        for cp in loc + cps:
            cp.start()

    mid = None

    def end(self, ins, outs, sems):
        ssem, rsem, _ = sems
        me = _me()
        for k, f in enumerate(FLIPS):
            p = _flip(me, f)
            for a in range(self.n):
                _remote(ins[a], outs[a].at[_lin(p)], ssem.at[7 * a + k], rsem.at[7 * a + k], p).wait_recv()
        cps, loc = self._copies(ins, outs, sems)
        for cp in cps:
            cp.wait_send()
        for cp in loc:
            cp.wait()


def _call(core, *, name, grid, in_specs, out_specs, out_shape, args, scratch=(), jobs=(), core_starts=False):
    n_in, n_out, n_sc = len(in_specs), len(out_specs), len(scratch)
    steps = 1
    for g in grid:
        steps *= g

    def body(*refs):
        pos = [0]

        def take(k):
            r = refs[pos[0]:pos[0] + k]
            pos[0] += k
            return r

        ins = take(n_in)
        j_ins = [take(len(j.ins)) for j in jobs]
        outs = take(n_out)
        j_outs = [take(len(j.out_shape)) for j in jobs]
        scs = take(n_sc)
        j_sems = [take(len(j.sems)) for j in jobs]
        if len(grid) == 2:
            step = pl.program_id(0) * grid[1] + pl.program_id(1)
        elif len(grid) == 1:
            step = pl.program_id(0)
        else:
            step = 0
        def start_jobs():
            for j, ji, jo, js in zip(jobs, j_ins, j_outs, j_sems):
                j.start(ji, jo, js)

        if grid:
            pl.when(step == 0)(start_jobs)
        elif not core_starts:
            start_jobs()
        for j, ji, jo, js in zip(jobs, j_ins, j_outs, j_sems):
            if j.mid is not None and grid:
                at = max(steps - 2, 0) if j.late_mid else (3 * steps) // 4
                pl.when(step == at)(lambda j=j, ji=ji, jo=jo, js=js: j.mid(ji, jo, js))
        def finish_jobs():
            for j, ji, jo, js in zip(jobs, j_ins, j_outs, j_sems):
                if j.mid is not None:
                    j.mid(ji, jo, js)
                j.end(ji, jo, js)

        if core_starts:
            core(ins, outs, scs, start_jobs, finish_jobs)
        elif core is not None:
            core(ins, outs, scs)
        if grid:
            for j, ji, jo, js in zip(jobs, j_ins, j_outs, j_sems):
                pl.when(step == steps - 1)(lambda j=j, ji=ji, jo=jo, js=js: j.end(ji, jo, js))
        elif not core_starts:
            finish_jobs()

    all_in = list(in_specs)
    all_args = list(args)
    all_out = list(out_specs)
    all_shape = list(out_shape)
    all_sc = list(scratch)
    for j in jobs:
        all_in += [HBM] * len(j.ins)
        all_args += j.ins
    for j in jobs:
        all_out += [HBM] * len(j.out_shape)
        all_shape += j.out_shape
        all_sc += j.sems
    params = dict(vmem_limit_bytes=VMEM_LIMIT)
    if grid:
        params["dimension_semantics"] = ("arbitrary",) * len(grid)
    res = pl.pallas_call(
        body, name=name, grid=grid, in_specs=all_in, out_specs=all_out, out_shape=all_shape,
        scratch_shapes=all_sc, compiler_params=pltpu.CompilerParams(**params),
    )(*all_args)
    core_res = list(res[:n_out])
    job_res = []
    pos = n_out
    for j in jobs:
        job_res.append(list(res[pos:pos + len(j.out_shape)]))
        pos += len(j.out_shape)
    return core_res, job_res


def _ffn_fwd(x, mod, gvec, w_in, w_out, tm, name, jobs=()):
    T = x.shape[0]
    nt = T // tm
    tps = nt // mod.shape[0]

    def core(ins, outs, _):
        x_ref, mod_ref, g_ref, win_ref, wout_ref = ins
        xo_ref, gu_ref, y_ref = outs
        xv = x_ref[...]
        sh, sc, gt = mod_ref[0:1, :], mod_ref[1:2, :], mod_ref[2:3, :]
        r = lax.rsqrt(_rowmean(xv * xv) + EPS)
        h = (xv * r * g_ref[0:1, :]) * (1.0 + sc) + sh
        hb = h.astype(BF16)
        y = jnp.zeros((tm, D), F32)
        for cidx in range(NCH):
            gate = _dot_nt(hb, win_ref[cidx])
            up = _dot_nt(hb, win_ref[NCH + cidx])
            gu_ref[cidx] = gate.astype(BF16)
            gu_ref[NCH + cidx] = up.astype(BF16)
            act = gate * _sigmoid(gate) * up
            y = y + _dot(act.astype(BF16), wout_ref[cidx])
        y_ref[...] = y
        ry = lax.rsqrt(_rowmean(y * y) + EPS)
        xo_ref[...] = xv + (HALF * gt) * (y * ry * g_ref[1:2, :])

    tile = pl.BlockSpec((tm, D), lambda i: (i, 0))
    return _call(
        core, name=name, grid=(nt,), jobs=jobs,
        in_specs=[tile, pl.BlockSpec((None, 8, D), lambda i: (i // tps, 0, 0)), _const_spec((8, D)),
                  _const_spec((8, FB, D)), _const_spec((4, FB, D))],
        out_specs=[tile, pl.BlockSpec((8, tm, FB), lambda i: (0, i, 0)), tile],
        out_shape=[jax.ShapeDtypeStruct((T, D), F32), jax.ShapeDtypeStruct((8, T, FB), BF16),
                   jax.ShapeDtypeStruct((T, D), F32)],
        args=[x, mod, gvec, w_in, w_out])


def _ffn_bwd(dxo, x, y, gu, mod, gvec, w_in, w_out, tm, name, jobs=()):
    T = x.shape[0]
    nt = T // tm
    nb = mod.shape[0]
    tps = nt // nb

    def core(ins, outs, _):
        dxo_ref, x_ref, y_ref, gu_ref, mod_ref, g_ref, win_ref, wout_ref = ins
        dx_ref, dg_ref, act_ref, hb_ref, dyb_ref, mg_ref, vg_ref = outs
        i = pl.program_id(0)
        xv = x_ref[...]
        dxo_v = dxo_ref[...]
        yv = y_ref[...]
        sh, sc, gt = mod_ref[0:1, :], mod_ref[1:2, :], mod_ref[2:3, :]
        gpre, gpost = g_ref[0:1, :], g_ref[1:2, :]
        r = lax.rsqrt(_rowmean(xv * xv) + EPS)
        xh = xv * r
        n = xh * gpre
        hb = (n * (1.0 + sc) + sh).astype(BF16)
        hb_ref[...] = hb
        ry = lax.rsqrt(_rowmean(yv * yv) + EPS)
        yh = yv * ry
        d_gt = _colsum(HALF * dxo_v * (yh * gpost))
        dp = (HALF * gt) * dxo_v
        d_gpost = _colsum(dp * yh)
        dyh = dp * gpost
        dy = ry * (dyh - yh * _rowmean(dyh * yh))
        dyb = dy.astype(BF16)
        dyb_ref[...] = dyb
        dh = jnp.zeros((tm, D), F32)
        for cidx in range(NCH):
            gate = gu_ref[cidx].astype(F32)
            up = gu_ref[NCH + cidx].astype(F32)
            sig = _sigmoid(gate)
            s = gate * sig
            act_ref[cidx] = (s * up).astype(BF16)
            d_act = _dot_nt(dyb, wout_ref[cidx])
            d_up = (d_act * s).astype(BF16)
            d_gate = (d_act * up * (sig * (1.0 + gate * (1.0 - sig)))).astype(BF16)
            dg_ref[cidx] = d_gate
            dg_ref[NCH + cidx] = d_up
            dh = dh + _dot(d_gate, win_ref[cidx]) + _dot(d_up, win_ref[NCH + cidx])
        d_sc = _colsum(dh * n)
        d_sh = _colsum(dh)
        dn = dh * (1.0 + sc)
        d_gpre = _colsum(dn * xh)
        dxh = dn * gpre
        dx_ref[...] = dxo_v + r * (dxh - xh * _rowmean(dxh * xh))

        @pl.when(i % tps == 0)
        def _():
            mg_ref[...] = jnp.zeros((8, D), F32)

        @pl.when(i == 0)
        def _():
            vg_ref[...] = jnp.zeros((8, D), F32)

        mg_ref[0:1, :] += d_sh
        mg_ref[1:2, :] += d_sc
        mg_ref[2:3, :] += d_gt
        vg_ref[0:1, :] += d_gpre
        vg_ref[1:2, :] += d_gpost

    tile = pl.BlockSpec((tm, D), lambda i: (i, 0))
    return _call(
        core, name=name, grid=(nt,), jobs=jobs,
        in_specs=[tile, tile, tile, pl.BlockSpec((8, tm, FB), lambda i: (0, i, 0)),
                  pl.BlockSpec((None, 8, D), lambda i: (i // tps, 0, 0)), _const_spec((8, D)),
                  _const_spec((8, FB, D)), _const_spec((4, FB, D))],
        out_specs=[tile, pl.BlockSpec((8, tm, FB), lambda i: (0, i, 0)),
                   pl.BlockSpec((4, tm, FB), lambda i: (0, i, 0)), tile, tile,
                   pl.BlockSpec((None, 8, D), lambda i: (i // tps, 0, 0)), pl.BlockSpec((8, D), lambda i: (0, 0))],
        out_shape=[jax.ShapeDtypeStruct((T, D), F32), jax.ShapeDtypeStruct((8, T, FB), BF16),
                   jax.ShapeDtypeStruct((4, T, FB), BF16), jax.ShapeDtypeStruct((T, D), BF16),
                   jax.ShapeDtypeStruct((T, D), BF16), jax.ShapeDtypeStruct((nb, 8, D), F32),
                   jax.ShapeDtypeStruct((8, D), F32)],
        args=[dxo, x, y, gu, mod, gvec, w_in, w_out])


def _ffn_last(x, target, mod, gvec, w_in, w_out, tm, name, jobs=()):
    T = x.shape[0]
    nt = T // tm
    nb = mod.shape[0]
    tps = nt // nb

    def core(ins, outs, scs):
        x_ref, t_ref, mod_ref, g_ref, wina_ref, winb_ref, wout_ref = ins
        dx_ref, dg_ref, act_ref, hb_ref, dyb_ref, mg_ref, vg_ref, loss_ref = outs
        hd2 = w_in[0].shape[2]
        (gu_s,) = scs
        i = pl.program_id(0)
        xv = x_ref[...]
        sh, sc, gt = mod_ref[0:1, :], mod_ref[1:2, :], mod_ref[2:3, :]
        gpre, gpost = g_ref[0:1, :], g_ref[1:2, :]
        r = lax.rsqrt(_rowmean(xv * xv) + EPS)
        xh = xv * r
        n = xh * gpre
        hb = (n * (1.0 + sc) + sh).astype(BF16)
        hb_ref[...] = hb
        hba, hbb = hb[:, 0:hd2], hb[:, hd2:D]
        yv = jnp.zeros((tm, D), F32)
        for cidx in range(NCH):
            gate = _dot_nt(hba, wina_ref[cidx]) + _dot_nt(hbb, winb_ref[cidx])
            up = _dot_nt(hba, wina_ref[NCH + cidx]) + _dot_nt(hbb, winb_ref[NCH + cidx])
            gu_s[cidx] = gate.astype(BF16)
            gu_s[NCH + cidx] = up.astype(BF16)
            act = gate * _sigmoid(gate) * up
            act_ref[cidx] = act.astype(BF16)
            yv = yv + _dot(act_ref[cidx], wout_ref[cidx])
        ry = lax.rsqrt(_rowmean(yv * yv) + EPS)
        yh = yv * ry
        pn = yh * gpost
        err = xv + (HALF * gt) * pn - t_ref[...]
        dxo_v = err * (1.0 / D)
        d_gt = _colsum(HALF * dxo_v * pn)
        dp = (HALF * gt) * dxo_v
        d_gpost = _colsum(dp * yh)
        dyh = dp * gpost
        dyb = (ry * (dyh - yh * _rowmean(dyh * yh))).astype(BF16)
        dyb_ref[...] = dyb
        dha = jnp.zeros((tm, hd2), F32)
        dhb = jnp.zeros((tm, D - hd2), F32)
        for cidx in range(NCH):
            gate = gu_s[cidx].astype(F32)
            up = gu_s[NCH + cidx].astype(F32)
            sig = _sigmoid(gate)
            s = gate * sig
            d_act = _dot_nt(dyb, wout_ref[cidx])
            d_up = (d_act * s).astype(BF16)
            d_gate = (d_act * up * (sig * (1.0 + gate * (1.0 - sig)))).astype(BF16)
            dg_ref[cidx] = d_gate
            dg_ref[NCH + cidx] = d_up
            dha = dha + _dot(d_gate, wina_ref[cidx]) + _dot(d_up, wina_ref[NCH + cidx])
            dhb = dhb + _dot(d_gate, winb_ref[cidx]) + _dot(d_up, winb_ref[NCH + cidx])
        dh = jnp.concatenate([dha, dhb], axis=1)
        d_sc = _colsum(dh * n)
        d_sh = _colsum(dh)
        dn = dh * (1.0 + sc)
        d_gpre = _colsum(dn * xh)
        dxh = dn * gpre
        dx_ref[...] = dxo_v + r * (dxh - xh * _rowmean(dxh * xh))

        @pl.when(i % tps == 0)
        def _():
            mg_ref[...] = jnp.zeros((8, D), F32)

        @pl.when(i == 0)
        def _():
            vg_ref[...] = jnp.zeros((8, D), F32)
            loss_ref[...] = jnp.zeros((8, D), F32)

        mg_ref[0:1, :] += d_sh
        mg_ref[1:2, :] += d_sc
        mg_ref[2:3, :] += d_gt
        vg_ref[0:1, :] += d_gpre
        vg_ref[1:2, :] += d_gpost
        loss_ref[...] += HALF * jnp.sum(_rowmean(err * err), axis=0, keepdims=True)

    tile = pl.BlockSpec((tm, D), lambda i: (i, 0))
    return _call(
        core, name=name, grid=(nt,), jobs=jobs,
        in_specs=[tile, tile, pl.BlockSpec((None, 8, D), lambda i: (i // tps, 0, 0)), _const_spec((8, D)),
                  _const_spec(w_in[0].shape), _const_spec(w_in[1].shape), _const_spec((4, FB, D))],
        out_specs=[tile, pl.BlockSpec((8, tm, FB), lambda i: (0, i, 0)),
                   pl.BlockSpec((4, tm, FB), lambda i: (0, i, 0)), tile, tile,
                   pl.BlockSpec((None, 8, D), lambda i: (i // tps, 0, 0)), pl.BlockSpec((8, D), lambda i: (0, 0)),
                   pl.BlockSpec((8, D), lambda i: (0, 0))],
        out_shape=[jax.ShapeDtypeStruct((T, D), F32), jax.ShapeDtypeStruct((8, T, FB), BF16),
                   jax.ShapeDtypeStruct((4, T, FB), BF16), jax.ShapeDtypeStruct((T, D), BF16),
                   jax.ShapeDtypeStruct((T, D), BF16), jax.ShapeDtypeStruct((nb, 8, D), F32),
                   jax.ShapeDtypeStruct((8, D), F32), jax.ShapeDtypeStruct((8, D), F32)],
        scratch=[pltpu.VMEM((8, tm, FB), BF16)],
        args=[x, target, mod, gvec, w_in[0], w_in[1], w_out])


def _masked_spatial(ws_ref):
    row = lax.broadcasted_iota(jnp.int32, (CHUNK, CHUNK), 0)
    col = lax.broadcasted_iota(jnp.int32, (CHUNK, CHUNK), 1)
    keep = col <= row
    return [jnp.where(keep, ws_ref[hd], 0.0).astype(BF16) for hd in range(NHEAD)]


def _head_pairs(mats, right, transpose=False):
    first = lax.broadcasted_iota(jnp.int32, (CHUNK, LANES), 1) < HD
    op = _dot_tn if transpose else _dot
    out = []
    for p in range(NHEAD // 2):
        slab = right[:, _lanes(p)]
        out.append(jnp.where(first, op(mats[2 * p], slab), op(mats[2 * p + 1], slab)))
    return jnp.concatenate(out, axis=1)


def _spatial_gate(wm, vb_chunk):
    return _head_pairs(wm, vb_chunk)


def _layer_norm_stats(v):
    mu = _rowmean(v)
    vc = v - mu
    rstd = lax.rsqrt(_rowmean(vc * vc) + EPS)
    return vc * rstd, rstd


def _pitch(tm):
    p = tm // 8
    while p % 8 != 4:
        p += 1
    return p


def _lanes(s):
    return slice(s * LANES, (s + 1) * LANES)


def _to_slabs(ref, row0, val):
    for s in range(NSLAB):
        ref[s, row0:row0 + val.shape[0], :] = val[:, _lanes(s)]


def _tap_sum(src, out, cw_ref, bias, tm, start):
    p = _pitch(tm)
    for s in range(NSLAB):
        accs = [jnp.broadcast_to(bias[:, _lanes(s)], (SUBL, LANES))] * p
        for k in range(CONV_K):
            w = jnp.broadcast_to(cw_ref[k:k + 1, _lanes(s)], (SUBL, LANES))
            for v in range(p):
                accs[v] = accs[v] + w * src[s, pl.ds(v + start(k), 8, stride=p), :]
        for v in range(p):
            out[s, pl.ds(v, 8, stride=p), :] = accs[v]
    return jnp.concatenate([out[s, 0:tm, :] for s in range(NSLAB)], axis=1)


def _mixer_fwd(x, mod, gvec, w_mi, w_mo, v512, ws, bias_full, cw, tm, name, jobs=()):
    T = x.shape[0]
    nt = T // tm
    tps = nt // mod.shape[0]
    ext_rows = 8 * _pitch(tm)

    def core(ins, outs, scs):
        x_ref, mod_ref, g_ref, wmi_ref, wmo_ref, v_ref, ws_ref, bias_ref, cw_ref = ins
        xo_ref, proj_ref, ym_ref, conv_ref = outs
        glu_ext, conv_scr = scs
        i = pl.program_id(0)
        xv = x_ref[...]
        sh, sc, gt = mod_ref[0:1, :], mod_ref[1:2, :], mod_ref[2:3, :]
        r = lax.rsqrt(_rowmean(xv * xv) + EPS)
        hb = ((xv * r * g_ref[0:1, :]) * (1.0 + sc) + sh).astype(BF16)
        for j in range(NDEV):
            proj_ref[:, j * MB:(j + 1) * MB] = _dot(hb, wmi_ref[j])
        u = proj_ref[:, 0:WA]
        v0 = proj_ref[:, WA:2 * WA]
        a = proj_ref[:, 2 * WA:3 * WA]
        g = proj_ref[:, 3 * WA:4 * WA]
        vh, _ = _layer_norm_stats(v0)
        vb = (vh * v_ref[0:1, :] + v_ref[1:2, :]).astype(BF16)
        wm = _masked_spatial(ws_ref)
        ya = []
        for q in range(tm // CHUNK):
            z = _spatial_gate(wm, vb[q * CHUNK:(q + 1) * CHUNK, :]) + bias_ref[...]
            ya.append(u[q * CHUNK:(q + 1) * CHUNK, :] * z)
        ya = jnp.concatenate(ya, axis=0)
        glu = a * _sigmoid(g)

        @pl.when(i == 0)
        def _():
            glu_ext[:, HALO + tm:HALO + ext_rows, :] = jnp.zeros((NSLAB, ext_rows - tm, LANES), F32)

        @pl.when(i % tps == 0)
        def _():
            glu_ext[:, 0:HALO, :] = jnp.zeros((NSLAB, HALO, LANES), F32)

        _to_slabs(glu_ext, HALO, glu)
        conv = _tap_sum(glu_ext, conv_scr, cw_ref, v_ref[2:3, :], tm, lambda k: HALO - (CONV_K - 1) + k)
        conv_ref[...] = conv
        glu_ext[:, 0:HALO, :] = glu_ext[:, tm:tm + HALO, :]
        ch, _ = _layer_norm_stats(conv)
        cn = ch * v_ref[3:4, :] + v_ref[4:5, :]
        yb = cn * _sigmoid(cn)
        pa = ya * lax.rsqrt(_rowmean(ya * ya) + EPS) * v_ref[5:6, :]
        pb = yb * lax.rsqrt(_rowmean(yb * yb) + EPS) * v_ref[6:7, :]
        ycat = jnp.concatenate([pa, pb], axis=1).astype(BF16)
        ym = _dot(ycat, wmo_ref[...])
        ym_ref[...] = ym
        rm = lax.rsqrt(_rowmean(ym * ym) + EPS)
        xo_ref[...] = xv + gt * (ym * rm * g_ref[1:2, :])

    tile = pl.BlockSpec((tm, D), lambda i: (i, 0))
    return _call(
        core, name=name, grid=(nt,), jobs=jobs,
        in_specs=[tile, pl.BlockSpec((None, 8, D), lambda i: (i // tps, 0, 0)), _const_spec((8, D)),
                  _const_spec((NDEV, D, MB)), _const_spec((D, D)), _const_spec((8, WA)),
                  _const_spec((NHEAD, CHUNK, CHUNK)), _const_spec((CHUNK, WA)), _const_spec((32, WA))],
        out_specs=[tile, pl.BlockSpec((tm, 4 * WA), lambda i: (i, 0)), tile, pl.BlockSpec((tm, WA), lambda i: (i, 0))],
        out_shape=[jax.ShapeDtypeStruct((T, D), F32), jax.ShapeDtypeStruct((T, 4 * WA), F32),
                   jax.ShapeDtypeStruct((T, D), F32), jax.ShapeDtypeStruct((T, WA), F32)],
        scratch=[pltpu.VMEM((NSLAB, HALO + ext_rows, LANES), F32), pltpu.VMEM((NSLAB, ext_rows, LANES), F32)],
        args=[x, mod, gvec, w_mi, w_mo, v512, ws, bias_full, cw])


def _mixer_bwd_a(dxo, ym, proj, conv, mod, gvec, w_mo, v512, ws, bias_full, esel, tm, name, jobs=()):
    T = dxo.shape[0]
    nt = T // tm
    nb = mod.shape[0]
    tps = nt // nb

    def core(ins, outs, scs):
        dxo_ref, ym_ref, proj_ref, conv_ref, mod_ref, g_ref, wmo_ref, v_ref, ws_ref, bias_ref, e_ref = ins
        dpart_ref, dymb_ref, ycat_ref, mg_ref, vg_ref, v5g_ref, gws_ref, gbs_ref = outs
        (dbs_acc,) = scs
        i = pl.program_id(0)
        dxo_v = dxo_ref[...]
        ymv = ym_ref[...]
        gt = mod_ref[2:3, :]
        gpost = g_ref[1:2, :]
        rm = lax.rsqrt(_rowmean(ymv * ymv) + EPS)
        ymh = ymv * rm
        d_gt = _colsum(dxo_v * (ymh * gpost))
        dpm = gt * dxo_v
        d_gpost = _colsum(dpm * ymh)
        dymh = dpm * gpost
        dym = (rm * (dymh - ymh * _rowmean(dymh * ymh))).astype(BF16)
        dymb_ref[...] = dym
        dycat = _dot_nt(dym, wmo_ref[...])
        u = proj_ref[:, 0:WA]
        v0 = proj_ref[:, WA:2 * WA]
        vh, rv = _layer_norm_stats(v0)
        vb = (vh * v_ref[0:1, :] + v_ref[1:2, :]).astype(BF16)
        wm = _masked_spatial(ws_ref)
        zs = []
        for q in range(tm // CHUNK):
            zs.append(_spatial_gate(wm, vb[q * CHUNK:(q + 1) * CHUNK, :]) + bias_ref[...])
        z = jnp.concatenate(zs, axis=0)
        ya = u * z
        ra = lax.rsqrt(_rowmean(ya * ya) + EPS)
        yah = ya * ra
        ch, rc = _layer_norm_stats(conv_ref[...])
        cn = ch * v_ref[3:4, :] + v_ref[4:5, :]
        sg = _sigmoid(cn)
        yb = cn * sg
        rb = lax.rsqrt(_rowmean(yb * yb) + EPS)
        ybh = yb * rb
        ycat_ref[...] = jnp.concatenate([yah * v_ref[5:6, :], ybh * v_ref[6:7, :]], axis=1).astype(BF16)
        dpa = dycat[:, 0:WA]
        dpb = dycat[:, WA:2 * WA]
        d_goa = _colsum(dpa * yah)
        d_gob = _colsum(dpb * ybh)
        dyah = dpa * v_ref[5:6, :]
        dybh = dpb * v_ref[6:7, :]
        dya = ra * (dyah - yah * _rowmean(dyah * yah))
        dyb = rb * (dybh - ybh * _rowmean(dybh * ybh))
        dpart_ref[:, 0:WA] = dya * z
        dz = dya * u

        @pl.when(i == 0)
        def _():
            gws_ref[...] = jnp.zeros((NHEAD, CHUNK, CHUNK), F32)
            dbs_acc[...] = jnp.zeros((CHUNK, WA), F32)
            vg_ref[...] = jnp.zeros((8, D), F32)
            v5g_ref[...] = jnp.zeros((8, WA), F32)

        first = lax.broadcasted_iota(jnp.int32, (CHUNK, LANES), 1) < HD
        dvs = []
        for q in range(tm // CHUNK):
            dz_q = dz[q * CHUNK:(q + 1) * CHUNK, :]
            vb_q = vb[q * CHUNK:(q + 1) * CHUNK, :]
            dbs_acc[...] += dz_q
            dzb = dz_q.astype(BF16)
            dvs.append(_head_pairs(wm, dzb, transpose=True))
            for hd in range(NHEAD):
                slab = dzb[:, _lanes(hd // 2)]
                dz_hd = jnp.where(first if hd % 2 == 0 else jnp.logical_not(first), slab, jnp.zeros_like(slab))
                gws_ref[hd] += _dot_nt(dz_hd, vb_q[:, _lanes(hd // 2)])
        dv = jnp.concatenate(dvs, axis=0)
        d_gng = _colsum(dv * vh)
        d_gnb = _colsum(dv)
        dvh = dv * v_ref[0:1, :]
        dpart_ref[:, WA:2 * WA] = rv * (dvh - _rowmean(dvh) - vh * _rowmean(dvh * vh))
        dcn = dyb * (sg * (1.0 + cn * (1.0 - sg)))
        d_cng = _colsum(dcn * ch)
        d_cnb = _colsum(dcn)
        dch = dcn * v_ref[3:4, :]
        dconv = rc * (dch - _rowmean(dch) - ch * _rowmean(dch * ch))
        dpart_ref[:, 2 * WA:3 * WA] = dconv
        dpart_ref[:, 3 * WA:4 * WA] = jnp.zeros((tm, WA), F32)
        d_cb = _colsum(dconv)

        @pl.when(i % tps == 0)
        def _():
            mg_ref[...] = jnp.zeros((8, D), F32)

        mg_ref[2:3, :] += d_gt
        vg_ref[1:2, :] += d_gpost
        v5g_ref[0:1, :] += d_gng
        v5g_ref[1:2, :] += d_gnb
        v5g_ref[2:3, :] += d_cb
        v5g_ref[3:4, :] += d_cng
        v5g_ref[4:5, :] += d_cnb
        v5g_ref[5:6, :] += d_goa
        v5g_ref[6:7, :] += d_gob

        @pl.when(i == nt - 1)
        def _():
            row = lax.broadcasted_iota(jnp.int32, (CHUNK, CHUNK), 0)
            col = lax.broadcasted_iota(jnp.int32, (CHUNK, CHUNK), 1)
            for hd in range(NHEAD):
                gws_ref[hd] = jnp.where(col <= row, gws_ref[hd], 0.0)
            gbs_ref[...] = lax.dot_general(e_ref[...], dbs_acc[...], (((1,), (1,)), ((), ())),
                                           precision=lax.Precision.HIGHEST, preferred_element_type=F32)

    tile = pl.BlockSpec((tm, D), lambda i: (i, 0))
    ptile = pl.BlockSpec((tm, 4 * WA), lambda i: (i, 0))
    return _call(
        core, name=name, grid=(nt,), jobs=jobs,
        in_specs=[tile, tile, pl.BlockSpec((tm, 2 * WA), lambda i: (i, 0)), pl.BlockSpec((tm, WA), lambda i: (i, 0)),
                  pl.BlockSpec((None, 8, D), lambda i: (i // tps, 0, 0)), _const_spec((8, D)), _const_spec((D, D)),
                  _const_spec((8, WA)), _const_spec((NHEAD, CHUNK, CHUNK)), _const_spec((CHUNK, WA)),
                  _const_spec((8, WA))],
        out_specs=[ptile, tile, tile, pl.BlockSpec((None, 8, D), lambda i: (i // tps, 0, 0)),
                   pl.BlockSpec((8, D), lambda i: (0, 0)), pl.BlockSpec((8, WA), lambda i: (0, 0)),
                   pl.BlockSpec((NHEAD, CHUNK, CHUNK), lambda i: (0, 0, 0)), pl.BlockSpec((8, CHUNK), lambda i: (0, 0))],
        out_shape=[jax.ShapeDtypeStruct((T, 4 * WA), F32), jax.ShapeDtypeStruct((T, D), BF16),
                   jax.ShapeDtypeStruct((T, D), BF16), jax.ShapeDtypeStruct((nb, 8, D), F32),
                   jax.ShapeDtypeStruct((8, D), F32), jax.ShapeDtypeStruct((8, WA), F32),
                   jax.ShapeDtypeStruct((NHEAD, CHUNK, CHUNK), F32), jax.ShapeDtypeStruct((8, CHUNK), F32)],
        scratch=[pltpu.VMEM((CHUNK, WA), F32)],
        args=[dxo, ym, proj, conv, mod, gvec, w_mo, v512, ws, bias_full, esel])


def _mixer_bwd_b(dxo, x, dpart, proj, mod, gvec, w_mi, cw, tm, name, jobs=()):
    T = x.shape[0]
    nt = T // tm
    nb = mod.shape[0]
    tps = nt // nb
    hpt = tm // HALO
    nh = T // HALO
    off = HALO - (CONV_K - 1)
    p = _pitch(tm)
    ext_rows = 8 * p

    def core(ins, outs, scs):
        dxo_ref, x_ref, dpart_ref, dnext_ref, ag_ref, halo_ref, mod_ref, g_ref, wmi_ref, cw_ref = ins
        dx_ref, dproj_ref, hb_ref, mg_ref, vg_ref, dcw_ref = outs
        glu_ext, dconv_ext, dglu_scr, dcw_acc = scs
        i = pl.program_id(0)
        first = i % tps == 0
        last = i % tps == tps - 1
        a = ag_ref[:, 0:WA]
        g = ag_ref[:, WA:2 * WA]
        sgg = _sigmoid(g)

        @pl.when(i == 0)
        def _():
            glu_ext[:, HALO + tm:HALO + ext_rows, :] = jnp.zeros((NSLAB, ext_rows - tm, LANES), F32)
            dconv_ext[:, HALO + tm:HALO + ext_rows, :] = jnp.zeros((NSLAB, ext_rows - tm, LANES), F32)
            dcw_acc[...] = jnp.zeros((32, 8, WA), F32)
            vg_ref[...] = jnp.zeros((8, D), F32)

        _to_slabs(glu_ext, 0, jnp.where(first, 0.0, halo_ref[:, 0:WA] * _sigmoid(halo_ref[:, WA:2 * WA])))
        _to_slabs(glu_ext, HALO, a * sgg)
        _to_slabs(dconv_ext, 0, dpart_ref[:, 2 * WA:3 * WA])
        _to_slabs(dconv_ext, tm, jnp.where(last, 0.0, dnext_ref[...]))
        sub = lax.broadcasted_iota(jnp.int32, (SUBL, LANES), 0)
        for s in range(NSLAB):
            accs = [jnp.zeros((SUBL, LANES), F32)] * CONV_K
            for v in range(p):
                dc = jnp.where(v + p * sub < tm, dconv_ext[s, pl.ds(v, 8, stride=p), :], 0.0)
                for k in range(CONV_K):
                    accs[k] = accs[k] + dc * glu_ext[s, pl.ds(v + off + k, 8, stride=p), :]
            for k in range(CONV_K):
                dcw_acc[k, :, _lanes(s)] += accs[k]
        dglu = _tap_sum(dconv_ext, dglu_scr, cw_ref, jnp.zeros((1, WA), F32), tm, lambda k: (CONV_K - 1) - k)

        @pl.when(i == nt - 1)
        def _():
            for k in range(CONV_K):
                dcw_ref[k:k + 1, :] = jnp.sum(dcw_acc[k], axis=0, keepdims=True)
            dcw_ref[CONV_K:32, :] = jnp.zeros((32 - CONV_K, WA), F32)

        da = dglu * sgg
        dgg = dglu * a * (sgg * (1.0 - sgg))
        dproj_ref[:, 0:2 * WA] = dpart_ref[:, 0:2 * WA].astype(BF16)
        dproj_ref[:, 2 * WA:3 * WA] = da.astype(BF16)
        dproj_ref[:, 3 * WA:4 * WA] = dgg.astype(BF16)
        dh = jnp.zeros((tm, D), F32)
        for j in range(NDEV):
            dh = dh + _dot_nt(dproj_ref[:, j * MB:(j + 1) * MB], wmi_ref[j])
        xv = x_ref[...]
        sc, sh = mod_ref[1:2, :], mod_ref[0:1, :]
        gpre = g_ref[0:1, :]
        r = lax.rsqrt(_rowmean(xv * xv) + EPS)
        xh = xv * r
        n = xh * gpre
        hb_ref[...] = (n * (1.0 + sc) + sh).astype(BF16)
        d_sc = _colsum(dh * n)
        d_sh = _colsum(dh)
        dn = dh * (1.0 + sc)
        d_gpre = _colsum(dn * xh)
        dxh = dn * gpre
        dx_ref[...] = dxo_ref[...] + r * (dxh - xh * _rowmean(dxh * xh))

        @pl.when(first)
        def _():
            mg_ref[...] = jnp.zeros((8, D), F32)

        mg_ref[0:1, :] += d_sh
        mg_ref[1:2, :] += d_sc
        vg_ref[0:1, :] += d_gpre

    tile = pl.BlockSpec((tm, D), lambda i: (i, 0))
    return _call(
        core, name=name, grid=(nt,), jobs=jobs,
        in_specs=[tile, tile, pl.BlockSpec((tm, 4 * WA), lambda i: (i, 0)),
                  pl.BlockSpec((HALO, WA), lambda i: (jnp.minimum((i + 1) * hpt, nh - 1), 2)),
                  pl.BlockSpec((tm, 2 * WA), lambda i: (i, 1)),
                  pl.BlockSpec((HALO, 2 * WA), lambda i: (jnp.maximum(i * hpt - 1, 0), 1)),
                  pl.BlockSpec((None, 8, D), lambda i: (i // tps, 0, 0)), _const_spec((8, D)),
                  _const_spec((NDEV, D, MB)), _const_spec((32, WA))],
        out_specs=[tile, pl.BlockSpec((tm, 4 * WA), lambda i: (i, 0)), tile,
                   pl.BlockSpec((None, 8, D), lambda i: (i // tps, 0, 0)), pl.BlockSpec((8, D), lambda i: (0, 0)),
                   pl.BlockSpec((32, WA), lambda i: (0, 0))],
        out_shape=[jax.ShapeDtypeStruct((T, D), F32), jax.ShapeDtypeStruct((T, 4 * WA), BF16),
                   jax.ShapeDtypeStruct((T, D), BF16), jax.ShapeDtypeStruct((nb, 8, D), F32),
                   jax.ShapeDtypeStruct((8, D), F32), jax.ShapeDtypeStruct((32, WA), F32)],
        scratch=[pltpu.VMEM((NSLAB, HALO + ext_rows, LANES), F32), pltpu.VMEM((NSLAB, HALO + ext_rows, LANES), F32),
                 pltpu.VMEM((NSLAB, ext_rows, LANES), F32), pltpu.VMEM((32, 8, WA), F32)],
        args=[dxo, x, dpart, dpart, proj, proj, mod, gvec, w_mi, cw])


def _grad_chip(a, b, a_spec, b_spec, prod_shape, half, name, jobs=(), via_b=False, after=None):
    steps = 8 if half is None else 4
    R = prod_shape[0] if half is None else half
    C = prod_shape[1]

    def core(ins, outs, scs):
        a_ref, b_ref = ins[:2]
        (o_ref,) = outs
        own, snd, rcv, ssem, rsem, lsem = scs
        s = pl.program_id(0)
        c = lax.axis_index("c")
        me = _me()
        sib = _flip(me, (0, 0, 1))
        if via_b:
            prod = _dot_tn(b_ref[...], a_ref[...]).T.astype(BF16)
        else:
            prod = _dot_tn(a_ref[...], b_ref[...]).astype(BF16)
        if half is None:
            q = s // 2

            @pl.when(s % 2 == c)
            def _():
                own[q] = prod

            @pl.when(s % 2 != c)
            def _():
                snd[q] = prod
                _remote(snd.at[q], rcv.at[q], ssem.at[q], rsem.at[q], sib).start()
        else:
            lo = prod[0:half, :]
            hi = prod[half:2 * half, :]
            own[s] = jnp.where(c == 0, lo, hi)
            snd[s] = jnp.where(c == 0, hi, lo)
            _remote(snd.at[s], rcv.at[s], ssem.at[s], rsem.at[s], sib).start()

        @pl.when(s == steps - 1)
        def _():
            for q4 in range(4):
                cp = _remote(snd.at[q4], rcv.at[q4], ssem.at[q4], rsem.at[q4], sib)
                cp.wait_recv()
                cp.wait_send()
                snd[q4] = (own[q4].astype(F32) + rcv[q4].astype(F32)).astype(BF16)
            out = pltpu.make_async_copy(snd, o_ref, lsem)
            out.start()
            out.wait()

    return _call(
        core, name=name, grid=(steps,), jobs=jobs, in_specs=[a_spec, b_spec] + [HBM] * (after is not None),
        out_specs=[HBM], out_shape=[jax.ShapeDtypeStruct((4, R, C), BF16)],
        scratch=[pltpu.VMEM((4, R, C), BF16), pltpu.VMEM((4, R, C), BF16), pltpu.VMEM((4, R, C), BF16),
                 pltpu.SemaphoreType.DMA((4,)), pltpu.SemaphoreType.DMA((4,)), pltpu.SemaphoreType.DMA],
        args=[a, b] + [after] * (after is not None))


def _grad_w_in(dg, hb, name, jobs=()):
    T = hb.shape[0]
    return _grad_chip(dg, hb, pl.BlockSpec((None, T, FB), lambda s: (s, 0, 0)), _const_spec((T, D)),
                      (FB, D), None, name, jobs)


def _grad_w_out(act, dyb, name, jobs=(), after=None):
    T = dyb.shape[0]
    return _grad_chip(act, dyb, pl.BlockSpec((None, T, FB), lambda s: (s, 0, 0)), _const_spec((T, D)),
                      (FB, D), FO, name, jobs, after=after)


def _grad_w_mi(hb, dproj, name, jobs=()):
    T = hb.shape[0]
    return _grad_chip(hb, dproj, _const_spec((T, D)), pl.BlockSpec((T, MB), lambda s: (0, s)),
                      (D, MB), None, name, jobs, via_b=True)


def _grad_w_mo(ycat, dym, name, jobs=()):
    T = ycat.shape[0]
    return _grad_chip(ycat, dym, pl.BlockSpec((T, 2 * MO), lambda s: (0, s)), _const_spec((T, D)),
                      (2 * MO, D), MO, name, jobs)


def _adamw_math(w, g, m, v):
    m2 = ADAM_B1 * m + (1.0 - ADAM_B1) * g
    v2 = ADAM_B2 * v + (1.0 - ADAM_B2) * (g * g)
    m_hat = m2 / (1.0 - ADAM_B1 ** ADAM_STEP)
    v_hat = v2 / (1.0 - ADAM_B2 ** ADAM_STEP)
    delta = -ADAM_LR * (m_hat / (jnp.sqrt(v_hat) + ADAM_EPS) + ADAM_WD * w)
    return delta, m2, v2


def _adamw_reduce(parts, w, m, v, tr, name, after=None):
    R, C = w.shape

    def core(ins, outs, _):
        p_ref, w_ref, m_ref, v_ref = ins[:4]
        g_ref, d_ref, m2_ref, v2_ref = outs
        g = p_ref[0].astype(F32)
        for s in range(1, 4):
            g = g + p_ref[s].astype(F32)
        g_ref[...] = g
        d_ref[...], m2_ref[...], v2_ref[...] = _adamw_math(w_ref[...], g, m_ref[...], v_ref[...])

    blk = pl.BlockSpec((tr, C), lambda i: (i, 0))
    in_specs = [pl.BlockSpec((4, tr, C), lambda i: (0, i, 0)), blk, blk, blk]
    args = [parts, w, m, v]
    if after is not None:
        in_specs.append(HBM)
        args.append(after)
    return _call(
        core, name=name, grid=(R // tr,), in_specs=in_specs,
        out_specs=[blk, blk, blk, blk], out_shape=[jax.ShapeDtypeStruct((R, C), F32)] * 4, args=args)[0]


HBM_ONLY = pl.BlockSpec(memory_space=pltpu.HBM)
SEM = pl.BlockSpec(memory_space=pltpu.SEMAPHORE)
EFFECT = pltpu.SideEffectType.DATAFLOW_SIDE_EFFECTING


def _chip_scatter_start(gs, name):
    n = len(gs)

    def body(*refs):
        g_refs, land_refs = refs[:n], refs[n:2 * n]
        ssem, rsem = refs[2 * n:2 * n + 2]
        token = refs[-1]
        me = _me()
        mq = 2 * me[0] + me[1]
        for k, f in enumerate(CHIP_FLIPS):
            p = _flip(me, f)
            for a in range(n):
                _remote(g_refs[a].at[2 * p[0] + p[1]], land_refs[a].at[mq], ssem.at[4 * a + k], rsem.at[3 * a + k], p).start()
        for a in range(n):
            pltpu.make_async_copy(g_refs[a].at[mq], land_refs[a].at[mq], ssem.at[4 * a + 3]).start()
        token[...] = jnp.zeros_like(token)

    gs = [pltpu.with_memory_space_constraint(g, pltpu.HBM) for g in gs]
    lands = [pltpu.with_memory_space_constraint(lax.empty(g.shape, g.dtype), pltpu.HBM) for g in gs]
    res = pl.pallas_call(
        body, name=name,
        out_shape=(pltpu.SemaphoreType.DMA((4 * n,)), pltpu.SemaphoreType.DMA((3 * n,)))
        + tuple(pltpu.HBM(g.shape, g.dtype) for g in gs) * 2 + (jax.ShapeDtypeStruct((SUBL, LANES), F32),),
        in_specs=(HBM_ONLY,) * (2 * n), out_specs=(SEM, SEM) + (HBM_ONLY,) * (2 * n) + (VM,),
        input_output_aliases={a: 2 + a for a in range(2 * n)},
        compiler_params=pltpu.CompilerParams(has_side_effects=EFFECT),
    )(*gs, *lands)
    return res[:-1], res[-1]


def _chip_scatter_wait(handle, after, name):
    ssem, rsem = handle[:2]
    n = (len(handle) - 2) // 2
    thru = handle[2:]

    def body(*refs):
        g_refs, land_refs = refs[:n], refs[n:2 * n]
        ssem, rsem = refs[2 * n:2 * n + 2]
        me = _me()
        mq = 2 * me[0] + me[1]
        for k, f in enumerate(CHIP_FLIPS):
            p = _flip(me, f)
            pq = 2 * p[0] + p[1]
            for a in range(n):
                _remote(g_refs[a].at[pq], land_refs[a].at[mq], ssem.at[4 * a + k], rsem.at[3 * a + k], p).wait_send()
                _remote(g_refs[a].at[mq], land_refs[a].at[pq], ssem.at[4 * a + k], rsem.at[3 * a + k], p).wait_recv()
        for a in range(n):
            pltpu.make_async_copy(g_refs[a].at[mq], land_refs[a].at[mq], ssem.at[4 * a + 3]).wait()

    res = pl.pallas_call(
        body, name=name,
        out_shape=tuple(pltpu.HBM(t.shape, t.dtype) for t in thru),
        in_specs=(HBM_ONLY,) * (2 * n) + (SEM, SEM, HBM), out_specs=(HBM_ONLY,) * (2 * n),
        input_output_aliases={a: a for a in range(2 * n)},
        compiler_params=pltpu.CompilerParams(has_side_effects=EFFECT),
    )(*thru, ssem, rsem, after)
    return list(res[:n]), list(res[n:])


def _adamw_ada(sc_all, dd, w, m, v, tr, name, after=None):
    R, C = w.shape

    def core(ins, outs, _):
        sc_ref, dd_ref, w_ref, m_ref, v_ref = ins[:5]
        g_ref, d_ref, m2_ref, v2_ref = outs
        g = _dot_tn(sc_ref[...].astype(BF16), dd_ref[...].astype(BF16))
        g_ref[...] = g
        d_ref[...], m2_ref[...], v2_ref[...] = _adamw_math(w_ref[...], g, m_ref[...], v_ref[...])

    blk = pl.BlockSpec((tr, C), lambda i: (i, 0))
    return _call(
        core, name=name, grid=(R // tr,),
        in_specs=[pl.BlockSpec((64, tr), lambda i: (0, i)), pl.BlockSpec((64, C), lambda i: (0, 0)), blk, blk, blk]
        + [HBM] * (after is not None),
        out_specs=[blk, blk, blk, blk], out_shape=[jax.ShapeDtypeStruct((R, C), F32)] * 4,
        args=[sc_all, dd, w, m, v] + [after] * (after is not None))[0]


def _adamw_small(gathered, plain, grads, wmv, emit, name, after=None):
    nw = len(grads)
    ng, npl, ne = len(gathered), len(plain), len(emit)

    def core(ins, outs, _):
        srcs = []
        for a in range(ng):
            s = ins[a][0]
            for dev in range(1, NDEV):
                s = s + ins[a][dev]
            srcs.append(s)
        srcs += [ins[ng + a][...] for a in range(npl)]
        w_refs = ins[ng + npl:]
        for e, a in enumerate(emit):
            outs[e][...] = srcs[a]
        for t in range(nw):
            src, row = grads[t]
            g = srcs[src] if row is None else srcs[src][row:row + 1, :]
            w_ref, m_ref, v_ref = w_refs[3 * t:3 * t + 3]
            g_ref, d_ref, m2_ref, v2_ref = outs[ne + 4 * t:ne + 4 * t + 4]
            g_ref[...] = g
            d_ref[...], m2_ref[...], v2_ref[...] = _adamw_math(w_ref[...], g, m_ref[...], v_ref[...])

    out_shape = [jax.ShapeDtypeStruct(gathered[a].shape[1:], F32) for a in emit]
    for t in range(nw):
        out_shape += [jax.ShapeDtypeStruct(wmv[3 * t].shape, F32)] * 4
    return _call(
        core, name=name, grid=(), in_specs=[VM] * (ng + npl + 3 * nw) + [HBM] * (after is not None),
        out_specs=[VM] * (ne + 4 * nw), out_shape=out_shape,
        args=list(gathered) + list(plain) + list(wmv) + [after] * (after is not None))[0]


def _ada_pieces():
    out = []
    for r in range(9):
        pos = r * D
        while pos < (r + 1) * D:
            j = pos // ADA_B
            nxt = min((r + 1) * D, (j + 1) * ADA_B)
            out.append((r, j, pos - r * D, pos - j * ADA_B, nxt - pos))
            pos = nxt
    return out


def _ada_fwd(c_pad, w_ada, b_cols, cw_pad, bl, jobs=()):
    def core(ins, outs, scs, start_jobs, finish_jobs):
        c_ref, w_ref, b_ref, cwp_ref = ins
        mod_refs, (sc_ref, cw_ref) = outs[0:3], outs[3:5]
        ada_ref, cbuf, send_buf, ssem, rsem = scs
        me = _me()
        mi = _lin(me)
        cbuf[mi] = c_ref[...]
        cw_ref[mi] = cwp_ref[...]
        peers = [_flip(me, f) for f in FLIPS]
        first = []
        for k, p in enumerate(peers):
            first.append(_remote(cbuf.at[mi], cbuf.at[mi], ssem.at[k], rsem.at[k], p))
            first.append(_remote(cw_ref.at[mi], cw_ref.at[mi], ssem.at[7 + k], rsem.at[7 + k], p))
        for cp in first:
            cp.start()
        start_jobs()
        for k, p in enumerate(peers):
            pi = _lin(p)
            _remote(cbuf.at[pi], cbuf.at[pi], ssem.at[k], rsem.at[k], p).wait_recv()
            _remote(cw_ref.at[pi], cw_ref.at[pi], ssem.at[7 + k], rsem.at[7 + k], p).wait_recv()
        c_all = cbuf[...].reshape(8 * 8, D)
        sc = c_all * _sigmoid(c_all)
        sc_ref[...] = sc
        res = _dot(sc.astype(BF16), w_ref[...].astype(BF16)) + b_ref[...]
        send_buf[...] = res.reshape(8, 8, ADA_B)
        ada_ref[mi] = send_buf[mi]
        second = []
        for k, p in enumerate(peers):
            second.append(_remote(send_buf.at[_lin(p)], ada_ref.at[mi], ssem.at[14 + k], rsem.at[14 + k], p))
        for cp in second:
            cp.start()
        finish_jobs()
        for k, p in enumerate(peers):
            _remote(send_buf.at[mi], ada_ref.at[_lin(p)], ssem.at[14 + k], rsem.at[14 + k], p).wait_recv()
        for m_ref in mod_refs:
            m_ref[...] = jnp.zeros_like(m_ref)
        for r, j, in_row, in_blk, width in _ada_pieces():
            for b in range(bl):
                mod_refs[r // 3][b, r % 3:r % 3 + 1, in_row:in_row + width] = ada_ref[j, b:b + 1, in_blk:in_blk + width]
        for cp in first + second:
            cp.wait_send()

    return _call(
        core, name="ada_fwd", grid=(), jobs=jobs, core_starts=True, in_specs=[VM, VM, VM, VM], out_specs=[VM] * 5,
        out_shape=[jax.ShapeDtypeStruct((bl, 8, D), F32)] * 3
        + [jax.ShapeDtypeStruct((64, D), F32), jax.ShapeDtypeStruct((8, 32, 64), F32)],
        scratch=[pltpu.VMEM((8, 8, ADA_B), F32), pltpu.VMEM((8, 8, D), F32), pltpu.VMEM((8, 8, ADA_B), F32),
                 pltpu.SemaphoreType.DMA((21,)), pltpu.SemaphoreType.DMA((21,))],
        args=[c_pad, w_ada, b_cols, cw_pad])


def _ada_bwd(mods, rows, jobs=()):
    bl = mods[0].shape[0]

    def core(ins, outs, scs):
        dd_ref, gb_ref = outs
        d_ref, rbuf, ssem, rsem = scs
        me = _me()
        mi = _lin(me)
        peers = [_flip(me, f) for f in FLIPS]
        d_ref[...] = jnp.zeros_like(d_ref)
        for r, j, in_row, in_blk, width in _ada_pieces():
            a, row = rows[r]
            for b in range(bl):
                d_ref[j, b:b + 1, in_blk:in_blk + width] = ins[a][b, row:row + 1, in_row:in_row + width]
        rbuf[mi] = d_ref[mi]
        first = []
        for k, p in enumerate(peers):
            first.append(_remote(d_ref.at[_lin(p)], rbuf.at[mi], ssem.at[k], rsem.at[k], p))
        for cp in first:
            cp.start()
        for k, p in enumerate(peers):
            _remote(d_ref.at[mi], rbuf.at[_lin(p)], ssem.at[k], rsem.at[k], p).wait_recv()
        dd = rbuf[...].reshape(64, ADA_B)
        dd_ref[...] = dd
        gb_ref[...] = jnp.broadcast_to(_colsum(dd), (8, ADA_B))
        for cp in first:
            cp.wait_send()

    return _call(
        core, name="ada_bwd", grid=(), jobs=jobs, in_specs=[VM] * len(mods), out_specs=[VM, VM],
        out_shape=[jax.ShapeDtypeStruct((64, ADA_B), F32), jax.ShapeDtypeStruct((8, ADA_B), F32)],
        scratch=[pltpu.VMEM((8, 8, ADA_B), F32), pltpu.VMEM((8, 8, ADA_B), F32),
                 pltpu.SemaphoreType.DMA((7,)), pltpu.SemaphoreType.DMA((7,))],
        args=list(mods))


SMALL_D = ("g_pre_f1", "g_post_f1", "g_pre_m", "g_post_m", "g_pre_f2", "g_post_f2")
SMALL_W = ("gmlp_norm_g", "gmlp_norm_b", "conv_b", "conv_norm_g", "conv_norm_b", "g_out_a", "g_out_b")


def kernel(x, c, w_ada, b_ada, g_pre_f1, g_post_f1, w_f1_in, w_f1_out, g_pre_m, g_post_m, w_mix_in, gmlp_norm_g, gmlp_norm_b, w_spatial, b_spatial, conv_w, conv_b, conv_norm_g, conv_norm_b, g_out_a, g_out_b, w_mix_out, g_pre_f2, g_post_f2, w_f2_in, w_f2_out, loss_target, m_w_ada, m_b_ada, m_g_pre_f1, m_g_post_f1, m_w_f1_in, m_w_f1_out, m_g_pre_m, m_g_post_m, m_w_mix_in, m_gmlp_norm_g, m_gmlp_norm_b, m_w_spatial, m_b_spatial, m_conv_w, m_conv_b, m_conv_norm_g, m_conv_norm_b, m_g_out_a, m_g_out_b, m_w_mix_out, m_g_pre_f2, m_g_post_f2, m_w_f2_in, m_w_f2_out, v_w_ada, v_b_ada, v_g_pre_f1, v_g_post_f1, v_w_f1_in, v_w_f1_out, v_g_pre_m, v_g_post_m, v_w_mix_in, v_gmlp_norm_g, v_gmlp_norm_b, v_w_spatial, v_b_spatial, v_conv_w, v_conv_b, v_conv_norm_g, v_conv_norm_b, v_g_out_a, v_g_out_b, v_w_mix_out, v_g_pre_f2, v_g_post_f2, v_w_f2_in, v_w_f2_out):
    given = dict(locals())
    bl, seq, _ = x.shape
    T = bl * seq
    tm = min(256, seq // 2)
    mi = _lin((lax.axis_index("x"), lax.axis_index("y"), lax.axis_index("c")))

    def shard_in(w):
        return w[0].T.astype(BF16)

    g_f1 = _RelayGather([shard_in(w_f1_in), w_f1_out[0].astype(BF16)], ("rows", "out"))
    s_f2 = shard_in(w_f2_in)
    g_mx = _Gather([w_mix_in[0].astype(BF16), w_mix_out[0].astype(BF16), w_f2_out[0].astype(BF16), s_f2[:, 0:D // 4]],
                   ("rows", "rows", "out", "rows"), late_mid=True)
    g_f2 = _Gather([s_f2[:, D // 4:D]], ("rows",))

    c_pad = jnp.pad(c, ((0, 8 - bl), (0, 0)))
    b_cols = lax.dynamic_slice(b_ada, (0, mi * ADA_B), (1, ADA_B))
    cw_pad = jnp.pad(conv_w[0], ((0, 1), (0, 0)))
    (mod1, mod2, mod3, sc_all, cw_all), ((wi1, wo1),) = _ada_fwd(c_pad, w_ada[0], b_cols, cw_pad, bl, jobs=[g_f1])
    cw_full = cw_all.transpose(1, 0, 2).reshape(32, WA)

    zrow = jnp.zeros((1, D), F32)
    gv1 = jnp.concatenate([g_pre_f1, g_post_f1] + [zrow] * 6, axis=0)
    gvm = jnp.concatenate([g_pre_m, g_post_m] + [zrow] * 6, axis=0)
    gv2 = jnp.concatenate([g_pre_f2, g_post_f2] + [zrow] * 6, axis=0)
    v512 = jnp.concatenate([gmlp_norm_g, gmlp_norm_b, conv_b, conv_norm_g, conv_norm_b, g_out_a, g_out_b,
                            jnp.zeros((1, WA), F32)], axis=0)
    ws = w_spatial[0]
    bias_full = jnp.repeat(b_spatial[0].T, HD, axis=1)
    esel = (lax.broadcasted_iota(jnp.int32, (8, WA), 1) // HD == lax.broadcasted_iota(jnp.int32, (8, WA), 0)).astype(F32)

    x0 = x.reshape(T, D)
    (x1, gu1, y1), ((wmi, wmo, wo2, wi2a),) = _ffn_fwd(x0, mod1, gv1, wi1, wo1, tm, "ffn1_fwd", jobs=[g_mx])
    wmo = wmo.reshape(D, D)
    (x2, proj, ym, conv), ((wi2b,),) = _mixer_fwd(x1, mod2, gvm, wmi, wmo, v512, ws, bias_full, cw_full, tm, "mixer_fwd", jobs=[g_f2])

    (dx2, dg2, act2, hb2, dyb2, mg3, vg3, loss_blk), _ = _ffn_last(
        x2, loss_target.reshape(T, D), mod3, gv2, (wi2a, wi2b), wo2, tm, "ffn2_fwd_bwd")
    (g_wi2,), _ = _grad_w_in(dg2, hb2, "ffn2_gw_in")
    (g_wo2,), _ = _grad_w_out(act2, dyb2, "ffn2_gw_out")
    (dpart, dymb, ycat, mg2a, vgma, v5g, gws, gbs), ((p_wo2,),) = _mixer_bwd_a(
        dx2, ym, proj, conv, mod2, gvm, wmo, v512, ws, bias_full, esel, tm, "mixer_bwd_a",
        jobs=[_ChipScatter([g_wo2])])
    (dx1, dproj, hbm, mg2b, vgmb, dcw), ((p_wi2,),) = _mixer_bwd_b(
        dx2, x1, dpart, proj, mod2, gvm, wmi, cw_full, tm, "mixer_bwd_b", jobs=[_ChipScatter([g_wi2])])
    (g_wmi,), _ = _grad_w_mi(hbm, dproj, "mixer_gw_in")
    (g_wmo,), _ = _grad_w_mo(ycat, dymb, "mixer_gw_out")
    p2 = jnp.concatenate([v5g, dcw], axis=0)
    (dx0, dg1, act1, hb1, dyb1, mg1, vg1), _ = _ffn_bwd(dx1, x0, y1, gu1, mod1, gv1, wi1, wo1, tm, "ffn1_bwd")

    ada_rows = [(0, 0), (0, 1), (0, 2), (1, 0), (1, 1), (2, 2), (3, 0), (3, 1), (3, 2)]
    p1 = jnp.concatenate([vg1[0:2], vgmb[0:1], vgma[1:2], vg3[0:2], loss_blk[0:1], zrow], axis=0)
    (dd_all, gb_own), ((a1,),) = _ada_bwd([mg1, mg2b, mg2a, mg3], ada_rows, jobs=[_AllGather([p1])])

    (g_wi1,), ((a2, a3, a4, gb_all), (p_wmi, p_wmo)) = _grad_w_in(
        dg1, hb1, "ffn1_gw_in", jobs=[_Gather([p2, gws, gbs, gb_own], ("rows",) * 4), _ChipScatter([g_wmi, g_wmo])])
    g_bada = gb_all[:, 0, :].reshape(1, 9 * D)

    h_i1, token = _chip_scatter_start([g_wi1], "tail_start")
    (g_wo1,), _ = _grad_w_out(act1, dyb1, "ffn1_gw_out", after=token)
    h_o1, token = _chip_scatter_start([g_wo1], "tail2_start")

    res = {}
    quad = _adamw_reduce(p_wi2, w_f2_in[0].T, m_w_f2_in[0].T, v_w_f2_in[0].T, FO, "adamw_w_f2_in", after=token)
    res["w_f2_in"] = tuple(t.T[None] for t in quad)
    for nm, part, tr in (("w_f2_out", p_wo2, FO), ("w_mix_in", p_wmi, 256), ("w_mix_out", p_wmo, MO)):
        quad = _adamw_reduce(part, given[nm][0], given["m_" + nm][0], given["v_" + nm][0], tr, "adamw_" + nm, after=quad[1])
        res[nm] = tuple(t[None] for t in quad)
    quad = _adamw_ada(sc_all, dd_all, w_ada[0], m_w_ada[0], v_w_ada[0], 256, "adamw_w_ada", after=quad[1])
    res["w_ada"] = tuple(t[None] for t in quad)

    small = SMALL_D + SMALL_W + ("w_spatial", "b_spatial", "b_ada")
    grads = [(0, r) for r in range(6)] + [(1, r) for r in range(7)] + [(2, None), (3, None), (4, None)]
    wmv = []
    for nm in small:
        for pre in ("", "m_", "v_"):
            wmv.append(given[pre + nm][0] if nm in ("w_spatial", "b_spatial") else given[pre + nm])
    outs = _adamw_small([a1, a2, a3, a4], [g_bada], grads, wmv, (0, 1), "adamw_small", after=quad[1])

    _, (p_wi1,) = _chip_scatter_wait(h_i1, outs[0], "tail_wait")
    quad = _adamw_reduce(p_wi1, w_f1_in[0].T, m_w_f1_in[0].T, v_w_f1_in[0].T, FO, "adamw_w_f1_in")
    res["w_f1_in"] = tuple(t.T[None] for t in quad)
    _, (p_wo1,) = _chip_scatter_wait(h_o1, quad[1], "tail2_wait")
    quad = _adamw_reduce(p_wo1, w_f1_out[0], m_w_f1_out[0], v_w_f1_out[0], FO, "adamw_w_f1_out")
    res["w_f1_out"] = tuple(t[None] for t in quad)
    loss = outs[0][6, 0]
    for t, nm in enumerate(small):
        quad = outs[2 + 4 * t:6 + 4 * t]
        res[nm] = tuple(q[None] for q in quad) if nm in ("w_spatial", "b_spatial") else tuple(quad)
    g_cw = lax.dynamic_slice(outs[1], (8, mi * 64), (32, 64))
    wmv = [jnp.pad(given[pre + "conv_w"][0], ((0, 1), (0, 0)), constant_values=1.0 if pre == "v_" else 0.0)
           for pre in ("", "m_", "v_")]
    quad = _adamw_small([], [g_cw], [(0, None)], wmv, (), "adamw_conv_w")
    res["conv_w"] = tuple(q[0:CONV_K][None] for q in quad)

    order = ["w_ada", "b_ada", "g_pre_f1", "g_post_f1", "w_f1_in", "w_f1_out", "g_pre_m", "g_post_m", "w_mix_in",
             "gmlp_norm_g", "gmlp_norm_b", "w_spatial", "b_spatial", "conv_w", "conv_b", "conv_norm_g", "conv_norm_b",
             "g_out_a", "g_out_b", "w_mix_out", "g_pre_f2", "g_post_f2", "w_f2_in", "w_f2_out"]
    out = [loss, dx0.reshape(bl, seq, D)]
    for k in range(4):
        out += [res[nm][k] for nm in order]
    return tuple(out)
```

```python
import jax
import jax.numpy as jnp
from jax import lax
from jax.experimental import pallas as pl
from jax.experimental.pallas import tpu as pltpu

F32 = jnp.float32
BF16 = jnp.bfloat16

D = 1024
DFF = 2816
NDEV = 8
FB = 2 * DFF // NDEV
NCH = DFF // FB
LANES = 128
SUBL = 8
FO = DFF // NDEV
WA = 512
NSLAB = WA // LANES
NHEAD = 8
HD = 64
CHUNK = 128
CONV_K = 31
HALO = 32
MB = 2 * (WA + WA) // NDEV
MO = D // NDEV
ADA_B = 9 * D // NDEV
EPS = 1e-6
HALF = 0.5

ADAM_LR = 0.001
ADAM_B1 = 0.9
ADAM_B2 = 0.999
ADAM_EPS = 1e-08
ADAM_WD = 0.01
ADAM_STEP = 10

VMEM_LIMIT = 56 * 1024 * 1024
MESH = pl.DeviceIdType.MESH
FLIPS = ((0, 0, 1), (1, 0, 0), (0, 1, 0), (1, 1, 0), (1, 0, 1), (0, 1, 1), (1, 1, 1))
CHIP_FLIPS = ((1, 0, 0), (0, 1, 0), (1, 1, 0))
HBM = pl.BlockSpec(memory_space=pl.ANY)
VM = pl.BlockSpec(memory_space=pltpu.VMEM)


def _dot(a, b):
    return lax.dot_general(a, b, (((1,), (0,)), ((), ())), preferred_element_type=F32)


def _dot_nt(a, b):
    return lax.dot_general(a, b, (((1,), (1,)), ((), ())), preferred_element_type=F32)


def _dot_tn(a, b):
    return lax.dot_general(a, b, (((0,), (0,)), ((), ())), preferred_element_type=F32)


def _rowmean(v):
    return jnp.mean(v, axis=-1, keepdims=True)


def _colsum(v):
    return jnp.sum(v, axis=0, keepdims=True)


def _sigmoid(v):
    return 0.5 * jnp.tanh(0.5 * v) + 0.5


def _const_spec(shape):
    nd = len(shape)
    return pl.BlockSpec(shape, lambda *_: (0,) * nd, pipeline_mode=pl.Buffered(1))


def _me():
    return lax.axis_index("x"), lax.axis_index("y"), lax.axis_index("c")


def _flip(me, f):
    return tuple(1 - v if b else v for v, b in zip(me, f))


def _lin(p):
    return 4 * p[0] + 2 * p[1] + p[2]


def _remote(src, dst, send_sem, recv_sem, dev):
    return pltpu.make_async_remote_copy(src_ref=src, dst_ref=dst, send_sem=send_sem, recv_sem=recv_sem,
                                        device_id=dev, device_id_type=MESH)


def _blk(kind, ref, p):
    if kind == "out":
        return ref.at[2 * p[0] + p[1], pl.ds(p[2] * FO, FO), :]
    return ref.at[_lin(p)]


class _Gather:
    def __init__(self, shards, kinds, late_mid=False):
        self.late_mid = late_mid
        self.kinds = kinds
        self.n = len(shards)
        self.ins = list(shards)
        self.out_shape = [jax.ShapeDtypeStruct((4, FB, D) if k == "out" else (NDEV,) + s.shape, s.dtype)
                          for s, k in zip(shards, kinds)]
        self.sems = [pltpu.SemaphoreType.DMA((7 * self.n,)), pltpu.SemaphoreType.DMA((7 * self.n,)),
                     pltpu.SemaphoreType.DMA((self.n,))]

    def _first(self, ins, outs, sems):
        ssem, rsem, lsem = sems
        me = _me()
        sib = _flip(me, (0, 0, 1))
        cps, loc = [], []
        for a in range(self.n):
            mine = _blk(self.kinds[a], outs[a], me)
            loc.append(pltpu.make_async_copy(ins[a], mine, lsem.at[a]))
            cps.append(_remote(ins[a], mine, ssem.at[7 * a], rsem.at[7 * a], sib))
            for j, f in enumerate(CHIP_FLIPS):
                cps.append(_remote(ins[a], mine, ssem.at[7 * a + 1 + j], rsem.at[7 * a + 1 + j], _flip(me, f)))
        return cps, loc

    def _passed(self, outs, sems):
        ssem, rsem, _ = sems
        me = _me()
        sib = _flip(me, (0, 0, 1))
        cps = []
        for j, f in enumerate(CHIP_FLIPS):
            for a in range(self.n):
                blk = _blk(self.kinds[a], outs[a], _flip(me, f))
                cps.append(_remote(blk, blk, ssem.at[7 * a + 4 + j], rsem.at[7 * a + 4 + j], sib))
        return cps

    def start(self, ins, outs, sems):
        cps, loc = self._first(ins, outs, sems)
        for cp in loc + cps:
            cp.start()

    def mid(self, ins, outs, sems):
        ssem, rsem, _ = sems
        me = _me()
        passed = self._passed(outs, sems)
        t = 0
        for j, f in enumerate(CHIP_FLIPS):
            for a in range(self.n):
                blk = _blk(self.kinds[a], outs[a], _flip(me, f))
                _remote(blk, blk, ssem.at[7 * a + 1 + j], rsem.at[7 * a + 1 + j], _flip(me, f)).wait_recv()
                passed[t].start()
                t += 1

    def end(self, ins, outs, sems):
        ssem, rsem, _ = sems
        me = _me()
        sib = _flip(me, (0, 0, 1))
        for a in range(self.n):
            blk = _blk(self.kinds[a], outs[a], sib)
            _remote(blk, blk, ssem.at[7 * a], rsem.at[7 * a], sib).wait_recv()
            for j, f in enumerate(CHIP_FLIPS):
                blk = _blk(self.kinds[a], outs[a], _flip(_flip(me, f), (0, 0, 1)))
                _remote(blk, blk, ssem.at[7 * a + 4 + j], rsem.at[7 * a + 4 + j], sib).wait_recv()
        cps, loc = self._first(ins, outs, sems)
        for cp in cps + self._passed(outs, sems):
            cp.wait_send()
        for cp in loc:
            cp.wait()


class _RelayGather(_Gather):
    def _peers(self):
        me = _me()
        c = me[2]
        to = (me[0] + (1 - c) - 2 * me[0] * (1 - c), me[1] + c - 2 * me[1] * c, c)
        frm = (me[0] + c - 2 * me[0] * c, me[1] + (1 - c) - 2 * me[1] * (1 - c), c)
        return me, _flip(me, (0, 0, 1)), to, frm, _flip(me, (1, 1, 0))

    def _first(self, ins, outs, sems):
        ssem, rsem, lsem = sems
        me, sib, to, frm, _ = self._peers()
        cps, loc = [], []
        for a in range(self.n):
            mine = _blk(self.kinds[a], outs[a], me)
            loc.append(pltpu.make_async_copy(ins[a], mine, lsem.at[a]))
            for slot, dev in ((0, sib), (1, to), (2, frm)):
                cps.append(_remote(ins[a], mine, ssem.at[7 * a + slot], rsem.at[7 * a + slot], dev))
        return cps, loc

    def _block_copy(self, outs, sems, a, slot, owner, dev):
        ssem, rsem, _ = sems
        blk = _blk(self.kinds[a], outs[a], owner)
        return _remote(blk, blk, ssem.at[7 * a + slot], rsem.at[7 * a + slot], dev)

    def mid(self, ins, outs, sems):
        me, sib, to, frm, _ = self._peers()
        for a in range(self.n):
            self._block_copy(outs, sems, a, 2, frm, frm).wait_recv()
            self._block_copy(outs, sems, a, 3, frm, to).start()
            self._block_copy(outs, sems, a, 5, frm, sib).start()
        for a in range(self.n):
            self._block_copy(outs, sems, a, 1, to, to).wait_recv()
            self._block_copy(outs, sems, a, 4, to, sib).start()

    def end(self, ins, outs, sems):
        me, sib, to, frm, far = self._peers()
        up = (0, 0, 1)
        for a in range(self.n):
            self._block_copy(outs, sems, a, 3, far, to).wait_recv()
            self._block_copy(outs, sems, a, 6, far, sib).start()
        for a in range(self.n):
            for slot, owner in ((0, sib), (4, _flip(frm, up)), (5, _flip(to, up)), (6, _flip(far, up))):
                self._block_copy(outs, sems, a, slot, owner, sib).wait_recv()
        cps, loc = self._first(ins, outs, sems)
        for a in range(self.n):
            cps += [self._block_copy(outs, sems, a, 3, frm, to), self._block_copy(outs, sems, a, 4, to, sib),
                    self._block_copy(outs, sems, a, 5, frm, sib), self._block_copy(outs, sems, a, 6, far, sib)]
        for cp in cps:
            cp.wait_send()
        for cp in loc:
            cp.wait()


class _ChipScatter:
    def __init__(self, grads):
        self.n = len(grads)
        self.ins = list(grads)
        self.out_shape = [jax.ShapeDtypeStruct(g.shape, BF16) for g in grads]
        self.sems = [pltpu.SemaphoreType.DMA((3 * self.n,)), pltpu.SemaphoreType.DMA((3 * self.n,)),
                     pltpu.SemaphoreType.DMA((self.n,))]

    def _copies(self, ins, outs, sems):
        ssem, rsem, lsem = sems
        me = _me()
        mq = 2 * me[0] + me[1]
        loc = [pltpu.make_async_copy(ins[a].at[mq], outs[a].at[mq], lsem.at[a]) for a in range(self.n)]
        cps = []
        for k, f in enumerate(CHIP_FLIPS):
            p = _flip(me, f)
            for a in range(self.n):
                cps.append(_remote(ins[a].at[2 * p[0] + p[1]], outs[a].at[mq], ssem.at[3 * a + k], rsem.at[3 * a + k], p))
        return cps, loc

    def start(self, ins, outs, sems):
        cps, loc = self._copies(ins, outs, sems)
        for cp in loc + cps:
            cp.start()

    mid = None

    def end(self, ins, outs, sems):
        ssem, rsem, _ = sems
        me = _me()
        mq = 2 * me[0] + me[1]
        for k, f in enumerate(CHIP_FLIPS):
            p = _flip(me, f)
            for a in range(self.n):
                _remote(ins[a].at[mq], outs[a].at[2 * p[0] + p[1]], ssem.at[3 * a + k], rsem.at[3 * a + k], p).wait_recv()
        cps, loc = self._copies(ins, outs, sems)
        for cp in cps:
            cp.wait_send()
        for cp in loc:
            cp.wait()


def _call(core, *, name, grid, in_specs, out_specs, out_shape, args, scratch=(), jobs=(), core_starts=False):
    n_in, n_out, n_sc = len(in_specs), len(out_specs), len(scratch)
    steps = 1
    for g in grid:
        steps *= g

    def body(*refs):
        pos = [0]

        def take(k):
            r = refs[pos[0]:pos[0] + k]
            pos[0] += k
            return r

        ins = take(n_in)
        j_ins = [take(len(j.ins)) for j in jobs]
        outs = take(n_out)
        j_outs = [take(len(j.out_shape)) for j in jobs]
        scs = take(n_sc)
        j_sems = [take(len(j.sems)) for j in jobs]
        if len(grid) == 2:
            step = pl.program_id(0) * grid[1] + pl.program_id(1)
        elif len(grid) == 1:
            step = pl.program_id(0)
        else:
            step = 0
        def start_jobs():
            for j, ji, jo, js in zip(jobs, j_ins, j_outs, j_sems):
                j.start(ji, jo, js)

        if grid:
            pl.when(step == 0)(start_jobs)
        elif not core_starts:
            start_jobs()
        for j, ji, jo, js in zip(jobs, j_ins, j_outs, j_sems):
            if j.mid is not None and grid:
                at = max(steps - 2, 0) if j.late_mid else (3 * steps) // 4
                pl.when(step == at)(lambda j=j, ji=ji, jo=jo, js=js: j.mid(ji, jo, js))
        def finish_jobs():
            for j, ji, jo, js in zip(jobs, j_ins, j_outs, j_sems):
                if j.mid is not None:
                    j.mid(ji, jo, js)
                j.end(ji, jo, js)

        if core_starts:
            core(ins, outs, scs, start_jobs, finish_jobs)
        elif core is not None:
            core(ins, outs, scs)
        if grid:
            for j, ji, jo, js in zip(jobs, j_ins, j_outs, j_sems):
                pl.when(step == steps - 1)(lambda j=j, ji=ji, jo=jo, js=js: j.end(ji, jo, js))
        elif not core_starts:
            finish_jobs()

    all_in = list(in_specs)
    all_args = list(args)
    all_out = list(out_specs)
    all_shape = list(out_shape)
    all_sc = list(scratch)
    for j in jobs:
        all_in += [HBM] * len(j.ins)
        all_args += j.ins
    for j in jobs:
        all_out += [HBM] * len(j.out_shape)
        all_shape += j.out_shape
        all_sc += j.sems
    params = dict(vmem_limit_bytes=VMEM_LIMIT)
    if grid:
        params["dimension_semantics"] = ("arbitrary",) * len(grid)
    res = pl.pallas_call(
        body, name=name, grid=grid, in_specs=all_in, out_specs=all_out, out_shape=all_shape,
        scratch_shapes=all_sc, compiler_params=pltpu.CompilerParams(**params),
    )(*all_args)
    core_res = list(res[:n_out])
    job_res = []
    pos = n_out
    for j in jobs:
        job_res.append(list(res[pos:pos + len(j.out_shape)]))
        pos += len(j.out_shape)
    return core_res, job_res


def _ffn_fwd(x, mod, gvec, w_in, w_out, tm, name, jobs=()):
    T = x.shape[0]
    nt = T // tm
    tps = nt // mod.shape[0]

    def core(ins, outs, _):
        x_ref, mod_ref, g_ref, win_ref, wout_ref = ins
        xo_ref, gu_ref, y_ref = outs
        xv = x_ref[...]
        sh, sc, gt = mod_ref[0:1, :], mod_ref[1:2, :], mod_ref[2:3, :]
        r = lax.rsqrt(_rowmean(xv * xv) + EPS)
        h = (xv * r * g_ref[0:1, :]) * (1.0 + sc) + sh
        hb = h.astype(BF16)
        y = jnp.zeros((tm, D), F32)
        for cidx in range(NCH):
            gate = _dot_nt(hb, win_ref[cidx])
            up = _dot_nt(hb, win_ref[NCH + cidx])
            gu_ref[cidx] = gate.astype(BF16)
            gu_ref[NCH + cidx] = up.astype(BF16)
            act = gate * _sigmoid(gate) * up
            y = y + _dot(act.astype(BF16), wout_ref[cidx])
        y_ref[...] = y
        ry = lax.rsqrt(_rowmean(y * y) + EPS)
        xo_ref[...] = xv + (HALF * gt) * (y * ry * g_ref[1:2, :])

    tile = pl.BlockSpec((tm, D), lambda i: (i, 0))
    return _call(
        core, name=name, grid=(nt,), jobs=jobs,
        in_specs=[tile, pl.BlockSpec((None, 8, D), lambda i: (i // tps, 0, 0)), _const_spec((8, D)),
                  _const_spec((8, FB, D)), _const_spec((4, FB, D))],
        out_specs=[tile, pl.BlockSpec((8, tm, FB), lambda i: (0, i, 0)), tile],
        out_shape=[jax.ShapeDtypeStruct((T, D), F32), jax.ShapeDtypeStruct((8, T, FB), BF16),
                   jax.ShapeDtypeStruct((T, D), F32)],
        args=[x, mod, gvec, w_in, w_out])


def _ffn_bwd(dxo, x, y, gu, mod, gvec, w_in, w_out, tm, name, jobs=()):
    T = x.shape[0]
    nt = T // tm
    nb = mod.shape[0]
    tps = nt // nb

    def core(ins, outs, _):
        dxo_ref, x_ref, y_ref, gu_ref, mod_ref, g_ref, win_ref, wout_ref = ins
        dx_ref, dg_ref, act_ref, hb_ref, dyb_ref, mg_ref, vg_ref = outs
        i = pl.program_id(0)
        xv = x_ref[...]
        dxo_v = dxo_ref[...]
        yv = y_ref[...]
        sh, sc, gt = mod_ref[0:1, :], mod_ref[1:2, :], mod_ref[2:3, :]
        gpre, gpost = g_ref[0:1, :], g_ref[1:2, :]
        r = lax.rsqrt(_rowmean(xv * xv) + EPS)
        xh = xv * r
        n = xh * gpre
        hb = (n * (1.0 + sc) + sh).astype(BF16)
        hb_ref[...] = hb
        ry = lax.rsqrt(_rowmean(yv * yv) + EPS)
        yh = yv * ry
        d_gt = _colsum(HALF * dxo_v * (yh * gpost))
        dp = (HALF * gt) * dxo_v
        d_gpost = _colsum(dp * yh)
        dyh = dp * gpost
        dy = ry * (dyh - yh * _rowmean(dyh * yh))
        dyb = dy.astype(BF16)
        dyb_ref[...] = dyb
        dh = jnp.zeros((tm, D), F32)
        for cidx in range(NCH):
            gate = gu_ref[cidx].astype(F32)
            up = gu_ref[NCH + cidx].astype(F32)
            sig = _sigmoid(gate)
            s = gate * sig
            act_ref[cidx] = (s * up).astype(BF16)
            d_act = _dot_nt(dyb, wout_ref[cidx])
            d_up = (d_act * s).astype(BF16)
            d_gate = (d_act * up * (sig * (1.0 + gate * (1.0 - sig)))).astype(BF16)
            dg_ref[cidx] = d_gate
            dg_ref[NCH + cidx] = d_up
            dh = dh + _dot(d_gate, win_ref[cidx]) + _dot(d_up, win_ref[NCH + cidx])
        d_sc = _colsum(dh * n)
        d_sh = _colsum(dh)
        dn = dh * (1.0 + sc)
        d_gpre = _colsum(dn * xh)
        dxh = dn * gpre
        dx_ref[...] = dxo_v + r * (dxh - xh * _rowmean(dxh * xh))

        @pl.when(i % tps == 0)
        def _():
            mg_ref[...] = jnp.zeros((8, D), F32)

        @pl.when(i == 0)
        def _():
            vg_ref[...] = jnp.zeros((8, D), F32)

        mg_ref[0:1, :] += d_sh
        mg_ref[1:2, :] += d_sc
        mg_ref[2:3, :] += d_gt
        vg_ref[0:1, :] += d_gpre
        vg_ref[1:2, :] += d_gpost

    tile = pl.BlockSpec((tm, D), lambda i: (i, 0))
    return _call(
        core, name=name, grid=(nt,), jobs=jobs,
        in_specs=[tile, tile, tile, pl.BlockSpec((8, tm, FB), lambda i: (0, i, 0)),
                  pl.BlockSpec((None, 8, D), lambda i: (i // tps, 0, 0)), _const_spec((8, D)),
                  _const_spec((8, FB, D)), _const_spec((4, FB, D))],
        out_specs=[tile, pl.BlockSpec((8, tm, FB), lambda i: (0, i, 0)),
                   pl.BlockSpec((4, tm, FB), lambda i: (0, i, 0)), tile, tile,
                   pl.BlockSpec((None, 8, D), lambda i: (i // tps, 0, 0)), pl.BlockSpec((8, D), lambda i: (0, 0))],
        out_shape=[jax.ShapeDtypeStruct((T, D), F32), jax.ShapeDtypeStruct((8, T, FB), BF16),
                   jax.ShapeDtypeStruct((4, T, FB), BF16), jax.ShapeDtypeStruct((T, D), BF16),
                   jax.ShapeDtypeStruct((T, D), BF16), jax.ShapeDtypeStruct((nb, 8, D), F32),
                   jax.ShapeDtypeStruct((8, D), F32)],
        args=[dxo, x, y, gu, mod, gvec, w_in, w_out])


def _ffn_last(x, target, mod, gvec, w_in, w_out, tm, name, jobs=()):
    T = x.shape[0]
    nt = T // tm
    nb = mod.shape[0]
    tps = nt // nb

    def core(ins, outs, scs):
        x_ref, t_ref, mod_ref, g_ref, wina_ref, winb_ref, wout_ref = ins
        dx_ref, dg_ref, act_ref, hb_ref, dyb_ref, mg_ref, vg_ref, loss_ref = outs
        hd2 = w_in[0].shape[2]
        (gu_s,) = scs
        i = pl.program_id(0)
        xv = x_ref[...]
        sh, sc, gt = mod_ref[0:1, :], mod_ref[1:2, :], mod_ref[2:3, :]
        gpre, gpost = g_ref[0:1, :], g_ref[1:2, :]
        r = lax.rsqrt(_rowmean(xv * xv) + EPS)
        xh = xv * r
        n = xh * gpre
        hb = (n * (1.0 + sc) + sh).astype(BF16)
        hb_ref[...] = hb
        hba, hbb = hb[:, 0:hd2], hb[:, hd2:D]
        yv = jnp.zeros((tm, D), F32)
        for cidx in range(NCH):
            gate = _dot_nt(hba, wina_ref[cidx]) + _dot_nt(hbb, winb_ref[cidx])
            up = _dot_nt(hba, wina_ref[NCH + cidx]) + _dot_nt(hbb, winb_ref[NCH + cidx])
            gu_s[cidx] = gate.astype(BF16)
            gu_s[NCH + cidx] = up.astype(BF16)
            act = gate * _sigmoid(gate) * up
            act_ref[cidx] = act.astype(BF16)
            yv = yv + _dot(act_ref[cidx], wout_ref[cidx])
        ry = lax.rsqrt(_rowmean(yv * yv) + EPS)
        yh = yv * ry
        pn = yh * gpost
        err = xv + (HALF * gt) * pn - t_ref[...]
        dxo_v = err * (1.0 / D)
        d_gt = _colsum(HALF * dxo_v * pn)
        dp = (HALF * gt) * dxo_v
        d_gpost = _colsum(dp * yh)
        dyh = dp * gpost
        dyb = (ry * (dyh - yh * _rowmean(dyh * yh))).astype(BF16)
        dyb_ref[...] = dyb
        dha = jnp.zeros((tm, hd2), F32)
        dhb = jnp.zeros((tm, D - hd2), F32)
        for cidx in range(NCH):
            gate = gu_s[cidx].astype(F32)
            up = gu_s[NCH + cidx].astype(F32)
            sig = _sigmoid(gate)
            s = gate * sig
            d_act = _dot_nt(dyb, wout_ref[cidx])
            d_up = (d_act * s).astype(BF16)
            d_gate = (d_act * up * (sig * (1.0 + gate * (1.0 - sig)))).astype(BF16)
            dg_ref[cidx] = d_gate
            dg_ref[NCH + cidx] = d_up
            dha = dha + _dot(d_gate, wina_ref[cidx]) + _dot(d_up, wina_ref[NCH + cidx])
            dhb = dhb + _dot(d_gate, winb_ref[cidx]) + _dot(d_up, winb_ref[NCH + cidx])
        dh = jnp.concatenate([dha, dhb], axis=1)
        d_sc = _colsum(dh * n)
        d_sh = _colsum(dh)
        dn = dh * (1.0 + sc)
        d_gpre = _colsum(dn * xh)
        dxh = dn * gpre
        dx_ref[...] = dxo_v + r * (dxh - xh * _rowmean(dxh * xh))

        @pl.when(i % tps == 0)
        def _():
            mg_ref[...] = jnp.zeros((8, D), F32)

        @pl.when(i == 0)
        def _():
            vg_ref[...] = jnp.zeros((8, D), F32)
            loss_ref[...] = jnp.zeros((8, D), F32)

        mg_ref[0:1, :] += d_sh
        mg_ref[1:2, :] += d_sc
        mg_ref[2:3, :] += d_gt
        vg_ref[0:1, :] += d_gpre
        vg_ref[1:2, :] += d_gpost
        loss_ref[...] += HALF * jnp.sum(_rowmean(err * err), axis=0, keepdims=True)

    tile = pl.BlockSpec((tm, D), lambda i: (i, 0))
    return _call(
        core, name=name, grid=(nt,), jobs=jobs,
        in_specs=[tile, tile, pl.BlockSpec((None, 8, D), lambda i: (i // tps, 0, 0)), _const_spec((8, D)),
                  _const_spec(w_in[0].shape), _const_spec(w_in[1].shape), _const_spec((4, FB, D))],
        out_specs=[tile, pl.BlockSpec((8, tm, FB), lambda i: (0, i, 0)),
                   pl.BlockSpec((4, tm, FB), lambda i: (0, i, 0)), tile, tile,
                   pl.BlockSpec((None, 8, D), lambda i: (i // tps, 0, 0)), pl.BlockSpec((8, D), lambda i: (0, 0)),
                   pl.BlockSpec((8, D), lambda i: (0, 0))],
        out_shape=[jax.ShapeDtypeStruct((T, D), F32), jax.ShapeDtypeStruct((8, T, FB), BF16),
                   jax.ShapeDtypeStruct((4, T, FB), BF16), jax.ShapeDtypeStruct((T, D), BF16),
                   jax.ShapeDtypeStruct((T, D), BF16), jax.ShapeDtypeStruct((nb, 8, D), F32),
                   jax.ShapeDtypeStruct((8, D), F32), jax.ShapeDtypeStruct((8, D), F32)],
        scratch=[pltpu.VMEM((8, tm, FB), BF16)],
        args=[x, target, mod, gvec, w_in[0], w_in[1], w_out])


def _masked_spatial(ws_ref):
    row = lax.broadcasted_iota(jnp.int32, (CHUNK, CHUNK), 0)
    col = lax.broadcasted_iota(jnp.int32, (CHUNK, CHUNK), 1)
    keep = col <= row
    return [jnp.where(keep, ws_ref[hd], 0.0).astype(BF16) for hd in range(NHEAD)]


def _head_pairs(mats, right, transpose=False):
    first = lax.broadcasted_iota(jnp.int32, (CHUNK, LANES), 1) < HD
    op = _dot_tn if transpose else _dot
    out = []
    for p in range(NHEAD // 2):
        slab = right[:, _lanes(p)]
        out.append(jnp.where(first, op(mats[2 * p], slab), op(mats[2 * p + 1], slab)))
    return jnp.concatenate(out, axis=1)


def _spatial_gate(wm, vb_chunk):
    return _head_pairs(wm, vb_chunk)


def _layer_norm_stats(v):
    mu = _rowmean(v)
    vc = v - mu
    rstd = lax.rsqrt(_rowmean(vc * vc) + EPS)
    return vc * rstd, rstd


def _pitch(tm):
    p = tm // 8
    while p % 8 != 4:
        p += 1
    return p


def _lanes(s):
    return slice(s * LANES, (s + 1) * LANES)


def _to_slabs(ref, row0, val):
    for s in range(NSLAB):
        ref[s, row0:row0 + val.shape[0], :] = val[:, _lanes(s)]


def _tap_sum(src, out, cw_ref, bias, tm, start):
    p = _pitch(tm)
    for s in range(NSLAB):
        accs = [jnp.broadcast_to(bias[:, _lanes(s)], (SUBL, LANES))] * p
        for k in range(CONV_K):
            w = jnp.broadcast_to(cw_ref[k:k + 1, _lanes(s)], (SUBL, LANES))
            for v in range(p):
                accs[v] = accs[v] + w * src[s, pl.ds(v + start(k), 8, stride=p), :]
        for v in range(p):
            out[s, pl.ds(v, 8, stride=p), :] = accs[v]
    return jnp.concatenate([out[s, 0:tm, :] for s in range(NSLAB)], axis=1)


def _mixer_fwd(x, mod, gvec, w_mi, w_mo, v512, ws, bias_full, cw, tm, name, jobs=()):
    T = x.shape[0]
    nt = T // tm
    tps = nt // mod.shape[0]
    ext_rows = 8 * _pitch(tm)

    def core(ins, outs, scs):
        x_ref, mod_ref, g_ref, wmi_ref, wmo_ref, v_ref, ws_ref, bias_ref, cw_ref = ins
        xo_ref, proj_ref, ym_ref, conv_ref = outs
        glu_ext, conv_scr = scs
        i = pl.program_id(0)
        xv = x_ref[...]
        sh, sc, gt = mod_ref[0:1, :], mod_ref[1:2, :], mod_ref[2:3, :]
        r = lax.rsqrt(_rowmean(xv * xv) + EPS)
        hb = ((xv * r * g_ref[0:1, :]) * (1.0 + sc) + sh).astype(BF16)
        for j in range(NDEV):
            proj_ref[:, j * MB:(j + 1) * MB] = _dot(hb, wmi_ref[j])
        u = proj_ref[:, 0:WA]
        v0 = proj_ref[:, WA:2 * WA]
        a = proj_ref[:, 2 * WA:3 * WA]
        g = proj_ref[:, 3 * WA:4 * WA]
        vh, _ = _layer_norm_stats(v0)
        vb = (vh * v_ref[0:1, :] + v_ref[1:2, :]).astype(BF16)
        wm = _masked_spatial(ws_ref)
        ya = []
        for q in range(tm // CHUNK):
            z = _spatial_gate(wm, vb[q * CHUNK:(q + 1) * CHUNK, :]) + bias_ref[...]
            ya.append(u[q * CHUNK:(q + 1) * CHUNK, :] * z)
        ya = jnp.concatenate(ya, axis=0)
        glu = a * _sigmoid(g)

        @pl.when(i == 0)
        def _():
            glu_ext[:, HALO + tm:HALO + ext_rows, :] = jnp.zeros((NSLAB, ext_rows - tm, LANES), F32)

        @pl.when(i % tps == 0)
        def _():
            glu_ext[:, 0:HALO, :] = jnp.zeros((NSLAB, HALO, LANES), F32)

        _to_slabs(glu_ext, HALO, glu)
        conv = _tap_sum(glu_ext, conv_scr, cw_ref, v_ref[2:3, :], tm, lambda k: HALO - (CONV_K - 1) + k)
        conv_ref[...] = conv
        glu_ext[:, 0:HALO, :] = glu_ext[:, tm:tm + HALO, :]
        ch, _ = _layer_norm_stats(conv)
        cn = ch * v_ref[3:4, :] + v_ref[4:5, :]
        yb = cn * _sigmoid(cn)
        pa = ya * lax.rsqrt(_rowmean(ya * ya) + EPS) * v_ref[5:6, :]
        pb = yb * lax.rsqrt(_rowmean(yb * yb) + EPS) * v_ref[6:7, :]
        ycat = jnp.concatenate([pa, pb], axis=1).astype(BF16)
        ym = _dot(ycat, wmo_ref[...])
        ym_ref[...] = ym
        rm = lax.rsqrt(_rowmean(ym * ym) + EPS)
        xo_ref[...] = xv + gt * (ym * rm * g_ref[1:2, :])

    tile = pl.BlockSpec((tm, D), lambda i: (i, 0))
    return _call(
        core, name=name, grid=(nt,), jobs=jobs,
        in_specs=[tile, pl.BlockSpec((None, 8, D), lambda i: (i // tps, 0, 0)), _const_spec((8, D)),
                  _const_spec((NDEV, D, MB)), _const_spec((D, D)), _const_spec((8, WA)),
                  _const_spec((NHEAD, CHUNK, CHUNK)), _const_spec((CHUNK, WA)), _const_spec((32, WA))],
        out_specs=[tile, pl.BlockSpec((tm, 4 * WA), lambda i: (i, 0)), tile, pl.BlockSpec((tm, WA), lambda i: (i, 0))],
        out_shape=[jax.ShapeDtypeStruct((T, D), F32), jax.ShapeDtypeStruct((T, 4 * WA), F32),
                   jax.ShapeDtypeStruct((T, D), F32), jax.ShapeDtypeStruct((T, WA), F32)],
        scratch=[pltpu.VMEM((NSLAB, HALO + ext_rows, LANES), F32), pltpu.VMEM((NSLAB, ext_rows, LANES), F32)],
        args=[x, mod, gvec, w_mi, w_mo, v512, ws, bias_full, cw])


def _mixer_bwd_a(dxo, ym, proj, conv, mod, gvec, w_mo, v512, ws, bias_full, esel, tm, name, jobs=()):
    T = dxo.shape[0]
    nt = T // tm
    nb = mod.shape[0]
    tps = nt // nb

    def core(ins, outs, scs):
        dxo_ref, ym_ref, proj_ref, conv_ref, mod_ref, g_ref, wmo_ref, v_ref, ws_ref, bias_ref, e_ref = ins
        dpart_ref, dymb_ref, ycat_ref, mg_ref, vg_ref, v5g_ref, gws_ref, gbs_ref = outs
        (dbs_acc,) = scs
        i = pl.program_id(0)
        dxo_v = dxo_ref[...]
        ymv = ym_ref[...]
        gt = mod_ref[2:3, :]
        gpost = g_ref[1:2, :]
        rm = lax.rsqrt(_rowmean(ymv * ymv) + EPS)
        ymh = ymv * rm
        d_gt = _colsum(dxo_v * (ymh * gpost))
        dpm = gt * dxo_v
        d_gpost = _colsum(dpm * ymh)
        dymh = dpm * gpost
        dym = (rm * (dymh - ymh * _rowmean(dymh * ymh))).astype(BF16)
        dymb_ref[...] = dym
        dycat = _dot_nt(dym, wmo_ref[...])
        u = proj_ref[:, 0:WA]
        v0 = proj_ref[:, WA:2 * WA]
        vh, rv = _layer_norm_stats(v0)
        vb = (vh * v_ref[0:1, :] + v_ref[1:2, :]).astype(BF16)
        wm = _masked_spatial(ws_ref)
        zs = []
        for q in range(tm // CHUNK):
            zs.append(_spatial_gate(wm, vb[q * CHUNK:(q + 1) * CHUNK, :]) + bias_ref[...])
        z = jnp.concatenate(zs, axis=0)
        ya = u * z
        ra = lax.rsqrt(_rowmean(ya * ya) + EPS)
        yah = ya * ra
        ch, rc = _layer_norm_stats(conv_ref[...])
        cn = ch * v_ref[3:4, :] + v_ref[4:5, :]
        sg = _sigmoid(cn)
        yb = cn * sg
        rb = lax.rsqrt(_rowmean(yb * yb) + EPS)
        ybh = yb * rb
        ycat_ref[...] = jnp.concatenate([yah * v_ref[5:6, :], ybh * v_ref[6:7, :]], axis=1).astype(BF16)
        dpa = dycat[:, 0:WA]
        dpb = dycat[:, WA:2 * WA]
        d_goa = _colsum(dpa * yah)
        d_gob = _colsum(dpb * ybh)
        dyah = dpa * v_ref[5:6, :]
        dybh = dpb * v_ref[6:7, :]
        dya = ra * (dyah - yah * _rowmean(dyah * yah))
        dyb = rb * (dybh - ybh * _rowmean(dybh * ybh))
        dpart_ref[:, 0:WA] = dya * z
        dz = dya * u

        @pl.when(i == 0)
        def _():
            gws_ref[...] = jnp.zeros((NHEAD, CHUNK, CHUNK), F32)
            dbs_acc[...] = jnp.zeros((CHUNK, WA), F32)
            vg_ref[...] = jnp.zeros((8, D), F32)
            v5g_ref[...] = jnp.zeros((8, WA), F32)

        first = lax.broadcasted_iota(jnp.int32, (CHUNK, LANES), 1) < HD
        dvs = []
        for q in range(tm // CHUNK):
            dz_q = dz[q * CHUNK:(q + 1) * CHUNK, :]
            vb_q = vb[q * CHUNK:(q + 1) * CHUNK, :]
            dbs_acc[...] += dz_q
            dzb = dz_q.astype(BF16)
            dvs.append(_head_pairs(wm, dzb, transpose=True))
            for hd in range(NHEAD):
                slab = dzb[:, _lanes(hd // 2)]
                dz_hd = jnp.where(first if hd % 2 == 0 else jnp.logical_not(first), slab, jnp.zeros_like(slab))
                gws_ref[hd] += _dot_nt(dz_hd, vb_q[:, _lanes(hd // 2)])
        dv = jnp.concatenate(dvs, axis=0)
        d_gng = _colsum(dv * vh)
        d_gnb = _colsum(dv)
        dvh = dv * v_ref[0:1, :]
        dpart_ref[:, WA:2 * WA] = rv * (dvh - _rowmean(dvh) - vh * _rowmean(dvh * vh))
        dcn = dyb * (sg * (1.0 + cn * (1.0 - sg)))
        d_cng = _colsum(dcn * ch)
        d_cnb = _colsum(dcn)
        dch = dcn * v_ref[3:4, :]
        dconv = rc * (dch - _rowmean(dch) - ch * _rowmean(dch * ch))
        dpart_ref[:, 2 * WA:3 * WA] = dconv
        dpart_ref[:, 3 * WA:4 * WA] = jnp.zeros((tm, WA), F32)
        d_cb = _colsum(dconv)

        @pl.when(i % tps == 0)
        def _():
            mg_ref[...] = jnp.zeros((8, D), F32)

        mg_ref[2:3, :] += d_gt
        vg_ref[1:2, :] += d_gpost
        v5g_ref[0:1, :] += d_gng
        v5g_ref[1:2, :] += d_gnb
        v5g_ref[2:3, :] += d_cb
        v5g_ref[3:4, :] += d_cng
        v5g_ref[4:5, :] += d_cnb
        v5g_ref[5:6, :] += d_goa
        v5g_ref[6:7, :] += d_gob

        @pl.when(i == nt - 1)
        def _():
            row = lax.broadcasted_iota(jnp.int32, (CHUNK, CHUNK), 0)
            col = lax.broadcasted_iota(jnp.int32, (CHUNK, CHUNK), 1)
            for hd in range(NHEAD):
                gws_ref[hd] = jnp.where(col <= row, gws_ref[hd], 0.0)
            gbs_ref[...] = lax.dot_general(e_ref[...], dbs_acc[...], (((1,), (1,)), ((), ())),
                                           precision=lax.Precision.HIGHEST, preferred_element_type=F32)

    tile = pl.BlockSpec((tm, D), lambda i: (i, 0))
    ptile = pl.BlockSpec((tm, 4 * WA), lambda i: (i, 0))
    return _call(
        core, name=name, grid=(nt,), jobs=jobs,
        in_specs=[tile, tile, pl.BlockSpec((tm, 2 * WA), lambda i: (i, 0)), pl.BlockSpec((tm, WA), lambda i: (i, 0)),
                  pl.BlockSpec((None, 8, D), lambda i: (i // tps, 0, 0)), _const_spec((8, D)), _const_spec((D, D)),
                  _const_spec((8, WA)), _const_spec((NHEAD, CHUNK, CHUNK)), _const_spec((CHUNK, WA)),
                  _const_spec((8, WA))],
        out_specs=[ptile, tile, tile, pl.BlockSpec((None, 8, D), lambda i: (i // tps, 0, 0)),
                   pl.BlockSpec((8, D), lambda i: (0, 0)), pl.BlockSpec((8, WA), lambda i: (0, 0)),
                   pl.BlockSpec((NHEAD, CHUNK, CHUNK), lambda i: (0, 0, 0)), pl.BlockSpec((8, CHUNK), lambda i: (0, 0))],
        out_shape=[jax.ShapeDtypeStruct((T, 4 * WA), F32), jax.ShapeDtypeStruct((T, D), BF16),
                   jax.ShapeDtypeStruct((T, D), BF16), jax.ShapeDtypeStruct((nb, 8, D), F32),
                   jax.ShapeDtypeStruct((8, D), F32), jax.ShapeDtypeStruct((8, WA), F32),
                   jax.ShapeDtypeStruct((NHEAD, CHUNK, CHUNK), F32), jax.ShapeDtypeStruct((8, CHUNK), F32)],
        scratch=[pltpu.VMEM((CHUNK, WA), F32)],
        args=[dxo, ym, proj, conv, mod, gvec, w_mo, v512, ws, bias_full, esel])


def _mixer_bwd_b(dxo, x, dpart, proj, mod, gvec, w_mi, cw, tm, name, jobs=()):
    T = x.shape[0]
    nt = T // tm
    nb = mod.shape[0]
    tps = nt // nb
    hpt = tm // HALO
    nh = T // HALO
    off = HALO - (CONV_K - 1)
    p = _pitch(tm)
    ext_rows = 8 * p

    def core(ins, outs, scs):
        dxo_ref, x_ref, dpart_ref, dnext_ref, ag_ref, halo_ref, mod_ref, g_ref, wmi_ref, cw_ref = ins
        dx_ref, dproj_ref, hb_ref, mg_ref, vg_ref, dcw_ref = outs
        glu_ext, dconv_ext, dglu_scr, dcw_acc = scs
        i = pl.program_id(0)
        first = i % tps == 0
        last = i % tps == tps - 1
        a = ag_ref[:, 0:WA]
        g = ag_ref[:, WA:2 * WA]
        sgg = _sigmoid(g)

        @pl.when(i == 0)
        def _():
            glu_ext[:, HALO + tm:HALO + ext_rows, :] = jnp.zeros((NSLAB, ext_rows - tm, LANES), F32)
            dconv_ext[:, HALO + tm:HALO + ext_rows, :] = jnp.zeros((NSLAB, ext_rows - tm, LANES), F32)
            dcw_acc[...] = jnp.zeros((32, 8, WA), F32)
            vg_ref[...] = jnp.zeros((8, D), F32)

        _to_slabs(glu_ext, 0, jnp.where(first, 0.0, halo_ref[:, 0:WA] * _sigmoid(halo_ref[:, WA:2 * WA])))
        _to_slabs(glu_ext, HALO, a * sgg)
        _to_slabs(dconv_ext, 0, dpart_ref[:, 2 * WA:3 * WA])
        _to_slabs(dconv_ext, tm, jnp.where(last, 0.0, dnext_ref[...]))
        sub = lax.broadcasted_iota(jnp.int32, (SUBL, LANES), 0)
        for s in range(NSLAB):
            accs = [jnp.zeros((SUBL, LANES), F32)] * CONV_K
            for v in range(p):
                dc = jnp.where(v + p * sub < tm, dconv_ext[s, pl.ds(v, 8, stride=p), :], 0.0)
                for k in range(CONV_K):
                    accs[k] = accs[k] + dc * glu_ext[s, pl.ds(v + off + k, 8, stride=p), :]
            for k in range(CONV_K):
                dcw_acc[k, :, _lanes(s)] += accs[k]
        dglu = _tap_sum(dconv_ext, dglu_scr, cw_ref, jnp.zeros((1, WA), F32), tm, lambda k: (CONV_K - 1) - k)

        @pl.when(i == nt - 1)
        def _():
            for k in range(CONV_K):
                dcw_ref[k:k + 1, :] = jnp.sum(dcw_acc[k], axis=0, keepdims=True)
            dcw_ref[CONV_K:32, :] = jnp.zeros((32 - CONV_K, WA), F32)

        da = dglu * sgg
        dgg = dglu * a * (sgg * (1.0 - sgg))
        dproj_ref[:, 0:2 * WA] = dpart_ref[:, 0:2 * WA].astype(BF16)
        dproj_ref[:, 2 * WA:3 * WA] = da.astype(BF16)
        dproj_ref[:, 3 * WA:4 * WA] = dgg.astype(BF16)
        dh = jnp.zeros((tm, D), F32)
        for j in range(NDEV):
            dh = dh + _dot_nt(dproj_ref[:, j * MB:(j + 1) * MB], wmi_ref[j])
        xv = x_ref[...]
        sc, sh = mod_ref[1:2, :], mod_ref[0:1, :]
        gpre = g_ref[0:1, :]
        r = lax.rsqrt(_rowmean(xv * xv) + EPS)
        xh = xv * r
        n = xh * gpre
        hb_ref[...] = (n * (1.0 + sc) + sh).astype(BF16)
        d_sc = _colsum(dh * n)
        d_sh = _colsum(dh)
        dn = dh * (1.0 + sc)
        d_gpre = _colsum(dn * xh)
        dxh = dn * gpre
        dx_ref[...] = dxo_ref[...] + r * (dxh - xh * _rowmean(dxh * xh))

        @pl.when(first)
        def _():
            mg_ref[...] = jnp.zeros((8, D), F32)

        mg_ref[0:1, :] += d_sh
        mg_ref[1:2, :] += d_sc
        vg_ref[0:1, :] += d_gpre

    tile = pl.BlockSpec((tm, D), lambda i: (i, 0))
    return _call(
        core, name=name, grid=(nt,), jobs=jobs,
        in_specs=[tile, tile, pl.BlockSpec((tm, 4 * WA), lambda i: (i, 0)),
                  pl.BlockSpec((HALO, WA), lambda i: (jnp.minimum((i + 1) * hpt, nh - 1), 2)),
                  pl.BlockSpec((tm, 2 * WA), lambda i: (i, 1)),
                  pl.BlockSpec((HALO, 2 * WA), lambda i: (jnp.maximum(i * hpt - 1, 0), 1)),
                  pl.BlockSpec((None, 8, D), lambda i: (i // tps, 0, 0)), _const_spec((8, D)),
                  _const_spec((NDEV, D, MB)), _const_spec((32, WA))],
        out_specs=[tile, pl.BlockSpec((tm, 4 * WA), lambda i: (i, 0)), tile,
                   pl.BlockSpec((None, 8, D), lambda i: (i // tps, 0, 0)), pl.BlockSpec((8, D), lambda i: (0, 0)),
                   pl.BlockSpec((32, WA), lambda i: (0, 0))],
        out_shape=[jax.ShapeDtypeStruct((T, D), F32), jax.ShapeDtypeStruct((T, 4 * WA), BF16),
                   jax.ShapeDtypeStruct((T, D), BF16), jax.ShapeDtypeStruct((nb, 8, D), F32),
                   jax.ShapeDtypeStruct((8, D), F32), jax.ShapeDtypeStruct((32, WA), F32)],
        scratch=[pltpu.VMEM((NSLAB, HALO + ext_rows, LANES), F32), pltpu.VMEM((NSLAB, HALO + ext_rows, LANES), F32),
                 pltpu.VMEM((NSLAB, ext_rows, LANES), F32), pltpu.VMEM((32, 8, WA), F32)],
        args=[dxo, x, dpart, dpart, proj, proj, mod, gvec, w_mi, cw])


def _grad_chip(a, b, a_spec, b_spec, prod_shape, half, name, jobs=(), via_b=False, after=None):
    steps = 8 if half is None else 4
    R = prod_shape[0] if half is None else half
    C = prod_shape[1]

    def core(ins, outs, scs):
        a_ref, b_ref = ins[:2]
        (o_ref,) = outs
        own, snd, rcv, ssem, rsem, lsem = scs
        s = pl.program_id(0)
        c = lax.axis_index("c")
        me = _me()
        sib = _flip(me, (0, 0, 1))
        if via_b:
            prod = _dot_tn(b_ref[...], a_ref[...]).T.astype(BF16)
        else:
            prod = _dot_tn(a_ref[...], b_ref[...]).astype(BF16)
        if half is None:
            q = s // 2

            @pl.when(s % 2 == c)
            def _():
                own[q] = prod

            @pl.when(s % 2 != c)
            def _():
                snd[q] = prod
                _remote(snd.at[q], rcv.at[q], ssem.at[q], rsem.at[q], sib).start()
        else:
            lo = prod[0:half, :]
            hi = prod[half:2 * half, :]
            own[s] = jnp.where(c == 0, lo, hi)
            snd[s] = jnp.where(c == 0, hi, lo)
            _remote(snd.at[s], rcv.at[s], ssem.at[s], rsem.at[s], sib).start()

        @pl.when(s == steps - 1)
        def _():
            for q4 in range(4):
                cp = _remote(snd.at[q4], rcv.at[q4], ssem.at[q4], rsem.at[q4], sib)
                cp.wait_recv()
                cp.wait_send()
                snd[q4] = (own[q4].astype(F32) + rcv[q4].astype(F32)).astype(BF16)
            out = pltpu.make_async_copy(snd, o_ref, lsem)
            out.start()
            out.wait()

    return _call(
        core, name=name, grid=(steps,), jobs=jobs, in_specs=[a_spec, b_spec] + [HBM] * (after is not None),
        out_specs=[HBM], out_shape=[jax.ShapeDtypeStruct((4, R, C), BF16)],
        scratch=[pltpu.VMEM((4, R, C), BF16), pltpu.VMEM((4, R, C), BF16), pltpu.VMEM((4, R, C), BF16),
                 pltpu.SemaphoreType.DMA((4,)), pltpu.SemaphoreType.DMA((4,)), pltpu.SemaphoreType.DMA],
        args=[a, b] + [after] * (after is not None))


def _grad_w_in(dg, hb, name, jobs=()):
    T = hb.shape[0]
    return _grad_chip(dg, hb, pl.BlockSpec((None, T, FB), lambda s: (s, 0, 0)), _const_spec((T, D)),
                      (FB, D), None, name, jobs)


def _grad_w_out(act, dyb, name, jobs=(), after=None):
    T = dyb.shape[0]
    return _grad_chip(act, dyb, pl.BlockSpec((None, T, FB), lambda s: (s, 0, 0)), _const_spec((T, D)),
                      (FB, D), FO, name, jobs, after=after)


def _grad_w_mi(hb, dproj, name, jobs=()):
    T = hb.shape[0]
    return _grad_chip(hb, dproj, _const_spec((T, D)), pl.BlockSpec((T, MB), lambda s: (0, s)),
                      (D, MB), None, name, jobs, via_b=True)


def _grad_w_mo(ycat, dym, name, jobs=()):
    T = ycat.shape[0]
    return _grad_chip(ycat, dym, pl.BlockSpec((T, 2 * MO), lambda s: (0, s)), _const_spec((T, D)),
                      (2 * MO, D), MO, name, jobs)


def _adamw_math(w, g, m, v):
    m2 = ADAM_B1 * m + (1.0 - ADAM_B1) * g
    v2 = ADAM_B2 * v + (1.0 - ADAM_B2) * (g * g)
    m_hat = m2 / (1.0 - ADAM_B1 ** ADAM_STEP)
    v_hat = v2 / (1.0 - ADAM_B2 ** ADAM_STEP)
    delta = -ADAM_LR * (m_hat / (jnp.sqrt(v_hat) + ADAM_EPS) + ADAM_WD * w)
    return delta, m2, v2


def _adamw_reduce(parts, w, m, v, tr, name, after=None):
    R, C = w.shape

    def core(ins, outs, _):
        p_ref, w_ref, m_ref, v_ref = ins[:4]
        g_ref, d_ref, m2_ref, v2_ref = outs
        g = p_ref[0].astype(F32)
        for s in range(1, 4):
            g = g + p_ref[s].astype(F32)
        g_ref[...] = g
        d_ref[...], m2_ref[...], v2_ref[...] = _adamw_math(w_ref[...], g, m_ref[...], v_ref[...])

    blk = pl.BlockSpec((tr, C), lambda i: (i, 0))
    in_specs = [pl.BlockSpec((4, tr, C), lambda i: (0, i, 0)), blk, blk, blk]
    args = [parts, w, m, v]
    if after is not None:
        in_specs.append(HBM)
        args.append(after)
    return _call(
        core, name=name, grid=(R // tr,), in_specs=in_specs,
        out_specs=[blk, blk, blk, blk], out_shape=[jax.ShapeDtypeStruct((R, C), F32)] * 4, args=args)[0]


HBM_ONLY = pl.BlockSpec(memory_space=pltpu.HBM)
SEM = pl.BlockSpec(memory_space=pltpu.SEMAPHORE)
EFFECT = pltpu.SideEffectType.DATAFLOW_SIDE_EFFECTING


def _chip_scatter_start(gs, name):
    n = len(gs)

    def body(*refs):
        g_refs, land_refs = refs[:n], refs[n:2 * n]
        ssem, rsem = refs[2 * n:2 * n + 2]
        token = refs[-1]
        me = _me()
        mq = 2 * me[0] + me[1]
        for k, f in enumerate(CHIP_FLIPS):
            p = _flip(me, f)
            for a in range(n):
                _remote(g_refs[a].at[2 * p[0] + p[1]], land_refs[a].at[mq], ssem.at[4 * a + k], rsem.at[3 * a + k], p).start()
        for a in range(n):
            pltpu.make_async_copy(g_refs[a].at[mq], land_refs[a].at[mq], ssem.at[4 * a + 3]).start()
        token[...] = jnp.zeros_like(token)

    gs = [pltpu.with_memory_space_constraint(g, pltpu.HBM) for g in gs]
    lands = [pltpu.with_memory_space_constraint(lax.empty(g.shape, g.dtype), pltpu.HBM) for g in gs]
    res = pl.pallas_call(
        body, name=name,
        out_shape=(pltpu.SemaphoreType.DMA((4 * n,)), pltpu.SemaphoreType.DMA((3 * n,)))
        + tuple(pltpu.HBM(g.shape, g.dtype) for g in gs) * 2 + (jax.ShapeDtypeStruct((SUBL, LANES), F32),),
        in_specs=(HBM_ONLY,) * (2 * n), out_specs=(SEM, SEM) + (HBM_ONLY,) * (2 * n) + (VM,),
        input_output_aliases={a: 2 + a for a in range(2 * n)},
        compiler_params=pltpu.CompilerParams(has_side_effects=EFFECT),
    )(*gs, *lands)
    return res[:-1], res[-1]


def _chip_scatter_wait(handle, after, name):
    ssem, rsem = handle[:2]
    n = (len(handle) - 2) // 2
    thru = handle[2:]

    def body(*refs):
        g_refs, land_refs = refs[:n], refs[n:2 * n]
        ssem, rsem = refs[2 * n:2 * n + 2]
        me = _me()
        mq = 2 * me[0] + me[1]
        for k, f in enumerate(CHIP_FLIPS):
            p = _flip(me, f)
            pq = 2 * p[0] + p[1]
            for a in range(n):
                _remote(g_refs[a].at[pq], land_refs[a].at[mq], ssem.at[4 * a + k], rsem.at[3 * a + k], p).wait_send()
                _remote(g_refs[a].at[mq], land_refs[a].at[pq], ssem.at[4 * a + k], rsem.at[3 * a + k], p).wait_recv()
        for a in range(n):
            pltpu.make_async_copy(g_refs[a].at[mq], land_refs[a].at[mq], ssem.at[4 * a + 3]).wait()

    res = pl.pallas_call(
        body, name=name,
        out_shape=tuple(pltpu.HBM(t.shape, t.dtype) for t in thru),
        in_specs=(HBM_ONLY,) * (2 * n) + (SEM, SEM, HBM), out_specs=(HBM_ONLY,) * (2 * n),
        input_output_aliases={a: a for a in range(2 * n)},
        compiler_params=pltpu.CompilerParams(has_side_effects=EFFECT),
    )(*thru, ssem, rsem, after)
    return list(res[:n]), list(res[n:])


def _adamw_ada(sc_all, dd, w, m, v, tr, name, after=None):
    R, C = w.shape

    def core(ins, outs, _):
        sc_ref, dd_ref, w_ref, m_ref, v_ref = ins[:5]
        g_ref, d_ref, m2_ref, v2_ref = outs
        g = _dot_tn(sc_ref[...].astype(BF16), dd_ref[...].astype(BF16))
        g_ref[...] = g
        d_ref[...], m2_ref[...], v2_ref[...] = _adamw_math(w_ref[...], g, m_ref[...], v_ref[...])

    blk = pl.BlockSpec((tr, C), lambda i: (i, 0))
    return _call(
        core, name=name, grid=(R // tr,),
        in_specs=[pl.BlockSpec((64, tr), lambda i: (0, i)), pl.BlockSpec((64, C), lambda i: (0, 0)), blk, blk, blk]
        + [HBM] * (after is not None),
        out_specs=[blk, blk, blk, blk], out_shape=[jax.ShapeDtypeStruct((R, C), F32)] * 4,
        args=[sc_all, dd, w, m, v] + [after] * (after is not None))[0]


def _adamw_small(gathered, plain, grads, wmv, emit, name, after=None):
    nw = len(grads)
    ng, npl, ne = len(gathered), len(plain), len(emit)

    def core(ins, outs, _):
        srcs = []
        for a in range(ng):
            s = ins[a][0]
            for dev in range(1, NDEV):
                s = s + ins[a][dev]
            srcs.append(s)
        srcs += [ins[ng + a][...] for a in range(npl)]
        w_refs = ins[ng + npl:]
        for e, a in enumerate(emit):
            outs[e][...] = srcs[a]
        for t in range(nw):
            src, row = grads[t]
            g = srcs[src] if row is None else srcs[src][row:row + 1, :]
            w_ref, m_ref, v_ref = w_refs[3 * t:3 * t + 3]
            g_ref, d_ref, m2_ref, v2_ref = outs[ne + 4 * t:ne + 4 * t + 4]
            g_ref[...] = g
            d_ref[...], m2_ref[...], v2_ref[...] = _adamw_math(w_ref[...], g, m_ref[...], v_ref[...])

    out_shape = [jax.ShapeDtypeStruct(gathered[a].shape[1:], F32) for a in emit]
    for t in range(nw):
        out_shape += [jax.ShapeDtypeStruct(wmv[3 * t].shape, F32)] * 4
    return _call(
        core, name=name, grid=(), in_specs=[VM] * (ng + npl + 3 * nw) + [HBM] * (after is not None),
        out_specs=[VM] * (ne + 4 * nw), out_shape=out_shape,
        args=list(gathered) + list(plain) + list(wmv) + [after] * (after is not None))[0]


def _ada_pieces():
    out = []
    for r in range(9):
        pos = r * D
        while pos < (r + 1) * D:
            j = pos // ADA_B
            nxt = min((r + 1) * D, (j + 1) * ADA_B)
            out.append((r, j, pos - r * D, pos - j * ADA_B, nxt - pos))
            pos = nxt
    return out


def _ada_fwd(c_pad, w_ada, b_cols, cw_pad, bl, jobs=()):
    def core(ins, outs, scs, start_jobs, finish_jobs):
        c_ref, w_ref, b_ref, cwp_ref = ins
        mod_refs, (sc_ref, cw_ref) = outs[0:3], outs[3:5]
        ada_ref, cbuf, send_buf, ssem, rsem = scs
        me = _me()
        mi = _lin(me)
        cbuf[mi] = c_ref[...]
        cw_ref[mi] = cwp_ref[...]
        peers = [_flip(me, f) for f in FLIPS]
        first = []
        for k, p in enumerate(peers):
            first.append(_remote(cbuf.at[mi], cbuf.at[mi], ssem.at[k], rsem.at[k], p))
            first.append(_remote(cw_ref.at[mi], cw_ref.at[mi], ssem.at[7 + k], rsem.at[7 + k], p))
        for cp in first:
            cp.start()
        start_jobs()
        for k, p in enumerate(peers):
            pi = _lin(p)
            _remote(cbuf.at[pi], cbuf.at[pi], ssem.at[k], rsem.at[k], p).wait_recv()
            _remote(cw_ref.at[pi], cw_ref.at[pi], ssem.at[7 + k], rsem.at[7 + k], p).wait_recv()
        c_all = cbuf[...].reshape(8 * 8, D)
        sc = c_all * _sigmoid(c_all)
        sc_ref[...] = sc
        res = _dot(sc.astype(BF16), w_ref[...].astype(BF16)) + b_ref[...]
        send_buf[...] = res.reshape(8, 8, ADA_B)
        ada_ref[mi] = send_buf[mi]
        second = []
        for k, p in enumerate(peers):
            second.append(_remote(send_buf.at[_lin(p)], ada_ref.at[mi], ssem.at[14 + k], rsem.at[14 + k], p))
        for cp in second:
            cp.start()
        finish_jobs()
        for k, p in enumerate(peers):
            _remote(send_buf.at[mi], ada_ref.at[_lin(p)], ssem.at[14 + k], rsem.at[14 + k], p).wait_recv()
        for m_ref in mod_refs:
            m_ref[...] = jnp.zeros_like(m_ref)
        for r, j, in_row, in_blk, width in _ada_pieces():
            for b in range(bl):
                mod_refs[r // 3][b, r % 3:r % 3 + 1, in_row:in_row + width] = ada_ref[j, b:b + 1, in_blk:in_blk + width]
        for cp in first + second:
            cp.wait_send()

    return _call(
        core, name="ada_fwd", grid=(), jobs=jobs, core_starts=True, in_specs=[VM, VM, VM, VM], out_specs=[VM] * 5,
        out_shape=[jax.ShapeDtypeStruct((bl, 8, D), F32)] * 3
        + [jax.ShapeDtypeStruct((64, D), F32), jax.ShapeDtypeStruct((8, 32, 64), F32)],
        scratch=[pltpu.VMEM((8, 8, ADA_B), F32), pltpu.VMEM((8, 8, D), F32), pltpu.VMEM((8, 8, ADA_B), F32),
                 pltpu.SemaphoreType.DMA((21,)), pltpu.SemaphoreType.DMA((21,))],
        args=[c_pad, w_ada, b_cols, cw_pad])


def _ada_bwd(mods, rows, vecs, vec_rows):
    bl = mods[0].shape[0]

    def core(ins, outs, scs):
        dd_ref, gb_ref, pack_ref = outs
        pack_ref[...] = jnp.zeros_like(pack_ref)
        for t, (a, row) in enumerate(vec_rows):
            pack_ref[t:t + 1, :] = ins[len(mods) + a][row:row + 1, :]
        d_ref, rbuf, ssem, rsem = scs
        me = _me()
        mi = _lin(me)
        peers = [_flip(me, f) for f in FLIPS]
        d_ref[...] = jnp.zeros_like(d_ref)
        for r, j, in_row, in_blk, width in _ada_pieces():
            a, row = rows[r]
            for b in range(bl):
                d_ref[j, b:b + 1, in_blk:in_blk + width] = ins[a][b, row:row + 1, in_row:in_row + width]
        rbuf[mi] = d_ref[mi]
        first = []
        for k, p in enumerate(peers):
            first.append(_remote(d_ref.at[_lin(p)], rbuf.at[mi], ssem.at[k], rsem.at[k], p))
        for cp in first:
            cp.start()
        for k, p in enumerate(peers):
            _remote(d_ref.at[mi], rbuf.at[_lin(p)], ssem.at[k], rsem.at[k], p).wait_recv()
        dd = rbuf[...].reshape(64, ADA_B)
        dd_ref[...] = dd
        gb_ref[...] = jnp.broadcast_to(_colsum(dd), (8, ADA_B))
        for cp in first:
            cp.wait_send()

    return _call(
        core, name="ada_bwd", grid=(), in_specs=[VM] * (len(mods) + len(vecs)), out_specs=[VM, VM, VM],
        out_shape=[jax.ShapeDtypeStruct((64, ADA_B), F32), jax.ShapeDtypeStruct((8, ADA_B), F32),
                   jax.ShapeDtypeStruct((8, D), F32)],
        scratch=[pltpu.VMEM((8, 8, ADA_B), F32), pltpu.VMEM((8, 8, ADA_B), F32),
                 pltpu.SemaphoreType.DMA((7,)), pltpu.SemaphoreType.DMA((7,))],
        args=list(mods) + list(vecs))[0]


SMALL_D = ("g_pre_f1", "g_post_f1", "g_pre_m", "g_post_m", "g_pre_f2", "g_post_f2")
SMALL_W = ("gmlp_norm_g", "gmlp_norm_b", "conv_b", "conv_norm_g", "conv_norm_b", "g_out_a", "g_out_b")


def kernel(x, c, w_ada, b_ada, g_pre_f1, g_post_f1, w_f1_in, w_f1_out, g_pre_m, g_post_m, w_mix_in, gmlp_norm_g, gmlp_norm_b, w_spatial, b_spatial, conv_w, conv_b, conv_norm_g, conv_norm_b, g_out_a, g_out_b, w_mix_out, g_pre_f2, g_post_f2, w_f2_in, w_f2_out, loss_target, m_w_ada, m_b_ada, m_g_pre_f1, m_g_post_f1, m_w_f1_in, m_w_f1_out, m_g_pre_m, m_g_post_m, m_w_mix_in, m_gmlp_norm_g, m_gmlp_norm_b, m_w_spatial, m_b_spatial, m_conv_w, m_conv_b, m_conv_norm_g, m_conv_norm_b, m_g_out_a, m_g_out_b, m_w_mix_out, m_g_pre_f2, m_g_post_f2, m_w_f2_in, m_w_f2_out, v_w_ada, v_b_ada, v_g_pre_f1, v_g_post_f1, v_w_f1_in, v_w_f1_out, v_g_pre_m, v_g_post_m, v_w_mix_in, v_gmlp_norm_g, v_gmlp_norm_b, v_w_spatial, v_b_spatial, v_conv_w, v_conv_b, v_conv_norm_g, v_conv_norm_b, v_g_out_a, v_g_out_b, v_w_mix_out, v_g_pre_f2, v_g_post_f2, v_w_f2_in, v_w_f2_out):
    given = dict(locals())
    bl, seq, _ = x.shape
    T = bl * seq
    tm = min(256, seq // 2)
    mi = _lin((lax.axis_index("x"), lax.axis_index("y"), lax.axis_index("c")))

    def shard_in(w):
        return w[0].T.astype(BF16)

    g_f1 = _RelayGather([shard_in(w_f1_in), w_f1_out[0].astype(BF16)], ("rows", "out"))
    s_f2 = shard_in(w_f2_in)
    g_mx = _Gather([w_mix_in[0].astype(BF16), w_mix_out[0].astype(BF16), w_f2_out[0].astype(BF16), s_f2[:, 0:D // 4]],
                   ("rows", "rows", "out", "rows"), late_mid=True)
    g_f2 = _Gather([s_f2[:, D // 4:D]], ("rows",))

    c_pad = jnp.pad(c, ((0, 8 - bl), (0, 0)))
    b_cols = lax.dynamic_slice(b_ada, (0, mi * ADA_B), (1, ADA_B))
    cw_pad = jnp.pad(conv_w[0], ((0, 1), (0, 0)))
    (mod1, mod2, mod3, sc_all, cw_all), ((wi1, wo1),) = _ada_fwd(c_pad, w_ada[0], b_cols, cw_pad, bl, jobs=[g_f1])
    cw_full = cw_all.transpose(1, 0, 2).reshape(32, WA)

    zrow = jnp.zeros((1, D), F32)
    gv1 = jnp.concatenate([g_pre_f1, g_post_f1] + [zrow] * 6, axis=0)
    gvm = jnp.concatenate([g_pre_m, g_post_m] + [zrow] * 6, axis=0)
    gv2 = jnp.concatenate([g_pre_f2, g_post_f2] + [zrow] * 6, axis=0)
    v512 = jnp.concatenate([gmlp_norm_g, gmlp_norm_b, conv_b, conv_norm_g, conv_norm_b, g_out_a, g_out_b,
                            jnp.zeros((1, WA), F32)], axis=0)
    ws = w_spatial[0]
    bias_full = jnp.repeat(b_spatial[0].T, HD, axis=1)
    esel = (lax.broadcasted_iota(jnp.int32, (8, WA), 1) // HD == lax.broadcasted_iota(jnp.int32, (8, WA), 0)).astype(F32)

    x0 = x.reshape(T, D)
    (x1, gu1, y1), ((wmi, wmo, wo2, wi2a),) = _ffn_fwd(x0, mod1, gv1, wi1, wo1, tm, "ffn1_fwd", jobs=[g_mx])
    wmo = wmo.reshape(D, D)
    (x2, proj, ym, conv), ((wi2b,),) = _mixer_fwd(x1, mod2, gvm, wmi, wmo, v512, ws, bias_full, cw_full, tm, "mixer_fwd", jobs=[g_f2])

    (dx2, dg2, act2, hb2, dyb2, mg3, vg3, loss_blk), _ = _ffn_last(
        x2, loss_target.reshape(T, D), mod3, gv2, (wi2a, wi2b), wo2, tm, "ffn2_fwd_bwd")
    (g_wi2,), _ = _grad_w_in(dg2, hb2, "ffn2_gw_in")
    (g_wo2,), _ = _grad_w_out(act2, dyb2, "ffn2_gw_out")
    (dpart, dymb, ycat, mg2a, vgma, v5g, gws, gbs), ((p_wo2,),) = _mixer_bwd_a(
        dx2, ym, proj, conv, mod2, gvm, wmo, v512, ws, bias_full, esel, tm, "mixer_bwd_a",
        jobs=[_ChipScatter([g_wo2])])
    (dx1, dproj, hbm, mg2b, vgmb, dcw), ((p_wi2,),) = _mixer_bwd_b(
        dx2, x1, dpart, proj, mod2, gvm, wmi, cw_full, tm, "mixer_bwd_b", jobs=[_ChipScatter([g_wi2])])
    (g_wmi,), _ = _grad_w_mi(hbm, dproj, "mixer_gw_in")
    (g_wmo,), _ = _grad_w_mo(ycat, dymb, "mixer_gw_out")
    p2 = jnp.concatenate([v5g, dcw], axis=0)
    (dx0, dg1, act1, hb1, dyb1, mg1, vg1), _ = _ffn_bwd(dx1, x0, y1, gu1, mod1, gv1, wi1, wo1, tm, "ffn1_bwd")

    ada_rows = [(0, 0), (0, 1), (0, 2), (1, 0), (1, 1), (2, 2), (3, 0), (3, 1), (3, 2)]
    p1_rows = [(0, 0), (0, 1), (1, 0), (2, 1), (3, 0), (3, 1), (4, 0)]
    dd_all, gb_own, p1 = _ada_bwd([mg1, mg2b, mg2a, mg3], ada_rows, [vg1, vgmb, vgma, vg3, loss_blk], p1_rows)

    (g_wi1,), ((a1, a2, a3, a4, gb_all), (p_wmi, p_wmo)) = _grad_w_in(
        dg1, hb1, "ffn1_gw_in",
        jobs=[_Gather([p1, p2, gws, gbs, gb_own], ("rows",) * 5), _ChipScatter([g_wmi, g_wmo])])
    g_bada = gb_all[:, 0, :].reshape(1, 9 * D)

    h_i1, token = _chip_scatter_start([g_wi1], "tail_start")
    (g_wo1,), _ = _grad_w_out(act1, dyb1, "ffn1_gw_out", after=token)
    h_o1, token = _chip_scatter_start([g_wo1], "tail2_start")

    res = {}
    quad = _adamw_reduce(p_wi2, w_f2_in[0].T, m_w_f2_in[0].T, v_w_f2_in[0].T, FO, "adamw_w_f2_in", after=token)
    res["w_f2_in"] = tuple(t.T[None] for t in quad)
    for nm, part, tr in (("w_f2_out", p_wo2, FO), ("w_mix_in", p_wmi, 256), ("w_mix_out", p_wmo, MO)):
        quad = _adamw_reduce(part, given[nm][0], given["m_" + nm][0], given["v_" + nm][0], tr, "adamw_" + nm, after=quad[1])
        res[nm] = tuple(t[None] for t in quad)
    quad = _adamw_ada(sc_all, dd_all, w_ada[0], m_w_ada[0], v_w_ada[0], 256, "adamw_w_ada", after=quad[1])
    res["w_ada"] = tuple(t[None] for t in quad)

    small = SMALL_D + SMALL_W + ("w_spatial", "b_spatial", "b_ada")
    grads = [(0, r) for r in range(6)] + [(1, r) for r in range(7)] + [(2, None), (3, None), (4, None)]
    wmv = []
    for nm in small:
        for pre in ("", "m_", "v_"):
            wmv.append(given[pre + nm][0] if nm in ("w_spatial", "b_spatial") else given[pre + nm])
    outs = _adamw_small([a1, a2, a3, a4], [g_bada], grads, wmv, (0, 1), "adamw_small", after=quad[1])

    _, (p_wi1,) = _chip_scatter_wait(h_i1, outs[0], "tail_wait")
    quad = _adamw_reduce(p_wi1, w_f1_in[0].T, m_w_f1_in[0].T, v_w_f1_in[0].T, FO, "adamw_w_f1_in")
    res["w_f1_in"] = tuple(t.T[None] for t in quad)
    _, (p_wo1,) = _chip_scatter_wait(h_o1, quad[1], "tail2_wait")
    quad = _adamw_reduce(p_wo1, w_f1_out[0], m_w_f1_out[0], v_w_f1_out[0], FO, "adamw_w_f1_out")
    res["w_f1_out"] = tuple(t[None] for t in quad)
    loss = outs[0][6, 0]
    for t, nm in enumerate(small):
        quad = outs[2 + 4 * t:6 + 4 * t]
        res[nm] = tuple(q[None] for q in quad) if nm in ("w_spatial", "b_spatial") else tuple(quad)
    g_cw = lax.dynamic_slice(outs[1], (8, mi * 64), (32, 64))
    wmv = [jnp.pad(given[pre + "conv_w"][0], ((0, 1), (0, 0)), constant_values=1.0 if pre == "v_" else 0.0)
           for pre in ("", "m_", "v_")]
    quad = _adamw_small([], [g_cw], [(0, None)], wmv, (), "adamw_conv_w")
    res["conv_w"] = tuple(q[0:CONV_K][None] for q in quad)

    order = ["w_ada", "b_ada", "g_pre_f1", "g_post_f1", "w_f1_in", "w_f1_out", "g_pre_m", "g_post_m", "w_mix_in",
             "gmlp_norm_g", "gmlp_norm_b", "w_spatial", "b_spatial", "conv_w", "conv_b", "conv_norm_g", "conv_norm_b",
             "g_out_a", "g_out_b", "w_mix_out", "g_pre_f2", "g_post_f2", "w_f2_in", "w_f2_out"]
    out = [loss, dx0.reshape(bl, seq, D)]
    for k in range(4):
        out += [res[nm][k] for nm in order]
    return tuple(out)
```

```python
import jax
import jax.numpy as jnp
from jax import lax
from jax.experimental import pallas as pl
from jax.experimental.pallas import tpu as pltpu

F32 = jnp.float32
BF16 = jnp.bfloat16

D = 1024
DFF = 2816
NDEV = 8
FB = 2 * DFF // NDEV
NCH = DFF // FB
LANES = 128
SUBL = 8
FO = DFF // NDEV
WA = 512
NSLAB = WA // LANES
NHEAD = 8
HD = 64
CHUNK = 128
CONV_K = 31
HALO = 32
MB = 2 * (WA + WA) // NDEV
MO = D // NDEV
ADA_B = 9 * D // NDEV
EPS = 1e-6
HALF = 0.5

ADAM_LR = 0.001
ADAM_B1 = 0.9
ADAM_B2 = 0.999
ADAM_EPS = 1e-08
ADAM_WD = 0.01
ADAM_STEP = 10

VMEM_LIMIT = 56 * 1024 * 1024
MESH = pl.DeviceIdType.MESH
FLIPS = ((0, 0, 1), (1, 0, 0), (0, 1, 0), (1, 1, 0), (1, 0, 1), (0, 1, 1), (1, 1, 1))
CHIP_FLIPS = ((1, 0, 0), (0, 1, 0), (1, 1, 0))
HBM = pl.BlockSpec(memory_space=pl.ANY)
VM = pl.BlockSpec(memory_space=pltpu.VMEM)


def _dot(a, b):
    return lax.dot_general(a, b, (((1,), (0,)), ((), ())), preferred_element_type=F32)


def _dot_nt(a, b):
    return lax.dot_general(a, b, (((1,), (1,)), ((), ())), preferred_element_type=F32)


def _dot_tn(a, b):
    return lax.dot_general(a, b, (((0,), (0,)), ((), ())), preferred_element_type=F32)


def _rowmean(v):
    return jnp.mean(v, axis=-1, keepdims=True)


def _colsum(v):
    return jnp.sum(v, axis=0, keepdims=True)


def _sigmoid(v):
    return 0.5 * jnp.tanh(0.5 * v) + 0.5


def _const_spec(shape):
    nd = len(shape)
    return pl.BlockSpec(shape, lambda *_: (0,) * nd, pipeline_mode=pl.Buffered(1))


def _me():
    return lax.axis_index("x"), lax.axis_index("y"), lax.axis_index("c")


def _flip(me, f):
    return tuple(1 - v if b else v for v, b in zip(me, f))


def _lin(p):
    return 4 * p[0] + 2 * p[1] + p[2]


def _remote(src, dst, send_sem, recv_sem, dev):
    return pltpu.make_async_remote_copy(src_ref=src, dst_ref=dst, send_sem=send_sem, recv_sem=recv_sem,
                                        device_id=dev, device_id_type=MESH)


def _blk(kind, ref, p):
    if kind == "out":
        return ref.at[2 * p[0] + p[1], pl.ds(p[2] * FO, FO), :]
    return ref.at[_lin(p)]


class _Gather:
    def __init__(self, shards, kinds, late_mid=False):
        self.late_mid = late_mid
        self.kinds = kinds
        self.n = len(shards)
        self.ins = list(shards)
        self.out_shape = [jax.ShapeDtypeStruct((4, FB, D) if k == "out" else (NDEV,) + s.shape, s.dtype)
                          for s, k in zip(shards, kinds)]
        self.sems = [pltpu.SemaphoreType.DMA((7 * self.n,)), pltpu.SemaphoreType.DMA((7 * self.n,)),
                     pltpu.SemaphoreType.DMA((self.n,))]

    def _first(self, ins, outs, sems):
        ssem, rsem, lsem = sems
        me = _me()
        sib = _flip(me, (0, 0, 1))
        cps, loc = [], []
        for a in range(self.n):
            mine = _blk(self.kinds[a], outs[a], me)
            loc.append(pltpu.make_async_copy(ins[a], mine, lsem.at[a]))
            cps.append(_remote(ins[a], mine, ssem.at[7 * a], rsem.at[7 * a], sib))
            for j, f in enumerate(CHIP_FLIPS):
                cps.append(_remote(ins[a], mine, ssem.at[7 * a + 1 + j], rsem.at[7 * a + 1 + j], _flip(me, f)))
        return cps, loc

    def _passed(self, outs, sems):
        ssem, rsem, _ = sems
        me = _me()
        sib = _flip(me, (0, 0, 1))
        cps = []
        for j, f in enumerate(CHIP_FLIPS):
            for a in range(self.n):
                blk = _blk(self.kinds[a], outs[a], _flip(me, f))
                cps.append(_remote(blk, blk, ssem.at[7 * a + 4 + j], rsem.at[7 * a + 4 + j], sib))
        return cps

    def start(self, ins, outs, sems):
        cps, loc = self._first(ins, outs, sems)
        for cp in loc + cps:
            cp.start()

    def mid(self, ins, outs, sems):
        ssem, rsem, _ = sems
        me = _me()
        passed = self._passed(outs, sems)
        t = 0
        for j, f in enumerate(CHIP_FLIPS):
            for a in range(self.n):
                blk = _blk(self.kinds[a], outs[a], _flip(me, f))
                _remote(blk, blk, ssem.at[7 * a + 1 + j], rsem.at[7 * a + 1 + j], _flip(me, f)).wait_recv()
                passed[t].start()
                t += 1

    def end(self, ins, outs, sems):
        ssem, rsem, _ = sems
        me = _me()
        sib = _flip(me, (0, 0, 1))
        for a in range(self.n):
            blk = _blk(self.kinds[a], outs[a], sib)
            _remote(blk, blk, ssem.at[7 * a], rsem.at[7 * a], sib).wait_recv()
            for j, f in enumerate(CHIP_FLIPS):
                blk = _blk(self.kinds[a], outs[a], _flip(_flip(me, f), (0, 0, 1)))
                _remote(blk, blk, ssem.at[7 * a + 4 + j], rsem.at[7 * a + 4 + j], sib).wait_recv()
        cps, loc = self._first(ins, outs, sems)
        for cp in cps + self._passed(outs, sems):
            cp.wait_send()
        for cp in loc:
            cp.wait()


class _RelayGather(_Gather):
    def _peers(self):
        me = _me()
        c = me[2]
        to = (me[0] + (1 - c) - 2 * me[0] * (1 - c), me[1] + c - 2 * me[1] * c, c)
        frm = (me[0] + c - 2 * me[0] * c, me[1] + (1 - c) - 2 * me[1] * (1 - c), c)
        return me, _flip(me, (0, 0, 1)), to, frm, _flip(me, (1, 1, 0))

    def _first(self, ins, outs, sems):
        ssem, rsem, lsem = sems
        me, sib, to, frm, _ = self._peers()
        cps, loc = [], []
        for a in range(self.n):
            mine = _blk(self.kinds[a], outs[a], me)
            loc.append(pltpu.make_async_copy(ins[a], mine, lsem.at[a]))
            for slot, dev in ((0, sib), (1, to), (2, frm)):
                cps.append(_remote(ins[a], mine, ssem.at[7 * a + slot], rsem.at[7 * a + slot], dev))
        return cps, loc

    def _block_copy(self, outs, sems, a, slot, owner, dev):
        ssem, rsem, _ = sems
        blk = _blk(self.kinds[a], outs[a], owner)
        return _remote(blk, blk, ssem.at[7 * a + slot], rsem.at[7 * a + slot], dev)

    def mid(self, ins, outs, sems):
        me, sib, to, frm, _ = self._peers()
        for a in range(self.n):
            self._block_copy(outs, sems, a, 2, frm, frm).wait_recv()
            self._block_copy(outs, sems, a, 3, frm, to).start()
            self._block_copy(outs, sems, a, 5, frm, sib).start()
        for a in range(self.n):
            self._block_copy(outs, sems, a, 1, to, to).wait_recv()
            self._block_copy(outs, sems, a, 4, to, sib).start()

    def end(self, ins, outs, sems):
        me, sib, to, frm, far = self._peers()
        up = (0, 0, 1)
        for a in range(self.n):
            self._block_copy(outs, sems, a, 3, far, to).wait_recv()
            self._block_copy(outs, sems, a, 6, far, sib).start()
        for a in range(self.n):
            for slot, owner in ((0, sib), (4, _flip(frm, up)), (5, _flip(to, up)), (6, _flip(far, up))):
                self._block_copy(outs, sems, a, slot, owner, sib).wait_recv()
        cps, loc = self._first(ins, outs, sems)
        for a in range(self.n):
            cps += [self._block_copy(outs, sems, a, 3, frm, to), self._block_copy(outs, sems, a, 4, to, sib),
                    self._block_copy(outs, sems, a, 5, frm, sib), self._block_copy(outs, sems, a, 6, far, sib)]
        for cp in cps:
            cp.wait_send()
        for cp in loc:
            cp.wait()


class _ChipScatter:
    def __init__(self, grads):
        self.n = len(grads)
        self.ins = list(grads)
        self.out_shape = [jax.ShapeDtypeStruct(g.shape, BF16) for g in grads]
        self.sems = [pltpu.SemaphoreType.DMA((3 * self.n,)), pltpu.SemaphoreType.DMA((3 * self.n,)),
                     pltpu.SemaphoreType.DMA((self.n,))]

    def _copies(self, ins, outs, sems):
        ssem, rsem, lsem = sems
        me = _me()
        mq = 2 * me[0] + me[1]
        loc = [pltpu.make_async_copy(ins[a].at[mq], outs[a].at[mq], lsem.at[a]) for a in range(self.n)]
        cps = []
        for k, f in enumerate(CHIP_FLIPS):
            p = _flip(me, f)
            for a in range(self.n):
                cps.append(_remote(ins[a].at[2 * p[0] + p[1]], outs[a].at[mq], ssem.at[3 * a + k], rsem.at[3 * a + k], p))
        return cps, loc

    def start(self, ins, outs, sems):
        cps, loc = self._copies(ins, outs, sems)
        for cp in loc + cps:
            cp.start()

    mid = None

    def end(self, ins, outs, sems):
        ssem, rsem, _ = sems
        me = _me()
        mq = 2 * me[0] + me[1]
        for k, f in enumerate(CHIP_FLIPS):
            p = _flip(me, f)
            for a in range(self.n):
                _remote(ins[a].at[mq], outs[a].at[2 * p[0] + p[1]], ssem.at[3 * a + k], rsem.at[3 * a + k], p).wait_recv()
        cps, loc = self._copies(ins, outs, sems)
        for cp in cps:
            cp.wait_send()
        for cp in loc:
            cp.wait()


def _call(core, *, name, grid, in_specs, out_specs, out_shape, args, scratch=(), jobs=(), core_starts=False):
    n_in, n_out, n_sc = len(in_specs), len(out_specs), len(scratch)
    steps = 1
    for g in grid:
        steps *= g

    def body(*refs):
        pos = [0]

        def take(k):
            r = refs[pos[0]:pos[0] + k]
            pos[0] += k
            return r

        ins = take(n_in)
        j_ins = [take(len(j.ins)) for j in jobs]
        outs = take(n_out)
        j_outs = [take(len(j.out_shape)) for j in jobs]
        scs = take(n_sc)
        j_sems = [take(len(j.sems)) for j in jobs]
        if len(grid) == 2:
            step = pl.program_id(0) * grid[1] + pl.program_id(1)
        elif len(grid) == 1:
            step = pl.program_id(0)
        else:
            step = 0
        def start_jobs():
            for j, ji, jo, js in zip(jobs, j_ins, j_outs, j_sems):
                j.start(ji, jo, js)

        if grid:
            pl.when(step == 0)(start_jobs)
        elif not core_starts:
            start_jobs()
        for j, ji, jo, js in zip(jobs, j_ins, j_outs, j_sems):
            if j.mid is not None and grid:
                at = steps // 2 if isinstance(j, _RelayGather) else max(steps - 2, 0) if j.late_mid else (3 * steps) // 4
                pl.when(step == at)(lambda j=j, ji=ji, jo=jo, js=js: j.mid(ji, jo, js))
        def finish_jobs():
            for j, ji, jo, js in zip(jobs, j_ins, j_outs, j_sems):
                if j.mid is not None:
                    j.mid(ji, jo, js)
                j.end(ji, jo, js)

        if core_starts:
            core(ins, outs, scs, start_jobs, finish_jobs)
        elif core is not None:
            core(ins, outs, scs)
        if grid:
            for j, ji, jo, js in zip(jobs, j_ins, j_outs, j_sems):
                pl.when(step == steps - 1)(lambda j=j, ji=ji, jo=jo, js=js: j.end(ji, jo, js))
        elif not core_starts:
            finish_jobs()

    all_in = list(in_specs)
    all_args = list(args)
    all_out = list(out_specs)
    all_shape = list(out_shape)
    all_sc = list(scratch)
    for j in jobs:
        all_in += [HBM] * len(j.ins)
        all_args += j.ins
    for j in jobs:
        all_out += [HBM] * len(j.out_shape)
        all_shape += j.out_shape
        all_sc += j.sems
    params = dict(vmem_limit_bytes=VMEM_LIMIT)
    if grid:
        params["dimension_semantics"] = ("arbitrary",) * len(grid)
    res = pl.pallas_call(
        body, name=name, grid=grid, in_specs=all_in, out_specs=all_out, out_shape=all_shape,
        scratch_shapes=all_sc, compiler_params=pltpu.CompilerParams(**params),
    )(*all_args)
    core_res = list(res[:n_out])
    job_res = []
    pos = n_out
    for j in jobs:
        job_res.append(list(res[pos:pos + len(j.out_shape)]))
        pos += len(j.out_shape)
    return core_res, job_res


def _ffn_fwd(x, mod, gvec, w_in, w_out, tm, name, jobs=()):
    T = x.shape[0]
    nt = T // tm
    tps = nt // mod.shape[0]

    def core(ins, outs, _):
        x_ref, mod_ref, g_ref, win_ref, wout_ref = ins
        xo_ref, gu_ref, y_ref = outs
        xv = x_ref[...]
        sh, sc, gt = mod_ref[0:1, :], mod_ref[1:2, :], mod_ref[2:3, :]
        r = lax.rsqrt(_rowmean(xv * xv) + EPS)
        h = (xv * r * g_ref[0:1, :]) * (1.0 + sc) + sh
        hb = h.astype(BF16)
        y = jnp.zeros((tm, D), F32)
        for cidx in range(NCH):
            gate = _dot_nt(hb, win_ref[cidx])
            up = _dot_nt(hb, win_ref[NCH + cidx])
            gu_ref[cidx] = gate.astype(BF16)
            gu_ref[NCH + cidx] = up.astype(BF16)
            act = gate * _sigmoid(gate) * up
            y = y + _dot(act.astype(BF16), wout_ref[cidx])
        y_ref[...] = y
        ry = lax.rsqrt(_rowmean(y * y) + EPS)
        xo_ref[...] = xv + (HALF * gt) * (y * ry * g_ref[1:2, :])

    tile = pl.BlockSpec((tm, D), lambda i: (i, 0))
    return _call(
        core, name=name, grid=(nt,), jobs=jobs,
        in_specs=[tile, pl.BlockSpec((None, 8, D), lambda i: (i // tps, 0, 0)), _const_spec((8, D)),
                  _const_spec((8, FB, D)), _const_spec((4, FB, D))],
        out_specs=[tile, pl.BlockSpec((8, tm, FB), lambda i: (0, i, 0)), tile],
        out_shape=[jax.ShapeDtypeStruct((T, D), F32), jax.ShapeDtypeStruct((8, T, FB), BF16),
                   jax.ShapeDtypeStruct((T, D), F32)],
        args=[x, mod, gvec, w_in, w_out])


def _ffn_bwd(dxo, x, y, gu, mod, gvec, w_in, w_out, tm, name, jobs=()):
    T = x.shape[0]
    nt = T // tm
    nb = mod.shape[0]
    tps = nt // nb

    def core(ins, outs, _):
        dxo_ref, x_ref, y_ref, gu_ref, mod_ref, g_ref, win_ref, wout_ref = ins
        dx_ref, dg_ref, act_ref, hb_ref, dyb_ref, mg_ref, vg_ref = outs
        i = pl.program_id(0)
        xv = x_ref[...]
        dxo_v = dxo_ref[...]
        yv = y_ref[...]
        sh, sc, gt = mod_ref[0:1, :], mod_ref[1:2, :], mod_ref[2:3, :]
        gpre, gpost = g_ref[0:1, :], g_ref[1:2, :]
        r = lax.rsqrt(_rowmean(xv * xv) + EPS)
        xh = xv * r
        n = xh * gpre
        hb = (n * (1.0 + sc) + sh).astype(BF16)
        hb_ref[...] = hb
        ry = lax.rsqrt(_rowmean(yv * yv) + EPS)
        yh = yv * ry
        d_gt = _colsum(HALF * dxo_v * (yh * gpost))
        dp = (HALF * gt) * dxo_v
        d_gpost = _colsum(dp * yh)
        dyh = dp * gpost
        dy = ry * (dyh - yh * _rowmean(dyh * yh))
        dyb = dy.astype(BF16)
        dyb_ref[...] = dyb
        dh = jnp.zeros((tm, D), F32)
        for cidx in range(NCH):
            gate = gu_ref[cidx].astype(F32)
            up = gu_ref[NCH + cidx].astype(F32)
            sig = _sigmoid(gate)
            s = gate * sig
            act_ref[cidx] = (s * up).astype(BF16)
            d_act = _dot_nt(dyb, wout_ref[cidx])
            d_up = (d_act * s).astype(BF16)
            d_gate = (d_act * up * (sig * (1.0 + gate * (1.0 - sig)))).astype(BF16)
            dg_ref[cidx] = d_gate
            dg_ref[NCH + cidx] = d_up
            dh = dh + _dot(d_gate, win_ref[cidx]) + _dot(d_up, win_ref[NCH + cidx])
        d_sc = _colsum(dh * n)
        d_sh = _colsum(dh)
        dn = dh * (1.0 + sc)
        d_gpre = _colsum(dn * xh)
        dxh = dn * gpre
        dx_ref[...] = dxo_v + r * (dxh - xh * _rowmean(dxh * xh))

        @pl.when(i % tps == 0)
        def _():
            mg_ref[...] = jnp.zeros((8, D), F32)

        @pl.when(i == 0)
        def _():
            vg_ref[...] = jnp.zeros((8, D), F32)

        mg_ref[0:1, :] += d_sh
        mg_ref[1:2, :] += d_sc
        mg_ref[2:3, :] += d_gt
        vg_ref[0:1, :] += d_gpre
        vg_ref[1:2, :] += d_gpost

    tile = pl.BlockSpec((tm, D), lambda i: (i, 0))
    return _call(
        core, name=name, grid=(nt,), jobs=jobs,
        in_specs=[tile, tile, tile, pl.BlockSpec((8, tm, FB), lambda i: (0, i, 0)),
                  pl.BlockSpec((None, 8, D), lambda i: (i // tps, 0, 0)), _const_spec((8, D)),
                  _const_spec((8, FB, D)), _const_spec((4, FB, D))],
        out_specs=[tile, pl.BlockSpec((8, tm, FB), lambda i: (0, i, 0)),
                   pl.BlockSpec((4, tm, FB), lambda i: (0, i, 0)), tile, tile,
                   pl.BlockSpec((None, 8, D), lambda i: (i // tps, 0, 0)), pl.BlockSpec((8, D), lambda i: (0, 0))],
        out_shape=[jax.ShapeDtypeStruct((T, D), F32), jax.ShapeDtypeStruct((8, T, FB), BF16),
                   jax.ShapeDtypeStruct((4, T, FB), BF16), jax.ShapeDtypeStruct((T, D), BF16),
                   jax.ShapeDtypeStruct((T, D), BF16), jax.ShapeDtypeStruct((nb, 8, D), F32),
                   jax.ShapeDtypeStruct((8, D), F32)],
        args=[dxo, x, y, gu, mod, gvec, w_in, w_out])


def _ffn_last(x, target, mod, gvec, w_in, w_out, tm, name, jobs=()):
    T = x.shape[0]
    nt = T // tm
    nb = mod.shape[0]
    tps = nt // nb

    def core(ins, outs, scs):
        x_ref, t_ref, mod_ref, g_ref, wina_ref, winb_ref, wout_ref = ins
        dx_ref, dg_ref, act_ref, hb_ref, dyb_ref, mg_ref, vg_ref, loss_ref = outs
        hd2 = w_in[0].shape[2]
        (gu_s,) = scs
        i = pl.program_id(0)
        xv = x_ref[...]
        sh, sc, gt = mod_ref[0:1, :], mod_ref[1:2, :], mod_ref[2:3, :]
        gpre, gpost = g_ref[0:1, :], g_ref[1:2, :]
        r = lax.rsqrt(_rowmean(xv * xv) + EPS)
        xh = xv * r
        n = xh * gpre
        hb = (n * (1.0 + sc) + sh).astype(BF16)
        hb_ref[...] = hb
        hba, hbb = hb[:, 0:hd2], hb[:, hd2:D]
        yv = jnp.zeros((tm, D), F32)
        for cidx in range(NCH):
            gate = _dot_nt(hba, wina_ref[cidx]) + _dot_nt(hbb, winb_ref[cidx])
            up = _dot_nt(hba, wina_ref[NCH + cidx]) + _dot_nt(hbb, winb_ref[NCH + cidx])
            gu_s[cidx] = gate.astype(BF16)
            gu_s[NCH + cidx] = up.astype(BF16)
            act = gate * _sigmoid(gate) * up
            act_ref[cidx] = act.astype(BF16)
            yv = yv + _dot(act_ref[cidx], wout_ref[cidx])
        ry = lax.rsqrt(_rowmean(yv * yv) + EPS)
        yh = yv * ry
        pn = yh * gpost
        err = xv + (HALF * gt) * pn - t_ref[...]
        dxo_v = err * (1.0 / D)
        d_gt = _colsum(HALF * dxo_v * pn)
        dp = (HALF * gt) * dxo_v
        d_gpost = _colsum(dp * yh)
        dyh = dp * gpost
        dyb = (ry * (dyh - yh * _rowmean(dyh * yh))).astype(BF16)
        dyb_ref[...] = dyb
        dha = jnp.zeros((tm, hd2), F32)
        dhb = jnp.zeros((tm, D - hd2), F32)
        for cidx in range(NCH):
            gate = gu_s[cidx].astype(F32)
            up = gu_s[NCH + cidx].astype(F32)
            sig = _sigmoid(gate)
            s = gate * sig
            d_act = _dot_nt(dyb, wout_ref[cidx])
            d_up = (d_act * s).astype(BF16)
            d_gate = (d_act * up * (sig * (1.0 + gate * (1.0 - sig)))).astype(BF16)
            dg_ref[cidx] = d_gate
            dg_ref[NCH + cidx] = d_up
            dha = dha + _dot(d_gate, wina_ref[cidx]) + _dot(d_up, wina_ref[NCH + cidx])
            dhb = dhb + _dot(d_gate, winb_ref[cidx]) + _dot(d_up, winb_ref[NCH + cidx])
        dh = jnp.concatenate([dha, dhb], axis=1)
        d_sc = _colsum(dh * n)
        d_sh = _colsum(dh)
        dn = dh * (1.0 + sc)
        d_gpre = _colsum(dn * xh)
        dxh = dn * gpre
        dx_ref[...] = dxo_v + r * (dxh - xh * _rowmean(dxh * xh))

        @pl.when(i % tps == 0)
        def _():
            mg_ref[...] = jnp.zeros((8, D), F32)

        @pl.when(i == 0)
        def _():
            vg_ref[...] = jnp.zeros((8, D), F32)
            loss_ref[...] = jnp.zeros((8, D), F32)

        mg_ref[0:1, :] += d_sh
        mg_ref[1:2, :] += d_sc
        mg_ref[2:3, :] += d_gt
        vg_ref[0:1, :] += d_gpre
        vg_ref[1:2, :] += d_gpost
        loss_ref[...] += HALF * jnp.sum(_rowmean(err * err), axis=0, keepdims=True)

    tile = pl.BlockSpec((tm, D), lambda i: (i, 0))
    return _call(
        core, name=name, grid=(nt,), jobs=jobs,
        in_specs=[tile, tile, pl.BlockSpec((None, 8, D), lambda i: (i // tps, 0, 0)), _const_spec((8, D)),
                  _const_spec(w_in[0].shape), _const_spec(w_in[1].shape), _const_spec((4, FB, D))],
        out_specs=[tile, pl.BlockSpec((8, tm, FB), lambda i: (0, i, 0)),
                   pl.BlockSpec((4, tm, FB), lambda i: (0, i, 0)), tile, tile,
                   pl.BlockSpec((None, 8, D), lambda i: (i // tps, 0, 0)), pl.BlockSpec((8, D), lambda i: (0, 0)),
                   pl.BlockSpec((8, D), lambda i: (0, 0))],
        out_shape=[jax.ShapeDtypeStruct((T, D), F32), jax.ShapeDtypeStruct((8, T, FB), BF16),
                   jax.ShapeDtypeStruct((4, T, FB), BF16), jax.ShapeDtypeStruct((T, D), BF16),
                   jax.ShapeDtypeStruct((T, D), BF16), jax.ShapeDtypeStruct((nb, 8, D), F32),
                   jax.ShapeDtypeStruct((8, D), F32), jax.ShapeDtypeStruct((8, D), F32)],
        scratch=[pltpu.VMEM((8, tm, FB), BF16)],
        args=[x, target, mod, gvec, w_in[0], w_in[1], w_out])


def _masked_spatial(ws_ref):
    row = lax.broadcasted_iota(jnp.int32, (CHUNK, CHUNK), 0)
    col = lax.broadcasted_iota(jnp.int32, (CHUNK, CHUNK), 1)
    keep = col <= row
    return [jnp.where(keep, ws_ref[hd], 0.0).astype(BF16) for hd in range(NHEAD)]


def _head_pairs(mats, right, transpose=False):
    first = lax.broadcasted_iota(jnp.int32, (CHUNK, LANES), 1) < HD
    op = _dot_tn if transpose else _dot
    out = []
    for p in range(NHEAD // 2):
        slab = right[:, _lanes(p)]
        out.append(jnp.where(first, op(mats[2 * p], slab), op(mats[2 * p + 1], slab)))
    return jnp.concatenate(out, axis=1)


def _spatial_gate(wm, vb_chunk):
    return _head_pairs(wm, vb_chunk)


def _layer_norm_stats(v):
    mu = _rowmean(v)
    vc = v - mu
    rstd = lax.rsqrt(_rowmean(vc * vc) + EPS)
    return vc * rstd, rstd


def _pitch(tm):
    p = tm // 8
    while p % 8 != 4:
        p += 1
    return p


def _lanes(s):
    return slice(s * LANES, (s + 1) * LANES)


def _to_slabs(ref, row0, val):
    for s in range(NSLAB):
        ref[s, row0:row0 + val.shape[0], :] = val[:, _lanes(s)]


def _tap_sum(src, out, cw_ref, bias, tm, start):
    p = _pitch(tm)
    for s in range(NSLAB):
        accs = [jnp.broadcast_to(bias[:, _lanes(s)], (SUBL, LANES))] * p
        for k in range(CONV_K):
            w = jnp.broadcast_to(cw_ref[k:k + 1, _lanes(s)], (SUBL, LANES))
            for v in range(p):
                accs[v] = accs[v] + w * src[s, pl.ds(v + start(k), 8, stride=p), :]
        for v in range(p):
            out[s, pl.ds(v, 8, stride=p), :] = accs[v]
    return jnp.concatenate([out[s, 0:tm, :] for s in range(NSLAB)], axis=1)


def _mixer_fwd(x, mod, gvec, w_mi, w_mo, v512, ws, bias_full, cw, tm, name, jobs=()):
    T = x.shape[0]
    nt = T // tm
    tps = nt // mod.shape[0]
    ext_rows = 8 * _pitch(tm)

    def core(ins, outs, scs):
        x_ref, mod_ref, g_ref, wmi_ref, wmo_ref, v_ref, ws_ref, bias_ref, cw_ref = ins
        xo_ref, proj_ref, ym_ref, conv_ref = outs
        glu_ext, conv_scr = scs
        i = pl.program_id(0)
        xv = x_ref[...]
        sh, sc, gt = mod_ref[0:1, :], mod_ref[1:2, :], mod_ref[2:3, :]
        r = lax.rsqrt(_rowmean(xv * xv) + EPS)
        hb = ((xv * r * g_ref[0:1, :]) * (1.0 + sc) + sh).astype(BF16)
        for j in range(NDEV):
            proj_ref[:, j * MB:(j + 1) * MB] = _dot(hb, wmi_ref[j])
        u = proj_ref[:, 0:WA]
        v0 = proj_ref[:, WA:2 * WA]
        a = proj_ref[:, 2 * WA:3 * WA]
        g = proj_ref[:, 3 * WA:4 * WA]
        vh, _ = _layer_norm_stats(v0)
        vb = (vh * v_ref[0:1, :] + v_ref[1:2, :]).astype(BF16)
        wm = _masked_spatial(ws_ref)
        ya = []
        for q in range(tm // CHUNK):
            z = _spatial_gate(wm, vb[q * CHUNK:(q + 1) * CHUNK, :]) + bias_ref[...]
            ya.append(u[q * CHUNK:(q + 1) * CHUNK, :] * z)
        ya = jnp.concatenate(ya, axis=0)
        glu = a * _sigmoid(g)

        @pl.when(i == 0)
        def _():
            glu_ext[:, HALO + tm:HALO + ext_rows, :] = jnp.zeros((NSLAB, ext_rows - tm, LANES), F32)

        @pl.when(i % tps == 0)
        def _():
            glu_ext[:, 0:HALO, :] = jnp.zeros((NSLAB, HALO, LANES), F32)

        _to_slabs(glu_ext, HALO, glu)
        conv = _tap_sum(glu_ext, conv_scr, cw_ref, v_ref[2:3, :], tm, lambda k: HALO - (CONV_K - 1) + k)
        conv_ref[...] = conv
        glu_ext[:, 0:HALO, :] = glu_ext[:, tm:tm + HALO, :]
        ch, _ = _layer_norm_stats(conv)
        cn = ch * v_ref[3:4, :] + v_ref[4:5, :]
        yb = cn * _sigmoid(cn)
        pa = ya * lax.rsqrt(_rowmean(ya * ya) + EPS) * v_ref[5:6, :]
        pb = yb * lax.rsqrt(_rowmean(yb * yb) + EPS) * v_ref[6:7, :]
        ycat = jnp.concatenate([pa, pb], axis=1).astype(BF16)
        ym = _dot(ycat, wmo_ref[...])
        ym_ref[...] = ym
        rm = lax.rsqrt(_rowmean(ym * ym) + EPS)
        xo_ref[...] = xv + gt * (ym * rm * g_ref[1:2, :])

    tile = pl.BlockSpec((tm, D), lambda i: (i, 0))
    return _call(
        core, name=name, grid=(nt,), jobs=jobs,
        in_specs=[tile, pl.BlockSpec((None, 8, D), lambda i: (i // tps, 0, 0)), _const_spec((8, D)),
                  _const_spec((NDEV, D, MB)), _const_spec((D, D)), _const_spec((8, WA)),
                  _const_spec((NHEAD, CHUNK, CHUNK)), _const_spec((CHUNK, WA)), _const_spec((32, WA))],
        out_specs=[tile, pl.BlockSpec((tm, 4 * WA), lambda i: (i, 0)), tile, pl.BlockSpec((tm, WA), lambda i: (i, 0))],
        out_shape=[jax.ShapeDtypeStruct((T, D), F32), jax.ShapeDtypeStruct((T, 4 * WA), F32),
                   jax.ShapeDtypeStruct((T, D), F32), jax.ShapeDtypeStruct((T, WA), F32)],
        scratch=[pltpu.VMEM((NSLAB, HALO + ext_rows, LANES), F32), pltpu.VMEM((NSLAB, ext_rows, LANES), F32)],
        args=[x, mod, gvec, w_mi, w_mo, v512, ws, bias_full, cw])


def _mixer_bwd_a(dxo, ym, proj, conv, mod, gvec, w_mo, v512, ws, bias_full, esel, tm, name, jobs=()):
    T = dxo.shape[0]
    nt = T // tm
    nb = mod.shape[0]
    tps = nt // nb

    def core(ins, outs, scs):
        dxo_ref, ym_ref, proj_ref, conv_ref, mod_ref, g_ref, wmo_ref, v_ref, ws_ref, bias_ref, e_ref = ins
        dpart_ref, dymb_ref, ycat_ref, mg_ref, vg_ref, v5g_ref, gws_ref, gbs_ref = outs
        (dbs_acc,) = scs
        i = pl.program_id(0)
        dxo_v = dxo_ref[...]
        ymv = ym_ref[...]
        gt = mod_ref[2:3, :]
        gpost = g_ref[1:2, :]
        rm = lax.rsqrt(_rowmean(ymv * ymv) + EPS)
        ymh = ymv * rm
        d_gt = _colsum(dxo_v * (ymh * gpost))
        dpm = gt * dxo_v
        d_gpost = _colsum(dpm * ymh)
        dymh = dpm * gpost
        dym = (rm * (dymh - ymh * _rowmean(dymh * ymh))).astype(BF16)
        dymb_ref[...] = dym
        dycat = _dot_nt(dym, wmo_ref[...])
        u = proj_ref[:, 0:WA]
        v0 = proj_ref[:, WA:2 * WA]
        vh, rv = _layer_norm_stats(v0)
        vb = (vh * v_ref[0:1, :] + v_ref[1:2, :]).astype(BF16)
        wm = _masked_spatial(ws_ref)
        zs = []
        for q in range(tm // CHUNK):
            zs.append(_spatial_gate(wm, vb[q * CHUNK:(q + 1) * CHUNK, :]) + bias_ref[...])
        z = jnp.concatenate(zs, axis=0)
        ya = u * z
        ra = lax.rsqrt(_rowmean(ya * ya) + EPS)
        yah = ya * ra
        ch, rc = _layer_norm_stats(conv_ref[...])
        cn = ch * v_ref[3:4, :] + v_ref[4:5, :]
        sg = _sigmoid(cn)
        yb = cn * sg
        rb = lax.rsqrt(_rowmean(yb * yb) + EPS)
        ybh = yb * rb
        ycat_ref[...] = jnp.concatenate([yah * v_ref[5:6, :], ybh * v_ref[6:7, :]], axis=1).astype(BF16)
        dpa = dycat[:, 0:WA]
        dpb = dycat[:, WA:2 * WA]
        d_goa = _colsum(dpa * yah)
        d_gob = _colsum(dpb * ybh)
        dyah = dpa * v_ref[5:6, :]
        dybh = dpb * v_ref[6:7, :]
        dya = ra * (dyah - yah * _rowmean(dyah * yah))
        dyb = rb * (dybh - ybh * _rowmean(dybh * ybh))
        dpart_ref[:, 0:WA] = dya * z
        dz = dya * u

        @pl.when(i == 0)
        def _():
            gws_ref[...] = jnp.zeros((NHEAD, CHUNK, CHUNK), F32)
            dbs_acc[...] = jnp.zeros((CHUNK, WA), F32)
            vg_ref[...] = jnp.zeros((8, D), F32)
            v5g_ref[...] = jnp.zeros((8, WA), F32)

        first = lax.broadcasted_iota(jnp.int32, (CHUNK, LANES), 1) < HD
        dvs = []
        for q in range(tm // CHUNK):
            dz_q = dz[q * CHUNK:(q + 1) * CHUNK, :]
            vb_q = vb[q * CHUNK:(q + 1) * CHUNK, :]
            dbs_acc[...] += dz_q
            dzb = dz_q.astype(BF16)
            dvs.append(_head_pairs(wm, dzb, transpose=True))
            for hd in range(NHEAD):
                slab = dzb[:, _lanes(hd // 2)]
                dz_hd = jnp.where(first if hd % 2 == 0 else jnp.logical_not(first), slab, jnp.zeros_like(slab))
                gws_ref[hd] += _dot_nt(dz_hd, vb_q[:, _lanes(hd // 2)])
        dv = jnp.concatenate(dvs, axis=0)
        d_gng = _colsum(dv * vh)
        d_gnb = _colsum(dv)
        dvh = dv * v_ref[0:1, :]
        dpart_ref[:, WA:2 * WA] = rv * (dvh - _rowmean(dvh) - vh * _rowmean(dvh * vh))
        dcn = dyb * (sg * (1.0 + cn * (1.0 - sg)))
        d_cng = _colsum(dcn * ch)
        d_cnb = _colsum(dcn)
        dch = dcn * v_ref[3:4, :]
        dconv = rc * (dch - _rowmean(dch) - ch * _rowmean(dch * ch))
        dpart_ref[:, 2 * WA:3 * WA] = dconv
        dpart_ref[:, 3 * WA:4 * WA] = jnp.zeros((tm, WA), F32)
        d_cb = _colsum(dconv)

        @pl.when(i % tps == 0)
        def _():
            mg_ref[...] = jnp.zeros((8, D), F32)

        mg_ref[2:3, :] += d_gt
        vg_ref[1:2, :] += d_gpost
        v5g_ref[0:1, :] += d_gng
        v5g_ref[1:2, :] += d_gnb
        v5g_ref[2:3, :] += d_cb
        v5g_ref[3:4, :] += d_cng
        v5g_ref[4:5, :] += d_cnb
        v5g_ref[5:6, :] += d_goa
        v5g_ref[6:7, :] += d_gob

        @pl.when(i == nt - 1)
        def _():
            row = lax.broadcasted_iota(jnp.int32, (CHUNK, CHUNK), 0)
            col = lax.broadcasted_iota(jnp.int32, (CHUNK, CHUNK), 1)
            for hd in range(NHEAD):
                gws_ref[hd] = jnp.where(col <= row, gws_ref[hd], 0.0)
            gbs_ref[...] = lax.dot_general(e_ref[...], dbs_acc[...], (((1,), (1,)), ((), ())),
                                           precision=lax.Precision.HIGHEST, preferred_element_type=F32)

    tile = pl.BlockSpec((tm, D), lambda i: (i, 0))
    ptile = pl.BlockSpec((tm, 4 * WA), lambda i: (i, 0))
    return _call(
        core, name=name, grid=(nt,), jobs=jobs,
        in_specs=[tile, tile, pl.BlockSpec((tm, 2 * WA), lambda i: (i, 0)), pl.BlockSpec((tm, WA), lambda i: (i, 0)),
                  pl.BlockSpec((None, 8, D), lambda i: (i // tps, 0, 0)), _const_spec((8, D)), _const_spec((D, D)),
                  _const_spec((8, WA)), _const_spec((NHEAD, CHUNK, CHUNK)), _const_spec((CHUNK, WA)),
                  _const_spec((8, WA))],
        out_specs=[ptile, tile, tile, pl.BlockSpec((None, 8, D), lambda i: (i // tps, 0, 0)),
                   pl.BlockSpec((8, D), lambda i: (0, 0)), pl.BlockSpec((8, WA), lambda i: (0, 0)),
                   pl.BlockSpec((NHEAD, CHUNK, CHUNK), lambda i: (0, 0, 0)), pl.BlockSpec((8, CHUNK), lambda i: (0, 0))],
        out_shape=[jax.ShapeDtypeStruct((T, 4 * WA), F32), jax.ShapeDtypeStruct((T, D), BF16),
                   jax.ShapeDtypeStruct((T, D), BF16), jax.ShapeDtypeStruct((nb, 8, D), F32),
                   jax.ShapeDtypeStruct((8, D), F32), jax.ShapeDtypeStruct((8, WA), F32),
                   jax.ShapeDtypeStruct((NHEAD, CHUNK, CHUNK), F32), jax.ShapeDtypeStruct((8, CHUNK), F32)],
        scratch=[pltpu.VMEM((CHUNK, WA), F32)],
        args=[dxo, ym, proj, conv, mod, gvec, w_mo, v512, ws, bias_full, esel])


def _mixer_bwd_b(dxo, x, dpart, proj, mod, gvec, w_mi, cw, tm, name, jobs=()):
    T = x.shape[0]
    nt = T // tm
    nb = mod.shape[0]
    tps = nt // nb
    hpt = tm // HALO
    nh = T // HALO
    off = HALO - (CONV_K - 1)
    p = _pitch(tm)
    ext_rows = 8 * p

    def core(ins, outs, scs):
        dxo_ref, x_ref, dpart_ref, dnext_ref, ag_ref, halo_ref, mod_ref, g_ref, wmi_ref, cw_ref = ins
        dx_ref, dproj_ref, hb_ref, mg_ref, vg_ref, dcw_ref = outs
        glu_ext, dconv_ext, dglu_scr, dcw_acc = scs
        i = pl.program_id(0)
        first = i % tps == 0
        last = i % tps == tps - 1
        a = ag_ref[:, 0:WA]
        g = ag_ref[:, WA:2 * WA]
        sgg = _sigmoid(g)

        @pl.when(i == 0)
        def _():
            glu_ext[:, HALO + tm:HALO + ext_rows, :] = jnp.zeros((NSLAB, ext_rows - tm, LANES), F32)
            dconv_ext[:, HALO + tm:HALO + ext_rows, :] = jnp.zeros((NSLAB, ext_rows - tm, LANES), F32)
            dcw_acc[...] = jnp.zeros((32, 8, WA), F32)
            vg_ref[...] = jnp.zeros((8, D), F32)

        _to_slabs(glu_ext, 0, jnp.where(first, 0.0, halo_ref[:, 0:WA] * _sigmoid(halo_ref[:, WA:2 * WA])))
        _to_slabs(glu_ext, HALO, a * sgg)
        _to_slabs(dconv_ext, 0, dpart_ref[:, 2 * WA:3 * WA])
        _to_slabs(dconv_ext, tm, jnp.where(last, 0.0, dnext_ref[...]))
        sub = lax.broadcasted_iota(jnp.int32, (SUBL, LANES), 0)
        for s in range(NSLAB):
            accs = [jnp.zeros((SUBL, LANES), F32)] * CONV_K
            for v in range(p):
                dc = jnp.where(v + p * sub < tm, dconv_ext[s, pl.ds(v, 8, stride=p), :], 0.0)
                for k in range(CONV_K):
                    accs[k] = accs[k] + dc * glu_ext[s, pl.ds(v + off + k, 8, stride=p), :]
            for k in range(CONV_K):
                dcw_acc[k, :, _lanes(s)] += accs[k]
        dglu = _tap_sum(dconv_ext, dglu_scr, cw_ref, jnp.zeros((1, WA), F32), tm, lambda k: (CONV_K - 1) - k)

        @pl.when(i == nt - 1)
        def _():
            for k in range(CONV_K):
                dcw_ref[k:k + 1, :] = jnp.sum(dcw_acc[k], axis=0, keepdims=True)
            dcw_ref[CONV_K:32, :] = jnp.zeros((32 - CONV_K, WA), F32)

        da = dglu * sgg
        dgg = dglu * a * (sgg * (1.0 - sgg))
        dproj_ref[:, 0:2 * WA] = dpart_ref[:, 0:2 * WA].astype(BF16)
        dproj_ref[:, 2 * WA:3 * WA] = da.astype(BF16)
        dproj_ref[:, 3 * WA:4 * WA] = dgg.astype(BF16)
        dh = jnp.zeros((tm, D), F32)
        for j in range(NDEV):
            dh = dh + _dot_nt(dproj_ref[:, j * MB:(j + 1) * MB], wmi_ref[j])
        xv = x_ref[...]
        sc, sh = mod_ref[1:2, :], mod_ref[0:1, :]
        gpre = g_ref[0:1, :]
        r = lax.rsqrt(_rowmean(xv * xv) + EPS)
        xh = xv * r
        n = xh * gpre
        hb_ref[...] = (n * (1.0 + sc) + sh).astype(BF16)
        d_sc = _colsum(dh * n)
        d_sh = _colsum(dh)
        dn = dh * (1.0 + sc)
        d_gpre = _colsum(dn * xh)
        dxh = dn * gpre
        dx_ref[...] = dxo_ref[...] + r * (dxh - xh * _rowmean(dxh * xh))

        @pl.when(first)
        def _():
            mg_ref[...] = jnp.zeros((8, D), F32)

        mg_ref[0:1, :] += d_sh
        mg_ref[1:2, :] += d_sc
        vg_ref[0:1, :] += d_gpre

    tile = pl.BlockSpec((tm, D), lambda i: (i, 0))
    return _call(
        core, name=name, grid=(nt,), jobs=jobs,
        in_specs=[tile, tile, pl.BlockSpec((tm, 4 * WA), lambda i: (i, 0)),
                  pl.BlockSpec((HALO, WA), lambda i: (jnp.minimum((i + 1) * hpt, nh - 1), 2)),
                  pl.BlockSpec((tm, 2 * WA), lambda i: (i, 1)),
                  pl.BlockSpec((HALO, 2 * WA), lambda i: (jnp.maximum(i * hpt - 1, 0), 1)),
                  pl.BlockSpec((None, 8, D), lambda i: (i // tps, 0, 0)), _const_spec((8, D)),
                  _const_spec((NDEV, D, MB)), _const_spec((32, WA))],
        out_specs=[tile, pl.BlockSpec((tm, 4 * WA), lambda i: (i, 0)), tile,
                   pl.BlockSpec((None, 8, D), lambda i: (i // tps, 0, 0)), pl.BlockSpec((8, D), lambda i: (0, 0)),
                   pl.BlockSpec((32, WA), lambda i: (0, 0))],
        out_shape=[jax.ShapeDtypeStruct((T, D), F32), jax.ShapeDtypeStruct((T, 4 * WA), BF16),
                   jax.ShapeDtypeStruct((T, D), BF16), jax.ShapeDtypeStruct((nb, 8, D), F32),
                   jax.ShapeDtypeStruct((8, D), F32), jax.ShapeDtypeStruct((32, WA), F32)],
        scratch=[pltpu.VMEM((NSLAB, HALO + ext_rows, LANES), F32), pltpu.VMEM((NSLAB, HALO + ext_rows, LANES), F32),
                 pltpu.VMEM((NSLAB, ext_rows, LANES), F32), pltpu.VMEM((32, 8, WA), F32)],
        args=[dxo, x, dpart, dpart, proj, proj, mod, gvec, w_mi, cw])


def _grad_chip(a, b, a_spec, b_spec, prod_shape, half, name, jobs=(), via_b=False, after=None):
    steps = 8 if half is None else 4
    R = prod_shape[0] if half is None else half
    C = prod_shape[1]

    def core(ins, outs, scs):
        a_ref, b_ref = ins[:2]
        (o_ref,) = outs
        own, snd, rcv, ssem, rsem, lsem = scs
        s = pl.program_id(0)
        c = lax.axis_index("c")
        me = _me()
        sib = _flip(me, (0, 0, 1))
        if via_b:
            prod = _dot_tn(b_ref[...], a_ref[...]).T.astype(BF16)
        else:
            prod = _dot_tn(a_ref[...], b_ref[...]).astype(BF16)
        if half is None:
            q = s // 2

            @pl.when(s % 2 == c)
            def _():
                own[q] = prod

            @pl.when(s % 2 != c)
            def _():
                snd[q] = prod
                _remote(snd.at[q], rcv.at[q], ssem.at[q], rsem.at[q], sib).start()
        else:
            lo = prod[0:half, :]
            hi = prod[half:2 * half, :]
            own[s] = jnp.where(c == 0, lo, hi)
            snd[s] = jnp.where(c == 0, hi, lo)
            _remote(snd.at[s], rcv.at[s], ssem.at[s], rsem.at[s], sib).start()

        @pl.when(s == steps - 1)
        def _():
            for q4 in range(4):
                cp = _remote(snd.at[q4], rcv.at[q4], ssem.at[q4], rsem.at[q4], sib)
                cp.wait_recv()
                cp.wait_send()
                snd[q4] = (own[q4].astype(F32) + rcv[q4].astype(F32)).astype(BF16)
            out = pltpu.make_async_copy(snd, o_ref, lsem)
            out.start()
            out.wait()

    return _call(
        core, name=name, grid=(steps,), jobs=jobs, in_specs=[a_spec, b_spec] + [HBM] * (after is not None),
        out_specs=[HBM], out_shape=[jax.ShapeDtypeStruct((4, R, C), BF16)],
        scratch=[pltpu.VMEM((4, R, C), BF16), pltpu.VMEM((4, R, C), BF16), pltpu.VMEM((4, R, C), BF16),
                 pltpu.SemaphoreType.DMA((4,)), pltpu.SemaphoreType.DMA((4,)), pltpu.SemaphoreType.DMA],
        args=[a, b] + [after] * (after is not None))


def _grad_w_in(dg, hb, name, jobs=()):
    T = hb.shape[0]
    return _grad_chip(dg, hb, pl.BlockSpec((None, T, FB), lambda s: (s, 0, 0)), _const_spec((T, D)),
                      (FB, D), None, name, jobs)


def _grad_w_out(act, dyb, name, jobs=(), after=None):
    T = dyb.shape[0]
    return _grad_chip(act, dyb, pl.BlockSpec((None, T, FB), lambda s: (s, 0, 0)), _const_spec((T, D)),
                      (FB, D), FO, name, jobs, after=after)


def _grad_w_mi(hb, dproj, name, jobs=()):
    T = hb.shape[0]
    return _grad_chip(hb, dproj, _const_spec((T, D)), pl.BlockSpec((T, MB), lambda s: (0, s)),
                      (D, MB), None, name, jobs, via_b=True)


def _grad_w_mo(ycat, dym, name, jobs=()):
    T = ycat.shape[0]
    return _grad_chip(ycat, dym, pl.BlockSpec((T, 2 * MO), lambda s: (0, s)), _const_spec((T, D)),
                      (2 * MO, D), MO, name, jobs)


def _adamw_math(w, g, m, v):
    m2 = ADAM_B1 * m + (1.0 - ADAM_B1) * g
    v2 = ADAM_B2 * v + (1.0 - ADAM_B2) * (g * g)
    m_hat = m2 / (1.0 - ADAM_B1 ** ADAM_STEP)
    v_hat = v2 / (1.0 - ADAM_B2 ** ADAM_STEP)
    delta = -ADAM_LR * (m_hat / (jnp.sqrt(v_hat) + ADAM_EPS) + ADAM_WD * w)
    return delta, m2, v2


def _adamw_reduce(parts, w, m, v, tr, name, after=None):
    R, C = w.shape

    def core(ins, outs, _):
        p_ref, w_ref, m_ref, v_ref = ins[:4]
        g_ref, d_ref, m2_ref, v2_ref = outs
        g = p_ref[0].astype(F32)
        for s in range(1, 4):
            g = g + p_ref[s].astype(F32)
        g_ref[...] = g
        d_ref[...], m2_ref[...], v2_ref[...] = _adamw_math(w_ref[...], g, m_ref[...], v_ref[...])

    blk = pl.BlockSpec((tr, C), lambda i: (i, 0))
    in_specs = [pl.BlockSpec((4, tr, C), lambda i: (0, i, 0)), blk, blk, blk]
    args = [parts, w, m, v]
    if after is not None:
        in_specs.append(HBM)
        args.append(after)
    return _call(
        core, name=name, grid=(R // tr,), in_specs=in_specs,
        out_specs=[blk, blk, blk, blk], out_shape=[jax.ShapeDtypeStruct((R, C), F32)] * 4, args=args)[0]


HBM_ONLY = pl.BlockSpec(memory_space=pltpu.HBM)
SEM = pl.BlockSpec(memory_space=pltpu.SEMAPHORE)
EFFECT = pltpu.SideEffectType.DATAFLOW_SIDE_EFFECTING


def _chip_scatter_start(gs, name):
    n = len(gs)

    def body(*refs):
        g_refs, land_refs = refs[:n], refs[n:2 * n]
        ssem, rsem = refs[2 * n:2 * n + 2]
        token = refs[-1]
        me = _me()
        mq = 2 * me[0] + me[1]
        for k, f in enumerate(CHIP_FLIPS):
            p = _flip(me, f)
            for a in range(n):
                _remote(g_refs[a].at[2 * p[0] + p[1]], land_refs[a].at[mq], ssem.at[4 * a + k], rsem.at[3 * a + k], p).start()
        for a in range(n):
            pltpu.make_async_copy(g_refs[a].at[mq], land_refs[a].at[mq], ssem.at[4 * a + 3]).start()
        token[...] = jnp.zeros_like(token)

    gs = [pltpu.with_memory_space_constraint(g, pltpu.HBM) for g in gs]
    lands = [pltpu.with_memory_space_constraint(lax.empty(g.shape, g.dtype), pltpu.HBM) for g in gs]
    res = pl.pallas_call(
        body, name=name,
        out_shape=(pltpu.SemaphoreType.DMA((4 * n,)), pltpu.SemaphoreType.DMA((3 * n,)))
        + tuple(pltpu.HBM(g.shape, g.dtype) for g in gs) * 2 + (jax.ShapeDtypeStruct((SUBL, LANES), F32),),
        in_specs=(HBM_ONLY,) * (2 * n), out_specs=(SEM, SEM) + (HBM_ONLY,) * (2 * n) + (VM,),
        input_output_aliases={a: 2 + a for a in range(2 * n)},
        compiler_params=pltpu.CompilerParams(has_side_effects=EFFECT),
    )(*gs, *lands)
    return res[:-1], res[-1]


def _chip_scatter_wait(handle, after, name):
    ssem, rsem = handle[:2]
    n = (len(handle) - 2) // 2
    thru = handle[2:]

    def body(*refs):
        g_refs, land_refs = refs[:n], refs[n:2 * n]
        ssem, rsem = refs[2 * n:2 * n + 2]
        me = _me()
        mq = 2 * me[0] + me[1]
        for k, f in enumerate(CHIP_FLIPS):
            p = _flip(me, f)
            pq = 2 * p[0] + p[1]
            for a in range(n):
                _remote(g_refs[a].at[pq], land_refs[a].at[mq], ssem.at[4 * a + k], rsem.at[3 * a + k], p).wait_send()
                _remote(g_refs[a].at[mq], land_refs[a].at[pq], ssem.at[4 * a + k], rsem.at[3 * a + k], p).wait_recv()
        for a in range(n):
            pltpu.make_async_copy(g_refs[a].at[mq], land_refs[a].at[mq], ssem.at[4 * a + 3]).wait()

    res = pl.pallas_call(
        body, name=name,
        out_shape=tuple(pltpu.HBM(t.shape, t.dtype) for t in thru),
        in_specs=(HBM_ONLY,) * (2 * n) + (SEM, SEM, HBM), out_specs=(HBM_ONLY,) * (2 * n),
        input_output_aliases={a: a for a in range(2 * n)},
        compiler_params=pltpu.CompilerParams(has_side_effects=EFFECT),
    )(*thru, ssem, rsem, after)
    return list(res[:n]), list(res[n:])


def _adamw_ada(sc_all, dd, w, m, v, tr, name, after=None):
    R, C = w.shape

    def core(ins, outs, _):
        sc_ref, dd_ref, w_ref, m_ref, v_ref = ins[:5]
        g_ref, d_ref, m2_ref, v2_ref = outs
        g = _dot_tn(sc_ref[...].astype(BF16), dd_ref[...].astype(BF16))
        g_ref[...] = g
        d_ref[...], m2_ref[...], v2_ref[...] = _adamw_math(w_ref[...], g, m_ref[...], v_ref[...])

    blk = pl.BlockSpec((tr, C), lambda i: (i, 0))
    return _call(
        core, name=name, grid=(R // tr,),
        in_specs=[pl.BlockSpec((64, tr), lambda i: (0, i)), pl.BlockSpec((64, C), lambda i: (0, 0)), blk, blk, blk]
        + [HBM] * (after is not None),
        out_specs=[blk, blk, blk, blk], out_shape=[jax.ShapeDtypeStruct((R, C), F32)] * 4,
        args=[sc_all, dd, w, m, v] + [after] * (after is not None))[0]


def _adamw_small(gathered, plain, grads, wmv, emit, name, after=None):
    nw = len(grads)
    ng, npl, ne = len(gathered), len(plain), len(emit)

    def core(ins, outs, _):
        srcs = []
        for a in range(ng):
            s = ins[a][0]
            for dev in range(1, NDEV):
                s = s + ins[a][dev]
            srcs.append(s)
        srcs += [ins[ng + a][...] for a in range(npl)]
        w_refs = ins[ng + npl:]
        for e, a in enumerate(emit):
            outs[e][...] = srcs[a]
        for t in range(nw):
            src, row = grads[t]
            g = srcs[src] if row is None else srcs[src][row:row + 1, :]
            w_ref, m_ref, v_ref = w_refs[3 * t:3 * t + 3]
            g_ref, d_ref, m2_ref, v2_ref = outs[ne + 4 * t:ne + 4 * t + 4]
            g_ref[...] = g
            d_ref[...], m2_ref[...], v2_ref[...] = _adamw_math(w_ref[...], g, m_ref[...], v_ref[...])

    out_shape = [jax.ShapeDtypeStruct(gathered[a].shape[1:], F32) for a in emit]
    for t in range(nw):
        out_shape += [jax.ShapeDtypeStruct(wmv[3 * t].shape, F32)] * 4
    return _call(
        core, name=name, grid=(), in_specs=[VM] * (ng + npl + 3 * nw) + [HBM] * (after is not None),
        out_specs=[VM] * (ne + 4 * nw), out_shape=out_shape,
        args=list(gathered) + list(plain) + list(wmv) + [after] * (after is not None))[0]


def _ada_pieces():
    out = []
    for r in range(9):
        pos = r * D
        while pos < (r + 1) * D:
            j = pos // ADA_B
            nxt = min((r + 1) * D, (j + 1) * ADA_B)
            out.append((r, j, pos - r * D, pos - j * ADA_B, nxt - pos))
            pos = nxt
    return out


def _ada_fwd(c_pad, w_ada, b_cols, cw_pad, bl, jobs=()):
    def core(ins, outs, scs, start_jobs, finish_jobs):
        c_ref, w_ref, b_ref, cwp_ref = ins
        mod_refs, (sc_ref, cw_ref) = outs[0:3], outs[3:5]
        ada_ref, cbuf, send_buf, ssem, rsem = scs
        me = _me()
        mi = _lin(me)
        cbuf[mi] = c_ref[...]
        cw_ref[mi] = cwp_ref[...]
        peers = [_flip(me, f) for f in FLIPS]
        first = []
        for k, p in enumerate(peers):
            first.append(_remote(cbuf.at[mi], cbuf.at[mi], ssem.at[k], rsem.at[k], p))
            first.append(_remote(cw_ref.at[mi], cw_ref.at[mi], ssem.at[7 + k], rsem.at[7 + k], p))
        for cp in first:
            cp.start()
        start_jobs()
        for k, p in enumerate(peers):
            pi = _lin(p)
            _remote(cbuf.at[pi], cbuf.at[pi], ssem.at[k], rsem.at[k], p).wait_recv()
            _remote(cw_ref.at[pi], cw_ref.at[pi], ssem.at[7 + k], rsem.at[7 + k], p).wait_recv()
        c_all = cbuf[...].reshape(8 * 8, D)
        sc = c_all * _sigmoid(c_all)
        sc_ref[...] = sc
        res = _dot(sc.astype(BF16), w_ref[...].astype(BF16)) + b_ref[...]
        send_buf[...] = res.reshape(8, 8, ADA_B)
        ada_ref[mi] = send_buf[mi]
        second = []
        for k, p in enumerate(peers):
            second.append(_remote(send_buf.at[_lin(p)], ada_ref.at[mi], ssem.at[14 + k], rsem.at[14 + k], p))
        for cp in second:
            cp.start()
        finish_jobs()
        for k, p in enumerate(peers):
            _remote(send_buf.at[mi], ada_ref.at[_lin(p)], ssem.at[14 + k], rsem.at[14 + k], p).wait_recv()
        for m_ref in mod_refs:
            m_ref[...] = jnp.zeros_like(m_ref)
        for r, j, in_row, in_blk, width in _ada_pieces():
            for b in range(bl):
                mod_refs[r // 3][b, r % 3:r % 3 + 1, in_row:in_row + width] = ada_ref[j, b:b + 1, in_blk:in_blk + width]
        for cp in first + second:
            cp.wait_send()

    return _call(
        core, name="ada_fwd", grid=(), jobs=jobs, core_starts=True, in_specs=[VM, VM, VM, VM], out_specs=[VM] * 5,
        out_shape=[jax.ShapeDtypeStruct((bl, 8, D), F32)] * 3
        + [jax.ShapeDtypeStruct((64, D), F32), jax.ShapeDtypeStruct((8, 32, 64), F32)],
        scratch=[pltpu.VMEM((8, 8, ADA_B), F32), pltpu.VMEM((8, 8, D), F32), pltpu.VMEM((8, 8, ADA_B), F32),
                 pltpu.SemaphoreType.DMA((21,)), pltpu.SemaphoreType.DMA((21,))],
        args=[c_pad, w_ada, b_cols, cw_pad])


def _ada_bwd(mods, rows, vecs, vec_rows):
    bl = mods[0].shape[0]

    def core(ins, outs, scs):
        dd_ref, gb_ref, pack_ref = outs
        pack_ref[...] = jnp.zeros_like(pack_ref)
        for t, (a, row) in enumerate(vec_rows):
            pack_ref[t:t + 1, :] = ins[len(mods) + a][row:row + 1, :]
        d_ref, rbuf, ssem, rsem = scs
        me = _me()
        mi = _lin(me)
        peers = [_flip(me, f) for f in FLIPS]
        d_ref[...] = jnp.zeros_like(d_ref)
        for r, j, in_row, in_blk, width in _ada_pieces():
            a, row = rows[r]
            for b in range(bl):
                d_ref[j, b:b + 1, in_blk:in_blk + width] = ins[a][b, row:row + 1, in_row:in_row + width]
        rbuf[mi] = d_ref[mi]
        first = []
        for k, p in enumerate(peers):
            first.append(_remote(d_ref.at[_lin(p)], rbuf.at[mi], ssem.at[k], rsem.at[k], p))
        for cp in first:
            cp.start()
        for k, p in enumerate(peers):
            _remote(d_ref.at[mi], rbuf.at[_lin(p)], ssem.at[k], rsem.at[k], p).wait_recv()
        dd = rbuf[...].reshape(64, ADA_B)
        dd_ref[...] = dd
        gb_ref[...] = jnp.broadcast_to(_colsum(dd), (8, ADA_B))
        for cp in first:
            cp.wait_send()

    return _call(
        core, name="ada_bwd", grid=(), in_specs=[VM] * (len(mods) + len(vecs)), out_specs=[VM, VM, VM],
        out_shape=[jax.ShapeDtypeStruct((64, ADA_B), F32), jax.ShapeDtypeStruct((8, ADA_B), F32),
                   jax.ShapeDtypeStruct((8, D), F32)],
        scratch=[pltpu.VMEM((8, 8, ADA_B), F32), pltpu.VMEM((8, 8, ADA_B), F32),
                 pltpu.SemaphoreType.DMA((7,)), pltpu.SemaphoreType.DMA((7,))],
        args=list(mods) + list(vecs))[0]


SMALL_D = ("g_pre_f1", "g_post_f1", "g_pre_m", "g_post_m", "g_pre_f2", "g_post_f2")
SMALL_W = ("gmlp_norm_g", "gmlp_norm_b", "conv_b", "conv_norm_g", "conv_norm_b", "g_out_a", "g_out_b")


def kernel(x, c, w_ada, b_ada, g_pre_f1, g_post_f1, w_f1_in, w_f1_out, g_pre_m, g_post_m, w_mix_in, gmlp_norm_g, gmlp_norm_b, w_spatial, b_spatial, conv_w, conv_b, conv_norm_g, conv_norm_b, g_out_a, g_out_b, w_mix_out, g_pre_f2, g_post_f2, w_f2_in, w_f2_out, loss_target, m_w_ada, m_b_ada, m_g_pre_f1, m_g_post_f1, m_w_f1_in, m_w_f1_out, m_g_pre_m, m_g_post_m, m_w_mix_in, m_gmlp_norm_g, m_gmlp_norm_b, m_w_spatial, m_b_spatial, m_conv_w, m_conv_b, m_conv_norm_g, m_conv_norm_b, m_g_out_a, m_g_out_b, m_w_mix_out, m_g_pre_f2, m_g_post_f2, m_w_f2_in, m_w_f2_out, v_w_ada, v_b_ada, v_g_pre_f1, v_g_post_f1, v_w_f1_in, v_w_f1_out, v_g_pre_m, v_g_post_m, v_w_mix_in, v_gmlp_norm_g, v_gmlp_norm_b, v_w_spatial, v_b_spatial, v_conv_w, v_conv_b, v_conv_norm_g, v_conv_norm_b, v_g_out_a, v_g_out_b, v_w_mix_out, v_g_pre_f2, v_g_post_f2, v_w_f2_in, v_w_f2_out):
    given = dict(locals())
    bl, seq, _ = x.shape
    T = bl * seq
    tm = min(256, seq // 2)
    mi = _lin((lax.axis_index("x"), lax.axis_index("y"), lax.axis_index("c")))

    def shard_in(w):
        return w[0].T.astype(BF16)

    g_f1 = _RelayGather([shard_in(w_f1_in), w_f1_out[0].astype(BF16)], ("rows", "out"))
    s_f2 = shard_in(w_f2_in)
    g_mx = _RelayGather([w_mix_in[0].astype(BF16), w_mix_out[0].astype(BF16), w_f2_out[0].astype(BF16), s_f2[:, 0:D // 4]],
                        ("rows", "rows", "out", "rows"))
    g_f2 = _RelayGather([s_f2[:, D // 4:D]], ("rows",))

    c_pad = jnp.pad(c, ((0, 8 - bl), (0, 0)))
    b_cols = lax.dynamic_slice(b_ada, (0, mi * ADA_B), (1, ADA_B))
    cw_pad = jnp.pad(conv_w[0], ((0, 1), (0, 0)))
    (mod1, mod2, mod3, sc_all, cw_all), ((wi1, wo1),) = _ada_fwd(c_pad, w_ada[0], b_cols, cw_pad, bl, jobs=[g_f1])
    cw_full = cw_all.transpose(1, 0, 2).reshape(32, WA)

    zrow = jnp.zeros((1, D), F32)
    gv1 = jnp.concatenate([g_pre_f1, g_post_f1] + [zrow] * 6, axis=0)
    gvm = jnp.concatenate([g_pre_m, g_post_m] + [zrow] * 6, axis=0)
    gv2 = jnp.concatenate([g_pre_f2, g_post_f2] + [zrow] * 6, axis=0)
    v512 = jnp.concatenate([gmlp_norm_g, gmlp_norm_b, conv_b, conv_norm_g, conv_norm_b, g_out_a, g_out_b,
                            jnp.zeros((1, WA), F32)], axis=0)
    ws = w_spatial[0]
    bias_full = jnp.repeat(b_spatial[0].T, HD, axis=1)
    esel = (lax.broadcasted_iota(jnp.int32, (8, WA), 1) // HD == lax.broadcasted_iota(jnp.int32, (8, WA), 0)).astype(F32)

    x0 = x.reshape(T, D)
    (x1, gu1, y1), ((wmi, wmo, wo2, wi2a),) = _ffn_fwd(x0, mod1, gv1, wi1, wo1, tm, "ffn1_fwd", jobs=[g_mx])
    wmo = wmo.reshape(D, D)
    (x2, proj, ym, conv), ((wi2b,),) = _mixer_fwd(x1, mod2, gvm, wmi, wmo, v512, ws, bias_full, cw_full, tm, "mixer_fwd", jobs=[g_f2])

    (dx2, dg2, act2, hb2, dyb2, mg3, vg3, loss_blk), _ = _ffn_last(
        x2, loss_target.reshape(T, D), mod3, gv2, (wi2a, wi2b), wo2, tm, "ffn2_fwd_bwd")
    (g_wi2,), _ = _grad_w_in(dg2, hb2, "ffn2_gw_in")
    (g_wo2,), _ = _grad_w_out(act2, dyb2, "ffn2_gw_out")
    (dpart, dymb, ycat, mg2a, vgma, v5g, gws, gbs), ((p_wo2,),) = _mixer_bwd_a(
        dx2, ym, proj, conv, mod2, gvm, wmo, v512, ws, bias_full, esel, tm, "mixer_bwd_a",
        jobs=[_ChipScatter([g_wo2])])
    (dx1, dproj, hbm, mg2b, vgmb, dcw), ((p_wi2,),) = _mixer_bwd_b(
        dx2, x1, dpart, proj, mod2, gvm, wmi, cw_full, tm, "mixer_bwd_b", jobs=[_ChipScatter([g_wi2])])
    (g_wmi,), _ = _grad_w_mi(hbm, dproj, "mixer_gw_in")
    (g_wmo,), _ = _grad_w_mo(ycat, dymb, "mixer_gw_out")
    p2 = jnp.concatenate([v5g, dcw], axis=0)
    (dx0, dg1, act1, hb1, dyb1, mg1, vg1), _ = _ffn_bwd(dx1, x0, y1, gu1, mod1, gv1, wi1, wo1, tm, "ffn1_bwd")

    ada_rows = [(0, 0), (0, 1), (0, 2), (1, 0), (1, 1), (2, 2), (3, 0), (3, 1), (3, 2)]
    p1_rows = [(0, 0), (0, 1), (1, 0), (2, 1), (3, 0), (3, 1), (4, 0)]
    dd_all, gb_own, p1 = _ada_bwd([mg1, mg2b, mg2a, mg3], ada_rows, [vg1, vgmb, vgma, vg3, loss_blk], p1_rows)

    (g_wi1,), ((a1, a2, a3, a4, gb_all), (p_wmi, p_wmo)) = _grad_w_in(
        dg1, hb1, "ffn1_gw_in",
        jobs=[_Gather([p1, p2, gws, gbs, gb_own], ("rows",) * 5), _ChipScatter([g_wmi, g_wmo])])
    g_bada = gb_all[:, 0, :].reshape(1, 9 * D)

    h_i1, token = _chip_scatter_start([g_wi1], "tail_start")
    (g_wo1,), _ = _grad_w_out(act1, dyb1, "ffn1_gw_out", after=token)
    h_o1, token = _chip_scatter_start([g_wo1], "tail2_start")

    res = {}
    quad = _adamw_reduce(p_wi2, w_f2_in[0].T, m_w_f2_in[0].T, v_w_f2_in[0].T, FO, "adamw_w_f2_in", after=token)
    res["w_f2_in"] = tuple(t.T[None] for t in quad)
    for nm, part, tr in (("w_f2_out", p_wo2, FO), ("w_mix_in", p_wmi, 256), ("w_mix_out", p_wmo, MO)):
        quad = _adamw_reduce(part, given[nm][0], given["m_" + nm][0], given["v_" + nm][0], tr, "adamw_" + nm, after=quad[1])
        res[nm] = tuple(t[None] for t in quad)
    quad = _adamw_ada(sc_all, dd_all, w_ada[0], m_w_ada[0], v_w_ada[0], 256, "adamw_w_ada", after=quad[1])
    res["w_ada"] = tuple(t[None] for t in quad)

    small = SMALL_D + SMALL_W + ("w_spatial", "b_spatial", "b_ada")
    grads = [(0, r) for r in range(6)] + [(1, r) for r in range(7)] + [(2, None), (3, None), (4, None)]
    wmv = []
    for nm in small:
        for pre in ("", "m_", "v_"):
            wmv.append(given[pre + nm][0] if nm in ("w_spatial", "b_spatial") else given[pre + nm])
    outs = _adamw_small([a1, a2, a3, a4], [g_bada], grads, wmv, (0, 1), "adamw_small", after=quad[1])

    _, (p_wi1,) = _chip_scatter_wait(h_i1, outs[0], "tail_wait")
    quad = _adamw_reduce(p_wi1, w_f1_in[0].T, m_w_f1_in[0].T, v_w_f1_in[0].T, FO, "adamw_w_f1_in")
    res["w_f1_in"] = tuple(t.T[None] for t in quad)
    _, (p_wo1,) = _chip_scatter_wait(h_o1, quad[1], "tail2_wait")
    quad = _adamw_reduce(p_wo1, w_f1_out[0], m_w_f1_out[0], v_w_f1_out[0], FO, "adamw_w_f1_out")
    res["w_f1_out"] = tuple(t[None] for t in quad)
    loss = outs[0][6, 0]
    for t, nm in enumerate(small):
        quad = outs[2 + 4 * t:6 + 4 * t]
        res[nm] = tuple(q[None] for q in quad) if nm in ("w_spatial", "b_spatial") else tuple(quad)
    g_cw = lax.dynamic_slice(outs[1], (8, mi * 64), (32, 64))
    wmv = [jnp.pad(given[pre + "conv_w"][0], ((0, 1), (0, 0)), constant_values=1.0 if pre == "v_" else 0.0)
           for pre in ("", "m_", "v_")]
    quad = _adamw_small([], [g_cw], [(0, None)], wmv, (), "adamw_conv_w")
    res["conv_w"] = tuple(q[0:CONV_K][None] for q in quad)

    order = ["w_ada", "b_ada", "g_pre_f1", "g_post_f1", "w_f1_in", "w_f1_out", "g_pre_m", "g_post_m", "w_mix_in",
             "gmlp_norm_g", "gmlp_norm_b", "w_spatial", "b_spatial", "conv_w", "conv_b", "conv_norm_g", "conv_norm_b",
             "g_out_a", "g_out_b", "w_mix_out", "g_pre_f2", "g_post_f2", "w_f2_in", "w_f2_out"]
    out = [loss, dx0.reshape(bl, seq, D)]
    for k in range(4):
        out += [res[nm][k] for nm in order]
    return tuple(out)
```
